```python
import math
import jax
import jax.numpy as jnp
from jax import lax
import numpy as np

D_MODEL = 1024
BATCH = 32
SEQ = 2048
DEPTH = 1

D_FF = 2816
DN_HEADS = 8
DN_HEAD_DIM = 64
DN_WIDTH = DN_HEADS * DN_HEAD_DIM
CONV_WIDTH = 4
CHUNK = 64
S5_GROUP_CH = 16
S5_GROUPS = 32
S5_WIDTH = S5_GROUPS * S5_GROUP_CH
S5_STATE = 64
N_MOD = 9
EPS = 1e-6
IN_WIDTH = 4 * DN_WIDTH + 2 * DN_HEADS + S5_WIDTH + 2 * D_MODEL

kernel_name = 'hybrid_deltanet_s5_macaron'


def rmsnorm(x, gain):
    x32 = x.astype(jnp.float32)
    y = x32 * lax.rsqrt(jnp.mean(x32 * x32, axis=-1, keepdims=True) + EPS)
    return (y * gain.astype(jnp.float32)).astype(x.dtype)


def modulate(x, shift, scale):
    return x * (1 + scale) + shift


def swiglu(x, w1, w3, w2):
    return (jax.nn.silu(x @ w1) * (x @ w3)) @ w2


def l2norm(t):
    return t * lax.rsqrt(jnp.sum(t * t, axis=-1, keepdims=True) + EPS)


def causal_depthwise_conv(x, w):
    return lax.conv_general_dilated(
        x, w[:, None, :].astype(x.dtype), window_strides=(1,),
        padding=[(CONV_WIDTH - 1, 0)], dimension_numbers=('NWC', 'WIO', 'NWC'),
        feature_group_count=x.shape[-1])


def split_combined(p):
    sizes = (DN_WIDTH, DN_WIDTH, DN_WIDTH, DN_WIDTH, DN_HEADS, DN_HEADS, S5_WIDTH, D_MODEL, D_MODEL)
    parts = []
    start = 0
    for size in sizes:
        parts.append(p[..., start:start + size])
        start += size
    return parts


def gated_deltanet(q, k, v, z, beta_logit, decay_logit, conv_w, a_log, dt_bias, g_onorm):
    f32 = jnp.float32
    dtype = q.dtype
    bsz, seq, _ = q.shape
    n_chunks = seq // CHUNK
    qkv = jax.nn.silu(causal_depthwise_conv(jnp.concatenate([q, k, v], axis=-1), conv_w)).astype(f32)
    q, k, v = jnp.split(qkv, 3, axis=-1)

    def to_chunks(t):
        return t.reshape(bsz, n_chunks, CHUNK, DN_HEADS, DN_HEAD_DIM).transpose(0, 3, 1, 2, 4)

    def to_chunks_h(t):
        return t.reshape(bsz, n_chunks, CHUNK, DN_HEADS).transpose(0, 3, 1, 2)

    q = l2norm(to_chunks(q)) * (DN_HEAD_DIM ** -0.5)
    k = l2norm(to_chunks(k))
    v = to_chunks(v)
    beta = to_chunks_h(jax.nn.sigmoid(beta_logit.astype(f32)))
    log_alpha = -jnp.exp(a_log.astype(f32)) * jax.nn.softplus(decay_logit.astype(f32) + dt_bias.astype(f32))
    g_cum = jnp.cumsum(to_chunks_h(log_alpha), axis=-1)
    causal = jnp.tril(jnp.ones((CHUNK, CHUNK), dtype=bool))
    strict = jnp.tril(jnp.ones((CHUNK, CHUNK), dtype=bool), k=-1)
    decay = jnp.exp(jnp.where(causal, g_cum[..., :, None] - g_cum[..., None, :], -jnp.inf))
    k_beta = k * beta[..., None]
    kk = jnp.where(strict, jnp.einsum('bhnik,bhnjk->bhnij', k_beta, k) * decay, 0.0)
    lhs = kk + jnp.eye(CHUNK, dtype=f32)
    rhs = jnp.concatenate([v * beta[..., None], k_beta * jnp.exp(g_cum)[..., None]], axis=-1)
    sol = lax.linalg.triangular_solve(lhs, rhs, left_side=True, lower=True, unit_diagonal=True)
    u_c, w_c = sol[..., :DN_HEAD_DIM], sol[..., DN_HEAD_DIM:]
    attn = jnp.einsum('bhnik,bhnjk->bhnij', q, k) * decay
    q_dec = q * jnp.exp(g_cum)[..., None]
    g_last = g_cum[..., -1]
    k_dec = k * jnp.exp(g_last[..., None] - g_cum)[..., None]

    def step(state, xs):
        q_i, k_i, u_i, w_i, a_i, gl_i = xs
        v_new = u_i - jnp.einsum('bhck,bhkv->bhcv', w_i, state)
        o_i = jnp.einsum('bhck,bhkv->bhcv', q_i, state) + jnp.einsum('bhij,bhjv->bhiv', a_i, v_new)
        state = state * jnp.exp(gl_i)[..., None, None] + jnp.einsum('bhck,bhcv->bhkv', k_i, v_new)
        return state, o_i

    xs = (q_dec, k_dec, u_c, w_c, attn, g_last)
    xs = tuple(jnp.moveaxis(t, 2, 0) for t in xs)
    state0 = jnp.zeros((bsz, DN_HEADS, DN_HEAD_DIM, DN_HEAD_DIM), f32)
    _, o = lax.scan(step, state0, xs)
    o = o.transpose(1, 0, 3, 2, 4).reshape(bsz, seq, DN_HEADS, DN_HEAD_DIM)
    gate = jax.nn.silu(z.astype(f32)).reshape(bsz, seq, DN_HEADS, DN_HEAD_DIM)
    o = o * lax.rsqrt(jnp.mean(o * o, axis=-1, keepdims=True) + EPS) * g_onorm.astype(f32) * gate
    return o.reshape(bsz, seq, DN_WIDTH).astype(dtype)


def s5_ssm(u_in, lam_re, lam_im, log_step, b_re, b_im, c_re, c_im, d_skip, w_glu, b_glu):
    f32 = jnp.float32
    dtype = u_in.dtype
    bsz, seq, _ = u_in.shape
    u = u_in.astype(f32).reshape(bsz, seq, S5_GROUPS, S5_GROUP_CH)
    lam_re = jnp.minimum(lam_re.astype(f32), -1e-4)
    lam_im = lam_im.astype(f32)
    step = jnp.exp(log_step.astype(f32))[:, None]
    mag = jnp.exp(lam_re * step)
    ang = lam_im * step
    lb_re = mag * jnp.cos(ang)
    lb_im = mag * jnp.sin(ang)
    den = lam_re * lam_re + lam_im * lam_im
    coef_re = ((lb_re - 1.0) * lam_re + lb_im * lam_im) / den
    coef_im = (lb_im * lam_re - (lb_re - 1.0) * lam_im) / den
    b_re = b_re.astype(f32)
    b_im = b_im.astype(f32)
    bb_re = coef_re[..., None] * b_re - coef_im[..., None] * b_im
    bb_im = coef_re[..., None] * b_im + coef_im[..., None] * b_re
    bu_re = jnp.einsum('bsgc,gpc->bsgp', u, bb_re)
    bu_im = jnp.einsum('bsgc,gpc->bsgp', u, bb_im)
    a_re = jnp.broadcast_to(lb_re, (1, seq, S5_GROUPS, S5_STATE))
    a_im = jnp.broadcast_to(lb_im, (1, seq, S5_GROUPS, S5_STATE))

    def combine(e1, e2):
        a1r, a1i, b1r, b1i = e1
        a2r, a2i, b2r, b2i = e2
        return (a2r * a1r - a2i * a1i,
                a2r * a1i + a2i * a1r,
                a2r * b1r - a2i * b1i + b2r,
                a2r * b1i + a2i * b1r + b2i)

    _, _, x_re, x_im = lax.associative_scan(combine, (a_re, a_im, bu_re, bu_im), axis=1)
    y = (jnp.einsum('bsgp,gcp->bsgc', x_re, c_re.astype(f32))
         - jnp.einsum('bsgp,gcp->bsgc', x_im, c_im.astype(f32))
         + d_skip.astype(f32).reshape(S5_GROUPS, S5_GROUP_CH) * u)
    y = jax.nn.gelu(y.reshape(bsz, seq, S5_WIDTH))
    y = y * jax.nn.sigmoid(y @ w_glu.astype(f32) + b_glu.astype(f32))
    return y.astype(dtype)


def hybrid_layer(h, c, w_ada, b_ada, g_ffn1, w1_ffn1, w3_ffn1, w2_ffn1, g_mix, w_in, conv_qkv,
                 a_log, dt_bias, g_onorm, lam_re, lam_im, log_step, b_re, b_im, c_re, c_im,
                 d_skip, w_glu, b_glu, w_proj_a, w_proj_b, w_out, g_ffn2, w1_ffn2, w3_ffn2, w2_ffn2):
    mod = jax.nn.silu(c) @ w_ada + b_ada
    sh1, sc1, gt1, sh2, sc2, gt2, sh3, sc3, gt3 = [m[:, None, :] for m in jnp.split(mod, N_MOD, axis=-1)]
    h = h + 0.5 * gt1 * swiglu(modulate(rmsnorm(h, g_ffn1), sh1, sc1), w1_ffn1, w3_ffn1, w2_ffn1)
    u = modulate(rmsnorm(h, g_mix), sh2, sc2)
    q, k, v, z, beta_logit, decay_logit, s5_in, gate_a, gate_b = split_combined(u @ w_in)
    y_a = gated_deltanet(q, k, v, z, beta_logit, decay_logit, conv_qkv, a_log, dt_bias, g_onorm) @ w_proj_a
    y_b = s5_ssm(s5_in, lam_re, lam_im, log_step, b_re, b_im, c_re, c_im, d_skip, w_glu, b_glu) @ w_proj_b
    merged = jax.nn.sigmoid(gate_a) * y_a + jax.nn.sigmoid(gate_b) * y_b
    h = h + gt2 * (merged @ w_out)
    h = h + 0.5 * gt3 * swiglu(modulate(rmsnorm(h, g_ffn2), sh3, sc3), w1_ffn2, w3_ffn2, w2_ffn2)
    return h


def _fwd_setup_inputs(seed: int = 0) -> dict:
    key = jax.random.key(seed)
    ks = jax.random.split(key, 40)
    f32 = jnp.float32
    L = DEPTH

    def nrm(k, shape, scale):
        return jax.random.normal(k, shape, f32) * scale

    def log_uniform(k, shape, lo, hi):
        return jax.random.uniform(k, shape, f32, math.log(lo), math.log(hi))

    dt = jnp.exp(log_uniform(ks[12], (L, DN_HEADS), 1e-3, 1e-1))
    n_idx = jnp.arange(S5_STATE, dtype=f32)
    return {
        'x': nrm(ks[0], (BATCH, SEQ, D_MODEL), 1.0),
        'c': nrm(ks[1], (BATCH, D_MODEL), 1.0),
        'w_ada': nrm(ks[2], (L, D_MODEL, N_MOD * D_MODEL), 0.5 * D_MODEL ** -0.5),
        'b_ada': nrm(ks[3], (L, N_MOD * D_MODEL), 0.02),
        'g_ffn1': 1.0 + nrm(ks[4], (L, D_MODEL), 0.02),
        'w1_ffn1': nrm(ks[5], (L, D_MODEL, D_FF), D_MODEL ** -0.5),
        'w3_ffn1': nrm(ks[6], (L, D_MODEL, D_FF), D_MODEL ** -0.5),
        'w2_ffn1': nrm(ks[7], (L, D_FF, D_MODEL), D_FF ** -0.5),
        'g_mix': 1.0 + nrm(ks[8], (L, D_MODEL), 0.02),
        'w_in': nrm(ks[9], (L, D_MODEL, IN_WIDTH), D_MODEL ** -0.5),
        'conv_qkv': nrm(ks[10], (L, CONV_WIDTH, 3 * DN_WIDTH), CONV_WIDTH ** -0.5),
        'a_log': jnp.log(jax.random.uniform(ks[11], (L, DN_HEADS), f32, 1.0, 16.0)),
        'dt_bias': dt + jnp.log(-jnp.expm1(-dt)),
        'g_onorm': 1.0 + nrm(ks[13], (L, DN_HEAD_DIM), 0.02),
        'lam_re': -0.5 + nrm(ks[14], (L, S5_GROUPS, S5_STATE), 0.01),
        'lam_im': math.pi * n_idx + nrm(ks[15], (L, S5_GROUPS, S5_STATE), 0.01),
        'log_step': log_uniform(ks[16], (L, S5_GROUPS), 1e-3, 1e-1),
        'b_re': nrm(ks[17], (L, S5_GROUPS, S5_STATE, S5_GROUP_CH), (2 * S5_GROUP_CH) ** -0.5),
        'b_im': nrm(ks[18], (L, S5_GROUPS, S5_STATE, S5_GROUP_CH), (2 * S5_GROUP_CH) ** -0.5),
        'c_re': nrm(ks[19], (L, S5_GROUPS, S5_GROUP_CH, S5_STATE), S5_STATE ** -0.5),
        'c_im': nrm(ks[20], (L, S5_GROUPS, S5_GROUP_CH, S5_STATE), S5_STATE ** -0.5),
        'd_skip': nrm(ks[21], (L, S5_WIDTH), 1.0),
        'w_glu': nrm(ks[22], (L, S5_WIDTH, S5_WIDTH), S5_WIDTH ** -0.5),
        'b_glu': nrm(ks[23], (L, S5_WIDTH), 0.02),
        'w_proj_a': nrm(ks[24], (L, DN_WIDTH, D_MODEL), DN_WIDTH ** -0.5),
        'w_proj_b': nrm(ks[25], (L, S5_WIDTH, D_MODEL), S5_WIDTH ** -0.5),
        'w_out': nrm(ks[26], (L, D_MODEL, D_MODEL), D_MODEL ** -0.5),
        'g_ffn2': 1.0 + nrm(ks[27], (L, D_MODEL), 0.02),
        'w1_ffn2': nrm(ks[28], (L, D_MODEL, D_FF), D_MODEL ** -0.5),
        'w3_ffn2': nrm(ks[29], (L, D_MODEL, D_FF), D_MODEL ** -0.5),
        'w2_ffn2': nrm(ks[30], (L, D_FF, D_MODEL), D_FF ** -0.5),
        'g_final': 1.0 + nrm(ks[31], (D_MODEL,), 0.02),
    }


def _fwd_reference(x, c, w_ada, b_ada, g_ffn1, w1_ffn1, w3_ffn1, w2_ffn1, g_mix, w_in, conv_qkv,
              a_log, dt_bias, g_onorm, lam_re, lam_im, log_step, b_re, b_im, c_re, c_im,
              d_skip, w_glu, b_glu, w_proj_a, w_proj_b, w_out, g_ffn2, w1_ffn2, w3_ffn2, w2_ffn2,
              g_final):
    h = x
    for layer in range(DEPTH):
        h = hybrid_layer(
            h, c, w_ada[layer], b_ada[layer], g_ffn1[layer], w1_ffn1[layer], w3_ffn1[layer],
            w2_ffn1[layer], g_mix[layer], w_in[layer], conv_qkv[layer], a_log[layer],
            dt_bias[layer], g_onorm[layer], lam_re[layer], lam_im[layer], log_step[layer],
            b_re[layer], b_im[layer], c_re[layer], c_im[layer], d_skip[layer], w_glu[layer],
            b_glu[layer], w_proj_a[layer], w_proj_b[layer], w_out[layer], g_ffn2[layer],
            w1_ffn2[layer], w3_ffn2[layer], w2_ffn2[layer])
    return rmsnorm(h, g_final)


import jax as _jax
import jax.numpy as _jnp

TWIN_FORMAT = 'train_step'
FWD_PARAMS = ['x', 'c', 'w_ada', 'b_ada', 'g_ffn1', 'w1_ffn1', 'w3_ffn1', 'w2_ffn1', 'g_mix', 'w_in', 'conv_qkv', 'a_log', 'dt_bias', 'g_onorm', 'lam_re', 'lam_im', 'log_step', 'b_re', 'b_im', 'c_re', 'c_im', 'd_skip', 'w_glu', 'b_glu', 'w_proj_a', 'w_proj_b', 'w_out', 'g_ffn2', 'w1_ffn2', 'w3_ffn2', 'w2_ffn2', 'g_final']
TWIN_WEIGHTS = ['w_ada', 'b_ada', 'g_ffn1', 'w1_ffn1', 'w3_ffn1', 'w2_ffn1', 'g_mix', 'w_in', 'conv_qkv', 'a_log', 'dt_bias', 'g_onorm', 'lam_re', 'lam_im', 'log_step', 'b_re', 'b_im', 'c_re', 'c_im', 'd_skip', 'w_glu', 'b_glu', 'w_proj_a', 'w_proj_b', 'w_out', 'g_ffn2', 'w1_ffn2', 'w3_ffn2', 'w2_ffn2', 'g_final']
TWIN_DIFF_INPUT = 'x'
TWIN_INPUTS = ['x', 'c', 'w_ada', 'b_ada', 'g_ffn1', 'w1_ffn1', 'w3_ffn1', 'w2_ffn1', 'g_mix', 'w_in', 'conv_qkv', 'a_log', 'dt_bias', 'g_onorm', 'lam_re', 'lam_im', 'log_step', 'b_re', 'b_im', 'c_re', 'c_im', 'd_skip', 'w_glu', 'b_glu', 'w_proj_a', 'w_proj_b', 'w_out', 'g_ffn2', 'w1_ffn2', 'w3_ffn2', 'w2_ffn2', 'g_final', 'loss_target', 'm_w_ada', 'm_b_ada', 'm_g_ffn1', 'm_w1_ffn1', 'm_w3_ffn1', 'm_w2_ffn1', 'm_g_mix', 'm_w_in', 'm_conv_qkv', 'm_a_log', 'm_dt_bias', 'm_g_onorm', 'm_lam_re', 'm_lam_im', 'm_log_step', 'm_b_re', 'm_b_im', 'm_c_re', 'm_c_im', 'm_d_skip', 'm_w_glu', 'm_b_glu', 'm_w_proj_a', 'm_w_proj_b', 'm_w_out', 'm_g_ffn2', 'm_w1_ffn2', 'm_w3_ffn2', 'm_w2_ffn2', 'm_g_final', 'v_w_ada', 'v_b_ada', 'v_g_ffn1', 'v_w1_ffn1', 'v_w3_ffn1', 'v_w2_ffn1', 'v_g_mix', 'v_w_in', 'v_conv_qkv', 'v_a_log', 'v_dt_bias', 'v_g_onorm', 'v_lam_re', 'v_lam_im', 'v_log_step', 'v_b_re', 'v_b_im', 'v_c_re', 'v_c_im', 'v_d_skip', 'v_w_glu', 'v_b_glu', 'v_w_proj_a', 'v_w_proj_b', 'v_w_out', 'v_g_ffn2', 'v_w1_ffn2', 'v_w3_ffn2', 'v_w2_ffn2', 'v_g_final']
TWIN_OUTPUTS = ['loss', 'grad_x', 'grad_w_ada', 'grad_b_ada', 'grad_g_ffn1', 'grad_w1_ffn1', 'grad_w3_ffn1', 'grad_w2_ffn1', 'grad_g_mix', 'grad_w_in', 'grad_conv_qkv', 'grad_a_log', 'grad_dt_bias', 'grad_g_onorm', 'grad_lam_re', 'grad_lam_im', 'grad_log_step', 'grad_b_re', 'grad_b_im', 'grad_c_re', 'grad_c_im', 'grad_d_skip', 'grad_w_glu', 'grad_b_glu', 'grad_w_proj_a', 'grad_w_proj_b', 'grad_w_out', 'grad_g_ffn2', 'grad_w1_ffn2', 'grad_w3_ffn2', 'grad_w2_ffn2', 'grad_g_final', 'delta_w_ada', 'delta_b_ada', 'delta_g_ffn1', 'delta_w1_ffn1', 'delta_w3_ffn1', 'delta_w2_ffn1', 'delta_g_mix', 'delta_w_in', 'delta_conv_qkv', 'delta_a_log', 'delta_dt_bias', 'delta_g_onorm', 'delta_lam_re', 'delta_lam_im', 'delta_log_step', 'delta_b_re', 'delta_b_im', 'delta_c_re', 'delta_c_im', 'delta_d_skip', 'delta_w_glu', 'delta_b_glu', 'delta_w_proj_a', 'delta_w_proj_b', 'delta_w_out', 'delta_g_ffn2', 'delta_w1_ffn2', 'delta_w3_ffn2', 'delta_w2_ffn2', 'delta_g_final', 'new_m_w_ada', 'new_m_b_ada', 'new_m_g_ffn1', 'new_m_w1_ffn1', 'new_m_w3_ffn1', 'new_m_w2_ffn1', 'new_m_g_mix', 'new_m_w_in', 'new_m_conv_qkv', 'new_m_a_log', 'new_m_dt_bias', 'new_m_g_onorm', 'new_m_lam_re', 'new_m_lam_im', 'new_m_log_step', 'new_m_b_re', 'new_m_b_im', 'new_m_c_re', 'new_m_c_im', 'new_m_d_skip', 'new_m_w_glu', 'new_m_b_glu', 'new_m_w_proj_a', 'new_m_w_proj_b', 'new_m_w_out', 'new_m_g_ffn2', 'new_m_w1_ffn2', 'new_m_w3_ffn2', 'new_m_w2_ffn2', 'new_m_g_final', 'new_v_w_ada', 'new_v_b_ada', 'new_v_g_ffn1', 'new_v_w1_ffn1', 'new_v_w3_ffn1', 'new_v_w2_ffn1', 'new_v_g_mix', 'new_v_w_in', 'new_v_conv_qkv', 'new_v_a_log', 'new_v_dt_bias', 'new_v_g_onorm', 'new_v_lam_re', 'new_v_lam_im', 'new_v_log_step', 'new_v_b_re', 'new_v_b_im', 'new_v_c_re', 'new_v_c_im', 'new_v_d_skip', 'new_v_w_glu', 'new_v_b_glu', 'new_v_w_proj_a', 'new_v_w_proj_b', 'new_v_w_out', 'new_v_g_ffn2', 'new_v_w1_ffn2', 'new_v_w3_ffn2', 'new_v_w2_ffn2', 'new_v_g_final']
TWIN_LEAF_KINDS = {'loss': 'loss', 'grad_x': 'grad_x', 'grad_w_ada': 'grad_w', 'grad_b_ada': 'grad_w', 'grad_g_ffn1': 'grad_w', 'grad_w1_ffn1': 'grad_w', 'grad_w3_ffn1': 'grad_w', 'grad_w2_ffn1': 'grad_w', 'grad_g_mix': 'grad_w', 'grad_w_in': 'grad_w', 'grad_conv_qkv': 'grad_w', 'grad_a_log': 'grad_w', 'grad_dt_bias': 'grad_w', 'grad_g_onorm': 'grad_w', 'grad_lam_re': 'grad_w', 'grad_lam_im': 'grad_w', 'grad_log_step': 'grad_w', 'grad_b_re': 'grad_w', 'grad_b_im': 'grad_w', 'grad_c_re': 'grad_w', 'grad_c_im': 'grad_w', 'grad_d_skip': 'grad_w', 'grad_w_glu': 'grad_w', 'grad_b_glu': 'grad_w', 'grad_w_proj_a': 'grad_w', 'grad_w_proj_b': 'grad_w', 'grad_w_out': 'grad_w', 'grad_g_ffn2': 'grad_w', 'grad_w1_ffn2': 'grad_w', 'grad_w3_ffn2': 'grad_w', 'grad_w2_ffn2': 'grad_w', 'grad_g_final': 'grad_w', 'delta_w_ada': 'delta_w', 'delta_b_ada': 'delta_w', 'delta_g_ffn1': 'delta_w', 'delta_w1_ffn1': 'delta_w', 'delta_w3_ffn1': 'delta_w', 'delta_w2_ffn1': 'delta_w', 'delta_g_mix': 'delta_w', 'delta_w_in': 'delta_w', 'delta_conv_qkv': 'delta_w', 'delta_a_log': 'delta_w', 'delta_dt_bias': 'delta_w', 'delta_g_onorm': 'delta_w', 'delta_lam_re': 'delta_w', 'delta_lam_im': 'delta_w', 'delta_log_step': 'delta_w', 'delta_b_re': 'delta_w', 'delta_b_im': 'delta_w', 'delta_c_re': 'delta_w', 'delta_c_im': 'delta_w', 'delta_d_skip': 'delta_w', 'delta_w_glu': 'delta_w', 'delta_b_glu': 'delta_w', 'delta_w_proj_a': 'delta_w', 'delta_w_proj_b': 'delta_w', 'delta_w_out': 'delta_w', 'delta_g_ffn2': 'delta_w', 'delta_w1_ffn2': 'delta_w', 'delta_w3_ffn2': 'delta_w', 'delta_w2_ffn2': 'delta_w', 'delta_g_final': 'delta_w', 'new_m_w_ada': 'new_m', 'new_m_b_ada': 'new_m', 'new_m_g_ffn1': 'new_m', 'new_m_w1_ffn1': 'new_m', 'new_m_w3_ffn1': 'new_m', 'new_m_w2_ffn1': 'new_m', 'new_m_g_mix': 'new_m', 'new_m_w_in': 'new_m', 'new_m_conv_qkv': 'new_m', 'new_m_a_log': 'new_m', 'new_m_dt_bias': 'new_m', 'new_m_g_onorm': 'new_m', 'new_m_lam_re': 'new_m', 'new_m_lam_im': 'new_m', 'new_m_log_step': 'new_m', 'new_m_b_re': 'new_m', 'new_m_b_im': 'new_m', 'new_m_c_re': 'new_m', 'new_m_c_im': 'new_m', 'new_m_d_skip': 'new_m', 'new_m_w_glu': 'new_m', 'new_m_b_glu': 'new_m', 'new_m_w_proj_a': 'new_m', 'new_m_w_proj_b': 'new_m', 'new_m_w_out': 'new_m', 'new_m_g_ffn2': 'new_m', 'new_m_w1_ffn2': 'new_m', 'new_m_w3_ffn2': 'new_m', 'new_m_w2_ffn2': 'new_m', 'new_m_g_final': 'new_m', 'new_v_w_ada': 'new_v', 'new_v_b_ada': 'new_v', 'new_v_g_ffn1': 'new_v', 'new_v_w1_ffn1': 'new_v', 'new_v_w3_ffn1': 'new_v', 'new_v_w2_ffn1': 'new_v', 'new_v_g_mix': 'new_v', 'new_v_w_in': 'new_v', 'new_v_conv_qkv': 'new_v', 'new_v_a_log': 'new_v', 'new_v_dt_bias': 'new_v', 'new_v_g_onorm': 'new_v', 'new_v_lam_re': 'new_v', 'new_v_lam_im': 'new_v', 'new_v_log_step': 'new_v', 'new_v_b_re': 'new_v', 'new_v_b_im': 'new_v', 'new_v_c_re': 'new_v', 'new_v_c_im': 'new_v', 'new_v_d_skip': 'new_v', 'new_v_w_glu': 'new_v', 'new_v_b_glu': 'new_v', 'new_v_w_proj_a': 'new_v', 'new_v_w_proj_b': 'new_v', 'new_v_w_out': 'new_v', 'new_v_g_ffn2': 'new_v', 'new_v_w1_ffn2': 'new_v', 'new_v_w3_ffn2': 'new_v', 'new_v_w2_ffn2': 'new_v', 'new_v_g_final': 'new_v'}


def _forward(args):
    return _fwd_reference(*[args[k] for k in FWD_PARAMS])


def _output_shape():
    out = _jax.eval_shape(lambda: _forward(_fwd_setup_inputs(0)))
    return out.shape, out.dtype

N_MICROBATCH = 1
ADAM_LR = 0.001
ADAM_B1 = 0.9
ADAM_B2 = 0.999
ADAM_EPS = 1e-08
ADAM_WD = 0.01
ADAM_STEP = 10
PER_EXAMPLE_BATCH_AXIS = {'x': 0, 'c': 0, 'loss_target': 0}
SHARED_INPUTS = []
_WEIGHT_DTYPES = {'w_ada': _jnp.float32, 'b_ada': _jnp.float32, 'g_ffn1': _jnp.float32, 'w1_ffn1': _jnp.float32, 'w3_ffn1': _jnp.float32, 'w2_ffn1': _jnp.float32, 'g_mix': _jnp.float32, 'w_in': _jnp.float32, 'conv_qkv': _jnp.float32, 'a_log': _jnp.float32, 'dt_bias': _jnp.float32, 'g_onorm': _jnp.float32, 'lam_re': _jnp.float32, 'lam_im': _jnp.float32, 'log_step': _jnp.float32, 'b_re': _jnp.float32, 'b_im': _jnp.float32, 'c_re': _jnp.float32, 'c_im': _jnp.float32, 'd_skip': _jnp.float32, 'w_glu': _jnp.float32, 'b_glu': _jnp.float32, 'w_proj_a': _jnp.float32, 'w_proj_b': _jnp.float32, 'w_out': _jnp.float32, 'g_ffn2': _jnp.float32, 'w1_ffn2': _jnp.float32, 'w3_ffn2': _jnp.float32, 'w2_ffn2': _jnp.float32, 'g_final': _jnp.float32}
MOMENT_SCALE = {'w_ada': 4.014851e-02, 'b_ada': 6.541862e-02, 'g_ffn1': 4.027564e-02, 'w1_ffn1': 1.729995e-02, 'w3_ffn1': 1.673267e-02, 'w2_ffn1': 2.776548e-02, 'g_mix': 5.640719e-02, 'w_in': 2.731324e-02, 'conv_qkv': 3.722900e-02, 'a_log': 2.107022e-01, 'dt_bias': 2.017889e-01, 'g_onorm': 1.718658e-01, 'lam_re': 1.950056e-03, 'lam_im': 3.204773e-03, 'log_step': 8.389733e-01, 'b_re': 1.483608e-03, 'b_im': 1.660033e-03, 'c_re': 1.995701e-03, 'c_im': 2.073142e-03, 'd_skip': 2.654376e-02, 'w_glu': 7.797633e-03, 'b_glu': 1.174565e-02, 'w_proj_a': 2.803724e-02, 'w_proj_b': 1.595062e-02, 'w_out': 3.301424e-02, 'g_ffn2': 3.837919e-02, 'w1_ffn2': 1.689890e-02, 'w3_ffn2': 1.639690e-02, 'w2_ffn2': 2.724278e-02, 'g_final': 6.389527e+01}


def _to_microbatches(a, axis):
    t = _jnp.moveaxis(a, axis, 0)
    t = t.reshape((N_MICROBATCH, t.shape[0] // N_MICROBATCH) + t.shape[1:])
    return _jnp.moveaxis(t, 1, axis + 1)


def setup_inputs(seed: int = 0) -> dict:
    inp = _fwd_setup_inputs(seed)
    key = _jax.random.fold_in(_jax.random.key(seed), 7919)
    shape, _ = _output_shape()
    out = dict(inp)
    out["loss_target"] = _jax.random.normal(_jax.random.fold_in(key, 0), shape, _jnp.float32)
    for i, name in enumerate(TWIN_WEIGHTS):
        w = inp[name].astype(_jnp.float32)
        if MOMENT_SCALE is None:
            s = _jnp.sqrt(_jnp.mean(_jnp.square(w)) + 1e-30)
        else:
            s = MOMENT_SCALE[name]
        km, kv = _jax.random.split(_jax.random.fold_in(key, i + 1))
        out[name] = w
        out["m_" + name] = s * _jax.random.normal(km, w.shape, _jnp.float32)
        out["v_" + name] = (s * s) * _jax.random.uniform(kv, w.shape, _jnp.float32, 0.5, 1.5)
    if N_MICROBATCH > 1:
        for name, axis in PER_EXAMPLE_BATCH_AXIS.items():
            out[name] = _to_microbatches(out[name], axis)
    return {'x': out['x'], 'c': out['c'], 'w_ada': out['w_ada'], 'b_ada': out['b_ada'], 'g_ffn1': out['g_ffn1'], 'w1_ffn1': out['w1_ffn1'], 'w3_ffn1': out['w3_ffn1'], 'w2_ffn1': out['w2_ffn1'], 'g_mix': out['g_mix'], 'w_in': out['w_in'], 'conv_qkv': out['conv_qkv'], 'a_log': out['a_log'], 'dt_bias': out['dt_bias'], 'g_onorm': out['g_onorm'], 'lam_re': out['lam_re'], 'lam_im': out['lam_im'], 'log_step': out['log_step'], 'b_re': out['b_re'], 'b_im': out['b_im'], 'c_re': out['c_re'], 'c_im': out['c_im'], 'd_skip': out['d_skip'], 'w_glu': out['w_glu'], 'b_glu': out['b_glu'], 'w_proj_a': out['w_proj_a'], 'w_proj_b': out['w_proj_b'], 'w_out': out['w_out'], 'g_ffn2': out['g_ffn2'], 'w1_ffn2': out['w1_ffn2'], 'w3_ffn2': out['w3_ffn2'], 'w2_ffn2': out['w2_ffn2'], 'g_final': out['g_final'], 'loss_target': out['loss_target'], 'm_w_ada': out['m_w_ada'], 'm_b_ada': out['m_b_ada'], 'm_g_ffn1': out['m_g_ffn1'], 'm_w1_ffn1': out['m_w1_ffn1'], 'm_w3_ffn1': out['m_w3_ffn1'], 'm_w2_ffn1': out['m_w2_ffn1'], 'm_g_mix': out['m_g_mix'], 'm_w_in': out['m_w_in'], 'm_conv_qkv': out['m_conv_qkv'], 'm_a_log': out['m_a_log'], 'm_dt_bias': out['m_dt_bias'], 'm_g_onorm': out['m_g_onorm'], 'm_lam_re': out['m_lam_re'], 'm_lam_im': out['m_lam_im'], 'm_log_step': out['m_log_step'], 'm_b_re': out['m_b_re'], 'm_b_im': out['m_b_im'], 'm_c_re': out['m_c_re'], 'm_c_im': out['m_c_im'], 'm_d_skip': out['m_d_skip'], 'm_w_glu': out['m_w_glu'], 'm_b_glu': out['m_b_glu'], 'm_w_proj_a': out['m_w_proj_a'], 'm_w_proj_b': out['m_w_proj_b'], 'm_w_out': out['m_w_out'], 'm_g_ffn2': out['m_g_ffn2'], 'm_w1_ffn2': out['m_w1_ffn2'], 'm_w3_ffn2': out['m_w3_ffn2'], 'm_w2_ffn2': out['m_w2_ffn2'], 'm_g_final': out['m_g_final'], 'v_w_ada': out['v_w_ada'], 'v_b_ada': out['v_b_ada'], 'v_g_ffn1': out['v_g_ffn1'], 'v_w1_ffn1': out['v_w1_ffn1'], 'v_w3_ffn1': out['v_w3_ffn1'], 'v_w2_ffn1': out['v_w2_ffn1'], 'v_g_mix': out['v_g_mix'], 'v_w_in': out['v_w_in'], 'v_conv_qkv': out['v_conv_qkv'], 'v_a_log': out['v_a_log'], 'v_dt_bias': out['v_dt_bias'], 'v_g_onorm': out['v_g_onorm'], 'v_lam_re': out['v_lam_re'], 'v_lam_im': out['v_lam_im'], 'v_log_step': out['v_log_step'], 'v_b_re': out['v_b_re'], 'v_b_im': out['v_b_im'], 'v_c_re': out['v_c_re'], 'v_c_im': out['v_c_im'], 'v_d_skip': out['v_d_skip'], 'v_w_glu': out['v_w_glu'], 'v_b_glu': out['v_b_glu'], 'v_w_proj_a': out['v_w_proj_a'], 'v_w_proj_b': out['v_w_proj_b'], 'v_w_out': out['v_w_out'], 'v_g_ffn2': out['v_g_ffn2'], 'v_w1_ffn2': out['v_w1_ffn2'], 'v_w3_ffn2': out['v_w3_ffn2'], 'v_w2_ffn2': out['v_w2_ffn2'], 'v_g_final': out['v_g_final']}


def _loss(weights, diff, rest, loss_target):
    with _jax.named_scope("forward"):
        args = {**rest, TWIN_DIFF_INPUT: diff, **{k: w.astype(_WEIGHT_DTYPES[k]) for k, w in weights.items()}}
        y = _forward(args)
    with _jax.named_scope("loss_head"):
        err = _jnp.square(y.astype(_jnp.float32) - loss_target)
        return 0.5 * _jnp.sum(_jnp.mean(err, axis=-1)) if err.ndim else 0.5 * err


def _adamw(w, g, m, v):
    m = ADAM_B1 * m + (1.0 - ADAM_B1) * g
    v = ADAM_B2 * v + (1.0 - ADAM_B2) * _jnp.square(g)
    m_hat = m / (1.0 - ADAM_B1 ** ADAM_STEP)
    v_hat = v / (1.0 - ADAM_B2 ** ADAM_STEP)
    delta = -ADAM_LR * (m_hat / (_jnp.sqrt(v_hat) + ADAM_EPS) + ADAM_WD * w)
    return delta, m, v


def reference(x, c, w_ada, b_ada, g_ffn1, w1_ffn1, w3_ffn1, w2_ffn1, g_mix, w_in, conv_qkv, a_log, dt_bias, g_onorm, lam_re, lam_im, log_step, b_re, b_im, c_re, c_im, d_skip, w_glu, b_glu, w_proj_a, w_proj_b, w_out, g_ffn2, w1_ffn2, w3_ffn2, w2_ffn2, g_final, loss_target, m_w_ada, m_b_ada, m_g_ffn1, m_w1_ffn1, m_w3_ffn1, m_w2_ffn1, m_g_mix, m_w_in, m_conv_qkv, m_a_log, m_dt_bias, m_g_onorm, m_lam_re, m_lam_im, m_log_step, m_b_re, m_b_im, m_c_re, m_c_im, m_d_skip, m_w_glu, m_b_glu, m_w_proj_a, m_w_proj_b, m_w_out, m_g_ffn2, m_w1_ffn2, m_w3_ffn2, m_w2_ffn2, m_g_final, v_w_ada, v_b_ada, v_g_ffn1, v_w1_ffn1, v_w3_ffn1, v_w2_ffn1, v_g_mix, v_w_in, v_conv_qkv, v_a_log, v_dt_bias, v_g_onorm, v_lam_re, v_lam_im, v_log_step, v_b_re, v_b_im, v_c_re, v_c_im, v_d_skip, v_w_glu, v_b_glu, v_w_proj_a, v_w_proj_b, v_w_out, v_g_ffn2, v_w1_ffn2, v_w3_ffn2, v_w2_ffn2, v_g_final):
    given = dict(x=x, c=c, w_ada=w_ada, b_ada=b_ada, g_ffn1=g_ffn1, w1_ffn1=w1_ffn1, w3_ffn1=w3_ffn1, w2_ffn1=w2_ffn1, g_mix=g_mix, w_in=w_in, conv_qkv=conv_qkv, a_log=a_log, dt_bias=dt_bias, g_onorm=g_onorm, lam_re=lam_re, lam_im=lam_im, log_step=log_step, b_re=b_re, b_im=b_im, c_re=c_re, c_im=c_im, d_skip=d_skip, w_glu=w_glu, b_glu=b_glu, w_proj_a=w_proj_a, w_proj_b=w_proj_b, w_out=w_out, g_ffn2=g_ffn2, w1_ffn2=w1_ffn2, w3_ffn2=w3_ffn2, w2_ffn2=w2_ffn2, g_final=g_final, loss_target=loss_target, m_w_ada=m_w_ada, m_b_ada=m_b_ada, m_g_ffn1=m_g_ffn1, m_w1_ffn1=m_w1_ffn1, m_w3_ffn1=m_w3_ffn1, m_w2_ffn1=m_w2_ffn1, m_g_mix=m_g_mix, m_w_in=m_w_in, m_conv_qkv=m_conv_qkv, m_a_log=m_a_log, m_dt_bias=m_dt_bias, m_g_onorm=m_g_onorm, m_lam_re=m_lam_re, m_lam_im=m_lam_im, m_log_step=m_log_step, m_b_re=m_b_re, m_b_im=m_b_im, m_c_re=m_c_re, m_c_im=m_c_im, m_d_skip=m_d_skip, m_w_glu=m_w_glu, m_b_glu=m_b_glu, m_w_proj_a=m_w_proj_a, m_w_proj_b=m_w_proj_b, m_w_out=m_w_out, m_g_ffn2=m_g_ffn2, m_w1_ffn2=m_w1_ffn2, m_w3_ffn2=m_w3_ffn2, m_w2_ffn2=m_w2_ffn2, m_g_final=m_g_final, v_w_ada=v_w_ada, v_b_ada=v_b_ada, v_g_ffn1=v_g_ffn1, v_w1_ffn1=v_w1_ffn1, v_w3_ffn1=v_w3_ffn1, v_w2_ffn1=v_w2_ffn1, v_g_mix=v_g_mix, v_w_in=v_w_in, v_conv_qkv=v_conv_qkv, v_a_log=v_a_log, v_dt_bias=v_dt_bias, v_g_onorm=v_g_onorm, v_lam_re=v_lam_re, v_lam_im=v_lam_im, v_log_step=v_log_step, v_b_re=v_b_re, v_b_im=v_b_im, v_c_re=v_c_re, v_c_im=v_c_im, v_d_skip=v_d_skip, v_w_glu=v_w_glu, v_b_glu=v_b_glu, v_w_proj_a=v_w_proj_a, v_w_proj_b=v_w_proj_b, v_w_out=v_w_out, v_g_ffn2=v_g_ffn2, v_w1_ffn2=v_w1_ffn2, v_w3_ffn2=v_w3_ffn2, v_w2_ffn2=v_w2_ffn2, v_g_final=v_g_final)
    weights = {n: given[n] for n in TWIN_WEIGHTS}
    shared = {n: given[n] for n in SHARED_INPUTS}
    per_example = {n: given[n] for n in ['x', 'c']}
    grad_fn = _jax.value_and_grad(_loss, argnums=(0, 1))

    def one_microbatch(ex, loss_target):
        ex = dict(ex)
        diff = ex.pop(TWIN_DIFF_INPUT)
        return grad_fn(weights, diff, {**shared, **ex}, loss_target)

    if N_MICROBATCH == 1:
        loss, (grad_w, grad_x) = one_microbatch(per_example, given["loss_target"])
    else:
        def body(carry, xs):
            loss_sum, grad_sum = carry
            l_k, (gw_k, gx_k) = one_microbatch(xs[0], xs[1])
            with _jax.named_scope("update"):
                return (loss_sum + l_k, _jax.tree.map(_jnp.add, grad_sum, gw_k)), gx_k

        init = (_jnp.zeros((), _jnp.float32), _jax.tree.map(_jnp.zeros_like, weights))
        (loss, grad_w), grad_x = _jax.lax.scan(body, init, (per_example, given["loss_target"]))
    with _jax.named_scope("update"):
        delta_w, new_m, new_v = {}, {}, {}
        for n in TWIN_WEIGHTS:
            delta_w[n], new_m[n], new_v[n] = _adamw(weights[n], grad_w[n], given["m_" + n], given["v_" + n])
    return (loss, grad_x, *[grad_w[n] for n in TWIN_WEIGHTS], *[delta_w[n] for n in TWIN_WEIGHTS],
            *[new_m[n] for n in TWIN_WEIGHTS], *[new_v[n] for n in TWIN_WEIGHTS])
```

```python
import functools
import math

import jax
import jax.numpy as jnp
from jax import lax
from jax.experimental import pallas as pl
from jax.experimental.pallas import tpu as pltpu

F32 = jnp.float32
BF16 = jnp.bfloat16
HI = lax.Precision.HIGHEST
SDS = jax.ShapeDtypeStruct

D = 1024
FF = 2816
NH = 8
DH = 64
DNW = NH * DH
CONVW = 4
CH = 64
S5W = 512
S5G = 32
S5P = 64
S5C = 16
S5N = S5G * S5P
GB = 4
NDEV = 8
EPS = 1e-6
LANES = 128
ROW = 1024
VMEM_LIMIT = 56 * 1024 * 1024

ADAM_LR, ADAM_B1, ADAM_B2, ADAM_EPS, ADAM_WD, ADAM_STEP = 0.001, 0.9, 0.999, 1e-08, 0.01, 10

WEIGHTS = ['w_ada', 'b_ada', 'g_ffn1', 'w1_ffn1', 'w3_ffn1', 'w2_ffn1', 'g_mix', 'w_in', 'conv_qkv', 'a_log',
           'dt_bias', 'g_onorm', 'lam_re', 'lam_im', 'log_step', 'b_re', 'b_im', 'c_re', 'c_im', 'd_skip', 'w_glu',
           'b_glu', 'w_proj_a', 'w_proj_b', 'w_out', 'g_ffn2', 'w1_ffn2', 'w3_ffn2', 'w2_ffn2', 'g_final']
RS_WEIGHTS = ['w1_ffn1', 'w3_ffn1', 'w2_ffn1', 'w_in', 'w_glu', 'w_proj_a', 'w_proj_b', 'w_out', 'w1_ffn2', 'w3_ffn2',
              'w2_ffn2']
COL_SHARDED = {'w1_ffn1', 'w3_ffn1', 'w_in', 'w_proj_a', 'w_proj_b', 'w1_ffn2', 'w3_ffn2'}
SMALL = ['g_ffn1', 'g_mix', 'a_log', 'dt_bias', 'g_onorm', 'lam_re', 'lam_im', 'log_step', 'b_re', 'b_im', 'c_re',
         'c_im', 'd_skip', 'b_glu', 'g_ffn2', 'g_final']


def _cp(n_grid=0):
    if n_grid:
        return pltpu.CompilerParams(vmem_limit_bytes=VMEM_LIMIT, dimension_semantics=("arbitrary",) * n_grid)
    return pltpu.CompilerParams(vmem_limit_bytes=VMEM_LIMIT)


def _dot(a, b):
    return jnp.dot(a.astype(BF16), b.astype(BF16), preferred_element_type=F32)


def _dot_nt(a, b):
    return lax.dot_general(a.astype(BF16), b.astype(BF16), (((1,), (1,)), ((), ())), preferred_element_type=F32)


def _dot_tn(a, b):
    return lax.dot_general(a.astype(BF16), b.astype(BF16), (((0,), (0,)), ((), ())), preferred_element_type=F32)


def _dot_hi(a, b):
    return jnp.dot(a, b, precision=HI, preferred_element_type=F32)


@jax.custom_vjp
def bdot(a, b):
    return _dot(a, b)


bdot.defvjp(lambda a, b: (_dot(a, b), (a, b)),
            lambda r, g: (_dot_nt(g, r[1]).astype(r[0].dtype), _dot_tn(r[0], g).astype(r[1].dtype)))


@jax.custom_vjp
def bdot_nt(a, b):
    return _dot_nt(a, b)


bdot_nt.defvjp(lambda a, b: (_dot_nt(a, b), (a, b)),
               lambda r, g: (_dot(g, r[1]).astype(r[0].dtype), _dot_tn(g, r[0]).astype(r[1].dtype)))


@jax.custom_vjp
def bdot_tn(a, b):
    return _dot_tn(a, b)


bdot_tn.defvjp(lambda a, b: (_dot_tn(a, b), (a, b)),
               lambda r, g: (_dot_nt(r[1], g).astype(r[0].dtype), _dot(r[0], g).astype(r[1].dtype)))


def _silu(x):
    return x * jax.nn.sigmoid(x)


def _iota2(shape, axis):
    return lax.broadcasted_iota(jnp.int32, shape, axis)


def normmod(h, g, sc, sh):
    y = h * lax.rsqrt(jnp.mean(h * h, axis=-1, keepdims=True) + EPS) * g
    return y * (1.0 + sc) + sh


def fn_normmod(h, sh, sc, g):
    return (normmod(h, g, sc, sh),)


def fn_resid(mo, gt):
    return (gt * mo,)


def fn_merge(gate, ya, yb):
    return (jax.nn.sigmoid(gate[:, :D]) * ya + jax.nn.sigmoid(gate[:, D:]) * yb,)


def fn_glu(y, w, b):
    ge = jax.nn.gelu(y)
    return (ge * jax.nn.sigmoid(bdot(ge, w) + b),)


def fn_onorm(o, z, g_on):
    r = _iota2((DH, DNW), 0)
    c = _iota2((DH, DNW), 1)
    expand = (c % DH == r).astype(F32)
    r2 = _iota2((DNW, DNW), 0)
    c2 = _iota2((DNW, DNW), 1)
    avg = (r2 // DH == c2 // DH).astype(F32) * (1.0 / DH)
    ms = _dot_hi(o * o, avg)
    return (o * lax.rsqrt(ms + EPS) * _dot_hi(g_on, expand) * _silu(z),)


def gate_fn(small, alp, dtp):
    beta = jax.nn.sigmoid(small)
    la = -jnp.exp(alp) * jax.nn.softplus(small + dtp)
    tri = (_iota2((CH, CH), 0) >= _iota2((CH, CH), 1)).astype(F32)
    gc = _dot_hi(tri, la)
    gct = lax.dot_general(la, tri, (((0,), (1,)), ((), ())), precision=HI, preferred_element_type=F32)
    return beta, gc, gct


def _unit_lower_inverse(a):
    r = _iota2((CH, CH), 0)
    c = _iota2((CH, CH), 1)
    eye = (r == c).astype(F32)
    d = jnp.where(r // 8 == c // 8, a, 0.0)
    inv = eye - d
    p = d
    for _ in range(2):
        p = _dot_hi(p, p)
        inv = inv + _dot_hi(inv, p)
    for blk in (16, 32, 64):
        off = jnp.where((r // blk == c // blk) & (r // (blk // 2) != c // (blk // 2)), a, 0.0)
        inv = inv - _dot_hi(_dot_hi(inv, off), inv)
    return inv


def _conv_act(x, w):
    c = x[5:69] * w[0:1] + x[6:70] * w[1:2] + x[7:71] * w[2:3] + x[8:72] * w[3:4]
    return _silu(c)


def head_fn(xq, xk, xv, wq, wk, wv, b, g, gt, s_prev):
    q = _conv_act(xq, wq)
    k = _conv_act(xk, wk)
    v = _conv_act(xv, wv)
    q = q * lax.rsqrt(jnp.sum(q * q, axis=-1, keepdims=True) + EPS) * (DH ** -0.5)
    k = k * lax.rsqrt(jnp.sum(k * k, axis=-1, keepdims=True) + EPS)
    r = _iota2((CH, CH), 0)
    c = _iota2((CH, CH), 1)
    causal = r >= c
    dec = jnp.where(causal, jnp.exp(jnp.where(causal, g - gt, 0.0)), 0.0)
    kb = k * b
    a = jnp.where(r > c, bdot_nt(kb, k) * dec, 0.0)
    tinv = _unit_lower_inverse(a)
    eg = jnp.exp(g)
    u = _dot_hi(tinv, v * b)
    w = _dot_hi(tinv, kb * eg)
    attn = bdot_nt(q, k) * dec
    g_last = g[CH - 1:CH]
    v_new = u - bdot(w, s_prev)
    o = bdot(q * eg, s_prev) + bdot(attn, v_new)
    s_new = s_prev * jnp.exp(g_last) + bdot_tn(k * jnp.exp(g_last - g), v_new)
    return o, s_new


def s5_chunk(u, xp_re, xp_im, bb_re, bb_im, cc_re, cc_im, p0r, p0i, p1r, p1i, pir, pii, dsk):
    bu_re = bdot(u, bb_re)
    bu_im = bdot(u, bb_im)
    xt_re = pir * bu_re - pii * bu_im
    xt_im = pir * bu_im + pii * bu_re
    tri = (_iota2((CH, CH), 0) >= _iota2((CH, CH), 1)).astype(F32)
    cs_re = _dot_hi(tri, xt_re)
    cs_im = _dot_hi(tri, xt_im)
    x_re = p0r * cs_re - p0i * cs_im + p1r * xp_re - p1i * xp_im
    x_im = p0r * cs_im + p0i * cs_re + p1r * xp_im + p1i * xp_re
    y = bdot_nt(x_re, cc_re) - bdot_nt(x_im, cc_im) + dsk * u
    return y, x_re[CH - 1:CH], x_im[CH - 1:CH]


def s5_tables(lam_re, lam_im, log_step, bre, bim, cre, cim):
    expand = (_iota2((S5G, S5N), 1) // S5P == _iota2((S5G, S5N), 0)).astype(F32)
    step = _dot_hi(jnp.exp(log_step), expand)
    lre = jnp.minimum(lam_re, -1e-4)
    lr = lre * step
    ang = lam_im * step
    mag = jnp.exp(lr)
    lb_re = mag * jnp.cos(ang)
    lb_im = mag * jnp.sin(ang)
    den = lre * lre + lam_im * lam_im
    coef_re = ((lb_re - 1.0) * lre + lb_im * lam_im) / den
    coef_im = (lb_im * lre - (lb_re - 1.0) * lam_im) / den
    bb_re = coef_re * bre - coef_im * bim
    bb_im = coef_re * bim + coef_im * bre
    j = _iota2((CH, 1), 0).astype(F32)
    e0 = jnp.exp(j * lr)
    e1 = jnp.exp((j + 1.0) * lr)
    ei = jnp.exp(-j * lr)
    mask = (_iota2((LANES, 512), 0) // S5C == _iota2((LANES, 512), 1) // S5P).astype(F32)

    def blocks(t):
        return jnp.stack([jnp.tile(t[:, gb * 512:(gb + 1) * 512], (LANES // S5C, 1)) * mask for gb in range(GB)])

    return (blocks(bb_re), blocks(bb_im), blocks(cre), blocks(cim),
            e0 * jnp.cos(j * ang), e0 * jnp.sin(j * ang),
            e1 * jnp.cos((j + 1.0) * ang), e1 * jnp.sin((j + 1.0) * ang),
            ei * jnp.cos(j * ang), -ei * jnp.sin(j * ang))


def _row_specs(tiled, batch, bcast, tm, tpb):
    specs = [pl.BlockSpec((tm, a.shape[1]), lambda i: (i, 0)) for a in tiled]
    specs += [pl.BlockSpec((None,) + a.shape[1:], lambda i: (i // tpb, 0, 0)) for a in batch]
    specs += [pl.BlockSpec(a.shape, lambda i, nd=a.ndim: (0,) * nd) for a in bcast]
    return specs


def ew_call(name, fn, tiled, batch, bcast, outs, tm, seq):
    t_rows = tiled[0].shape[0]
    n_in = len(tiled) + len(batch) + len(bcast)

    def body(*refs):
        vals = [r[...].astype(F32) for r in refs[:n_in]]
        for r, o in zip(refs[n_in:], fn(*vals)):
            r[...] = o.astype(r.dtype)

    return pl.pallas_call(
        body, grid=(t_rows // tm,), in_specs=_row_specs(tiled, batch, bcast, tm, seq // tm),
        out_specs=[pl.BlockSpec((tm, w), lambda i: (i, 0)) for w, _ in outs],
        out_shape=[SDS((t_rows, w), dt) for w, dt in outs], name=name, compiler_params=_cp(1))(*tiled, *batch, *bcast)


def ew_vjp_call(name, fn, tiled, batch, bcast, cts, want, tm, seq):
    t_rows = tiled[0].shape[0]
    tpb = seq // tm
    n_t, n_b, n_c = len(tiled), len(batch), len(bcast)
    n_in = n_t + n_b + n_c

    def body(*refs):
        i = pl.program_id(0)
        vals = [r[...].astype(F32) for r in refs[:n_in]]
        ctv = tuple(r[...].astype(F32) for r in refs[n_in:n_in + len(cts)])
        outs = refs[n_in + len(cts):]
        _, vjp = jax.vjp(fn, *vals)
        grads = vjp(ctv)
        for r, (idx, _) in zip(outs[:len(want)], want):
            r[...] = grads[idx].astype(r.dtype)
        for k in range(n_b):
            r, g = outs[len(want) + k], grads[n_t + k]

            @pl.when(i % tpb == 0)
            def _(r=r, g=g):
                r[...] = g

            @pl.when(i % tpb != 0)
            def _(r=r, g=g):
                r[...] += g
        for k in range(n_c):
            r, g = outs[len(want) + n_b + k], grads[n_t + n_b + k]

            @pl.when(i == 0)
            def _(r=r, g=g):
                r[...] = g

            @pl.when(i != 0)
            def _(r=r, g=g):
                r[...] += g

    out_specs = [pl.BlockSpec((tm, tiled[idx].shape[1]), lambda i: (i, 0)) for idx, _ in want]
    out_specs += [pl.BlockSpec((None,) + a.shape[1:], lambda i: (i // tpb, 0, 0)) for a in batch]
    out_specs += [pl.BlockSpec(a.shape, lambda i, nd=a.ndim: (0,) * nd) for a in bcast]
    out_shape = [SDS(tiled[idx].shape, dt) for idx, dt in want]
    out_shape += [SDS(a.shape, F32) for a in batch] + [SDS(a.shape, F32) for a in bcast]
    res = pl.pallas_call(
        body, grid=(t_rows // tm,),
        in_specs=_row_specs(tiled, batch, bcast, tm, tpb) + [pl.BlockSpec((tm, a.shape[1]), lambda i: (i, 0)) for a in cts],
        out_specs=out_specs, out_shape=out_shape, name=name, compiler_params=_cp(1))(*tiled, *batch, *bcast, *cts)
    return res[:len(want)], res[len(want):len(want) + n_b], res[len(want) + n_b:]


def _pick(n, cands):
    for c in cands:
        if n % c == 0:
            return c
    return n


def mm(name, pairs, nt, out_dtype):
    m = pairs[0][0].shape[0]
    n = pairs[0][1].shape[0 if nt else 1]
    tm = _pick(m, (512, 256, 128))
    tn = _pick(n, (512, 256, 128))
    np_ = len(pairs)

    def body(*refs):
        acc = None
        for p in range(np_):
            a, b = refs[2 * p][...], refs[2 * p + 1][...]
            t = _dot_nt(a, b) if nt else _dot(a, b)
            acc = t if acc is None else acc + t
        refs[2 * np_][...] = acc.astype(out_dtype)

    in_specs, ops = [], []
    for a, b in pairs:
        k = a.shape[1]
        in_specs.append(pl.BlockSpec((tm, k), lambda i, j: (i, 0)))
        in_specs.append(pl.BlockSpec((tn, k), lambda i, j: (j, 0)) if nt else pl.BlockSpec((k, tn), lambda i, j: (0, j)))
        ops += [a, b]
    return pl.pallas_call(
        body, grid=(m // tm, n // tn), in_specs=in_specs, out_specs=pl.BlockSpec((tm, tn), lambda i, j: (i, j)),
        out_shape=SDS((m, n), out_dtype), name=name, compiler_params=_cp(2))(*ops)


def mm_tn(name, a, b):
    t_rows, m = a.shape
    n = b.shape[1]
    tm = m if m <= 1024 else _pick(m, (1024, 512, 256))
    tn = _pick(n, (512, 256, 128))
    tk = _pick(t_rows, (512, 256, 128, 64))

    def body(a_ref, b_ref, o_ref):
        @pl.when(pl.program_id(2) == 0)
        def _():
            o_ref[...] = jnp.zeros_like(o_ref)

        o_ref[...] += _dot_tn(a_ref[...], b_ref[...])

    return pl.pallas_call(
        body, grid=(m // tm, n // tn, t_rows // tk),
        in_specs=[pl.BlockSpec((tk, tm), lambda i, j, k: (k, i)), pl.BlockSpec((tk, tn), lambda i, j, k: (k, j))],
        out_specs=pl.BlockSpec((tm, tn), lambda i, j, k: (i, j)), out_shape=SDS((m, n), F32), name=name,
        compiler_params=_cp(3))(a, b)


def ffn_fwd(name, h, mod3, g, w1, w3, w2, seq):
    t_rows = h.shape[0]
    tm = _pick(seq, (512, 256, 128, 64))
    tf = 256
    tpb = seq // tm
    nf = FF // tf

    def body(h_ref, mod_ref, g_ref, w1_ref, w3_ref, w2_ref, ho_ref, f_ref, u_ref, acc):
        j = pl.program_id(1)

        @pl.when(j == 0)
        def _():
            u_ref[...] = normmod(h_ref[...], g_ref[...], mod_ref[1:2, :], mod_ref[0:1, :]).astype(BF16)
            acc[...] = jnp.zeros_like(acc)

        u = u_ref[...]
        a = _silu(_dot(u, w1_ref[...])) * _dot(u, w3_ref[...])
        acc[...] += _dot(a, w2_ref[...])

        @pl.when(j == nf - 1)
        def _():
            f_ref[...] = acc[...]
            ho_ref[...] = h_ref[...] + 0.5 * mod_ref[2:3, :] * acc[...]

    row = lambda i, j: (i, 0)
    return pl.pallas_call(
        body, grid=(t_rows // tm, nf),
        in_specs=[pl.BlockSpec((tm, D), row), pl.BlockSpec((None, 3, D), lambda i, j: (i // tpb, 0, 0)),
                  pl.BlockSpec((1, D), lambda i, j: (0, 0)), pl.BlockSpec((D, tf), lambda i, j: (0, j)),
                  pl.BlockSpec((D, tf), lambda i, j: (0, j)), pl.BlockSpec((tf, D), lambda i, j: (j, 0))],
        out_specs=[pl.BlockSpec((tm, D), row), pl.BlockSpec((tm, D), row), pl.BlockSpec((tm, D), row)],
        out_shape=[SDS((t_rows, D), F32), SDS((t_rows, D), F32), SDS((t_rows, D), BF16)],
        scratch_shapes=[pltpu.VMEM((tm, D), F32)], name=name, compiler_params=_cp(2))(h, mod3, g, w1, w3, w2)


def ffn_bwd(name, dho, h, f_out, u, mod3, g, w1, w3, w2, seq):
    t_rows = h.shape[0]
    tm = _pick(seq, (512, 256, 128, 64))
    tf = 256
    tpb = seq // tm
    nf = FF // tf

    def body(dho_ref, h_ref, f_ref, u_ref, mod_ref, g_ref, w1_ref, w3_ref, w2_ref,
             dh_ref, a_ref, dh1_ref, dh3_ref, df_scr, dmod_ref, dg_ref, du_acc):
        i, j = pl.program_id(0), pl.program_id(1)

        @pl.when(j == 0)
        def _():
            df_scr[...] = (0.5 * mod_ref[2:3, :] * dho_ref[...]).astype(BF16)
            du_acc[...] = jnp.zeros_like(du_acc)

        uu = u_ref[...]
        h1 = _dot(uu, w1_ref[...])
        h3 = _dot(uu, w3_ref[...])
        sg = jax.nn.sigmoid(h1)
        s = h1 * sg
        da = _dot_nt(df_scr[...], w2_ref[...])
        dh3 = (da * s).astype(BF16)
        dh1 = (da * h3 * (sg * (1.0 + h1 * (1.0 - sg)))).astype(BF16)
        a_ref[...] = (s * h3).astype(BF16)
        dh1_ref[...] = dh1
        dh3_ref[...] = dh3
        du_acc[...] += _dot_nt(dh1, w1_ref[...]) + _dot_nt(dh3, w3_ref[...])

        @pl.when(j == nf - 1)
        def _():
            _, vjp = jax.vjp(normmod, h_ref[...], g_ref[...], mod_ref[1:2, :], mod_ref[0:1, :])
            dh_n, dg, dsc, dsh = vjp(du_acc[...])
            dh_ref[...] = dho_ref[...] + dh_n
            dgt = jnp.sum(0.5 * dho_ref[...] * f_ref[...], axis=0, keepdims=True)
            dmod = jnp.concatenate([dsh, dsc, dgt], axis=0)

            @pl.when(i % tpb == 0)
            def _():
                dmod_ref[...] = dmod

            @pl.when(i % tpb != 0)
            def _():
                dmod_ref[...] += dmod

            @pl.when(i == 0)
            def _():
                dg_ref[...] = dg

            @pl.when(i != 0)
            def _():
                dg_ref[...] += dg

    row = lambda i, j: (i, 0)
    col = lambda i, j: (i, j)
    return pl.pallas_call(
        body, grid=(t_rows // tm, nf),
        in_specs=[pl.BlockSpec((tm, D), row), pl.BlockSpec((tm, D), row), pl.BlockSpec((tm, D), row),
                  pl.BlockSpec((tm, D), row), pl.BlockSpec((None, 3, D), lambda i, j: (i // tpb, 0, 0)),
                  pl.BlockSpec((1, D), lambda i, j: (0, 0)), pl.BlockSpec((D, tf), lambda i, j: (0, j)),
                  pl.BlockSpec((D, tf), lambda i, j: (0, j)), pl.BlockSpec((tf, D), lambda i, j: (j, 0))],
        out_specs=[pl.BlockSpec((tm, D), row), pl.BlockSpec((tm, tf), col), pl.BlockSpec((tm, tf), col),
                   pl.BlockSpec((tm, tf), col), pl.BlockSpec((tm, D), row),
                   pl.BlockSpec((None, 3, D), lambda i, j: (i // tpb, 0, 0)), pl.BlockSpec((1, D), lambda i, j: (0, 0))],
        out_shape=[SDS((t_rows, D), F32), SDS((t_rows, FF), BF16), SDS((t_rows, FF), BF16), SDS((t_rows, FF), BF16),
                   SDS((t_rows, D), BF16), SDS(mod3.shape, F32), SDS((1, D), F32)],
        scratch_shapes=[pltpu.VMEM((tm, D), F32)], name=name,
        compiler_params=_cp(2))(dho, h, f_out, u, mod3, g, w1, w3, w2)


def _head_inputs(raw_ref, halo_ref, conv_ref, hm, hd):
    xs, ws = [], []
    for part in range(3):
        cols = slice(part * DNW + hd * DH, part * DNW + (hd + 1) * DH)
        xs.append(jnp.concatenate([halo_ref[:, cols] * hm, raw_ref[:, cols]], axis=0))
        ws.append(conv_ref[0:CONVW, cols])
    return xs, ws


def _halo_map(nc, rev):
    def index(b, r):
        n = nc - 1 - r if rev else r
        return (jnp.maximum((b * nc + n) * (CH // 8) - 1, 0), 0)
    return index


def deltanet_fwd(p_dn, p_small, conv8, alp, dtp, bl, nc):
    t_rows = p_dn.shape[0]

    def body(raw_ref, halo_ref, small_ref, conv_ref, alp_ref, dtp_ref, o_ref, sprev_ref, s_scr):
        n = pl.program_id(1)

        @pl.when(n == 0)
        def _():
            s_scr[...] = jnp.zeros_like(s_scr)

        hm = (n > 0).astype(F32)
        beta, gc, gct = gate_fn(small_ref[...], alp_ref[...], dtp_ref[...])
        for hd in range(NH):
            xs, ws = _head_inputs(raw_ref, halo_ref, conv_ref, hm, hd)
            s_prev = s_scr[hd]
            sprev_ref[hd] = s_prev
            o, s_new = head_fn(*xs, *ws, beta[:, hd:hd + 1], gc[:, NH + hd:NH + hd + 1], gct[NH + hd:NH + hd + 1, :], s_prev)
            o_ref[:, hd * DH:(hd + 1) * DH] = o
            s_scr[hd] = s_new

    blk = lambda b, n: (b * nc + n, 0)
    const = lambda b, n: (0, 0)
    return pl.pallas_call(
        body, grid=(bl, nc),
        in_specs=[pl.BlockSpec((CH, 4 * DNW), blk), pl.BlockSpec((8, 3 * DNW), _halo_map(nc, False)),
                  pl.BlockSpec((CH, LANES), blk), pl.BlockSpec((8, 3 * DNW), const), pl.BlockSpec((1, LANES), const),
                  pl.BlockSpec((1, LANES), const)],
        out_specs=[pl.BlockSpec((CH, DNW), blk), pl.BlockSpec((None, NH, DH, DH), lambda b, n: (b * nc + n, 0, 0, 0))],
        out_shape=[SDS((t_rows, DNW), F32), SDS((bl * nc, NH, DH, DH), F32)],
        scratch_shapes=[pltpu.VMEM((NH, DH, DH), F32)], name="deltanet_fwd",
        compiler_params=_cp(2))(p_dn, p_dn, p_small, conv8, alp, dtp)


def deltanet_bwd(p_dn, p_small, conv8, alp, dtp, sprev, d_o, d_z, bl, nc):
    t_rows = p_dn.shape[0]

    def body(raw_ref, halo_ref, small_ref, conv_ref, alp_ref, dtp_ref, sprev_ref, do_ref, dz_ref,
             draw_ref, dsmall_ref, dconv_ref, dalp_ref, ddtp_ref, ds_scr, dhalo_scr):
        b, r = pl.program_id(0), pl.program_id(1)
        n = nc - 1 - r

        @pl.when((b == 0) & (r == 0))
        def _():
            dconv_ref[...] = jnp.zeros_like(dconv_ref)
            dalp_ref[...] = jnp.zeros_like(dalp_ref)
            ddtp_ref[...] = jnp.zeros_like(ddtp_ref)

        @pl.when(r == 0)
        def _():
            ds_scr[...] = jnp.zeros_like(ds_scr)
            dhalo_scr[...] = jnp.zeros_like(dhalo_scr)

        hm = (n > 0).astype(F32)
        (beta, gc, gct), gate_vjp = jax.vjp(gate_fn, small_ref[...], alp_ref[...], dtp_ref[...])
        lane = _iota2((CH, LANES), 1)
        rowi = _iota2((LANES, CH), 0)
        d_beta = jnp.zeros((CH, LANES), F32)
        d_gc = jnp.zeros((CH, LANES), F32)
        d_gct = jnp.zeros((LANES, CH), F32)
        for hd in range(NH):
            xs, ws = _head_inputs(raw_ref, halo_ref, conv_ref, hm, hd)
            _, vjp = jax.vjp(head_fn, *xs, *ws, beta[:, hd:hd + 1], gc[:, NH + hd:NH + hd + 1],
                             gct[NH + hd:NH + hd + 1, :], sprev_ref[hd])
            grads = vjp((do_ref[:, hd * DH:(hd + 1) * DH], ds_scr[hd]))
            ds_scr[hd] = grads[9]
            for part in range(3):
                cols = slice(part * DNW + hd * DH, part * DNW + (hd + 1) * DH)
                dx = grads[part]
                draw_ref[0:CH - 8, cols] = dx[8:CH]
                draw_ref[CH - 8:CH, cols] = dx[CH:CH + 8] + dhalo_scr[:, cols]
                dhalo_scr[:, cols] = dx[0:8] * hm
                dconv_ref[0:CONVW, cols] += grads[3 + part]
            d_beta = d_beta + jnp.where(lane == hd, grads[6], 0.0)
            d_gc = d_gc + jnp.where(lane == NH + hd, grads[7], 0.0)
            d_gct = d_gct + jnp.where(rowi == NH + hd, grads[8], 0.0)
        d_small, d_alp, d_dtp = gate_vjp((d_beta, d_gc, d_gct))
        dsmall_ref[...] = d_small
        dalp_ref[...] += d_alp
        ddtp_ref[...] += d_dtp
        draw_ref[:, 3 * DNW:4 * DNW] = dz_ref[...]

    blk = lambda b, r: (b * nc + nc - 1 - r, 0)
    const = lambda b, r: (0, 0)
    return pl.pallas_call(
        body, grid=(bl, nc),
        in_specs=[pl.BlockSpec((CH, 4 * DNW), blk), pl.BlockSpec((8, 3 * DNW), _halo_map(nc, True)),
                  pl.BlockSpec((CH, LANES), blk), pl.BlockSpec((8, 3 * DNW), const), pl.BlockSpec((1, LANES), const),
                  pl.BlockSpec((1, LANES), const),
                  pl.BlockSpec((None, NH, DH, DH), lambda b, r: (b * nc + nc - 1 - r, 0, 0, 0)),
                  pl.BlockSpec((CH, DNW), blk), pl.BlockSpec((CH, DNW), blk)],
        out_specs=[pl.BlockSpec((CH, 4 * DNW), blk), pl.BlockSpec((CH, LANES), blk), pl.BlockSpec((8, 3 * DNW), const),
                   pl.BlockSpec((1, LANES), const), pl.BlockSpec((1, LANES), const)],
        out_shape=[SDS((t_rows, 4 * DNW), F32), SDS((t_rows, LANES), F32), SDS((8, 3 * DNW), F32), SDS((1, LANES), F32),
                   SDS((1, LANES), F32)],
        scratch_shapes=[pltpu.VMEM((NH, DH, DH), F32), pltpu.VMEM((8, 3 * DNW), F32)], name="deltanet_bwd",
        compiler_params=_cp(2))(p_dn, p_dn, p_small, conv8, alp, dtp, sprev, d_o, d_z)


def _s5_table_specs(rev_unused=None):
    tab3 = pl.BlockSpec((None, LANES, 512), lambda gb, b, n: (gb, 0, 0))
    tab2 = pl.BlockSpec((CH, 512), lambda gb, b, n: (0, gb))
    return [tab3] * 4 + [tab2] * 6 + [pl.BlockSpec((1, LANES), lambda gb, b, n: (0, gb))]


def s5_fwd(u, tables, dsk, bl, nc):
    t_rows = u.shape[0]

    def body(u_ref, *rest):
        tabs, (y_ref, xs_ref, x_scr) = rest[:11], rest[11:]
        n = pl.program_id(2)

        @pl.when(n == 0)
        def _():
            x_scr[...] = jnp.zeros_like(x_scr)

        xp_re, xp_im = x_scr[0:1, :], x_scr[1:2, :]
        xs_ref[...] = jnp.concatenate([xp_re, xp_im, jnp.zeros((6, 512), F32)], axis=0)
        y, xn_re, xn_im = s5_chunk(u_ref[...], xp_re, xp_im, *[t[...] for t in tabs])
        y_ref[...] = y
        x_scr[0:1, :] = xn_re
        x_scr[1:2, :] = xn_im

    blk = lambda gb, b, n: (b * nc + n, gb)
    return pl.pallas_call(
        body, grid=(GB, bl, nc), in_specs=[pl.BlockSpec((CH, LANES), blk)] + _s5_table_specs(),
        out_specs=[pl.BlockSpec((CH, LANES), blk), pl.BlockSpec((None, 8, 512), lambda gb, b, n: ((gb * bl + b) * nc + n, 0, 0))],
        out_shape=[SDS((t_rows, S5W), F32), SDS((GB * bl * nc, 8, 512), F32)],
        scratch_shapes=[pltpu.VMEM((8, 512), F32)], name="s5_fwd", compiler_params=_cp(3))(u, *tables, dsk)


def s5_bwd(u, tables, dsk, xs, dy, bl, nc):
    t_rows = u.shape[0]

    def body(u_ref, *rest):
        tabs, xs_ref, dy_ref = rest[:11], rest[11], rest[12]
        du_ref, dtabs, dx_scr = rest[13], rest[14:25], rest[25]
        b, r = pl.program_id(1), pl.program_id(2)

        @pl.when((b == 0) & (r == 0))
        def _():
            for t in dtabs:
                t[...] = jnp.zeros_like(t)

        @pl.when(r == 0)
        def _():
            dx_scr[...] = jnp.zeros_like(dx_scr)

        _, vjp = jax.vjp(s5_chunk, u_ref[...], xs_ref[0:1, :], xs_ref[1:2, :], *[t[...] for t in tabs])
        grads = vjp((dy_ref[...], dx_scr[0:1, :], dx_scr[1:2, :]))
        du_ref[...] = grads[0]
        dx_scr[0:1, :] = grads[1]
        dx_scr[1:2, :] = grads[2]
        for t, g in zip(dtabs, grads[3:]):
            t[...] += g

    blk = lambda gb, b, r: (b * nc + nc - 1 - r, gb)
    tab_shapes = [SDS(t.shape, F32) for t in tables] + [SDS(dsk.shape, F32)]
    return pl.pallas_call(
        body, grid=(GB, bl, nc),
        in_specs=[pl.BlockSpec((CH, LANES), blk)] + _s5_table_specs()
        + [pl.BlockSpec((None, 8, 512), lambda gb, b, r: ((gb * bl + b) * nc + nc - 1 - r, 0, 0)), pl.BlockSpec((CH, LANES), blk)],
        out_specs=[pl.BlockSpec((CH, LANES), blk)] + _s5_table_specs(),
        out_shape=[SDS((t_rows, S5W), F32)] + tab_shapes,
        scratch_shapes=[pltpu.VMEM((8, 512), F32)], name="s5_bwd", compiler_params=_cp(3))(u, *tables, dsk, xs, dy)


def s5_tables_fwd(params):
    shapes = [SDS((GB, LANES, 512), F32)] * 4 + [SDS((CH, S5N), F32)] * 6

    def body(*refs):
        for r, t in zip(refs[7:], s5_tables(*[p[...] for p in refs[:7]])):
            r[...] = t

    return pl.pallas_call(body, out_shape=shapes, name="s5_tables_fwd", compiler_params=_cp())(*params)


def s5_tables_bwd(params, dtables):
    def body(*refs):
        _, vjp = jax.vjp(s5_tables, *[p[...] for p in refs[:7]])
        for r, g in zip(refs[17:], vjp(tuple(t[...] for t in refs[7:17]))):
            r[...] = g

    return pl.pallas_call(body, out_shape=[SDS(p.shape, F32) for p in params], name="s5_tables_bwd",
                          compiler_params=_cp())(*params, *dtables)


def ada_fwd(c_all, w_loc, b_loc):
    def body(c_ref, w_ref, b_ref, o_ref):
        o_ref[...] = _dot(_silu(c_ref[...]), w_ref[...]) + b_ref[...]

    return pl.pallas_call(body, out_shape=SDS((c_all.shape[0], w_loc.shape[1]), F32), name="ada_fwd",
                          compiler_params=_cp())(c_all, w_loc, b_loc)


def ada_bwd(c_all, dmod_mine, dmod_all):
    def body(c_ref, dm_ref, da_ref, gw_ref, gb_ref):
        gw_ref[...] = _dot_tn(_silu(c_ref[...]), dm_ref[...])
        gb_ref[...] = jnp.sum(da_ref[...], axis=0, keepdims=True)

    return pl.pallas_call(body, out_shape=[SDS((D, dmod_mine.shape[1]), F32), SDS((1, dmod_all.shape[1]), F32)],
                          name="ada_bwd", compiler_params=_cp())(c_all, dmod_mine, dmod_all)


def loss_head(h, tgt, g, seq):
    t_rows = h.shape[0]
    tm = _pick(seq, (256, 128, 64))

    def body(h_ref, t_ref, g_ref, dh_ref, dg_ref, loss_ref):
        i = pl.program_id(0)
        y, vjp = jax.vjp(lambda hh, gg: hh * lax.rsqrt(jnp.mean(hh * hh, axis=-1, keepdims=True) + EPS) * gg,
                         h_ref[...], g_ref[...])
        e = y - t_ref[...]
        dh, dg = vjp(e * (1.0 / D))
        part = jnp.sum(jnp.sum(e * e, axis=1, keepdims=True), axis=0, keepdims=True) * (0.5 / D) + jnp.zeros((1, LANES), F32)
        dh_ref[...] = dh

        @pl.when(i == 0)
        def _():
            dg_ref[...] = dg
            loss_ref[...] = part

        @pl.when(i != 0)
        def _():
            dg_ref[...] += dg
            loss_ref[...] += part

    row = lambda i: (i, 0)
    const = lambda i: (0, 0)
    return pl.pallas_call(
        body, grid=(t_rows // tm,),
        in_specs=[pl.BlockSpec((tm, D), row), pl.BlockSpec((tm, D), row), pl.BlockSpec((1, D), const)],
        out_specs=[pl.BlockSpec((tm, D), row), pl.BlockSpec((1, D), const), pl.BlockSpec((1, LANES), const)],
        out_shape=[SDS((t_rows, D), F32), SDS((1, D), F32), SDS((1, LANES), F32)], name="loss_head",
        compiler_params=_cp(1))(h, tgt, g)


def adamw(name, parts, w, m, v):
    k_parts, rows, cols = parts.shape
    tr = _pick(rows, (256, 128, 64, 32, 16, 8))

    def body(p_ref, w_ref, m_ref, v_ref, g_ref, d_ref, mo_ref, vo_ref):
        g = p_ref[0].astype(F32)
        for k in range(1, k_parts):
            g = g + p_ref[k].astype(F32)
        m_new = ADAM_B1 * m_ref[...] + (1.0 - ADAM_B1) * g
        v_new = ADAM_B2 * v_ref[...] + (1.0 - ADAM_B2) * (g * g)
        m_hat = m_new / (1.0 - ADAM_B1 ** ADAM_STEP)
        v_hat = v_new / (1.0 - ADAM_B2 ** ADAM_STEP)
        g_ref[...] = g
        d_ref[...] = -ADAM_LR * (m_hat / (jnp.sqrt(v_hat) + ADAM_EPS) + ADAM_WD * w_ref[...])
        mo_ref[...] = m_new
        vo_ref[...] = v_new

    blk = pl.BlockSpec((tr, cols), lambda i: (i, 0))
    return pl.pallas_call(
        body, grid=(rows // tr,), in_specs=[pl.BlockSpec((k_parts, tr, cols), lambda i: (0, i, 0)), blk, blk, blk],
        out_specs=[blk] * 4, out_shape=[SDS((rows, cols), F32)] * 4, name=name, compiler_params=_cp(1))(parts, w, m, v)


def all_gather(name, x):
    rows, cols = x.shape

    def body(x_ref, out_ref, send_sems, recv_sems, local_sem):
        mx, my, mc = lax.axis_index("x"), lax.axis_index("y"), lax.axis_index("c")
        me, sibling = (mx, my, mc), (mx, my, 1 - mc)
        chips = [(1 - mx, my), (mx, 1 - my), (1 - mx, 1 - my)]

        def slot(px, py, pc):
            return out_ref.at[4 * px + 2 * py + pc]

        def copy(k, block, to, src=None):
            return pltpu.make_async_remote_copy(
                src_ref=slot(*block) if src is None else src, dst_ref=slot(*block), send_sem=send_sems.at[k],
                recv_sem=recv_sems.at[k], device_id=to, device_id_type=pl.DeviceIdType.MESH)

        mine = pltpu.make_async_copy(x_ref, slot(*me), local_sem)
        mine.start()
        first = [copy(0, me, sibling, src=x_ref)]
        first += [copy(1 + j, me, (*chip, mc), src=x_ref) for j, chip in enumerate(chips)]
        for cp in first:
            cp.start()
        passed = [copy(4 + j, (*chip, mc), sibling) for j, chip in enumerate(chips)]
        for j, chip in enumerate(chips):
            copy(1 + j, (*chip, mc), me).wait_recv()
            passed[j].start()
        copy(0, sibling, me).wait_recv()
        for j, chip in enumerate(chips):
            copy(4 + j, (*chip, 1 - mc), me).wait_recv()
        for cp in first + passed:
            cp.wait_send()
        mine.wait()

    return pl.pallas_call(
        body, out_shape=SDS((NDEV, rows, cols), x.dtype), in_specs=[pl.BlockSpec(memory_space=pl.ANY)],
        out_specs=pl.BlockSpec(memory_space=pl.ANY),
        scratch_shapes=[pltpu.SemaphoreType.DMA((7,)), pltpu.SemaphoreType.DMA((7,)), pltpu.SemaphoreType.DMA],
        name=name)(x)


def all_to_all(name, x):
    def body(x_ref, out_ref, send_sems, recv_sems, local_sem):
        mx, my, mc = lax.axis_index("x"), lax.axis_index("y"), lax.axis_index("c")
        me = 4 * mx + 2 * my + mc
        mine = pltpu.make_async_copy(x_ref.at[me], out_ref.at[me], local_sem)
        mine.start()
        copies = []
        for k in range(1, NDEV):
            px, py, pc = mx ^ (k >> 2), my ^ ((k >> 1) & 1), mc ^ (k & 1)
            copies.append(pltpu.make_async_remote_copy(
                src_ref=x_ref.at[4 * px + 2 * py + pc], dst_ref=out_ref.at[me], send_sem=send_sems.at[k - 1],
                recv_sem=recv_sems.at[k - 1], device_id=(px, py, pc), device_id_type=pl.DeviceIdType.MESH))
        for cp in copies:
            cp.start()
        for k in range(1, NDEV):
            px, py, pc = mx ^ (k >> 2), my ^ ((k >> 1) & 1), mc ^ (k & 1)
            pltpu.make_async_remote_copy(
                src_ref=x_ref.at[me], dst_ref=out_ref.at[4 * px + 2 * py + pc], send_sem=send_sems.at[k - 1],
                recv_sem=recv_sems.at[k - 1], device_id=(px, py, pc), device_id_type=pl.DeviceIdType.MESH).wait_recv()
        for cp in copies:
            cp.wait_send()
        mine.wait()

    return pl.pallas_call(
        body, out_shape=SDS(x.shape, x.dtype), in_specs=[pl.BlockSpec(memory_space=pl.ANY)],
        out_specs=pl.BlockSpec(memory_space=pl.ANY),
        scratch_shapes=[pltpu.SemaphoreType.DMA((7,)), pltpu.SemaphoreType.DMA((7,)), pltpu.SemaphoreType.DMA],
        name=name)(x)


def _pack(arrs, dtype, row_mult=8):
    segs = []
    for a in arrs:
        flat = a.reshape(-1).astype(dtype)
        segs.append(jnp.pad(flat, (0, (-flat.shape[0]) % ROW)))
    flat = jnp.concatenate(segs)
    flat = jnp.pad(flat, (0, (-flat.shape[0]) % (ROW * row_mult)))
    return flat.reshape(-1, ROW)


def _unpack(buf, shapes):
    flat = buf.reshape(-1)
    out, off = [], 0
    for s in shapes:
        n = math.prod(s)
        out.append(flat[off:off + n].reshape(s))
        off += n + (-n) % ROW
    return out


def _pack8(arrs, dtype):
    segs = []
    for a in arrs:
        flat = a.reshape(NDEV, -1).astype(dtype)
        segs.append(jnp.pad(flat, ((0, 0), (0, (-flat.shape[1]) % ROW))))
    flat = jnp.concatenate(segs, axis=1)
    flat = jnp.pad(flat, ((0, 0), (0, (-flat.shape[1]) % (ROW * 16))))
    return flat.reshape(NDEV, -1, ROW)


def _unpack8(buf, shapes):
    flat = buf.reshape(NDEV, -1)
    out, off = [], 0
    for s in shapes:
        n = math.prod(s)
        out.append(flat[:, off:off + n].reshape((NDEV,) + tuple(s)))
        off += n + (-n) % ROW
    return out


def _full(blocks, name):
    if name in COL_SHARDED:
        return blocks.transpose(1, 0, 2).reshape(blocks.shape[1], -1)
    return blocks.reshape(-1, blocks.shape[2])


def _shards(full, name):
    if name in COL_SHARDED:
        return full.reshape(full.shape[0], NDEV, -1).transpose(1, 0, 2)
    return full.reshape(NDEV, -1, full.shape[1])


def _pad_cols(a, n):
    return jnp.pad(a, ((0, 0), (0, n - a.shape[1])))


def kernel(x, c, w_ada, b_ada, g_ffn1, w1_ffn1, w3_ffn1, w2_ffn1, g_mix, w_in, conv_qkv, a_log, dt_bias, g_onorm, lam_re, lam_im, log_step, b_re, b_im, c_re, c_im, d_skip, w_glu, b_glu, w_proj_a, w_proj_b, w_out, g_ffn2, w1_ffn2, w3_ffn2, w2_ffn2, g_final, loss_target, m_w_ada, m_b_ada, m_g_ffn1, m_w1_ffn1, m_w3_ffn1, m_w2_ffn1, m_g_mix, m_w_in, m_conv_qkv, m_a_log, m_dt_bias, m_g_onorm, m_lam_re, m_lam_im, m_log_step, m_b_re, m_b_im, m_c_re, m_c_im, m_d_skip, m_w_glu, m_b_glu, m_w_proj_a, m_w_proj_b, m_w_out, m_g_ffn2, m_w1_ffn2, m_w3_ffn2, m_w2_ffn2, m_g_final, v_w_ada, v_b_ada, v_g_ffn1, v_w1_ffn1, v_w3_ffn1, v_w2_ffn1, v_g_mix, v_w_in, v_conv_qkv, v_a_log, v_dt_bias, v_g_onorm, v_lam_re, v_lam_im, v_log_step, v_b_re, v_b_im, v_c_re, v_c_im, v_d_skip, v_w_glu, v_b_glu, v_w_proj_a, v_w_proj_b, v_w_out, v_g_ffn2, v_w1_ffn2, v_w3_ffn2, v_w2_ffn2, v_g_final):
    a = dict(locals())
    bl, seq, _ = x.shape
    t_rows = bl * seq
    nc = seq // CH
    me = 4 * lax.axis_index("x") + 2 * lax.axis_index("y") + lax.axis_index("c")
    tm_ew = _pick(seq, (256, 128, 64))

    sm = all_gather("gather_small", _pack([c, conv_qkv[0]], F32))
    c_loc, conv_loc = _unpack8(sm, [c.shape, conv_qkv.shape[1:]])
    c_all = c_loc.reshape(NDEV * bl, D)
    conv_full = conv_loc.transpose(1, 0, 2).reshape(CONVW, 3 * DNW)
    rs_shapes = [a[n].shape[1:] for n in RS_WEIGHTS]
    wg = all_gather("gather_weights", _pack([a[n][0] for n in RS_WEIGHTS], BF16, 16))
    wfull = {n: _full(blk, n) for n, blk in zip(RS_WEIGHTS, _unpack8(wg, rs_shapes))}
    win = wfull['w_in']
    w_dn, w_small = win[:, :4 * DNW], _pad_cols(win[:, 4 * DNW:4 * DNW + 2 * NH], LANES)
    w_s5, w_gate = win[:, 4 * DNW + 2 * NH:4 * DNW + 2 * NH + S5W], win[:, 4 * DNW + 2 * NH + S5W:]

    n_ada = w_ada.shape[2]
    mod_part = ada_fwd(c_all, w_ada[0], lax.dynamic_slice(b_ada, (0, me * n_ada), (1, n_ada)))
    mod_all = all_gather("gather_mod", mod_part).transpose(1, 0, 2).reshape(NDEV * bl, 9 * D)
    mod = lax.dynamic_slice(mod_all, (me * bl, 0), (bl, 9 * D)).reshape(bl, 9, D)
    mods = [mod[:, k:k + 1, :] for k in range(9)]

    h0 = x.reshape(t_rows, D)
    h1, f1, u1 = ffn_fwd("ffn1_fwd", h0, mod[:, 0:3, :], g_ffn1, wfull['w1_ffn1'], wfull['w3_ffn1'], wfull['w2_ffn1'], seq)
    (u2,) = ew_call("mix_norm", fn_normmod, [h1], [mods[3], mods[4]], [g_mix], [(D, BF16)], tm_ew, seq)
    p_dn = mm("proj_dn", [(u2, w_dn)], False, F32)
    p_small = mm("proj_small", [(u2, w_small)], False, F32)
    p_s5 = mm("proj_s5", [(u2, w_s5)], False, F32)
    p_gate = mm("proj_gate", [(u2, w_gate)], False, F32)

    conv8 = jnp.pad(conv_full, ((0, 8 - CONVW), (0, 0)))
    alp = jnp.pad(a_log, ((0, 0), (NH, LANES - 2 * NH)))
    dtp = jnp.pad(dt_bias, ((0, 0), (NH, LANES - 2 * NH)))
    o_pre, sprev = deltanet_fwd(p_dn, p_small, conv8, alp, dtp, bl, nc)
    z_raw = p_dn[:, 3 * DNW:]
    (oa,) = ew_call("dn_onorm", fn_onorm, [o_pre, z_raw], [], [g_onorm], [(DNW, BF16)], tm_ew, seq)
    ya = mm("proj_a", [(oa, wfull['w_proj_a'])], False, F32)

    s5_params = [lam_re.reshape(1, S5N), lam_im.reshape(1, S5N), log_step,
                 b_re[0].transpose(2, 0, 1).reshape(S5C, S5N), b_im[0].transpose(2, 0, 1).reshape(S5C, S5N),
                 c_re[0].transpose(1, 0, 2).reshape(S5C, S5N), c_im[0].transpose(1, 0, 2).reshape(S5C, S5N)]
    tables = s5_tables_fwd(s5_params)
    y_s5, xs = s5_fwd(p_s5, tables, d_skip, bl, nc)
    (ob,) = ew_call("s5_glu", fn_glu, [y_s5], [], [wfull['w_glu'], b_glu], [(S5W, BF16)], tm_ew, seq)
    yb = mm("proj_b", [(ob, wfull['w_proj_b'])], False, F32)

    (merged,) = ew_call("merge", fn_merge, [p_gate, ya, yb], [], [], [(D, BF16)], tm_ew, seq)
    mo = mm("proj_out", [(merged, wfull['w_out'])], False, F32)
    (gmo,) = ew_call("mix_resid", fn_resid, [mo], [mods[5]], [], [(D, F32)], tm_ew, seq)
    h2 = ew_call("mix_add", lambda p, q: (p + q,), [h1, gmo], [], [], [(D, F32)], tm_ew, seq)[0]
    h3, f3, u3 = ffn_fwd("ffn2_fwd", h2, mod[:, 6:9, :], g_ffn2, wfull['w1_ffn2'], wfull['w3_ffn2'], wfull['w2_ffn2'], seq)

    dh3, dg_final, loss_part = loss_head(h3, loss_target.reshape(t_rows, D), g_final.reshape(1, D), seq)
    loss = lax.psum(loss_part[0, 0], ("x", "y", "c"))

    gw = {}
    dh2, a3, d1_3, d3_3, df3, dmod_c, dg_ffn2 = ffn_bwd("ffn2_bwd", dh3, h2, f3, u3, mod[:, 6:9, :], g_ffn2, wfull['w1_ffn2'],
                                                   wfull['w3_ffn2'], wfull['w2_ffn2'], seq)
    gw['w1_ffn2'] = mm_tn("gw1_ffn2", u3, d1_3)
    gw['w3_ffn2'] = mm_tn("gw3_ffn2", u3, d3_3)
    gw['w2_ffn2'] = mm_tn("gw2_ffn2", a3, df3)

    (dmo,), (dgt2,), _ = ew_vjp_call("mix_resid_bwd", fn_resid, [mo], [mods[5]], [], [dh2], [(0, BF16)], tm_ew, seq)
    gw['w_out'] = mm_tn("gw_out", merged, dmo)
    d_merged = mm("d_merged", [(dmo, wfull['w_out'])], True, F32)
    (d_gate, d_ya, d_yb), _, _ = ew_vjp_call("merge_bwd", fn_merge, [p_gate, ya, yb], [], [], [d_merged],
                                             [(0, BF16), (1, BF16), (2, BF16)], tm_ew, seq)
    gw['w_proj_a'] = mm_tn("gw_proj_a", oa, d_ya)
    gw['w_proj_b'] = mm_tn("gw_proj_b", ob, d_yb)
    d_oa = mm("d_oa", [(d_ya, wfull['w_proj_a'])], True, F32)
    d_ob = mm("d_ob", [(d_yb, wfull['w_proj_b'])], True, F32)

    (d_opre, d_z), _, (dg_onorm,) = ew_vjp_call("dn_onorm_bwd", fn_onorm, [o_pre, z_raw], [], [g_onorm], [d_oa],
                                                [(0, F32), (1, F32)], tm_ew, seq)
    d_pdn, d_psmall, d_conv8, d_alp, d_dtp = deltanet_bwd(p_dn, p_small, conv8, alp, dtp, sprev, d_opre, d_z, bl, nc)

    (d_ys5,), _, (g_wglu, dg_bglu) = ew_vjp_call("s5_glu_bwd", fn_glu, [y_s5], [], [wfull['w_glu'], b_glu], [d_ob],
                                                 [(0, F32)], tm_ew, seq)
    gw['w_glu'] = g_wglu
    s5_out = s5_bwd(p_s5, tables, d_skip, xs, d_ys5, bl, nc)
    d_ps5, d_tables, dg_dskip = s5_out[0], s5_out[1:11], s5_out[11]
    d_s5p = s5_tables_bwd(s5_params, d_tables)

    d_pdn_b, d_psm_b, d_ps5_b = (t.astype(BF16) for t in (d_pdn, d_psmall, d_ps5))
    gw['w_in'] = jnp.concatenate([mm_tn("gw_dn", u2, d_pdn_b), mm_tn("gw_small", u2, d_psm_b)[:, :2 * NH],
                                  mm_tn("gw_s5", u2, d_ps5_b), mm_tn("gw_gate", u2, d_gate)], axis=1)
    du2 = mm("d_u2", [(d_pdn_b, w_dn), (d_psm_b, w_small), (d_ps5_b, w_s5), (d_gate, w_gate)], True, F32)
    (dh1_n,), (dsh2, dsc2), (dg_mix,) = ew_vjp_call("mix_norm_bwd", fn_normmod, [h1], [mods[3], mods[4]], [g_mix], [du2],
                                                    [(0, F32)], tm_ew, seq)
    dh1 = ew_call("mix_add_bwd", lambda p, q: (p + q,), [dh2, dh1_n], [], [], [(D, F32)], tm_ew, seq)[0]

    dh0, a1, d1_1, d3_1, df1, dmod_a, dg_ffn1 = ffn_bwd("ffn1_bwd", dh1, h0, f1, u1, mod[:, 0:3, :], g_ffn1, wfull['w1_ffn1'],
                                                   wfull['w3_ffn1'], wfull['w2_ffn1'], seq)
    gw['w1_ffn1'] = mm_tn("gw1_ffn1", u1, d1_1)
    gw['w3_ffn1'] = mm_tn("gw3_ffn1", u1, d3_1)
    gw['w2_ffn1'] = mm_tn("gw2_ffn1", a1, df1)

    rs_in = _pack8([_shards(gw[n], n) for n in RS_WEIGHTS], BF16)
    rs_out = all_to_all("scatter_grads", rs_in)
    n_rows = rs_out.shape[1]
    packed = [_pack([a[p + n][0] for n in RS_WEIGHTS], F32, 16) for p in ("", "m_", "v_")]
    rs_res = adamw("adamw_sharded", rs_out, *packed)
    res = {}
    for kind, buf in zip(("grad", "delta", "new_m", "new_v"), rs_res):
        for n, t in zip(RS_WEIGHTS, _unpack(buf, rs_shapes)):
            res[kind + "_" + n] = t[None]
    del n_rows

    dmod_mine = jnp.concatenate([dmod_a, dsh2, dsc2, dgt2, dmod_c], axis=1).reshape(bl, 9 * D)
    small_grads = {
        'g_ffn1': dg_ffn1, 'g_mix': dg_mix, 'a_log': d_alp[:, NH:2 * NH], 'dt_bias': d_dtp[:, NH:2 * NH],
        'g_onorm': dg_onorm, 'lam_re': d_s5p[0].reshape(1, S5G, S5P), 'lam_im': d_s5p[1].reshape(1, S5G, S5P),
        'log_step': d_s5p[2],
        'b_re': d_s5p[3].reshape(S5C, S5G, S5P).transpose(1, 2, 0)[None],
        'b_im': d_s5p[4].reshape(S5C, S5G, S5P).transpose(1, 2, 0)[None],
        'c_re': d_s5p[5].reshape(S5C, S5G, S5P).transpose(1, 0, 2)[None],
        'c_im': d_s5p[6].reshape(S5C, S5G, S5P).transpose(1, 0, 2)[None],
        'd_skip': dg_dskip, 'b_glu': dg_bglu, 'g_ffn2': dg_ffn2, 'g_final': dg_final.reshape(D)}
    small_shapes = [a[n].shape for n in SMALL]
    sg = all_gather("gather_small_grads", _pack([dmod_mine, d_conv8[:CONVW]] + [small_grads[n] for n in SMALL], F32))
    pieces = _unpack8(sg, [dmod_mine.shape, (CONVW, 3 * DNW)] + small_shapes)
    dmod_all = pieces[0].reshape(NDEV * bl, 9 * D)
    g_wada, g_bada = ada_bwd(c_all, lax.dynamic_slice(dmod_all, (0, me * n_ada), (NDEV * bl, n_ada)), dmod_all)

    n_conv = conv_qkv.shape[2]
    conv_parts = lax.dynamic_slice(pieces[1], (0, 0, me * n_conv), (NDEV, CONVW, n_conv))
    conv_parts = jnp.pad(conv_parts.reshape(NDEV, 1, -1), ((0, 0), (0, 7), (0, 0)))
    pad8 = lambda t: jnp.pad(t.reshape(1, -1), ((0, 7), (0, 0)))
    conv_res = adamw("adamw_conv", conv_parts, pad8(conv_qkv), pad8(m_conv_qkv), pad8(v_conv_qkv))
    for kind, buf in zip(("grad", "delta", "new_m", "new_v"), conv_res):
        res[kind + "_conv_qkv"] = buf[0].reshape(conv_qkv.shape)

    small_parts = jnp.stack([_pack([p[k] for p in pieces[2:]], F32) for k in range(NDEV)])
    small_res = adamw("adamw_small", small_parts, *[_pack([a[p + n] for n in SMALL], F32) for p in ("", "m_", "v_")])
    for kind, buf in zip(("grad", "delta", "new_m", "new_v"), small_res):
        for n, t in zip(SMALL, _unpack(buf, small_shapes)):
            res[kind + "_" + n] = t

    for n, g in (("w_ada", g_wada), ("b_ada", g_bada)):
        shp = a[n].shape
        r2 = lambda t: t.reshape(-1, shp[-1]) if n == "w_ada" else pad8(t)
        out = adamw("adamw_" + n, r2(g)[None], r2(a[n]), r2(a["m_" + n]), r2(a["v_" + n]))
        for kind, buf in zip(("grad", "delta", "new_m", "new_v"), out):
            res[kind + "_" + n] = (buf if n == "w_ada" else buf[0:1]).reshape(shp)

    outs = [loss, dh0.reshape(x.shape)]
    for kind in ("grad", "delta", "new_m", "new_v"):
        outs += [res[kind + "_" + n] for n in WEIGHTS]
    return tuple(outs)
```

```python
import functools
import math

import jax
import jax.numpy as jnp
from jax import lax
from jax.experimental import pallas as pl
from jax.experimental.pallas import tpu as pltpu

F32 = jnp.float32
BF16 = jnp.bfloat16
HI = lax.Precision.HIGHEST
SDS = jax.ShapeDtypeStruct

D = 1024
FF = 2816
NH = 8
DH = 64
DNW = NH * DH
CONVW = 4
CH = 64
DN_ROWS = 2
S5W = 512
S5G = 32
S5P = 64
S5C = 16
S5N = S5G * S5P
GB = 4
NDEV = 8
EPS = 1e-6
LANES = 128
ROW = 1024
VMEM_LIMIT = 56 * 1024 * 1024

ADAM_LR, ADAM_B1, ADAM_B2, ADAM_EPS, ADAM_WD, ADAM_STEP = 0.001, 0.9, 0.999, 1e-08, 0.01, 10

WEIGHTS = ['w_ada', 'b_ada', 'g_ffn1', 'w1_ffn1', 'w3_ffn1', 'w2_ffn1', 'g_mix', 'w_in', 'conv_qkv', 'a_log',
           'dt_bias', 'g_onorm', 'lam_re', 'lam_im', 'log_step', 'b_re', 'b_im', 'c_re', 'c_im', 'd_skip', 'w_glu',
           'b_glu', 'w_proj_a', 'w_proj_b', 'w_out', 'g_ffn2', 'w1_ffn2', 'w3_ffn2', 'w2_ffn2', 'g_final']
RS_WEIGHTS = ['w1_ffn1', 'w3_ffn1', 'w2_ffn1', 'w_in', 'w_glu', 'w_proj_a', 'w_proj_b', 'w_out', 'w1_ffn2', 'w3_ffn2',
              'w2_ffn2']
COL_SHARDED = {'w1_ffn1', 'w3_ffn1', 'w_in', 'w_proj_a', 'w_proj_b', 'w1_ffn2', 'w3_ffn2'}
SMALL = ['g_ffn1', 'g_mix', 'a_log', 'dt_bias', 'g_onorm', 'lam_re', 'lam_im', 'log_step', 'b_re', 'b_im', 'c_re',
         'c_im', 'd_skip', 'b_glu', 'g_ffn2', 'g_final']


def _cp(n_grid=0):
    if n_grid:
        return pltpu.CompilerParams(vmem_limit_bytes=VMEM_LIMIT, dimension_semantics=("arbitrary",) * n_grid)
    return pltpu.CompilerParams(vmem_limit_bytes=VMEM_LIMIT)


def _dot(a, b):
    return jnp.dot(a.astype(BF16), b.astype(BF16), preferred_element_type=F32)


def _dot_nt(a, b):
    return lax.dot_general(a.astype(BF16), b.astype(BF16), (((1,), (1,)), ((), ())), preferred_element_type=F32)


def _dot_tn(a, b):
    return lax.dot_general(a.astype(BF16), b.astype(BF16), (((0,), (0,)), ((), ())), preferred_element_type=F32)


def _dot_hi(a, b):
    return jnp.dot(a, b, precision=HI, preferred_element_type=F32)


@jax.custom_vjp
def bdot(a, b):
    return _dot(a, b)


bdot.defvjp(lambda a, b: (_dot(a, b), (a, b)),
            lambda r, g: (_dot_nt(g, r[1]).astype(r[0].dtype), _dot_tn(r[0], g).astype(r[1].dtype)))


@jax.custom_vjp
def bdot_nt(a, b):
    return _dot_nt(a, b)


bdot_nt.defvjp(lambda a, b: (_dot_nt(a, b), (a, b)),
               lambda r, g: (_dot(g, r[1]).astype(r[0].dtype), _dot_tn(g, r[0]).astype(r[1].dtype)))


@jax.custom_vjp
def bdot_tn(a, b):
    return _dot_tn(a, b)


bdot_tn.defvjp(lambda a, b: (_dot_tn(a, b), (a, b)),
               lambda r, g: (_dot_nt(r[1], g).astype(r[0].dtype), _dot(r[0], g).astype(r[1].dtype)))


def _silu(x):
    return x * jax.nn.sigmoid(x)


def _iota2(shape, axis):
    return lax.broadcasted_iota(jnp.int32, shape, axis)


def normmod(h, g, sc, sh):
    y = h * lax.rsqrt(jnp.mean(h * h, axis=-1, keepdims=True) + EPS) * g
    return y * (1.0 + sc) + sh


def fn_normmod(h, sh, sc, g):
    return (normmod(h, g, sc, sh),)


def fn_resid(mo, gt):
    return (gt * mo,)


def fn_merge(gate, ya, yb):
    return (jax.nn.sigmoid(gate[:, :D]) * ya + jax.nn.sigmoid(gate[:, D:]) * yb,)


def fn_glu(y, w, b):
    ge = jax.nn.gelu(y)
    return (ge * jax.nn.sigmoid(bdot(ge, w) + b),)


def fn_onorm(o, z, g_on):
    r = _iota2((DH, DNW), 0)
    c = _iota2((DH, DNW), 1)
    expand = (c % DH == r).astype(F32)
    r2 = _iota2((DNW, DNW), 0)
    c2 = _iota2((DNW, DNW), 1)
    avg = (r2 // DH == c2 // DH).astype(F32) * (1.0 / DH)
    ms = _dot_hi(o * o, avg)
    return (o * lax.rsqrt(ms + EPS) * _dot_hi(g_on, expand) * _silu(z),)


def gate_fn(small, alp, dtp):
    beta = jax.nn.sigmoid(small)
    la = -jnp.exp(alp) * jax.nn.softplus(small + dtp)
    tri = (_iota2((CH, CH), 0) >= _iota2((CH, CH), 1)).astype(F32)
    gc = _dot_hi(tri, la)
    gct = lax.dot_general(la, tri, (((0,), (1,)), ((), ())), precision=HI, preferred_element_type=F32)
    return beta, gc, gct


def _bdg(a, b, ca, cb, hi):
    if not hi:
        a, b = a.astype(BF16), b.astype(BF16)
    return lax.dot_general(a, b, (((ca,), (cb,)), ((0,), (0,))), precision=HI if hi else None,
                           preferred_element_type=F32)


def _batched_matmuls(hi):
    nn_ = lambda a, b: _bdg(a, b, 2, 1, hi)
    nt_ = lambda a, b: _bdg(a, b, 2, 2, hi)
    tn_ = lambda a, b: _bdg(a, b, 1, 1, hi)
    nn = jax.custom_vjp(nn_)
    nn.defvjp(lambda a, b: (nn_(a, b), (a, b)), lambda r, g: (nt_(g, r[1]), tn_(r[0], g)))
    nt = jax.custom_vjp(nt_)
    nt.defvjp(lambda a, b: (nt_(a, b), (a, b)), lambda r, g: (nn_(g, r[1]), tn_(g, r[0])))
    tn = jax.custom_vjp(tn_)
    tn.defvjp(lambda a, b: (tn_(a, b), (a, b)), lambda r, g: (nt_(r[1], g), nn_(r[0], g)))
    return nn, nt, tn


bnn, bnt, btn = _batched_matmuls(False)
hnn, hnt, htn = _batched_matmuls(True)


def _unit_lower_inverse(a):
    r = _iota2((1, CH, CH), 1)
    c = _iota2((1, CH, CH), 2)
    eye = (r == c).astype(F32)
    d = jnp.where(r // 8 == c // 8, a, 0.0)
    inv = eye - d
    p = d
    for _ in range(2):
        p = hnn(p, p)
        inv = inv + hnn(inv, p)
    for blk in (16, 32, 64):
        off = jnp.where((r // blk == c // blk) & (r // (blk // 2) != c // (blk // 2)), a, 0.0)
        inv = inv - hnn(hnn(inv, off), inv)
    return inv


@jax.custom_vjp
def _inverse_given(a, t):
    return t


_inverse_given.defvjp(lambda a, t: (t, t), lambda t, g: (-hnt(htn(t, g), t), jnp.zeros_like(t)))


def _conv_act(x, w):
    c = x[:, 5:69] * w[:, 0:1] + x[:, 6:70] * w[:, 1:2] + x[:, 7:71] * w[:, 2:3] + x[:, 8:72] * w[:, 3:4]
    return _silu(c)


def dn_chunk(xq, xk, xv, wq, wk, wv, b, g, gt, s_prev, t_saved=None):
    q = _conv_act(xq, wq)
    k = _conv_act(xk, wk)
    v = _conv_act(xv, wv)
    q = q * lax.rsqrt(jnp.sum(q * q, axis=-1, keepdims=True) + EPS) * (DH ** -0.5)
    k = k * lax.rsqrt(jnp.sum(k * k, axis=-1, keepdims=True) + EPS)
    r = _iota2((1, CH, CH), 1)
    c = _iota2((1, CH, CH), 2)
    causal = r >= c
    dec = jnp.where(causal, jnp.exp(jnp.where(causal, g - gt, 0.0)), 0.0)
    kb = k * b
    qk = bnt(jnp.concatenate([q, kb], axis=1), k)
    attn = qk[:, :CH] * dec
    a = jnp.where(r > c, qk[:, CH:] * dec, 0.0)
    tinv = _unit_lower_inverse(a) if t_saved is None else _inverse_given(a, t_saved)
    eg = jnp.exp(g)
    uw = hnn(tinv, jnp.concatenate([v * b, kb * eg], axis=2))
    g_last = g[:, CH - 1:CH]
    ws = bnn(jnp.concatenate([uw[..., DH:], q * eg], axis=1), s_prev)
    v_new = uw[..., :DH] - ws[:, :CH]
    o = ws[:, CH:] + bnn(attn, v_new)
    s_new = s_prev * jnp.exp(g_last) + btn(k * jnp.exp(g_last - g), v_new)
    return o, s_new, tinv


def s5_chunk(u, xp_re, xp_im, bb_re, bb_im, cc_re, cc_im, p0r, p0i, p1r, p1i, pir, pii, dsk):
    nb = u.shape[0]
    u2 = u.reshape(nb * CH, LANES)
    bu_re = bdot(u2, bb_re).reshape(nb, CH, 512)
    bu_im = bdot(u2, bb_im).reshape(nb, CH, 512)
    xt_re = pir * bu_re - pii * bu_im
    xt_im = pir * bu_im + pii * bu_re
    tri = jnp.broadcast_to((_iota2((1, CH, CH), 1) >= _iota2((1, CH, CH), 2)).astype(F32), (nb, CH, CH))
    cs_re = hnn(tri, xt_re)
    cs_im = hnn(tri, xt_im)
    x_re = p0r * cs_re - p0i * cs_im + p1r * xp_re - p1i * xp_im
    x_im = p0r * cs_im + p0i * cs_re + p1r * xp_im + p1i * xp_re
    y = bdot_nt(x_re.reshape(nb * CH, 512), cc_re) - bdot_nt(x_im.reshape(nb * CH, 512), cc_im) + dsk * u2
    return y.reshape(nb, CH, LANES), x_re[:, CH - 1:CH], x_im[:, CH - 1:CH]


def s5_tables(lam_re, lam_im, log_step, bre, bim, cre, cim):
    expand = (_iota2((S5G, S5N), 1) // S5P == _iota2((S5G, S5N), 0)).astype(F32)
    step = _dot_hi(jnp.exp(log_step), expand)
    lre = jnp.minimum(lam_re, -1e-4)
    lr = lre * step
    ang = lam_im * step
    mag = jnp.exp(lr)
    lb_re = mag * jnp.cos(ang)
    lb_im = mag * jnp.sin(ang)
    den = lre * lre + lam_im * lam_im
    coef_re = ((lb_re - 1.0) * lre + lb_im * lam_im) / den
    coef_im = (lb_im * lre - (lb_re - 1.0) * lam_im) / den
    bb_re = coef_re * bre - coef_im * bim
    bb_im = coef_re * bim + coef_im * bre
    j = _iota2((CH, 1), 0).astype(F32)
    e0 = jnp.exp(j * lr)
    e1 = jnp.exp((j + 1.0) * lr)
    ei = jnp.exp(-j * lr)
    mask = (_iota2((LANES, 512), 0) // S5C == _iota2((LANES, 512), 1) // S5P).astype(F32)

    def blocks(t):
        return jnp.concatenate([(jnp.tile(t[:, gb * 512:(gb + 1) * 512], (LANES // S5C, 1)) * mask)[None]
                                for gb in range(GB)], axis=0)

    return (blocks(bb_re), blocks(bb_im), blocks(cre), blocks(cim),
            e0 * jnp.cos(j * ang), e0 * jnp.sin(j * ang),
            e1 * jnp.cos((j + 1.0) * ang), e1 * jnp.sin((j + 1.0) * ang),
            ei * jnp.cos(j * ang), -ei * jnp.sin(j * ang))


def _row_specs(tiled, batch, bcast, tm, tpb):
    specs = [pl.BlockSpec((tm, a.shape[1]), lambda i: (i, 0)) for a in tiled]
    specs += [pl.BlockSpec((None,) + a.shape[1:], lambda i: (i // tpb, 0, 0)) for a in batch]
    specs += [pl.BlockSpec(a.shape, lambda i, nd=a.ndim: (0,) * nd) for a in bcast]
    return specs


def ew_call(name, fn, tiled, batch, bcast, outs, tm, seq):
    t_rows = tiled[0].shape[0]
    n_in = len(tiled) + len(batch) + len(bcast)

    def body(*refs):
        vals = [r[...].astype(F32) for r in refs[:n_in]]
        for r, o in zip(refs[n_in:], fn(*vals)):
            r[...] = o.astype(r.dtype)

    return pl.pallas_call(
        body, grid=(t_rows // tm,), in_specs=_row_specs(tiled, batch, bcast, tm, seq // tm),
        out_specs=[pl.BlockSpec((tm, w), lambda i: (i, 0)) for w, _ in outs],
        out_shape=[SDS((t_rows, w), dt) for w, dt in outs], name=name, compiler_params=_cp(1))(*tiled, *batch, *bcast)


def ew_vjp_call(name, fn, tiled, batch, bcast, cts, want, tm, seq):
    t_rows = tiled[0].shape[0]
    tpb = seq // tm
    n_t, n_b, n_c = len(tiled), len(batch), len(bcast)
    n_in = n_t + n_b + n_c

    def body(*refs):
        i = pl.program_id(0)
        vals = [r[...].astype(F32) for r in refs[:n_in]]
        ctv = tuple(r[...].astype(F32) for r in refs[n_in:n_in + len(cts)])
        outs = refs[n_in + len(cts):]
        _, vjp = jax.vjp(fn, *vals)
        grads = vjp(ctv)
        for r, (idx, _) in zip(outs[:len(want)], want):
            r[...] = grads[idx].astype(r.dtype)
        for k in range(n_b):
            r, g = outs[len(want) + k], grads[n_t + k]

            @pl.when(i % tpb == 0)
            def _(r=r, g=g):
                r[...] = g

            @pl.when(i % tpb != 0)
            def _(r=r, g=g):
                r[...] += g
        for k in range(n_c):
            r, g = outs[len(want) + n_b + k], grads[n_t + n_b + k]

            @pl.when(i == 0)
            def _(r=r, g=g):
                r[...] = g

            @pl.when(i != 0)
            def _(r=r, g=g):
                r[...] += g

    out_specs = [pl.BlockSpec((tm, tiled[idx].shape[1]), lambda i: (i, 0)) for idx, _ in want]
    out_specs += [pl.BlockSpec((None,) + a.shape[1:], lambda i: (i // tpb, 0, 0)) for a in batch]
    out_specs += [pl.BlockSpec(a.shape, lambda i, nd=a.ndim: (0,) * nd) for a in bcast]
    out_shape = [SDS(tiled[idx].shape, dt) for idx, dt in want]
    out_shape += [SDS(a.shape, F32) for a in batch] + [SDS(a.shape, F32) for a in bcast]
    res = pl.pallas_call(
        body, grid=(t_rows // tm,),
        in_specs=_row_specs(tiled, batch, bcast, tm, tpb) + [pl.BlockSpec((tm, a.shape[1]), lambda i: (i, 0)) for a in cts],
        out_specs=out_specs, out_shape=out_shape, name=name, compiler_params=_cp(1))(*tiled, *batch, *bcast, *cts)
    return res[:len(want)], res[len(want):len(want) + n_b], res[len(want) + n_b:]


def _pick(n, cands):
    for c in cands:
        if n % c == 0:
            return c
    return n


def mm(name, pairs, nt, out_dtype):
    m = pairs[0][0].shape[0]
    n = pairs[0][1].shape[0 if nt else 1]
    tm = _pick(m, (512, 256, 128))
    tn = _pick(n, (512, 256, 128))
    np_ = len(pairs)

    def body(*refs):
        acc = None
        for p in range(np_):
            a, b = refs[2 * p][...], refs[2 * p + 1][...]
            t = _dot_nt(a, b) if nt else _dot(a, b)
            acc = t if acc is None else acc + t
        refs[2 * np_][...] = acc.astype(out_dtype)

    in_specs, ops = [], []
    for a, b in pairs:
        k = a.shape[1]
        in_specs.append(pl.BlockSpec((tm, k), lambda i, j: (i, 0)))
        in_specs.append(pl.BlockSpec((tn, k), lambda i, j: (j, 0)) if nt else pl.BlockSpec((k, tn), lambda i, j: (0, j)))
        ops += [a, b]
    return pl.pallas_call(
        body, grid=(m // tm, n // tn), in_specs=in_specs, out_specs=pl.BlockSpec((tm, tn), lambda i, j: (i, j)),
        out_shape=SDS((m, n), out_dtype), name=name, compiler_params=_cp(2))(*ops)


def mm_tn(name, a, b):
    t_rows, m = a.shape
    n = b.shape[1]
    tm = m if m <= 1024 else _pick(m, (1024, 512, 256))
    tn = _pick(n, (512, 256, 128))
    tk = _pick(t_rows, (512, 256, 128, 64))

    def body(a_ref, b_ref, o_ref):
        @pl.when(pl.program_id(2) == 0)
        def _():
            o_ref[...] = jnp.zeros_like(o_ref)

        o_ref[...] += _dot_tn(a_ref[...], b_ref[...])

    return pl.pallas_call(
        body, grid=(m // tm, n // tn, t_rows // tk),
        in_specs=[pl.BlockSpec((tk, tm), lambda i, j, k: (k, i)), pl.BlockSpec((tk, tn), lambda i, j, k: (k, j))],
        out_specs=pl.BlockSpec((tm, tn), lambda i, j, k: (i, j)), out_shape=SDS((m, n), F32), name=name,
        compiler_params=_cp(3))(a, b)


def ffn_fwd(name, h, mod3, g, w1, w3, w2, seq):
    t_rows = h.shape[0]
    tm = _pick(seq, (512, 256, 128, 64))
    tf = 256
    tpb = seq // tm
    nf = FF // tf

    def body(h_ref, mod_ref, g_ref, w1_ref, w3_ref, w2_ref, ho_ref, f_ref, u_ref, acc):
        j = pl.program_id(1)

        @pl.when(j == 0)
        def _():
            u_ref[...] = normmod(h_ref[...], g_ref[...], mod_ref[1:2, :], mod_ref[0:1, :]).astype(BF16)
            acc[...] = jnp.zeros_like(acc)

        u = u_ref[...]
        a = _silu(_dot(u, w1_ref[...])) * _dot(u, w3_ref[...])
        acc[...] += _dot(a, w2_ref[...])

        @pl.when(j == nf - 1)
        def _():
            f_ref[...] = acc[...]
            ho_ref[...] = h_ref[...] + 0.5 * mod_ref[2:3, :] * acc[...]

    row = lambda i, j: (i, 0)
    return pl.pallas_call(
        body, grid=(t_rows // tm, nf),
        in_specs=[pl.BlockSpec((tm, D), row), pl.BlockSpec((None, 3, D), lambda i, j: (i // tpb, 0, 0)),
                  pl.BlockSpec((1, D), lambda i, j: (0, 0)), pl.BlockSpec((D, tf), lambda i, j: (0, j)),
                  pl.BlockSpec((D, tf), lambda i, j: (0, j)), pl.BlockSpec((tf, D), lambda i, j: (j, 0))],
        out_specs=[pl.BlockSpec((tm, D), row), pl.BlockSpec((tm, D), row), pl.BlockSpec((tm, D), row)],
        out_shape=[SDS((t_rows, D), F32), SDS((t_rows, D), F32), SDS((t_rows, D), BF16)],
        scratch_shapes=[pltpu.VMEM((tm, D), F32)], name=name, compiler_params=_cp(2))(h, mod3, g, w1, w3, w2)


def ffn_bwd(name, dho, h, f_out, u, mod3, g, w1, w3, w2, seq):
    t_rows = h.shape[0]
    tm = _pick(seq, (512, 256, 128, 64))
    tf = 256
    tpb = seq // tm
    nf = FF // tf

    def body(dho_ref, h_ref, f_ref, u_ref, mod_ref, g_ref, w1_ref, w3_ref, w2_ref,
             dh_ref, a_ref, dh1_ref, dh3_ref, df_scr, dmod_ref, dg_ref, du_acc):
        i, j = pl.program_id(0), pl.program_id(1)

        @pl.when(j == 0)
        def _():
            df_scr[...] = (0.5 * mod_ref[2:3, :] * dho_ref[...]).astype(BF16)
            du_acc[...] = jnp.zeros_like(du_acc)

        uu = u_ref[...]
        h1 = _dot(uu, w1_ref[...])
        h3 = _dot(uu, w3_ref[...])
        sg = jax.nn.sigmoid(h1)
        s = h1 * sg
        da = _dot_nt(df_scr[...], w2_ref[...])
        dh3 = (da * s).astype(BF16)
        dh1 = (da * h3 * (sg * (1.0 + h1 * (1.0 - sg)))).astype(BF16)
        a_ref[...] = (s * h3).astype(BF16)
        dh1_ref[...] = dh1
        dh3_ref[...] = dh3
        du_acc[...] += _dot_nt(dh1, w1_ref[...]) + _dot_nt(dh3, w3_ref[...])

        @pl.when(j == nf - 1)
        def _():
            _, vjp = jax.vjp(normmod, h_ref[...], g_ref[...], mod_ref[1:2, :], mod_ref[0:1, :])
            dh_n, dg, dsc, dsh = vjp(du_acc[...])
            dh_ref[...] = dho_ref[...] + dh_n
            dgt = jnp.sum(0.5 * dho_ref[...] * f_ref[...], axis=0, keepdims=True)
            dmod = jnp.concatenate([dsh, dsc, dgt], axis=0)

            @pl.when(i % tpb == 0)
            def _():
                dmod_ref[...] = dmod

            @pl.when(i % tpb != 0)
            def _():
                dmod_ref[...] += dmod

            @pl.when(i == 0)
            def _():
                dg_ref[...] = dg

            @pl.when(i != 0)
            def _():
                dg_ref[...] += dg

    row = lambda i, j: (i, 0)
    col = lambda i, j: (i, j)
    return pl.pallas_call(
        body, grid=(t_rows // tm, nf),
        in_specs=[pl.BlockSpec((tm, D), row), pl.BlockSpec((tm, D), row), pl.BlockSpec((tm, D), row),
                  pl.BlockSpec((tm, D), row), pl.BlockSpec((None, 3, D), lambda i, j: (i // tpb, 0, 0)),
                  pl.BlockSpec((1, D), lambda i, j: (0, 0)), pl.BlockSpec((D, tf), lambda i, j: (0, j)),
                  pl.BlockSpec((D, tf), lambda i, j: (0, j)), pl.BlockSpec((tf, D), lambda i, j: (j, 0))],
        out_specs=[pl.BlockSpec((tm, D), row), pl.BlockSpec((tm, tf), col), pl.BlockSpec((tm, tf), col),
                   pl.BlockSpec((tm, tf), col), pl.BlockSpec((tm, D), row),
                   pl.BlockSpec((None, 3, D), lambda i, j: (i // tpb, 0, 0)), pl.BlockSpec((1, D), lambda i, j: (0, 0))],
        out_shape=[SDS((t_rows, D), F32), SDS((t_rows, FF), BF16), SDS((t_rows, FF), BF16), SDS((t_rows, FF), BF16),
                   SDS((t_rows, D), BF16), SDS(mod3.shape, F32), SDS((1, D), F32)],
        scratch_shapes=[pltpu.VMEM((tm, D), F32)], name=name,
        compiler_params=_cp(2))(dho, h, f_out, u, mod3, g, w1, w3, w2)


def _dn_cols(part, hd):
    return slice(part * DNW + hd * DH, part * DNW + (hd + 1) * DH)


def _dn_stacks(raw_ref, halo_ref, conv_ref, hm, nb):
    pairs = [(b, hd) for b in range(nb) for hd in range(NH)]
    xs = [jnp.stack([jnp.concatenate([halo_ref[b, :, _dn_cols(part, hd)] * hm, raw_ref[b, :, _dn_cols(part, hd)]], axis=0)
                     for b, hd in pairs]) for part in range(3)]
    ws = [jnp.stack([conv_ref[0:CONVW, _dn_cols(part, hd)] for b, hd in pairs]) for part in range(3)]
    return xs, ws


def _gate_stacks(gates, nb):
    pairs = [(b, hd) for b in range(nb) for hd in range(NH)]
    bs = jnp.stack([gates[b][0][:, hd:hd + 1] for b, hd in pairs])
    gs = jnp.stack([gates[b][1][:, NH + hd:NH + hd + 1] for b, hd in pairs])
    gts = jnp.stack([gates[b][2][NH + hd:NH + hd + 1, :] for b, hd in pairs])
    return bs, gs, gts


def deltanet_fwd(p_dn, p_small, conv8, alp, dtp, nb):
    bl, seq, _ = p_dn.shape
    nc = seq // CH
    ng = nb * NH

    def body(raw_ref, halo_ref, small_ref, conv_ref, alp_ref, dtp_ref, o_ref, sprev_ref, tinv_ref, s_scr):
        n = pl.program_id(1)

        @pl.when(n == 0)
        def _():
            s_scr[...] = jnp.zeros_like(s_scr)

        hm = (n > 0).astype(F32)
        gates = [gate_fn(small_ref[b], alp_ref[...], dtp_ref[...]) for b in range(nb)]
        xs, ws = _dn_stacks(raw_ref, halo_ref, conv_ref, hm, nb)
        s_prev = s_scr[...]
        o, s_new, tinv = dn_chunk(*xs, *ws, *_gate_stacks(gates, nb), s_prev)
        sprev_ref[...] = s_prev
        tinv_ref[...] = tinv
        s_scr[...] = s_new
        for b in range(nb):
            for hd in range(NH):
                o_ref[b, :, hd * DH:(hd + 1) * DH] = o[b * NH + hd]

    blk = lambda bb, n: (bb, n, 0)
    const = lambda bb, n: (0, 0)
    saved = pl.BlockSpec((None, ng, DH, DH), lambda bb, n: (bb * nc + n, 0, 0, 0))
    return pl.pallas_call(
        body, grid=(bl // nb, nc),
        in_specs=[pl.BlockSpec((nb, CH, 4 * DNW), blk),
                  pl.BlockSpec((nb, 8, 3 * DNW), lambda bb, n: (bb, jnp.maximum(n * (CH // 8) - 1, 0), 0)),
                  pl.BlockSpec((nb, CH, LANES), blk), pl.BlockSpec((8, 3 * DNW), const), pl.BlockSpec((1, LANES), const),
                  pl.BlockSpec((1, LANES), const)],
        out_specs=[pl.BlockSpec((nb, CH, DNW), blk), saved, saved],
        out_shape=[SDS((bl, seq, DNW), F32), SDS((bl // nb * nc, ng, DH, DH), F32), SDS((bl // nb * nc, ng, DH, DH), F32)],
        scratch_shapes=[pltpu.VMEM((ng, DH, DH), F32)], name="deltanet_fwd",
        compiler_params=_cp(2))(p_dn, p_dn, p_small, conv8, alp, dtp)


def deltanet_bwd(p_dn, p_small, conv8, alp, dtp, sprev, tinv, d_o, d_z, nb):
    bl, seq, _ = p_dn.shape
    nc = seq // CH
    ng = nb * NH

    def body(raw_ref, halo_ref, small_ref, conv_ref, alp_ref, dtp_ref, sprev_ref, tinv_ref, do_ref, dz_ref,
             draw_ref, dsmall_ref, dconv_ref, dalp_ref, ddtp_ref, ds_scr, dhalo_scr):
        bb, r = pl.program_id(0), pl.program_id(1)
        n = nc - 1 - r

        @pl.when((bb == 0) & (r == 0))
        def _():
            dconv_ref[...] = jnp.zeros_like(dconv_ref)
            dalp_ref[...] = jnp.zeros_like(dalp_ref)
            ddtp_ref[...] = jnp.zeros_like(ddtp_ref)

        @pl.when(r == 0)
        def _():
            ds_scr[...] = jnp.zeros_like(ds_scr)
            dhalo_scr[...] = jnp.zeros_like(dhalo_scr)

        hm = (n > 0).astype(F32)
        gates, gate_vjps = [], []
        for b in range(nb):
            out, gvjp = jax.vjp(gate_fn, small_ref[b], alp_ref[...], dtp_ref[...])
            gates.append(out)
            gate_vjps.append(gvjp)
        xs, ws = _dn_stacks(raw_ref, halo_ref, conv_ref, hm, nb)
        t_saved = tinv_ref[...]
        _, vjp = jax.vjp(lambda *args: dn_chunk(*args, t_saved)[:2], *xs, *ws, *_gate_stacks(gates, nb), sprev_ref[...])
        d_out = jnp.stack([do_ref[b, :, hd * DH:(hd + 1) * DH] for b in range(nb) for hd in range(NH)])
        grads = vjp((d_out, ds_scr[...]))
        ds_scr[...] = grads[9]
        lane = _iota2((CH, LANES), 1)
        rowi = _iota2((LANES, CH), 0)
        for b in range(nb):
            d_beta = jnp.zeros((CH, LANES), F32)
            d_gc = jnp.zeros((CH, LANES), F32)
            d_gct = jnp.zeros((LANES, CH), F32)
            for hd in range(NH):
                i = b * NH + hd
                for part in range(3):
                    cols = _dn_cols(part, hd)
                    dx = grads[part][i]
                    draw_ref[b, 0:CH - 8, cols] = dx[8:CH]
                    draw_ref[b, CH - 8:CH, cols] = dx[CH:CH + 8] + dhalo_scr[b, :, cols]
                    dhalo_scr[b, :, cols] = dx[0:8] * hm
                d_beta = d_beta + jnp.where(lane == hd, grads[6][i], 0.0)
                d_gc = d_gc + jnp.where(lane == NH + hd, grads[7][i], 0.0)
                d_gct = d_gct + jnp.where(rowi == NH + hd, grads[8][i], 0.0)
            d_small, d_alp, d_dtp = gate_vjps[b]((d_beta, d_gc, d_gct))
            dsmall_ref[b] = d_small
            dalp_ref[...] += d_alp
            ddtp_ref[...] += d_dtp
            draw_ref[b, :, 3 * DNW:4 * DNW] = dz_ref[b]
        for hd in range(NH):
            for part in range(3):
                dw = grads[3 + part][hd]
                for b in range(1, nb):
                    dw = dw + grads[3 + part][b * NH + hd]
                dconv_ref[0:CONVW, _dn_cols(part, hd)] += dw

    blk = lambda bb, r: (bb, nc - 1 - r, 0)
    const = lambda bb, r: (0, 0)
    saved = pl.BlockSpec((None, ng, DH, DH), lambda bb, r: (bb * nc + nc - 1 - r, 0, 0, 0))
    return pl.pallas_call(
        body, grid=(bl // nb, nc),
        in_specs=[pl.BlockSpec((nb, CH, 4 * DNW), blk),
                  pl.BlockSpec((nb, 8, 3 * DNW), lambda bb, r: (bb, jnp.maximum((nc - 1 - r) * (CH // 8) - 1, 0), 0)),
                  pl.BlockSpec((nb, CH, LANES), blk), pl.BlockSpec((8, 3 * DNW), const), pl.BlockSpec((1, LANES), const),
                  pl.BlockSpec((1, LANES), const), saved, saved,
                  pl.BlockSpec((nb, CH, DNW), blk), pl.BlockSpec((nb, CH, DNW), blk)],
        out_specs=[pl.BlockSpec((nb, CH, 4 * DNW), blk), pl.BlockSpec((nb, CH, LANES), blk),
                   pl.BlockSpec((8, 3 * DNW), const), pl.BlockSpec((1, LANES), const), pl.BlockSpec((1, LANES), const)],
        out_shape=[SDS((bl, seq, 4 * DNW), F32), SDS((bl, seq, LANES), F32), SDS((8, 3 * DNW), F32), SDS((1, LANES), F32),
                   SDS((1, LANES), F32)],
        scratch_shapes=[pltpu.VMEM((ng, DH, DH), F32), pltpu.VMEM((nb, 8, 3 * DNW), F32)], name="deltanet_bwd",
        compiler_params=_cp(2))(p_dn, p_dn, p_small, conv8, alp, dtp, sprev, tinv, d_o, d_z)


def _s5_table_specs():
    tab3 = pl.BlockSpec((None, LANES, 512), lambda gb, n: (gb, 0, 0))
    tab2 = pl.BlockSpec((CH, 512), lambda gb, n: (0, gb))
    return [tab3] * 4 + [tab2] * 6 + [pl.BlockSpec((1, LANES), lambda gb, n: (0, gb))]


def s5_fwd(u, tables, dsk):
    bl, seq, _ = u.shape
    nc = seq // CH

    def body(u_ref, *rest):
        tabs, (y_ref, xs_ref, xr_scr, xi_scr) = rest[:11], rest[11:]

        @pl.when(pl.program_id(1) == 0)
        def _():
            xr_scr[...] = jnp.zeros_like(xr_scr)
            xi_scr[...] = jnp.zeros_like(xi_scr)

        xp_re, xp_im = xr_scr[...], xi_scr[...]
        xs_ref[0:bl] = xp_re
        xs_ref[bl:2 * bl] = xp_im
        y, xn_re, xn_im = s5_chunk(u_ref[...], xp_re, xp_im, *[t[...] for t in tabs])
        y_ref[...] = y
        xr_scr[...] = xn_re
        xi_scr[...] = xn_im

    blk = lambda gb, n: (0, n, gb)
    return pl.pallas_call(
        body, grid=(GB, nc), in_specs=[pl.BlockSpec((bl, CH, LANES), blk)] + _s5_table_specs(),
        out_specs=[pl.BlockSpec((bl, CH, LANES), blk),
                   pl.BlockSpec((None, 2 * bl, 1, 512), lambda gb, n: (gb * nc + n, 0, 0, 0))],
        out_shape=[SDS((bl, seq, S5W), F32), SDS((GB * nc, 2 * bl, 1, 512), F32)],
        scratch_shapes=[pltpu.VMEM((bl, 1, 512), F32), pltpu.VMEM((bl, 1, 512), F32)], name="s5_fwd",
        compiler_params=_cp(2))(u, *tables, dsk)


def s5_bwd(u, tables, dsk, xs, dy):
    bl, seq, _ = u.shape
    nc = seq // CH

    def body(u_ref, *rest):
        tabs, xs_ref, dy_ref = rest[:11], rest[11], rest[12]
        du_ref, dtabs, dxr_scr, dxi_scr = rest[13], rest[14:25], rest[25], rest[26]
        r = pl.program_id(1)

        @pl.when(r == 0)
        def _():
            for t in dtabs:
                t[...] = jnp.zeros_like(t)
            dxr_scr[...] = jnp.zeros_like(dxr_scr)
            dxi_scr[...] = jnp.zeros_like(dxi_scr)

        _, vjp = jax.vjp(s5_chunk, u_ref[...], xs_ref[0:bl], xs_ref[bl:2 * bl], *[t[...] for t in tabs])
        grads = vjp((dy_ref[...], dxr_scr[...], dxi_scr[...]))
        du_ref[...] = grads[0]
        dxr_scr[...] = grads[1]
        dxi_scr[...] = grads[2]
        for t, g in zip(dtabs, grads[3:]):
            t[...] += g

    blk = lambda gb, r: (0, nc - 1 - r, gb)
    tab_shapes = [SDS(t.shape, F32) for t in tables] + [SDS(dsk.shape, F32)]
    return pl.pallas_call(
        body, grid=(GB, nc),
        in_specs=[pl.BlockSpec((bl, CH, LANES), blk)] + _s5_table_specs()
        + [pl.BlockSpec((None, 2 * bl, 1, 512), lambda gb, r: (gb * nc + nc - 1 - r, 0, 0, 0)), pl.BlockSpec((bl, CH, LANES), blk)],
        out_specs=[pl.BlockSpec((bl, CH, LANES), blk)] + _s5_table_specs(),
        out_shape=[SDS((bl, seq, S5W), F32)] + tab_shapes,
        scratch_shapes=[pltpu.VMEM((bl, 1, 512), F32), pltpu.VMEM((bl, 1, 512), F32)], name="s5_bwd",
        compiler_params=_cp(2))(u, *tables, dsk, xs, dy)


def s5_tables_fwd(params):
    shapes = [SDS((GB, LANES, 512), F32)] * 4 + [SDS((CH, S5N), F32)] * 6

    def body(*refs):
        for r, t in zip(refs[7:], s5_tables(*[p[...] for p in refs[:7]])):
            r[...] = t

    return pl.pallas_call(body, out_shape=shapes, name="s5_tables_fwd", compiler_params=_cp())(*params)


def s5_tables_bwd(params, dtables):
    def body(*refs):
        _, vjp = jax.vjp(s5_tables, *[p[...] for p in refs[:7]])
        for r, g in zip(refs[17:], vjp(tuple(t[...] for t in refs[7:17]))):
            r[...] = g

    return pl.pallas_call(body, out_shape=[SDS(p.shape, F32) for p in params], name="s5_tables_bwd",
                          compiler_params=_cp())(*params, *dtables)


def ada_fwd(c_all, w_loc, b_loc):
    def body(c_ref, w_ref, b_ref, o_ref):
        o_ref[...] = _dot(_silu(c_ref[...]), w_ref[...]) + b_ref[...]

    return pl.pallas_call(body, out_shape=SDS((c_all.shape[0], w_loc.shape[1]), F32), name="ada_fwd",
                          compiler_params=_cp())(c_all, w_loc, b_loc)


def ada_bwd(c_all, dmod_mine, dmod_all):
    def body(c_ref, dm_ref, da_ref, gw_ref, gb_ref):
        gw_ref[...] = _dot_tn(_silu(c_ref[...]), dm_ref[...])
        gb_ref[...] = jnp.sum(da_ref[...], axis=0, keepdims=True)

    return pl.pallas_call(body, out_shape=[SDS((D, dmod_mine.shape[1]), F32), SDS((1, dmod_all.shape[1]), F32)],
                          name="ada_bwd", compiler_params=_cp())(c_all, dmod_mine, dmod_all)


def loss_head(h, tgt, g, seq):
    t_rows = h.shape[0]
    tm = _pick(seq, (256, 128, 64))

    def body(h_ref, t_ref, g_ref, dh_ref, dg_ref, loss_ref):
        i = pl.program_id(0)
        y, vjp = jax.vjp(lambda hh, gg: hh * lax.rsqrt(jnp.mean(hh * hh, axis=-1, keepdims=True) + EPS) * gg,
                         h_ref[...], g_ref[...])
        e = y - t_ref[...]
        dh, dg = vjp(e * (1.0 / D))
        part = jnp.sum(jnp.sum(e * e, axis=1, keepdims=True), axis=0, keepdims=True) * (0.5 / D) + jnp.zeros((1, LANES), F32)
        dh_ref[...] = dh

        @pl.when(i == 0)
        def _():
            dg_ref[...] = dg
            loss_ref[...] = part

        @pl.when(i != 0)
        def _():
            dg_ref[...] += dg
            loss_ref[...] += part

    row = lambda i: (i, 0)
    const = lambda i: (0, 0)
    return pl.pallas_call(
        body, grid=(t_rows // tm,),
        in_specs=[pl.BlockSpec((tm, D), row), pl.BlockSpec((tm, D), row), pl.BlockSpec((1, D), const)],
        out_specs=[pl.BlockSpec((tm, D), row), pl.BlockSpec((1, D), const), pl.BlockSpec((1, LANES), const)],
        out_shape=[SDS((t_rows, D), F32), SDS((1, D), F32), SDS((1, LANES), F32)], name="loss_head",
        compiler_params=_cp(1))(h, tgt, g)


def adamw(name, parts, w, m, v):
    k_parts, rows, cols = parts.shape
    tr = _pick(rows, (256, 128, 64, 32, 16, 8))

    def body(p_ref, w_ref, m_ref, v_ref, g_ref, d_ref, mo_ref, vo_ref):
        g = p_ref[0].astype(F32)
        for k in range(1, k_parts):
            g = g + p_ref[k].astype(F32)
        m_new = ADAM_B1 * m_ref[...] + (1.0 - ADAM_B1) * g
        v_new = ADAM_B2 * v_ref[...] + (1.0 - ADAM_B2) * (g * g)
        m_hat = m_new / (1.0 - ADAM_B1 ** ADAM_STEP)
        v_hat = v_new / (1.0 - ADAM_B2 ** ADAM_STEP)
        g_ref[...] = g
        d_ref[...] = -ADAM_LR * (m_hat / (jnp.sqrt(v_hat) + ADAM_EPS) + ADAM_WD * w_ref[...])
        mo_ref[...] = m_new
        vo_ref[...] = v_new

    blk = pl.BlockSpec((tr, cols), lambda i: (i, 0))
    return pl.pallas_call(
        body, grid=(rows // tr,), in_specs=[pl.BlockSpec((k_parts, tr, cols), lambda i: (0, i, 0)), blk, blk, blk],
        out_specs=[blk] * 4, out_shape=[SDS((rows, cols), F32)] * 4, name=name, compiler_params=_cp(1))(parts, w, m, v)


def all_gather(name, x):
    rows, cols = x.shape

    def body(x_ref, out_ref, send_sems, recv_sems, local_sem):
        mx, my, mc = lax.axis_index("x"), lax.axis_index("y"), lax.axis_index("c")
        me, sibling = (mx, my, mc), (mx, my, 1 - mc)
        chips = [(1 - mx, my), (mx, 1 - my), (1 - mx, 1 - my)]

        def slot(px, py, pc):
            return out_ref.at[4 * px + 2 * py + pc]

        def copy(k, block, to, src=None):
            return pltpu.make_async_remote_copy(
                src_ref=slot(*block) if src is None else src, dst_ref=slot(*block), send_sem=send_sems.at[k],
                recv_sem=recv_sems.at[k], device_id=to, device_id_type=pl.DeviceIdType.MESH)

        mine = pltpu.make_async_copy(x_ref, slot(*me), local_sem)
        mine.start()
        first = [copy(0, me, sibling, src=x_ref)]
        first += [copy(1 + j, me, (*chip, mc), src=x_ref) for j, chip in enumerate(chips)]
        for cp in first:
            cp.start()
        passed = [copy(4 + j, (*chip, mc), sibling) for j, chip in enumerate(chips)]
        for j, chip in enumerate(chips):
            copy(1 + j, (*chip, mc), me).wait_recv()
            passed[j].start()
        copy(0, sibling, me).wait_recv()
        for j, chip in enumerate(chips):
            copy(4 + j, (*chip, 1 - mc), me).wait_recv()
        for cp in first + passed:
            cp.wait_send()
        mine.wait()

    return pl.pallas_call(
        body, out_shape=SDS((NDEV, rows, cols), x.dtype), in_specs=[pl.BlockSpec(memory_space=pl.ANY)],
        out_specs=pl.BlockSpec(memory_space=pl.ANY),
        scratch_shapes=[pltpu.SemaphoreType.DMA((7,)), pltpu.SemaphoreType.DMA((7,)), pltpu.SemaphoreType.DMA],
        name=name)(x)


def all_to_all(name, x):
    def body(x_ref, out_ref, send_sems, recv_sems, local_sem):
        mx, my, mc = lax.axis_index("x"), lax.axis_index("y"), lax.axis_index("c")
        me = 4 * mx + 2 * my + mc
        mine = pltpu.make_async_copy(x_ref.at[me], out_ref.at[me], local_sem)
        mine.start()
        copies = []
        for k in range(1, NDEV):
            px, py, pc = mx ^ (k >> 2), my ^ ((k >> 1) & 1), mc ^ (k & 1)
            copies.append(pltpu.make_async_remote_copy(
                src_ref=x_ref.at[4 * px + 2 * py + pc], dst_ref=out_ref.at[me], send_sem=send_sems.at[k - 1],
                recv_sem=recv_sems.at[k - 1], device_id=(px, py, pc), device_id_type=pl.DeviceIdType.MESH))
        for cp in copies:
            cp.start()
        for k in range(1, NDEV):
            px, py, pc = mx ^ (k >> 2), my ^ ((k >> 1) & 1), mc ^ (k & 1)
            pltpu.make_async_remote_copy(
                src_ref=x_ref.at[me], dst_ref=out_ref.at[4 * px + 2 * py + pc], send_sem=send_sems.at[k - 1],
                recv_sem=recv_sems.at[k - 1], device_id=(px, py, pc), device_id_type=pl.DeviceIdType.MESH).wait_recv()
        for cp in copies:
            cp.wait_send()
        mine.wait()

    return pl.pallas_call(
        body, out_shape=SDS(x.shape, x.dtype), in_specs=[pl.BlockSpec(memory_space=pl.ANY)],
        out_specs=pl.BlockSpec(memory_space=pl.ANY),
        scratch_shapes=[pltpu.SemaphoreType.DMA((7,)), pltpu.SemaphoreType.DMA((7,)), pltpu.SemaphoreType.DMA],
        name=name)(x)


def _pack(arrs, dtype, row_mult=8):
    segs = []
    for a in arrs:
        flat = a.reshape(-1).astype(dtype)
        segs.append(jnp.pad(flat, (0, (-flat.shape[0]) % ROW)))
    flat = jnp.concatenate(segs)
    flat = jnp.pad(flat, (0, (-flat.shape[0]) % (ROW * row_mult)))
    return flat.reshape(-1, ROW)


def _unpack(buf, shapes):
    flat = buf.reshape(-1)
    out, off = [], 0
    for s in shapes:
        n = math.prod(s)
        out.append(flat[off:off + n].reshape(s))
        off += n + (-n) % ROW
    return out


def _pack8(arrs, dtype):
    segs = []
    for a in arrs:
        flat = a.reshape(NDEV, -1).astype(dtype)
        segs.append(jnp.pad(flat, ((0, 0), (0, (-flat.shape[1]) % ROW))))
    flat = jnp.concatenate(segs, axis=1)
    flat = jnp.pad(flat, ((0, 0), (0, (-flat.shape[1]) % (ROW * 16))))
    return flat.reshape(NDEV, -1, ROW)


def _unpack8(buf, shapes):
    flat = buf.reshape(NDEV, -1)
    out, off = [], 0
    for s in shapes:
        n = math.prod(s)
        out.append(flat[:, off:off + n].reshape((NDEV,) + tuple(s)))
        off += n + (-n) % ROW
    return out


def _full(blocks, name):
    if name in COL_SHARDED:
        return blocks.transpose(1, 0, 2).reshape(blocks.shape[1], -1)
    return blocks.reshape(-1, blocks.shape[2])


def _shards(full, name):
    if name in COL_SHARDED:
        return full.reshape(full.shape[0], NDEV, -1).transpose(1, 0, 2)
    return full.reshape(NDEV, -1, full.shape[1])


def _pad_cols(a, n):
    return jnp.pad(a, ((0, 0), (0, n - a.shape[1])))


def kernel(x, c, w_ada, b_ada, g_ffn1, w1_ffn1, w3_ffn1, w2_ffn1, g_mix, w_in, conv_qkv, a_log, dt_bias, g_onorm, lam_re, lam_im, log_step, b_re, b_im, c_re, c_im, d_skip, w_glu, b_glu, w_proj_a, w_proj_b, w_out, g_ffn2, w1_ffn2, w3_ffn2, w2_ffn2, g_final, loss_target, m_w_ada, m_b_ada, m_g_ffn1, m_w1_ffn1, m_w3_ffn1, m_w2_ffn1, m_g_mix, m_w_in, m_conv_qkv, m_a_log, m_dt_bias, m_g_onorm, m_lam_re, m_lam_im, m_log_step, m_b_re, m_b_im, m_c_re, m_c_im, m_d_skip, m_w_glu, m_b_glu, m_w_proj_a, m_w_proj_b, m_w_out, m_g_ffn2, m_w1_ffn2, m_w3_ffn2, m_w2_ffn2, m_g_final, v_w_ada, v_b_ada, v_g_ffn1, v_w1_ffn1, v_w3_ffn1, v_w2_ffn1, v_g_mix, v_w_in, v_conv_qkv, v_a_log, v_dt_bias, v_g_onorm, v_lam_re, v_lam_im, v_log_step, v_b_re, v_b_im, v_c_re, v_c_im, v_d_skip, v_w_glu, v_b_glu, v_w_proj_a, v_w_proj_b, v_w_out, v_g_ffn2, v_w1_ffn2, v_w3_ffn2, v_w2_ffn2, v_g_final):
    a = dict(locals())
    bl, seq, _ = x.shape
    t_rows = bl * seq
    nc = seq // CH
    me = 4 * lax.axis_index("x") + 2 * lax.axis_index("y") + lax.axis_index("c")
    tm_ew = _pick(seq, (256, 128, 64))

    sm = all_gather("gather_small", _pack([c, conv_qkv[0]], F32))
    c_loc, conv_loc = _unpack8(sm, [c.shape, conv_qkv.shape[1:]])
    c_all = c_loc.reshape(NDEV * bl, D)
    conv_full = conv_loc.transpose(1, 0, 2).reshape(CONVW, 3 * DNW)
    rs_shapes = [a[n].shape[1:] for n in RS_WEIGHTS]
    wg = all_gather("gather_weights", _pack([a[n][0] for n in RS_WEIGHTS], BF16, 16))
    wfull = {n: _full(blk, n) for n, blk in zip(RS_WEIGHTS, _unpack8(wg, rs_shapes))}
    win = wfull['w_in']
    w_dn, w_small = win[:, :4 * DNW], _pad_cols(win[:, 4 * DNW:4 * DNW + 2 * NH], LANES)
    w_s5, w_gate = win[:, 4 * DNW + 2 * NH:4 * DNW + 2 * NH + S5W], win[:, 4 * DNW + 2 * NH + S5W:]

    n_ada = w_ada.shape[2]
    mod_part = ada_fwd(c_all, w_ada[0], lax.dynamic_slice(b_ada, (0, me * n_ada), (1, n_ada)))
    mod_all = all_gather("gather_mod", mod_part).transpose(1, 0, 2).reshape(NDEV * bl, 9 * D)
    mod = lax.dynamic_slice(mod_all, (me * bl, 0), (bl, 9 * D)).reshape(bl, 9, D)
    mods = [mod[:, k:k + 1, :] for k in range(9)]

    h0 = x.reshape(t_rows, D)
    h1, f1, u1 = ffn_fwd("ffn1_fwd", h0, mod[:, 0:3, :], g_ffn1, wfull['w1_ffn1'], wfull['w3_ffn1'], wfull['w2_ffn1'], seq)
    (u2,) = ew_call("mix_norm", fn_normmod, [h1], [mods[3], mods[4]], [g_mix], [(D, BF16)], tm_ew, seq)
    p_dn = mm("proj_dn", [(u2, w_dn)], False, F32)
    p_small = mm("proj_small", [(u2, w_small)], False, F32)
    p_s5 = mm("proj_s5", [(u2, w_s5)], False, F32)
    p_gate = mm("proj_gate", [(u2, w_gate)], False, F32)

    conv8 = jnp.pad(conv_full, ((0, 8 - CONVW), (0, 0)))
    alp = jnp.pad(a_log, ((0, 0), (NH, LANES - 2 * NH)))
    dtp = jnp.pad(dt_bias, ((0, 0), (NH, LANES - 2 * NH)))
    nb_dn = DN_ROWS if bl % DN_ROWS == 0 else 1
    p_dn3, p_small3 = p_dn.reshape(bl, seq, 4 * DNW), p_small.reshape(bl, seq, LANES)
    o_pre3, sprev, tinv = deltanet_fwd(p_dn3, p_small3, conv8, alp, dtp, nb_dn)
    o_pre = o_pre3.reshape(t_rows, DNW)
    z_raw = p_dn[:, 3 * DNW:]
    (oa,) = ew_call("dn_onorm", fn_onorm, [o_pre, z_raw], [], [g_onorm], [(DNW, BF16)], tm_ew, seq)
    ya = mm("proj_a", [(oa, wfull['w_proj_a'])], False, F32)

    s5_params = [lam_re.reshape(1, S5N), lam_im.reshape(1, S5N), log_step,
                 b_re[0].transpose(2, 0, 1).reshape(S5C, S5N), b_im[0].transpose(2, 0, 1).reshape(S5C, S5N),
                 c_re[0].transpose(1, 0, 2).reshape(S5C, S5N), c_im[0].transpose(1, 0, 2).reshape(S5C, S5N)]
    tables = s5_tables_fwd(s5_params)
    p_s53 = p_s5.reshape(bl, seq, S5W)
    y_s53, xs = s5_fwd(p_s53, tables, d_skip)
    y_s5 = y_s53.reshape(t_rows, S5W)
    (ob,) = ew_call("s5_glu", fn_glu, [y_s5], [], [wfull['w_glu'], b_glu], [(S5W, BF16)], tm_ew, seq)
    yb = mm("proj_b", [(ob, wfull['w_proj_b'])], False, F32)

    (merged,) = ew_call("merge", fn_merge, [p_gate, ya, yb], [], [], [(D, BF16)], tm_ew, seq)
    mo = mm("proj_out", [(merged, wfull['w_out'])], False, F32)
    (gmo,) = ew_call("mix_resid", fn_resid, [mo], [mods[5]], [], [(D, F32)], tm_ew, seq)
    h2 = ew_call("mix_add", lambda p, q: (p + q,), [h1, gmo], [], [], [(D, F32)], tm_ew, seq)[0]
    h3, f3, u3 = ffn_fwd("ffn2_fwd", h2, mod[:, 6:9, :], g_ffn2, wfull['w1_ffn2'], wfull['w3_ffn2'], wfull['w2_ffn2'], seq)

    dh3, dg_final, loss_part = loss_head(h3, loss_target.reshape(t_rows, D), g_final.reshape(1, D), seq)
    loss = lax.psum(loss_part[0, 0], ("x", "y", "c"))

    gw = {}
    dh2, a3, d1_3, d3_3, df3, dmod_c, dg_ffn2 = ffn_bwd("ffn2_bwd", dh3, h2, f3, u3, mod[:, 6:9, :], g_ffn2, wfull['w1_ffn2'],
                                                   wfull['w3_ffn2'], wfull['w2_ffn2'], seq)
    gw['w1_ffn2'] = mm_tn("gw1_ffn2", u3, d1_3)
    gw['w3_ffn2'] = mm_tn("gw3_ffn2", u3, d3_3)
    gw['w2_ffn2'] = mm_tn("gw2_ffn2", a3, df3)

    (dmo,), (dgt2,), _ = ew_vjp_call("mix_resid_bwd", fn_resid, [mo], [mods[5]], [], [dh2], [(0, BF16)], tm_ew, seq)
    gw['w_out'] = mm_tn("gw_out", merged, dmo)
    d_merged = mm("d_merged", [(dmo, wfull['w_out'])], True, F32)
    (d_gate, d_ya, d_yb), _, _ = ew_vjp_call("merge_bwd", fn_merge, [p_gate, ya, yb], [], [], [d_merged],
                                             [(0, BF16), (1, BF16), (2, BF16)], tm_ew, seq)
    gw['w_proj_a'] = mm_tn("gw_proj_a", oa, d_ya)
    gw['w_proj_b'] = mm_tn("gw_proj_b", ob, d_yb)
    d_oa = mm("d_oa", [(d_ya, wfull['w_proj_a'])], True, F32)
    d_ob = mm("d_ob", [(d_yb, wfull['w_proj_b'])], True, F32)

    (d_opre, d_z), _, (dg_onorm,) = ew_vjp_call("dn_onorm_bwd", fn_onorm, [o_pre, z_raw], [], [g_onorm], [d_oa],
                                                [(0, F32), (1, F32)], tm_ew, seq)
    d_pdn3, d_psmall3, d_conv8, d_alp, d_dtp = deltanet_bwd(
        p_dn3, p_small3, conv8, alp, dtp, sprev, tinv, d_opre.reshape(bl, seq, DNW), d_z.reshape(bl, seq, DNW), nb_dn)
    d_pdn, d_psmall = d_pdn3.reshape(t_rows, 4 * DNW), d_psmall3.reshape(t_rows, LANES)

    (d_ys5,), _, (g_wglu, dg_bglu) = ew_vjp_call("s5_glu_bwd", fn_glu, [y_s5], [], [wfull['w_glu'], b_glu], [d_ob],
                                                 [(0, F32)], tm_ew, seq)
    gw['w_glu'] = g_wglu
    s5_out = s5_bwd(p_s53, tables, d_skip, xs, d_ys5.reshape(bl, seq, S5W))
    d_ps5, d_tables, dg_dskip = s5_out[0].reshape(t_rows, S5W), s5_out[1:11], s5_out[11]
    d_s5p = s5_tables_bwd(s5_params, d_tables)

    d_pdn_b, d_psm_b, d_ps5_b = (t.astype(BF16) for t in (d_pdn, d_psmall, d_ps5))
    gw['w_in'] = jnp.concatenate([mm_tn("gw_dn", u2, d_pdn_b), mm_tn("gw_small", u2, d_psm_b)[:, :2 * NH],
                                  mm_tn("gw_s5", u2, d_ps5_b), mm_tn("gw_gate", u2, d_gate)], axis=1)
    du2 = mm("d_u2", [(d_pdn_b, w_dn), (d_psm_b, w_small), (d_ps5_b, w_s5), (d_gate, w_gate)], True, F32)
    (dh1_n,), (dsh2, dsc2), (dg_mix,) = ew_vjp_call("mix_norm_bwd", fn_normmod, [h1], [mods[3], mods[4]], [g_mix], [du2],
                                                    [(0, F32)], tm_ew, seq)
    dh1 = ew_call("mix_add_bwd", lambda p, q: (p + q,), [dh2, dh1_n], [], [], [(D, F32)], tm_ew, seq)[0]

    dh0, a1, d1_1, d3_1, df1, dmod_a, dg_ffn1 = ffn_bwd("ffn1_bwd", dh1, h0, f1, u1, mod[:, 0:3, :], g_ffn1, wfull['w1_ffn1'],
                                                   wfull['w3_ffn1'], wfull['w2_ffn1'], seq)
    gw['w1_ffn1'] = mm_tn("gw1_ffn1", u1, d1_1)
    gw['w3_ffn1'] = mm_tn("gw3_ffn1", u1, d3_1)
    gw['w2_ffn1'] = mm_tn("gw2_ffn1", a1, df1)

    rs_in = _pack8([_shards(gw[n], n) for n in RS_WEIGHTS], BF16)
    rs_out = all_to_all("scatter_grads", rs_in)
    n_rows = rs_out.shape[1]
    packed = [_pack([a[p + n][0] for n in RS_WEIGHTS], F32, 16) for p in ("", "m_", "v_")]
    rs_res = adamw("adamw_sharded", rs_out, *packed)
    res = {}
    for kind, buf in zip(("grad", "delta", "new_m", "new_v"), rs_res):
        for n, t in zip(RS_WEIGHTS, _unpack(buf, rs_shapes)):
            res[kind + "_" + n] = t[None]
    del n_rows

    dmod_mine = jnp.concatenate([dmod_a, dsh2, dsc2, dgt2, dmod_c], axis=1).reshape(bl, 9 * D)
    small_grads = {
        'g_ffn1': dg_ffn1, 'g_mix': dg_mix, 'a_log': d_alp[:, NH:2 * NH], 'dt_bias': d_dtp[:, NH:2 * NH],
        'g_onorm': dg_onorm, 'lam_re': d_s5p[0].reshape(1, S5G, S5P), 'lam_im': d_s5p[1].reshape(1, S5G, S5P),
        'log_step': d_s5p[2],
        'b_re': d_s5p[3].reshape(S5C, S5G, S5P).transpose(1, 2, 0)[None],
        'b_im': d_s5p[4].reshape(S5C, S5G, S5P).transpose(1, 2, 0)[None],
        'c_re': d_s5p[5].reshape(S5C, S5G, S5P).transpose(1, 0, 2)[None],
        'c_im': d_s5p[6].reshape(S5C, S5G, S5P).transpose(1, 0, 2)[None],
        'd_skip': dg_dskip, 'b_glu': dg_bglu, 'g_ffn2': dg_ffn2, 'g_final': dg_final.reshape(D)}
    small_shapes = [a[n].shape for n in SMALL]
    sg = all_gather("gather_small_grads", _pack([dmod_mine, d_conv8[:CONVW]] + [small_grads[n] for n in SMALL], F32))
    pieces = _unpack8(sg, [dmod_mine.shape, (CONVW, 3 * DNW)] + small_shapes)
    dmod_all = pieces[0].reshape(NDEV * bl, 9 * D)
    g_wada, g_bada = ada_bwd(c_all, lax.dynamic_slice(dmod_all, (0, me * n_ada), (NDEV * bl, n_ada)), dmod_all)

    n_conv = conv_qkv.shape[2]
    conv_parts = lax.dynamic_slice(pieces[1], (0, 0, me * n_conv), (NDEV, CONVW, n_conv))
    conv_parts = jnp.pad(conv_parts.reshape(NDEV, 1, -1), ((0, 0), (0, 7), (0, 0)))
    pad8 = lambda t: jnp.pad(t.reshape(1, -1), ((0, 7), (0, 0)))
    conv_res = adamw("adamw_conv", conv_parts, pad8(conv_qkv), pad8(m_conv_qkv), pad8(v_conv_qkv))
    for kind, buf in zip(("grad", "delta", "new_m", "new_v"), conv_res):
        res[kind + "_conv_qkv"] = buf[0].reshape(conv_qkv.shape)

    small_parts = jnp.stack([_pack([p[k] for p in pieces[2:]], F32) for k in range(NDEV)])
    small_res = adamw("adamw_small", small_parts, *[_pack([a[p + n] for n in SMALL], F32) for p in ("", "m_", "v_")])
    for kind, buf in zip(("grad", "delta", "new_m", "new_v"), small_res):
        for n, t in zip(SMALL, _unpack(buf, small_shapes)):
            res[kind + "_" + n] = t

    for n, g in (("w_ada", g_wada), ("b_ada", g_bada)):
        shp = a[n].shape
        r2 = lambda t: t.reshape(-1, shp[-1]) if n == "w_ada" else pad8(t)
        out = adamw("adamw_" + n, r2(g)[None], r2(a[n]), r2(a["m_" + n]), r2(a["v_" + n]))
        for kind, buf in zip(("grad", "delta", "new_m", "new_v"), out):
            res[kind + "_" + n] = (buf if n == "w_ada" else buf[0:1]).reshape(shp)

    outs = [loss, dh0.reshape(x.shape)]
    for kind in ("grad", "delta", "new_m", "new_v"):
        outs += [res[kind + "_" + n] for n in WEIGHTS]
    return tuple(outs)
```

```python
import functools
import math

import jax
import jax.numpy as jnp
from jax import lax
from jax.experimental import pallas as pl
from jax.experimental.pallas import tpu as pltpu

F32 = jnp.float32
BF16 = jnp.bfloat16
HI = lax.Precision.HIGHEST
SDS = jax.ShapeDtypeStruct

D = 1024
FF = 2816
NH = 8
DH = 64
DNW = NH * DH
CONVW = 4
CH = 64
DN_ROWS = 2
S5W = 512
S5G = 32
S5P = 64
S5C = 16
S5N = S5G * S5P
GB = 4
NDEV = 8
EPS = 1e-6
LANES = 128
ROW = 1024
VMEM_LIMIT = 56 * 1024 * 1024

ADAM_LR, ADAM_B1, ADAM_B2, ADAM_EPS, ADAM_WD, ADAM_STEP = 0.001, 0.9, 0.999, 1e-08, 0.01, 10

WEIGHTS = ['w_ada', 'b_ada', 'g_ffn1', 'w1_ffn1', 'w3_ffn1', 'w2_ffn1', 'g_mix', 'w_in', 'conv_qkv', 'a_log',
           'dt_bias', 'g_onorm', 'lam_re', 'lam_im', 'log_step', 'b_re', 'b_im', 'c_re', 'c_im', 'd_skip', 'w_glu',
           'b_glu', 'w_proj_a', 'w_proj_b', 'w_out', 'g_ffn2', 'w1_ffn2', 'w3_ffn2', 'w2_ffn2', 'g_final']
RS_WEIGHTS = ['w1_ffn1', 'w3_ffn1', 'w2_ffn1', 'w_in', 'w_glu', 'w_proj_a', 'w_proj_b', 'w_out', 'w1_ffn2', 'w3_ffn2',
              'w2_ffn2']
COL_SHARDED = {'w1_ffn1', 'w3_ffn1', 'w_in', 'w_proj_a', 'w_proj_b', 'w1_ffn2', 'w3_ffn2'}
SMALL = ['g_ffn1', 'g_mix', 'a_log', 'dt_bias', 'g_onorm', 'lam_re', 'lam_im', 'log_step', 'b_re', 'b_im', 'c_re',
         'c_im', 'd_skip', 'b_glu', 'g_ffn2', 'g_final']


def _cp(n_grid=0):
    if n_grid:
        return pltpu.CompilerParams(vmem_limit_bytes=VMEM_LIMIT, dimension_semantics=("arbitrary",) * n_grid)
    return pltpu.CompilerParams(vmem_limit_bytes=VMEM_LIMIT)


def _dot(a, b):
    return jnp.dot(a.astype(BF16), b.astype(BF16), preferred_element_type=F32)


def _dot_nt(a, b):
    return lax.dot_general(a.astype(BF16), b.astype(BF16), (((1,), (1,)), ((), ())), preferred_element_type=F32)


def _dot_tn(a, b):
    return lax.dot_general(a.astype(BF16), b.astype(BF16), (((0,), (0,)), ((), ())), preferred_element_type=F32)


def _dot_hi(a, b):
    return jnp.dot(a, b, precision=HI, preferred_element_type=F32)


@jax.custom_vjp
def bdot(a, b):
    return _dot(a, b)


bdot.defvjp(lambda a, b: (_dot(a, b), (a, b)),
            lambda r, g: (_dot_nt(g, r[1]).astype(r[0].dtype), _dot_tn(r[0], g).astype(r[1].dtype)))


@jax.custom_vjp
def bdot_nt(a, b):
    return _dot_nt(a, b)


bdot_nt.defvjp(lambda a, b: (_dot_nt(a, b), (a, b)),
               lambda r, g: (_dot(g, r[1]).astype(r[0].dtype), _dot_tn(g, r[0]).astype(r[1].dtype)))


@jax.custom_vjp
def bdot_tn(a, b):
    return _dot_tn(a, b)


bdot_tn.defvjp(lambda a, b: (_dot_tn(a, b), (a, b)),
               lambda r, g: (_dot_nt(r[1], g).astype(r[0].dtype), _dot(r[0], g).astype(r[1].dtype)))


def _silu(x):
    return x * jax.nn.sigmoid(x)


def _iota2(shape, axis):
    return lax.broadcasted_iota(jnp.int32, shape, axis)


def normmod(h, g, sc, sh):
    y = h * lax.rsqrt(jnp.mean(h * h, axis=-1, keepdims=True) + EPS) * g
    return y * (1.0 + sc) + sh


def fn_normmod(h, sh, sc, g):
    return (normmod(h, g, sc, sh),)


def fn_resid(mo, gt):
    return (gt * mo,)


def fn_merge(gate, ya, yb):
    return (jax.nn.sigmoid(gate[:, :D]) * ya + jax.nn.sigmoid(gate[:, D:]) * yb,)


def fn_glu(y, w, b):
    ge = jax.nn.gelu(y)
    return (ge * jax.nn.sigmoid(bdot(ge, w) + b),)


def fn_onorm(o, z, g_on):
    r = _iota2((DH, DNW), 0)
    c = _iota2((DH, DNW), 1)
    expand = (c % DH == r).astype(F32)
    r2 = _iota2((DNW, DNW), 0)
    c2 = _iota2((DNW, DNW), 1)
    avg = (r2 // DH == c2 // DH).astype(F32) * (1.0 / DH)
    ms = _dot_hi(o * o, avg)
    return (o * lax.rsqrt(ms + EPS) * _dot_hi(g_on, expand) * _silu(z),)


def gate_fn(small, alp, dtp):
    beta = jax.nn.sigmoid(small)
    la = -jnp.exp(alp) * jax.nn.softplus(small + dtp)
    tri = (_iota2((CH, CH), 0) >= _iota2((CH, CH), 1)).astype(F32)
    gc = _dot_hi(tri, la)
    gct = lax.dot_general(la, tri, (((0,), (1,)), ((), ())), precision=HI, preferred_element_type=F32)
    return beta, gc, gct


def _bdg(a, b, ca, cb, hi):
    if not hi:
        a, b = a.astype(BF16), b.astype(BF16)
    return lax.dot_general(a, b, (((ca,), (cb,)), ((0,), (0,))), precision=HI if hi else None,
                           preferred_element_type=F32)


def _batched_matmuls(hi):
    nn_ = lambda a, b: _bdg(a, b, 2, 1, hi)
    nt_ = lambda a, b: _bdg(a, b, 2, 2, hi)
    tn_ = lambda a, b: _bdg(a, b, 1, 1, hi)
    nn = jax.custom_vjp(nn_)
    nn.defvjp(lambda a, b: (nn_(a, b), (a, b)), lambda r, g: (nt_(g, r[1]), tn_(r[0], g)))
    nt = jax.custom_vjp(nt_)
    nt.defvjp(lambda a, b: (nt_(a, b), (a, b)), lambda r, g: (nn_(g, r[1]), tn_(g, r[0])))
    tn = jax.custom_vjp(tn_)
    tn.defvjp(lambda a, b: (tn_(a, b), (a, b)), lambda r, g: (nt_(r[1], g), nn_(r[0], g)))
    return nn, nt, tn


bnn, bnt, btn = _batched_matmuls(False)
hnn, hnt, htn = _batched_matmuls(True)


def _unit_lower_inverse(a):
    r = _iota2((1, CH, CH), 1)
    c = _iota2((1, CH, CH), 2)
    eye = (r == c).astype(F32)
    d = jnp.where(r // 8 == c // 8, a, 0.0)
    inv = eye - d
    p = d
    for _ in range(2):
        p = hnn(p, p)
        inv = inv + hnn(inv, p)
    for blk in (16, 32, 64):
        off = jnp.where((r // blk == c // blk) & (r // (blk // 2) != c // (blk // 2)), a, 0.0)
        inv = inv - hnn(hnn(inv, off), inv)
    return inv


@jax.custom_vjp
def _inverse_given(a, t):
    return t


_inverse_given.defvjp(lambda a, t: (t, t), lambda t, g: (-hnt(htn(t, g), t), jnp.zeros_like(t)))


def _conv_act(x, w):
    c = x[:, 5:69] * w[:, 0:1] + x[:, 6:70] * w[:, 1:2] + x[:, 7:71] * w[:, 2:3] + x[:, 8:72] * w[:, 3:4]
    return _silu(c)


def dn_chunk(xq, xk, xv, wq, wk, wv, b, g, gt, s_prev, t_saved=None):
    q = _conv_act(xq, wq)
    k = _conv_act(xk, wk)
    v = _conv_act(xv, wv)
    q = q * lax.rsqrt(jnp.sum(q * q, axis=-1, keepdims=True) + EPS) * (DH ** -0.5)
    k = k * lax.rsqrt(jnp.sum(k * k, axis=-1, keepdims=True) + EPS)
    r = _iota2((1, CH, CH), 1)
    c = _iota2((1, CH, CH), 2)
    causal = r >= c
    dec = jnp.where(causal, jnp.exp(jnp.where(causal, g - gt, 0.0)), 0.0)
    kb = k * b
    qk = bnt(jnp.concatenate([q, kb], axis=1), k)
    attn = qk[:, :CH] * dec
    a = jnp.where(r > c, qk[:, CH:] * dec, 0.0)
    tinv = _unit_lower_inverse(a) if t_saved is None else _inverse_given(a, t_saved)
    eg = jnp.exp(g)
    uw = hnn(tinv, jnp.concatenate([v * b, kb * eg], axis=2))
    g_last = g[:, CH - 1:CH]
    ws = bnn(jnp.concatenate([uw[..., DH:], q * eg], axis=1), s_prev)
    v_new = uw[..., :DH] - ws[:, :CH]
    o = ws[:, CH:] + bnn(attn, v_new)
    s_new = s_prev * jnp.exp(g_last) + btn(k * jnp.exp(g_last - g), v_new)
    return o, s_new, tinv


def s5_chunk(u, xp_re, xp_im, bb_re, bb_im, cc_re, cc_im, p0r, p0i, p1r, p1i, pir, pii, dsk):
    nb = u.shape[0]
    u2 = u.reshape(nb * CH, LANES)
    bu_re = bdot(u2, bb_re).reshape(nb, CH, 512)
    bu_im = bdot(u2, bb_im).reshape(nb, CH, 512)
    xt_re = pir * bu_re - pii * bu_im
    xt_im = pir * bu_im + pii * bu_re
    tri = jnp.broadcast_to((_iota2((1, CH, CH), 1) >= _iota2((1, CH, CH), 2)).astype(F32), (nb, CH, CH))
    cs_re = hnn(tri, xt_re)
    cs_im = hnn(tri, xt_im)
    x_re = p0r * cs_re - p0i * cs_im + p1r * xp_re - p1i * xp_im
    x_im = p0r * cs_im + p0i * cs_re + p1r * xp_im + p1i * xp_re
    y = bdot_nt(x_re.reshape(nb * CH, 512), cc_re) - bdot_nt(x_im.reshape(nb * CH, 512), cc_im) + dsk * u2
    return y.reshape(nb, CH, LANES), x_re[:, CH - 1:CH], x_im[:, CH - 1:CH]


def s5_tables(lam_re, lam_im, log_step, bre, bim, cre, cim):
    expand = (_iota2((S5G, S5N), 1) // S5P == _iota2((S5G, S5N), 0)).astype(F32)
    step = _dot_hi(jnp.exp(log_step), expand)
    lre = jnp.minimum(lam_re, -1e-4)
    lr = lre * step
    ang = lam_im * step
    mag = jnp.exp(lr)
    lb_re = mag * jnp.cos(ang)
    lb_im = mag * jnp.sin(ang)
    den = lre * lre + lam_im * lam_im
    coef_re = ((lb_re - 1.0) * lre + lb_im * lam_im) / den
    coef_im = (lb_im * lre - (lb_re - 1.0) * lam_im) / den
    bb_re = coef_re * bre - coef_im * bim
    bb_im = coef_re * bim + coef_im * bre
    j = _iota2((CH, 1), 0).astype(F32)
    e0 = jnp.exp(j * lr)
    e1 = jnp.exp((j + 1.0) * lr)
    ei = jnp.exp(-j * lr)
    mask = (_iota2((LANES, 512), 0) // S5C == _iota2((LANES, 512), 1) // S5P).astype(F32)

    def blocks(t):
        return jnp.concatenate([(jnp.tile(t[:, gb * 512:(gb + 1) * 512], (LANES // S5C, 1)) * mask)[None]
                                for gb in range(GB)], axis=0)

    return (blocks(bb_re), blocks(bb_im), blocks(cre), blocks(cim),
            e0 * jnp.cos(j * ang), e0 * jnp.sin(j * ang),
            e1 * jnp.cos((j + 1.0) * ang), e1 * jnp.sin((j + 1.0) * ang),
            ei * jnp.cos(j * ang), -ei * jnp.sin(j * ang))


def _row_specs(tiled, batch, bcast, tm, tpb):
    specs = [pl.BlockSpec((tm, a.shape[1]), lambda i: (i, 0)) for a in tiled]
    specs += [pl.BlockSpec((None,) + a.shape[1:], lambda i: (i // tpb, 0, 0)) for a in batch]
    specs += [pl.BlockSpec(a.shape, lambda i, nd=a.ndim: (0,) * nd) for a in bcast]
    return specs


def ew_call(name, fn, tiled, batch, bcast, outs, tm, seq):
    t_rows = tiled[0].shape[0]
    n_in = len(tiled) + len(batch) + len(bcast)

    def body(*refs):
        vals = [r[...].astype(F32) for r in refs[:n_in]]
        for r, o in zip(refs[n_in:], fn(*vals)):
            r[...] = o.astype(r.dtype)

    return pl.pallas_call(
        body, grid=(t_rows // tm,), in_specs=_row_specs(tiled, batch, bcast, tm, seq // tm),
        out_specs=[pl.BlockSpec((tm, w), lambda i: (i, 0)) for w, _ in outs],
        out_shape=[SDS((t_rows, w), dt) for w, dt in outs], name=name, compiler_params=_cp(1))(*tiled, *batch, *bcast)


def ew_vjp_call(name, fn, tiled, batch, bcast, cts, want, tm, seq, addend=None):
    t_rows = tiled[0].shape[0]
    tpb = seq // tm
    n_t, n_b, n_c = len(tiled), len(batch), len(bcast)
    n_in = n_t + n_b + n_c
    extra = [] if addend is None else [addend]

    def body(*refs):
        i = pl.program_id(0)
        vals = [r[...].astype(F32) for r in refs[:n_in]]
        ctv = tuple(r[...].astype(F32) for r in refs[n_in:n_in + len(cts)])
        outs = refs[n_in + len(cts) + len(extra):]
        _, vjp = jax.vjp(fn, *vals)
        grads = vjp(ctv)
        for k, (r, (idx, _)) in enumerate(zip(outs[:len(want)], want)):
            g = grads[idx]
            if k == 0 and extra:
                g = g + refs[n_in + len(cts)][...]
            r[...] = g.astype(r.dtype)
        for k in range(n_b):
            r, g = outs[len(want) + k], grads[n_t + k]

            @pl.when(i % tpb == 0)
            def _(r=r, g=g):
                r[...] = g

            @pl.when(i % tpb != 0)
            def _(r=r, g=g):
                r[...] += g
        for k in range(n_c):
            r, g = outs[len(want) + n_b + k], grads[n_t + n_b + k]

            @pl.when(i == 0)
            def _(r=r, g=g):
                r[...] = g

            @pl.when(i != 0)
            def _(r=r, g=g):
                r[...] += g

    out_specs = [pl.BlockSpec((tm, tiled[idx].shape[1]), lambda i: (i, 0)) for idx, _ in want]
    out_specs += [pl.BlockSpec((None,) + a.shape[1:], lambda i: (i // tpb, 0, 0)) for a in batch]
    out_specs += [pl.BlockSpec(a.shape, lambda i, nd=a.ndim: (0,) * nd) for a in bcast]
    out_shape = [SDS(tiled[idx].shape, dt) for idx, dt in want]
    out_shape += [SDS(a.shape, F32) for a in batch] + [SDS(a.shape, F32) for a in bcast]
    res = pl.pallas_call(
        body, grid=(t_rows // tm,),
        in_specs=_row_specs(tiled, batch, bcast, tm, tpb)
        + [pl.BlockSpec((tm, a.shape[1]), lambda i: (i, 0)) for a in list(cts) + extra],
        out_specs=out_specs, out_shape=out_shape, name=name, compiler_params=_cp(1))(*tiled, *batch, *bcast, *cts, *extra)
    return res[:len(want)], res[len(want):len(want) + n_b], res[len(want) + n_b:]


def _pick(n, cands):
    for c in cands:
        if n % c == 0:
            return c
    return n


def mm(name, pairs, nt, out_dtype):
    m = pairs[0][0].shape[0]
    n = pairs[0][1].shape[0 if nt else 1]
    tm = _pick(m, (512, 256, 128))
    tn = _pick(n, (512, 256, 128))
    np_ = len(pairs)

    def body(*refs):
        acc = None
        for p in range(np_):
            a, b = refs[2 * p][...], refs[2 * p + 1][...]
            t = _dot_nt(a, b) if nt else _dot(a, b)
            acc = t if acc is None else acc + t
        refs[2 * np_][...] = acc.astype(out_dtype)

    in_specs, ops = [], []
    for a, b in pairs:
        k = a.shape[1]
        in_specs.append(pl.BlockSpec((tm, k), lambda i, j: (i, 0)))
        in_specs.append(pl.BlockSpec((tn, k), lambda i, j: (j, 0)) if nt else pl.BlockSpec((k, tn), lambda i, j: (0, j)))
        ops += [a, b]
    return pl.pallas_call(
        body, grid=(m // tm, n // tn), in_specs=in_specs, out_specs=pl.BlockSpec((tm, tn), lambda i, j: (i, j)),
        out_shape=SDS((m, n), out_dtype), name=name, compiler_params=_cp(2))(*ops)


def mm_tn(name, a, b):
    t_rows, m = a.shape
    n = b.shape[1]
    tn = n if n <= 1024 else _pick(n, (1024, 512, 256, 128))
    tm = max([t for t in range(LANES, m + 1, LANES) if m % t == 0 and t * tn * 4 <= 6 * 1024 * 1024] or [m])
    tk = _pick(t_rows, (512, 256, 128, 64))

    def body(a_ref, b_ref, o_ref):
        @pl.when(pl.program_id(2) == 0)
        def _():
            o_ref[...] = jnp.zeros_like(o_ref)

        o_ref[...] += _dot_tn(a_ref[...], b_ref[...])

    return pl.pallas_call(
        body, grid=(m // tm, n // tn, t_rows // tk),
        in_specs=[pl.BlockSpec((tk, tm), lambda i, j, k: (k, i)), pl.BlockSpec((tk, tn), lambda i, j, k: (k, j))],
        out_specs=pl.BlockSpec((tm, tn), lambda i, j, k: (i, j)), out_shape=SDS((m, n), F32), name=name,
        compiler_params=_cp(3))(a, b)


def ffn_fwd(name, h, mod3, g, w1, w3, w2, seq):
    t_rows = h.shape[0]
    tm = _pick(seq, (512, 256, 128, 64))
    tf = 256
    tpb = seq // tm
    nf = FF // tf

    def body(h_ref, mod_ref, g_ref, w1_ref, w3_ref, w2_ref, ho_ref, f_ref, u_ref, acc):
        j = pl.program_id(1)

        @pl.when(j == 0)
        def _():
            u_ref[...] = normmod(h_ref[...], g_ref[...], mod_ref[1:2, :], mod_ref[0:1, :]).astype(BF16)
            acc[...] = jnp.zeros_like(acc)

        u = u_ref[...]
        a = _silu(_dot_nt(u, w1_ref[...])) * _dot_nt(u, w3_ref[...])
        acc[...] += _dot(a, w2_ref[...])

        @pl.when(j == nf - 1)
        def _():
            f_ref[...] = acc[...]
            ho_ref[...] = h_ref[...] + 0.5 * mod_ref[2:3, :] * acc[...]

    row = lambda i, j: (i, 0)
    return pl.pallas_call(
        body, grid=(t_rows // tm, nf),
        in_specs=[pl.BlockSpec((tm, D), row), pl.BlockSpec((None, 3, D), lambda i, j: (i // tpb, 0, 0)),
                  pl.BlockSpec((1, D), lambda i, j: (0, 0)), pl.BlockSpec((tf, D), lambda i, j: (j, 0)),
                  pl.BlockSpec((tf, D), lambda i, j: (j, 0)), pl.BlockSpec((tf, D), lambda i, j: (j, 0))],
        out_specs=[pl.BlockSpec((tm, D), row), pl.BlockSpec((tm, D), row), pl.BlockSpec((tm, D), row)],
        out_shape=[SDS((t_rows, D), F32), SDS((t_rows, D), F32), SDS((t_rows, D), BF16)],
        scratch_shapes=[pltpu.VMEM((tm, D), F32)], name=name, compiler_params=_cp(2))(h, mod3, g, w1, w3, w2)


def ffn_bwd(name, dho, h, f_out, u, mod3, g, w1, w3, w2, seq):
    t_rows = h.shape[0]
    tm = _pick(seq, (512, 256, 128, 64))
    tf = 256
    tpb = seq // tm
    nf = FF // tf

    def body(dho_ref, h_ref, f_ref, u_ref, mod_ref, g_ref, w1_ref, w3_ref, w2_ref,
             dh_ref, a_ref, dh1_ref, dh3_ref, df_scr, dmod_ref, dg_ref, du_acc):
        i, j = pl.program_id(0), pl.program_id(1)

        @pl.when(j == 0)
        def _():
            df_scr[...] = (0.5 * mod_ref[2:3, :] * dho_ref[...]).astype(BF16)
            du_acc[...] = jnp.zeros_like(du_acc)

        uu = u_ref[...]
        h1 = _dot_nt(uu, w1_ref[...])
        h3 = _dot_nt(uu, w3_ref[...])
        sg = jax.nn.sigmoid(h1)
        s = h1 * sg
        da = _dot_nt(df_scr[...], w2_ref[...])
        dh3 = (da * s).astype(BF16)
        dh1 = (da * h3 * (sg * (1.0 + h1 * (1.0 - sg)))).astype(BF16)
        a_ref[...] = (s * h3).astype(BF16)
        dh1_ref[...] = dh1
        dh3_ref[...] = dh3
        du_acc[...] += _dot(dh1, w1_ref[...]) + _dot(dh3, w3_ref[...])

        @pl.when(j == nf - 1)
        def _():
            _, vjp = jax.vjp(normmod, h_ref[...], g_ref[...], mod_ref[1:2, :], mod_ref[0:1, :])
            dh_n, dg, dsc, dsh = vjp(du_acc[...])
            dh_ref[...] = dho_ref[...] + dh_n
            dgt = jnp.sum(0.5 * dho_ref[...] * f_ref[...], axis=0, keepdims=True)
            dmod = jnp.concatenate([dsh, dsc, dgt], axis=0)

            @pl.when(i % tpb == 0)
            def _():
                dmod_ref[...] = dmod

            @pl.when(i % tpb != 0)
            def _():
                dmod_ref[...] += dmod

            @pl.when(i == 0)
            def _():
                dg_ref[...] = dg

            @pl.when(i != 0)
            def _():
                dg_ref[...] += dg

    row = lambda i, j: (i, 0)
    col = lambda i, j: (i, j)
    return pl.pallas_call(
        body, grid=(t_rows // tm, nf),
        in_specs=[pl.BlockSpec((tm, D), row), pl.BlockSpec((tm, D), row), pl.BlockSpec((tm, D), row),
                  pl.BlockSpec((tm, D), row), pl.BlockSpec((None, 3, D), lambda i, j: (i // tpb, 0, 0)),
                  pl.BlockSpec((1, D), lambda i, j: (0, 0)), pl.BlockSpec((tf, D), lambda i, j: (j, 0)),
                  pl.BlockSpec((tf, D), lambda i, j: (j, 0)), pl.BlockSpec((tf, D), lambda i, j: (j, 0))],
        out_specs=[pl.BlockSpec((tm, D), row), pl.BlockSpec((tm, tf), col), pl.BlockSpec((tm, tf), col),
                   pl.BlockSpec((tm, tf), col), pl.BlockSpec((tm, D), row),
                   pl.BlockSpec((None, 3, D), lambda i, j: (i // tpb, 0, 0)), pl.BlockSpec((1, D), lambda i, j: (0, 0))],
        out_shape=[SDS((t_rows, D), F32), SDS((t_rows, FF), BF16), SDS((t_rows, FF), BF16), SDS((t_rows, FF), BF16),
                   SDS((t_rows, D), BF16), SDS(mod3.shape, F32), SDS((1, D), F32)],
        scratch_shapes=[pltpu.VMEM((tm, D), F32)], name=name,
        compiler_params=_cp(2))(dho, h, f_out, u, mod3, g, w1, w3, w2)


def _dn_cols(part, hd):
    return slice(part * DNW + hd * DH, part * DNW + (hd + 1) * DH)


def _dn_stacks(raw_ref, halo_ref, conv_ref, hm, nb):
    pairs = [(b, hd) for b in range(nb) for hd in range(NH)]
    xs = [jnp.stack([jnp.concatenate([halo_ref[b, :, _dn_cols(part, hd)] * hm, raw_ref[b, :, _dn_cols(part, hd)]], axis=0)
                     for b, hd in pairs]) for part in range(3)]
    ws = [jnp.stack([conv_ref[0:CONVW, _dn_cols(part, hd)] for b, hd in pairs]) for part in range(3)]
    return xs, ws


def _gate_stacks(gates, nb):
    pairs = [(b, hd) for b in range(nb) for hd in range(NH)]
    bs = jnp.stack([gates[b][0][:, hd:hd + 1] for b, hd in pairs])
    gs = jnp.stack([gates[b][1][:, NH + hd:NH + hd + 1] for b, hd in pairs])
    gts = jnp.stack([gates[b][2][NH + hd:NH + hd + 1, :] for b, hd in pairs])
    return bs, gs, gts


def deltanet_fwd(p_dn, p_small, conv8, alp, dtp, nb):
    bl, seq, _ = p_dn.shape
    nc = seq // CH
    ng = nb * NH

    def body(raw_ref, halo_ref, small_ref, conv_ref, alp_ref, dtp_ref, o_ref, sprev_ref, tinv_ref, s_scr):
        n = pl.program_id(1)

        @pl.when(n == 0)
        def _():
            s_scr[...] = jnp.zeros_like(s_scr)

        hm = (n > 0).astype(F32)
        gates = [gate_fn(small_ref[b], alp_ref[...], dtp_ref[...]) for b in range(nb)]
        xs, ws = _dn_stacks(raw_ref, halo_ref, conv_ref, hm, nb)
        s_prev = s_scr[...]
        o, s_new, tinv = dn_chunk(*xs, *ws, *_gate_stacks(gates, nb), s_prev)
        sprev_ref[...] = s_prev
        tinv_ref[...] = tinv
        s_scr[...] = s_new
        for b in range(nb):
            for hd in range(NH):
                o_ref[b, :, hd * DH:(hd + 1) * DH] = o[b * NH + hd]

    blk = lambda bb, n: (bb, n, 0)
    const = lambda bb, n: (0, 0)
    saved = pl.BlockSpec((None, ng, DH, DH), lambda bb, n: (bb * nc + n, 0, 0, 0))
    return pl.pallas_call(
        body, grid=(bl // nb, nc),
        in_specs=[pl.BlockSpec((nb, CH, 4 * DNW), blk),
                  pl.BlockSpec((nb, 8, 3 * DNW), lambda bb, n: (bb, jnp.maximum(n * (CH // 8) - 1, 0), 0)),
                  pl.BlockSpec((nb, CH, LANES), blk), pl.BlockSpec((8, 3 * DNW), const), pl.BlockSpec((1, LANES), const),
                  pl.BlockSpec((1, LANES), const)],
        out_specs=[pl.BlockSpec((nb, CH, DNW), blk), saved, saved],
        out_shape=[SDS((bl, seq, DNW), F32), SDS((bl // nb * nc, ng, DH, DH), F32), SDS((bl // nb * nc, ng, DH, DH), F32)],
        scratch_shapes=[pltpu.VMEM((ng, DH, DH), F32)], name="deltanet_fwd",
        compiler_params=_cp(2))(p_dn, p_dn, p_small, conv8, alp, dtp)


def deltanet_bwd(p_dn, p_small, conv8, alp, dtp, sprev, tinv, d_o, d_z, nb):
    bl, seq, _ = p_dn.shape
    nc = seq // CH
    ng = nb * NH

    def body(raw_ref, halo_ref, small_ref, conv_ref, alp_ref, dtp_ref, sprev_ref, tinv_ref, do_ref, dz_ref,
             draw_ref, dsmall_ref, dconv_ref, dalp_ref, ddtp_ref, ds_scr, dhalo_scr):
        bb, r = pl.program_id(0), pl.program_id(1)
        n = nc - 1 - r

        @pl.when((bb == 0) & (r == 0))
        def _():
            dconv_ref[...] = jnp.zeros_like(dconv_ref)
            dalp_ref[...] = jnp.zeros_like(dalp_ref)
            ddtp_ref[...] = jnp.zeros_like(ddtp_ref)

        @pl.when(r == 0)
        def _():
            ds_scr[...] = jnp.zeros_like(ds_scr)
            dhalo_scr[...] = jnp.zeros_like(dhalo_scr)

        hm = (n > 0).astype(F32)
        gates, gate_vjps = [], []
        for b in range(nb):
            out, gvjp = jax.vjp(gate_fn, small_ref[b], alp_ref[...], dtp_ref[...])
            gates.append(out)
            gate_vjps.append(gvjp)
        xs, ws = _dn_stacks(raw_ref, halo_ref, conv_ref, hm, nb)
        t_saved = tinv_ref[...]
        _, vjp = jax.vjp(lambda *args: dn_chunk(*args, t_saved)[:2], *xs, *ws, *_gate_stacks(gates, nb), sprev_ref[...])
        d_out = jnp.stack([do_ref[b, :, hd * DH:(hd + 1) * DH] for b in range(nb) for hd in range(NH)])
        grads = vjp((d_out, ds_scr[...]))
        ds_scr[...] = grads[9]
        lane = _iota2((CH, LANES), 1)
        rowi = _iota2((LANES, CH), 0)
        for b in range(nb):
            d_beta = jnp.zeros((CH, LANES), F32)
            d_gc = jnp.zeros((CH, LANES), F32)
            d_gct = jnp.zeros((LANES, CH), F32)
            for hd in range(NH):
                i = b * NH + hd
                for part in range(3):
                    cols = _dn_cols(part, hd)
                    dx = grads[part][i]
                    draw_ref[b, 0:CH - 8, cols] = dx[8:CH]
                    draw_ref[b, CH - 8:CH, cols] = dx[CH:CH + 8] + dhalo_scr[b, :, cols]
                    dhalo_scr[b, :, cols] = dx[0:8] * hm
                d_beta = d_beta + jnp.where(lane == hd, grads[6][i], 0.0)
                d_gc = d_gc + jnp.where(lane == NH + hd, grads[7][i], 0.0)
                d_gct = d_gct + jnp.where(rowi == NH + hd, grads[8][i], 0.0)
            d_small, d_alp, d_dtp = gate_vjps[b]((d_beta, d_gc, d_gct))
            dsmall_ref[b] = d_small
            dalp_ref[...] += d_alp
            ddtp_ref[...] += d_dtp
            draw_ref[b, :, 3 * DNW:4 * DNW] = dz_ref[b]
        for hd in range(NH):
            for part in range(3):
                dw = grads[3 + part][hd]
                for b in range(1, nb):
                    dw = dw + grads[3 + part][b * NH + hd]
                dconv_ref[0:CONVW, _dn_cols(part, hd)] += dw

    blk = lambda bb, r: (bb, nc - 1 - r, 0)
    const = lambda bb, r: (0, 0)
    saved = pl.BlockSpec((None, ng, DH, DH), lambda bb, r: (bb * nc + nc - 1 - r, 0, 0, 0))
    return pl.pallas_call(
        body, grid=(bl // nb, nc),
        in_specs=[pl.BlockSpec((nb, CH, 4 * DNW), blk),
                  pl.BlockSpec((nb, 8, 3 * DNW), lambda bb, r: (bb, jnp.maximum((nc - 1 - r) * (CH // 8) - 1, 0), 0)),
                  pl.BlockSpec((nb, CH, LANES), blk), pl.BlockSpec((8, 3 * DNW), const), pl.BlockSpec((1, LANES), const),
                  pl.BlockSpec((1, LANES), const), saved, saved,
                  pl.BlockSpec((nb, CH, DNW), blk), pl.BlockSpec((nb, CH, DNW), blk)],
        out_specs=[pl.BlockSpec((nb, CH, 4 * DNW), blk), pl.BlockSpec((nb, CH, LANES), blk),
                   pl.BlockSpec((8, 3 * DNW), const), pl.BlockSpec((1, LANES), const), pl.BlockSpec((1, LANES), const)],
        out_shape=[SDS((bl, seq, 4 * DNW), F32), SDS((bl, seq, LANES), F32), SDS((8, 3 * DNW), F32), SDS((1, LANES), F32),
                   SDS((1, LANES), F32)],
        scratch_shapes=[pltpu.VMEM((ng, DH, DH), F32), pltpu.VMEM((nb, 8, 3 * DNW), F32)], name="deltanet_bwd",
        compiler_params=_cp(2))(p_dn, p_dn, p_small, conv8, alp, dtp, sprev, tinv, d_o, d_z)


def _s5_table_specs():
    tab3 = pl.BlockSpec((None, LANES, 512), lambda gb, n: (gb, 0, 0))
    tab2 = pl.BlockSpec((CH, 512), lambda gb, n: (0, gb))
    return [tab3] * 4 + [tab2] * 6 + [pl.BlockSpec((1, LANES), lambda gb, n: (0, gb))]


def s5_fwd(u, tables, dsk):
    bl, seq, _ = u.shape
    nc = seq // CH

    def body(u_ref, *rest):
        tabs, (y_ref, xs_ref, xr_scr, xi_scr) = rest[:11], rest[11:]

        @pl.when(pl.program_id(1) == 0)
        def _():
            xr_scr[...] = jnp.zeros_like(xr_scr)
            xi_scr[...] = jnp.zeros_like(xi_scr)

        xp_re, xp_im = xr_scr[...], xi_scr[...]
        xs_ref[0:bl] = xp_re
        xs_ref[bl:2 * bl] = xp_im
        y, xn_re, xn_im = s5_chunk(u_ref[...], xp_re, xp_im, *[t[...] for t in tabs])
        y_ref[...] = y
        xr_scr[...] = xn_re
        xi_scr[...] = xn_im

    blk = lambda gb, n: (0, n, gb)
    return pl.pallas_call(
        body, grid=(GB, nc), in_specs=[pl.BlockSpec((bl, CH, LANES), blk)] + _s5_table_specs(),
        out_specs=[pl.BlockSpec((bl, CH, LANES), blk),
                   pl.BlockSpec((None, 2 * bl, 1, 512), lambda gb, n: (gb * nc + n, 0, 0, 0))],
        out_shape=[SDS((bl, seq, S5W), F32), SDS((GB * nc, 2 * bl, 1, 512), F32)],
        scratch_shapes=[pltpu.VMEM((bl, 1, 512), F32), pltpu.VMEM((bl, 1, 512), F32)], name="s5_fwd",
        compiler_params=_cp(2))(u, *tables, dsk)


def s5_bwd(u, tables, dsk, xs, dy):
    bl, seq, _ = u.shape
    nc = seq // CH

    def body(u_ref, *rest):
        tabs, xs_ref, dy_ref = rest[:11], rest[11], rest[12]
        du_ref, dtabs, dxr_scr, dxi_scr = rest[13], rest[14:25], rest[25], rest[26]
        r = pl.program_id(1)

        @pl.when(r == 0)
        def _():
            for t in dtabs:
                t[...] = jnp.zeros_like(t)
            dxr_scr[...] = jnp.zeros_like(dxr_scr)
            dxi_scr[...] = jnp.zeros_like(dxi_scr)

        _, vjp = jax.vjp(s5_chunk, u_ref[...], xs_ref[0:bl], xs_ref[bl:2 * bl], *[t[...] for t in tabs])
        grads = vjp((dy_ref[...], dxr_scr[...], dxi_scr[...]))
        du_ref[...] = grads[0]
        dxr_scr[...] = grads[1]
        dxi_scr[...] = grads[2]
        for t, g in zip(dtabs, grads[3:]):
            t[...] += g

    blk = lambda gb, r: (0, nc - 1 - r, gb)
    tab_shapes = [SDS(t.shape, F32) for t in tables] + [SDS(dsk.shape, F32)]
    return pl.pallas_call(
        body, grid=(GB, nc),
        in_specs=[pl.BlockSpec((bl, CH, LANES), blk)] + _s5_table_specs()
        + [pl.BlockSpec((None, 2 * bl, 1, 512), lambda gb, r: (gb * nc + nc - 1 - r, 0, 0, 0)), pl.BlockSpec((bl, CH, LANES), blk)],
        out_specs=[pl.BlockSpec((bl, CH, LANES), blk)] + _s5_table_specs(),
        out_shape=[SDS((bl, seq, S5W), F32)] + tab_shapes,
        scratch_shapes=[pltpu.VMEM((bl, 1, 512), F32), pltpu.VMEM((bl, 1, 512), F32)], name="s5_bwd",
        compiler_params=_cp(2))(u, *tables, dsk, xs, dy)


def s5_tables_fwd(params):
    shapes = [SDS((GB, LANES, 512), F32)] * 4 + [SDS((CH, S5N), F32)] * 6

    def body(*refs):
        for r, t in zip(refs[7:], s5_tables(*[p[...] for p in refs[:7]])):
            r[...] = t

    return pl.pallas_call(body, out_shape=shapes, name="s5_tables_fwd", compiler_params=_cp())(*params)


def s5_tables_bwd(params, dtables):
    def body(*refs):
        _, vjp = jax.vjp(s5_tables, *[p[...] for p in refs[:7]])
        for r, g in zip(refs[17:], vjp(tuple(t[...] for t in refs[7:17]))):
            r[...] = g

    return pl.pallas_call(body, out_shape=[SDS(p.shape, F32) for p in params], name="s5_tables_bwd",
                          compiler_params=_cp())(*params, *dtables)


def ada_fwd(c_all, w_loc, b_loc):
    def body(c_ref, w_ref, b_ref, o_ref):
        o_ref[...] = _dot(_silu(c_ref[...]), w_ref[...]) + b_ref[...]

    return pl.pallas_call(body, out_shape=SDS((c_all.shape[0], w_loc.shape[1]), F32), name="ada_fwd",
                          compiler_params=_cp())(c_all, w_loc, b_loc)


def ada_bwd(c_all, dmod_mine, dmod_all):
    def body(c_ref, dm_ref, da_ref, gw_ref, gb_ref):
        gw_ref[...] = _dot_tn(_silu(c_ref[...]), dm_ref[...])
        gb_ref[...] = jnp.sum(da_ref[...], axis=0, keepdims=True)

    return pl.pallas_call(body, out_shape=[SDS((D, dmod_mine.shape[1]), F32), SDS((1, dmod_all.shape[1]), F32)],
                          name="ada_bwd", compiler_params=_cp())(c_all, dmod_mine, dmod_all)


def loss_head(h, tgt, g, seq):
    t_rows = h.shape[0]
    tm = _pick(seq, (256, 128, 64))

    def body(h_ref, t_ref, g_ref, dh_ref, dg_ref, loss_ref):
        i = pl.program_id(0)
        y, vjp = jax.vjp(lambda hh, gg: hh * lax.rsqrt(jnp.mean(hh * hh, axis=-1, keepdims=True) + EPS) * gg,
                         h_ref[...], g_ref[...])
        e = y - t_ref[...]
        dh, dg = vjp(e * (1.0 / D))
        part = jnp.sum(jnp.sum(e * e, axis=1, keepdims=True), axis=0, keepdims=True) * (0.5 / D) + jnp.zeros((1, LANES), F32)
        dh_ref[...] = dh

        @pl.when(i == 0)
        def _():
            dg_ref[...] = dg
            loss_ref[...] = part

        @pl.when(i != 0)
        def _():
            dg_ref[...] += dg
            loss_ref[...] += part

    row = lambda i: (i, 0)
    const = lambda i: (0, 0)
    return pl.pallas_call(
        body, grid=(t_rows // tm,),
        in_specs=[pl.BlockSpec((tm, D), row), pl.BlockSpec((tm, D), row), pl.BlockSpec((1, D), const)],
        out_specs=[pl.BlockSpec((tm, D), row), pl.BlockSpec((1, D), const), pl.BlockSpec((1, LANES), const)],
        out_shape=[SDS((t_rows, D), F32), SDS((1, D), F32), SDS((1, LANES), F32)], name="loss_head",
        compiler_params=_cp(1))(h, tgt, g)


def adamw(name, parts, w, m, v):
    k_parts, rows, cols = parts.shape
    tr = _pick(rows, (256, 128, 64, 32, 16, 8))

    def body(p_ref, w_ref, m_ref, v_ref, g_ref, d_ref, mo_ref, vo_ref):
        g = p_ref[0].astype(F32)
        for k in range(1, k_parts):
            g = g + p_ref[k].astype(F32)
        _adam_store(g, w_ref, m_ref, v_ref, g_ref, d_ref, mo_ref, vo_ref)

    blk = pl.BlockSpec((tr, cols), lambda i: (i, 0))
    return pl.pallas_call(
        body, grid=(rows // tr,), in_specs=[pl.BlockSpec((k_parts, tr, cols), lambda i: (0, i, 0)), blk, blk, blk],
        out_specs=[blk] * 4, out_shape=[SDS((rows, cols), F32)] * 4, name=name, compiler_params=_cp(1))(parts, w, m, v)


def _adam_store(g, w_ref, m_ref, v_ref, g_ref, d_ref, mo_ref, vo_ref):
    m_new = ADAM_B1 * m_ref[...] + (1.0 - ADAM_B1) * g
    v_new = ADAM_B2 * v_ref[...] + (1.0 - ADAM_B2) * (g * g)
    m_hat = m_new / (1.0 - ADAM_B1 ** ADAM_STEP)
    v_hat = v_new / (1.0 - ADAM_B2 ** ADAM_STEP)
    g_ref[...] = g
    d_ref[...] = -ADAM_LR * (m_hat / (jnp.sqrt(v_hat) + ADAM_EPS) + ADAM_WD * w_ref[...])
    mo_ref[...] = m_new
    vo_ref[...] = v_new


def adamw_t(name, parts, w, m, v):
    k_parts, r, c = parts.shape
    tc = _pick(c, (256, 128))

    def body(p_ref, w_ref, m_ref, v_ref, g_ref, d_ref, mo_ref, vo_ref):
        gt = p_ref[0].astype(F32)
        for k in range(1, k_parts):
            gt = gt + p_ref[k].astype(F32)
        _adam_store(gt.T, w_ref, m_ref, v_ref, g_ref, d_ref, mo_ref, vo_ref)

    blk = pl.BlockSpec((tc, r), lambda j: (j, 0))
    return pl.pallas_call(
        body, grid=(c // tc,), in_specs=[pl.BlockSpec((k_parts, r, tc), lambda j: (0, 0, j)), blk, blk, blk],
        out_specs=[blk] * 4, out_shape=[SDS((c, r), F32)] * 4, name=name, compiler_params=_cp(1))(parts, w, m, v)


def all_gather(name, x):
    rows, cols = x.shape

    def body(x_ref, out_ref, send_sems, recv_sems, local_sem):
        mx, my, mc = lax.axis_index("x"), lax.axis_index("y"), lax.axis_index("c")
        me, sibling = (mx, my, mc), (mx, my, 1 - mc)
        chips = [(1 - mx, my), (mx, 1 - my), (1 - mx, 1 - my)]

        def slot(px, py, pc):
            return out_ref.at[4 * px + 2 * py + pc]

        def copy(k, block, to, src=None):
            return pltpu.make_async_remote_copy(
                src_ref=slot(*block) if src is None else src, dst_ref=slot(*block), send_sem=send_sems.at[k],
                recv_sem=recv_sems.at[k], device_id=to, device_id_type=pl.DeviceIdType.MESH)

        mine = pltpu.make_async_copy(x_ref, slot(*me), local_sem)
        mine.start()
        first = [copy(0, me, sibling, src=x_ref)]
        first += [copy(1 + j, me, (*chip, mc), src=x_ref) for j, chip in enumerate(chips)]
        for cp in first:
            cp.start()
        passed = [copy(4 + j, (*chip, mc), sibling) for j, chip in enumerate(chips)]
        for j, chip in enumerate(chips):
            copy(1 + j, (*chip, mc), me).wait_recv()
            passed[j].start()
        copy(0, sibling, me).wait_recv()
        for j, chip in enumerate(chips):
            copy(4 + j, (*chip, 1 - mc), me).wait_recv()
        for cp in first + passed:
            cp.wait_send()
        mine.wait()

    return pl.pallas_call(
        body, out_shape=SDS((NDEV, rows, cols), x.dtype), in_specs=[pl.BlockSpec(memory_space=pl.ANY)],
        out_specs=pl.BlockSpec(memory_space=pl.ANY),
        scratch_shapes=[pltpu.SemaphoreType.DMA((7,)), pltpu.SemaphoreType.DMA((7,)), pltpu.SemaphoreType.DMA],
        name=name)(x)


def all_to_all(name, x):
    def body(x_ref, out_ref, send_sems, recv_sems, local_sem):
        mx, my, mc = lax.axis_index("x"), lax.axis_index("y"), lax.axis_index("c")
        me = 4 * mx + 2 * my + mc
        mine = pltpu.make_async_copy(x_ref.at[me], out_ref.at[me], local_sem)
        mine.start()
        copies = []
        for k in range(1, NDEV):
            px, py, pc = mx ^ (k >> 2), my ^ ((k >> 1) & 1), mc ^ (k & 1)
            copies.append(pltpu.make_async_remote_copy(
                src_ref=x_ref.at[4 * px + 2 * py + pc], dst_ref=out_ref.at[me], send_sem=send_sems.at[k - 1],
                recv_sem=recv_sems.at[k - 1], device_id=(px, py, pc), device_id_type=pl.DeviceIdType.MESH))
        for cp in copies:
            cp.start()
        for k in range(1, NDEV):
            px, py, pc = mx ^ (k >> 2), my ^ ((k >> 1) & 1), mc ^ (k & 1)
            pltpu.make_async_remote_copy(
                src_ref=x_ref.at[me], dst_ref=out_ref.at[4 * px + 2 * py + pc], send_sem=send_sems.at[k - 1],
                recv_sem=recv_sems.at[k - 1], device_id=(px, py, pc), device_id_type=pl.DeviceIdType.MESH).wait_recv()
        for cp in copies:
            cp.wait_send()
        mine.wait()

    return pl.pallas_call(
        body, out_shape=SDS(x.shape, x.dtype), in_specs=[pl.BlockSpec(memory_space=pl.ANY)],
        out_specs=pl.BlockSpec(memory_space=pl.ANY),
        scratch_shapes=[pltpu.SemaphoreType.DMA((7,)), pltpu.SemaphoreType.DMA((7,)), pltpu.SemaphoreType.DMA],
        name=name)(x)


def _pack(arrs, dtype, row_mult=8):
    segs = []
    for a in arrs:
        flat = a.reshape(-1).astype(dtype)
        segs.append(jnp.pad(flat, (0, (-flat.shape[0]) % ROW)))
    flat = jnp.concatenate(segs)
    flat = jnp.pad(flat, (0, (-flat.shape[0]) % (ROW * row_mult)))
    return flat.reshape(-1, ROW)


def _unpack(buf, shapes):
    flat = buf.reshape(-1)
    out, off = [], 0
    for s in shapes:
        n = math.prod(s)
        out.append(flat[off:off + n].reshape(s))
        off += n + (-n) % ROW
    return out


def _pack_rows(arrs, axis):
    padded = []
    for t in arrs:
        pad = [(0, 0)] * t.ndim
        pad[axis] = (0, _tile_rows(t.shape[axis]) - t.shape[axis])
        padded.append(jnp.pad(t, pad))
    return jnp.concatenate(padded, axis=axis)


def _tile_rows(r):
    return r + (-r) % 16


def _unpack8(buf, shapes):
    flat = buf.reshape(NDEV, -1)
    out, off = [], 0
    for s in shapes:
        n = math.prod(s)
        out.append(flat[:, off:off + n].reshape((NDEV,) + tuple(s)))
        off += n + (-n) % ROW
    return out


def kernel(x, c, w_ada, b_ada, g_ffn1, w1_ffn1, w3_ffn1, w2_ffn1, g_mix, w_in, conv_qkv, a_log, dt_bias, g_onorm, lam_re, lam_im, log_step, b_re, b_im, c_re, c_im, d_skip, w_glu, b_glu, w_proj_a, w_proj_b, w_out, g_ffn2, w1_ffn2, w3_ffn2, w2_ffn2, g_final, loss_target, m_w_ada, m_b_ada, m_g_ffn1, m_w1_ffn1, m_w3_ffn1, m_w2_ffn1, m_g_mix, m_w_in, m_conv_qkv, m_a_log, m_dt_bias, m_g_onorm, m_lam_re, m_lam_im, m_log_step, m_b_re, m_b_im, m_c_re, m_c_im, m_d_skip, m_w_glu, m_b_glu, m_w_proj_a, m_w_proj_b, m_w_out, m_g_ffn2, m_w1_ffn2, m_w3_ffn2, m_w2_ffn2, m_g_final, v_w_ada, v_b_ada, v_g_ffn1, v_w1_ffn1, v_w3_ffn1, v_w2_ffn1, v_g_mix, v_w_in, v_conv_qkv, v_a_log, v_dt_bias, v_g_onorm, v_lam_re, v_lam_im, v_log_step, v_b_re, v_b_im, v_c_re, v_c_im, v_d_skip, v_w_glu, v_b_glu, v_w_proj_a, v_w_proj_b, v_w_out, v_g_ffn2, v_w1_ffn2, v_w3_ffn2, v_w2_ffn2, v_g_final):
    a = dict(locals())
    bl, seq, _ = x.shape
    t_rows = bl * seq
    nc = seq // CH
    me = 4 * lax.axis_index("x") + 2 * lax.axis_index("y") + lax.axis_index("c")
    tm_ew = _pick(seq, (256, 128, 64))

    sm = all_gather("gather_small", _pack([c, conv_qkv[0]], F32))
    c_loc, conv_loc = _unpack8(sm, [c.shape, conv_qkv.shape[1:]])
    c_all = c_loc.reshape(NDEV * bl, D)
    conv_full = conv_loc.transpose(1, 0, 2).reshape(CONVW, 3 * DNW)
    loc = {n: (a[n][0].T if n in COL_SHARDED else a[n][0]) for n in RS_WEIGHTS}
    rs_rows = [loc[n].size // ROW for n in RS_WEIGHTS]
    wg = all_gather("gather_weights", _pack_rows([loc[n].astype(BF16).reshape(-1, ROW) for n in RS_WEIGHTS], 0))
    wfull, r0 = {}, 0
    for n, r in zip(RS_WEIGHTS, rs_rows):
        wfull[n] = wg[:, r0:r0 + r, :].reshape(-1, loc[n].shape[1])
        r0 += _tile_rows(r)
    win = wfull['w_in']
    o_small, o_s5, o_gate = 4 * DNW, 4 * DNW + 2 * NH, 4 * DNW + 2 * NH + S5W
    w_dn, w_small = win[:o_small], jnp.pad(win[o_small:o_s5], ((0, LANES - 2 * NH), (0, 0)))
    w_s5, w_gate = win[o_s5:o_gate], win[o_gate:]

    n_ada = w_ada.shape[2]
    mod_part = ada_fwd(c_all, w_ada[0], lax.dynamic_slice(b_ada, (0, me * n_ada), (1, n_ada)))
    mod_all = all_gather("gather_mod", mod_part).transpose(1, 0, 2).reshape(NDEV * bl, 9 * D)
    mod = lax.dynamic_slice(mod_all, (me * bl, 0), (bl, 9 * D)).reshape(bl, 9, D)
    mods = [mod[:, k:k + 1, :] for k in range(9)]

    h0 = x.reshape(t_rows, D)
    h1, f1, u1 = ffn_fwd("ffn1_fwd", h0, mod[:, 0:3, :], g_ffn1, wfull['w1_ffn1'], wfull['w3_ffn1'], wfull['w2_ffn1'], seq)
    (u2,) = ew_call("mix_norm", fn_normmod, [h1], [mods[3], mods[4]], [g_mix], [(D, BF16)], tm_ew, seq)
    p_dn = mm("proj_dn", [(u2, w_dn)], True, F32)
    p_small = mm("proj_small", [(u2, w_small)], True, F32)
    p_s5 = mm("proj_s5", [(u2, w_s5)], True, F32)
    p_gate = mm("proj_gate", [(u2, w_gate)], True, F32)

    conv8 = jnp.pad(conv_full, ((0, 8 - CONVW), (0, 0)))
    alp = jnp.pad(a_log, ((0, 0), (NH, LANES - 2 * NH)))
    dtp = jnp.pad(dt_bias, ((0, 0), (NH, LANES - 2 * NH)))
    nb_dn = DN_ROWS if bl % DN_ROWS == 0 else 1
    p_dn3, p_small3 = p_dn.reshape(bl, seq, 4 * DNW), p_small.reshape(bl, seq, LANES)
    o_pre3, sprev, tinv = deltanet_fwd(p_dn3, p_small3, conv8, alp, dtp, nb_dn)
    o_pre = o_pre3.reshape(t_rows, DNW)
    z_raw = p_dn[:, 3 * DNW:]
    (oa,) = ew_call("dn_onorm", fn_onorm, [o_pre, z_raw], [], [g_onorm], [(DNW, BF16)], tm_ew, seq)
    ya = mm("proj_a", [(oa, wfull['w_proj_a'])], True, F32)

    s5_params = [lam_re.reshape(1, S5N), lam_im.reshape(1, S5N), log_step,
                 b_re[0].transpose(2, 0, 1).reshape(S5C, S5N), b_im[0].transpose(2, 0, 1).reshape(S5C, S5N),
                 c_re[0].transpose(1, 0, 2).reshape(S5C, S5N), c_im[0].transpose(1, 0, 2).reshape(S5C, S5N)]
    tables = s5_tables_fwd(s5_params)
    p_s53 = p_s5.reshape(bl, seq, S5W)
    y_s53, xs = s5_fwd(p_s53, tables, d_skip)
    y_s5 = y_s53.reshape(t_rows, S5W)
    (ob,) = ew_call("s5_glu", fn_glu, [y_s5], [], [wfull['w_glu'], b_glu], [(S5W, BF16)], tm_ew, seq)
    yb = mm("proj_b", [(ob, wfull['w_proj_b'])], True, F32)

    (merged,) = ew_call("merge", fn_merge, [p_gate, ya, yb], [], [], [(D, BF16)], tm_ew, seq)
    mo = mm("proj_out", [(merged, wfull['w_out'])], False, F32)
    (h2,) = ew_call("mix_resid", lambda p, q, gt: (q + gt * p,), [mo, h1], [mods[5]], [], [(D, F32)], tm_ew, seq)
    h3, f3, u3 = ffn_fwd("ffn2_fwd", h2, mod[:, 6:9, :], g_ffn2, wfull['w1_ffn2'], wfull['w3_ffn2'], wfull['w2_ffn2'], seq)

    dh3, dg_final, loss_part = loss_head(h3, loss_target.reshape(t_rows, D), g_final.reshape(1, D), seq)
    loss = lax.psum(loss_part[0, 0], ("x", "y", "c"))

    gw = {}
    dh2, a3, d1_3, d3_3, df3, dmod_c, dg_ffn2 = ffn_bwd("ffn2_bwd", dh3, h2, f3, u3, mod[:, 6:9, :], g_ffn2, wfull['w1_ffn2'],
                                                   wfull['w3_ffn2'], wfull['w2_ffn2'], seq)
    gw['w1_ffn2'] = mm_tn("gw1_ffn2", d1_3, u3)
    gw['w3_ffn2'] = mm_tn("gw3_ffn2", d3_3, u3)
    gw['w2_ffn2'] = mm_tn("gw2_ffn2", a3, df3)

    (dmo,), (dgt2,), _ = ew_vjp_call("mix_resid_bwd", fn_resid, [mo], [mods[5]], [], [dh2], [(0, BF16)], tm_ew, seq)
    gw['w_out'] = mm_tn("gw_out", merged, dmo)
    d_merged = mm("d_merged", [(dmo, wfull['w_out'])], True, F32)
    (d_gate, d_ya, d_yb), _, _ = ew_vjp_call("merge_bwd", fn_merge, [p_gate, ya, yb], [], [], [d_merged],
                                             [(0, BF16), (1, BF16), (2, BF16)], tm_ew, seq)
    gw['w_proj_a'] = mm_tn("gw_proj_a", d_ya, oa)
    gw['w_proj_b'] = mm_tn("gw_proj_b", d_yb, ob)
    d_oa = mm("d_oa", [(d_ya, wfull['w_proj_a'])], False, F32)
    d_ob = mm("d_ob", [(d_yb, wfull['w_proj_b'])], False, F32)

    (d_opre, d_z), _, (dg_onorm,) = ew_vjp_call("dn_onorm_bwd", fn_onorm, [o_pre, z_raw], [], [g_onorm], [d_oa],
                                                [(0, F32), (1, F32)], tm_ew, seq)
    d_pdn3, d_psmall3, d_conv8, d_alp, d_dtp = deltanet_bwd(
        p_dn3, p_small3, conv8, alp, dtp, sprev, tinv, d_opre.reshape(bl, seq, DNW), d_z.reshape(bl, seq, DNW), nb_dn)
    d_pdn, d_psmall = d_pdn3.reshape(t_rows, 4 * DNW), d_psmall3.reshape(t_rows, LANES)

    (d_ys5,), _, (g_wglu, dg_bglu) = ew_vjp_call("s5_glu_bwd", fn_glu, [y_s5], [], [wfull['w_glu'], b_glu], [d_ob],
                                                 [(0, F32)], tm_ew, seq)
    gw['w_glu'] = g_wglu
    s5_out = s5_bwd(p_s53, tables, d_skip, xs, d_ys5.reshape(bl, seq, S5W))
    d_ps5, d_tables, dg_dskip = s5_out[0].reshape(t_rows, S5W), s5_out[1:11], s5_out[11]
    d_s5p = s5_tables_bwd(s5_params, d_tables)

    d_pdn_b, d_psm_b, d_ps5_b = (t.astype(BF16) for t in (d_pdn, d_psmall, d_ps5))
    gw['w_in'] = jnp.concatenate([mm_tn("gw_dn", d_pdn_b, u2), mm_tn("gw_small", d_psm_b, u2)[:2 * NH],
                                  mm_tn("gw_s5", d_ps5_b, u2), mm_tn("gw_gate", d_gate, u2)], axis=0)
    du2 = mm("d_u2", [(d_pdn_b, w_dn), (d_psm_b, w_small), (d_ps5_b, w_s5), (d_gate, w_gate)], False, F32)
    (dh1,), (dsh2, dsc2), (dg_mix,) = ew_vjp_call("mix_norm_bwd", fn_normmod, [h1], [mods[3], mods[4]], [g_mix], [du2],
                                                  [(0, F32)], tm_ew, seq, addend=dh2)

    dh0, a1, d1_1, d3_1, df1, dmod_a, dg_ffn1 = ffn_bwd("ffn1_bwd", dh1, h0, f1, u1, mod[:, 0:3, :], g_ffn1, wfull['w1_ffn1'],
                                                   wfull['w3_ffn1'], wfull['w2_ffn1'], seq)
    gw['w1_ffn1'] = mm_tn("gw1_ffn1", d1_1, u1)
    gw['w3_ffn1'] = mm_tn("gw3_ffn1", d3_1, u1)
    gw['w2_ffn1'] = mm_tn("gw2_ffn1", a1, df1)

    rs_in = _pack_rows([gw[n].astype(BF16).reshape(NDEV, -1, ROW) for n in RS_WEIGHTS], 1)
    rs_out = all_to_all("scatter_grads", rs_in)
    res, r0 = {}, 0
    for n, r in zip(RS_WEIGHTS, rs_rows):
        parts = rs_out[:, r0:r0 + r, :].reshape((NDEV,) + loc[n].shape)
        r0 += _tile_rows(r)
        update = adamw_t if n in COL_SHARDED else adamw
        out = update("adamw_" + n, parts, a[n][0], a["m_" + n][0], a["v_" + n][0])
        for kind, t in zip(("grad", "delta", "new_m", "new_v"), out):
            res[kind + "_" + n] = t[None]

    dmod_mine = jnp.concatenate([dmod_a, dsh2, dsc2, dgt2, dmod_c], axis=1).reshape(bl, 9 * D)
    small_grads = {
        'g_ffn1': dg_ffn1, 'g_mix': dg_mix, 'a_log': d_alp[:, NH:2 * NH], 'dt_bias': d_dtp[:, NH:2 * NH],
        'g_onorm': dg_onorm, 'lam_re': d_s5p[0].reshape(1, S5G, S5P), 'lam_im': d_s5p[1].reshape(1, S5G, S5P),
        'log_step': d_s5p[2],
        'b_re': d_s5p[3].reshape(S5C, S5G, S5P).transpose(1, 2, 0)[None],
        'b_im': d_s5p[4].reshape(S5C, S5G, S5P).transpose(1, 2, 0)[None],
        'c_re': d_s5p[5].reshape(S5C, S5G, S5P).transpose(1, 0, 2)[None],
        'c_im': d_s5p[6].reshape(S5C, S5G, S5P).transpose(1, 0, 2)[None],
        'd_skip': dg_dskip, 'b_glu': dg_bglu, 'g_ffn2': dg_ffn2, 'g_final': dg_final.reshape(D)}
    small_shapes = [a[n].shape for n in SMALL]
    sg = all_gather("gather_small_grads", _pack([dmod_mine, d_conv8[:CONVW]] + [small_grads[n] for n in SMALL], F32))
    pieces = _unpack8(sg, [dmod_mine.shape, (CONVW, 3 * DNW)] + small_shapes)
    dmod_all = pieces[0].reshape(NDEV * bl, 9 * D)
    g_wada, g_bada = ada_bwd(c_all, lax.dynamic_slice(dmod_all, (0, me * n_ada), (NDEV * bl, n_ada)), dmod_all)

    n_conv = conv_qkv.shape[2]
    conv_parts = lax.dynamic_slice(pieces[1], (0, 0, me * n_conv), (NDEV, CONVW, n_conv))
    conv_parts = jnp.pad(conv_parts.reshape(NDEV, 1, -1), ((0, 0), (0, 7), (0, 0)))
    pad8 = lambda t: jnp.pad(t.reshape(1, -1), ((0, 7), (0, 0)))
    conv_res = adamw("adamw_conv", conv_parts, pad8(conv_qkv), pad8(m_conv_qkv), pad8(v_conv_qkv))
    for kind, buf in zip(("grad", "delta", "new_m", "new_v"), conv_res):
        res[kind + "_conv_qkv"] = buf[0].reshape(conv_qkv.shape)

    small_parts = jnp.stack([_pack([p[k] for p in pieces[2:]], F32) for k in range(NDEV)])
    small_res = adamw("adamw_small", small_parts, *[_pack([a[p + n] for n in SMALL], F32) for p in ("", "m_", "v_")])
    for kind, buf in zip(("grad", "delta", "new_m", "new_v"), small_res):
        for n, t in zip(SMALL, _unpack(buf, small_shapes)):
            res[kind + "_" + n] = t

    for n, g in (("w_ada", g_wada), ("b_ada", g_bada)):
        shp = a[n].shape
        r2 = lambda t: t.reshape(-1, shp[-1]) if n == "w_ada" else pad8(t)
        out = adamw("adamw_" + n, r2(g)[None], r2(a[n]), r2(a["m_" + n]), r2(a["v_" + n]))
        for kind, buf in zip(("grad", "delta", "new_m", "new_v"), out):
            res[kind + "_" + n] = (buf if n == "w_ada" else buf[0:1]).reshape(shp)

    outs = [loss, dh0.reshape(x.shape)]
    for kind in ("grad", "delta", "new_m", "new_v"):
        outs += [res[kind + "_" + n] for n in WEIGHTS]
    return tuple(outs)
```

```python
import functools
import math

import jax
import jax.numpy as jnp
from jax import lax
from jax.experimental import pallas as pl
from jax.experimental.pallas import tpu as pltpu

F32 = jnp.float32
BF16 = jnp.bfloat16
HI = lax.Precision.HIGHEST
H3 = lax.Precision.HIGH
SDS = jax.ShapeDtypeStruct

D = 1024
FF = 2816
FFN_TF = FF // 2
NH = 8
DH = 64
DNW = NH * DH
CONVW = 4
CH = 64
DN_ROWS = 2
S5W = 512
S5G = 32
S5P = 64
S5C = 16
S5N = S5G * S5P
GB = 4
NDEV = 8
EPS = 1e-6
LANES = 128
ROW = 1024
VMEM_LIMIT = 56 * 1024 * 1024

ADAM_LR, ADAM_B1, ADAM_B2, ADAM_EPS, ADAM_WD, ADAM_STEP = 0.001, 0.9, 0.999, 1e-08, 0.01, 10

WEIGHTS = ['w_ada', 'b_ada', 'g_ffn1', 'w1_ffn1', 'w3_ffn1', 'w2_ffn1', 'g_mix', 'w_in', 'conv_qkv', 'a_log',
           'dt_bias', 'g_onorm', 'lam_re', 'lam_im', 'log_step', 'b_re', 'b_im', 'c_re', 'c_im', 'd_skip', 'w_glu',
           'b_glu', 'w_proj_a', 'w_proj_b', 'w_out', 'g_ffn2', 'w1_ffn2', 'w3_ffn2', 'w2_ffn2', 'g_final']
RS_WEIGHTS = ['w1_ffn1', 'w3_ffn1', 'w2_ffn1', 'w_in', 'w_glu', 'w_proj_a', 'w_proj_b', 'w_out', 'w1_ffn2', 'w3_ffn2',
              'w2_ffn2']
COL_SHARDED = {'w1_ffn1', 'w3_ffn1', 'w_in', 'w_proj_a', 'w_proj_b', 'w1_ffn2', 'w3_ffn2'}
SMALL = ['g_ffn1', 'g_mix', 'a_log', 'dt_bias', 'g_onorm', 'lam_re', 'lam_im', 'log_step', 'b_re', 'b_im', 'c_re',
         'c_im', 'd_skip', 'b_glu', 'g_ffn2', 'g_final']


def _cp(n_grid=0):
    if n_grid:
        return pltpu.CompilerParams(vmem_limit_bytes=VMEM_LIMIT, dimension_semantics=("arbitrary",) * n_grid)
    return pltpu.CompilerParams(vmem_limit_bytes=VMEM_LIMIT)


def _dot(a, b):
    return jnp.dot(a.astype(BF16), b.astype(BF16), preferred_element_type=F32)


def _dot_nt(a, b):
    return lax.dot_general(a.astype(BF16), b.astype(BF16), (((1,), (1,)), ((), ())), preferred_element_type=F32)


def _dot_tn(a, b):
    return lax.dot_general(a.astype(BF16), b.astype(BF16), (((0,), (0,)), ((), ())), preferred_element_type=F32)


def _dot_hi(a, b):
    return jnp.dot(a, b, precision=HI, preferred_element_type=F32)


def _dot_h3(a, b):
    return jnp.dot(a, b, precision=H3, preferred_element_type=F32)


@jax.custom_vjp
def bdot(a, b):
    return _dot(a, b)


bdot.defvjp(lambda a, b: (_dot(a, b), (a, b)),
            lambda r, g: (_dot_nt(g, r[1]).astype(r[0].dtype), _dot_tn(r[0], g).astype(r[1].dtype)))


@jax.custom_vjp
def bdot_nt(a, b):
    return _dot_nt(a, b)


bdot_nt.defvjp(lambda a, b: (_dot_nt(a, b), (a, b)),
               lambda r, g: (_dot(g, r[1]).astype(r[0].dtype), _dot_tn(g, r[0]).astype(r[1].dtype)))


@jax.custom_vjp
def bdot_tn(a, b):
    return _dot_tn(a, b)


bdot_tn.defvjp(lambda a, b: (_dot_tn(a, b), (a, b)),
               lambda r, g: (_dot_nt(r[1], g).astype(r[0].dtype), _dot(r[0], g).astype(r[1].dtype)))


def _silu(x):
    return x * jax.nn.sigmoid(x)


def _iota2(shape, axis):
    return lax.broadcasted_iota(jnp.int32, shape, axis)


def normmod(h, g, sc, sh):
    y = h * lax.rsqrt(jnp.mean(h * h, axis=-1, keepdims=True) + EPS) * g
    return y * (1.0 + sc) + sh


def fn_normmod(h, sh, sc, g):
    return (normmod(h, g, sc, sh),)


def fn_resid(mo, gt):
    return (gt * mo,)


def fn_merge(gate, ya, yb):
    return (jax.nn.sigmoid(gate[:, :D]) * ya + jax.nn.sigmoid(gate[:, D:]) * yb,)


def fn_glu(y, w, b):
    ge = jax.nn.gelu(y)
    return (ge * jax.nn.sigmoid(bdot(ge, w) + b),)


def fn_onorm(o, z, g_on):
    r = _iota2((DH, DNW), 0)
    c = _iota2((DH, DNW), 1)
    expand = (c % DH == r).astype(F32)
    r2 = _iota2((DNW, DNW), 0)
    c2 = _iota2((DNW, DNW), 1)
    avg = (r2 // DH == c2 // DH).astype(F32) * (1.0 / DH)
    ms = _dot_h3(o * o, avg)
    return (o * lax.rsqrt(ms + EPS) * _dot_hi(g_on, expand) * _silu(z),)


def gate_fn(small, alp, dtp):
    beta = jax.nn.sigmoid(small)
    la = -jnp.exp(alp) * jax.nn.softplus(small + dtp)
    tri = (_iota2((CH, CH), 0) >= _iota2((CH, CH), 1)).astype(F32)
    gc = _dot_hi(tri, la)
    gct = lax.dot_general(la, tri, (((0,), (1,)), ((), ())), precision=HI, preferred_element_type=F32)
    return beta, gc, gct


def _bdg(a, b, ca, cb, hi):
    if not hi:
        a, b = a.astype(BF16), b.astype(BF16)
    return lax.dot_general(a, b, (((ca,), (cb,)), ((0,), (0,))), precision=H3 if hi else None,
                           preferred_element_type=F32)


def _batched_matmuls(hi):
    nn_ = lambda a, b: _bdg(a, b, 2, 1, hi)
    nt_ = lambda a, b: _bdg(a, b, 2, 2, hi)
    tn_ = lambda a, b: _bdg(a, b, 1, 1, hi)
    nn = jax.custom_vjp(nn_)
    nn.defvjp(lambda a, b: (nn_(a, b), (a, b)), lambda r, g: (nt_(g, r[1]), tn_(r[0], g)))
    nt = jax.custom_vjp(nt_)
    nt.defvjp(lambda a, b: (nt_(a, b), (a, b)), lambda r, g: (nn_(g, r[1]), tn_(g, r[0])))
    tn = jax.custom_vjp(tn_)
    tn.defvjp(lambda a, b: (tn_(a, b), (a, b)), lambda r, g: (nt_(r[1], g), nn_(r[0], g)))
    return nn, nt, tn


bnn, bnt, btn = _batched_matmuls(False)
hnn, hnt, htn = _batched_matmuls(True)


def _unit_lower_inverse(a):
    r = _iota2((1, CH, CH), 1)
    c = _iota2((1, CH, CH), 2)
    eye = (r == c).astype(F32)
    d = jnp.where(r // 8 == c // 8, a, 0.0)
    inv = eye - d
    p = d
    for _ in range(2):
        p = hnn(p, p)
        inv = inv + hnn(inv, p)
    for blk in (16, 32, 64):
        off = jnp.where((r // blk == c // blk) & (r // (blk // 2) != c // (blk // 2)), a, 0.0)
        inv = inv - hnn(hnn(inv, off), inv)
    return inv


@jax.custom_vjp
def _inverse_given(a, t):
    return t


_inverse_given.defvjp(lambda a, t: (t, t), lambda t, g: (-hnt(htn(t, g), t), jnp.zeros_like(t)))


def _conv_act(x, w):
    c = x[:, 5:69] * w[:, 0:1] + x[:, 6:70] * w[:, 1:2] + x[:, 7:71] * w[:, 2:3] + x[:, 8:72] * w[:, 3:4]
    return _silu(c)


def dn_chunk(xq, xk, xv, wq, wk, wv, b, g, gt, s_prev, t_saved=None):
    q = _conv_act(xq, wq)
    k = _conv_act(xk, wk)
    v = _conv_act(xv, wv)
    q = q * lax.rsqrt(jnp.sum(q * q, axis=-1, keepdims=True) + EPS) * (DH ** -0.5)
    k = k * lax.rsqrt(jnp.sum(k * k, axis=-1, keepdims=True) + EPS)
    r = _iota2((1, CH, CH), 1)
    c = _iota2((1, CH, CH), 2)
    causal = r >= c
    dec = jnp.where(causal, jnp.exp(jnp.where(causal, g - gt, 0.0)), 0.0)
    kb = k * b
    qk = bnt(jnp.concatenate([q, kb], axis=1), k)
    attn = qk[:, :CH] * dec
    a = jnp.where(r > c, qk[:, CH:] * dec, 0.0)
    tinv = _unit_lower_inverse(a) if t_saved is None else _inverse_given(a, t_saved)
    eg = jnp.exp(g)
    uw = hnn(tinv, jnp.concatenate([v * b, kb * eg], axis=2))
    g_last = g[:, CH - 1:CH]
    ws = bnn(jnp.concatenate([uw[..., DH:], q * eg], axis=1), s_prev)
    v_new = uw[..., :DH] - ws[:, :CH]
    o = ws[:, CH:] + bnn(attn, v_new)
    s_new = s_prev * jnp.exp(g_last) + btn(k * jnp.exp(g_last - g), v_new)
    return o, s_new, tinv


def s5_chunk(u, xp_re, xp_im, bb_re, bb_im, cc_re, cc_im, p0r, p0i, p1r, p1i, pir, pii, dsk):
    nb = u.shape[0]
    u2 = u.reshape(nb * CH, LANES)
    bu_re = bdot(u2, bb_re).reshape(nb, CH, 512)
    bu_im = bdot(u2, bb_im).reshape(nb, CH, 512)
    xt_re = pir * bu_re - pii * bu_im
    xt_im = pir * bu_im + pii * bu_re
    tri = jnp.broadcast_to((_iota2((1, CH, CH), 1) >= _iota2((1, CH, CH), 2)).astype(F32), (nb, CH, CH))
    cs_re = hnn(tri, xt_re)
    cs_im = hnn(tri, xt_im)
    x_re = p0r * cs_re - p0i * cs_im + p1r * xp_re - p1i * xp_im
    x_im = p0r * cs_im + p0i * cs_re + p1r * xp_im + p1i * xp_re
    y = bdot_nt(x_re.reshape(nb * CH, 512), cc_re) - bdot_nt(x_im.reshape(nb * CH, 512), cc_im) + dsk * u2
    return y.reshape(nb, CH, LANES), x_re[:, CH - 1:CH], x_im[:, CH - 1:CH]


def s5_tables(lam_re, lam_im, log_step, bre, bim, cre, cim):
    expand = (_iota2((S5G, S5N), 1) // S5P == _iota2((S5G, S5N), 0)).astype(F32)
    step = _dot_hi(jnp.exp(log_step), expand)
    lre = jnp.minimum(lam_re, -1e-4)
    lr = lre * step
    ang = lam_im * step
    mag = jnp.exp(lr)
    lb_re = mag * jnp.cos(ang)
    lb_im = mag * jnp.sin(ang)
    den = lre * lre + lam_im * lam_im
    coef_re = ((lb_re - 1.0) * lre + lb_im * lam_im) / den
    coef_im = (lb_im * lre - (lb_re - 1.0) * lam_im) / den
    bb_re = coef_re * bre - coef_im * bim
    bb_im = coef_re * bim + coef_im * bre
    j = _iota2((CH, 1), 0).astype(F32)
    e0 = jnp.exp(j * lr)
    e1 = jnp.exp((j + 1.0) * lr)
    ei = jnp.exp(-j * lr)
    mask = (_iota2((LANES, 512), 0) // S5C == _iota2((LANES, 512), 1) // S5P).astype(F32)

    def blocks(t):
        return jnp.concatenate([(jnp.tile(t[:, gb * 512:(gb + 1) * 512], (LANES // S5C, 1)) * mask)[None]
                                for gb in range(GB)], axis=0)

    return (blocks(bb_re), blocks(bb_im), blocks(cre), blocks(cim),
            e0 * jnp.cos(j * ang), e0 * jnp.sin(j * ang),
            e1 * jnp.cos((j + 1.0) * ang), e1 * jnp.sin((j + 1.0) * ang),
            ei * jnp.cos(j * ang), -ei * jnp.sin(j * ang))


def _row_specs(tiled, batch, bcast, tm, tpb):
    specs = [pl.BlockSpec((tm, a.shape[1]), lambda i: (i, 0)) for a in tiled]
    specs += [pl.BlockSpec((None,) + a.shape[1:], lambda i: (i // tpb, 0, 0)) for a in batch]
    specs += [pl.BlockSpec(a.shape, lambda i, nd=a.ndim: (0,) * nd) for a in bcast]
    return specs


def ew_call(name, fn, tiled, batch, bcast, outs, tm, seq):
    t_rows = tiled[0].shape[0]
    n_in = len(tiled) + len(batch) + len(bcast)

    def body(*refs):
        vals = [r[...].astype(F32) for r in refs[:n_in]]
        for r, o in zip(refs[n_in:], fn(*vals)):
            r[...] = o.astype(r.dtype)

    return pl.pallas_call(
        body, grid=(t_rows // tm,), in_specs=_row_specs(tiled, batch, bcast, tm, seq // tm),
        out_specs=[pl.BlockSpec((tm, w), lambda i: (i, 0)) for w, _ in outs],
        out_shape=[SDS((t_rows, w), dt) for w, dt in outs], name=name, compiler_params=_cp(1))(*tiled, *batch, *bcast)


def ew_vjp_call(name, fn, tiled, batch, bcast, cts, want, tm, seq, addend=None):
    t_rows = tiled[0].shape[0]
    tpb = seq // tm
    n_t, n_b, n_c = len(tiled), len(batch), len(bcast)
    n_in = n_t + n_b + n_c
    extra = [] if addend is None else [addend]

    def body(*refs):
        i = pl.program_id(0)
        vals = [r[...].astype(F32) for r in refs[:n_in]]
        ctv = tuple(r[...].astype(F32) for r in refs[n_in:n_in + len(cts)])
        outs = refs[n_in + len(cts) + len(extra):]
        _, vjp = jax.vjp(fn, *vals)
        grads = vjp(ctv)
        for k, (r, (idx, _)) in enumerate(zip(outs[:len(want)], want)):
            g = grads[idx]
            if k == 0 and extra:
                g = g + refs[n_in + len(cts)][...]
            r[...] = g.astype(r.dtype)
        for k in range(n_b):
            r, g = outs[len(want) + k], grads[n_t + k]

            @pl.when(i % tpb == 0)
            def _(r=r, g=g):
                r[...] = g

            @pl.when(i % tpb != 0)
            def _(r=r, g=g):
                r[...] += g
        for k in range(n_c):
            r, g = outs[len(want) + n_b + k], grads[n_t + n_b + k]

            @pl.when(i == 0)
            def _(r=r, g=g):
                r[...] = g

            @pl.when(i != 0)
            def _(r=r, g=g):
                r[...] += g

    out_specs = [pl.BlockSpec((tm, tiled[idx].shape[1]), lambda i: (i, 0)) for idx, _ in want]
    out_specs += [pl.BlockSpec((None,) + a.shape[1:], lambda i: (i // tpb, 0, 0)) for a in batch]
    out_specs += [pl.BlockSpec(a.shape, lambda i, nd=a.ndim: (0,) * nd) for a in bcast]
    out_shape = [SDS(tiled[idx].shape, dt) for idx, dt in want]
    out_shape += [SDS(a.shape, F32) for a in batch] + [SDS(a.shape, F32) for a in bcast]
    res = pl.pallas_call(
        body, grid=(t_rows // tm,),
        in_specs=_row_specs(tiled, batch, bcast, tm, tpb)
        + [pl.BlockSpec((tm, a.shape[1]), lambda i: (i, 0)) for a in list(cts) + extra],
        out_specs=out_specs, out_shape=out_shape, name=name, compiler_params=_cp(1))(*tiled, *batch, *bcast, *cts, *extra)
    return res[:len(want)], res[len(want):len(want) + n_b], res[len(want) + n_b:]


def _pick(n, cands):
    for c in cands:
        if n % c == 0:
            return c
    return n


def mm(name, pairs, nt, out_dtype):
    m = pairs[0][0].shape[0]
    n = pairs[0][1].shape[0 if nt else 1]
    k_total = sum(a.shape[1] for a, _ in pairs)
    tm = _pick(m, (1024, 512, 256, 128) if k_total <= 2048 else (512, 256, 128))
    tn = _pick(n, (512, 256, 128))
    np_ = len(pairs)

    def body(*refs):
        acc = None
        for p in range(np_):
            a, b = refs[2 * p][...], refs[2 * p + 1][...]
            t = _dot_nt(a, b) if nt else _dot(a, b)
            acc = t if acc is None else acc + t
        refs[2 * np_][...] = acc.astype(out_dtype)

    in_specs, ops = [], []
    for a, b in pairs:
        k = a.shape[1]
        in_specs.append(pl.BlockSpec((tm, k), lambda i, j: (i, 0)))
        in_specs.append(pl.BlockSpec((tn, k), lambda i, j: (j, 0)) if nt else pl.BlockSpec((k, tn), lambda i, j: (0, j)))
        ops += [a, b]
    return pl.pallas_call(
        body, grid=(m // tm, n // tn), in_specs=in_specs, out_specs=pl.BlockSpec((tm, tn), lambda i, j: (i, j)),
        out_shape=SDS((m, n), out_dtype), name=name, compiler_params=_cp(2))(*ops)


def mm_tn(name, a, b):
    t_rows, m = a.shape
    n = b.shape[1]
    tn = n if n <= 1024 else _pick(n, (1024, 512, 256, 128))
    tm = max([t for t in range(LANES, m + 1, LANES) if m % t == 0 and t * tn * 4 <= 6 * 1024 * 1024] or [m])
    tk = _pick(t_rows, (512, 256, 128, 64))

    def body(a_ref, b_ref, o_ref):
        @pl.when(pl.program_id(2) == 0)
        def _():
            o_ref[...] = jnp.zeros_like(o_ref)

        o_ref[...] += _dot_tn(a_ref[...], b_ref[...])

    return pl.pallas_call(
        body, grid=(m // tm, n // tn, t_rows // tk),
        in_specs=[pl.BlockSpec((tk, tm), lambda i, j, k: (k, i)), pl.BlockSpec((tk, tn), lambda i, j, k: (k, j))],
        out_specs=pl.BlockSpec((tm, tn), lambda i, j, k: (i, j)), out_shape=SDS((m, n), F32), name=name,
        compiler_params=_cp(3))(a, b)


def ffn_fwd(name, h, mod3, g, w1, w3, w2, seq):
    t_rows = h.shape[0]
    tm = _pick(seq, (512, 256, 128, 64))
    tf = FFN_TF
    tpb = seq // tm
    nf = FF // tf

    def body(h_ref, mod_ref, g_ref, w1_ref, w3_ref, w2_ref, ho_ref, f_ref, u_ref, acc):
        j = pl.program_id(1)

        @pl.when(j == 0)
        def _():
            u_ref[...] = normmod(h_ref[...], g_ref[...], mod_ref[1:2, :], mod_ref[0:1, :]).astype(BF16)
            acc[...] = jnp.zeros_like(acc)

        u = u_ref[...]
        a = _silu(_dot_nt(u, w1_ref[...])) * _dot_nt(u, w3_ref[...])
        acc[...] += _dot(a, w2_ref[...])

        @pl.when(j == nf - 1)
        def _():
            f_ref[...] = acc[...]
            ho_ref[...] = h_ref[...] + 0.5 * mod_ref[2:3, :] * acc[...]

    row = lambda i, j: (i, 0)
    return pl.pallas_call(
        body, grid=(t_rows // tm, nf),
        in_specs=[pl.BlockSpec((tm, D), row), pl.BlockSpec((None, 3, D), lambda i, j: (i // tpb, 0, 0)),
                  pl.BlockSpec((1, D), lambda i, j: (0, 0)), pl.BlockSpec((tf, D), lambda i, j: (j, 0)),
                  pl.BlockSpec((tf, D), lambda i, j: (j, 0)), pl.BlockSpec((tf, D), lambda i, j: (j, 0))],
        out_specs=[pl.BlockSpec((tm, D), row), pl.BlockSpec((tm, D), row), pl.BlockSpec((tm, D), row)],
        out_shape=[SDS((t_rows, D), F32), SDS((t_rows, D), F32), SDS((t_rows, D), BF16)],
        scratch_shapes=[pltpu.VMEM((tm, D), F32)], name=name, compiler_params=_cp(2))(h, mod3, g, w1, w3, w2)


def ffn_bwd(name, dho, h, f_out, u, mod3, g, w1, w3, w2, seq):
    t_rows = h.shape[0]
    tm = _pick(seq, (256, 128, 64))
    tf = FFN_TF
    tpb = seq // tm
    nf = FF // tf

    def body(dho_ref, h_ref, f_ref, u_ref, mod_ref, g_ref, w1_ref, w3_ref, w2_ref,
             dh_ref, a_ref, dh1_ref, dh3_ref, df_scr, dmod_ref, dg_ref, du_acc):
        i, j = pl.program_id(0), pl.program_id(1)

        @pl.when(j == 0)
        def _():
            df_scr[...] = (0.5 * mod_ref[2:3, :] * dho_ref[...]).astype(BF16)
            du_acc[...] = jnp.zeros_like(du_acc)

        uu = u_ref[...]
        h1 = _dot_nt(uu, w1_ref[...])
        h3 = _dot_nt(uu, w3_ref[...])
        sg = jax.nn.sigmoid(h1)
        s = h1 * sg
        da = _dot_nt(df_scr[...], w2_ref[...])
        dh3 = (da * s).astype(BF16)
        dh1 = (da * h3 * (sg * (1.0 + h1 * (1.0 - sg)))).astype(BF16)
        a_ref[...] = (s * h3).astype(BF16)
        dh1_ref[...] = dh1
        dh3_ref[...] = dh3
        du_acc[...] += _dot(dh1, w1_ref[...]) + _dot(dh3, w3_ref[...])

        @pl.when(j == nf - 1)
        def _():
            _, vjp = jax.vjp(normmod, h_ref[...], g_ref[...], mod_ref[1:2, :], mod_ref[0:1, :])
            dh_n, dg, dsc, dsh = vjp(du_acc[...])
            dh_ref[...] = dho_ref[...] + dh_n
            dgt = jnp.sum(0.5 * dho_ref[...] * f_ref[...], axis=0, keepdims=True)
            dmod = jnp.concatenate([dsh, dsc, dgt], axis=0)

            @pl.when(i % tpb == 0)
            def _():
                dmod_ref[...] = dmod

            @pl.when(i % tpb != 0)
            def _():
                dmod_ref[...] += dmod

            @pl.when(i == 0)
            def _():
                dg_ref[...] = dg

            @pl.when(i != 0)
            def _():
                dg_ref[...] += dg

    row = lambda i, j: (i, 0)
    col = lambda i, j: (i, j)
    return pl.pallas_call(
        body, grid=(t_rows // tm, nf),
        in_specs=[pl.BlockSpec((tm, D), row), pl.BlockSpec((tm, D), row), pl.BlockSpec((tm, D), row),
                  pl.BlockSpec((tm, D), row), pl.BlockSpec((None, 3, D), lambda i, j: (i // tpb, 0, 0)),
                  pl.BlockSpec((1, D), lambda i, j: (0, 0)), pl.BlockSpec((tf, D), lambda i, j: (j, 0)),
                  pl.BlockSpec((tf, D), lambda i, j: (j, 0)), pl.BlockSpec((tf, D), lambda i, j: (j, 0))],
        out_specs=[pl.BlockSpec((tm, D), row), pl.BlockSpec((tm, tf), col), pl.BlockSpec((tm, tf), col),
                   pl.BlockSpec((tm, tf), col), pl.BlockSpec((tm, D), row),
                   pl.BlockSpec((None, 3, D), lambda i, j: (i // tpb, 0, 0)), pl.BlockSpec((1, D), lambda i, j: (0, 0))],
        out_shape=[SDS((t_rows, D), F32), SDS((t_rows, FF), BF16), SDS((t_rows, FF), BF16), SDS((t_rows, FF), BF16),
                   SDS((t_rows, D), BF16), SDS(mod3.shape, F32), SDS((1, D), F32)],
        scratch_shapes=[pltpu.VMEM((tm, D), F32)], name=name,
        compiler_params=_cp(2))(dho, h, f_out, u, mod3, g, w1, w3, w2)


def _dn_cols(part, hd):
    return slice(part * DNW + hd * DH, part * DNW + (hd + 1) * DH)


def _dn_stacks(raw_ref, halo_ref, conv_ref, hm, nb):
    pairs = [(b, hd) for b in range(nb) for hd in range(NH)]
    xs = [jnp.stack([jnp.concatenate([halo_ref[b, :, _dn_cols(part, hd)] * hm, raw_ref[b, :, _dn_cols(part, hd)]], axis=0)
                     for b, hd in pairs]) for part in range(3)]
    ws = [jnp.stack([conv_ref[0:CONVW, _dn_cols(part, hd)] for b, hd in pairs]) for part in range(3)]
    return xs, ws


def _gate_stacks(gates, nb):
    pairs = [(b, hd) for b in range(nb) for hd in range(NH)]
    bs = jnp.stack([gates[b][0][:, hd:hd + 1] for b, hd in pairs])
    gs = jnp.stack([gates[b][1][:, NH + hd:NH + hd + 1] for b, hd in pairs])
    gts = jnp.stack([gates[b][2][NH + hd:NH + hd + 1, :] for b, hd in pairs])
    return bs, gs, gts


def deltanet_fwd(p_dn, p_small, conv8, alp, dtp, nb):
    bl, seq, _ = p_dn.shape
    nc = seq // CH
    ng = nb * NH

    def body(raw_ref, halo_ref, small_ref, conv_ref, alp_ref, dtp_ref, o_ref, sprev_ref, tinv_ref, s_scr):
        n = pl.program_id(1)

        @pl.when(n == 0)
        def _():
            s_scr[...] = jnp.zeros_like(s_scr)

        hm = (n > 0).astype(F32)
        gates = [gate_fn(small_ref[b], alp_ref[...], dtp_ref[...]) for b in range(nb)]
        xs, ws = _dn_stacks(raw_ref, halo_ref, conv_ref, hm, nb)
        s_prev = s_scr[...]
        o, s_new, tinv = dn_chunk(*xs, *ws, *_gate_stacks(gates, nb), s_prev)
        sprev_ref[...] = s_prev
        tinv_ref[...] = tinv
        s_scr[...] = s_new
        for b in range(nb):
            for hd in range(NH):
                o_ref[b, :, hd * DH:(hd + 1) * DH] = o[b * NH + hd]

    blk = lambda bb, n: (bb, n, 0)
    const = lambda bb, n: (0, 0)
    saved = pl.BlockSpec((None, ng, DH, DH), lambda bb, n: (bb * nc + n, 0, 0, 0))
    return pl.pallas_call(
        body, grid=(bl // nb, nc),
        in_specs=[pl.BlockSpec((nb, CH, 4 * DNW), blk),
                  pl.BlockSpec((nb, 8, 3 * DNW), lambda bb, n: (bb, jnp.maximum(n * (CH // 8) - 1, 0), 0)),
                  pl.BlockSpec((nb, CH, LANES), blk), pl.BlockSpec((8, 3 * DNW), const), pl.BlockSpec((1, LANES), const),
                  pl.BlockSpec((1, LANES), const)],
        out_specs=[pl.BlockSpec((nb, CH, DNW), blk), saved, saved],
        out_shape=[SDS((bl, seq, DNW), F32), SDS((bl // nb * nc, ng, DH, DH), F32), SDS((bl // nb * nc, ng, DH, DH), F32)],
        scratch_shapes=[pltpu.VMEM((ng, DH, DH), F32)], name="deltanet_fwd",
        compiler_params=_cp(2))(p_dn, p_dn, p_small, conv8, alp, dtp)


def deltanet_bwd(p_dn, p_small, conv8, alp, dtp, sprev, tinv, d_o, d_z, nb):
    bl, seq, _ = p_dn.shape
    nc = seq // CH
    ng = nb * NH

    def body(raw_ref, halo_ref, small_ref, conv_ref, alp_ref, dtp_ref, sprev_ref, tinv_ref, do_ref, dz_ref,
             draw_ref, dsmall_ref, dconv_ref, dalp_ref, ddtp_ref, ds_scr, dhalo_scr):
        bb, r = pl.program_id(0), pl.program_id(1)
        n = nc - 1 - r

        @pl.when((bb == 0) & (r == 0))
        def _():
            dconv_ref[...] = jnp.zeros_like(dconv_ref)
            dalp_ref[...] = jnp.zeros_like(dalp_ref)
            ddtp_ref[...] = jnp.zeros_like(ddtp_ref)

        @pl.when(r == 0)
        def _():
            ds_scr[...] = jnp.zeros_like(ds_scr)
            dhalo_scr[...] = jnp.zeros_like(dhalo_scr)

        hm = (n > 0).astype(F32)
        gates, gate_vjps = [], []
        for b in range(nb):
            out, gvjp = jax.vjp(gate_fn, small_ref[b], alp_ref[...], dtp_ref[...])
            gates.append(out)
            gate_vjps.append(gvjp)
        xs, ws = _dn_stacks(raw_ref, halo_ref, conv_ref, hm, nb)
        t_saved = tinv_ref[...]
        _, vjp = jax.vjp(lambda *args: dn_chunk(*args, t_saved)[:2], *xs, *ws, *_gate_stacks(gates, nb), sprev_ref[...])
        d_out = jnp.stack([do_ref[b, :, hd * DH:(hd + 1) * DH] for b in range(nb) for hd in range(NH)])
        grads = vjp((d_out, ds_scr[...]))
        ds_scr[...] = grads[9]
        lane = _iota2((CH, LANES), 1)
        rowi = _iota2((LANES, CH), 0)
        for b in range(nb):
            d_beta = jnp.zeros((CH, LANES), F32)
            d_gc = jnp.zeros((CH, LANES), F32)
            d_gct = jnp.zeros((LANES, CH), F32)
            for hd in range(NH):
                i = b * NH + hd
                for part in range(3):
                    cols = _dn_cols(part, hd)
                    dx = grads[part][i]
                    tail = dx[CH:CH + 8] + dhalo_scr[b, :, cols]
                    draw_ref[b, :, cols] = jnp.concatenate([dx[8:CH], tail], axis=0).astype(BF16)
                    dhalo_scr[b, :, cols] = dx[0:8] * hm
                d_beta = d_beta + jnp.where(lane == hd, grads[6][i], 0.0)
                d_gc = d_gc + jnp.where(lane == NH + hd, grads[7][i], 0.0)
                d_gct = d_gct + jnp.where(rowi == NH + hd, grads[8][i], 0.0)
            d_small, d_alp, d_dtp = gate_vjps[b]((d_beta, d_gc, d_gct))
            dsmall_ref[b] = d_small.astype(BF16)
            dalp_ref[...] += d_alp
            ddtp_ref[...] += d_dtp
            draw_ref[b, :, 3 * DNW:4 * DNW] = dz_ref[b].astype(BF16)
        for hd in range(NH):
            for part in range(3):
                dw = grads[3 + part][hd]
                for b in range(1, nb):
                    dw = dw + grads[3 + part][b * NH + hd]
                dconv_ref[0:CONVW, _dn_cols(part, hd)] += dw

    blk = lambda bb, r: (bb, nc - 1 - r, 0)
    const = lambda bb, r: (0, 0)
    saved = pl.BlockSpec((None, ng, DH, DH), lambda bb, r: (bb * nc + nc - 1 - r, 0, 0, 0))
    return pl.pallas_call(
        body, grid=(bl // nb, nc),
        in_specs=[pl.BlockSpec((nb, CH, 4 * DNW), blk),
                  pl.BlockSpec((nb, 8, 3 * DNW), lambda bb, r: (bb, jnp.maximum((nc - 1 - r) * (CH // 8) - 1, 0), 0)),
                  pl.BlockSpec((nb, CH, LANES), blk), pl.BlockSpec((8, 3 * DNW), const), pl.BlockSpec((1, LANES), const),
                  pl.BlockSpec((1, LANES), const), saved, saved,
                  pl.BlockSpec((nb, CH, DNW), blk), pl.BlockSpec((nb, CH, DNW), blk)],
        out_specs=[pl.BlockSpec((nb, CH, 4 * DNW), blk), pl.BlockSpec((nb, CH, LANES), blk),
                   pl.BlockSpec((8, 3 * DNW), const), pl.BlockSpec((1, LANES), const), pl.BlockSpec((1, LANES), const)],
        out_shape=[SDS((bl, seq, 4 * DNW), BF16), SDS((bl, seq, LANES), BF16), SDS((8, 3 * DNW), F32), SDS((1, LANES), F32),
                   SDS((1, LANES), F32)],
        scratch_shapes=[pltpu.VMEM((ng, DH, DH), F32), pltpu.VMEM((nb, 8, 3 * DNW), F32)], name="deltanet_bwd",
        compiler_params=_cp(2))(p_dn, p_dn, p_small, conv8, alp, dtp, sprev, tinv, d_o, d_z)


def _s5_table_specs():
    tab3 = pl.BlockSpec((None, LANES, 512), lambda gb, n: (gb, 0, 0))
    tab2 = pl.BlockSpec((CH, 512), lambda gb, n: (0, gb))
    return [tab3] * 4 + [tab2] * 6 + [pl.BlockSpec((1, LANES), lambda gb, n: (0, gb))]


def s5_fwd(u, tables, dsk):
    bl, seq, _ = u.shape
    nc = seq // CH

    def body(u_ref, *rest):
        tabs, (y_ref, xs_ref, xr_scr, xi_scr) = rest[:11], rest[11:]

        @pl.when(pl.program_id(1) == 0)
        def _():
            xr_scr[...] = jnp.zeros_like(xr_scr)
            xi_scr[...] = jnp.zeros_like(xi_scr)

        xp_re, xp_im = xr_scr[...], xi_scr[...]
        xs_ref[0:bl] = xp_re
        xs_ref[bl:2 * bl] = xp_im
        y, xn_re, xn_im = s5_chunk(u_ref[...], xp_re, xp_im, *[t[...] for t in tabs])
        y_ref[...] = y
        xr_scr[...] = xn_re
        xi_scr[...] = xn_im

    blk = lambda gb, n: (0, n, gb)
    return pl.pallas_call(
        body, grid=(GB, nc), in_specs=[pl.BlockSpec((bl, CH, LANES), blk)] + _s5_table_specs(),
        out_specs=[pl.BlockSpec((bl, CH, LANES), blk),
                   pl.BlockSpec((None, 2 * bl, 1, 512), lambda gb, n: (gb * nc + n, 0, 0, 0))],
        out_shape=[SDS((bl, seq, S5W), F32), SDS((GB * nc, 2 * bl, 1, 512), F32)],
        scratch_shapes=[pltpu.VMEM((bl, 1, 512), F32), pltpu.VMEM((bl, 1, 512), F32)], name="s5_fwd",
        compiler_params=_cp(2))(u, *tables, dsk)


def s5_bwd(u, tables, dsk, xs, dy):
    bl, seq, _ = u.shape
    nc = seq // CH

    def body(u_ref, *rest):
        tabs, xs_ref, dy_ref = rest[:11], rest[11], rest[12]
        du_ref, dtabs, dxr_scr, dxi_scr = rest[13], rest[14:25], rest[25], rest[26]
        r = pl.program_id(1)

        @pl.when(r == 0)
        def _():
            for t in dtabs:
                t[...] = jnp.zeros_like(t)
            dxr_scr[...] = jnp.zeros_like(dxr_scr)
            dxi_scr[...] = jnp.zeros_like(dxi_scr)

        _, vjp = jax.vjp(s5_chunk, u_ref[...], xs_ref[0:bl], xs_ref[bl:2 * bl], *[t[...] for t in tabs])
        grads = vjp((dy_ref[...], dxr_scr[...], dxi_scr[...]))
        du_ref[...] = grads[0].astype(BF16)
        dxr_scr[...] = grads[1]
        dxi_scr[...] = grads[2]
        for t, g in zip(dtabs, grads[3:]):
            t[...] += g

    blk = lambda gb, r: (0, nc - 1 - r, gb)
    tab_shapes = [SDS(t.shape, F32) for t in tables] + [SDS(dsk.shape, F32)]
    return pl.pallas_call(
        body, grid=(GB, nc),
        in_specs=[pl.BlockSpec((bl, CH, LANES), blk)] + _s5_table_specs()
        + [pl.BlockSpec((None, 2 * bl, 1, 512), lambda gb, r: (gb * nc + nc - 1 - r, 0, 0, 0)), pl.BlockSpec((bl, CH, LANES), blk)],
        out_specs=[pl.BlockSpec((bl, CH, LANES), blk)] + _s5_table_specs(),
        out_shape=[SDS((bl, seq, S5W), BF16)] + tab_shapes,
        scratch_shapes=[pltpu.VMEM((bl, 1, 512), F32), pltpu.VMEM((bl, 1, 512), F32)], name="s5_bwd",
        compiler_params=_cp(2))(u, *tables, dsk, xs, dy)


def s5_tables_fwd(params):
    shapes = [SDS((GB, LANES, 512), F32)] * 4 + [SDS((CH, S5N), F32)] * 6

    def body(*refs):
        for r, t in zip(refs[7:], s5_tables(*[p[...] for p in refs[:7]])):
            r[...] = t

    return pl.pallas_call(body, out_shape=shapes, name="s5_tables_fwd", compiler_params=_cp())(*params)


def s5_tables_bwd(params, dtables):
    def body(*refs):
        _, vjp = jax.vjp(s5_tables, *[p[...] for p in refs[:7]])
        for r, g in zip(refs[17:], vjp(tuple(t[...] for t in refs[7:17]))):
            r[...] = g

    return pl.pallas_call(body, out_shape=[SDS(p.shape, F32) for p in params], name="s5_tables_bwd",
                          compiler_params=_cp())(*params, *dtables)


def ada_fwd(c_all, w_loc, b_loc):
    def body(c_ref, w_ref, b_ref, o_ref):
        o_ref[...] = _dot(_silu(c_ref[...]), w_ref[...]) + b_ref[...]

    return pl.pallas_call(body, out_shape=SDS((c_all.shape[0], w_loc.shape[1]), F32), name="ada_fwd",
                          compiler_params=_cp())(c_all, w_loc, b_loc)


def ada_bwd(c_all, dmod_mine, dmod_all):
    def body(c_ref, dm_ref, da_ref, gw_ref, gb_ref):
        gw_ref[...] = _dot_tn(_silu(c_ref[...]), dm_ref[...])
        gb_ref[...] = jnp.sum(da_ref[...], axis=0, keepdims=True)

    return pl.pallas_call(body, out_shape=[SDS((D, dmod_mine.shape[1]), F32), SDS((1, dmod_all.shape[1]), F32)],
                          name="ada_bwd", compiler_params=_cp())(c_all, dmod_mine, dmod_all)


def loss_head(h, tgt, g, seq):
    t_rows = h.shape[0]
    tm = _pick(seq, (256, 128, 64))

    def body(h_ref, t_ref, g_ref, dh_ref, dg_ref, loss_ref):
        i = pl.program_id(0)
        y, vjp = jax.vjp(lambda hh, gg: hh * lax.rsqrt(jnp.mean(hh * hh, axis=-1, keepdims=True) + EPS) * gg,
                         h_ref[...], g_ref[...])
        e = y - t_ref[...]
        dh, dg = vjp(e * (1.0 / D))
        part = jnp.sum(jnp.sum(e * e, axis=1, keepdims=True), axis=0, keepdims=True) * (0.5 / D) + jnp.zeros((1, LANES), F32)
        dh_ref[...] = dh

        @pl.when(i == 0)
        def _():
            dg_ref[...] = dg
            loss_ref[...] = part

        @pl.when(i != 0)
        def _():
            dg_ref[...] += dg
            loss_ref[...] += part

    row = lambda i: (i, 0)
    const = lambda i: (0, 0)
    return pl.pallas_call(
        body, grid=(t_rows // tm,),
        in_specs=[pl.BlockSpec((tm, D), row), pl.BlockSpec((tm, D), row), pl.BlockSpec((1, D), const)],
        out_specs=[pl.BlockSpec((tm, D), row), pl.BlockSpec((1, D), const), pl.BlockSpec((1, LANES), const)],
        out_shape=[SDS((t_rows, D), F32), SDS((1, D), F32), SDS((1, LANES), F32)], name="loss_head",
        compiler_params=_cp(1))(h, tgt, g)


def adamw(name, parts, w, m, v):
    k_parts, rows, cols = parts.shape
    tr = _pick(rows, (256, 128, 64, 32, 16, 8))

    def body(p_ref, w_ref, m_ref, v_ref, g_ref, d_ref, mo_ref, vo_ref):
        g = p_ref[0].astype(F32)
        for k in range(1, k_parts):
            g = g + p_ref[k].astype(F32)
        _adam_store(g, w_ref, m_ref, v_ref, g_ref, d_ref, mo_ref, vo_ref)

    blk = pl.BlockSpec((tr, cols), lambda i: (i, 0))
    return pl.pallas_call(
        body, grid=(rows // tr,), in_specs=[pl.BlockSpec((k_parts, tr, cols), lambda i: (0, i, 0)), blk, blk, blk],
        out_specs=[blk] * 4, out_shape=[SDS((rows, cols), F32)] * 4, name=name, compiler_params=_cp(1))(parts, w, m, v)


def _adam_store(g, w_ref, m_ref, v_ref, g_ref, d_ref, mo_ref, vo_ref):
    m_new = ADAM_B1 * m_ref[...] + (1.0 - ADAM_B1) * g
    v_new = ADAM_B2 * v_ref[...] + (1.0 - ADAM_B2) * (g * g)
    m_hat = m_new / (1.0 - ADAM_B1 ** ADAM_STEP)
    v_hat = v_new / (1.0 - ADAM_B2 ** ADAM_STEP)
    g_ref[...] = g
    d_ref[...] = -ADAM_LR * (m_hat / (jnp.sqrt(v_hat) + ADAM_EPS) + ADAM_WD * w_ref[...])
    mo_ref[...] = m_new
    vo_ref[...] = v_new


def adamw_t(name, parts, w, m, v):
    k_parts, r, c = parts.shape
    tc = _pick(c, (256, 128))

    def body(p_ref, w_ref, m_ref, v_ref, g_ref, d_ref, mo_ref, vo_ref):
        gt = p_ref[0].astype(F32)
        for k in range(1, k_parts):
            gt = gt + p_ref[k].astype(F32)
        _adam_store(gt.T, w_ref, m_ref, v_ref, g_ref, d_ref, mo_ref, vo_ref)

    blk = pl.BlockSpec((tc, r), lambda j: (j, 0))
    return pl.pallas_call(
        body, grid=(c // tc,), in_specs=[pl.BlockSpec((k_parts, r, tc), lambda j: (0, 0, j)), blk, blk, blk],
        out_specs=[blk] * 4, out_shape=[SDS((c, r), F32)] * 4, name=name, compiler_params=_cp(1))(parts, w, m, v)


def all_gather(name, x):
    rows, cols = x.shape

    def body(x_ref, out_ref, send_sems, recv_sems, local_sem):
        mx, my, mc = lax.axis_index("x"), lax.axis_index("y"), lax.axis_index("c")
        me, sibling = (mx, my, mc), (mx, my, 1 - mc)
        chips = [(1 - mx, my), (mx, 1 - my), (1 - mx, 1 - my)]

        def slot(px, py, pc):
            return out_ref.at[4 * px + 2 * py + pc]

        def copy(k, block, to, src=None):
            return pltpu.make_async_remote_copy(
                src_ref=slot(*block) if src is None else src, dst_ref=slot(*block), send_sem=send_sems.at[k],
                recv_sem=recv_sems.at[k], device_id=to, device_id_type=pl.DeviceIdType.MESH)

        mine = pltpu.make_async_copy(x_ref, slot(*me), local_sem)
        mine.start()
        first = [copy(0, me, sibling, src=x_ref)]
        first += [copy(1 + j, me, (*chip, mc), src=x_ref) for j, chip in enumerate(chips)]
        for cp in first:
            cp.start()
        passed = [copy(4 + j, (*chip, mc), sibling) for j, chip in enumerate(chips)]
        for j, chip in enumerate(chips):
            copy(1 + j, (*chip, mc), me).wait_recv()
            passed[j].start()
        copy(0, sibling, me).wait_recv()
        for j, chip in enumerate(chips):
            copy(4 + j, (*chip, 1 - mc), me).wait_recv()
        for cp in first + passed:
            cp.wait_send()
        mine.wait()

    return pl.pallas_call(
        body, out_shape=SDS((NDEV, rows, cols), x.dtype), in_specs=[pl.BlockSpec(memory_space=pl.ANY)],
        out_specs=pl.BlockSpec(memory_space=pl.ANY),
        scratch_shapes=[pltpu.SemaphoreType.DMA((7,)), pltpu.SemaphoreType.DMA((7,)), pltpu.SemaphoreType.DMA],
        name=name)(x)


def all_to_all(name, x):
    def body(x_ref, out_ref, send_sems, recv_sems, local_sem):
        mx, my, mc = lax.axis_index("x"), lax.axis_index("y"), lax.axis_index("c")
        me = 4 * mx + 2 * my + mc
        mine = pltpu.make_async_copy(x_ref.at[me], out_ref.at[me], local_sem)
        mine.start()
        copies = []
        for k in range(1, NDEV):
            px, py, pc = mx ^ (k >> 2), my ^ ((k >> 1) & 1), mc ^ (k & 1)
            copies.append(pltpu.make_async_remote_copy(
                src_ref=x_ref.at[4 * px + 2 * py + pc], dst_ref=out_ref.at[me], send_sem=send_sems.at[k - 1],
                recv_sem=recv_sems.at[k - 1], device_id=(px, py, pc), device_id_type=pl.DeviceIdType.MESH))
        for cp in copies:
            cp.start()
        for k in range(1, NDEV):
            px, py, pc = mx ^ (k >> 2), my ^ ((k >> 1) & 1), mc ^ (k & 1)
            pltpu.make_async_remote_copy(
                src_ref=x_ref.at[me], dst_ref=out_ref.at[4 * px + 2 * py + pc], send_sem=send_sems.at[k - 1],
                recv_sem=recv_sems.at[k - 1], device_id=(px, py, pc), device_id_type=pl.DeviceIdType.MESH).wait_recv()
        for cp in copies:
            cp.wait_send()
        mine.wait()

    return pl.pallas_call(
        body, out_shape=SDS(x.shape, x.dtype), in_specs=[pl.BlockSpec(memory_space=pl.ANY)],
        out_specs=pl.BlockSpec(memory_space=pl.ANY),
        scratch_shapes=[pltpu.SemaphoreType.DMA((7,)), pltpu.SemaphoreType.DMA((7,)), pltpu.SemaphoreType.DMA],
        name=name)(x)


def _pack(arrs, dtype, row_mult=8):
    segs = []
    for a in arrs:
        flat = a.reshape(-1).astype(dtype)
        segs.append(jnp.pad(flat, (0, (-flat.shape[0]) % ROW)))
    flat = jnp.concatenate(segs)
    flat = jnp.pad(flat, (0, (-flat.shape[0]) % (ROW * row_mult)))
    return flat.reshape(-1, ROW)


def _unpack(buf, shapes):
    flat = buf.reshape(-1)
    out, off = [], 0
    for s in shapes:
        n = math.prod(s)
        out.append(flat[off:off + n].reshape(s))
        off += n + (-n) % ROW
    return out


def _pack_rows(arrs, axis):
    padded = []
    for t in arrs:
        pad = [(0, 0)] * t.ndim
        pad[axis] = (0, _tile_rows(t.shape[axis]) - t.shape[axis])
        padded.append(jnp.pad(t, pad))
    return jnp.concatenate(padded, axis=axis)


def _tile_rows(r):
    return r + (-r) % 16


def _unpack8(buf, shapes):
    flat = buf.reshape(NDEV, -1)
    out, off = [], 0
    for s in shapes:
        n = math.prod(s)
        out.append(flat[:, off:off + n].reshape((NDEV,) + tuple(s)))
        off += n + (-n) % ROW
    return out


def kernel(x, c, w_ada, b_ada, g_ffn1, w1_ffn1, w3_ffn1, w2_ffn1, g_mix, w_in, conv_qkv, a_log, dt_bias, g_onorm, lam_re, lam_im, log_step, b_re, b_im, c_re, c_im, d_skip, w_glu, b_glu, w_proj_a, w_proj_b, w_out, g_ffn2, w1_ffn2, w3_ffn2, w2_ffn2, g_final, loss_target, m_w_ada, m_b_ada, m_g_ffn1, m_w1_ffn1, m_w3_ffn1, m_w2_ffn1, m_g_mix, m_w_in, m_conv_qkv, m_a_log, m_dt_bias, m_g_onorm, m_lam_re, m_lam_im, m_log_step, m_b_re, m_b_im, m_c_re, m_c_im, m_d_skip, m_w_glu, m_b_glu, m_w_proj_a, m_w_proj_b, m_w_out, m_g_ffn2, m_w1_ffn2, m_w3_ffn2, m_w2_ffn2, m_g_final, v_w_ada, v_b_ada, v_g_ffn1, v_w1_ffn1, v_w3_ffn1, v_w2_ffn1, v_g_mix, v_w_in, v_conv_qkv, v_a_log, v_dt_bias, v_g_onorm, v_lam_re, v_lam_im, v_log_step, v_b_re, v_b_im, v_c_re, v_c_im, v_d_skip, v_w_glu, v_b_glu, v_w_proj_a, v_w_proj_b, v_w_out, v_g_ffn2, v_w1_ffn2, v_w3_ffn2, v_w2_ffn2, v_g_final):
    a = dict(locals())
    bl, seq, _ = x.shape
    t_rows = bl * seq
    nc = seq // CH
    me = 4 * lax.axis_index("x") + 2 * lax.axis_index("y") + lax.axis_index("c")
    tm_ew = _pick(seq, (256, 128, 64))

    sm = all_gather("gather_small", _pack([c, conv_qkv[0]], F32))
    c_loc, conv_loc = _unpack8(sm, [c.shape, conv_qkv.shape[1:]])
    c_all = c_loc.reshape(NDEV * bl, D)
    conv_full = conv_loc.transpose(1, 0, 2).reshape(CONVW, 3 * DNW)
    loc = {n: (a[n][0].T if n in COL_SHARDED else a[n][0]) for n in RS_WEIGHTS}
    rs_rows = [loc[n].size // ROW for n in RS_WEIGHTS]
    wg = all_gather("gather_weights", _pack_rows([loc[n].astype(BF16).reshape(-1, ROW) for n in RS_WEIGHTS], 0))
    wfull, r0 = {}, 0
    for n, r in zip(RS_WEIGHTS, rs_rows):
        wfull[n] = wg[:, r0:r0 + r, :].reshape(-1, loc[n].shape[1])
        r0 += _tile_rows(r)
    win = wfull['w_in']
    o_small, o_s5, o_gate = 4 * DNW, 4 * DNW + 2 * NH, 4 * DNW + 2 * NH + S5W
    w_dn, w_small = win[:o_small], jnp.pad(win[o_small:o_s5], ((0, LANES - 2 * NH), (0, 0)))
    w_s5, w_gate = win[o_s5:o_gate], win[o_gate:]

    n_ada = w_ada.shape[2]
    mod_part = ada_fwd(c_all, w_ada[0], lax.dynamic_slice(b_ada, (0, me * n_ada), (1, n_ada)))
    mod_all = all_gather("gather_mod", mod_part).transpose(1, 0, 2).reshape(NDEV * bl, 9 * D)
    mod = lax.dynamic_slice(mod_all, (me * bl, 0), (bl, 9 * D)).reshape(bl, 9, D)
    mods = [mod[:, k:k + 1, :] for k in range(9)]

    h0 = x.reshape(t_rows, D)
    h1, f1, u1 = ffn_fwd("ffn1_fwd", h0, mod[:, 0:3, :], g_ffn1, wfull['w1_ffn1'], wfull['w3_ffn1'], wfull['w2_ffn1'], seq)
    (u2,) = ew_call("mix_norm", fn_normmod, [h1], [mods[3], mods[4]], [g_mix], [(D, BF16)], tm_ew, seq)
    p_dn = mm("proj_dn", [(u2, w_dn)], True, F32)
    p_small = mm("proj_small", [(u2, w_small)], True, F32)
    p_s5 = mm("proj_s5", [(u2, w_s5)], True, F32)
    p_gate = mm("proj_gate", [(u2, w_gate)], True, F32)

    conv8 = jnp.pad(conv_full, ((0, 8 - CONVW), (0, 0)))
    alp = jnp.pad(a_log, ((0, 0), (NH, LANES - 2 * NH)))
    dtp = jnp.pad(dt_bias, ((0, 0), (NH, LANES - 2 * NH)))
    nb_dn = DN_ROWS if bl % DN_ROWS == 0 else 1
    p_dn3, p_small3 = p_dn.reshape(bl, seq, 4 * DNW), p_small.reshape(bl, seq, LANES)
    o_pre3, sprev, tinv = deltanet_fwd(p_dn3, p_small3, conv8, alp, dtp, nb_dn)
    o_pre = o_pre3.reshape(t_rows, DNW)
    z_raw = p_dn[:, 3 * DNW:]
    (oa,) = ew_call("dn_onorm", fn_onorm, [o_pre, z_raw], [], [g_onorm], [(DNW, BF16)], tm_ew, seq)
    ya = mm("proj_a", [(oa, wfull['w_proj_a'])], True, F32)

    s5_params = [lam_re.reshape(1, S5N), lam_im.reshape(1, S5N), log_step,
                 b_re[0].transpose(2, 0, 1).reshape(S5C, S5N), b_im[0].transpose(2, 0, 1).reshape(S5C, S5N),
                 c_re[0].transpose(1, 0, 2).reshape(S5C, S5N), c_im[0].transpose(1, 0, 2).reshape(S5C, S5N)]
    tables = s5_tables_fwd(s5_params)
    p_s53 = p_s5.reshape(bl, seq, S5W)
    y_s53, xs = s5_fwd(p_s53, tables, d_skip)
    y_s5 = y_s53.reshape(t_rows, S5W)
    (ob,) = ew_call("s5_glu", fn_glu, [y_s5], [], [wfull['w_glu'], b_glu], [(S5W, BF16)], tm_ew, seq)
    yb = mm("proj_b", [(ob, wfull['w_proj_b'])], True, F32)

    (merged,) = ew_call("merge", fn_merge, [p_gate, ya, yb], [], [], [(D, BF16)], tm_ew, seq)
    mo = mm("proj_out", [(merged, wfull['w_out'])], False, F32)
    (h2,) = ew_call("mix_resid", lambda p, q, gt: (q + gt * p,), [mo, h1], [mods[5]], [], [(D, F32)], tm_ew, seq)
    h3, f3, u3 = ffn_fwd("ffn2_fwd", h2, mod[:, 6:9, :], g_ffn2, wfull['w1_ffn2'], wfull['w3_ffn2'], wfull['w2_ffn2'], seq)

    dh3, dg_final, loss_part = loss_head(h3, loss_target.reshape(t_rows, D), g_final.reshape(1, D), seq)
    loss = lax.psum(loss_part[0, 0], ("x", "y", "c"))

    gw = {}
    dh2, a3, d1_3, d3_3, df3, dmod_c, dg_ffn2 = ffn_bwd("ffn2_bwd", dh3, h2, f3, u3, mod[:, 6:9, :], g_ffn2, wfull['w1_ffn2'],
                                                   wfull['w3_ffn2'], wfull['w2_ffn2'], seq)
    gw['w1_ffn2'] = mm_tn("gw1_ffn2", d1_3, u3)
    gw['w3_ffn2'] = mm_tn("gw3_ffn2", d3_3, u3)
    gw['w2_ffn2'] = mm_tn("gw2_ffn2", a3, df3)

    (dmo,), (dgt2,), _ = ew_vjp_call("mix_resid_bwd", fn_resid, [mo], [mods[5]], [], [dh2], [(0, BF16)], tm_ew, seq)
    gw['w_out'] = mm_tn("gw_out", merged, dmo)
    d_merged = mm("d_merged", [(dmo, wfull['w_out'])], True, F32)
    (d_gate, d_ya, d_yb), _, _ = ew_vjp_call("merge_bwd", fn_merge, [p_gate, ya, yb], [], [], [d_merged],
                                             [(0, BF16), (1, BF16), (2, BF16)], tm_ew, seq)
    gw['w_proj_a'] = mm_tn("gw_proj_a", d_ya, oa)
    gw['w_proj_b'] = mm_tn("gw_proj_b", d_yb, ob)
    d_oa = mm("d_oa", [(d_ya, wfull['w_proj_a'])], False, F32)
    d_ob = mm("d_ob", [(d_yb, wfull['w_proj_b'])], False, F32)

    (d_opre, d_z), _, (dg_onorm,) = ew_vjp_call("dn_onorm_bwd", fn_onorm, [o_pre, z_raw], [], [g_onorm], [d_oa],
                                                [(0, F32), (1, F32)], tm_ew, seq)
    d_pdn3, d_psmall3, d_conv8, d_alp, d_dtp = deltanet_bwd(
        p_dn3, p_small3, conv8, alp, dtp, sprev, tinv, d_opre.reshape(bl, seq, DNW), d_z.reshape(bl, seq, DNW), nb_dn)
    d_pdn, d_psmall = d_pdn3.reshape(t_rows, 4 * DNW), d_psmall3.reshape(t_rows, LANES)

    (d_ys5,), _, (g_wglu, dg_bglu) = ew_vjp_call("s5_glu_bwd", fn_glu, [y_s5], [], [wfull['w_glu'], b_glu], [d_ob],
                                                 [(0, F32)], tm_ew, seq)
    gw['w_glu'] = g_wglu
    s5_out = s5_bwd(p_s53, tables, d_skip, xs, d_ys5.reshape(bl, seq, S5W))
    d_ps5, d_tables, dg_dskip = s5_out[0].reshape(t_rows, S5W), s5_out[1:11], s5_out[11]
    d_s5p = s5_tables_bwd(s5_params, d_tables)

    d_pdn_b, d_psm_b, d_ps5_b = d_pdn, d_psmall, d_ps5
    gw['w_in'] = jnp.concatenate([mm_tn("gw_dn", d_pdn_b, u2), mm_tn("gw_small", d_psm_b, u2)[:2 * NH],
                                  mm_tn("gw_s5", d_ps5_b, u2), mm_tn("gw_gate", d_gate, u2)], axis=0)
    du2 = mm("d_u2", [(d_pdn_b, w_dn), (d_psm_b, w_small), (d_ps5_b, w_s5), (d_gate, w_gate)], False, F32)
    (dh1,), (dsh2, dsc2), (dg_mix,) = ew_vjp_call("mix_norm_bwd", fn_normmod, [h1], [mods[3], mods[4]], [g_mix], [du2],
                                                  [(0, F32)], tm_ew, seq, addend=dh2)

    dh0, a1, d1_1, d3_1, df1, dmod_a, dg_ffn1 = ffn_bwd("ffn1_bwd", dh1, h0, f1, u1, mod[:, 0:3, :], g_ffn1, wfull['w1_ffn1'],
                                                   wfull['w3_ffn1'], wfull['w2_ffn1'], seq)
    gw['w1_ffn1'] = mm_tn("gw1_ffn1", d1_1, u1)
    gw['w3_ffn1'] = mm_tn("gw3_ffn1", d3_1, u1)
    gw['w2_ffn1'] = mm_tn("gw2_ffn1", a1, df1)

    rs_in = _pack_rows([gw[n].astype(BF16).reshape(NDEV, -1, ROW) for n in RS_WEIGHTS], 1)
    rs_out = all_to_all("scatter_grads", rs_in)
    res, r0 = {}, 0
    for n, r in zip(RS_WEIGHTS, rs_rows):
        parts = rs_out[:, r0:r0 + r, :].reshape((NDEV,) + loc[n].shape)
        r0 += _tile_rows(r)
        update = adamw_t if n in COL_SHARDED else adamw
        out = update("adamw_" + n, parts, a[n][0], a["m_" + n][0], a["v_" + n][0])
        for kind, t in zip(("grad", "delta", "new_m", "new_v"), out):
            res[kind + "_" + n] = t[None]

    dmod_mine = jnp.concatenate([dmod_a, dsh2, dsc2, dgt2, dmod_c], axis=1).reshape(bl, 9 * D)
    small_grads = {
        'g_ffn1': dg_ffn1, 'g_mix': dg_mix, 'a_log': d_alp[:, NH:2 * NH], 'dt_bias': d_dtp[:, NH:2 * NH],
        'g_onorm': dg_onorm, 'lam_re': d_s5p[0].reshape(1, S5G, S5P), 'lam_im': d_s5p[1].reshape(1, S5G, S5P),
        'log_step': d_s5p[2],
        'b_re': d_s5p[3].reshape(S5C, S5G, S5P).transpose(1, 2, 0)[None],
        'b_im': d_s5p[4].reshape(S5C, S5G, S5P).transpose(1, 2, 0)[None],
        'c_re': d_s5p[5].reshape(S5C, S5G, S5P).transpose(1, 0, 2)[None],
        'c_im': d_s5p[6].reshape(S5C, S5G, S5P).transpose(1, 0, 2)[None],
        'd_skip': dg_dskip, 'b_glu': dg_bglu, 'g_ffn2': dg_ffn2, 'g_final': dg_final.reshape(D)}
    small_shapes = [a[n].shape for n in SMALL]
    sg = all_gather("gather_small_grads", _pack([dmod_mine, d_conv8[:CONVW]] + [small_grads[n] for n in SMALL], F32))
    pieces = _unpack8(sg, [dmod_mine.shape, (CONVW, 3 * DNW)] + small_shapes)
    dmod_all = pieces[0].reshape(NDEV * bl, 9 * D)
    g_wada, g_bada = ada_bwd(c_all, lax.dynamic_slice(dmod_all, (0, me * n_ada), (NDEV * bl, n_ada)), dmod_all)

    n_conv = conv_qkv.shape[2]
    conv_parts = lax.dynamic_slice(pieces[1], (0, 0, me * n_conv), (NDEV, CONVW, n_conv))
    conv_parts = jnp.pad(conv_parts.reshape(NDEV, 1, -1), ((0, 0), (0, 7), (0, 0)))
    pad8 = lambda t: jnp.pad(t.reshape(1, -1), ((0, 7), (0, 0)))
    conv_res = adamw("adamw_conv", conv_parts, pad8(conv_qkv), pad8(m_conv_qkv), pad8(v_conv_qkv))
    for kind, buf in zip(("grad", "delta", "new_m", "new_v"), conv_res):
        res[kind + "_conv_qkv"] = buf[0].reshape(conv_qkv.shape)

    small_parts = jnp.stack([_pack([p[k] for p in pieces[2:]], F32) for k in range(NDEV)])
    small_res = adamw("adamw_small", small_parts, *[_pack([a[p + n] for n in SMALL], F32) for p in ("", "m_", "v_")])
    for kind, buf in zip(("grad", "delta", "new_m", "new_v"), small_res):
        for n, t in zip(SMALL, _unpack(buf, small_shapes)):
            res[kind + "_" + n] = t

    for n, g in (("w_ada", g_wada), ("b_ada", g_bada)):
        shp = a[n].shape
        r2 = lambda t: t.reshape(-1, shp[-1]) if n == "w_ada" else pad8(t)
        out = adamw("adamw_" + n, r2(g)[None], r2(a[n]), r2(a["m_" + n]), r2(a["v_" + n]))
        for kind, buf in zip(("grad", "delta", "new_m", "new_v"), out):
            res[kind + "_" + n] = (buf if n == "w_ada" else buf[0:1]).reshape(shp)

    outs = [loss, dh0.reshape(x.shape)]
    for kind in ("grad", "delta", "new_m", "new_v"):
        outs += [res[kind + "_" + n] for n in WEIGHTS]
    return tuple(outs)
```

```python
import functools
import math

import jax
import jax.numpy as jnp
from jax import lax
from jax.experimental import pallas as pl
from jax.experimental.pallas import tpu as pltpu

F32 = jnp.float32
BF16 = jnp.bfloat16
HI = lax.Precision.HIGHEST
H3 = lax.Precision.HIGH
SDS = jax.ShapeDtypeStruct

D = 1024
FF = 2816
FFN_TF = FF // 2
NH = 8
DH = 64
DNW = NH * DH
CONVW = 4
CH = 64
DN_ROWS = 2
S5W = 512
S5G = 32
S5P = 64
S5C = 16
S5N = S5G * S5P
GB = 4
NDEV = 8
EPS = 1e-6
LANES = 128
ROW = 1024
VMEM_LIMIT = 56 * 1024 * 1024

ADAM_LR, ADAM_B1, ADAM_B2, ADAM_EPS, ADAM_WD, ADAM_STEP = 0.001, 0.9, 0.999, 1e-08, 0.01, 10

WEIGHTS = ['w_ada', 'b_ada', 'g_ffn1', 'w1_ffn1', 'w3_ffn1', 'w2_ffn1', 'g_mix', 'w_in', 'conv_qkv', 'a_log',
           'dt_bias', 'g_onorm', 'lam_re', 'lam_im', 'log_step', 'b_re', 'b_im', 'c_re', 'c_im', 'd_skip', 'w_glu',
           'b_glu', 'w_proj_a', 'w_proj_b', 'w_out', 'g_ffn2', 'w1_ffn2', 'w3_ffn2', 'w2_ffn2', 'g_final']
RS_WEIGHTS = ['w1_ffn1', 'w3_ffn1', 'w2_ffn1', 'w_in', 'w_glu', 'w_proj_a', 'w_proj_b', 'w_out', 'w1_ffn2', 'w3_ffn2',
              'w2_ffn2']
COL_SHARDED = {'w1_ffn1', 'w3_ffn1', 'w_in', 'w_proj_a', 'w_proj_b', 'w1_ffn2', 'w3_ffn2'}
G_FFN1 = ['w1_ffn1', 'w3_ffn1', 'w2_ffn1']
G_MIX = ['w_in', 'w_glu', 'w_proj_a', 'w_proj_b', 'w_out']
G_FFN2 = ['w1_ffn2', 'w3_ffn2', 'w2_ffn2']
SMALL = ['g_ffn1', 'g_mix', 'a_log', 'dt_bias', 'g_onorm', 'lam_re', 'lam_im', 'log_step', 'b_re', 'b_im', 'c_re',
         'c_im', 'd_skip', 'b_glu', 'g_ffn2', 'g_final']


def _cp(n_grid=0):
    if n_grid:
        return pltpu.CompilerParams(vmem_limit_bytes=VMEM_LIMIT, dimension_semantics=("arbitrary",) * n_grid)
    return pltpu.CompilerParams(vmem_limit_bytes=VMEM_LIMIT)


def _dot(a, b):
    return jnp.dot(a.astype(BF16), b.astype(BF16), preferred_element_type=F32)


def _dot_nt(a, b):
    return lax.dot_general(a.astype(BF16), b.astype(BF16), (((1,), (1,)), ((), ())), preferred_element_type=F32)


def _dot_tn(a, b):
    return lax.dot_general(a.astype(BF16), b.astype(BF16), (((0,), (0,)), ((), ())), preferred_element_type=F32)


def _dot_hi(a, b):
    return jnp.dot(a, b, precision=HI, preferred_element_type=F32)


def _dot_h3(a, b):
    return jnp.dot(a, b, precision=H3, preferred_element_type=F32)


@jax.custom_vjp
def bdot(a, b):
    return _dot(a, b)


bdot.defvjp(lambda a, b: (_dot(a, b), (a, b)),
            lambda r, g: (_dot_nt(g, r[1]).astype(r[0].dtype), _dot_tn(r[0], g).astype(r[1].dtype)))


@jax.custom_vjp
def bdot_nt(a, b):
    return _dot_nt(a, b)


bdot_nt.defvjp(lambda a, b: (_dot_nt(a, b), (a, b)),
               lambda r, g: (_dot(g, r[1]).astype(r[0].dtype), _dot_tn(g, r[0]).astype(r[1].dtype)))


@jax.custom_vjp
def bdot_tn(a, b):
    return _dot_tn(a, b)


bdot_tn.defvjp(lambda a, b: (_dot_tn(a, b), (a, b)),
               lambda r, g: (_dot_nt(r[1], g).astype(r[0].dtype), _dot(r[0], g).astype(r[1].dtype)))


def _silu(x):
    return x * jax.nn.sigmoid(x)


def _iota2(shape, axis):
    return lax.broadcasted_iota(jnp.int32, shape, axis)


def normmod(h, g, sc, sh):
    y = h * lax.rsqrt(jnp.mean(h * h, axis=-1, keepdims=True) + EPS) * g
    return y * (1.0 + sc) + sh


def fn_normmod(h, sh, sc, g):
    return (normmod(h, g, sc, sh),)


def fn_resid(mo, gt):
    return (gt * mo,)


def fn_merge(gate, ya, yb):
    return (jax.nn.sigmoid(gate[:, :D]) * ya + jax.nn.sigmoid(gate[:, D:]) * yb,)


def fn_glu(y, w, b):
    ge = jax.nn.gelu(y)
    return (ge * jax.nn.sigmoid(bdot(ge, w) + b),)


def fn_onorm(o, z, g_on):
    r = _iota2((DH, DNW), 0)
    c = _iota2((DH, DNW), 1)
    expand = (c % DH == r).astype(F32)
    r2 = _iota2((DNW, DNW), 0)
    c2 = _iota2((DNW, DNW), 1)
    avg = (r2 // DH == c2 // DH).astype(F32) * (1.0 / DH)
    ms = _dot_h3(o * o, avg)
    return (o * lax.rsqrt(ms + EPS) * _dot_hi(g_on, expand) * _silu(z),)


def gate_fn(small, alp, dtp):
    beta = jax.nn.sigmoid(small)
    la = -jnp.exp(alp) * jax.nn.softplus(small + dtp)
    tri = (_iota2((CH, CH), 0) >= _iota2((CH, CH), 1)).astype(F32)
    gc = _dot_hi(tri, la)
    gct = lax.dot_general(la, tri, (((0,), (1,)), ((), ())), precision=HI, preferred_element_type=F32)
    return beta, gc, gct


def _bdg(a, b, ca, cb, hi):
    if not hi:
        a, b = a.astype(BF16), b.astype(BF16)
    return lax.dot_general(a, b, (((ca,), (cb,)), ((0,), (0,))), precision=H3 if hi else None,
                           preferred_element_type=F32)


def _batched_matmuls(hi):
    nn_ = lambda a, b: _bdg(a, b, 2, 1, hi)
    nt_ = lambda a, b: _bdg(a, b, 2, 2, hi)
    tn_ = lambda a, b: _bdg(a, b, 1, 1, hi)
    nn = jax.custom_vjp(nn_)
    nn.defvjp(lambda a, b: (nn_(a, b), (a, b)), lambda r, g: (nt_(g, r[1]), tn_(r[0], g)))
    nt = jax.custom_vjp(nt_)
    nt.defvjp(lambda a, b: (nt_(a, b), (a, b)), lambda r, g: (nn_(g, r[1]), tn_(g, r[0])))
    tn = jax.custom_vjp(tn_)
    tn.defvjp(lambda a, b: (tn_(a, b), (a, b)), lambda r, g: (nt_(r[1], g), nn_(r[0], g)))
    return nn, nt, tn


bnn, bnt, btn = _batched_matmuls(False)
hnn, hnt, htn = _batched_matmuls(True)


def _unit_lower_inverse(a):
    r = _iota2((1, CH, CH), 1)
    c = _iota2((1, CH, CH), 2)
    eye = (r == c).astype(F32)
    d = jnp.where(r // 8 == c // 8, a, 0.0)
    inv = eye - d
    p = d
    for _ in range(2):
        p = hnn(p, p)
        inv = inv + hnn(inv, p)
    for blk in (16, 32, 64):
        off = jnp.where((r // blk == c // blk) & (r // (blk // 2) != c // (blk // 2)), a, 0.0)
        inv = inv - hnn(hnn(inv, off), inv)
    return inv


@jax.custom_vjp
def _inverse_given(a, t):
    return t


_inverse_given.defvjp(lambda a, t: (t, t), lambda t, g: (-hnt(htn(t, g), t), jnp.zeros_like(t)))


def _conv_act(x, w):
    c = x[:, 5:69] * w[:, 0:1] + x[:, 6:70] * w[:, 1:2] + x[:, 7:71] * w[:, 2:3] + x[:, 8:72] * w[:, 3:4]
    return _silu(c)


def dn_chunk(xq, xk, xv, wq, wk, wv, b, g, gt, s_prev, t_saved=None):
    q = _conv_act(xq, wq)
    k = _conv_act(xk, wk)
    v = _conv_act(xv, wv)
    q = q * lax.rsqrt(jnp.sum(q * q, axis=-1, keepdims=True) + EPS) * (DH ** -0.5)
    k = k * lax.rsqrt(jnp.sum(k * k, axis=-1, keepdims=True) + EPS)
    r = _iota2((1, CH, CH), 1)
    c = _iota2((1, CH, CH), 2)
    causal = r >= c
    dec = jnp.where(causal, jnp.exp(jnp.where(causal, g - gt, 0.0)), 0.0)
    kb = k * b
    qk = bnt(jnp.concatenate([q, kb], axis=1), k)
    attn = qk[:, :CH] * dec
    a = jnp.where(r > c, qk[:, CH:] * dec, 0.0)
    tinv = _unit_lower_inverse(a) if t_saved is None else _inverse_given(a, t_saved)
    eg = jnp.exp(g)
    uw = hnn(tinv, jnp.concatenate([v * b, kb * eg], axis=2))
    g_last = g[:, CH - 1:CH]
    ws = bnn(jnp.concatenate([uw[..., DH:], q * eg], axis=1), s_prev)
    v_new = uw[..., :DH] - ws[:, :CH]
    o = ws[:, CH:] + bnn(attn, v_new)
    s_new = s_prev * jnp.exp(g_last) + btn(k * jnp.exp(g_last - g), v_new)
    return o, s_new, tinv


def s5_chunk(u, xp_re, xp_im, bb_re, bb_im, cc_re, cc_im, p0r, p0i, p1r, p1i, pir, pii, dsk):
    nb = u.shape[0]
    u2 = u.reshape(nb * CH, LANES)
    bu_re = bdot(u2, bb_re).reshape(nb, CH, 512)
    bu_im = bdot(u2, bb_im).reshape(nb, CH, 512)
    xt_re = pir * bu_re - pii * bu_im
    xt_im = pir * bu_im + pii * bu_re
    tri = jnp.broadcast_to((_iota2((1, CH, CH), 1) >= _iota2((1, CH, CH), 2)).astype(F32), (nb, CH, CH))
    cs_re = hnn(tri, xt_re)
    cs_im = hnn(tri, xt_im)
    x_re = p0r * cs_re - p0i * cs_im + p1r * xp_re - p1i * xp_im
    x_im = p0r * cs_im + p0i * cs_re + p1r * xp_im + p1i * xp_re
    y = bdot_nt(x_re.reshape(nb * CH, 512), cc_re) - bdot_nt(x_im.reshape(nb * CH, 512), cc_im) + dsk * u2
    return y.reshape(nb, CH, LANES), x_re[:, CH - 1:CH], x_im[:, CH - 1:CH]


def s5_tables(lam_re, lam_im, log_step, bre, bim, cre, cim):
    expand = (_iota2((S5G, S5N), 1) // S5P == _iota2((S5G, S5N), 0)).astype(F32)
    step = _dot_hi(jnp.exp(log_step), expand)
    lre = jnp.minimum(lam_re, -1e-4)
    lr = lre * step
    ang = lam_im * step
    mag = jnp.exp(lr)
    lb_re = mag * jnp.cos(ang)
    lb_im = mag * jnp.sin(ang)
    den = lre * lre + lam_im * lam_im
    coef_re = ((lb_re - 1.0) * lre + lb_im * lam_im) / den
    coef_im = (lb_im * lre - (lb_re - 1.0) * lam_im) / den
    bb_re = coef_re * bre - coef_im * bim
    bb_im = coef_re * bim + coef_im * bre
    j = _iota2((CH, 1), 0).astype(F32)
    e0 = jnp.exp(j * lr)
    e1 = jnp.exp((j + 1.0) * lr)
    ei = jnp.exp(-j * lr)
    mask = (_iota2((LANES, 512), 0) // S5C == _iota2((LANES, 512), 1) // S5P).astype(F32)

    def blocks(t):
        return jnp.concatenate([(jnp.tile(t[:, gb * 512:(gb + 1) * 512], (LANES // S5C, 1)) * mask)[None]
                                for gb in range(GB)], axis=0)

    return (blocks(bb_re), blocks(bb_im), blocks(cre), blocks(cim),
            e0 * jnp.cos(j * ang), e0 * jnp.sin(j * ang),
            e1 * jnp.cos((j + 1.0) * ang), e1 * jnp.sin((j + 1.0) * ang),
            ei * jnp.cos(j * ang), -ei * jnp.sin(j * ang))


def _row_specs(tiled, batch, bcast, tm, tpb):
    specs = [pl.BlockSpec((tm, a.shape[1]), lambda i: (i, 0)) for a in tiled]
    specs += [pl.BlockSpec((None,) + a.shape[1:], lambda i: (i // tpb, 0, 0)) for a in batch]
    specs += [pl.BlockSpec(a.shape, lambda i, nd=a.ndim: (0,) * nd) for a in bcast]
    return specs


def ew_call(name, fn, tiled, batch, bcast, outs, tm, seq):
    t_rows = tiled[0].shape[0]
    n_in = len(tiled) + len(batch) + len(bcast)

    def body(*refs):
        vals = [r[...].astype(F32) for r in refs[:n_in]]
        for r, o in zip(refs[n_in:], fn(*vals)):
            r[...] = o.astype(r.dtype)

    return pl.pallas_call(
        body, grid=(t_rows // tm,), in_specs=_row_specs(tiled, batch, bcast, tm, seq // tm),
        out_specs=[pl.BlockSpec((tm, w), lambda i: (i, 0)) for w, _ in outs],
        out_shape=[SDS((t_rows, w), dt) for w, dt in outs], name=name, compiler_params=_cp(1))(*tiled, *batch, *bcast)


def ew_vjp_call(name, fn, tiled, batch, bcast, cts, want, tm, seq, addend=None):
    t_rows = tiled[0].shape[0]
    tpb = seq // tm
    n_t, n_b, n_c = len(tiled), len(batch), len(bcast)
    n_in = n_t + n_b + n_c
    extra = [] if addend is None else [addend]

    def body(*refs):
        i = pl.program_id(0)
        vals = [r[...].astype(F32) for r in refs[:n_in]]
        ctv = tuple(r[...].astype(F32) for r in refs[n_in:n_in + len(cts)])
        outs = refs[n_in + len(cts) + len(extra):]
        _, vjp = jax.vjp(fn, *vals)
        grads = vjp(ctv)
        for k, (r, (idx, _)) in enumerate(zip(outs[:len(want)], want)):
            g = grads[idx]
            if k == 0 and extra:
                g = g + refs[n_in + len(cts)][...]
            r[...] = g.astype(r.dtype)
        for k in range(n_b):
            r, g = outs[len(want) + k], grads[n_t + k]

            @pl.when(i % tpb == 0)
            def _(r=r, g=g):
                r[...] = g

            @pl.when(i % tpb != 0)
            def _(r=r, g=g):
                r[...] += g
        for k in range(n_c):
            r, g = outs[len(want) + n_b + k], grads[n_t + n_b + k]

            @pl.when(i == 0)
            def _(r=r, g=g):
                r[...] = g

            @pl.when(i != 0)
            def _(r=r, g=g):
                r[...] += g

    out_specs = [pl.BlockSpec((tm, tiled[idx].shape[1]), lambda i: (i, 0)) for idx, _ in want]
    out_specs += [pl.BlockSpec((None,) + a.shape[1:], lambda i: (i // tpb, 0, 0)) for a in batch]
    out_specs += [pl.BlockSpec(a.shape, lambda i, nd=a.ndim: (0,) * nd) for a in bcast]
    out_shape = [SDS(tiled[idx].shape, dt) for idx, dt in want]
    out_shape += [SDS(a.shape, F32) for a in batch] + [SDS(a.shape, F32) for a in bcast]
    res = pl.pallas_call(
        body, grid=(t_rows // tm,),
        in_specs=_row_specs(tiled, batch, bcast, tm, tpb)
        + [pl.BlockSpec((tm, a.shape[1]), lambda i: (i, 0)) for a in list(cts) + extra],
        out_specs=out_specs, out_shape=out_shape, name=name, compiler_params=_cp(1))(*tiled, *batch, *bcast, *cts, *extra)
    return res[:len(want)], res[len(want):len(want) + n_b], res[len(want) + n_b:]


def _pick(n, cands):
    for c in cands:
        if n % c == 0:
            return c
    return n


def mm(name, pairs, nt, out_dtype):
    m = pairs[0][0].shape[0]
    n = pairs[0][1].shape[0 if nt else 1]
    k_total = sum(a.shape[1] for a, _ in pairs)
    tm = _pick(m, (1024, 512, 256, 128) if k_total <= 2048 else (512, 256, 128))
    tn = _pick(n, (512, 256, 128))
    np_ = len(pairs)

    def body(*refs):
        acc = None
        for p in range(np_):
            a, b = refs[2 * p][...], refs[2 * p + 1][...]
            t = _dot_nt(a, b) if nt else _dot(a, b)
            acc = t if acc is None else acc + t
        refs[2 * np_][...] = acc.astype(out_dtype)

    in_specs, ops = [], []
    for a, b in pairs:
        k = a.shape[1]
        in_specs.append(pl.BlockSpec((tm, k), lambda i, j: (i, 0)))
        in_specs.append(pl.BlockSpec((tn, k), lambda i, j: (j, 0)) if nt else pl.BlockSpec((k, tn), lambda i, j: (0, j)))
        ops += [a, b]
    return pl.pallas_call(
        body, grid=(m // tm, n // tn), in_specs=in_specs, out_specs=pl.BlockSpec((tm, tn), lambda i, j: (i, j)),
        out_shape=SDS((m, n), out_dtype), name=name, compiler_params=_cp(2))(*ops)


def mm_tn(name, a, b):
    t_rows, m = a.shape
    n = b.shape[1]
    tn = n if n <= 1024 else _pick(n, (1024, 512, 256, 128))
    tm = max([t for t in range(LANES, m + 1, LANES) if m % t == 0 and t * tn * 4 <= 6 * 1024 * 1024] or [m])
    tk = _pick(t_rows, (512, 256, 128, 64))

    def body(a_ref, b_ref, o_ref):
        @pl.when(pl.program_id(2) == 0)
        def _():
            o_ref[...] = jnp.zeros_like(o_ref)

        o_ref[...] += _dot_tn(a_ref[...], b_ref[...])

    return pl.pallas_call(
        body, grid=(m // tm, n // tn, t_rows // tk),
        in_specs=[pl.BlockSpec((tk, tm), lambda i, j, k: (k, i)), pl.BlockSpec((tk, tn), lambda i, j, k: (k, j))],
        out_specs=pl.BlockSpec((tm, tn), lambda i, j, k: (i, j)), out_shape=SDS((m, n), F32), name=name,
        compiler_params=_cp(3))(a, b)


def ffn_fwd(name, h, mod3, g, w1, w3, w2, seq, gather=None):
    t_rows = h.shape[0]
    tm = _pick(seq, (512, 256, 128, 64))
    tf = FFN_TF
    tpb = seq // tm
    nf = FF // tf
    nt = t_rows // tm
    extra = [] if gather is None else [gather]

    def body(*refs):
        h_ref, mod_ref, g_ref, w1_ref, w3_ref, w2_ref = refs[:6]
        ho_ref, f_ref, u_ref = refs[6 + len(extra):9 + len(extra)]
        acc = refs[9 + 2 * len(extra)]
        i, j = pl.program_id(0), pl.program_id(1)
        if extra:
            start, forward, finish = _gather_phases(refs[6], refs[10], *refs[12:15])
            pl.when((i == 0) & (j == 0))(start)
            pl.when((i == nt // 2) & (j == 0))(forward)

        @pl.when(j == 0)
        def _():
            u_ref[...] = normmod(h_ref[...], g_ref[...], mod_ref[1:2, :], mod_ref[0:1, :]).astype(BF16)
            acc[...] = jnp.zeros_like(acc)

        u = u_ref[...]
        a = _silu(_dot_nt(u, w1_ref[...])) * _dot_nt(u, w3_ref[...])
        acc[...] += _dot(a, w2_ref[...])

        @pl.when(j == nf - 1)
        def _():
            f_ref[...] = acc[...]
            ho_ref[...] = h_ref[...] + 0.5 * mod_ref[2:3, :] * acc[...]

        if extra:
            pl.when((i == nt - 1) & (j == nf - 1))(finish)

    row = lambda i, j: (i, 0)
    return pl.pallas_call(
        body, grid=(nt, nf),
        in_specs=[pl.BlockSpec((tm, D), row), pl.BlockSpec((None, 3, D), lambda i, j: (i // tpb, 0, 0)),
                  pl.BlockSpec((1, D), lambda i, j: (0, 0)), pl.BlockSpec((tf, D), lambda i, j: (j, 0)),
                  pl.BlockSpec((tf, D), lambda i, j: (j, 0)), pl.BlockSpec((tf, D), lambda i, j: (j, 0))]
        + [HBM_SPEC] * len(extra),
        out_specs=[pl.BlockSpec((tm, D), row), pl.BlockSpec((tm, D), row), pl.BlockSpec((tm, D), row)]
        + [HBM_SPEC] * len(extra),
        out_shape=[SDS((t_rows, D), F32), SDS((t_rows, D), F32), SDS((t_rows, D), BF16)]
        + [SDS((NDEV,) + x.shape, x.dtype) for x in extra],
        scratch_shapes=[pltpu.VMEM((tm, D), F32)] + (_comm_scratch() if extra else []), name=name,
        compiler_params=_cp(2))(h, mod3, g, w1, w3, w2, *extra)


def ffn_bwd(name, dho, h, f_out, u, mod3, g, w1, w3, w2, seq, exchange=None):
    t_rows = h.shape[0]
    tm = _pick(seq, (256, 128, 64))
    tf = FFN_TF
    tpb = seq // tm
    nf = FF // tf
    nt = t_rows // tm
    extra = [] if exchange is None else [exchange]

    def body(*refs):
        dho_ref, h_ref, f_ref, u_ref, mod_ref, g_ref, w1_ref, w3_ref, w2_ref = refs[:9]
        dh_ref, a_ref, dh1_ref, dh3_ref, df_scr, dmod_ref, dg_ref = refs[9 + len(extra):16 + len(extra)]
        du_acc = refs[16 + 2 * len(extra)]
        i, j = pl.program_id(0), pl.program_id(1)
        if extra:
            start, finish = _exchange_phases(refs[9], refs[17], *refs[19:22])
            pl.when((i == 0) & (j == 0))(start)

        @pl.when(j == 0)
        def _():
            df_scr[...] = (0.5 * mod_ref[2:3, :] * dho_ref[...]).astype(BF16)
            du_acc[...] = jnp.zeros_like(du_acc)

        uu = u_ref[...]
        h1 = _dot_nt(uu, w1_ref[...])
        h3 = _dot_nt(uu, w3_ref[...])
        sg = jax.nn.sigmoid(h1)
        s = h1 * sg
        da = _dot_nt(df_scr[...], w2_ref[...])
        dh3 = (da * s).astype(BF16)
        dh1 = (da * h3 * (sg * (1.0 + h1 * (1.0 - sg)))).astype(BF16)
        a_ref[...] = (s * h3).astype(BF16)
        dh1_ref[...] = dh1
        dh3_ref[...] = dh3
        du_acc[...] += _dot(dh1, w1_ref[...]) + _dot(dh3, w3_ref[...])

        @pl.when(j == nf - 1)
        def _():
            _, vjp = jax.vjp(normmod, h_ref[...], g_ref[...], mod_ref[1:2, :], mod_ref[0:1, :])
            dh_n, dg, dsc, dsh = vjp(du_acc[...])
            dh_ref[...] = dho_ref[...] + dh_n
            dgt = jnp.sum(0.5 * dho_ref[...] * f_ref[...], axis=0, keepdims=True)
            dmod = jnp.concatenate([dsh, dsc, dgt], axis=0)

            @pl.when(i % tpb == 0)
            def _():
                dmod_ref[...] = dmod

            @pl.when(i % tpb != 0)
            def _():
                dmod_ref[...] += dmod

            @pl.when(i == 0)
            def _():
                dg_ref[...] = dg

            @pl.when(i != 0)
            def _():
                dg_ref[...] += dg

        if extra:
            pl.when((i == nt - 1) & (j == nf - 1))(finish)

    row = lambda i, j: (i, 0)
    col = lambda i, j: (i, j)
    return pl.pallas_call(
        body, grid=(nt, nf),
        in_specs=[pl.BlockSpec((tm, D), row), pl.BlockSpec((tm, D), row), pl.BlockSpec((tm, D), row),
                  pl.BlockSpec((tm, D), row), pl.BlockSpec((None, 3, D), lambda i, j: (i // tpb, 0, 0)),
                  pl.BlockSpec((1, D), lambda i, j: (0, 0)), pl.BlockSpec((tf, D), lambda i, j: (j, 0)),
                  pl.BlockSpec((tf, D), lambda i, j: (j, 0)), pl.BlockSpec((tf, D), lambda i, j: (j, 0))]
        + [HBM_SPEC] * len(extra),
        out_specs=[pl.BlockSpec((tm, D), row), pl.BlockSpec((tm, tf), col), pl.BlockSpec((tm, tf), col),
                   pl.BlockSpec((tm, tf), col), pl.BlockSpec((tm, D), row),
                   pl.BlockSpec((None, 3, D), lambda i, j: (i // tpb, 0, 0)), pl.BlockSpec((1, D), lambda i, j: (0, 0))]
        + [HBM_SPEC] * len(extra),
        out_shape=[SDS((t_rows, D), F32), SDS((t_rows, FF), BF16), SDS((t_rows, FF), BF16), SDS((t_rows, FF), BF16),
                   SDS((t_rows, D), BF16), SDS(mod3.shape, F32), SDS((1, D), F32)] + [SDS(x.shape, x.dtype) for x in extra],
        scratch_shapes=[pltpu.VMEM((tm, D), F32)] + (_comm_scratch() if extra else []), name=name,
        compiler_params=_cp(2))(dho, h, f_out, u, mod3, g, w1, w3, w2, *extra)


def _dn_cols(part, hd):
    return slice(part * DNW + hd * DH, part * DNW + (hd + 1) * DH)


def _dn_stacks(raw_ref, halo_ref, conv_ref, hm, nb):
    pairs = [(b, hd) for b in range(nb) for hd in range(NH)]
    xs = [jnp.stack([jnp.concatenate([halo_ref[b, :, _dn_cols(part, hd)] * hm, raw_ref[b, :, _dn_cols(part, hd)]], axis=0)
                     for b, hd in pairs]) for part in range(3)]
    ws = [jnp.stack([conv_ref[0:CONVW, _dn_cols(part, hd)] for b, hd in pairs]) for part in range(3)]
    return xs, ws


def _gate_stacks(gates, nb):
    pairs = [(b, hd) for b in range(nb) for hd in range(NH)]
    bs = jnp.stack([gates[b][0][:, hd:hd + 1] for b, hd in pairs])
    gs = jnp.stack([gates[b][1][:, NH + hd:NH + hd + 1] for b, hd in pairs])
    gts = jnp.stack([gates[b][2][NH + hd:NH + hd + 1, :] for b, hd in pairs])
    return bs, gs, gts


def deltanet_fwd(p_dn, p_small, conv8, alp, dtp, nb):
    bl, seq, _ = p_dn.shape
    nc = seq // CH
    ng = nb * NH

    def body(raw_ref, halo_ref, small_ref, conv_ref, alp_ref, dtp_ref, o_ref, sprev_ref, tinv_ref, s_scr):
        n = pl.program_id(1)

        @pl.when(n == 0)
        def _():
            s_scr[...] = jnp.zeros_like(s_scr)

        hm = (n > 0).astype(F32)
        gates = [gate_fn(small_ref[b], alp_ref[...], dtp_ref[...]) for b in range(nb)]
        xs, ws = _dn_stacks(raw_ref, halo_ref, conv_ref, hm, nb)
        s_prev = s_scr[...]
        o, s_new, tinv = dn_chunk(*xs, *ws, *_gate_stacks(gates, nb), s_prev)
        sprev_ref[...] = s_prev
        tinv_ref[...] = tinv
        s_scr[...] = s_new
        for b in range(nb):
            for hd in range(NH):
                o_ref[b, :, hd * DH:(hd + 1) * DH] = o[b * NH + hd]

    blk = lambda bb, n: (bb, n, 0)
    const = lambda bb, n: (0, 0)
    saved = pl.BlockSpec((None, ng, DH, DH), lambda bb, n: (bb * nc + n, 0, 0, 0))
    return pl.pallas_call(
        body, grid=(bl // nb, nc),
        in_specs=[pl.BlockSpec((nb, CH, 4 * DNW), blk),
                  pl.BlockSpec((nb, 8, 3 * DNW), lambda bb, n: (bb, jnp.maximum(n * (CH // 8) - 1, 0), 0)),
                  pl.BlockSpec((nb, CH, LANES), blk), pl.BlockSpec((8, 3 * DNW), const), pl.BlockSpec((1, LANES), const),
                  pl.BlockSpec((1, LANES), const)],
        out_specs=[pl.BlockSpec((nb, CH, DNW), blk), saved, saved],
        out_shape=[SDS((bl, seq, DNW), F32), SDS((bl // nb * nc, ng, DH, DH), F32), SDS((bl // nb * nc, ng, DH, DH), F32)],
        scratch_shapes=[pltpu.VMEM((ng, DH, DH), F32)], name="deltanet_fwd",
        compiler_params=_cp(2))(p_dn, p_dn, p_small, conv8, alp, dtp)


def deltanet_bwd(p_dn, p_small, conv8, alp, dtp, sprev, tinv, d_o, d_z, nb, exchange=None):
    bl, seq, _ = p_dn.shape
    nc = seq // CH
    ng = nb * NH
    extra = [] if exchange is None else [exchange]

    def body(*refs):
        raw_ref, halo_ref, small_ref, conv_ref, alp_ref, dtp_ref, sprev_ref, tinv_ref, do_ref, dz_ref = refs[:10]
        draw_ref, dsmall_ref, dconv_ref, dalp_ref, ddtp_ref = refs[10 + len(extra):15 + len(extra)]
        ds_scr, dhalo_scr = refs[15 + 2 * len(extra):17 + 2 * len(extra)]
        bb, r = pl.program_id(0), pl.program_id(1)
        n = nc - 1 - r
        if extra:
            start, finish = _exchange_phases(refs[10], refs[16], *refs[19:22])
            pl.when((bb == 0) & (r == 0))(start)

        @pl.when((bb == 0) & (r == 0))
        def _():
            dconv_ref[...] = jnp.zeros_like(dconv_ref)
            dalp_ref[...] = jnp.zeros_like(dalp_ref)
            ddtp_ref[...] = jnp.zeros_like(ddtp_ref)

        @pl.when(r == 0)
        def _():
            ds_scr[...] = jnp.zeros_like(ds_scr)
            dhalo_scr[...] = jnp.zeros_like(dhalo_scr)

        hm = (n > 0).astype(F32)
        gates, gate_vjps = [], []
        for b in range(nb):
            out, gvjp = jax.vjp(gate_fn, small_ref[b], alp_ref[...], dtp_ref[...])
            gates.append(out)
            gate_vjps.append(gvjp)
        xs, ws = _dn_stacks(raw_ref, halo_ref, conv_ref, hm, nb)
        t_saved = tinv_ref[...]
        _, vjp = jax.vjp(lambda *args: dn_chunk(*args, t_saved)[:2], *xs, *ws, *_gate_stacks(gates, nb), sprev_ref[...])
        d_out = jnp.stack([do_ref[b, :, hd * DH:(hd + 1) * DH] for b in range(nb) for hd in range(NH)])
        grads = vjp((d_out, ds_scr[...]))
        ds_scr[...] = grads[9]
        lane = _iota2((CH, LANES), 1)
        rowi = _iota2((LANES, CH), 0)
        for b in range(nb):
            d_beta = jnp.zeros((CH, LANES), F32)
            d_gc = jnp.zeros((CH, LANES), F32)
            d_gct = jnp.zeros((LANES, CH), F32)
            for hd in range(NH):
                i = b * NH + hd
                for part in range(3):
                    cols = _dn_cols(part, hd)
                    dx = grads[part][i]
                    tail = dx[CH:CH + 8] + dhalo_scr[b, :, cols]
                    draw_ref[b, :, cols] = jnp.concatenate([dx[8:CH], tail], axis=0).astype(BF16)
                    dhalo_scr[b, :, cols] = dx[0:8] * hm
                d_beta = d_beta + jnp.where(lane == hd, grads[6][i], 0.0)
                d_gc = d_gc + jnp.where(lane == NH + hd, grads[7][i], 0.0)
                d_gct = d_gct + jnp.where(rowi == NH + hd, grads[8][i], 0.0)
            d_small, d_alp, d_dtp = gate_vjps[b]((d_beta, d_gc, d_gct))
            dsmall_ref[b] = d_small.astype(BF16)
            dalp_ref[...] += d_alp
            ddtp_ref[...] += d_dtp
            draw_ref[b, :, 3 * DNW:4 * DNW] = dz_ref[b].astype(BF16)
        for hd in range(NH):
            for part in range(3):
                dw = grads[3 + part][hd]
                for b in range(1, nb):
                    dw = dw + grads[3 + part][b * NH + hd]
                dconv_ref[0:CONVW, _dn_cols(part, hd)] += dw
        if extra:
            pl.when((bb == bl // nb - 1) & (r == nc - 1))(finish)

    blk = lambda bb, r: (bb, nc - 1 - r, 0)
    const = lambda bb, r: (0, 0)
    saved = pl.BlockSpec((None, ng, DH, DH), lambda bb, r: (bb * nc + nc - 1 - r, 0, 0, 0))
    return pl.pallas_call(
        body, grid=(bl // nb, nc),
        in_specs=[pl.BlockSpec((nb, CH, 4 * DNW), blk),
                  pl.BlockSpec((nb, 8, 3 * DNW), lambda bb, r: (bb, jnp.maximum((nc - 1 - r) * (CH // 8) - 1, 0), 0)),
                  pl.BlockSpec((nb, CH, LANES), blk), pl.BlockSpec((8, 3 * DNW), const), pl.BlockSpec((1, LANES), const),
                  pl.BlockSpec((1, LANES), const), saved, saved,
                  pl.BlockSpec((nb, CH, DNW), blk), pl.BlockSpec((nb, CH, DNW), blk)] + [HBM_SPEC] * len(extra),
        out_specs=[pl.BlockSpec((nb, CH, 4 * DNW), blk), pl.BlockSpec((nb, CH, LANES), blk),
                   pl.BlockSpec((8, 3 * DNW), const), pl.BlockSpec((1, LANES), const), pl.BlockSpec((1, LANES), const)]
        + [HBM_SPEC] * len(extra),
        out_shape=[SDS((bl, seq, 4 * DNW), BF16), SDS((bl, seq, LANES), BF16), SDS((8, 3 * DNW), F32), SDS((1, LANES), F32),
                   SDS((1, LANES), F32)] + [SDS(x.shape, x.dtype) for x in extra],
        scratch_shapes=[pltpu.VMEM((ng, DH, DH), F32), pltpu.VMEM((nb, 8, 3 * DNW), F32)]
        + (_comm_scratch() if extra else []), name="deltanet_bwd",
        compiler_params=_cp(2))(p_dn, p_dn, p_small, conv8, alp, dtp, sprev, tinv, d_o, d_z, *extra)


def _s5_table_specs():
    tab3 = pl.BlockSpec((None, LANES, 512), lambda gb, n: (gb, 0, 0))
    tab2 = pl.BlockSpec((CH, 512), lambda gb, n: (0, gb))
    return [tab3] * 4 + [tab2] * 6 + [pl.BlockSpec((1, LANES), lambda gb, n: (0, gb))]


def s5_fwd(u, tables, dsk):
    bl, seq, _ = u.shape
    nc = seq // CH

    def body(u_ref, *rest):
        tabs, (y_ref, xs_ref, xr_scr, xi_scr) = rest[:11], rest[11:]

        @pl.when(pl.program_id(1) == 0)
        def _():
            xr_scr[...] = jnp.zeros_like(xr_scr)
            xi_scr[...] = jnp.zeros_like(xi_scr)

        xp_re, xp_im = xr_scr[...], xi_scr[...]
        xs_ref[0:bl] = xp_re
        xs_ref[bl:2 * bl] = xp_im
        y, xn_re, xn_im = s5_chunk(u_ref[...], xp_re, xp_im, *[t[...] for t in tabs])
        y_ref[...] = y
        xr_scr[...] = xn_re
        xi_scr[...] = xn_im

    blk = lambda gb, n: (0, n, gb)
    return pl.pallas_call(
        body, grid=(GB, nc), in_specs=[pl.BlockSpec((bl, CH, LANES), blk)] + _s5_table_specs(),
        out_specs=[pl.BlockSpec((bl, CH, LANES), blk),
                   pl.BlockSpec((None, 2 * bl, 1, 512), lambda gb, n: (gb * nc + n, 0, 0, 0))],
        out_shape=[SDS((bl, seq, S5W), F32), SDS((GB * nc, 2 * bl, 1, 512), F32)],
        scratch_shapes=[pltpu.VMEM((bl, 1, 512), F32), pltpu.VMEM((bl, 1, 512), F32)], name="s5_fwd",
        compiler_params=_cp(2))(u, *tables, dsk)


def s5_bwd(u, tables, dsk, xs, dy):
    bl, seq, _ = u.shape
    nc = seq // CH

    def body(u_ref, *rest):
        tabs, xs_ref, dy_ref = rest[:11], rest[11], rest[12]
        du_ref, dtabs, dxr_scr, dxi_scr = rest[13], rest[14:25], rest[25], rest[26]
        r = pl.program_id(1)

        @pl.when(r == 0)
        def _():
            for t in dtabs:
                t[...] = jnp.zeros_like(t)
            dxr_scr[...] = jnp.zeros_like(dxr_scr)
            dxi_scr[...] = jnp.zeros_like(dxi_scr)

        _, vjp = jax.vjp(s5_chunk, u_ref[...], xs_ref[0:bl], xs_ref[bl:2 * bl], *[t[...] for t in tabs])
        grads = vjp((dy_ref[...], dxr_scr[...], dxi_scr[...]))
        du_ref[...] = grads[0].astype(BF16)
        dxr_scr[...] = grads[1]
        dxi_scr[...] = grads[2]
        for t, g in zip(dtabs, grads[3:]):
            t[...] += g

    blk = lambda gb, r: (0, nc - 1 - r, gb)
    tab_shapes = [SDS(t.shape, F32) for t in tables] + [SDS(dsk.shape, F32)]
    return pl.pallas_call(
        body, grid=(GB, nc),
        in_specs=[pl.BlockSpec((bl, CH, LANES), blk)] + _s5_table_specs()
        + [pl.BlockSpec((None, 2 * bl, 1, 512), lambda gb, r: (gb * nc + nc - 1 - r, 0, 0, 0)), pl.BlockSpec((bl, CH, LANES), blk)],
        out_specs=[pl.BlockSpec((bl, CH, LANES), blk)] + _s5_table_specs(),
        out_shape=[SDS((bl, seq, S5W), BF16)] + tab_shapes,
        scratch_shapes=[pltpu.VMEM((bl, 1, 512), F32), pltpu.VMEM((bl, 1, 512), F32)], name="s5_bwd",
        compiler_params=_cp(2))(u, *tables, dsk, xs, dy)


def s5_tables_fwd(params):
    shapes = [SDS((GB, LANES, 512), F32)] * 4 + [SDS((CH, S5N), F32)] * 6

    def body(*refs):
        for r, t in zip(refs[7:], s5_tables(*[p[...] for p in refs[:7]])):
            r[...] = t

    return pl.pallas_call(body, out_shape=shapes, name="s5_tables_fwd", compiler_params=_cp())(*params)


def s5_tables_bwd(params, dtables):
    def body(*refs):
        _, vjp = jax.vjp(s5_tables, *[p[...] for p in refs[:7]])
        for r, g in zip(refs[17:], vjp(tuple(t[...] for t in refs[7:17]))):
            r[...] = g

    return pl.pallas_call(body, out_shape=[SDS(p.shape, F32) for p in params], name="s5_tables_bwd",
                          compiler_params=_cp())(*params, *dtables)


def ada_fwd(c_all, w_loc, b_loc):
    def body(c_ref, w_ref, b_ref, o_ref):
        o_ref[...] = _dot(_silu(c_ref[...]), w_ref[...]) + b_ref[...]

    return pl.pallas_call(body, out_shape=SDS((c_all.shape[0], w_loc.shape[1]), F32), name="ada_fwd",
                          compiler_params=_cp())(c_all, w_loc, b_loc)


def ada_bwd(c_all, dmod_mine, dmod_all):
    def body(c_ref, dm_ref, da_ref, gw_ref, gb_ref):
        gw_ref[...] = _dot_tn(_silu(c_ref[...]), dm_ref[...])
        gb_ref[...] = jnp.sum(da_ref[...], axis=0, keepdims=True)

    return pl.pallas_call(body, out_shape=[SDS((D, dmod_mine.shape[1]), F32), SDS((1, dmod_all.shape[1]), F32)],
                          name="ada_bwd", compiler_params=_cp())(c_all, dmod_mine, dmod_all)


def loss_head(h, tgt, g, seq):
    t_rows = h.shape[0]
    tm = _pick(seq, (256, 128, 64))

    def body(h_ref, t_ref, g_ref, dh_ref, dg_ref, loss_ref):
        i = pl.program_id(0)
        y, vjp = jax.vjp(lambda hh, gg: hh * lax.rsqrt(jnp.mean(hh * hh, axis=-1, keepdims=True) + EPS) * gg,
                         h_ref[...], g_ref[...])
        e = y - t_ref[...]
        dh, dg = vjp(e * (1.0 / D))
        part = jnp.sum(jnp.sum(e * e, axis=1, keepdims=True), axis=0, keepdims=True) * (0.5 / D) + jnp.zeros((1, LANES), F32)
        dh_ref[...] = dh

        @pl.when(i == 0)
        def _():
            dg_ref[...] = dg
            loss_ref[...] = part

        @pl.when(i != 0)
        def _():
            dg_ref[...] += dg
            loss_ref[...] += part

    row = lambda i: (i, 0)
    const = lambda i: (0, 0)
    return pl.pallas_call(
        body, grid=(t_rows // tm,),
        in_specs=[pl.BlockSpec((tm, D), row), pl.BlockSpec((tm, D), row), pl.BlockSpec((1, D), const)],
        out_specs=[pl.BlockSpec((tm, D), row), pl.BlockSpec((1, D), const), pl.BlockSpec((1, LANES), const)],
        out_shape=[SDS((t_rows, D), F32), SDS((1, D), F32), SDS((1, LANES), F32)], name="loss_head",
        compiler_params=_cp(1))(h, tgt, g)


def adamw(name, parts, w, m, v):
    k_parts, rows, cols = parts.shape
    tr = _pick(rows, (256, 128, 64, 32, 16, 8))

    def body(p_ref, w_ref, m_ref, v_ref, g_ref, d_ref, mo_ref, vo_ref):
        g = p_ref[0].astype(F32)
        for k in range(1, k_parts):
            g = g + p_ref[k].astype(F32)
        _adam_store(g, w_ref, m_ref, v_ref, g_ref, d_ref, mo_ref, vo_ref)

    blk = pl.BlockSpec((tr, cols), lambda i: (i, 0))
    return pl.pallas_call(
        body, grid=(rows // tr,), in_specs=[pl.BlockSpec((k_parts, tr, cols), lambda i: (0, i, 0)), blk, blk, blk],
        out_specs=[blk] * 4, out_shape=[SDS((rows, cols), F32)] * 4, name=name, compiler_params=_cp(1))(parts, w, m, v)


def _adam_store(g, w_ref, m_ref, v_ref, g_ref, d_ref, mo_ref, vo_ref):
    m_new = ADAM_B1 * m_ref[...] + (1.0 - ADAM_B1) * g
    v_new = ADAM_B2 * v_ref[...] + (1.0 - ADAM_B2) * (g * g)
    m_hat = m_new / (1.0 - ADAM_B1 ** ADAM_STEP)
    v_hat = v_new / (1.0 - ADAM_B2 ** ADAM_STEP)
    g_ref[...] = g
    d_ref[...] = -ADAM_LR * (m_hat / (jnp.sqrt(v_hat) + ADAM_EPS) + ADAM_WD * w_ref[...])
    mo_ref[...] = m_new
    vo_ref[...] = v_new


def adamw_t(name, parts, w, m, v):
    k_parts, r, c = parts.shape
    tc = _pick(c, (256, 128))

    def body(p_ref, w_ref, m_ref, v_ref, g_ref, d_ref, mo_ref, vo_ref):
        gt = p_ref[0].astype(F32)
        for k in range(1, k_parts):
            gt = gt + p_ref[k].astype(F32)
        _adam_store(gt.T, w_ref, m_ref, v_ref, g_ref, d_ref, mo_ref, vo_ref)

    blk = pl.BlockSpec((tc, r), lambda j: (j, 0))
    return pl.pallas_call(
        body, grid=(c // tc,), in_specs=[pl.BlockSpec((k_parts, r, tc), lambda j: (0, 0, j)), blk, blk, blk],
        out_specs=[blk] * 4, out_shape=[SDS((c, r), F32)] * 4, name=name, compiler_params=_cp(1))(parts, w, m, v)


def _comm_scratch():
    return [pltpu.SemaphoreType.DMA((7,)), pltpu.SemaphoreType.DMA((7,)), pltpu.SemaphoreType.DMA]


HBM_SPEC = pl.BlockSpec(memory_space=pl.ANY)


def _gather_phases(x_ref, out_ref, send_sems, recv_sems, local_sem):
    mx, my, mc = lax.axis_index("x"), lax.axis_index("y"), lax.axis_index("c")
    me, sibling = (mx, my, mc), (mx, my, 1 - mc)
    chips = [(1 - mx, my), (mx, 1 - my), (1 - mx, 1 - my)]

    def slot(px, py, pc):
        return out_ref.at[4 * px + 2 * py + pc]

    def copy(k, block, to, src=None):
        return pltpu.make_async_remote_copy(
            src_ref=slot(*block) if src is None else src, dst_ref=slot(*block), send_sem=send_sems.at[k],
            recv_sem=recv_sems.at[k], device_id=to, device_id_type=pl.DeviceIdType.MESH)

    def first():
        return [copy(0, me, sibling, src=x_ref)] + [copy(1 + j, me, (*chip, mc), src=x_ref) for j, chip in enumerate(chips)]

    def passed():
        return [copy(4 + j, (*chip, mc), sibling) for j, chip in enumerate(chips)]

    def start():
        pltpu.make_async_copy(x_ref, slot(*me), local_sem).start()
        for cp in first():
            cp.start()

    def forward():
        for j, chip in enumerate(chips):
            copy(1 + j, (*chip, mc), me).wait_recv()
            passed()[j].start()

    def finish():
        copy(0, sibling, me).wait_recv()
        for j, chip in enumerate(chips):
            copy(4 + j, (*chip, 1 - mc), me).wait_recv()
        for cp in first() + passed():
            cp.wait_send()
        pltpu.make_async_copy(x_ref, slot(*me), local_sem).wait()

    return start, forward, finish


def _exchange_phases(x_ref, out_ref, send_sems, recv_sems, local_sem):
    mx, my, mc = lax.axis_index("x"), lax.axis_index("y"), lax.axis_index("c")
    me = 4 * mx + 2 * my + mc

    def peer(k):
        return mx ^ (k >> 2), my ^ ((k >> 1) & 1), mc ^ (k & 1)

    def sends():
        out = []
        for k in range(1, NDEV):
            px, py, pc = peer(k)
            out.append(pltpu.make_async_remote_copy(
                src_ref=x_ref.at[4 * px + 2 * py + pc], dst_ref=out_ref.at[me], send_sem=send_sems.at[k - 1],
                recv_sem=recv_sems.at[k - 1], device_id=(px, py, pc), device_id_type=pl.DeviceIdType.MESH))
        return out

    def start():
        pltpu.make_async_copy(x_ref.at[me], out_ref.at[me], local_sem).start()
        for cp in sends():
            cp.start()

    def finish():
        for k in range(1, NDEV):
            px, py, pc = peer(k)
            pltpu.make_async_remote_copy(
                src_ref=x_ref.at[me], dst_ref=out_ref.at[4 * px + 2 * py + pc], send_sem=send_sems.at[k - 1],
                recv_sem=recv_sems.at[k - 1], device_id=(px, py, pc), device_id_type=pl.DeviceIdType.MESH).wait_recv()
        for cp in sends():
            cp.wait_send()
        pltpu.make_async_copy(x_ref.at[me], out_ref.at[me], local_sem).wait()

    return start, finish


def all_gather(name, x):
    def body(x_ref, out_ref, send_sems, recv_sems, local_sem):
        for phase in _gather_phases(x_ref, out_ref, send_sems, recv_sems, local_sem):
            phase()

    return pl.pallas_call(body, out_shape=SDS((NDEV,) + x.shape, x.dtype), in_specs=[HBM_SPEC], out_specs=HBM_SPEC,
                          scratch_shapes=_comm_scratch(), name=name)(x)


def all_to_all(name, x):
    def body(x_ref, out_ref, send_sems, recv_sems, local_sem):
        for phase in _exchange_phases(x_ref, out_ref, send_sems, recv_sems, local_sem):
            phase()

    return pl.pallas_call(body, out_shape=SDS(x.shape, x.dtype), in_specs=[HBM_SPEC], out_specs=HBM_SPEC,
                          scratch_shapes=_comm_scratch(), name=name)(x)


def _pack(arrs, dtype, row_mult=8):
    segs = []
    for a in arrs:
        flat = a.reshape(-1).astype(dtype)
        segs.append(jnp.pad(flat, (0, (-flat.shape[0]) % ROW)))
    flat = jnp.concatenate(segs)
    flat = jnp.pad(flat, (0, (-flat.shape[0]) % (ROW * row_mult)))
    return flat.reshape(-1, ROW)


def _unpack(buf, shapes):
    flat = buf.reshape(-1)
    out, off = [], 0
    for s in shapes:
        n = math.prod(s)
        out.append(flat[off:off + n].reshape(s))
        off += n + (-n) % ROW
    return out


def _pack_rows(arrs, axis):
    padded = []
    for t in arrs:
        pad = [(0, 0)] * t.ndim
        pad[axis] = (0, _tile_rows(t.shape[axis]) - t.shape[axis])
        padded.append(jnp.pad(t, pad))
    return jnp.concatenate(padded, axis=axis)


def _tile_rows(r):
    return r + (-r) % 16


def _unpack8(buf, shapes):
    flat = buf.reshape(NDEV, -1)
    out, off = [], 0
    for s in shapes:
        n = math.prod(s)
        out.append(flat[:, off:off + n].reshape((NDEV,) + tuple(s)))
        off += n + (-n) % ROW
    return out


def kernel(x, c, w_ada, b_ada, g_ffn1, w1_ffn1, w3_ffn1, w2_ffn1, g_mix, w_in, conv_qkv, a_log, dt_bias, g_onorm, lam_re, lam_im, log_step, b_re, b_im, c_re, c_im, d_skip, w_glu, b_glu, w_proj_a, w_proj_b, w_out, g_ffn2, w1_ffn2, w3_ffn2, w2_ffn2, g_final, loss_target, m_w_ada, m_b_ada, m_g_ffn1, m_w1_ffn1, m_w3_ffn1, m_w2_ffn1, m_g_mix, m_w_in, m_conv_qkv, m_a_log, m_dt_bias, m_g_onorm, m_lam_re, m_lam_im, m_log_step, m_b_re, m_b_im, m_c_re, m_c_im, m_d_skip, m_w_glu, m_b_glu, m_w_proj_a, m_w_proj_b, m_w_out, m_g_ffn2, m_w1_ffn2, m_w3_ffn2, m_w2_ffn2, m_g_final, v_w_ada, v_b_ada, v_g_ffn1, v_w1_ffn1, v_w3_ffn1, v_w2_ffn1, v_g_mix, v_w_in, v_conv_qkv, v_a_log, v_dt_bias, v_g_onorm, v_lam_re, v_lam_im, v_log_step, v_b_re, v_b_im, v_c_re, v_c_im, v_d_skip, v_w_glu, v_b_glu, v_w_proj_a, v_w_proj_b, v_w_out, v_g_ffn2, v_w1_ffn2, v_w3_ffn2, v_w2_ffn2, v_g_final):
    a = dict(locals())
    bl, seq, _ = x.shape
    t_rows = bl * seq
    nc = seq // CH
    me = 4 * lax.axis_index("x") + 2 * lax.axis_index("y") + lax.axis_index("c")
    tm_ew = _pick(seq, (256, 128, 64))

    sm = all_gather("gather_small", _pack([c, conv_qkv[0]], F32))
    c_loc, conv_loc = _unpack8(sm, [c.shape, conv_qkv.shape[1:]])
    c_all = c_loc.reshape(NDEV * bl, D)
    conv_full = conv_loc.transpose(1, 0, 2).reshape(CONVW, 3 * DNW)
    loc = {n: (a[n][0].T if n in COL_SHARDED else a[n][0]) for n in RS_WEIGHTS}
    wfull, gw, res = {}, {}, {}

    def pack_local(names):
        return _pack_rows([loc[n].astype(BF16).reshape(-1, ROW) for n in names], 0)

    def unpack_full(buf, names):
        r0 = 0
        for n in names:
            r = loc[n].size // ROW
            wfull[n] = buf[:, r0:r0 + r, :].reshape(-1, loc[n].shape[1])
            r0 += _tile_rows(r)

    def pack_grads(names):
        return _pack_rows([gw[n].astype(BF16).reshape(NDEV, -1, ROW) for n in names], 1)

    def update(buf, names):
        r0 = 0
        for n in names:
            r = loc[n].size // ROW
            parts = buf[:, r0:r0 + r, :].reshape((NDEV,) + loc[n].shape)
            r0 += _tile_rows(r)
            step = adamw_t if n in COL_SHARDED else adamw
            out = step("adamw_" + n, parts, a[n][0], a["m_" + n][0], a["v_" + n][0])
            for kind, t in zip(("grad", "delta", "new_m", "new_v"), out):
                res[kind + "_" + n] = t[None]

    unpack_full(all_gather("gather_ffn1", pack_local(G_FFN1)), G_FFN1)

    n_ada = w_ada.shape[2]
    mod_part = ada_fwd(c_all, w_ada[0], lax.dynamic_slice(b_ada, (0, me * n_ada), (1, n_ada)))
    mod_all = all_gather("gather_mod", mod_part).transpose(1, 0, 2).reshape(NDEV * bl, 9 * D)
    mod = lax.dynamic_slice(mod_all, (me * bl, 0), (bl, 9 * D)).reshape(bl, 9, D)
    mods = [mod[:, k:k + 1, :] for k in range(9)]

    h0 = x.reshape(t_rows, D)
    h1, f1, u1, wg_rest = ffn_fwd("ffn1_fwd", h0, mod[:, 0:3, :], g_ffn1, wfull['w1_ffn1'], wfull['w3_ffn1'],
                                  wfull['w2_ffn1'], seq, gather=pack_local(G_MIX + G_FFN2))
    unpack_full(wg_rest, G_MIX + G_FFN2)
    win = wfull['w_in']
    o_small, o_s5, o_gate = 4 * DNW, 4 * DNW + 2 * NH, 4 * DNW + 2 * NH + S5W
    w_dn, w_small = win[:o_small], jnp.pad(win[o_small:o_s5], ((0, LANES - 2 * NH), (0, 0)))
    w_s5, w_gate = win[o_s5:o_gate], win[o_gate:]
    (u2,) = ew_call("mix_norm", fn_normmod, [h1], [mods[3], mods[4]], [g_mix], [(D, BF16)], tm_ew, seq)
    p_dn = mm("proj_dn", [(u2, w_dn)], True, F32)
    p_small = mm("proj_small", [(u2, w_small)], True, F32)
    p_s5 = mm("proj_s5", [(u2, w_s5)], True, F32)
    p_gate = mm("proj_gate", [(u2, w_gate)], True, F32)

    conv8 = jnp.pad(conv_full, ((0, 8 - CONVW), (0, 0)))
    alp = jnp.pad(a_log, ((0, 0), (NH, LANES - 2 * NH)))
    dtp = jnp.pad(dt_bias, ((0, 0), (NH, LANES - 2 * NH)))
    nb_dn = DN_ROWS if bl % DN_ROWS == 0 else 1
    p_dn3, p_small3 = p_dn.reshape(bl, seq, 4 * DNW), p_small.reshape(bl, seq, LANES)
    o_pre3, sprev, tinv = deltanet_fwd(p_dn3, p_small3, conv8, alp, dtp, nb_dn)
    o_pre = o_pre3.reshape(t_rows, DNW)
    z_raw = p_dn[:, 3 * DNW:]
    (oa,) = ew_call("dn_onorm", fn_onorm, [o_pre, z_raw], [], [g_onorm], [(DNW, BF16)], tm_ew, seq)
    ya = mm("proj_a", [(oa, wfull['w_proj_a'])], True, F32)

    s5_params = [lam_re.reshape(1, S5N), lam_im.reshape(1, S5N), log_step,
                 b_re[0].transpose(2, 0, 1).reshape(S5C, S5N), b_im[0].transpose(2, 0, 1).reshape(S5C, S5N),
                 c_re[0].transpose(1, 0, 2).reshape(S5C, S5N), c_im[0].transpose(1, 0, 2).reshape(S5C, S5N)]
    tables = s5_tables_fwd(s5_params)
    p_s53 = p_s5.reshape(bl, seq, S5W)
    y_s53, xs = s5_fwd(p_s53, tables, d_skip)
    y_s5 = y_s53.reshape(t_rows, S5W)
    (ob,) = ew_call("s5_glu", fn_glu, [y_s5], [], [wfull['w_glu'], b_glu], [(S5W, BF16)], tm_ew, seq)
    yb = mm("proj_b", [(ob, wfull['w_proj_b'])], True, F32)

    (merged,) = ew_call("merge", fn_merge, [p_gate, ya, yb], [], [], [(D, BF16)], tm_ew, seq)
    mo = mm("proj_out", [(merged, wfull['w_out'])], False, F32)
    (h2,) = ew_call("mix_resid", lambda p, q, gt: (q + gt * p,), [mo, h1], [mods[5]], [], [(D, F32)], tm_ew, seq)
    h3, f3, u3 = ffn_fwd("ffn2_fwd", h2, mod[:, 6:9, :], g_ffn2, wfull['w1_ffn2'], wfull['w3_ffn2'], wfull['w2_ffn2'], seq)

    dh3, dg_final, loss_part = loss_head(h3, loss_target.reshape(t_rows, D), g_final.reshape(1, D), seq)
    loss = lax.psum(loss_part[0, 0], ("x", "y", "c"))

    dh2, a3, d1_3, d3_3, df3, dmod_c, dg_ffn2 = ffn_bwd("ffn2_bwd", dh3, h2, f3, u3, mod[:, 6:9, :], g_ffn2, wfull['w1_ffn2'],
                                                   wfull['w3_ffn2'], wfull['w2_ffn2'], seq)
    gw['w1_ffn2'] = mm_tn("gw1_ffn2", d1_3, u3)
    gw['w3_ffn2'] = mm_tn("gw3_ffn2", d3_3, u3)
    gw['w2_ffn2'] = mm_tn("gw2_ffn2", a3, df3)

    (dmo,), (dgt2,), _ = ew_vjp_call("mix_resid_bwd", fn_resid, [mo], [mods[5]], [], [dh2], [(0, BF16)], tm_ew, seq)
    gw['w_out'] = mm_tn("gw_out", merged, dmo)
    d_merged = mm("d_merged", [(dmo, wfull['w_out'])], True, F32)
    (d_gate, d_ya, d_yb), _, _ = ew_vjp_call("merge_bwd", fn_merge, [p_gate, ya, yb], [], [], [d_merged],
                                             [(0, BF16), (1, BF16), (2, BF16)], tm_ew, seq)
    gw['w_proj_a'] = mm_tn("gw_proj_a", d_ya, oa)
    gw['w_proj_b'] = mm_tn("gw_proj_b", d_yb, ob)
    d_oa = mm("d_oa", [(d_ya, wfull['w_proj_a'])], False, F32)
    d_ob = mm("d_ob", [(d_yb, wfull['w_proj_b'])], False, F32)

    (d_opre, d_z), _, (dg_onorm,) = ew_vjp_call("dn_onorm_bwd", fn_onorm, [o_pre, z_raw], [], [g_onorm], [d_oa],
                                                [(0, F32), (1, F32)], tm_ew, seq)
    d_pdn3, d_psmall3, d_conv8, d_alp, d_dtp, rs_ffn2 = deltanet_bwd(
        p_dn3, p_small3, conv8, alp, dtp, sprev, tinv, d_opre.reshape(bl, seq, DNW), d_z.reshape(bl, seq, DNW), nb_dn,
        exchange=pack_grads(G_FFN2))
    d_pdn, d_psmall = d_pdn3.reshape(t_rows, 4 * DNW), d_psmall3.reshape(t_rows, LANES)

    (d_ys5,), _, (g_wglu, dg_bglu) = ew_vjp_call("s5_glu_bwd", fn_glu, [y_s5], [], [wfull['w_glu'], b_glu], [d_ob],
                                                 [(0, F32)], tm_ew, seq)
    gw['w_glu'] = g_wglu
    s5_out = s5_bwd(p_s53, tables, d_skip, xs, d_ys5.reshape(bl, seq, S5W))
    d_ps5, d_tables, dg_dskip = s5_out[0].reshape(t_rows, S5W), s5_out[1:11], s5_out[11]
    d_s5p = s5_tables_bwd(s5_params, d_tables)

    d_pdn_b, d_psm_b, d_ps5_b = d_pdn, d_psmall, d_ps5
    gw['w_in'] = jnp.concatenate([mm_tn("gw_dn", d_pdn_b, u2), mm_tn("gw_small", d_psm_b, u2)[:2 * NH],
                                  mm_tn("gw_s5", d_ps5_b, u2), mm_tn("gw_gate", d_gate, u2)], axis=0)
    du2 = mm("d_u2", [(d_pdn_b, w_dn), (d_psm_b, w_small), (d_ps5_b, w_s5), (d_gate, w_gate)], False, F32)
    (dh1,), (dsh2, dsc2), (dg_mix,) = ew_vjp_call("mix_norm_bwd", fn_normmod, [h1], [mods[3], mods[4]], [g_mix], [du2],
                                                  [(0, F32)], tm_ew, seq, addend=dh2)

    dh0, a1, d1_1, d3_1, df1, dmod_a, dg_ffn1, rs_mix = ffn_bwd(
        "ffn1_bwd", dh1, h0, f1, u1, mod[:, 0:3, :], g_ffn1, wfull['w1_ffn1'], wfull['w3_ffn1'], wfull['w2_ffn1'], seq,
        exchange=pack_grads(G_MIX))
    gw['w1_ffn1'] = mm_tn("gw1_ffn1", d1_1, u1)
    gw['w3_ffn1'] = mm_tn("gw3_ffn1", d3_1, u1)
    gw['w2_ffn1'] = mm_tn("gw2_ffn1", a1, df1)

    update(rs_ffn2, G_FFN2)
    update(rs_mix, G_MIX)
    update(all_to_all("scatter_ffn1", pack_grads(G_FFN1)), G_FFN1)

    dmod_mine = jnp.concatenate([dmod_a, dsh2, dsc2, dgt2, dmod_c], axis=1).reshape(bl, 9 * D)
    small_grads = {
        'g_ffn1': dg_ffn1, 'g_mix': dg_mix, 'a_log': d_alp[:, NH:2 * NH], 'dt_bias': d_dtp[:, NH:2 * NH],
        'g_onorm': dg_onorm, 'lam_re': d_s5p[0].reshape(1, S5G, S5P), 'lam_im': d_s5p[1].reshape(1, S5G, S5P),
        'log_step': d_s5p[2],
        'b_re': d_s5p[3].reshape(S5C, S5G, S5P).transpose(1, 2, 0)[None],
        'b_im': d_s5p[4].reshape(S5C, S5G, S5P).transpose(1, 2, 0)[None],
        'c_re': d_s5p[5].reshape(S5C, S5G, S5P).transpose(1, 0, 2)[None],
        'c_im': d_s5p[6].reshape(S5C, S5G, S5P).transpose(1, 0, 2)[None],
        'd_skip': dg_dskip, 'b_glu': dg_bglu, 'g_ffn2': dg_ffn2, 'g_final': dg_final.reshape(D)}
    small_shapes = [a[n].shape for n in SMALL]
    sg = all_gather("gather_small_grads", _pack([dmod_mine, d_conv8[:CONVW]] + [small_grads[n] for n in SMALL], F32))
    pieces = _unpack8(sg, [dmod_mine.shape, (CONVW, 3 * DNW)] + small_shapes)
    dmod_all = pieces[0].reshape(NDEV * bl, 9 * D)
    g_wada, g_bada = ada_bwd(c_all, lax.dynamic_slice(dmod_all, (0, me * n_ada), (NDEV * bl, n_ada)), dmod_all)

    n_conv = conv_qkv.shape[2]
    conv_parts = lax.dynamic_slice(pieces[1], (0, 0, me * n_conv), (NDEV, CONVW, n_conv))
    conv_parts = jnp.pad(conv_parts.reshape(NDEV, 1, -1), ((0, 0), (0, 7), (0, 0)))
    pad8 = lambda t: jnp.pad(t.reshape(1, -1), ((0, 7), (0, 0)))
    conv_res = adamw("adamw_conv", conv_parts, pad8(conv_qkv), pad8(m_conv_qkv), pad8(v_conv_qkv))
    for kind, buf in zip(("grad", "delta", "new_m", "new_v"), conv_res):
        res[kind + "_conv_qkv"] = buf[0].reshape(conv_qkv.shape)

    small_parts = jnp.stack([_pack([p[k] for p in pieces[2:]], F32) for k in range(NDEV)])
    small_res = adamw("adamw_small", small_parts, *[_pack([a[p + n] for n in SMALL], F32) for p in ("", "m_", "v_")])
    for kind, buf in zip(("grad", "delta", "new_m", "new_v"), small_res):
        for n, t in zip(SMALL, _unpack(buf, small_shapes)):
            res[kind + "_" + n] = t

    for n, g in (("w_ada", g_wada), ("b_ada", g_bada)):
        shp = a[n].shape
        r2 = lambda t: t.reshape(-1, shp[-1]) if n == "w_ada" else pad8(t)
        out = adamw("adamw_" + n, r2(g)[None], r2(a[n]), r2(a["m_" + n]), r2(a["v_" + n]))
        for kind, buf in zip(("grad", "delta", "new_m", "new_v"), out):
            res[kind + "_" + n] = (buf if n == "w_ada" else buf[0:1]).reshape(shp)

    outs = [loss, dh0.reshape(x.shape)]
    for kind in ("grad", "delta", "new_m", "new_v"):
        outs += [res[kind + "_" + n] for n in WEIGHTS]
    return tuple(outs)
```

```python
import functools
import math

import jax
import jax.numpy as jnp
from jax import lax
from jax.experimental import pallas as pl
from jax.experimental.pallas import tpu as pltpu

F32 = jnp.float32
BF16 = jnp.bfloat16
HI = lax.Precision.HIGHEST
H3 = lax.Precision.HIGH
SDS = jax.ShapeDtypeStruct

D = 1024
FF = 2816
FFN_TF = FF // 2
NH = 8
DH = 64
DNW = NH * DH
CONVW = 4
CH = 64
DN_ROWS = 2
S5W = 512
S5G = 32
S5P = 64
S5C = 16
S5N = S5G * S5P
GB = 4
NDEV = 8
EPS = 1e-6
LANES = 128
ROW = 1024
VMEM_LIMIT = 56 * 1024 * 1024

ADAM_LR, ADAM_B1, ADAM_B2, ADAM_EPS, ADAM_WD, ADAM_STEP = 0.001, 0.9, 0.999, 1e-08, 0.01, 10

WEIGHTS = ['w_ada', 'b_ada', 'g_ffn1', 'w1_ffn1', 'w3_ffn1', 'w2_ffn1', 'g_mix', 'w_in', 'conv_qkv', 'a_log',
           'dt_bias', 'g_onorm', 'lam_re', 'lam_im', 'log_step', 'b_re', 'b_im', 'c_re', 'c_im', 'd_skip', 'w_glu',
           'b_glu', 'w_proj_a', 'w_proj_b', 'w_out', 'g_ffn2', 'w1_ffn2', 'w3_ffn2', 'w2_ffn2', 'g_final']
RS_WEIGHTS = ['w1_ffn1', 'w3_ffn1', 'w2_ffn1', 'w_in', 'w_glu', 'w_proj_a', 'w_proj_b', 'w_out', 'w1_ffn2', 'w3_ffn2',
              'w2_ffn2']
COL_SHARDED = {'w1_ffn1', 'w3_ffn1', 'w_in', 'w_proj_a', 'w_proj_b', 'w1_ffn2', 'w3_ffn2'}
G_FFN1 = ['w1_ffn1', 'w3_ffn1', 'w2_ffn1']
G_MIX = ['w_in', 'w_glu', 'w_proj_a', 'w_proj_b', 'w_out']
G_FFN2 = ['w1_ffn2', 'w3_ffn2', 'w2_ffn2']
SMALL = ['g_ffn1', 'g_mix', 'a_log', 'dt_bias', 'g_onorm', 'lam_re', 'lam_im', 'log_step', 'b_re', 'b_im', 'c_re',
         'c_im', 'd_skip', 'b_glu', 'g_ffn2', 'g_final']


def _cp(n_grid=0):
    if n_grid:
        return pltpu.CompilerParams(vmem_limit_bytes=VMEM_LIMIT, dimension_semantics=("arbitrary",) * n_grid)
    return pltpu.CompilerParams(vmem_limit_bytes=VMEM_LIMIT)


def _dot(a, b):
    return jnp.dot(a.astype(BF16), b.astype(BF16), preferred_element_type=F32)


def _dot_nt(a, b):
    return lax.dot_general(a.astype(BF16), b.astype(BF16), (((1,), (1,)), ((), ())), preferred_element_type=F32)


def _dot_tn(a, b):
    return lax.dot_general(a.astype(BF16), b.astype(BF16), (((0,), (0,)), ((), ())), preferred_element_type=F32)


def _dot_hi(a, b):
    return jnp.dot(a, b, precision=HI, preferred_element_type=F32)


def _dot_h3(a, b):
    return jnp.dot(a, b, precision=H3, preferred_element_type=F32)


@jax.custom_vjp
def bdot(a, b):
    return _dot(a, b)


bdot.defvjp(lambda a, b: (_dot(a, b), (a, b)),
            lambda r, g: (_dot_nt(g, r[1]).astype(r[0].dtype), _dot_tn(r[0], g).astype(r[1].dtype)))


@jax.custom_vjp
def bdot_nt(a, b):
    return _dot_nt(a, b)


bdot_nt.defvjp(lambda a, b: (_dot_nt(a, b), (a, b)),
               lambda r, g: (_dot(g, r[1]).astype(r[0].dtype), _dot_tn(g, r[0]).astype(r[1].dtype)))


@jax.custom_vjp
def bdot_tn(a, b):
    return _dot_tn(a, b)


bdot_tn.defvjp(lambda a, b: (_dot_tn(a, b), (a, b)),
               lambda r, g: (_dot_nt(r[1], g).astype(r[0].dtype), _dot(r[0], g).astype(r[1].dtype)))


def _silu(x):
    return x * jax.nn.sigmoid(x)


def _iota2(shape, axis):
    return lax.broadcasted_iota(jnp.int32, shape, axis)


def normmod(h, g, sc, sh):
    y = h * lax.rsqrt(jnp.mean(h * h, axis=-1, keepdims=True) + EPS) * g
    return y * (1.0 + sc) + sh


def fn_normmod(h, sh, sc, g):
    return (normmod(h, g, sc, sh),)


def fn_resid(mo, gt):
    return (gt * mo,)


def fn_merge(gate, ya, yb):
    return (jax.nn.sigmoid(gate[:, :D]) * ya + jax.nn.sigmoid(gate[:, D:]) * yb,)


def fn_glu(y, w, b):
    ge = jax.nn.gelu(y)
    return (ge * jax.nn.sigmoid(bdot(ge, w) + b),)


def fn_onorm(o, z, g_on):
    r = _iota2((DH, DNW), 0)
    c = _iota2((DH, DNW), 1)
    expand = (c % DH == r).astype(F32)
    r2 = _iota2((DNW, DNW), 0)
    c2 = _iota2((DNW, DNW), 1)
    avg = (r2 // DH == c2 // DH).astype(F32) * (1.0 / DH)
    ms = _dot_h3(o * o, avg)
    return (o * lax.rsqrt(ms + EPS) * _dot_hi(g_on, expand) * _silu(z),)


def gate_fn(small, alp, dtp):
    beta = jax.nn.sigmoid(small)
    la = -jnp.exp(alp) * jax.nn.softplus(small + dtp)
    tri = (_iota2((CH, CH), 0) >= _iota2((CH, CH), 1)).astype(F32)
    gc = _dot_hi(tri, la)
    gct = lax.dot_general(la, tri, (((0,), (1,)), ((), ())), precision=HI, preferred_element_type=F32)
    return beta, gc, gct


def _bdg(a, b, ca, cb, hi):
    if not hi:
        a, b = a.astype(BF16), b.astype(BF16)
    return lax.dot_general(a, b, (((ca,), (cb,)), ((0,), (0,))), precision=H3 if hi else None,
                           preferred_element_type=F32)


def _batched_matmuls(hi):
    nn_ = lambda a, b: _bdg(a, b, 2, 1, hi)
    nt_ = lambda a, b: _bdg(a, b, 2, 2, hi)
    tn_ = lambda a, b: _bdg(a, b, 1, 1, hi)
    nn = jax.custom_vjp(nn_)
    nn.defvjp(lambda a, b: (nn_(a, b), (a, b)), lambda r, g: (nt_(g, r[1]), tn_(r[0], g)))
    nt = jax.custom_vjp(nt_)
    nt.defvjp(lambda a, b: (nt_(a, b), (a, b)), lambda r, g: (nn_(g, r[1]), tn_(g, r[0])))
    tn = jax.custom_vjp(tn_)
    tn.defvjp(lambda a, b: (tn_(a, b), (a, b)), lambda r, g: (nt_(r[1], g), nn_(r[0], g)))
    return nn, nt, tn


bnn, bnt, btn = _batched_matmuls(False)
hnn, hnt, htn = _batched_matmuls(True)


def _unit_lower_inverse(a):
    r = _iota2((1, CH, CH), 1)
    c = _iota2((1, CH, CH), 2)
    eye = (r == c).astype(F32)
    d = jnp.where(r // 8 == c // 8, a, 0.0)
    inv = eye - d
    p = d
    for _ in range(2):
        p = hnn(p, p)
        inv = inv + hnn(inv, p)
    for blk in (16, 32, 64):
        off = jnp.where((r // blk == c // blk) & (r // (blk // 2) != c // (blk // 2)), a, 0.0)
        inv = inv - hnn(hnn(inv, off), inv)
    return inv


@jax.custom_vjp
def _inverse_given(a, t):
    return t


_inverse_given.defvjp(lambda a, t: (t, t), lambda t, g: (-hnt(htn(t, g), t), jnp.zeros_like(t)))


def _conv_act(x, w):
    c = x[:, 5:69] * w[:, 0:1] + x[:, 6:70] * w[:, 1:2] + x[:, 7:71] * w[:, 2:3] + x[:, 8:72] * w[:, 3:4]
    return _silu(c)


def dn_chunk(xq, xk, xv, wq, wk, wv, b, g, gt, s_prev, t_saved=None):
    q = _conv_act(xq, wq)
    k = _conv_act(xk, wk)
    v = _conv_act(xv, wv)
    q = q * lax.rsqrt(jnp.sum(q * q, axis=-1, keepdims=True) + EPS) * (DH ** -0.5)
    k = k * lax.rsqrt(jnp.sum(k * k, axis=-1, keepdims=True) + EPS)
    r = _iota2((1, CH, CH), 1)
    c = _iota2((1, CH, CH), 2)
    causal = r >= c
    dec = jnp.where(causal, jnp.exp(jnp.where(causal, g - gt, 0.0)), 0.0)
    kb = k * b
    qk = bnt(jnp.concatenate([q, kb], axis=1), k)
    attn = qk[:, :CH] * dec
    a = jnp.where(r > c, qk[:, CH:] * dec, 0.0)
    tinv = _unit_lower_inverse(a) if t_saved is None else _inverse_given(a, t_saved)
    eg = jnp.exp(g)
    uw = hnn(tinv, jnp.concatenate([v * b, kb * eg], axis=2))
    g_last = g[:, CH - 1:CH]
    ws = bnn(jnp.concatenate([uw[..., DH:], q * eg], axis=1), s_prev)
    v_new = uw[..., :DH] - ws[:, :CH]
    o = ws[:, CH:] + bnn(attn, v_new)
    s_new = s_prev * jnp.exp(g_last) + btn(k * jnp.exp(g_last - g), v_new)
    return o, s_new, tinv


def s5_chunk(u, xp_re, xp_im, bb_re, bb_im, cc_re, cc_im, p0r, p0i, p1r, p1i, pir, pii, dsk):
    nb = u.shape[0]
    u2 = u.reshape(nb * CH, LANES)
    bu_re = bdot(u2, bb_re).reshape(nb, CH, 512)
    bu_im = bdot(u2, bb_im).reshape(nb, CH, 512)
    xt_re = pir * bu_re - pii * bu_im
    xt_im = pir * bu_im + pii * bu_re
    tri = jnp.broadcast_to((_iota2((1, CH, CH), 1) >= _iota2((1, CH, CH), 2)).astype(F32), (nb, CH, CH))
    cs_re = hnn(tri, xt_re)
    cs_im = hnn(tri, xt_im)
    x_re = p0r * cs_re - p0i * cs_im + p1r * xp_re - p1i * xp_im
    x_im = p0r * cs_im + p0i * cs_re + p1r * xp_im + p1i * xp_re
    y = bdot_nt(x_re.reshape(nb * CH, 512), cc_re) - bdot_nt(x_im.reshape(nb * CH, 512), cc_im) + dsk * u2
    return y.reshape(nb, CH, LANES), x_re[:, CH - 1:CH], x_im[:, CH - 1:CH]


def s5_tables(lam_re, lam_im, log_step, bre, bim, cre, cim):
    expand = (_iota2((S5G, S5N), 1) // S5P == _iota2((S5G, S5N), 0)).astype(F32)
    step = _dot_hi(jnp.exp(log_step), expand)
    lre = jnp.minimum(lam_re, -1e-4)
    lr = lre * step
    ang = lam_im * step
    mag = jnp.exp(lr)
    lb_re = mag * jnp.cos(ang)
    lb_im = mag * jnp.sin(ang)
    den = lre * lre + lam_im * lam_im
    coef_re = ((lb_re - 1.0) * lre + lb_im * lam_im) / den
    coef_im = (lb_im * lre - (lb_re - 1.0) * lam_im) / den
    bb_re = coef_re * bre - coef_im * bim
    bb_im = coef_re * bim + coef_im * bre
    j = _iota2((CH, 1), 0).astype(F32)
    e0 = jnp.exp(j * lr)
    e1 = jnp.exp((j + 1.0) * lr)
    ei = jnp.exp(-j * lr)
    mask = (_iota2((LANES, 512), 0) // S5C == _iota2((LANES, 512), 1) // S5P).astype(F32)

    def blocks(t):
        return jnp.concatenate([(jnp.tile(t[:, gb * 512:(gb + 1) * 512], (LANES // S5C, 1)) * mask)[None]
                                for gb in range(GB)], axis=0)

    return (blocks(bb_re), blocks(bb_im), blocks(cre), blocks(cim),
            e0 * jnp.cos(j * ang), e0 * jnp.sin(j * ang),
            e1 * jnp.cos((j + 1.0) * ang), e1 * jnp.sin((j + 1.0) * ang),
            ei * jnp.cos(j * ang), -ei * jnp.sin(j * ang))


def _row_specs(tiled, batch, bcast, tm, tpb):
    specs = [pl.BlockSpec((tm, a.shape[1]), lambda i: (i, 0)) for a in tiled]
    specs += [pl.BlockSpec((None,) + a.shape[1:], lambda i: (i // tpb, 0, 0)) for a in batch]
    specs += [pl.BlockSpec(a.shape, lambda i, nd=a.ndim: (0,) * nd) for a in bcast]
    return specs


def ew_call(name, fn, tiled, batch, bcast, outs, tm, seq):
    t_rows = tiled[0].shape[0]
    n_in = len(tiled) + len(batch) + len(bcast)

    def body(*refs):
        vals = [r[...].astype(F32) for r in refs[:n_in]]
        for r, o in zip(refs[n_in:], fn(*vals)):
            r[...] = o.astype(r.dtype)

    return pl.pallas_call(
        body, grid=(t_rows // tm,), in_specs=_row_specs(tiled, batch, bcast, tm, seq // tm),
        out_specs=[pl.BlockSpec((tm, w), lambda i: (i, 0)) for w, _ in outs],
        out_shape=[SDS((t_rows, w), dt) for w, dt in outs], name=name, compiler_params=_cp(1))(*tiled, *batch, *bcast)


def ew_vjp_call(name, fn, tiled, batch, bcast, cts, want, tm, seq, addend=None):
    t_rows = tiled[0].shape[0]
    tpb = seq // tm
    n_t, n_b, n_c = len(tiled), len(batch), len(bcast)
    n_in = n_t + n_b + n_c
    extra = [] if addend is None else [addend]

    def body(*refs):
        i = pl.program_id(0)
        vals = [r[...].astype(F32) for r in refs[:n_in]]
        ctv = tuple(r[...].astype(F32) for r in refs[n_in:n_in + len(cts)])
        outs = refs[n_in + len(cts) + len(extra):]
        _, vjp = jax.vjp(fn, *vals)
        grads = vjp(ctv)
        for k, (r, (idx, _)) in enumerate(zip(outs[:len(want)], want)):
            g = grads[idx]
            if k == 0 and extra:
                g = g + refs[n_in + len(cts)][...]
            r[...] = g.astype(r.dtype)
        for k in range(n_b):
            r, g = outs[len(want) + k], grads[n_t + k]

            @pl.when(i % tpb == 0)
            def _(r=r, g=g):
                r[...] = g

            @pl.when(i % tpb != 0)
            def _(r=r, g=g):
                r[...] += g
        for k in range(n_c):
            r, g = outs[len(want) + n_b + k], grads[n_t + n_b + k]

            @pl.when(i == 0)
            def _(r=r, g=g):
                r[...] = g

            @pl.when(i != 0)
            def _(r=r, g=g):
                r[...] += g

    out_specs = [pl.BlockSpec((tm, tiled[idx].shape[1]), lambda i: (i, 0)) for idx, _ in want]
    out_specs += [pl.BlockSpec((None,) + a.shape[1:], lambda i: (i // tpb, 0, 0)) for a in batch]
    out_specs += [pl.BlockSpec(a.shape, lambda i, nd=a.ndim: (0,) * nd) for a in bcast]
    out_shape = [SDS(tiled[idx].shape, dt) for idx, dt in want]
    out_shape += [SDS(a.shape, F32) for a in batch] + [SDS(a.shape, F32) for a in bcast]
    res = pl.pallas_call(
        body, grid=(t_rows // tm,),
        in_specs=_row_specs(tiled, batch, bcast, tm, tpb)
        + [pl.BlockSpec((tm, a.shape[1]), lambda i: (i, 0)) for a in list(cts) + extra],
        out_specs=out_specs, out_shape=out_shape, name=name, compiler_params=_cp(1))(*tiled, *batch, *bcast, *cts, *extra)
    return res[:len(want)], res[len(want):len(want) + n_b], res[len(want) + n_b:]


def _pick(n, cands):
    for c in cands:
        if n % c == 0:
            return c
    return n


def mm(name, pairs, nt, out_dtype):
    m = pairs[0][0].shape[0]
    n = pairs[0][1].shape[0 if nt else 1]
    k_total = sum(a.shape[1] for a, _ in pairs)
    tm = _pick(m, (1024, 512, 256, 128) if k_total <= 2048 else (512, 256, 128))
    tn = _pick(n, (512, 256, 128))
    np_ = len(pairs)

    def body(*refs):
        acc = None
        for p in range(np_):
            a, b = refs[2 * p][...], refs[2 * p + 1][...]
            t = _dot_nt(a, b) if nt else _dot(a, b)
            acc = t if acc is None else acc + t
        refs[2 * np_][...] = acc.astype(out_dtype)

    in_specs, ops = [], []
    for a, b in pairs:
        k = a.shape[1]
        in_specs.append(pl.BlockSpec((tm, k), lambda i, j: (i, 0)))
        in_specs.append(pl.BlockSpec((tn, k), lambda i, j: (j, 0)) if nt else pl.BlockSpec((k, tn), lambda i, j: (0, j)))
        ops += [a, b]
    return pl.pallas_call(
        body, grid=(m // tm, n // tn), in_specs=in_specs, out_specs=pl.BlockSpec((tm, tn), lambda i, j: (i, j)),
        out_shape=SDS((m, n), out_dtype), name=name, compiler_params=_cp(2))(*ops)


def mm_tn(name, a, b, exchange=None):
    t_rows, m = a.shape
    n = b.shape[1]
    tn = n if n <= 1024 else _pick(n, (1024, 512, 256, 128))
    tm = max([t for t in range(LANES, m + 1, LANES) if m % t == 0 and t * tn * 4 <= 6 * 1024 * 1024] or [m])
    tk = _pick(t_rows, (512, 256, 128, 64))
    grid = (m // tm, n // tn, t_rows // tk)
    extra = [] if exchange is None else [exchange]

    def body(*refs):
        a_ref, b_ref = refs[:2]
        o_ref, acc = refs[2 + len(extra)], refs[3 + 2 * len(extra)]
        i, j, k = pl.program_id(0), pl.program_id(1), pl.program_id(2)
        if extra:
            start, finish = _exchange_phases(refs[2], refs[4], *refs[6:9])
            pl.when((i == 0) & (j == 0) & (k == 0))(start)

        @pl.when(k == 0)
        def _():
            acc[...] = jnp.zeros_like(acc)

        acc[...] += _dot_tn(a_ref[...], b_ref[...])

        @pl.when(k == grid[2] - 1)
        def _():
            o_ref[...] = acc[...].astype(BF16)

        if extra:
            pl.when((i == grid[0] - 1) & (j == grid[1] - 1) & (k == grid[2] - 1))(finish)

    res = pl.pallas_call(
        body, grid=grid,
        in_specs=[pl.BlockSpec((tk, tm), lambda i, j, k: (k, i)), pl.BlockSpec((tk, tn), lambda i, j, k: (k, j))]
        + [HBM_SPEC] * len(extra),
        out_specs=[pl.BlockSpec((tm, tn), lambda i, j, k: (i, j))] + [HBM_SPEC] * len(extra),
        out_shape=[SDS((m, n), BF16)] + [SDS(x.shape, x.dtype) for x in extra],
        scratch_shapes=[pltpu.VMEM((tm, tn), F32)] + (_comm_scratch() if extra else []), name=name,
        compiler_params=_cp(3))(a, b, *extra)
    return res if extra else res[0]


def ffn_fwd(name, h, mod3, g, w1, w3, w2, seq, gather=None):
    t_rows = h.shape[0]
    tm = _pick(seq, (512, 256, 128, 64))
    tf = FFN_TF
    tpb = seq // tm
    nf = FF // tf
    nt = t_rows // tm
    extra = [] if gather is None else [gather]

    def body(*refs):
        h_ref, mod_ref, g_ref, w1_ref, w3_ref, w2_ref = refs[:6]
        ho_ref, f_ref, u_ref = refs[6 + len(extra):9 + len(extra)]
        acc = refs[9 + 2 * len(extra)]
        i, j = pl.program_id(0), pl.program_id(1)
        if extra:
            start, forward, finish = _gather_phases(refs[6], refs[10], *refs[12:15])
            pl.when((i == 0) & (j == 0))(start)
            pl.when((i == nt - 1) & (j == 0))(forward)

        @pl.when(j == 0)
        def _():
            u_ref[...] = normmod(h_ref[...], g_ref[...], mod_ref[1:2, :], mod_ref[0:1, :]).astype(BF16)
            acc[...] = jnp.zeros_like(acc)

        u = u_ref[...]
        a = _silu(_dot_nt(u, w1_ref[...])) * _dot_nt(u, w3_ref[...])
        acc[...] += _dot(a, w2_ref[...])

        @pl.when(j == nf - 1)
        def _():
            f_ref[...] = acc[...]
            ho_ref[...] = h_ref[...] + 0.5 * mod_ref[2:3, :] * acc[...]

        if extra:
            pl.when((i == nt - 1) & (j == nf - 1))(finish)

    row = lambda i, j: (i, 0)
    return pl.pallas_call(
        body, grid=(nt, nf),
        in_specs=[pl.BlockSpec((tm, D), row), pl.BlockSpec((None, 3, D), lambda i, j: (i // tpb, 0, 0)),
                  pl.BlockSpec((1, D), lambda i, j: (0, 0)), pl.BlockSpec((tf, D), lambda i, j: (j, 0)),
                  pl.BlockSpec((tf, D), lambda i, j: (j, 0)), pl.BlockSpec((tf, D), lambda i, j: (j, 0))]
        + [HBM_SPEC] * len(extra),
        out_specs=[pl.BlockSpec((tm, D), row), pl.BlockSpec((tm, D), row), pl.BlockSpec((tm, D), row)]
        + [HBM_SPEC] * len(extra),
        out_shape=[SDS((t_rows, D), F32), SDS((t_rows, D), F32), SDS((t_rows, D), BF16)]
        + [SDS((NDEV,) + x.shape, x.dtype) for x in extra],
        scratch_shapes=[pltpu.VMEM((tm, D), F32)] + (_comm_scratch() if extra else []), name=name,
        compiler_params=_cp(2))(h, mod3, g, w1, w3, w2, *extra)


def ffn_bwd(name, dho, h, f_out, u, mod3, g, w1, w3, w2, seq, exchange=None):
    t_rows = h.shape[0]
    tm = _pick(seq, (256, 128, 64))
    tf = FFN_TF
    tpb = seq // tm
    nf = FF // tf
    nt = t_rows // tm
    extra = [] if exchange is None else [exchange]

    def body(*refs):
        dho_ref, h_ref, f_ref, u_ref, mod_ref, g_ref, w1_ref, w3_ref, w2_ref = refs[:9]
        dh_ref, a_ref, dh1_ref, dh3_ref, df_scr, dmod_ref, dg_ref = refs[9 + len(extra):16 + len(extra)]
        du_acc = refs[16 + 2 * len(extra)]
        i, j = pl.program_id(0), pl.program_id(1)
        if extra:
            start, finish = _exchange_phases(refs[9], refs[17], *refs[19:22])
            pl.when((i == 0) & (j == 0))(start)

        @pl.when(j == 0)
        def _():
            df_scr[...] = (0.5 * mod_ref[2:3, :] * dho_ref[...]).astype(BF16)
            du_acc[...] = jnp.zeros_like(du_acc)

        uu = u_ref[...]
        h1 = _dot_nt(uu, w1_ref[...])
        h3 = _dot_nt(uu, w3_ref[...])
        sg = jax.nn.sigmoid(h1)
        s = h1 * sg
        da = _dot_nt(df_scr[...], w2_ref[...])
        dh3 = (da * s).astype(BF16)
        dh1 = (da * h3 * (sg * (1.0 + h1 * (1.0 - sg)))).astype(BF16)
        a_ref[...] = (s * h3).astype(BF16)
        dh1_ref[...] = dh1
        dh3_ref[...] = dh3
        du_acc[...] += _dot(dh1, w1_ref[...]) + _dot(dh3, w3_ref[...])

        @pl.when(j == nf - 1)
        def _():
            _, vjp = jax.vjp(normmod, h_ref[...], g_ref[...], mod_ref[1:2, :], mod_ref[0:1, :])
            dh_n, dg, dsc, dsh = vjp(du_acc[...])
            dh_ref[...] = dho_ref[...] + dh_n
            dgt = jnp.sum(0.5 * dho_ref[...] * f_ref[...], axis=0, keepdims=True)
            dmod = jnp.concatenate([dsh, dsc, dgt], axis=0)

            @pl.when(i % tpb == 0)
            def _():
                dmod_ref[...] = dmod

            @pl.when(i % tpb != 0)
            def _():
                dmod_ref[...] += dmod

            @pl.when(i == 0)
            def _():
                dg_ref[...] = dg

            @pl.when(i != 0)
            def _():
                dg_ref[...] += dg

        if extra:
            pl.when((i == nt - 1) & (j == nf - 1))(finish)

    row = lambda i, j: (i, 0)
    col = lambda i, j: (i, j)
    return pl.pallas_call(
        body, grid=(nt, nf),
        in_specs=[pl.BlockSpec((tm, D), row), pl.BlockSpec((tm, D), row), pl.BlockSpec((tm, D), row),
                  pl.BlockSpec((tm, D), row), pl.BlockSpec((None, 3, D), lambda i, j: (i // tpb, 0, 0)),
                  pl.BlockSpec((1, D), lambda i, j: (0, 0)), pl.BlockSpec((tf, D), lambda i, j: (j, 0)),
                  pl.BlockSpec((tf, D), lambda i, j: (j, 0)), pl.BlockSpec((tf, D), lambda i, j: (j, 0))]
        + [HBM_SPEC] * len(extra),
        out_specs=[pl.BlockSpec((tm, D), row), pl.BlockSpec((tm, tf), col), pl.BlockSpec((tm, tf), col),
                   pl.BlockSpec((tm, tf), col), pl.BlockSpec((tm, D), row),
                   pl.BlockSpec((None, 3, D), lambda i, j: (i // tpb, 0, 0)), pl.BlockSpec((1, D), lambda i, j: (0, 0))]
        + [HBM_SPEC] * len(extra),
        out_shape=[SDS((t_rows, D), F32), SDS((t_rows, FF), BF16), SDS((t_rows, FF), BF16), SDS((t_rows, FF), BF16),
                   SDS((t_rows, D), BF16), SDS(mod3.shape, F32), SDS((1, D), F32)] + [SDS(x.shape, x.dtype) for x in extra],
        scratch_shapes=[pltpu.VMEM((tm, D), F32)] + (_comm_scratch() if extra else []), name=name,
        compiler_params=_cp(2))(dho, h, f_out, u, mod3, g, w1, w3, w2, *extra)


def _dn_cols(part, hd):
    return slice(part * DNW + hd * DH, part * DNW + (hd + 1) * DH)


def _dn_stacks(raw_ref, halo_ref, conv_ref, hm, nb):
    pairs = [(b, hd) for b in range(nb) for hd in range(NH)]
    xs = [jnp.stack([jnp.concatenate([halo_ref[b, :, _dn_cols(part, hd)] * hm, raw_ref[b, :, _dn_cols(part, hd)]], axis=0)
                     for b, hd in pairs]) for part in range(3)]
    ws = [jnp.stack([conv_ref[0:CONVW, _dn_cols(part, hd)] for b, hd in pairs]) for part in range(3)]
    return xs, ws


def _gate_stacks(gates, nb):
    pairs = [(b, hd) for b in range(nb) for hd in range(NH)]
    bs = jnp.stack([gates[b][0][:, hd:hd + 1] for b, hd in pairs])
    gs = jnp.stack([gates[b][1][:, NH + hd:NH + hd + 1] for b, hd in pairs])
    gts = jnp.stack([gates[b][2][NH + hd:NH + hd + 1, :] for b, hd in pairs])
    return bs, gs, gts


def deltanet_fwd(p_dn, p_small, conv8, alp, dtp, nb):
    bl, seq, _ = p_dn.shape
    nc = seq // CH
    ng = nb * NH

    def body(raw_ref, halo_ref, small_ref, conv_ref, alp_ref, dtp_ref, o_ref, sprev_ref, tinv_ref, s_scr):
        n = pl.program_id(1)

        @pl.when(n == 0)
        def _():
            s_scr[...] = jnp.zeros_like(s_scr)

        hm = (n > 0).astype(F32)
        gates = [gate_fn(small_ref[b], alp_ref[...], dtp_ref[...]) for b in range(nb)]
        xs, ws = _dn_stacks(raw_ref, halo_ref, conv_ref, hm, nb)
        s_prev = s_scr[...]
        o, s_new, tinv = dn_chunk(*xs, *ws, *_gate_stacks(gates, nb), s_prev)
        sprev_ref[...] = s_prev
        tinv_ref[...] = tinv
        s_scr[...] = s_new
        for b in range(nb):
            for hd in range(NH):
                o_ref[b, :, hd * DH:(hd + 1) * DH] = o[b * NH + hd]

    blk = lambda bb, n: (bb, n, 0)
    const = lambda bb, n: (0, 0)
    saved = pl.BlockSpec((None, ng, DH, DH), lambda bb, n: (bb * nc + n, 0, 0, 0))
    return pl.pallas_call(
        body, grid=(bl // nb, nc),
        in_specs=[pl.BlockSpec((nb, CH, 4 * DNW), blk),
                  pl.BlockSpec((nb, 8, 3 * DNW), lambda bb, n: (bb, jnp.maximum(n * (CH // 8) - 1, 0), 0)),
                  pl.BlockSpec((nb, CH, LANES), blk), pl.BlockSpec((8, 3 * DNW), const), pl.BlockSpec((1, LANES), const),
                  pl.BlockSpec((1, LANES), const)],
        out_specs=[pl.BlockSpec((nb, CH, DNW), blk), saved, saved],
        out_shape=[SDS((bl, seq, DNW), F32), SDS((bl // nb * nc, ng, DH, DH), F32), SDS((bl // nb * nc, ng, DH, DH), F32)],
        scratch_shapes=[pltpu.VMEM((ng, DH, DH), F32)], name="deltanet_fwd",
        compiler_params=_cp(2))(p_dn, p_dn, p_small, conv8, alp, dtp)


def deltanet_bwd(p_dn, p_small, conv8, alp, dtp, sprev, tinv, d_o, d_z, nb, exchange=None):
    bl, seq, _ = p_dn.shape
    nc = seq // CH
    ng = nb * NH
    extra = [] if exchange is None else [exchange]

    def body(*refs):
        raw_ref, halo_ref, small_ref, conv_ref, alp_ref, dtp_ref, sprev_ref, tinv_ref, do_ref, dz_ref = refs[:10]
        draw_ref, dsmall_ref, dconv_ref, dalp_ref, ddtp_ref = refs[10 + len(extra):15 + len(extra)]
        ds_scr, dhalo_scr = refs[15 + 2 * len(extra):17 + 2 * len(extra)]
        bb, r = pl.program_id(0), pl.program_id(1)
        n = nc - 1 - r
        if extra:
            start, finish = _exchange_phases(refs[10], refs[16], *refs[19:22])
            pl.when((bb == 0) & (r == 0))(start)

        @pl.when((bb == 0) & (r == 0))
        def _():
            dconv_ref[...] = jnp.zeros_like(dconv_ref)
            dalp_ref[...] = jnp.zeros_like(dalp_ref)
            ddtp_ref[...] = jnp.zeros_like(ddtp_ref)

        @pl.when(r == 0)
        def _():
            ds_scr[...] = jnp.zeros_like(ds_scr)
            dhalo_scr[...] = jnp.zeros_like(dhalo_scr)

        hm = (n > 0).astype(F32)
        gates, gate_vjps = [], []
        for b in range(nb):
            out, gvjp = jax.vjp(gate_fn, small_ref[b], alp_ref[...], dtp_ref[...])
            gates.append(out)
            gate_vjps.append(gvjp)
        xs, ws = _dn_stacks(raw_ref, halo_ref, conv_ref, hm, nb)
        t_saved = tinv_ref[...]
        _, vjp = jax.vjp(lambda *args: dn_chunk(*args, t_saved)[:2], *xs, *ws, *_gate_stacks(gates, nb), sprev_ref[...])
        d_out = jnp.stack([do_ref[b, :, hd * DH:(hd + 1) * DH] for b in range(nb) for hd in range(NH)])
        grads = vjp((d_out, ds_scr[...]))
        ds_scr[...] = grads[9]
        lane = _iota2((CH, LANES), 1)
        rowi = _iota2((LANES, CH), 0)
        for b in range(nb):
            d_beta = jnp.zeros((CH, LANES), F32)
            d_gc = jnp.zeros((CH, LANES), F32)
            d_gct = jnp.zeros((LANES, CH), F32)
            for hd in range(NH):
                i = b * NH + hd
                for part in range(3):
                    cols = _dn_cols(part, hd)
                    dx = grads[part][i]
                    tail = dx[CH:CH + 8] + dhalo_scr[b, :, cols]
                    draw_ref[b, :, cols] = jnp.concatenate([dx[8:CH], tail], axis=0).astype(BF16)
                    dhalo_scr[b, :, cols] = dx[0:8] * hm
                d_beta = d_beta + jnp.where(lane == hd, grads[6][i], 0.0)
                d_gc = d_gc + jnp.where(lane == NH + hd, grads[7][i], 0.0)
                d_gct = d_gct + jnp.where(rowi == NH + hd, grads[8][i], 0.0)
            d_small, d_alp, d_dtp = gate_vjps[b]((d_beta, d_gc, d_gct))
            dsmall_ref[b] = d_small.astype(BF16)
            dalp_ref[...] += d_alp
            ddtp_ref[...] += d_dtp
            draw_ref[b, :, 3 * DNW:4 * DNW] = dz_ref[b].astype(BF16)
        for hd in range(NH):
            for part in range(3):
                dw = grads[3 + part][hd]
                for b in range(1, nb):
                    dw = dw + grads[3 + part][b * NH + hd]
                dconv_ref[0:CONVW, _dn_cols(part, hd)] += dw
        if extra:
            pl.when((bb == bl // nb - 1) & (r == nc - 1))(finish)

    blk = lambda bb, r: (bb, nc - 1 - r, 0)
    const = lambda bb, r: (0, 0)
    saved = pl.BlockSpec((None, ng, DH, DH), lambda bb, r: (bb * nc + nc - 1 - r, 0, 0, 0))
    return pl.pallas_call(
        body, grid=(bl // nb, nc),
        in_specs=[pl.BlockSpec((nb, CH, 4 * DNW), blk),
                  pl.BlockSpec((nb, 8, 3 * DNW), lambda bb, r: (bb, jnp.maximum((nc - 1 - r) * (CH // 8) - 1, 0), 0)),
                  pl.BlockSpec((nb, CH, LANES), blk), pl.BlockSpec((8, 3 * DNW), const), pl.BlockSpec((1, LANES), const),
                  pl.BlockSpec((1, LANES), const), saved, saved,
                  pl.BlockSpec((nb, CH, DNW), blk), pl.BlockSpec((nb, CH, DNW), blk)] + [HBM_SPEC] * len(extra),
        out_specs=[pl.BlockSpec((nb, CH, 4 * DNW), blk), pl.BlockSpec((nb, CH, LANES), blk),
                   pl.BlockSpec((8, 3 * DNW), const), pl.BlockSpec((1, LANES), const), pl.BlockSpec((1, LANES), const)]
        + [HBM_SPEC] * len(extra),
        out_shape=[SDS((bl, seq, 4 * DNW), BF16), SDS((bl, seq, LANES), BF16), SDS((8, 3 * DNW), F32), SDS((1, LANES), F32),
                   SDS((1, LANES), F32)] + [SDS(x.shape, x.dtype) for x in extra],
        scratch_shapes=[pltpu.VMEM((ng, DH, DH), F32), pltpu.VMEM((nb, 8, 3 * DNW), F32)]
        + (_comm_scratch() if extra else []), name="deltanet_bwd",
        compiler_params=_cp(2))(p_dn, p_dn, p_small, conv8, alp, dtp, sprev, tinv, d_o, d_z, *extra)


def _s5_table_specs():
    tab3 = pl.BlockSpec((None, LANES, 512), lambda gb, n: (gb, 0, 0))
    tab2 = pl.BlockSpec((CH, 512), lambda gb, n: (0, gb))
    return [tab3] * 4 + [tab2] * 6 + [pl.BlockSpec((1, LANES), lambda gb, n: (0, gb))]


def s5_fwd(u, tables, dsk):
    bl, seq, _ = u.shape
    nc = seq // CH

    def body(u_ref, *rest):
        tabs, (y_ref, xs_ref, xr_scr, xi_scr) = rest[:11], rest[11:]

        @pl.when(pl.program_id(1) == 0)
        def _():
            xr_scr[...] = jnp.zeros_like(xr_scr)
            xi_scr[...] = jnp.zeros_like(xi_scr)

        xp_re, xp_im = xr_scr[...], xi_scr[...]
        xs_ref[0:bl] = xp_re
        xs_ref[bl:2 * bl] = xp_im
        y, xn_re, xn_im = s5_chunk(u_ref[...], xp_re, xp_im, *[t[...] for t in tabs])
        y_ref[...] = y
        xr_scr[...] = xn_re
        xi_scr[...] = xn_im

    blk = lambda gb, n: (0, n, gb)
    return pl.pallas_call(
        body, grid=(GB, nc), in_specs=[pl.BlockSpec((bl, CH, LANES), blk)] + _s5_table_specs(),
        out_specs=[pl.BlockSpec((bl, CH, LANES), blk),
                   pl.BlockSpec((None, 2 * bl, 1, 512), lambda gb, n: (gb * nc + n, 0, 0, 0))],
        out_shape=[SDS((bl, seq, S5W), F32), SDS((GB * nc, 2 * bl, 1, 512), F32)],
        scratch_shapes=[pltpu.VMEM((bl, 1, 512), F32), pltpu.VMEM((bl, 1, 512), F32)], name="s5_fwd",
        compiler_params=_cp(2))(u, *tables, dsk)


def s5_bwd(u, tables, dsk, xs, dy):
    bl, seq, _ = u.shape
    nc = seq // CH

    def body(u_ref, *rest):
        tabs, xs_ref, dy_ref = rest[:11], rest[11], rest[12]
        du_ref, dtabs, dxr_scr, dxi_scr = rest[13], rest[14:25], rest[25], rest[26]
        r = pl.program_id(1)

        @pl.when(r == 0)
        def _():
            for t in dtabs:
                t[...] = jnp.zeros_like(t)
            dxr_scr[...] = jnp.zeros_like(dxr_scr)
            dxi_scr[...] = jnp.zeros_like(dxi_scr)

        _, vjp = jax.vjp(s5_chunk, u_ref[...], xs_ref[0:bl], xs_ref[bl:2 * bl], *[t[...] for t in tabs])
        grads = vjp((dy_ref[...], dxr_scr[...], dxi_scr[...]))
        du_ref[...] = grads[0].astype(BF16)
        dxr_scr[...] = grads[1]
        dxi_scr[...] = grads[2]
        for t, g in zip(dtabs, grads[3:]):
            t[...] += g

    blk = lambda gb, r: (0, nc - 1 - r, gb)
    tab_shapes = [SDS(t.shape, F32) for t in tables] + [SDS(dsk.shape, F32)]
    return pl.pallas_call(
        body, grid=(GB, nc),
        in_specs=[pl.BlockSpec((bl, CH, LANES), blk)] + _s5_table_specs()
        + [pl.BlockSpec((None, 2 * bl, 1, 512), lambda gb, r: (gb * nc + nc - 1 - r, 0, 0, 0)), pl.BlockSpec((bl, CH, LANES), blk)],
        out_specs=[pl.BlockSpec((bl, CH, LANES), blk)] + _s5_table_specs(),
        out_shape=[SDS((bl, seq, S5W), BF16)] + tab_shapes,
        scratch_shapes=[pltpu.VMEM((bl, 1, 512), F32), pltpu.VMEM((bl, 1, 512), F32)], name="s5_bwd",
        compiler_params=_cp(2))(u, *tables, dsk, xs, dy)


def s5_tables_fwd(params):
    shapes = [SDS((GB, LANES, 512), F32)] * 4 + [SDS((CH, S5N), F32)] * 6

    def body(*refs):
        for r, t in zip(refs[7:], s5_tables(*[p[...] for p in refs[:7]])):
            r[...] = t

    return pl.pallas_call(body, out_shape=shapes, name="s5_tables_fwd", compiler_params=_cp())(*params)


def s5_tables_bwd(params, dtables):
    def body(*refs):
        _, vjp = jax.vjp(s5_tables, *[p[...] for p in refs[:7]])
        for r, g in zip(refs[17:], vjp(tuple(t[...] for t in refs[7:17]))):
            r[...] = g

    return pl.pallas_call(body, out_shape=[SDS(p.shape, F32) for p in params], name="s5_tables_bwd",
                          compiler_params=_cp())(*params, *dtables)


def ada_fwd(c_all, w_loc, b_loc):
    def body(c_ref, w_ref, b_ref, o_ref):
        o_ref[...] = _dot(_silu(c_ref[...]), w_ref[...]) + b_ref[...]

    return pl.pallas_call(body, out_shape=SDS((c_all.shape[0], w_loc.shape[1]), F32), name="ada_fwd",
                          compiler_params=_cp())(c_all, w_loc, b_loc)


def ada_bwd(c_all, dmod_mine, dmod_all):
    def body(c_ref, dm_ref, da_ref, gw_ref, gb_ref):
        gw_ref[...] = _dot_tn(_silu(c_ref[...]), dm_ref[...])
        gb_ref[...] = jnp.sum(da_ref[...], axis=0, keepdims=True)

    return pl.pallas_call(body, out_shape=[SDS((D, dmod_mine.shape[1]), F32), SDS((1, dmod_all.shape[1]), F32)],
                          name="ada_bwd", compiler_params=_cp())(c_all, dmod_mine, dmod_all)


def loss_head(h, tgt, g, seq):
    t_rows = h.shape[0]
    tm = _pick(seq, (256, 128, 64))

    def body(h_ref, t_ref, g_ref, dh_ref, dg_ref, loss_ref):
        i = pl.program_id(0)
        y, vjp = jax.vjp(lambda hh, gg: hh * lax.rsqrt(jnp.mean(hh * hh, axis=-1, keepdims=True) + EPS) * gg,
                         h_ref[...], g_ref[...])
        e = y - t_ref[...]
        dh, dg = vjp(e * (1.0 / D))
        part = jnp.sum(jnp.sum(e * e, axis=1, keepdims=True), axis=0, keepdims=True) * (0.5 / D) + jnp.zeros((1, LANES), F32)
        dh_ref[...] = dh

        @pl.when(i == 0)
        def _():
            dg_ref[...] = dg
            loss_ref[...] = part

        @pl.when(i != 0)
        def _():
            dg_ref[...] += dg
            loss_ref[...] += part

    row = lambda i: (i, 0)
    const = lambda i: (0, 0)
    return pl.pallas_call(
        body, grid=(t_rows // tm,),
        in_specs=[pl.BlockSpec((tm, D), row), pl.BlockSpec((tm, D), row), pl.BlockSpec((1, D), const)],
        out_specs=[pl.BlockSpec((tm, D), row), pl.BlockSpec((1, D), const), pl.BlockSpec((1, LANES), const)],
        out_shape=[SDS((t_rows, D), F32), SDS((1, D), F32), SDS((1, LANES), F32)], name="loss_head",
        compiler_params=_cp(1))(h, tgt, g)


def adamw(name, parts, w, m, v):
    k_parts, rows, cols = parts.shape
    tr = _pick(rows, (256, 128, 64, 32, 16, 8))

    def body(p_ref, w_ref, m_ref, v_ref, g_ref, d_ref, mo_ref, vo_ref):
        g = p_ref[0].astype(F32)
        for k in range(1, k_parts):
            g = g + p_ref[k].astype(F32)
        _adam_store(g, w_ref, m_ref, v_ref, g_ref, d_ref, mo_ref, vo_ref)

    blk = pl.BlockSpec((tr, cols), lambda i: (i, 0))
    return pl.pallas_call(
        body, grid=(rows // tr,), in_specs=[pl.BlockSpec((k_parts, tr, cols), lambda i: (0, i, 0)), blk, blk, blk],
        out_specs=[blk] * 4, out_shape=[SDS((rows, cols), F32)] * 4, name=name, compiler_params=_cp(1))(parts, w, m, v)


def _adam_store(g, w_ref, m_ref, v_ref, g_ref, d_ref, mo_ref, vo_ref):
    m_new = ADAM_B1 * m_ref[...] + (1.0 - ADAM_B1) * g
    v_new = ADAM_B2 * v_ref[...] + (1.0 - ADAM_B2) * (g * g)
    m_hat = m_new / (1.0 - ADAM_B1 ** ADAM_STEP)
    v_hat = v_new / (1.0 - ADAM_B2 ** ADAM_STEP)
    g_ref[...] = g
    d_ref[...] = -ADAM_LR * (m_hat / (jnp.sqrt(v_hat) + ADAM_EPS) + ADAM_WD * w_ref[...])
    mo_ref[...] = m_new
    vo_ref[...] = v_new


def adamw_t(name, parts, w, m, v):
    k_parts, r, c = parts.shape
    tc = _pick(c, (256, 128))

    def body(p_ref, w_ref, m_ref, v_ref, g_ref, d_ref, mo_ref, vo_ref):
        gt = p_ref[0].astype(F32)
        for k in range(1, k_parts):
            gt = gt + p_ref[k].astype(F32)
        _adam_store(gt.T, w_ref, m_ref, v_ref, g_ref, d_ref, mo_ref, vo_ref)

    blk = pl.BlockSpec((tc, r), lambda j: (j, 0))
    return pl.pallas_call(
        body, grid=(c // tc,), in_specs=[pl.BlockSpec((k_parts, r, tc), lambda j: (0, 0, j)), blk, blk, blk],
        out_specs=[blk] * 4, out_shape=[SDS((c, r), F32)] * 4, name=name, compiler_params=_cp(1))(parts, w, m, v)


def _comm_scratch():
    return [pltpu.SemaphoreType.DMA((7,)), pltpu.SemaphoreType.DMA((7,)), pltpu.SemaphoreType.DMA]


HBM_SPEC = pl.BlockSpec(memory_space=pl.ANY)


def _gather_phases(x_ref, out_ref, send_sems, recv_sems, local_sem):
    mx, my, mc = lax.axis_index("x"), lax.axis_index("y"), lax.axis_index("c")
    me, sibling = (mx, my, mc), (mx, my, 1 - mc)
    chips = [(1 - mx, my), (mx, 1 - my), (1 - mx, 1 - my)]

    def slot(px, py, pc):
        return out_ref.at[4 * px + 2 * py + pc]

    def copy(k, block, to, src=None):
        return pltpu.make_async_remote_copy(
            src_ref=slot(*block) if src is None else src, dst_ref=slot(*block), send_sem=send_sems.at[k],
            recv_sem=recv_sems.at[k], device_id=to, device_id_type=pl.DeviceIdType.MESH)

    def first():
        return [copy(0, me, sibling, src=x_ref)] + [copy(1 + j, me, (*chip, mc), src=x_ref) for j, chip in enumerate(chips)]

    def passed():
        return [copy(4 + j, (*chip, mc), sibling) for j, chip in enumerate(chips)]

    def start():
        pltpu.make_async_copy(x_ref, slot(*me), local_sem).start()
        for cp in first():
            cp.start()

    def forward():
        for j, chip in enumerate(chips):
            copy(1 + j, (*chip, mc), me).wait_recv()
            passed()[j].start()

    def finish():
        copy(0, sibling, me).wait_recv()
        for j, chip in enumerate(chips):
            copy(4 + j, (*chip, 1 - mc), me).wait_recv()
        for cp in first() + passed():
            cp.wait_send()
        pltpu.make_async_copy(x_ref, slot(*me), local_sem).wait()

    return start, forward, finish


def _exchange_phases(x_ref, out_ref, send_sems, recv_sems, local_sem):
    mx, my, mc = lax.axis_index("x"), lax.axis_index("y"), lax.axis_index("c")
    me = 4 * mx + 2 * my + mc

    def peer(k):
        return mx ^ (k >> 2), my ^ ((k >> 1) & 1), mc ^ (k & 1)

    def sends():
        out = []
        for k in range(1, NDEV):
            px, py, pc = peer(k)
            out.append(pltpu.make_async_remote_copy(
                src_ref=x_ref.at[4 * px + 2 * py + pc], dst_ref=out_ref.at[me], send_sem=send_sems.at[k - 1],
                recv_sem=recv_sems.at[k - 1], device_id=(px, py, pc), device_id_type=pl.DeviceIdType.MESH))
        return out

    def start():
        pltpu.make_async_copy(x_ref.at[me], out_ref.at[me], local_sem).start()
        for cp in sends():
            cp.start()

    def finish():
        for k in range(1, NDEV):
            px, py, pc = peer(k)
            pltpu.make_async_remote_copy(
                src_ref=x_ref.at[me], dst_ref=out_ref.at[4 * px + 2 * py + pc], send_sem=send_sems.at[k - 1],
                recv_sem=recv_sems.at[k - 1], device_id=(px, py, pc), device_id_type=pl.DeviceIdType.MESH).wait_recv()
        for cp in sends():
            cp.wait_send()
        pltpu.make_async_copy(x_ref.at[me], out_ref.at[me], local_sem).wait()

    return start, finish


def all_gather(name, x):
    def body(x_ref, out_ref, send_sems, recv_sems, local_sem):
        for phase in _gather_phases(x_ref, out_ref, send_sems, recv_sems, local_sem):
            phase()

    return pl.pallas_call(body, out_shape=SDS((NDEV,) + x.shape, x.dtype), in_specs=[HBM_SPEC], out_specs=HBM_SPEC,
                          scratch_shapes=_comm_scratch(), name=name)(x)


def all_to_all(name, x):
    def body(x_ref, out_ref, send_sems, recv_sems, local_sem):
        for phase in _exchange_phases(x_ref, out_ref, send_sems, recv_sems, local_sem):
            phase()

    return pl.pallas_call(body, out_shape=SDS(x.shape, x.dtype), in_specs=[HBM_SPEC], out_specs=HBM_SPEC,
                          scratch_shapes=_comm_scratch(), name=name)(x)


def _pack(arrs, dtype, row_mult=8):
    segs = []
    for a in arrs:
        flat = a.reshape(-1).astype(dtype)
        segs.append(jnp.pad(flat, (0, (-flat.shape[0]) % ROW)))
    flat = jnp.concatenate(segs)
    flat = jnp.pad(flat, (0, (-flat.shape[0]) % (ROW * row_mult)))
    return flat.reshape(-1, ROW)


def _unpack(buf, shapes):
    flat = buf.reshape(-1)
    out, off = [], 0
    for s in shapes:
        n = math.prod(s)
        out.append(flat[off:off + n].reshape(s))
        off += n + (-n) % ROW
    return out


def _pack_rows(arrs, axis):
    padded = []
    for t in arrs:
        pad = [(0, 0)] * t.ndim
        pad[axis] = (0, _tile_rows(t.shape[axis]) - t.shape[axis])
        padded.append(jnp.pad(t, pad))
    return jnp.concatenate(padded, axis=axis)


def _tile_rows(r):
    return r + (-r) % 16


def _unpack8(buf, shapes):
    flat = buf.reshape(NDEV, -1)
    out, off = [], 0
    for s in shapes:
        n = math.prod(s)
        out.append(flat[:, off:off + n].reshape((NDEV,) + tuple(s)))
        off += n + (-n) % ROW
    return out


def kernel(x, c, w_ada, b_ada, g_ffn1, w1_ffn1, w3_ffn1, w2_ffn1, g_mix, w_in, conv_qkv, a_log, dt_bias, g_onorm, lam_re, lam_im, log_step, b_re, b_im, c_re, c_im, d_skip, w_glu, b_glu, w_proj_a, w_proj_b, w_out, g_ffn2, w1_ffn2, w3_ffn2, w2_ffn2, g_final, loss_target, m_w_ada, m_b_ada, m_g_ffn1, m_w1_ffn1, m_w3_ffn1, m_w2_ffn1, m_g_mix, m_w_in, m_conv_qkv, m_a_log, m_dt_bias, m_g_onorm, m_lam_re, m_lam_im, m_log_step, m_b_re, m_b_im, m_c_re, m_c_im, m_d_skip, m_w_glu, m_b_glu, m_w_proj_a, m_w_proj_b, m_w_out, m_g_ffn2, m_w1_ffn2, m_w3_ffn2, m_w2_ffn2, m_g_final, v_w_ada, v_b_ada, v_g_ffn1, v_w1_ffn1, v_w3_ffn1, v_w2_ffn1, v_g_mix, v_w_in, v_conv_qkv, v_a_log, v_dt_bias, v_g_onorm, v_lam_re, v_lam_im, v_log_step, v_b_re, v_b_im, v_c_re, v_c_im, v_d_skip, v_w_glu, v_b_glu, v_w_proj_a, v_w_proj_b, v_w_out, v_g_ffn2, v_w1_ffn2, v_w3_ffn2, v_w2_ffn2, v_g_final):
    a = dict(locals())
    bl, seq, _ = x.shape
    t_rows = bl * seq
    nc = seq // CH
    me = 4 * lax.axis_index("x") + 2 * lax.axis_index("y") + lax.axis_index("c")
    tm_ew = _pick(seq, (256, 128, 64))

    sm = all_gather("gather_small", _pack([c, conv_qkv[0]], F32))
    c_loc, conv_loc = _unpack8(sm, [c.shape, conv_qkv.shape[1:]])
    c_all = c_loc.reshape(NDEV * bl, D)
    conv_full = conv_loc.transpose(1, 0, 2).reshape(CONVW, 3 * DNW)
    loc = {n: (a[n][0].T if n in COL_SHARDED else a[n][0]) for n in RS_WEIGHTS}
    wfull, gw, res = {}, {}, {}

    def pack_local(names):
        return _pack_rows([loc[n].astype(BF16).reshape(-1, ROW) for n in names], 0)

    def unpack_full(buf, names):
        r0 = 0
        for n in names:
            r = loc[n].size // ROW
            wfull[n] = buf[:, r0:r0 + r, :].reshape(-1, loc[n].shape[1])
            r0 += _tile_rows(r)

    def pack_grads(names):
        return _pack_rows([gw[n].astype(BF16).reshape(NDEV, -1, ROW) for n in names], 1)

    def update(buf, names):
        r0 = 0
        for n in names:
            r = loc[n].size // ROW
            parts = buf[:, r0:r0 + r, :].reshape((NDEV,) + loc[n].shape)
            r0 += _tile_rows(r)
            step = adamw_t if n in COL_SHARDED else adamw
            out = step("adamw_" + n, parts, a[n][0], a["m_" + n][0], a["v_" + n][0])
            for kind, t in zip(("grad", "delta", "new_m", "new_v"), out):
                res[kind + "_" + n] = t[None]

    unpack_full(all_gather("gather_ffn1", pack_local(G_FFN1)), G_FFN1)

    n_ada = w_ada.shape[2]
    mod_part = ada_fwd(c_all, w_ada[0], lax.dynamic_slice(b_ada, (0, me * n_ada), (1, n_ada)))
    mod_all = all_gather("gather_mod", mod_part).transpose(1, 0, 2).reshape(NDEV * bl, 9 * D)
    mod = lax.dynamic_slice(mod_all, (me * bl, 0), (bl, 9 * D)).reshape(bl, 9, D)
    mods = [mod[:, k:k + 1, :] for k in range(9)]

    h0 = x.reshape(t_rows, D)
    h1, f1, u1, wg_rest = ffn_fwd("ffn1_fwd", h0, mod[:, 0:3, :], g_ffn1, wfull['w1_ffn1'], wfull['w3_ffn1'],
                                  wfull['w2_ffn1'], seq, gather=pack_local(G_MIX + G_FFN2))
    unpack_full(wg_rest, G_MIX + G_FFN2)
    win = wfull['w_in']
    o_small, o_s5, o_gate = 4 * DNW, 4 * DNW + 2 * NH, 4 * DNW + 2 * NH + S5W
    w_dn, w_small = win[:o_small], jnp.pad(win[o_small:o_s5], ((0, LANES - 2 * NH), (0, 0)))
    w_s5, w_gate = win[o_s5:o_gate], win[o_gate:]
    (u2,) = ew_call("mix_norm", fn_normmod, [h1], [mods[3], mods[4]], [g_mix], [(D, BF16)], tm_ew, seq)
    p_dn = mm("proj_dn", [(u2, w_dn)], True, F32)
    p_small = mm("proj_small", [(u2, w_small)], True, F32)
    p_s5 = mm("proj_s5", [(u2, w_s5)], True, F32)
    p_gate = mm("proj_gate", [(u2, w_gate)], True, F32)

    conv8 = jnp.pad(conv_full, ((0, 8 - CONVW), (0, 0)))
    alp = jnp.pad(a_log, ((0, 0), (NH, LANES - 2 * NH)))
    dtp = jnp.pad(dt_bias, ((0, 0), (NH, LANES - 2 * NH)))
    nb_dn = DN_ROWS if bl % DN_ROWS == 0 else 1
    p_dn3, p_small3 = p_dn.reshape(bl, seq, 4 * DNW), p_small.reshape(bl, seq, LANES)
    o_pre3, sprev, tinv = deltanet_fwd(p_dn3, p_small3, conv8, alp, dtp, nb_dn)
    o_pre = o_pre3.reshape(t_rows, DNW)
    z_raw = p_dn[:, 3 * DNW:]
    (oa,) = ew_call("dn_onorm", fn_onorm, [o_pre, z_raw], [], [g_onorm], [(DNW, BF16)], tm_ew, seq)
    ya = mm("proj_a", [(oa, wfull['w_proj_a'])], True, F32)

    s5_params = [lam_re.reshape(1, S5N), lam_im.reshape(1, S5N), log_step,
                 b_re[0].transpose(2, 0, 1).reshape(S5C, S5N), b_im[0].transpose(2, 0, 1).reshape(S5C, S5N),
                 c_re[0].transpose(1, 0, 2).reshape(S5C, S5N), c_im[0].transpose(1, 0, 2).reshape(S5C, S5N)]
    tables = s5_tables_fwd(s5_params)
    p_s53 = p_s5.reshape(bl, seq, S5W)
    y_s53, xs = s5_fwd(p_s53, tables, d_skip)
    y_s5 = y_s53.reshape(t_rows, S5W)
    (ob,) = ew_call("s5_glu", fn_glu, [y_s5], [], [wfull['w_glu'], b_glu], [(S5W, BF16)], tm_ew, seq)
    yb = mm("proj_b", [(ob, wfull['w_proj_b'])], True, F32)

    (merged,) = ew_call("merge", fn_merge, [p_gate, ya, yb], [], [], [(D, BF16)], tm_ew, seq)
    mo = mm("proj_out", [(merged, wfull['w_out'])], False, F32)
    (h2,) = ew_call("mix_resid", lambda p, q, gt: (q + gt * p,), [mo, h1], [mods[5]], [], [(D, F32)], tm_ew, seq)
    h3, f3, u3 = ffn_fwd("ffn2_fwd", h2, mod[:, 6:9, :], g_ffn2, wfull['w1_ffn2'], wfull['w3_ffn2'], wfull['w2_ffn2'], seq)

    dh3, dg_final, loss_part = loss_head(h3, loss_target.reshape(t_rows, D), g_final.reshape(1, D), seq)
    loss = lax.psum(loss_part[0, 0], ("x", "y", "c"))

    dh2, a3, d1_3, d3_3, df3, dmod_c, dg_ffn2 = ffn_bwd("ffn2_bwd", dh3, h2, f3, u3, mod[:, 6:9, :], g_ffn2, wfull['w1_ffn2'],
                                                   wfull['w3_ffn2'], wfull['w2_ffn2'], seq)
    gw['w1_ffn2'] = mm_tn("gw1_ffn2", d1_3, u3)
    gw['w3_ffn2'] = mm_tn("gw3_ffn2", d3_3, u3)
    gw['w2_ffn2'] = mm_tn("gw2_ffn2", a3, df3)

    (dmo,), (dgt2,), _ = ew_vjp_call("mix_resid_bwd", fn_resid, [mo], [mods[5]], [], [dh2], [(0, BF16)], tm_ew, seq)
    gw['w_out'] = mm_tn("gw_out", merged, dmo)
    d_merged = mm("d_merged", [(dmo, wfull['w_out'])], True, F32)
    (d_gate, d_ya, d_yb), _, _ = ew_vjp_call("merge_bwd", fn_merge, [p_gate, ya, yb], [], [], [d_merged],
                                             [(0, BF16), (1, BF16), (2, BF16)], tm_ew, seq)
    gw['w_proj_a'] = mm_tn("gw_proj_a", d_ya, oa)
    gw['w_proj_b'] = mm_tn("gw_proj_b", d_yb, ob)
    d_oa = mm("d_oa", [(d_ya, wfull['w_proj_a'])], False, F32)
    d_ob = mm("d_ob", [(d_yb, wfull['w_proj_b'])], False, F32)

    (d_opre, d_z), _, (dg_onorm,) = ew_vjp_call("dn_onorm_bwd", fn_onorm, [o_pre, z_raw], [], [g_onorm], [d_oa],
                                                [(0, F32), (1, F32)], tm_ew, seq)
    d_pdn3, d_psmall3, d_conv8, d_alp, d_dtp, rs_ffn2 = deltanet_bwd(
        p_dn3, p_small3, conv8, alp, dtp, sprev, tinv, d_opre.reshape(bl, seq, DNW), d_z.reshape(bl, seq, DNW), nb_dn,
        exchange=pack_grads(G_FFN2))
    d_pdn, d_psmall = d_pdn3.reshape(t_rows, 4 * DNW), d_psmall3.reshape(t_rows, LANES)

    (d_ys5,), _, (g_wglu, dg_bglu) = ew_vjp_call("s5_glu_bwd", fn_glu, [y_s5], [], [wfull['w_glu'], b_glu], [d_ob],
                                                 [(0, F32)], tm_ew, seq)
    gw['w_glu'] = g_wglu
    s5_out = s5_bwd(p_s53, tables, d_skip, xs, d_ys5.reshape(bl, seq, S5W))
    d_ps5, d_tables, dg_dskip = s5_out[0].reshape(t_rows, S5W), s5_out[1:11], s5_out[11]
    d_s5p = s5_tables_bwd(s5_params, d_tables)

    d_pdn_b, d_psm_b, d_ps5_b = d_pdn, d_psmall, d_ps5
    gw['w_in'] = jnp.concatenate([mm_tn("gw_dn", d_pdn_b, u2), mm_tn("gw_small", d_psm_b, u2)[:2 * NH],
                                  mm_tn("gw_s5", d_ps5_b, u2), mm_tn("gw_gate", d_gate, u2)], axis=0)
    du2 = mm("d_u2", [(d_pdn_b, w_dn), (d_psm_b, w_small), (d_ps5_b, w_s5), (d_gate, w_gate)], False, F32)
    (dh1,), (dsh2, dsc2), (dg_mix,) = ew_vjp_call("mix_norm_bwd", fn_normmod, [h1], [mods[3], mods[4]], [g_mix], [du2],
                                                  [(0, F32)], tm_ew, seq, addend=dh2)

    dh0, a1, d1_1, d3_1, df1, dmod_a, dg_ffn1, rs_mix = ffn_bwd(
        "ffn1_bwd", dh1, h0, f1, u1, mod[:, 0:3, :], g_ffn1, wfull['w1_ffn1'], wfull['w3_ffn1'], wfull['w2_ffn1'], seq,
        exchange=pack_grads(G_MIX))
    gw['w1_ffn1'] = mm_tn("gw1_ffn1", d1_1, u1)
    gw['w3_ffn1'], rs_w1 = mm_tn("gw3_ffn1", d3_1, u1, exchange=pack_grads(['w1_ffn1']))
    gw['w2_ffn1'], rs_w3 = mm_tn("gw2_ffn1", a1, df1, exchange=pack_grads(['w3_ffn1']))

    update(rs_ffn2, G_FFN2)
    update(rs_mix, G_MIX)
    update(rs_w1, ['w1_ffn1'])
    update(rs_w3, ['w3_ffn1'])
    update(all_to_all("scatter_w2_ffn1", pack_grads(['w2_ffn1'])), ['w2_ffn1'])

    dmod_mine = jnp.concatenate([dmod_a, dsh2, dsc2, dgt2, dmod_c], axis=1).reshape(bl, 9 * D)
    small_grads = {
        'g_ffn1': dg_ffn1, 'g_mix': dg_mix, 'a_log': d_alp[:, NH:2 * NH], 'dt_bias': d_dtp[:, NH:2 * NH],
        'g_onorm': dg_onorm, 'lam_re': d_s5p[0].reshape(1, S5G, S5P), 'lam_im': d_s5p[1].reshape(1, S5G, S5P),
        'log_step': d_s5p[2],
        'b_re': d_s5p[3].reshape(S5C, S5G, S5P).transpose(1, 2, 0)[None],
        'b_im': d_s5p[4].reshape(S5C, S5G, S5P).transpose(1, 2, 0)[None],
        'c_re': d_s5p[5].reshape(S5C, S5G, S5P).transpose(1, 0, 2)[None],
        'c_im': d_s5p[6].reshape(S5C, S5G, S5P).transpose(1, 0, 2)[None],
        'd_skip': dg_dskip, 'b_glu': dg_bglu, 'g_ffn2': dg_ffn2, 'g_final': dg_final.reshape(D)}
    small_shapes = [a[n].shape for n in SMALL]
    small_pack = _pack([small_grads[n] for n in SMALL], F32)
    n_small = small_pack.shape[0]
    sg = all_gather("gather_small_grads",
                    jnp.concatenate([small_pack, _pack([dmod_mine, d_conv8[:CONVW]], F32)], axis=0))
    pieces = _unpack8(sg[:, n_small:, :], [dmod_mine.shape, (CONVW, 3 * DNW)])
    dmod_all = pieces[0].reshape(NDEV * bl, 9 * D)
    g_wada, g_bada = ada_bwd(c_all, lax.dynamic_slice(dmod_all, (0, me * n_ada), (NDEV * bl, n_ada)), dmod_all)

    n_conv = conv_qkv.shape[2]
    conv_parts = lax.dynamic_slice(pieces[1], (0, 0, me * n_conv), (NDEV, CONVW, n_conv))
    conv_parts = jnp.pad(conv_parts.reshape(NDEV, 1, -1), ((0, 0), (0, 7), (0, 0)))
    pad8 = lambda t: jnp.pad(t.reshape(1, -1), ((0, 7), (0, 0)))
    conv_res = adamw("adamw_conv", conv_parts, pad8(conv_qkv), pad8(m_conv_qkv), pad8(v_conv_qkv))
    for kind, buf in zip(("grad", "delta", "new_m", "new_v"), conv_res):
        res[kind + "_conv_qkv"] = buf[0].reshape(conv_qkv.shape)

    small_res = adamw("adamw_small", sg[:, :n_small, :], *[_pack([a[p + n] for n in SMALL], F32) for p in ("", "m_", "v_")])
    for kind, buf in zip(("grad", "delta", "new_m", "new_v"), small_res):
        for n, t in zip(SMALL, _unpack(buf, small_shapes)):
            res[kind + "_" + n] = t

    for n, g in (("w_ada", g_wada), ("b_ada", g_bada)):
        shp = a[n].shape
        r2 = lambda t: t.reshape(-1, shp[-1]) if n == "w_ada" else pad8(t)
        out = adamw("adamw_" + n, r2(g)[None], r2(a[n]), r2(a["m_" + n]), r2(a["v_" + n]))
        for kind, buf in zip(("grad", "delta", "new_m", "new_v"), out):
            res[kind + "_" + n] = (buf if n == "w_ada" else buf[0:1]).reshape(shp)

    outs = [loss, dh0.reshape(x.shape)]
    for kind in ("grad", "delta", "new_m", "new_v"):
        outs += [res[kind + "_" + n] for n in WEIGHTS]
    return tuple(outs)
```

```python
import functools
import math

import jax
import jax.numpy as jnp
from jax import lax
from jax.experimental import pallas as pl
from jax.experimental.pallas import tpu as pltpu

F32 = jnp.float32
BF16 = jnp.bfloat16
HI = lax.Precision.HIGHEST
H3 = lax.Precision.HIGH
SDS = jax.ShapeDtypeStruct

D = 1024
FF = 2816
FFN_TF = FF // 2
NH = 8
DH = 64
DNW = NH * DH
CONVW = 4
CH = 64
DN_ROWS = 2
S5W = 512
S5G = 32
S5P = 64
S5C = 16
S5N = S5G * S5P
GB = 4
NDEV = 8
EPS = 1e-6
LANES = 128
ROW = 1024
VMEM_LIMIT = 56 * 1024 * 1024

ADAM_LR, ADAM_B1, ADAM_B2, ADAM_EPS, ADAM_WD, ADAM_STEP = 0.001, 0.9, 0.999, 1e-08, 0.01, 10

WEIGHTS = ['w_ada', 'b_ada', 'g_ffn1', 'w1_ffn1', 'w3_ffn1', 'w2_ffn1', 'g_mix', 'w_in', 'conv_qkv', 'a_log',
           'dt_bias', 'g_onorm', 'lam_re', 'lam_im', 'log_step', 'b_re', 'b_im', 'c_re', 'c_im', 'd_skip', 'w_glu',
           'b_glu', 'w_proj_a', 'w_proj_b', 'w_out', 'g_ffn2', 'w1_ffn2', 'w3_ffn2', 'w2_ffn2', 'g_final']
RS_WEIGHTS = ['w1_ffn1', 'w3_ffn1', 'w2_ffn1', 'w_in', 'w_glu', 'w_proj_a', 'w_proj_b', 'w_out', 'w1_ffn2', 'w3_ffn2',
              'w2_ffn2']
COL_SHARDED = {'w1_ffn1', 'w3_ffn1', 'w_in', 'w_proj_a', 'w_proj_b', 'w1_ffn2', 'w3_ffn2'}
G_FFN1 = ['w1_ffn1', 'w3_ffn1', 'w2_ffn1']
G_MIX = ['w_in', 'w_glu', 'w_proj_a', 'w_proj_b', 'w_out']
G_FFN2 = ['w1_ffn2', 'w3_ffn2', 'w2_ffn2']
SMALL = ['g_ffn1', 'g_mix', 'a_log', 'dt_bias', 'g_onorm', 'lam_re', 'lam_im', 'log_step', 'b_re', 'b_im', 'c_re',
         'c_im', 'd_skip', 'b_glu', 'g_ffn2', 'g_final']


def _cp(n_grid=0):
    if n_grid:
        return pltpu.CompilerParams(vmem_limit_bytes=VMEM_LIMIT, dimension_semantics=("arbitrary",) * n_grid)
    return pltpu.CompilerParams(vmem_limit_bytes=VMEM_LIMIT)


def _dot(a, b):
    return jnp.dot(a.astype(BF16), b.astype(BF16), preferred_element_type=F32)


def _dot_nt(a, b):
    return lax.dot_general(a.astype(BF16), b.astype(BF16), (((1,), (1,)), ((), ())), preferred_element_type=F32)


def _dot_tn(a, b):
    return lax.dot_general(a.astype(BF16), b.astype(BF16), (((0,), (0,)), ((), ())), preferred_element_type=F32)


def _dot_hi(a, b):
    return jnp.dot(a, b, precision=HI, preferred_element_type=F32)


def _dot_h3(a, b):
    return jnp.dot(a, b, precision=H3, preferred_element_type=F32)


@jax.custom_vjp
def bdot(a, b):
    return _dot(a, b)


bdot.defvjp(lambda a, b: (_dot(a, b), (a, b)),
            lambda r, g: (_dot_nt(g, r[1]).astype(r[0].dtype), _dot_tn(r[0], g).astype(r[1].dtype)))


@jax.custom_vjp
def bdot_nt(a, b):
    return _dot_nt(a, b)


bdot_nt.defvjp(lambda a, b: (_dot_nt(a, b), (a, b)),
               lambda r, g: (_dot(g, r[1]).astype(r[0].dtype), _dot_tn(g, r[0]).astype(r[1].dtype)))


@jax.custom_vjp
def bdot_tn(a, b):
    return _dot_tn(a, b)


bdot_tn.defvjp(lambda a, b: (_dot_tn(a, b), (a, b)),
               lambda r, g: (_dot_nt(r[1], g).astype(r[0].dtype), _dot(r[0], g).astype(r[1].dtype)))


def _silu(x):
    return x * jax.nn.sigmoid(x)


def _iota2(shape, axis):
    return lax.broadcasted_iota(jnp.int32, shape, axis)


def normmod(h, g, sc, sh):
    y = h * lax.rsqrt(jnp.mean(h * h, axis=-1, keepdims=True) + EPS) * g
    return y * (1.0 + sc) + sh


def fn_normmod(h, sh, sc, g):
    return (normmod(h, g, sc, sh),)


def fn_resid(mo, gt):
    return (gt * mo,)


def fn_merge(gate, ya, yb):
    return (jax.nn.sigmoid(gate[:, :D]) * ya + jax.nn.sigmoid(gate[:, D:]) * yb,)


def fn_glu(y, w, b):
    ge = jax.nn.gelu(y)
    return (ge * jax.nn.sigmoid(bdot(ge, w) + b),)


def fn_onorm(o, z, g_on):
    r = _iota2((DH, DNW), 0)
    c = _iota2((DH, DNW), 1)
    expand = (c % DH == r).astype(F32)
    r2 = _iota2((DNW, DNW), 0)
    c2 = _iota2((DNW, DNW), 1)
    avg = (r2 // DH == c2 // DH).astype(F32) * (1.0 / DH)
    ms = _dot_h3(o * o, avg)
    return (o * lax.rsqrt(ms + EPS) * _dot_hi(g_on, expand) * _silu(z),)


def gate_fn(small, alp, dtp):
    beta = jax.nn.sigmoid(small)
    la = -jnp.exp(alp) * jax.nn.softplus(small + dtp)
    tri = (_iota2((CH, CH), 0) >= _iota2((CH, CH), 1)).astype(F32)
    gc = _dot_hi(tri, la)
    gct = lax.dot_general(la, tri, (((0,), (1,)), ((), ())), precision=HI, preferred_element_type=F32)
    return beta, gc, gct


def _bdg(a, b, ca, cb, hi):
    if not hi:
        a, b = a.astype(BF16), b.astype(BF16)
    return lax.dot_general(a, b, (((ca,), (cb,)), ((0,), (0,))), precision=H3 if hi else None,
                           preferred_element_type=F32)


def _batched_matmuls(hi):
    nn_ = lambda a, b: _bdg(a, b, 2, 1, hi)
    nt_ = lambda a, b: _bdg(a, b, 2, 2, hi)
    tn_ = lambda a, b: _bdg(a, b, 1, 1, hi)
    nn = jax.custom_vjp(nn_)
    nn.defvjp(lambda a, b: (nn_(a, b), (a, b)), lambda r, g: (nt_(g, r[1]), tn_(r[0], g)))
    nt = jax.custom_vjp(nt_)
    nt.defvjp(lambda a, b: (nt_(a, b), (a, b)), lambda r, g: (nn_(g, r[1]), tn_(g, r[0])))
    tn = jax.custom_vjp(tn_)
    tn.defvjp(lambda a, b: (tn_(a, b), (a, b)), lambda r, g: (nt_(r[1], g), nn_(r[0], g)))
    return nn, nt, tn


bnn, bnt, btn = _batched_matmuls(False)
hnn, hnt, htn = _batched_matmuls(True)


def _unit_lower_inverse(a):
    r = _iota2((1, CH, CH), 1)
    c = _iota2((1, CH, CH), 2)
    eye = (r == c).astype(F32)
    d = jnp.where(r // 8 == c // 8, a, 0.0)
    inv = eye - d
    p = d
    for _ in range(2):
        p = hnn(p, p)
        inv = inv + hnn(inv, p)
    for blk in (16, 32, 64):
        off = jnp.where((r // blk == c // blk) & (r // (blk // 2) != c // (blk // 2)), a, 0.0)
        inv = inv - hnn(hnn(inv, off), inv)
    return inv


@jax.custom_vjp
def _inverse_given(a, t):
    return t


_inverse_given.defvjp(lambda a, t: (t, t), lambda t, g: (-hnt(htn(t, g), t), jnp.zeros_like(t)))


def dn_prep(xc, w):
    t = xc.shape[0] - 8
    c = xc[5:5 + t] * w[0:1] + xc[6:6 + t] * w[1:2] + xc[7:7 + t] * w[2:3] + xc[8:8 + t] * w[3:4]
    act = _silu(c)
    q, k, v = act[:, :DNW], act[:, DNW:2 * DNW], act[:, 2 * DNW:]
    ones = (_iota2((DNW, DNW), 0) // DH == _iota2((DNW, DNW), 1) // DH).astype(F32)
    q = q * lax.rsqrt(_dot_h3(q * q, ones) + EPS) * (DH ** -0.5)
    k = k * lax.rsqrt(_dot_h3(k * k, ones) + EPS)
    return jnp.concatenate([q, k, v], axis=1)


def dn_chunk(q, k, v, b, g, gt, s_prev, t_saved=None):
    r = _iota2((1, CH, CH), 1)
    c = _iota2((1, CH, CH), 2)
    causal = r >= c
    dec = jnp.where(causal, jnp.exp(jnp.where(causal, g - gt, 0.0)), 0.0)
    kb = k * b
    qk = bnt(jnp.concatenate([q, kb], axis=1), k)
    attn = qk[:, :CH] * dec
    a = jnp.where(r > c, qk[:, CH:] * dec, 0.0)
    tinv = _unit_lower_inverse(a) if t_saved is None else _inverse_given(a, t_saved)
    eg = jnp.exp(g)
    uw = hnn(tinv, jnp.concatenate([v * b, kb * eg], axis=2))
    g_last = g[:, CH - 1:CH]
    ws = bnn(jnp.concatenate([uw[..., DH:], q * eg], axis=1), s_prev)
    v_new = uw[..., :DH] - ws[:, :CH]
    o = ws[:, CH:] + bnn(attn, v_new)
    s_new = s_prev * jnp.exp(g_last) + btn(k * jnp.exp(g_last - g), v_new)
    return o, s_new, tinv


def s5_chunk(u, xp_re, xp_im, bb_re, bb_im, cc_re, cc_im, p0r, p0i, p1r, p1i, pir, pii, dsk):
    nb = u.shape[0]
    u2 = u.reshape(nb * CH, LANES)
    bu_re = bdot(u2, bb_re).reshape(nb, CH, 512)
    bu_im = bdot(u2, bb_im).reshape(nb, CH, 512)
    xt_re = pir * bu_re - pii * bu_im
    xt_im = pir * bu_im + pii * bu_re
    tri = jnp.broadcast_to((_iota2((1, CH, CH), 1) >= _iota2((1, CH, CH), 2)).astype(F32), (nb, CH, CH))
    cs_re = hnn(tri, xt_re)
    cs_im = hnn(tri, xt_im)
    x_re = p0r * cs_re - p0i * cs_im + p1r * xp_re - p1i * xp_im
    x_im = p0r * cs_im + p0i * cs_re + p1r * xp_im + p1i * xp_re
    y = bdot_nt(x_re.reshape(nb * CH, 512), cc_re) - bdot_nt(x_im.reshape(nb * CH, 512), cc_im) + dsk * u2
    return y.reshape(nb, CH, LANES), x_re[:, CH - 1:CH], x_im[:, CH - 1:CH]


def s5_tables(lam_re, lam_im, log_step, bre, bim, cre, cim):
    expand = (_iota2((S5G, S5N), 1) // S5P == _iota2((S5G, S5N), 0)).astype(F32)
    step = _dot_hi(jnp.exp(log_step), expand)
    lre = jnp.minimum(lam_re, -1e-4)
    lr = lre * step
    ang = lam_im * step
    mag = jnp.exp(lr)
    lb_re = mag * jnp.cos(ang)
    lb_im = mag * jnp.sin(ang)
    den = lre * lre + lam_im * lam_im
    coef_re = ((lb_re - 1.0) * lre + lb_im * lam_im) / den
    coef_im = (lb_im * lre - (lb_re - 1.0) * lam_im) / den
    bb_re = coef_re * bre - coef_im * bim
    bb_im = coef_re * bim + coef_im * bre
    j = _iota2((CH, 1), 0).astype(F32)
    e0 = jnp.exp(j * lr)
    e1 = jnp.exp((j + 1.0) * lr)
    ei = jnp.exp(-j * lr)
    mask = (_iota2((LANES, 512), 0) // S5C == _iota2((LANES, 512), 1) // S5P).astype(F32)

    def blocks(t):
        return jnp.concatenate([(jnp.tile(t[:, gb * 512:(gb + 1) * 512], (LANES // S5C, 1)) * mask)[None]
                                for gb in range(GB)], axis=0)

    return (blocks(bb_re), blocks(bb_im), blocks(cre), blocks(cim),
            e0 * jnp.cos(j * ang), e0 * jnp.sin(j * ang),
            e1 * jnp.cos((j + 1.0) * ang), e1 * jnp.sin((j + 1.0) * ang),
            ei * jnp.cos(j * ang), -ei * jnp.sin(j * ang))


def _row_specs(tiled, batch, bcast, tm, tpb):
    specs = [pl.BlockSpec((tm, a.shape[1]), lambda i: (i, 0)) for a in tiled]
    specs += [pl.BlockSpec((None,) + a.shape[1:], lambda i: (i // tpb, 0, 0)) for a in batch]
    specs += [pl.BlockSpec(a.shape, lambda i, nd=a.ndim: (0,) * nd) for a in bcast]
    return specs


def ew_call(name, fn, tiled, batch, bcast, outs, tm, seq):
    t_rows = tiled[0].shape[0]
    n_in = len(tiled) + len(batch) + len(bcast)

    def body(*refs):
        vals = [r[...].astype(F32) for r in refs[:n_in]]
        for r, o in zip(refs[n_in:], fn(*vals)):
            r[...] = o.astype(r.dtype)

    return pl.pallas_call(
        body, grid=(t_rows // tm,), in_specs=_row_specs(tiled, batch, bcast, tm, seq // tm),
        out_specs=[pl.BlockSpec((tm, w), lambda i: (i, 0)) for w, _ in outs],
        out_shape=[SDS((t_rows, w), dt) for w, dt in outs], name=name, compiler_params=_cp(1))(*tiled, *batch, *bcast)


def ew_vjp_call(name, fn, tiled, batch, bcast, cts, want, tm, seq, addend=None):
    t_rows = tiled[0].shape[0]
    tpb = seq // tm
    n_t, n_b, n_c = len(tiled), len(batch), len(bcast)
    n_in = n_t + n_b + n_c
    extra = [] if addend is None else [addend]

    def body(*refs):
        i = pl.program_id(0)
        vals = [r[...].astype(F32) for r in refs[:n_in]]
        ctv = tuple(r[...].astype(F32) for r in refs[n_in:n_in + len(cts)])
        outs = refs[n_in + len(cts) + len(extra):]
        _, vjp = jax.vjp(fn, *vals)
        grads = vjp(ctv)
        for k, (r, (idx, _)) in enumerate(zip(outs[:len(want)], want)):
            g = grads[idx]
            if k == 0 and extra:
                g = g + refs[n_in + len(cts)][...]
            r[...] = g.astype(r.dtype)
        for k in range(n_b):
            r, g = outs[len(want) + k], grads[n_t + k]

            @pl.when(i % tpb == 0)
            def _(r=r, g=g):
                r[...] = g

            @pl.when(i % tpb != 0)
            def _(r=r, g=g):
                r[...] += g
        for k in range(n_c):
            r, g = outs[len(want) + n_b + k], grads[n_t + n_b + k]

            @pl.when(i == 0)
            def _(r=r, g=g):
                r[...] = g

            @pl.when(i != 0)
            def _(r=r, g=g):
                r[...] += g

    out_specs = [pl.BlockSpec((tm, tiled[idx].shape[1]), lambda i: (i, 0)) for idx, _ in want]
    out_specs += [pl.BlockSpec((None,) + a.shape[1:], lambda i: (i // tpb, 0, 0)) for a in batch]
    out_specs += [pl.BlockSpec(a.shape, lambda i, nd=a.ndim: (0,) * nd) for a in bcast]
    out_shape = [SDS(tiled[idx].shape, dt) for idx, dt in want]
    out_shape += [SDS(a.shape, F32) for a in batch] + [SDS(a.shape, F32) for a in bcast]
    res = pl.pallas_call(
        body, grid=(t_rows // tm,),
        in_specs=_row_specs(tiled, batch, bcast, tm, tpb)
        + [pl.BlockSpec((tm, a.shape[1]), lambda i: (i, 0)) for a in list(cts) + extra],
        out_specs=out_specs, out_shape=out_shape, name=name, compiler_params=_cp(1))(*tiled, *batch, *bcast, *cts, *extra)
    return res[:len(want)], res[len(want):len(want) + n_b], res[len(want) + n_b:]


def _pick(n, cands):
    for c in cands:
        if n % c == 0:
            return c
    return n


def mm(name, pairs, nt, out_dtype):
    m = pairs[0][0].shape[0]
    n = pairs[0][1].shape[0 if nt else 1]
    k_total = sum(a.shape[1] for a, _ in pairs)
    tm = _pick(m, (1024, 512, 256, 128) if k_total <= 2048 else (512, 256, 128))
    tn = _pick(n, (512, 256, 128))
    np_ = len(pairs)

    def body(*refs):
        acc = None
        for p in range(np_):
            a, b = refs[2 * p][...], refs[2 * p + 1][...]
            t = _dot_nt(a, b) if nt else _dot(a, b)
            acc = t if acc is None else acc + t
        refs[2 * np_][...] = acc.astype(out_dtype)

    in_specs, ops = [], []
    for a, b in pairs:
        k = a.shape[1]
        in_specs.append(pl.BlockSpec((tm, k), lambda i, j: (i, 0)))
        in_specs.append(pl.BlockSpec((tn, k), lambda i, j: (j, 0)) if nt else pl.BlockSpec((k, tn), lambda i, j: (0, j)))
        ops += [a, b]
    return pl.pallas_call(
        body, grid=(m // tm, n // tn), in_specs=in_specs, out_specs=pl.BlockSpec((tm, tn), lambda i, j: (i, j)),
        out_shape=SDS((m, n), out_dtype), name=name, compiler_params=_cp(2))(*ops)


def mm_tn(name, a, b, exchange=None):
    t_rows, m = a.shape
    n = b.shape[1]
    tn = n if n <= 1024 else _pick(n, (1024, 512, 256, 128))
    tm = max([t for t in range(LANES, m + 1, LANES) if m % t == 0 and t * tn * 4 <= 6 * 1024 * 1024] or [m])
    tk = _pick(t_rows, (512, 256, 128, 64))
    grid = (m // tm, n // tn, t_rows // tk)
    extra = [] if exchange is None else [exchange]

    def body(*refs):
        a_ref, b_ref = refs[:2]
        o_ref, acc = refs[2 + len(extra)], refs[3 + 2 * len(extra)]
        i, j, k = pl.program_id(0), pl.program_id(1), pl.program_id(2)
        if extra:
            start, finish = _exchange_phases(refs[2], refs[4], *refs[6:9])
            pl.when((i == 0) & (j == 0) & (k == 0))(start)

        @pl.when(k == 0)
        def _():
            acc[...] = jnp.zeros_like(acc)

        acc[...] += _dot_tn(a_ref[...], b_ref[...])

        @pl.when(k == grid[2] - 1)
        def _():
            o_ref[...] = acc[...].astype(BF16)

        if extra:
            pl.when((i == grid[0] - 1) & (j == grid[1] - 1) & (k == grid[2] - 1))(finish)

    res = pl.pallas_call(
        body, grid=grid,
        in_specs=[pl.BlockSpec((tk, tm), lambda i, j, k: (k, i)), pl.BlockSpec((tk, tn), lambda i, j, k: (k, j))]
        + [HBM_SPEC] * len(extra),
        out_specs=[pl.BlockSpec((tm, tn), lambda i, j, k: (i, j))] + [HBM_SPEC] * len(extra),
        out_shape=[SDS((m, n), BF16)] + [SDS(x.shape, x.dtype) for x in extra],
        scratch_shapes=[pltpu.VMEM((tm, tn), F32)] + (_comm_scratch() if extra else []), name=name,
        compiler_params=_cp(3))(a, b, *extra)
    return res if extra else res[0]


def ffn_fwd(name, h, mod3, g, w1, w3, w2, seq, gather=None):
    t_rows = h.shape[0]
    tm = _pick(seq, (512, 256, 128, 64))
    tf = FFN_TF
    tpb = seq // tm
    nf = FF // tf
    nt = t_rows // tm
    extra = [] if gather is None else [gather]

    def body(*refs):
        h_ref, mod_ref, g_ref, w1_ref, w3_ref, w2_ref = refs[:6]
        ho_ref, f_ref, u_ref = refs[6 + len(extra):9 + len(extra)]
        acc = refs[9 + 2 * len(extra)]
        i, j = pl.program_id(0), pl.program_id(1)
        if extra:
            start, forward, finish = _gather_phases(refs[6], refs[10], *refs[12:15])
            pl.when((i == 0) & (j == 0))(start)
            pl.when((i == nt - 1) & (j == 0))(forward)

        @pl.when(j == 0)
        def _():
            u_ref[...] = normmod(h_ref[...], g_ref[...], mod_ref[1:2, :], mod_ref[0:1, :]).astype(BF16)
            acc[...] = jnp.zeros_like(acc)

        u = u_ref[...]
        a = _silu(_dot_nt(u, w1_ref[...])) * _dot_nt(u, w3_ref[...])
        acc[...] += _dot(a, w2_ref[...])

        @pl.when(j == nf - 1)
        def _():
            f_ref[...] = acc[...]
            ho_ref[...] = h_ref[...] + 0.5 * mod_ref[2:3, :] * acc[...]

        if extra:
            pl.when((i == nt - 1) & (j == nf - 1))(finish)

    row = lambda i, j: (i, 0)
    return pl.pallas_call(
        body, grid=(nt, nf),
        in_specs=[pl.BlockSpec((tm, D), row), pl.BlockSpec((None, 3, D), lambda i, j: (i // tpb, 0, 0)),
                  pl.BlockSpec((1, D), lambda i, j: (0, 0)), pl.BlockSpec((tf, D), lambda i, j: (j, 0)),
                  pl.BlockSpec((tf, D), lambda i, j: (j, 0)), pl.BlockSpec((tf, D), lambda i, j: (j, 0))]
        + [HBM_SPEC] * len(extra),
        out_specs=[pl.BlockSpec((tm, D), row), pl.BlockSpec((tm, D), row), pl.BlockSpec((tm, D), row)]
        + [HBM_SPEC] * len(extra),
        out_shape=[SDS((t_rows, D), F32), SDS((t_rows, D), F32), SDS((t_rows, D), BF16)]
        + [SDS((NDEV,) + x.shape, x.dtype) for x in extra],
        scratch_shapes=[pltpu.VMEM((tm, D), F32)] + (_comm_scratch() if extra else []), name=name,
        compiler_params=_cp(2))(h, mod3, g, w1, w3, w2, *extra)


def ffn_bwd(name, dho, h, f_out, u, mod3, g, w1, w3, w2, seq, exchange=None):
    t_rows = h.shape[0]
    tm = _pick(seq, (256, 128, 64))
    tf = FFN_TF
    tpb = seq // tm
    nf = FF // tf
    nt = t_rows // tm
    extra = [] if exchange is None else [exchange]

    def body(*refs):
        dho_ref, h_ref, f_ref, u_ref, mod_ref, g_ref, w1_ref, w3_ref, w2_ref = refs[:9]
        dh_ref, a_ref, dh1_ref, dh3_ref, df_scr, dmod_ref, dg_ref = refs[9 + len(extra):16 + len(extra)]
        du_acc = refs[16 + 2 * len(extra)]
        i, j = pl.program_id(0), pl.program_id(1)
        if extra:
            start, finish = _exchange_phases(refs[9], refs[17], *refs[19:22])
            pl.when((i == 0) & (j == 0))(start)

        @pl.when(j == 0)
        def _():
            df_scr[...] = (0.5 * mod_ref[2:3, :] * dho_ref[...]).astype(BF16)
            du_acc[...] = jnp.zeros_like(du_acc)

        uu = u_ref[...]
        h1 = _dot_nt(uu, w1_ref[...])
        h3 = _dot_nt(uu, w3_ref[...])
        sg = jax.nn.sigmoid(h1)
        s = h1 * sg
        da = _dot_nt(df_scr[...], w2_ref[...])
        dh3 = (da * s).astype(BF16)
        dh1 = (da * h3 * (sg * (1.0 + h1 * (1.0 - sg)))).astype(BF16)
        a_ref[...] = (s * h3).astype(BF16)
        dh1_ref[...] = dh1
        dh3_ref[...] = dh3
        du_acc[...] += _dot(dh1, w1_ref[...]) + _dot(dh3, w3_ref[...])

        @pl.when(j == nf - 1)
        def _():
            _, vjp = jax.vjp(normmod, h_ref[...], g_ref[...], mod_ref[1:2, :], mod_ref[0:1, :])
            dh_n, dg, dsc, dsh = vjp(du_acc[...])
            dh_ref[...] = dho_ref[...] + dh_n
            dgt = jnp.sum(0.5 * dho_ref[...] * f_ref[...], axis=0, keepdims=True)
            dmod = jnp.concatenate([dsh, dsc, dgt], axis=0)

            @pl.when(i % tpb == 0)
            def _():
                dmod_ref[...] = dmod

            @pl.when(i % tpb != 0)
            def _():
                dmod_ref[...] += dmod

            @pl.when(i == 0)
            def _():
                dg_ref[...] = dg

            @pl.when(i != 0)
            def _():
                dg_ref[...] += dg

        if extra:
            pl.when((i == nt - 1) & (j == nf - 1))(finish)

    row = lambda i, j: (i, 0)
    col = lambda i, j: (i, j)
    return pl.pallas_call(
        body, grid=(nt, nf),
        in_specs=[pl.BlockSpec((tm, D), row), pl.BlockSpec((tm, D), row), pl.BlockSpec((tm, D), row),
                  pl.BlockSpec((tm, D), row), pl.BlockSpec((None, 3, D), lambda i, j: (i // tpb, 0, 0)),
                  pl.BlockSpec((1, D), lambda i, j: (0, 0)), pl.BlockSpec((tf, D), lambda i, j: (j, 0)),
                  pl.BlockSpec((tf, D), lambda i, j: (j, 0)), pl.BlockSpec((tf, D), lambda i, j: (j, 0))]
        + [HBM_SPEC] * len(extra),
        out_specs=[pl.BlockSpec((tm, D), row), pl.BlockSpec((tm, tf), col), pl.BlockSpec((tm, tf), col),
                   pl.BlockSpec((tm, tf), col), pl.BlockSpec((tm, D), row),
                   pl.BlockSpec((None, 3, D), lambda i, j: (i // tpb, 0, 0)), pl.BlockSpec((1, D), lambda i, j: (0, 0))]
        + [HBM_SPEC] * len(extra),
        out_shape=[SDS((t_rows, D), F32), SDS((t_rows, FF), BF16), SDS((t_rows, FF), BF16), SDS((t_rows, FF), BF16),
                   SDS((t_rows, D), BF16), SDS(mod3.shape, F32), SDS((1, D), F32)] + [SDS(x.shape, x.dtype) for x in extra],
        scratch_shapes=[pltpu.VMEM((tm, D), F32)] + (_comm_scratch() if extra else []), name=name,
        compiler_params=_cp(2))(dho, h, f_out, u, mod3, g, w1, w3, w2, *extra)


def _dn_cols(part, hd):
    return slice(part * DNW + hd * DH, part * DNW + (hd + 1) * DH)


def _qkv_stacks(qkv_ref, nb):
    pairs = [(b, hd) for b in range(nb) for hd in range(NH)]
    return [jnp.stack([qkv_ref[b, :, _dn_cols(part, hd)] for b, hd in pairs]) for part in range(3)]


def dn_prep_fwd(p_dn, conv8):
    bl, seq, _ = p_dn.shape
    tp = _pick(seq, (256, 128, 64))

    def body(raw_ref, halo_ref, conv_ref, o_ref):
        hm = (pl.program_id(1) > 0).astype(F32)
        o_ref[...] = dn_prep(jnp.concatenate([halo_ref[...] * hm, raw_ref[...]], axis=0), conv_ref[...])

    return pl.pallas_call(
        body, grid=(bl, seq // tp),
        in_specs=[pl.BlockSpec((None, tp, 3 * DNW), lambda b, i: (b, i, 0)),
                  pl.BlockSpec((None, 8, 3 * DNW), lambda b, i: (b, jnp.maximum(i * (tp // 8) - 1, 0), 0)),
                  pl.BlockSpec((8, 3 * DNW), lambda b, i: (0, 0))],
        out_specs=pl.BlockSpec((None, tp, 3 * DNW), lambda b, i: (b, i, 0)),
        out_shape=SDS((bl, seq, 3 * DNW), F32), name="dn_prep_fwd", compiler_params=_cp(2))(p_dn, p_dn, conv8)


def dn_prep_bwd(p_dn, conv8, d_qkv, d_z):
    bl, seq, _ = p_dn.shape
    tp = _pick(seq, (256, 128, 64))
    nt = seq // tp

    def body(raw_ref, halo_ref, conv_ref, dq_ref, dz_ref, draw_ref, dconv_ref, carry):
        b, r = pl.program_id(0), pl.program_id(1)

        @pl.when((b == 0) & (r == 0))
        def _():
            dconv_ref[...] = jnp.zeros_like(dconv_ref)

        @pl.when(r == 0)
        def _():
            carry[...] = jnp.zeros_like(carry)

        hm = (r < nt - 1).astype(F32)
        _, vjp = jax.vjp(dn_prep, jnp.concatenate([halo_ref[...] * hm, raw_ref[...]], axis=0), conv_ref[...])
        dxc, dw = vjp(dq_ref[...])
        tail = dxc[tp:tp + 8] + carry[...]
        draw_ref[:, 0:3 * DNW] = jnp.concatenate([dxc[8:tp], tail], axis=0).astype(BF16)
        draw_ref[:, 3 * DNW:4 * DNW] = dz_ref[...].astype(BF16)
        carry[...] = dxc[0:8] * hm
        dconv_ref[...] += dw

    blk = lambda b, r: (b, nt - 1 - r, 0)
    return pl.pallas_call(
        body, grid=(bl, nt),
        in_specs=[pl.BlockSpec((None, tp, 3 * DNW), blk),
                  pl.BlockSpec((None, 8, 3 * DNW), lambda b, r: (b, jnp.maximum((nt - 1 - r) * (tp // 8) - 1, 0), 0)),
                  pl.BlockSpec((8, 3 * DNW), lambda b, r: (0, 0)), pl.BlockSpec((None, tp, 3 * DNW), blk),
                  pl.BlockSpec((None, tp, DNW), blk)],
        out_specs=[pl.BlockSpec((None, tp, 4 * DNW), blk), pl.BlockSpec((8, 3 * DNW), lambda b, r: (0, 0))],
        out_shape=[SDS((bl, seq, 4 * DNW), BF16), SDS((8, 3 * DNW), F32)],
        scratch_shapes=[pltpu.VMEM((8, 3 * DNW), F32)], name="dn_prep_bwd", compiler_params=_cp(2))(p_dn, p_dn, conv8, d_qkv, d_z)


def _gate_stacks(gates, nb):
    pairs = [(b, hd) for b in range(nb) for hd in range(NH)]
    bs = jnp.stack([gates[b][0][:, hd:hd + 1] for b, hd in pairs])
    gs = jnp.stack([gates[b][1][:, NH + hd:NH + hd + 1] for b, hd in pairs])
    gts = jnp.stack([gates[b][2][NH + hd:NH + hd + 1, :] for b, hd in pairs])
    return bs, gs, gts


def deltanet_fwd(qkv, p_small, alp, dtp, nb):
    bl, seq, _ = qkv.shape
    nc = seq // CH
    ng = nb * NH

    def body(qkv_ref, small_ref, alp_ref, dtp_ref, o_ref, sprev_ref, tinv_ref, s_scr):
        @pl.when(pl.program_id(1) == 0)
        def _():
            s_scr[...] = jnp.zeros_like(s_scr)

        gates = [gate_fn(small_ref[b], alp_ref[...], dtp_ref[...]) for b in range(nb)]
        s_prev = s_scr[...]
        o, s_new, tinv = dn_chunk(*_qkv_stacks(qkv_ref, nb), *_gate_stacks(gates, nb), s_prev)
        sprev_ref[...] = s_prev
        tinv_ref[...] = tinv
        s_scr[...] = s_new
        for b in range(nb):
            for hd in range(NH):
                o_ref[b, :, hd * DH:(hd + 1) * DH] = o[b * NH + hd]

    blk = lambda bb, n: (bb, n, 0)
    const = lambda bb, n: (0, 0)
    saved = pl.BlockSpec((None, ng, DH, DH), lambda bb, n: (bb * nc + n, 0, 0, 0))
    return pl.pallas_call(
        body, grid=(bl // nb, nc),
        in_specs=[pl.BlockSpec((nb, CH, 3 * DNW), blk), pl.BlockSpec((nb, CH, LANES), blk),
                  pl.BlockSpec((1, LANES), const), pl.BlockSpec((1, LANES), const)],
        out_specs=[pl.BlockSpec((nb, CH, DNW), blk), saved, saved],
        out_shape=[SDS((bl, seq, DNW), F32), SDS((bl // nb * nc, ng, DH, DH), F32), SDS((bl // nb * nc, ng, DH, DH), F32)],
        scratch_shapes=[pltpu.VMEM((ng, DH, DH), F32)], name="deltanet_fwd",
        compiler_params=_cp(2))(qkv, p_small, alp, dtp)


def deltanet_bwd(qkv, p_small, alp, dtp, sprev, tinv, d_o, nb, exchange=None):
    bl, seq, _ = qkv.shape
    nc = seq // CH
    ng = nb * NH
    extra = [] if exchange is None else [exchange]

    def body(*refs):
        qkv_ref, small_ref, alp_ref, dtp_ref, sprev_ref, tinv_ref, do_ref = refs[:7]
        dqkv_ref, dsmall_ref, dalp_ref, ddtp_ref = refs[7 + len(extra):11 + len(extra)]
        ds_scr = refs[11 + 2 * len(extra)]
        bb, r = pl.program_id(0), pl.program_id(1)
        if extra:
            start, finish = _exchange_phases(refs[7], refs[12], *refs[14:17])
            pl.when((bb == 0) & (r == 0))(start)

        @pl.when((bb == 0) & (r == 0))
        def _():
            dalp_ref[...] = jnp.zeros_like(dalp_ref)
            ddtp_ref[...] = jnp.zeros_like(ddtp_ref)

        @pl.when(r == 0)
        def _():
            ds_scr[...] = jnp.zeros_like(ds_scr)

        gates, gate_vjps = [], []
        for b in range(nb):
            out, gvjp = jax.vjp(gate_fn, small_ref[b], alp_ref[...], dtp_ref[...])
            gates.append(out)
            gate_vjps.append(gvjp)
        t_saved = tinv_ref[...]
        _, vjp = jax.vjp(lambda *args: dn_chunk(*args, t_saved)[:2], *_qkv_stacks(qkv_ref, nb), *_gate_stacks(gates, nb),
                         sprev_ref[...])
        d_out = jnp.stack([do_ref[b, :, hd * DH:(hd + 1) * DH] for b in range(nb) for hd in range(NH)])
        grads = vjp((d_out, ds_scr[...]))
        ds_scr[...] = grads[6]
        lane = _iota2((CH, LANES), 1)
        rowi = _iota2((LANES, CH), 0)
        for b in range(nb):
            d_beta = jnp.zeros((CH, LANES), F32)
            d_gc = jnp.zeros((CH, LANES), F32)
            d_gct = jnp.zeros((LANES, CH), F32)
            for hd in range(NH):
                i = b * NH + hd
                for part in range(3):
                    dqkv_ref[b, :, _dn_cols(part, hd)] = grads[part][i]
                d_beta = d_beta + jnp.where(lane == hd, grads[3][i], 0.0)
                d_gc = d_gc + jnp.where(lane == NH + hd, grads[4][i], 0.0)
                d_gct = d_gct + jnp.where(rowi == NH + hd, grads[5][i], 0.0)
            d_small, d_alp, d_dtp = gate_vjps[b]((d_beta, d_gc, d_gct))
            dsmall_ref[b] = d_small.astype(BF16)
            dalp_ref[...] += d_alp
            ddtp_ref[...] += d_dtp
        if extra:
            pl.when((bb == bl // nb - 1) & (r == nc - 1))(finish)

    blk = lambda bb, r: (bb, nc - 1 - r, 0)
    const = lambda bb, r: (0, 0)
    saved = pl.BlockSpec((None, ng, DH, DH), lambda bb, r: (bb * nc + nc - 1 - r, 0, 0, 0))
    return pl.pallas_call(
        body, grid=(bl // nb, nc),
        in_specs=[pl.BlockSpec((nb, CH, 3 * DNW), blk), pl.BlockSpec((nb, CH, LANES), blk), pl.BlockSpec((1, LANES), const),
                  pl.BlockSpec((1, LANES), const), saved, saved, pl.BlockSpec((nb, CH, DNW), blk)] + [HBM_SPEC] * len(extra),
        out_specs=[pl.BlockSpec((nb, CH, 3 * DNW), blk), pl.BlockSpec((nb, CH, LANES), blk), pl.BlockSpec((1, LANES), const),
                   pl.BlockSpec((1, LANES), const)] + [HBM_SPEC] * len(extra),
        out_shape=[SDS((bl, seq, 3 * DNW), F32), SDS((bl, seq, LANES), BF16), SDS((1, LANES), F32), SDS((1, LANES), F32)]
        + [SDS(x.shape, x.dtype) for x in extra],
        scratch_shapes=[pltpu.VMEM((ng, DH, DH), F32)] + (_comm_scratch() if extra else []), name="deltanet_bwd",
        compiler_params=_cp(2))(qkv, p_small, alp, dtp, sprev, tinv, d_o, *extra)


def _s5_table_specs():
    tab3 = pl.BlockSpec((None, LANES, 512), lambda gb, n: (gb, 0, 0))
    tab2 = pl.BlockSpec((CH, 512), lambda gb, n: (0, gb))
    return [tab3] * 4 + [tab2] * 6 + [pl.BlockSpec((1, LANES), lambda gb, n: (0, gb))]


def s5_fwd(u, tables, dsk):
    bl, seq, _ = u.shape
    nc = seq // CH

    def body(u_ref, *rest):
        tabs, (y_ref, xs_ref, xr_scr, xi_scr) = rest[:11], rest[11:]

        @pl.when(pl.program_id(1) == 0)
        def _():
            xr_scr[...] = jnp.zeros_like(xr_scr)
            xi_scr[...] = jnp.zeros_like(xi_scr)

        xp_re, xp_im = xr_scr[...], xi_scr[...]
        xs_ref[0:bl] = xp_re
        xs_ref[bl:2 * bl] = xp_im
        y, xn_re, xn_im = s5_chunk(u_ref[...], xp_re, xp_im, *[t[...] for t in tabs])
        y_ref[...] = y
        xr_scr[...] = xn_re
        xi_scr[...] = xn_im

    blk = lambda gb, n: (0, n, gb)
    return pl.pallas_call(
        body, grid=(GB, nc), in_specs=[pl.BlockSpec((bl, CH, LANES), blk)] + _s5_table_specs(),
        out_specs=[pl.BlockSpec((bl, CH, LANES), blk),
                   pl.BlockSpec((None, 2 * bl, 1, 512), lambda gb, n: (gb * nc + n, 0, 0, 0))],
        out_shape=[SDS((bl, seq, S5W), F32), SDS((GB * nc, 2 * bl, 1, 512), F32)],
        scratch_shapes=[pltpu.VMEM((bl, 1, 512), F32), pltpu.VMEM((bl, 1, 512), F32)], name="s5_fwd",
        compiler_params=_cp(2))(u, *tables, dsk)


def s5_bwd(u, tables, dsk, xs, dy):
    bl, seq, _ = u.shape
    nc = seq // CH

    def body(u_ref, *rest):
        tabs, xs_ref, dy_ref = rest[:11], rest[11], rest[12]
        du_ref, dtabs, dxr_scr, dxi_scr = rest[13], rest[14:25], rest[25], rest[26]
        r = pl.program_id(1)

        @pl.when(r == 0)
        def _():
            for t in dtabs:
                t[...] = jnp.zeros_like(t)
            dxr_scr[...] = jnp.zeros_like(dxr_scr)
            dxi_scr[...] = jnp.zeros_like(dxi_scr)

        _, vjp = jax.vjp(s5_chunk, u_ref[...], xs_ref[0:bl], xs_ref[bl:2 * bl], *[t[...] for t in tabs])
        grads = vjp((dy_ref[...], dxr_scr[...], dxi_scr[...]))
        du_ref[...] = grads[0].astype(BF16)
        dxr_scr[...] = grads[1]
        dxi_scr[...] = grads[2]
        for t, g in zip(dtabs, grads[3:]):
            t[...] += g

    blk = lambda gb, r: (0, nc - 1 - r, gb)
    tab_shapes = [SDS(t.shape, F32) for t in tables] + [SDS(dsk.shape, F32)]
    return pl.pallas_call(
        body, grid=(GB, nc),
        in_specs=[pl.BlockSpec((bl, CH, LANES), blk)] + _s5_table_specs()
        + [pl.BlockSpec((None, 2 * bl, 1, 512), lambda gb, r: (gb * nc + nc - 1 - r, 0, 0, 0)), pl.BlockSpec((bl, CH, LANES), blk)],
        out_specs=[pl.BlockSpec((bl, CH, LANES), blk)] + _s5_table_specs(),
        out_shape=[SDS((bl, seq, S5W), BF16)] + tab_shapes,
        scratch_shapes=[pltpu.VMEM((bl, 1, 512), F32), pltpu.VMEM((bl, 1, 512), F32)], name="s5_bwd",
        compiler_params=_cp(2))(u, *tables, dsk, xs, dy)


def s5_tables_fwd(params):
    shapes = [SDS((GB, LANES, 512), F32)] * 4 + [SDS((CH, S5N), F32)] * 6

    def body(*refs):
        for r, t in zip(refs[7:], s5_tables(*[p[...] for p in refs[:7]])):
            r[...] = t

    return pl.pallas_call(body, out_shape=shapes, name="s5_tables_fwd", compiler_params=_cp())(*params)


def s5_tables_bwd(params, dtables):
    def body(*refs):
        _, vjp = jax.vjp(s5_tables, *[p[...] for p in refs[:7]])
        for r, g in zip(refs[17:], vjp(tuple(t[...] for t in refs[7:17]))):
            r[...] = g

    return pl.pallas_call(body, out_shape=[SDS(p.shape, F32) for p in params], name="s5_tables_bwd",
                          compiler_params=_cp())(*params, *dtables)


def ada_fwd(c_all, w_loc, b_loc):
    def body(c_ref, w_ref, b_ref, o_ref):
        o_ref[...] = _dot(_silu(c_ref[...]), w_ref[...]) + b_ref[...]

    return pl.pallas_call(body, out_shape=SDS((c_all.shape[0], w_loc.shape[1]), F32), name="ada_fwd",
                          compiler_params=_cp())(c_all, w_loc, b_loc)


def ada_bwd(c_all, dmod_mine, dmod_all):
    def body(c_ref, dm_ref, da_ref, gw_ref, gb_ref):
        gw_ref[...] = _dot_tn(_silu(c_ref[...]), dm_ref[...])
        gb_ref[...] = jnp.sum(da_ref[...], axis=0, keepdims=True)

    return pl.pallas_call(body, out_shape=[SDS((D, dmod_mine.shape[1]), F32), SDS((1, dmod_all.shape[1]), F32)],
                          name="ada_bwd", compiler_params=_cp())(c_all, dmod_mine, dmod_all)


def loss_head(h, tgt, g, seq):
    t_rows = h.shape[0]
    tm = _pick(seq, (256, 128, 64))

    def body(h_ref, t_ref, g_ref, dh_ref, dg_ref, loss_ref):
        i = pl.program_id(0)
        y, vjp = jax.vjp(lambda hh, gg: hh * lax.rsqrt(jnp.mean(hh * hh, axis=-1, keepdims=True) + EPS) * gg,
                         h_ref[...], g_ref[...])
        e = y - t_ref[...]
        dh, dg = vjp(e * (1.0 / D))
        part = jnp.sum(jnp.sum(e * e, axis=1, keepdims=True), axis=0, keepdims=True) * (0.5 / D) + jnp.zeros((1, LANES), F32)
        dh_ref[...] = dh

        @pl.when(i == 0)
        def _():
            dg_ref[...] = dg
            loss_ref[...] = part

        @pl.when(i != 0)
        def _():
            dg_ref[...] += dg
            loss_ref[...] += part

    row = lambda i: (i, 0)
    const = lambda i: (0, 0)
    return pl.pallas_call(
        body, grid=(t_rows // tm,),
        in_specs=[pl.BlockSpec((tm, D), row), pl.BlockSpec((tm, D), row), pl.BlockSpec((1, D), const)],
        out_specs=[pl.BlockSpec((tm, D), row), pl.BlockSpec((1, D), const), pl.BlockSpec((1, LANES), const)],
        out_shape=[SDS((t_rows, D), F32), SDS((1, D), F32), SDS((1, LANES), F32)], name="loss_head",
        compiler_params=_cp(1))(h, tgt, g)


def adamw(name, parts, w, m, v):
    k_parts, rows, cols = parts.shape
    tr = _pick(rows, (256, 128, 64, 32, 16, 8))

    def body(p_ref, w_ref, m_ref, v_ref, g_ref, d_ref, mo_ref, vo_ref):
        g = p_ref[0].astype(F32)
        for k in range(1, k_parts):
            g = g + p_ref[k].astype(F32)
        _adam_store(g, w_ref, m_ref, v_ref, g_ref, d_ref, mo_ref, vo_ref)

    blk = pl.BlockSpec((tr, cols), lambda i: (i, 0))
    return pl.pallas_call(
        body, grid=(rows // tr,), in_specs=[pl.BlockSpec((k_parts, tr, cols), lambda i: (0, i, 0)), blk, blk, blk],
        out_specs=[blk] * 4, out_shape=[SDS((rows, cols), F32)] * 4, name=name, compiler_params=_cp(1))(parts, w, m, v)


def _adam_store(g, w_ref, m_ref, v_ref, g_ref, d_ref, mo_ref, vo_ref):
    m_new = ADAM_B1 * m_ref[...] + (1.0 - ADAM_B1) * g
    v_new = ADAM_B2 * v_ref[...] + (1.0 - ADAM_B2) * (g * g)
    m_hat = m_new / (1.0 - ADAM_B1 ** ADAM_STEP)
    v_hat = v_new / (1.0 - ADAM_B2 ** ADAM_STEP)
    g_ref[...] = g
    d_ref[...] = -ADAM_LR * (m_hat / (jnp.sqrt(v_hat) + ADAM_EPS) + ADAM_WD * w_ref[...])
    mo_ref[...] = m_new
    vo_ref[...] = v_new


def adamw_t(name, parts, w, m, v):
    k_parts, r, c = parts.shape
    tc = _pick(c, (256, 128))

    def body(p_ref, w_ref, m_ref, v_ref, g_ref, d_ref, mo_ref, vo_ref):
        gt = p_ref[0].astype(F32)
        for k in range(1, k_parts):
            gt = gt + p_ref[k].astype(F32)
        _adam_store(gt.T, w_ref, m_ref, v_ref, g_ref, d_ref, mo_ref, vo_ref)

    blk = pl.BlockSpec((tc, r), lambda j: (j, 0))
    return pl.pallas_call(
        body, grid=(c // tc,), in_specs=[pl.BlockSpec((k_parts, r, tc), lambda j: (0, 0, j)), blk, blk, blk],
        out_specs=[blk] * 4, out_shape=[SDS((c, r), F32)] * 4, name=name, compiler_params=_cp(1))(parts, w, m, v)


def _comm_scratch():
    return [pltpu.SemaphoreType.DMA((7,)), pltpu.SemaphoreType.DMA((7,)), pltpu.SemaphoreType.DMA]


HBM_SPEC = pl.BlockSpec(memory_space=pl.ANY)


def _gather_phases(x_ref, out_ref, send_sems, recv_sems, local_sem):
    mx, my, mc = lax.axis_index("x"), lax.axis_index("y"), lax.axis_index("c")
    me, sibling = (mx, my, mc), (mx, my, 1 - mc)
    chips = [(1 - mx, my), (mx, 1 - my), (1 - mx, 1 - my)]

    def slot(px, py, pc):
        return out_ref.at[4 * px + 2 * py + pc]

    def copy(k, block, to, src=None):
        return pltpu.make_async_remote_copy(
            src_ref=slot(*block) if src is None else src, dst_ref=slot(*block), send_sem=send_sems.at[k],
            recv_sem=recv_sems.at[k], device_id=to, device_id_type=pl.DeviceIdType.MESH)

    def first():
        return [copy(0, me, sibling, src=x_ref)] + [copy(1 + j, me, (*chip, mc), src=x_ref) for j, chip in enumerate(chips)]

    def passed():
        return [copy(4 + j, (*chip, mc), sibling) for j, chip in enumerate(chips)]

    def start():
        pltpu.make_async_copy(x_ref, slot(*me), local_sem).start()
        for cp in first():
            cp.start()

    def forward():
        for j, chip in enumerate(chips):
            copy(1 + j, (*chip, mc), me).wait_recv()
            passed()[j].start()

    def finish():
        copy(0, sibling, me).wait_recv()
        for j, chip in enumerate(chips):
            copy(4 + j, (*chip, 1 - mc), me).wait_recv()
        for cp in first() + passed():
            cp.wait_send()
        pltpu.make_async_copy(x_ref, slot(*me), local_sem).wait()

    return start, forward, finish


def _exchange_phases(x_ref, out_ref, send_sems, recv_sems, local_sem):
    mx, my, mc = lax.axis_index("x"), lax.axis_index("y"), lax.axis_index("c")
    me = 4 * mx + 2 * my + mc

    def peer(k):
        return mx ^ (k >> 2), my ^ ((k >> 1) & 1), mc ^ (k & 1)

    def sends():
        out = []
        for k in range(1, NDEV):
            px, py, pc = peer(k)
            out.append(pltpu.make_async_remote_copy(
                src_ref=x_ref.at[4 * px + 2 * py + pc], dst_ref=out_ref.at[me], send_sem=send_sems.at[k - 1],
                recv_sem=recv_sems.at[k - 1], device_id=(px, py, pc), device_id_type=pl.DeviceIdType.MESH))
        return out

    def start():
        pltpu.make_async_copy(x_ref.at[me], out_ref.at[me], local_sem).start()
        for cp in sends():
            cp.start()

    def finish():
        for k in range(1, NDEV):
            px, py, pc = peer(k)
            pltpu.make_async_remote_copy(
                src_ref=x_ref.at[me], dst_ref=out_ref.at[4 * px + 2 * py + pc], send_sem=send_sems.at[k - 1],
                recv_sem=recv_sems.at[k - 1], device_id=(px, py, pc), device_id_type=pl.DeviceIdType.MESH).wait_recv()
        for cp in sends():
            cp.wait_send()
        pltpu.make_async_copy(x_ref.at[me], out_ref.at[me], local_sem).wait()

    return start, finish


def all_gather(name, x):
    def body(x_ref, out_ref, send_sems, recv_sems, local_sem):
        for phase in _gather_phases(x_ref, out_ref, send_sems, recv_sems, local_sem):
            phase()

    return pl.pallas_call(body, out_shape=SDS((NDEV,) + x.shape, x.dtype), in_specs=[HBM_SPEC], out_specs=HBM_SPEC,
                          scratch_shapes=_comm_scratch(), name=name)(x)


def all_to_all(name, x):
    def body(x_ref, out_ref, send_sems, recv_sems, local_sem):
        for phase in _exchange_phases(x_ref, out_ref, send_sems, recv_sems, local_sem):
            phase()

    return pl.pallas_call(body, out_shape=SDS(x.shape, x.dtype), in_specs=[HBM_SPEC], out_specs=HBM_SPEC,
                          scratch_shapes=_comm_scratch(), name=name)(x)


def _pack(arrs, dtype, row_mult=8):
    segs = []
    for a in arrs:
        flat = a.reshape(-1).astype(dtype)
        segs.append(jnp.pad(flat, (0, (-flat.shape[0]) % ROW)))
    flat = jnp.concatenate(segs)
    flat = jnp.pad(flat, (0, (-flat.shape[0]) % (ROW * row_mult)))
    return flat.reshape(-1, ROW)


def _unpack(buf, shapes):
    flat = buf.reshape(-1)
    out, off = [], 0
    for s in shapes:
        n = math.prod(s)
        out.append(flat[off:off + n].reshape(s))
        off += n + (-n) % ROW
    return out


def _pack_rows(arrs, axis):
    padded = []
    for t in arrs:
        pad = [(0, 0)] * t.ndim
        pad[axis] = (0, _tile_rows(t.shape[axis]) - t.shape[axis])
        padded.append(jnp.pad(t, pad))
    return jnp.concatenate(padded, axis=axis)


def _tile_rows(r):
    return r + (-r) % 16


def _unpack8(buf, shapes):
    flat = buf.reshape(NDEV, -1)
    out, off = [], 0
    for s in shapes:
        n = math.prod(s)
        out.append(flat[:, off:off + n].reshape((NDEV,) + tuple(s)))
        off += n + (-n) % ROW
    return out


def kernel(x, c, w_ada, b_ada, g_ffn1, w1_ffn1, w3_ffn1, w2_ffn1, g_mix, w_in, conv_qkv, a_log, dt_bias, g_onorm, lam_re, lam_im, log_step, b_re, b_im, c_re, c_im, d_skip, w_glu, b_glu, w_proj_a, w_proj_b, w_out, g_ffn2, w1_ffn2, w3_ffn2, w2_ffn2, g_final, loss_target, m_w_ada, m_b_ada, m_g_ffn1, m_w1_ffn1, m_w3_ffn1, m_w2_ffn1, m_g_mix, m_w_in, m_conv_qkv, m_a_log, m_dt_bias, m_g_onorm, m_lam_re, m_lam_im, m_log_step, m_b_re, m_b_im, m_c_re, m_c_im, m_d_skip, m_w_glu, m_b_glu, m_w_proj_a, m_w_proj_b, m_w_out, m_g_ffn2, m_w1_ffn2, m_w3_ffn2, m_w2_ffn2, m_g_final, v_w_ada, v_b_ada, v_g_ffn1, v_w1_ffn1, v_w3_ffn1, v_w2_ffn1, v_g_mix, v_w_in, v_conv_qkv, v_a_log, v_dt_bias, v_g_onorm, v_lam_re, v_lam_im, v_log_step, v_b_re, v_b_im, v_c_re, v_c_im, v_d_skip, v_w_glu, v_b_glu, v_w_proj_a, v_w_proj_b, v_w_out, v_g_ffn2, v_w1_ffn2, v_w3_ffn2, v_w2_ffn2, v_g_final):
    a = dict(locals())
    bl, seq, _ = x.shape
    t_rows = bl * seq
    nc = seq // CH
    me = 4 * lax.axis_index("x") + 2 * lax.axis_index("y") + lax.axis_index("c")
    tm_ew = _pick(seq, (256, 128, 64))

    sm = all_gather("gather_small", _pack([c, conv_qkv[0]], F32))
    c_loc, conv_loc = _unpack8(sm, [c.shape, conv_qkv.shape[1:]])
    c_all = c_loc.reshape(NDEV * bl, D)
    conv_full = conv_loc.transpose(1, 0, 2).reshape(CONVW, 3 * DNW)
    loc = {n: (a[n][0].T if n in COL_SHARDED else a[n][0]) for n in RS_WEIGHTS}
    wfull, gw, res = {}, {}, {}

    def pack_local(names):
        return _pack_rows([loc[n].astype(BF16).reshape(-1, ROW) for n in names], 0)

    def unpack_full(buf, names):
        r0 = 0
        for n in names:
            r = loc[n].size // ROW
            wfull[n] = buf[:, r0:r0 + r, :].reshape(-1, loc[n].shape[1])
            r0 += _tile_rows(r)

    def pack_grads(names):
        return _pack_rows([gw[n].astype(BF16).reshape(NDEV, -1, ROW) for n in names], 1)

    def update(buf, names):
        r0 = 0
        for n in names:
            r = loc[n].size // ROW
            parts = buf[:, r0:r0 + r, :].reshape((NDEV,) + loc[n].shape)
            r0 += _tile_rows(r)
            step = adamw_t if n in COL_SHARDED else adamw
            out = step("adamw_" + n, parts, a[n][0], a["m_" + n][0], a["v_" + n][0])
            for kind, t in zip(("grad", "delta", "new_m", "new_v"), out):
                res[kind + "_" + n] = t[None]

    unpack_full(all_gather("gather_ffn1", pack_local(G_FFN1)), G_FFN1)

    n_ada = w_ada.shape[2]
    mod_part = ada_fwd(c_all, w_ada[0], lax.dynamic_slice(b_ada, (0, me * n_ada), (1, n_ada)))
    mod_all = all_gather("gather_mod", mod_part).transpose(1, 0, 2).reshape(NDEV * bl, 9 * D)
    mod = lax.dynamic_slice(mod_all, (me * bl, 0), (bl, 9 * D)).reshape(bl, 9, D)
    mods = [mod[:, k:k + 1, :] for k in range(9)]

    h0 = x.reshape(t_rows, D)
    h1, f1, u1, wg_rest = ffn_fwd("ffn1_fwd", h0, mod[:, 0:3, :], g_ffn1, wfull['w1_ffn1'], wfull['w3_ffn1'],
                                  wfull['w2_ffn1'], seq, gather=pack_local(G_MIX + G_FFN2))
    unpack_full(wg_rest, G_MIX + G_FFN2)
    win = wfull['w_in']
    o_small, o_s5, o_gate = 4 * DNW, 4 * DNW + 2 * NH, 4 * DNW + 2 * NH + S5W
    w_dn, w_small = win[:o_small], jnp.pad(win[o_small:o_s5], ((0, LANES - 2 * NH), (0, 0)))
    w_s5, w_gate = win[o_s5:o_gate], win[o_gate:]
    (u2,) = ew_call("mix_norm", fn_normmod, [h1], [mods[3], mods[4]], [g_mix], [(D, BF16)], tm_ew, seq)
    p_dn = mm("proj_dn", [(u2, w_dn)], True, F32)
    p_small = mm("proj_small", [(u2, w_small)], True, F32)
    p_s5 = mm("proj_s5", [(u2, w_s5)], True, F32)
    p_gate = mm("proj_gate", [(u2, w_gate)], True, F32)

    conv8 = jnp.pad(conv_full, ((0, 8 - CONVW), (0, 0)))
    alp = jnp.pad(a_log, ((0, 0), (NH, LANES - 2 * NH)))
    dtp = jnp.pad(dt_bias, ((0, 0), (NH, LANES - 2 * NH)))
    nb_dn = DN_ROWS if bl % DN_ROWS == 0 else 1
    p_dn3, p_small3 = p_dn.reshape(bl, seq, 4 * DNW), p_small.reshape(bl, seq, LANES)
    qkv3 = dn_prep_fwd(p_dn3, conv8)
    o_pre3, sprev, tinv = deltanet_fwd(qkv3, p_small3, alp, dtp, nb_dn)
    o_pre = o_pre3.reshape(t_rows, DNW)
    z_raw = p_dn[:, 3 * DNW:]
    (oa,) = ew_call("dn_onorm", fn_onorm, [o_pre, z_raw], [], [g_onorm], [(DNW, BF16)], tm_ew, seq)
    ya = mm("proj_a", [(oa, wfull['w_proj_a'])], True, F32)

    s5_params = [lam_re.reshape(1, S5N), lam_im.reshape(1, S5N), log_step,
                 b_re[0].transpose(2, 0, 1).reshape(S5C, S5N), b_im[0].transpose(2, 0, 1).reshape(S5C, S5N),
                 c_re[0].transpose(1, 0, 2).reshape(S5C, S5N), c_im[0].transpose(1, 0, 2).reshape(S5C, S5N)]
    tables = s5_tables_fwd(s5_params)
    p_s53 = p_s5.reshape(bl, seq, S5W)
    y_s53, xs = s5_fwd(p_s53, tables, d_skip)
    y_s5 = y_s53.reshape(t_rows, S5W)
    (ob,) = ew_call("s5_glu", fn_glu, [y_s5], [], [wfull['w_glu'], b_glu], [(S5W, BF16)], tm_ew, seq)
    yb = mm("proj_b", [(ob, wfull['w_proj_b'])], True, F32)

    (merged,) = ew_call("merge", fn_merge, [p_gate, ya, yb], [], [], [(D, BF16)], tm_ew, seq)
    mo = mm("proj_out", [(merged, wfull['w_out'])], False, F32)
    (h2,) = ew_call("mix_resid", lambda p, q, gt: (q + gt * p,), [mo, h1], [mods[5]], [], [(D, F32)], tm_ew, seq)
    h3, f3, u3 = ffn_fwd("ffn2_fwd", h2, mod[:, 6:9, :], g_ffn2, wfull['w1_ffn2'], wfull['w3_ffn2'], wfull['w2_ffn2'], seq)

    dh3, dg_final, loss_part = loss_head(h3, loss_target.reshape(t_rows, D), g_final.reshape(1, D), seq)
    loss = lax.psum(loss_part[0, 0], ("x", "y", "c"))

    dh2, a3, d1_3, d3_3, df3, dmod_c, dg_ffn2 = ffn_bwd("ffn2_bwd", dh3, h2, f3, u3, mod[:, 6:9, :], g_ffn2, wfull['w1_ffn2'],
                                                   wfull['w3_ffn2'], wfull['w2_ffn2'], seq)
    gw['w1_ffn2'] = mm_tn("gw1_ffn2", d1_3, u3)
    gw['w3_ffn2'] = mm_tn("gw3_ffn2", d3_3, u3)
    gw['w2_ffn2'] = mm_tn("gw2_ffn2", a3, df3)

    (dmo,), (dgt2,), _ = ew_vjp_call("mix_resid_bwd", fn_resid, [mo], [mods[5]], [], [dh2], [(0, BF16)], tm_ew, seq)
    gw['w_out'] = mm_tn("gw_out", merged, dmo)
    d_merged = mm("d_merged", [(dmo, wfull['w_out'])], True, F32)
    (d_gate, d_ya, d_yb), _, _ = ew_vjp_call("merge_bwd", fn_merge, [p_gate, ya, yb], [], [], [d_merged],
                                             [(0, BF16), (1, BF16), (2, BF16)], tm_ew, seq)
    gw['w_proj_a'] = mm_tn("gw_proj_a", d_ya, oa)
    gw['w_proj_b'] = mm_tn("gw_proj_b", d_yb, ob)
    d_oa = mm("d_oa", [(d_ya, wfull['w_proj_a'])], False, F32)
    d_ob = mm("d_ob", [(d_yb, wfull['w_proj_b'])], False, F32)

    (d_opre, d_z), _, (dg_onorm,) = ew_vjp_call("dn_onorm_bwd", fn_onorm, [o_pre, z_raw], [], [g_onorm], [d_oa],
                                                [(0, F32), (1, F32)], tm_ew, seq)
    d_qkv3, d_psmall3, d_alp, d_dtp, rs_ffn2 = deltanet_bwd(
        qkv3, p_small3, alp, dtp, sprev, tinv, d_opre.reshape(bl, seq, DNW), nb_dn, exchange=pack_grads(G_FFN2))
    d_pdn3, d_conv8 = dn_prep_bwd(p_dn3, conv8, d_qkv3, d_z.reshape(bl, seq, DNW))
    d_pdn, d_psmall = d_pdn3.reshape(t_rows, 4 * DNW), d_psmall3.reshape(t_rows, LANES)

    (d_ys5,), _, (g_wglu, dg_bglu) = ew_vjp_call("s5_glu_bwd", fn_glu, [y_s5], [], [wfull['w_glu'], b_glu], [d_ob],
                                                 [(0, F32)], tm_ew, seq)
    gw['w_glu'] = g_wglu
    s5_out = s5_bwd(p_s53, tables, d_skip, xs, d_ys5.reshape(bl, seq, S5W))
    d_ps5, d_tables, dg_dskip = s5_out[0].reshape(t_rows, S5W), s5_out[1:11], s5_out[11]
    d_s5p = s5_tables_bwd(s5_params, d_tables)

    d_pdn_b, d_psm_b, d_ps5_b = d_pdn, d_psmall, d_ps5
    gw['w_in'] = jnp.concatenate([mm_tn("gw_dn", d_pdn_b, u2), mm_tn("gw_small", d_psm_b, u2)[:2 * NH],
                                  mm_tn("gw_s5", d_ps5_b, u2), mm_tn("gw_gate", d_gate, u2)], axis=0)
    du2 = mm("d_u2", [(d_pdn_b, w_dn), (d_psm_b, w_small), (d_ps5_b, w_s5), (d_gate, w_gate)], False, F32)
    (dh1,), (dsh2, dsc2), (dg_mix,) = ew_vjp_call("mix_norm_bwd", fn_normmod, [h1], [mods[3], mods[4]], [g_mix], [du2],
                                                  [(0, F32)], tm_ew, seq, addend=dh2)

    dh0, a1, d1_1, d3_1, df1, dmod_a, dg_ffn1, rs_mix = ffn_bwd(
        "ffn1_bwd", dh1, h0, f1, u1, mod[:, 0:3, :], g_ffn1, wfull['w1_ffn1'], wfull['w3_ffn1'], wfull['w2_ffn1'], seq,
        exchange=pack_grads(G_MIX))
    gw['w1_ffn1'] = mm_tn("gw1_ffn1", d1_1, u1)
    gw['w3_ffn1'], rs_w1 = mm_tn("gw3_ffn1", d3_1, u1, exchange=pack_grads(['w1_ffn1']))
    gw['w2_ffn1'], rs_w3 = mm_tn("gw2_ffn1", a1, df1, exchange=pack_grads(['w3_ffn1']))

    update(rs_ffn2, G_FFN2)
    update(rs_mix, G_MIX)
    update(rs_w1, ['w1_ffn1'])
    update(rs_w3, ['w3_ffn1'])
    update(all_to_all("scatter_w2_ffn1", pack_grads(['w2_ffn1'])), ['w2_ffn1'])

    dmod_mine = jnp.concatenate([dmod_a, dsh2, dsc2, dgt2, dmod_c], axis=1).reshape(bl, 9 * D)
    small_grads = {
        'g_ffn1': dg_ffn1, 'g_mix': dg_mix, 'a_log': d_alp[:, NH:2 * NH], 'dt_bias': d_dtp[:, NH:2 * NH],
        'g_onorm': dg_onorm, 'lam_re': d_s5p[0].reshape(1, S5G, S5P), 'lam_im': d_s5p[1].reshape(1, S5G, S5P),
        'log_step': d_s5p[2],
        'b_re': d_s5p[3].reshape(S5C, S5G, S5P).transpose(1, 2, 0)[None],
        'b_im': d_s5p[4].reshape(S5C, S5G, S5P).transpose(1, 2, 0)[None],
        'c_re': d_s5p[5].reshape(S5C, S5G, S5P).transpose(1, 0, 2)[None],
        'c_im': d_s5p[6].reshape(S5C, S5G, S5P).transpose(1, 0, 2)[None],
        'd_skip': dg_dskip, 'b_glu': dg_bglu, 'g_ffn2': dg_ffn2, 'g_final': dg_final.reshape(D)}
    small_shapes = [a[n].shape for n in SMALL]
    small_pack = _pack([small_grads[n] for n in SMALL], F32)
    n_small = small_pack.shape[0]
    sg = all_gather("gather_small_grads",
                    jnp.concatenate([small_pack, _pack([dmod_mine, d_conv8[:CONVW]], F32)], axis=0))
    pieces = _unpack8(sg[:, n_small:, :], [dmod_mine.shape, (CONVW, 3 * DNW)])
    dmod_all = pieces[0].reshape(NDEV * bl, 9 * D)
    g_wada, g_bada = ada_bwd(c_all, lax.dynamic_slice(dmod_all, (0, me * n_ada), (NDEV * bl, n_ada)), dmod_all)

    n_conv = conv_qkv.shape[2]
    conv_parts = lax.dynamic_slice(pieces[1], (0, 0, me * n_conv), (NDEV, CONVW, n_conv))
    conv_parts = jnp.pad(conv_parts.reshape(NDEV, 1, -1), ((0, 0), (0, 7), (0, 0)))
    pad8 = lambda t: jnp.pad(t.reshape(1, -1), ((0, 7), (0, 0)))
    conv_res = adamw("adamw_conv", conv_parts, pad8(conv_qkv), pad8(m_conv_qkv), pad8(v_conv_qkv))
    for kind, buf in zip(("grad", "delta", "new_m", "new_v"), conv_res):
        res[kind + "_conv_qkv"] = buf[0].reshape(conv_qkv.shape)

    small_res = adamw("adamw_small", sg[:, :n_small, :], *[_pack([a[p + n] for n in SMALL], F32) for p in ("", "m_", "v_")])
    for kind, buf in zip(("grad", "delta", "new_m", "new_v"), small_res):
        for n, t in zip(SMALL, _unpack(buf, small_shapes)):
            res[kind + "_" + n] = t

    for n, g in (("w_ada", g_wada), ("b_ada", g_bada)):
        shp = a[n].shape
        r2 = lambda t: t.reshape(-1, shp[-1]) if n == "w_ada" else pad8(t)
        out = adamw("adamw_" + n, r2(g)[None], r2(a[n]), r2(a["m_" + n]), r2(a["v_" + n]))
        for kind, buf in zip(("grad", "delta", "new_m", "new_v"), out):
            res[kind + "_" + n] = (buf if n == "w_ada" else buf[0:1]).reshape(shp)

    outs = [loss, dh0.reshape(x.shape)]
    for kind in ("grad", "delta", "new_m", "new_v"):
        outs += [res[kind + "_" + n] for n in WEIGHTS]
    return tuple(outs)
```

```python
import functools
import math

import jax
import jax.numpy as jnp
from jax import lax
from jax.experimental import pallas as pl
from jax.experimental.pallas import tpu as pltpu

F32 = jnp.float32
BF16 = jnp.bfloat16
HI = lax.Precision.HIGHEST
H3 = lax.Precision.HIGH
SDS = jax.ShapeDtypeStruct

D = 1024
FF = 2816
FFN_TF = FF // 2
FFN_BWD_TM = 256
NH = 8
DH = 64
DNW = NH * DH
CONVW = 4
CH = 64
DN_ROWS = 2
S5W = 512
S5G = 32
S5P = 64
S5C = 16
S5N = S5G * S5P
GB = 4
NDEV = 8
EPS = 1e-6
LANES = 128
ROW = 1024
VMEM_LIMIT = 56 * 1024 * 1024

ADAM_LR, ADAM_B1, ADAM_B2, ADAM_EPS, ADAM_WD, ADAM_STEP = 0.001, 0.9, 0.999, 1e-08, 0.01, 10

WEIGHTS = ['w_ada', 'b_ada', 'g_ffn1', 'w1_ffn1', 'w3_ffn1', 'w2_ffn1', 'g_mix', 'w_in', 'conv_qkv', 'a_log',
           'dt_bias', 'g_onorm', 'lam_re', 'lam_im', 'log_step', 'b_re', 'b_im', 'c_re', 'c_im', 'd_skip', 'w_glu',
           'b_glu', 'w_proj_a', 'w_proj_b', 'w_out', 'g_ffn2', 'w1_ffn2', 'w3_ffn2', 'w2_ffn2', 'g_final']
RS_WEIGHTS = ['w1_ffn1', 'w3_ffn1', 'w2_ffn1', 'w_in', 'w_glu', 'w_proj_a', 'w_proj_b', 'w_out', 'w1_ffn2', 'w3_ffn2',
              'w2_ffn2']
COL_SHARDED = {'w1_ffn1', 'w3_ffn1', 'w_in', 'w_proj_a', 'w_proj_b', 'w1_ffn2', 'w3_ffn2'}
G_FFN1 = ['w1_ffn1', 'w3_ffn1', 'w2_ffn1']
G_MIX = ['w_in', 'w_glu', 'w_proj_a', 'w_proj_b', 'w_out']
G_FFN2 = ['w1_ffn2', 'w3_ffn2', 'w2_ffn2']
SMALL = ['g_ffn1', 'g_mix', 'a_log', 'dt_bias', 'g_onorm', 'lam_re', 'lam_im', 'log_step', 'b_re', 'b_im', 'c_re',
         'c_im', 'd_skip', 'b_glu', 'g_ffn2', 'g_final']


def _cp(n_grid=0):
    if n_grid:
        return pltpu.CompilerParams(vmem_limit_bytes=VMEM_LIMIT, dimension_semantics=("arbitrary",) * n_grid)
    return pltpu.CompilerParams(vmem_limit_bytes=VMEM_LIMIT)


def _dot(a, b):
    return jnp.dot(a.astype(BF16), b.astype(BF16), preferred_element_type=F32)


def _dot_nt(a, b):
    return lax.dot_general(a.astype(BF16), b.astype(BF16), (((1,), (1,)), ((), ())), preferred_element_type=F32)


def _dot_tn(a, b):
    return lax.dot_general(a.astype(BF16), b.astype(BF16), (((0,), (0,)), ((), ())), preferred_element_type=F32)


def _dot_hi(a, b):
    return jnp.dot(a, b, precision=HI, preferred_element_type=F32)


def _dot_h3(a, b):
    return jnp.dot(a, b, precision=H3, preferred_element_type=F32)


@jax.custom_vjp
def bdot(a, b):
    return _dot(a, b)


bdot.defvjp(lambda a, b: (_dot(a, b), (a, b)),
            lambda r, g: (_dot_nt(g, r[1]).astype(r[0].dtype), _dot_tn(r[0], g).astype(r[1].dtype)))


@jax.custom_vjp
def bdot_nt(a, b):
    return _dot_nt(a, b)


bdot_nt.defvjp(lambda a, b: (_dot_nt(a, b), (a, b)),
               lambda r, g: (_dot(g, r[1]).astype(r[0].dtype), _dot_tn(g, r[0]).astype(r[1].dtype)))


@jax.custom_vjp
def bdot_tn(a, b):
    return _dot_tn(a, b)


bdot_tn.defvjp(lambda a, b: (_dot_tn(a, b), (a, b)),
               lambda r, g: (_dot_nt(r[1], g).astype(r[0].dtype), _dot(r[0], g).astype(r[1].dtype)))


def _silu(x):
    return x * jax.nn.sigmoid(x)


def _iota2(shape, axis):
    return lax.broadcasted_iota(jnp.int32, shape, axis)


def normmod(h, g, sc, sh):
    y = h * lax.rsqrt(jnp.mean(h * h, axis=-1, keepdims=True) + EPS) * g
    return y * (1.0 + sc) + sh


def fn_normmod(h, sh, sc, g):
    return (normmod(h, g, sc, sh),)


def fn_resid(mo, gt):
    return (gt * mo,)


def fn_merge(gate, ya, yb):
    return (jax.nn.sigmoid(gate[:, :D]) * ya + jax.nn.sigmoid(gate[:, D:]) * yb,)


def fn_glu(y, w, b):
    ge = jax.nn.gelu(y)
    return (ge * jax.nn.sigmoid(bdot(ge, w) + b),)


def fn_onorm(o, z, g_on):
    r = _iota2((DH, DNW), 0)
    c = _iota2((DH, DNW), 1)
    expand = (c % DH == r).astype(F32)
    r2 = _iota2((DNW, DNW), 0)
    c2 = _iota2((DNW, DNW), 1)
    avg = (r2 // DH == c2 // DH).astype(F32) * (1.0 / DH)
    ms = _dot_h3(o * o, avg)
    return (o * lax.rsqrt(ms + EPS) * _dot_hi(g_on, expand) * _silu(z),)


def gate_fn(small, alp, dtp):
    beta = jax.nn.sigmoid(small)
    la = -jnp.exp(alp) * jax.nn.softplus(small + dtp)
    tri = (_iota2((CH, CH), 0) >= _iota2((CH, CH), 1)).astype(F32)
    gc = _dot_hi(tri, la)
    gct = lax.dot_general(la, tri, (((0,), (1,)), ((), ())), precision=HI, preferred_element_type=F32)
    return beta, gc, gct


def _bdg(a, b, ca, cb, hi):
    if not hi:
        a, b = a.astype(BF16), b.astype(BF16)
    return lax.dot_general(a, b, (((ca,), (cb,)), ((0,), (0,))), precision=H3 if hi else None,
                           preferred_element_type=F32)


def _batched_matmuls(hi):
    nn_ = lambda a, b: _bdg(a, b, 2, 1, hi)
    nt_ = lambda a, b: _bdg(a, b, 2, 2, hi)
    tn_ = lambda a, b: _bdg(a, b, 1, 1, hi)
    nn = jax.custom_vjp(nn_)
    nn.defvjp(lambda a, b: (nn_(a, b), (a, b)), lambda r, g: (nt_(g, r[1]), tn_(r[0], g)))
    nt = jax.custom_vjp(nt_)
    nt.defvjp(lambda a, b: (nt_(a, b), (a, b)), lambda r, g: (nn_(g, r[1]), tn_(g, r[0])))
    tn = jax.custom_vjp(tn_)
    tn.defvjp(lambda a, b: (tn_(a, b), (a, b)), lambda r, g: (nt_(r[1], g), nn_(r[0], g)))
    return nn, nt, tn


bnn, bnt, btn = _batched_matmuls(False)
hnn, hnt, htn = _batched_matmuls(True)


def _unit_lower_inverse(a):
    r = _iota2((1, CH, CH), 1)
    c = _iota2((1, CH, CH), 2)
    eye = (r == c).astype(F32)
    d = jnp.where(r // 8 == c // 8, a, 0.0)
    inv = eye - d
    p = d
    for _ in range(2):
        p = hnn(p, p)
        inv = inv + hnn(inv, p)
    for blk in (16, 32, 64):
        off = jnp.where((r // blk == c // blk) & (r // (blk // 2) != c // (blk // 2)), a, 0.0)
        inv = inv - hnn(hnn(inv, off), inv)
    return inv


@jax.custom_vjp
def _inverse_given(a, t):
    return t


_inverse_given.defvjp(lambda a, t: (t, t), lambda t, g: (-hnt(htn(t, g), t), jnp.zeros_like(t)))


def dn_prep(xc, w):
    t = xc.shape[0] - 8
    c = xc[5:5 + t] * w[0:1] + xc[6:6 + t] * w[1:2] + xc[7:7 + t] * w[2:3] + xc[8:8 + t] * w[3:4]
    act = _silu(c)
    q, k, v = act[:, :DNW], act[:, DNW:2 * DNW], act[:, 2 * DNW:]
    ones = (_iota2((DNW, DNW), 0) // DH == _iota2((DNW, DNW), 1) // DH).astype(F32)
    q = q * lax.rsqrt(_dot_h3(q * q, ones) + EPS) * (DH ** -0.5)
    k = k * lax.rsqrt(_dot_h3(k * k, ones) + EPS)
    return jnp.concatenate([q, k, v], axis=1)


def dn_chunk(q, k, v, b, g, gt, s_prev, t_saved=None):
    r = _iota2((1, CH, CH), 1)
    c = _iota2((1, CH, CH), 2)
    causal = r >= c
    dec = jnp.where(causal, jnp.exp(jnp.where(causal, g - gt, 0.0)), 0.0)
    kb = k * b
    qk = bnt(jnp.concatenate([q, kb], axis=1), k)
    attn = qk[:, :CH] * dec
    a = jnp.where(r > c, qk[:, CH:] * dec, 0.0)
    tinv = _unit_lower_inverse(a) if t_saved is None else _inverse_given(a, t_saved)
    eg = jnp.exp(g)
    uw = hnn(tinv, jnp.concatenate([v * b, kb * eg], axis=2))
    g_last = g[:, CH - 1:CH]
    ws = bnn(jnp.concatenate([uw[..., DH:], q * eg], axis=1), s_prev)
    v_new = uw[..., :DH] - ws[:, :CH]
    o = ws[:, CH:] + bnn(attn, v_new)
    s_new = s_prev * jnp.exp(g_last) + btn(k * jnp.exp(g_last - g), v_new)
    return o, s_new, tinv


def s5_chunk(u, xp_re, xp_im, bb_re, bb_im, cc_re, cc_im, p0r, p0i, p1r, p1i, pir, pii, dsk):
    nb = u.shape[0]
    u2 = u.reshape(nb * CH, LANES)
    bu_re = bdot(u2, bb_re).reshape(nb, CH, 512)
    bu_im = bdot(u2, bb_im).reshape(nb, CH, 512)
    xt_re = pir * bu_re - pii * bu_im
    xt_im = pir * bu_im + pii * bu_re
    tri = jnp.broadcast_to((_iota2((1, CH, CH), 1) >= _iota2((1, CH, CH), 2)).astype(F32), (nb, CH, CH))
    cs_re = hnn(tri, xt_re)
    cs_im = hnn(tri, xt_im)
    x_re = p0r * cs_re - p0i * cs_im + p1r * xp_re - p1i * xp_im
    x_im = p0r * cs_im + p0i * cs_re + p1r * xp_im + p1i * xp_re
    y = bdot_nt(x_re.reshape(nb * CH, 512), cc_re) - bdot_nt(x_im.reshape(nb * CH, 512), cc_im) + dsk * u2
    return y.reshape(nb, CH, LANES), x_re[:, CH - 1:CH], x_im[:, CH - 1:CH]


def s5_tables(lam_re, lam_im, log_step, bre, bim, cre, cim):
    expand = (_iota2((S5G, S5N), 1) // S5P == _iota2((S5G, S5N), 0)).astype(F32)
    step = _dot_hi(jnp.exp(log_step), expand)
    lre = jnp.minimum(lam_re, -1e-4)
    lr = lre * step
    ang = lam_im * step
    mag = jnp.exp(lr)
    lb_re = mag * jnp.cos(ang)
    lb_im = mag * jnp.sin(ang)
    den = lre * lre + lam_im * lam_im
    coef_re = ((lb_re - 1.0) * lre + lb_im * lam_im) / den
    coef_im = (lb_im * lre - (lb_re - 1.0) * lam_im) / den
    bb_re = coef_re * bre - coef_im * bim
    bb_im = coef_re * bim + coef_im * bre
    j = _iota2((CH, 1), 0).astype(F32)
    e0 = jnp.exp(j * lr)
    e1 = jnp.exp((j + 1.0) * lr)
    ei = jnp.exp(-j * lr)
    mask = (_iota2((LANES, 512), 0) // S5C == _iota2((LANES, 512), 1) // S5P).astype(F32)

    def blocks(t):
        return jnp.concatenate([(jnp.tile(t[:, gb * 512:(gb + 1) * 512], (LANES // S5C, 1)) * mask)[None]
                                for gb in range(GB)], axis=0)

    return (blocks(bb_re), blocks(bb_im), blocks(cre), blocks(cim),
            e0 * jnp.cos(j * ang), e0 * jnp.sin(j * ang),
            e1 * jnp.cos((j + 1.0) * ang), e1 * jnp.sin((j + 1.0) * ang),
            ei * jnp.cos(j * ang), -ei * jnp.sin(j * ang))


def _row_specs(tiled, batch, bcast, tm, tpb):
    specs = [pl.BlockSpec((tm, a.shape[1]), lambda i: (i, 0)) for a in tiled]
    specs += [pl.BlockSpec((None,) + a.shape[1:], lambda i: (i // tpb, 0, 0)) for a in batch]
    specs += [pl.BlockSpec(a.shape, lambda i, nd=a.ndim: (0,) * nd) for a in bcast]
    return specs


def ew_call(name, fn, tiled, batch, bcast, outs, tm, seq):
    t_rows = tiled[0].shape[0]
    n_in = len(tiled) + len(batch) + len(bcast)

    def body(*refs):
        vals = [r[...].astype(F32) for r in refs[:n_in]]
        for r, o in zip(refs[n_in:], fn(*vals)):
            r[...] = o.astype(r.dtype)

    return pl.pallas_call(
        body, grid=(t_rows // tm,), in_specs=_row_specs(tiled, batch, bcast, tm, seq // tm),
        out_specs=[pl.BlockSpec((tm, w), lambda i: (i, 0)) for w, _ in outs],
        out_shape=[SDS((t_rows, w), dt) for w, dt in outs], name=name, compiler_params=_cp(1))(*tiled, *batch, *bcast)


def ew_vjp_call(name, fn, tiled, batch, bcast, cts, want, tm, seq, addend=None):
    t_rows = tiled[0].shape[0]
    tpb = seq // tm
    n_t, n_b, n_c = len(tiled), len(batch), len(bcast)
    n_in = n_t + n_b + n_c
    extra = [] if addend is None else [addend]

    def body(*refs):
        i = pl.program_id(0)
        vals = [r[...].astype(F32) for r in refs[:n_in]]
        ctv = tuple(r[...].astype(F32) for r in refs[n_in:n_in + len(cts)])
        outs = refs[n_in + len(cts) + len(extra):]
        _, vjp = jax.vjp(fn, *vals)
        grads = vjp(ctv)
        for k, (r, (idx, _)) in enumerate(zip(outs[:len(want)], want)):
            g = grads[idx]
            if k == 0 and extra:
                g = g + refs[n_in + len(cts)][...]
            r[...] = g.astype(r.dtype)
        for k in range(n_b):
            r, g = outs[len(want) + k], grads[n_t + k]

            @pl.when(i % tpb == 0)
            def _(r=r, g=g):
                r[...] = g

            @pl.when(i % tpb != 0)
            def _(r=r, g=g):
                r[...] += g
        for k in range(n_c):
            r, g = outs[len(want) + n_b + k], grads[n_t + n_b + k]

            @pl.when(i == 0)
            def _(r=r, g=g):
                r[...] = g

            @pl.when(i != 0)
            def _(r=r, g=g):
                r[...] += g

    out_specs = [pl.BlockSpec((tm, tiled[idx].shape[1]), lambda i: (i, 0)) for idx, _ in want]
    out_specs += [pl.BlockSpec((None,) + a.shape[1:], lambda i: (i // tpb, 0, 0)) for a in batch]
    out_specs += [pl.BlockSpec(a.shape, lambda i, nd=a.ndim: (0,) * nd) for a in bcast]
    out_shape = [SDS(tiled[idx].shape, dt) for idx, dt in want]
    out_shape += [SDS(a.shape, F32) for a in batch] + [SDS(a.shape, F32) for a in bcast]
    res = pl.pallas_call(
        body, grid=(t_rows // tm,),
        in_specs=_row_specs(tiled, batch, bcast, tm, tpb)
        + [pl.BlockSpec((tm, a.shape[1]), lambda i: (i, 0)) for a in list(cts) + extra],
        out_specs=out_specs, out_shape=out_shape, name=name, compiler_params=_cp(1))(*tiled, *batch, *bcast, *cts, *extra)
    return res[:len(want)], res[len(want):len(want) + n_b], res[len(want) + n_b:]


def _pick(n, cands):
    for c in cands:
        if n % c == 0:
            return c
    return n


def mm(name, pairs, nt, out_dtype):
    m = pairs[0][0].shape[0]
    n = pairs[0][1].shape[0 if nt else 1]
    k_total = sum(a.shape[1] for a, _ in pairs)
    tm = _pick(m, (1024, 512, 256, 128) if k_total <= 2048 else (512, 256, 128))
    tn = _pick(n, (512, 256, 128))
    np_ = len(pairs)

    def body(*refs):
        acc = None
        for p in range(np_):
            a, b = refs[2 * p][...], refs[2 * p + 1][...]
            t = _dot_nt(a, b) if nt else _dot(a, b)
            acc = t if acc is None else acc + t
        refs[2 * np_][...] = acc.astype(out_dtype)

    in_specs, ops = [], []
    for a, b in pairs:
        k = a.shape[1]
        in_specs.append(pl.BlockSpec((tm, k), lambda i, j: (i, 0)))
        in_specs.append(pl.BlockSpec((tn, k), lambda i, j: (j, 0)) if nt else pl.BlockSpec((k, tn), lambda i, j: (0, j)))
        ops += [a, b]
    return pl.pallas_call(
        body, grid=(m // tm, n // tn), in_specs=in_specs, out_specs=pl.BlockSpec((tm, tn), lambda i, j: (i, j)),
        out_shape=SDS((m, n), out_dtype), name=name, compiler_params=_cp(2))(*ops)


def mm_tn(name, a, b, exchange=None):
    t_rows, m = a.shape
    n = b.shape[1]
    tn = n if n <= 1024 else _pick(n, (1024, 512, 256, 128))
    tm = max([t for t in range(LANES, m + 1, LANES) if m % t == 0 and t * tn * 4 <= 6 * 1024 * 1024] or [m])
    tk = _pick(t_rows, (512, 256, 128, 64))
    grid = (m // tm, n // tn, t_rows // tk)
    extra = [] if exchange is None else [exchange]

    def body(*refs):
        a_ref, b_ref = refs[:2]
        o_ref, acc = refs[2 + len(extra)], refs[3 + 2 * len(extra)]
        i, j, k = pl.program_id(0), pl.program_id(1), pl.program_id(2)
        if extra:
            start, finish = _exchange_phases(refs[2], refs[4], *refs[6:9])
            pl.when((i == 0) & (j == 0) & (k == 0))(start)

        @pl.when(k == 0)
        def _():
            acc[...] = jnp.zeros_like(acc)

        acc[...] += _dot_tn(a_ref[...], b_ref[...])

        @pl.when(k == grid[2] - 1)
        def _():
            o_ref[...] = acc[...].astype(BF16)

        if extra:
            pl.when((i == grid[0] - 1) & (j == grid[1] - 1) & (k == grid[2] - 1))(finish)

    res = pl.pallas_call(
        body, grid=grid,
        in_specs=[pl.BlockSpec((tk, tm), lambda i, j, k: (k, i)), pl.BlockSpec((tk, tn), lambda i, j, k: (k, j))]
        + [HBM_SPEC] * len(extra),
        out_specs=[pl.BlockSpec((tm, tn), lambda i, j, k: (i, j))] + [HBM_SPEC] * len(extra),
        out_shape=[SDS((m, n), BF16)] + [SDS(x.shape, x.dtype) for x in extra],
        scratch_shapes=[pltpu.VMEM((tm, tn), F32)] + (_comm_scratch() if extra else []), name=name,
        compiler_params=_cp(3))(a, b, *extra)
    return res if extra else res[0]


def ffn_fwd(name, h, mod3, g, w1, w3, w2, seq, gather=None):
    t_rows = h.shape[0]
    tm = _pick(seq, (512, 256, 128, 64))
    tf = FFN_TF
    tpb = seq // tm
    nf = FF // tf
    nt = t_rows // tm
    extra = [] if gather is None else [gather]

    def body(*refs):
        h_ref, mod_ref, g_ref, w1_ref, w3_ref, w2_ref = refs[:6]
        ho_ref, f_ref, u_ref, h1_ref, h3_ref = refs[6 + len(extra):11 + len(extra)]
        acc = refs[11 + 2 * len(extra)]
        i, j = pl.program_id(0), pl.program_id(1)
        if extra:
            start, forward, finish = _gather_phases(refs[6], refs[12], *refs[14:17])
            pl.when((i == 0) & (j == 0))(start)
            pl.when((i == nt - 1) & (j == 0))(forward)

        @pl.when(j == 0)
        def _():
            u_ref[...] = normmod(h_ref[...], g_ref[...], mod_ref[1:2, :], mod_ref[0:1, :]).astype(BF16)
            acc[...] = jnp.zeros_like(acc)

        u = u_ref[...]
        h1 = _dot_nt(u, w1_ref[...])
        h3 = _dot_nt(u, w3_ref[...])
        h1_ref[...] = h1.astype(BF16)
        h3_ref[...] = h3.astype(BF16)
        acc[...] += _dot(_silu(h1) * h3, w2_ref[...])

        @pl.when(j == nf - 1)
        def _():
            f_ref[...] = acc[...]
            ho_ref[...] = h_ref[...] + 0.5 * mod_ref[2:3, :] * acc[...]

        if extra:
            pl.when((i == nt - 1) & (j == nf - 1))(finish)

    row = lambda i, j: (i, 0)
    return pl.pallas_call(
        body, grid=(nt, nf),
        in_specs=[pl.BlockSpec((tm, D), row), pl.BlockSpec((None, 3, D), lambda i, j: (i // tpb, 0, 0)),
                  pl.BlockSpec((1, D), lambda i, j: (0, 0)), pl.BlockSpec((tf, D), lambda i, j: (j, 0)),
                  pl.BlockSpec((tf, D), lambda i, j: (j, 0)), pl.BlockSpec((tf, D), lambda i, j: (j, 0))]
        + [HBM_SPEC] * len(extra),
        out_specs=[pl.BlockSpec((tm, D), row), pl.BlockSpec((tm, D), row), pl.BlockSpec((tm, D), row),
                   pl.BlockSpec((tm, tf), lambda i, j: (i, j)), pl.BlockSpec((tm, tf), lambda i, j: (i, j))]
        + [HBM_SPEC] * len(extra),
        out_shape=[SDS((t_rows, D), F32), SDS((t_rows, D), F32), SDS((t_rows, D), BF16), SDS((t_rows, FF), BF16),
                   SDS((t_rows, FF), BF16)] + [SDS((NDEV,) + x.shape, x.dtype) for x in extra],
        scratch_shapes=[pltpu.VMEM((tm, D), F32)] + (_comm_scratch() if extra else []), name=name,
        compiler_params=_cp(2))(h, mod3, g, w1, w3, w2, *extra)


def ffn_bwd(name, dho, h, f_out, h1_in, h3_in, mod3, g, w1, w3, w2, seq, exchange=None):
    t_rows = h.shape[0]
    tm = _pick(seq, (FFN_BWD_TM, 128, 64))
    tf = FFN_TF
    tpb = seq // tm
    nf = FF // tf
    nt = t_rows // tm
    extra = [] if exchange is None else [exchange]

    def body(*refs):
        dho_ref, h_ref, f_ref, h1_ref, h3_ref, mod_ref, g_ref, w1_ref, w3_ref, w2_ref = refs[:10]
        dh_ref, a_ref, dh1_ref, dh3_ref, df_scr, dmod_ref, dg_ref = refs[10 + len(extra):17 + len(extra)]
        du_acc = refs[17 + 2 * len(extra)]
        i, j = pl.program_id(0), pl.program_id(1)
        if extra:
            start, finish = _exchange_phases(refs[10], refs[18], *refs[20:23])
            pl.when((i == 0) & (j == 0))(start)

        @pl.when(j == 0)
        def _():
            df_scr[...] = (0.5 * mod_ref[2:3, :] * dho_ref[...]).astype(BF16)
            du_acc[...] = jnp.zeros_like(du_acc)

        h1 = h1_ref[...].astype(F32)
        h3 = h3_ref[...].astype(F32)
        sg = jax.nn.sigmoid(h1)
        s = h1 * sg
        da = _dot_nt(df_scr[...], w2_ref[...])
        dh3 = (da * s).astype(BF16)
        dh1 = (da * h3 * (sg * (1.0 + h1 * (1.0 - sg)))).astype(BF16)
        a_ref[...] = (s * h3).astype(BF16)
        dh1_ref[...] = dh1
        dh3_ref[...] = dh3
        du_acc[...] += _dot(dh1, w1_ref[...]) + _dot(dh3, w3_ref[...])

        @pl.when(j == nf - 1)
        def _():
            _, vjp = jax.vjp(normmod, h_ref[...], g_ref[...], mod_ref[1:2, :], mod_ref[0:1, :])
            dh_n, dg, dsc, dsh = vjp(du_acc[...])
            dh_ref[...] = dho_ref[...] + dh_n
            dgt = jnp.sum(0.5 * dho_ref[...] * f_ref[...], axis=0, keepdims=True)
            dmod = jnp.concatenate([dsh, dsc, dgt], axis=0)

            @pl.when(i % tpb == 0)
            def _():
                dmod_ref[...] = dmod

            @pl.when(i % tpb != 0)
            def _():
                dmod_ref[...] += dmod

            @pl.when(i == 0)
            def _():
                dg_ref[...] = dg

            @pl.when(i != 0)
            def _():
                dg_ref[...] += dg

        if extra:
            pl.when((i == nt - 1) & (j == nf - 1))(finish)

    row = lambda i, j: (i, 0)
    col = lambda i, j: (i, j)
    return pl.pallas_call(
        body, grid=(nt, nf),
        in_specs=[pl.BlockSpec((tm, D), row), pl.BlockSpec((tm, D), row), pl.BlockSpec((tm, D), row),
                  pl.BlockSpec((tm, tf), col), pl.BlockSpec((tm, tf), col),
                  pl.BlockSpec((None, 3, D), lambda i, j: (i // tpb, 0, 0)),
                  pl.BlockSpec((1, D), lambda i, j: (0, 0)), pl.BlockSpec((tf, D), lambda i, j: (j, 0)),
                  pl.BlockSpec((tf, D), lambda i, j: (j, 0)), pl.BlockSpec((tf, D), lambda i, j: (j, 0))]
        + [HBM_SPEC] * len(extra),
        out_specs=[pl.BlockSpec((tm, D), row), pl.BlockSpec((tm, tf), col), pl.BlockSpec((tm, tf), col),
                   pl.BlockSpec((tm, tf), col), pl.BlockSpec((tm, D), row),
                   pl.BlockSpec((None, 3, D), lambda i, j: (i // tpb, 0, 0)), pl.BlockSpec((1, D), lambda i, j: (0, 0))]
        + [HBM_SPEC] * len(extra),
        out_shape=[SDS((t_rows, D), F32), SDS((t_rows, FF), BF16), SDS((t_rows, FF), BF16), SDS((t_rows, FF), BF16),
                   SDS((t_rows, D), BF16), SDS(mod3.shape, F32), SDS((1, D), F32)] + [SDS(x.shape, x.dtype) for x in extra],
        scratch_shapes=[pltpu.VMEM((tm, D), F32)] + (_comm_scratch() if extra else []), name=name,
        compiler_params=_cp(2))(dho, h, f_out, h1_in, h3_in, mod3, g, w1, w3, w2, *extra)


def _dn_cols(part, hd):
    return slice(part * DNW + hd * DH, part * DNW + (hd + 1) * DH)


def _qkv_stacks(qkv_ref, nb):
    pairs = [(b, hd) for b in range(nb) for hd in range(NH)]
    return [jnp.stack([qkv_ref[b, :, _dn_cols(part, hd)] for b, hd in pairs]) for part in range(3)]


def dn_prep_fwd(p_dn, conv8):
    bl, seq, _ = p_dn.shape
    tp = _pick(seq, (256, 128, 64))

    def body(raw_ref, halo_ref, conv_ref, o_ref):
        hm = (pl.program_id(1) > 0).astype(F32)
        o_ref[...] = dn_prep(jnp.concatenate([halo_ref[...] * hm, raw_ref[...]], axis=0), conv_ref[...])

    return pl.pallas_call(
        body, grid=(bl, seq // tp),
        in_specs=[pl.BlockSpec((None, tp, 3 * DNW), lambda b, i: (b, i, 0)),
                  pl.BlockSpec((None, 8, 3 * DNW), lambda b, i: (b, jnp.maximum(i * (tp // 8) - 1, 0), 0)),
                  pl.BlockSpec((8, 3 * DNW), lambda b, i: (0, 0))],
        out_specs=pl.BlockSpec((None, tp, 3 * DNW), lambda b, i: (b, i, 0)),
        out_shape=SDS((bl, seq, 3 * DNW), F32), name="dn_prep_fwd", compiler_params=_cp(2))(p_dn, p_dn, conv8)


def dn_prep_bwd(p_dn, conv8, d_qkv, d_z):
    bl, seq, _ = p_dn.shape
    tp = _pick(seq, (256, 128, 64))
    nt = seq // tp

    def body(raw_ref, halo_ref, conv_ref, dq_ref, dz_ref, draw_ref, dconv_ref, carry):
        b, r = pl.program_id(0), pl.program_id(1)

        @pl.when((b == 0) & (r == 0))
        def _():
            dconv_ref[...] = jnp.zeros_like(dconv_ref)

        @pl.when(r == 0)
        def _():
            carry[...] = jnp.zeros_like(carry)

        hm = (r < nt - 1).astype(F32)
        _, vjp = jax.vjp(dn_prep, jnp.concatenate([halo_ref[...] * hm, raw_ref[...]], axis=0), conv_ref[...])
        dxc, dw = vjp(dq_ref[...])
        tail = dxc[tp:tp + 8] + carry[...]
        draw_ref[:, 0:3 * DNW] = jnp.concatenate([dxc[8:tp], tail], axis=0).astype(BF16)
        draw_ref[:, 3 * DNW:4 * DNW] = dz_ref[...].astype(BF16)
        carry[...] = dxc[0:8] * hm
        dconv_ref[...] += dw

    blk = lambda b, r: (b, nt - 1 - r, 0)
    return pl.pallas_call(
        body, grid=(bl, nt),
        in_specs=[pl.BlockSpec((None, tp, 3 * DNW), blk),
                  pl.BlockSpec((None, 8, 3 * DNW), lambda b, r: (b, jnp.maximum((nt - 1 - r) * (tp // 8) - 1, 0), 0)),
                  pl.BlockSpec((8, 3 * DNW), lambda b, r: (0, 0)), pl.BlockSpec((None, tp, 3 * DNW), blk),
                  pl.BlockSpec((None, tp, DNW), blk)],
        out_specs=[pl.BlockSpec((None, tp, 4 * DNW), blk), pl.BlockSpec((8, 3 * DNW), lambda b, r: (0, 0))],
        out_shape=[SDS((bl, seq, 4 * DNW), BF16), SDS((8, 3 * DNW), F32)],
        scratch_shapes=[pltpu.VMEM((8, 3 * DNW), F32)], name="dn_prep_bwd", compiler_params=_cp(2))(p_dn, p_dn, conv8, d_qkv, d_z)


def _gate_stacks(gates, nb):
    pairs = [(b, hd) for b in range(nb) for hd in range(NH)]
    bs = jnp.stack([gates[b][0][:, hd:hd + 1] for b, hd in pairs])
    gs = jnp.stack([gates[b][1][:, NH + hd:NH + hd + 1] for b, hd in pairs])
    gts = jnp.stack([gates[b][2][NH + hd:NH + hd + 1, :] for b, hd in pairs])
    return bs, gs, gts


def deltanet_fwd(qkv, p_small, alp, dtp, nb):
    bl, seq, _ = qkv.shape
    nc = seq // CH
    ng = nb * NH

    def body(qkv_ref, small_ref, alp_ref, dtp_ref, o_ref, sprev_ref, tinv_ref, s_scr):
        @pl.when(pl.program_id(1) == 0)
        def _():
            s_scr[...] = jnp.zeros_like(s_scr)

        gates = [gate_fn(small_ref[b], alp_ref[...], dtp_ref[...]) for b in range(nb)]
        s_prev = s_scr[...]
        o, s_new, tinv = dn_chunk(*_qkv_stacks(qkv_ref, nb), *_gate_stacks(gates, nb), s_prev)
        sprev_ref[...] = s_prev
        tinv_ref[...] = tinv
        s_scr[...] = s_new
        for b in range(nb):
            for hd in range(NH):
                o_ref[b, :, hd * DH:(hd + 1) * DH] = o[b * NH + hd]

    blk = lambda bb, n: (bb, n, 0)
    const = lambda bb, n: (0, 0)
    saved = pl.BlockSpec((None, ng, DH, DH), lambda bb, n: (bb * nc + n, 0, 0, 0))
    return pl.pallas_call(
        body, grid=(bl // nb, nc),
        in_specs=[pl.BlockSpec((nb, CH, 3 * DNW), blk), pl.BlockSpec((nb, CH, LANES), blk),
                  pl.BlockSpec((1, LANES), const), pl.BlockSpec((1, LANES), const)],
        out_specs=[pl.BlockSpec((nb, CH, DNW), blk), saved, saved],
        out_shape=[SDS((bl, seq, DNW), F32), SDS((bl // nb * nc, ng, DH, DH), F32), SDS((bl // nb * nc, ng, DH, DH), F32)],
        scratch_shapes=[pltpu.VMEM((ng, DH, DH), F32)], name="deltanet_fwd",
        compiler_params=_cp(2))(qkv, p_small, alp, dtp)


def deltanet_bwd(qkv, p_small, alp, dtp, sprev, tinv, d_o, nb, exchange=None):
    bl, seq, _ = qkv.shape
    nc = seq // CH
    ng = nb * NH
    extra = [] if exchange is None else [exchange]

    def body(*refs):
        qkv_ref, small_ref, alp_ref, dtp_ref, sprev_ref, tinv_ref, do_ref = refs[:7]
        dqkv_ref, dsmall_ref, dalp_ref, ddtp_ref = refs[7 + len(extra):11 + len(extra)]
        ds_scr = refs[11 + 2 * len(extra)]
        bb, r = pl.program_id(0), pl.program_id(1)
        if extra:
            start, finish = _exchange_phases(refs[7], refs[12], *refs[14:17])
            pl.when((bb == 0) & (r == 0))(start)

        @pl.when((bb == 0) & (r == 0))
        def _():
            dalp_ref[...] = jnp.zeros_like(dalp_ref)
            ddtp_ref[...] = jnp.zeros_like(ddtp_ref)

        @pl.when(r == 0)
        def _():
            ds_scr[...] = jnp.zeros_like(ds_scr)

        gates, gate_vjps = [], []
        for b in range(nb):
            out, gvjp = jax.vjp(gate_fn, small_ref[b], alp_ref[...], dtp_ref[...])
            gates.append(out)
            gate_vjps.append(gvjp)
        t_saved = tinv_ref[...]
        _, vjp = jax.vjp(lambda *args: dn_chunk(*args, t_saved)[:2], *_qkv_stacks(qkv_ref, nb), *_gate_stacks(gates, nb),
                         sprev_ref[...])
        d_out = jnp.stack([do_ref[b, :, hd * DH:(hd + 1) * DH] for b in range(nb) for hd in range(NH)])
        grads = vjp((d_out, ds_scr[...]))
        ds_scr[...] = grads[6]
        lane = _iota2((CH, LANES), 1)
        rowi = _iota2((LANES, CH), 0)
        for b in range(nb):
            d_beta = jnp.zeros((CH, LANES), F32)
            d_gc = jnp.zeros((CH, LANES), F32)
            d_gct = jnp.zeros((LANES, CH), F32)
            for hd in range(NH):
                i = b * NH + hd
                for part in range(3):
                    dqkv_ref[b, :, _dn_cols(part, hd)] = grads[part][i]
                d_beta = d_beta + jnp.where(lane == hd, grads[3][i], 0.0)
                d_gc = d_gc + jnp.where(lane == NH + hd, grads[4][i], 0.0)
                d_gct = d_gct + jnp.where(rowi == NH + hd, grads[5][i], 0.0)
            d_small, d_alp, d_dtp = gate_vjps[b]((d_beta, d_gc, d_gct))
            dsmall_ref[b] = d_small.astype(BF16)
            dalp_ref[...] += d_alp
            ddtp_ref[...] += d_dtp
        if extra:
            pl.when((bb == bl // nb - 1) & (r == nc - 1))(finish)

    blk = lambda bb, r: (bb, nc - 1 - r, 0)
    const = lambda bb, r: (0, 0)
    saved = pl.BlockSpec((None, ng, DH, DH), lambda bb, r: (bb * nc + nc - 1 - r, 0, 0, 0))
    return pl.pallas_call(
        body, grid=(bl // nb, nc),
        in_specs=[pl.BlockSpec((nb, CH, 3 * DNW), blk), pl.BlockSpec((nb, CH, LANES), blk), pl.BlockSpec((1, LANES), const),
                  pl.BlockSpec((1, LANES), const), saved, saved, pl.BlockSpec((nb, CH, DNW), blk)] + [HBM_SPEC] * len(extra),
        out_specs=[pl.BlockSpec((nb, CH, 3 * DNW), blk), pl.BlockSpec((nb, CH, LANES), blk), pl.BlockSpec((1, LANES), const),
                   pl.BlockSpec((1, LANES), const)] + [HBM_SPEC] * len(extra),
        out_shape=[SDS((bl, seq, 3 * DNW), F32), SDS((bl, seq, LANES), BF16), SDS((1, LANES), F32), SDS((1, LANES), F32)]
        + [SDS(x.shape, x.dtype) for x in extra],
        scratch_shapes=[pltpu.VMEM((ng, DH, DH), F32)] + (_comm_scratch() if extra else []), name="deltanet_bwd",
        compiler_params=_cp(2))(qkv, p_small, alp, dtp, sprev, tinv, d_o, *extra)


def _s5_table_specs():
    tab3 = pl.BlockSpec((None, LANES, 512), lambda gb, n: (gb, 0, 0))
    tab2 = pl.BlockSpec((CH, 512), lambda gb, n: (0, gb))
    return [tab3] * 4 + [tab2] * 6 + [pl.BlockSpec((1, LANES), lambda gb, n: (0, gb))]


def s5_fwd(u, tables, dsk):
    bl, seq, _ = u.shape
    nc = seq // CH

    def body(u_ref, *rest):
        tabs, (y_ref, xs_ref, xr_scr, xi_scr) = rest[:11], rest[11:]

        @pl.when(pl.program_id(1) == 0)
        def _():
            xr_scr[...] = jnp.zeros_like(xr_scr)
            xi_scr[...] = jnp.zeros_like(xi_scr)

        xp_re, xp_im = xr_scr[...], xi_scr[...]
        xs_ref[0:bl] = xp_re
        xs_ref[bl:2 * bl] = xp_im
        y, xn_re, xn_im = s5_chunk(u_ref[...], xp_re, xp_im, *[t[...] for t in tabs])
        y_ref[...] = y
        xr_scr[...] = xn_re
        xi_scr[...] = xn_im

    blk = lambda gb, n: (0, n, gb)
    return pl.pallas_call(
        body, grid=(GB, nc), in_specs=[pl.BlockSpec((bl, CH, LANES), blk)] + _s5_table_specs(),
        out_specs=[pl.BlockSpec((bl, CH, LANES), blk),
                   pl.BlockSpec((None, 2 * bl, 1, 512), lambda gb, n: (gb * nc + n, 0, 0, 0))],
        out_shape=[SDS((bl, seq, S5W), F32), SDS((GB * nc, 2 * bl, 1, 512), F32)],
        scratch_shapes=[pltpu.VMEM((bl, 1, 512), F32), pltpu.VMEM((bl, 1, 512), F32)], name="s5_fwd",
        compiler_params=_cp(2))(u, *tables, dsk)


def s5_bwd(u, tables, dsk, xs, dy):
    bl, seq, _ = u.shape
    nc = seq // CH

    def body(u_ref, *rest):
        tabs, xs_ref, dy_ref = rest[:11], rest[11], rest[12]
        du_ref, dtabs, dxr_scr, dxi_scr = rest[13], rest[14:25], rest[25], rest[26]
        r = pl.program_id(1)

        @pl.when(r == 0)
        def _():
            for t in dtabs:
                t[...] = jnp.zeros_like(t)
            dxr_scr[...] = jnp.zeros_like(dxr_scr)
            dxi_scr[...] = jnp.zeros_like(dxi_scr)

        _, vjp = jax.vjp(s5_chunk, u_ref[...], xs_ref[0:bl], xs_ref[bl:2 * bl], *[t[...] for t in tabs])
        grads = vjp((dy_ref[...], dxr_scr[...], dxi_scr[...]))
        du_ref[...] = grads[0].astype(BF16)
        dxr_scr[...] = grads[1]
        dxi_scr[...] = grads[2]
        for t, g in zip(dtabs, grads[3:]):
            t[...] += g

    blk = lambda gb, r: (0, nc - 1 - r, gb)
    tab_shapes = [SDS(t.shape, F32) for t in tables] + [SDS(dsk.shape, F32)]
    return pl.pallas_call(
        body, grid=(GB, nc),
        in_specs=[pl.BlockSpec((bl, CH, LANES), blk)] + _s5_table_specs()
        + [pl.BlockSpec((None, 2 * bl, 1, 512), lambda gb, r: (gb * nc + nc - 1 - r, 0, 0, 0)), pl.BlockSpec((bl, CH, LANES), blk)],
        out_specs=[pl.BlockSpec((bl, CH, LANES), blk)] + _s5_table_specs(),
        out_shape=[SDS((bl, seq, S5W), BF16)] + tab_shapes,
        scratch_shapes=[pltpu.VMEM((bl, 1, 512), F32), pltpu.VMEM((bl, 1, 512), F32)], name="s5_bwd",
        compiler_params=_cp(2))(u, *tables, dsk, xs, dy)


def s5_tables_fwd(params):
    shapes = [SDS((GB, LANES, 512), F32)] * 4 + [SDS((CH, S5N), F32)] * 6

    def body(*refs):
        for r, t in zip(refs[7:], s5_tables(*[p[...] for p in refs[:7]])):
            r[...] = t

    return pl.pallas_call(body, out_shape=shapes, name="s5_tables_fwd", compiler_params=_cp())(*params)


def s5_tables_bwd(params, dtables):
    def body(*refs):
        _, vjp = jax.vjp(s5_tables, *[p[...] for p in refs[:7]])
        for r, g in zip(refs[17:], vjp(tuple(t[...] for t in refs[7:17]))):
            r[...] = g

    return pl.pallas_call(body, out_shape=[SDS(p.shape, F32) for p in params], name="s5_tables_bwd",
                          compiler_params=_cp())(*params, *dtables)


def ada_fwd(c_all, w_loc, b_loc):
    def body(c_ref, w_ref, b_ref, o_ref):
        o_ref[...] = _dot(_silu(c_ref[...]), w_ref[...]) + b_ref[...]

    return pl.pallas_call(body, out_shape=SDS((c_all.shape[0], w_loc.shape[1]), F32), name="ada_fwd",
                          compiler_params=_cp())(c_all, w_loc, b_loc)


def ada_bwd(c_all, dmod_mine, dmod_all):
    def body(c_ref, dm_ref, da_ref, gw_ref, gb_ref):
        gw_ref[...] = _dot_tn(_silu(c_ref[...]), dm_ref[...])
        gb_ref[...] = jnp.sum(da_ref[...], axis=0, keepdims=True)

    return pl.pallas_call(body, out_shape=[SDS((D, dmod_mine.shape[1]), F32), SDS((1, dmod_all.shape[1]), F32)],
                          name="ada_bwd", compiler_params=_cp())(c_all, dmod_mine, dmod_all)


def loss_head(h, tgt, g, seq):
    t_rows = h.shape[0]
    tm = _pick(seq, (256, 128, 64))

    def body(h_ref, t_ref, g_ref, dh_ref, dg_ref, loss_ref):
        i = pl.program_id(0)
        y, vjp = jax.vjp(lambda hh, gg: hh * lax.rsqrt(jnp.mean(hh * hh, axis=-1, keepdims=True) + EPS) * gg,
                         h_ref[...], g_ref[...])
        e = y - t_ref[...]
        dh, dg = vjp(e * (1.0 / D))
        part = jnp.sum(jnp.sum(e * e, axis=1, keepdims=True), axis=0, keepdims=True) * (0.5 / D) + jnp.zeros((1, LANES), F32)
        dh_ref[...] = dh

        @pl.when(i == 0)
        def _():
            dg_ref[...] = dg
            loss_ref[...] = part

        @pl.when(i != 0)
        def _():
            dg_ref[...] += dg
            loss_ref[...] += part

    row = lambda i: (i, 0)
    const = lambda i: (0, 0)
    return pl.pallas_call(
        body, grid=(t_rows // tm,),
        in_specs=[pl.BlockSpec((tm, D), row), pl.BlockSpec((tm, D), row), pl.BlockSpec((1, D), const)],
        out_specs=[pl.BlockSpec((tm, D), row), pl.BlockSpec((1, D), const), pl.BlockSpec((1, LANES), const)],
        out_shape=[SDS((t_rows, D), F32), SDS((1, D), F32), SDS((1, LANES), F32)], name="loss_head",
        compiler_params=_cp(1))(h, tgt, g)


def adamw(name, parts, w, m, v):
    k_parts, rows, cols = parts.shape
    tr = _pick(rows, (256, 128, 64, 32, 16, 8))

    def body(p_ref, w_ref, m_ref, v_ref, g_ref, d_ref, mo_ref, vo_ref):
        g = p_ref[0].astype(F32)
        for k in range(1, k_parts):
            g = g + p_ref[k].astype(F32)
        _adam_store(g, w_ref, m_ref, v_ref, g_ref, d_ref, mo_ref, vo_ref)

    blk = pl.BlockSpec((tr, cols), lambda i: (i, 0))
    return pl.pallas_call(
        body, grid=(rows // tr,), in_specs=[pl.BlockSpec((k_parts, tr, cols), lambda i: (0, i, 0)), blk, blk, blk],
        out_specs=[blk] * 4, out_shape=[SDS((rows, cols), F32)] * 4, name=name, compiler_params=_cp(1))(parts, w, m, v)


def _adam_store(g, w_ref, m_ref, v_ref, g_ref, d_ref, mo_ref, vo_ref):
    m_new = ADAM_B1 * m_ref[...] + (1.0 - ADAM_B1) * g
    v_new = ADAM_B2 * v_ref[...] + (1.0 - ADAM_B2) * (g * g)
    m_hat = m_new / (1.0 - ADAM_B1 ** ADAM_STEP)
    v_hat = v_new / (1.0 - ADAM_B2 ** ADAM_STEP)
    g_ref[...] = g
    d_ref[...] = -ADAM_LR * (m_hat / (jnp.sqrt(v_hat) + ADAM_EPS) + ADAM_WD * w_ref[...])
    mo_ref[...] = m_new
    vo_ref[...] = v_new


def adamw_t(name, parts, w, m, v):
    k_parts, r, c = parts.shape
    tc = _pick(c, (256, 128))

    def body(p_ref, w_ref, m_ref, v_ref, g_ref, d_ref, mo_ref, vo_ref):
        gt = p_ref[0].astype(F32)
        for k in range(1, k_parts):
            gt = gt + p_ref[k].astype(F32)
        _adam_store(gt.T, w_ref, m_ref, v_ref, g_ref, d_ref, mo_ref, vo_ref)

    blk = pl.BlockSpec((tc, r), lambda j: (j, 0))
    return pl.pallas_call(
        body, grid=(c // tc,), in_specs=[pl.BlockSpec((k_parts, r, tc), lambda j: (0, 0, j)), blk, blk, blk],
        out_specs=[blk] * 4, out_shape=[SDS((c, r), F32)] * 4, name=name, compiler_params=_cp(1))(parts, w, m, v)


def _comm_scratch():
    return [pltpu.SemaphoreType.DMA((7,)), pltpu.SemaphoreType.DMA((7,)), pltpu.SemaphoreType.DMA]


HBM_SPEC = pl.BlockSpec(memory_space=pl.ANY)


def _gather_phases(x_ref, out_ref, send_sems, recv_sems, local_sem):
    mx, my, mc = lax.axis_index("x"), lax.axis_index("y"), lax.axis_index("c")
    me, sibling = (mx, my, mc), (mx, my, 1 - mc)
    chips = [(1 - mx, my), (mx, 1 - my), (1 - mx, 1 - my)]

    def slot(px, py, pc):
        return out_ref.at[4 * px + 2 * py + pc]

    def copy(k, block, to, src=None):
        return pltpu.make_async_remote_copy(
            src_ref=slot(*block) if src is None else src, dst_ref=slot(*block), send_sem=send_sems.at[k],
            recv_sem=recv_sems.at[k], device_id=to, device_id_type=pl.DeviceIdType.MESH)

    def first():
        return [copy(0, me, sibling, src=x_ref)] + [copy(1 + j, me, (*chip, mc), src=x_ref) for j, chip in enumerate(chips)]

    def passed():
        return [copy(4 + j, (*chip, mc), sibling) for j, chip in enumerate(chips)]

    def start():
        pltpu.make_async_copy(x_ref, slot(*me), local_sem).start()
        for cp in first():
            cp.start()

    def forward():
        for j, chip in enumerate(chips):
            copy(1 + j, (*chip, mc), me).wait_recv()
            passed()[j].start()

    def finish():
        copy(0, sibling, me).wait_recv()
        for j, chip in enumerate(chips):
            copy(4 + j, (*chip, 1 - mc), me).wait_recv()
        for cp in first() + passed():
            cp.wait_send()
        pltpu.make_async_copy(x_ref, slot(*me), local_sem).wait()

    return start, forward, finish


def _exchange_phases(x_ref, out_ref, send_sems, recv_sems, local_sem):
    mx, my, mc = lax.axis_index("x"), lax.axis_index("y"), lax.axis_index("c")
    me = 4 * mx + 2 * my + mc

    def peer(k):
        return mx ^ (k >> 2), my ^ ((k >> 1) & 1), mc ^ (k & 1)

    def sends():
        out = []
        for k in range(1, NDEV):
            px, py, pc = peer(k)
            out.append(pltpu.make_async_remote_copy(
                src_ref=x_ref.at[4 * px + 2 * py + pc], dst_ref=out_ref.at[me], send_sem=send_sems.at[k - 1],
                recv_sem=recv_sems.at[k - 1], device_id=(px, py, pc), device_id_type=pl.DeviceIdType.MESH))
        return out

    def start():
        pltpu.make_async_copy(x_ref.at[me], out_ref.at[me], local_sem).start()
        for cp in sends():
            cp.start()

    def finish():
        for k in range(1, NDEV):
            px, py, pc = peer(k)
            pltpu.make_async_remote_copy(
                src_ref=x_ref.at[me], dst_ref=out_ref.at[4 * px + 2 * py + pc], send_sem=send_sems.at[k - 1],
                recv_sem=recv_sems.at[k - 1], device_id=(px, py, pc), device_id_type=pl.DeviceIdType.MESH).wait_recv()
        for cp in sends():
            cp.wait_send()
        pltpu.make_async_copy(x_ref.at[me], out_ref.at[me], local_sem).wait()

    return start, finish


def all_gather(name, x):
    def body(x_ref, out_ref, send_sems, recv_sems, local_sem):
        for phase in _gather_phases(x_ref, out_ref, send_sems, recv_sems, local_sem):
            phase()

    return pl.pallas_call(body, out_shape=SDS((NDEV,) + x.shape, x.dtype), in_specs=[HBM_SPEC], out_specs=HBM_SPEC,
                          scratch_shapes=_comm_scratch(), name=name)(x)


def all_to_all(name, x):
    def body(x_ref, out_ref, send_sems, recv_sems, local_sem):
        for phase in _exchange_phases(x_ref, out_ref, send_sems, recv_sems, local_sem):
            phase()

    return pl.pallas_call(body, out_shape=SDS(x.shape, x.dtype), in_specs=[HBM_SPEC], out_specs=HBM_SPEC,
                          scratch_shapes=_comm_scratch(), name=name)(x)


def _pack(arrs, dtype, row_mult=8):
    segs = []
    for a in arrs:
        flat = a.reshape(-1).astype(dtype)
        segs.append(jnp.pad(flat, (0, (-flat.shape[0]) % ROW)))
    flat = jnp.concatenate(segs)
    flat = jnp.pad(flat, (0, (-flat.shape[0]) % (ROW * row_mult)))
    return flat.reshape(-1, ROW)


def _unpack(buf, shapes):
    flat = buf.reshape(-1)
    out, off = [], 0
    for s in shapes:
        n = math.prod(s)
        out.append(flat[off:off + n].reshape(s))
        off += n + (-n) % ROW
    return out


def _pack_rows(arrs, axis):
    padded = []
    for t in arrs:
        pad = [(0, 0)] * t.ndim
        pad[axis] = (0, _tile_rows(t.shape[axis]) - t.shape[axis])
        padded.append(jnp.pad(t, pad))
    return jnp.concatenate(padded, axis=axis)


def _tile_rows(r):
    return r + (-r) % 16


def _unpack8(buf, shapes):
    flat = buf.reshape(NDEV, -1)
    out, off = [], 0
    for s in shapes:
        n = math.prod(s)
        out.append(flat[:, off:off + n].reshape((NDEV,) + tuple(s)))
        off += n + (-n) % ROW
    return out


def kernel(x, c, w_ada, b_ada, g_ffn1, w1_ffn1, w3_ffn1, w2_ffn1, g_mix, w_in, conv_qkv, a_log, dt_bias, g_onorm, lam_re, lam_im, log_step, b_re, b_im, c_re, c_im, d_skip, w_glu, b_glu, w_proj_a, w_proj_b, w_out, g_ffn2, w1_ffn2, w3_ffn2, w2_ffn2, g_final, loss_target, m_w_ada, m_b_ada, m_g_ffn1, m_w1_ffn1, m_w3_ffn1, m_w2_ffn1, m_g_mix, m_w_in, m_conv_qkv, m_a_log, m_dt_bias, m_g_onorm, m_lam_re, m_lam_im, m_log_step, m_b_re, m_b_im, m_c_re, m_c_im, m_d_skip, m_w_glu, m_b_glu, m_w_proj_a, m_w_proj_b, m_w_out, m_g_ffn2, m_w1_ffn2, m_w3_ffn2, m_w2_ffn2, m_g_final, v_w_ada, v_b_ada, v_g_ffn1, v_w1_ffn1, v_w3_ffn1, v_w2_ffn1, v_g_mix, v_w_in, v_conv_qkv, v_a_log, v_dt_bias, v_g_onorm, v_lam_re, v_lam_im, v_log_step, v_b_re, v_b_im, v_c_re, v_c_im, v_d_skip, v_w_glu, v_b_glu, v_w_proj_a, v_w_proj_b, v_w_out, v_g_ffn2, v_w1_ffn2, v_w3_ffn2, v_w2_ffn2, v_g_final):
    a = dict(locals())
    bl, seq, _ = x.shape
    t_rows = bl * seq
    nc = seq // CH
    me = 4 * lax.axis_index("x") + 2 * lax.axis_index("y") + lax.axis_index("c")
    tm_ew = _pick(seq, (256, 128, 64))

    sm = all_gather("gather_small", _pack([c, conv_qkv[0]], F32))
    c_loc, conv_loc = _unpack8(sm, [c.shape, conv_qkv.shape[1:]])
    c_all = c_loc.reshape(NDEV * bl, D)
    conv_full = conv_loc.transpose(1, 0, 2).reshape(CONVW, 3 * DNW)
    loc = {n: (a[n][0].T if n in COL_SHARDED else a[n][0]) for n in RS_WEIGHTS}
    wfull, gw, res = {}, {}, {}

    def pack_local(names):
        return _pack_rows([loc[n].astype(BF16).reshape(-1, ROW) for n in names], 0)

    def unpack_full(buf, names):
        r0 = 0
        for n in names:
            r = loc[n].size // ROW
            wfull[n] = buf[:, r0:r0 + r, :].reshape(-1, loc[n].shape[1])
            r0 += _tile_rows(r)

    def pack_grads(names):
        return _pack_rows([gw[n].astype(BF16).reshape(NDEV, -1, ROW) for n in names], 1)

    def update(buf, names):
        r0 = 0
        for n in names:
            r = loc[n].size // ROW
            parts = buf[:, r0:r0 + r, :].reshape((NDEV,) + loc[n].shape)
            r0 += _tile_rows(r)
            step = adamw_t if n in COL_SHARDED else adamw
            out = step("adamw_" + n, parts, a[n][0], a["m_" + n][0], a["v_" + n][0])
            for kind, t in zip(("grad", "delta", "new_m", "new_v"), out):
                res[kind + "_" + n] = t[None]

    unpack_full(all_gather("gather_ffn1", pack_local(G_FFN1)), G_FFN1)

    n_ada = w_ada.shape[2]
    mod_part = ada_fwd(c_all, w_ada[0], lax.dynamic_slice(b_ada, (0, me * n_ada), (1, n_ada)))
    mod_all = all_gather("gather_mod", mod_part).transpose(1, 0, 2).reshape(NDEV * bl, 9 * D)
    mod = lax.dynamic_slice(mod_all, (me * bl, 0), (bl, 9 * D)).reshape(bl, 9, D)
    mods = [mod[:, k:k + 1, :] for k in range(9)]

    h0 = x.reshape(t_rows, D)
    h1, f1, u1, pa1, pb1, wg_rest = ffn_fwd("ffn1_fwd", h0, mod[:, 0:3, :], g_ffn1, wfull['w1_ffn1'], wfull['w3_ffn1'],
                                  wfull['w2_ffn1'], seq, gather=pack_local(G_MIX + G_FFN2))
    unpack_full(wg_rest, G_MIX + G_FFN2)
    win = wfull['w_in']
    o_small, o_s5, o_gate = 4 * DNW, 4 * DNW + 2 * NH, 4 * DNW + 2 * NH + S5W
    w_dn, w_small = win[:o_small], jnp.pad(win[o_small:o_s5], ((0, LANES - 2 * NH), (0, 0)))
    w_s5, w_gate = win[o_s5:o_gate], win[o_gate:]
    (u2,) = ew_call("mix_norm", fn_normmod, [h1], [mods[3], mods[4]], [g_mix], [(D, BF16)], tm_ew, seq)
    p_dn = mm("proj_dn", [(u2, w_dn)], True, F32)
    p_small = mm("proj_small", [(u2, w_small)], True, F32)
    p_s5 = mm("proj_s5", [(u2, w_s5)], True, F32)
    p_gate = mm("proj_gate", [(u2, w_gate)], True, F32)

    conv8 = jnp.pad(conv_full, ((0, 8 - CONVW), (0, 0)))
    alp = jnp.pad(a_log, ((0, 0), (NH, LANES - 2 * NH)))
    dtp = jnp.pad(dt_bias, ((0, 0), (NH, LANES - 2 * NH)))
    nb_dn = DN_ROWS if bl % DN_ROWS == 0 else 1
    p_dn3, p_small3 = p_dn.reshape(bl, seq, 4 * DNW), p_small.reshape(bl, seq, LANES)
    qkv3 = dn_prep_fwd(p_dn3, conv8)
    o_pre3, sprev, tinv = deltanet_fwd(qkv3, p_small3, alp, dtp, nb_dn)
    o_pre = o_pre3.reshape(t_rows, DNW)
    z_raw = p_dn[:, 3 * DNW:]
    (oa,) = ew_call("dn_onorm", fn_onorm, [o_pre, z_raw], [], [g_onorm], [(DNW, BF16)], tm_ew, seq)
    ya = mm("proj_a", [(oa, wfull['w_proj_a'])], True, F32)

    s5_params = [lam_re.reshape(1, S5N), lam_im.reshape(1, S5N), log_step,
                 b_re[0].transpose(2, 0, 1).reshape(S5C, S5N), b_im[0].transpose(2, 0, 1).reshape(S5C, S5N),
                 c_re[0].transpose(1, 0, 2).reshape(S5C, S5N), c_im[0].transpose(1, 0, 2).reshape(S5C, S5N)]
    tables = s5_tables_fwd(s5_params)
    p_s53 = p_s5.reshape(bl, seq, S5W)
    y_s53, xs = s5_fwd(p_s53, tables, d_skip)
    y_s5 = y_s53.reshape(t_rows, S5W)
    (ob,) = ew_call("s5_glu", fn_glu, [y_s5], [], [wfull['w_glu'], b_glu], [(S5W, BF16)], tm_ew, seq)
    yb = mm("proj_b", [(ob, wfull['w_proj_b'])], True, F32)

    (merged,) = ew_call("merge", fn_merge, [p_gate, ya, yb], [], [], [(D, BF16)], tm_ew, seq)
    mo = mm("proj_out", [(merged, wfull['w_out'])], False, F32)
    (h2,) = ew_call("mix_resid", lambda p, q, gt: (q + gt * p,), [mo, h1], [mods[5]], [], [(D, F32)], tm_ew, seq)
    h3, f3, u3, pa3, pb3 = ffn_fwd("ffn2_fwd", h2, mod[:, 6:9, :], g_ffn2, wfull['w1_ffn2'], wfull['w3_ffn2'], wfull['w2_ffn2'], seq)

    dh3, dg_final, loss_part = loss_head(h3, loss_target.reshape(t_rows, D), g_final.reshape(1, D), seq)
    loss = lax.psum(loss_part[0, 0], ("x", "y", "c"))

    dh2, a3, d1_3, d3_3, df3, dmod_c, dg_ffn2 = ffn_bwd("ffn2_bwd", dh3, h2, f3, pa3, pb3, mod[:, 6:9, :], g_ffn2, wfull['w1_ffn2'],
                                                   wfull['w3_ffn2'], wfull['w2_ffn2'], seq)
    gw['w1_ffn2'] = mm_tn("gw1_ffn2", d1_3, u3)
    gw['w3_ffn2'] = mm_tn("gw3_ffn2", d3_3, u3)
    gw['w2_ffn2'] = mm_tn("gw2_ffn2", a3, df3)

    (dmo,), (dgt2,), _ = ew_vjp_call("mix_resid_bwd", fn_resid, [mo], [mods[5]], [], [dh2], [(0, BF16)], tm_ew, seq)
    gw['w_out'] = mm_tn("gw_out", merged, dmo)
    d_merged = mm("d_merged", [(dmo, wfull['w_out'])], True, F32)
    (d_gate, d_ya, d_yb), _, _ = ew_vjp_call("merge_bwd", fn_merge, [p_gate, ya, yb], [], [], [d_merged],
                                             [(0, BF16), (1, BF16), (2, BF16)], tm_ew, seq)
    gw['w_proj_a'] = mm_tn("gw_proj_a", d_ya, oa)
    gw['w_proj_b'] = mm_tn("gw_proj_b", d_yb, ob)
    d_oa = mm("d_oa", [(d_ya, wfull['w_proj_a'])], False, F32)
    d_ob = mm("d_ob", [(d_yb, wfull['w_proj_b'])], False, F32)

    (d_opre, d_z), _, (dg_onorm,) = ew_vjp_call("dn_onorm_bwd", fn_onorm, [o_pre, z_raw], [], [g_onorm], [d_oa],
                                                [(0, F32), (1, F32)], tm_ew, seq)
    d_qkv3, d_psmall3, d_alp, d_dtp, rs_ffn2 = deltanet_bwd(
        qkv3, p_small3, alp, dtp, sprev, tinv, d_opre.reshape(bl, seq, DNW), nb_dn, exchange=pack_grads(G_FFN2))
    d_pdn3, d_conv8 = dn_prep_bwd(p_dn3, conv8, d_qkv3, d_z.reshape(bl, seq, DNW))
    d_pdn, d_psmall = d_pdn3.reshape(t_rows, 4 * DNW), d_psmall3.reshape(t_rows, LANES)

    (d_ys5,), _, (g_wglu, dg_bglu) = ew_vjp_call("s5_glu_bwd", fn_glu, [y_s5], [], [wfull['w_glu'], b_glu], [d_ob],
                                                 [(0, F32)], tm_ew, seq)
    gw['w_glu'] = g_wglu
    s5_out = s5_bwd(p_s53, tables, d_skip, xs, d_ys5.reshape(bl, seq, S5W))
    d_ps5, d_tables, dg_dskip = s5_out[0].reshape(t_rows, S5W), s5_out[1:11], s5_out[11]
    d_s5p = s5_tables_bwd(s5_params, d_tables)

    d_pdn_b, d_psm_b, d_ps5_b = d_pdn, d_psmall, d_ps5
    gw['w_in'] = jnp.concatenate([mm_tn("gw_dn", d_pdn_b, u2), mm_tn("gw_small", d_psm_b, u2)[:2 * NH],
                                  mm_tn("gw_s5", d_ps5_b, u2), mm_tn("gw_gate", d_gate, u2)], axis=0)
    du2 = mm("d_u2", [(d_pdn_b, w_dn), (d_psm_b, w_small), (d_ps5_b, w_s5), (d_gate, w_gate)], False, F32)
    (dh1,), (dsh2, dsc2), (dg_mix,) = ew_vjp_call("mix_norm_bwd", fn_normmod, [h1], [mods[3], mods[4]], [g_mix], [du2],
                                                  [(0, F32)], tm_ew, seq, addend=dh2)

    dh0, a1, d1_1, d3_1, df1, dmod_a, dg_ffn1, rs_mix = ffn_bwd(
        "ffn1_bwd", dh1, h0, f1, pa1, pb1, mod[:, 0:3, :], g_ffn1, wfull['w1_ffn1'], wfull['w3_ffn1'], wfull['w2_ffn1'], seq,
        exchange=pack_grads(G_MIX))
    gw['w1_ffn1'] = mm_tn("gw1_ffn1", d1_1, u1)
    gw['w3_ffn1'], rs_w1 = mm_tn("gw3_ffn1", d3_1, u1, exchange=pack_grads(['w1_ffn1']))
    gw['w2_ffn1'], rs_w3 = mm_tn("gw2_ffn1", a1, df1, exchange=pack_grads(['w3_ffn1']))

    update(rs_ffn2, G_FFN2)
    update(rs_mix, G_MIX)
    update(rs_w1, ['w1_ffn1'])
    update(rs_w3, ['w3_ffn1'])
    update(all_to_all("scatter_w2_ffn1", pack_grads(['w2_ffn1'])), ['w2_ffn1'])

    dmod_mine = jnp.concatenate([dmod_a, dsh2, dsc2, dgt2, dmod_c], axis=1).reshape(bl, 9 * D)
    small_grads = {
        'g_ffn1': dg_ffn1, 'g_mix': dg_mix, 'a_log': d_alp[:, NH:2 * NH], 'dt_bias': d_dtp[:, NH:2 * NH],
        'g_onorm': dg_onorm, 'lam_re': d_s5p[0].reshape(1, S5G, S5P), 'lam_im': d_s5p[1].reshape(1, S5G, S5P),
        'log_step': d_s5p[2],
        'b_re': d_s5p[3].reshape(S5C, S5G, S5P).transpose(1, 2, 0)[None],
        'b_im': d_s5p[4].reshape(S5C, S5G, S5P).transpose(1, 2, 0)[None],
        'c_re': d_s5p[5].reshape(S5C, S5G, S5P).transpose(1, 0, 2)[None],
        'c_im': d_s5p[6].reshape(S5C, S5G, S5P).transpose(1, 0, 2)[None],
        'd_skip': dg_dskip, 'b_glu': dg_bglu, 'g_ffn2': dg_ffn2, 'g_final': dg_final.reshape(D)}
    small_shapes = [a[n].shape for n in SMALL]
    small_pack = _pack([small_grads[n] for n in SMALL], F32)
    n_small = small_pack.shape[0]
    sg = all_gather("gather_small_grads",
                    jnp.concatenate([small_pack, _pack([dmod_mine, d_conv8[:CONVW]], F32)], axis=0))
    pieces = _unpack8(sg[:, n_small:, :], [dmod_mine.shape, (CONVW, 3 * DNW)])
    dmod_all = pieces[0].reshape(NDEV * bl, 9 * D)
    g_wada, g_bada = ada_bwd(c_all, lax.dynamic_slice(dmod_all, (0, me * n_ada), (NDEV * bl, n_ada)), dmod_all)

    n_conv = conv_qkv.shape[2]
    conv_parts = lax.dynamic_slice(pieces[1], (0, 0, me * n_conv), (NDEV, CONVW, n_conv))
    conv_parts = jnp.pad(conv_parts.reshape(NDEV, 1, -1), ((0, 0), (0, 7), (0, 0)))
    pad8 = lambda t: jnp.pad(t.reshape(1, -1), ((0, 7), (0, 0)))
    conv_res = adamw("adamw_conv", conv_parts, pad8(conv_qkv), pad8(m_conv_qkv), pad8(v_conv_qkv))
    for kind, buf in zip(("grad", "delta", "new_m", "new_v"), conv_res):
        res[kind + "_conv_qkv"] = buf[0].reshape(conv_qkv.shape)

    small_res = adamw("adamw_small", sg[:, :n_small, :], *[_pack([a[p + n] for n in SMALL], F32) for p in ("", "m_", "v_")])
    for kind, buf in zip(("grad", "delta", "new_m", "new_v"), small_res):
        for n, t in zip(SMALL, _unpack(buf, small_shapes)):
            res[kind + "_" + n] = t

    for n, g in (("w_ada", g_wada), ("b_ada", g_bada)):
        shp = a[n].shape
        r2 = lambda t: t.reshape(-1, shp[-1]) if n == "w_ada" else pad8(t)
        out = adamw("adamw_" + n, r2(g)[None], r2(a[n]), r2(a["m_" + n]), r2(a["v_" + n]))
        for kind, buf in zip(("grad", "delta", "new_m", "new_v"), out):
            res[kind + "_" + n] = (buf if n == "w_ada" else buf[0:1]).reshape(shp)

    outs = [loss, dh0.reshape(x.shape)]
    for kind in ("grad", "delta", "new_m", "new_v"):
        outs += [res[kind + "_" + n] for n in WEIGHTS]
    return tuple(outs)
```

```python
import functools
import math

import jax
import jax.numpy as jnp
from jax import lax
from jax.experimental import pallas as pl
from jax.experimental.pallas import tpu as pltpu

F32 = jnp.float32
BF16 = jnp.bfloat16
HI = lax.Precision.HIGHEST
H3 = lax.Precision.HIGH
SDS = jax.ShapeDtypeStruct

D = 1024
FF = 2816
FFN_TF = FF // 2
FFN_BWD_TM = 256
NH = 8
DH = 64
DNW = NH * DH
CONVW = 4
CH = 64
DN_ROWS = 2
S5W = 512
S5G = 32
S5P = 64
S5C = 16
S5N = S5G * S5P
GB = 4
NDEV = 8
EPS = 1e-6
LANES = 128
ROW = 1024
VMEM_LIMIT = 56 * 1024 * 1024

ADAM_LR, ADAM_B1, ADAM_B2, ADAM_EPS, ADAM_WD, ADAM_STEP = 0.001, 0.9, 0.999, 1e-08, 0.01, 10

WEIGHTS = ['w_ada', 'b_ada', 'g_ffn1', 'w1_ffn1', 'w3_ffn1', 'w2_ffn1', 'g_mix', 'w_in', 'conv_qkv', 'a_log',
           'dt_bias', 'g_onorm', 'lam_re', 'lam_im', 'log_step', 'b_re', 'b_im', 'c_re', 'c_im', 'd_skip', 'w_glu',
           'b_glu', 'w_proj_a', 'w_proj_b', 'w_out', 'g_ffn2', 'w1_ffn2', 'w3_ffn2', 'w2_ffn2', 'g_final']
RS_WEIGHTS = ['w1_ffn1', 'w3_ffn1', 'w2_ffn1', 'w_in', 'w_glu', 'w_proj_a', 'w_proj_b', 'w_out', 'w1_ffn2', 'w3_ffn2',
              'w2_ffn2']
COL_SHARDED = {'w1_ffn1', 'w3_ffn1', 'w_in', 'w_proj_a', 'w_proj_b', 'w1_ffn2', 'w3_ffn2'}
G_FFN1 = ['w1_ffn1', 'w3_ffn1', 'w2_ffn1']
G_MIX = ['w_in', 'w_glu', 'w_proj_a', 'w_proj_b', 'w_out']
G_FFN2 = ['w1_ffn2', 'w3_ffn2', 'w2_ffn2']
SMALL = ['g_ffn1', 'g_mix', 'a_log', 'dt_bias', 'g_onorm', 'lam_re', 'lam_im', 'log_step', 'b_re', 'b_im', 'c_re',
         'c_im', 'd_skip', 'b_glu', 'g_ffn2', 'g_final']


def _cp(n_grid=0):
    if n_grid:
        return pltpu.CompilerParams(vmem_limit_bytes=VMEM_LIMIT, dimension_semantics=("arbitrary",) * n_grid)
    return pltpu.CompilerParams(vmem_limit_bytes=VMEM_LIMIT)


def _dot(a, b):
    return jnp.dot(a.astype(BF16), b.astype(BF16), preferred_element_type=F32)


def _dot_nt(a, b):
    return lax.dot_general(a.astype(BF16), b.astype(BF16), (((1,), (1,)), ((), ())), preferred_element_type=F32)


def _dot_tn(a, b):
    return lax.dot_general(a.astype(BF16), b.astype(BF16), (((0,), (0,)), ((), ())), preferred_element_type=F32)


def _dot_hi(a, b):
    return jnp.dot(a, b, precision=HI, preferred_element_type=F32)


def _dot_h3(a, b):
    return jnp.dot(a, b, precision=H3, preferred_element_type=F32)


@jax.custom_vjp
def bdot(a, b):
    return _dot(a, b)


bdot.defvjp(lambda a, b: (_dot(a, b), (a, b)),
            lambda r, g: (_dot_nt(g, r[1]).astype(r[0].dtype), _dot_tn(r[0], g).astype(r[1].dtype)))


@jax.custom_vjp
def bdot_nt(a, b):
    return _dot_nt(a, b)


bdot_nt.defvjp(lambda a, b: (_dot_nt(a, b), (a, b)),
               lambda r, g: (_dot(g, r[1]).astype(r[0].dtype), _dot_tn(g, r[0]).astype(r[1].dtype)))


@jax.custom_vjp
def bdot_tn(a, b):
    return _dot_tn(a, b)


bdot_tn.defvjp(lambda a, b: (_dot_tn(a, b), (a, b)),
               lambda r, g: (_dot_nt(r[1], g).astype(r[0].dtype), _dot(r[0], g).astype(r[1].dtype)))


def _silu(x):
    return x * jax.nn.sigmoid(x)


def _iota2(shape, axis):
    return lax.broadcasted_iota(jnp.int32, shape, axis)


def normmod(h, g, sc, sh):
    y = h * lax.rsqrt(jnp.mean(h * h, axis=-1, keepdims=True) + EPS) * g
    return y * (1.0 + sc) + sh


def fn_normmod(h, sh, sc, g):
    return (normmod(h, g, sc, sh),)


def fn_resid(mo, gt):
    return (gt * mo,)


def fn_merge(gate, ya, yb):
    return (jax.nn.sigmoid(gate[:, :D]) * ya + jax.nn.sigmoid(gate[:, D:]) * yb,)


def fn_glu(y, w, b):
    ge = jax.nn.gelu(y)
    return (ge * jax.nn.sigmoid(bdot(ge, w) + b),)


def fn_onorm(o, z, g_on):
    r = _iota2((DH, DNW), 0)
    c = _iota2((DH, DNW), 1)
    expand = (c % DH == r).astype(F32)
    r2 = _iota2((DNW, DNW), 0)
    c2 = _iota2((DNW, DNW), 1)
    avg = (r2 // DH == c2 // DH).astype(F32) * (1.0 / DH)
    ms = _dot_h3(o * o, avg)
    return (o * lax.rsqrt(ms + EPS) * _dot_hi(g_on, expand) * _silu(z),)


def fn_mix_tail(o_pre, z, y_s5, gate, g_on, w_glu, b_glu, wa_t, wb_t):
    (oa,) = fn_onorm(o_pre, z, g_on)
    (ob,) = fn_glu(y_s5, w_glu, b_glu)
    return fn_merge(gate, bdot_nt(oa, wa_t), bdot_nt(ob, wb_t))


def gate_fn(small, alp, dtp):
    beta = jax.nn.sigmoid(small)
    la = -jnp.exp(alp) * jax.nn.softplus(small + dtp)
    tri = (_iota2((CH, CH), 0) >= _iota2((CH, CH), 1)).astype(F32)
    gc = _dot_hi(tri, la)
    gct = lax.dot_general(la, tri, (((0,), (1,)), ((), ())), precision=HI, preferred_element_type=F32)
    return beta, gc, gct


def _bdg(a, b, ca, cb, hi):
    if not hi:
        a, b = a.astype(BF16), b.astype(BF16)
    return lax.dot_general(a, b, (((ca,), (cb,)), ((0,), (0,))), precision=H3 if hi else None,
                           preferred_element_type=F32)


def _batched_matmuls(hi):
    nn_ = lambda a, b: _bdg(a, b, 2, 1, hi)
    nt_ = lambda a, b: _bdg(a, b, 2, 2, hi)
    tn_ = lambda a, b: _bdg(a, b, 1, 1, hi)
    nn = jax.custom_vjp(nn_)
    nn.defvjp(lambda a, b: (nn_(a, b), (a, b)), lambda r, g: (nt_(g, r[1]), tn_(r[0], g)))
    nt = jax.custom_vjp(nt_)
    nt.defvjp(lambda a, b: (nt_(a, b), (a, b)), lambda r, g: (nn_(g, r[1]), tn_(g, r[0])))
    tn = jax.custom_vjp(tn_)
    tn.defvjp(lambda a, b: (tn_(a, b), (a, b)), lambda r, g: (nt_(r[1], g), nn_(r[0], g)))
    return nn, nt, tn


bnn, bnt, btn = _batched_matmuls(False)
hnn, hnt, htn = _batched_matmuls(True)


def _unit_lower_inverse(a):
    r = _iota2((1, CH, CH), 1)
    c = _iota2((1, CH, CH), 2)
    eye = (r == c).astype(F32)
    d = jnp.where(r // 8 == c // 8, a, 0.0)
    inv = eye - d
    p = d
    for _ in range(2):
        p = hnn(p, p)
        inv = inv + hnn(inv, p)
    for blk in (16, 32, 64):
        off = jnp.where((r // blk == c // blk) & (r // (blk // 2) != c // (blk // 2)), a, 0.0)
        inv = inv - hnn(hnn(inv, off), inv)
    return inv


@jax.custom_vjp
def _inverse_given(a, t):
    return t


_inverse_given.defvjp(lambda a, t: (t, t), lambda t, g: (-hnt(htn(t, g), t), jnp.zeros_like(t)))


def dn_prep(xc, w):
    t = xc.shape[0] - 8
    c = xc[5:5 + t] * w[0:1] + xc[6:6 + t] * w[1:2] + xc[7:7 + t] * w[2:3] + xc[8:8 + t] * w[3:4]
    act = _silu(c)
    q, k, v = act[:, :DNW], act[:, DNW:2 * DNW], act[:, 2 * DNW:]
    ones = (_iota2((DNW, DNW), 0) // DH == _iota2((DNW, DNW), 1) // DH).astype(F32)
    q = q * lax.rsqrt(_dot_h3(q * q, ones) + EPS) * (DH ** -0.5)
    k = k * lax.rsqrt(_dot_h3(k * k, ones) + EPS)
    return jnp.concatenate([q, k, v], axis=1)


def dn_chunk(q, k, v, b, g, gt, s_prev, t_saved=None):
    r = _iota2((1, CH, CH), 1)
    c = _iota2((1, CH, CH), 2)
    causal = r >= c
    dec = jnp.where(causal, jnp.exp(jnp.where(causal, g - gt, 0.0)), 0.0)
    kb = k * b
    qk = bnt(jnp.concatenate([q, kb], axis=1), k)
    attn = qk[:, :CH] * dec
    a = jnp.where(r > c, qk[:, CH:] * dec, 0.0)
    tinv = _unit_lower_inverse(a) if t_saved is None else _inverse_given(a, t_saved)
    eg = jnp.exp(g)
    uw = hnn(tinv, jnp.concatenate([v * b, kb * eg], axis=2))
    g_last = g[:, CH - 1:CH]
    ws = bnn(jnp.concatenate([uw[..., DH:], q * eg], axis=1), s_prev)
    v_new = uw[..., :DH] - ws[:, :CH]
    o = ws[:, CH:] + bnn(attn, v_new)
    s_new = s_prev * jnp.exp(g_last) + btn(k * jnp.exp(g_last - g), v_new)
    return o, s_new, tinv


def s5_chunk(u, xp_re, xp_im, bb_re, bb_im, cc_re, cc_im, p0r, p0i, p1r, p1i, pir, pii, dsk):
    nb = u.shape[0]
    u2 = u.reshape(nb * CH, LANES)
    bu_re = bdot(u2, bb_re).reshape(nb, CH, 512)
    bu_im = bdot(u2, bb_im).reshape(nb, CH, 512)
    xt_re = pir * bu_re - pii * bu_im
    xt_im = pir * bu_im + pii * bu_re
    tri = jnp.broadcast_to((_iota2((1, CH, CH), 1) >= _iota2((1, CH, CH), 2)).astype(F32), (nb, CH, CH))
    cs_re = hnn(tri, xt_re)
    cs_im = hnn(tri, xt_im)
    x_re = p0r * cs_re - p0i * cs_im + p1r * xp_re - p1i * xp_im
    x_im = p0r * cs_im + p0i * cs_re + p1r * xp_im + p1i * xp_re
    y = bdot_nt(x_re.reshape(nb * CH, 512), cc_re) - bdot_nt(x_im.reshape(nb * CH, 512), cc_im) + dsk * u2
    return y.reshape(nb, CH, LANES), x_re[:, CH - 1:CH], x_im[:, CH - 1:CH]


def s5_tables(lam_re, lam_im, log_step, bre, bim, cre, cim):
    expand = (_iota2((S5G, S5N), 1) // S5P == _iota2((S5G, S5N), 0)).astype(F32)
    step = _dot_hi(jnp.exp(log_step), expand)
    lre = jnp.minimum(lam_re, -1e-4)
    lr = lre * step
    ang = lam_im * step
    mag = jnp.exp(lr)
    lb_re = mag * jnp.cos(ang)
    lb_im = mag * jnp.sin(ang)
    den = lre * lre + lam_im * lam_im
    coef_re = ((lb_re - 1.0) * lre + lb_im * lam_im) / den
    coef_im = (lb_im * lre - (lb_re - 1.0) * lam_im) / den
    bb_re = coef_re * bre - coef_im * bim
    bb_im = coef_re * bim + coef_im * bre
    j = _iota2((CH, 1), 0).astype(F32)
    e0 = jnp.exp(j * lr)
    e1 = jnp.exp((j + 1.0) * lr)
    ei = jnp.exp(-j * lr)
    mask = (_iota2((LANES, 512), 0) // S5C == _iota2((LANES, 512), 1) // S5P).astype(F32)

    def blocks(t):
        return jnp.concatenate([(jnp.tile(t[:, gb * 512:(gb + 1) * 512], (LANES // S5C, 1)) * mask)[None]
                                for gb in range(GB)], axis=0)

    return (blocks(bb_re), blocks(bb_im), blocks(cre), blocks(cim),
            e0 * jnp.cos(j * ang), e0 * jnp.sin(j * ang),
            e1 * jnp.cos((j + 1.0) * ang), e1 * jnp.sin((j + 1.0) * ang),
            ei * jnp.cos(j * ang), -ei * jnp.sin(j * ang))


def _row_specs(tiled, batch, bcast, tm, tpb):
    specs = [pl.BlockSpec((tm, a.shape[1]), lambda i: (i, 0)) for a in tiled]
    specs += [pl.BlockSpec((None,) + a.shape[1:], lambda i: (i // tpb, 0, 0)) for a in batch]
    specs += [pl.BlockSpec(a.shape, lambda i, nd=a.ndim: (0,) * nd) for a in bcast]
    return specs


def ew_call(name, fn, tiled, batch, bcast, outs, tm, seq):
    t_rows = tiled[0].shape[0]
    n_in = len(tiled) + len(batch) + len(bcast)

    def body(*refs):
        vals = [r[...].astype(F32) for r in refs[:n_in]]
        for r, o in zip(refs[n_in:], fn(*vals)):
            r[...] = o.astype(r.dtype)

    return pl.pallas_call(
        body, grid=(t_rows // tm,), in_specs=_row_specs(tiled, batch, bcast, tm, seq // tm),
        out_specs=[pl.BlockSpec((tm, w), lambda i: (i, 0)) for w, _ in outs],
        out_shape=[SDS((t_rows, w), dt) for w, dt in outs], name=name, compiler_params=_cp(1))(*tiled, *batch, *bcast)


def ew_vjp_call(name, fn, tiled, batch, bcast, cts, want, tm, seq, addend=None):
    t_rows = tiled[0].shape[0]
    tpb = seq // tm
    n_t, n_b, n_c = len(tiled), len(batch), len(bcast)
    n_in = n_t + n_b + n_c
    extra = [] if addend is None else [addend]

    def body(*refs):
        i = pl.program_id(0)
        vals = [r[...].astype(F32) for r in refs[:n_in]]
        ctv = tuple(r[...].astype(F32) for r in refs[n_in:n_in + len(cts)])
        outs = refs[n_in + len(cts) + len(extra):]
        _, vjp = jax.vjp(fn, *vals)
        grads = vjp(ctv)
        for k, (r, (idx, _)) in enumerate(zip(outs[:len(want)], want)):
            g = grads[idx]
            if k == 0 and extra:
                g = g + refs[n_in + len(cts)][...]
            r[...] = g.astype(r.dtype)
        for k in range(n_b):
            r, g = outs[len(want) + k], grads[n_t + k]

            @pl.when(i % tpb == 0)
            def _(r=r, g=g):
                r[...] = g

            @pl.when(i % tpb != 0)
            def _(r=r, g=g):
                r[...] += g
        for k in range(n_c):
            r, g = outs[len(want) + n_b + k], grads[n_t + n_b + k]

            @pl.when(i == 0)
            def _(r=r, g=g):
                r[...] = g

            @pl.when(i != 0)
            def _(r=r, g=g):
                r[...] += g

    out_specs = [pl.BlockSpec((tm, tiled[idx].shape[1]), lambda i: (i, 0)) for idx, _ in want]
    out_specs += [pl.BlockSpec((None,) + a.shape[1:], lambda i: (i // tpb, 0, 0)) for a in batch]
    out_specs += [pl.BlockSpec(a.shape, lambda i, nd=a.ndim: (0,) * nd) for a in bcast]
    out_shape = [SDS(tiled[idx].shape, dt) for idx, dt in want]
    out_shape += [SDS(a.shape, F32) for a in batch] + [SDS(a.shape, F32) for a in bcast]
    res = pl.pallas_call(
        body, grid=(t_rows // tm,),
        in_specs=_row_specs(tiled, batch, bcast, tm, tpb)
        + [pl.BlockSpec((tm, a.shape[1]), lambda i: (i, 0)) for a in list(cts) + extra],
        out_specs=out_specs, out_shape=out_shape, name=name, compiler_params=_cp(1))(*tiled, *batch, *bcast, *cts, *extra)
    return res[:len(want)], res[len(want):len(want) + n_b], res[len(want) + n_b:]


def _pick(n, cands):
    for c in cands:
        if n % c == 0:
            return c
    return n


def mm(name, pairs, nt, out_dtype):
    m = pairs[0][0].shape[0]
    n = pairs[0][1].shape[0 if nt else 1]
    k_total = sum(a.shape[1] for a, _ in pairs)
    tm = _pick(m, (1024, 512, 256, 128) if k_total <= 2048 else (512, 256, 128))
    tn = _pick(n, (512, 256, 128))
    np_ = len(pairs)

    def body(*refs):
        acc = None
        for p in range(np_):
            a, b = refs[2 * p][...], refs[2 * p + 1][...]
            t = _dot_nt(a, b) if nt else _dot(a, b)
            acc = t if acc is None else acc + t
        refs[2 * np_][...] = acc.astype(out_dtype)

    in_specs, ops = [], []
    for a, b in pairs:
        k = a.shape[1]
        in_specs.append(pl.BlockSpec((tm, k), lambda i, j: (i, 0)))
        in_specs.append(pl.BlockSpec((tn, k), lambda i, j: (j, 0)) if nt else pl.BlockSpec((k, tn), lambda i, j: (0, j)))
        ops += [a, b]
    return pl.pallas_call(
        body, grid=(m // tm, n // tn), in_specs=in_specs, out_specs=pl.BlockSpec((tm, tn), lambda i, j: (i, j)),
        out_shape=SDS((m, n), out_dtype), name=name, compiler_params=_cp(2))(*ops)


def mm_tn(name, a, b, exchange=None):
    t_rows, m = a.shape
    n = b.shape[1]
    tn = n if n <= 1024 else _pick(n, (1024, 512, 256, 128))
    tm = max([t for t in range(LANES, m + 1, LANES) if m % t == 0 and t * tn * 4 <= 6 * 1024 * 1024] or [m])
    tk = _pick(t_rows, (512, 256, 128, 64))
    grid = (m // tm, n // tn, t_rows // tk)
    extra = [] if exchange is None else [exchange]

    def body(*refs):
        a_ref, b_ref = refs[:2]
        o_ref, acc = refs[2 + len(extra)], refs[3 + 2 * len(extra)]
        i, j, k = pl.program_id(0), pl.program_id(1), pl.program_id(2)
        if extra:
            start, finish = _exchange_phases(refs[2], refs[4], *refs[6:9])
            pl.when((i == 0) & (j == 0) & (k == 0))(start)

        @pl.when(k == 0)
        def _():
            acc[...] = jnp.zeros_like(acc)

        acc[...] += _dot_tn(a_ref[...], b_ref[...])

        @pl.when(k == grid[2] - 1)
        def _():
            o_ref[...] = acc[...].astype(BF16)

        if extra:
            pl.when((i == grid[0] - 1) & (j == grid[1] - 1) & (k == grid[2] - 1))(finish)

    res = pl.pallas_call(
        body, grid=grid,
        in_specs=[pl.BlockSpec((tk, tm), lambda i, j, k: (k, i)), pl.BlockSpec((tk, tn), lambda i, j, k: (k, j))]
        + [HBM_SPEC] * len(extra),
        out_specs=[pl.BlockSpec((tm, tn), lambda i, j, k: (i, j))] + [HBM_SPEC] * len(extra),
        out_shape=[SDS((m, n), BF16)] + [SDS(x.shape, x.dtype) for x in extra],
        scratch_shapes=[pltpu.VMEM((tm, tn), F32)] + (_comm_scratch() if extra else []), name=name,
        compiler_params=_cp(3))(a, b, *extra)
    return res if extra else res[0]


def ffn_fwd(name, h, mod3, g, w1, w3, w2, seq, gather=None):
    t_rows = h.shape[0]
    tm = _pick(seq, (512, 256, 128, 64))
    tf = FFN_TF
    tpb = seq // tm
    nf = FF // tf
    nt = t_rows // tm
    extra = [] if gather is None else [gather]

    def body(*refs):
        h_ref, mod_ref, g_ref, w1_ref, w3_ref, w2_ref = refs[:6]
        ho_ref, f_ref, u_ref, h1_ref, h3_ref = refs[6 + len(extra):11 + len(extra)]
        acc = refs[11 + 2 * len(extra)]
        i, j = pl.program_id(0), pl.program_id(1)
        if extra:
            start, forward, finish = _gather_phases(refs[6], refs[12], *refs[14:17])
            pl.when((i == 0) & (j == 0))(start)
            pl.when((i == nt - 1) & (j == 0))(forward)

        @pl.when(j == 0)
        def _():
            u_ref[...] = normmod(h_ref[...], g_ref[...], mod_ref[1:2, :], mod_ref[0:1, :]).astype(BF16)
            acc[...] = jnp.zeros_like(acc)

        u = u_ref[...]
        h1 = _dot_nt(u, w1_ref[...])
        h3 = _dot_nt(u, w3_ref[...])
        h1_ref[...] = h1.astype(BF16)
        h3_ref[...] = h3.astype(BF16)
        acc[...] += _dot(_silu(h1) * h3, w2_ref[...])

        @pl.when(j == nf - 1)
        def _():
            f_ref[...] = acc[...]
            ho_ref[...] = h_ref[...] + 0.5 * mod_ref[2:3, :] * acc[...]

        if extra:
            pl.when((i == nt - 1) & (j == nf - 1))(finish)

    row = lambda i, j: (i, 0)
    return pl.pallas_call(
        body, grid=(nt, nf),
        in_specs=[pl.BlockSpec((tm, D), row), pl.BlockSpec((None, 3, D), lambda i, j: (i // tpb, 0, 0)),
                  pl.BlockSpec((1, D), lambda i, j: (0, 0)), pl.BlockSpec((tf, D), lambda i, j: (j, 0)),
                  pl.BlockSpec((tf, D), lambda i, j: (j, 0)), pl.BlockSpec((tf, D), lambda i, j: (j, 0))]
        + [HBM_SPEC] * len(extra),
        out_specs=[pl.BlockSpec((tm, D), row), pl.BlockSpec((tm, D), row), pl.BlockSpec((tm, D), row),
                   pl.BlockSpec((tm, tf), lambda i, j: (i, j)), pl.BlockSpec((tm, tf), lambda i, j: (i, j))]
        + [HBM_SPEC] * len(extra),
        out_shape=[SDS((t_rows, D), F32), SDS((t_rows, D), F32), SDS((t_rows, D), BF16), SDS((t_rows, FF), BF16),
                   SDS((t_rows, FF), BF16)] + [SDS((NDEV,) + x.shape, x.dtype) for x in extra],
        scratch_shapes=[pltpu.VMEM((tm, D), F32)] + (_comm_scratch() if extra else []), name=name,
        compiler_params=_cp(2))(h, mod3, g, w1, w3, w2, *extra)


def ffn_bwd(name, dho, h, f_out, h1_in, h3_in, mod3, g, w1, w3, w2, seq, exchange=None):
    t_rows = h.shape[0]
    tm = _pick(seq, (FFN_BWD_TM, 128, 64))
    tf = FFN_TF
    tpb = seq // tm
    nf = FF // tf
    nt = t_rows // tm
    extra = [] if exchange is None else [exchange]

    def body(*refs):
        dho_ref, h_ref, f_ref, h1_ref, h3_ref, mod_ref, g_ref, w1_ref, w3_ref, w2_ref = refs[:10]
        dh_ref, a_ref, dh1_ref, dh3_ref, df_scr, dmod_ref, dg_ref = refs[10 + len(extra):17 + len(extra)]
        du_acc = refs[17 + 2 * len(extra)]
        i, j = pl.program_id(0), pl.program_id(1)
        if extra:
            start, finish = _exchange_phases(refs[10], refs[18], *refs[20:23])
            pl.when((i == 0) & (j == 0))(start)

        @pl.when(j == 0)
        def _():
            df_scr[...] = (0.5 * mod_ref[2:3, :] * dho_ref[...]).astype(BF16)
            du_acc[...] = jnp.zeros_like(du_acc)

        h1 = h1_ref[...].astype(F32)
        h3 = h3_ref[...].astype(F32)
        sg = jax.nn.sigmoid(h1)
        s = h1 * sg
        da = _dot_nt(df_scr[...], w2_ref[...])
        dh3 = (da * s).astype(BF16)
        dh1 = (da * h3 * (sg * (1.0 + h1 * (1.0 - sg)))).astype(BF16)
        a_ref[...] = (s * h3).astype(BF16)
        dh1_ref[...] = dh1
        dh3_ref[...] = dh3
        du_acc[...] += _dot(dh1, w1_ref[...]) + _dot(dh3, w3_ref[...])

        @pl.when(j == nf - 1)
        def _():
            _, vjp = jax.vjp(normmod, h_ref[...], g_ref[...], mod_ref[1:2, :], mod_ref[0:1, :])
            dh_n, dg, dsc, dsh = vjp(du_acc[...])
            dh_ref[...] = dho_ref[...] + dh_n
            dgt = jnp.sum(0.5 * dho_ref[...] * f_ref[...], axis=0, keepdims=True)
            dmod = jnp.concatenate([dsh, dsc, dgt], axis=0)

            @pl.when(i % tpb == 0)
            def _():
                dmod_ref[...] = dmod

            @pl.when(i % tpb != 0)
            def _():
                dmod_ref[...] += dmod

            @pl.when(i == 0)
            def _():
                dg_ref[...] = dg

            @pl.when(i != 0)
            def _():
                dg_ref[...] += dg

        if extra:
            pl.when((i == nt - 1) & (j == nf - 1))(finish)

    row = lambda i, j: (i, 0)
    col = lambda i, j: (i, j)
    return pl.pallas_call(
        body, grid=(nt, nf),
        in_specs=[pl.BlockSpec((tm, D), row), pl.BlockSpec((tm, D), row), pl.BlockSpec((tm, D), row),
                  pl.BlockSpec((tm, tf), col), pl.BlockSpec((tm, tf), col),
                  pl.BlockSpec((None, 3, D), lambda i, j: (i // tpb, 0, 0)),
                  pl.BlockSpec((1, D), lambda i, j: (0, 0)), pl.BlockSpec((tf, D), lambda i, j: (j, 0)),
                  pl.BlockSpec((tf, D), lambda i, j: (j, 0)), pl.BlockSpec((tf, D), lambda i, j: (j, 0))]
        + [HBM_SPEC] * len(extra),
        out_specs=[pl.BlockSpec((tm, D), row), pl.BlockSpec((tm, tf), col), pl.BlockSpec((tm, tf), col),
                   pl.BlockSpec((tm, tf), col), pl.BlockSpec((tm, D), row),
                   pl.BlockSpec((None, 3, D), lambda i, j: (i // tpb, 0, 0)), pl.BlockSpec((1, D), lambda i, j: (0, 0))]
        + [HBM_SPEC] * len(extra),
        out_shape=[SDS((t_rows, D), F32), SDS((t_rows, FF), BF16), SDS((t_rows, FF), BF16), SDS((t_rows, FF), BF16),
                   SDS((t_rows, D), BF16), SDS(mod3.shape, F32), SDS((1, D), F32)] + [SDS(x.shape, x.dtype) for x in extra],
        scratch_shapes=[pltpu.VMEM((tm, D), F32)] + (_comm_scratch() if extra else []), name=name,
        compiler_params=_cp(2))(dho, h, f_out, h1_in, h3_in, mod3, g, w1, w3, w2, *extra)


def _dn_cols(part, hd):
    return slice(part * DNW + hd * DH, part * DNW + (hd + 1) * DH)


def _qkv_stacks(qkv_ref, nb):
    pairs = [(b, hd) for b in range(nb) for hd in range(NH)]
    return [jnp.stack([qkv_ref[b, :, _dn_cols(part, hd)] for b, hd in pairs]) for part in range(3)]


def dn_prep_fwd(p_dn, conv8):
    bl, seq, _ = p_dn.shape
    tp = _pick(seq, (256, 128, 64))

    def body(raw_ref, halo_ref, conv_ref, o_ref):
        hm = (pl.program_id(1) > 0).astype(F32)
        o_ref[...] = dn_prep(jnp.concatenate([halo_ref[...] * hm, raw_ref[...]], axis=0), conv_ref[...])

    return pl.pallas_call(
        body, grid=(bl, seq // tp),
        in_specs=[pl.BlockSpec((None, tp, 3 * DNW), lambda b, i: (b, i, 0)),
                  pl.BlockSpec((None, 8, 3 * DNW), lambda b, i: (b, jnp.maximum(i * (tp // 8) - 1, 0), 0)),
                  pl.BlockSpec((8, 3 * DNW), lambda b, i: (0, 0))],
        out_specs=pl.BlockSpec((None, tp, 3 * DNW), lambda b, i: (b, i, 0)),
        out_shape=SDS((bl, seq, 3 * DNW), F32), name="dn_prep_fwd", compiler_params=_cp(2))(p_dn, p_dn, conv8)


def dn_prep_bwd(p_dn, conv8, d_qkv, d_z):
    bl, seq, _ = p_dn.shape
    tp = _pick(seq, (256, 128, 64))
    nt = seq // tp

    def body(raw_ref, halo_ref, conv_ref, dq_ref, dz_ref, draw_ref, dconv_ref, carry):
        b, r = pl.program_id(0), pl.program_id(1)

        @pl.when((b == 0) & (r == 0))
        def _():
            dconv_ref[...] = jnp.zeros_like(dconv_ref)

        @pl.when(r == 0)
        def _():
            carry[...] = jnp.zeros_like(carry)

        hm = (r < nt - 1).astype(F32)
        _, vjp = jax.vjp(dn_prep, jnp.concatenate([halo_ref[...] * hm, raw_ref[...]], axis=0), conv_ref[...])
        dxc, dw = vjp(dq_ref[...])
        tail = dxc[tp:tp + 8] + carry[...]
        draw_ref[:, 0:3 * DNW] = jnp.concatenate([dxc[8:tp], tail], axis=0).astype(BF16)
        draw_ref[:, 3 * DNW:4 * DNW] = dz_ref[...].astype(BF16)
        carry[...] = dxc[0:8] * hm
        dconv_ref[...] += dw

    blk = lambda b, r: (b, nt - 1 - r, 0)
    return pl.pallas_call(
        body, grid=(bl, nt),
        in_specs=[pl.BlockSpec((None, tp, 3 * DNW), blk),
                  pl.BlockSpec((None, 8, 3 * DNW), lambda b, r: (b, jnp.maximum((nt - 1 - r) * (tp // 8) - 1, 0), 0)),
                  pl.BlockSpec((8, 3 * DNW), lambda b, r: (0, 0)), pl.BlockSpec((None, tp, 3 * DNW), blk),
                  pl.BlockSpec((None, tp, DNW), blk)],
        out_specs=[pl.BlockSpec((None, tp, 4 * DNW), blk), pl.BlockSpec((8, 3 * DNW), lambda b, r: (0, 0))],
        out_shape=[SDS((bl, seq, 4 * DNW), BF16), SDS((8, 3 * DNW), F32)],
        scratch_shapes=[pltpu.VMEM((8, 3 * DNW), F32)], name="dn_prep_bwd", compiler_params=_cp(2))(p_dn, p_dn, conv8, d_qkv, d_z)


def _gate_stacks(gates, nb):
    pairs = [(b, hd) for b in range(nb) for hd in range(NH)]
    bs = jnp.stack([gates[b][0][:, hd:hd + 1] for b, hd in pairs])
    gs = jnp.stack([gates[b][1][:, NH + hd:NH + hd + 1] for b, hd in pairs])
    gts = jnp.stack([gates[b][2][NH + hd:NH + hd + 1, :] for b, hd in pairs])
    return bs, gs, gts


def deltanet_fwd(qkv, p_small, alp, dtp, nb):
    bl, seq, _ = qkv.shape
    nc = seq // CH
    ng = nb * NH

    def body(qkv_ref, small_ref, alp_ref, dtp_ref, o_ref, sprev_ref, tinv_ref, s_scr):
        @pl.when(pl.program_id(1) == 0)
        def _():
            s_scr[...] = jnp.zeros_like(s_scr)

        gates = [gate_fn(small_ref[b], alp_ref[...], dtp_ref[...]) for b in range(nb)]
        s_prev = s_scr[...]
        o, s_new, tinv = dn_chunk(*_qkv_stacks(qkv_ref, nb), *_gate_stacks(gates, nb), s_prev)
        sprev_ref[...] = s_prev
        tinv_ref[...] = tinv
        s_scr[...] = s_new
        for b in range(nb):
            for hd in range(NH):
                o_ref[b, :, hd * DH:(hd + 1) * DH] = o[b * NH + hd]

    blk = lambda bb, n: (bb, n, 0)
    const = lambda bb, n: (0, 0)
    saved = pl.BlockSpec((None, ng, DH, DH), lambda bb, n: (bb * nc + n, 0, 0, 0))
    return pl.pallas_call(
        body, grid=(bl // nb, nc),
        in_specs=[pl.BlockSpec((nb, CH, 3 * DNW), blk), pl.BlockSpec((nb, CH, LANES), blk),
                  pl.BlockSpec((1, LANES), const), pl.BlockSpec((1, LANES), const)],
        out_specs=[pl.BlockSpec((nb, CH, DNW), blk), saved, saved],
        out_shape=[SDS((bl, seq, DNW), F32), SDS((bl // nb * nc, ng, DH, DH), F32), SDS((bl // nb * nc, ng, DH, DH), F32)],
        scratch_shapes=[pltpu.VMEM((ng, DH, DH), F32)], name="deltanet_fwd",
        compiler_params=_cp(2))(qkv, p_small, alp, dtp)


def deltanet_bwd(qkv, p_small, alp, dtp, sprev, tinv, d_o, nb, exchange=None):
    bl, seq, _ = qkv.shape
    nc = seq // CH
    ng = nb * NH
    extra = [] if exchange is None else [exchange]

    def body(*refs):
        qkv_ref, small_ref, alp_ref, dtp_ref, sprev_ref, tinv_ref, do_ref = refs[:7]
        dqkv_ref, dsmall_ref, dalp_ref, ddtp_ref = refs[7 + len(extra):11 + len(extra)]
        ds_scr = refs[11 + 2 * len(extra)]
        bb, r = pl.program_id(0), pl.program_id(1)
        if extra:
            start, finish = _exchange_phases(refs[7], refs[12], *refs[14:17])
            pl.when((bb == 0) & (r == 0))(start)

        @pl.when((bb == 0) & (r == 0))
        def _():
            dalp_ref[...] = jnp.zeros_like(dalp_ref)
            ddtp_ref[...] = jnp.zeros_like(ddtp_ref)

        @pl.when(r == 0)
        def _():
            ds_scr[...] = jnp.zeros_like(ds_scr)

        gates, gate_vjps = [], []
        for b in range(nb):
            out, gvjp = jax.vjp(gate_fn, small_ref[b], alp_ref[...], dtp_ref[...])
            gates.append(out)
            gate_vjps.append(gvjp)
        t_saved = tinv_ref[...]
        _, vjp = jax.vjp(lambda *args: dn_chunk(*args, t_saved)[:2], *_qkv_stacks(qkv_ref, nb), *_gate_stacks(gates, nb),
                         sprev_ref[...])
        d_out = jnp.stack([do_ref[b, :, hd * DH:(hd + 1) * DH] for b in range(nb) for hd in range(NH)])
        grads = vjp((d_out, ds_scr[...]))
        ds_scr[...] = grads[6]
        lane = _iota2((CH, LANES), 1)
        rowi = _iota2((LANES, CH), 0)
        for b in range(nb):
            d_beta = jnp.zeros((CH, LANES), F32)
            d_gc = jnp.zeros((CH, LANES), F32)
            d_gct = jnp.zeros((LANES, CH), F32)
            for hd in range(NH):
                i = b * NH + hd
                for part in range(3):
                    dqkv_ref[b, :, _dn_cols(part, hd)] = grads[part][i]
                d_beta = d_beta + jnp.where(lane == hd, grads[3][i], 0.0)
                d_gc = d_gc + jnp.where(lane == NH + hd, grads[4][i], 0.0)
                d_gct = d_gct + jnp.where(rowi == NH + hd, grads[5][i], 0.0)
            d_small, d_alp, d_dtp = gate_vjps[b]((d_beta, d_gc, d_gct))
            dsmall_ref[b] = d_small.astype(BF16)
            dalp_ref[...] += d_alp
            ddtp_ref[...] += d_dtp
        if extra:
            pl.when((bb == bl // nb - 1) & (r == nc - 1))(finish)

    blk = lambda bb, r: (bb, nc - 1 - r, 0)
    const = lambda bb, r: (0, 0)
    saved = pl.BlockSpec((None, ng, DH, DH), lambda bb, r: (bb * nc + nc - 1 - r, 0, 0, 0))
    return pl.pallas_call(
        body, grid=(bl // nb, nc),
        in_specs=[pl.BlockSpec((nb, CH, 3 * DNW), blk), pl.BlockSpec((nb, CH, LANES), blk), pl.BlockSpec((1, LANES), const),
                  pl.BlockSpec((1, LANES), const), saved, saved, pl.BlockSpec((nb, CH, DNW), blk)] + [HBM_SPEC] * len(extra),
        out_specs=[pl.BlockSpec((nb, CH, 3 * DNW), blk), pl.BlockSpec((nb, CH, LANES), blk), pl.BlockSpec((1, LANES), const),
                   pl.BlockSpec((1, LANES), const)] + [HBM_SPEC] * len(extra),
        out_shape=[SDS((bl, seq, 3 * DNW), F32), SDS((bl, seq, LANES), BF16), SDS((1, LANES), F32), SDS((1, LANES), F32)]
        + [SDS(x.shape, x.dtype) for x in extra],
        scratch_shapes=[pltpu.VMEM((ng, DH, DH), F32)] + (_comm_scratch() if extra else []), name="deltanet_bwd",
        compiler_params=_cp(2))(qkv, p_small, alp, dtp, sprev, tinv, d_o, *extra)


def _s5_table_specs():
    tab3 = pl.BlockSpec((None, LANES, 512), lambda gb, n: (gb, 0, 0))
    tab2 = pl.BlockSpec((CH, 512), lambda gb, n: (0, gb))
    return [tab3] * 4 + [tab2] * 6 + [pl.BlockSpec((1, LANES), lambda gb, n: (0, gb))]


def s5_fwd(u, tables, dsk):
    bl, seq, _ = u.shape
    nc = seq // CH

    def body(u_ref, *rest):
        tabs, (y_ref, xs_ref, xr_scr, xi_scr) = rest[:11], rest[11:]

        @pl.when(pl.program_id(1) == 0)
        def _():
            xr_scr[...] = jnp.zeros_like(xr_scr)
            xi_scr[...] = jnp.zeros_like(xi_scr)

        xp_re, xp_im = xr_scr[...], xi_scr[...]
        xs_ref[0:bl] = xp_re
        xs_ref[bl:2 * bl] = xp_im
        y, xn_re, xn_im = s5_chunk(u_ref[...], xp_re, xp_im, *[t[...] for t in tabs])
        y_ref[...] = y
        xr_scr[...] = xn_re
        xi_scr[...] = xn_im

    blk = lambda gb, n: (0, n, gb)
    return pl.pallas_call(
        body, grid=(GB, nc), in_specs=[pl.BlockSpec((bl, CH, LANES), blk)] + _s5_table_specs(),
        out_specs=[pl.BlockSpec((bl, CH, LANES), blk),
                   pl.BlockSpec((None, 2 * bl, 1, 512), lambda gb, n: (gb * nc + n, 0, 0, 0))],
        out_shape=[SDS((bl, seq, S5W), F32), SDS((GB * nc, 2 * bl, 1, 512), F32)],
        scratch_shapes=[pltpu.VMEM((bl, 1, 512), F32), pltpu.VMEM((bl, 1, 512), F32)], name="s5_fwd",
        compiler_params=_cp(2))(u, *tables, dsk)


def s5_bwd(u, tables, dsk, xs, dy):
    bl, seq, _ = u.shape
    nc = seq // CH

    def body(u_ref, *rest):
        tabs, xs_ref, dy_ref = rest[:11], rest[11], rest[12]
        du_ref, dtabs, dxr_scr, dxi_scr = rest[13], rest[14:25], rest[25], rest[26]
        r = pl.program_id(1)

        @pl.when(r == 0)
        def _():
            for t in dtabs:
                t[...] = jnp.zeros_like(t)
            dxr_scr[...] = jnp.zeros_like(dxr_scr)
            dxi_scr[...] = jnp.zeros_like(dxi_scr)

        _, vjp = jax.vjp(s5_chunk, u_ref[...], xs_ref[0:bl], xs_ref[bl:2 * bl], *[t[...] for t in tabs])
        grads = vjp((dy_ref[...], dxr_scr[...], dxi_scr[...]))
        du_ref[...] = grads[0].astype(BF16)
        dxr_scr[...] = grads[1]
        dxi_scr[...] = grads[2]
        for t, g in zip(dtabs, grads[3:]):
            t[...] += g

    blk = lambda gb, r: (0, nc - 1 - r, gb)
    tab_shapes = [SDS(t.shape, F32) for t in tables] + [SDS(dsk.shape, F32)]
    return pl.pallas_call(
        body, grid=(GB, nc),
        in_specs=[pl.BlockSpec((bl, CH, LANES), blk)] + _s5_table_specs()
        + [pl.BlockSpec((None, 2 * bl, 1, 512), lambda gb, r: (gb * nc + nc - 1 - r, 0, 0, 0)), pl.BlockSpec((bl, CH, LANES), blk)],
        out_specs=[pl.BlockSpec((bl, CH, LANES), blk)] + _s5_table_specs(),
        out_shape=[SDS((bl, seq, S5W), BF16)] + tab_shapes,
        scratch_shapes=[pltpu.VMEM((bl, 1, 512), F32), pltpu.VMEM((bl, 1, 512), F32)], name="s5_bwd",
        compiler_params=_cp(2))(u, *tables, dsk, xs, dy)


def s5_tables_fwd(params):
    shapes = [SDS((GB, LANES, 512), F32)] * 4 + [SDS((CH, S5N), F32)] * 6

    def body(*refs):
        for r, t in zip(refs[7:], s5_tables(*[p[...] for p in refs[:7]])):
            r[...] = t

    return pl.pallas_call(body, out_shape=shapes, name="s5_tables_fwd", compiler_params=_cp())(*params)


def s5_tables_bwd(params, dtables):
    def body(*refs):
        _, vjp = jax.vjp(s5_tables, *[p[...] for p in refs[:7]])
        for r, g in zip(refs[17:], vjp(tuple(t[...] for t in refs[7:17]))):
            r[...] = g

    return pl.pallas_call(body, out_shape=[SDS(p.shape, F32) for p in params], name="s5_tables_bwd",
                          compiler_params=_cp())(*params, *dtables)


def ada_fwd(c_all, w_loc, b_loc):
    def body(c_ref, w_ref, b_ref, o_ref):
        o_ref[...] = _dot(_silu(c_ref[...]), w_ref[...]) + b_ref[...]

    return pl.pallas_call(body, out_shape=SDS((c_all.shape[0], w_loc.shape[1]), F32), name="ada_fwd",
                          compiler_params=_cp())(c_all, w_loc, b_loc)


def ada_bwd(c_all, dmod_mine, dmod_all):
    def body(c_ref, dm_ref, da_ref, gw_ref, gb_ref):
        gw_ref[...] = _dot_tn(_silu(c_ref[...]), dm_ref[...])
        gb_ref[...] = jnp.sum(da_ref[...], axis=0, keepdims=True)

    return pl.pallas_call(body, out_shape=[SDS((D, dmod_mine.shape[1]), F32), SDS((1, dmod_all.shape[1]), F32)],
                          name="ada_bwd", compiler_params=_cp())(c_all, dmod_mine, dmod_all)


def loss_head(h, tgt, g, seq):
    t_rows = h.shape[0]
    tm = _pick(seq, (256, 128, 64))

    def body(h_ref, t_ref, g_ref, dh_ref, dg_ref, loss_ref):
        i = pl.program_id(0)
        y, vjp = jax.vjp(lambda hh, gg: hh * lax.rsqrt(jnp.mean(hh * hh, axis=-1, keepdims=True) + EPS) * gg,
                         h_ref[...], g_ref[...])
        e = y - t_ref[...]
        dh, dg = vjp(e * (1.0 / D))
        part = jnp.sum(jnp.sum(e * e, axis=1, keepdims=True), axis=0, keepdims=True) * (0.5 / D) + jnp.zeros((1, LANES), F32)
        dh_ref[...] = dh

        @pl.when(i == 0)
        def _():
            dg_ref[...] = dg
            loss_ref[...] = part

        @pl.when(i != 0)
        def _():
            dg_ref[...] += dg
            loss_ref[...] += part

    row = lambda i: (i, 0)
    const = lambda i: (0, 0)
    return pl.pallas_call(
        body, grid=(t_rows // tm,),
        in_specs=[pl.BlockSpec((tm, D), row), pl.BlockSpec((tm, D), row), pl.BlockSpec((1, D), const)],
        out_specs=[pl.BlockSpec((tm, D), row), pl.BlockSpec((1, D), const), pl.BlockSpec((1, LANES), const)],
        out_shape=[SDS((t_rows, D), F32), SDS((1, D), F32), SDS((1, LANES), F32)], name="loss_head",
        compiler_params=_cp(1))(h, tgt, g)


def adamw(name, parts, w, m, v):
    k_parts, rows, cols = parts.shape
    tr = _pick(rows, (256, 128, 64, 32, 16, 8))

    def body(p_ref, w_ref, m_ref, v_ref, g_ref, d_ref, mo_ref, vo_ref):
        g = p_ref[0].astype(F32)
        for k in range(1, k_parts):
            g = g + p_ref[k].astype(F32)
        _adam_store(g, w_ref, m_ref, v_ref, g_ref, d_ref, mo_ref, vo_ref)

    blk = pl.BlockSpec((tr, cols), lambda i: (i, 0))
    return pl.pallas_call(
        body, grid=(rows // tr,), in_specs=[pl.BlockSpec((k_parts, tr, cols), lambda i: (0, i, 0)), blk, blk, blk],
        out_specs=[blk] * 4, out_shape=[SDS((rows, cols), F32)] * 4, name=name, compiler_params=_cp(1))(parts, w, m, v)


def _adam_store(g, w_ref, m_ref, v_ref, g_ref, d_ref, mo_ref, vo_ref):
    m_new = ADAM_B1 * m_ref[...] + (1.0 - ADAM_B1) * g
    v_new = ADAM_B2 * v_ref[...] + (1.0 - ADAM_B2) * (g * g)
    m_hat = m_new / (1.0 - ADAM_B1 ** ADAM_STEP)
    v_hat = v_new / (1.0 - ADAM_B2 ** ADAM_STEP)
    g_ref[...] = g
    d_ref[...] = -ADAM_LR * (m_hat / (jnp.sqrt(v_hat) + ADAM_EPS) + ADAM_WD * w_ref[...])
    mo_ref[...] = m_new
    vo_ref[...] = v_new


def adamw_t(name, parts, w, m, v):
    k_parts, r, c = parts.shape
    tc = _pick(c, (256, 128))

    def body(p_ref, w_ref, m_ref, v_ref, g_ref, d_ref, mo_ref, vo_ref):
        gt = p_ref[0].astype(F32)
        for k in range(1, k_parts):
            gt = gt + p_ref[k].astype(F32)
        _adam_store(gt.T, w_ref, m_ref, v_ref, g_ref, d_ref, mo_ref, vo_ref)

    blk = pl.BlockSpec((tc, r), lambda j: (j, 0))
    return pl.pallas_call(
        body, grid=(c // tc,), in_specs=[pl.BlockSpec((k_parts, r, tc), lambda j: (0, 0, j)), blk, blk, blk],
        out_specs=[blk] * 4, out_shape=[SDS((c, r), F32)] * 4, name=name, compiler_params=_cp(1))(parts, w, m, v)


def _comm_scratch():
    return [pltpu.SemaphoreType.DMA((7,)), pltpu.SemaphoreType.DMA((7,)), pltpu.SemaphoreType.DMA]


HBM_SPEC = pl.BlockSpec(memory_space=pl.ANY)


def _gather_phases(x_ref, out_ref, send_sems, recv_sems, local_sem):
    mx, my, mc = lax.axis_index("x"), lax.axis_index("y"), lax.axis_index("c")
    me, sibling = (mx, my, mc), (mx, my, 1 - mc)
    chips = [(1 - mx, my), (mx, 1 - my), (1 - mx, 1 - my)]

    def slot(px, py, pc):
        return out_ref.at[4 * px + 2 * py + pc]

    def copy(k, block, to, src=None):
        return pltpu.make_async_remote_copy(
            src_ref=slot(*block) if src is None else src, dst_ref=slot(*block), send_sem=send_sems.at[k],
            recv_sem=recv_sems.at[k], device_id=to, device_id_type=pl.DeviceIdType.MESH)

    def first():
        return [copy(0, me, sibling, src=x_ref)] + [copy(1 + j, me, (*chip, mc), src=x_ref) for j, chip in enumerate(chips)]

    def passed():
        return [copy(4 + j, (*chip, mc), sibling) for j, chip in enumerate(chips)]

    def start():
        pltpu.make_async_copy(x_ref, slot(*me), local_sem).start()
        for cp in first():
            cp.start()

    def forward():
        for j, chip in enumerate(chips):
            copy(1 + j, (*chip, mc), me).wait_recv()
            passed()[j].start()

    def finish():
        copy(0, sibling, me).wait_recv()
        for j, chip in enumerate(chips):
            copy(4 + j, (*chip, 1 - mc), me).wait_recv()
        for cp in first() + passed():
            cp.wait_send()
        pltpu.make_async_copy(x_ref, slot(*me), local_sem).wait()

    return start, forward, finish


def _exchange_phases(x_ref, out_ref, send_sems, recv_sems, local_sem):
    mx, my, mc = lax.axis_index("x"), lax.axis_index("y"), lax.axis_index("c")
    me = 4 * mx + 2 * my + mc

    def peer(k):
        return mx ^ (k >> 2), my ^ ((k >> 1) & 1), mc ^ (k & 1)

    def sends():
        out = []
        for k in range(1, NDEV):
            px, py, pc = peer(k)
            out.append(pltpu.make_async_remote_copy(
                src_ref=x_ref.at[4 * px + 2 * py + pc], dst_ref=out_ref.at[me], send_sem=send_sems.at[k - 1],
                recv_sem=recv_sems.at[k - 1], device_id=(px, py, pc), device_id_type=pl.DeviceIdType.MESH))
        return out

    def start():
        pltpu.make_async_copy(x_ref.at[me], out_ref.at[me], local_sem).start()
        for cp in sends():
            cp.start()

    def finish():
        for k in range(1, NDEV):
            px, py, pc = peer(k)
            pltpu.make_async_remote_copy(
                src_ref=x_ref.at[me], dst_ref=out_ref.at[4 * px + 2 * py + pc], send_sem=send_sems.at[k - 1],
                recv_sem=recv_sems.at[k - 1], device_id=(px, py, pc), device_id_type=pl.DeviceIdType.MESH).wait_recv()
        for cp in sends():
            cp.wait_send()
        pltpu.make_async_copy(x_ref.at[me], out_ref.at[me], local_sem).wait()

    return start, finish


def all_gather(name, x):
    def body(x_ref, out_ref, send_sems, recv_sems, local_sem):
        for phase in _gather_phases(x_ref, out_ref, send_sems, recv_sems, local_sem):
            phase()

    return pl.pallas_call(body, out_shape=SDS((NDEV,) + x.shape, x.dtype), in_specs=[HBM_SPEC], out_specs=HBM_SPEC,
                          scratch_shapes=_comm_scratch(), name=name)(x)


def all_to_all(name, x):
    def body(x_ref, out_ref, send_sems, recv_sems, local_sem):
        for phase in _exchange_phases(x_ref, out_ref, send_sems, recv_sems, local_sem):
            phase()

    return pl.pallas_call(body, out_shape=SDS(x.shape, x.dtype), in_specs=[HBM_SPEC], out_specs=HBM_SPEC,
                          scratch_shapes=_comm_scratch(), name=name)(x)


def _pack(arrs, dtype, row_mult=8):
    segs = []
    for a in arrs:
        flat = a.reshape(-1).astype(dtype)
        segs.append(jnp.pad(flat, (0, (-flat.shape[0]) % ROW)))
    flat = jnp.concatenate(segs)
    flat = jnp.pad(flat, (0, (-flat.shape[0]) % (ROW * row_mult)))
    return flat.reshape(-1, ROW)


def _unpack(buf, shapes):
    flat = buf.reshape(-1)
    out, off = [], 0
    for s in shapes:
        n = math.prod(s)
        out.append(flat[off:off + n].reshape(s))
        off += n + (-n) % ROW
    return out


def _pack_rows(arrs, axis):
    padded = []
    for t in arrs:
        pad = [(0, 0)] * t.ndim
        pad[axis] = (0, _tile_rows(t.shape[axis]) - t.shape[axis])
        padded.append(jnp.pad(t, pad))
    return jnp.concatenate(padded, axis=axis)


def _tile_rows(r):
    return r + (-r) % 16


def _unpack8(buf, shapes):
    flat = buf.reshape(NDEV, -1)
    out, off = [], 0
    for s in shapes:
        n = math.prod(s)
        out.append(flat[:, off:off + n].reshape((NDEV,) + tuple(s)))
        off += n + (-n) % ROW
    return out


def kernel(x, c, w_ada, b_ada, g_ffn1, w1_ffn1, w3_ffn1, w2_ffn1, g_mix, w_in, conv_qkv, a_log, dt_bias, g_onorm, lam_re, lam_im, log_step, b_re, b_im, c_re, c_im, d_skip, w_glu, b_glu, w_proj_a, w_proj_b, w_out, g_ffn2, w1_ffn2, w3_ffn2, w2_ffn2, g_final, loss_target, m_w_ada, m_b_ada, m_g_ffn1, m_w1_ffn1, m_w3_ffn1, m_w2_ffn1, m_g_mix, m_w_in, m_conv_qkv, m_a_log, m_dt_bias, m_g_onorm, m_lam_re, m_lam_im, m_log_step, m_b_re, m_b_im, m_c_re, m_c_im, m_d_skip, m_w_glu, m_b_glu, m_w_proj_a, m_w_proj_b, m_w_out, m_g_ffn2, m_w1_ffn2, m_w3_ffn2, m_w2_ffn2, m_g_final, v_w_ada, v_b_ada, v_g_ffn1, v_w1_ffn1, v_w3_ffn1, v_w2_ffn1, v_g_mix, v_w_in, v_conv_qkv, v_a_log, v_dt_bias, v_g_onorm, v_lam_re, v_lam_im, v_log_step, v_b_re, v_b_im, v_c_re, v_c_im, v_d_skip, v_w_glu, v_b_glu, v_w_proj_a, v_w_proj_b, v_w_out, v_g_ffn2, v_w1_ffn2, v_w3_ffn2, v_w2_ffn2, v_g_final):
    a = dict(locals())
    bl, seq, _ = x.shape
    t_rows = bl * seq
    nc = seq // CH
    me = 4 * lax.axis_index("x") + 2 * lax.axis_index("y") + lax.axis_index("c")
    tm_ew = _pick(seq, (256, 128, 64))
    tm_tail = _pick(seq, (256, 128, 64))

    sm = all_gather("gather_small", _pack([c, conv_qkv[0]], F32))
    c_loc, conv_loc = _unpack8(sm, [c.shape, conv_qkv.shape[1:]])
    c_all = c_loc.reshape(NDEV * bl, D)
    conv_full = conv_loc.transpose(1, 0, 2).reshape(CONVW, 3 * DNW)
    loc = {n: (a[n][0].T if n in COL_SHARDED else a[n][0]) for n in RS_WEIGHTS}
    wfull, gw, res = {}, {}, {}

    def pack_local(names):
        return _pack_rows([loc[n].astype(BF16).reshape(-1, ROW) for n in names], 0)

    def unpack_full(buf, names):
        r0 = 0
        for n in names:
            r = loc[n].size // ROW
            wfull[n] = buf[:, r0:r0 + r, :].reshape(-1, loc[n].shape[1])
            r0 += _tile_rows(r)

    def pack_grads(names):
        return _pack_rows([gw[n].astype(BF16).reshape(NDEV, -1, ROW) for n in names], 1)

    def update(buf, names):
        r0 = 0
        for n in names:
            r = loc[n].size // ROW
            parts = buf[:, r0:r0 + r, :].reshape((NDEV,) + loc[n].shape)
            r0 += _tile_rows(r)
            step = adamw_t if n in COL_SHARDED else adamw
            out = step("adamw_" + n, parts, a[n][0], a["m_" + n][0], a["v_" + n][0])
            for kind, t in zip(("grad", "delta", "new_m", "new_v"), out):
                res[kind + "_" + n] = t[None]

    unpack_full(all_gather("gather_ffn1", pack_local(G_FFN1)), G_FFN1)

    n_ada = w_ada.shape[2]
    mod_part = ada_fwd(c_all, w_ada[0], lax.dynamic_slice(b_ada, (0, me * n_ada), (1, n_ada)))
    mod_all = all_gather("gather_mod", mod_part).transpose(1, 0, 2).reshape(NDEV * bl, 9 * D)
    mod = lax.dynamic_slice(mod_all, (me * bl, 0), (bl, 9 * D)).reshape(bl, 9, D)
    mods = [mod[:, k:k + 1, :] for k in range(9)]

    h0 = x.reshape(t_rows, D)
    h1, f1, u1, pa1, pb1, wg_rest = ffn_fwd("ffn1_fwd", h0, mod[:, 0:3, :], g_ffn1, wfull['w1_ffn1'], wfull['w3_ffn1'],
                                  wfull['w2_ffn1'], seq, gather=pack_local(G_MIX + G_FFN2))
    unpack_full(wg_rest, G_MIX + G_FFN2)
    win = wfull['w_in']
    o_small, o_s5, o_gate = 4 * DNW, 4 * DNW + 2 * NH, 4 * DNW + 2 * NH + S5W
    w_dn, w_small = win[:o_small], jnp.pad(win[o_small:o_s5], ((0, LANES - 2 * NH), (0, 0)))
    w_s5, w_gate = win[o_s5:o_gate], win[o_gate:]
    (u2,) = ew_call("mix_norm", fn_normmod, [h1], [mods[3], mods[4]], [g_mix], [(D, BF16)], tm_ew, seq)
    p_dn = mm("proj_dn", [(u2, w_dn)], True, F32)
    p_small = mm("proj_small", [(u2, w_small)], True, F32)
    p_s5 = mm("proj_s5", [(u2, w_s5)], True, F32)
    p_gate = mm("proj_gate", [(u2, w_gate)], True, F32)

    conv8 = jnp.pad(conv_full, ((0, 8 - CONVW), (0, 0)))
    alp = jnp.pad(a_log, ((0, 0), (NH, LANES - 2 * NH)))
    dtp = jnp.pad(dt_bias, ((0, 0), (NH, LANES - 2 * NH)))
    nb_dn = DN_ROWS if bl % DN_ROWS == 0 else 1
    p_dn3, p_small3 = p_dn.reshape(bl, seq, 4 * DNW), p_small.reshape(bl, seq, LANES)
    qkv3 = dn_prep_fwd(p_dn3, conv8)
    o_pre3, sprev, tinv = deltanet_fwd(qkv3, p_small3, alp, dtp, nb_dn)
    o_pre = o_pre3.reshape(t_rows, DNW)
    z_raw = p_dn[:, 3 * DNW:]

    s5_params = [lam_re.reshape(1, S5N), lam_im.reshape(1, S5N), log_step,
                 b_re[0].transpose(2, 0, 1).reshape(S5C, S5N), b_im[0].transpose(2, 0, 1).reshape(S5C, S5N),
                 c_re[0].transpose(1, 0, 2).reshape(S5C, S5N), c_im[0].transpose(1, 0, 2).reshape(S5C, S5N)]
    tables = s5_tables_fwd(s5_params)
    p_s53 = p_s5.reshape(bl, seq, S5W)
    y_s53, xs = s5_fwd(p_s53, tables, d_skip)
    y_s5 = y_s53.reshape(t_rows, S5W)
    tail_in = [o_pre, z_raw, y_s5, p_gate]
    tail_w = [g_onorm, wfull['w_glu'], b_glu, wfull['w_proj_a'], wfull['w_proj_b']]
    (merged,) = ew_call("mix_tail", fn_mix_tail, tail_in, [], tail_w, [(D, BF16)], tm_tail, seq)
    mo = mm("proj_out", [(merged, wfull['w_out'])], False, F32)
    (h2,) = ew_call("mix_resid", lambda p, q, gt: (q + gt * p,), [mo, h1], [mods[5]], [], [(D, F32)], tm_ew, seq)
    h3, f3, u3, pa3, pb3 = ffn_fwd("ffn2_fwd", h2, mod[:, 6:9, :], g_ffn2, wfull['w1_ffn2'], wfull['w3_ffn2'], wfull['w2_ffn2'], seq)

    dh3, dg_final, loss_part = loss_head(h3, loss_target.reshape(t_rows, D), g_final.reshape(1, D), seq)
    loss = lax.psum(loss_part[0, 0], ("x", "y", "c"))

    dh2, a3, d1_3, d3_3, df3, dmod_c, dg_ffn2 = ffn_bwd("ffn2_bwd", dh3, h2, f3, pa3, pb3, mod[:, 6:9, :], g_ffn2, wfull['w1_ffn2'],
                                                   wfull['w3_ffn2'], wfull['w2_ffn2'], seq)
    gw['w1_ffn2'] = mm_tn("gw1_ffn2", d1_3, u3)
    gw['w3_ffn2'] = mm_tn("gw3_ffn2", d3_3, u3)
    gw['w2_ffn2'] = mm_tn("gw2_ffn2", a3, df3)

    (dmo,), (dgt2,), _ = ew_vjp_call("mix_resid_bwd", fn_resid, [mo], [mods[5]], [], [dh2], [(0, BF16)], tm_ew, seq)
    gw['w_out'] = mm_tn("gw_out", merged, dmo)
    d_merged = mm("d_merged", [(dmo, wfull['w_out'])], True, F32)
    (d_opre, d_z, d_ys5, d_gate), _, tail_gw = ew_vjp_call(
        "mix_tail_bwd", fn_mix_tail, tail_in, [], tail_w, [d_merged], [(0, F32), (1, F32), (2, F32), (3, BF16)], tm_tail, seq)
    dg_onorm, gw['w_glu'], dg_bglu, gw['w_proj_a'], gw['w_proj_b'] = tail_gw
    d_qkv3, d_psmall3, d_alp, d_dtp, rs_ffn2 = deltanet_bwd(
        qkv3, p_small3, alp, dtp, sprev, tinv, d_opre.reshape(bl, seq, DNW), nb_dn, exchange=pack_grads(G_FFN2))
    d_pdn3, d_conv8 = dn_prep_bwd(p_dn3, conv8, d_qkv3, d_z.reshape(bl, seq, DNW))
    d_pdn, d_psmall = d_pdn3.reshape(t_rows, 4 * DNW), d_psmall3.reshape(t_rows, LANES)

    s5_out = s5_bwd(p_s53, tables, d_skip, xs, d_ys5.reshape(bl, seq, S5W))
    d_ps5, d_tables, dg_dskip = s5_out[0].reshape(t_rows, S5W), s5_out[1:11], s5_out[11]
    d_s5p = s5_tables_bwd(s5_params, d_tables)

    d_pdn_b, d_psm_b, d_ps5_b = d_pdn, d_psmall, d_ps5
    gw['w_in'] = jnp.concatenate([mm_tn("gw_dn", d_pdn_b, u2), mm_tn("gw_small", d_psm_b, u2)[:2 * NH],
                                  mm_tn("gw_s5", d_ps5_b, u2), mm_tn("gw_gate", d_gate, u2)], axis=0)
    du2 = mm("d_u2", [(d_pdn_b, w_dn), (d_psm_b, w_small), (d_ps5_b, w_s5), (d_gate, w_gate)], False, F32)
    (dh1,), (dsh2, dsc2), (dg_mix,) = ew_vjp_call("mix_norm_bwd", fn_normmod, [h1], [mods[3], mods[4]], [g_mix], [du2],
                                                  [(0, F32)], tm_ew, seq, addend=dh2)

    dh0, a1, d1_1, d3_1, df1, dmod_a, dg_ffn1, rs_mix = ffn_bwd(
        "ffn1_bwd", dh1, h0, f1, pa1, pb1, mod[:, 0:3, :], g_ffn1, wfull['w1_ffn1'], wfull['w3_ffn1'], wfull['w2_ffn1'], seq,
        exchange=pack_grads(G_MIX))
    gw['w1_ffn1'] = mm_tn("gw1_ffn1", d1_1, u1)
    gw['w3_ffn1'], rs_w1 = mm_tn("gw3_ffn1", d3_1, u1, exchange=pack_grads(['w1_ffn1']))
    gw['w2_ffn1'], rs_w3 = mm_tn("gw2_ffn1", a1, df1, exchange=pack_grads(['w3_ffn1']))

    update(rs_ffn2, G_FFN2)
    update(rs_mix, G_MIX)
    update(rs_w1, ['w1_ffn1'])
    update(rs_w3, ['w3_ffn1'])
    update(all_to_all("scatter_w2_ffn1", pack_grads(['w2_ffn1'])), ['w2_ffn1'])

    dmod_mine = jnp.concatenate([dmod_a, dsh2, dsc2, dgt2, dmod_c], axis=1).reshape(bl, 9 * D)
    small_grads = {
        'g_ffn1': dg_ffn1, 'g_mix': dg_mix, 'a_log': d_alp[:, NH:2 * NH], 'dt_bias': d_dtp[:, NH:2 * NH],
        'g_onorm': dg_onorm, 'lam_re': d_s5p[0].reshape(1, S5G, S5P), 'lam_im': d_s5p[1].reshape(1, S5G, S5P),
        'log_step': d_s5p[2],
        'b_re': d_s5p[3].reshape(S5C, S5G, S5P).transpose(1, 2, 0)[None],
        'b_im': d_s5p[4].reshape(S5C, S5G, S5P).transpose(1, 2, 0)[None],
        'c_re': d_s5p[5].reshape(S5C, S5G, S5P).transpose(1, 0, 2)[None],
        'c_im': d_s5p[6].reshape(S5C, S5G, S5P).transpose(1, 0, 2)[None],
        'd_skip': dg_dskip, 'b_glu': dg_bglu, 'g_ffn2': dg_ffn2, 'g_final': dg_final.reshape(D)}
    small_shapes = [a[n].shape for n in SMALL]
    small_pack = _pack([small_grads[n] for n in SMALL], F32)
    n_small = small_pack.shape[0]
    sg = all_gather("gather_small_grads",
                    jnp.concatenate([small_pack, _pack([dmod_mine, d_conv8[:CONVW]], F32)], axis=0))
    pieces = _unpack8(sg[:, n_small:, :], [dmod_mine.shape, (CONVW, 3 * DNW)])
    dmod_all = pieces[0].reshape(NDEV * bl, 9 * D)
    g_wada, g_bada = ada_bwd(c_all, lax.dynamic_slice(dmod_all, (0, me * n_ada), (NDEV * bl, n_ada)), dmod_all)

    n_conv = conv_qkv.shape[2]
    conv_parts = lax.dynamic_slice(pieces[1], (0, 0, me * n_conv), (NDEV, CONVW, n_conv))
    conv_parts = jnp.pad(conv_parts.reshape(NDEV, 1, -1), ((0, 0), (0, 7), (0, 0)))
    pad8 = lambda t: jnp.pad(t.reshape(1, -1), ((0, 7), (0, 0)))
    conv_res = adamw("adamw_conv", conv_parts, pad8(conv_qkv), pad8(m_conv_qkv), pad8(v_conv_qkv))
    for kind, buf in zip(("grad", "delta", "new_m", "new_v"), conv_res):
        res[kind + "_conv_qkv"] = buf[0].reshape(conv_qkv.shape)

    small_res = adamw("adamw_small", sg[:, :n_small, :], *[_pack([a[p + n] for n in SMALL], F32) for p in ("", "m_", "v_")])
    for kind, buf in zip(("grad", "delta", "new_m", "new_v"), small_res):
        for n, t in zip(SMALL, _unpack(buf, small_shapes)):
            res[kind + "_" + n] = t

    for n, g in (("w_ada", g_wada), ("b_ada", g_bada)):
        shp = a[n].shape
        r2 = lambda t: t.reshape(-1, shp[-1]) if n == "w_ada" else pad8(t)
        out = adamw("adamw_" + n, r2(g)[None], r2(a[n]), r2(a["m_" + n]), r2(a["v_" + n]))
        for kind, buf in zip(("grad", "delta", "new_m", "new_v"), out):
            res[kind + "_" + n] = (buf if n == "w_ada" else buf[0:1]).reshape(shp)

    outs = [loss, dh0.reshape(x.shape)]
    for kind in ("grad", "delta", "new_m", "new_v"):
        outs += [res[kind + "_" + n] for n in WEIGHTS]
    return tuple(outs)
```

```python
import functools
import math

import jax
import jax.numpy as jnp
from jax import lax
from jax.experimental import pallas as pl
from jax.experimental.pallas import tpu as pltpu

F32 = jnp.float32
BF16 = jnp.bfloat16
HI = lax.Precision.HIGHEST
H3 = lax.Precision.HIGH
SDS = jax.ShapeDtypeStruct

D = 1024
FF = 2816
FFN_TF = FF
FFN_FWD_TM = 256
FFN_BWD_TM = 256
NH = 8
DH = 64
DNW = NH * DH
CONVW = 4
CH = 64
DN_ROWS = 2
S5W = 512
S5G = 32
S5P = 64
S5C = 16
S5N = S5G * S5P
GB = 4
NDEV = 8
EPS = 1e-6
LANES = 128
ROW = 1024
VMEM_LIMIT = 56 * 1024 * 1024

ADAM_LR, ADAM_B1, ADAM_B2, ADAM_EPS, ADAM_WD, ADAM_STEP = 0.001, 0.9, 0.999, 1e-08, 0.01, 10

WEIGHTS = ['w_ada', 'b_ada', 'g_ffn1', 'w1_ffn1', 'w3_ffn1', 'w2_ffn1', 'g_mix', 'w_in', 'conv_qkv', 'a_log',
           'dt_bias', 'g_onorm', 'lam_re', 'lam_im', 'log_step', 'b_re', 'b_im', 'c_re', 'c_im', 'd_skip', 'w_glu',
           'b_glu', 'w_proj_a', 'w_proj_b', 'w_out', 'g_ffn2', 'w1_ffn2', 'w3_ffn2', 'w2_ffn2', 'g_final']
RS_WEIGHTS = ['w1_ffn1', 'w3_ffn1', 'w2_ffn1', 'w_in', 'w_glu', 'w_proj_a', 'w_proj_b', 'w_out', 'w1_ffn2', 'w3_ffn2',
              'w2_ffn2']
COL_SHARDED = {'w1_ffn1', 'w3_ffn1', 'w_in', 'w_proj_a', 'w_proj_b', 'w1_ffn2', 'w3_ffn2'}
G_FFN1 = ['w1_ffn1', 'w3_ffn1', 'w2_ffn1']
G_MIX = ['w_in', 'w_glu', 'w_proj_a', 'w_proj_b', 'w_out']
G_FFN2 = ['w1_ffn2', 'w3_ffn2', 'w2_ffn2']
SMALL = ['g_ffn1', 'g_mix', 'a_log', 'dt_bias', 'g_onorm', 'lam_re', 'lam_im', 'log_step', 'b_re', 'b_im', 'c_re',
         'c_im', 'd_skip', 'b_glu', 'g_ffn2', 'g_final']


def _cp(n_grid=0):
    if n_grid:
        return pltpu.CompilerParams(vmem_limit_bytes=VMEM_LIMIT, dimension_semantics=("arbitrary",) * n_grid)
    return pltpu.CompilerParams(vmem_limit_bytes=VMEM_LIMIT)


def _dot(a, b):
    return jnp.dot(a.astype(BF16), b.astype(BF16), preferred_element_type=F32)


def _dot_nt(a, b):
    return lax.dot_general(a.astype(BF16), b.astype(BF16), (((1,), (1,)), ((), ())), preferred_element_type=F32)


def _dot_tn(a, b):
    return lax.dot_general(a.astype(BF16), b.astype(BF16), (((0,), (0,)), ((), ())), preferred_element_type=F32)


def _dot_hi(a, b):
    return jnp.dot(a, b, precision=HI, preferred_element_type=F32)


def _dot_h3(a, b):
    return jnp.dot(a, b, precision=H3, preferred_element_type=F32)


@jax.custom_vjp
def bdot(a, b):
    return _dot(a, b)


bdot.defvjp(lambda a, b: (_dot(a, b), (a, b)),
            lambda r, g: (_dot_nt(g, r[1]).astype(r[0].dtype), _dot_tn(r[0], g).astype(r[1].dtype)))


@jax.custom_vjp
def bdot_nt(a, b):
    return _dot_nt(a, b)


bdot_nt.defvjp(lambda a, b: (_dot_nt(a, b), (a, b)),
               lambda r, g: (_dot(g, r[1]).astype(r[0].dtype), _dot_tn(g, r[0]).astype(r[1].dtype)))


@jax.custom_vjp
def bdot_tn(a, b):
    return _dot_tn(a, b)


bdot_tn.defvjp(lambda a, b: (_dot_tn(a, b), (a, b)),
               lambda r, g: (_dot_nt(r[1], g).astype(r[0].dtype), _dot(r[0], g).astype(r[1].dtype)))


def _silu(x):
    return x * jax.nn.sigmoid(x)


def _iota2(shape, axis):
    return lax.broadcasted_iota(jnp.int32, shape, axis)


def normmod(h, g, sc, sh):
    y = h * lax.rsqrt(jnp.mean(h * h, axis=-1, keepdims=True) + EPS) * g
    return y * (1.0 + sc) + sh


def fn_normmod(h, sh, sc, g):
    return (normmod(h, g, sc, sh),)


def fn_resid(mo, gt):
    return (gt * mo,)


def fn_merge(gate, ya, yb):
    return (jax.nn.sigmoid(gate[:, :D]) * ya + jax.nn.sigmoid(gate[:, D:]) * yb,)


def fn_glu(y, w, b):
    ge = jax.nn.gelu(y)
    return (ge * jax.nn.sigmoid(bdot(ge, w) + b),)


def fn_onorm(o, z, g_on):
    r = _iota2((DH, DNW), 0)
    c = _iota2((DH, DNW), 1)
    expand = (c % DH == r).astype(F32)
    r2 = _iota2((DNW, DNW), 0)
    c2 = _iota2((DNW, DNW), 1)
    avg = (r2 // DH == c2 // DH).astype(F32) * (1.0 / DH)
    ms = _dot_h3(o * o, avg)
    return (o * lax.rsqrt(ms + EPS) * _dot_hi(g_on, expand) * _silu(z),)


def fn_mix_tail(o_pre, z, y_s5, gate, g_on, w_glu, b_glu, wa_t, wb_t):
    (oa,) = fn_onorm(o_pre, z, g_on)
    (ob,) = fn_glu(y_s5, w_glu, b_glu)
    return fn_merge(gate, bdot_nt(oa, wa_t), bdot_nt(ob, wb_t))


def gate_fn(small, alp, dtp):
    beta = jax.nn.sigmoid(small)
    la = -jnp.exp(alp) * jax.nn.softplus(small + dtp)
    tri = (_iota2((CH, CH), 0) >= _iota2((CH, CH), 1)).astype(F32)
    gc = _dot_hi(tri, la)
    gct = lax.dot_general(la, tri, (((0,), (1,)), ((), ())), precision=HI, preferred_element_type=F32)
    return beta, gc, gct


def _bdg(a, b, ca, cb, hi):
    if not hi:
        a, b = a.astype(BF16), b.astype(BF16)
    return lax.dot_general(a, b, (((ca,), (cb,)), ((0,), (0,))), precision=H3 if hi else None,
                           preferred_element_type=F32)


def _batched_matmuls(hi):
    nn_ = lambda a, b: _bdg(a, b, 2, 1, hi)
    nt_ = lambda a, b: _bdg(a, b, 2, 2, hi)
    tn_ = lambda a, b: _bdg(a, b, 1, 1, hi)
    nn = jax.custom_vjp(nn_)
    nn.defvjp(lambda a, b: (nn_(a, b), (a, b)), lambda r, g: (nt_(g, r[1]), tn_(r[0], g)))
    nt = jax.custom_vjp(nt_)
    nt.defvjp(lambda a, b: (nt_(a, b), (a, b)), lambda r, g: (nn_(g, r[1]), tn_(g, r[0])))
    tn = jax.custom_vjp(tn_)
    tn.defvjp(lambda a, b: (tn_(a, b), (a, b)), lambda r, g: (nt_(r[1], g), nn_(r[0], g)))
    return nn, nt, tn


bnn, bnt, btn = _batched_matmuls(False)
hnn, hnt, htn = _batched_matmuls(True)


def _unit_lower_inverse(a):
    r = _iota2((1, CH, CH), 1)
    c = _iota2((1, CH, CH), 2)
    eye = (r == c).astype(F32)
    d = jnp.where(r // 8 == c // 8, a, 0.0)
    inv = eye - d
    p = d
    for _ in range(2):
        p = hnn(p, p)
        inv = inv + hnn(inv, p)
    for blk in (16, 32, 64):
        off = jnp.where((r // blk == c // blk) & (r // (blk // 2) != c // (blk // 2)), a, 0.0)
        inv = inv - hnn(hnn(inv, off), inv)
    return inv


@jax.custom_vjp
def _inverse_given(a, t):
    return t


_inverse_given.defvjp(lambda a, t: (t, t), lambda t, g: (-hnt(htn(t, g), t), jnp.zeros_like(t)))


def dn_prep(xc, w):
    t = xc.shape[0] - 8
    c = xc[5:5 + t] * w[0:1] + xc[6:6 + t] * w[1:2] + xc[7:7 + t] * w[2:3] + xc[8:8 + t] * w[3:4]
    act = _silu(c)
    q, k, v = act[:, :DNW], act[:, DNW:2 * DNW], act[:, 2 * DNW:]
    ones = (_iota2((DNW, DNW), 0) // DH == _iota2((DNW, DNW), 1) // DH).astype(F32)
    q = q * lax.rsqrt(_dot_h3(q * q, ones) + EPS) * (DH ** -0.5)
    k = k * lax.rsqrt(_dot_h3(k * k, ones) + EPS)
    return jnp.concatenate([q, k, v], axis=1)


def dn_chunk(q, k, v, b, g, gt, s_prev, t_saved=None):
    r = _iota2((1, CH, CH), 1)
    c = _iota2((1, CH, CH), 2)
    causal = r >= c
    dec = jnp.where(causal, jnp.exp(jnp.where(causal, g - gt, 0.0)), 0.0)
    kb = k * b
    qk = bnt(jnp.concatenate([q, kb], axis=1), k)
    attn = qk[:, :CH] * dec
    a = jnp.where(r > c, qk[:, CH:] * dec, 0.0)
    tinv = _unit_lower_inverse(a) if t_saved is None else _inverse_given(a, t_saved)
    eg = jnp.exp(g)
    uw = hnn(tinv, jnp.concatenate([v * b, kb * eg], axis=2))
    g_last = g[:, CH - 1:CH]
    ws = bnn(jnp.concatenate([uw[..., DH:], q * eg], axis=1), s_prev)
    v_new = uw[..., :DH] - ws[:, :CH]
    o = ws[:, CH:] + bnn(attn, v_new)
    s_new = s_prev * jnp.exp(g_last) + btn(k * jnp.exp(g_last - g), v_new)
    return o, s_new, tinv


def s5_chunk(u, xp_re, xp_im, bb_re, bb_im, cc_re, cc_im, p0r, p0i, p1r, p1i, pir, pii, dsk):
    nb = u.shape[0]
    u2 = u.reshape(nb * CH, LANES)
    bu_re = bdot(u2, bb_re).reshape(nb, CH, 512)
    bu_im = bdot(u2, bb_im).reshape(nb, CH, 512)
    xt_re = pir * bu_re - pii * bu_im
    xt_im = pir * bu_im + pii * bu_re
    tri = jnp.broadcast_to((_iota2((1, CH, CH), 1) >= _iota2((1, CH, CH), 2)).astype(F32), (nb, CH, CH))
    cs_re = hnn(tri, xt_re)
    cs_im = hnn(tri, xt_im)
    x_re = p0r * cs_re - p0i * cs_im + p1r * xp_re - p1i * xp_im
    x_im = p0r * cs_im + p0i * cs_re + p1r * xp_im + p1i * xp_re
    y = bdot_nt(x_re.reshape(nb * CH, 512), cc_re) - bdot_nt(x_im.reshape(nb * CH, 512), cc_im) + dsk * u2
    return y.reshape(nb, CH, LANES), x_re[:, CH - 1:CH], x_im[:, CH - 1:CH]


def s5_tables(lam_re, lam_im, log_step, bre, bim, cre, cim):
    expand = (_iota2((S5G, S5N), 1) // S5P == _iota2((S5G, S5N), 0)).astype(F32)
    step = _dot_hi(jnp.exp(log_step), expand)
    lre = jnp.minimum(lam_re, -1e-4)
    lr = lre * step
    ang = lam_im * step
    mag = jnp.exp(lr)
    lb_re = mag * jnp.cos(ang)
    lb_im = mag * jnp.sin(ang)
    den = lre * lre + lam_im * lam_im
    coef_re = ((lb_re - 1.0) * lre + lb_im * lam_im) / den
    coef_im = (lb_im * lre - (lb_re - 1.0) * lam_im) / den
    bb_re = coef_re * bre - coef_im * bim
    bb_im = coef_re * bim + coef_im * bre
    j = _iota2((CH, 1), 0).astype(F32)
    e0 = jnp.exp(j * lr)
    e1 = jnp.exp((j + 1.0) * lr)
    ei = jnp.exp(-j * lr)
    mask = (_iota2((LANES, 512), 0) // S5C == _iota2((LANES, 512), 1) // S5P).astype(F32)

    def blocks(t):
        return jnp.concatenate([(jnp.tile(t[:, gb * 512:(gb + 1) * 512], (LANES // S5C, 1)) * mask)[None]
                                for gb in range(GB)], axis=0)

    return (blocks(bb_re), blocks(bb_im), blocks(cre), blocks(cim),
            e0 * jnp.cos(j * ang), e0 * jnp.sin(j * ang),
            e1 * jnp.cos((j + 1.0) * ang), e1 * jnp.sin((j + 1.0) * ang),
            ei * jnp.cos(j * ang), -ei * jnp.sin(j * ang))


def _row_specs(tiled, batch, bcast, tm, tpb):
    specs = [pl.BlockSpec((tm, a.shape[1]), lambda i: (i, 0)) for a in tiled]
    specs += [pl.BlockSpec((None,) + a.shape[1:], lambda i: (i // tpb, 0, 0)) for a in batch]
    specs += [pl.BlockSpec(a.shape, lambda i, nd=a.ndim: (0,) * nd) for a in bcast]
    return specs


def ew_call(name, fn, tiled, batch, bcast, outs, tm, seq):
    t_rows = tiled[0].shape[0]
    n_in = len(tiled) + len(batch) + len(bcast)

    def body(*refs):
        vals = [r[...].astype(F32) for r in refs[:n_in]]
        for r, o in zip(refs[n_in:], fn(*vals)):
            r[...] = o.astype(r.dtype)

    return pl.pallas_call(
        body, grid=(t_rows // tm,), in_specs=_row_specs(tiled, batch, bcast, tm, seq // tm),
        out_specs=[pl.BlockSpec((tm, w), lambda i: (i, 0)) for w, _ in outs],
        out_shape=[SDS((t_rows, w), dt) for w, dt in outs], name=name, compiler_params=_cp(1))(*tiled, *batch, *bcast)


def ew_vjp_call(name, fn, tiled, batch, bcast, cts, want, tm, seq, addend=None):
    t_rows = tiled[0].shape[0]
    tpb = seq // tm
    n_t, n_b, n_c = len(tiled), len(batch), len(bcast)
    n_in = n_t + n_b + n_c
    extra = [] if addend is None else [addend]

    def body(*refs):
        i = pl.program_id(0)
        vals = [r[...].astype(F32) for r in refs[:n_in]]
        ctv = tuple(r[...].astype(F32) for r in refs[n_in:n_in + len(cts)])
        outs = refs[n_in + len(cts) + len(extra):]
        _, vjp = jax.vjp(fn, *vals)
        grads = vjp(ctv)
        for k, (r, (idx, _)) in enumerate(zip(outs[:len(want)], want)):
            g = grads[idx]
            if k == 0 and extra:
                g = g + refs[n_in + len(cts)][...]
            r[...] = g.astype(r.dtype)
        for k in range(n_b):
            r, g = outs[len(want) + k], grads[n_t + k]

            @pl.when(i % tpb == 0)
            def _(r=r, g=g):
                r[...] = g

            @pl.when(i % tpb != 0)
            def _(r=r, g=g):
                r[...] += g
        for k in range(n_c):
            r, g = outs[len(want) + n_b + k], grads[n_t + n_b + k]

            @pl.when(i == 0)
            def _(r=r, g=g):
                r[...] = g

            @pl.when(i != 0)
            def _(r=r, g=g):
                r[...] += g

    out_specs = [pl.BlockSpec((tm, tiled[idx].shape[1]), lambda i: (i, 0)) for idx, _ in want]
    out_specs += [pl.BlockSpec((None,) + a.shape[1:], lambda i: (i // tpb, 0, 0)) for a in batch]
    out_specs += [pl.BlockSpec(a.shape, lambda i, nd=a.ndim: (0,) * nd) for a in bcast]
    out_shape = [SDS(tiled[idx].shape, dt) for idx, dt in want]
    out_shape += [SDS(a.shape, F32) for a in batch] + [SDS(a.shape, F32) for a in bcast]
    res = pl.pallas_call(
        body, grid=(t_rows // tm,),
        in_specs=_row_specs(tiled, batch, bcast, tm, tpb)
        + [pl.BlockSpec((tm, a.shape[1]), lambda i: (i, 0)) for a in list(cts) + extra],
        out_specs=out_specs, out_shape=out_shape, name=name, compiler_params=_cp(1))(*tiled, *batch, *bcast, *cts, *extra)
    return res[:len(want)], res[len(want):len(want) + n_b], res[len(want) + n_b:]


def _pick(n, cands):
    for c in cands:
        if n % c == 0:
            return c
    return n


def mm(name, pairs, nt, out_dtype):
    m = pairs[0][0].shape[0]
    n = pairs[0][1].shape[0 if nt else 1]
    k_total = sum(a.shape[1] for a, _ in pairs)
    tm = _pick(m, (1024, 512, 256, 128) if k_total <= 2048 else (512, 256, 128))
    tn = _pick(n, (512, 256, 128))
    np_ = len(pairs)

    def body(*refs):
        acc = None
        for p in range(np_):
            a, b = refs[2 * p][...], refs[2 * p + 1][...]
            t = _dot_nt(a, b) if nt else _dot(a, b)
            acc = t if acc is None else acc + t
        refs[2 * np_][...] = acc.astype(out_dtype)

    in_specs, ops = [], []
    for a, b in pairs:
        k = a.shape[1]
        in_specs.append(pl.BlockSpec((tm, k), lambda i, j: (i, 0)))
        in_specs.append(pl.BlockSpec((tn, k), lambda i, j: (j, 0)) if nt else pl.BlockSpec((k, tn), lambda i, j: (0, j)))
        ops += [a, b]
    return pl.pallas_call(
        body, grid=(m // tm, n // tn), in_specs=in_specs, out_specs=pl.BlockSpec((tm, tn), lambda i, j: (i, j)),
        out_shape=SDS((m, n), out_dtype), name=name, compiler_params=_cp(2))(*ops)


def mm_tn(name, a, b, exchange=None):
    t_rows, m = a.shape
    n = b.shape[1]
    tn = n if n <= 1024 else _pick(n, (1024, 512, 256, 128))
    tm = max([t for t in range(LANES, m + 1, LANES) if m % t == 0 and t * tn * 4 <= 6 * 1024 * 1024] or [m])
    tk = _pick(t_rows, (512, 256, 128, 64))
    grid = (m // tm, n // tn, t_rows // tk)
    extra = [] if exchange is None else [exchange]

    def body(*refs):
        a_ref, b_ref = refs[:2]
        o_ref, acc = refs[2 + len(extra)], refs[3 + 2 * len(extra)]
        i, j, k = pl.program_id(0), pl.program_id(1), pl.program_id(2)
        if extra:
            start, finish = _exchange_phases(refs[2], refs[4], *refs[6:9])
            pl.when((i == 0) & (j == 0) & (k == 0))(start)

        @pl.when(k == 0)
        def _():
            acc[...] = jnp.zeros_like(acc)

        acc[...] += _dot_tn(a_ref[...], b_ref[...])

        @pl.when(k == grid[2] - 1)
        def _():
            o_ref[...] = acc[...].astype(BF16)

        if extra:
            pl.when((i == grid[0] - 1) & (j == grid[1] - 1) & (k == grid[2] - 1))(finish)

    res = pl.pallas_call(
        body, grid=grid,
        in_specs=[pl.BlockSpec((tk, tm), lambda i, j, k: (k, i)), pl.BlockSpec((tk, tn), lambda i, j, k: (k, j))]
        + [HBM_SPEC] * len(extra),
        out_specs=[pl.BlockSpec((tm, tn), lambda i, j, k: (i, j))] + [HBM_SPEC] * len(extra),
        out_shape=[SDS((m, n), BF16)] + [SDS(x.shape, x.dtype) for x in extra],
        scratch_shapes=[pltpu.VMEM((tm, tn), F32)] + (_comm_scratch() if extra else []), name=name,
        compiler_params=_cp(3))(a, b, *extra)
    return res if extra else res[0]


def _ffn_weight_spec():
    if FFN_TF == FF:
        return pl.BlockSpec((FF, D), lambda i, j: (0, 0), pipeline_mode=pl.Buffered(1))
    return pl.BlockSpec((FFN_TF, D), lambda i, j: (j, 0))


def ffn_fwd(name, h, mod3, g, w1, w3, w2, seq, gather=None):
    t_rows = h.shape[0]
    tm = _pick(seq, (FFN_FWD_TM, 128, 64))
    tf = FFN_TF
    tpb = seq // tm
    nf = FF // tf
    nt = t_rows // tm
    extra = [] if gather is None else [gather]

    def body(*refs):
        h_ref, mod_ref, g_ref, w1_ref, w3_ref, w2_ref = refs[:6]
        ho_ref, f_ref, u_ref, h1_ref, h3_ref = refs[6 + len(extra):11 + len(extra)]
        acc = refs[11 + 2 * len(extra)]
        i, j = pl.program_id(0), pl.program_id(1)
        if extra:
            start, forward, finish = _gather_phases(refs[6], refs[12], *refs[14:17])
            pl.when((i == 0) & (j == 0))(start)
            pl.when((i == nt - 1) & (j == 0))(forward)

        @pl.when(j == 0)
        def _():
            u_ref[...] = normmod(h_ref[...], g_ref[...], mod_ref[1:2, :], mod_ref[0:1, :]).astype(BF16)
            acc[...] = jnp.zeros_like(acc)

        u = u_ref[...]
        h1 = _dot_nt(u, w1_ref[...])
        h3 = _dot_nt(u, w3_ref[...])
        h1_ref[...] = h1.astype(BF16)
        h3_ref[...] = h3.astype(BF16)
        acc[...] += _dot(_silu(h1) * h3, w2_ref[...])

        @pl.when(j == nf - 1)
        def _():
            f_ref[...] = acc[...]
            ho_ref[...] = h_ref[...] + 0.5 * mod_ref[2:3, :] * acc[...]

        if extra:
            pl.when((i == nt - 1) & (j == nf - 1))(finish)

    row = lambda i, j: (i, 0)
    return pl.pallas_call(
        body, grid=(nt, nf),
        in_specs=[pl.BlockSpec((tm, D), row), pl.BlockSpec((None, 3, D), lambda i, j: (i // tpb, 0, 0)),
                  pl.BlockSpec((1, D), lambda i, j: (0, 0)), _ffn_weight_spec(), _ffn_weight_spec(), _ffn_weight_spec()]
        + [HBM_SPEC] * len(extra),
        out_specs=[pl.BlockSpec((tm, D), row), pl.BlockSpec((tm, D), row), pl.BlockSpec((tm, D), row),
                   pl.BlockSpec((tm, tf), lambda i, j: (i, j)), pl.BlockSpec((tm, tf), lambda i, j: (i, j))]
        + [HBM_SPEC] * len(extra),
        out_shape=[SDS((t_rows, D), F32), SDS((t_rows, D), F32), SDS((t_rows, D), BF16), SDS((t_rows, FF), BF16),
                   SDS((t_rows, FF), BF16)] + [SDS((NDEV,) + x.shape, x.dtype) for x in extra],
        scratch_shapes=[pltpu.VMEM((tm, D), F32)] + (_comm_scratch() if extra else []), name=name,
        compiler_params=_cp(2))(h, mod3, g, w1, w3, w2, *extra)


def ffn_bwd(name, dho, h, f_out, h1_in, h3_in, mod3, g, w1, w3, w2, seq, exchange=None):
    t_rows = h.shape[0]
    tm = _pick(seq, (FFN_BWD_TM, 128, 64))
    tf = FFN_TF
    tpb = seq // tm
    nf = FF // tf
    nt = t_rows // tm
    extra = [] if exchange is None else [exchange]

    def body(*refs):
        dho_ref, h_ref, f_ref, h1_ref, h3_ref, mod_ref, g_ref, w1_ref, w3_ref, w2_ref = refs[:10]
        dh_ref, a_ref, dh1_ref, dh3_ref, df_scr, dmod_ref, dg_ref = refs[10 + len(extra):17 + len(extra)]
        du_acc = refs[17 + 2 * len(extra)]
        i, j = pl.program_id(0), pl.program_id(1)
        if extra:
            start, finish = _exchange_phases(refs[10], refs[18], *refs[20:23])
            pl.when((i == 0) & (j == 0))(start)

        @pl.when(j == 0)
        def _():
            df_scr[...] = (0.5 * mod_ref[2:3, :] * dho_ref[...]).astype(BF16)
            du_acc[...] = jnp.zeros_like(du_acc)

        h1 = h1_ref[...].astype(F32)
        h3 = h3_ref[...].astype(F32)
        sg = jax.nn.sigmoid(h1)
        s = h1 * sg
        da = _dot_nt(df_scr[...], w2_ref[...])
        dh3 = (da * s).astype(BF16)
        dh1 = (da * h3 * (sg * (1.0 + h1 * (1.0 - sg)))).astype(BF16)
        a_ref[...] = (s * h3).astype(BF16)
        dh1_ref[...] = dh1
        dh3_ref[...] = dh3
        du_acc[...] += _dot(dh1, w1_ref[...]) + _dot(dh3, w3_ref[...])

        @pl.when(j == nf - 1)
        def _():
            _, vjp = jax.vjp(normmod, h_ref[...], g_ref[...], mod_ref[1:2, :], mod_ref[0:1, :])
            dh_n, dg, dsc, dsh = vjp(du_acc[...])
            dh_ref[...] = dho_ref[...] + dh_n
            dgt = jnp.sum(0.5 * dho_ref[...] * f_ref[...], axis=0, keepdims=True)
            dmod = jnp.concatenate([dsh, dsc, dgt], axis=0)

            @pl.when(i % tpb == 0)
            def _():
                dmod_ref[...] = dmod

            @pl.when(i % tpb != 0)
            def _():
                dmod_ref[...] += dmod

            @pl.when(i == 0)
            def _():
                dg_ref[...] = dg

            @pl.when(i != 0)
            def _():
                dg_ref[...] += dg

        if extra:
            pl.when((i == nt - 1) & (j == nf - 1))(finish)

    row = lambda i, j: (i, 0)
    col = lambda i, j: (i, j)
    return pl.pallas_call(
        body, grid=(nt, nf),
        in_specs=[pl.BlockSpec((tm, D), row), pl.BlockSpec((tm, D), row), pl.BlockSpec((tm, D), row),
                  pl.BlockSpec((tm, tf), col), pl.BlockSpec((tm, tf), col),
                  pl.BlockSpec((None, 3, D), lambda i, j: (i // tpb, 0, 0)),
                  pl.BlockSpec((1, D), lambda i, j: (0, 0)), _ffn_weight_spec(), _ffn_weight_spec(), _ffn_weight_spec()]
        + [HBM_SPEC] * len(extra),
        out_specs=[pl.BlockSpec((tm, D), row), pl.BlockSpec((tm, tf), col), pl.BlockSpec((tm, tf), col),
                   pl.BlockSpec((tm, tf), col), pl.BlockSpec((tm, D), row),
                   pl.BlockSpec((None, 3, D), lambda i, j: (i // tpb, 0, 0)), pl.BlockSpec((1, D), lambda i, j: (0, 0))]
        + [HBM_SPEC] * len(extra),
        out_shape=[SDS((t_rows, D), F32), SDS((t_rows, FF), BF16), SDS((t_rows, FF), BF16), SDS((t_rows, FF), BF16),
                   SDS((t_rows, D), BF16), SDS(mod3.shape, F32), SDS((1, D), F32)] + [SDS(x.shape, x.dtype) for x in extra],
        scratch_shapes=[pltpu.VMEM((tm, D), F32)] + (_comm_scratch() if extra else []), name=name,
        compiler_params=_cp(2))(dho, h, f_out, h1_in, h3_in, mod3, g, w1, w3, w2, *extra)


def _dn_cols(part, hd):
    return slice(part * DNW + hd * DH, part * DNW + (hd + 1) * DH)


def _qkv_stacks(qkv_ref, nb):
    pairs = [(b, hd) for b in range(nb) for hd in range(NH)]
    return [jnp.stack([qkv_ref[b, :, _dn_cols(part, hd)] for b, hd in pairs]) for part in range(3)]


def dn_prep_fwd(p_dn, conv8):
    bl, seq, _ = p_dn.shape
    tp = _pick(seq, (256, 128, 64))

    def body(raw_ref, halo_ref, conv_ref, o_ref):
        hm = (pl.program_id(1) > 0).astype(F32)
        o_ref[...] = dn_prep(jnp.concatenate([halo_ref[...] * hm, raw_ref[...]], axis=0), conv_ref[...])

    return pl.pallas_call(
        body, grid=(bl, seq // tp),
        in_specs=[pl.BlockSpec((None, tp, 3 * DNW), lambda b, i: (b, i, 0)),
                  pl.BlockSpec((None, 8, 3 * DNW), lambda b, i: (b, jnp.maximum(i * (tp // 8) - 1, 0), 0)),
                  pl.BlockSpec((8, 3 * DNW), lambda b, i: (0, 0))],
        out_specs=pl.BlockSpec((None, tp, 3 * DNW), lambda b, i: (b, i, 0)),
        out_shape=SDS((bl, seq, 3 * DNW), F32), name="dn_prep_fwd", compiler_params=_cp(2))(p_dn, p_dn, conv8)


def dn_prep_bwd(p_dn, conv8, d_qkv, d_z):
    bl, seq, _ = p_dn.shape
    tp = _pick(seq, (256, 128, 64))
    nt = seq // tp

    def body(raw_ref, halo_ref, conv_ref, dq_ref, dz_ref, draw_ref, dconv_ref, carry):
        b, r = pl.program_id(0), pl.program_id(1)

        @pl.when((b == 0) & (r == 0))
        def _():
            dconv_ref[...] = jnp.zeros_like(dconv_ref)

        @pl.when(r == 0)
        def _():
            carry[...] = jnp.zeros_like(carry)

        hm = (r < nt - 1).astype(F32)
        _, vjp = jax.vjp(dn_prep, jnp.concatenate([halo_ref[...] * hm, raw_ref[...]], axis=0), conv_ref[...])
        dxc, dw = vjp(dq_ref[...])
        tail = dxc[tp:tp + 8] + carry[...]
        draw_ref[:, 0:3 * DNW] = jnp.concatenate([dxc[8:tp], tail], axis=0).astype(BF16)
        draw_ref[:, 3 * DNW:4 * DNW] = dz_ref[...].astype(BF16)
        carry[...] = dxc[0:8] * hm
        dconv_ref[...] += dw

    blk = lambda b, r: (b, nt - 1 - r, 0)
    return pl.pallas_call(
        body, grid=(bl, nt),
        in_specs=[pl.BlockSpec((None, tp, 3 * DNW), blk),
                  pl.BlockSpec((None, 8, 3 * DNW), lambda b, r: (b, jnp.maximum((nt - 1 - r) * (tp // 8) - 1, 0), 0)),
                  pl.BlockSpec((8, 3 * DNW), lambda b, r: (0, 0)), pl.BlockSpec((None, tp, 3 * DNW), blk),
                  pl.BlockSpec((None, tp, DNW), blk)],
        out_specs=[pl.BlockSpec((None, tp, 4 * DNW), blk), pl.BlockSpec((8, 3 * DNW), lambda b, r: (0, 0))],
        out_shape=[SDS((bl, seq, 4 * DNW), BF16), SDS((8, 3 * DNW), F32)],
        scratch_shapes=[pltpu.VMEM((8, 3 * DNW), F32)], name="dn_prep_bwd", compiler_params=_cp(2))(p_dn, p_dn, conv8, d_qkv, d_z)


def _gate_stacks(gates, nb):
    pairs = [(b, hd) for b in range(nb) for hd in range(NH)]
    bs = jnp.stack([gates[b][0][:, hd:hd + 1] for b, hd in pairs])
    gs = jnp.stack([gates[b][1][:, NH + hd:NH + hd + 1] for b, hd in pairs])
    gts = jnp.stack([gates[b][2][NH + hd:NH + hd + 1, :] for b, hd in pairs])
    return bs, gs, gts


def deltanet_fwd(qkv, p_small, alp, dtp, nb):
    bl, seq, _ = qkv.shape
    nc = seq // CH
    ng = nb * NH

    def body(qkv_ref, small_ref, alp_ref, dtp_ref, o_ref, sprev_ref, tinv_ref, s_scr):
        @pl.when(pl.program_id(1) == 0)
        def _():
            s_scr[...] = jnp.zeros_like(s_scr)

        gates = [gate_fn(small_ref[b], alp_ref[...], dtp_ref[...]) for b in range(nb)]
        s_prev = s_scr[...]
        o, s_new, tinv = dn_chunk(*_qkv_stacks(qkv_ref, nb), *_gate_stacks(gates, nb), s_prev)
        sprev_ref[...] = s_prev
        tinv_ref[...] = tinv
        s_scr[...] = s_new
        for b in range(nb):
            for hd in range(NH):
                o_ref[b, :, hd * DH:(hd + 1) * DH] = o[b * NH + hd]

    blk = lambda bb, n: (bb, n, 0)
    const = lambda bb, n: (0, 0)
    saved = pl.BlockSpec((None, ng, DH, DH), lambda bb, n: (bb * nc + n, 0, 0, 0))
    return pl.pallas_call(
        body, grid=(bl // nb, nc),
        in_specs=[pl.BlockSpec((nb, CH, 3 * DNW), blk), pl.BlockSpec((nb, CH, LANES), blk),
                  pl.BlockSpec((1, LANES), const), pl.BlockSpec((1, LANES), const)],
        out_specs=[pl.BlockSpec((nb, CH, DNW), blk), saved, saved],
        out_shape=[SDS((bl, seq, DNW), F32), SDS((bl // nb * nc, ng, DH, DH), F32), SDS((bl // nb * nc, ng, DH, DH), F32)],
        scratch_shapes=[pltpu.VMEM((ng, DH, DH), F32)], name="deltanet_fwd",
        compiler_params=_cp(2))(qkv, p_small, alp, dtp)


def deltanet_bwd(qkv, p_small, alp, dtp, sprev, tinv, d_o, nb, exchange=None):
    bl, seq, _ = qkv.shape
    nc = seq // CH
    ng = nb * NH
    extra = [] if exchange is None else [exchange]

    def body(*refs):
        qkv_ref, small_ref, alp_ref, dtp_ref, sprev_ref, tinv_ref, do_ref = refs[:7]
        dqkv_ref, dsmall_ref, dalp_ref, ddtp_ref = refs[7 + len(extra):11 + len(extra)]
        ds_scr = refs[11 + 2 * len(extra)]
        bb, r = pl.program_id(0), pl.program_id(1)
        if extra:
            start, finish = _exchange_phases(refs[7], refs[12], *refs[14:17])
            pl.when((bb == 0) & (r == 0))(start)

        @pl.when((bb == 0) & (r == 0))
        def _():
            dalp_ref[...] = jnp.zeros_like(dalp_ref)
            ddtp_ref[...] = jnp.zeros_like(ddtp_ref)

        @pl.when(r == 0)
        def _():
            ds_scr[...] = jnp.zeros_like(ds_scr)

        gates, gate_vjps = [], []
        for b in range(nb):
            out, gvjp = jax.vjp(gate_fn, small_ref[b], alp_ref[...], dtp_ref[...])
            gates.append(out)
            gate_vjps.append(gvjp)
        t_saved = tinv_ref[...]
        _, vjp = jax.vjp(lambda *args: dn_chunk(*args, t_saved)[:2], *_qkv_stacks(qkv_ref, nb), *_gate_stacks(gates, nb),
                         sprev_ref[...])
        d_out = jnp.stack([do_ref[b, :, hd * DH:(hd + 1) * DH] for b in range(nb) for hd in range(NH)])
        grads = vjp((d_out, ds_scr[...]))
        ds_scr[...] = grads[6]
        lane = _iota2((CH, LANES), 1)
        rowi = _iota2((LANES, CH), 0)
        for b in range(nb):
            d_beta = jnp.zeros((CH, LANES), F32)
            d_gc = jnp.zeros((CH, LANES), F32)
            d_gct = jnp.zeros((LANES, CH), F32)
            for hd in range(NH):
                i = b * NH + hd
                for part in range(3):
                    dqkv_ref[b, :, _dn_cols(part, hd)] = grads[part][i]
                d_beta = d_beta + jnp.where(lane == hd, grads[3][i], 0.0)
                d_gc = d_gc + jnp.where(lane == NH + hd, grads[4][i], 0.0)
                d_gct = d_gct + jnp.where(rowi == NH + hd, grads[5][i], 0.0)
            d_small, d_alp, d_dtp = gate_vjps[b]((d_beta, d_gc, d_gct))
            dsmall_ref[b] = d_small.astype(BF16)
            dalp_ref[...] += d_alp
            ddtp_ref[...] += d_dtp
        if extra:
            pl.when((bb == bl // nb - 1) & (r == nc - 1))(finish)

    blk = lambda bb, r: (bb, nc - 1 - r, 0)
    const = lambda bb, r: (0, 0)
    saved = pl.BlockSpec((None, ng, DH, DH), lambda bb, r: (bb * nc + nc - 1 - r, 0, 0, 0))
    return pl.pallas_call(
        body, grid=(bl // nb, nc),
        in_specs=[pl.BlockSpec((nb, CH, 3 * DNW), blk), pl.BlockSpec((nb, CH, LANES), blk), pl.BlockSpec((1, LANES), const),
                  pl.BlockSpec((1, LANES), const), saved, saved, pl.BlockSpec((nb, CH, DNW), blk)] + [HBM_SPEC] * len(extra),
        out_specs=[pl.BlockSpec((nb, CH, 3 * DNW), blk), pl.BlockSpec((nb, CH, LANES), blk), pl.BlockSpec((1, LANES), const),
                   pl.BlockSpec((1, LANES), const)] + [HBM_SPEC] * len(extra),
        out_shape=[SDS((bl, seq, 3 * DNW), F32), SDS((bl, seq, LANES), BF16), SDS((1, LANES), F32), SDS((1, LANES), F32)]
        + [SDS(x.shape, x.dtype) for x in extra],
        scratch_shapes=[pltpu.VMEM((ng, DH, DH), F32)] + (_comm_scratch() if extra else []), name="deltanet_bwd",
        compiler_params=_cp(2))(qkv, p_small, alp, dtp, sprev, tinv, d_o, *extra)


def _s5_table_specs():
    tab3 = pl.BlockSpec((None, LANES, 512), lambda gb, n: (gb, 0, 0))
    tab2 = pl.BlockSpec((CH, 512), lambda gb, n: (0, gb))
    return [tab3] * 4 + [tab2] * 6 + [pl.BlockSpec((1, LANES), lambda gb, n: (0, gb))]


def s5_fwd(u, tables, dsk):
    bl, seq, _ = u.shape
    nc = seq // CH

    def body(u_ref, *rest):
        tabs, (y_ref, xs_ref, xr_scr, xi_scr) = rest[:11], rest[11:]

        @pl.when(pl.program_id(1) == 0)
        def _():
            xr_scr[...] = jnp.zeros_like(xr_scr)
            xi_scr[...] = jnp.zeros_like(xi_scr)

        xp_re, xp_im = xr_scr[...], xi_scr[...]
        xs_ref[0:bl] = xp_re
        xs_ref[bl:2 * bl] = xp_im
        y, xn_re, xn_im = s5_chunk(u_ref[...], xp_re, xp_im, *[t[...] for t in tabs])
        y_ref[...] = y
        xr_scr[...] = xn_re
        xi_scr[...] = xn_im

    blk = lambda gb, n: (0, n, gb)
    return pl.pallas_call(
        body, grid=(GB, nc), in_specs=[pl.BlockSpec((bl, CH, LANES), blk)] + _s5_table_specs(),
        out_specs=[pl.BlockSpec((bl, CH, LANES), blk),
                   pl.BlockSpec((None, 2 * bl, 1, 512), lambda gb, n: (gb * nc + n, 0, 0, 0))],
        out_shape=[SDS((bl, seq, S5W), F32), SDS((GB * nc, 2 * bl, 1, 512), F32)],
        scratch_shapes=[pltpu.VMEM((bl, 1, 512), F32), pltpu.VMEM((bl, 1, 512), F32)], name="s5_fwd",
        compiler_params=_cp(2))(u, *tables, dsk)


def s5_bwd(u, tables, dsk, xs, dy):
    bl, seq, _ = u.shape
    nc = seq // CH

    def body(u_ref, *rest):
        tabs, xs_ref, dy_ref = rest[:11], rest[11], rest[12]
        du_ref, dtabs, dxr_scr, dxi_scr = rest[13], rest[14:25], rest[25], rest[26]
        r = pl.program_id(1)

        @pl.when(r == 0)
        def _():
            for t in dtabs:
                t[...] = jnp.zeros_like(t)
            dxr_scr[...] = jnp.zeros_like(dxr_scr)
            dxi_scr[...] = jnp.zeros_like(dxi_scr)

        _, vjp = jax.vjp(s5_chunk, u_ref[...], xs_ref[0:bl], xs_ref[bl:2 * bl], *[t[...] for t in tabs])
        grads = vjp((dy_ref[...], dxr_scr[...], dxi_scr[...]))
        du_ref[...] = grads[0].astype(BF16)
        dxr_scr[...] = grads[1]
        dxi_scr[...] = grads[2]
        for t, g in zip(dtabs, grads[3:]):
            t[...] += g

    blk = lambda gb, r: (0, nc - 1 - r, gb)
    tab_shapes = [SDS(t.shape, F32) for t in tables] + [SDS(dsk.shape, F32)]
    return pl.pallas_call(
        body, grid=(GB, nc),
        in_specs=[pl.BlockSpec((bl, CH, LANES), blk)] + _s5_table_specs()
        + [pl.BlockSpec((None, 2 * bl, 1, 512), lambda gb, r: (gb * nc + nc - 1 - r, 0, 0, 0)), pl.BlockSpec((bl, CH, LANES), blk)],
        out_specs=[pl.BlockSpec((bl, CH, LANES), blk)] + _s5_table_specs(),
        out_shape=[SDS((bl, seq, S5W), BF16)] + tab_shapes,
        scratch_shapes=[pltpu.VMEM((bl, 1, 512), F32), pltpu.VMEM((bl, 1, 512), F32)], name="s5_bwd",
        compiler_params=_cp(2))(u, *tables, dsk, xs, dy)


def s5_tables_fwd(params):
    shapes = [SDS((GB, LANES, 512), F32)] * 4 + [SDS((CH, S5N), F32)] * 6

    def body(*refs):
        for r, t in zip(refs[7:], s5_tables(*[p[...] for p in refs[:7]])):
            r[...] = t

    return pl.pallas_call(body, out_shape=shapes, name="s5_tables_fwd", compiler_params=_cp())(*params)


def s5_tables_bwd(params, dtables):
    def body(*refs):
        _, vjp = jax.vjp(s5_tables, *[p[...] for p in refs[:7]])
        for r, g in zip(refs[17:], vjp(tuple(t[...] for t in refs[7:17]))):
            r[...] = g

    return pl.pallas_call(body, out_shape=[SDS(p.shape, F32) for p in params], name="s5_tables_bwd",
                          compiler_params=_cp())(*params, *dtables)


def ada_fwd(c_all, w_loc, b_loc):
    def body(c_ref, w_ref, b_ref, o_ref):
        o_ref[...] = _dot(_silu(c_ref[...]), w_ref[...]) + b_ref[...]

    return pl.pallas_call(body, out_shape=SDS((c_all.shape[0], w_loc.shape[1]), F32), name="ada_fwd",
                          compiler_params=_cp())(c_all, w_loc, b_loc)


def ada_bwd(c_all, dmod_mine, dmod_all):
    def body(c_ref, dm_ref, da_ref, gw_ref, gb_ref):
        gw_ref[...] = _dot_tn(_silu(c_ref[...]), dm_ref[...])
        gb_ref[...] = jnp.sum(da_ref[...], axis=0, keepdims=True)

    return pl.pallas_call(body, out_shape=[SDS((D, dmod_mine.shape[1]), F32), SDS((1, dmod_all.shape[1]), F32)],
                          name="ada_bwd", compiler_params=_cp())(c_all, dmod_mine, dmod_all)


def loss_head(h, tgt, g, seq):
    t_rows = h.shape[0]
    tm = _pick(seq, (256, 128, 64))

    def body(h_ref, t_ref, g_ref, dh_ref, dg_ref, loss_ref):
        i = pl.program_id(0)
        y, vjp = jax.vjp(lambda hh, gg: hh * lax.rsqrt(jnp.mean(hh * hh, axis=-1, keepdims=True) + EPS) * gg,
                         h_ref[...], g_ref[...])
        e = y - t_ref[...]
        dh, dg = vjp(e * (1.0 / D))
        part = jnp.sum(jnp.sum(e * e, axis=1, keepdims=True), axis=0, keepdims=True) * (0.5 / D) + jnp.zeros((1, LANES), F32)
        dh_ref[...] = dh

        @pl.when(i == 0)
        def _():
            dg_ref[...] = dg
            loss_ref[...] = part

        @pl.when(i != 0)
        def _():
            dg_ref[...] += dg
            loss_ref[...] += part

    row = lambda i: (i, 0)
    const = lambda i: (0, 0)
    return pl.pallas_call(
        body, grid=(t_rows // tm,),
        in_specs=[pl.BlockSpec((tm, D), row), pl.BlockSpec((tm, D), row), pl.BlockSpec((1, D), const)],
        out_specs=[pl.BlockSpec((tm, D), row), pl.BlockSpec((1, D), const), pl.BlockSpec((1, LANES), const)],
        out_shape=[SDS((t_rows, D), F32), SDS((1, D), F32), SDS((1, LANES), F32)], name="loss_head",
        compiler_params=_cp(1))(h, tgt, g)


def adamw(name, parts, w, m, v):
    k_parts, rows, cols = parts.shape
    tr = _pick(rows, (256, 128, 64, 32, 16, 8))

    def body(p_ref, w_ref, m_ref, v_ref, g_ref, d_ref, mo_ref, vo_ref):
        g = p_ref[0].astype(F32)
        for k in range(1, k_parts):
            g = g + p_ref[k].astype(F32)
        _adam_store(g, w_ref, m_ref, v_ref, g_ref, d_ref, mo_ref, vo_ref)

    blk = pl.BlockSpec((tr, cols), lambda i: (i, 0))
    return pl.pallas_call(
        body, grid=(rows // tr,), in_specs=[pl.BlockSpec((k_parts, tr, cols), lambda i: (0, i, 0)), blk, blk, blk],
        out_specs=[blk] * 4, out_shape=[SDS((rows, cols), F32)] * 4, name=name, compiler_params=_cp(1))(parts, w, m, v)


def _adam_store(g, w_ref, m_ref, v_ref, g_ref, d_ref, mo_ref, vo_ref):
    m_new = ADAM_B1 * m_ref[...] + (1.0 - ADAM_B1) * g
    v_new = ADAM_B2 * v_ref[...] + (1.0 - ADAM_B2) * (g * g)
    m_hat = m_new / (1.0 - ADAM_B1 ** ADAM_STEP)
    v_hat = v_new / (1.0 - ADAM_B2 ** ADAM_STEP)
    g_ref[...] = g
    d_ref[...] = -ADAM_LR * (m_hat / (jnp.sqrt(v_hat) + ADAM_EPS) + ADAM_WD * w_ref[...])
    mo_ref[...] = m_new
    vo_ref[...] = v_new


def adamw_t(name, parts, w, m, v):
    k_parts, r, c = parts.shape
    tc = _pick(c, (256, 128))

    def body(p_ref, w_ref, m_ref, v_ref, g_ref, d_ref, mo_ref, vo_ref):
        gt = p_ref[0].astype(F32)
        for k in range(1, k_parts):
            gt = gt + p_ref[k].astype(F32)
        _adam_store(gt.T, w_ref, m_ref, v_ref, g_ref, d_ref, mo_ref, vo_ref)

    blk = pl.BlockSpec((tc, r), lambda j: (j, 0))
    return pl.pallas_call(
        body, grid=(c // tc,), in_specs=[pl.BlockSpec((k_parts, r, tc), lambda j: (0, 0, j)), blk, blk, blk],
        out_specs=[blk] * 4, out_shape=[SDS((c, r), F32)] * 4, name=name, compiler_params=_cp(1))(parts, w, m, v)


def _comm_scratch():
    return [pltpu.SemaphoreType.DMA((7,)), pltpu.SemaphoreType.DMA((7,)), pltpu.SemaphoreType.DMA]


HBM_SPEC = pl.BlockSpec(memory_space=pl.ANY)


def _gather_phases(x_ref, out_ref, send_sems, recv_sems, local_sem):
    mx, my, mc = lax.axis_index("x"), lax.axis_index("y"), lax.axis_index("c")
    me, sibling = (mx, my, mc), (mx, my, 1 - mc)
    chips = [(1 - mx, my), (mx, 1 - my), (1 - mx, 1 - my)]

    def slot(px, py, pc):
        return out_ref.at[4 * px + 2 * py + pc]

    def copy(k, block, to, src=None):
        return pltpu.make_async_remote_copy(
            src_ref=slot(*block) if src is None else src, dst_ref=slot(*block), send_sem=send_sems.at[k],
            recv_sem=recv_sems.at[k], device_id=to, device_id_type=pl.DeviceIdType.MESH)

    def first():
        return [copy(0, me, sibling, src=x_ref)] + [copy(1 + j, me, (*chip, mc), src=x_ref) for j, chip in enumerate(chips)]

    def passed():
        return [copy(4 + j, (*chip, mc), sibling) for j, chip in enumerate(chips)]

    def start():
        pltpu.make_async_copy(x_ref, slot(*me), local_sem).start()
        for cp in first():
            cp.start()

    def forward():
        for j, chip in enumerate(chips):
            copy(1 + j, (*chip, mc), me).wait_recv()
            passed()[j].start()

    def finish():
        copy(0, sibling, me).wait_recv()
        for j, chip in enumerate(chips):
            copy(4 + j, (*chip, 1 - mc), me).wait_recv()
        for cp in first() + passed():
            cp.wait_send()
        pltpu.make_async_copy(x_ref, slot(*me), local_sem).wait()

    return start, forward, finish


def _exchange_phases(x_ref, out_ref, send_sems, recv_sems, local_sem):
    mx, my, mc = lax.axis_index("x"), lax.axis_index("y"), lax.axis_index("c")
    me = 4 * mx + 2 * my + mc

    def peer(k):
        return mx ^ (k >> 2), my ^ ((k >> 1) & 1), mc ^ (k & 1)

    def sends():
        out = []
        for k in range(1, NDEV):
            px, py, pc = peer(k)
            out.append(pltpu.make_async_remote_copy(
                src_ref=x_ref.at[4 * px + 2 * py + pc], dst_ref=out_ref.at[me], send_sem=send_sems.at[k - 1],
                recv_sem=recv_sems.at[k - 1], device_id=(px, py, pc), device_id_type=pl.DeviceIdType.MESH))
        return out

    def start():
        pltpu.make_async_copy(x_ref.at[me], out_ref.at[me], local_sem).start()
        for cp in sends():
            cp.start()

    def finish():
        for k in range(1, NDEV):
            px, py, pc = peer(k)
            pltpu.make_async_remote_copy(
                src_ref=x_ref.at[me], dst_ref=out_ref.at[4 * px + 2 * py + pc], send_sem=send_sems.at[k - 1],
                recv_sem=recv_sems.at[k - 1], device_id=(px, py, pc), device_id_type=pl.DeviceIdType.MESH).wait_recv()
        for cp in sends():
            cp.wait_send()
        pltpu.make_async_copy(x_ref.at[me], out_ref.at[me], local_sem).wait()

    return start, finish


def all_gather(name, x):
    def body(x_ref, out_ref, send_sems, recv_sems, local_sem):
        for phase in _gather_phases(x_ref, out_ref, send_sems, recv_sems, local_sem):
            phase()

    return pl.pallas_call(body, out_shape=SDS((NDEV,) + x.shape, x.dtype), in_specs=[HBM_SPEC], out_specs=HBM_SPEC,
                          scratch_shapes=_comm_scratch(), name=name)(x)


def all_to_all(name, x):
    def body(x_ref, out_ref, send_sems, recv_sems, local_sem):
        for phase in _exchange_phases(x_ref, out_ref, send_sems, recv_sems, local_sem):
            phase()

    return pl.pallas_call(body, out_shape=SDS(x.shape, x.dtype), in_specs=[HBM_SPEC], out_specs=HBM_SPEC,
                          scratch_shapes=_comm_scratch(), name=name)(x)


def _pack(arrs, dtype, row_mult=8):
    segs = []
    for a in arrs:
        flat = a.reshape(-1).astype(dtype)
        segs.append(jnp.pad(flat, (0, (-flat.shape[0]) % ROW)))
    flat = jnp.concatenate(segs)
    flat = jnp.pad(flat, (0, (-flat.shape[0]) % (ROW * row_mult)))
    return flat.reshape(-1, ROW)


def _unpack(buf, shapes):
    flat = buf.reshape(-1)
    out, off = [], 0
    for s in shapes:
        n = math.prod(s)
        out.append(flat[off:off + n].reshape(s))
        off += n + (-n) % ROW
    return out


def _pack_rows(arrs, axis):
    padded = []
    for t in arrs:
        pad = [(0, 0)] * t.ndim
        pad[axis] = (0, _tile_rows(t.shape[axis]) - t.shape[axis])
        padded.append(jnp.pad(t, pad))
    return jnp.concatenate(padded, axis=axis)


def _tile_rows(r):
    return r + (-r) % 16


def _unpack8(buf, shapes):
    flat = buf.reshape(NDEV, -1)
    out, off = [], 0
    for s in shapes:
        n = math.prod(s)
        out.append(flat[:, off:off + n].reshape((NDEV,) + tuple(s)))
        off += n + (-n) % ROW
    return out


def kernel(x, c, w_ada, b_ada, g_ffn1, w1_ffn1, w3_ffn1, w2_ffn1, g_mix, w_in, conv_qkv, a_log, dt_bias, g_onorm, lam_re, lam_im, log_step, b_re, b_im, c_re, c_im, d_skip, w_glu, b_glu, w_proj_a, w_proj_b, w_out, g_ffn2, w1_ffn2, w3_ffn2, w2_ffn2, g_final, loss_target, m_w_ada, m_b_ada, m_g_ffn1, m_w1_ffn1, m_w3_ffn1, m_w2_ffn1, m_g_mix, m_w_in, m_conv_qkv, m_a_log, m_dt_bias, m_g_onorm, m_lam_re, m_lam_im, m_log_step, m_b_re, m_b_im, m_c_re, m_c_im, m_d_skip, m_w_glu, m_b_glu, m_w_proj_a, m_w_proj_b, m_w_out, m_g_ffn2, m_w1_ffn2, m_w3_ffn2, m_w2_ffn2, m_g_final, v_w_ada, v_b_ada, v_g_ffn1, v_w1_ffn1, v_w3_ffn1, v_w2_ffn1, v_g_mix, v_w_in, v_conv_qkv, v_a_log, v_dt_bias, v_g_onorm, v_lam_re, v_lam_im, v_log_step, v_b_re, v_b_im, v_c_re, v_c_im, v_d_skip, v_w_glu, v_b_glu, v_w_proj_a, v_w_proj_b, v_w_out, v_g_ffn2, v_w1_ffn2, v_w3_ffn2, v_w2_ffn2, v_g_final):
    a = dict(locals())
    bl, seq, _ = x.shape
    t_rows = bl * seq
    nc = seq // CH
    me = 4 * lax.axis_index("x") + 2 * lax.axis_index("y") + lax.axis_index("c")
    tm_ew = _pick(seq, (256, 128, 64))
    tm_tail = _pick(seq, (256, 128, 64))

    sm = all_gather("gather_small", _pack([c, conv_qkv[0]], F32))
    c_loc, conv_loc = _unpack8(sm, [c.shape, conv_qkv.shape[1:]])
    c_all = c_loc.reshape(NDEV * bl, D)
    conv_full = conv_loc.transpose(1, 0, 2).reshape(CONVW, 3 * DNW)
    loc = {n: (a[n][0].T if n in COL_SHARDED else a[n][0]) for n in RS_WEIGHTS}
    wfull, gw, res = {}, {}, {}

    def pack_local(names):
        return _pack_rows([loc[n].astype(BF16).reshape(-1, ROW) for n in names], 0)

    def unpack_full(buf, names):
        r0 = 0
        for n in names:
            r = loc[n].size // ROW
            wfull[n] = buf[:, r0:r0 + r, :].reshape(-1, loc[n].shape[1])
            r0 += _tile_rows(r)

    def pack_grads(names):
        return _pack_rows([gw[n].astype(BF16).reshape(NDEV, -1, ROW) for n in names], 1)

    def update(buf, names):
        r0 = 0
        for n in names:
            r = loc[n].size // ROW
            parts = buf[:, r0:r0 + r, :].reshape((NDEV,) + loc[n].shape)
            r0 += _tile_rows(r)
            step = adamw_t if n in COL_SHARDED else adamw
            out = step("adamw_" + n, parts, a[n][0], a["m_" + n][0], a["v_" + n][0])
            for kind, t in zip(("grad", "delta", "new_m", "new_v"), out):
                res[kind + "_" + n] = t[None]

    unpack_full(all_gather("gather_ffn1", pack_local(G_FFN1)), G_FFN1)

    n_ada = w_ada.shape[2]
    mod_part = ada_fwd(c_all, w_ada[0], lax.dynamic_slice(b_ada, (0, me * n_ada), (1, n_ada)))
    mod_all = all_gather("gather_mod", mod_part).transpose(1, 0, 2).reshape(NDEV * bl, 9 * D)
    mod = lax.dynamic_slice(mod_all, (me * bl, 0), (bl, 9 * D)).reshape(bl, 9, D)
    mods = [mod[:, k:k + 1, :] for k in range(9)]

    h0 = x.reshape(t_rows, D)
    h1, f1, u1, pa1, pb1, wg_rest = ffn_fwd("ffn1_fwd", h0, mod[:, 0:3, :], g_ffn1, wfull['w1_ffn1'], wfull['w3_ffn1'],
                                  wfull['w2_ffn1'], seq, gather=pack_local(G_MIX + G_FFN2))
    unpack_full(wg_rest, G_MIX + G_FFN2)
    win = wfull['w_in']
    o_small, o_s5, o_gate = 4 * DNW, 4 * DNW + 2 * NH, 4 * DNW + 2 * NH + S5W
    w_dn, w_small = win[:o_small], jnp.pad(win[o_small:o_s5], ((0, LANES - 2 * NH), (0, 0)))
    w_s5, w_gate = win[o_s5:o_gate], win[o_gate:]
    (u2,) = ew_call("mix_norm", fn_normmod, [h1], [mods[3], mods[4]], [g_mix], [(D, BF16)], tm_ew, seq)
    p_dn = mm("proj_dn", [(u2, w_dn)], True, F32)
    p_small = mm("proj_small", [(u2, w_small)], True, F32)
    p_s5 = mm("proj_s5", [(u2, w_s5)], True, F32)
    p_gate = mm("proj_gate", [(u2, w_gate)], True, F32)

    conv8 = jnp.pad(conv_full, ((0, 8 - CONVW), (0, 0)))
    alp = jnp.pad(a_log, ((0, 0), (NH, LANES - 2 * NH)))
    dtp = jnp.pad(dt_bias, ((0, 0), (NH, LANES - 2 * NH)))
    nb_dn = DN_ROWS if bl % DN_ROWS == 0 else 1
    p_dn3, p_small3 = p_dn.reshape(bl, seq, 4 * DNW), p_small.reshape(bl, seq, LANES)
    qkv3 = dn_prep_fwd(p_dn3, conv8)
    o_pre3, sprev, tinv = deltanet_fwd(qkv3, p_small3, alp, dtp, nb_dn)
    o_pre = o_pre3.reshape(t_rows, DNW)
    z_raw = p_dn[:, 3 * DNW:]

    s5_params = [lam_re.reshape(1, S5N), lam_im.reshape(1, S5N), log_step,
                 b_re[0].transpose(2, 0, 1).reshape(S5C, S5N), b_im[0].transpose(2, 0, 1).reshape(S5C, S5N),
                 c_re[0].transpose(1, 0, 2).reshape(S5C, S5N), c_im[0].transpose(1, 0, 2).reshape(S5C, S5N)]
    tables = s5_tables_fwd(s5_params)
    p_s53 = p_s5.reshape(bl, seq, S5W)
    y_s53, xs = s5_fwd(p_s53, tables, d_skip)
    y_s5 = y_s53.reshape(t_rows, S5W)
    tail_in = [o_pre, z_raw, y_s5, p_gate]
    tail_w = [g_onorm, wfull['w_glu'], b_glu, wfull['w_proj_a'], wfull['w_proj_b']]
    (merged,) = ew_call("mix_tail", fn_mix_tail, tail_in, [], tail_w, [(D, BF16)], tm_tail, seq)
    mo = mm("proj_out", [(merged, wfull['w_out'])], False, F32)
    (h2,) = ew_call("mix_resid", lambda p, q, gt: (q + gt * p,), [mo, h1], [mods[5]], [], [(D, F32)], tm_ew, seq)
    h3, f3, u3, pa3, pb3 = ffn_fwd("ffn2_fwd", h2, mod[:, 6:9, :], g_ffn2, wfull['w1_ffn2'], wfull['w3_ffn2'], wfull['w2_ffn2'], seq)

    dh3, dg_final, loss_part = loss_head(h3, loss_target.reshape(t_rows, D), g_final.reshape(1, D), seq)
    loss = lax.psum(loss_part[0, 0], ("x", "y", "c"))

    dh2, a3, d1_3, d3_3, df3, dmod_c, dg_ffn2 = ffn_bwd("ffn2_bwd", dh3, h2, f3, pa3, pb3, mod[:, 6:9, :], g_ffn2, wfull['w1_ffn2'],
                                                   wfull['w3_ffn2'], wfull['w2_ffn2'], seq)
    gw['w1_ffn2'] = mm_tn("gw1_ffn2", d1_3, u3)
    gw['w3_ffn2'] = mm_tn("gw3_ffn2", d3_3, u3)
    gw['w2_ffn2'] = mm_tn("gw2_ffn2", a3, df3)

    (dmo,), (dgt2,), _ = ew_vjp_call("mix_resid_bwd", fn_resid, [mo], [mods[5]], [], [dh2], [(0, BF16)], tm_ew, seq)
    gw['w_out'] = mm_tn("gw_out", merged, dmo)
    d_merged = mm("d_merged", [(dmo, wfull['w_out'])], True, F32)
    (d_opre, d_z, d_ys5, d_gate), _, tail_gw = ew_vjp_call(
        "mix_tail_bwd", fn_mix_tail, tail_in, [], tail_w, [d_merged], [(0, F32), (1, F32), (2, F32), (3, BF16)], tm_tail, seq)
    dg_onorm, gw['w_glu'], dg_bglu, gw['w_proj_a'], gw['w_proj_b'] = tail_gw
    d_qkv3, d_psmall3, d_alp, d_dtp, rs_ffn2 = deltanet_bwd(
        qkv3, p_small3, alp, dtp, sprev, tinv, d_opre.reshape(bl, seq, DNW), nb_dn, exchange=pack_grads(G_FFN2))
    d_pdn3, d_conv8 = dn_prep_bwd(p_dn3, conv8, d_qkv3, d_z.reshape(bl, seq, DNW))
    d_pdn, d_psmall = d_pdn3.reshape(t_rows, 4 * DNW), d_psmall3.reshape(t_rows, LANES)

    s5_out = s5_bwd(p_s53, tables, d_skip, xs, d_ys5.reshape(bl, seq, S5W))
    d_ps5, d_tables, dg_dskip = s5_out[0].reshape(t_rows, S5W), s5_out[1:11], s5_out[11]
    d_s5p = s5_tables_bwd(s5_params, d_tables)

    d_pdn_b, d_psm_b, d_ps5_b = d_pdn, d_psmall, d_ps5
    gw['w_in'] = jnp.concatenate([mm_tn("gw_dn", d_pdn_b, u2), mm_tn("gw_small", d_psm_b, u2)[:2 * NH],
                                  mm_tn("gw_s5", d_ps5_b, u2), mm_tn("gw_gate", d_gate, u2)], axis=0)
    du2 = mm("d_u2", [(d_pdn_b, w_dn), (d_psm_b, w_small), (d_ps5_b, w_s5), (d_gate, w_gate)], False, F32)
    (dh1,), (dsh2, dsc2), (dg_mix,) = ew_vjp_call("mix_norm_bwd", fn_normmod, [h1], [mods[3], mods[4]], [g_mix], [du2],
                                                  [(0, F32)], tm_ew, seq, addend=dh2)

    dh0, a1, d1_1, d3_1, df1, dmod_a, dg_ffn1, rs_mix = ffn_bwd(
        "ffn1_bwd", dh1, h0, f1, pa1, pb1, mod[:, 0:3, :], g_ffn1, wfull['w1_ffn1'], wfull['w3_ffn1'], wfull['w2_ffn1'], seq,
        exchange=pack_grads(G_MIX))
    gw['w1_ffn1'] = mm_tn("gw1_ffn1", d1_1, u1)
    gw['w3_ffn1'], rs_w1 = mm_tn("gw3_ffn1", d3_1, u1, exchange=pack_grads(['w1_ffn1']))
    gw['w2_ffn1'], rs_w3 = mm_tn("gw2_ffn1", a1, df1, exchange=pack_grads(['w3_ffn1']))

    update(rs_ffn2, G_FFN2)
    update(rs_mix, G_MIX)
    update(rs_w1, ['w1_ffn1'])
    update(rs_w3, ['w3_ffn1'])
    update(all_to_all("scatter_w2_ffn1", pack_grads(['w2_ffn1'])), ['w2_ffn1'])

    dmod_mine = jnp.concatenate([dmod_a, dsh2, dsc2, dgt2, dmod_c], axis=1).reshape(bl, 9 * D)
    small_grads = {
        'g_ffn1': dg_ffn1, 'g_mix': dg_mix, 'a_log': d_alp[:, NH:2 * NH], 'dt_bias': d_dtp[:, NH:2 * NH],
        'g_onorm': dg_onorm, 'lam_re': d_s5p[0].reshape(1, S5G, S5P), 'lam_im': d_s5p[1].reshape(1, S5G, S5P),
        'log_step': d_s5p[2],
        'b_re': d_s5p[3].reshape(S5C, S5G, S5P).transpose(1, 2, 0)[None],
        'b_im': d_s5p[4].reshape(S5C, S5G, S5P).transpose(1, 2, 0)[None],
        'c_re': d_s5p[5].reshape(S5C, S5G, S5P).transpose(1, 0, 2)[None],
        'c_im': d_s5p[6].reshape(S5C, S5G, S5P).transpose(1, 0, 2)[None],
        'd_skip': dg_dskip, 'b_glu': dg_bglu, 'g_ffn2': dg_ffn2, 'g_final': dg_final.reshape(D)}
    small_shapes = [a[n].shape for n in SMALL]
    small_pack = _pack([small_grads[n] for n in SMALL], F32)
    n_small = small_pack.shape[0]
    sg = all_gather("gather_small_grads",
                    jnp.concatenate([small_pack, _pack([dmod_mine, d_conv8[:CONVW]], F32)], axis=0))
    pieces = _unpack8(sg[:, n_small:, :], [dmod_mine.shape, (CONVW, 3 * DNW)])
    dmod_all = pieces[0].reshape(NDEV * bl, 9 * D)
    g_wada, g_bada = ada_bwd(c_all, lax.dynamic_slice(dmod_all, (0, me * n_ada), (NDEV * bl, n_ada)), dmod_all)

    n_conv = conv_qkv.shape[2]
    conv_parts = lax.dynamic_slice(pieces[1], (0, 0, me * n_conv), (NDEV, CONVW, n_conv))
    conv_parts = jnp.pad(conv_parts.reshape(NDEV, 1, -1), ((0, 0), (0, 7), (0, 0)))
    pad8 = lambda t: jnp.pad(t.reshape(1, -1), ((0, 7), (0, 0)))
    conv_res = adamw("adamw_conv", conv_parts, pad8(conv_qkv), pad8(m_conv_qkv), pad8(v_conv_qkv))
    for kind, buf in zip(("grad", "delta", "new_m", "new_v"), conv_res):
        res[kind + "_conv_qkv"] = buf[0].reshape(conv_qkv.shape)

    small_res = adamw("adamw_small", sg[:, :n_small, :], *[_pack([a[p + n] for n in SMALL], F32) for p in ("", "m_", "v_")])
    for kind, buf in zip(("grad", "delta", "new_m", "new_v"), small_res):
        for n, t in zip(SMALL, _unpack(buf, small_shapes)):
            res[kind + "_" + n] = t

    for n, g in (("w_ada", g_wada), ("b_ada", g_bada)):
        shp = a[n].shape
        r2 = lambda t: t.reshape(-1, shp[-1]) if n == "w_ada" else pad8(t)
        out = adamw("adamw_" + n, r2(g)[None], r2(a[n]), r2(a["m_" + n]), r2(a["v_" + n]))
        for kind, buf in zip(("grad", "delta", "new_m", "new_v"), out):
            res[kind + "_" + n] = (buf if n == "w_ada" else buf[0:1]).reshape(shp)

    outs = [loss, dh0.reshape(x.shape)]
    for kind in ("grad", "delta", "new_m", "new_v"):
        outs += [res[kind + "_" + n] for n in WEIGHTS]
    return tuple(outs)
```

```python
import math

import jax
import jax.numpy as jnp
from jax import lax
from jax.experimental import pallas as pl
from jax.experimental.pallas import tpu as pltpu

F32 = jnp.float32
BF16 = jnp.bfloat16
HI = lax.Precision.HIGHEST
H3 = lax.Precision.HIGH
SDS = jax.ShapeDtypeStruct

D = 1024
FF = 2816
FFN_TF = FF
FFN_FWD_TM = 256
FFN_BWD_TM = 256
NH = 8
DH = 64
DNW = NH * DH
CONVW = 4
CH = 64
DN_ROWS = 2
S5W = 512
S5G = 32
S5P = 64
S5C = 16
S5N = S5G * S5P
GB = 4
NDEV = 8
EPS = 1e-6
LANES = 128
ROW = 1024
VMEM_LIMIT = 56 * 1024 * 1024

ADAM_LR, ADAM_B1, ADAM_B2, ADAM_EPS, ADAM_WD, ADAM_STEP = 0.001, 0.9, 0.999, 1e-08, 0.01, 10

WEIGHTS = ['w_ada', 'b_ada', 'g_ffn1', 'w1_ffn1', 'w3_ffn1', 'w2_ffn1', 'g_mix', 'w_in', 'conv_qkv', 'a_log',
           'dt_bias', 'g_onorm', 'lam_re', 'lam_im', 'log_step', 'b_re', 'b_im', 'c_re', 'c_im', 'd_skip', 'w_glu',
           'b_glu', 'w_proj_a', 'w_proj_b', 'w_out', 'g_ffn2', 'w1_ffn2', 'w3_ffn2', 'w2_ffn2', 'g_final']
RS_WEIGHTS = ['w1_ffn1', 'w3_ffn1', 'w2_ffn1', 'w_in', 'w_glu', 'w_proj_a', 'w_proj_b', 'w_out', 'w1_ffn2', 'w3_ffn2',
              'w2_ffn2']
COL_SHARDED = {'w1_ffn1', 'w3_ffn1', 'w_in', 'w_proj_a', 'w_proj_b', 'w1_ffn2', 'w3_ffn2'}
G_FFN1 = ['w1_ffn1', 'w3_ffn1', 'w2_ffn1']
G_MIX = ['w_in', 'w_glu', 'w_proj_a', 'w_proj_b', 'w_out']
G_FFN2 = ['w1_ffn2', 'w3_ffn2', 'w2_ffn2']
SMALL = ['g_ffn1', 'g_mix', 'a_log', 'dt_bias', 'g_onorm', 'lam_re', 'lam_im', 'log_step', 'b_re', 'b_im', 'c_re',
         'c_im', 'd_skip', 'b_glu', 'g_ffn2', 'g_final']


def _cp(n_grid=0):
    if n_grid:
        return pltpu.CompilerParams(vmem_limit_bytes=VMEM_LIMIT, dimension_semantics=("arbitrary",) * n_grid)
    return pltpu.CompilerParams(vmem_limit_bytes=VMEM_LIMIT)


def _dot(a, b):
    return jnp.dot(a.astype(BF16), b.astype(BF16), preferred_element_type=F32)


def _dot_nt(a, b):
    return lax.dot_general(a.astype(BF16), b.astype(BF16), (((1,), (1,)), ((), ())), preferred_element_type=F32)


def _dot_tn(a, b):
    return lax.dot_general(a.astype(BF16), b.astype(BF16), (((0,), (0,)), ((), ())), preferred_element_type=F32)


def _dot_hi(a, b):
    return jnp.dot(a, b, precision=HI, preferred_element_type=F32)


def _dot_h3(a, b):
    return jnp.dot(a, b, precision=H3, preferred_element_type=F32)


@jax.custom_vjp
def bdot(a, b):
    return _dot(a, b)


bdot.defvjp(lambda a, b: (_dot(a, b), (a, b)),
            lambda r, g: (_dot_nt(g, r[1]).astype(r[0].dtype), _dot_tn(r[0], g).astype(r[1].dtype)))


@jax.custom_vjp
def bdot_nt(a, b):
    return _dot_nt(a, b)


bdot_nt.defvjp(lambda a, b: (_dot_nt(a, b), (a, b)),
               lambda r, g: (_dot(g, r[1]).astype(r[0].dtype), _dot_tn(g, r[0]).astype(r[1].dtype)))


def _silu(x):
    return x * jax.nn.sigmoid(x)


def _iota2(shape, axis):
    return lax.broadcasted_iota(jnp.int32, shape, axis)


def normmod(h, g, sc, sh):
    y = h * lax.rsqrt(jnp.mean(h * h, axis=-1, keepdims=True) + EPS) * g
    return y * (1.0 + sc) + sh


def fn_merge(gate, ya, yb):
    return (jax.nn.sigmoid(gate[:, :D]) * ya + jax.nn.sigmoid(gate[:, D:]) * yb,)


def fn_glu(y, w, b):
    ge = jax.nn.gelu(y)
    return (ge * jax.nn.sigmoid(bdot(ge, w) + b),)


def fn_onorm(o, z, g_on):
    r = _iota2((DH, DNW), 0)
    c = _iota2((DH, DNW), 1)
    expand = (c % DH == r).astype(F32)
    r2 = _iota2((DNW, DNW), 0)
    c2 = _iota2((DNW, DNW), 1)
    avg = (r2 // DH == c2 // DH).astype(F32) * (1.0 / DH)
    ms = _dot_h3(o * o, avg)
    return (o * lax.rsqrt(ms + EPS) * _dot_hi(g_on, expand) * _silu(z),)


def fn_mix_tail(o_pre, z, y_s5, gate, g_on, w_glu, b_glu, wa_t, wb_t):
    (oa,) = fn_onorm(o_pre, z, g_on)
    (ob,) = fn_glu(y_s5, w_glu, b_glu)
    return fn_merge(gate, bdot_nt(oa, wa_t), bdot_nt(ob, wb_t))


def gate_fn(small, alp, dtp):
    beta = jax.nn.sigmoid(small)
    la = -jnp.exp(alp) * jax.nn.softplus(small + dtp)
    tri = (_iota2((CH, CH), 0) >= _iota2((CH, CH), 1)).astype(F32)
    gc = _dot_hi(tri, la)
    gct = lax.dot_general(la, tri, (((0,), (1,)), ((), ())), precision=HI, preferred_element_type=F32)
    return beta, gc, gct


def _bdg(a, b, ca, cb, hi):
    if not hi:
        a, b = a.astype(BF16), b.astype(BF16)
    return lax.dot_general(a, b, (((ca,), (cb,)), ((0,), (0,))), precision=H3 if hi else None,
                           preferred_element_type=F32)


def _batched_matmuls(hi):
    nn_ = lambda a, b: _bdg(a, b, 2, 1, hi)
    nt_ = lambda a, b: _bdg(a, b, 2, 2, hi)
    tn_ = lambda a, b: _bdg(a, b, 1, 1, hi)
    nn = jax.custom_vjp(nn_)
    nn.defvjp(lambda a, b: (nn_(a, b), (a, b)), lambda r, g: (nt_(g, r[1]), tn_(r[0], g)))
    nt = jax.custom_vjp(nt_)
    nt.defvjp(lambda a, b: (nt_(a, b), (a, b)), lambda r, g: (nn_(g, r[1]), tn_(g, r[0])))
    tn = jax.custom_vjp(tn_)
    tn.defvjp(lambda a, b: (tn_(a, b), (a, b)), lambda r, g: (nt_(r[1], g), nn_(r[0], g)))
    return nn, nt, tn


bnn, bnt, btn = _batched_matmuls(False)
hnn, hnt, htn = _batched_matmuls(True)


def _unit_lower_inverse(a):
    r = _iota2((1, CH, CH), 1)
    c = _iota2((1, CH, CH), 2)
    eye = (r == c).astype(F32)
    d = jnp.where(r // 8 == c // 8, a, 0.0)
    inv = eye - d
    p = d
    for _ in range(2):
        p = hnn(p, p)
        inv = inv + hnn(inv, p)
    for blk in (16, 32, 64):
        off = jnp.where((r // blk == c // blk) & (r // (blk // 2) != c // (blk // 2)), a, 0.0)
        inv = inv - hnn(hnn(inv, off), inv)
    return inv


@jax.custom_vjp
def _inverse_given(a, t):
    return t


_inverse_given.defvjp(lambda a, t: (t, t), lambda t, g: (-hnt(htn(t, g), t), jnp.zeros_like(t)))


def dn_prep(xc, w):
    t = xc.shape[0] - 8
    c = xc[5:5 + t] * w[0:1] + xc[6:6 + t] * w[1:2] + xc[7:7 + t] * w[2:3] + xc[8:8 + t] * w[3:4]
    act = _silu(c)
    q, k, v = act[:, :DNW], act[:, DNW:2 * DNW], act[:, 2 * DNW:]
    ones = (_iota2((DNW, DNW), 0) // DH == _iota2((DNW, DNW), 1) // DH).astype(F32)
    q = q * lax.rsqrt(_dot_h3(q * q, ones) + EPS) * (DH ** -0.5)
    k = k * lax.rsqrt(_dot_h3(k * k, ones) + EPS)
    return jnp.concatenate([q, k, v], axis=1)


def dn_chunk(q, k, v, b, g, gt, s_prev, t_saved=None):
    r = _iota2((1, CH, CH), 1)
    c = _iota2((1, CH, CH), 2)
    causal = r >= c
    dec = jnp.where(causal, jnp.exp(jnp.where(causal, g - gt, 0.0)), 0.0)
    kb = k * b
    qk = bnt(jnp.concatenate([q, kb], axis=1), k)
    attn = qk[:, :CH] * dec
    a = jnp.where(r > c, qk[:, CH:] * dec, 0.0)
    tinv = _unit_lower_inverse(a) if t_saved is None else _inverse_given(a, t_saved)
    eg = jnp.exp(g)
    uw = hnn(tinv, jnp.concatenate([v * b, kb * eg], axis=2))
    g_last = g[:, CH - 1:CH]
    ws = bnn(jnp.concatenate([uw[..., DH:], q * eg], axis=1), s_prev)
    v_new = uw[..., :DH] - ws[:, :CH]
    o = ws[:, CH:] + bnn(attn, v_new)
    s_new = s_prev * jnp.exp(g_last) + btn(k * jnp.exp(g_last - g), v_new)
    return o, s_new, tinv


def s5_chunk(u, xp_re, xp_im, bb_re, bb_im, cc_re, cc_im, p0r, p0i, p1r, p1i, pir, pii, dsk):
    nb = u.shape[0]
    u2 = u.reshape(nb * CH, LANES)
    bu_re = bdot(u2, bb_re).reshape(nb, CH, 512)
    bu_im = bdot(u2, bb_im).reshape(nb, CH, 512)
    xt_re = pir * bu_re - pii * bu_im
    xt_im = pir * bu_im + pii * bu_re
    tri = jnp.broadcast_to((_iota2((1, CH, CH), 1) >= _iota2((1, CH, CH), 2)).astype(F32), (nb, CH, CH))
    cs_re = hnn(tri, xt_re)
    cs_im = hnn(tri, xt_im)
    x_re = p0r * cs_re - p0i * cs_im + p1r * xp_re - p1i * xp_im
    x_im = p0r * cs_im + p0i * cs_re + p1r * xp_im + p1i * xp_re
    y = bdot_nt(x_re.reshape(nb * CH, 512), cc_re) - bdot_nt(x_im.reshape(nb * CH, 512), cc_im) + dsk * u2
    return y.reshape(nb, CH, LANES), x_re[:, CH - 1:CH], x_im[:, CH - 1:CH]


def s5_tables(lam_re, lam_im, log_step, bre, bim, cre, cim):
    expand = (_iota2((S5G, S5N), 1) // S5P == _iota2((S5G, S5N), 0)).astype(F32)
    step = _dot_hi(jnp.exp(log_step), expand)
    lre = jnp.minimum(lam_re, -1e-4)
    lr = lre * step
    ang = lam_im * step
    mag = jnp.exp(lr)
    lb_re = mag * jnp.cos(ang)
    lb_im = mag * jnp.sin(ang)
    den = lre * lre + lam_im * lam_im
    coef_re = ((lb_re - 1.0) * lre + lb_im * lam_im) / den
    coef_im = (lb_im * lre - (lb_re - 1.0) * lam_im) / den
    bb_re = coef_re * bre - coef_im * bim
    bb_im = coef_re * bim + coef_im * bre
    j = _iota2((CH, 1), 0).astype(F32)
    e0 = jnp.exp(j * lr)
    e1 = jnp.exp((j + 1.0) * lr)
    ei = jnp.exp(-j * lr)
    mask = (_iota2((LANES, 512), 0) // S5C == _iota2((LANES, 512), 1) // S5P).astype(F32)

    def blocks(t):
        return jnp.concatenate([(jnp.tile(t[:, gb * 512:(gb + 1) * 512], (LANES // S5C, 1)) * mask)[None]
                                for gb in range(GB)], axis=0)

    return (blocks(bb_re), blocks(bb_im), blocks(cre), blocks(cim),
            e0 * jnp.cos(j * ang), e0 * jnp.sin(j * ang),
            e1 * jnp.cos((j + 1.0) * ang), e1 * jnp.sin((j + 1.0) * ang),
            ei * jnp.cos(j * ang), -ei * jnp.sin(j * ang))


def _row_specs(tiled, batch, bcast, tm, tpb):
    specs = [pl.BlockSpec((tm, a.shape[1]), lambda i: (i, 0)) for a in tiled]
    specs += [pl.BlockSpec((None,) + a.shape[1:], lambda i: (i // tpb, 0, 0)) for a in batch]
    specs += [pl.BlockSpec(a.shape, lambda i, nd=a.ndim: (0,) * nd) for a in bcast]
    return specs


def ew_call(name, fn, tiled, batch, bcast, outs, tm, seq):
    t_rows = tiled[0].shape[0]
    n_in = len(tiled) + len(batch) + len(bcast)

    def body(*refs):
        vals = [r[...].astype(F32) for r in refs[:n_in]]
        for r, o in zip(refs[n_in:], fn(*vals)):
            r[...] = o.astype(r.dtype)

    return pl.pallas_call(
        body, grid=(t_rows // tm,), in_specs=_row_specs(tiled, batch, bcast, tm, seq // tm),
        out_specs=[pl.BlockSpec((tm, w), lambda i: (i, 0)) for w, _ in outs],
        out_shape=[SDS((t_rows, w), dt) for w, dt in outs], name=name, compiler_params=_cp(1))(*tiled, *batch, *bcast)


def ew_vjp_call(name, fn, tiled, batch, bcast, cts, want, tm, seq, addend=None):
    t_rows = tiled[0].shape[0]
    tpb = seq // tm
    n_t, n_b, n_c = len(tiled), len(batch), len(bcast)
    n_in = n_t + n_b + n_c
    extra = [] if addend is None else [addend]

    def body(*refs):
        i = pl.program_id(0)
        vals = [r[...].astype(F32) for r in refs[:n_in]]
        ctv = tuple(r[...].astype(F32) for r in refs[n_in:n_in + len(cts)])
        outs = refs[n_in + len(cts) + len(extra):]
        _, vjp = jax.vjp(fn, *vals)
        grads = vjp(ctv)
        for k, (r, (idx, _)) in enumerate(zip(outs[:len(want)], want)):
            g = grads[idx]
            if k == 0 and extra:
                g = g + refs[n_in + len(cts)][...]
            r[...] = g.astype(r.dtype)
        for k in range(n_b):
            r, g = outs[len(want) + k], grads[n_t + k]

            @pl.when(i % tpb == 0)
            def _(r=r, g=g):
                r[...] = g

            @pl.when(i % tpb != 0)
            def _(r=r, g=g):
                r[...] += g
        for k in range(n_c):
            r, g = outs[len(want) + n_b + k], grads[n_t + n_b + k]

            @pl.when(i == 0)
            def _(r=r, g=g):
                r[...] = g

            @pl.when(i != 0)
            def _(r=r, g=g):
                r[...] += g

    out_specs = [pl.BlockSpec((tm, tiled[idx].shape[1]), lambda i: (i, 0)) for idx, _ in want]
    out_specs += [pl.BlockSpec((None,) + a.shape[1:], lambda i: (i // tpb, 0, 0)) for a in batch]
    out_specs += [pl.BlockSpec(a.shape, lambda i, nd=a.ndim: (0,) * nd) for a in bcast]
    out_shape = [SDS(tiled[idx].shape, dt) for idx, dt in want]
    out_shape += [SDS(a.shape, F32) for a in batch] + [SDS(a.shape, F32) for a in bcast]
    res = pl.pallas_call(
        body, grid=(t_rows // tm,),
        in_specs=_row_specs(tiled, batch, bcast, tm, tpb)
        + [pl.BlockSpec((tm, a.shape[1]), lambda i: (i, 0)) for a in list(cts) + extra],
        out_specs=out_specs, out_shape=out_shape, name=name, compiler_params=_cp(1))(*tiled, *batch, *bcast, *cts, *extra)
    return res[:len(want)], res[len(want):len(want) + n_b], res[len(want) + n_b:]


def _pick(n, cands):
    for c in cands:
        if n % c == 0:
            return c
    return n


def mm_tn(name, a, b, exchange=None):
    t_rows, m = a.shape
    n = b.shape[1]
    tn = n if n <= 1024 else _pick(n, (1024, 512, 256, 128))
    tm = max([t for t in range(LANES, m + 1, LANES) if m % t == 0 and t * tn * 4 <= 6 * 1024 * 1024] or [m])
    tk = _pick(t_rows, (512, 256, 128, 64))
    grid = (m // tm, n // tn, t_rows // tk)
    extra = [] if exchange is None else [exchange]

    def body(*refs):
        a_ref, b_ref = refs[:2]
        o_ref, acc = refs[2 + len(extra)], refs[3 + 2 * len(extra)]
        i, j, k = pl.program_id(0), pl.program_id(1), pl.program_id(2)
        if extra:
            start, finish = _exchange_phases(refs[2], refs[4], *refs[6:9])
            pl.when((i == 0) & (j == 0) & (k == 0))(start)

        @pl.when(k == 0)
        def _():
            acc[...] = jnp.zeros_like(acc)

        acc[...] += _dot_tn(a_ref[...], b_ref[...])

        @pl.when(k == grid[2] - 1)
        def _():
            o_ref[...] = acc[...].astype(BF16)

        if extra:
            pl.when((i == grid[0] - 1) & (j == grid[1] - 1) & (k == grid[2] - 1))(finish)

    res = pl.pallas_call(
        body, grid=grid,
        in_specs=[pl.BlockSpec((tk, tm), lambda i, j, k: (k, i)), pl.BlockSpec((tk, tn), lambda i, j, k: (k, j))]
        + [HBM_SPEC] * len(extra),
        out_specs=[pl.BlockSpec((tm, tn), lambda i, j, k: (i, j))] + [HBM_SPEC] * len(extra),
        out_shape=[SDS((m, n), BF16)] + [SDS(x.shape, x.dtype) for x in extra],
        scratch_shapes=[pltpu.VMEM((tm, tn), F32)] + (_comm_scratch() if extra else []), name=name,
        compiler_params=_cp(3))(a, b, *extra)
    return res if extra else res[0]


def _ffn_weight_spec():
    if FFN_TF == FF:
        return pl.BlockSpec((FF, D), lambda i, j: (0, 0), pipeline_mode=pl.Buffered(1))
    return pl.BlockSpec((FFN_TF, D), lambda i, j: (j, 0))


def ffn_fwd(name, h, mod3, g, w1, w3, w2, seq, gather=None):
    t_rows = h.shape[0]
    tm = _pick(seq, (FFN_FWD_TM, 128, 64))
    tf = FFN_TF
    tpb = seq // tm
    nf = FF // tf
    nt = t_rows // tm
    extra = [] if gather is None else [gather]

    def body(*refs):
        h_ref, mod_ref, g_ref, w1_ref, w3_ref, w2_ref = refs[:6]
        ho_ref, f_ref, u_ref, h1_ref, h3_ref = refs[6 + len(extra):11 + len(extra)]
        acc = refs[11 + 2 * len(extra)]
        i, j = pl.program_id(0), pl.program_id(1)
        if extra:
            start, forward, finish = _gather_phases(refs[6], refs[12], *refs[14:17])
            pl.when((i == 0) & (j == 0))(start)
            pl.when((i == nt - 1) & (j == 0))(forward)

        @pl.when(j == 0)
        def _():
            u_ref[...] = normmod(h_ref[...], g_ref[...], mod_ref[1:2, :], mod_ref[0:1, :]).astype(BF16)
            acc[...] = jnp.zeros_like(acc)

        u = u_ref[...]
        h1 = _dot_nt(u, w1_ref[...])
        h3 = _dot_nt(u, w3_ref[...])
        h1_ref[...] = h1.astype(BF16)
        h3_ref[...] = h3.astype(BF16)
        acc[...] += _dot(_silu(h1) * h3, w2_ref[...])

        @pl.when(j == nf - 1)
        def _():
            f_ref[...] = acc[...]
            ho_ref[...] = h_ref[...] + 0.5 * mod_ref[2:3, :] * acc[...]

        if extra:
            pl.when((i == nt - 1) & (j == nf - 1))(finish)

    row = lambda i, j: (i, 0)
    return pl.pallas_call(
        body, grid=(nt, nf),
        in_specs=[pl.BlockSpec((tm, D), row), pl.BlockSpec((None, 3, D), lambda i, j: (i // tpb, 0, 0)),
                  pl.BlockSpec((1, D), lambda i, j: (0, 0)), _ffn_weight_spec(), _ffn_weight_spec(), _ffn_weight_spec()]
        + [HBM_SPEC] * len(extra),
        out_specs=[pl.BlockSpec((tm, D), row), pl.BlockSpec((tm, D), row), pl.BlockSpec((tm, D), row),
                   pl.BlockSpec((tm, tf), lambda i, j: (i, j)), pl.BlockSpec((tm, tf), lambda i, j: (i, j))]
        + [HBM_SPEC] * len(extra),
        out_shape=[SDS((t_rows, D), F32), SDS((t_rows, D), F32), SDS((t_rows, D), BF16), SDS((t_rows, FF), BF16),
                   SDS((t_rows, FF), BF16)] + [SDS((NDEV,) + x.shape, x.dtype) for x in extra],
        scratch_shapes=[pltpu.VMEM((tm, D), F32)] + (_comm_scratch() if extra else []), name=name,
        compiler_params=_cp(2))(h, mod3, g, w1, w3, w2, *extra)


def ffn_bwd(name, dho, h, f_out, h1_in, h3_in, mod3, g, w1, w3, w2, seq, exchange=None):
    t_rows = h.shape[0]
    tm = _pick(seq, (FFN_BWD_TM, 128, 64))
    tf = FFN_TF
    tpb = seq // tm
    nf = FF // tf
    nt = t_rows // tm
    extra = [] if exchange is None else [exchange]

    def body(*refs):
        dho_ref, h_ref, f_ref, h1_ref, h3_ref, mod_ref, g_ref, w1_ref, w3_ref, w2_ref = refs[:10]
        dh_ref, a_ref, dh1_ref, dh3_ref, df_scr, dmod_ref, dg_ref = refs[10 + len(extra):17 + len(extra)]
        du_acc = refs[17 + 2 * len(extra)]
        i, j = pl.program_id(0), pl.program_id(1)
        if extra:
            start, finish = _exchange_phases(refs[10], refs[18], *refs[20:23])
            pl.when((i == 0) & (j == 0))(start)

        @pl.when(j == 0)
        def _():
            df_scr[...] = (0.5 * mod_ref[2:3, :] * dho_ref[...]).astype(BF16)
            du_acc[...] = jnp.zeros_like(du_acc)

        h1 = h1_ref[...].astype(F32)
        h3 = h3_ref[...].astype(F32)
        sg = jax.nn.sigmoid(h1)
        s = h1 * sg
        da = _dot_nt(df_scr[...], w2_ref[...])
        dh3 = (da * s).astype(BF16)
        dh1 = (da * h3 * (sg * (1.0 + h1 * (1.0 - sg)))).astype(BF16)
        a_ref[...] = (s * h3).astype(BF16)
        dh1_ref[...] = dh1
        dh3_ref[...] = dh3
        du_acc[...] += _dot(dh1, w1_ref[...]) + _dot(dh3, w3_ref[...])

        @pl.when(j == nf - 1)
        def _():
            _, vjp = jax.vjp(normmod, h_ref[...], g_ref[...], mod_ref[1:2, :], mod_ref[0:1, :])
            dh_n, dg, dsc, dsh = vjp(du_acc[...])
            dh_ref[...] = dho_ref[...] + dh_n
            dgt = jnp.sum(0.5 * dho_ref[...] * f_ref[...], axis=0, keepdims=True)
            dmod = jnp.concatenate([dsh, dsc, dgt], axis=0)

            @pl.when(i % tpb == 0)
            def _():
                dmod_ref[...] = dmod

            @pl.when(i % tpb != 0)
            def _():
                dmod_ref[...] += dmod

            @pl.when(i == 0)
            def _():
                dg_ref[...] = dg

            @pl.when(i != 0)
            def _():
                dg_ref[...] += dg

        if extra:
            pl.when((i == nt - 1) & (j == nf - 1))(finish)

    row = lambda i, j: (i, 0)
    col = lambda i, j: (i, j)
    return pl.pallas_call(
        body, grid=(nt, nf),
        in_specs=[pl.BlockSpec((tm, D), row), pl.BlockSpec((tm, D), row), pl.BlockSpec((tm, D), row),
                  pl.BlockSpec((tm, tf), col), pl.BlockSpec((tm, tf), col),
                  pl.BlockSpec((None, 3, D), lambda i, j: (i // tpb, 0, 0)),
                  pl.BlockSpec((1, D), lambda i, j: (0, 0)), _ffn_weight_spec(), _ffn_weight_spec(), _ffn_weight_spec()]
        + [HBM_SPEC] * len(extra),
        out_specs=[pl.BlockSpec((tm, D), row), pl.BlockSpec((tm, tf), col), pl.BlockSpec((tm, tf), col),
                   pl.BlockSpec((tm, tf), col), pl.BlockSpec((tm, D), row),
                   pl.BlockSpec((None, 3, D), lambda i, j: (i // tpb, 0, 0)), pl.BlockSpec((1, D), lambda i, j: (0, 0))]
        + [HBM_SPEC] * len(extra),
        out_shape=[SDS((t_rows, D), F32), SDS((t_rows, FF), BF16), SDS((t_rows, FF), BF16), SDS((t_rows, FF), BF16),
                   SDS((t_rows, D), BF16), SDS(mod3.shape, F32), SDS((1, D), F32)] + [SDS(x.shape, x.dtype) for x in extra],
        scratch_shapes=[pltpu.VMEM((tm, D), F32)] + (_comm_scratch() if extra else []), name=name,
        compiler_params=_cp(2))(dho, h, f_out, h1_in, h3_in, mod3, g, w1, w3, w2, *extra)


def _resident(shape):
    return pl.BlockSpec(shape, lambda i: (0,) * len(shape), pipeline_mode=pl.Buffered(1))


def mix_in_fwd(h, sh, sc, g, ws, seq):
    t_rows = h.shape[0]
    tm = _pick(seq, (256, 128, 64))
    tpb = seq // tm
    nw = len(ws)

    def body(h_ref, sh_ref, sc_ref, g_ref, *rest):
        u = normmod(h_ref[...], g_ref[...], sc_ref[...], sh_ref[...]).astype(BF16)
        rest[nw][...] = u
        for w_ref, p_ref in zip(rest[:nw], rest[nw + 1:]):
            p_ref[...] = _dot_nt(u, w_ref[...])

    row = lambda i: (i, 0)
    batch = pl.BlockSpec((None, 1, D), lambda i: (i // tpb, 0, 0))
    return pl.pallas_call(
        body, grid=(t_rows // tm,),
        in_specs=[pl.BlockSpec((tm, D), row), batch, batch, pl.BlockSpec((1, D), lambda i: (0, 0))]
        + [_resident(w.shape) for w in ws],
        out_specs=[pl.BlockSpec((tm, D), row)] + [pl.BlockSpec((tm, w.shape[0]), row) for w in ws],
        out_shape=[SDS((t_rows, D), BF16)] + [SDS((t_rows, w.shape[0]), F32) for w in ws], name="mix_in_fwd",
        compiler_params=_cp(1))(h, sh, sc, g, *ws)


def mix_in_bwd(dps, ws, h, sh, sc, g, dh_add, seq):
    t_rows = h.shape[0]
    tm = _pick(seq, (256, 128, 64))
    tpb = seq // tm
    nw = len(ws)

    def body(*refs):
        h_ref, sh_ref, sc_ref, g_ref, add_ref, dh_ref, dsh_ref, dsc_ref, dg_ref = refs[2 * nw:]
        i = pl.program_id(0)
        du = _dot(refs[0][...], refs[nw][...])
        for k in range(1, nw):
            du = du + _dot(refs[k][...], refs[nw + k][...])
        _, vjp = jax.vjp(normmod, h_ref[...], g_ref[...], sc_ref[...], sh_ref[...])
        dh_n, dg, dsc, dsh = vjp(du)
        dh_ref[...] = add_ref[...] + dh_n

        @pl.when(i % tpb == 0)
        def _():
            dsh_ref[...] = dsh
            dsc_ref[...] = dsc

        @pl.when(i % tpb != 0)
        def _():
            dsh_ref[...] += dsh
            dsc_ref[...] += dsc

        @pl.when(i == 0)
        def _():
            dg_ref[...] = dg

        @pl.when(i != 0)
        def _():
            dg_ref[...] += dg

    row = lambda i: (i, 0)
    batch = pl.BlockSpec((None, 1, D), lambda i: (i // tpb, 0, 0))
    gain = pl.BlockSpec((1, D), lambda i: (0, 0))
    return pl.pallas_call(
        body, grid=(t_rows // tm,),
        in_specs=[pl.BlockSpec((tm, dp.shape[1]), row) for dp in dps] + [_resident(w.shape) for w in ws]
        + [pl.BlockSpec((tm, D), row), batch, batch, gain, pl.BlockSpec((tm, D), row)],
        out_specs=[pl.BlockSpec((tm, D), row), batch, batch, gain],
        out_shape=[SDS((t_rows, D), F32), SDS(sh.shape, F32), SDS(sc.shape, F32), SDS((1, D), F32)], name="mix_in_bwd",
        compiler_params=_cp(1))(*dps, *ws, h, sh, sc, g, dh_add)


def mix_out_fwd(merged, w_out, h_prev, gt, seq):
    t_rows = merged.shape[0]
    tm = _pick(seq, (256, 128, 64))
    tpb = seq // tm

    def body(m_ref, w_ref, h_ref, gt_ref, mo_ref, ho_ref):
        mo = _dot(m_ref[...], w_ref[...])
        mo_ref[...] = mo
        ho_ref[...] = h_ref[...] + gt_ref[...] * mo

    row = lambda i: (i, 0)
    return pl.pallas_call(
        body, grid=(t_rows // tm,),
        in_specs=[pl.BlockSpec((tm, D), row), _resident(w_out.shape), pl.BlockSpec((tm, D), row),
                  pl.BlockSpec((None, 1, D), lambda i: (i // tpb, 0, 0))],
        out_specs=[pl.BlockSpec((tm, D), row), pl.BlockSpec((tm, D), row)],
        out_shape=[SDS((t_rows, D), F32), SDS((t_rows, D), F32)], name="mix_out_fwd",
        compiler_params=_cp(1))(merged, w_out, h_prev, gt)


def mix_out_bwd(dh, mo, w_out, gt, seq):
    t_rows = dh.shape[0]
    tm = _pick(seq, (256, 128, 64))
    tpb = seq // tm

    def body(dh_ref, mo_ref, w_ref, gt_ref, dmo_ref, dm_ref, dgt_ref):
        i = pl.program_id(0)
        dmo = (gt_ref[...] * dh_ref[...]).astype(BF16)
        dmo_ref[...] = dmo
        dm_ref[...] = _dot_nt(dmo, w_ref[...])
        dgt = jnp.sum(dh_ref[...] * mo_ref[...], axis=0, keepdims=True)

        @pl.when(i % tpb == 0)
        def _():
            dgt_ref[...] = dgt

        @pl.when(i % tpb != 0)
        def _():
            dgt_ref[...] += dgt

    row = lambda i: (i, 0)
    batch = pl.BlockSpec((None, 1, D), lambda i: (i // tpb, 0, 0))
    return pl.pallas_call(
        body, grid=(t_rows // tm,),
        in_specs=[pl.BlockSpec((tm, D), row), pl.BlockSpec((tm, D), row), _resident(w_out.shape), batch],
        out_specs=[pl.BlockSpec((tm, D), row), pl.BlockSpec((tm, D), row), batch],
        out_shape=[SDS((t_rows, D), BF16), SDS((t_rows, D), F32), SDS(gt.shape, F32)], name="mix_out_bwd",
        compiler_params=_cp(1))(dh, mo, w_out, gt)


def _dn_cols(part, hd):
    return slice(part * DNW + hd * DH, part * DNW + (hd + 1) * DH)


def _qkv_stacks(qkv_ref, nb):
    pairs = [(b, hd) for b in range(nb) for hd in range(NH)]
    return [jnp.stack([qkv_ref[b, :, _dn_cols(part, hd)] for b, hd in pairs]) for part in range(3)]


def dn_prep_fwd(p_dn, conv8):
    bl, seq, _ = p_dn.shape
    tp = _pick(seq, (256, 128, 64))

    def body(raw_ref, halo_ref, conv_ref, o_ref):
        hm = (pl.program_id(1) > 0).astype(F32)
        o_ref[...] = dn_prep(jnp.concatenate([halo_ref[...] * hm, raw_ref[...]], axis=0), conv_ref[...])

    return pl.pallas_call(
        body, grid=(bl, seq // tp),
        in_specs=[pl.BlockSpec((None, tp, 3 * DNW), lambda b, i: (b, i, 0)),
                  pl.BlockSpec((None, 8, 3 * DNW), lambda b, i: (b, jnp.maximum(i * (tp // 8) - 1, 0), 0)),
                  pl.BlockSpec((8, 3 * DNW), lambda b, i: (0, 0))],
        out_specs=pl.BlockSpec((None, tp, 3 * DNW), lambda b, i: (b, i, 0)),
        out_shape=SDS((bl, seq, 3 * DNW), F32), name="dn_prep_fwd", compiler_params=_cp(2))(p_dn, p_dn, conv8)


def dn_prep_bwd(p_dn, conv8, d_qkv, d_z):
    bl, seq, _ = p_dn.shape
    tp = _pick(seq, (256, 128, 64))
    nt = seq // tp

    def body(raw_ref, halo_ref, conv_ref, dq_ref, dz_ref, draw_ref, dconv_ref, carry):
        b, r = pl.program_id(0), pl.program_id(1)

        @pl.when((b == 0) & (r == 0))
        def _():
            dconv_ref[...] = jnp.zeros_like(dconv_ref)

        @pl.when(r == 0)
        def _():
            carry[...] = jnp.zeros_like(carry)

        hm = (r < nt - 1).astype(F32)
        _, vjp = jax.vjp(dn_prep, jnp.concatenate([halo_ref[...] * hm, raw_ref[...]], axis=0), conv_ref[...])
        dxc, dw = vjp(dq_ref[...])
        tail = dxc[tp:tp + 8] + carry[...]
        draw_ref[:, 0:3 * DNW] = jnp.concatenate([dxc[8:tp], tail], axis=0).astype(BF16)
        draw_ref[:, 3 * DNW:4 * DNW] = dz_ref[...].astype(BF16)
        carry[...] = dxc[0:8] * hm
        dconv_ref[...] += dw

    blk = lambda b, r: (b, nt - 1 - r, 0)
    return pl.pallas_call(
        body, grid=(bl, nt),
        in_specs=[pl.BlockSpec((None, tp, 3 * DNW), blk),
                  pl.BlockSpec((None, 8, 3 * DNW), lambda b, r: (b, jnp.maximum((nt - 1 - r) * (tp // 8) - 1, 0), 0)),
                  pl.BlockSpec((8, 3 * DNW), lambda b, r: (0, 0)), pl.BlockSpec((None, tp, 3 * DNW), blk),
                  pl.BlockSpec((None, tp, DNW), blk)],
        out_specs=[pl.BlockSpec((None, tp, 4 * DNW), blk), pl.BlockSpec((8, 3 * DNW), lambda b, r: (0, 0))],
        out_shape=[SDS((bl, seq, 4 * DNW), BF16), SDS((8, 3 * DNW), F32)],
        scratch_shapes=[pltpu.VMEM((8, 3 * DNW), F32)], name="dn_prep_bwd", compiler_params=_cp(2))(p_dn, p_dn, conv8, d_qkv, d_z)


def _gate_stacks(gates, nb):
    pairs = [(b, hd) for b in range(nb) for hd in range(NH)]
    bs = jnp.stack([gates[b][0][:, hd:hd + 1] for b, hd in pairs])
    gs = jnp.stack([gates[b][1][:, NH + hd:NH + hd + 1] for b, hd in pairs])
    gts = jnp.stack([gates[b][2][NH + hd:NH + hd + 1, :] for b, hd in pairs])
    return bs, gs, gts


def deltanet_fwd(qkv, p_small, alp, dtp, nb):
    bl, seq, _ = qkv.shape
    nc = seq // CH
    ng = nb * NH

    def body(qkv_ref, small_ref, alp_ref, dtp_ref, o_ref, sprev_ref, tinv_ref, s_scr):
        @pl.when(pl.program_id(1) == 0)
        def _():
            s_scr[...] = jnp.zeros_like(s_scr)

        gates = [gate_fn(small_ref[b], alp_ref[...], dtp_ref[...]) for b in range(nb)]
        s_prev = s_scr[...]
        o, s_new, tinv = dn_chunk(*_qkv_stacks(qkv_ref, nb), *_gate_stacks(gates, nb), s_prev)
        sprev_ref[...] = s_prev
        tinv_ref[...] = tinv
        s_scr[...] = s_new
        for b in range(nb):
            for hd in range(NH):
                o_ref[b, :, hd * DH:(hd + 1) * DH] = o[b * NH + hd]

    blk = lambda bb, n: (bb, n, 0)
    const = lambda bb, n: (0, 0)
    saved = pl.BlockSpec((None, ng, DH, DH), lambda bb, n: (bb * nc + n, 0, 0, 0))
    return pl.pallas_call(
        body, grid=(bl // nb, nc),
        in_specs=[pl.BlockSpec((nb, CH, 3 * DNW), blk), pl.BlockSpec((nb, CH, LANES), blk),
                  pl.BlockSpec((1, LANES), const), pl.BlockSpec((1, LANES), const)],
        out_specs=[pl.BlockSpec((nb, CH, DNW), blk), saved, saved],
        out_shape=[SDS((bl, seq, DNW), F32), SDS((bl // nb * nc, ng, DH, DH), F32), SDS((bl // nb * nc, ng, DH, DH), F32)],
        scratch_shapes=[pltpu.VMEM((ng, DH, DH), F32)], name="deltanet_fwd",
        compiler_params=_cp(2))(qkv, p_small, alp, dtp)


def deltanet_bwd(qkv, p_small, alp, dtp, sprev, tinv, d_o, nb, exchange=None):
    bl, seq, _ = qkv.shape
    nc = seq // CH
    ng = nb * NH
    extra = [] if exchange is None else [exchange]

    def body(*refs):
        qkv_ref, small_ref, alp_ref, dtp_ref, sprev_ref, tinv_ref, do_ref = refs[:7]
        dqkv_ref, dsmall_ref, dalp_ref, ddtp_ref = refs[7 + len(extra):11 + len(extra)]
        ds_scr = refs[11 + 2 * len(extra)]
        bb, r = pl.program_id(0), pl.program_id(1)
        if extra:
            start, finish = _exchange_phases(refs[7], refs[12], *refs[14:17])
            pl.when((bb == 0) & (r == 0))(start)

        @pl.when((bb == 0) & (r == 0))
        def _():
            dalp_ref[...] = jnp.zeros_like(dalp_ref)
            ddtp_ref[...] = jnp.zeros_like(ddtp_ref)

        @pl.when(r == 0)
        def _():
            ds_scr[...] = jnp.zeros_like(ds_scr)

        gates, gate_vjps = [], []
        for b in range(nb):
            out, gvjp = jax.vjp(gate_fn, small_ref[b], alp_ref[...], dtp_ref[...])
            gates.append(out)
            gate_vjps.append(gvjp)
        t_saved = tinv_ref[...]
        _, vjp = jax.vjp(lambda *args: dn_chunk(*args, t_saved)[:2], *_qkv_stacks(qkv_ref, nb), *_gate_stacks(gates, nb),
                         sprev_ref[...])
        d_out = jnp.stack([do_ref[b, :, hd * DH:(hd + 1) * DH] for b in range(nb) for hd in range(NH)])
        grads = vjp((d_out, ds_scr[...]))
        ds_scr[...] = grads[6]
        lane = _iota2((CH, LANES), 1)
        rowi = _iota2((LANES, CH), 0)
        for b in range(nb):
            d_beta = jnp.zeros((CH, LANES), F32)
            d_gc = jnp.zeros((CH, LANES), F32)
            d_gct = jnp.zeros((LANES, CH), F32)
            for hd in range(NH):
                i = b * NH + hd
                for part in range(3):
                    dqkv_ref[b, :, _dn_cols(part, hd)] = grads[part][i]
                d_beta = d_beta + jnp.where(lane == hd, grads[3][i], 0.0)
                d_gc = d_gc + jnp.where(lane == NH + hd, grads[4][i], 0.0)
                d_gct = d_gct + jnp.where(rowi == NH + hd, grads[5][i], 0.0)
            d_small, d_alp, d_dtp = gate_vjps[b]((d_beta, d_gc, d_gct))
            dsmall_ref[b] = d_small.astype(BF16)
            dalp_ref[...] += d_alp
            ddtp_ref[...] += d_dtp
        if extra:
            pl.when((bb == bl // nb - 1) & (r == nc - 1))(finish)

    blk = lambda bb, r: (bb, nc - 1 - r, 0)
    const = lambda bb, r: (0, 0)
    saved = pl.BlockSpec((None, ng, DH, DH), lambda bb, r: (bb * nc + nc - 1 - r, 0, 0, 0))
    return pl.pallas_call(
        body, grid=(bl // nb, nc),
        in_specs=[pl.BlockSpec((nb, CH, 3 * DNW), blk), pl.BlockSpec((nb, CH, LANES), blk), pl.BlockSpec((1, LANES), const),
                  pl.BlockSpec((1, LANES), const), saved, saved, pl.BlockSpec((nb, CH, DNW), blk)] + [HBM_SPEC] * len(extra),
        out_specs=[pl.BlockSpec((nb, CH, 3 * DNW), blk), pl.BlockSpec((nb, CH, LANES), blk), pl.BlockSpec((1, LANES), const),
                   pl.BlockSpec((1, LANES), const)] + [HBM_SPEC] * len(extra),
        out_shape=[SDS((bl, seq, 3 * DNW), F32), SDS((bl, seq, LANES), BF16), SDS((1, LANES), F32), SDS((1, LANES), F32)]
        + [SDS(x.shape, x.dtype) for x in extra],
        scratch_shapes=[pltpu.VMEM((ng, DH, DH), F32)] + (_comm_scratch() if extra else []), name="deltanet_bwd",
        compiler_params=_cp(2))(qkv, p_small, alp, dtp, sprev, tinv, d_o, *extra)


def _s5_table_specs():
    tab3 = pl.BlockSpec((None, LANES, 512), lambda gb, n: (gb, 0, 0))
    tab2 = pl.BlockSpec((CH, 512), lambda gb, n: (0, gb))
    return [tab3] * 4 + [tab2] * 6 + [pl.BlockSpec((1, LANES), lambda gb, n: (0, gb))]


def s5_fwd(u, tables, dsk):
    bl, seq, _ = u.shape
    nc = seq // CH

    def body(u_ref, *rest):
        tabs, (y_ref, xs_ref, xr_scr, xi_scr) = rest[:11], rest[11:]

        @pl.when(pl.program_id(1) == 0)
        def _():
            xr_scr[...] = jnp.zeros_like(xr_scr)
            xi_scr[...] = jnp.zeros_like(xi_scr)

        xp_re, xp_im = xr_scr[...], xi_scr[...]
        xs_ref[0:bl] = xp_re
        xs_ref[bl:2 * bl] = xp_im
        y, xn_re, xn_im = s5_chunk(u_ref[...], xp_re, xp_im, *[t[...] for t in tabs])
        y_ref[...] = y
        xr_scr[...] = xn_re
        xi_scr[...] = xn_im

    blk = lambda gb, n: (0, n, gb)
    return pl.pallas_call(
        body, grid=(GB, nc), in_specs=[pl.BlockSpec((bl, CH, LANES), blk)] + _s5_table_specs(),
        out_specs=[pl.BlockSpec((bl, CH, LANES), blk),
                   pl.BlockSpec((None, 2 * bl, 1, 512), lambda gb, n: (gb * nc + n, 0, 0, 0))],
        out_shape=[SDS((bl, seq, S5W), F32), SDS((GB * nc, 2 * bl, 1, 512), F32)],
        scratch_shapes=[pltpu.VMEM((bl, 1, 512), F32), pltpu.VMEM((bl, 1, 512), F32)], name="s5_fwd",
        compiler_params=_cp(2))(u, *tables, dsk)


def s5_bwd(u, tables, dsk, xs, dy):
    bl, seq, _ = u.shape
    nc = seq // CH

    def body(u_ref, *rest):
        tabs, xs_ref, dy_ref = rest[:11], rest[11], rest[12]
        du_ref, dtabs, dxr_scr, dxi_scr = rest[13], rest[14:25], rest[25], rest[26]
        r = pl.program_id(1)

        @pl.when(r == 0)
        def _():
            for t in dtabs:
                t[...] = jnp.zeros_like(t)
            dxr_scr[...] = jnp.zeros_like(dxr_scr)
            dxi_scr[...] = jnp.zeros_like(dxi_scr)

        _, vjp = jax.vjp(s5_chunk, u_ref[...], xs_ref[0:bl], xs_ref[bl:2 * bl], *[t[...] for t in tabs])
        grads = vjp((dy_ref[...], dxr_scr[...], dxi_scr[...]))
        du_ref[...] = grads[0].astype(BF16)
        dxr_scr[...] = grads[1]
        dxi_scr[...] = grads[2]
        for t, g in zip(dtabs, grads[3:]):
            t[...] += g

    blk = lambda gb, r: (0, nc - 1 - r, gb)
    tab_shapes = [SDS(t.shape, F32) for t in tables] + [SDS(dsk.shape, F32)]
    return pl.pallas_call(
        body, grid=(GB, nc),
        in_specs=[pl.BlockSpec((bl, CH, LANES), blk)] + _s5_table_specs()
        + [pl.BlockSpec((None, 2 * bl, 1, 512), lambda gb, r: (gb * nc + nc - 1 - r, 0, 0, 0)), pl.BlockSpec((bl, CH, LANES), blk)],
        out_specs=[pl.BlockSpec((bl, CH, LANES), blk)] + _s5_table_specs(),
        out_shape=[SDS((bl, seq, S5W), BF16)] + tab_shapes,
        scratch_shapes=[pltpu.VMEM((bl, 1, 512), F32), pltpu.VMEM((bl, 1, 512), F32)], name="s5_bwd",
        compiler_params=_cp(2))(u, *tables, dsk, xs, dy)


def s5_tables_fwd(params):
    shapes = [SDS((GB, LANES, 512), F32)] * 4 + [SDS((CH, S5N), F32)] * 6

    def body(*refs):
        for r, t in zip(refs[7:], s5_tables(*[p[...] for p in refs[:7]])):
            r[...] = t

    return pl.pallas_call(body, out_shape=shapes, name="s5_tables_fwd", compiler_params=_cp())(*params)


def s5_tables_bwd(params, dtables):
    def body(*refs):
        _, vjp = jax.vjp(s5_tables, *[p[...] for p in refs[:7]])
        for r, g in zip(refs[17:], vjp(tuple(t[...] for t in refs[7:17]))):
            r[...] = g

    return pl.pallas_call(body, out_shape=[SDS(p.shape, F32) for p in params], name="s5_tables_bwd",
                          compiler_params=_cp())(*params, *dtables)


def ada_fwd(c_all, w_loc, b_loc):
    def body(c_ref, w_ref, b_ref, o_ref):
        o_ref[...] = _dot(_silu(c_ref[...]), w_ref[...]) + b_ref[...]

    return pl.pallas_call(body, out_shape=SDS((c_all.shape[0], w_loc.shape[1]), F32), name="ada_fwd",
                          compiler_params=_cp())(c_all, w_loc, b_loc)


def ada_bwd(c_all, dmod_mine, dmod_all):
    def body(c_ref, dm_ref, da_ref, gw_ref, gb_ref):
        gw_ref[...] = _dot_tn(_silu(c_ref[...]), dm_ref[...])
        gb_ref[...] = jnp.sum(da_ref[...], axis=0, keepdims=True)

    return pl.pallas_call(body, out_shape=[SDS((D, dmod_mine.shape[1]), F32), SDS((1, dmod_all.shape[1]), F32)],
                          name="ada_bwd", compiler_params=_cp())(c_all, dmod_mine, dmod_all)


def loss_head(h, tgt, g, seq):
    t_rows = h.shape[0]
    tm = _pick(seq, (256, 128, 64))

    def body(h_ref, t_ref, g_ref, dh_ref, dg_ref, loss_ref):
        i = pl.program_id(0)
        y, vjp = jax.vjp(lambda hh, gg: hh * lax.rsqrt(jnp.mean(hh * hh, axis=-1, keepdims=True) + EPS) * gg,
                         h_ref[...], g_ref[...])
        e = y - t_ref[...]
        dh, dg = vjp(e * (1.0 / D))
        part = jnp.sum(jnp.sum(e * e, axis=1, keepdims=True), axis=0, keepdims=True) * (0.5 / D) + jnp.zeros((1, LANES), F32)
        dh_ref[...] = dh

        @pl.when(i == 0)
        def _():
            dg_ref[...] = dg
            loss_ref[...] = part

        @pl.when(i != 0)
        def _():
            dg_ref[...] += dg
            loss_ref[...] += part

    row = lambda i: (i, 0)
    const = lambda i: (0, 0)
    return pl.pallas_call(
        body, grid=(t_rows // tm,),
        in_specs=[pl.BlockSpec((tm, D), row), pl.BlockSpec((tm, D), row), pl.BlockSpec((1, D), const)],
        out_specs=[pl.BlockSpec((tm, D), row), pl.BlockSpec((1, D), const), pl.BlockSpec((1, LANES), const)],
        out_shape=[SDS((t_rows, D), F32), SDS((1, D), F32), SDS((1, LANES), F32)], name="loss_head",
        compiler_params=_cp(1))(h, tgt, g)


def adamw(name, parts, w, m, v):
    k_parts, rows, cols = parts.shape
    tr = _pick(rows, (256, 128, 64, 32, 16, 8))

    def body(p_ref, w_ref, m_ref, v_ref, g_ref, d_ref, mo_ref, vo_ref):
        g = p_ref[0].astype(F32)
        for k in range(1, k_parts):
            g = g + p_ref[k].astype(F32)
        _adam_store(g, w_ref, m_ref, v_ref, g_ref, d_ref, mo_ref, vo_ref)

    blk = pl.BlockSpec((tr, cols), lambda i: (i, 0))
    return pl.pallas_call(
        body, grid=(rows // tr,), in_specs=[pl.BlockSpec((k_parts, tr, cols), lambda i: (0, i, 0)), blk, blk, blk],
        out_specs=[blk] * 4, out_shape=[SDS((rows, cols), F32)] * 4, name=name, compiler_params=_cp(1))(parts, w, m, v)


def _adam_store(g, w_ref, m_ref, v_ref, g_ref, d_ref, mo_ref, vo_ref):
    m_new = ADAM_B1 * m_ref[...] + (1.0 - ADAM_B1) * g
    v_new = ADAM_B2 * v_ref[...] + (1.0 - ADAM_B2) * (g * g)
    m_hat = m_new / (1.0 - ADAM_B1 ** ADAM_STEP)
    v_hat = v_new / (1.0 - ADAM_B2 ** ADAM_STEP)
    g_ref[...] = g
    d_ref[...] = -ADAM_LR * (m_hat / (jnp.sqrt(v_hat) + ADAM_EPS) + ADAM_WD * w_ref[...])
    mo_ref[...] = m_new
    vo_ref[...] = v_new


def adamw_t(name, parts, w, m, v):
    k_parts, r, c = parts.shape
    tc = _pick(c, (256, 128))

    def body(p_ref, w_ref, m_ref, v_ref, g_ref, d_ref, mo_ref, vo_ref):
        gt = p_ref[0].astype(F32)
        for k in range(1, k_parts):
            gt = gt + p_ref[k].astype(F32)
        _adam_store(gt.T, w_ref, m_ref, v_ref, g_ref, d_ref, mo_ref, vo_ref)

    blk = pl.BlockSpec((tc, r), lambda j: (j, 0))
    return pl.pallas_call(
        body, grid=(c // tc,), in_specs=[pl.BlockSpec((k_parts, r, tc), lambda j: (0, 0, j)), blk, blk, blk],
        out_specs=[blk] * 4, out_shape=[SDS((c, r), F32)] * 4, name=name, compiler_params=_cp(1))(parts, w, m, v)


def _comm_scratch():
    return [pltpu.SemaphoreType.DMA((7,)), pltpu.SemaphoreType.DMA((7,)), pltpu.SemaphoreType.DMA]


HBM_SPEC = pl.BlockSpec(memory_space=pl.ANY)


def _gather_phases(x_ref, out_ref, send_sems, recv_sems, local_sem):
    mx, my, mc = lax.axis_index("x"), lax.axis_index("y"), lax.axis_index("c")
    me, sibling = (mx, my, mc), (mx, my, 1 - mc)
    chips = [(1 - mx, my), (mx, 1 - my), (1 - mx, 1 - my)]

    def slot(px, py, pc):
        return out_ref.at[4 * px + 2 * py + pc]

    def copy(k, block, to, src=None):
        return pltpu.make_async_remote_copy(
            src_ref=slot(*block) if src is None else src, dst_ref=slot(*block), send_sem=send_sems.at[k],
            recv_sem=recv_sems.at[k], device_id=to, device_id_type=pl.DeviceIdType.MESH)

    def first():
        return [copy(0, me, sibling, src=x_ref)] + [copy(1 + j, me, (*chip, mc), src=x_ref) for j, chip in enumerate(chips)]

    def passed():
        return [copy(4 + j, (*chip, mc), sibling) for j, chip in enumerate(chips)]

    def start():
        pltpu.make_async_copy(x_ref, slot(*me), local_sem).start()
        for cp in first():
            cp.start()

    def forward():
        for j, chip in enumerate(chips):
            copy(1 + j, (*chip, mc), me).wait_recv()
            passed()[j].start()

    def finish():
        copy(0, sibling, me).wait_recv()
        for j, chip in enumerate(chips):
            copy(4 + j, (*chip, 1 - mc), me).wait_recv()
        for cp in first() + passed():
            cp.wait_send()
        pltpu.make_async_copy(x_ref, slot(*me), local_sem).wait()

    return start, forward, finish


def _exchange_phases(x_ref, out_ref, send_sems, recv_sems, local_sem):
    mx, my, mc = lax.axis_index("x"), lax.axis_index("y"), lax.axis_index("c")
    me = 4 * mx + 2 * my + mc

    def peer(k):
        return mx ^ (k >> 2), my ^ ((k >> 1) & 1), mc ^ (k & 1)

    def sends():
        out = []
        for k in range(1, NDEV):
            px, py, pc = peer(k)
            out.append(pltpu.make_async_remote_copy(
                src_ref=x_ref.at[4 * px + 2 * py + pc], dst_ref=out_ref.at[me], send_sem=send_sems.at[k - 1],
                recv_sem=recv_sems.at[k - 1], device_id=(px, py, pc), device_id_type=pl.DeviceIdType.MESH))
        return out

    def start():
        pltpu.make_async_copy(x_ref.at[me], out_ref.at[me], local_sem).start()
        for cp in sends():
            cp.start()

    def finish():
        for k in range(1, NDEV):
            px, py, pc = peer(k)
            pltpu.make_async_remote_copy(
                src_ref=x_ref.at[me], dst_ref=out_ref.at[4 * px + 2 * py + pc], send_sem=send_sems.at[k - 1],
                recv_sem=recv_sems.at[k - 1], device_id=(px, py, pc), device_id_type=pl.DeviceIdType.MESH).wait_recv()
        for cp in sends():
            cp.wait_send()
        pltpu.make_async_copy(x_ref.at[me], out_ref.at[me], local_sem).wait()

    return start, finish


def all_gather(name, x):
    def body(x_ref, out_ref, send_sems, recv_sems, local_sem):
        for phase in _gather_phases(x_ref, out_ref, send_sems, recv_sems, local_sem):
            phase()

    return pl.pallas_call(body, out_shape=SDS((NDEV,) + x.shape, x.dtype), in_specs=[HBM_SPEC], out_specs=HBM_SPEC,
                          scratch_shapes=_comm_scratch(), name=name)(x)


def all_gather_pair(name, x1, x2):
    def body(x1_ref, x2_ref, o1_ref, o2_ref, *sems):
        first = _gather_phases(x1_ref, o1_ref, *sems[:3])
        second = _gather_phases(x2_ref, o2_ref, *sems[3:])
        for phase1, phase2 in zip(first, second):
            phase1()
            phase2()

    return pl.pallas_call(
        body, out_shape=[SDS((NDEV,) + x1.shape, x1.dtype), SDS((NDEV,) + x2.shape, x2.dtype)], in_specs=[HBM_SPEC] * 2,
        out_specs=[HBM_SPEC] * 2, scratch_shapes=_comm_scratch() + _comm_scratch(), name=name)(x1, x2)


def gather_with_exchange(name, xg, xe):
    def body(g_ref, e_ref, go_ref, eo_ref, *sems):
        g_start, g_forward, g_finish = _gather_phases(g_ref, go_ref, *sems[:3])
        e_start, e_finish = _exchange_phases(e_ref, eo_ref, *sems[3:])
        e_start()
        g_start()
        g_forward()
        g_finish()
        e_finish()

    return pl.pallas_call(
        body, out_shape=[SDS((NDEV,) + xg.shape, xg.dtype), SDS(xe.shape, xe.dtype)], in_specs=[HBM_SPEC] * 2,
        out_specs=[HBM_SPEC] * 2, scratch_shapes=_comm_scratch() + _comm_scratch(), name=name)(xg, xe)


def _pack(arrs, dtype, row_mult=8):
    segs = []
    for a in arrs:
        flat = a.reshape(-1).astype(dtype)
        segs.append(jnp.pad(flat, (0, (-flat.shape[0]) % ROW)))
    flat = jnp.concatenate(segs)
    flat = jnp.pad(flat, (0, (-flat.shape[0]) % (ROW * row_mult)))
    return flat.reshape(-1, ROW)


def _unpack(buf, shapes):
    flat = buf.reshape(-1)
    out, off = [], 0
    for s in shapes:
        n = math.prod(s)
        out.append(flat[off:off + n].reshape(s))
        off += n + (-n) % ROW
    return out


def _pack_rows(arrs, axis):
    padded = []
    for t in arrs:
        pad = [(0, 0)] * t.ndim
        pad[axis] = (0, _tile_rows(t.shape[axis]) - t.shape[axis])
        padded.append(jnp.pad(t, pad))
    return jnp.concatenate(padded, axis=axis)


def _tile_rows(r):
    return r + (-r) % 16


def _unpack8(buf, shapes):
    flat = buf.reshape(NDEV, -1)
    out, off = [], 0
    for s in shapes:
        n = math.prod(s)
        out.append(flat[:, off:off + n].reshape((NDEV,) + tuple(s)))
        off += n + (-n) % ROW
    return out


def kernel(x, c, w_ada, b_ada, g_ffn1, w1_ffn1, w3_ffn1, w2_ffn1, g_mix, w_in, conv_qkv, a_log, dt_bias, g_onorm, lam_re, lam_im, log_step, b_re, b_im, c_re, c_im, d_skip, w_glu, b_glu, w_proj_a, w_proj_b, w_out, g_ffn2, w1_ffn2, w3_ffn2, w2_ffn2, g_final, loss_target, m_w_ada, m_b_ada, m_g_ffn1, m_w1_ffn1, m_w3_ffn1, m_w2_ffn1, m_g_mix, m_w_in, m_conv_qkv, m_a_log, m_dt_bias, m_g_onorm, m_lam_re, m_lam_im, m_log_step, m_b_re, m_b_im, m_c_re, m_c_im, m_d_skip, m_w_glu, m_b_glu, m_w_proj_a, m_w_proj_b, m_w_out, m_g_ffn2, m_w1_ffn2, m_w3_ffn2, m_w2_ffn2, m_g_final, v_w_ada, v_b_ada, v_g_ffn1, v_w1_ffn1, v_w3_ffn1, v_w2_ffn1, v_g_mix, v_w_in, v_conv_qkv, v_a_log, v_dt_bias, v_g_onorm, v_lam_re, v_lam_im, v_log_step, v_b_re, v_b_im, v_c_re, v_c_im, v_d_skip, v_w_glu, v_b_glu, v_w_proj_a, v_w_proj_b, v_w_out, v_g_ffn2, v_w1_ffn2, v_w3_ffn2, v_w2_ffn2, v_g_final):
    a = dict(locals())
    bl, seq, _ = x.shape
    t_rows = bl * seq
    me = 4 * lax.axis_index("x") + 2 * lax.axis_index("y") + lax.axis_index("c")
    tm_ew = _pick(seq, (256, 128, 64))

    loc = {n: (a[n][0].T if n in COL_SHARDED else a[n][0]) for n in RS_WEIGHTS}
    wfull, gw, res = {}, {}, {}

    def pack_local(names):
        return _pack_rows([loc[n].astype(BF16).reshape(-1, ROW) for n in names], 0)

    def unpack_full(buf, names):
        r0 = 0
        for n in names:
            r = loc[n].size // ROW
            wfull[n] = buf[:, r0:r0 + r, :].reshape(-1, loc[n].shape[1])
            r0 += _tile_rows(r)

    def pack_grads(names):
        return _pack_rows([gw[n].astype(BF16).reshape(NDEV, -1, ROW) for n in names], 1)

    def update(buf, names):
        r0 = 0
        for n in names:
            r = loc[n].size // ROW
            parts = buf[:, r0:r0 + r, :].reshape((NDEV,) + loc[n].shape)
            r0 += _tile_rows(r)
            step = adamw_t if n in COL_SHARDED else adamw
            out = step("adamw_" + n, parts, a[n][0], a["m_" + n][0], a["v_" + n][0])
            for kind, t in zip(("grad", "delta", "new_m", "new_v"), out):
                res[kind + "_" + n] = t[None]

    sm, wg_ffn1 = all_gather_pair("gather_inputs", _pack([c, conv_qkv[0]], F32), pack_local(G_FFN1))
    unpack_full(wg_ffn1, G_FFN1)
    c_loc, conv_loc = _unpack8(sm, [c.shape, conv_qkv.shape[1:]])
    c_all = c_loc.reshape(NDEV * bl, D)
    conv_full = conv_loc.transpose(1, 0, 2).reshape(CONVW, 3 * DNW)

    n_ada = w_ada.shape[2]
    mod_part = ada_fwd(c_all, w_ada[0], lax.dynamic_slice(b_ada, (0, me * n_ada), (1, n_ada)))
    mod_all = all_gather("gather_mod", mod_part).transpose(1, 0, 2).reshape(NDEV * bl, 9 * D)
    mod = lax.dynamic_slice(mod_all, (me * bl, 0), (bl, 9 * D)).reshape(bl, 9, D)
    mods = [mod[:, k:k + 1, :] for k in range(9)]

    h0 = x.reshape(t_rows, D)
    h1, f1, u1, pa1, pb1, wg_rest = ffn_fwd("ffn1_fwd", h0, mod[:, 0:3, :], g_ffn1, wfull['w1_ffn1'], wfull['w3_ffn1'],
                                  wfull['w2_ffn1'], seq, gather=pack_local(G_MIX + G_FFN2))
    unpack_full(wg_rest, G_MIX + G_FFN2)
    win = wfull['w_in']
    o_small, o_s5, o_gate = 4 * DNW, 4 * DNW + 2 * NH, 4 * DNW + 2 * NH + S5W
    w_dn, w_small = win[:o_small], jnp.pad(win[o_small:o_s5], ((0, LANES - 2 * NH), (0, 0)))
    w_s5, w_gate = win[o_s5:o_gate], win[o_gate:]
    w_pieces = [w_dn, w_small, w_s5, w_gate]
    u2, p_dn, p_small, p_s5, p_gate = mix_in_fwd(h1, mods[3], mods[4], g_mix, w_pieces, seq)

    conv8 = jnp.pad(conv_full, ((0, 8 - CONVW), (0, 0)))
    alp = jnp.pad(a_log, ((0, 0), (NH, LANES - 2 * NH)))
    dtp = jnp.pad(dt_bias, ((0, 0), (NH, LANES - 2 * NH)))
    nb_dn = DN_ROWS if bl % DN_ROWS == 0 else 1
    p_dn3, p_small3 = p_dn.reshape(bl, seq, 4 * DNW), p_small.reshape(bl, seq, LANES)
    qkv3 = dn_prep_fwd(p_dn3, conv8)
    o_pre3, sprev, tinv = deltanet_fwd(qkv3, p_small3, alp, dtp, nb_dn)
    o_pre = o_pre3.reshape(t_rows, DNW)
    z_raw = p_dn[:, 3 * DNW:]

    s5_params = [lam_re.reshape(1, S5N), lam_im.reshape(1, S5N), log_step,
                 b_re[0].transpose(2, 0, 1).reshape(S5C, S5N), b_im[0].transpose(2, 0, 1).reshape(S5C, S5N),
                 c_re[0].transpose(1, 0, 2).reshape(S5C, S5N), c_im[0].transpose(1, 0, 2).reshape(S5C, S5N)]
    tables = s5_tables_fwd(s5_params)
    p_s53 = p_s5.reshape(bl, seq, S5W)
    y_s53, xs = s5_fwd(p_s53, tables, d_skip)
    y_s5 = y_s53.reshape(t_rows, S5W)
    tail_in = [o_pre, z_raw, y_s5, p_gate]
    tail_w = [g_onorm, wfull['w_glu'], b_glu, wfull['w_proj_a'], wfull['w_proj_b']]
    (merged,) = ew_call("mix_tail", fn_mix_tail, tail_in, [], tail_w, [(D, BF16)], tm_ew, seq)
    mo, h2 = mix_out_fwd(merged, wfull['w_out'], h1, mods[5], seq)
    h3, f3, u3, pa3, pb3 = ffn_fwd("ffn2_fwd", h2, mod[:, 6:9, :], g_ffn2, wfull['w1_ffn2'], wfull['w3_ffn2'], wfull['w2_ffn2'], seq)

    dh3, dg_final, loss_part = loss_head(h3, loss_target.reshape(t_rows, D), g_final.reshape(1, D), seq)
    loss = lax.psum(loss_part[0, 0], ("x", "y", "c"))

    dh2, a3, d1_3, d3_3, df3, dmod_c, dg_ffn2 = ffn_bwd("ffn2_bwd", dh3, h2, f3, pa3, pb3, mod[:, 6:9, :], g_ffn2, wfull['w1_ffn2'],
                                                   wfull['w3_ffn2'], wfull['w2_ffn2'], seq)
    gw['w1_ffn2'] = mm_tn("gw1_ffn2", d1_3, u3)
    gw['w3_ffn2'] = mm_tn("gw3_ffn2", d3_3, u3)
    gw['w2_ffn2'] = mm_tn("gw2_ffn2", a3, df3)

    dmo, d_merged, dgt2 = mix_out_bwd(dh2, mo, wfull['w_out'], mods[5], seq)
    gw['w_out'] = mm_tn("gw_out", merged, dmo)
    (d_opre, d_z, d_ys5, d_gate), _, tail_gw = ew_vjp_call(
        "mix_tail_bwd", fn_mix_tail, tail_in, [], tail_w, [d_merged], [(0, F32), (1, F32), (2, F32), (3, BF16)], tm_ew, seq)
    dg_onorm, gw['w_glu'], dg_bglu, gw['w_proj_a'], gw['w_proj_b'] = tail_gw
    d_qkv3, d_psmall3, d_alp, d_dtp, rs_ffn2 = deltanet_bwd(
        qkv3, p_small3, alp, dtp, sprev, tinv, d_opre.reshape(bl, seq, DNW), nb_dn, exchange=pack_grads(G_FFN2))
    d_pdn3, d_conv8 = dn_prep_bwd(p_dn3, conv8, d_qkv3, d_z.reshape(bl, seq, DNW))
    d_pdn, d_psmall = d_pdn3.reshape(t_rows, 4 * DNW), d_psmall3.reshape(t_rows, LANES)

    s5_out = s5_bwd(p_s53, tables, d_skip, xs, d_ys5.reshape(bl, seq, S5W))
    d_ps5, d_tables, dg_dskip = s5_out[0].reshape(t_rows, S5W), s5_out[1:11], s5_out[11]
    d_s5p = s5_tables_bwd(s5_params, d_tables)

    gw['w_in'] = jnp.concatenate([mm_tn("gw_dn", d_pdn, u2), mm_tn("gw_small", d_psmall, u2)[:2 * NH],
                                  mm_tn("gw_s5", d_ps5, u2), mm_tn("gw_gate", d_gate, u2)], axis=0)
    dh1, dsh2, dsc2, dg_mix = mix_in_bwd([d_pdn, d_psmall, d_ps5, d_gate], w_pieces, h1, mods[3], mods[4], g_mix, dh2, seq)

    dh0, a1, d1_1, d3_1, df1, dmod_a, dg_ffn1, rs_mix = ffn_bwd(
        "ffn1_bwd", dh1, h0, f1, pa1, pb1, mod[:, 0:3, :], g_ffn1, wfull['w1_ffn1'], wfull['w3_ffn1'], wfull['w2_ffn1'], seq,
        exchange=pack_grads(G_MIX))
    gw['w1_ffn1'] = mm_tn("gw1_ffn1", d1_1, u1)
    gw['w3_ffn1'], rs_w1 = mm_tn("gw3_ffn1", d3_1, u1, exchange=pack_grads(['w1_ffn1']))
    gw['w2_ffn1'], rs_w3 = mm_tn("gw2_ffn1", a1, df1, exchange=pack_grads(['w3_ffn1']))

    update(rs_ffn2, G_FFN2)
    update(rs_mix, G_MIX)
    update(rs_w1, ['w1_ffn1'])
    update(rs_w3, ['w3_ffn1'])

    dmod_mine = jnp.concatenate([dmod_a, dsh2, dsc2, dgt2, dmod_c], axis=1).reshape(bl, 9 * D)
    small_grads = {
        'g_ffn1': dg_ffn1, 'g_mix': dg_mix, 'a_log': d_alp[:, NH:2 * NH], 'dt_bias': d_dtp[:, NH:2 * NH],
        'g_onorm': dg_onorm, 'lam_re': d_s5p[0].reshape(1, S5G, S5P), 'lam_im': d_s5p[1].reshape(1, S5G, S5P),
        'log_step': d_s5p[2],
        'b_re': d_s5p[3].reshape(S5C, S5G, S5P).transpose(1, 2, 0)[None],
        'b_im': d_s5p[4].reshape(S5C, S5G, S5P).transpose(1, 2, 0)[None],
        'c_re': d_s5p[5].reshape(S5C, S5G, S5P).transpose(1, 0, 2)[None],
        'c_im': d_s5p[6].reshape(S5C, S5G, S5P).transpose(1, 0, 2)[None],
        'd_skip': dg_dskip, 'b_glu': dg_bglu, 'g_ffn2': dg_ffn2, 'g_final': dg_final.reshape(D)}
    small_shapes = [a[n].shape for n in SMALL]
    small_pack = _pack([small_grads[n] for n in SMALL], F32)
    n_small = small_pack.shape[0]
    sg, rs_w2 = gather_with_exchange("gather_small_grads",
                                     jnp.concatenate([small_pack, _pack([dmod_mine, d_conv8[:CONVW]], F32)], axis=0),
                                     pack_grads(['w2_ffn1']))
    update(rs_w2, ['w2_ffn1'])
    pieces = _unpack8(sg[:, n_small:, :], [dmod_mine.shape, (CONVW, 3 * DNW)])
    dmod_all = pieces[0].reshape(NDEV * bl, 9 * D)
    g_wada, g_bada = ada_bwd(c_all, lax.dynamic_slice(dmod_all, (0, me * n_ada), (NDEV * bl, n_ada)), dmod_all)

    n_conv = conv_qkv.shape[2]
    conv_parts = lax.dynamic_slice(pieces[1], (0, 0, me * n_conv), (NDEV, CONVW, n_conv))
    conv_parts = jnp.pad(conv_parts.reshape(NDEV, 1, -1), ((0, 0), (0, 7), (0, 0)))
    pad8 = lambda t: jnp.pad(t.reshape(1, -1), ((0, 7), (0, 0)))
    conv_res = adamw("adamw_conv", conv_parts, pad8(conv_qkv), pad8(m_conv_qkv), pad8(v_conv_qkv))
    for kind, buf in zip(("grad", "delta", "new_m", "new_v"), conv_res):
        res[kind + "_conv_qkv"] = buf[0].reshape(conv_qkv.shape)

    small_res = adamw("adamw_small", sg[:, :n_small, :], *[_pack([a[p + n] for n in SMALL], F32) for p in ("", "m_", "v_")])
    for kind, buf in zip(("grad", "delta", "new_m", "new_v"), small_res):
        for n, t in zip(SMALL, _unpack(buf, small_shapes)):
            res[kind + "_" + n] = t

    for n, g in (("w_ada", g_wada), ("b_ada", g_bada)):
        shp = a[n].shape
        r2 = lambda t: t.reshape(-1, shp[-1]) if n == "w_ada" else pad8(t)
        out = adamw("adamw_" + n, r2(g)[None], r2(a[n]), r2(a["m_" + n]), r2(a["v_" + n]))
        for kind, buf in zip(("grad", "delta", "new_m", "new_v"), out):
            res[kind + "_" + n] = (buf if n == "w_ada" else buf[0:1]).reshape(shp)

    outs = [loss, dh0.reshape(x.shape)]
    for kind in ("grad", "delta", "new_m", "new_v"):
        outs += [res[kind + "_" + n] for n in WEIGHTS]
    return tuple(outs)
```

```python
import math

import jax
import jax.numpy as jnp
from jax import lax
from jax.experimental import pallas as pl
from jax.experimental.pallas import tpu as pltpu

F32 = jnp.float32
BF16 = jnp.bfloat16
HI = lax.Precision.HIGHEST
H3 = lax.Precision.HIGH
SDS = jax.ShapeDtypeStruct

D = 1024
FF = 2816
FFN_TF = FF
FFN_FWD_TM = 256
FFN_BWD_TM = 256
NH = 8
DH = 64
DNW = NH * DH
CONVW = 4
CH = 64
S5_CH = 128
ACC_LIMIT = 6 * 1024 * 1024
BF16_TILE_ROWS = 16
DN_ROWS = 2
S5W = 512
S5G = 32
S5P = 64
S5C = 16
S5N = S5G * S5P
GB = 4
NDEV = 8
EPS = 1e-6
LANES = 128
ROW = 1024
VMEM_LIMIT = 56 * 1024 * 1024

ADAM_LR, ADAM_B1, ADAM_B2, ADAM_EPS, ADAM_WD, ADAM_STEP = 0.001, 0.9, 0.999, 1e-08, 0.01, 10

WEIGHTS = ['w_ada', 'b_ada', 'g_ffn1', 'w1_ffn1', 'w3_ffn1', 'w2_ffn1', 'g_mix', 'w_in', 'conv_qkv', 'a_log',
           'dt_bias', 'g_onorm', 'lam_re', 'lam_im', 'log_step', 'b_re', 'b_im', 'c_re', 'c_im', 'd_skip', 'w_glu',
           'b_glu', 'w_proj_a', 'w_proj_b', 'w_out', 'g_ffn2', 'w1_ffn2', 'w3_ffn2', 'w2_ffn2', 'g_final']
RS_WEIGHTS = ['w1_ffn1', 'w3_ffn1', 'w2_ffn1', 'w_in', 'w_glu', 'w_proj_a', 'w_proj_b', 'w_out', 'w1_ffn2', 'w3_ffn2',
              'w2_ffn2']
COL_SHARDED = {'w1_ffn1', 'w3_ffn1', 'w_in', 'w_proj_a', 'w_proj_b', 'w1_ffn2', 'w3_ffn2'}
G_FFN1 = ['w1_ffn1', 'w3_ffn1', 'w2_ffn1']
G_MIX = ['w_in', 'w_glu', 'w_proj_a', 'w_proj_b', 'w_out']
G_FFN2 = ['w1_ffn2', 'w3_ffn2', 'w2_ffn2']
SMALL = ['g_ffn1', 'g_mix', 'a_log', 'dt_bias', 'g_onorm', 'lam_re', 'lam_im', 'log_step', 'b_re', 'b_im', 'c_re',
         'c_im', 'd_skip', 'b_glu', 'g_ffn2', 'g_final']


def _cp(n_grid=0):
    if n_grid:
        return pltpu.CompilerParams(vmem_limit_bytes=VMEM_LIMIT, dimension_semantics=("arbitrary",) * n_grid)
    return pltpu.CompilerParams(vmem_limit_bytes=VMEM_LIMIT)


def _dot(a, b):
    return jnp.dot(a.astype(BF16), b.astype(BF16), preferred_element_type=F32)


def _dot_nt(a, b):
    return lax.dot_general(a.astype(BF16), b.astype(BF16), (((1,), (1,)), ((), ())), preferred_element_type=F32)


def _dot_tn(a, b):
    return lax.dot_general(a.astype(BF16), b.astype(BF16), (((0,), (0,)), ((), ())), preferred_element_type=F32)


def _dot_hi(a, b):
    return jnp.dot(a, b, precision=HI, preferred_element_type=F32)


def _dot_h3(a, b):
    return jnp.dot(a, b, precision=H3, preferred_element_type=F32)


@jax.custom_vjp
def bdot(a, b):
    return _dot(a, b)


bdot.defvjp(lambda a, b: (_dot(a, b), (a, b)),
            lambda r, g: (_dot_nt(g, r[1]).astype(r[0].dtype), _dot_tn(r[0], g).astype(r[1].dtype)))


@jax.custom_vjp
def bdot_nt(a, b):
    return _dot_nt(a, b)


bdot_nt.defvjp(lambda a, b: (_dot_nt(a, b), (a, b)),
               lambda r, g: (_dot(g, r[1]).astype(r[0].dtype), _dot_tn(g, r[0]).astype(r[1].dtype)))


def _silu(x):
    return x * jax.nn.sigmoid(x)


def _iota2(shape, axis):
    return lax.broadcasted_iota(jnp.int32, shape, axis)


def normmod(h, g, sc, sh):
    y = h * lax.rsqrt(jnp.mean(h * h, axis=-1, keepdims=True) + EPS) * g
    return y * (1.0 + sc) + sh


def fn_merge(gate, ya, yb):
    return (jax.nn.sigmoid(gate[:, :D]) * ya + jax.nn.sigmoid(gate[:, D:]) * yb,)


def fn_glu(y, w, b):
    ge = jax.nn.gelu(y)
    return (ge * jax.nn.sigmoid(bdot(ge, w) + b),)


def fn_onorm(o, z, g_on):
    r = _iota2((DH, DNW), 0)
    c = _iota2((DH, DNW), 1)
    expand = (c % DH == r).astype(F32)
    r2 = _iota2((DNW, DNW), 0)
    c2 = _iota2((DNW, DNW), 1)
    avg = (r2 // DH == c2 // DH).astype(F32) * (1.0 / DH)
    ms = _dot_h3(o * o, avg)
    return (o * lax.rsqrt(ms + EPS) * _dot_hi(g_on, expand) * _silu(z),)


def fn_mix_tail(o_pre, z, y_s5, gate, g_on, w_glu, b_glu, wa_t, wb_t):
    (oa,) = fn_onorm(o_pre, z, g_on)
    (ob,) = fn_glu(y_s5, w_glu, b_glu)
    return fn_merge(gate, bdot_nt(oa, wa_t), bdot_nt(ob, wb_t))


def gate_fn(small, alp, dtp):
    beta = jax.nn.sigmoid(small)
    la = -jnp.exp(alp) * jax.nn.softplus(small + dtp)
    tri = (_iota2((CH, CH), 0) >= _iota2((CH, CH), 1)).astype(F32)
    gc = _dot_hi(tri, la)
    gct = lax.dot_general(la, tri, (((0,), (1,)), ((), ())), precision=HI, preferred_element_type=F32)
    return beta, gc, gct


def _bdg(a, b, ca, cb, hi):
    if not hi:
        a, b = a.astype(BF16), b.astype(BF16)
    return lax.dot_general(a, b, (((ca,), (cb,)), ((0,), (0,))), precision=H3 if hi else None,
                           preferred_element_type=F32)


def _batched_matmuls(hi):
    nn_ = lambda a, b: _bdg(a, b, 2, 1, hi)
    nt_ = lambda a, b: _bdg(a, b, 2, 2, hi)
    tn_ = lambda a, b: _bdg(a, b, 1, 1, hi)
    nn = jax.custom_vjp(nn_)
    nn.defvjp(lambda a, b: (nn_(a, b), (a, b)), lambda r, g: (nt_(g, r[1]), tn_(r[0], g)))
    nt = jax.custom_vjp(nt_)
    nt.defvjp(lambda a, b: (nt_(a, b), (a, b)), lambda r, g: (nn_(g, r[1]), tn_(g, r[0])))
    tn = jax.custom_vjp(tn_)
    tn.defvjp(lambda a, b: (tn_(a, b), (a, b)), lambda r, g: (nt_(r[1], g), nn_(r[0], g)))
    return nn, nt, tn


bnn, bnt, btn = _batched_matmuls(False)
hnn, hnt, htn = _batched_matmuls(True)


def _unit_lower_inverse(a):
    r = _iota2((1, CH, CH), 1)
    c = _iota2((1, CH, CH), 2)
    eye = (r == c).astype(F32)
    d = jnp.where(r // 8 == c // 8, a, 0.0)
    inv = eye - d
    p = d
    for _ in range(2):
        p = hnn(p, p)
        inv = inv + hnn(inv, p)
    for blk in (16, 32, 64):
        off = jnp.where((r // blk == c // blk) & (r // (blk // 2) != c // (blk // 2)), a, 0.0)
        inv = inv - hnn(hnn(inv, off), inv)
    return inv


@jax.custom_vjp
def _inverse_given(a, t):
    return t


_inverse_given.defvjp(lambda a, t: (t, t), lambda t, g: (-hnt(htn(t, g), t), jnp.zeros_like(t)))


def dn_prep(xc, w):
    t = xc.shape[0] - 8
    c = xc[5:5 + t] * w[0:1] + xc[6:6 + t] * w[1:2] + xc[7:7 + t] * w[2:3] + xc[8:8 + t] * w[3:4]
    act = _silu(c)
    q, k, v = act[:, :DNW], act[:, DNW:2 * DNW], act[:, 2 * DNW:]
    ones = (_iota2((DNW, DNW), 0) // DH == _iota2((DNW, DNW), 1) // DH).astype(F32)
    q = q * lax.rsqrt(_dot_h3(q * q, ones) + EPS) * (DH ** -0.5)
    k = k * lax.rsqrt(_dot_h3(k * k, ones) + EPS)
    return jnp.concatenate([q, k, v], axis=1)


def dn_chunk(q, k, v, b, g, gt, s_prev, t_saved=None):
    r = _iota2((1, CH, CH), 1)
    c = _iota2((1, CH, CH), 2)
    causal = r >= c
    dec = jnp.where(causal, jnp.exp(jnp.where(causal, g - gt, 0.0)), 0.0)
    kb = k * b
    qk = bnt(jnp.concatenate([q, kb], axis=1), k)
    attn = qk[:, :CH] * dec
    a = jnp.where(r > c, qk[:, CH:] * dec, 0.0)
    tinv = _unit_lower_inverse(a) if t_saved is None else _inverse_given(a, t_saved)
    eg = jnp.exp(g)
    uw = hnn(tinv, jnp.concatenate([v * b, kb * eg], axis=2))
    g_last = g[:, CH - 1:CH]
    ws = bnn(jnp.concatenate([uw[..., DH:], q * eg], axis=1), s_prev)
    v_new = uw[..., :DH] - ws[:, :CH]
    o = ws[:, CH:] + bnn(attn, v_new)
    s_new = s_prev * jnp.exp(g_last) + btn(k * jnp.exp(g_last - g), v_new)
    return o, s_new, tinv


def s5_chunk(u, xp_re, xp_im, bb_re, bb_im, cc_re, cc_im, p0r, p0i, p1r, p1i, pir, pii, dsk):
    nb, ch, _ = u.shape
    u2 = u.reshape(nb * ch, LANES)
    bu_re = bdot(u2, bb_re).reshape(nb, ch, 512)
    bu_im = bdot(u2, bb_im).reshape(nb, ch, 512)
    xt_re = pir * bu_re - pii * bu_im
    xt_im = pir * bu_im + pii * bu_re
    tri = jnp.broadcast_to((_iota2((1, ch, ch), 1) >= _iota2((1, ch, ch), 2)).astype(F32), (nb, ch, ch))
    cs_re = hnn(tri, xt_re)
    cs_im = hnn(tri, xt_im)
    x_re = p0r * cs_re - p0i * cs_im + p1r * xp_re - p1i * xp_im
    x_im = p0r * cs_im + p0i * cs_re + p1r * xp_im + p1i * xp_re
    y = bdot_nt(x_re.reshape(nb * ch, 512), cc_re) - bdot_nt(x_im.reshape(nb * ch, 512), cc_im) + dsk * u2
    return y.reshape(nb, ch, LANES), x_re[:, ch - 1:ch], x_im[:, ch - 1:ch]


def s5_tables(lam_re, lam_im, log_step, bre, bim, cre, cim):
    expand = (_iota2((S5G, S5N), 1) // S5P == _iota2((S5G, S5N), 0)).astype(F32)
    step = _dot_hi(jnp.exp(log_step), expand)
    lre = jnp.minimum(lam_re, -1e-4)
    lr = lre * step
    ang = lam_im * step
    mag = jnp.exp(lr)
    lb_re = mag * jnp.cos(ang)
    lb_im = mag * jnp.sin(ang)
    den = lre * lre + lam_im * lam_im
    coef_re = ((lb_re - 1.0) * lre + lb_im * lam_im) / den
    coef_im = (lb_im * lre - (lb_re - 1.0) * lam_im) / den
    bb_re = coef_re * bre - coef_im * bim
    bb_im = coef_re * bim + coef_im * bre
    j = _iota2((S5_CH, 1), 0).astype(F32)
    jc = j - S5_CH // 2
    e0 = jnp.exp(jc * lr)
    e1 = jnp.exp((j + 1.0) * lr)
    ei = jnp.exp(-jc * lr)
    mask = (_iota2((LANES, 512), 0) // S5C == _iota2((LANES, 512), 1) // S5P).astype(F32)

    def blocks(t):
        return jnp.concatenate([(jnp.tile(t[:, gb * 512:(gb + 1) * 512], (LANES // S5C, 1)) * mask)[None]
                                for gb in range(GB)], axis=0)

    return (blocks(bb_re), blocks(bb_im), blocks(cre), blocks(cim),
            e0 * jnp.cos(jc * ang), e0 * jnp.sin(jc * ang),
            e1 * jnp.cos((j + 1.0) * ang), e1 * jnp.sin((j + 1.0) * ang),
            ei * jnp.cos(jc * ang), -ei * jnp.sin(jc * ang))


def _row_specs(tiled, batch, bcast, tm, tpb):
    specs = [pl.BlockSpec((tm, a.shape[1]), lambda i: (i, 0)) for a in tiled]
    specs += [pl.BlockSpec((None,) + a.shape[1:], lambda i: (i // tpb, 0, 0)) for a in batch]
    specs += [pl.BlockSpec(a.shape, lambda i, nd=a.ndim: (0,) * nd) for a in bcast]
    return specs


def ew_call(name, fn, tiled, batch, bcast, outs, tm, seq):
    t_rows = tiled[0].shape[0]
    n_in = len(tiled) + len(batch) + len(bcast)

    def body(*refs):
        vals = [r[...].astype(F32) for r in refs[:n_in]]
        for r, o in zip(refs[n_in:], fn(*vals)):
            r[...] = o.astype(r.dtype)

    return pl.pallas_call(
        body, grid=(t_rows // tm,), in_specs=_row_specs(tiled, batch, bcast, tm, seq // tm),
        out_specs=[pl.BlockSpec((tm, w), lambda i: (i, 0)) for w, _ in outs],
        out_shape=[SDS((t_rows, w), dt) for w, dt in outs], name=name, compiler_params=_cp(1))(*tiled, *batch, *bcast)


def ew_vjp_call(name, fn, tiled, batch, bcast, cts, want, tm, seq, addend=None):
    t_rows = tiled[0].shape[0]
    tpb = seq // tm
    n_t, n_b, n_c = len(tiled), len(batch), len(bcast)
    n_in = n_t + n_b + n_c
    extra = [] if addend is None else [addend]

    def body(*refs):
        i = pl.program_id(0)
        vals = [r[...].astype(F32) for r in refs[:n_in]]
        ctv = tuple(r[...].astype(F32) for r in refs[n_in:n_in + len(cts)])
        outs = refs[n_in + len(cts) + len(extra):]
        _, vjp = jax.vjp(fn, *vals)
        grads = vjp(ctv)
        for k, (r, (idx, _)) in enumerate(zip(outs[:len(want)], want)):
            g = grads[idx]
            if k == 0 and extra:
                g = g + refs[n_in + len(cts)][...]
            r[...] = g.astype(r.dtype)
        for k in range(n_b):
            r, g = outs[len(want) + k], grads[n_t + k]

            @pl.when(i % tpb == 0)
            def _(r=r, g=g):
                r[...] = g

            @pl.when(i % tpb != 0)
            def _(r=r, g=g):
                r[...] += g
        for k in range(n_c):
            r, g = outs[len(want) + n_b + k], grads[n_t + n_b + k]

            @pl.when(i == 0)
            def _(r=r, g=g):
                r[...] = g

            @pl.when(i != 0)
            def _(r=r, g=g):
                r[...] += g

    out_specs = [pl.BlockSpec((tm, tiled[idx].shape[1]), lambda i: (i, 0)) for idx, _ in want]
    out_specs += [pl.BlockSpec((None,) + a.shape[1:], lambda i: (i // tpb, 0, 0)) for a in batch]
    out_specs += [pl.BlockSpec(a.shape, lambda i, nd=a.ndim: (0,) * nd) for a in bcast]
    out_shape = [SDS(tiled[idx].shape, dt) for idx, dt in want]
    out_shape += [SDS(a.shape, F32) for a in batch] + [SDS(a.shape, F32) for a in bcast]
    res = pl.pallas_call(
        body, grid=(t_rows // tm,),
        in_specs=_row_specs(tiled, batch, bcast, tm, tpb)
        + [pl.BlockSpec((tm, a.shape[1]), lambda i: (i, 0)) for a in list(cts) + extra],
        out_specs=out_specs, out_shape=out_shape, name=name, compiler_params=_cp(1))(*tiled, *batch, *bcast, *cts, *extra)
    return res[:len(want)], res[len(want):len(want) + n_b], res[len(want) + n_b:]


def _pick(n, cands):
    for c in cands:
        if n % c == 0:
            return c
    return n


def mm_tn(name, a, b, exchange=None):
    t_rows, m = a.shape
    n = b.shape[1]
    tn = n if n <= 1024 else _pick(n, (1024, 512, 256, 128))
    tm = max([t for t in range(LANES, m + 1, LANES) if m % t == 0 and t * tn * 4 <= ACC_LIMIT] or [m])
    tk = _pick(t_rows, (512, 256, 128, 64))
    grid = (m // tm, n // tn, t_rows // tk)
    extra = [] if exchange is None else [exchange]

    def body(*refs):
        a_ref, b_ref = refs[:2]
        o_ref, acc = refs[2 + len(extra)], refs[3 + 2 * len(extra)]
        i, j, k = pl.program_id(0), pl.program_id(1), pl.program_id(2)
        if extra:
            start, finish = _exchange_phases(refs[2], refs[4], *refs[6:9])
            pl.when((i == 0) & (j == 0) & (k == 0))(start)

        @pl.when(k == 0)
        def _():
            acc[...] = jnp.zeros_like(acc)

        acc[...] += _dot_tn(a_ref[...], b_ref[...])

        @pl.when(k == grid[2] - 1)
        def _():
            o_ref[...] = acc[...].astype(BF16)

        if extra:
            pl.when((i == grid[0] - 1) & (j == grid[1] - 1) & (k == grid[2] - 1))(finish)

    res = pl.pallas_call(
        body, grid=grid,
        in_specs=[pl.BlockSpec((tk, tm), lambda i, j, k: (k, i)), pl.BlockSpec((tk, tn), lambda i, j, k: (k, j))]
        + [HBM_SPEC] * len(extra),
        out_specs=[pl.BlockSpec((tm, tn), lambda i, j, k: (i, j))] + [HBM_SPEC] * len(extra),
        out_shape=[SDS((m, n), BF16)] + [SDS(x.shape, x.dtype) for x in extra],
        scratch_shapes=[pltpu.VMEM((tm, tn), F32)] + (_comm_scratch() if extra else []), name=name,
        compiler_params=_cp(3))(a, b, *extra)
    return res if extra else res[0]


def _ffn_weight_spec():
    if FFN_TF == FF:
        return pl.BlockSpec((FF, D), lambda i, j: (0, 0), pipeline_mode=pl.Buffered(1))
    return pl.BlockSpec((FFN_TF, D), lambda i, j: (j, 0))


def ffn_fwd(name, h, mod3, g, w1, w3, w2, seq, gather=None):
    t_rows = h.shape[0]
    tm = _pick(seq, (FFN_FWD_TM, 128, 64))
    tf = FFN_TF
    tpb = seq // tm
    nf = FF // tf
    nt = t_rows // tm
    extra = [] if gather is None else [gather]

    def body(*refs):
        h_ref, mod_ref, g_ref, w1_ref, w3_ref, w2_ref = refs[:6]
        ho_ref, f_ref, u_ref, h1_ref, h3_ref = refs[6 + len(extra):11 + len(extra)]
        acc = refs[11 + 2 * len(extra)]
        i, j = pl.program_id(0), pl.program_id(1)
        if extra:
            start, forward, finish = _gather_phases(refs[6], refs[12], *refs[14:17])
            pl.when((i == 0) & (j == 0))(start)
            pl.when((i == nt - 1) & (j == 0))(forward)

        @pl.when(j == 0)
        def _():
            u_ref[...] = normmod(h_ref[...], g_ref[...], mod_ref[1:2, :], mod_ref[0:1, :]).astype(BF16)
            acc[...] = jnp.zeros_like(acc)

        u = u_ref[...]
        h1 = _dot_nt(u, w1_ref[...])
        h3 = _dot_nt(u, w3_ref[...])
        h1_ref[...] = h1.astype(BF16)
        h3_ref[...] = h3.astype(BF16)
        acc[...] += _dot(_silu(h1) * h3, w2_ref[...])

        @pl.when(j == nf - 1)
        def _():
            f_ref[...] = acc[...]
            ho_ref[...] = h_ref[...] + 0.5 * mod_ref[2:3, :] * acc[...]

        if extra:
            pl.when((i == nt - 1) & (j == nf - 1))(finish)

    row = lambda i, j: (i, 0)
    return pl.pallas_call(
        body, grid=(nt, nf),
        in_specs=[pl.BlockSpec((tm, D), row), pl.BlockSpec((None, 3, D), lambda i, j: (i // tpb, 0, 0)),
                  pl.BlockSpec((1, D), lambda i, j: (0, 0)), _ffn_weight_spec(), _ffn_weight_spec(), _ffn_weight_spec()]
        + [HBM_SPEC] * len(extra),
        out_specs=[pl.BlockSpec((tm, D), row), pl.BlockSpec((tm, D), row), pl.BlockSpec((tm, D), row),
                   pl.BlockSpec((tm, tf), lambda i, j: (i, j)), pl.BlockSpec((tm, tf), lambda i, j: (i, j))]
        + [HBM_SPEC] * len(extra),
        out_shape=[SDS((t_rows, D), F32), SDS((t_rows, D), F32), SDS((t_rows, D), BF16), SDS((t_rows, FF), BF16),
                   SDS((t_rows, FF), BF16)] + [SDS((NDEV,) + x.shape, x.dtype) for x in extra],
        scratch_shapes=[pltpu.VMEM((tm, D), F32)] + (_comm_scratch() if extra else []), name=name,
        compiler_params=_cp(2))(h, mod3, g, w1, w3, w2, *extra)


def ffn_bwd(name, dho, h, f_out, h1_in, h3_in, mod3, g, w1, w3, w2, seq, exchange=None):
    t_rows = h.shape[0]
    tm = _pick(seq, (FFN_BWD_TM, 128, 64))
    tf = FFN_TF
    tpb = seq // tm
    nf = FF // tf
    nt = t_rows // tm
    extra = [] if exchange is None else [exchange]

    def body(*refs):
        dho_ref, h_ref, f_ref, h1_ref, h3_ref, mod_ref, g_ref, w1_ref, w3_ref, w2_ref = refs[:10]
        dh_ref, a_ref, dh1_ref, dh3_ref, df_scr, dmod_ref, dg_ref = refs[10 + len(extra):17 + len(extra)]
        du_acc = refs[17 + 2 * len(extra)]
        i, j = pl.program_id(0), pl.program_id(1)
        if extra:
            start, finish = _exchange_phases(refs[10], refs[18], *refs[20:23])
            pl.when((i == 0) & (j == 0))(start)

        @pl.when(j == 0)
        def _():
            df_scr[...] = (0.5 * mod_ref[2:3, :] * dho_ref[...]).astype(BF16)
            du_acc[...] = jnp.zeros_like(du_acc)

        h1 = h1_ref[...].astype(F32)
        h3 = h3_ref[...].astype(F32)
        sg = jax.nn.sigmoid(h1)
        s = h1 * sg
        da = _dot_nt(df_scr[...], w2_ref[...])
        dh3 = (da * s).astype(BF16)
        dh1 = (da * h3 * (sg * (1.0 + h1 * (1.0 - sg)))).astype(BF16)
        a_ref[...] = (s * h3).astype(BF16)
        dh1_ref[...] = dh1
        dh3_ref[...] = dh3
        du_acc[...] += _dot(dh1, w1_ref[...]) + _dot(dh3, w3_ref[...])

        @pl.when(j == nf - 1)
        def _():
            _, vjp = jax.vjp(normmod, h_ref[...], g_ref[...], mod_ref[1:2, :], mod_ref[0:1, :])
            dh_n, dg, dsc, dsh = vjp(du_acc[...])
            dh_ref[...] = dho_ref[...] + dh_n
            dgt = jnp.sum(0.5 * dho_ref[...] * f_ref[...], axis=0, keepdims=True)
            dmod = jnp.concatenate([dsh, dsc, dgt], axis=0)

            @pl.when(i % tpb == 0)
            def _():
                dmod_ref[...] = dmod

            @pl.when(i % tpb != 0)
            def _():
                dmod_ref[...] += dmod

            @pl.when(i == 0)
            def _():
                dg_ref[...] = dg

            @pl.when(i != 0)
            def _():
                dg_ref[...] += dg

        if extra:
            pl.when((i == nt - 1) & (j == nf - 1))(finish)

    row = lambda i, j: (i, 0)
    col = lambda i, j: (i, j)
    return pl.pallas_call(
        body, grid=(nt, nf),
        in_specs=[pl.BlockSpec((tm, D), row), pl.BlockSpec((tm, D), row), pl.BlockSpec((tm, D), row),
                  pl.BlockSpec((tm, tf), col), pl.BlockSpec((tm, tf), col),
                  pl.BlockSpec((None, 3, D), lambda i, j: (i // tpb, 0, 0)),
                  pl.BlockSpec((1, D), lambda i, j: (0, 0)), _ffn_weight_spec(), _ffn_weight_spec(), _ffn_weight_spec()]
        + [HBM_SPEC] * len(extra),
        out_specs=[pl.BlockSpec((tm, D), row), pl.BlockSpec((tm, tf), col), pl.BlockSpec((tm, tf), col),
                   pl.BlockSpec((tm, tf), col), pl.BlockSpec((tm, D), row),
                   pl.BlockSpec((None, 3, D), lambda i, j: (i // tpb, 0, 0)), pl.BlockSpec((1, D), lambda i, j: (0, 0))]
        + [HBM_SPEC] * len(extra),
        out_shape=[SDS((t_rows, D), F32), SDS((t_rows, FF), BF16), SDS((t_rows, FF), BF16), SDS((t_rows, FF), BF16),
                   SDS((t_rows, D), BF16), SDS(mod3.shape, F32), SDS((1, D), F32)] + [SDS(x.shape, x.dtype) for x in extra],
        scratch_shapes=[pltpu.VMEM((tm, D), F32)] + (_comm_scratch() if extra else []), name=name,
        compiler_params=_cp(2))(dho, h, f_out, h1_in, h3_in, mod3, g, w1, w3, w2, *extra)


def _resident(shape):
    return pl.BlockSpec(shape, lambda i: (0,) * len(shape), pipeline_mode=pl.Buffered(1))


def mix_in_fwd(h, sh, sc, g, ws, seq):
    t_rows = h.shape[0]
    tm = _pick(seq, (256, 128, 64))
    tpb = seq // tm
    nw = len(ws)

    def body(h_ref, sh_ref, sc_ref, g_ref, *rest):
        u = normmod(h_ref[...], g_ref[...], sc_ref[...], sh_ref[...]).astype(BF16)
        rest[nw][...] = u
        for w_ref, p_ref in zip(rest[:nw], rest[nw + 1:]):
            p_ref[...] = _dot_nt(u, w_ref[...])

    row = lambda i: (i, 0)
    batch = pl.BlockSpec((None, 1, D), lambda i: (i // tpb, 0, 0))
    return pl.pallas_call(
        body, grid=(t_rows // tm,),
        in_specs=[pl.BlockSpec((tm, D), row), batch, batch, pl.BlockSpec((1, D), lambda i: (0, 0))]
        + [_resident(w.shape) for w in ws],
        out_specs=[pl.BlockSpec((tm, D), row)] + [pl.BlockSpec((tm, w.shape[0]), row) for w in ws],
        out_shape=[SDS((t_rows, D), BF16)] + [SDS((t_rows, w.shape[0]), F32) for w in ws], name="mix_in_fwd",
        compiler_params=_cp(1))(h, sh, sc, g, *ws)


def mix_in_bwd(dps, ws, h, sh, sc, g, dh_add, seq):
    t_rows = h.shape[0]
    tm = _pick(seq, (256, 128, 64))
    tpb = seq // tm
    nw = len(ws)

    def body(*refs):
        h_ref, sh_ref, sc_ref, g_ref, add_ref, dh_ref, dsh_ref, dsc_ref, dg_ref = refs[2 * nw:]
        i = pl.program_id(0)
        du = _dot(refs[0][...], refs[nw][...])
        for k in range(1, nw):
            du = du + _dot(refs[k][...], refs[nw + k][...])
        _, vjp = jax.vjp(normmod, h_ref[...], g_ref[...], sc_ref[...], sh_ref[...])
        dh_n, dg, dsc, dsh = vjp(du)
        dh_ref[...] = add_ref[...] + dh_n

        @pl.when(i % tpb == 0)
        def _():
            dsh_ref[...] = dsh
            dsc_ref[...] = dsc

        @pl.when(i % tpb != 0)
        def _():
            dsh_ref[...] += dsh
            dsc_ref[...] += dsc

        @pl.when(i == 0)
        def _():
            dg_ref[...] = dg

        @pl.when(i != 0)
        def _():
            dg_ref[...] += dg

    row = lambda i: (i, 0)
    batch = pl.BlockSpec((None, 1, D), lambda i: (i // tpb, 0, 0))
    gain = pl.BlockSpec((1, D), lambda i: (0, 0))
    return pl.pallas_call(
        body, grid=(t_rows // tm,),
        in_specs=[pl.BlockSpec((tm, dp.shape[1]), row) for dp in dps] + [_resident(w.shape) for w in ws]
        + [pl.BlockSpec((tm, D), row), batch, batch, gain, pl.BlockSpec((tm, D), row)],
        out_specs=[pl.BlockSpec((tm, D), row), batch, batch, gain],
        out_shape=[SDS((t_rows, D), F32), SDS(sh.shape, F32), SDS(sc.shape, F32), SDS((1, D), F32)], name="mix_in_bwd",
        compiler_params=_cp(1))(*dps, *ws, h, sh, sc, g, dh_add)


def mix_out_fwd(merged, w_out, h_prev, gt, seq):
    t_rows = merged.shape[0]
    tm = _pick(seq, (256, 128, 64))
    tpb = seq // tm

    def body(m_ref, w_ref, h_ref, gt_ref, mo_ref, ho_ref):
        mo = _dot(m_ref[...], w_ref[...])
        mo_ref[...] = mo
        ho_ref[...] = h_ref[...] + gt_ref[...] * mo

    row = lambda i: (i, 0)
    return pl.pallas_call(
        body, grid=(t_rows // tm,),
        in_specs=[pl.BlockSpec((tm, D), row), _resident(w_out.shape), pl.BlockSpec((tm, D), row),
                  pl.BlockSpec((None, 1, D), lambda i: (i // tpb, 0, 0))],
        out_specs=[pl.BlockSpec((tm, D), row), pl.BlockSpec((tm, D), row)],
        out_shape=[SDS((t_rows, D), F32), SDS((t_rows, D), F32)], name="mix_out_fwd",
        compiler_params=_cp(1))(merged, w_out, h_prev, gt)


def mix_out_bwd(dh, mo, w_out, gt, seq):
    t_rows = dh.shape[0]
    tm = _pick(seq, (256, 128, 64))
    tpb = seq // tm

    def body(dh_ref, mo_ref, w_ref, gt_ref, dmo_ref, dm_ref, dgt_ref):
        i = pl.program_id(0)
        dmo = (gt_ref[...] * dh_ref[...]).astype(BF16)
        dmo_ref[...] = dmo
        dm_ref[...] = _dot_nt(dmo, w_ref[...])
        dgt = jnp.sum(dh_ref[...] * mo_ref[...], axis=0, keepdims=True)

        @pl.when(i % tpb == 0)
        def _():
            dgt_ref[...] = dgt

        @pl.when(i % tpb != 0)
        def _():
            dgt_ref[...] += dgt

    row = lambda i: (i, 0)
    batch = pl.BlockSpec((None, 1, D), lambda i: (i // tpb, 0, 0))
    return pl.pallas_call(
        body, grid=(t_rows // tm,),
        in_specs=[pl.BlockSpec((tm, D), row), pl.BlockSpec((tm, D), row), _resident(w_out.shape), batch],
        out_specs=[pl.BlockSpec((tm, D), row), pl.BlockSpec((tm, D), row), batch],
        out_shape=[SDS((t_rows, D), BF16), SDS((t_rows, D), F32), SDS(gt.shape, F32)], name="mix_out_bwd",
        compiler_params=_cp(1))(dh, mo, w_out, gt)


def _dn_cols(part, hd):
    return slice(part * DNW + hd * DH, part * DNW + (hd + 1) * DH)


def _qkv_stacks(qkv_ref, nb):
    pairs = [(b, hd) for b in range(nb) for hd in range(NH)]
    return [jnp.stack([qkv_ref[b, :, _dn_cols(part, hd)] for b, hd in pairs]) for part in range(3)]


def dn_prep_fwd(p_dn, conv8):
    bl, seq, _ = p_dn.shape
    tp = _pick(seq, (256, 128, 64))

    def body(raw_ref, halo_ref, conv_ref, o_ref):
        hm = (pl.program_id(1) > 0).astype(F32)
        o_ref[...] = dn_prep(jnp.concatenate([halo_ref[...] * hm, raw_ref[...]], axis=0), conv_ref[...])

    return pl.pallas_call(
        body, grid=(bl, seq // tp),
        in_specs=[pl.BlockSpec((None, tp, 3 * DNW), lambda b, i: (b, i, 0)),
                  pl.BlockSpec((None, 8, 3 * DNW), lambda b, i: (b, jnp.maximum(i * (tp // 8) - 1, 0), 0)),
                  pl.BlockSpec((8, 3 * DNW), lambda b, i: (0, 0))],
        out_specs=pl.BlockSpec((None, tp, 3 * DNW), lambda b, i: (b, i, 0)),
        out_shape=SDS((bl, seq, 3 * DNW), F32), name="dn_prep_fwd", compiler_params=_cp(2))(p_dn, p_dn, conv8)


def dn_prep_bwd(p_dn, conv8, d_qkv, d_z):
    bl, seq, _ = p_dn.shape
    tp = _pick(seq, (256, 128, 64))
    nt = seq // tp

    def body(raw_ref, halo_ref, conv_ref, dq_ref, dz_ref, draw_ref, dconv_ref, carry):
        b, r = pl.program_id(0), pl.program_id(1)

        @pl.when((b == 0) & (r == 0))
        def _():
            dconv_ref[...] = jnp.zeros_like(dconv_ref)

        @pl.when(r == 0)
        def _():
            carry[...] = jnp.zeros_like(carry)

        hm = (r < nt - 1).astype(F32)
        _, vjp = jax.vjp(dn_prep, jnp.concatenate([halo_ref[...] * hm, raw_ref[...]], axis=0), conv_ref[...])
        dxc, dw = vjp(dq_ref[...])
        tail = dxc[tp:tp + 8] + carry[...]
        draw_ref[:, 0:3 * DNW] = jnp.concatenate([dxc[8:tp], tail], axis=0).astype(BF16)
        draw_ref[:, 3 * DNW:4 * DNW] = dz_ref[...].astype(BF16)
        carry[...] = dxc[0:8] * hm
        dconv_ref[...] += dw

    blk = lambda b, r: (b, nt - 1 - r, 0)
    return pl.pallas_call(
        body, grid=(bl, nt),
        in_specs=[pl.BlockSpec((None, tp, 3 * DNW), blk),
                  pl.BlockSpec((None, 8, 3 * DNW), lambda b, r: (b, jnp.maximum((nt - 1 - r) * (tp // 8) - 1, 0), 0)),
                  pl.BlockSpec((8, 3 * DNW), lambda b, r: (0, 0)), pl.BlockSpec((None, tp, 3 * DNW), blk),
                  pl.BlockSpec((None, tp, DNW), blk)],
        out_specs=[pl.BlockSpec((None, tp, 4 * DNW), blk), pl.BlockSpec((8, 3 * DNW), lambda b, r: (0, 0))],
        out_shape=[SDS((bl, seq, 4 * DNW), BF16), SDS((8, 3 * DNW), F32)],
        scratch_shapes=[pltpu.VMEM((8, 3 * DNW), F32)], name="dn_prep_bwd", compiler_params=_cp(2))(p_dn, p_dn, conv8, d_qkv, d_z)


def _gate_stacks(gates, nb):
    pairs = [(b, hd) for b in range(nb) for hd in range(NH)]
    bs = jnp.stack([gates[b][0][:, hd:hd + 1] for b, hd in pairs])
    gs = jnp.stack([gates[b][1][:, NH + hd:NH + hd + 1] for b, hd in pairs])
    gts = jnp.stack([gates[b][2][NH + hd:NH + hd + 1, :] for b, hd in pairs])
    return bs, gs, gts


def deltanet_fwd(qkv, p_small, alp, dtp, nb):
    bl, seq, _ = qkv.shape
    nc = seq // CH
    ng = nb * NH

    def body(qkv_ref, small_ref, alp_ref, dtp_ref, o_ref, sprev_ref, tinv_ref, s_scr):
        @pl.when(pl.program_id(1) == 0)
        def _():
            s_scr[...] = jnp.zeros_like(s_scr)

        gates = [gate_fn(small_ref[b], alp_ref[...], dtp_ref[...]) for b in range(nb)]
        s_prev = s_scr[...]
        o, s_new, tinv = dn_chunk(*_qkv_stacks(qkv_ref, nb), *_gate_stacks(gates, nb), s_prev)
        sprev_ref[...] = s_prev
        tinv_ref[...] = tinv
        s_scr[...] = s_new
        for b in range(nb):
            for hd in range(NH):
                o_ref[b, :, hd * DH:(hd + 1) * DH] = o[b * NH + hd]

    blk = lambda bb, n: (bb, n, 0)
    const = lambda bb, n: (0, 0)
    saved = pl.BlockSpec((None, ng, DH, DH), lambda bb, n: (bb * nc + n, 0, 0, 0))
    return pl.pallas_call(
        body, grid=(bl // nb, nc),
        in_specs=[pl.BlockSpec((nb, CH, 3 * DNW), blk), pl.BlockSpec((nb, CH, LANES), blk),
                  pl.BlockSpec((1, LANES), const), pl.BlockSpec((1, LANES), const)],
        out_specs=[pl.BlockSpec((nb, CH, DNW), blk), saved, saved],
        out_shape=[SDS((bl, seq, DNW), F32), SDS((bl // nb * nc, ng, DH, DH), F32), SDS((bl // nb * nc, ng, DH, DH), F32)],
        scratch_shapes=[pltpu.VMEM((ng, DH, DH), F32)], name="deltanet_fwd",
        compiler_params=_cp(2))(qkv, p_small, alp, dtp)


def deltanet_bwd(qkv, p_small, alp, dtp, sprev, tinv, d_o, nb, exchange=None):
    bl, seq, _ = qkv.shape
    nc = seq // CH
    ng = nb * NH
    extra = [] if exchange is None else [exchange]

    def body(*refs):
        qkv_ref, small_ref, alp_ref, dtp_ref, sprev_ref, tinv_ref, do_ref = refs[:7]
        dqkv_ref, dsmall_ref, dalp_ref, ddtp_ref = refs[7 + len(extra):11 + len(extra)]
        ds_scr = refs[11 + 2 * len(extra)]
        bb, r = pl.program_id(0), pl.program_id(1)
        if extra:
            start, finish = _exchange_phases(refs[7], refs[12], *refs[14:17])
            pl.when((bb == 0) & (r == 0))(start)

        @pl.when((bb == 0) & (r == 0))
        def _():
            dalp_ref[...] = jnp.zeros_like(dalp_ref)
            ddtp_ref[...] = jnp.zeros_like(ddtp_ref)

        @pl.when(r == 0)
        def _():
            ds_scr[...] = jnp.zeros_like(ds_scr)

        gates, gate_vjps = [], []
        for b in range(nb):
            out, gvjp = jax.vjp(gate_fn, small_ref[b], alp_ref[...], dtp_ref[...])
            gates.append(out)
            gate_vjps.append(gvjp)
        t_saved = tinv_ref[...]
        _, vjp = jax.vjp(lambda *args: dn_chunk(*args, t_saved)[:2], *_qkv_stacks(qkv_ref, nb), *_gate_stacks(gates, nb),
                         sprev_ref[...])
        d_out = jnp.stack([do_ref[b, :, hd * DH:(hd + 1) * DH] for b in range(nb) for hd in range(NH)])
        grads = vjp((d_out, ds_scr[...]))
        ds_scr[...] = grads[6]
        lane = _iota2((CH, LANES), 1)
        rowi = _iota2((LANES, CH), 0)
        for b in range(nb):
            d_beta = jnp.zeros((CH, LANES), F32)
            d_gc = jnp.zeros((CH, LANES), F32)
            d_gct = jnp.zeros((LANES, CH), F32)
            for hd in range(NH):
                i = b * NH + hd
                for part in range(3):
                    dqkv_ref[b, :, _dn_cols(part, hd)] = grads[part][i]
                d_beta = d_beta + jnp.where(lane == hd, grads[3][i], 0.0)
                d_gc = d_gc + jnp.where(lane == NH + hd, grads[4][i], 0.0)
                d_gct = d_gct + jnp.where(rowi == NH + hd, grads[5][i], 0.0)
            d_small, d_alp, d_dtp = gate_vjps[b]((d_beta, d_gc, d_gct))
            dsmall_ref[b] = d_small.astype(BF16)
            dalp_ref[...] += d_alp
            ddtp_ref[...] += d_dtp
        if extra:
            pl.when((bb == bl // nb - 1) & (r == nc - 1))(finish)

    blk = lambda bb, r: (bb, nc - 1 - r, 0)
    const = lambda bb, r: (0, 0)
    saved = pl.BlockSpec((None, ng, DH, DH), lambda bb, r: (bb * nc + nc - 1 - r, 0, 0, 0))
    return pl.pallas_call(
        body, grid=(bl // nb, nc),
        in_specs=[pl.BlockSpec((nb, CH, 3 * DNW), blk), pl.BlockSpec((nb, CH, LANES), blk), pl.BlockSpec((1, LANES), const),
                  pl.BlockSpec((1, LANES), const), saved, saved, pl.BlockSpec((nb, CH, DNW), blk)] + [HBM_SPEC] * len(extra),
        out_specs=[pl.BlockSpec((nb, CH, 3 * DNW), blk), pl.BlockSpec((nb, CH, LANES), blk), pl.BlockSpec((1, LANES), const),
                   pl.BlockSpec((1, LANES), const)] + [HBM_SPEC] * len(extra),
        out_shape=[SDS((bl, seq, 3 * DNW), F32), SDS((bl, seq, LANES), BF16), SDS((1, LANES), F32), SDS((1, LANES), F32)]
        + [SDS(x.shape, x.dtype) for x in extra],
        scratch_shapes=[pltpu.VMEM((ng, DH, DH), F32)] + (_comm_scratch() if extra else []), name="deltanet_bwd",
        compiler_params=_cp(2))(qkv, p_small, alp, dtp, sprev, tinv, d_o, *extra)


def _s5_table_specs():
    tab3 = pl.BlockSpec((None, LANES, 512), lambda gb, n: (gb, 0, 0))
    tab2 = pl.BlockSpec((S5_CH, 512), lambda gb, n: (0, gb))
    return [tab3] * 4 + [tab2] * 6 + [pl.BlockSpec((1, LANES), lambda gb, n: (0, gb))]


def s5_fwd(u, tables, dsk):
    bl, seq, _ = u.shape
    nc = seq // S5_CH

    def body(u_ref, *rest):
        tabs, (y_ref, xs_ref, xr_scr, xi_scr) = rest[:11], rest[11:]

        @pl.when(pl.program_id(1) == 0)
        def _():
            xr_scr[...] = jnp.zeros_like(xr_scr)
            xi_scr[...] = jnp.zeros_like(xi_scr)

        xp_re, xp_im = xr_scr[...], xi_scr[...]
        xs_ref[0:bl] = xp_re
        xs_ref[bl:2 * bl] = xp_im
        y, xn_re, xn_im = s5_chunk(u_ref[...], xp_re, xp_im, *[t[...] for t in tabs])
        y_ref[...] = y
        xr_scr[...] = xn_re
        xi_scr[...] = xn_im

    blk = lambda gb, n: (0, n, gb)
    return pl.pallas_call(
        body, grid=(GB, nc), in_specs=[pl.BlockSpec((bl, S5_CH, LANES), blk)] + _s5_table_specs(),
        out_specs=[pl.BlockSpec((bl, S5_CH, LANES), blk),
                   pl.BlockSpec((None, 2 * bl, 1, 512), lambda gb, n: (gb * nc + n, 0, 0, 0))],
        out_shape=[SDS((bl, seq, S5W), F32), SDS((GB * nc, 2 * bl, 1, 512), F32)],
        scratch_shapes=[pltpu.VMEM((bl, 1, 512), F32), pltpu.VMEM((bl, 1, 512), F32)], name="s5_fwd",
        compiler_params=_cp(2))(u, *tables, dsk)


def s5_bwd(u, tables, dsk, xs, dy):
    bl, seq, _ = u.shape
    nc = seq // S5_CH

    def body(u_ref, *rest):
        tabs, xs_ref, dy_ref = rest[:11], rest[11], rest[12]
        du_ref, dtabs, dxr_scr, dxi_scr = rest[13], rest[14:25], rest[25], rest[26]
        r = pl.program_id(1)

        @pl.when(r == 0)
        def _():
            for t in dtabs:
                t[...] = jnp.zeros_like(t)
            dxr_scr[...] = jnp.zeros_like(dxr_scr)
            dxi_scr[...] = jnp.zeros_like(dxi_scr)

        _, vjp = jax.vjp(s5_chunk, u_ref[...], xs_ref[0:bl], xs_ref[bl:2 * bl], *[t[...] for t in tabs])
        grads = vjp((dy_ref[...], dxr_scr[...], dxi_scr[...]))
        du_ref[...] = grads[0].astype(BF16)
        dxr_scr[...] = grads[1]
        dxi_scr[...] = grads[2]
        for t, g in zip(dtabs, grads[3:]):
            t[...] += g

    blk = lambda gb, r: (0, nc - 1 - r, gb)
    tab_shapes = [SDS(t.shape, F32) for t in tables] + [SDS(dsk.shape, F32)]
    return pl.pallas_call(
        body, grid=(GB, nc),
        in_specs=[pl.BlockSpec((bl, S5_CH, LANES), blk)] + _s5_table_specs()
        + [pl.BlockSpec((None, 2 * bl, 1, 512), lambda gb, r: (gb * nc + nc - 1 - r, 0, 0, 0)), pl.BlockSpec((bl, S5_CH, LANES), blk)],
        out_specs=[pl.BlockSpec((bl, S5_CH, LANES), blk)] + _s5_table_specs(),
        out_shape=[SDS((bl, seq, S5W), BF16)] + tab_shapes,
        scratch_shapes=[pltpu.VMEM((bl, 1, 512), F32), pltpu.VMEM((bl, 1, 512), F32)], name="s5_bwd",
        compiler_params=_cp(2))(u, *tables, dsk, xs, dy)


def s5_tables_fwd(params):
    shapes = [SDS((GB, LANES, 512), F32)] * 4 + [SDS((S5_CH, S5N), F32)] * 6

    def body(*refs):
        for r, t in zip(refs[7:], s5_tables(*[p[...] for p in refs[:7]])):
            r[...] = t

    return pl.pallas_call(body, out_shape=shapes, name="s5_tables_fwd", compiler_params=_cp())(*params)


def s5_tables_bwd(params, dtables):
    def body(*refs):
        _, vjp = jax.vjp(s5_tables, *[p[...] for p in refs[:7]])
        for r, g in zip(refs[17:], vjp(tuple(t[...] for t in refs[7:17]))):
            r[...] = g

    return pl.pallas_call(body, out_shape=[SDS(p.shape, F32) for p in params], name="s5_tables_bwd",
                          compiler_params=_cp())(*params, *dtables)


def ada_fwd(c_all, w_loc, b_loc):
    def body(c_ref, w_ref, b_ref, o_ref):
        o_ref[...] = _dot(_silu(c_ref[...]), w_ref[...]) + b_ref[...]

    return pl.pallas_call(body, out_shape=SDS((c_all.shape[0], w_loc.shape[1]), F32), name="ada_fwd",
                          compiler_params=_cp())(c_all, w_loc, b_loc)


def ada_bwd(c_all, dmod_mine, dmod_all):
    def body(c_ref, dm_ref, da_ref, gw_ref, gb_ref):
        gw_ref[...] = _dot_tn(_silu(c_ref[...]), dm_ref[...])
        gb_ref[...] = jnp.sum(da_ref[...], axis=0, keepdims=True)

    return pl.pallas_call(body, out_shape=[SDS((D, dmod_mine.shape[1]), F32), SDS((1, dmod_all.shape[1]), F32)],
                          name="ada_bwd", compiler_params=_cp())(c_all, dmod_mine, dmod_all)


def loss_head(h, tgt, g, seq):
    t_rows = h.shape[0]
    tm = _pick(seq, (256, 128, 64))

    def body(h_ref, t_ref, g_ref, dh_ref, dg_ref, loss_ref):
        i = pl.program_id(0)
        y, vjp = jax.vjp(lambda hh, gg: hh * lax.rsqrt(jnp.mean(hh * hh, axis=-1, keepdims=True) + EPS) * gg,
                         h_ref[...], g_ref[...])
        e = y - t_ref[...]
        dh, dg = vjp(e * (1.0 / D))
        part = jnp.sum(jnp.sum(e * e, axis=1, keepdims=True), axis=0, keepdims=True) * (0.5 / D) + jnp.zeros((1, LANES), F32)
        dh_ref[...] = dh

        @pl.when(i == 0)
        def _():
            dg_ref[...] = dg
            loss_ref[...] = part

        @pl.when(i != 0)
        def _():
            dg_ref[...] += dg
            loss_ref[...] += part

    row = lambda i: (i, 0)
    const = lambda i: (0, 0)
    return pl.pallas_call(
        body, grid=(t_rows // tm,),
        in_specs=[pl.BlockSpec((tm, D), row), pl.BlockSpec((tm, D), row), pl.BlockSpec((1, D), const)],
        out_specs=[pl.BlockSpec((tm, D), row), pl.BlockSpec((1, D), const), pl.BlockSpec((1, LANES), const)],
        out_shape=[SDS((t_rows, D), F32), SDS((1, D), F32), SDS((1, LANES), F32)], name="loss_head",
        compiler_params=_cp(1))(h, tgt, g)


def adamw(name, parts, w, m, v):
    k_parts, rows, cols = parts.shape
    tr = _pick(rows, (256, 128, 64, 32, 16, 8))

    def body(p_ref, w_ref, m_ref, v_ref, g_ref, d_ref, mo_ref, vo_ref):
        g = p_ref[0].astype(F32)
        for k in range(1, k_parts):
            g = g + p_ref[k].astype(F32)
        _adam_store(g, w_ref, m_ref, v_ref, g_ref, d_ref, mo_ref, vo_ref)

    blk = pl.BlockSpec((tr, cols), lambda i: (i, 0))
    return pl.pallas_call(
        body, grid=(rows // tr,), in_specs=[pl.BlockSpec((k_parts, tr, cols), lambda i: (0, i, 0)), blk, blk, blk],
        out_specs=[blk] * 4, out_shape=[SDS((rows, cols), F32)] * 4, name=name, compiler_params=_cp(1))(parts, w, m, v)


def _adam_store(g, w_ref, m_ref, v_ref, g_ref, d_ref, mo_ref, vo_ref):
    m_new = ADAM_B1 * m_ref[...] + (1.0 - ADAM_B1) * g
    v_new = ADAM_B2 * v_ref[...] + (1.0 - ADAM_B2) * (g * g)
    m_hat = m_new / (1.0 - ADAM_B1 ** ADAM_STEP)
    v_hat = v_new / (1.0 - ADAM_B2 ** ADAM_STEP)
    g_ref[...] = g
    d_ref[...] = -ADAM_LR * (m_hat / (jnp.sqrt(v_hat) + ADAM_EPS) + ADAM_WD * w_ref[...])
    mo_ref[...] = m_new
    vo_ref[...] = v_new


def adamw_t(name, parts, w, m, v):
    k_parts, r, c = parts.shape
    tc = _pick(c, (256, 128))

    def body(p_ref, w_ref, m_ref, v_ref, g_ref, d_ref, mo_ref, vo_ref):
        gt = p_ref[0].astype(F32)
        for k in range(1, k_parts):
            gt = gt + p_ref[k].astype(F32)
        _adam_store(gt.T, w_ref, m_ref, v_ref, g_ref, d_ref, mo_ref, vo_ref)

    blk = pl.BlockSpec((tc, r), lambda j: (j, 0))
    return pl.pallas_call(
        body, grid=(c // tc,), in_specs=[pl.BlockSpec((k_parts, r, tc), lambda j: (0, 0, j)), blk, blk, blk],
        out_specs=[blk] * 4, out_shape=[SDS((c, r), F32)] * 4, name=name, compiler_params=_cp(1))(parts, w, m, v)


def _comm_scratch():
    return [pltpu.SemaphoreType.DMA((7,)), pltpu.SemaphoreType.DMA((7,)), pltpu.SemaphoreType.DMA]


HBM_SPEC = pl.BlockSpec(memory_space=pl.ANY)


def _gather_phases(x_ref, out_ref, send_sems, recv_sems, local_sem):
    mx, my, mc = lax.axis_index("x"), lax.axis_index("y"), lax.axis_index("c")
    me, sibling = (mx, my, mc), (mx, my, 1 - mc)
    chips = [(1 - mx, my), (mx, 1 - my), (1 - mx, 1 - my)]

    def slot(px, py, pc):
        return out_ref.at[4 * px + 2 * py + pc]

    def copy(k, block, to, src=None):
        return pltpu.make_async_remote_copy(
            src_ref=slot(*block) if src is None else src, dst_ref=slot(*block), send_sem=send_sems.at[k],
            recv_sem=recv_sems.at[k], device_id=to, device_id_type=pl.DeviceIdType.MESH)

    def first():
        return [copy(0, me, sibling, src=x_ref)] + [copy(1 + j, me, (*chip, mc), src=x_ref) for j, chip in enumerate(chips)]

    def passed():
        return [copy(4 + j, (*chip, mc), sibling) for j, chip in enumerate(chips)]

    def start():
        pltpu.make_async_copy(x_ref, slot(*me), local_sem).start()
        for cp in first():
            cp.start()

    def forward():
        for j, chip in enumerate(chips):
            copy(1 + j, (*chip, mc), me).wait_recv()
            passed()[j].start()

    def finish():
        copy(0, sibling, me).wait_recv()
        for j, chip in enumerate(chips):
            copy(4 + j, (*chip, 1 - mc), me).wait_recv()
        for cp in first() + passed():
            cp.wait_send()
        pltpu.make_async_copy(x_ref, slot(*me), local_sem).wait()

    return start, forward, finish


def _exchange_phases(x_ref, out_ref, send_sems, recv_sems, local_sem):
    mx, my, mc = lax.axis_index("x"), lax.axis_index("y"), lax.axis_index("c")
    me = 4 * mx + 2 * my + mc

    def peer(k):
        return mx ^ (k >> 2), my ^ ((k >> 1) & 1), mc ^ (k & 1)

    def sends():
        out = []
        for k in range(1, NDEV):
            px, py, pc = peer(k)
            out.append(pltpu.make_async_remote_copy(
                src_ref=x_ref.at[4 * px + 2 * py + pc], dst_ref=out_ref.at[me], send_sem=send_sems.at[k - 1],
                recv_sem=recv_sems.at[k - 1], device_id=(px, py, pc), device_id_type=pl.DeviceIdType.MESH))
        return out

    def start():
        pltpu.make_async_copy(x_ref.at[me], out_ref.at[me], local_sem).start()
        for cp in sends():
            cp.start()

    def finish():
        for k in range(1, NDEV):
            px, py, pc = peer(k)
            pltpu.make_async_remote_copy(
                src_ref=x_ref.at[me], dst_ref=out_ref.at[4 * px + 2 * py + pc], send_sem=send_sems.at[k - 1],
                recv_sem=recv_sems.at[k - 1], device_id=(px, py, pc), device_id_type=pl.DeviceIdType.MESH).wait_recv()
        for cp in sends():
            cp.wait_send()
        pltpu.make_async_copy(x_ref.at[me], out_ref.at[me], local_sem).wait()

    return start, finish


def all_gather(name, x):
    def body(x_ref, out_ref, send_sems, recv_sems, local_sem):
        for phase in _gather_phases(x_ref, out_ref, send_sems, recv_sems, local_sem):
            phase()

    return pl.pallas_call(body, out_shape=SDS((NDEV,) + x.shape, x.dtype), in_specs=[HBM_SPEC], out_specs=HBM_SPEC,
                          scratch_shapes=_comm_scratch(), name=name)(x)


def all_gather_pair(name, x1, x2):
    def body(x1_ref, x2_ref, o1_ref, o2_ref, *sems):
        first = _gather_phases(x1_ref, o1_ref, *sems[:3])
        second = _gather_phases(x2_ref, o2_ref, *sems[3:])
        for phase1, phase2 in zip(first, second):
            phase1()
            phase2()

    return pl.pallas_call(
        body, out_shape=[SDS((NDEV,) + x1.shape, x1.dtype), SDS((NDEV,) + x2.shape, x2.dtype)], in_specs=[HBM_SPEC] * 2,
        out_specs=[HBM_SPEC] * 2, scratch_shapes=_comm_scratch() + _comm_scratch(), name=name)(x1, x2)


def gather_with_exchange(name, xg, xe):
    def body(g_ref, e_ref, go_ref, eo_ref, *sems):
        g_start, g_forward, g_finish = _gather_phases(g_ref, go_ref, *sems[:3])
        e_start, e_finish = _exchange_phases(e_ref, eo_ref, *sems[3:])
        e_start()
        g_start()
        g_forward()
        g_finish()
        e_finish()

    return pl.pallas_call(
        body, out_shape=[SDS((NDEV,) + xg.shape, xg.dtype), SDS(xe.shape, xe.dtype)], in_specs=[HBM_SPEC] * 2,
        out_specs=[HBM_SPEC] * 2, scratch_shapes=_comm_scratch() + _comm_scratch(), name=name)(xg, xe)


def _pack(arrs, dtype, row_mult=8):
    segs = []
    for a in arrs:
        flat = a.reshape(-1).astype(dtype)
        segs.append(jnp.pad(flat, (0, (-flat.shape[0]) % ROW)))
    flat = jnp.concatenate(segs)
    flat = jnp.pad(flat, (0, (-flat.shape[0]) % (ROW * row_mult)))
    return flat.reshape(-1, ROW)


def _unpack(buf, shapes):
    flat = buf.reshape(-1)
    out, off = [], 0
    for s in shapes:
        n = math.prod(s)
        out.append(flat[off:off + n].reshape(s))
        off += n + (-n) % ROW
    return out


def _pack_rows(arrs, axis):
    padded = []
    for t in arrs:
        pad = [(0, 0)] * t.ndim
        pad[axis] = (0, _tile_rows(t.shape[axis]) - t.shape[axis])
        padded.append(jnp.pad(t, pad))
    return jnp.concatenate(padded, axis=axis)


def _tile_rows(r):
    return r + (-r) % BF16_TILE_ROWS


def _unpack8(buf, shapes):
    flat = buf.reshape(NDEV, -1)
    out, off = [], 0
    for s in shapes:
        n = math.prod(s)
        out.append(flat[:, off:off + n].reshape((NDEV,) + tuple(s)))
        off += n + (-n) % ROW
    return out


def kernel(x, c, w_ada, b_ada, g_ffn1, w1_ffn1, w3_ffn1, w2_ffn1, g_mix, w_in, conv_qkv, a_log, dt_bias, g_onorm, lam_re, lam_im, log_step, b_re, b_im, c_re, c_im, d_skip, w_glu, b_glu, w_proj_a, w_proj_b, w_out, g_ffn2, w1_ffn2, w3_ffn2, w2_ffn2, g_final, loss_target, m_w_ada, m_b_ada, m_g_ffn1, m_w1_ffn1, m_w3_ffn1, m_w2_ffn1, m_g_mix, m_w_in, m_conv_qkv, m_a_log, m_dt_bias, m_g_onorm, m_lam_re, m_lam_im, m_log_step, m_b_re, m_b_im, m_c_re, m_c_im, m_d_skip, m_w_glu, m_b_glu, m_w_proj_a, m_w_proj_b, m_w_out, m_g_ffn2, m_w1_ffn2, m_w3_ffn2, m_w2_ffn2, m_g_final, v_w_ada, v_b_ada, v_g_ffn1, v_w1_ffn1, v_w3_ffn1, v_w2_ffn1, v_g_mix, v_w_in, v_conv_qkv, v_a_log, v_dt_bias, v_g_onorm, v_lam_re, v_lam_im, v_log_step, v_b_re, v_b_im, v_c_re, v_c_im, v_d_skip, v_w_glu, v_b_glu, v_w_proj_a, v_w_proj_b, v_w_out, v_g_ffn2, v_w1_ffn2, v_w3_ffn2, v_w2_ffn2, v_g_final):
    a = dict(locals())
    bl, seq, _ = x.shape
    t_rows = bl * seq
    me = 4 * lax.axis_index("x") + 2 * lax.axis_index("y") + lax.axis_index("c")
    tm_ew = _pick(seq, (256, 128, 64))

    loc = {n: (a[n][0].T if n in COL_SHARDED else a[n][0]) for n in RS_WEIGHTS}
    wfull, gw, res = {}, {}, {}

    def pack_local(names):
        return _pack_rows([loc[n].astype(BF16).reshape(-1, ROW) for n in names], 0)

    def unpack_full(buf, names):
        r0 = 0
        for n in names:
            r = loc[n].size // ROW
            wfull[n] = buf[:, r0:r0 + r, :].reshape(-1, loc[n].shape[1])
            r0 += _tile_rows(r)

    def pack_grads(names):
        return _pack_rows([gw[n].astype(BF16).reshape(NDEV, -1, ROW) for n in names], 1)

    def update(buf, names):
        r0 = 0
        for n in names:
            r = loc[n].size // ROW
            parts = buf[:, r0:r0 + r, :].reshape((NDEV,) + loc[n].shape)
            r0 += _tile_rows(r)
            step = adamw_t if n in COL_SHARDED else adamw
            out = step("adamw_" + n, parts, a[n][0], a["m_" + n][0], a["v_" + n][0])
            for kind, t in zip(("grad", "delta", "new_m", "new_v"), out):
                res[kind + "_" + n] = t[None]

    sm, wg_ffn1 = all_gather_pair("gather_inputs", _pack([c, conv_qkv[0]], F32), pack_local(G_FFN1))
    unpack_full(wg_ffn1, G_FFN1)
    c_loc, conv_loc = _unpack8(sm, [c.shape, conv_qkv.shape[1:]])
    c_all = c_loc.reshape(NDEV * bl, D)
    conv_full = conv_loc.transpose(1, 0, 2).reshape(CONVW, 3 * DNW)

    n_ada = w_ada.shape[2]
    mod_part = ada_fwd(c_all, w_ada[0], lax.dynamic_slice(b_ada, (0, me * n_ada), (1, n_ada)))
    mod_all = all_gather("gather_mod", mod_part).transpose(1, 0, 2).reshape(NDEV * bl, 9 * D)
    mod = lax.dynamic_slice(mod_all, (me * bl, 0), (bl, 9 * D)).reshape(bl, 9, D)
    mods = [mod[:, k:k + 1, :] for k in range(9)]

    h0 = x.reshape(t_rows, D)
    h1, f1, u1, pa1, pb1, wg_rest = ffn_fwd("ffn1_fwd", h0, mod[:, 0:3, :], g_ffn1, wfull['w1_ffn1'], wfull['w3_ffn1'],
                                  wfull['w2_ffn1'], seq, gather=pack_local(G_MIX + G_FFN2))
    unpack_full(wg_rest, G_MIX + G_FFN2)
    win = wfull['w_in']
    o_small, o_s5, o_gate = 4 * DNW, 4 * DNW + 2 * NH, 4 * DNW + 2 * NH + S5W
    w_dn, w_small = win[:o_small], jnp.pad(win[o_small:o_s5], ((0, LANES - 2 * NH), (0, 0)))
    w_s5, w_gate = win[o_s5:o_gate], win[o_gate:]
    w_pieces = [w_dn, w_small, w_s5, w_gate]
    u2, p_dn, p_small, p_s5, p_gate = mix_in_fwd(h1, mods[3], mods[4], g_mix, w_pieces, seq)

    conv8 = jnp.pad(conv_full, ((0, 8 - CONVW), (0, 0)))
    alp = jnp.pad(a_log, ((0, 0), (NH, LANES - 2 * NH)))
    dtp = jnp.pad(dt_bias, ((0, 0), (NH, LANES - 2 * NH)))
    nb_dn = DN_ROWS if bl % DN_ROWS == 0 else 1
    p_dn3, p_small3 = p_dn.reshape(bl, seq, 4 * DNW), p_small.reshape(bl, seq, LANES)
    qkv3 = dn_prep_fwd(p_dn3, conv8)
    o_pre3, sprev, tinv = deltanet_fwd(qkv3, p_small3, alp, dtp, nb_dn)
    o_pre = o_pre3.reshape(t_rows, DNW)
    z_raw = p_dn[:, 3 * DNW:]

    s5_params = [lam_re.reshape(1, S5N), lam_im.reshape(1, S5N), log_step,
                 b_re[0].transpose(2, 0, 1).reshape(S5C, S5N), b_im[0].transpose(2, 0, 1).reshape(S5C, S5N),
                 c_re[0].transpose(1, 0, 2).reshape(S5C, S5N), c_im[0].transpose(1, 0, 2).reshape(S5C, S5N)]
    tables = s5_tables_fwd(s5_params)
    p_s53 = p_s5.reshape(bl, seq, S5W)
    y_s53, xs = s5_fwd(p_s53, tables, d_skip)
    y_s5 = y_s53.reshape(t_rows, S5W)
    tail_in = [o_pre, z_raw, y_s5, p_gate]
    tail_w = [g_onorm, wfull['w_glu'], b_glu, wfull['w_proj_a'], wfull['w_proj_b']]
    (merged,) = ew_call("mix_tail", fn_mix_tail, tail_in, [], tail_w, [(D, BF16)], tm_ew, seq)
    mo, h2 = mix_out_fwd(merged, wfull['w_out'], h1, mods[5], seq)
    h3, f3, u3, pa3, pb3 = ffn_fwd("ffn2_fwd", h2, mod[:, 6:9, :], g_ffn2, wfull['w1_ffn2'], wfull['w3_ffn2'], wfull['w2_ffn2'], seq)

    dh3, dg_final, loss_part = loss_head(h3, loss_target.reshape(t_rows, D), g_final.reshape(1, D), seq)
    loss = lax.psum(loss_part[0, 0], ("x", "y", "c"))

    dh2, a3, d1_3, d3_3, df3, dmod_c, dg_ffn2 = ffn_bwd("ffn2_bwd", dh3, h2, f3, pa3, pb3, mod[:, 6:9, :], g_ffn2, wfull['w1_ffn2'],
                                                   wfull['w3_ffn2'], wfull['w2_ffn2'], seq)
    gw['w1_ffn2'] = mm_tn("gw1_ffn2", d1_3, u3)
    gw['w3_ffn2'] = mm_tn("gw3_ffn2", d3_3, u3)
    gw['w2_ffn2'] = mm_tn("gw2_ffn2", a3, df3)

    dmo, d_merged, dgt2 = mix_out_bwd(dh2, mo, wfull['w_out'], mods[5], seq)
    gw['w_out'] = mm_tn("gw_out", merged, dmo)
    (d_opre, d_z, d_ys5, d_gate), _, tail_gw = ew_vjp_call(
        "mix_tail_bwd", fn_mix_tail, tail_in, [], tail_w, [d_merged], [(0, F32), (1, F32), (2, F32), (3, BF16)], tm_ew, seq)
    dg_onorm, gw['w_glu'], dg_bglu, gw['w_proj_a'], gw['w_proj_b'] = tail_gw
    d_qkv3, d_psmall3, d_alp, d_dtp, rs_ffn2 = deltanet_bwd(
        qkv3, p_small3, alp, dtp, sprev, tinv, d_opre.reshape(bl, seq, DNW), nb_dn, exchange=pack_grads(G_FFN2))
    d_pdn3, d_conv8 = dn_prep_bwd(p_dn3, conv8, d_qkv3, d_z.reshape(bl, seq, DNW))
    d_pdn, d_psmall = d_pdn3.reshape(t_rows, 4 * DNW), d_psmall3.reshape(t_rows, LANES)

    s5_out = s5_bwd(p_s53, tables, d_skip, xs, d_ys5.reshape(bl, seq, S5W))
    d_ps5, d_tables, dg_dskip = s5_out[0].reshape(t_rows, S5W), s5_out[1:11], s5_out[11]
    d_s5p = s5_tables_bwd(s5_params, d_tables)

    gw['w_in'] = jnp.concatenate([mm_tn("gw_dn", d_pdn, u2), mm_tn("gw_small", d_psmall, u2)[:2 * NH],
                                  mm_tn("gw_s5", d_ps5, u2), mm_tn("gw_gate", d_gate, u2)], axis=0)
    dh1, dsh2, dsc2, dg_mix = mix_in_bwd([d_pdn, d_psmall, d_ps5, d_gate], w_pieces, h1, mods[3], mods[4], g_mix, dh2, seq)

    dh0, a1, d1_1, d3_1, df1, dmod_a, dg_ffn1, rs_mix = ffn_bwd(
        "ffn1_bwd", dh1, h0, f1, pa1, pb1, mod[:, 0:3, :], g_ffn1, wfull['w1_ffn1'], wfull['w3_ffn1'], wfull['w2_ffn1'], seq,
        exchange=pack_grads(G_MIX))
    gw['w1_ffn1'] = mm_tn("gw1_ffn1", d1_1, u1)
    gw['w3_ffn1'], rs_w1 = mm_tn("gw3_ffn1", d3_1, u1, exchange=pack_grads(['w1_ffn1']))
    gw['w2_ffn1'], rs_w3 = mm_tn("gw2_ffn1", a1, df1, exchange=pack_grads(['w3_ffn1']))

    update(rs_ffn2, G_FFN2)
    update(rs_mix, G_MIX)
    update(rs_w1, ['w1_ffn1'])
    update(rs_w3, ['w3_ffn1'])

    dmod_mine = jnp.concatenate([dmod_a, dsh2, dsc2, dgt2, dmod_c], axis=1).reshape(bl, 9 * D)
    small_grads = {
        'g_ffn1': dg_ffn1, 'g_mix': dg_mix, 'a_log': d_alp[:, NH:2 * NH], 'dt_bias': d_dtp[:, NH:2 * NH],
        'g_onorm': dg_onorm, 'lam_re': d_s5p[0].reshape(1, S5G, S5P), 'lam_im': d_s5p[1].reshape(1, S5G, S5P),
        'log_step': d_s5p[2],
        'b_re': d_s5p[3].reshape(S5C, S5G, S5P).transpose(1, 2, 0)[None],
        'b_im': d_s5p[4].reshape(S5C, S5G, S5P).transpose(1, 2, 0)[None],
        'c_re': d_s5p[5].reshape(S5C, S5G, S5P).transpose(1, 0, 2)[None],
        'c_im': d_s5p[6].reshape(S5C, S5G, S5P).transpose(1, 0, 2)[None],
        'd_skip': dg_dskip, 'b_glu': dg_bglu, 'g_ffn2': dg_ffn2, 'g_final': dg_final.reshape(D)}
    small_shapes = [a[n].shape for n in SMALL]
    small_pack = _pack([small_grads[n] for n in SMALL], F32)
    n_small = small_pack.shape[0]
    sg, rs_w2 = gather_with_exchange("gather_small_grads",
                                     jnp.concatenate([small_pack, _pack([dmod_mine, d_conv8[:CONVW]], F32)], axis=0),
                                     pack_grads(['w2_ffn1']))
    update(rs_w2, ['w2_ffn1'])
    pieces = _unpack8(sg[:, n_small:, :], [dmod_mine.shape, (CONVW, 3 * DNW)])
    dmod_all = pieces[0].reshape(NDEV * bl, 9 * D)
    g_wada, g_bada = ada_bwd(c_all, lax.dynamic_slice(dmod_all, (0, me * n_ada), (NDEV * bl, n_ada)), dmod_all)

    n_conv = conv_qkv.shape[2]
    conv_parts = lax.dynamic_slice(pieces[1], (0, 0, me * n_conv), (NDEV, CONVW, n_conv))
    conv_parts = jnp.pad(conv_parts.reshape(NDEV, 1, -1), ((0, 0), (0, 7), (0, 0)))
    pad8 = lambda t: jnp.pad(t.reshape(1, -1), ((0, 7), (0, 0)))
    conv_res = adamw("adamw_conv", conv_parts, pad8(conv_qkv), pad8(m_conv_qkv), pad8(v_conv_qkv))
    for kind, buf in zip(("grad", "delta", "new_m", "new_v"), conv_res):
        res[kind + "_conv_qkv"] = buf[0].reshape(conv_qkv.shape)

    small_res = adamw("adamw_small", sg[:, :n_small, :], *[_pack([a[p + n] for n in SMALL], F32) for p in ("", "m_", "v_")])
    for kind, buf in zip(("grad", "delta", "new_m", "new_v"), small_res):
        for n, t in zip(SMALL, _unpack(buf, small_shapes)):
            res[kind + "_" + n] = t

    for n, g in (("w_ada", g_wada), ("b_ada", g_bada)):
        shp = a[n].shape
        r2 = lambda t: t.reshape(-1, shp[-1]) if n == "w_ada" else pad8(t)
        out = adamw("adamw_" + n, r2(g)[None], r2(a[n]), r2(a["m_" + n]), r2(a["v_" + n]))
        for kind, buf in zip(("grad", "delta", "new_m", "new_v"), out):
            res[kind + "_" + n] = (buf if n == "w_ada" else buf[0:1]).reshape(shp)

    outs = [loss, dh0.reshape(x.shape)]
    for kind in ("grad", "delta", "new_m", "new_v"):
        outs += [res[kind + "_" + n] for n in WEIGHTS]
    return tuple(outs)
```

```python
import math

import jax
import jax.numpy as jnp
from jax import lax
from jax.experimental import pallas as pl
from jax.experimental.pallas import tpu as pltpu

F32 = jnp.float32
BF16 = jnp.bfloat16
HI = lax.Precision.HIGHEST
H3 = lax.Precision.HIGH
SDS = jax.ShapeDtypeStruct

D = 1024
FF = 2816
FFN_TF = FF
FFN_FWD_TM = 256
FFN_BWD_TM = 256
NH = 8
DH = 64
DNW = NH * DH
CONVW = 4
CH = 64
S5_CH = 128
ACC_LIMIT = 6 * 1024 * 1024
BF16_TILE_ROWS = 16
DN_ROWS = 4
S5W = 512
S5G = 32
S5P = 64
S5C = 16
S5N = S5G * S5P
GB = 4
NDEV = 8
EPS = 1e-6
LANES = 128
ROW = 1024
VMEM_LIMIT = 56 * 1024 * 1024

ADAM_LR, ADAM_B1, ADAM_B2, ADAM_EPS, ADAM_WD, ADAM_STEP = 0.001, 0.9, 0.999, 1e-08, 0.01, 10

WEIGHTS = ['w_ada', 'b_ada', 'g_ffn1', 'w1_ffn1', 'w3_ffn1', 'w2_ffn1', 'g_mix', 'w_in', 'conv_qkv', 'a_log',
           'dt_bias', 'g_onorm', 'lam_re', 'lam_im', 'log_step', 'b_re', 'b_im', 'c_re', 'c_im', 'd_skip', 'w_glu',
           'b_glu', 'w_proj_a', 'w_proj_b', 'w_out', 'g_ffn2', 'w1_ffn2', 'w3_ffn2', 'w2_ffn2', 'g_final']
RS_WEIGHTS = ['w1_ffn1', 'w3_ffn1', 'w2_ffn1', 'w_in', 'w_glu', 'w_proj_a', 'w_proj_b', 'w_out', 'w1_ffn2', 'w3_ffn2',
              'w2_ffn2']
COL_SHARDED = {'w1_ffn1', 'w3_ffn1', 'w_in', 'w_proj_a', 'w_proj_b', 'w1_ffn2', 'w3_ffn2'}
G_FFN1 = ['w1_ffn1', 'w3_ffn1', 'w2_ffn1']
G_MIX = ['w_in', 'w_glu', 'w_proj_a', 'w_proj_b', 'w_out']
G_FFN2 = ['w1_ffn2', 'w3_ffn2', 'w2_ffn2']
SMALL = ['g_ffn1', 'g_mix', 'a_log', 'dt_bias', 'g_onorm', 'lam_re', 'lam_im', 'log_step', 'b_re', 'b_im', 'c_re',
         'c_im', 'd_skip', 'b_glu', 'g_ffn2', 'g_final']


def _cp(n_grid=0):
    if n_grid:
        return pltpu.CompilerParams(vmem_limit_bytes=VMEM_LIMIT, dimension_semantics=("arbitrary",) * n_grid)
    return pltpu.CompilerParams(vmem_limit_bytes=VMEM_LIMIT)


def _dot(a, b):
    return jnp.dot(a.astype(BF16), b.astype(BF16), preferred_element_type=F32)


def _dot_nt(a, b):
    return lax.dot_general(a.astype(BF16), b.astype(BF16), (((1,), (1,)), ((), ())), preferred_element_type=F32)


def _dot_tn(a, b):
    return lax.dot_general(a.astype(BF16), b.astype(BF16), (((0,), (0,)), ((), ())), preferred_element_type=F32)


def _dot_hi(a, b):
    return jnp.dot(a, b, precision=HI, preferred_element_type=F32)


def _dot_h3(a, b):
    return jnp.dot(a, b, precision=H3, preferred_element_type=F32)


@jax.custom_vjp
def bdot(a, b):
    return _dot(a, b)


bdot.defvjp(lambda a, b: (_dot(a, b), (a, b)),
            lambda r, g: (_dot_nt(g, r[1]).astype(r[0].dtype), _dot_tn(r[0], g).astype(r[1].dtype)))


@jax.custom_vjp
def bdot_nt(a, b):
    return _dot_nt(a, b)


bdot_nt.defvjp(lambda a, b: (_dot_nt(a, b), (a, b)),
               lambda r, g: (_dot(g, r[1]).astype(r[0].dtype), _dot_tn(g, r[0]).astype(r[1].dtype)))


def _silu(x):
    return x * jax.nn.sigmoid(x)


def _iota2(shape, axis):
    return lax.broadcasted_iota(jnp.int32, shape, axis)


def normmod(h, g, sc, sh):
    y = h * lax.rsqrt(jnp.mean(h * h, axis=-1, keepdims=True) + EPS) * g
    return y * (1.0 + sc) + sh


def fn_merge(gate, ya, yb):
    return (jax.nn.sigmoid(gate[:, :D]) * ya + jax.nn.sigmoid(gate[:, D:]) * yb,)


def fn_glu(y, w, b):
    ge = jax.nn.gelu(y)
    return (ge * jax.nn.sigmoid(bdot(ge, w) + b),)


def fn_onorm(o, z, g_on):
    r = _iota2((DH, DNW), 0)
    c = _iota2((DH, DNW), 1)
    expand = (c % DH == r).astype(F32)
    r2 = _iota2((DNW, DNW), 0)
    c2 = _iota2((DNW, DNW), 1)
    avg = (r2 // DH == c2 // DH).astype(F32) * (1.0 / DH)
    ms = _dot_h3(o * o, avg)
    return (o * lax.rsqrt(ms + EPS) * _dot_hi(g_on, expand) * _silu(z),)


def fn_mix_tail(o_pre, z, y_s5, gate, g_on, w_glu, b_glu, wa_t, wb_t):
    (oa,) = fn_onorm(o_pre, z, g_on)
    (ob,) = fn_glu(y_s5, w_glu, b_glu)
    return fn_merge(gate, bdot_nt(oa, wa_t), bdot_nt(ob, wb_t))


def gate_fn(small, alp, dtp):
    beta = jax.nn.sigmoid(small)
    la = -jnp.exp(alp) * jax.nn.softplus(small + dtp)
    tri = (_iota2((CH, CH), 0) >= _iota2((CH, CH), 1)).astype(F32)
    gc = _dot_hi(tri, la)
    gct = lax.dot_general(la, tri, (((0,), (1,)), ((), ())), precision=HI, preferred_element_type=F32)
    return beta, gc, gct


def _bdg(a, b, ca, cb, hi):
    if not hi:
        a, b = a.astype(BF16), b.astype(BF16)
    return lax.dot_general(a, b, (((ca,), (cb,)), ((0,), (0,))), precision=H3 if hi else None,
                           preferred_element_type=F32)


def _batched_matmuls(hi):
    nn_ = lambda a, b: _bdg(a, b, 2, 1, hi)
    nt_ = lambda a, b: _bdg(a, b, 2, 2, hi)
    tn_ = lambda a, b: _bdg(a, b, 1, 1, hi)
    nn = jax.custom_vjp(nn_)
    nn.defvjp(lambda a, b: (nn_(a, b), (a, b)), lambda r, g: (nt_(g, r[1]), tn_(r[0], g)))
    nt = jax.custom_vjp(nt_)
    nt.defvjp(lambda a, b: (nt_(a, b), (a, b)), lambda r, g: (nn_(g, r[1]), tn_(g, r[0])))
    tn = jax.custom_vjp(tn_)
    tn.defvjp(lambda a, b: (tn_(a, b), (a, b)), lambda r, g: (nt_(r[1], g), nn_(r[0], g)))
    return nn, nt, tn


bnn, bnt, btn = _batched_matmuls(False)
hnn, hnt, htn = _batched_matmuls(True)


def _unit_lower_inverse(a):
    r = _iota2((1, CH, CH), 1)
    c = _iota2((1, CH, CH), 2)
    eye = (r == c).astype(F32)
    d = jnp.where(r // 8 == c // 8, a, 0.0)
    inv = eye - d
    p = d
    for _ in range(2):
        p = hnn(p, p)
        inv = inv + hnn(inv, p)
    for blk in (16, 32, 64):
        off = jnp.where((r // blk == c // blk) & (r // (blk // 2) != c // (blk // 2)), a, 0.0)
        inv = inv - hnn(hnn(inv, off), inv)
    return inv


@jax.custom_vjp
def _inverse_given(a, t):
    return t


_inverse_given.defvjp(lambda a, t: (t, t), lambda t, g: (-hnt(htn(t, g), t), jnp.zeros_like(t)))


def dn_prep(xc, w):
    t = xc.shape[0] - 8
    c = xc[5:5 + t] * w[0:1] + xc[6:6 + t] * w[1:2] + xc[7:7 + t] * w[2:3] + xc[8:8 + t] * w[3:4]
    act = _silu(c)
    q, k, v = act[:, :DNW], act[:, DNW:2 * DNW], act[:, 2 * DNW:]
    ones = (_iota2((DNW, DNW), 0) // DH == _iota2((DNW, DNW), 1) // DH).astype(F32)
    q = q * lax.rsqrt(_dot_h3(q * q, ones) + EPS) * (DH ** -0.5)
    k = k * lax.rsqrt(_dot_h3(k * k, ones) + EPS)
    return jnp.concatenate([q, k, v], axis=1)


def dn_chunk(q, k, v, b, g, gt, s_prev, t_saved=None):
    r = _iota2((1, CH, CH), 1)
    c = _iota2((1, CH, CH), 2)
    causal = r >= c
    dec = jnp.where(causal, jnp.exp(jnp.where(causal, g - gt, 0.0)), 0.0)
    kb = k * b
    qk = bnt(jnp.concatenate([q, kb], axis=1), k)
    attn = qk[:, :CH] * dec
    a = jnp.where(r > c, qk[:, CH:] * dec, 0.0)
    tinv = _unit_lower_inverse(a) if t_saved is None else _inverse_given(a, t_saved)
    eg = jnp.exp(g)
    uw = hnn(tinv, jnp.concatenate([v * b, kb * eg], axis=2))
    g_last = g[:, CH - 1:CH]
    ws = bnn(jnp.concatenate([uw[..., DH:], q * eg], axis=1), s_prev)
    v_new = uw[..., :DH] - ws[:, :CH]
    o = ws[:, CH:] + bnn(attn, v_new)
    s_new = s_prev * jnp.exp(g_last) + btn(k * jnp.exp(g_last - g), v_new)
    return o, s_new, tinv


def s5_chunk(u, xp_re, xp_im, bb_re, bb_im, cc_re, cc_im, p0r, p0i, p1r, p1i, pir, pii, dsk):
    nb, ch, _ = u.shape
    u2 = u.reshape(nb * ch, LANES)
    bu_re = bdot(u2, bb_re).reshape(nb, ch, 512)
    bu_im = bdot(u2, bb_im).reshape(nb, ch, 512)
    xt_re = pir * bu_re - pii * bu_im
    xt_im = pir * bu_im + pii * bu_re
    tri = jnp.broadcast_to((_iota2((1, ch, ch), 1) >= _iota2((1, ch, ch), 2)).astype(F32), (nb, ch, ch))
    cs_re = hnn(tri, xt_re)
    cs_im = hnn(tri, xt_im)
    x_re = p0r * cs_re - p0i * cs_im + p1r * xp_re - p1i * xp_im
    x_im = p0r * cs_im + p0i * cs_re + p1r * xp_im + p1i * xp_re
    y = bdot_nt(x_re.reshape(nb * ch, 512), cc_re) - bdot_nt(x_im.reshape(nb * ch, 512), cc_im) + dsk * u2
    return y.reshape(nb, ch, LANES), x_re[:, ch - 1:ch], x_im[:, ch - 1:ch]


def s5_tables(lam_re, lam_im, log_step, bre, bim, cre, cim):
    expand = (_iota2((S5G, S5N), 1) // S5P == _iota2((S5G, S5N), 0)).astype(F32)
    step = _dot_hi(jnp.exp(log_step), expand)
    lre = jnp.minimum(lam_re, -1e-4)
    lr = lre * step
    ang = lam_im * step
    mag = jnp.exp(lr)
    lb_re = mag * jnp.cos(ang)
    lb_im = mag * jnp.sin(ang)
    den = lre * lre + lam_im * lam_im
    coef_re = ((lb_re - 1.0) * lre + lb_im * lam_im) / den
    coef_im = (lb_im * lre - (lb_re - 1.0) * lam_im) / den
    bb_re = coef_re * bre - coef_im * bim
    bb_im = coef_re * bim + coef_im * bre
    j = _iota2((S5_CH, 1), 0).astype(F32)
    jc = j - S5_CH // 2
    e0 = jnp.exp(jc * lr)
    e1 = jnp.exp((j + 1.0) * lr)
    ei = jnp.exp(-jc * lr)
    mask = (_iota2((LANES, 512), 0) // S5C == _iota2((LANES, 512), 1) // S5P).astype(F32)

    def blocks(t):
        return jnp.concatenate([(jnp.tile(t[:, gb * 512:(gb + 1) * 512], (LANES // S5C, 1)) * mask)[None]
                                for gb in range(GB)], axis=0)

    return (blocks(bb_re), blocks(bb_im), blocks(cre), blocks(cim),
            e0 * jnp.cos(jc * ang), e0 * jnp.sin(jc * ang),
            e1 * jnp.cos((j + 1.0) * ang), e1 * jnp.sin((j + 1.0) * ang),
            ei * jnp.cos(jc * ang), -ei * jnp.sin(jc * ang))


def _row_specs(tiled, batch, bcast, tm, tpb):
    specs = [pl.BlockSpec((tm, a.shape[1]), lambda i: (i, 0)) for a in tiled]
    specs += [pl.BlockSpec((None,) + a.shape[1:], lambda i: (i // tpb, 0, 0)) for a in batch]
    specs += [pl.BlockSpec(a.shape, lambda i, nd=a.ndim: (0,) * nd) for a in bcast]
    return specs


def ew_call(name, fn, tiled, batch, bcast, outs, tm, seq):
    t_rows = tiled[0].shape[0]
    n_in = len(tiled) + len(batch) + len(bcast)

    def body(*refs):
        vals = [r[...].astype(F32) for r in refs[:n_in]]
        for r, o in zip(refs[n_in:], fn(*vals)):
            r[...] = o.astype(r.dtype)

    return pl.pallas_call(
        body, grid=(t_rows // tm,), in_specs=_row_specs(tiled, batch, bcast, tm, seq // tm),
        out_specs=[pl.BlockSpec((tm, w), lambda i: (i, 0)) for w, _ in outs],
        out_shape=[SDS((t_rows, w), dt) for w, dt in outs], name=name, compiler_params=_cp(1))(*tiled, *batch, *bcast)


def ew_vjp_call(name, fn, tiled, batch, bcast, cts, want, tm, seq, addend=None):
    t_rows = tiled[0].shape[0]
    tpb = seq // tm
    n_t, n_b, n_c = len(tiled), len(batch), len(bcast)
    n_in = n_t + n_b + n_c
    extra = [] if addend is None else [addend]

    def body(*refs):
        i = pl.program_id(0)
        vals = [r[...].astype(F32) for r in refs[:n_in]]
        ctv = tuple(r[...].astype(F32) for r in refs[n_in:n_in + len(cts)])
        outs = refs[n_in + len(cts) + len(extra):]
        _, vjp = jax.vjp(fn, *vals)
        grads = vjp(ctv)
        for k, (r, (idx, _)) in enumerate(zip(outs[:len(want)], want)):
            g = grads[idx]
            if k == 0 and extra:
                g = g + refs[n_in + len(cts)][...]
            r[...] = g.astype(r.dtype)
        for k in range(n_b):
            r, g = outs[len(want) + k], grads[n_t + k]

            @pl.when(i % tpb == 0)
            def _(r=r, g=g):
                r[...] = g

            @pl.when(i % tpb != 0)
            def _(r=r, g=g):
                r[...] += g
        for k in range(n_c):
            r, g = outs[len(want) + n_b + k], grads[n_t + n_b + k]

            @pl.when(i == 0)
            def _(r=r, g=g):
                r[...] = g

            @pl.when(i != 0)
            def _(r=r, g=g):
                r[...] += g

    out_specs = [pl.BlockSpec((tm, tiled[idx].shape[1]), lambda i: (i, 0)) for idx, _ in want]
    out_specs += [pl.BlockSpec((None,) + a.shape[1:], lambda i: (i // tpb, 0, 0)) for a in batch]
    out_specs += [pl.BlockSpec(a.shape, lambda i, nd=a.ndim: (0,) * nd) for a in bcast]
    out_shape = [SDS(tiled[idx].shape, dt) for idx, dt in want]
    out_shape += [SDS(a.shape, F32) for a in batch] + [SDS(a.shape, F32) for a in bcast]
    res = pl.pallas_call(
        body, grid=(t_rows // tm,),
        in_specs=_row_specs(tiled, batch, bcast, tm, tpb)
        + [pl.BlockSpec((tm, a.shape[1]), lambda i: (i, 0)) for a in list(cts) + extra],
        out_specs=out_specs, out_shape=out_shape, name=name, compiler_params=_cp(1))(*tiled, *batch, *bcast, *cts, *extra)
    return res[:len(want)], res[len(want):len(want) + n_b], res[len(want) + n_b:]


def _pick(n, cands):
    for c in cands:
        if n % c == 0:
            return c
    return n


def mm_tn(name, a, b, exchange=None):
    t_rows, m = a.shape
    n = b.shape[1]
    tn = n if n <= 1024 else _pick(n, (1024, 512, 256, 128))
    tm = max([t for t in range(LANES, m + 1, LANES) if m % t == 0 and t * tn * 4 <= ACC_LIMIT] or [m])
    tk = _pick(t_rows, (512, 256, 128, 64))
    grid = (m // tm, n // tn, t_rows // tk)
    extra = [] if exchange is None else [exchange]

    def body(*refs):
        a_ref, b_ref = refs[:2]
        o_ref, acc = refs[2 + len(extra)], refs[3 + 2 * len(extra)]
        i, j, k = pl.program_id(0), pl.program_id(1), pl.program_id(2)
        if extra:
            start, finish = _exchange_phases(refs[2], refs[4], *refs[6:9])
            pl.when((i == 0) & (j == 0) & (k == 0))(start)

        @pl.when(k == 0)
        def _():
            acc[...] = jnp.zeros_like(acc)

        acc[...] += _dot_tn(a_ref[...], b_ref[...])

        @pl.when(k == grid[2] - 1)
        def _():
            o_ref[...] = acc[...].astype(BF16)

        if extra:
            pl.when((i == grid[0] - 1) & (j == grid[1] - 1) & (k == grid[2] - 1))(finish)

    res = pl.pallas_call(
        body, grid=grid,
        in_specs=[pl.BlockSpec((tk, tm), lambda i, j, k: (k, i)), pl.BlockSpec((tk, tn), lambda i, j, k: (k, j))]
        + [HBM_SPEC] * len(extra),
        out_specs=[pl.BlockSpec((tm, tn), lambda i, j, k: (i, j))] + [HBM_SPEC] * len(extra),
        out_shape=[SDS((m, n), BF16)] + [SDS(x.shape, x.dtype) for x in extra],
        scratch_shapes=[pltpu.VMEM((tm, tn), F32)] + (_comm_scratch() if extra else []), name=name,
        compiler_params=_cp(3))(a, b, *extra)
    return res if extra else res[0]


def _ffn_weight_spec():
    if FFN_TF == FF:
        return pl.BlockSpec((FF, D), lambda i, j: (0, 0), pipeline_mode=pl.Buffered(1))
    return pl.BlockSpec((FFN_TF, D), lambda i, j: (j, 0))


def ffn_fwd(name, h, mod3, g, w1, w3, w2, seq, gather=None):
    t_rows = h.shape[0]
    tm = _pick(seq, (FFN_FWD_TM, 128, 64))
    tf = FFN_TF
    tpb = seq // tm
    nf = FF // tf
    nt = t_rows // tm
    extra = [] if gather is None else [gather]

    def body(*refs):
        h_ref, mod_ref, g_ref, w1_ref, w3_ref, w2_ref = refs[:6]
        ho_ref, f_ref, u_ref, h1_ref, h3_ref = refs[6 + len(extra):11 + len(extra)]
        acc = refs[11 + 2 * len(extra)]
        i, j = pl.program_id(0), pl.program_id(1)
        if extra:
            start, forward, finish = _gather_phases(refs[6], refs[12], *refs[14:17])
            pl.when((i == 0) & (j == 0))(start)
            pl.when((i == nt - 1) & (j == 0))(forward)

        @pl.when(j == 0)
        def _():
            u_ref[...] = normmod(h_ref[...], g_ref[...], mod_ref[1:2, :], mod_ref[0:1, :]).astype(BF16)
            acc[...] = jnp.zeros_like(acc)

        u = u_ref[...]
        h1 = _dot_nt(u, w1_ref[...])
        h3 = _dot_nt(u, w3_ref[...])
        h1_ref[...] = h1.astype(BF16)
        h3_ref[...] = h3.astype(BF16)
        acc[...] += _dot(_silu(h1) * h3, w2_ref[...])

        @pl.when(j == nf - 1)
        def _():
            f_ref[...] = acc[...]
            ho_ref[...] = h_ref[...] + 0.5 * mod_ref[2:3, :] * acc[...]

        if extra:
            pl.when((i == nt - 1) & (j == nf - 1))(finish)

    row = lambda i, j: (i, 0)
    return pl.pallas_call(
        body, grid=(nt, nf),
        in_specs=[pl.BlockSpec((tm, D), row), pl.BlockSpec((None, 3, D), lambda i, j: (i // tpb, 0, 0)),
                  pl.BlockSpec((1, D), lambda i, j: (0, 0)), _ffn_weight_spec(), _ffn_weight_spec(), _ffn_weight_spec()]
        + [HBM_SPEC] * len(extra),
        out_specs=[pl.BlockSpec((tm, D), row), pl.BlockSpec((tm, D), row), pl.BlockSpec((tm, D), row),
                   pl.BlockSpec((tm, tf), lambda i, j: (i, j)), pl.BlockSpec((tm, tf), lambda i, j: (i, j))]
        + [HBM_SPEC] * len(extra),
        out_shape=[SDS((t_rows, D), F32), SDS((t_rows, D), F32), SDS((t_rows, D), BF16), SDS((t_rows, FF), BF16),
                   SDS((t_rows, FF), BF16)] + [SDS((NDEV,) + x.shape, x.dtype) for x in extra],
        scratch_shapes=[pltpu.VMEM((tm, D), F32)] + (_comm_scratch() if extra else []), name=name,
        compiler_params=_cp(2))(h, mod3, g, w1, w3, w2, *extra)


def ffn_bwd(name, dho, h, f_out, h1_in, h3_in, mod3, g, w1, w3, w2, seq, exchange=None):
    t_rows = h.shape[0]
    tm = _pick(seq, (FFN_BWD_TM, 128, 64))
    tf = FFN_TF
    tpb = seq // tm
    nf = FF // tf
    nt = t_rows // tm
    extra = [] if exchange is None else [exchange]

    def body(*refs):
        dho_ref, h_ref, f_ref, h1_ref, h3_ref, mod_ref, g_ref, w1_ref, w3_ref, w2_ref = refs[:10]
        dh_ref, a_ref, dh1_ref, dh3_ref, df_scr, dmod_ref, dg_ref = refs[10 + len(extra):17 + len(extra)]
        du_acc = refs[17 + 2 * len(extra)]
        i, j = pl.program_id(0), pl.program_id(1)
        if extra:
            start, finish = _exchange_phases(refs[10], refs[18], *refs[20:23])
            pl.when((i == 0) & (j == 0))(start)

        @pl.when(j == 0)
        def _():
            df_scr[...] = (0.5 * mod_ref[2:3, :] * dho_ref[...]).astype(BF16)
            du_acc[...] = jnp.zeros_like(du_acc)

        h1 = h1_ref[...].astype(F32)
        h3 = h3_ref[...].astype(F32)
        sg = jax.nn.sigmoid(h1)
        s = h1 * sg
        da = _dot_nt(df_scr[...], w2_ref[...])
        dh3 = (da * s).astype(BF16)
        dh1 = (da * h3 * (sg * (1.0 + h1 * (1.0 - sg)))).astype(BF16)
        a_ref[...] = (s * h3).astype(BF16)
        dh1_ref[...] = dh1
        dh3_ref[...] = dh3
        du_acc[...] += _dot(dh1, w1_ref[...]) + _dot(dh3, w3_ref[...])

        @pl.when(j == nf - 1)
        def _():
            _, vjp = jax.vjp(normmod, h_ref[...], g_ref[...], mod_ref[1:2, :], mod_ref[0:1, :])
            dh_n, dg, dsc, dsh = vjp(du_acc[...])
            dh_ref[...] = dho_ref[...] + dh_n
            dgt = jnp.sum(0.5 * dho_ref[...] * f_ref[...], axis=0, keepdims=True)
            dmod = jnp.concatenate([dsh, dsc, dgt], axis=0)

            @pl.when(i % tpb == 0)
            def _():
                dmod_ref[...] = dmod

            @pl.when(i % tpb != 0)
            def _():
                dmod_ref[...] += dmod

            @pl.when(i == 0)
            def _():
                dg_ref[...] = dg

            @pl.when(i != 0)
            def _():
                dg_ref[...] += dg

        if extra:
            pl.when((i == nt - 1) & (j == nf - 1))(finish)

    row = lambda i, j: (i, 0)
    col = lambda i, j: (i, j)
    return pl.pallas_call(
        body, grid=(nt, nf),
        in_specs=[pl.BlockSpec((tm, D), row), pl.BlockSpec((tm, D), row), pl.BlockSpec((tm, D), row),
                  pl.BlockSpec((tm, tf), col), pl.BlockSpec((tm, tf), col),
                  pl.BlockSpec((None, 3, D), lambda i, j: (i // tpb, 0, 0)),
                  pl.BlockSpec((1, D), lambda i, j: (0, 0)), _ffn_weight_spec(), _ffn_weight_spec(), _ffn_weight_spec()]
        + [HBM_SPEC] * len(extra),
        out_specs=[pl.BlockSpec((tm, D), row), pl.BlockSpec((tm, tf), col), pl.BlockSpec((tm, tf), col),
                   pl.BlockSpec((tm, tf), col), pl.BlockSpec((tm, D), row),
                   pl.BlockSpec((None, 3, D), lambda i, j: (i // tpb, 0, 0)), pl.BlockSpec((1, D), lambda i, j: (0, 0))]
        + [HBM_SPEC] * len(extra),
        out_shape=[SDS((t_rows, D), F32), SDS((t_rows, FF), BF16), SDS((t_rows, FF), BF16), SDS((t_rows, FF), BF16),
                   SDS((t_rows, D), BF16), SDS(mod3.shape, F32), SDS((1, D), F32)] + [SDS(x.shape, x.dtype) for x in extra],
        scratch_shapes=[pltpu.VMEM((tm, D), F32)] + (_comm_scratch() if extra else []), name=name,
        compiler_params=_cp(2))(dho, h, f_out, h1_in, h3_in, mod3, g, w1, w3, w2, *extra)


def _resident(shape):
    return pl.BlockSpec(shape, lambda i: (0,) * len(shape), pipeline_mode=pl.Buffered(1))


def mix_in_fwd(h, sh, sc, g, ws, seq):
    t_rows = h.shape[0]
    tm = _pick(seq, (256, 128, 64))
    tpb = seq // tm
    nw = len(ws)

    def body(h_ref, sh_ref, sc_ref, g_ref, *rest):
        u = normmod(h_ref[...], g_ref[...], sc_ref[...], sh_ref[...]).astype(BF16)
        rest[nw][...] = u
        for w_ref, p_ref in zip(rest[:nw], rest[nw + 1:]):
            p_ref[...] = _dot_nt(u, w_ref[...])

    row = lambda i: (i, 0)
    batch = pl.BlockSpec((None, 1, D), lambda i: (i // tpb, 0, 0))
    return pl.pallas_call(
        body, grid=(t_rows // tm,),
        in_specs=[pl.BlockSpec((tm, D), row), batch, batch, pl.BlockSpec((1, D), lambda i: (0, 0))]
        + [_resident(w.shape) for w in ws],
        out_specs=[pl.BlockSpec((tm, D), row)] + [pl.BlockSpec((tm, w.shape[0]), row) for w in ws],
        out_shape=[SDS((t_rows, D), BF16)] + [SDS((t_rows, w.shape[0]), F32) for w in ws], name="mix_in_fwd",
        compiler_params=_cp(1))(h, sh, sc, g, *ws)


def mix_in_bwd(dps, ws, h, sh, sc, g, dh_add, seq):
    t_rows = h.shape[0]
    tm = _pick(seq, (256, 128, 64))
    tpb = seq // tm
    nw = len(ws)

    def body(*refs):
        h_ref, sh_ref, sc_ref, g_ref, add_ref, dh_ref, dsh_ref, dsc_ref, dg_ref = refs[2 * nw:]
        i = pl.program_id(0)
        du = _dot(refs[0][...], refs[nw][...])
        for k in range(1, nw):
            du = du + _dot(refs[k][...], refs[nw + k][...])
        _, vjp = jax.vjp(normmod, h_ref[...], g_ref[...], sc_ref[...], sh_ref[...])
        dh_n, dg, dsc, dsh = vjp(du)
        dh_ref[...] = add_ref[...] + dh_n

        @pl.when(i % tpb == 0)
        def _():
            dsh_ref[...] = dsh
            dsc_ref[...] = dsc

        @pl.when(i % tpb != 0)
        def _():
            dsh_ref[...] += dsh
            dsc_ref[...] += dsc

        @pl.when(i == 0)
        def _():
            dg_ref[...] = dg

        @pl.when(i != 0)
        def _():
            dg_ref[...] += dg

    row = lambda i: (i, 0)
    batch = pl.BlockSpec((None, 1, D), lambda i: (i // tpb, 0, 0))
    gain = pl.BlockSpec((1, D), lambda i: (0, 0))
    return pl.pallas_call(
        body, grid=(t_rows // tm,),
        in_specs=[pl.BlockSpec((tm, dp.shape[1]), row) for dp in dps] + [_resident(w.shape) for w in ws]
        + [pl.BlockSpec((tm, D), row), batch, batch, gain, pl.BlockSpec((tm, D), row)],
        out_specs=[pl.BlockSpec((tm, D), row), batch, batch, gain],
        out_shape=[SDS((t_rows, D), F32), SDS(sh.shape, F32), SDS(sc.shape, F32), SDS((1, D), F32)], name="mix_in_bwd",
        compiler_params=_cp(1))(*dps, *ws, h, sh, sc, g, dh_add)


def mix_out_fwd(merged, w_out, h_prev, gt, seq):
    t_rows = merged.shape[0]
    tm = _pick(seq, (256, 128, 64))
    tpb = seq // tm

    def body(m_ref, w_ref, h_ref, gt_ref, mo_ref, ho_ref):
        mo = _dot(m_ref[...], w_ref[...])
        mo_ref[...] = mo
        ho_ref[...] = h_ref[...] + gt_ref[...] * mo

    row = lambda i: (i, 0)
    return pl.pallas_call(
        body, grid=(t_rows // tm,),
        in_specs=[pl.BlockSpec((tm, D), row), _resident(w_out.shape), pl.BlockSpec((tm, D), row),
                  pl.BlockSpec((None, 1, D), lambda i: (i // tpb, 0, 0))],
        out_specs=[pl.BlockSpec((tm, D), row), pl.BlockSpec((tm, D), row)],
        out_shape=[SDS((t_rows, D), F32), SDS((t_rows, D), F32)], name="mix_out_fwd",
        compiler_params=_cp(1))(merged, w_out, h_prev, gt)


def mix_out_bwd(dh, mo, w_out, gt, seq):
    t_rows = dh.shape[0]
    tm = _pick(seq, (256, 128, 64))
    tpb = seq // tm

    def body(dh_ref, mo_ref, w_ref, gt_ref, dmo_ref, dm_ref, dgt_ref):
        i = pl.program_id(0)
        dmo = (gt_ref[...] * dh_ref[...]).astype(BF16)
        dmo_ref[...] = dmo
        dm_ref[...] = _dot_nt(dmo, w_ref[...])
        dgt = jnp.sum(dh_ref[...] * mo_ref[...], axis=0, keepdims=True)

        @pl.when(i % tpb == 0)
        def _():
            dgt_ref[...] = dgt

        @pl.when(i % tpb != 0)
        def _():
            dgt_ref[...] += dgt

    row = lambda i: (i, 0)
    batch = pl.BlockSpec((None, 1, D), lambda i: (i // tpb, 0, 0))
    return pl.pallas_call(
        body, grid=(t_rows // tm,),
        in_specs=[pl.BlockSpec((tm, D), row), pl.BlockSpec((tm, D), row), _resident(w_out.shape), batch],
        out_specs=[pl.BlockSpec((tm, D), row), pl.BlockSpec((tm, D), row), batch],
        out_shape=[SDS((t_rows, D), BF16), SDS((t_rows, D), F32), SDS(gt.shape, F32)], name="mix_out_bwd",
        compiler_params=_cp(1))(dh, mo, w_out, gt)


def _dn_cols(part, hd):
    return slice(part * DNW + hd * DH, part * DNW + (hd + 1) * DH)


def _qkv_stacks(qkv_ref, nb):
    pairs = [(b, hd) for b in range(nb) for hd in range(NH)]
    return [jnp.stack([qkv_ref[b, :, _dn_cols(part, hd)] for b, hd in pairs]) for part in range(3)]


def dn_prep_fwd(p_dn, conv8):
    bl, seq, _ = p_dn.shape
    tp = _pick(seq, (256, 128, 64))

    def body(raw_ref, halo_ref, conv_ref, o_ref):
        hm = (pl.program_id(1) > 0).astype(F32)
        o_ref[...] = dn_prep(jnp.concatenate([halo_ref[...] * hm, raw_ref[...]], axis=0), conv_ref[...])

    return pl.pallas_call(
        body, grid=(bl, seq // tp),
        in_specs=[pl.BlockSpec((None, tp, 3 * DNW), lambda b, i: (b, i, 0)),
                  pl.BlockSpec((None, 8, 3 * DNW), lambda b, i: (b, jnp.maximum(i * (tp // 8) - 1, 0), 0)),
                  pl.BlockSpec((8, 3 * DNW), lambda b, i: (0, 0))],
        out_specs=pl.BlockSpec((None, tp, 3 * DNW), lambda b, i: (b, i, 0)),
        out_shape=SDS((bl, seq, 3 * DNW), F32), name="dn_prep_fwd", compiler_params=_cp(2))(p_dn, p_dn, conv8)


def dn_prep_bwd(p_dn, conv8, d_qkv, d_z):
    bl, seq, _ = p_dn.shape
    tp = _pick(seq, (256, 128, 64))
    nt = seq // tp

    def body(raw_ref, halo_ref, conv_ref, dq_ref, dz_ref, draw_ref, dconv_ref, carry):
        b, r = pl.program_id(0), pl.program_id(1)

        @pl.when((b == 0) & (r == 0))
        def _():
            dconv_ref[...] = jnp.zeros_like(dconv_ref)

        @pl.when(r == 0)
        def _():
            carry[...] = jnp.zeros_like(carry)

        hm = (r < nt - 1).astype(F32)
        _, vjp = jax.vjp(dn_prep, jnp.concatenate([halo_ref[...] * hm, raw_ref[...]], axis=0), conv_ref[...])
        dxc, dw = vjp(dq_ref[...])
        tail = dxc[tp:tp + 8] + carry[...]
        draw_ref[:, 0:3 * DNW] = jnp.concatenate([dxc[8:tp], tail], axis=0).astype(BF16)
        draw_ref[:, 3 * DNW:4 * DNW] = dz_ref[...].astype(BF16)
        carry[...] = dxc[0:8] * hm
        dconv_ref[...] += dw

    blk = lambda b, r: (b, nt - 1 - r, 0)
    return pl.pallas_call(
        body, grid=(bl, nt),
        in_specs=[pl.BlockSpec((None, tp, 3 * DNW), blk),
                  pl.BlockSpec((None, 8, 3 * DNW), lambda b, r: (b, jnp.maximum((nt - 1 - r) * (tp // 8) - 1, 0), 0)),
                  pl.BlockSpec((8, 3 * DNW), lambda b, r: (0, 0)), pl.BlockSpec((None, tp, 3 * DNW), blk),
                  pl.BlockSpec((None, tp, DNW), blk)],
        out_specs=[pl.BlockSpec((None, tp, 4 * DNW), blk), pl.BlockSpec((8, 3 * DNW), lambda b, r: (0, 0))],
        out_shape=[SDS((bl, seq, 4 * DNW), BF16), SDS((8, 3 * DNW), F32)],
        scratch_shapes=[pltpu.VMEM((8, 3 * DNW), F32)], name="dn_prep_bwd", compiler_params=_cp(2))(p_dn, p_dn, conv8, d_qkv, d_z)


def _gate_stacks(gates, nb):
    pairs = [(b, hd) for b in range(nb) for hd in range(NH)]
    bs = jnp.stack([gates[b][0][:, hd:hd + 1] for b, hd in pairs])
    gs = jnp.stack([gates[b][1][:, NH + hd:NH + hd + 1] for b, hd in pairs])
    gts = jnp.stack([gates[b][2][NH + hd:NH + hd + 1, :] for b, hd in pairs])
    return bs, gs, gts


def deltanet_fwd(qkv, p_small, alp, dtp, nb):
    bl, seq, _ = qkv.shape
    nc = seq // CH
    ng = nb * NH

    def body(qkv_ref, small_ref, alp_ref, dtp_ref, o_ref, sprev_ref, tinv_ref, s_scr):
        @pl.when(pl.program_id(1) == 0)
        def _():
            s_scr[...] = jnp.zeros_like(s_scr)

        gates = [gate_fn(small_ref[b], alp_ref[...], dtp_ref[...]) for b in range(nb)]
        s_prev = s_scr[...]
        o, s_new, tinv = dn_chunk(*_qkv_stacks(qkv_ref, nb), *_gate_stacks(gates, nb), s_prev)
        sprev_ref[...] = s_prev
        tinv_ref[...] = tinv
        s_scr[...] = s_new
        for b in range(nb):
            for hd in range(NH):
                o_ref[b, :, hd * DH:(hd + 1) * DH] = o[b * NH + hd]

    blk = lambda bb, n: (bb, n, 0)
    const = lambda bb, n: (0, 0)
    saved = pl.BlockSpec((None, ng, DH, DH), lambda bb, n: (bb * nc + n, 0, 0, 0))
    return pl.pallas_call(
        body, grid=(bl // nb, nc),
        in_specs=[pl.BlockSpec((nb, CH, 3 * DNW), blk), pl.BlockSpec((nb, CH, LANES), blk),
                  pl.BlockSpec((1, LANES), const), pl.BlockSpec((1, LANES), const)],
        out_specs=[pl.BlockSpec((nb, CH, DNW), blk), saved, saved],
        out_shape=[SDS((bl, seq, DNW), F32), SDS((bl // nb * nc, ng, DH, DH), F32), SDS((bl // nb * nc, ng, DH, DH), F32)],
        scratch_shapes=[pltpu.VMEM((ng, DH, DH), F32)], name="deltanet_fwd",
        compiler_params=_cp(2))(qkv, p_small, alp, dtp)


def deltanet_bwd(qkv, p_small, alp, dtp, sprev, tinv, d_o, nb, exchange=None):
    bl, seq, _ = qkv.shape
    nc = seq // CH
    ng = nb * NH
    extra = [] if exchange is None else [exchange]

    def body(*refs):
        qkv_ref, small_ref, alp_ref, dtp_ref, sprev_ref, tinv_ref, do_ref = refs[:7]
        dqkv_ref, dsmall_ref, dalp_ref, ddtp_ref = refs[7 + len(extra):11 + len(extra)]
        ds_scr = refs[11 + 2 * len(extra)]
        bb, r = pl.program_id(0), pl.program_id(1)
        if extra:
            start, finish = _exchange_phases(refs[7], refs[12], *refs[14:17])
            pl.when((bb == 0) & (r == 0))(start)

        @pl.when((bb == 0) & (r == 0))
        def _():
            dalp_ref[...] = jnp.zeros_like(dalp_ref)
            ddtp_ref[...] = jnp.zeros_like(ddtp_ref)

        @pl.when(r == 0)
        def _():
            ds_scr[...] = jnp.zeros_like(ds_scr)

        gates, gate_vjps = [], []
        for b in range(nb):
            out, gvjp = jax.vjp(gate_fn, small_ref[b], alp_ref[...], dtp_ref[...])
            gates.append(out)
            gate_vjps.append(gvjp)
        t_saved = tinv_ref[...]
        _, vjp = jax.vjp(lambda *args: dn_chunk(*args, t_saved)[:2], *_qkv_stacks(qkv_ref, nb), *_gate_stacks(gates, nb),
                         sprev_ref[...])
        d_out = jnp.stack([do_ref[b, :, hd * DH:(hd + 1) * DH] for b in range(nb) for hd in range(NH)])
        grads = vjp((d_out, ds_scr[...]))
        ds_scr[...] = grads[6]
        lane = _iota2((CH, LANES), 1)
        rowi = _iota2((LANES, CH), 0)
        for b in range(nb):
            d_beta = jnp.zeros((CH, LANES), F32)
            d_gc = jnp.zeros((CH, LANES), F32)
            d_gct = jnp.zeros((LANES, CH), F32)
            for hd in range(NH):
                i = b * NH + hd
                for part in range(3):
                    dqkv_ref[b, :, _dn_cols(part, hd)] = grads[part][i]
                d_beta = d_beta + jnp.where(lane == hd, grads[3][i], 0.0)
                d_gc = d_gc + jnp.where(lane == NH + hd, grads[4][i], 0.0)
                d_gct = d_gct + jnp.where(rowi == NH + hd, grads[5][i], 0.0)
            d_small, d_alp, d_dtp = gate_vjps[b]((d_beta, d_gc, d_gct))
            dsmall_ref[b] = d_small.astype(BF16)
            dalp_ref[...] += d_alp
            ddtp_ref[...] += d_dtp
        if extra:
            pl.when((bb == bl // nb - 1) & (r == nc - 1))(finish)

    blk = lambda bb, r: (bb, nc - 1 - r, 0)
    const = lambda bb, r: (0, 0)
    saved = pl.BlockSpec((None, ng, DH, DH), lambda bb, r: (bb * nc + nc - 1 - r, 0, 0, 0))
    return pl.pallas_call(
        body, grid=(bl // nb, nc),
        in_specs=[pl.BlockSpec((nb, CH, 3 * DNW), blk), pl.BlockSpec((nb, CH, LANES), blk), pl.BlockSpec((1, LANES), const),
                  pl.BlockSpec((1, LANES), const), saved, saved, pl.BlockSpec((nb, CH, DNW), blk)] + [HBM_SPEC] * len(extra),
        out_specs=[pl.BlockSpec((nb, CH, 3 * DNW), blk), pl.BlockSpec((nb, CH, LANES), blk), pl.BlockSpec((1, LANES), const),
                   pl.BlockSpec((1, LANES), const)] + [HBM_SPEC] * len(extra),
        out_shape=[SDS((bl, seq, 3 * DNW), F32), SDS((bl, seq, LANES), BF16), SDS((1, LANES), F32), SDS((1, LANES), F32)]
        + [SDS(x.shape, x.dtype) for x in extra],
        scratch_shapes=[pltpu.VMEM((ng, DH, DH), F32)] + (_comm_scratch() if extra else []), name="deltanet_bwd",
        compiler_params=_cp(2))(qkv, p_small, alp, dtp, sprev, tinv, d_o, *extra)


def _s5_table_specs():
    tab3 = pl.BlockSpec((None, LANES, 512), lambda gb, n: (gb, 0, 0))
    tab2 = pl.BlockSpec((S5_CH, 512), lambda gb, n: (0, gb))
    return [tab3] * 4 + [tab2] * 6 + [pl.BlockSpec((1, LANES), lambda gb, n: (0, gb))]


def s5_fwd(u, tables, dsk):
    bl, seq, _ = u.shape
    nc = seq // S5_CH

    def body(u_ref, *rest):
        tabs, (y_ref, xs_ref, xr_scr, xi_scr) = rest[:11], rest[11:]

        @pl.when(pl.program_id(1) == 0)
        def _():
            xr_scr[...] = jnp.zeros_like(xr_scr)
            xi_scr[...] = jnp.zeros_like(xi_scr)

        xp_re, xp_im = xr_scr[...], xi_scr[...]
        xs_ref[0:bl] = xp_re
        xs_ref[bl:2 * bl] = xp_im
        y, xn_re, xn_im = s5_chunk(u_ref[...], xp_re, xp_im, *[t[...] for t in tabs])
        y_ref[...] = y
        xr_scr[...] = xn_re
        xi_scr[...] = xn_im

    blk = lambda gb, n: (0, n, gb)
    return pl.pallas_call(
        body, grid=(GB, nc), in_specs=[pl.BlockSpec((bl, S5_CH, LANES), blk)] + _s5_table_specs(),
        out_specs=[pl.BlockSpec((bl, S5_CH, LANES), blk),
                   pl.BlockSpec((None, 2 * bl, 1, 512), lambda gb, n: (gb * nc + n, 0, 0, 0))],
        out_shape=[SDS((bl, seq, S5W), F32), SDS((GB * nc, 2 * bl, 1, 512), F32)],
        scratch_shapes=[pltpu.VMEM((bl, 1, 512), F32), pltpu.VMEM((bl, 1, 512), F32)], name="s5_fwd",
        compiler_params=_cp(2))(u, *tables, dsk)


def s5_bwd(u, tables, dsk, xs, dy):
    bl, seq, _ = u.shape
    nc = seq // S5_CH

    def body(u_ref, *rest):
        tabs, xs_ref, dy_ref = rest[:11], rest[11], rest[12]
        du_ref, dtabs, dxr_scr, dxi_scr = rest[13], rest[14:25], rest[25], rest[26]
        r = pl.program_id(1)

        @pl.when(r == 0)
        def _():
            for t in dtabs:
                t[...] = jnp.zeros_like(t)
            dxr_scr[...] = jnp.zeros_like(dxr_scr)
            dxi_scr[...] = jnp.zeros_like(dxi_scr)

        _, vjp = jax.vjp(s5_chunk, u_ref[...], xs_ref[0:bl], xs_ref[bl:2 * bl], *[t[...] for t in tabs])
        grads = vjp((dy_ref[...], dxr_scr[...], dxi_scr[...]))
        du_ref[...] = grads[0].astype(BF16)
        dxr_scr[...] = grads[1]
        dxi_scr[...] = grads[2]
        for t, g in zip(dtabs, grads[3:]):
            t[...] += g

    blk = lambda gb, r: (0, nc - 1 - r, gb)
    tab_shapes = [SDS(t.shape, F32) for t in tables] + [SDS(dsk.shape, F32)]
    return pl.pallas_call(
        body, grid=(GB, nc),
        in_specs=[pl.BlockSpec((bl, S5_CH, LANES), blk)] + _s5_table_specs()
        + [pl.BlockSpec((None, 2 * bl, 1, 512), lambda gb, r: (gb * nc + nc - 1 - r, 0, 0, 0)), pl.BlockSpec((bl, S5_CH, LANES), blk)],
        out_specs=[pl.BlockSpec((bl, S5_CH, LANES), blk)] + _s5_table_specs(),
        out_shape=[SDS((bl, seq, S5W), BF16)] + tab_shapes,
        scratch_shapes=[pltpu.VMEM((bl, 1, 512), F32), pltpu.VMEM((bl, 1, 512), F32)], name="s5_bwd",
        compiler_params=_cp(2))(u, *tables, dsk, xs, dy)


def s5_tables_fwd(params):
    shapes = [SDS((GB, LANES, 512), F32)] * 4 + [SDS((S5_CH, S5N), F32)] * 6

    def body(*refs):
        for r, t in zip(refs[7:], s5_tables(*[p[...] for p in refs[:7]])):
            r[...] = t

    return pl.pallas_call(body, out_shape=shapes, name="s5_tables_fwd", compiler_params=_cp())(*params)


def s5_tables_bwd(params, dtables):
    def body(*refs):
        _, vjp = jax.vjp(s5_tables, *[p[...] for p in refs[:7]])
        for r, g in zip(refs[17:], vjp(tuple(t[...] for t in refs[7:17]))):
            r[...] = g

    return pl.pallas_call(body, out_shape=[SDS(p.shape, F32) for p in params], name="s5_tables_bwd",
                          compiler_params=_cp())(*params, *dtables)


def ada_fwd(c_all, w_loc, b_loc):
    def body(c_ref, w_ref, b_ref, o_ref):
        o_ref[...] = _dot(_silu(c_ref[...]), w_ref[...]) + b_ref[...]

    return pl.pallas_call(body, out_shape=SDS((c_all.shape[0], w_loc.shape[1]), F32), name="ada_fwd",
                          compiler_params=_cp())(c_all, w_loc, b_loc)


def ada_bwd(c_all, dmod_mine, dmod_all):
    def body(c_ref, dm_ref, da_ref, gw_ref, gb_ref):
        gw_ref[...] = _dot_tn(_silu(c_ref[...]), dm_ref[...])
        gb_ref[...] = jnp.sum(da_ref[...], axis=0, keepdims=True)

    return pl.pallas_call(body, out_shape=[SDS((D, dmod_mine.shape[1]), F32), SDS((1, dmod_all.shape[1]), F32)],
                          name="ada_bwd", compiler_params=_cp())(c_all, dmod_mine, dmod_all)


def loss_head(h, tgt, g, seq):
    t_rows = h.shape[0]
    tm = _pick(seq, (256, 128, 64))

    def body(h_ref, t_ref, g_ref, dh_ref, dg_ref, loss_ref):
        i = pl.program_id(0)
        y, vjp = jax.vjp(lambda hh, gg: hh * lax.rsqrt(jnp.mean(hh * hh, axis=-1, keepdims=True) + EPS) * gg,
                         h_ref[...], g_ref[...])
        e = y - t_ref[...]
        dh, dg = vjp(e * (1.0 / D))
        part = jnp.sum(jnp.sum(e * e, axis=1, keepdims=True), axis=0, keepdims=True) * (0.5 / D) + jnp.zeros((1, LANES), F32)
        dh_ref[...] = dh

        @pl.when(i == 0)
        def _():
            dg_ref[...] = dg
            loss_ref[...] = part

        @pl.when(i != 0)
        def _():
            dg_ref[...] += dg
            loss_ref[...] += part

    row = lambda i: (i, 0)
    const = lambda i: (0, 0)
    return pl.pallas_call(
        body, grid=(t_rows // tm,),
        in_specs=[pl.BlockSpec((tm, D), row), pl.BlockSpec((tm, D), row), pl.BlockSpec((1, D), const)],
        out_specs=[pl.BlockSpec((tm, D), row), pl.BlockSpec((1, D), const), pl.BlockSpec((1, LANES), const)],
        out_shape=[SDS((t_rows, D), F32), SDS((1, D), F32), SDS((1, LANES), F32)], name="loss_head",
        compiler_params=_cp(1))(h, tgt, g)


def adamw(name, parts, w, m, v):
    k_parts, rows, cols = parts.shape
    tr = _pick(rows, (256, 128, 64, 32, 16, 8))

    def body(p_ref, w_ref, m_ref, v_ref, g_ref, d_ref, mo_ref, vo_ref):
        g = p_ref[0].astype(F32)
        for k in range(1, k_parts):
            g = g + p_ref[k].astype(F32)
        _adam_store(g, w_ref, m_ref, v_ref, g_ref, d_ref, mo_ref, vo_ref)

    blk = pl.BlockSpec((tr, cols), lambda i: (i, 0))
    return pl.pallas_call(
        body, grid=(rows // tr,), in_specs=[pl.BlockSpec((k_parts, tr, cols), lambda i: (0, i, 0)), blk, blk, blk],
        out_specs=[blk] * 4, out_shape=[SDS((rows, cols), F32)] * 4, name=name, compiler_params=_cp(1))(parts, w, m, v)


def _adam_store(g, w_ref, m_ref, v_ref, g_ref, d_ref, mo_ref, vo_ref):
    m_new = ADAM_B1 * m_ref[...] + (1.0 - ADAM_B1) * g
    v_new = ADAM_B2 * v_ref[...] + (1.0 - ADAM_B2) * (g * g)
    m_hat = m_new / (1.0 - ADAM_B1 ** ADAM_STEP)
    v_hat = v_new / (1.0 - ADAM_B2 ** ADAM_STEP)
    g_ref[...] = g
    d_ref[...] = -ADAM_LR * (m_hat / (jnp.sqrt(v_hat) + ADAM_EPS) + ADAM_WD * w_ref[...])
    mo_ref[...] = m_new
    vo_ref[...] = v_new


def adamw_t(name, parts, w, m, v):
    k_parts, r, c = parts.shape
    tc = _pick(c, (256, 128))

    def body(p_ref, w_ref, m_ref, v_ref, g_ref, d_ref, mo_ref, vo_ref):
        gt = p_ref[0].astype(F32)
        for k in range(1, k_parts):
            gt = gt + p_ref[k].astype(F32)
        _adam_store(gt.T, w_ref, m_ref, v_ref, g_ref, d_ref, mo_ref, vo_ref)

    blk = pl.BlockSpec((tc, r), lambda j: (j, 0))
    return pl.pallas_call(
        body, grid=(c // tc,), in_specs=[pl.BlockSpec((k_parts, r, tc), lambda j: (0, 0, j)), blk, blk, blk],
        out_specs=[blk] * 4, out_shape=[SDS((c, r), F32)] * 4, name=name, compiler_params=_cp(1))(parts, w, m, v)


def _comm_scratch():
    return [pltpu.SemaphoreType.DMA((7,)), pltpu.SemaphoreType.DMA((7,)), pltpu.SemaphoreType.DMA]


HBM_SPEC = pl.BlockSpec(memory_space=pl.ANY)


def _gather_phases(x_ref, out_ref, send_sems, recv_sems, local_sem):
    mx, my, mc = lax.axis_index("x"), lax.axis_index("y"), lax.axis_index("c")
    me, sibling = (mx, my, mc), (mx, my, 1 - mc)
    chips = [(1 - mx, my), (mx, 1 - my), (1 - mx, 1 - my)]

    def slot(px, py, pc):
        return out_ref.at[4 * px + 2 * py + pc]

    def copy(k, block, to, src=None):
        return pltpu.make_async_remote_copy(
            src_ref=slot(*block) if src is None else src, dst_ref=slot(*block), send_sem=send_sems.at[k],
            recv_sem=recv_sems.at[k], device_id=to, device_id_type=pl.DeviceIdType.MESH)

    def first():
        return [copy(0, me, sibling, src=x_ref)] + [copy(1 + j, me, (*chip, mc), src=x_ref) for j, chip in enumerate(chips)]

    def passed():
        return [copy(4 + j, (*chip, mc), sibling) for j, chip in enumerate(chips)]

    def start():
        pltpu.make_async_copy(x_ref, slot(*me), local_sem).start()
        for cp in first():
            cp.start()

    def forward():
        for j, chip in enumerate(chips):
            copy(1 + j, (*chip, mc), me).wait_recv()
            passed()[j].start()

    def finish():
        copy(0, sibling, me).wait_recv()
        for j, chip in enumerate(chips):
            copy(4 + j, (*chip, 1 - mc), me).wait_recv()
        for cp in first() + passed():
            cp.wait_send()
        pltpu.make_async_copy(x_ref, slot(*me), local_sem).wait()

    return start, forward, finish


def _exchange_phases(x_ref, out_ref, send_sems, recv_sems, local_sem):
    mx, my, mc = lax.axis_index("x"), lax.axis_index("y"), lax.axis_index("c")
    me = 4 * mx + 2 * my + mc

    def peer(k):
        return mx ^ (k >> 2), my ^ ((k >> 1) & 1), mc ^ (k & 1)

    def sends():
        out = []
        for k in range(1, NDEV):
            px, py, pc = peer(k)
            out.append(pltpu.make_async_remote_copy(
                src_ref=x_ref.at[4 * px + 2 * py + pc], dst_ref=out_ref.at[me], send_sem=send_sems.at[k - 1],
                recv_sem=recv_sems.at[k - 1], device_id=(px, py, pc), device_id_type=pl.DeviceIdType.MESH))
        return out

    def start():
        pltpu.make_async_copy(x_ref.at[me], out_ref.at[me], local_sem).start()
        for cp in sends():
            cp.start()

    def finish():
        for k in range(1, NDEV):
            px, py, pc = peer(k)
            pltpu.make_async_remote_copy(
                src_ref=x_ref.at[me], dst_ref=out_ref.at[4 * px + 2 * py + pc], send_sem=send_sems.at[k - 1],
                recv_sem=recv_sems.at[k - 1], device_id=(px, py, pc), device_id_type=pl.DeviceIdType.MESH).wait_recv()
        for cp in sends():
            cp.wait_send()
        pltpu.make_async_copy(x_ref.at[me], out_ref.at[me], local_sem).wait()

    return start, finish


def all_gather(name, x):
    def body(x_ref, out_ref, send_sems, recv_sems, local_sem):
        for phase in _gather_phases(x_ref, out_ref, send_sems, recv_sems, local_sem):
            phase()

    return pl.pallas_call(body, out_shape=SDS((NDEV,) + x.shape, x.dtype), in_specs=[HBM_SPEC], out_specs=HBM_SPEC,
                          scratch_shapes=_comm_scratch(), name=name)(x)


def all_gather_pair(name, x1, x2):
    def body(x1_ref, x2_ref, o1_ref, o2_ref, *sems):
        first = _gather_phases(x1_ref, o1_ref, *sems[:3])
        second = _gather_phases(x2_ref, o2_ref, *sems[3:])
        for phase1, phase2 in zip(first, second):
            phase1()
            phase2()

    return pl.pallas_call(
        body, out_shape=[SDS((NDEV,) + x1.shape, x1.dtype), SDS((NDEV,) + x2.shape, x2.dtype)], in_specs=[HBM_SPEC] * 2,
        out_specs=[HBM_SPEC] * 2, scratch_shapes=_comm_scratch() + _comm_scratch(), name=name)(x1, x2)


def gather_with_exchange(name, xg, xe):
    def body(g_ref, e_ref, go_ref, eo_ref, *sems):
        g_start, g_forward, g_finish = _gather_phases(g_ref, go_ref, *sems[:3])
        e_start, e_finish = _exchange_phases(e_ref, eo_ref, *sems[3:])
        e_start()
        g_start()
        g_forward()
        g_finish()
        e_finish()

    return pl.pallas_call(
        body, out_shape=[SDS((NDEV,) + xg.shape, xg.dtype), SDS(xe.shape, xe.dtype)], in_specs=[HBM_SPEC] * 2,
        out_specs=[HBM_SPEC] * 2, scratch_shapes=_comm_scratch() + _comm_scratch(), name=name)(xg, xe)


def _pack(arrs, dtype, row_mult=8):
    segs = []
    for a in arrs:
        flat = a.reshape(-1).astype(dtype)
        segs.append(jnp.pad(flat, (0, (-flat.shape[0]) % ROW)))
    flat = jnp.concatenate(segs)
    flat = jnp.pad(flat, (0, (-flat.shape[0]) % (ROW * row_mult)))
    return flat.reshape(-1, ROW)


def _unpack(buf, shapes):
    flat = buf.reshape(-1)
    out, off = [], 0
    for s in shapes:
        n = math.prod(s)
        out.append(flat[off:off + n].reshape(s))
        off += n + (-n) % ROW
    return out


def _pack_rows(arrs, axis):
    padded = []
    for t in arrs:
        pad = [(0, 0)] * t.ndim
        pad[axis] = (0, _tile_rows(t.shape[axis]) - t.shape[axis])
        padded.append(jnp.pad(t, pad))
    return jnp.concatenate(padded, axis=axis)


def _tile_rows(r):
    return r + (-r) % BF16_TILE_ROWS


def _unpack8(buf, shapes):
    flat = buf.reshape(NDEV, -1)
    out, off = [], 0
    for s in shapes:
        n = math.prod(s)
        out.append(flat[:, off:off + n].reshape((NDEV,) + tuple(s)))
        off += n + (-n) % ROW
    return out


def kernel(x, c, w_ada, b_ada, g_ffn1, w1_ffn1, w3_ffn1, w2_ffn1, g_mix, w_in, conv_qkv, a_log, dt_bias, g_onorm, lam_re, lam_im, log_step, b_re, b_im, c_re, c_im, d_skip, w_glu, b_glu, w_proj_a, w_proj_b, w_out, g_ffn2, w1_ffn2, w3_ffn2, w2_ffn2, g_final, loss_target, m_w_ada, m_b_ada, m_g_ffn1, m_w1_ffn1, m_w3_ffn1, m_w2_ffn1, m_g_mix, m_w_in, m_conv_qkv, m_a_log, m_dt_bias, m_g_onorm, m_lam_re, m_lam_im, m_log_step, m_b_re, m_b_im, m_c_re, m_c_im, m_d_skip, m_w_glu, m_b_glu, m_w_proj_a, m_w_proj_b, m_w_out, m_g_ffn2, m_w1_ffn2, m_w3_ffn2, m_w2_ffn2, m_g_final, v_w_ada, v_b_ada, v_g_ffn1, v_w1_ffn1, v_w3_ffn1, v_w2_ffn1, v_g_mix, v_w_in, v_conv_qkv, v_a_log, v_dt_bias, v_g_onorm, v_lam_re, v_lam_im, v_log_step, v_b_re, v_b_im, v_c_re, v_c_im, v_d_skip, v_w_glu, v_b_glu, v_w_proj_a, v_w_proj_b, v_w_out, v_g_ffn2, v_w1_ffn2, v_w3_ffn2, v_w2_ffn2, v_g_final):
    a = dict(locals())
    bl, seq, _ = x.shape
    t_rows = bl * seq
    me = 4 * lax.axis_index("x") + 2 * lax.axis_index("y") + lax.axis_index("c")
    tm_ew = _pick(seq, (256, 128, 64))

    loc = {n: (a[n][0].T if n in COL_SHARDED else a[n][0]) for n in RS_WEIGHTS}
    wfull, gw, res = {}, {}, {}

    def pack_local(names):
        return _pack_rows([loc[n].astype(BF16).reshape(-1, ROW) for n in names], 0)

    def unpack_full(buf, names):
        r0 = 0
        for n in names:
            r = loc[n].size // ROW
            wfull[n] = buf[:, r0:r0 + r, :].reshape(-1, loc[n].shape[1])
            r0 += _tile_rows(r)

    def pack_grads(names):
        return _pack_rows([gw[n].astype(BF16).reshape(NDEV, -1, ROW) for n in names], 1)

    def update(buf, names):
        r0 = 0
        for n in names:
            r = loc[n].size // ROW
            parts = buf[:, r0:r0 + r, :].reshape((NDEV,) + loc[n].shape)
            r0 += _tile_rows(r)
            step = adamw_t if n in COL_SHARDED else adamw
            out = step("adamw_" + n, parts, a[n][0], a["m_" + n][0], a["v_" + n][0])
            for kind, t in zip(("grad", "delta", "new_m", "new_v"), out):
                res[kind + "_" + n] = t[None]

    sm, wg_ffn1 = all_gather_pair("gather_inputs", _pack([c, conv_qkv[0]], F32), pack_local(G_FFN1))
    unpack_full(wg_ffn1, G_FFN1)
    c_loc, conv_loc = _unpack8(sm, [c.shape, conv_qkv.shape[1:]])
    c_all = c_loc.reshape(NDEV * bl, D)
    conv_full = conv_loc.transpose(1, 0, 2).reshape(CONVW, 3 * DNW)

    n_ada = w_ada.shape[2]
    mod_part = ada_fwd(c_all, w_ada[0], lax.dynamic_slice(b_ada, (0, me * n_ada), (1, n_ada)))
    mod_all = all_gather("gather_mod", mod_part).transpose(1, 0, 2).reshape(NDEV * bl, 9 * D)
    mod = lax.dynamic_slice(mod_all, (me * bl, 0), (bl, 9 * D)).reshape(bl, 9, D)
    mods = [mod[:, k:k + 1, :] for k in range(9)]

    h0 = x.reshape(t_rows, D)
    h1, f1, u1, pa1, pb1, wg_rest = ffn_fwd("ffn1_fwd", h0, mod[:, 0:3, :], g_ffn1, wfull['w1_ffn1'], wfull['w3_ffn1'],
                                  wfull['w2_ffn1'], seq, gather=pack_local(G_MIX + G_FFN2))
    unpack_full(wg_rest, G_MIX + G_FFN2)
    win = wfull['w_in']
    o_small, o_s5, o_gate = 4 * DNW, 4 * DNW + 2 * NH, 4 * DNW + 2 * NH + S5W
    w_dn, w_small = win[:o_small], jnp.pad(win[o_small:o_s5], ((0, LANES - 2 * NH), (0, 0)))
    w_s5, w_gate = win[o_s5:o_gate], win[o_gate:]
    w_pieces = [w_dn, w_small, w_s5, w_gate]
    u2, p_dn, p_small, p_s5, p_gate = mix_in_fwd(h1, mods[3], mods[4], g_mix, w_pieces, seq)

    conv8 = jnp.pad(conv_full, ((0, 8 - CONVW), (0, 0)))
    alp = jnp.pad(a_log, ((0, 0), (NH, LANES - 2 * NH)))
    dtp = jnp.pad(dt_bias, ((0, 0), (NH, LANES - 2 * NH)))
    nb_dn = DN_ROWS if bl % DN_ROWS == 0 else 1
    p_dn3, p_small3 = p_dn.reshape(bl, seq, 4 * DNW), p_small.reshape(bl, seq, LANES)
    qkv3 = dn_prep_fwd(p_dn3, conv8)
    o_pre3, sprev, tinv = deltanet_fwd(qkv3, p_small3, alp, dtp, nb_dn)
    o_pre = o_pre3.reshape(t_rows, DNW)
    z_raw = p_dn[:, 3 * DNW:]

    s5_params = [lam_re.reshape(1, S5N), lam_im.reshape(1, S5N), log_step,
                 b_re[0].transpose(2, 0, 1).reshape(S5C, S5N), b_im[0].transpose(2, 0, 1).reshape(S5C, S5N),
                 c_re[0].transpose(1, 0, 2).reshape(S5C, S5N), c_im[0].transpose(1, 0, 2).reshape(S5C, S5N)]
    tables = s5_tables_fwd(s5_params)
    p_s53 = p_s5.reshape(bl, seq, S5W)
    y_s53, xs = s5_fwd(p_s53, tables, d_skip)
    y_s5 = y_s53.reshape(t_rows, S5W)
    tail_in = [o_pre, z_raw, y_s5, p_gate]
    tail_w = [g_onorm, wfull['w_glu'], b_glu, wfull['w_proj_a'], wfull['w_proj_b']]
    (merged,) = ew_call("mix_tail", fn_mix_tail, tail_in, [], tail_w, [(D, BF16)], tm_ew, seq)
    mo, h2 = mix_out_fwd(merged, wfull['w_out'], h1, mods[5], seq)
    h3, f3, u3, pa3, pb3 = ffn_fwd("ffn2_fwd", h2, mod[:, 6:9, :], g_ffn2, wfull['w1_ffn2'], wfull['w3_ffn2'], wfull['w2_ffn2'], seq)

    dh3, dg_final, loss_part = loss_head(h3, loss_target.reshape(t_rows, D), g_final.reshape(1, D), seq)

    dh2, a3, d1_3, d3_3, df3, dmod_c, dg_ffn2 = ffn_bwd("ffn2_bwd", dh3, h2, f3, pa3, pb3, mod[:, 6:9, :], g_ffn2, wfull['w1_ffn2'],
                                                   wfull['w3_ffn2'], wfull['w2_ffn2'], seq)
    gw['w1_ffn2'] = mm_tn("gw1_ffn2", d1_3, u3)
    gw['w3_ffn2'] = mm_tn("gw3_ffn2", d3_3, u3)
    gw['w2_ffn2'] = mm_tn("gw2_ffn2", a3, df3)

    dmo, d_merged, dgt2 = mix_out_bwd(dh2, mo, wfull['w_out'], mods[5], seq)
    gw['w_out'] = mm_tn("gw_out", merged, dmo)
    (d_opre, d_z, d_ys5, d_gate), _, tail_gw = ew_vjp_call(
        "mix_tail_bwd", fn_mix_tail, tail_in, [], tail_w, [d_merged], [(0, F32), (1, F32), (2, F32), (3, BF16)], tm_ew, seq)
    dg_onorm, gw['w_glu'], dg_bglu, gw['w_proj_a'], gw['w_proj_b'] = tail_gw
    d_qkv3, d_psmall3, d_alp, d_dtp, rs_ffn2 = deltanet_bwd(
        qkv3, p_small3, alp, dtp, sprev, tinv, d_opre.reshape(bl, seq, DNW), nb_dn, exchange=pack_grads(G_FFN2))
    d_pdn3, d_conv8 = dn_prep_bwd(p_dn3, conv8, d_qkv3, d_z.reshape(bl, seq, DNW))
    d_pdn, d_psmall = d_pdn3.reshape(t_rows, 4 * DNW), d_psmall3.reshape(t_rows, LANES)

    s5_out = s5_bwd(p_s53, tables, d_skip, xs, d_ys5.reshape(bl, seq, S5W))
    d_ps5, d_tables, dg_dskip = s5_out[0].reshape(t_rows, S5W), s5_out[1:11], s5_out[11]
    d_s5p = s5_tables_bwd(s5_params, d_tables)

    gw['w_in'] = jnp.concatenate([mm_tn("gw_dn", d_pdn, u2), mm_tn("gw_small", d_psmall, u2)[:2 * NH],
                                  mm_tn("gw_s5", d_ps5, u2), mm_tn("gw_gate", d_gate, u2)], axis=0)
    dh1, dsh2, dsc2, dg_mix = mix_in_bwd([d_pdn, d_psmall, d_ps5, d_gate], w_pieces, h1, mods[3], mods[4], g_mix, dh2, seq)

    dh0, a1, d1_1, d3_1, df1, dmod_a, dg_ffn1, rs_mix = ffn_bwd(
        "ffn1_bwd", dh1, h0, f1, pa1, pb1, mod[:, 0:3, :], g_ffn1, wfull['w1_ffn1'], wfull['w3_ffn1'], wfull['w2_ffn1'], seq,
        exchange=pack_grads(G_MIX))
    gw['w1_ffn1'] = mm_tn("gw1_ffn1", d1_1, u1)
    gw['w3_ffn1'], rs_w1 = mm_tn("gw3_ffn1", d3_1, u1, exchange=pack_grads(['w1_ffn1']))
    gw['w2_ffn1'], rs_w3 = mm_tn("gw2_ffn1", a1, df1, exchange=pack_grads(['w3_ffn1']))

    update(rs_ffn2, G_FFN2)
    update(rs_mix, G_MIX)
    update(rs_w1, ['w1_ffn1'])
    update(rs_w3, ['w3_ffn1'])

    dmod_mine = jnp.concatenate([dmod_a, dsh2, dsc2, dgt2, dmod_c], axis=1).reshape(bl, 9 * D)
    small_grads = {
        'g_ffn1': dg_ffn1, 'g_mix': dg_mix, 'a_log': d_alp[:, NH:2 * NH], 'dt_bias': d_dtp[:, NH:2 * NH],
        'g_onorm': dg_onorm, 'lam_re': d_s5p[0].reshape(1, S5G, S5P), 'lam_im': d_s5p[1].reshape(1, S5G, S5P),
        'log_step': d_s5p[2],
        'b_re': d_s5p[3].reshape(S5C, S5G, S5P).transpose(1, 2, 0)[None],
        'b_im': d_s5p[4].reshape(S5C, S5G, S5P).transpose(1, 2, 0)[None],
        'c_re': d_s5p[5].reshape(S5C, S5G, S5P).transpose(1, 0, 2)[None],
        'c_im': d_s5p[6].reshape(S5C, S5G, S5P).transpose(1, 0, 2)[None],
        'd_skip': dg_dskip, 'b_glu': dg_bglu, 'g_ffn2': dg_ffn2, 'g_final': dg_final.reshape(D)}
    small_shapes = [a[n].shape for n in SMALL]
    small_pack = _pack([small_grads[n] for n in SMALL] + [loss_part], F32)
    n_small = small_pack.shape[0]
    sg, rs_w2 = gather_with_exchange("gather_small_grads",
                                     jnp.concatenate([small_pack, _pack([dmod_mine, d_conv8[:CONVW]], F32)], axis=0),
                                     pack_grads(['w2_ffn1']))
    update(rs_w2, ['w2_ffn1'])
    pieces = _unpack8(sg[:, n_small:, :], [dmod_mine.shape, (CONVW, 3 * DNW)])
    dmod_all = pieces[0].reshape(NDEV * bl, 9 * D)
    g_wada, g_bada = ada_bwd(c_all, lax.dynamic_slice(dmod_all, (0, me * n_ada), (NDEV * bl, n_ada)), dmod_all)

    n_conv = conv_qkv.shape[2]
    conv_parts = lax.dynamic_slice(pieces[1], (0, 0, me * n_conv), (NDEV, CONVW, n_conv))
    conv_parts = jnp.pad(conv_parts.reshape(NDEV, 1, -1), ((0, 0), (0, 7), (0, 0)))
    pad8 = lambda t: jnp.pad(t.reshape(1, -1), ((0, 7), (0, 0)))
    conv_res = adamw("adamw_conv", conv_parts, pad8(conv_qkv), pad8(m_conv_qkv), pad8(v_conv_qkv))
    for kind, buf in zip(("grad", "delta", "new_m", "new_v"), conv_res):
        res[kind + "_conv_qkv"] = buf[0].reshape(conv_qkv.shape)

    no_param = jnp.zeros_like(loss_part)
    small_res = adamw("adamw_small", sg[:, :n_small, :],
                      *[_pack([a[p + n] for n in SMALL] + [no_param], F32) for p in ("", "m_", "v_")])
    for kind, buf in zip(("grad", "delta", "new_m", "new_v"), small_res):
        for n, t in zip(SMALL, _unpack(buf, small_shapes)):
            res[kind + "_" + n] = t
    loss = _unpack(small_res[0], small_shapes + [loss_part.shape])[-1][0, 0]

    for n, g in (("w_ada", g_wada), ("b_ada", g_bada)):
        shp = a[n].shape
        r2 = lambda t: t.reshape(-1, shp[-1]) if n == "w_ada" else pad8(t)
        out = adamw("adamw_" + n, r2(g)[None], r2(a[n]), r2(a["m_" + n]), r2(a["v_" + n]))
        for kind, buf in zip(("grad", "delta", "new_m", "new_v"), out):
            res[kind + "_" + n] = (buf if n == "w_ada" else buf[0:1]).reshape(shp)

    outs = [loss, dh0.reshape(x.shape)]
    for kind in ("grad", "delta", "new_m", "new_v"):
        outs += [res[kind + "_" + n] for n in WEIGHTS]
    return tuple(outs)
```

```python
import math

import jax
import jax.numpy as jnp
from jax import lax
from jax.experimental import pallas as pl
from jax.experimental.pallas import tpu as pltpu

F32 = jnp.float32
BF16 = jnp.bfloat16
HI = lax.Precision.HIGHEST
H3 = lax.Precision.HIGH
SDS = jax.ShapeDtypeStruct

D = 1024
FF = 2816
FFN_TF = FF
FFN_FWD_TM = 256
FFN_BWD_TM = 256
NH = 8
DH = 64
DNW = NH * DH
CONVW = 4
CH = 64
S5_CH = 128
ACC_LIMIT = 6 * 1024 * 1024
BF16_TILE_ROWS = 16
DN_ROWS = 4
S5W = 512
S5G = 32
S5P = 64
S5C = 16
S5N = S5G * S5P
GB = 4
NDEV = 8
EPS = 1e-6
LANES = 128
ROW = 1024
VMEM_LIMIT = 56 * 1024 * 1024

ADAM_LR, ADAM_B1, ADAM_B2, ADAM_EPS, ADAM_WD, ADAM_STEP = 0.001, 0.9, 0.999, 1e-08, 0.01, 10

WEIGHTS = ['w_ada', 'b_ada', 'g_ffn1', 'w1_ffn1', 'w3_ffn1', 'w2_ffn1', 'g_mix', 'w_in', 'conv_qkv', 'a_log',
           'dt_bias', 'g_onorm', 'lam_re', 'lam_im', 'log_step', 'b_re', 'b_im', 'c_re', 'c_im', 'd_skip', 'w_glu',
           'b_glu', 'w_proj_a', 'w_proj_b', 'w_out', 'g_ffn2', 'w1_ffn2', 'w3_ffn2', 'w2_ffn2', 'g_final']
RS_WEIGHTS = ['w1_ffn1', 'w3_ffn1', 'w2_ffn1', 'w_in', 'w_glu', 'w_proj_a', 'w_proj_b', 'w_out', 'w1_ffn2', 'w3_ffn2',
              'w2_ffn2']
COL_SHARDED = {'w1_ffn1', 'w3_ffn1', 'w_in', 'w_proj_a', 'w_proj_b', 'w1_ffn2', 'w3_ffn2'}
G_FFN1 = ['w1_ffn1', 'w3_ffn1', 'w2_ffn1']
G_MIX = ['w_in', 'w_glu', 'w_proj_a', 'w_proj_b', 'w_out']
G_FFN2 = ['w1_ffn2', 'w3_ffn2', 'w2_ffn2']
SMALL = ['g_ffn1', 'g_mix', 'a_log', 'dt_bias', 'g_onorm', 'lam_re', 'lam_im', 'log_step', 'b_re', 'b_im', 'c_re',
         'c_im', 'd_skip', 'b_glu', 'g_ffn2', 'g_final']


def _cp(n_grid=0):
    if n_grid:
        return pltpu.CompilerParams(vmem_limit_bytes=VMEM_LIMIT, dimension_semantics=("arbitrary",) * n_grid)
    return pltpu.CompilerParams(vmem_limit_bytes=VMEM_LIMIT)


def _dot(a, b):
    return jnp.dot(a.astype(BF16), b.astype(BF16), preferred_element_type=F32)


def _dot_nt(a, b):
    return lax.dot_general(a.astype(BF16), b.astype(BF16), (((1,), (1,)), ((), ())), preferred_element_type=F32)


def _dot_tn(a, b):
    return lax.dot_general(a.astype(BF16), b.astype(BF16), (((0,), (0,)), ((), ())), preferred_element_type=F32)


def _dot_hi(a, b):
    return jnp.dot(a, b, precision=HI, preferred_element_type=F32)


def _dot_h3(a, b):
    return jnp.dot(a, b, precision=H3, preferred_element_type=F32)


@jax.custom_vjp
def bdot(a, b):
    return _dot(a, b)


bdot.defvjp(lambda a, b: (_dot(a, b), (a, b)),
            lambda r, g: (_dot_nt(g, r[1]).astype(r[0].dtype), _dot_tn(r[0], g).astype(r[1].dtype)))


@jax.custom_vjp
def bdot_nt(a, b):
    return _dot_nt(a, b)


bdot_nt.defvjp(lambda a, b: (_dot_nt(a, b), (a, b)),
               lambda r, g: (_dot(g, r[1]).astype(r[0].dtype), _dot_tn(g, r[0]).astype(r[1].dtype)))


def _silu(x):
    return x * jax.nn.sigmoid(x)


def _iota2(shape, axis):
    return lax.broadcasted_iota(jnp.int32, shape, axis)


def normmod(h, g, sc, sh):
    y = h * lax.rsqrt(jnp.mean(h * h, axis=-1, keepdims=True) + EPS) * g
    return y * (1.0 + sc) + sh


def fn_merge(gate, ya, yb):
    return (jax.nn.sigmoid(gate[:, :D]) * ya + jax.nn.sigmoid(gate[:, D:]) * yb,)


def fn_glu(y, w, b):
    ge = jax.nn.gelu(y)
    return (ge * jax.nn.sigmoid(bdot(ge, w) + b),)


def fn_onorm(o, z, g_on):
    r = _iota2((DH, DNW), 0)
    c = _iota2((DH, DNW), 1)
    expand = (c % DH == r).astype(F32)
    r2 = _iota2((DNW, DNW), 0)
    c2 = _iota2((DNW, DNW), 1)
    avg = (r2 // DH == c2 // DH).astype(F32) * (1.0 / DH)
    ms = _dot_h3(o * o, avg)
    return (o * lax.rsqrt(ms + EPS) * _dot_hi(g_on, expand) * _silu(z),)


def fn_mix_tail(o_pre, z, y_s5, gate, g_on, w_glu, b_glu, wa_t, wb_t):
    (oa,) = fn_onorm(o_pre, z, g_on)
    (ob,) = fn_glu(y_s5, w_glu, b_glu)
    return fn_merge(gate, bdot_nt(oa, wa_t), bdot_nt(ob, wb_t))


def gate_fn(small, alp, dtp):
    beta = jax.nn.sigmoid(small)
    la = -jnp.exp(alp) * jax.nn.softplus(small + dtp)
    tri = (_iota2((CH, CH), 0) >= _iota2((CH, CH), 1)).astype(F32)
    gc = _dot_hi(tri, la)
    gct = lax.dot_general(la, tri, (((0,), (1,)), ((), ())), precision=HI, preferred_element_type=F32)
    return beta, gc, gct


def _bdg(a, b, ca, cb, hi):
    if not hi:
        a, b = a.astype(BF16), b.astype(BF16)
    return lax.dot_general(a, b, (((ca,), (cb,)), ((0,), (0,))), precision=H3 if hi else None,
                           preferred_element_type=F32)


def _batched_matmuls(hi):
    nn_ = lambda a, b: _bdg(a, b, 2, 1, hi)
    nt_ = lambda a, b: _bdg(a, b, 2, 2, hi)
    tn_ = lambda a, b: _bdg(a, b, 1, 1, hi)
    nn = jax.custom_vjp(nn_)
    nn.defvjp(lambda a, b: (nn_(a, b), (a, b)), lambda r, g: (nt_(g, r[1]), tn_(r[0], g)))
    nt = jax.custom_vjp(nt_)
    nt.defvjp(lambda a, b: (nt_(a, b), (a, b)), lambda r, g: (nn_(g, r[1]), tn_(g, r[0])))
    tn = jax.custom_vjp(tn_)
    tn.defvjp(lambda a, b: (tn_(a, b), (a, b)), lambda r, g: (nt_(r[1], g), nn_(r[0], g)))
    return nn, nt, tn


bnn, bnt, btn = _batched_matmuls(False)
hnn, hnt, htn = _batched_matmuls(True)


def _unit_lower_inverse(a):
    r = _iota2((1, CH, CH), 1)
    c = _iota2((1, CH, CH), 2)
    eye = (r == c).astype(F32)
    d = jnp.where(r // 8 == c // 8, a, 0.0)
    inv = eye - d
    p = d
    for _ in range(2):
        p = hnn(p, p)
        inv = inv + hnn(inv, p)
    for blk in (16, 32, 64):
        off = jnp.where((r // blk == c // blk) & (r // (blk // 2) != c // (blk // 2)), a, 0.0)
        inv = inv - hnn(hnn(inv, off), inv)
    return inv


@jax.custom_vjp
def _inverse_given(a, t):
    return t


_inverse_given.defvjp(lambda a, t: (t, t), lambda t, g: (-hnt(htn(t, g), t), jnp.zeros_like(t)))


def dn_prep(xc, w):
    t = xc.shape[0] - 8
    c = xc[5:5 + t] * w[0:1] + xc[6:6 + t] * w[1:2] + xc[7:7 + t] * w[2:3] + xc[8:8 + t] * w[3:4]
    act = _silu(c)
    q, k, v = act[:, :DNW], act[:, DNW:2 * DNW], act[:, 2 * DNW:]
    ones = (_iota2((DNW, DNW), 0) // DH == _iota2((DNW, DNW), 1) // DH).astype(F32)
    q = q * lax.rsqrt(_dot_h3(q * q, ones) + EPS) * (DH ** -0.5)
    k = k * lax.rsqrt(_dot_h3(k * k, ones) + EPS)
    return jnp.concatenate([q, k, v], axis=1)


def dn_chunk(q, k, v, b, g, gt, s_prev, t_saved=None):
    r = _iota2((1, CH, CH), 1)
    c = _iota2((1, CH, CH), 2)
    causal = r >= c
    dec = jnp.where(causal, jnp.exp(jnp.where(causal, g - gt, 0.0)), 0.0)
    kb = k * b
    qk = bnt(jnp.concatenate([q, kb], axis=1), k)
    attn = qk[:, :CH] * dec
    a = jnp.where(r > c, qk[:, CH:] * dec, 0.0)
    tinv = _unit_lower_inverse(a) if t_saved is None else _inverse_given(a, t_saved)
    eg = jnp.exp(g)
    uw = hnn(tinv, jnp.concatenate([v * b, kb * eg], axis=2))
    g_last = g[:, CH - 1:CH]
    ws = bnn(jnp.concatenate([uw[..., DH:], q * eg], axis=1), s_prev)
    v_new = uw[..., :DH] - ws[:, :CH]
    o = ws[:, CH:] + bnn(attn, v_new)
    s_new = s_prev * jnp.exp(g_last) + btn(k * jnp.exp(g_last - g), v_new)
    return o, s_new, tinv


def s5_chunk(u, xp_re, xp_im, bb_re, bb_im, cc_re, cc_im, p0r, p0i, p1r, p1i, pir, pii, dsk):
    nb, ch, _ = u.shape
    u2 = u.reshape(nb * ch, LANES)
    bu_re = bdot(u2, bb_re).reshape(nb, ch, 512)
    bu_im = bdot(u2, bb_im).reshape(nb, ch, 512)
    xt_re = pir * bu_re - pii * bu_im
    xt_im = pir * bu_im + pii * bu_re
    tri = jnp.broadcast_to((_iota2((1, ch, ch), 1) >= _iota2((1, ch, ch), 2)).astype(F32), (nb, ch, ch))
    cs_re = hnn(tri, xt_re)
    cs_im = hnn(tri, xt_im)
    x_re = p0r * cs_re - p0i * cs_im + p1r * xp_re - p1i * xp_im
    x_im = p0r * cs_im + p0i * cs_re + p1r * xp_im + p1i * xp_re
    y = bdot_nt(x_re.reshape(nb * ch, 512), cc_re) - bdot_nt(x_im.reshape(nb * ch, 512), cc_im) + dsk * u2
    return y.reshape(nb, ch, LANES), x_re[:, ch - 1:ch], x_im[:, ch - 1:ch]


def s5_tables(lam_re, lam_im, log_step, bre, bim, cre, cim):
    expand = (_iota2((S5G, S5N), 1) // S5P == _iota2((S5G, S5N), 0)).astype(F32)
    step = _dot_hi(jnp.exp(log_step), expand)
    lre = jnp.minimum(lam_re, -1e-4)
    lr = lre * step
    ang = lam_im * step
    mag = jnp.exp(lr)
    lb_re = mag * jnp.cos(ang)
    lb_im = mag * jnp.sin(ang)
    den = lre * lre + lam_im * lam_im
    coef_re = ((lb_re - 1.0) * lre + lb_im * lam_im) / den
    coef_im = (lb_im * lre - (lb_re - 1.0) * lam_im) / den
    bb_re = coef_re * bre - coef_im * bim
    bb_im = coef_re * bim + coef_im * bre
    j = _iota2((S5_CH, 1), 0).astype(F32)
    jc = j - S5_CH // 2
    e0 = jnp.exp(jc * lr)
    e1 = jnp.exp((j + 1.0) * lr)
    ei = jnp.exp(-jc * lr)
    mask = (_iota2((LANES, 512), 0) // S5C == _iota2((LANES, 512), 1) // S5P).astype(F32)

    def blocks(t):
        return jnp.concatenate([(jnp.tile(t[:, gb * 512:(gb + 1) * 512], (LANES // S5C, 1)) * mask)[None]
                                for gb in range(GB)], axis=0)

    return (blocks(bb_re), blocks(bb_im), blocks(cre), blocks(cim),
            e0 * jnp.cos(jc * ang), e0 * jnp.sin(jc * ang),
            e1 * jnp.cos((j + 1.0) * ang), e1 * jnp.sin((j + 1.0) * ang),
            ei * jnp.cos(jc * ang), -ei * jnp.sin(jc * ang))


def _row_specs(tiled, batch, bcast, tm, tpb):
    specs = [pl.BlockSpec((tm, a.shape[1]), lambda i: (i, 0)) for a in tiled]
    specs += [pl.BlockSpec((None,) + a.shape[1:], lambda i: (i // tpb, 0, 0)) for a in batch]
    specs += [pl.BlockSpec(a.shape, lambda i, nd=a.ndim: (0,) * nd) for a in bcast]
    return specs


def ew_call(name, fn, tiled, batch, bcast, outs, tm, seq):
    t_rows = tiled[0].shape[0]
    n_in = len(tiled) + len(batch) + len(bcast)

    def body(*refs):
        vals = [r[...].astype(F32) for r in refs[:n_in]]
        for r, o in zip(refs[n_in:], fn(*vals)):
            r[...] = o.astype(r.dtype)

    return pl.pallas_call(
        body, grid=(t_rows // tm,), in_specs=_row_specs(tiled, batch, bcast, tm, seq // tm),
        out_specs=[pl.BlockSpec((tm, w), lambda i: (i, 0)) for w, _ in outs],
        out_shape=[SDS((t_rows, w), dt) for w, dt in outs], name=name, compiler_params=_cp(1))(*tiled, *batch, *bcast)


def ew_vjp_call(name, fn, tiled, batch, bcast, cts, want, tm, seq, addend=None):
    t_rows = tiled[0].shape[0]
    tpb = seq // tm
    n_t, n_b, n_c = len(tiled), len(batch), len(bcast)
    n_in = n_t + n_b + n_c
    extra = [] if addend is None else [addend]

    def body(*refs):
        i = pl.program_id(0)
        vals = [r[...].astype(F32) for r in refs[:n_in]]
        ctv = tuple(r[...].astype(F32) for r in refs[n_in:n_in + len(cts)])
        outs = refs[n_in + len(cts) + len(extra):]
        _, vjp = jax.vjp(fn, *vals)
        grads = vjp(ctv)
        for k, (r, (idx, _)) in enumerate(zip(outs[:len(want)], want)):
            g = grads[idx]
            if k == 0 and extra:
                g = g + refs[n_in + len(cts)][...]
            r[...] = g.astype(r.dtype)
        for k in range(n_b):
            r, g = outs[len(want) + k], grads[n_t + k]

            @pl.when(i % tpb == 0)
            def _(r=r, g=g):
                r[...] = g

            @pl.when(i % tpb != 0)
            def _(r=r, g=g):
                r[...] += g
        for k in range(n_c):
            r, g = outs[len(want) + n_b + k], grads[n_t + n_b + k]

            @pl.when(i == 0)
            def _(r=r, g=g):
                r[...] = g

            @pl.when(i != 0)
            def _(r=r, g=g):
                r[...] += g

    out_specs = [pl.BlockSpec((tm, tiled[idx].shape[1]), lambda i: (i, 0)) for idx, _ in want]
    out_specs += [pl.BlockSpec((None,) + a.shape[1:], lambda i: (i // tpb, 0, 0)) for a in batch]
    out_specs += [pl.BlockSpec(a.shape, lambda i, nd=a.ndim: (0,) * nd) for a in bcast]
    out_shape = [SDS(tiled[idx].shape, dt) for idx, dt in want]
    out_shape += [SDS(a.shape, F32) for a in batch] + [SDS(a.shape, F32) for a in bcast]
    res = pl.pallas_call(
        body, grid=(t_rows // tm,),
        in_specs=_row_specs(tiled, batch, bcast, tm, tpb)
        + [pl.BlockSpec((tm, a.shape[1]), lambda i: (i, 0)) for a in list(cts) + extra],
        out_specs=out_specs, out_shape=out_shape, name=name, compiler_params=_cp(1))(*tiled, *batch, *bcast, *cts, *extra)
    return res[:len(want)], res[len(want):len(want) + n_b], res[len(want) + n_b:]


def _pick(n, cands):
    for c in cands:
        if n % c == 0:
            return c
    return n


def mm_tn(name, a, b, exchange=None, gather=None):
    t_rows, m = a.shape
    n = b.shape[1]
    tn = n if n <= 1024 else _pick(n, (1024, 512, 256, 128))
    tm = max([t for t in range(LANES, m + 1, LANES) if m % t == 0 and t * tn * 4 <= ACC_LIMIT] or [m])
    tk = _pick(t_rows, (512, 256, 128, 64))
    grid = (m // tm, n // tn, t_rows // tk)
    extra = [x for x in (exchange, gather) if x is not None]
    ne = len(extra)

    def body(*refs):
        a_ref, b_ref = refs[:2]
        o_ref, acc = refs[2 + ne], refs[3 + 2 * ne]
        i, j, k = pl.program_id(0), pl.program_id(1), pl.program_id(2)
        first = (i == 0) & (j == 0) & (k == 0)
        last = (i == grid[0] - 1) & (j == grid[1] - 1) & (k == grid[2] - 1)
        at_end = []
        for e, x in enumerate(extra):
            comm_refs = (refs[2 + e], refs[3 + ne + e]) + tuple(refs[4 + 2 * ne + 3 * e:7 + 2 * ne + 3 * e])
            if x is exchange:
                start, finish = _exchange_phases(*comm_refs)
                at_end.append(finish)
            else:
                start, forward, finish = _gather_phases(*comm_refs)
                at_end += [forward, finish]
            pl.when(first)(start)

        @pl.when(k == 0)
        def _():
            acc[...] = jnp.zeros_like(acc)

        acc[...] += _dot_tn(a_ref[...], b_ref[...])

        @pl.when(k == grid[2] - 1)
        def _():
            o_ref[...] = acc[...].astype(BF16)

        for phase in at_end:
            pl.when(last)(phase)

    res = pl.pallas_call(
        body, grid=grid,
        in_specs=[pl.BlockSpec((tk, tm), lambda i, j, k: (k, i)), pl.BlockSpec((tk, tn), lambda i, j, k: (k, j))]
        + [HBM_SPEC] * ne,
        out_specs=[pl.BlockSpec((tm, tn), lambda i, j, k: (i, j))] + [HBM_SPEC] * ne,
        out_shape=[SDS((m, n), BF16)] + [SDS(x.shape if x is exchange else (NDEV,) + x.shape, x.dtype) for x in extra],
        scratch_shapes=[pltpu.VMEM((tm, tn), F32)] + _comm_scratch() * ne, name=name,
        compiler_params=_cp(3))(a, b, *extra)
    return res if extra else res[0]


def _ffn_weight_spec():
    if FFN_TF == FF:
        return pl.BlockSpec((FF, D), lambda i, j: (0, 0), pipeline_mode=pl.Buffered(1))
    return pl.BlockSpec((FFN_TF, D), lambda i, j: (j, 0))


def ffn_fwd(name, h, mod3, g, w1, w3, w2, seq, gather=None):
    t_rows = h.shape[0]
    tm = _pick(seq, (FFN_FWD_TM, 128, 64))
    tf = FFN_TF
    tpb = seq // tm
    nf = FF // tf
    nt = t_rows // tm
    extra = [] if gather is None else [gather]

    def body(*refs):
        h_ref, mod_ref, g_ref, w1_ref, w3_ref, w2_ref = refs[:6]
        ho_ref, f_ref, u_ref, h1_ref, h3_ref = refs[6 + len(extra):11 + len(extra)]
        acc = refs[11 + 2 * len(extra)]
        i, j = pl.program_id(0), pl.program_id(1)
        if extra:
            start, forward, finish = _gather_phases(refs[6], refs[12], *refs[14:17])
            pl.when((i == 0) & (j == 0))(start)
            pl.when((i == nt - 1) & (j == 0))(forward)

        @pl.when(j == 0)
        def _():
            u_ref[...] = normmod(h_ref[...], g_ref[...], mod_ref[1:2, :], mod_ref[0:1, :]).astype(BF16)
            acc[...] = jnp.zeros_like(acc)

        u = u_ref[...]
        h1 = _dot_nt(u, w1_ref[...])
        h3 = _dot_nt(u, w3_ref[...])
        h1_ref[...] = h1.astype(BF16)
        h3_ref[...] = h3.astype(BF16)
        acc[...] += _dot(_silu(h1) * h3, w2_ref[...])

        @pl.when(j == nf - 1)
        def _():
            f_ref[...] = acc[...]
            ho_ref[...] = h_ref[...] + 0.5 * mod_ref[2:3, :] * acc[...]

        if extra:
            pl.when((i == nt - 1) & (j == nf - 1))(finish)

    row = lambda i, j: (i, 0)
    return pl.pallas_call(
        body, grid=(nt, nf),
        in_specs=[pl.BlockSpec((tm, D), row), pl.BlockSpec((None, 3, D), lambda i, j: (i // tpb, 0, 0)),
                  pl.BlockSpec((1, D), lambda i, j: (0, 0)), _ffn_weight_spec(), _ffn_weight_spec(), _ffn_weight_spec()]
        + [HBM_SPEC] * len(extra),
        out_specs=[pl.BlockSpec((tm, D), row), pl.BlockSpec((tm, D), row), pl.BlockSpec((tm, D), row),
                   pl.BlockSpec((tm, tf), lambda i, j: (i, j)), pl.BlockSpec((tm, tf), lambda i, j: (i, j))]
        + [HBM_SPEC] * len(extra),
        out_shape=[SDS((t_rows, D), F32), SDS((t_rows, D), F32), SDS((t_rows, D), BF16), SDS((t_rows, FF), BF16),
                   SDS((t_rows, FF), BF16)] + [SDS((NDEV,) + x.shape, x.dtype) for x in extra],
        scratch_shapes=[pltpu.VMEM((tm, D), F32)] + (_comm_scratch() if extra else []), name=name,
        compiler_params=_cp(2))(h, mod3, g, w1, w3, w2, *extra)


def ffn_bwd(name, dho, h, f_out, h1_in, h3_in, mod3, g, w1, w3, w2, seq, exchange=None):
    t_rows = h.shape[0]
    tm = _pick(seq, (FFN_BWD_TM, 128, 64))
    tf = FFN_TF
    tpb = seq // tm
    nf = FF // tf
    nt = t_rows // tm
    extra = [] if exchange is None else [exchange]

    def body(*refs):
        dho_ref, h_ref, f_ref, h1_ref, h3_ref, mod_ref, g_ref, w1_ref, w3_ref, w2_ref = refs[:10]
        dh_ref, a_ref, dh1_ref, dh3_ref, df_scr, dmod_ref, dg_ref = refs[10 + len(extra):17 + len(extra)]
        du_acc = refs[17 + 2 * len(extra)]
        i, j = pl.program_id(0), pl.program_id(1)
        if extra:
            start, finish = _exchange_phases(refs[10], refs[18], *refs[20:23])
            pl.when((i == 0) & (j == 0))(start)

        @pl.when(j == 0)
        def _():
            df_scr[...] = (0.5 * mod_ref[2:3, :] * dho_ref[...]).astype(BF16)
            du_acc[...] = jnp.zeros_like(du_acc)

        h1 = h1_ref[...].astype(F32)
        h3 = h3_ref[...].astype(F32)
        sg = jax.nn.sigmoid(h1)
        s = h1 * sg
        da = _dot_nt(df_scr[...], w2_ref[...])
        dh3 = (da * s).astype(BF16)
        dh1 = (da * h3 * (sg * (1.0 + h1 * (1.0 - sg)))).astype(BF16)
        a_ref[...] = (s * h3).astype(BF16)
        dh1_ref[...] = dh1
        dh3_ref[...] = dh3
        du_acc[...] += _dot(dh1, w1_ref[...]) + _dot(dh3, w3_ref[...])

        @pl.when(j == nf - 1)
        def _():
            _, vjp = jax.vjp(normmod, h_ref[...], g_ref[...], mod_ref[1:2, :], mod_ref[0:1, :])
            dh_n, dg, dsc, dsh = vjp(du_acc[...])
            dh_ref[...] = dho_ref[...] + dh_n
            dgt = jnp.sum(0.5 * dho_ref[...] * f_ref[...], axis=0, keepdims=True)
            dmod = jnp.concatenate([dsh, dsc, dgt], axis=0)

            @pl.when(i % tpb == 0)
            def _():
                dmod_ref[...] = dmod

            @pl.when(i % tpb != 0)
            def _():
                dmod_ref[...] += dmod

            @pl.when(i == 0)
            def _():
                dg_ref[...] = dg

            @pl.when(i != 0)
            def _():
                dg_ref[...] += dg

        if extra:
            pl.when((i == nt - 1) & (j == nf - 1))(finish)

    row = lambda i, j: (i, 0)
    col = lambda i, j: (i, j)
    return pl.pallas_call(
        body, grid=(nt, nf),
        in_specs=[pl.BlockSpec((tm, D), row), pl.BlockSpec((tm, D), row), pl.BlockSpec((tm, D), row),
                  pl.BlockSpec((tm, tf), col), pl.BlockSpec((tm, tf), col),
                  pl.BlockSpec((None, 3, D), lambda i, j: (i // tpb, 0, 0)),
                  pl.BlockSpec((1, D), lambda i, j: (0, 0)), _ffn_weight_spec(), _ffn_weight_spec(), _ffn_weight_spec()]
        + [HBM_SPEC] * len(extra),
        out_specs=[pl.BlockSpec((tm, D), row), pl.BlockSpec((tm, tf), col), pl.BlockSpec((tm, tf), col),
                   pl.BlockSpec((tm, tf), col), pl.BlockSpec((tm, D), row),
                   pl.BlockSpec((None, 3, D), lambda i, j: (i // tpb, 0, 0)), pl.BlockSpec((1, D), lambda i, j: (0, 0))]
        + [HBM_SPEC] * len(extra),
        out_shape=[SDS((t_rows, D), F32), SDS((t_rows, FF), BF16), SDS((t_rows, FF), BF16), SDS((t_rows, FF), BF16),
                   SDS((t_rows, D), BF16), SDS(mod3.shape, F32), SDS((1, D), F32)] + [SDS(x.shape, x.dtype) for x in extra],
        scratch_shapes=[pltpu.VMEM((tm, D), F32)] + (_comm_scratch() if extra else []), name=name,
        compiler_params=_cp(2))(dho, h, f_out, h1_in, h3_in, mod3, g, w1, w3, w2, *extra)


def _resident(shape):
    return pl.BlockSpec(shape, lambda i: (0,) * len(shape), pipeline_mode=pl.Buffered(1))


def mix_in_fwd(h, sh, sc, g, ws, seq):
    t_rows = h.shape[0]
    tm = _pick(seq, (256, 128, 64))
    tpb = seq // tm
    nw = len(ws)

    def body(h_ref, sh_ref, sc_ref, g_ref, *rest):
        u = normmod(h_ref[...], g_ref[...], sc_ref[...], sh_ref[...]).astype(BF16)
        rest[nw][...] = u
        for w_ref, p_ref in zip(rest[:nw], rest[nw + 1:]):
            p_ref[...] = _dot_nt(u, w_ref[...])

    row = lambda i: (i, 0)
    batch = pl.BlockSpec((None, 1, D), lambda i: (i // tpb, 0, 0))
    return pl.pallas_call(
        body, grid=(t_rows // tm,),
        in_specs=[pl.BlockSpec((tm, D), row), batch, batch, pl.BlockSpec((1, D), lambda i: (0, 0))]
        + [_resident(w.shape) for w in ws],
        out_specs=[pl.BlockSpec((tm, D), row)] + [pl.BlockSpec((tm, w.shape[0]), row) for w in ws],
        out_shape=[SDS((t_rows, D), BF16)] + [SDS((t_rows, w.shape[0]), F32) for w in ws], name="mix_in_fwd",
        compiler_params=_cp(1))(h, sh, sc, g, *ws)


def mix_in_bwd(dps, ws, h, sh, sc, g, dh_add, seq):
    t_rows = h.shape[0]
    tm = _pick(seq, (256, 128, 64))
    tpb = seq // tm
    nw = len(ws)

    def body(*refs):
        h_ref, sh_ref, sc_ref, g_ref, add_ref, dh_ref, dsh_ref, dsc_ref, dg_ref = refs[2 * nw:]
        i = pl.program_id(0)
        du = _dot(refs[0][...], refs[nw][...])
        for k in range(1, nw):
            du = du + _dot(refs[k][...], refs[nw + k][...])
        _, vjp = jax.vjp(normmod, h_ref[...], g_ref[...], sc_ref[...], sh_ref[...])
        dh_n, dg, dsc, dsh = vjp(du)
        dh_ref[...] = add_ref[...] + dh_n

        @pl.when(i % tpb == 0)
        def _():
            dsh_ref[...] = dsh
            dsc_ref[...] = dsc

        @pl.when(i % tpb != 0)
        def _():
            dsh_ref[...] += dsh
            dsc_ref[...] += dsc

        @pl.when(i == 0)
        def _():
            dg_ref[...] = dg

        @pl.when(i != 0)
        def _():
            dg_ref[...] += dg

    row = lambda i: (i, 0)
    batch = pl.BlockSpec((None, 1, D), lambda i: (i // tpb, 0, 0))
    gain = pl.BlockSpec((1, D), lambda i: (0, 0))
    return pl.pallas_call(
        body, grid=(t_rows // tm,),
        in_specs=[pl.BlockSpec((tm, dp.shape[1]), row) for dp in dps] + [_resident(w.shape) for w in ws]
        + [pl.BlockSpec((tm, D), row), batch, batch, gain, pl.BlockSpec((tm, D), row)],
        out_specs=[pl.BlockSpec((tm, D), row), batch, batch, gain],
        out_shape=[SDS((t_rows, D), F32), SDS(sh.shape, F32), SDS(sc.shape, F32), SDS((1, D), F32)], name="mix_in_bwd",
        compiler_params=_cp(1))(*dps, *ws, h, sh, sc, g, dh_add)


def mix_out_fwd(merged, w_out, h_prev, gt, seq):
    t_rows = merged.shape[0]
    tm = _pick(seq, (256, 128, 64))
    tpb = seq // tm

    def body(m_ref, w_ref, h_ref, gt_ref, mo_ref, ho_ref):
        mo = _dot(m_ref[...], w_ref[...])
        mo_ref[...] = mo
        ho_ref[...] = h_ref[...] + gt_ref[...] * mo

    row = lambda i: (i, 0)
    return pl.pallas_call(
        body, grid=(t_rows // tm,),
        in_specs=[pl.BlockSpec((tm, D), row), _resident(w_out.shape), pl.BlockSpec((tm, D), row),
                  pl.BlockSpec((None, 1, D), lambda i: (i // tpb, 0, 0))],
        out_specs=[pl.BlockSpec((tm, D), row), pl.BlockSpec((tm, D), row)],
        out_shape=[SDS((t_rows, D), F32), SDS((t_rows, D), F32)], name="mix_out_fwd",
        compiler_params=_cp(1))(merged, w_out, h_prev, gt)


def mix_out_bwd(dh, mo, w_out, gt, seq):
    t_rows = dh.shape[0]
    tm = _pick(seq, (256, 128, 64))
    tpb = seq // tm

    def body(dh_ref, mo_ref, w_ref, gt_ref, dmo_ref, dm_ref, dgt_ref):
        i = pl.program_id(0)
        dmo = (gt_ref[...] * dh_ref[...]).astype(BF16)
        dmo_ref[...] = dmo
        dm_ref[...] = _dot_nt(dmo, w_ref[...])
        dgt = jnp.sum(dh_ref[...] * mo_ref[...], axis=0, keepdims=True)

        @pl.when(i % tpb == 0)
        def _():
            dgt_ref[...] = dgt

        @pl.when(i % tpb != 0)
        def _():
            dgt_ref[...] += dgt

    row = lambda i: (i, 0)
    batch = pl.BlockSpec((None, 1, D), lambda i: (i // tpb, 0, 0))
    return pl.pallas_call(
        body, grid=(t_rows // tm,),
        in_specs=[pl.BlockSpec((tm, D), row), pl.BlockSpec((tm, D), row), _resident(w_out.shape), batch],
        out_specs=[pl.BlockSpec((tm, D), row), pl.BlockSpec((tm, D), row), batch],
        out_shape=[SDS((t_rows, D), BF16), SDS((t_rows, D), F32), SDS(gt.shape, F32)], name="mix_out_bwd",
        compiler_params=_cp(1))(dh, mo, w_out, gt)


def _dn_cols(part, hd):
    return slice(part * DNW + hd * DH, part * DNW + (hd + 1) * DH)


def _qkv_stacks(qkv_ref, nb):
    pairs = [(b, hd) for b in range(nb) for hd in range(NH)]
    return [jnp.stack([qkv_ref[b, :, _dn_cols(part, hd)] for b, hd in pairs]) for part in range(3)]


def dn_prep_fwd(p_dn, conv8):
    bl, seq, _ = p_dn.shape
    tp = _pick(seq, (256, 128, 64))

    def body(raw_ref, halo_ref, conv_ref, o_ref):
        hm = (pl.program_id(1) > 0).astype(F32)
        o_ref[...] = dn_prep(jnp.concatenate([halo_ref[...] * hm, raw_ref[...]], axis=0), conv_ref[...])

    return pl.pallas_call(
        body, grid=(bl, seq // tp),
        in_specs=[pl.BlockSpec((None, tp, 3 * DNW), lambda b, i: (b, i, 0)),
                  pl.BlockSpec((None, 8, 3 * DNW), lambda b, i: (b, jnp.maximum(i * (tp // 8) - 1, 0), 0)),
                  pl.BlockSpec((8, 3 * DNW), lambda b, i: (0, 0))],
        out_specs=pl.BlockSpec((None, tp, 3 * DNW), lambda b, i: (b, i, 0)),
        out_shape=SDS((bl, seq, 3 * DNW), F32), name="dn_prep_fwd", compiler_params=_cp(2))(p_dn, p_dn, conv8)


def dn_prep_bwd(p_dn, conv8, d_qkv, d_z):
    bl, seq, _ = p_dn.shape
    tp = _pick(seq, (256, 128, 64))
    nt = seq // tp

    def body(raw_ref, halo_ref, conv_ref, dq_ref, dz_ref, draw_ref, dconv_ref, carry):
        b, r = pl.program_id(0), pl.program_id(1)

        @pl.when((b == 0) & (r == 0))
        def _():
            dconv_ref[...] = jnp.zeros_like(dconv_ref)

        @pl.when(r == 0)
        def _():
            carry[...] = jnp.zeros_like(carry)

        hm = (r < nt - 1).astype(F32)
        _, vjp = jax.vjp(dn_prep, jnp.concatenate([halo_ref[...] * hm, raw_ref[...]], axis=0), conv_ref[...])
        dxc, dw = vjp(dq_ref[...])
        tail = dxc[tp:tp + 8] + carry[...]
        draw_ref[:, 0:3 * DNW] = jnp.concatenate([dxc[8:tp], tail], axis=0).astype(BF16)
        draw_ref[:, 3 * DNW:4 * DNW] = dz_ref[...].astype(BF16)
        carry[...] = dxc[0:8] * hm
        dconv_ref[...] += dw

    blk = lambda b, r: (b, nt - 1 - r, 0)
    return pl.pallas_call(
        body, grid=(bl, nt),
        in_specs=[pl.BlockSpec((None, tp, 3 * DNW), blk),
                  pl.BlockSpec((None, 8, 3 * DNW), lambda b, r: (b, jnp.maximum((nt - 1 - r) * (tp // 8) - 1, 0), 0)),
                  pl.BlockSpec((8, 3 * DNW), lambda b, r: (0, 0)), pl.BlockSpec((None, tp, 3 * DNW), blk),
                  pl.BlockSpec((None, tp, DNW), blk)],
        out_specs=[pl.BlockSpec((None, tp, 4 * DNW), blk), pl.BlockSpec((8, 3 * DNW), lambda b, r: (0, 0))],
        out_shape=[SDS((bl, seq, 4 * DNW), BF16), SDS((8, 3 * DNW), F32)],
        scratch_shapes=[pltpu.VMEM((8, 3 * DNW), F32)], name="dn_prep_bwd", compiler_params=_cp(2))(p_dn, p_dn, conv8, d_qkv, d_z)


def _gate_stacks(gates, nb):
    pairs = [(b, hd) for b in range(nb) for hd in range(NH)]
    bs = jnp.stack([gates[b][0][:, hd:hd + 1] for b, hd in pairs])
    gs = jnp.stack([gates[b][1][:, NH + hd:NH + hd + 1] for b, hd in pairs])
    gts = jnp.stack([gates[b][2][NH + hd:NH + hd + 1, :] for b, hd in pairs])
    return bs, gs, gts


def deltanet_fwd(qkv, p_small, alp, dtp, nb):
    bl, seq, _ = qkv.shape
    nc = seq // CH
    ng = nb * NH

    def body(qkv_ref, small_ref, alp_ref, dtp_ref, o_ref, sprev_ref, tinv_ref, s_scr):
        @pl.when(pl.program_id(1) == 0)
        def _():
            s_scr[...] = jnp.zeros_like(s_scr)

        gates = [gate_fn(small_ref[b], alp_ref[...], dtp_ref[...]) for b in range(nb)]
        s_prev = s_scr[...]
        o, s_new, tinv = dn_chunk(*_qkv_stacks(qkv_ref, nb), *_gate_stacks(gates, nb), s_prev)
        sprev_ref[...] = s_prev
        tinv_ref[...] = tinv
        s_scr[...] = s_new
        for b in range(nb):
            for hd in range(NH):
                o_ref[b, :, hd * DH:(hd + 1) * DH] = o[b * NH + hd]

    blk = lambda bb, n: (bb, n, 0)
    const = lambda bb, n: (0, 0)
    saved = pl.BlockSpec((None, ng, DH, DH), lambda bb, n: (bb * nc + n, 0, 0, 0))
    return pl.pallas_call(
        body, grid=(bl // nb, nc),
        in_specs=[pl.BlockSpec((nb, CH, 3 * DNW), blk), pl.BlockSpec((nb, CH, LANES), blk),
                  pl.BlockSpec((1, LANES), const), pl.BlockSpec((1, LANES), const)],
        out_specs=[pl.BlockSpec((nb, CH, DNW), blk), saved, saved],
        out_shape=[SDS((bl, seq, DNW), F32), SDS((bl // nb * nc, ng, DH, DH), F32), SDS((bl // nb * nc, ng, DH, DH), F32)],
        scratch_shapes=[pltpu.VMEM((ng, DH, DH), F32)], name="deltanet_fwd",
        compiler_params=_cp(2))(qkv, p_small, alp, dtp)


def deltanet_bwd(qkv, p_small, alp, dtp, sprev, tinv, d_o, nb, exchange=None):
    bl, seq, _ = qkv.shape
    nc = seq // CH
    ng = nb * NH
    extra = [] if exchange is None else [exchange]

    def body(*refs):
        qkv_ref, small_ref, alp_ref, dtp_ref, sprev_ref, tinv_ref, do_ref = refs[:7]
        dqkv_ref, dsmall_ref, dalp_ref, ddtp_ref = refs[7 + len(extra):11 + len(extra)]
        ds_scr = refs[11 + 2 * len(extra)]
        bb, r = pl.program_id(0), pl.program_id(1)
        if extra:
            start, finish = _exchange_phases(refs[7], refs[12], *refs[14:17])
            pl.when((bb == 0) & (r == 0))(start)

        @pl.when((bb == 0) & (r == 0))
        def _():
            dalp_ref[...] = jnp.zeros_like(dalp_ref)
            ddtp_ref[...] = jnp.zeros_like(ddtp_ref)

        @pl.when(r == 0)
        def _():
            ds_scr[...] = jnp.zeros_like(ds_scr)

        gates, gate_vjps = [], []
        for b in range(nb):
            out, gvjp = jax.vjp(gate_fn, small_ref[b], alp_ref[...], dtp_ref[...])
            gates.append(out)
            gate_vjps.append(gvjp)
        t_saved = tinv_ref[...]
        _, vjp = jax.vjp(lambda *args: dn_chunk(*args, t_saved)[:2], *_qkv_stacks(qkv_ref, nb), *_gate_stacks(gates, nb),
                         sprev_ref[...])
        d_out = jnp.stack([do_ref[b, :, hd * DH:(hd + 1) * DH] for b in range(nb) for hd in range(NH)])
        grads = vjp((d_out, ds_scr[...]))
        ds_scr[...] = grads[6]
        lane = _iota2((CH, LANES), 1)
        rowi = _iota2((LANES, CH), 0)
        for b in range(nb):
            d_beta = jnp.zeros((CH, LANES), F32)
            d_gc = jnp.zeros((CH, LANES), F32)
            d_gct = jnp.zeros((LANES, CH), F32)
            for hd in range(NH):
                i = b * NH + hd
                for part in range(3):
                    dqkv_ref[b, :, _dn_cols(part, hd)] = grads[part][i]
                d_beta = d_beta + jnp.where(lane == hd, grads[3][i], 0.0)
                d_gc = d_gc + jnp.where(lane == NH + hd, grads[4][i], 0.0)
                d_gct = d_gct + jnp.where(rowi == NH + hd, grads[5][i], 0.0)
            d_small, d_alp, d_dtp = gate_vjps[b]((d_beta, d_gc, d_gct))
            dsmall_ref[b] = d_small.astype(BF16)
            dalp_ref[...] += d_alp
            ddtp_ref[...] += d_dtp
        if extra:
            pl.when((bb == bl // nb - 1) & (r == nc - 1))(finish)

    blk = lambda bb, r: (bb, nc - 1 - r, 0)
    const = lambda bb, r: (0, 0)
    saved = pl.BlockSpec((None, ng, DH, DH), lambda bb, r: (bb * nc + nc - 1 - r, 0, 0, 0))
    return pl.pallas_call(
        body, grid=(bl // nb, nc),
        in_specs=[pl.BlockSpec((nb, CH, 3 * DNW), blk), pl.BlockSpec((nb, CH, LANES), blk), pl.BlockSpec((1, LANES), const),
                  pl.BlockSpec((1, LANES), const), saved, saved, pl.BlockSpec((nb, CH, DNW), blk)] + [HBM_SPEC] * len(extra),
        out_specs=[pl.BlockSpec((nb, CH, 3 * DNW), blk), pl.BlockSpec((nb, CH, LANES), blk), pl.BlockSpec((1, LANES), const),
                   pl.BlockSpec((1, LANES), const)] + [HBM_SPEC] * len(extra),
        out_shape=[SDS((bl, seq, 3 * DNW), F32), SDS((bl, seq, LANES), BF16), SDS((1, LANES), F32), SDS((1, LANES), F32)]
        + [SDS(x.shape, x.dtype) for x in extra],
        scratch_shapes=[pltpu.VMEM((ng, DH, DH), F32)] + (_comm_scratch() if extra else []), name="deltanet_bwd",
        compiler_params=_cp(2))(qkv, p_small, alp, dtp, sprev, tinv, d_o, *extra)


def _s5_table_specs():
    tab3 = pl.BlockSpec((None, LANES, 512), lambda gb, n: (gb, 0, 0))
    tab2 = pl.BlockSpec((S5_CH, 512), lambda gb, n: (0, gb))
    return [tab3] * 4 + [tab2] * 6 + [pl.BlockSpec((1, LANES), lambda gb, n: (0, gb))]


def s5_fwd(u, tables, dsk):
    bl, seq, _ = u.shape
    nc = seq // S5_CH

    def body(u_ref, *rest):
        tabs, (y_ref, xs_ref, xr_scr, xi_scr) = rest[:11], rest[11:]

        @pl.when(pl.program_id(1) == 0)
        def _():
            xr_scr[...] = jnp.zeros_like(xr_scr)
            xi_scr[...] = jnp.zeros_like(xi_scr)

        xp_re, xp_im = xr_scr[...], xi_scr[...]
        xs_ref[0:bl] = xp_re
        xs_ref[bl:2 * bl] = xp_im
        y, xn_re, xn_im = s5_chunk(u_ref[...], xp_re, xp_im, *[t[...] for t in tabs])
        y_ref[...] = y
        xr_scr[...] = xn_re
        xi_scr[...] = xn_im

    blk = lambda gb, n: (0, n, gb)
    return pl.pallas_call(
        body, grid=(GB, nc), in_specs=[pl.BlockSpec((bl, S5_CH, LANES), blk)] + _s5_table_specs(),
        out_specs=[pl.BlockSpec((bl, S5_CH, LANES), blk),
                   pl.BlockSpec((None, 2 * bl, 1, 512), lambda gb, n: (gb * nc + n, 0, 0, 0))],
        out_shape=[SDS((bl, seq, S5W), F32), SDS((GB * nc, 2 * bl, 1, 512), F32)],
        scratch_shapes=[pltpu.VMEM((bl, 1, 512), F32), pltpu.VMEM((bl, 1, 512), F32)], name="s5_fwd",
        compiler_params=_cp(2))(u, *tables, dsk)


def s5_bwd(u, tables, dsk, xs, dy):
    bl, seq, _ = u.shape
    nc = seq // S5_CH

    def body(u_ref, *rest):
        tabs, xs_ref, dy_ref = rest[:11], rest[11], rest[12]
        du_ref, dtabs, dxr_scr, dxi_scr = rest[13], rest[14:25], rest[25], rest[26]
        r = pl.program_id(1)

        @pl.when(r == 0)
        def _():
            for t in dtabs:
                t[...] = jnp.zeros_like(t)
            dxr_scr[...] = jnp.zeros_like(dxr_scr)
            dxi_scr[...] = jnp.zeros_like(dxi_scr)

        _, vjp = jax.vjp(s5_chunk, u_ref[...], xs_ref[0:bl], xs_ref[bl:2 * bl], *[t[...] for t in tabs])
        grads = vjp((dy_ref[...], dxr_scr[...], dxi_scr[...]))
        du_ref[...] = grads[0].astype(BF16)
        dxr_scr[...] = grads[1]
        dxi_scr[...] = grads[2]
        for t, g in zip(dtabs, grads[3:]):
            t[...] += g

    blk = lambda gb, r: (0, nc - 1 - r, gb)
    tab_shapes = [SDS(t.shape, F32) for t in tables] + [SDS(dsk.shape, F32)]
    return pl.pallas_call(
        body, grid=(GB, nc),
        in_specs=[pl.BlockSpec((bl, S5_CH, LANES), blk)] + _s5_table_specs()
        + [pl.BlockSpec((None, 2 * bl, 1, 512), lambda gb, r: (gb * nc + nc - 1 - r, 0, 0, 0)), pl.BlockSpec((bl, S5_CH, LANES), blk)],
        out_specs=[pl.BlockSpec((bl, S5_CH, LANES), blk)] + _s5_table_specs(),
        out_shape=[SDS((bl, seq, S5W), BF16)] + tab_shapes,
        scratch_shapes=[pltpu.VMEM((bl, 1, 512), F32), pltpu.VMEM((bl, 1, 512), F32)], name="s5_bwd",
        compiler_params=_cp(2))(u, *tables, dsk, xs, dy)


def s5_tables_fwd(params):
    shapes = [SDS((GB, LANES, 512), F32)] * 4 + [SDS((S5_CH, S5N), F32)] * 6

    def body(*refs):
        for r, t in zip(refs[7:], s5_tables(*[p[...] for p in refs[:7]])):
            r[...] = t

    return pl.pallas_call(body, out_shape=shapes, name="s5_tables_fwd", compiler_params=_cp())(*params)


def s5_tables_bwd(params, dtables):
    def body(*refs):
        _, vjp = jax.vjp(s5_tables, *[p[...] for p in refs[:7]])
        for r, g in zip(refs[17:], vjp(tuple(t[...] for t in refs[7:17]))):
            r[...] = g

    return pl.pallas_call(body, out_shape=[SDS(p.shape, F32) for p in params], name="s5_tables_bwd",
                          compiler_params=_cp())(*params, *dtables)


def ada_fwd(c_all, w_loc, b_loc):
    def body(c_ref, w_ref, b_ref, o_ref):
        o_ref[...] = _dot(_silu(c_ref[...]), w_ref[...]) + b_ref[...]

    return pl.pallas_call(body, out_shape=SDS((c_all.shape[0], w_loc.shape[1]), F32), name="ada_fwd",
                          compiler_params=_cp())(c_all, w_loc, b_loc)


def ada_bwd(c_all, dmod_mine, dmod_all):
    def body(c_ref, dm_ref, da_ref, gw_ref, gb_ref):
        gw_ref[...] = _dot_tn(_silu(c_ref[...]), dm_ref[...])
        gb_ref[...] = jnp.sum(da_ref[...], axis=0, keepdims=True)

    return pl.pallas_call(body, out_shape=[SDS((D, dmod_mine.shape[1]), F32), SDS((1, dmod_all.shape[1]), F32)],
                          name="ada_bwd", compiler_params=_cp())(c_all, dmod_mine, dmod_all)


def loss_head(h, tgt, g, seq):
    t_rows = h.shape[0]
    tm = _pick(seq, (256, 128, 64))

    def body(h_ref, t_ref, g_ref, dh_ref, dg_ref, loss_ref):
        i = pl.program_id(0)
        y, vjp = jax.vjp(lambda hh, gg: hh * lax.rsqrt(jnp.mean(hh * hh, axis=-1, keepdims=True) + EPS) * gg,
                         h_ref[...], g_ref[...])
        e = y - t_ref[...]
        dh, dg = vjp(e * (1.0 / D))
        part = jnp.sum(jnp.sum(e * e, axis=1, keepdims=True), axis=0, keepdims=True) * (0.5 / D) + jnp.zeros((1, LANES), F32)
        dh_ref[...] = dh

        @pl.when(i == 0)
        def _():
            dg_ref[...] = dg
            loss_ref[...] = part

        @pl.when(i != 0)
        def _():
            dg_ref[...] += dg
            loss_ref[...] += part

    row = lambda i: (i, 0)
    const = lambda i: (0, 0)
    return pl.pallas_call(
        body, grid=(t_rows // tm,),
        in_specs=[pl.BlockSpec((tm, D), row), pl.BlockSpec((tm, D), row), pl.BlockSpec((1, D), const)],
        out_specs=[pl.BlockSpec((tm, D), row), pl.BlockSpec((1, D), const), pl.BlockSpec((1, LANES), const)],
        out_shape=[SDS((t_rows, D), F32), SDS((1, D), F32), SDS((1, LANES), F32)], name="loss_head",
        compiler_params=_cp(1))(h, tgt, g)


def adamw(name, parts, w, m, v):
    k_parts, rows, cols = parts.shape
    tr = _pick(rows, (256, 128, 64, 32, 16, 8))

    def body(p_ref, w_ref, m_ref, v_ref, g_ref, d_ref, mo_ref, vo_ref):
        g = p_ref[0].astype(F32)
        for k in range(1, k_parts):
            g = g + p_ref[k].astype(F32)
        _adam_store(g, w_ref, m_ref, v_ref, g_ref, d_ref, mo_ref, vo_ref)

    blk = pl.BlockSpec((tr, cols), lambda i: (i, 0))
    return pl.pallas_call(
        body, grid=(rows // tr,), in_specs=[pl.BlockSpec((k_parts, tr, cols), lambda i: (0, i, 0)), blk, blk, blk],
        out_specs=[blk] * 4, out_shape=[SDS((rows, cols), F32)] * 4, name=name, compiler_params=_cp(1))(parts, w, m, v)


def _adam_store(g, w_ref, m_ref, v_ref, g_ref, d_ref, mo_ref, vo_ref):
    m_new = ADAM_B1 * m_ref[...] + (1.0 - ADAM_B1) * g
    v_new = ADAM_B2 * v_ref[...] + (1.0 - ADAM_B2) * (g * g)
    m_hat = m_new / (1.0 - ADAM_B1 ** ADAM_STEP)
    v_hat = v_new / (1.0 - ADAM_B2 ** ADAM_STEP)
    g_ref[...] = g
    d_ref[...] = -ADAM_LR * (m_hat / (jnp.sqrt(v_hat) + ADAM_EPS) + ADAM_WD * w_ref[...])
    mo_ref[...] = m_new
    vo_ref[...] = v_new


def adamw_t(name, parts, w, m, v):
    k_parts, r, c = parts.shape
    tc = _pick(c, (256, 128))

    def body(p_ref, w_ref, m_ref, v_ref, g_ref, d_ref, mo_ref, vo_ref):
        gt = p_ref[0].astype(F32)
        for k in range(1, k_parts):
            gt = gt + p_ref[k].astype(F32)
        _adam_store(gt.T, w_ref, m_ref, v_ref, g_ref, d_ref, mo_ref, vo_ref)

    blk = pl.BlockSpec((tc, r), lambda j: (j, 0))
    return pl.pallas_call(
        body, grid=(c // tc,), in_specs=[pl.BlockSpec((k_parts, r, tc), lambda j: (0, 0, j)), blk, blk, blk],
        out_specs=[blk] * 4, out_shape=[SDS((c, r), F32)] * 4, name=name, compiler_params=_cp(1))(parts, w, m, v)


def _comm_scratch():
    return [pltpu.SemaphoreType.DMA((7,)), pltpu.SemaphoreType.DMA((7,)), pltpu.SemaphoreType.DMA]


HBM_SPEC = pl.BlockSpec(memory_space=pl.ANY)


def _gather_phases(x_ref, out_ref, send_sems, recv_sems, local_sem):
    mx, my, mc = lax.axis_index("x"), lax.axis_index("y"), lax.axis_index("c")
    me, sibling = (mx, my, mc), (mx, my, 1 - mc)
    chips = [(1 - mx, my), (mx, 1 - my), (1 - mx, 1 - my)]

    def slot(px, py, pc):
        return out_ref.at[4 * px + 2 * py + pc]

    def copy(k, block, to, src=None):
        return pltpu.make_async_remote_copy(
            src_ref=slot(*block) if src is None else src, dst_ref=slot(*block), send_sem=send_sems.at[k],
            recv_sem=recv_sems.at[k], device_id=to, device_id_type=pl.DeviceIdType.MESH)

    def first():
        return [copy(0, me, sibling, src=x_ref)] + [copy(1 + j, me, (*chip, mc), src=x_ref) for j, chip in enumerate(chips)]

    def passed():
        return [copy(4 + j, (*chip, mc), sibling) for j, chip in enumerate(chips)]

    def start():
        pltpu.make_async_copy(x_ref, slot(*me), local_sem).start()
        for cp in first():
            cp.start()

    def forward():
        for j, chip in enumerate(chips):
            copy(1 + j, (*chip, mc), me).wait_recv()
            passed()[j].start()

    def finish():
        copy(0, sibling, me).wait_recv()
        for j, chip in enumerate(chips):
            copy(4 + j, (*chip, 1 - mc), me).wait_recv()
        for cp in first() + passed():
            cp.wait_send()
        pltpu.make_async_copy(x_ref, slot(*me), local_sem).wait()

    return start, forward, finish


def _exchange_phases(x_ref, out_ref, send_sems, recv_sems, local_sem):
    mx, my, mc = lax.axis_index("x"), lax.axis_index("y"), lax.axis_index("c")
    me = 4 * mx + 2 * my + mc

    def peer(k):
        return mx ^ (k >> 2), my ^ ((k >> 1) & 1), mc ^ (k & 1)

    def sends():
        out = []
        for k in range(1, NDEV):
            px, py, pc = peer(k)
            out.append(pltpu.make_async_remote_copy(
                src_ref=x_ref.at[4 * px + 2 * py + pc], dst_ref=out_ref.at[me], send_sem=send_sems.at[k - 1],
                recv_sem=recv_sems.at[k - 1], device_id=(px, py, pc), device_id_type=pl.DeviceIdType.MESH))
        return out

    def start():
        pltpu.make_async_copy(x_ref.at[me], out_ref.at[me], local_sem).start()
        for cp in sends():
            cp.start()

    def finish():
        for k in range(1, NDEV):
            px, py, pc = peer(k)
            pltpu.make_async_remote_copy(
                src_ref=x_ref.at[me], dst_ref=out_ref.at[4 * px + 2 * py + pc], send_sem=send_sems.at[k - 1],
                recv_sem=recv_sems.at[k - 1], device_id=(px, py, pc), device_id_type=pl.DeviceIdType.MESH).wait_recv()
        for cp in sends():
            cp.wait_send()
        pltpu.make_async_copy(x_ref.at[me], out_ref.at[me], local_sem).wait()

    return start, finish


def all_gather(name, x):
    def body(x_ref, out_ref, send_sems, recv_sems, local_sem):
        for phase in _gather_phases(x_ref, out_ref, send_sems, recv_sems, local_sem):
            phase()

    return pl.pallas_call(body, out_shape=SDS((NDEV,) + x.shape, x.dtype), in_specs=[HBM_SPEC], out_specs=HBM_SPEC,
                          scratch_shapes=_comm_scratch(), name=name)(x)


def all_gather_pair(name, x1, x2):
    def body(x1_ref, x2_ref, o1_ref, o2_ref, *sems):
        first = _gather_phases(x1_ref, o1_ref, *sems[:3])
        second = _gather_phases(x2_ref, o2_ref, *sems[3:])
        for phase1, phase2 in zip(first, second):
            phase1()
            phase2()

    return pl.pallas_call(
        body, out_shape=[SDS((NDEV,) + x1.shape, x1.dtype), SDS((NDEV,) + x2.shape, x2.dtype)], in_specs=[HBM_SPEC] * 2,
        out_specs=[HBM_SPEC] * 2, scratch_shapes=_comm_scratch() + _comm_scratch(), name=name)(x1, x2)


def all_to_all(name, x):
    def body(x_ref, out_ref, send_sems, recv_sems, local_sem):
        for phase in _exchange_phases(x_ref, out_ref, send_sems, recv_sems, local_sem):
            phase()

    return pl.pallas_call(body, out_shape=SDS(x.shape, x.dtype), in_specs=[HBM_SPEC], out_specs=HBM_SPEC,
                          scratch_shapes=_comm_scratch(), name=name)(x)


def _pack(arrs, dtype, row_mult=8):
    segs = []
    for a in arrs:
        flat = a.reshape(-1).astype(dtype)
        segs.append(jnp.pad(flat, (0, (-flat.shape[0]) % ROW)))
    flat = jnp.concatenate(segs)
    flat = jnp.pad(flat, (0, (-flat.shape[0]) % (ROW * row_mult)))
    return flat.reshape(-1, ROW)


def _unpack(buf, shapes):
    flat = buf.reshape(-1)
    out, off = [], 0
    for s in shapes:
        n = math.prod(s)
        out.append(flat[off:off + n].reshape(s))
        off += n + (-n) % ROW
    return out


def _pack_rows(arrs, axis):
    padded = []
    for t in arrs:
        pad = [(0, 0)] * t.ndim
        pad[axis] = (0, _tile_rows(t.shape[axis]) - t.shape[axis])
        padded.append(jnp.pad(t, pad))
    return jnp.concatenate(padded, axis=axis)


def _tile_rows(r):
    return r + (-r) % BF16_TILE_ROWS


def _unpack8(buf, shapes):
    flat = buf.reshape(NDEV, -1)
    out, off = [], 0
    for s in shapes:
        n = math.prod(s)
        out.append(flat[:, off:off + n].reshape((NDEV,) + tuple(s)))
        off += n + (-n) % ROW
    return out


def kernel(x, c, w_ada, b_ada, g_ffn1, w1_ffn1, w3_ffn1, w2_ffn1, g_mix, w_in, conv_qkv, a_log, dt_bias, g_onorm, lam_re, lam_im, log_step, b_re, b_im, c_re, c_im, d_skip, w_glu, b_glu, w_proj_a, w_proj_b, w_out, g_ffn2, w1_ffn2, w3_ffn2, w2_ffn2, g_final, loss_target, m_w_ada, m_b_ada, m_g_ffn1, m_w1_ffn1, m_w3_ffn1, m_w2_ffn1, m_g_mix, m_w_in, m_conv_qkv, m_a_log, m_dt_bias, m_g_onorm, m_lam_re, m_lam_im, m_log_step, m_b_re, m_b_im, m_c_re, m_c_im, m_d_skip, m_w_glu, m_b_glu, m_w_proj_a, m_w_proj_b, m_w_out, m_g_ffn2, m_w1_ffn2, m_w3_ffn2, m_w2_ffn2, m_g_final, v_w_ada, v_b_ada, v_g_ffn1, v_w1_ffn1, v_w3_ffn1, v_w2_ffn1, v_g_mix, v_w_in, v_conv_qkv, v_a_log, v_dt_bias, v_g_onorm, v_lam_re, v_lam_im, v_log_step, v_b_re, v_b_im, v_c_re, v_c_im, v_d_skip, v_w_glu, v_b_glu, v_w_proj_a, v_w_proj_b, v_w_out, v_g_ffn2, v_w1_ffn2, v_w3_ffn2, v_w2_ffn2, v_g_final):
    a = dict(locals())
    bl, seq, _ = x.shape
    t_rows = bl * seq
    me = 4 * lax.axis_index("x") + 2 * lax.axis_index("y") + lax.axis_index("c")
    tm_ew = _pick(seq, (256, 128, 64))

    loc = {n: (a[n][0].T if n in COL_SHARDED else a[n][0]) for n in RS_WEIGHTS}
    wfull, gw, res = {}, {}, {}

    def pack_local(names):
        return _pack_rows([loc[n].astype(BF16).reshape(-1, ROW) for n in names], 0)

    def unpack_full(buf, names):
        r0 = 0
        for n in names:
            r = loc[n].size // ROW
            wfull[n] = buf[:, r0:r0 + r, :].reshape(-1, loc[n].shape[1])
            r0 += _tile_rows(r)

    def pack_grads(names):
        return _pack_rows([gw[n].astype(BF16).reshape(NDEV, -1, ROW) for n in names], 1)

    def update(buf, names):
        r0 = 0
        for n in names:
            r = loc[n].size // ROW
            parts = buf[:, r0:r0 + r, :].reshape((NDEV,) + loc[n].shape)
            r0 += _tile_rows(r)
            step = adamw_t if n in COL_SHARDED else adamw
            out = step("adamw_" + n, parts, a[n][0], a["m_" + n][0], a["v_" + n][0])
            for kind, t in zip(("grad", "delta", "new_m", "new_v"), out):
                res[kind + "_" + n] = t[None]

    sm, wg_ffn1 = all_gather_pair("gather_inputs", _pack([c, conv_qkv[0]], F32), pack_local(G_FFN1))
    unpack_full(wg_ffn1, G_FFN1)
    c_loc, conv_loc = _unpack8(sm, [c.shape, conv_qkv.shape[1:]])
    c_all = c_loc.reshape(NDEV * bl, D)
    conv_full = conv_loc.transpose(1, 0, 2).reshape(CONVW, 3 * DNW)

    n_ada = w_ada.shape[2]
    mod_part = ada_fwd(c_all, w_ada[0], lax.dynamic_slice(b_ada, (0, me * n_ada), (1, n_ada)))
    mod_all = all_gather("gather_mod", mod_part).transpose(1, 0, 2).reshape(NDEV * bl, 9 * D)
    mod = lax.dynamic_slice(mod_all, (me * bl, 0), (bl, 9 * D)).reshape(bl, 9, D)
    mods = [mod[:, k:k + 1, :] for k in range(9)]

    h0 = x.reshape(t_rows, D)
    h1, f1, u1, pa1, pb1, wg_rest = ffn_fwd("ffn1_fwd", h0, mod[:, 0:3, :], g_ffn1, wfull['w1_ffn1'], wfull['w3_ffn1'],
                                  wfull['w2_ffn1'], seq, gather=pack_local(G_MIX + G_FFN2))
    unpack_full(wg_rest, G_MIX + G_FFN2)
    win = wfull['w_in']
    o_small, o_s5, o_gate = 4 * DNW, 4 * DNW + 2 * NH, 4 * DNW + 2 * NH + S5W
    w_dn, w_small = win[:o_small], jnp.pad(win[o_small:o_s5], ((0, LANES - 2 * NH), (0, 0)))
    w_s5, w_gate = win[o_s5:o_gate], win[o_gate:]
    w_pieces = [w_dn, w_small, w_s5, w_gate]
    u2, p_dn, p_small, p_s5, p_gate = mix_in_fwd(h1, mods[3], mods[4], g_mix, w_pieces, seq)

    conv8 = jnp.pad(conv_full, ((0, 8 - CONVW), (0, 0)))
    alp = jnp.pad(a_log, ((0, 0), (NH, LANES - 2 * NH)))
    dtp = jnp.pad(dt_bias, ((0, 0), (NH, LANES - 2 * NH)))
    nb_dn = DN_ROWS if bl % DN_ROWS == 0 else 1
    p_dn3, p_small3 = p_dn.reshape(bl, seq, 4 * DNW), p_small.reshape(bl, seq, LANES)
    qkv3 = dn_prep_fwd(p_dn3, conv8)
    o_pre3, sprev, tinv = deltanet_fwd(qkv3, p_small3, alp, dtp, nb_dn)
    o_pre = o_pre3.reshape(t_rows, DNW)
    z_raw = p_dn[:, 3 * DNW:]

    s5_params = [lam_re.reshape(1, S5N), lam_im.reshape(1, S5N), log_step,
                 b_re[0].transpose(2, 0, 1).reshape(S5C, S5N), b_im[0].transpose(2, 0, 1).reshape(S5C, S5N),
                 c_re[0].transpose(1, 0, 2).reshape(S5C, S5N), c_im[0].transpose(1, 0, 2).reshape(S5C, S5N)]
    tables = s5_tables_fwd(s5_params)
    p_s53 = p_s5.reshape(bl, seq, S5W)
    y_s53, xs = s5_fwd(p_s53, tables, d_skip)
    y_s5 = y_s53.reshape(t_rows, S5W)
    tail_in = [o_pre, z_raw, y_s5, p_gate]
    tail_w = [g_onorm, wfull['w_glu'], b_glu, wfull['w_proj_a'], wfull['w_proj_b']]
    (merged,) = ew_call("mix_tail", fn_mix_tail, tail_in, [], tail_w, [(D, BF16)], tm_ew, seq)
    mo, h2 = mix_out_fwd(merged, wfull['w_out'], h1, mods[5], seq)
    h3, f3, u3, pa3, pb3 = ffn_fwd("ffn2_fwd", h2, mod[:, 6:9, :], g_ffn2, wfull['w1_ffn2'], wfull['w3_ffn2'], wfull['w2_ffn2'], seq)

    dh3, dg_final, loss_part = loss_head(h3, loss_target.reshape(t_rows, D), g_final.reshape(1, D), seq)

    dh2, a3, d1_3, d3_3, df3, dmod_c, dg_ffn2 = ffn_bwd("ffn2_bwd", dh3, h2, f3, pa3, pb3, mod[:, 6:9, :], g_ffn2, wfull['w1_ffn2'],
                                                   wfull['w3_ffn2'], wfull['w2_ffn2'], seq)
    gw['w1_ffn2'] = mm_tn("gw1_ffn2", d1_3, u3)
    gw['w3_ffn2'] = mm_tn("gw3_ffn2", d3_3, u3)
    gw['w2_ffn2'] = mm_tn("gw2_ffn2", a3, df3)

    dmo, d_merged, dgt2 = mix_out_bwd(dh2, mo, wfull['w_out'], mods[5], seq)
    gw['w_out'] = mm_tn("gw_out", merged, dmo)
    (d_opre, d_z, d_ys5, d_gate), _, tail_gw = ew_vjp_call(
        "mix_tail_bwd", fn_mix_tail, tail_in, [], tail_w, [d_merged], [(0, F32), (1, F32), (2, F32), (3, BF16)], tm_ew, seq)
    dg_onorm, gw['w_glu'], dg_bglu, gw['w_proj_a'], gw['w_proj_b'] = tail_gw
    d_qkv3, d_psmall3, d_alp, d_dtp, rs_ffn2 = deltanet_bwd(
        qkv3, p_small3, alp, dtp, sprev, tinv, d_opre.reshape(bl, seq, DNW), nb_dn, exchange=pack_grads(G_FFN2))
    d_pdn3, d_conv8 = dn_prep_bwd(p_dn3, conv8, d_qkv3, d_z.reshape(bl, seq, DNW))
    d_pdn, d_psmall = d_pdn3.reshape(t_rows, 4 * DNW), d_psmall3.reshape(t_rows, LANES)

    s5_out = s5_bwd(p_s53, tables, d_skip, xs, d_ys5.reshape(bl, seq, S5W))
    d_ps5, d_tables, dg_dskip = s5_out[0].reshape(t_rows, S5W), s5_out[1:11], s5_out[11]
    d_s5p = s5_tables_bwd(s5_params, d_tables)

    gw['w_in'] = jnp.concatenate([mm_tn("gw_dn", d_pdn, u2), mm_tn("gw_small", d_psmall, u2)[:2 * NH],
                                  mm_tn("gw_s5", d_ps5, u2), mm_tn("gw_gate", d_gate, u2)], axis=0)
    dh1, dsh2, dsc2, dg_mix = mix_in_bwd([d_pdn, d_psmall, d_ps5, d_gate], w_pieces, h1, mods[3], mods[4], g_mix, dh2, seq)

    dh0, a1, d1_1, d3_1, df1, dmod_a, dg_ffn1, rs_mix = ffn_bwd(
        "ffn1_bwd", dh1, h0, f1, pa1, pb1, mod[:, 0:3, :], g_ffn1, wfull['w1_ffn1'], wfull['w3_ffn1'], wfull['w2_ffn1'], seq,
        exchange=pack_grads(G_MIX))
    dmod_mine = jnp.concatenate([dmod_a, dsh2, dsc2, dgt2, dmod_c], axis=1).reshape(bl, 9 * D)
    small_grads = {
        'g_ffn1': dg_ffn1, 'g_mix': dg_mix, 'a_log': d_alp[:, NH:2 * NH], 'dt_bias': d_dtp[:, NH:2 * NH],
        'g_onorm': dg_onorm, 'lam_re': d_s5p[0].reshape(1, S5G, S5P), 'lam_im': d_s5p[1].reshape(1, S5G, S5P),
        'log_step': d_s5p[2],
        'b_re': d_s5p[3].reshape(S5C, S5G, S5P).transpose(1, 2, 0)[None],
        'b_im': d_s5p[4].reshape(S5C, S5G, S5P).transpose(1, 2, 0)[None],
        'c_re': d_s5p[5].reshape(S5C, S5G, S5P).transpose(1, 0, 2)[None],
        'c_im': d_s5p[6].reshape(S5C, S5G, S5P).transpose(1, 0, 2)[None],
        'd_skip': dg_dskip, 'b_glu': dg_bglu, 'g_ffn2': dg_ffn2, 'g_final': dg_final.reshape(D)}
    small_shapes = [a[n].shape for n in SMALL]
    small_pack = _pack([small_grads[n] for n in SMALL] + [loss_part], F32)
    n_small = small_pack.shape[0]
    small_buf = jnp.concatenate([small_pack, _pack([dmod_mine, d_conv8[:CONVW]], F32)], axis=0)

    gw['w1_ffn1'] = mm_tn("gw1_ffn1", d1_1, u1)
    gw['w3_ffn1'], rs_w1 = mm_tn("gw3_ffn1", d3_1, u1, exchange=pack_grads(['w1_ffn1']))
    gw['w2_ffn1'], rs_w3, sg = mm_tn("gw2_ffn1", a1, df1, exchange=pack_grads(['w3_ffn1']), gather=small_buf)
    rs_w2 = all_to_all("scatter_w2_ffn1", pack_grads(['w2_ffn1']))

    update(rs_ffn2, G_FFN2)
    update(rs_mix, G_MIX)
    update(rs_w1, ['w1_ffn1'])
    update(rs_w3, ['w3_ffn1'])
    update(rs_w2, ['w2_ffn1'])
    pieces = _unpack8(sg[:, n_small:, :], [dmod_mine.shape, (CONVW, 3 * DNW)])
    dmod_all = pieces[0].reshape(NDEV * bl, 9 * D)
    g_wada, g_bada = ada_bwd(c_all, lax.dynamic_slice(dmod_all, (0, me * n_ada), (NDEV * bl, n_ada)), dmod_all)

    n_conv = conv_qkv.shape[2]
    conv_parts = lax.dynamic_slice(pieces[1], (0, 0, me * n_conv), (NDEV, CONVW, n_conv))
    conv_parts = jnp.pad(conv_parts.reshape(NDEV, 1, -1), ((0, 0), (0, 7), (0, 0)))
    pad8 = lambda t: jnp.pad(t.reshape(1, -1), ((0, 7), (0, 0)))
    conv_res = adamw("adamw_conv", conv_parts, pad8(conv_qkv), pad8(m_conv_qkv), pad8(v_conv_qkv))
    for kind, buf in zip(("grad", "delta", "new_m", "new_v"), conv_res):
        res[kind + "_conv_qkv"] = buf[0].reshape(conv_qkv.shape)

    no_param = jnp.zeros_like(loss_part)
    small_res = adamw("adamw_small", sg[:, :n_small, :],
                      *[_pack([a[p + n] for n in SMALL] + [no_param], F32) for p in ("", "m_", "v_")])
    for kind, buf in zip(("grad", "delta", "new_m", "new_v"), small_res):
        for n, t in zip(SMALL, _unpack(buf, small_shapes)):
            res[kind + "_" + n] = t
    loss = _unpack(small_res[0], small_shapes + [loss_part.shape])[-1][0, 0]

    for n, g in (("w_ada", g_wada), ("b_ada", g_bada)):
        shp = a[n].shape
        r2 = lambda t: t.reshape(-1, shp[-1]) if n == "w_ada" else pad8(t)
        out = adamw("adamw_" + n, r2(g)[None], r2(a[n]), r2(a["m_" + n]), r2(a["v_" + n]))
        for kind, buf in zip(("grad", "delta", "new_m", "new_v"), out):
            res[kind + "_" + n] = (buf if n == "w_ada" else buf[0:1]).reshape(shp)

    outs = [loss, dh0.reshape(x.shape)]
    for kind in ("grad", "delta", "new_m", "new_v"):
        outs += [res[kind + "_" + n] for n in WEIGHTS]
    return tuple(outs)
```

```python
import math

import jax
import jax.numpy as jnp
from jax import lax
from jax.experimental import pallas as pl
from jax.experimental.pallas import tpu as pltpu

F32 = jnp.float32
BF16 = jnp.bfloat16
HI = lax.Precision.HIGHEST
H3 = lax.Precision.HIGH
SDS = jax.ShapeDtypeStruct

D = 1024
FF = 2816
FFN_TF = FF
FFN_FWD_TM = 256
FFN_BWD_TM = 256
NH = 8
DH = 64
DNW = NH * DH
CONVW = 4
CH = 64
S5_CH = 128
ACC_LIMIT = 6 * 1024 * 1024
BF16_TILE_ROWS = 16
DN_ROWS = 4
S5W = 512
S5G = 32
S5P = 64
S5C = 16
S5N = S5G * S5P
GB = 4
NDEV = 8
EPS = 1e-6
LANES = 128
ROW = 1024
VMEM_LIMIT = 56 * 1024 * 1024

ADAM_LR, ADAM_B1, ADAM_B2, ADAM_EPS, ADAM_WD, ADAM_STEP = 0.001, 0.9, 0.999, 1e-08, 0.01, 10

WEIGHTS = ['w_ada', 'b_ada', 'g_ffn1', 'w1_ffn1', 'w3_ffn1', 'w2_ffn1', 'g_mix', 'w_in', 'conv_qkv', 'a_log',
           'dt_bias', 'g_onorm', 'lam_re', 'lam_im', 'log_step', 'b_re', 'b_im', 'c_re', 'c_im', 'd_skip', 'w_glu',
           'b_glu', 'w_proj_a', 'w_proj_b', 'w_out', 'g_ffn2', 'w1_ffn2', 'w3_ffn2', 'w2_ffn2', 'g_final']
RS_WEIGHTS = ['w1_ffn1', 'w3_ffn1', 'w2_ffn1', 'w_in', 'w_glu', 'w_proj_a', 'w_proj_b', 'w_out', 'w1_ffn2', 'w3_ffn2',
              'w2_ffn2']
COL_SHARDED = {'w1_ffn1', 'w3_ffn1', 'w_in', 'w_proj_a', 'w_proj_b', 'w1_ffn2', 'w3_ffn2'}
G_FFN1 = ['w1_ffn1', 'w3_ffn1', 'w2_ffn1']
G_MIX = ['w_in', 'w_glu', 'w_proj_a', 'w_proj_b', 'w_out']
G_FFN2 = ['w1_ffn2', 'w3_ffn2', 'w2_ffn2']
SMALL = ['g_ffn1', 'g_mix', 'a_log', 'dt_bias', 'g_onorm', 'lam_re', 'lam_im', 'log_step', 'b_re', 'b_im', 'c_re',
         'c_im', 'd_skip', 'b_glu', 'g_ffn2', 'g_final']


def _cp(n_grid=0):
    if n_grid:
        return pltpu.CompilerParams(vmem_limit_bytes=VMEM_LIMIT, dimension_semantics=("arbitrary",) * n_grid)
    return pltpu.CompilerParams(vmem_limit_bytes=VMEM_LIMIT)


def _dot(a, b):
    return jnp.dot(a.astype(BF16), b.astype(BF16), preferred_element_type=F32)


def _dot_nt(a, b):
    return lax.dot_general(a.astype(BF16), b.astype(BF16), (((1,), (1,)), ((), ())), preferred_element_type=F32)


def _dot_tn(a, b):
    return lax.dot_general(a.astype(BF16), b.astype(BF16), (((0,), (0,)), ((), ())), preferred_element_type=F32)


def _dot_hi(a, b):
    return jnp.dot(a, b, precision=HI, preferred_element_type=F32)


def _dot_h3(a, b):
    return jnp.dot(a, b, precision=H3, preferred_element_type=F32)


@jax.custom_vjp
def bdot(a, b):
    return _dot(a, b)


bdot.defvjp(lambda a, b: (_dot(a, b), (a, b)),
            lambda r, g: (_dot_nt(g, r[1]).astype(r[0].dtype), _dot_tn(r[0], g).astype(r[1].dtype)))


@jax.custom_vjp
def bdot_nt(a, b):
    return _dot_nt(a, b)


bdot_nt.defvjp(lambda a, b: (_dot_nt(a, b), (a, b)),
               lambda r, g: (_dot(g, r[1]).astype(r[0].dtype), _dot_tn(g, r[0]).astype(r[1].dtype)))


def _silu(x):
    return x * jax.nn.sigmoid(x)


def _iota2(shape, axis):
    return lax.broadcasted_iota(jnp.int32, shape, axis)


def normmod(h, g, sc, sh):
    y = h * lax.rsqrt(jnp.mean(h * h, axis=-1, keepdims=True) + EPS) * g
    return y * (1.0 + sc) + sh


def fn_merge(gate, ya, yb):
    return (jax.nn.sigmoid(gate[:, :D]) * ya + jax.nn.sigmoid(gate[:, D:]) * yb,)


def fn_glu(y, w, b):
    ge = jax.nn.gelu(y)
    return (ge * jax.nn.sigmoid(bdot(ge, w) + b),)


def fn_onorm(o, z, g_on):
    r = _iota2((DH, DNW), 0)
    c = _iota2((DH, DNW), 1)
    expand = (c % DH == r).astype(F32)
    r2 = _iota2((DNW, DNW), 0)
    c2 = _iota2((DNW, DNW), 1)
    avg = (r2 // DH == c2 // DH).astype(F32) * (1.0 / DH)
    ms = _dot_h3(o * o, avg)
    return (o * lax.rsqrt(ms + EPS) * _dot_hi(g_on, expand) * _silu(z),)


def fn_mix_tail(o_pre, z, y_s5, gate, g_on, w_glu, b_glu, wa_t, wb_t):
    (oa,) = fn_onorm(o_pre, z, g_on)
    (ob,) = fn_glu(y_s5, w_glu, b_glu)
    return fn_merge(gate, bdot_nt(oa, wa_t), bdot_nt(ob, wb_t))


def gate_fn(small, alp, dtp):
    beta = jax.nn.sigmoid(small)
    la = -jnp.exp(alp) * jax.nn.softplus(small + dtp)
    tri = (_iota2((CH, CH), 0) >= _iota2((CH, CH), 1)).astype(F32)
    gc = _dot_hi(tri, la)
    gct = lax.dot_general(la, tri, (((0,), (1,)), ((), ())), precision=HI, preferred_element_type=F32)
    return beta, gc, gct


def _bdg(a, b, ca, cb, hi):
    if not hi:
        a, b = a.astype(BF16), b.astype(BF16)
    return lax.dot_general(a, b, (((ca,), (cb,)), ((0,), (0,))), precision=H3 if hi else None,
                           preferred_element_type=F32)


def _batched_matmuls(hi):
    nn_ = lambda a, b: _bdg(a, b, 2, 1, hi)
    nt_ = lambda a, b: _bdg(a, b, 2, 2, hi)
    tn_ = lambda a, b: _bdg(a, b, 1, 1, hi)
    nn = jax.custom_vjp(nn_)
    nn.defvjp(lambda a, b: (nn_(a, b), (a, b)), lambda r, g: (nt_(g, r[1]), tn_(r[0], g)))
    nt = jax.custom_vjp(nt_)
    nt.defvjp(lambda a, b: (nt_(a, b), (a, b)), lambda r, g: (nn_(g, r[1]), tn_(g, r[0])))
    tn = jax.custom_vjp(tn_)
    tn.defvjp(lambda a, b: (tn_(a, b), (a, b)), lambda r, g: (nt_(r[1], g), nn_(r[0], g)))
    return nn, nt, tn


bnn, bnt, btn = _batched_matmuls(False)
hnn, hnt, htn = _batched_matmuls(True)


def _unit_lower_inverse(a):
    r = _iota2((1, CH, CH), 1)
    c = _iota2((1, CH, CH), 2)
    eye = (r == c).astype(F32)
    d = jnp.where(r // 8 == c // 8, a, 0.0)
    inv = eye - d
    p = d
    for _ in range(2):
        p = hnn(p, p)
        inv = inv + hnn(inv, p)
    for blk in (16, 32, 64):
        off = jnp.where((r // blk == c // blk) & (r // (blk // 2) != c // (blk // 2)), a, 0.0)
        mm = hnn if blk == 16 else bnn
        inv = inv - mm(mm(inv, off), inv)
    return inv


@jax.custom_vjp
def _inverse_given(a, t):
    return t


_inverse_given.defvjp(lambda a, t: (t, t), lambda t, g: (-hnt(htn(t, g), t), jnp.zeros_like(t)))


def dn_prep(xc, w):
    t = xc.shape[0] - 8
    c = xc[5:5 + t] * w[0:1] + xc[6:6 + t] * w[1:2] + xc[7:7 + t] * w[2:3] + xc[8:8 + t] * w[3:4]
    act = _silu(c)
    q, k, v = act[:, :DNW], act[:, DNW:2 * DNW], act[:, 2 * DNW:]
    ones = (_iota2((DNW, DNW), 0) // DH == _iota2((DNW, DNW), 1) // DH).astype(F32)
    q = q * lax.rsqrt(_dot_h3(q * q, ones) + EPS) * (DH ** -0.5)
    k = k * lax.rsqrt(_dot_h3(k * k, ones) + EPS)
    return jnp.concatenate([q, k, v], axis=1)


def dn_chunk(q, k, v, b, g, gt, s_prev, t_saved=None):
    r = _iota2((1, CH, CH), 1)
    c = _iota2((1, CH, CH), 2)
    causal = r >= c
    dec = jnp.where(causal, jnp.exp(jnp.where(causal, g - gt, 0.0)), 0.0)
    kb = k * b
    qk = bnt(jnp.concatenate([q, kb], axis=1), k)
    attn = qk[:, :CH] * dec
    a = jnp.where(r > c, qk[:, CH:] * dec, 0.0)
    tinv = _unit_lower_inverse(a) if t_saved is None else _inverse_given(a, t_saved)
    eg = jnp.exp(g)
    uw = hnn(tinv, jnp.concatenate([v * b, kb * eg], axis=2))
    g_last = g[:, CH - 1:CH]
    ws = bnn(jnp.concatenate([uw[..., DH:], q * eg], axis=1), s_prev)
    v_new = uw[..., :DH] - ws[:, :CH]
    o = ws[:, CH:] + bnn(attn, v_new)
    s_new = s_prev * jnp.exp(g_last) + btn(k * jnp.exp(g_last - g), v_new)
    return o, s_new, tinv


def s5_chunk(u, xp_re, xp_im, bb_re, bb_im, cc_re, cc_im, p0r, p0i, p1r, p1i, pir, pii, dsk):
    nb, ch, _ = u.shape
    u2 = u.reshape(nb * ch, LANES)
    bu_re = bdot(u2, bb_re).reshape(nb, ch, 512)
    bu_im = bdot(u2, bb_im).reshape(nb, ch, 512)
    xt_re = pir * bu_re - pii * bu_im
    xt_im = pir * bu_im + pii * bu_re
    tri = jnp.broadcast_to((_iota2((1, ch, ch), 1) >= _iota2((1, ch, ch), 2)).astype(F32), (nb, ch, ch))
    cs_re = hnn(tri, xt_re)
    cs_im = hnn(tri, xt_im)
    x_re = p0r * cs_re - p0i * cs_im + p1r * xp_re - p1i * xp_im
    x_im = p0r * cs_im + p0i * cs_re + p1r * xp_im + p1i * xp_re
    y = bdot_nt(x_re.reshape(nb * ch, 512), cc_re) - bdot_nt(x_im.reshape(nb * ch, 512), cc_im) + dsk * u2
    return y.reshape(nb, ch, LANES), x_re[:, ch - 1:ch], x_im[:, ch - 1:ch]


def s5_tables(lam_re, lam_im, log_step, bre, bim, cre, cim):
    expand = (_iota2((S5G, S5N), 1) // S5P == _iota2((S5G, S5N), 0)).astype(F32)
    step = _dot_hi(jnp.exp(log_step), expand)
    lre = jnp.minimum(lam_re, -1e-4)
    lr = lre * step
    ang = lam_im * step
    mag = jnp.exp(lr)
    lb_re = mag * jnp.cos(ang)
    lb_im = mag * jnp.sin(ang)
    den = lre * lre + lam_im * lam_im
    coef_re = ((lb_re - 1.0) * lre + lb_im * lam_im) / den
    coef_im = (lb_im * lre - (lb_re - 1.0) * lam_im) / den
    bb_re = coef_re * bre - coef_im * bim
    bb_im = coef_re * bim + coef_im * bre
    j = _iota2((S5_CH, 1), 0).astype(F32)
    jc = j - S5_CH // 2
    e0 = jnp.exp(jc * lr)
    e1 = jnp.exp((j + 1.0) * lr)
    ei = jnp.exp(-jc * lr)
    mask = (_iota2((LANES, 512), 0) // S5C == _iota2((LANES, 512), 1) // S5P).astype(F32)

    def blocks(t):
        return jnp.concatenate([(jnp.tile(t[:, gb * 512:(gb + 1) * 512], (LANES // S5C, 1)) * mask)[None]
                                for gb in range(GB)], axis=0)

    return (blocks(bb_re), blocks(bb_im), blocks(cre), blocks(cim),
            e0 * jnp.cos(jc * ang), e0 * jnp.sin(jc * ang),
            e1 * jnp.cos((j + 1.0) * ang), e1 * jnp.sin((j + 1.0) * ang),
            ei * jnp.cos(jc * ang), -ei * jnp.sin(jc * ang))


def _row_specs(tiled, batch, bcast, tm, tpb):
    specs = [pl.BlockSpec((tm, a.shape[1]), lambda i: (i, 0)) for a in tiled]
    specs += [pl.BlockSpec((None,) + a.shape[1:], lambda i: (i // tpb, 0, 0)) for a in batch]
    specs += [pl.BlockSpec(a.shape, lambda i, nd=a.ndim: (0,) * nd) for a in bcast]
    return specs


def ew_call(name, fn, tiled, batch, bcast, outs, tm, seq):
    t_rows = tiled[0].shape[0]
    n_in = len(tiled) + len(batch) + len(bcast)

    def body(*refs):
        vals = [r[...].astype(F32) for r in refs[:n_in]]
        for r, o in zip(refs[n_in:], fn(*vals)):
            r[...] = o.astype(r.dtype)

    return pl.pallas_call(
        body, grid=(t_rows // tm,), in_specs=_row_specs(tiled, batch, bcast, tm, seq // tm),
        out_specs=[pl.BlockSpec((tm, w), lambda i: (i, 0)) for w, _ in outs],
        out_shape=[SDS((t_rows, w), dt) for w, dt in outs], name=name, compiler_params=_cp(1))(*tiled, *batch, *bcast)


def ew_vjp_call(name, fn, tiled, batch, bcast, cts, want, tm, seq, addend=None):
    t_rows = tiled[0].shape[0]
    tpb = seq // tm
    n_t, n_b, n_c = len(tiled), len(batch), len(bcast)
    n_in = n_t + n_b + n_c
    extra = [] if addend is None else [addend]

    def body(*refs):
        i = pl.program_id(0)
        vals = [r[...].astype(F32) for r in refs[:n_in]]
        ctv = tuple(r[...].astype(F32) for r in refs[n_in:n_in + len(cts)])
        outs = refs[n_in + len(cts) + len(extra):]
        _, vjp = jax.vjp(fn, *vals)
        grads = vjp(ctv)
        for k, (r, (idx, _)) in enumerate(zip(outs[:len(want)], want)):
            g = grads[idx]
            if k == 0 and extra:
                g = g + refs[n_in + len(cts)][...]
            r[...] = g.astype(r.dtype)
        for k in range(n_b):
            r, g = outs[len(want) + k], grads[n_t + k]

            @pl.when(i % tpb == 0)
            def _(r=r, g=g):
                r[...] = g

            @pl.when(i % tpb != 0)
            def _(r=r, g=g):
                r[...] += g
        for k in range(n_c):
            r, g = outs[len(want) + n_b + k], grads[n_t + n_b + k]

            @pl.when(i == 0)
            def _(r=r, g=g):
                r[...] = g

            @pl.when(i != 0)
            def _(r=r, g=g):
                r[...] += g

    out_specs = [pl.BlockSpec((tm, tiled[idx].shape[1]), lambda i: (i, 0)) for idx, _ in want]
    out_specs += [pl.BlockSpec((None,) + a.shape[1:], lambda i: (i // tpb, 0, 0)) for a in batch]
    out_specs += [pl.BlockSpec(a.shape, lambda i, nd=a.ndim: (0,) * nd) for a in bcast]
    out_shape = [SDS(tiled[idx].shape, dt) for idx, dt in want]
    out_shape += [SDS(a.shape, F32) for a in batch] + [SDS(a.shape, F32) for a in bcast]
    res = pl.pallas_call(
        body, grid=(t_rows // tm,),
        in_specs=_row_specs(tiled, batch, bcast, tm, tpb)
        + [pl.BlockSpec((tm, a.shape[1]), lambda i: (i, 0)) for a in list(cts) + extra],
        out_specs=out_specs, out_shape=out_shape, name=name, compiler_params=_cp(1))(*tiled, *batch, *bcast, *cts, *extra)
    return res[:len(want)], res[len(want):len(want) + n_b], res[len(want) + n_b:]


def _pick(n, cands):
    for c in cands:
        if n % c == 0:
            return c
    return n


def mm_tn(name, a, b, exchange=None, gather=None):
    t_rows, m = a.shape
    n = b.shape[1]
    tn = n if n <= 1024 else _pick(n, (1024, 512, 256, 128))
    tm = max([t for t in range(LANES, m + 1, LANES) if m % t == 0 and t * tn * 4 <= ACC_LIMIT] or [m])
    tk = _pick(t_rows, (512, 256, 128, 64))
    grid = (m // tm, n // tn, t_rows // tk)
    extra = [x for x in (exchange, gather) if x is not None]
    ne = len(extra)

    def body(*refs):
        a_ref, b_ref = refs[:2]
        o_ref, acc = refs[2 + ne], refs[3 + 2 * ne]
        i, j, k = pl.program_id(0), pl.program_id(1), pl.program_id(2)
        first = (i == 0) & (j == 0) & (k == 0)
        last = (i == grid[0] - 1) & (j == grid[1] - 1) & (k == grid[2] - 1)
        at_end = []
        for e, x in enumerate(extra):
            comm_refs = (refs[2 + e], refs[3 + ne + e]) + tuple(refs[4 + 2 * ne + 3 * e:7 + 2 * ne + 3 * e])
            if x is exchange:
                start, finish = _exchange_phases(*comm_refs)
                at_end.append(finish)
            else:
                start, forward, finish = _gather_phases(*comm_refs)
                at_end += [forward, finish]
            pl.when(first)(start)

        @pl.when(k == 0)
        def _():
            acc[...] = jnp.zeros_like(acc)

        acc[...] += _dot_tn(a_ref[...], b_ref[...])

        @pl.when(k == grid[2] - 1)
        def _():
            o_ref[...] = acc[...].astype(BF16)

        for phase in at_end:
            pl.when(last)(phase)

    res = pl.pallas_call(
        body, grid=grid,
        in_specs=[pl.BlockSpec((tk, tm), lambda i, j, k: (k, i)), pl.BlockSpec((tk, tn), lambda i, j, k: (k, j))]
        + [HBM_SPEC] * ne,
        out_specs=[pl.BlockSpec((tm, tn), lambda i, j, k: (i, j))] + [HBM_SPEC] * ne,
        out_shape=[SDS((m, n), BF16)] + [SDS(x.shape if x is exchange else (NDEV,) + x.shape, x.dtype) for x in extra],
        scratch_shapes=[pltpu.VMEM((tm, tn), F32)] + _comm_scratch() * ne, name=name,
        compiler_params=_cp(3))(a, b, *extra)
    return res if extra else res[0]


def _ffn_weight_spec():
    if FFN_TF == FF:
        return pl.BlockSpec((FF, D), lambda i, j: (0, 0), pipeline_mode=pl.Buffered(1))
    return pl.BlockSpec((FFN_TF, D), lambda i, j: (j, 0))


def ffn_fwd(name, h, mod3, g, w1, w3, w2, seq, gather=None):
    t_rows = h.shape[0]
    tm = _pick(seq, (FFN_FWD_TM, 128, 64))
    tf = FFN_TF
    tpb = seq // tm
    nf = FF // tf
    nt = t_rows // tm
    extra = [] if gather is None else [gather]

    def body(*refs):
        h_ref, mod_ref, g_ref, w1_ref, w3_ref, w2_ref = refs[:6]
        ho_ref, f_ref, u_ref, h1_ref, h3_ref = refs[6 + len(extra):11 + len(extra)]
        acc = refs[11 + 2 * len(extra)]
        i, j = pl.program_id(0), pl.program_id(1)
        if extra:
            start, forward, finish = _gather_phases(refs[6], refs[12], *refs[14:17])
            pl.when((i == 0) & (j == 0))(start)
            pl.when((i == nt - 1) & (j == 0))(forward)

        @pl.when(j == 0)
        def _():
            u_ref[...] = normmod(h_ref[...], g_ref[...], mod_ref[1:2, :], mod_ref[0:1, :]).astype(BF16)
            acc[...] = jnp.zeros_like(acc)

        u = u_ref[...]
        h1 = _dot_nt(u, w1_ref[...])
        h3 = _dot_nt(u, w3_ref[...])
        h1_ref[...] = h1.astype(BF16)
        h3_ref[...] = h3.astype(BF16)
        acc[...] += _dot(_silu(h1) * h3, w2_ref[...])

        @pl.when(j == nf - 1)
        def _():
            f_ref[...] = acc[...]
            ho_ref[...] = h_ref[...] + 0.5 * mod_ref[2:3, :] * acc[...]

        if extra:
            pl.when((i == nt - 1) & (j == nf - 1))(finish)

    row = lambda i, j: (i, 0)
    return pl.pallas_call(
        body, grid=(nt, nf),
        in_specs=[pl.BlockSpec((tm, D), row), pl.BlockSpec((None, 3, D), lambda i, j: (i // tpb, 0, 0)),
                  pl.BlockSpec((1, D), lambda i, j: (0, 0)), _ffn_weight_spec(), _ffn_weight_spec(), _ffn_weight_spec()]
        + [HBM_SPEC] * len(extra),
        out_specs=[pl.BlockSpec((tm, D), row), pl.BlockSpec((tm, D), row), pl.BlockSpec((tm, D), row),
                   pl.BlockSpec((tm, tf), lambda i, j: (i, j)), pl.BlockSpec((tm, tf), lambda i, j: (i, j))]
        + [HBM_SPEC] * len(extra),
        out_shape=[SDS((t_rows, D), F32), SDS((t_rows, D), F32), SDS((t_rows, D), BF16), SDS((t_rows, FF), BF16),
                   SDS((t_rows, FF), BF16)] + [SDS((NDEV,) + x.shape, x.dtype) for x in extra],
        scratch_shapes=[pltpu.VMEM((tm, D), F32)] + (_comm_scratch() if extra else []), name=name,
        compiler_params=_cp(2))(h, mod3, g, w1, w3, w2, *extra)


def ffn_bwd(name, dho, h, f_out, h1_in, h3_in, mod3, g, w1, w3, w2, seq, exchange=None):
    t_rows = h.shape[0]
    tm = _pick(seq, (FFN_BWD_TM, 128, 64))
    tf = FFN_TF
    tpb = seq // tm
    nf = FF // tf
    nt = t_rows // tm
    extra = [] if exchange is None else [exchange]

    def body(*refs):
        dho_ref, h_ref, f_ref, h1_ref, h3_ref, mod_ref, g_ref, w1_ref, w3_ref, w2_ref = refs[:10]
        dh_ref, a_ref, dh1_ref, dh3_ref, df_scr, dmod_ref, dg_ref = refs[10 + len(extra):17 + len(extra)]
        du_acc = refs[17 + 2 * len(extra)]
        i, j = pl.program_id(0), pl.program_id(1)
        if extra:
            start, finish = _exchange_phases(refs[10], refs[18], *refs[20:23])
            pl.when((i == 0) & (j == 0))(start)

        @pl.when(j == 0)
        def _():
            df_scr[...] = (0.5 * mod_ref[2:3, :] * dho_ref[...]).astype(BF16)
            du_acc[...] = jnp.zeros_like(du_acc)

        h1 = h1_ref[...].astype(F32)
        h3 = h3_ref[...].astype(F32)
        sg = jax.nn.sigmoid(h1)
        s = h1 * sg
        da = _dot_nt(df_scr[...], w2_ref[...])
        dh3 = (da * s).astype(BF16)
        dh1 = (da * h3 * (sg * (1.0 + h1 * (1.0 - sg)))).astype(BF16)
        a_ref[...] = (s * h3).astype(BF16)
        dh1_ref[...] = dh1
        dh3_ref[...] = dh3
        du_acc[...] += _dot(dh1, w1_ref[...]) + _dot(dh3, w3_ref[...])

        @pl.when(j == nf - 1)
        def _():
            _, vjp = jax.vjp(normmod, h_ref[...], g_ref[...], mod_ref[1:2, :], mod_ref[0:1, :])
            dh_n, dg, dsc, dsh = vjp(du_acc[...])
            dh_ref[...] = dho_ref[...] + dh_n
            dgt = jnp.sum(0.5 * dho_ref[...] * f_ref[...], axis=0, keepdims=True)
            dmod = jnp.concatenate([dsh, dsc, dgt], axis=0)

            @pl.when(i % tpb == 0)
            def _():
                dmod_ref[...] = dmod

            @pl.when(i % tpb != 0)
            def _():
                dmod_ref[...] += dmod

            @pl.when(i == 0)
            def _():
                dg_ref[...] = dg

            @pl.when(i != 0)
            def _():
                dg_ref[...] += dg

        if extra:
            pl.when((i == nt - 1) & (j == nf - 1))(finish)

    row = lambda i, j: (i, 0)
    col = lambda i, j: (i, j)
    return pl.pallas_call(
        body, grid=(nt, nf),
        in_specs=[pl.BlockSpec((tm, D), row), pl.BlockSpec((tm, D), row), pl.BlockSpec((tm, D), row),
                  pl.BlockSpec((tm, tf), col), pl.BlockSpec((tm, tf), col),
                  pl.BlockSpec((None, 3, D), lambda i, j: (i // tpb, 0, 0)),
                  pl.BlockSpec((1, D), lambda i, j: (0, 0)), _ffn_weight_spec(), _ffn_weight_spec(), _ffn_weight_spec()]
        + [HBM_SPEC] * len(extra),
        out_specs=[pl.BlockSpec((tm, D), row), pl.BlockSpec((tm, tf), col), pl.BlockSpec((tm, tf), col),
                   pl.BlockSpec((tm, tf), col), pl.BlockSpec((tm, D), row),
                   pl.BlockSpec((None, 3, D), lambda i, j: (i // tpb, 0, 0)), pl.BlockSpec((1, D), lambda i, j: (0, 0))]
        + [HBM_SPEC] * len(extra),
        out_shape=[SDS((t_rows, D), F32), SDS((t_rows, FF), BF16), SDS((t_rows, FF), BF16), SDS((t_rows, FF), BF16),
                   SDS((t_rows, D), BF16), SDS(mod3.shape, F32), SDS((1, D), F32)] + [SDS(x.shape, x.dtype) for x in extra],
        scratch_shapes=[pltpu.VMEM((tm, D), F32)] + (_comm_scratch() if extra else []), name=name,
        compiler_params=_cp(2))(dho, h, f_out, h1_in, h3_in, mod3, g, w1, w3, w2, *extra)


def _resident(shape):
    return pl.BlockSpec(shape, lambda i: (0,) * len(shape), pipeline_mode=pl.Buffered(1))


def mix_in_fwd(h, sh, sc, g, ws, seq):
    t_rows = h.shape[0]
    tm = _pick(seq, (256, 128, 64))
    tpb = seq // tm
    nw = len(ws)

    def body(h_ref, sh_ref, sc_ref, g_ref, *rest):
        u = normmod(h_ref[...], g_ref[...], sc_ref[...], sh_ref[...]).astype(BF16)
        rest[nw][...] = u
        for w_ref, p_ref in zip(rest[:nw], rest[nw + 1:]):
            p_ref[...] = _dot_nt(u, w_ref[...])

    row = lambda i: (i, 0)
    batch = pl.BlockSpec((None, 1, D), lambda i: (i // tpb, 0, 0))
    return pl.pallas_call(
        body, grid=(t_rows // tm,),
        in_specs=[pl.BlockSpec((tm, D), row), batch, batch, pl.BlockSpec((1, D), lambda i: (0, 0))]
        + [_resident(w.shape) for w in ws],
        out_specs=[pl.BlockSpec((tm, D), row)] + [pl.BlockSpec((tm, w.shape[0]), row) for w in ws],
        out_shape=[SDS((t_rows, D), BF16)] + [SDS((t_rows, w.shape[0]), F32) for w in ws], name="mix_in_fwd",
        compiler_params=_cp(1))(h, sh, sc, g, *ws)


def mix_in_bwd(dps, ws, h, sh, sc, g, dh_add, seq):
    t_rows = h.shape[0]
    tm = _pick(seq, (256, 128, 64))
    tpb = seq // tm
    nw = len(ws)

    def body(*refs):
        h_ref, sh_ref, sc_ref, g_ref, add_ref, dh_ref, dsh_ref, dsc_ref, dg_ref = refs[2 * nw:]
        i = pl.program_id(0)
        du = _dot(refs[0][...], refs[nw][...])
        for k in range(1, nw):
            du = du + _dot(refs[k][...], refs[nw + k][...])
        _, vjp = jax.vjp(normmod, h_ref[...], g_ref[...], sc_ref[...], sh_ref[...])
        dh_n, dg, dsc, dsh = vjp(du)
        dh_ref[...] = add_ref[...] + dh_n

        @pl.when(i % tpb == 0)
        def _():
            dsh_ref[...] = dsh
            dsc_ref[...] = dsc

        @pl.when(i % tpb != 0)
        def _():
            dsh_ref[...] += dsh
            dsc_ref[...] += dsc

        @pl.when(i == 0)
        def _():
            dg_ref[...] = dg

        @pl.when(i != 0)
        def _():
            dg_ref[...] += dg

    row = lambda i: (i, 0)
    batch = pl.BlockSpec((None, 1, D), lambda i: (i // tpb, 0, 0))
    gain = pl.BlockSpec((1, D), lambda i: (0, 0))
    return pl.pallas_call(
        body, grid=(t_rows // tm,),
        in_specs=[pl.BlockSpec((tm, dp.shape[1]), row) for dp in dps] + [_resident(w.shape) for w in ws]
        + [pl.BlockSpec((tm, D), row), batch, batch, gain, pl.BlockSpec((tm, D), row)],
        out_specs=[pl.BlockSpec((tm, D), row), batch, batch, gain],
        out_shape=[SDS((t_rows, D), F32), SDS(sh.shape, F32), SDS(sc.shape, F32), SDS((1, D), F32)], name="mix_in_bwd",
        compiler_params=_cp(1))(*dps, *ws, h, sh, sc, g, dh_add)


def mix_out_fwd(merged, w_out, h_prev, gt, seq):
    t_rows = merged.shape[0]
    tm = _pick(seq, (256, 128, 64))
    tpb = seq // tm

    def body(m_ref, w_ref, h_ref, gt_ref, mo_ref, ho_ref):
        mo = _dot(m_ref[...], w_ref[...])
        mo_ref[...] = mo
        ho_ref[...] = h_ref[...] + gt_ref[...] * mo

    row = lambda i: (i, 0)
    return pl.pallas_call(
        body, grid=(t_rows // tm,),
        in_specs=[pl.BlockSpec((tm, D), row), _resident(w_out.shape), pl.BlockSpec((tm, D), row),
                  pl.BlockSpec((None, 1, D), lambda i: (i // tpb, 0, 0))],
        out_specs=[pl.BlockSpec((tm, D), row), pl.BlockSpec((tm, D), row)],
        out_shape=[SDS((t_rows, D), F32), SDS((t_rows, D), F32)], name="mix_out_fwd",
        compiler_params=_cp(1))(merged, w_out, h_prev, gt)


def mix_out_bwd(dh, mo, w_out, gt, seq):
    t_rows = dh.shape[0]
    tm = _pick(seq, (256, 128, 64))
    tpb = seq // tm

    def body(dh_ref, mo_ref, w_ref, gt_ref, dmo_ref, dm_ref, dgt_ref):
        i = pl.program_id(0)
        dmo = (gt_ref[...] * dh_ref[...]).astype(BF16)
        dmo_ref[...] = dmo
        dm_ref[...] = _dot_nt(dmo, w_ref[...])
        dgt = jnp.sum(dh_ref[...] * mo_ref[...], axis=0, keepdims=True)

        @pl.when(i % tpb == 0)
        def _():
            dgt_ref[...] = dgt

        @pl.when(i % tpb != 0)
        def _():
            dgt_ref[...] += dgt

    row = lambda i: (i, 0)
    batch = pl.BlockSpec((None, 1, D), lambda i: (i // tpb, 0, 0))
    return pl.pallas_call(
        body, grid=(t_rows // tm,),
        in_specs=[pl.BlockSpec((tm, D), row), pl.BlockSpec((tm, D), row), _resident(w_out.shape), batch],
        out_specs=[pl.BlockSpec((tm, D), row), pl.BlockSpec((tm, D), row), batch],
        out_shape=[SDS((t_rows, D), BF16), SDS((t_rows, D), F32), SDS(gt.shape, F32)], name="mix_out_bwd",
        compiler_params=_cp(1))(dh, mo, w_out, gt)


def _dn_cols(part, hd):
    return slice(part * DNW + hd * DH, part * DNW + (hd + 1) * DH)


def _qkv_stacks(qkv_ref, nb):
    pairs = [(b, hd) for b in range(nb) for hd in range(NH)]
    return [jnp.stack([qkv_ref[b, :, _dn_cols(part, hd)] for b, hd in pairs]) for part in range(3)]


def dn_prep_fwd(p_dn, conv8):
    bl, seq, _ = p_dn.shape
    tp = _pick(seq, (256, 128, 64))

    def body(raw_ref, halo_ref, conv_ref, o_ref):
        hm = (pl.program_id(1) > 0).astype(F32)
        o_ref[...] = dn_prep(jnp.concatenate([halo_ref[...] * hm, raw_ref[...]], axis=0), conv_ref[...])

    return pl.pallas_call(
        body, grid=(bl, seq // tp),
        in_specs=[pl.BlockSpec((None, tp, 3 * DNW), lambda b, i: (b, i, 0)),
                  pl.BlockSpec((None, 8, 3 * DNW), lambda b, i: (b, jnp.maximum(i * (tp // 8) - 1, 0), 0)),
                  pl.BlockSpec((8, 3 * DNW), lambda b, i: (0, 0))],
        out_specs=pl.BlockSpec((None, tp, 3 * DNW), lambda b, i: (b, i, 0)),
        out_shape=SDS((bl, seq, 3 * DNW), F32), name="dn_prep_fwd", compiler_params=_cp(2))(p_dn, p_dn, conv8)


def dn_prep_bwd(p_dn, conv8, d_qkv, d_z):
    bl, seq, _ = p_dn.shape
    tp = _pick(seq, (256, 128, 64))
    nt = seq // tp

    def body(raw_ref, halo_ref, conv_ref, dq_ref, dz_ref, draw_ref, dconv_ref, carry):
        b, r = pl.program_id(0), pl.program_id(1)

        @pl.when((b == 0) & (r == 0))
        def _():
            dconv_ref[...] = jnp.zeros_like(dconv_ref)

        @pl.when(r == 0)
        def _():
            carry[...] = jnp.zeros_like(carry)

        hm = (r < nt - 1).astype(F32)
        _, vjp = jax.vjp(dn_prep, jnp.concatenate([halo_ref[...] * hm, raw_ref[...]], axis=0), conv_ref[...])
        dxc, dw = vjp(dq_ref[...])
        tail = dxc[tp:tp + 8] + carry[...]
        draw_ref[:, 0:3 * DNW] = jnp.concatenate([dxc[8:tp], tail], axis=0).astype(BF16)
        draw_ref[:, 3 * DNW:4 * DNW] = dz_ref[...].astype(BF16)
        carry[...] = dxc[0:8] * hm
        dconv_ref[...] += dw

    blk = lambda b, r: (b, nt - 1 - r, 0)
    return pl.pallas_call(
        body, grid=(bl, nt),
        in_specs=[pl.BlockSpec((None, tp, 3 * DNW), blk),
                  pl.BlockSpec((None, 8, 3 * DNW), lambda b, r: (b, jnp.maximum((nt - 1 - r) * (tp // 8) - 1, 0), 0)),
                  pl.BlockSpec((8, 3 * DNW), lambda b, r: (0, 0)), pl.BlockSpec((None, tp, 3 * DNW), blk),
                  pl.BlockSpec((None, tp, DNW), blk)],
        out_specs=[pl.BlockSpec((None, tp, 4 * DNW), blk), pl.BlockSpec((8, 3 * DNW), lambda b, r: (0, 0))],
        out_shape=[SDS((bl, seq, 4 * DNW), BF16), SDS((8, 3 * DNW), F32)],
        scratch_shapes=[pltpu.VMEM((8, 3 * DNW), F32)], name="dn_prep_bwd", compiler_params=_cp(2))(p_dn, p_dn, conv8, d_qkv, d_z)


def _gate_stacks(gates, nb):
    pairs = [(b, hd) for b in range(nb) for hd in range(NH)]
    bs = jnp.stack([gates[b][0][:, hd:hd + 1] for b, hd in pairs])
    gs = jnp.stack([gates[b][1][:, NH + hd:NH + hd + 1] for b, hd in pairs])
    gts = jnp.stack([gates[b][2][NH + hd:NH + hd + 1, :] for b, hd in pairs])
    return bs, gs, gts


def deltanet_fwd(qkv, p_small, alp, dtp, nb):
    bl, seq, _ = qkv.shape
    nc = seq // CH
    ng = nb * NH

    def body(qkv_ref, small_ref, alp_ref, dtp_ref, o_ref, sprev_ref, tinv_ref, s_scr):
        @pl.when(pl.program_id(1) == 0)
        def _():
            s_scr[...] = jnp.zeros_like(s_scr)

        gates = [gate_fn(small_ref[b], alp_ref[...], dtp_ref[...]) for b in range(nb)]
        s_prev = s_scr[...]
        o, s_new, tinv = dn_chunk(*_qkv_stacks(qkv_ref, nb), *_gate_stacks(gates, nb), s_prev)
        sprev_ref[...] = s_prev
        tinv_ref[...] = tinv
        s_scr[...] = s_new
        for b in range(nb):
            for hd in range(NH):
                o_ref[b, :, hd * DH:(hd + 1) * DH] = o[b * NH + hd]

    blk = lambda bb, n: (bb, n, 0)
    const = lambda bb, n: (0, 0)
    saved = pl.BlockSpec((None, ng, DH, DH), lambda bb, n: (bb * nc + n, 0, 0, 0))
    return pl.pallas_call(
        body, grid=(bl // nb, nc),
        in_specs=[pl.BlockSpec((nb, CH, 3 * DNW), blk), pl.BlockSpec((nb, CH, LANES), blk),
                  pl.BlockSpec((1, LANES), const), pl.BlockSpec((1, LANES), const)],
        out_specs=[pl.BlockSpec((nb, CH, DNW), blk), saved, saved],
        out_shape=[SDS((bl, seq, DNW), F32), SDS((bl // nb * nc, ng, DH, DH), F32), SDS((bl // nb * nc, ng, DH, DH), F32)],
        scratch_shapes=[pltpu.VMEM((ng, DH, DH), F32)], name="deltanet_fwd",
        compiler_params=_cp(2))(qkv, p_small, alp, dtp)


def deltanet_bwd(qkv, p_small, alp, dtp, sprev, tinv, d_o, nb, exchange=None):
    bl, seq, _ = qkv.shape
    nc = seq // CH
    ng = nb * NH
    extra = [] if exchange is None else [exchange]

    def body(*refs):
        qkv_ref, small_ref, alp_ref, dtp_ref, sprev_ref, tinv_ref, do_ref = refs[:7]
        dqkv_ref, dsmall_ref, dalp_ref, ddtp_ref = refs[7 + len(extra):11 + len(extra)]
        ds_scr = refs[11 + 2 * len(extra)]
        bb, r = pl.program_id(0), pl.program_id(1)
        if extra:
            start, finish = _exchange_phases(refs[7], refs[12], *refs[14:17])
            pl.when((bb == 0) & (r == 0))(start)

        @pl.when((bb == 0) & (r == 0))
        def _():
            dalp_ref[...] = jnp.zeros_like(dalp_ref)
            ddtp_ref[...] = jnp.zeros_like(ddtp_ref)

        @pl.when(r == 0)
        def _():
            ds_scr[...] = jnp.zeros_like(ds_scr)

        gates, gate_vjps = [], []
        for b in range(nb):
            out, gvjp = jax.vjp(gate_fn, small_ref[b], alp_ref[...], dtp_ref[...])
            gates.append(out)
            gate_vjps.append(gvjp)
        t_saved = tinv_ref[...]
        _, vjp = jax.vjp(lambda *args: dn_chunk(*args, t_saved)[:2], *_qkv_stacks(qkv_ref, nb), *_gate_stacks(gates, nb),
                         sprev_ref[...])
        d_out = jnp.stack([do_ref[b, :, hd * DH:(hd + 1) * DH] for b in range(nb) for hd in range(NH)])
        grads = vjp((d_out, ds_scr[...]))
        ds_scr[...] = grads[6]
        lane = _iota2((CH, LANES), 1)
        rowi = _iota2((LANES, CH), 0)
        for b in range(nb):
            d_beta = jnp.zeros((CH, LANES), F32)
            d_gc = jnp.zeros((CH, LANES), F32)
            d_gct = jnp.zeros((LANES, CH), F32)
            for hd in range(NH):
                i = b * NH + hd
                for part in range(3):
                    dqkv_ref[b, :, _dn_cols(part, hd)] = grads[part][i]
                d_beta = d_beta + jnp.where(lane == hd, grads[3][i], 0.0)
                d_gc = d_gc + jnp.where(lane == NH + hd, grads[4][i], 0.0)
                d_gct = d_gct + jnp.where(rowi == NH + hd, grads[5][i], 0.0)
            d_small, d_alp, d_dtp = gate_vjps[b]((d_beta, d_gc, d_gct))
            dsmall_ref[b] = d_small.astype(BF16)
            dalp_ref[...] += d_alp
            ddtp_ref[...] += d_dtp
        if extra:
            pl.when((bb == bl // nb - 1) & (r == nc - 1))(finish)

    blk = lambda bb, r: (bb, nc - 1 - r, 0)
    const = lambda bb, r: (0, 0)
    saved = pl.BlockSpec((None, ng, DH, DH), lambda bb, r: (bb * nc + nc - 1 - r, 0, 0, 0))
    return pl.pallas_call(
        body, grid=(bl // nb, nc),
        in_specs=[pl.BlockSpec((nb, CH, 3 * DNW), blk), pl.BlockSpec((nb, CH, LANES), blk), pl.BlockSpec((1, LANES), const),
                  pl.BlockSpec((1, LANES), const), saved, saved, pl.BlockSpec((nb, CH, DNW), blk)] + [HBM_SPEC] * len(extra),
        out_specs=[pl.BlockSpec((nb, CH, 3 * DNW), blk), pl.BlockSpec((nb, CH, LANES), blk), pl.BlockSpec((1, LANES), const),
                   pl.BlockSpec((1, LANES), const)] + [HBM_SPEC] * len(extra),
        out_shape=[SDS((bl, seq, 3 * DNW), F32), SDS((bl, seq, LANES), BF16), SDS((1, LANES), F32), SDS((1, LANES), F32)]
        + [SDS(x.shape, x.dtype) for x in extra],
        scratch_shapes=[pltpu.VMEM((ng, DH, DH), F32)] + (_comm_scratch() if extra else []), name="deltanet_bwd",
        compiler_params=_cp(2))(qkv, p_small, alp, dtp, sprev, tinv, d_o, *extra)


def _s5_table_specs():
    tab3 = pl.BlockSpec((None, LANES, 512), lambda gb, n: (gb, 0, 0))
    tab2 = pl.BlockSpec((S5_CH, 512), lambda gb, n: (0, gb))
    return [tab3] * 4 + [tab2] * 6 + [pl.BlockSpec((1, LANES), lambda gb, n: (0, gb))]


def s5_fwd(u, tables, dsk):
    bl, seq, _ = u.shape
    nc = seq // S5_CH

    def body(u_ref, *rest):
        tabs, (y_ref, xs_ref, xr_scr, xi_scr) = rest[:11], rest[11:]

        @pl.when(pl.program_id(1) == 0)
        def _():
            xr_scr[...] = jnp.zeros_like(xr_scr)
            xi_scr[...] = jnp.zeros_like(xi_scr)

        xp_re, xp_im = xr_scr[...], xi_scr[...]
        xs_ref[0:bl] = xp_re
        xs_ref[bl:2 * bl] = xp_im
        y, xn_re, xn_im = s5_chunk(u_ref[...], xp_re, xp_im, *[t[...] for t in tabs])
        y_ref[...] = y
        xr_scr[...] = xn_re
        xi_scr[...] = xn_im

    blk = lambda gb, n: (0, n, gb)
    return pl.pallas_call(
        body, grid=(GB, nc), in_specs=[pl.BlockSpec((bl, S5_CH, LANES), blk)] + _s5_table_specs(),
        out_specs=[pl.BlockSpec((bl, S5_CH, LANES), blk),
                   pl.BlockSpec((None, 2 * bl, 1, 512), lambda gb, n: (gb * nc + n, 0, 0, 0))],
        out_shape=[SDS((bl, seq, S5W), F32), SDS((GB * nc, 2 * bl, 1, 512), F32)],
        scratch_shapes=[pltpu.VMEM((bl, 1, 512), F32), pltpu.VMEM((bl, 1, 512), F32)], name="s5_fwd",
        compiler_params=_cp(2))(u, *tables, dsk)


def s5_bwd(u, tables, dsk, xs, dy):
    bl, seq, _ = u.shape
    nc = seq // S5_CH

    def body(u_ref, *rest):
        tabs, xs_ref, dy_ref = rest[:11], rest[11], rest[12]
        du_ref, dtabs, dxr_scr, dxi_scr = rest[13], rest[14:25], rest[25], rest[26]
        r = pl.program_id(1)

        @pl.when(r == 0)
        def _():
            for t in dtabs:
                t[...] = jnp.zeros_like(t)
            dxr_scr[...] = jnp.zeros_like(dxr_scr)
            dxi_scr[...] = jnp.zeros_like(dxi_scr)

        _, vjp = jax.vjp(s5_chunk, u_ref[...], xs_ref[0:bl], xs_ref[bl:2 * bl], *[t[...] for t in tabs])
        grads = vjp((dy_ref[...], dxr_scr[...], dxi_scr[...]))
        du_ref[...] = grads[0].astype(BF16)
        dxr_scr[...] = grads[1]
        dxi_scr[...] = grads[2]
        for t, g in zip(dtabs, grads[3:]):
            t[...] += g

    blk = lambda gb, r: (0, nc - 1 - r, gb)
    tab_shapes = [SDS(t.shape, F32) for t in tables] + [SDS(dsk.shape, F32)]
    return pl.pallas_call(
        body, grid=(GB, nc),
        in_specs=[pl.BlockSpec((bl, S5_CH, LANES), blk)] + _s5_table_specs()
        + [pl.BlockSpec((None, 2 * bl, 1, 512), lambda gb, r: (gb * nc + nc - 1 - r, 0, 0, 0)), pl.BlockSpec((bl, S5_CH, LANES), blk)],
        out_specs=[pl.BlockSpec((bl, S5_CH, LANES), blk)] + _s5_table_specs(),
        out_shape=[SDS((bl, seq, S5W), BF16)] + tab_shapes,
        scratch_shapes=[pltpu.VMEM((bl, 1, 512), F32), pltpu.VMEM((bl, 1, 512), F32)], name="s5_bwd",
        compiler_params=_cp(2))(u, *tables, dsk, xs, dy)


def s5_tables_fwd(params):
    shapes = [SDS((GB, LANES, 512), F32)] * 4 + [SDS((S5_CH, S5N), F32)] * 6

    def body(*refs):
        for r, t in zip(refs[7:], s5_tables(*[p[...] for p in refs[:7]])):
            r[...] = t

    return pl.pallas_call(body, out_shape=shapes, name="s5_tables_fwd", compiler_params=_cp())(*params)


def s5_tables_bwd(params, dtables):
    def body(*refs):
        _, vjp = jax.vjp(s5_tables, *[p[...] for p in refs[:7]])
        for r, g in zip(refs[17:], vjp(tuple(t[...] for t in refs[7:17]))):
            r[...] = g

    return pl.pallas_call(body, out_shape=[SDS(p.shape, F32) for p in params], name="s5_tables_bwd",
                          compiler_params=_cp())(*params, *dtables)


def ada_fwd(c_all, w_loc, b_loc):
    def body(c_ref, w_ref, b_ref, o_ref):
        o_ref[...] = _dot(_silu(c_ref[...]), w_ref[...]) + b_ref[...]

    return pl.pallas_call(body, out_shape=SDS((c_all.shape[0], w_loc.shape[1]), F32), name="ada_fwd",
                          compiler_params=_cp())(c_all, w_loc, b_loc)


def ada_bwd(c_all, dmod_mine, dmod_all):
    def body(c_ref, dm_ref, da_ref, gw_ref, gb_ref):
        gw_ref[...] = _dot_tn(_silu(c_ref[...]), dm_ref[...])
        gb_ref[...] = jnp.sum(da_ref[...], axis=0, keepdims=True)

    return pl.pallas_call(body, out_shape=[SDS((D, dmod_mine.shape[1]), F32), SDS((1, dmod_all.shape[1]), F32)],
                          name="ada_bwd", compiler_params=_cp())(c_all, dmod_mine, dmod_all)


def loss_head(h, tgt, g, seq):
    t_rows = h.shape[0]
    tm = _pick(seq, (256, 128, 64))

    def body(h_ref, t_ref, g_ref, dh_ref, dg_ref, loss_ref):
        i = pl.program_id(0)
        y, vjp = jax.vjp(lambda hh, gg: hh * lax.rsqrt(jnp.mean(hh * hh, axis=-1, keepdims=True) + EPS) * gg,
                         h_ref[...], g_ref[...])
        e = y - t_ref[...]
        dh, dg = vjp(e * (1.0 / D))
        part = jnp.sum(jnp.sum(e * e, axis=1, keepdims=True), axis=0, keepdims=True) * (0.5 / D) + jnp.zeros((1, LANES), F32)
        dh_ref[...] = dh

        @pl.when(i == 0)
        def _():
            dg_ref[...] = dg
            loss_ref[...] = part

        @pl.when(i != 0)
        def _():
            dg_ref[...] += dg
            loss_ref[...] += part

    row = lambda i: (i, 0)
    const = lambda i: (0, 0)
    return pl.pallas_call(
        body, grid=(t_rows // tm,),
        in_specs=[pl.BlockSpec((tm, D), row), pl.BlockSpec((tm, D), row), pl.BlockSpec((1, D), const)],
        out_specs=[pl.BlockSpec((tm, D), row), pl.BlockSpec((1, D), const), pl.BlockSpec((1, LANES), const)],
        out_shape=[SDS((t_rows, D), F32), SDS((1, D), F32), SDS((1, LANES), F32)], name="loss_head",
        compiler_params=_cp(1))(h, tgt, g)


def adamw(name, parts, w, m, v):
    k_parts, rows, cols = parts.shape
    tr = _pick(rows, (256, 128, 64, 32, 16, 8))

    def body(p_ref, w_ref, m_ref, v_ref, g_ref, d_ref, mo_ref, vo_ref):
        g = p_ref[0].astype(F32)
        for k in range(1, k_parts):
            g = g + p_ref[k].astype(F32)
        _adam_store(g, w_ref, m_ref, v_ref, g_ref, d_ref, mo_ref, vo_ref)

    blk = pl.BlockSpec((tr, cols), lambda i: (i, 0))
    return pl.pallas_call(
        body, grid=(rows // tr,), in_specs=[pl.BlockSpec((k_parts, tr, cols), lambda i: (0, i, 0)), blk, blk, blk],
        out_specs=[blk] * 4, out_shape=[SDS((rows, cols), F32)] * 4, name=name, compiler_params=_cp(1))(parts, w, m, v)


def _adam_store(g, w_ref, m_ref, v_ref, g_ref, d_ref, mo_ref, vo_ref):
    m_new = ADAM_B1 * m_ref[...] + (1.0 - ADAM_B1) * g
    v_new = ADAM_B2 * v_ref[...] + (1.0 - ADAM_B2) * (g * g)
    m_hat = m_new / (1.0 - ADAM_B1 ** ADAM_STEP)
    v_hat = v_new / (1.0 - ADAM_B2 ** ADAM_STEP)
    g_ref[...] = g
    d_ref[...] = -ADAM_LR * (m_hat / (jnp.sqrt(v_hat) + ADAM_EPS) + ADAM_WD * w_ref[...])
    mo_ref[...] = m_new
    vo_ref[...] = v_new


def adamw_t(name, parts, w, m, v):
    k_parts, r, c = parts.shape
    tc = _pick(c, (256, 128))

    def body(p_ref, w_ref, m_ref, v_ref, g_ref, d_ref, mo_ref, vo_ref):
        gt = p_ref[0].astype(F32)
        for k in range(1, k_parts):
            gt = gt + p_ref[k].astype(F32)
        _adam_store(gt.T, w_ref, m_ref, v_ref, g_ref, d_ref, mo_ref, vo_ref)

    blk = pl.BlockSpec((tc, r), lambda j: (j, 0))
    return pl.pallas_call(
        body, grid=(c // tc,), in_specs=[pl.BlockSpec((k_parts, r, tc), lambda j: (0, 0, j)), blk, blk, blk],
        out_specs=[blk] * 4, out_shape=[SDS((c, r), F32)] * 4, name=name, compiler_params=_cp(1))(parts, w, m, v)


def _comm_scratch():
    return [pltpu.SemaphoreType.DMA((7,)), pltpu.SemaphoreType.DMA((7,)), pltpu.SemaphoreType.DMA]


HBM_SPEC = pl.BlockSpec(memory_space=pl.ANY)


def _gather_phases(x_ref, out_ref, send_sems, recv_sems, local_sem):
    mx, my, mc = lax.axis_index("x"), lax.axis_index("y"), lax.axis_index("c")
    me, sibling = (mx, my, mc), (mx, my, 1 - mc)
    chips = [(1 - mx, my), (mx, 1 - my), (1 - mx, 1 - my)]

    def slot(px, py, pc):
        return out_ref.at[4 * px + 2 * py + pc]

    def copy(k, block, to, src=None):
        return pltpu.make_async_remote_copy(
            src_ref=slot(*block) if src is None else src, dst_ref=slot(*block), send_sem=send_sems.at[k],
            recv_sem=recv_sems.at[k], device_id=to, device_id_type=pl.DeviceIdType.MESH)

    def first():
        return [copy(0, me, sibling, src=x_ref)] + [copy(1 + j, me, (*chip, mc), src=x_ref) for j, chip in enumerate(chips)]

    def passed():
        return [copy(4 + j, (*chip, mc), sibling) for j, chip in enumerate(chips)]

    def start():
        pltpu.make_async_copy(x_ref, slot(*me), local_sem).start()
        for cp in first():
            cp.start()

    def forward():
        for j, chip in enumerate(chips):
            copy(1 + j, (*chip, mc), me).wait_recv()
            passed()[j].start()

    def finish():
        copy(0, sibling, me).wait_recv()
        for j, chip in enumerate(chips):
            copy(4 + j, (*chip, 1 - mc), me).wait_recv()
        for cp in first() + passed():
            cp.wait_send()
        pltpu.make_async_copy(x_ref, slot(*me), local_sem).wait()

    return start, forward, finish


def _exchange_phases(x_ref, out_ref, send_sems, recv_sems, local_sem):
    mx, my, mc = lax.axis_index("x"), lax.axis_index("y"), lax.axis_index("c")
    me = 4 * mx + 2 * my + mc

    def peer(k):
        return mx ^ (k >> 2), my ^ ((k >> 1) & 1), mc ^ (k & 1)

    def sends():
        out = []
        for k in range(1, NDEV):
            px, py, pc = peer(k)
            out.append(pltpu.make_async_remote_copy(
                src_ref=x_ref.at[4 * px + 2 * py + pc], dst_ref=out_ref.at[me], send_sem=send_sems.at[k - 1],
                recv_sem=recv_sems.at[k - 1], device_id=(px, py, pc), device_id_type=pl.DeviceIdType.MESH))
        return out

    def start():
        pltpu.make_async_copy(x_ref.at[me], out_ref.at[me], local_sem).start()
        for cp in sends():
            cp.start()

    def finish():
        for k in range(1, NDEV):
            px, py, pc = peer(k)
            pltpu.make_async_remote_copy(
                src_ref=x_ref.at[me], dst_ref=out_ref.at[4 * px + 2 * py + pc], send_sem=send_sems.at[k - 1],
                recv_sem=recv_sems.at[k - 1], device_id=(px, py, pc), device_id_type=pl.DeviceIdType.MESH).wait_recv()
        for cp in sends():
            cp.wait_send()
        pltpu.make_async_copy(x_ref.at[me], out_ref.at[me], local_sem).wait()

    return start, finish


def all_gather(name, x):
    def body(x_ref, out_ref, send_sems, recv_sems, local_sem):
        for phase in _gather_phases(x_ref, out_ref, send_sems, recv_sems, local_sem):
            phase()

    return pl.pallas_call(body, out_shape=SDS((NDEV,) + x.shape, x.dtype), in_specs=[HBM_SPEC], out_specs=HBM_SPEC,
                          scratch_shapes=_comm_scratch(), name=name)(x)


def all_gather_pair(name, x1, x2):
    def body(x1_ref, x2_ref, o1_ref, o2_ref, *sems):
        first = _gather_phases(x1_ref, o1_ref, *sems[:3])
        second = _gather_phases(x2_ref, o2_ref, *sems[3:])
        for phase1, phase2 in zip(first, second):
            phase1()
            phase2()

    return pl.pallas_call(
        body, out_shape=[SDS((NDEV,) + x1.shape, x1.dtype), SDS((NDEV,) + x2.shape, x2.dtype)], in_specs=[HBM_SPEC] * 2,
        out_specs=[HBM_SPEC] * 2, scratch_shapes=_comm_scratch() + _comm_scratch(), name=name)(x1, x2)


def all_to_all(name, x):
    def body(x_ref, out_ref, send_sems, recv_sems, local_sem):
        for phase in _exchange_phases(x_ref, out_ref, send_sems, recv_sems, local_sem):
            phase()

    return pl.pallas_call(body, out_shape=SDS(x.shape, x.dtype), in_specs=[HBM_SPEC], out_specs=HBM_SPEC,
                          scratch_shapes=_comm_scratch(), name=name)(x)


def _pack(arrs, dtype, row_mult=8):
    segs = []
    for a in arrs:
        flat = a.reshape(-1).astype(dtype)
        segs.append(jnp.pad(flat, (0, (-flat.shape[0]) % ROW)))
    flat = jnp.concatenate(segs)
    flat = jnp.pad(flat, (0, (-flat.shape[0]) % (ROW * row_mult)))
    return flat.reshape(-1, ROW)


def _unpack(buf, shapes):
    flat = buf.reshape(-1)
    out, off = [], 0
    for s in shapes:
        n = math.prod(s)
        out.append(flat[off:off + n].reshape(s))
        off += n + (-n) % ROW
    return out


def _pack_rows(arrs, axis):
    padded = []
    for t in arrs:
        pad = [(0, 0)] * t.ndim
        pad[axis] = (0, _tile_rows(t.shape[axis]) - t.shape[axis])
        padded.append(jnp.pad(t, pad))
    return jnp.concatenate(padded, axis=axis)


def _tile_rows(r):
    return r + (-r) % BF16_TILE_ROWS


def _unpack8(buf, shapes):
    flat = buf.reshape(NDEV, -1)
    out, off = [], 0
    for s in shapes:
        n = math.prod(s)
        out.append(flat[:, off:off + n].reshape((NDEV,) + tuple(s)))
        off += n + (-n) % ROW
    return out


def kernel(x, c, w_ada, b_ada, g_ffn1, w1_ffn1, w3_ffn1, w2_ffn1, g_mix, w_in, conv_qkv, a_log, dt_bias, g_onorm, lam_re, lam_im, log_step, b_re, b_im, c_re, c_im, d_skip, w_glu, b_glu, w_proj_a, w_proj_b, w_out, g_ffn2, w1_ffn2, w3_ffn2, w2_ffn2, g_final, loss_target, m_w_ada, m_b_ada, m_g_ffn1, m_w1_ffn1, m_w3_ffn1, m_w2_ffn1, m_g_mix, m_w_in, m_conv_qkv, m_a_log, m_dt_bias, m_g_onorm, m_lam_re, m_lam_im, m_log_step, m_b_re, m_b_im, m_c_re, m_c_im, m_d_skip, m_w_glu, m_b_glu, m_w_proj_a, m_w_proj_b, m_w_out, m_g_ffn2, m_w1_ffn2, m_w3_ffn2, m_w2_ffn2, m_g_final, v_w_ada, v_b_ada, v_g_ffn1, v_w1_ffn1, v_w3_ffn1, v_w2_ffn1, v_g_mix, v_w_in, v_conv_qkv, v_a_log, v_dt_bias, v_g_onorm, v_lam_re, v_lam_im, v_log_step, v_b_re, v_b_im, v_c_re, v_c_im, v_d_skip, v_w_glu, v_b_glu, v_w_proj_a, v_w_proj_b, v_w_out, v_g_ffn2, v_w1_ffn2, v_w3_ffn2, v_w2_ffn2, v_g_final):
    a = dict(locals())
    bl, seq, _ = x.shape
    t_rows = bl * seq
    me = 4 * lax.axis_index("x") + 2 * lax.axis_index("y") + lax.axis_index("c")
    tm_ew = _pick(seq, (256, 128, 64))

    loc = {n: (a[n][0].T if n in COL_SHARDED else a[n][0]) for n in RS_WEIGHTS}
    wfull, gw, res = {}, {}, {}

    def pack_local(names):
        return _pack_rows([loc[n].astype(BF16).reshape(-1, ROW) for n in names], 0)

    def unpack_full(buf, names):
        r0 = 0
        for n in names:
            r = loc[n].size // ROW
            wfull[n] = buf[:, r0:r0 + r, :].reshape(-1, loc[n].shape[1])
            r0 += _tile_rows(r)

    def pack_grads(names):
        return _pack_rows([gw[n].astype(BF16).reshape(NDEV, -1, ROW) for n in names], 1)

    def update(buf, names):
        r0 = 0
        for n in names:
            r = loc[n].size // ROW
            parts = buf[:, r0:r0 + r, :].reshape((NDEV,) + loc[n].shape)
            r0 += _tile_rows(r)
            step = adamw_t if n in COL_SHARDED else adamw
            out = step("adamw_" + n, parts, a[n][0], a["m_" + n][0], a["v_" + n][0])
            for kind, t in zip(("grad", "delta", "new_m", "new_v"), out):
                res[kind + "_" + n] = t[None]

    sm, wg_ffn1 = all_gather_pair("gather_inputs", _pack([c, conv_qkv[0]], F32), pack_local(G_FFN1))
    unpack_full(wg_ffn1, G_FFN1)
    c_loc, conv_loc = _unpack8(sm, [c.shape, conv_qkv.shape[1:]])
    c_all = c_loc.reshape(NDEV * bl, D)
    conv_full = conv_loc.transpose(1, 0, 2).reshape(CONVW, 3 * DNW)

    n_ada = w_ada.shape[2]
    mod_part = ada_fwd(c_all, w_ada[0], lax.dynamic_slice(b_ada, (0, me * n_ada), (1, n_ada)))
    mod_all = all_gather("gather_mod", mod_part).transpose(1, 0, 2).reshape(NDEV * bl, 9 * D)
    mod = lax.dynamic_slice(mod_all, (me * bl, 0), (bl, 9 * D)).reshape(bl, 9, D)
    mods = [mod[:, k:k + 1, :] for k in range(9)]

    h0 = x.reshape(t_rows, D)
    h1, f1, u1, pa1, pb1, wg_rest = ffn_fwd("ffn1_fwd", h0, mod[:, 0:3, :], g_ffn1, wfull['w1_ffn1'], wfull['w3_ffn1'],
                                  wfull['w2_ffn1'], seq, gather=pack_local(G_MIX + G_FFN2))
    unpack_full(wg_rest, G_MIX + G_FFN2)
    win = wfull['w_in']
    o_small, o_s5, o_gate = 4 * DNW, 4 * DNW + 2 * NH, 4 * DNW + 2 * NH + S5W
    w_dn, w_small = win[:o_small], jnp.pad(win[o_small:o_s5], ((0, LANES - 2 * NH), (0, 0)))
    w_s5, w_gate = win[o_s5:o_gate], win[o_gate:]
    w_pieces = [w_dn, w_small, w_s5, w_gate]
    u2, p_dn, p_small, p_s5, p_gate = mix_in_fwd(h1, mods[3], mods[4], g_mix, w_pieces, seq)

    conv8 = jnp.pad(conv_full, ((0, 8 - CONVW), (0, 0)))
    alp = jnp.pad(a_log, ((0, 0), (NH, LANES - 2 * NH)))
    dtp = jnp.pad(dt_bias, ((0, 0), (NH, LANES - 2 * NH)))
    nb_dn = DN_ROWS if bl % DN_ROWS == 0 else 1
    p_dn3, p_small3 = p_dn.reshape(bl, seq, 4 * DNW), p_small.reshape(bl, seq, LANES)
    qkv3 = dn_prep_fwd(p_dn3, conv8)
    o_pre3, sprev, tinv = deltanet_fwd(qkv3, p_small3, alp, dtp, nb_dn)
    o_pre = o_pre3.reshape(t_rows, DNW)
    z_raw = p_dn[:, 3 * DNW:]

    s5_params = [lam_re.reshape(1, S5N), lam_im.reshape(1, S5N), log_step,
                 b_re[0].transpose(2, 0, 1).reshape(S5C, S5N), b_im[0].transpose(2, 0, 1).reshape(S5C, S5N),
                 c_re[0].transpose(1, 0, 2).reshape(S5C, S5N), c_im[0].transpose(1, 0, 2).reshape(S5C, S5N)]
    tables = s5_tables_fwd(s5_params)
    p_s53 = p_s5.reshape(bl, seq, S5W)
    y_s53, xs = s5_fwd(p_s53, tables, d_skip)
    y_s5 = y_s53.reshape(t_rows, S5W)
    tail_in = [o_pre, z_raw, y_s5, p_gate]
    tail_w = [g_onorm, wfull['w_glu'], b_glu, wfull['w_proj_a'], wfull['w_proj_b']]
    (merged,) = ew_call("mix_tail", fn_mix_tail, tail_in, [], tail_w, [(D, BF16)], tm_ew, seq)
    mo, h2 = mix_out_fwd(merged, wfull['w_out'], h1, mods[5], seq)
    h3, f3, u3, pa3, pb3 = ffn_fwd("ffn2_fwd", h2, mod[:, 6:9, :], g_ffn2, wfull['w1_ffn2'], wfull['w3_ffn2'], wfull['w2_ffn2'], seq)

    dh3, dg_final, loss_part = loss_head(h3, loss_target.reshape(t_rows, D), g_final.reshape(1, D), seq)

    dh2, a3, d1_3, d3_3, df3, dmod_c, dg_ffn2 = ffn_bwd("ffn2_bwd", dh3, h2, f3, pa3, pb3, mod[:, 6:9, :], g_ffn2, wfull['w1_ffn2'],
                                                   wfull['w3_ffn2'], wfull['w2_ffn2'], seq)
    gw['w1_ffn2'] = mm_tn("gw1_ffn2", d1_3, u3)
    gw['w3_ffn2'] = mm_tn("gw3_ffn2", d3_3, u3)
    gw['w2_ffn2'] = mm_tn("gw2_ffn2", a3, df3)

    dmo, d_merged, dgt2 = mix_out_bwd(dh2, mo, wfull['w_out'], mods[5], seq)
    gw['w_out'] = mm_tn("gw_out", merged, dmo)
    (d_opre, d_z, d_ys5, d_gate), _, tail_gw = ew_vjp_call(
        "mix_tail_bwd", fn_mix_tail, tail_in, [], tail_w, [d_merged], [(0, F32), (1, F32), (2, F32), (3, BF16)],
        _pick(seq, (512, 256, 128, 64)), seq)
    dg_onorm, gw['w_glu'], dg_bglu, gw['w_proj_a'], gw['w_proj_b'] = tail_gw
    d_qkv3, d_psmall3, d_alp, d_dtp, rs_ffn2 = deltanet_bwd(
        qkv3, p_small3, alp, dtp, sprev, tinv, d_opre.reshape(bl, seq, DNW), nb_dn, exchange=pack_grads(G_FFN2))
    d_pdn3, d_conv8 = dn_prep_bwd(p_dn3, conv8, d_qkv3, d_z.reshape(bl, seq, DNW))
    d_pdn, d_psmall = d_pdn3.reshape(t_rows, 4 * DNW), d_psmall3.reshape(t_rows, LANES)

    s5_out = s5_bwd(p_s53, tables, d_skip, xs, d_ys5.reshape(bl, seq, S5W))
    d_ps5, d_tables, dg_dskip = s5_out[0].reshape(t_rows, S5W), s5_out[1:11], s5_out[11]
    d_s5p = s5_tables_bwd(s5_params, d_tables)

    gw['w_in'] = jnp.concatenate([mm_tn("gw_dn", d_pdn, u2), mm_tn("gw_small", d_psmall, u2)[:2 * NH],
                                  mm_tn("gw_s5", d_ps5, u2), mm_tn("gw_gate", d_gate, u2)], axis=0)
    dh1, dsh2, dsc2, dg_mix = mix_in_bwd([d_pdn, d_psmall, d_ps5, d_gate], w_pieces, h1, mods[3], mods[4], g_mix, dh2, seq)

    dh0, a1, d1_1, d3_1, df1, dmod_a, dg_ffn1, rs_mix = ffn_bwd(
        "ffn1_bwd", dh1, h0, f1, pa1, pb1, mod[:, 0:3, :], g_ffn1, wfull['w1_ffn1'], wfull['w3_ffn1'], wfull['w2_ffn1'], seq,
        exchange=pack_grads(G_MIX))
    dmod_mine = jnp.concatenate([dmod_a, dsh2, dsc2, dgt2, dmod_c], axis=1).reshape(bl, 9 * D)
    small_grads = {
        'g_ffn1': dg_ffn1, 'g_mix': dg_mix, 'a_log': d_alp[:, NH:2 * NH], 'dt_bias': d_dtp[:, NH:2 * NH],
        'g_onorm': dg_onorm, 'lam_re': d_s5p[0].reshape(1, S5G, S5P), 'lam_im': d_s5p[1].reshape(1, S5G, S5P),
        'log_step': d_s5p[2],
        'b_re': d_s5p[3].reshape(S5C, S5G, S5P).transpose(1, 2, 0)[None],
        'b_im': d_s5p[4].reshape(S5C, S5G, S5P).transpose(1, 2, 0)[None],
        'c_re': d_s5p[5].reshape(S5C, S5G, S5P).transpose(1, 0, 2)[None],
        'c_im': d_s5p[6].reshape(S5C, S5G, S5P).transpose(1, 0, 2)[None],
        'd_skip': dg_dskip, 'b_glu': dg_bglu, 'g_ffn2': dg_ffn2, 'g_final': dg_final.reshape(D)}
    small_shapes = [a[n].shape for n in SMALL]
    small_pack = _pack([small_grads[n] for n in SMALL] + [loss_part], F32)
    n_small = small_pack.shape[0]
    small_buf = jnp.concatenate([small_pack, _pack([dmod_mine, d_conv8[:CONVW]], F32)], axis=0)

    gw['w1_ffn1'] = mm_tn("gw1_ffn1", d1_1, u1)
    gw['w3_ffn1'], rs_w1 = mm_tn("gw3_ffn1", d3_1, u1, exchange=pack_grads(['w1_ffn1']))
    gw['w2_ffn1'], rs_w3, sg = mm_tn("gw2_ffn1", a1, df1, exchange=pack_grads(['w3_ffn1']), gather=small_buf)
    rs_w2 = all_to_all("scatter_w2_ffn1", pack_grads(['w2_ffn1']))

    update(rs_ffn2, G_FFN2)
    update(rs_mix, G_MIX)
    update(rs_w1, ['w1_ffn1'])
    update(rs_w3, ['w3_ffn1'])
    update(rs_w2, ['w2_ffn1'])
    pieces = _unpack8(sg[:, n_small:, :], [dmod_mine.shape, (CONVW, 3 * DNW)])
    dmod_all = pieces[0].reshape(NDEV * bl, 9 * D)
    g_wada, g_bada = ada_bwd(c_all, lax.dynamic_slice(dmod_all, (0, me * n_ada), (NDEV * bl, n_ada)), dmod_all)

    n_conv = conv_qkv.shape[2]
    conv_parts = lax.dynamic_slice(pieces[1], (0, 0, me * n_conv), (NDEV, CONVW, n_conv))
    conv_parts = jnp.pad(conv_parts.reshape(NDEV, 1, -1), ((0, 0), (0, 7), (0, 0)))
    pad8 = lambda t: jnp.pad(t.reshape(1, -1), ((0, 7), (0, 0)))
    conv_res = adamw("adamw_conv", conv_parts, pad8(conv_qkv), pad8(m_conv_qkv), pad8(v_conv_qkv))
    for kind, buf in zip(("grad", "delta", "new_m", "new_v"), conv_res):
        res[kind + "_conv_qkv"] = buf[0].reshape(conv_qkv.shape)

    no_param = jnp.zeros_like(loss_part)
    small_res = adamw("adamw_small", sg[:, :n_small, :],
                      *[_pack([a[p + n] for n in SMALL] + [no_param], F32) for p in ("", "m_", "v_")])
    for kind, buf in zip(("grad", "delta", "new_m", "new_v"), small_res):
        for n, t in zip(SMALL, _unpack(buf, small_shapes)):
            res[kind + "_" + n] = t
    loss = _unpack(small_res[0], small_shapes + [loss_part.shape])[-1][0, 0]

    for n, g in (("w_ada", g_wada), ("b_ada", g_bada)):
        shp = a[n].shape
        r2 = lambda t: t.reshape(-1, shp[-1]) if n == "w_ada" else pad8(t)
        out = adamw("adamw_" + n, r2(g)[None], r2(a[n]), r2(a["m_" + n]), r2(a["v_" + n]))
        for kind, buf in zip(("grad", "delta", "new_m", "new_v"), out):
            res[kind + "_" + n] = (buf if n == "w_ada" else buf[0:1]).reshape(shp)

    outs = [loss, dh0.reshape(x.shape)]
    for kind in ("grad", "delta", "new_m", "new_v"):
        outs += [res[kind + "_" + n] for n in WEIGHTS]
    return tuple(outs)
```

```python
import math

import jax
import jax.numpy as jnp
from jax import lax
from jax.experimental import pallas as pl
from jax.experimental.pallas import tpu as pltpu

F32 = jnp.float32
BF16 = jnp.bfloat16
HI = lax.Precision.HIGHEST
H3 = lax.Precision.HIGH
SDS = jax.ShapeDtypeStruct

D = 1024
FF = 2816
FFN_TF = FF
FFN_FWD_TM = 256
FFN_BWD_TM = 256
NH = 8
DH = 64
DNW = NH * DH
CONVW = 4
CH = 64
S5_CH = 128
ACC_LIMIT = 6 * 1024 * 1024
BF16_TILE_ROWS = 16
DN_ROWS = 4
S5W = 512
S5G = 32
S5P = 64
S5C = 16
S5N = S5G * S5P
GB = 4
NDEV = 8
EPS = 1e-6
LANES = 128
ROW = 1024
VMEM_LIMIT = 56 * 1024 * 1024

ADAM_LR, ADAM_B1, ADAM_B2, ADAM_EPS, ADAM_WD, ADAM_STEP = 0.001, 0.9, 0.999, 1e-08, 0.01, 10

WEIGHTS = ['w_ada', 'b_ada', 'g_ffn1', 'w1_ffn1', 'w3_ffn1', 'w2_ffn1', 'g_mix', 'w_in', 'conv_qkv', 'a_log',
           'dt_bias', 'g_onorm', 'lam_re', 'lam_im', 'log_step', 'b_re', 'b_im', 'c_re', 'c_im', 'd_skip', 'w_glu',
           'b_glu', 'w_proj_a', 'w_proj_b', 'w_out', 'g_ffn2', 'w1_ffn2', 'w3_ffn2', 'w2_ffn2', 'g_final']
RS_WEIGHTS = ['w1_ffn1', 'w3_ffn1', 'w2_ffn1', 'w_in', 'w_glu', 'w_proj_a', 'w_proj_b', 'w_out', 'w1_ffn2', 'w3_ffn2',
              'w2_ffn2']
COL_SHARDED = {'w1_ffn1', 'w3_ffn1', 'w_in', 'w_proj_a', 'w_proj_b', 'w1_ffn2', 'w3_ffn2'}
G_FFN1 = ['w1_ffn1', 'w3_ffn1', 'w2_ffn1']
G_MIX = ['w_in', 'w_glu', 'w_proj_a', 'w_proj_b', 'w_out']
G_FFN2 = ['w1_ffn2', 'w3_ffn2', 'w2_ffn2']
SMALL = ['g_ffn1', 'g_mix', 'a_log', 'dt_bias', 'g_onorm', 'lam_re', 'lam_im', 'log_step', 'b_re', 'b_im', 'c_re',
         'c_im', 'd_skip', 'b_glu', 'g_ffn2', 'g_final']


def _cp(n_grid=0):
    if n_grid:
        return pltpu.CompilerParams(vmem_limit_bytes=VMEM_LIMIT, dimension_semantics=("arbitrary",) * n_grid)
    return pltpu.CompilerParams(vmem_limit_bytes=VMEM_LIMIT)


def _dot(a, b):
    return jnp.dot(a.astype(BF16), b.astype(BF16), preferred_element_type=F32)


def _dot_nt(a, b):
    return lax.dot_general(a.astype(BF16), b.astype(BF16), (((1,), (1,)), ((), ())), preferred_element_type=F32)


def _dot_tn(a, b):
    return lax.dot_general(a.astype(BF16), b.astype(BF16), (((0,), (0,)), ((), ())), preferred_element_type=F32)


def _dot_hi(a, b):
    return jnp.dot(a, b, precision=HI, preferred_element_type=F32)


def _dot_h3(a, b):
    return jnp.dot(a, b, precision=H3, preferred_element_type=F32)


@jax.custom_vjp
def bdot(a, b):
    return _dot(a, b)


bdot.defvjp(lambda a, b: (_dot(a, b), (a, b)),
            lambda r, g: (_dot_nt(g, r[1]).astype(r[0].dtype), _dot_tn(r[0], g).astype(r[1].dtype)))


@jax.custom_vjp
def bdot_nt(a, b):
    return _dot_nt(a, b)


bdot_nt.defvjp(lambda a, b: (_dot_nt(a, b), (a, b)),
               lambda r, g: (_dot(g, r[1]).astype(r[0].dtype), _dot_tn(g, r[0]).astype(r[1].dtype)))


def _silu(x):
    return x * jax.nn.sigmoid(x)


def _iota2(shape, axis):
    return lax.broadcasted_iota(jnp.int32, shape, axis)


def normmod(h, g, sc, sh):
    y = h * lax.rsqrt(jnp.mean(h * h, axis=-1, keepdims=True) + EPS) * g
    return y * (1.0 + sc) + sh


def fn_merge(gate, ya, yb):
    return (jax.nn.sigmoid(gate[:, :D]) * ya + jax.nn.sigmoid(gate[:, D:]) * yb,)


def fn_glu(y, w, b):
    ge = jax.nn.gelu(y)
    return (ge * jax.nn.sigmoid(bdot(ge, w) + b),)


def fn_onorm(o, z, g_on):
    r = _iota2((DH, DNW), 0)
    c = _iota2((DH, DNW), 1)
    expand = (c % DH == r).astype(F32)
    r2 = _iota2((DNW, DNW), 0)
    c2 = _iota2((DNW, DNW), 1)
    avg = (r2 // DH == c2 // DH).astype(F32) * (1.0 / DH)
    ms = bdot(o * o, avg)
    return (o * lax.rsqrt(ms + EPS) * _dot_hi(g_on, expand) * _silu(z),)


def fn_mix_tail(o_pre, z, y_s5, gate, g_on, w_glu, b_glu, wa_t, wb_t):
    (oa,) = fn_onorm(o_pre, z, g_on)
    (ob,) = fn_glu(y_s5, w_glu, b_glu)
    return fn_merge(gate, bdot_nt(oa, wa_t), bdot_nt(ob, wb_t))


def gate_fn(small, alp, dtp):
    beta = jax.nn.sigmoid(small)
    la = -jnp.exp(alp) * jax.nn.softplus(small + dtp)
    tri = (_iota2((CH, CH), 0) >= _iota2((CH, CH), 1)).astype(F32)
    gc = _dot_hi(tri, la)
    gct = lax.dot_general(la, tri, (((0,), (1,)), ((), ())), precision=HI, preferred_element_type=F32)
    return beta, gc, gct


def _bdg(a, b, ca, cb, hi):
    if not hi:
        a, b = a.astype(BF16), b.astype(BF16)
    return lax.dot_general(a, b, (((ca,), (cb,)), ((0,), (0,))), precision=H3 if hi else None,
                           preferred_element_type=F32)


def _batched_matmuls(hi):
    nn_ = lambda a, b: _bdg(a, b, 2, 1, hi)
    nt_ = lambda a, b: _bdg(a, b, 2, 2, hi)
    tn_ = lambda a, b: _bdg(a, b, 1, 1, hi)
    nn = jax.custom_vjp(nn_)
    nn.defvjp(lambda a, b: (nn_(a, b), (a, b)), lambda r, g: (nt_(g, r[1]), tn_(r[0], g)))
    nt = jax.custom_vjp(nt_)
    nt.defvjp(lambda a, b: (nt_(a, b), (a, b)), lambda r, g: (nn_(g, r[1]), tn_(g, r[0])))
    tn = jax.custom_vjp(tn_)
    tn.defvjp(lambda a, b: (tn_(a, b), (a, b)), lambda r, g: (nt_(r[1], g), nn_(r[0], g)))
    return nn, nt, tn


bnn, bnt, btn = _batched_matmuls(False)
hnn, hnt, htn = _batched_matmuls(True)


def _unit_lower_inverse(a):
    r = _iota2((1, CH, CH), 1)
    c = _iota2((1, CH, CH), 2)
    eye = (r == c).astype(F32)
    d = jnp.where(r // 8 == c // 8, a, 0.0)
    inv = eye - d
    p = d
    for _ in range(2):
        p = hnn(p, p)
        inv = inv + hnn(inv, p)
    for blk in (16, 32, 64):
        off = jnp.where((r // blk == c // blk) & (r // (blk // 2) != c // (blk // 2)), a, 0.0)
        mm = hnn if blk == 16 else bnn
        inv = inv - mm(mm(inv, off), inv)
    return inv


@jax.custom_vjp
def _inverse_given(a, t):
    return t


_inverse_given.defvjp(lambda a, t: (t, t), lambda t, g: (-hnt(htn(t, g), t), jnp.zeros_like(t)))


def dn_prep(xc, w):
    t = xc.shape[0] - 8
    c = xc[5:5 + t] * w[0:1] + xc[6:6 + t] * w[1:2] + xc[7:7 + t] * w[2:3] + xc[8:8 + t] * w[3:4]
    act = _silu(c)
    q, k, v = act[:, :DNW], act[:, DNW:2 * DNW], act[:, 2 * DNW:]
    ones = (_iota2((DNW, DNW), 0) // DH == _iota2((DNW, DNW), 1) // DH).astype(F32)
    q = q * lax.rsqrt(bdot(q * q, ones) + EPS) * (DH ** -0.5)
    k = k * lax.rsqrt(bdot(k * k, ones) + EPS)
    return jnp.concatenate([q, k, v], axis=1)


def dn_chunk(q, k, v, b, g, gt, s_prev, t_saved=None):
    r = _iota2((1, CH, CH), 1)
    c = _iota2((1, CH, CH), 2)
    causal = r >= c
    dec = jnp.where(causal, jnp.exp(jnp.where(causal, g - gt, 0.0)), 0.0)
    kb = k * b
    qk = bnt(jnp.concatenate([q, kb], axis=1), k)
    attn = qk[:, :CH] * dec
    a = jnp.where(r > c, qk[:, CH:] * dec, 0.0)
    tinv = _unit_lower_inverse(a) if t_saved is None else _inverse_given(a, t_saved)
    eg = jnp.exp(g)
    uw = hnn(tinv, jnp.concatenate([v * b, kb * eg], axis=2))
    g_last = g[:, CH - 1:CH]
    ws = bnn(jnp.concatenate([uw[..., DH:], q * eg], axis=1), s_prev)
    v_new = uw[..., :DH] - ws[:, :CH]
    o = ws[:, CH:] + bnn(attn, v_new)
    s_new = s_prev * jnp.exp(g_last) + btn(k * jnp.exp(g_last - g), v_new)
    return o, s_new, tinv


def s5_chunk(u, xp_re, xp_im, bb_re, bb_im, cc_re, cc_im, p0r, p0i, p1r, p1i, pir, pii, dsk):
    nb, ch, _ = u.shape
    u2 = u.reshape(nb * ch, LANES)
    bu_re = bdot(u2, bb_re).reshape(nb, ch, 512)
    bu_im = bdot(u2, bb_im).reshape(nb, ch, 512)
    xt_re = pir * bu_re - pii * bu_im
    xt_im = pir * bu_im + pii * bu_re
    tri = jnp.broadcast_to((_iota2((1, ch, ch), 1) >= _iota2((1, ch, ch), 2)).astype(F32), (nb, ch, ch))
    cs_re = hnn(tri, xt_re)
    cs_im = hnn(tri, xt_im)
    x_re = p0r * cs_re - p0i * cs_im + p1r * xp_re - p1i * xp_im
    x_im = p0r * cs_im + p0i * cs_re + p1r * xp_im + p1i * xp_re
    y = bdot_nt(x_re.reshape(nb * ch, 512), cc_re) - bdot_nt(x_im.reshape(nb * ch, 512), cc_im) + dsk * u2
    return y.reshape(nb, ch, LANES), x_re[:, ch - 1:ch], x_im[:, ch - 1:ch]


def s5_tables(lam_re, lam_im, log_step, bre, bim, cre, cim):
    expand = (_iota2((S5G, S5N), 1) // S5P == _iota2((S5G, S5N), 0)).astype(F32)
    step = _dot_hi(jnp.exp(log_step), expand)
    lre = jnp.minimum(lam_re, -1e-4)
    lr = lre * step
    ang = lam_im * step
    mag = jnp.exp(lr)
    lb_re = mag * jnp.cos(ang)
    lb_im = mag * jnp.sin(ang)
    den = lre * lre + lam_im * lam_im
    coef_re = ((lb_re - 1.0) * lre + lb_im * lam_im) / den
    coef_im = (lb_im * lre - (lb_re - 1.0) * lam_im) / den
    bb_re = coef_re * bre - coef_im * bim
    bb_im = coef_re * bim + coef_im * bre
    j = _iota2((S5_CH, 1), 0).astype(F32)
    jc = j - S5_CH // 2
    e0 = jnp.exp(jc * lr)
    e1 = jnp.exp((j + 1.0) * lr)
    ei = jnp.exp(-jc * lr)
    mask = (_iota2((LANES, 512), 0) // S5C == _iota2((LANES, 512), 1) // S5P).astype(F32)

    def blocks(t):
        return jnp.concatenate([(jnp.tile(t[:, gb * 512:(gb + 1) * 512], (LANES // S5C, 1)) * mask)[None]
                                for gb in range(GB)], axis=0)

    return (blocks(bb_re), blocks(bb_im), blocks(cre), blocks(cim),
            e0 * jnp.cos(jc * ang), e0 * jnp.sin(jc * ang),
            e1 * jnp.cos((j + 1.0) * ang), e1 * jnp.sin((j + 1.0) * ang),
            ei * jnp.cos(jc * ang), -ei * jnp.sin(jc * ang))


def _row_specs(tiled, batch, bcast, tm, tpb):
    specs = [pl.BlockSpec((tm, a.shape[1]), lambda i: (i, 0)) for a in tiled]
    specs += [pl.BlockSpec((None,) + a.shape[1:], lambda i: (i // tpb, 0, 0)) for a in batch]
    specs += [pl.BlockSpec(a.shape, lambda i, nd=a.ndim: (0,) * nd) for a in bcast]
    return specs


def ew_call(name, fn, tiled, batch, bcast, outs, tm, seq):
    t_rows = tiled[0].shape[0]
    n_in = len(tiled) + len(batch) + len(bcast)

    def body(*refs):
        vals = [r[...].astype(F32) for r in refs[:n_in]]
        for r, o in zip(refs[n_in:], fn(*vals)):
            r[...] = o.astype(r.dtype)

    return pl.pallas_call(
        body, grid=(t_rows // tm,), in_specs=_row_specs(tiled, batch, bcast, tm, seq // tm),
        out_specs=[pl.BlockSpec((tm, w), lambda i: (i, 0)) for w, _ in outs],
        out_shape=[SDS((t_rows, w), dt) for w, dt in outs], name=name, compiler_params=_cp(1))(*tiled, *batch, *bcast)


def ew_vjp_call(name, fn, tiled, batch, bcast, cts, want, tm, seq, addend=None):
    t_rows = tiled[0].shape[0]
    tpb = seq // tm
    n_t, n_b, n_c = len(tiled), len(batch), len(bcast)
    n_in = n_t + n_b + n_c
    extra = [] if addend is None else [addend]

    def body(*refs):
        i = pl.program_id(0)
        vals = [r[...].astype(F32) for r in refs[:n_in]]
        ctv = tuple(r[...].astype(F32) for r in refs[n_in:n_in + len(cts)])
        outs = refs[n_in + len(cts) + len(extra):]
        _, vjp = jax.vjp(fn, *vals)
        grads = vjp(ctv)
        for k, (r, (idx, _)) in enumerate(zip(outs[:len(want)], want)):
            g = grads[idx]
            if k == 0 and extra:
                g = g + refs[n_in + len(cts)][...]
            r[...] = g.astype(r.dtype)
        for k in range(n_b):
            r, g = outs[len(want) + k], grads[n_t + k]

            @pl.when(i % tpb == 0)
            def _(r=r, g=g):
                r[...] = g

            @pl.when(i % tpb != 0)
            def _(r=r, g=g):
                r[...] += g
        for k in range(n_c):
            r, g = outs[len(want) + n_b + k], grads[n_t + n_b + k]

            @pl.when(i == 0)
            def _(r=r, g=g):
                r[...] = g

            @pl.when(i != 0)
            def _(r=r, g=g):
                r[...] += g

    out_specs = [pl.BlockSpec((tm, tiled[idx].shape[1]), lambda i: (i, 0)) for idx, _ in want]
    out_specs += [pl.BlockSpec((None,) + a.shape[1:], lambda i: (i // tpb, 0, 0)) for a in batch]
    out_specs += [pl.BlockSpec(a.shape, lambda i, nd=a.ndim: (0,) * nd) for a in bcast]
    out_shape = [SDS(tiled[idx].shape, dt) for idx, dt in want]
    out_shape += [SDS(a.shape, F32) for a in batch] + [SDS(a.shape, F32) for a in bcast]
    res = pl.pallas_call(
        body, grid=(t_rows // tm,),
        in_specs=_row_specs(tiled, batch, bcast, tm, tpb)
        + [pl.BlockSpec((tm, a.shape[1]), lambda i: (i, 0)) for a in list(cts) + extra],
        out_specs=out_specs, out_shape=out_shape, name=name, compiler_params=_cp(1))(*tiled, *batch, *bcast, *cts, *extra)
    return res[:len(want)], res[len(want):len(want) + n_b], res[len(want) + n_b:]


def _pick(n, cands):
    for c in cands:
        if n % c == 0:
            return c
    return n


def mm_tn(name, a, b, exchange=None, gather=None):
    t_rows, m = a.shape
    n = b.shape[1]
    tn = n if n <= 1024 else _pick(n, (1024, 512, 256, 128))
    tm = max([t for t in range(LANES, m + 1, LANES) if m % t == 0 and t * tn * 4 <= ACC_LIMIT] or [m])
    tk = _pick(t_rows, (512, 256, 128, 64))
    grid = (m // tm, n // tn, t_rows // tk)
    extra = [x for x in (exchange, gather) if x is not None]
    ne = len(extra)

    def body(*refs):
        a_ref, b_ref = refs[:2]
        o_ref, acc = refs[2 + ne], refs[3 + 2 * ne]
        i, j, k = pl.program_id(0), pl.program_id(1), pl.program_id(2)
        first = (i == 0) & (j == 0) & (k == 0)
        last = (i == grid[0] - 1) & (j == grid[1] - 1) & (k == grid[2] - 1)
        at_end = []
        for e, x in enumerate(extra):
            comm_refs = (refs[2 + e], refs[3 + ne + e]) + tuple(refs[4 + 2 * ne + 3 * e:7 + 2 * ne + 3 * e])
            if x is exchange:
                start, finish = _exchange_phases(*comm_refs)
                at_end.append(finish)
            else:
                start, forward, finish = _gather_phases(*comm_refs)
                at_end += [forward, finish]
            pl.when(first)(start)

        @pl.when(k == 0)
        def _():
            acc[...] = jnp.zeros_like(acc)

        acc[...] += _dot_tn(a_ref[...], b_ref[...])

        @pl.when(k == grid[2] - 1)
        def _():
            o_ref[...] = acc[...].astype(BF16)

        for phase in at_end:
            pl.when(last)(phase)

    res = pl.pallas_call(
        body, grid=grid,
        in_specs=[pl.BlockSpec((tk, tm), lambda i, j, k: (k, i)), pl.BlockSpec((tk, tn), lambda i, j, k: (k, j))]
        + [HBM_SPEC] * ne,
        out_specs=[pl.BlockSpec((tm, tn), lambda i, j, k: (i, j))] + [HBM_SPEC] * ne,
        out_shape=[SDS((m, n), BF16)] + [SDS(x.shape if x is exchange else (NDEV,) + x.shape, x.dtype) for x in extra],
        scratch_shapes=[pltpu.VMEM((tm, tn), F32)] + _comm_scratch() * ne, name=name,
        compiler_params=_cp(3))(a, b, *extra)
    return res if extra else res[0]


def _ffn_weight_spec():
    if FFN_TF == FF:
        return pl.BlockSpec((FF, D), lambda i, j: (0, 0), pipeline_mode=pl.Buffered(1))
    return pl.BlockSpec((FFN_TF, D), lambda i, j: (j, 0))


def ffn_fwd(name, h, mod3, g, w1, w3, w2, seq, gather=None):
    t_rows = h.shape[0]
    tm = _pick(seq, (FFN_FWD_TM, 128, 64))
    tf = FFN_TF
    tpb = seq // tm
    nf = FF // tf
    nt = t_rows // tm
    extra = [] if gather is None else [gather]

    def body(*refs):
        h_ref, mod_ref, g_ref, w1_ref, w3_ref, w2_ref = refs[:6]
        ho_ref, f_ref, u_ref, h1_ref, h3_ref = refs[6 + len(extra):11 + len(extra)]
        acc = refs[11 + 2 * len(extra)]
        i, j = pl.program_id(0), pl.program_id(1)
        if extra:
            start, forward, finish = _gather_phases(refs[6], refs[12], *refs[14:17])
            pl.when((i == 0) & (j == 0))(start)
            pl.when((i == nt - 1) & (j == 0))(forward)

        @pl.when(j == 0)
        def _():
            u_ref[...] = normmod(h_ref[...], g_ref[...], mod_ref[1:2, :], mod_ref[0:1, :]).astype(BF16)
            acc[...] = jnp.zeros_like(acc)

        u = u_ref[...]
        h1 = _dot_nt(u, w1_ref[...])
        h3 = _dot_nt(u, w3_ref[...])
        h1_ref[...] = h1.astype(BF16)
        h3_ref[...] = h3.astype(BF16)
        acc[...] += _dot(_silu(h1) * h3, w2_ref[...])

        @pl.when(j == nf - 1)
        def _():
            f_ref[...] = acc[...]
            ho_ref[...] = h_ref[...] + 0.5 * mod_ref[2:3, :] * acc[...]

        if extra:
            pl.when((i == nt - 1) & (j == nf - 1))(finish)

    row = lambda i, j: (i, 0)
    return pl.pallas_call(
        body, grid=(nt, nf),
        in_specs=[pl.BlockSpec((tm, D), row), pl.BlockSpec((None, 3, D), lambda i, j: (i // tpb, 0, 0)),
                  pl.BlockSpec((1, D), lambda i, j: (0, 0)), _ffn_weight_spec(), _ffn_weight_spec(), _ffn_weight_spec()]
        + [HBM_SPEC] * len(extra),
        out_specs=[pl.BlockSpec((tm, D), row), pl.BlockSpec((tm, D), row), pl.BlockSpec((tm, D), row),
                   pl.BlockSpec((tm, tf), lambda i, j: (i, j)), pl.BlockSpec((tm, tf), lambda i, j: (i, j))]
        + [HBM_SPEC] * len(extra),
        out_shape=[SDS((t_rows, D), F32), SDS((t_rows, D), F32), SDS((t_rows, D), BF16), SDS((t_rows, FF), BF16),
                   SDS((t_rows, FF), BF16)] + [SDS((NDEV,) + x.shape, x.dtype) for x in extra],
        scratch_shapes=[pltpu.VMEM((tm, D), F32)] + (_comm_scratch() if extra else []), name=name,
        compiler_params=_cp(2))(h, mod3, g, w1, w3, w2, *extra)


def ffn_bwd(name, dho, h, f_out, h1_in, h3_in, mod3, g, w1, w3, w2, seq, exchange=None):
    t_rows = h.shape[0]
    tm = _pick(seq, (FFN_BWD_TM, 128, 64))
    tf = FFN_TF
    tpb = seq // tm
    nf = FF // tf
    nt = t_rows // tm
    extra = [] if exchange is None else [exchange]

    def body(*refs):
        dho_ref, h_ref, f_ref, h1_ref, h3_ref, mod_ref, g_ref, w1_ref, w3_ref, w2_ref = refs[:10]
        dh_ref, a_ref, dh1_ref, dh3_ref, df_scr, dmod_ref, dg_ref = refs[10 + len(extra):17 + len(extra)]
        du_acc = refs[17 + 2 * len(extra)]
        i, j = pl.program_id(0), pl.program_id(1)
        if extra:
            start, finish = _exchange_phases(refs[10], refs[18], *refs[20:23])
            pl.when((i == 0) & (j == 0))(start)

        @pl.when(j == 0)
        def _():
            df_scr[...] = (0.5 * mod_ref[2:3, :] * dho_ref[...]).astype(BF16)
            du_acc[...] = jnp.zeros_like(du_acc)

        h1 = h1_ref[...].astype(F32)
        h3 = h3_ref[...].astype(F32)
        sg = jax.nn.sigmoid(h1)
        s = h1 * sg
        da = _dot_nt(df_scr[...], w2_ref[...])
        dh3 = (da * s).astype(BF16)
        dh1 = (da * h3 * (sg * (1.0 + h1 * (1.0 - sg)))).astype(BF16)
        a_ref[...] = (s * h3).astype(BF16)
        dh1_ref[...] = dh1
        dh3_ref[...] = dh3
        du_acc[...] += _dot(dh1, w1_ref[...]) + _dot(dh3, w3_ref[...])

        @pl.when(j == nf - 1)
        def _():
            _, vjp = jax.vjp(normmod, h_ref[...], g_ref[...], mod_ref[1:2, :], mod_ref[0:1, :])
            dh_n, dg, dsc, dsh = vjp(du_acc[...])
            dh_ref[...] = dho_ref[...] + dh_n
            dgt = jnp.sum(0.5 * dho_ref[...] * f_ref[...], axis=0, keepdims=True)
            dmod = jnp.concatenate([dsh, dsc, dgt], axis=0)

            @pl.when(i % tpb == 0)
            def _():
                dmod_ref[...] = dmod

            @pl.when(i % tpb != 0)
            def _():
                dmod_ref[...] += dmod

            @pl.when(i == 0)
            def _():
                dg_ref[...] = dg

            @pl.when(i != 0)
            def _():
                dg_ref[...] += dg

        if extra:
            pl.when((i == nt - 1) & (j == nf - 1))(finish)

    row = lambda i, j: (i, 0)
    col = lambda i, j: (i, j)
    return pl.pallas_call(
        body, grid=(nt, nf),
        in_specs=[pl.BlockSpec((tm, D), row), pl.BlockSpec((tm, D), row), pl.BlockSpec((tm, D), row),
                  pl.BlockSpec((tm, tf), col), pl.BlockSpec((tm, tf), col),
                  pl.BlockSpec((None, 3, D), lambda i, j: (i // tpb, 0, 0)),
                  pl.BlockSpec((1, D), lambda i, j: (0, 0)), _ffn_weight_spec(), _ffn_weight_spec(), _ffn_weight_spec()]
        + [HBM_SPEC] * len(extra),
        out_specs=[pl.BlockSpec((tm, D), row), pl.BlockSpec((tm, tf), col), pl.BlockSpec((tm, tf), col),
                   pl.BlockSpec((tm, tf), col), pl.BlockSpec((tm, D), row),
                   pl.BlockSpec((None, 3, D), lambda i, j: (i // tpb, 0, 0)), pl.BlockSpec((1, D), lambda i, j: (0, 0))]
        + [HBM_SPEC] * len(extra),
        out_shape=[SDS((t_rows, D), F32), SDS((t_rows, FF), BF16), SDS((t_rows, FF), BF16), SDS((t_rows, FF), BF16),
                   SDS((t_rows, D), BF16), SDS(mod3.shape, F32), SDS((1, D), F32)] + [SDS(x.shape, x.dtype) for x in extra],
        scratch_shapes=[pltpu.VMEM((tm, D), F32)] + (_comm_scratch() if extra else []), name=name,
        compiler_params=_cp(2))(dho, h, f_out, h1_in, h3_in, mod3, g, w1, w3, w2, *extra)


def _resident(shape):
    return pl.BlockSpec(shape, lambda i: (0,) * len(shape), pipeline_mode=pl.Buffered(1))


def mix_in_fwd(h, sh, sc, g, ws, seq):
    t_rows = h.shape[0]
    tm = _pick(seq, (256, 128, 64))
    tpb = seq // tm
    nw = len(ws)

    def body(h_ref, sh_ref, sc_ref, g_ref, *rest):
        u = normmod(h_ref[...], g_ref[...], sc_ref[...], sh_ref[...]).astype(BF16)
        rest[nw][...] = u
        for w_ref, p_ref in zip(rest[:nw], rest[nw + 1:]):
            p_ref[...] = _dot_nt(u, w_ref[...])

    row = lambda i: (i, 0)
    batch = pl.BlockSpec((None, 1, D), lambda i: (i // tpb, 0, 0))
    return pl.pallas_call(
        body, grid=(t_rows // tm,),
        in_specs=[pl.BlockSpec((tm, D), row), batch, batch, pl.BlockSpec((1, D), lambda i: (0, 0))]
        + [_resident(w.shape) for w in ws],
        out_specs=[pl.BlockSpec((tm, D), row)] + [pl.BlockSpec((tm, w.shape[0]), row) for w in ws],
        out_shape=[SDS((t_rows, D), BF16)] + [SDS((t_rows, w.shape[0]), F32) for w in ws], name="mix_in_fwd",
        compiler_params=_cp(1))(h, sh, sc, g, *ws)


def mix_in_bwd(dps, ws, h, sh, sc, g, dh_add, seq):
    t_rows = h.shape[0]
    tm = _pick(seq, (256, 128, 64))
    tpb = seq // tm
    nw = len(ws)

    def body(*refs):
        h_ref, sh_ref, sc_ref, g_ref, add_ref, dh_ref, dsh_ref, dsc_ref, dg_ref = refs[2 * nw:]
        i = pl.program_id(0)
        du = _dot(refs[0][...], refs[nw][...])
        for k in range(1, nw):
            du = du + _dot(refs[k][...], refs[nw + k][...])
        _, vjp = jax.vjp(normmod, h_ref[...], g_ref[...], sc_ref[...], sh_ref[...])
        dh_n, dg, dsc, dsh = vjp(du)
        dh_ref[...] = add_ref[...] + dh_n

        @pl.when(i % tpb == 0)
        def _():
            dsh_ref[...] = dsh
            dsc_ref[...] = dsc

        @pl.when(i % tpb != 0)
        def _():
            dsh_ref[...] += dsh
            dsc_ref[...] += dsc

        @pl.when(i == 0)
        def _():
            dg_ref[...] = dg

        @pl.when(i != 0)
        def _():
            dg_ref[...] += dg

    row = lambda i: (i, 0)
    batch = pl.BlockSpec((None, 1, D), lambda i: (i // tpb, 0, 0))
    gain = pl.BlockSpec((1, D), lambda i: (0, 0))
    return pl.pallas_call(
        body, grid=(t_rows // tm,),
        in_specs=[pl.BlockSpec((tm, dp.shape[1]), row) for dp in dps] + [_resident(w.shape) for w in ws]
        + [pl.BlockSpec((tm, D), row), batch, batch, gain, pl.BlockSpec((tm, D), row)],
        out_specs=[pl.BlockSpec((tm, D), row), batch, batch, gain],
        out_shape=[SDS((t_rows, D), F32), SDS(sh.shape, F32), SDS(sc.shape, F32), SDS((1, D), F32)], name="mix_in_bwd",
        compiler_params=_cp(1))(*dps, *ws, h, sh, sc, g, dh_add)


def mix_out_fwd(merged, w_out, h_prev, gt, seq):
    t_rows = merged.shape[0]
    tm = _pick(seq, (256, 128, 64))
    tpb = seq // tm

    def body(m_ref, w_ref, h_ref, gt_ref, mo_ref, ho_ref):
        mo = _dot(m_ref[...], w_ref[...])
        mo_ref[...] = mo
        ho_ref[...] = h_ref[...] + gt_ref[...] * mo

    row = lambda i: (i, 0)
    return pl.pallas_call(
        body, grid=(t_rows // tm,),
        in_specs=[pl.BlockSpec((tm, D), row), _resident(w_out.shape), pl.BlockSpec((tm, D), row),
                  pl.BlockSpec((None, 1, D), lambda i: (i // tpb, 0, 0))],
        out_specs=[pl.BlockSpec((tm, D), row), pl.BlockSpec((tm, D), row)],
        out_shape=[SDS((t_rows, D), F32), SDS((t_rows, D), F32)], name="mix_out_fwd",
        compiler_params=_cp(1))(merged, w_out, h_prev, gt)


def mix_out_bwd(dh, mo, w_out, gt, seq):
    t_rows = dh.shape[0]
    tm = _pick(seq, (256, 128, 64))
    tpb = seq // tm

    def body(dh_ref, mo_ref, w_ref, gt_ref, dmo_ref, dm_ref, dgt_ref):
        i = pl.program_id(0)
        dmo = (gt_ref[...] * dh_ref[...]).astype(BF16)
        dmo_ref[...] = dmo
        dm_ref[...] = _dot_nt(dmo, w_ref[...])
        dgt = jnp.sum(dh_ref[...] * mo_ref[...], axis=0, keepdims=True)

        @pl.when(i % tpb == 0)
        def _():
            dgt_ref[...] = dgt

        @pl.when(i % tpb != 0)
        def _():
            dgt_ref[...] += dgt

    row = lambda i: (i, 0)
    batch = pl.BlockSpec((None, 1, D), lambda i: (i // tpb, 0, 0))
    return pl.pallas_call(
        body, grid=(t_rows // tm,),
        in_specs=[pl.BlockSpec((tm, D), row), pl.BlockSpec((tm, D), row), _resident(w_out.shape), batch],
        out_specs=[pl.BlockSpec((tm, D), row), pl.BlockSpec((tm, D), row), batch],
        out_shape=[SDS((t_rows, D), BF16), SDS((t_rows, D), F32), SDS(gt.shape, F32)], name="mix_out_bwd",
        compiler_params=_cp(1))(dh, mo, w_out, gt)


def _dn_cols(part, hd):
    return slice(part * DNW + hd * DH, part * DNW + (hd + 1) * DH)


def _qkv_stacks(qkv_ref, nb):
    pairs = [(b, hd) for b in range(nb) for hd in range(NH)]
    return [jnp.stack([qkv_ref[b, :, _dn_cols(part, hd)] for b, hd in pairs]) for part in range(3)]


def dn_prep_fwd(p_dn, conv8):
    bl, seq, _ = p_dn.shape
    tp = _pick(seq, (256, 128, 64))

    def body(raw_ref, halo_ref, conv_ref, o_ref):
        hm = (pl.program_id(1) > 0).astype(F32)
        o_ref[...] = dn_prep(jnp.concatenate([halo_ref[...] * hm, raw_ref[...]], axis=0), conv_ref[...])

    return pl.pallas_call(
        body, grid=(bl, seq // tp),
        in_specs=[pl.BlockSpec((None, tp, 3 * DNW), lambda b, i: (b, i, 0)),
                  pl.BlockSpec((None, 8, 3 * DNW), lambda b, i: (b, jnp.maximum(i * (tp // 8) - 1, 0), 0)),
                  pl.BlockSpec((8, 3 * DNW), lambda b, i: (0, 0))],
        out_specs=pl.BlockSpec((None, tp, 3 * DNW), lambda b, i: (b, i, 0)),
        out_shape=SDS((bl, seq, 3 * DNW), F32), name="dn_prep_fwd", compiler_params=_cp(2))(p_dn, p_dn, conv8)


def dn_prep_bwd(p_dn, conv8, d_qkv, d_z):
    bl, seq, _ = p_dn.shape
    tp = _pick(seq, (256, 128, 64))
    nt = seq // tp

    def body(raw_ref, halo_ref, conv_ref, dq_ref, dz_ref, draw_ref, dconv_ref, carry):
        b, r = pl.program_id(0), pl.program_id(1)

        @pl.when((b == 0) & (r == 0))
        def _():
            dconv_ref[...] = jnp.zeros_like(dconv_ref)

        @pl.when(r == 0)
        def _():
            carry[...] = jnp.zeros_like(carry)

        hm = (r < nt - 1).astype(F32)
        _, vjp = jax.vjp(dn_prep, jnp.concatenate([halo_ref[...] * hm, raw_ref[...]], axis=0), conv_ref[...])
        dxc, dw = vjp(dq_ref[...])
        tail = dxc[tp:tp + 8] + carry[...]
        draw_ref[:, 0:3 * DNW] = jnp.concatenate([dxc[8:tp], tail], axis=0).astype(BF16)
        draw_ref[:, 3 * DNW:4 * DNW] = dz_ref[...].astype(BF16)
        carry[...] = dxc[0:8] * hm
        dconv_ref[...] += dw

    blk = lambda b, r: (b, nt - 1 - r, 0)
    return pl.pallas_call(
        body, grid=(bl, nt),
        in_specs=[pl.BlockSpec((None, tp, 3 * DNW), blk),
                  pl.BlockSpec((None, 8, 3 * DNW), lambda b, r: (b, jnp.maximum((nt - 1 - r) * (tp // 8) - 1, 0), 0)),
                  pl.BlockSpec((8, 3 * DNW), lambda b, r: (0, 0)), pl.BlockSpec((None, tp, 3 * DNW), blk),
                  pl.BlockSpec((None, tp, DNW), blk)],
        out_specs=[pl.BlockSpec((None, tp, 4 * DNW), blk), pl.BlockSpec((8, 3 * DNW), lambda b, r: (0, 0))],
        out_shape=[SDS((bl, seq, 4 * DNW), BF16), SDS((8, 3 * DNW), F32)],
        scratch_shapes=[pltpu.VMEM((8, 3 * DNW), F32)], name="dn_prep_bwd", compiler_params=_cp(2))(p_dn, p_dn, conv8, d_qkv, d_z)


def _gate_stacks(gates, nb):
    pairs = [(b, hd) for b in range(nb) for hd in range(NH)]
    bs = jnp.stack([gates[b][0][:, hd:hd + 1] for b, hd in pairs])
    gs = jnp.stack([gates[b][1][:, NH + hd:NH + hd + 1] for b, hd in pairs])
    gts = jnp.stack([gates[b][2][NH + hd:NH + hd + 1, :] for b, hd in pairs])
    return bs, gs, gts


def deltanet_fwd(qkv, p_small, alp, dtp, nb, gather=None):
    bl, seq, _ = qkv.shape
    nc = seq // CH
    ng = nb * NH
    extra = [] if gather is None else [gather]

    def body(*refs):
        qkv_ref, small_ref, alp_ref, dtp_ref = refs[:4]
        o_ref, sprev_ref, tinv_ref = refs[4 + len(extra):7 + len(extra)]
        s_scr = refs[7 + 2 * len(extra)]
        bb, n = pl.program_id(0), pl.program_id(1)
        if extra:
            start, forward, finish = _gather_phases(refs[4], refs[8], *refs[10:13])
            pl.when((bb == 0) & (n == 0))(start)

        @pl.when(n == 0)
        def _():
            s_scr[...] = jnp.zeros_like(s_scr)

        gates = [gate_fn(small_ref[b], alp_ref[...], dtp_ref[...]) for b in range(nb)]
        s_prev = s_scr[...]
        o, s_new, tinv = dn_chunk(*_qkv_stacks(qkv_ref, nb), *_gate_stacks(gates, nb), s_prev)
        sprev_ref[...] = s_prev
        tinv_ref[...] = tinv
        s_scr[...] = s_new
        for b in range(nb):
            for hd in range(NH):
                o_ref[b, :, hd * DH:(hd + 1) * DH] = o[b * NH + hd]
        if extra:
            at_end = (bb == bl // nb - 1) & (n == nc - 1)
            pl.when(at_end)(forward)
            pl.when(at_end)(finish)

    blk = lambda bb, n: (bb, n, 0)
    const = lambda bb, n: (0, 0)
    saved = pl.BlockSpec((None, ng, DH, DH), lambda bb, n: (bb * nc + n, 0, 0, 0))
    return pl.pallas_call(
        body, grid=(bl // nb, nc),
        in_specs=[pl.BlockSpec((nb, CH, 3 * DNW), blk), pl.BlockSpec((nb, CH, LANES), blk),
                  pl.BlockSpec((1, LANES), const), pl.BlockSpec((1, LANES), const)] + [HBM_SPEC] * len(extra),
        out_specs=[pl.BlockSpec((nb, CH, DNW), blk), saved, saved] + [HBM_SPEC] * len(extra),
        out_shape=[SDS((bl, seq, DNW), F32), SDS((bl // nb * nc, ng, DH, DH), F32), SDS((bl // nb * nc, ng, DH, DH), F32)]
        + [SDS((NDEV,) + x.shape, x.dtype) for x in extra],
        scratch_shapes=[pltpu.VMEM((ng, DH, DH), F32)] + (_comm_scratch() if extra else []), name="deltanet_fwd",
        compiler_params=_cp(2))(qkv, p_small, alp, dtp, *extra)


def deltanet_bwd(qkv, p_small, alp, dtp, sprev, tinv, d_o, nb, exchange=None):
    bl, seq, _ = qkv.shape
    nc = seq // CH
    ng = nb * NH
    extra = [] if exchange is None else [exchange]

    def body(*refs):
        qkv_ref, small_ref, alp_ref, dtp_ref, sprev_ref, tinv_ref, do_ref = refs[:7]
        dqkv_ref, dsmall_ref, dalp_ref, ddtp_ref = refs[7 + len(extra):11 + len(extra)]
        ds_scr = refs[11 + 2 * len(extra)]
        bb, r = pl.program_id(0), pl.program_id(1)
        if extra:
            start, finish = _exchange_phases(refs[7], refs[12], *refs[14:17])
            pl.when((bb == 0) & (r == 0))(start)

        @pl.when((bb == 0) & (r == 0))
        def _():
            dalp_ref[...] = jnp.zeros_like(dalp_ref)
            ddtp_ref[...] = jnp.zeros_like(ddtp_ref)

        @pl.when(r == 0)
        def _():
            ds_scr[...] = jnp.zeros_like(ds_scr)

        gates, gate_vjps = [], []
        for b in range(nb):
            out, gvjp = jax.vjp(gate_fn, small_ref[b], alp_ref[...], dtp_ref[...])
            gates.append(out)
            gate_vjps.append(gvjp)
        t_saved = tinv_ref[...]
        _, vjp = jax.vjp(lambda *args: dn_chunk(*args, t_saved)[:2], *_qkv_stacks(qkv_ref, nb), *_gate_stacks(gates, nb),
                         sprev_ref[...])
        d_out = jnp.stack([do_ref[b, :, hd * DH:(hd + 1) * DH] for b in range(nb) for hd in range(NH)])
        grads = vjp((d_out, ds_scr[...]))
        ds_scr[...] = grads[6]
        lane = _iota2((CH, LANES), 1)
        rowi = _iota2((LANES, CH), 0)
        for b in range(nb):
            d_beta = jnp.zeros((CH, LANES), F32)
            d_gc = jnp.zeros((CH, LANES), F32)
            d_gct = jnp.zeros((LANES, CH), F32)
            for hd in range(NH):
                i = b * NH + hd
                for part in range(3):
                    dqkv_ref[b, :, _dn_cols(part, hd)] = grads[part][i]
                d_beta = d_beta + jnp.where(lane == hd, grads[3][i], 0.0)
                d_gc = d_gc + jnp.where(lane == NH + hd, grads[4][i], 0.0)
                d_gct = d_gct + jnp.where(rowi == NH + hd, grads[5][i], 0.0)
            d_small, d_alp, d_dtp = gate_vjps[b]((d_beta, d_gc, d_gct))
            dsmall_ref[b] = d_small.astype(BF16)
            dalp_ref[...] += d_alp
            ddtp_ref[...] += d_dtp
        if extra:
            pl.when((bb == bl // nb - 1) & (r == nc - 1))(finish)

    blk = lambda bb, r: (bb, nc - 1 - r, 0)
    const = lambda bb, r: (0, 0)
    saved = pl.BlockSpec((None, ng, DH, DH), lambda bb, r: (bb * nc + nc - 1 - r, 0, 0, 0))
    return pl.pallas_call(
        body, grid=(bl // nb, nc),
        in_specs=[pl.BlockSpec((nb, CH, 3 * DNW), blk), pl.BlockSpec((nb, CH, LANES), blk), pl.BlockSpec((1, LANES), const),
                  pl.BlockSpec((1, LANES), const), saved, saved, pl.BlockSpec((nb, CH, DNW), blk)] + [HBM_SPEC] * len(extra),
        out_specs=[pl.BlockSpec((nb, CH, 3 * DNW), blk), pl.BlockSpec((nb, CH, LANES), blk), pl.BlockSpec((1, LANES), const),
                   pl.BlockSpec((1, LANES), const)] + [HBM_SPEC] * len(extra),
        out_shape=[SDS((bl, seq, 3 * DNW), F32), SDS((bl, seq, LANES), BF16), SDS((1, LANES), F32), SDS((1, LANES), F32)]
        + [SDS(x.shape, x.dtype) for x in extra],
        scratch_shapes=[pltpu.VMEM((ng, DH, DH), F32)] + (_comm_scratch() if extra else []), name="deltanet_bwd",
        compiler_params=_cp(2))(qkv, p_small, alp, dtp, sprev, tinv, d_o, *extra)


def _s5_table_specs():
    tab3 = pl.BlockSpec((None, LANES, 512), lambda gb, n: (gb, 0, 0))
    tab2 = pl.BlockSpec((S5_CH, 512), lambda gb, n: (0, gb))
    return [tab3] * 4 + [tab2] * 6 + [pl.BlockSpec((1, LANES), lambda gb, n: (0, gb))]


def s5_fwd(u, tables, dsk):
    bl, seq, _ = u.shape
    nc = seq // S5_CH

    def body(u_ref, *rest):
        tabs, (y_ref, xs_ref, xr_scr, xi_scr) = rest[:11], rest[11:]

        @pl.when(pl.program_id(1) == 0)
        def _():
            xr_scr[...] = jnp.zeros_like(xr_scr)
            xi_scr[...] = jnp.zeros_like(xi_scr)

        xp_re, xp_im = xr_scr[...], xi_scr[...]
        xs_ref[0:bl] = xp_re
        xs_ref[bl:2 * bl] = xp_im
        y, xn_re, xn_im = s5_chunk(u_ref[...], xp_re, xp_im, *[t[...] for t in tabs])
        y_ref[...] = y
        xr_scr[...] = xn_re
        xi_scr[...] = xn_im

    blk = lambda gb, n: (0, n, gb)
    return pl.pallas_call(
        body, grid=(GB, nc), in_specs=[pl.BlockSpec((bl, S5_CH, LANES), blk)] + _s5_table_specs(),
        out_specs=[pl.BlockSpec((bl, S5_CH, LANES), blk),
                   pl.BlockSpec((None, 2 * bl, 1, 512), lambda gb, n: (gb * nc + n, 0, 0, 0))],
        out_shape=[SDS((bl, seq, S5W), F32), SDS((GB * nc, 2 * bl, 1, 512), F32)],
        scratch_shapes=[pltpu.VMEM((bl, 1, 512), F32), pltpu.VMEM((bl, 1, 512), F32)], name="s5_fwd",
        compiler_params=_cp(2))(u, *tables, dsk)


def s5_bwd(u, tables, dsk, xs, dy):
    bl, seq, _ = u.shape
    nc = seq // S5_CH

    def body(u_ref, *rest):
        tabs, xs_ref, dy_ref = rest[:11], rest[11], rest[12]
        du_ref, dtabs, dxr_scr, dxi_scr = rest[13], rest[14:25], rest[25], rest[26]
        r = pl.program_id(1)

        @pl.when(r == 0)
        def _():
            for t in dtabs:
                t[...] = jnp.zeros_like(t)
            dxr_scr[...] = jnp.zeros_like(dxr_scr)
            dxi_scr[...] = jnp.zeros_like(dxi_scr)

        _, vjp = jax.vjp(s5_chunk, u_ref[...], xs_ref[0:bl], xs_ref[bl:2 * bl], *[t[...] for t in tabs])
        grads = vjp((dy_ref[...], dxr_scr[...], dxi_scr[...]))
        du_ref[...] = grads[0].astype(BF16)
        dxr_scr[...] = grads[1]
        dxi_scr[...] = grads[2]
        for t, g in zip(dtabs, grads[3:]):
            t[...] += g

    blk = lambda gb, r: (0, nc - 1 - r, gb)
    tab_shapes = [SDS(t.shape, F32) for t in tables] + [SDS(dsk.shape, F32)]
    return pl.pallas_call(
        body, grid=(GB, nc),
        in_specs=[pl.BlockSpec((bl, S5_CH, LANES), blk)] + _s5_table_specs()
        + [pl.BlockSpec((None, 2 * bl, 1, 512), lambda gb, r: (gb * nc + nc - 1 - r, 0, 0, 0)), pl.BlockSpec((bl, S5_CH, LANES), blk)],
        out_specs=[pl.BlockSpec((bl, S5_CH, LANES), blk)] + _s5_table_specs(),
        out_shape=[SDS((bl, seq, S5W), BF16)] + tab_shapes,
        scratch_shapes=[pltpu.VMEM((bl, 1, 512), F32), pltpu.VMEM((bl, 1, 512), F32)], name="s5_bwd",
        compiler_params=_cp(2))(u, *tables, dsk, xs, dy)


def s5_tables_fwd(params):
    shapes = [SDS((GB, LANES, 512), F32)] * 4 + [SDS((S5_CH, S5N), F32)] * 6

    def body(*refs):
        for r, t in zip(refs[7:], s5_tables(*[p[...] for p in refs[:7]])):
            r[...] = t

    return pl.pallas_call(body, out_shape=shapes, name="s5_tables_fwd", compiler_params=_cp())(*params)


def s5_tables_bwd(params, dtables):
    def body(*refs):
        _, vjp = jax.vjp(s5_tables, *[p[...] for p in refs[:7]])
        for r, g in zip(refs[17:], vjp(tuple(t[...] for t in refs[7:17]))):
            r[...] = g

    return pl.pallas_call(body, out_shape=[SDS(p.shape, F32) for p in params], name="s5_tables_bwd",
                          compiler_params=_cp())(*params, *dtables)


def ada_fwd(c_all, w_loc, b_loc):
    def body(c_ref, w_ref, b_ref, o_ref):
        o_ref[...] = _dot(_silu(c_ref[...]), w_ref[...]) + b_ref[...]

    return pl.pallas_call(body, out_shape=SDS((c_all.shape[0], w_loc.shape[1]), F32), name="ada_fwd",
                          compiler_params=_cp())(c_all, w_loc, b_loc)


def ada_bwd(c_all, dmod_mine, dmod_all):
    def body(c_ref, dm_ref, da_ref, gw_ref, gb_ref):
        gw_ref[...] = _dot_tn(_silu(c_ref[...]), dm_ref[...])
        gb_ref[...] = jnp.sum(da_ref[...], axis=0, keepdims=True)

    return pl.pallas_call(body, out_shape=[SDS((D, dmod_mine.shape[1]), F32), SDS((1, dmod_all.shape[1]), F32)],
                          name="ada_bwd", compiler_params=_cp())(c_all, dmod_mine, dmod_all)


def loss_head(h, tgt, g, seq):
    t_rows = h.shape[0]
    tm = _pick(seq, (256, 128, 64))

    def body(h_ref, t_ref, g_ref, dh_ref, dg_ref, loss_ref):
        i = pl.program_id(0)
        y, vjp = jax.vjp(lambda hh, gg: hh * lax.rsqrt(jnp.mean(hh * hh, axis=-1, keepdims=True) + EPS) * gg,
                         h_ref[...], g_ref[...])
        e = y - t_ref[...]
        dh, dg = vjp(e * (1.0 / D))
        part = jnp.sum(jnp.sum(e * e, axis=1, keepdims=True), axis=0, keepdims=True) * (0.5 / D) + jnp.zeros((1, LANES), F32)
        dh_ref[...] = dh

        @pl.when(i == 0)
        def _():
            dg_ref[...] = dg
            loss_ref[...] = part

        @pl.when(i != 0)
        def _():
            dg_ref[...] += dg
            loss_ref[...] += part

    row = lambda i: (i, 0)
    const = lambda i: (0, 0)
    return pl.pallas_call(
        body, grid=(t_rows // tm,),
        in_specs=[pl.BlockSpec((tm, D), row), pl.BlockSpec((tm, D), row), pl.BlockSpec((1, D), const)],
        out_specs=[pl.BlockSpec((tm, D), row), pl.BlockSpec((1, D), const), pl.BlockSpec((1, LANES), const)],
        out_shape=[SDS((t_rows, D), F32), SDS((1, D), F32), SDS((1, LANES), F32)], name="loss_head",
        compiler_params=_cp(1))(h, tgt, g)


def adamw(name, parts, w, m, v):
    k_parts, rows, cols = parts.shape
    tr = _pick(rows, (256, 128, 64, 32, 16, 8))

    def body(p_ref, w_ref, m_ref, v_ref, g_ref, d_ref, mo_ref, vo_ref):
        g = p_ref[0].astype(F32)
        for k in range(1, k_parts):
            g = g + p_ref[k].astype(F32)
        _adam_store(g, w_ref, m_ref, v_ref, g_ref, d_ref, mo_ref, vo_ref)

    blk = pl.BlockSpec((tr, cols), lambda i: (i, 0))
    return pl.pallas_call(
        body, grid=(rows // tr,), in_specs=[pl.BlockSpec((k_parts, tr, cols), lambda i: (0, i, 0)), blk, blk, blk],
        out_specs=[blk] * 4, out_shape=[SDS((rows, cols), F32)] * 4, name=name, compiler_params=_cp(1))(parts, w, m, v)


def _adam_store(g, w_ref, m_ref, v_ref, g_ref, d_ref, mo_ref, vo_ref):
    m_new = ADAM_B1 * m_ref[...] + (1.0 - ADAM_B1) * g
    v_new = ADAM_B2 * v_ref[...] + (1.0 - ADAM_B2) * (g * g)
    m_hat = m_new / (1.0 - ADAM_B1 ** ADAM_STEP)
    v_hat = v_new / (1.0 - ADAM_B2 ** ADAM_STEP)
    g_ref[...] = g
    d_ref[...] = -ADAM_LR * (m_hat / (jnp.sqrt(v_hat) + ADAM_EPS) + ADAM_WD * w_ref[...])
    mo_ref[...] = m_new
    vo_ref[...] = v_new


def adamw_t(name, parts, w, m, v):
    k_parts, r, c = parts.shape
    tc = _pick(c, (256, 128))

    def body(p_ref, w_ref, m_ref, v_ref, g_ref, d_ref, mo_ref, vo_ref):
        gt = p_ref[0].astype(F32)
        for k in range(1, k_parts):
            gt = gt + p_ref[k].astype(F32)
        _adam_store(gt.T, w_ref, m_ref, v_ref, g_ref, d_ref, mo_ref, vo_ref)

    blk = pl.BlockSpec((tc, r), lambda j: (j, 0))
    return pl.pallas_call(
        body, grid=(c // tc,), in_specs=[pl.BlockSpec((k_parts, r, tc), lambda j: (0, 0, j)), blk, blk, blk],
        out_specs=[blk] * 4, out_shape=[SDS((c, r), F32)] * 4, name=name, compiler_params=_cp(1))(parts, w, m, v)


def _comm_scratch():
    return [pltpu.SemaphoreType.DMA((7,)), pltpu.SemaphoreType.DMA((7,)), pltpu.SemaphoreType.DMA]


HBM_SPEC = pl.BlockSpec(memory_space=pl.ANY)


def _gather_phases(x_ref, out_ref, send_sems, recv_sems, local_sem):
    mx, my, mc = lax.axis_index("x"), lax.axis_index("y"), lax.axis_index("c")
    me, sibling = (mx, my, mc), (mx, my, 1 - mc)
    chips = [(1 - mx, my), (mx, 1 - my), (1 - mx, 1 - my)]

    def slot(px, py, pc):
        return out_ref.at[4 * px + 2 * py + pc]

    def copy(k, block, to, src=None):
        return pltpu.make_async_remote_copy(
            src_ref=slot(*block) if src is None else src, dst_ref=slot(*block), send_sem=send_sems.at[k],
            recv_sem=recv_sems.at[k], device_id=to, device_id_type=pl.DeviceIdType.MESH)

    def first():
        return [copy(0, me, sibling, src=x_ref)] + [copy(1 + j, me, (*chip, mc), src=x_ref) for j, chip in enumerate(chips)]

    def passed():
        return [copy(4 + j, (*chip, mc), sibling) for j, chip in enumerate(chips)]

    def start():
        pltpu.make_async_copy(x_ref, slot(*me), local_sem).start()
        for cp in first():
            cp.start()

    def forward():
        for j, chip in enumerate(chips):
            copy(1 + j, (*chip, mc), me).wait_recv()
            passed()[j].start()

    def finish():
        copy(0, sibling, me).wait_recv()
        for j, chip in enumerate(chips):
            copy(4 + j, (*chip, 1 - mc), me).wait_recv()
        for cp in first() + passed():
            cp.wait_send()
        pltpu.make_async_copy(x_ref, slot(*me), local_sem).wait()

    return start, forward, finish


def _exchange_phases(x_ref, out_ref, send_sems, recv_sems, local_sem):
    mx, my, mc = lax.axis_index("x"), lax.axis_index("y"), lax.axis_index("c")
    me = 4 * mx + 2 * my + mc

    def peer(k):
        return mx ^ (k >> 2), my ^ ((k >> 1) & 1), mc ^ (k & 1)

    def sends():
        out = []
        for k in range(1, NDEV):
            px, py, pc = peer(k)
            out.append(pltpu.make_async_remote_copy(
                src_ref=x_ref.at[4 * px + 2 * py + pc], dst_ref=out_ref.at[me], send_sem=send_sems.at[k - 1],
                recv_sem=recv_sems.at[k - 1], device_id=(px, py, pc), device_id_type=pl.DeviceIdType.MESH))
        return out

    def start():
        pltpu.make_async_copy(x_ref.at[me], out_ref.at[me], local_sem).start()
        for cp in sends():
            cp.start()

    def finish():
        for k in range(1, NDEV):
            px, py, pc = peer(k)
            pltpu.make_async_remote_copy(
                src_ref=x_ref.at[me], dst_ref=out_ref.at[4 * px + 2 * py + pc], send_sem=send_sems.at[k - 1],
                recv_sem=recv_sems.at[k - 1], device_id=(px, py, pc), device_id_type=pl.DeviceIdType.MESH).wait_recv()
        for cp in sends():
            cp.wait_send()
        pltpu.make_async_copy(x_ref.at[me], out_ref.at[me], local_sem).wait()

    return start, finish


def all_gather(name, x):
    def body(x_ref, out_ref, send_sems, recv_sems, local_sem):
        for phase in _gather_phases(x_ref, out_ref, send_sems, recv_sems, local_sem):
            phase()

    return pl.pallas_call(body, out_shape=SDS((NDEV,) + x.shape, x.dtype), in_specs=[HBM_SPEC], out_specs=HBM_SPEC,
                          scratch_shapes=_comm_scratch(), name=name)(x)


def all_gather_pair(name, x1, x2):
    def body(x1_ref, x2_ref, o1_ref, o2_ref, *sems):
        first = _gather_phases(x1_ref, o1_ref, *sems[:3])
        second = _gather_phases(x2_ref, o2_ref, *sems[3:])
        for phase1, phase2 in zip(first, second):
            phase1()
            phase2()

    return pl.pallas_call(
        body, out_shape=[SDS((NDEV,) + x1.shape, x1.dtype), SDS((NDEV,) + x2.shape, x2.dtype)], in_specs=[HBM_SPEC] * 2,
        out_specs=[HBM_SPEC] * 2, scratch_shapes=_comm_scratch() + _comm_scratch(), name=name)(x1, x2)


def all_to_all(name, x):
    def body(x_ref, out_ref, send_sems, recv_sems, local_sem):
        for phase in _exchange_phases(x_ref, out_ref, send_sems, recv_sems, local_sem):
            phase()

    return pl.pallas_call(body, out_shape=SDS(x.shape, x.dtype), in_specs=[HBM_SPEC], out_specs=HBM_SPEC,
                          scratch_shapes=_comm_scratch(), name=name)(x)


def _pack(arrs, dtype, row_mult=8):
    segs = []
    for a in arrs:
        flat = a.reshape(-1).astype(dtype)
        segs.append(jnp.pad(flat, (0, (-flat.shape[0]) % ROW)))
    flat = jnp.concatenate(segs)
    flat = jnp.pad(flat, (0, (-flat.shape[0]) % (ROW * row_mult)))
    return flat.reshape(-1, ROW)


def _unpack(buf, shapes):
    flat = buf.reshape(-1)
    out, off = [], 0
    for s in shapes:
        n = math.prod(s)
        out.append(flat[off:off + n].reshape(s))
        off += n + (-n) % ROW
    return out


def _pack_rows(arrs, axis):
    padded = []
    for t in arrs:
        pad = [(0, 0)] * t.ndim
        pad[axis] = (0, _tile_rows(t.shape[axis]) - t.shape[axis])
        padded.append(jnp.pad(t, pad))
    return jnp.concatenate(padded, axis=axis)


def _tile_rows(r):
    return r + (-r) % BF16_TILE_ROWS


def _unpack8(buf, shapes):
    flat = buf.reshape(NDEV, -1)
    out, off = [], 0
    for s in shapes:
        n = math.prod(s)
        out.append(flat[:, off:off + n].reshape((NDEV,) + tuple(s)))
        off += n + (-n) % ROW
    return out


def kernel(x, c, w_ada, b_ada, g_ffn1, w1_ffn1, w3_ffn1, w2_ffn1, g_mix, w_in, conv_qkv, a_log, dt_bias, g_onorm, lam_re, lam_im, log_step, b_re, b_im, c_re, c_im, d_skip, w_glu, b_glu, w_proj_a, w_proj_b, w_out, g_ffn2, w1_ffn2, w3_ffn2, w2_ffn2, g_final, loss_target, m_w_ada, m_b_ada, m_g_ffn1, m_w1_ffn1, m_w3_ffn1, m_w2_ffn1, m_g_mix, m_w_in, m_conv_qkv, m_a_log, m_dt_bias, m_g_onorm, m_lam_re, m_lam_im, m_log_step, m_b_re, m_b_im, m_c_re, m_c_im, m_d_skip, m_w_glu, m_b_glu, m_w_proj_a, m_w_proj_b, m_w_out, m_g_ffn2, m_w1_ffn2, m_w3_ffn2, m_w2_ffn2, m_g_final, v_w_ada, v_b_ada, v_g_ffn1, v_w1_ffn1, v_w3_ffn1, v_w2_ffn1, v_g_mix, v_w_in, v_conv_qkv, v_a_log, v_dt_bias, v_g_onorm, v_lam_re, v_lam_im, v_log_step, v_b_re, v_b_im, v_c_re, v_c_im, v_d_skip, v_w_glu, v_b_glu, v_w_proj_a, v_w_proj_b, v_w_out, v_g_ffn2, v_w1_ffn2, v_w3_ffn2, v_w2_ffn2, v_g_final):
    a = dict(locals())
    bl, seq, _ = x.shape
    t_rows = bl * seq
    me = 4 * lax.axis_index("x") + 2 * lax.axis_index("y") + lax.axis_index("c")
    tm_ew = _pick(seq, (256, 128, 64))

    loc = {n: (a[n][0].T if n in COL_SHARDED else a[n][0]) for n in RS_WEIGHTS}
    wfull, gw, res = {}, {}, {}

    def pack_local(names):
        return _pack_rows([loc[n].astype(BF16).reshape(-1, ROW) for n in names], 0)

    def unpack_full(buf, names):
        r0 = 0
        for n in names:
            r = loc[n].size // ROW
            wfull[n] = buf[:, r0:r0 + r, :].reshape(-1, loc[n].shape[1])
            r0 += _tile_rows(r)

    def pack_grads(names):
        return _pack_rows([gw[n].astype(BF16).reshape(NDEV, -1, ROW) for n in names], 1)

    def update(buf, names):
        r0 = 0
        for n in names:
            r = loc[n].size // ROW
            parts = buf[:, r0:r0 + r, :].reshape((NDEV,) + loc[n].shape)
            r0 += _tile_rows(r)
            step = adamw_t if n in COL_SHARDED else adamw
            out = step("adamw_" + n, parts, a[n][0], a["m_" + n][0], a["v_" + n][0])
            for kind, t in zip(("grad", "delta", "new_m", "new_v"), out):
                res[kind + "_" + n] = t[None]

    sm, wg_ffn1 = all_gather_pair("gather_inputs", _pack([c, conv_qkv[0]], F32), pack_local(G_FFN1))
    unpack_full(wg_ffn1, G_FFN1)
    c_loc, conv_loc = _unpack8(sm, [c.shape, conv_qkv.shape[1:]])
    c_all = c_loc.reshape(NDEV * bl, D)
    conv_full = conv_loc.transpose(1, 0, 2).reshape(CONVW, 3 * DNW)

    n_ada = w_ada.shape[2]
    mod_part = ada_fwd(c_all, w_ada[0], lax.dynamic_slice(b_ada, (0, me * n_ada), (1, n_ada)))
    mod_all = all_gather("gather_mod", mod_part).transpose(1, 0, 2).reshape(NDEV * bl, 9 * D)
    mod = lax.dynamic_slice(mod_all, (me * bl, 0), (bl, 9 * D)).reshape(bl, 9, D)
    mods = [mod[:, k:k + 1, :] for k in range(9)]

    h0 = x.reshape(t_rows, D)
    h1, f1, u1, pa1, pb1, wg_rest = ffn_fwd("ffn1_fwd", h0, mod[:, 0:3, :], g_ffn1, wfull['w1_ffn1'], wfull['w3_ffn1'],
                                  wfull['w2_ffn1'], seq, gather=pack_local(G_MIX))
    unpack_full(wg_rest, G_MIX)
    win = wfull['w_in']
    o_small, o_s5, o_gate = 4 * DNW, 4 * DNW + 2 * NH, 4 * DNW + 2 * NH + S5W
    w_dn, w_small = win[:o_small], jnp.pad(win[o_small:o_s5], ((0, LANES - 2 * NH), (0, 0)))
    w_s5, w_gate = win[o_s5:o_gate], win[o_gate:]
    w_pieces = [w_dn, w_small, w_s5, w_gate]
    u2, p_dn, p_small, p_s5, p_gate = mix_in_fwd(h1, mods[3], mods[4], g_mix, w_pieces, seq)

    conv8 = jnp.pad(conv_full, ((0, 8 - CONVW), (0, 0)))
    alp = jnp.pad(a_log, ((0, 0), (NH, LANES - 2 * NH)))
    dtp = jnp.pad(dt_bias, ((0, 0), (NH, LANES - 2 * NH)))
    nb_dn = DN_ROWS if bl % DN_ROWS == 0 else 1
    p_dn3, p_small3 = p_dn.reshape(bl, seq, 4 * DNW), p_small.reshape(bl, seq, LANES)
    qkv3 = dn_prep_fwd(p_dn3, conv8)
    o_pre3, sprev, tinv, wg_ffn2 = deltanet_fwd(qkv3, p_small3, alp, dtp, nb_dn, gather=pack_local(G_FFN2))
    unpack_full(wg_ffn2, G_FFN2)
    o_pre = o_pre3.reshape(t_rows, DNW)
    z_raw = p_dn[:, 3 * DNW:]

    s5_params = [lam_re.reshape(1, S5N), lam_im.reshape(1, S5N), log_step,
                 b_re[0].transpose(2, 0, 1).reshape(S5C, S5N), b_im[0].transpose(2, 0, 1).reshape(S5C, S5N),
                 c_re[0].transpose(1, 0, 2).reshape(S5C, S5N), c_im[0].transpose(1, 0, 2).reshape(S5C, S5N)]
    tables = s5_tables_fwd(s5_params)
    p_s53 = p_s5.reshape(bl, seq, S5W)
    y_s53, xs = s5_fwd(p_s53, tables, d_skip)
    y_s5 = y_s53.reshape(t_rows, S5W)
    tail_in = [o_pre, z_raw, y_s5, p_gate]
    tail_w = [g_onorm, wfull['w_glu'], b_glu, wfull['w_proj_a'], wfull['w_proj_b']]
    (merged,) = ew_call("mix_tail", fn_mix_tail, tail_in, [], tail_w, [(D, BF16)], tm_ew, seq)
    mo, h2 = mix_out_fwd(merged, wfull['w_out'], h1, mods[5], seq)
    h3, f3, u3, pa3, pb3 = ffn_fwd("ffn2_fwd", h2, mod[:, 6:9, :], g_ffn2, wfull['w1_ffn2'], wfull['w3_ffn2'], wfull['w2_ffn2'], seq)

    dh3, dg_final, loss_part = loss_head(h3, loss_target.reshape(t_rows, D), g_final.reshape(1, D), seq)

    dh2, a3, d1_3, d3_3, df3, dmod_c, dg_ffn2 = ffn_bwd("ffn2_bwd", dh3, h2, f3, pa3, pb3, mod[:, 6:9, :], g_ffn2, wfull['w1_ffn2'],
                                                   wfull['w3_ffn2'], wfull['w2_ffn2'], seq)
    gw['w1_ffn2'] = mm_tn("gw1_ffn2", d1_3, u3)
    gw['w3_ffn2'] = mm_tn("gw3_ffn2", d3_3, u3)
    gw['w2_ffn2'] = mm_tn("gw2_ffn2", a3, df3)

    dmo, d_merged, dgt2 = mix_out_bwd(dh2, mo, wfull['w_out'], mods[5], seq)
    gw['w_out'] = mm_tn("gw_out", merged, dmo)
    (d_opre, d_z, d_ys5, d_gate), _, tail_gw = ew_vjp_call(
        "mix_tail_bwd", fn_mix_tail, tail_in, [], tail_w, [d_merged], [(0, F32), (1, F32), (2, F32), (3, BF16)],
        _pick(seq, (512, 256, 128, 64)), seq)
    dg_onorm, gw['w_glu'], dg_bglu, gw['w_proj_a'], gw['w_proj_b'] = tail_gw
    d_qkv3, d_psmall3, d_alp, d_dtp, rs_ffn2 = deltanet_bwd(
        qkv3, p_small3, alp, dtp, sprev, tinv, d_opre.reshape(bl, seq, DNW), nb_dn, exchange=pack_grads(G_FFN2))
    d_pdn3, d_conv8 = dn_prep_bwd(p_dn3, conv8, d_qkv3, d_z.reshape(bl, seq, DNW))
    d_pdn, d_psmall = d_pdn3.reshape(t_rows, 4 * DNW), d_psmall3.reshape(t_rows, LANES)

    s5_out = s5_bwd(p_s53, tables, d_skip, xs, d_ys5.reshape(bl, seq, S5W))
    d_ps5, d_tables, dg_dskip = s5_out[0].reshape(t_rows, S5W), s5_out[1:11], s5_out[11]
    d_s5p = s5_tables_bwd(s5_params, d_tables)

    gw['w_in'] = jnp.concatenate([mm_tn("gw_dn", d_pdn, u2), mm_tn("gw_small", d_psmall, u2)[:2 * NH],
                                  mm_tn("gw_s5", d_ps5, u2), mm_tn("gw_gate", d_gate, u2)], axis=0)
    dh1, dsh2, dsc2, dg_mix = mix_in_bwd([d_pdn, d_psmall, d_ps5, d_gate], w_pieces, h1, mods[3], mods[4], g_mix, dh2, seq)

    dh0, a1, d1_1, d3_1, df1, dmod_a, dg_ffn1, rs_mix = ffn_bwd(
        "ffn1_bwd", dh1, h0, f1, pa1, pb1, mod[:, 0:3, :], g_ffn1, wfull['w1_ffn1'], wfull['w3_ffn1'], wfull['w2_ffn1'], seq,
        exchange=pack_grads(G_MIX))
    dmod_mine = jnp.concatenate([dmod_a, dsh2, dsc2, dgt2, dmod_c], axis=1).reshape(bl, 9 * D)
    small_grads = {
        'g_ffn1': dg_ffn1, 'g_mix': dg_mix, 'a_log': d_alp[:, NH:2 * NH], 'dt_bias': d_dtp[:, NH:2 * NH],
        'g_onorm': dg_onorm, 'lam_re': d_s5p[0].reshape(1, S5G, S5P), 'lam_im': d_s5p[1].reshape(1, S5G, S5P),
        'log_step': d_s5p[2],
        'b_re': d_s5p[3].reshape(S5C, S5G, S5P).transpose(1, 2, 0)[None],
        'b_im': d_s5p[4].reshape(S5C, S5G, S5P).transpose(1, 2, 0)[None],
        'c_re': d_s5p[5].reshape(S5C, S5G, S5P).transpose(1, 0, 2)[None],
        'c_im': d_s5p[6].reshape(S5C, S5G, S5P).transpose(1, 0, 2)[None],
        'd_skip': dg_dskip, 'b_glu': dg_bglu, 'g_ffn2': dg_ffn2, 'g_final': dg_final.reshape(D)}
    small_shapes = [a[n].shape for n in SMALL]
    small_pack = _pack([small_grads[n] for n in SMALL] + [loss_part], F32)
    n_small = small_pack.shape[0]
    small_buf = jnp.concatenate([small_pack, _pack([dmod_mine, d_conv8[:CONVW]], F32)], axis=0)

    gw['w1_ffn1'], sg = mm_tn("gw1_ffn1", d1_1, u1, gather=small_buf)
    gw['w3_ffn1'], rs_w1 = mm_tn("gw3_ffn1", d3_1, u1, exchange=pack_grads(['w1_ffn1']))
    gw['w2_ffn1'], rs_w3 = mm_tn("gw2_ffn1", a1, df1, exchange=pack_grads(['w3_ffn1']))
    rs_w2 = all_to_all("scatter_w2_ffn1", pack_grads(['w2_ffn1']))

    update(rs_ffn2, G_FFN2)
    update(rs_mix, G_MIX)
    update(rs_w1, ['w1_ffn1'])
    update(rs_w3, ['w3_ffn1'])
    update(rs_w2, ['w2_ffn1'])
    pieces = _unpack8(sg[:, n_small:, :], [dmod_mine.shape, (CONVW, 3 * DNW)])
    dmod_all = pieces[0].reshape(NDEV * bl, 9 * D)
    g_wada, g_bada = ada_bwd(c_all, lax.dynamic_slice(dmod_all, (0, me * n_ada), (NDEV * bl, n_ada)), dmod_all)

    n_conv = conv_qkv.shape[2]
    conv_parts = lax.dynamic_slice(pieces[1], (0, 0, me * n_conv), (NDEV, CONVW, n_conv))
    conv_parts = jnp.pad(conv_parts.reshape(NDEV, 1, -1), ((0, 0), (0, 7), (0, 0)))
    pad8 = lambda t: jnp.pad(t.reshape(1, -1), ((0, 7), (0, 0)))
    conv_res = adamw("adamw_conv", conv_parts, pad8(conv_qkv), pad8(m_conv_qkv), pad8(v_conv_qkv))
    for kind, buf in zip(("grad", "delta", "new_m", "new_v"), conv_res):
        res[kind + "_conv_qkv"] = buf[0].reshape(conv_qkv.shape)

    no_param = jnp.zeros_like(loss_part)
    small_res = adamw("adamw_small", sg[:, :n_small, :],
                      *[_pack([a[p + n] for n in SMALL] + [no_param], F32) for p in ("", "m_", "v_")])
    for kind, buf in zip(("grad", "delta", "new_m", "new_v"), small_res):
        for n, t in zip(SMALL, _unpack(buf, small_shapes)):
            res[kind + "_" + n] = t
    loss = _unpack(small_res[0], small_shapes + [loss_part.shape])[-1][0, 0]

    for n, g in (("w_ada", g_wada), ("b_ada", g_bada)):
        shp = a[n].shape
        r2 = lambda t: t.reshape(-1, shp[-1]) if n == "w_ada" else pad8(t)
        out = adamw("adamw_" + n, r2(g)[None], r2(a[n]), r2(a["m_" + n]), r2(a["v_" + n]))
        for kind, buf in zip(("grad", "delta", "new_m", "new_v"), out):
            res[kind + "_" + n] = (buf if n == "w_ada" else buf[0:1]).reshape(shp)

    outs = [loss, dh0.reshape(x.shape)]
    for kind in ("grad", "delta", "new_m", "new_v"):
        outs += [res[kind + "_" + n] for n in WEIGHTS]
    return tuple(outs)
```

```python
import math

import jax
import jax.numpy as jnp
from jax import lax
from jax.experimental import pallas as pl
from jax.experimental.pallas import tpu as pltpu

F32 = jnp.float32
BF16 = jnp.bfloat16
HI = lax.Precision.HIGHEST
H3 = lax.Precision.HIGH
SDS = jax.ShapeDtypeStruct

D = 1024
FF = 2816
FFN_TF = FF
FFN_FWD_TM = 256
FFN_BWD_TM = 256
NH = 8
DH = 64
DNW = NH * DH
CONVW = 4
CH = 64
S5_CH = 128
ACC_LIMIT = 6 * 1024 * 1024
BF16_TILE_ROWS = 16
DN_ROWS = 4
S5W = 512
S5G = 32
S5P = 64
S5C = 16
S5N = S5G * S5P
GB = 4
NDEV = 8
EPS = 1e-6
LANES = 128
ROW = 1024
VMEM_LIMIT = 56 * 1024 * 1024

ADAM_LR, ADAM_B1, ADAM_B2, ADAM_EPS, ADAM_WD, ADAM_STEP = 0.001, 0.9, 0.999, 1e-08, 0.01, 10

WEIGHTS = ['w_ada', 'b_ada', 'g_ffn1', 'w1_ffn1', 'w3_ffn1', 'w2_ffn1', 'g_mix', 'w_in', 'conv_qkv', 'a_log',
           'dt_bias', 'g_onorm', 'lam_re', 'lam_im', 'log_step', 'b_re', 'b_im', 'c_re', 'c_im', 'd_skip', 'w_glu',
           'b_glu', 'w_proj_a', 'w_proj_b', 'w_out', 'g_ffn2', 'w1_ffn2', 'w3_ffn2', 'w2_ffn2', 'g_final']
RS_WEIGHTS = ['w1_ffn1', 'w3_ffn1', 'w2_ffn1', 'w_in', 'w_glu', 'w_proj_a', 'w_proj_b', 'w_out', 'w1_ffn2', 'w3_ffn2',
              'w2_ffn2']
COL_SHARDED = {'w1_ffn1', 'w3_ffn1', 'w_in', 'w_proj_a', 'w_proj_b', 'w1_ffn2', 'w3_ffn2'}
G_FFN1 = ['w1_ffn1', 'w3_ffn1', 'w2_ffn1']
G_MIX = ['w_in', 'w_glu', 'w_proj_a', 'w_proj_b', 'w_out']
G_FFN2 = ['w1_ffn2', 'w3_ffn2', 'w2_ffn2']
SMALL = ['g_ffn1', 'g_mix', 'a_log', 'dt_bias', 'g_onorm', 'lam_re', 'lam_im', 'log_step', 'b_re', 'b_im', 'c_re',
         'c_im', 'd_skip', 'b_glu', 'g_ffn2', 'g_final']


def _cp(n_grid=0):
    if n_grid:
        return pltpu.CompilerParams(vmem_limit_bytes=VMEM_LIMIT, dimension_semantics=("arbitrary",) * n_grid)
    return pltpu.CompilerParams(vmem_limit_bytes=VMEM_LIMIT)


def _dot(a, b):
    return jnp.dot(a.astype(BF16), b.astype(BF16), preferred_element_type=F32)


def _dot_nt(a, b):
    return lax.dot_general(a.astype(BF16), b.astype(BF16), (((1,), (1,)), ((), ())), preferred_element_type=F32)


def _dot_tn(a, b):
    return lax.dot_general(a.astype(BF16), b.astype(BF16), (((0,), (0,)), ((), ())), preferred_element_type=F32)


def _dot_hi(a, b):
    return jnp.dot(a, b, precision=HI, preferred_element_type=F32)


@jax.custom_vjp
def bdot(a, b):
    return _dot(a, b)


bdot.defvjp(lambda a, b: (_dot(a, b), (a, b)),
            lambda r, g: (_dot_nt(g, r[1]).astype(r[0].dtype), _dot_tn(r[0], g).astype(r[1].dtype)))


@jax.custom_vjp
def bdot_nt(a, b):
    return _dot_nt(a, b)


bdot_nt.defvjp(lambda a, b: (_dot_nt(a, b), (a, b)),
               lambda r, g: (_dot(g, r[1]).astype(r[0].dtype), _dot_tn(g, r[0]).astype(r[1].dtype)))


def _silu(x):
    return x * jax.nn.sigmoid(x)


def _iota2(shape, axis):
    return lax.broadcasted_iota(jnp.int32, shape, axis)


def normmod(h, g, sc, sh):
    y = h * lax.rsqrt(jnp.mean(h * h, axis=-1, keepdims=True) + EPS) * g
    return y * (1.0 + sc) + sh


def fn_merge(gate, ya, yb):
    return (jax.nn.sigmoid(gate[:, :D]) * ya + jax.nn.sigmoid(gate[:, D:]) * yb,)


def fn_glu(y, w, b):
    ge = jax.nn.gelu(y)
    return (ge * jax.nn.sigmoid(bdot(ge, w) + b),)


def fn_onorm(o, z, g_on):
    r = _iota2((DH, DNW), 0)
    c = _iota2((DH, DNW), 1)
    expand = (c % DH == r).astype(F32)
    r2 = _iota2((DNW, DNW), 0)
    c2 = _iota2((DNW, DNW), 1)
    avg = (r2 // DH == c2 // DH).astype(F32) * (1.0 / DH)
    ms = bdot(o * o, avg)
    return (o * lax.rsqrt(ms + EPS) * _dot_hi(g_on, expand) * _silu(z),)


def fn_mix_tail(o_pre, z, y_s5, gate, g_on, w_glu, b_glu, wa_t, wb_t):
    (oa,) = fn_onorm(o_pre, z, g_on)
    (ob,) = fn_glu(y_s5, w_glu, b_glu)
    return fn_merge(gate, bdot_nt(oa, wa_t), bdot_nt(ob, wb_t))


def gate_fn(small, alp, dtp):
    beta = jax.nn.sigmoid(small)
    la = -jnp.exp(alp) * jax.nn.softplus(small + dtp)
    tri = (_iota2((CH, CH), 0) >= _iota2((CH, CH), 1)).astype(F32)
    gc = _dot_hi(tri, la)
    gct = lax.dot_general(la, tri, (((0,), (1,)), ((), ())), precision=HI, preferred_element_type=F32)
    return beta, gc, gct


def _bdg(a, b, ca, cb, hi):
    if not hi:
        a, b = a.astype(BF16), b.astype(BF16)
    return lax.dot_general(a, b, (((ca,), (cb,)), ((0,), (0,))), precision=H3 if hi else None,
                           preferred_element_type=F32)


def _batched_matmuls(hi):
    nn_ = lambda a, b: _bdg(a, b, 2, 1, hi)
    nt_ = lambda a, b: _bdg(a, b, 2, 2, hi)
    tn_ = lambda a, b: _bdg(a, b, 1, 1, hi)
    nn = jax.custom_vjp(nn_)
    nn.defvjp(lambda a, b: (nn_(a, b), (a, b)), lambda r, g: (nt_(g, r[1]), tn_(r[0], g)))
    nt = jax.custom_vjp(nt_)
    nt.defvjp(lambda a, b: (nt_(a, b), (a, b)), lambda r, g: (nn_(g, r[1]), tn_(g, r[0])))
    tn = jax.custom_vjp(tn_)
    tn.defvjp(lambda a, b: (tn_(a, b), (a, b)), lambda r, g: (nt_(r[1], g), nn_(r[0], g)))
    return nn, nt, tn


bnn, bnt, btn = _batched_matmuls(False)
hnn, hnt, htn = _batched_matmuls(True)


def _unit_lower_inverse(a):
    r = _iota2((1, CH, CH), 1)
    c = _iota2((1, CH, CH), 2)
    eye = (r == c).astype(F32)
    d = jnp.where(r // 8 == c // 8, a, 0.0)
    inv = eye - d
    p = d
    for _ in range(2):
        p = hnn(p, p)
        inv = inv + hnn(inv, p)
    for blk in (16, 32, 64):
        off = jnp.where((r // blk == c // blk) & (r // (blk // 2) != c // (blk // 2)), a, 0.0)
        mm = hnn if blk == 16 else bnn
        inv = inv - mm(mm(inv, off), inv)
    return inv


@jax.custom_vjp
def _inverse_given(a, t):
    return t


_inverse_given.defvjp(lambda a, t: (t, t), lambda t, g: (-hnt(htn(t, g), t), jnp.zeros_like(t)))


def dn_prep(xc, w):
    t = xc.shape[0] - 8
    c = xc[5:5 + t] * w[0:1] + xc[6:6 + t] * w[1:2] + xc[7:7 + t] * w[2:3] + xc[8:8 + t] * w[3:4]
    act = _silu(c)
    q, k, v = act[:, :DNW], act[:, DNW:2 * DNW], act[:, 2 * DNW:]
    ones = (_iota2((DNW, DNW), 0) // DH == _iota2((DNW, DNW), 1) // DH).astype(F32)
    q = q * lax.rsqrt(bdot(q * q, ones) + EPS) * (DH ** -0.5)
    k = k * lax.rsqrt(bdot(k * k, ones) + EPS)
    return jnp.concatenate([q, k, v], axis=1)


def dn_chunk(q, k, v, b, g, gt, s_prev, t_saved=None):
    r = _iota2((1, CH, CH), 1)
    c = _iota2((1, CH, CH), 2)
    causal = r >= c
    dec = jnp.where(causal, jnp.exp(jnp.where(causal, g - gt, 0.0)), 0.0)
    kb = k * b
    qk = bnt(jnp.concatenate([q, kb], axis=1), k)
    attn = qk[:, :CH] * dec
    a = jnp.where(r > c, qk[:, CH:] * dec, 0.0)
    tinv = _unit_lower_inverse(a) if t_saved is None else _inverse_given(a, t_saved)
    eg = jnp.exp(g)
    uw = hnn(tinv, jnp.concatenate([v * b, kb * eg], axis=2))
    g_last = g[:, CH - 1:CH]
    ws = bnn(jnp.concatenate([uw[..., DH:], q * eg], axis=1), s_prev)
    v_new = uw[..., :DH] - ws[:, :CH]
    o = ws[:, CH:] + bnn(attn, v_new)
    s_new = s_prev * jnp.exp(g_last) + btn(k * jnp.exp(g_last - g), v_new)
    return o, s_new, tinv


def s5_chunk(u, xp_re, xp_im, bb_re, bb_im, cc_re, cc_im, p0r, p0i, p1r, p1i, pir, pii, dsk):
    nb, ch, _ = u.shape
    u2 = u.reshape(nb * ch, LANES)
    bu_re = bdot(u2, bb_re).reshape(nb, ch, 512)
    bu_im = bdot(u2, bb_im).reshape(nb, ch, 512)
    xt_re = pir * bu_re - pii * bu_im
    xt_im = pir * bu_im + pii * bu_re
    tri = jnp.broadcast_to((_iota2((1, ch, ch), 1) >= _iota2((1, ch, ch), 2)).astype(F32), (nb, ch, ch))
    cs_re = hnn(tri, xt_re)
    cs_im = hnn(tri, xt_im)
    x_re = p0r * cs_re - p0i * cs_im + p1r * xp_re - p1i * xp_im
    x_im = p0r * cs_im + p0i * cs_re + p1r * xp_im + p1i * xp_re
    y = bdot_nt(x_re.reshape(nb * ch, 512), cc_re) - bdot_nt(x_im.reshape(nb * ch, 512), cc_im) + dsk * u2
    return y.reshape(nb, ch, LANES), x_re[:, ch - 1:ch], x_im[:, ch - 1:ch]


def s5_tables(lam_re, lam_im, log_step, bre, bim, cre, cim):
    expand = (_iota2((S5G, S5N), 1) // S5P == _iota2((S5G, S5N), 0)).astype(F32)
    step = _dot_hi(jnp.exp(log_step), expand)
    lre = jnp.minimum(lam_re, -1e-4)
    lr = lre * step
    ang = lam_im * step
    mag = jnp.exp(lr)
    lb_re = mag * jnp.cos(ang)
    lb_im = mag * jnp.sin(ang)
    den = lre * lre + lam_im * lam_im
    coef_re = ((lb_re - 1.0) * lre + lb_im * lam_im) / den
    coef_im = (lb_im * lre - (lb_re - 1.0) * lam_im) / den
    bb_re = coef_re * bre - coef_im * bim
    bb_im = coef_re * bim + coef_im * bre
    j = _iota2((S5_CH, 1), 0).astype(F32)
    jc = j - S5_CH // 2
    e0 = jnp.exp(jc * lr)
    e1 = jnp.exp((j + 1.0) * lr)
    ei = jnp.exp(-jc * lr)
    mask = (_iota2((LANES, 512), 0) // S5C == _iota2((LANES, 512), 1) // S5P).astype(F32)

    def blocks(t):
        return jnp.concatenate([(jnp.tile(t[:, gb * 512:(gb + 1) * 512], (LANES // S5C, 1)) * mask)[None]
                                for gb in range(GB)], axis=0)

    return (blocks(bb_re), blocks(bb_im), blocks(cre), blocks(cim),
            e0 * jnp.cos(jc * ang), e0 * jnp.sin(jc * ang),
            e1 * jnp.cos((j + 1.0) * ang), e1 * jnp.sin((j + 1.0) * ang),
            ei * jnp.cos(jc * ang), -ei * jnp.sin(jc * ang))


def _row_specs(tiled, batch, bcast, tm, tpb):
    specs = [pl.BlockSpec((tm, a.shape[1]), lambda i: (i, 0)) for a in tiled]
    specs += [pl.BlockSpec((None,) + a.shape[1:], lambda i: (i // tpb, 0, 0)) for a in batch]
    specs += [pl.BlockSpec(a.shape, lambda i, nd=a.ndim: (0,) * nd) for a in bcast]
    return specs


def ew_call(name, fn, tiled, batch, bcast, outs, tm, seq):
    t_rows = tiled[0].shape[0]
    n_in = len(tiled) + len(batch) + len(bcast)

    def body(*refs):
        vals = [r[...].astype(F32) for r in refs[:n_in]]
        for r, o in zip(refs[n_in:], fn(*vals)):
            r[...] = o.astype(r.dtype)

    return pl.pallas_call(
        body, grid=(t_rows // tm,), in_specs=_row_specs(tiled, batch, bcast, tm, seq // tm),
        out_specs=[pl.BlockSpec((tm, w), lambda i: (i, 0)) for w, _ in outs],
        out_shape=[SDS((t_rows, w), dt) for w, dt in outs], name=name, compiler_params=_cp(1))(*tiled, *batch, *bcast)


def ew_vjp_call(name, fn, tiled, batch, bcast, cts, want, tm, seq, addend=None):
    t_rows = tiled[0].shape[0]
    tpb = seq // tm
    n_t, n_b, n_c = len(tiled), len(batch), len(bcast)
    n_in = n_t + n_b + n_c
    extra = [] if addend is None else [addend]

    def body(*refs):
        i = pl.program_id(0)
        vals = [r[...].astype(F32) for r in refs[:n_in]]
        ctv = tuple(r[...].astype(F32) for r in refs[n_in:n_in + len(cts)])
        outs = refs[n_in + len(cts) + len(extra):]
        _, vjp = jax.vjp(fn, *vals)
        grads = vjp(ctv)
        for k, (r, (idx, _)) in enumerate(zip(outs[:len(want)], want)):
            g = grads[idx]
            if k == 0 and extra:
                g = g + refs[n_in + len(cts)][...]
            r[...] = g.astype(r.dtype)
        for k in range(n_b):
            r, g = outs[len(want) + k], grads[n_t + k]

            @pl.when(i % tpb == 0)
            def _(r=r, g=g):
                r[...] = g

            @pl.when(i % tpb != 0)
            def _(r=r, g=g):
                r[...] += g
        for k in range(n_c):
            r, g = outs[len(want) + n_b + k], grads[n_t + n_b + k]

            @pl.when(i == 0)
            def _(r=r, g=g):
                r[...] = g

            @pl.when(i != 0)
            def _(r=r, g=g):
                r[...] += g

    out_specs = [pl.BlockSpec((tm, tiled[idx].shape[1]), lambda i: (i, 0)) for idx, _ in want]
    out_specs += [pl.BlockSpec((None,) + a.shape[1:], lambda i: (i // tpb, 0, 0)) for a in batch]
    out_specs += [pl.BlockSpec(a.shape, lambda i, nd=a.ndim: (0,) * nd) for a in bcast]
    out_shape = [SDS(tiled[idx].shape, dt) for idx, dt in want]
    out_shape += [SDS(a.shape, F32) for a in batch] + [SDS(a.shape, F32) for a in bcast]
    res = pl.pallas_call(
        body, grid=(t_rows // tm,),
        in_specs=_row_specs(tiled, batch, bcast, tm, tpb)
        + [pl.BlockSpec((tm, a.shape[1]), lambda i: (i, 0)) for a in list(cts) + extra],
        out_specs=out_specs, out_shape=out_shape, name=name, compiler_params=_cp(1))(*tiled, *batch, *bcast, *cts, *extra)
    return res[:len(want)], res[len(want):len(want) + n_b], res[len(want) + n_b:]


def _pick(n, cands):
    for c in cands:
        if n % c == 0:
            return c
    return n


def mm_tn(name, a, b, exchange=None, gather=None):
    t_rows, m = a.shape
    n = b.shape[1]
    tn = n if n <= 1024 else _pick(n, (1024, 512, 256, 128))
    tm = max([t for t in range(LANES, m + 1, LANES) if m % t == 0 and t * tn * 4 <= ACC_LIMIT] or [m])
    tk = _pick(t_rows, (512, 256, 128, 64))
    grid = (m // tm, n // tn, t_rows // tk)
    extra = [x for x in (exchange, gather) if x is not None]
    ne = len(extra)

    def body(*refs):
        a_ref, b_ref = refs[:2]
        o_ref, acc = refs[2 + ne], refs[3 + 2 * ne]
        i, j, k = pl.program_id(0), pl.program_id(1), pl.program_id(2)
        first = (i == 0) & (j == 0) & (k == 0)
        middle = (i == grid[0] - 1) & (j == grid[1] - 1) & (k == grid[2] // 2)
        last = (i == grid[0] - 1) & (j == grid[1] - 1) & (k == grid[2] - 1)
        at_end = []
        for e, x in enumerate(extra):
            comm_refs = (refs[2 + e], refs[3 + ne + e]) + tuple(refs[4 + 2 * ne + 3 * e:7 + 2 * ne + 3 * e])
            if x is exchange:
                start, finish = _exchange_phases(*comm_refs)
                pl.when(first)(start)
            else:
                start, forward, finish = _gather_phases(*comm_refs)
                pl.when(first)(start)
                pl.when(middle)(forward)
            at_end.append(finish)

        @pl.when(k == 0)
        def _():
            acc[...] = jnp.zeros_like(acc)

        acc[...] += _dot_tn(a_ref[...], b_ref[...])

        @pl.when(k == grid[2] - 1)
        def _():
            o_ref[...] = acc[...].astype(BF16)

        for phase in at_end:
            pl.when(last)(phase)

    res = pl.pallas_call(
        body, grid=grid,
        in_specs=[pl.BlockSpec((tk, tm), lambda i, j, k: (k, i)), pl.BlockSpec((tk, tn), lambda i, j, k: (k, j))]
        + [HBM_SPEC] * ne,
        out_specs=[pl.BlockSpec((tm, tn), lambda i, j, k: (i, j))] + [HBM_SPEC] * ne,
        out_shape=[SDS((m, n), BF16)] + [SDS(x.shape if x is exchange else (NDEV,) + x.shape, x.dtype) for x in extra],
        scratch_shapes=[pltpu.VMEM((tm, tn), F32)] + _comm_scratch() * ne, name=name,
        compiler_params=_cp(3))(a, b, *extra)
    return res if extra else res[0]


def _ffn_weight_spec():
    if FFN_TF == FF:
        return pl.BlockSpec((FF, D), lambda i, j: (0, 0), pipeline_mode=pl.Buffered(1))
    return pl.BlockSpec((FFN_TF, D), lambda i, j: (j, 0))


def ffn_fwd(name, h, mod3, g, w1, w3, w2, seq, gather=None):
    t_rows = h.shape[0]
    tm = _pick(seq, (FFN_FWD_TM, 128, 64))
    tf = FFN_TF
    tpb = seq // tm
    nf = FF // tf
    nt = t_rows // tm
    extra = [] if gather is None else [gather]

    def body(*refs):
        h_ref, mod_ref, g_ref, w1_ref, w3_ref, w2_ref = refs[:6]
        ho_ref, f_ref, u_ref, h1_ref, h3_ref = refs[6 + len(extra):11 + len(extra)]
        acc = refs[11 + 2 * len(extra)]
        i, j = pl.program_id(0), pl.program_id(1)
        if extra:
            start, forward, finish = _gather_phases(refs[6], refs[12], *refs[14:17])
            pl.when((i == 0) & (j == 0))(start)
            pl.when((i == nt // 2) & (j == 0))(forward)

        @pl.when(j == 0)
        def _():
            u_ref[...] = normmod(h_ref[...], g_ref[...], mod_ref[1:2, :], mod_ref[0:1, :]).astype(BF16)
            acc[...] = jnp.zeros_like(acc)

        u = u_ref[...]
        h1 = _dot_nt(u, w1_ref[...])
        h3 = _dot_nt(u, w3_ref[...])
        h1_ref[...] = h1.astype(BF16)
        h3_ref[...] = h3.astype(BF16)
        acc[...] += _dot(_silu(h1) * h3, w2_ref[...])

        @pl.when(j == nf - 1)
        def _():
            f_ref[...] = acc[...]
            ho_ref[...] = h_ref[...] + 0.5 * mod_ref[2:3, :] * acc[...]

        if extra:
            pl.when((i == nt - 1) & (j == nf - 1))(finish)

    row = lambda i, j: (i, 0)
    return pl.pallas_call(
        body, grid=(nt, nf),
        in_specs=[pl.BlockSpec((tm, D), row), pl.BlockSpec((None, 3, D), lambda i, j: (i // tpb, 0, 0)),
                  pl.BlockSpec((1, D), lambda i, j: (0, 0)), _ffn_weight_spec(), _ffn_weight_spec(), _ffn_weight_spec()]
        + [HBM_SPEC] * len(extra),
        out_specs=[pl.BlockSpec((tm, D), row), pl.BlockSpec((tm, D), row), pl.BlockSpec((tm, D), row),
                   pl.BlockSpec((tm, tf), lambda i, j: (i, j)), pl.BlockSpec((tm, tf), lambda i, j: (i, j))]
        + [HBM_SPEC] * len(extra),
        out_shape=[SDS((t_rows, D), F32), SDS((t_rows, D), F32), SDS((t_rows, D), BF16), SDS((t_rows, FF), BF16),
                   SDS((t_rows, FF), BF16)] + [SDS((NDEV,) + x.shape, x.dtype) for x in extra],
        scratch_shapes=[pltpu.VMEM((tm, D), F32)] + (_comm_scratch() if extra else []), name=name,
        compiler_params=_cp(2))(h, mod3, g, w1, w3, w2, *extra)


def ffn_bwd(name, dho, h, f_out, h1_in, h3_in, mod3, g, w1, w3, w2, seq, exchange=None):
    t_rows = h.shape[0]
    tm = _pick(seq, (FFN_BWD_TM, 128, 64))
    tf = FFN_TF
    tpb = seq // tm
    nf = FF // tf
    nt = t_rows // tm
    extra = [] if exchange is None else [exchange]

    def body(*refs):
        dho_ref, h_ref, f_ref, h1_ref, h3_ref, mod_ref, g_ref, w1_ref, w3_ref, w2_ref = refs[:10]
        dh_ref, a_ref, dh1_ref, dh3_ref, df_scr, dmod_ref, dg_ref = refs[10 + len(extra):17 + len(extra)]
        du_acc = refs[17 + 2 * len(extra)]
        i, j = pl.program_id(0), pl.program_id(1)
        if extra:
            start, finish = _exchange_phases(refs[10], refs[18], *refs[20:23])
            pl.when((i == 0) & (j == 0))(start)

        @pl.when(j == 0)
        def _():
            df_scr[...] = (0.5 * mod_ref[2:3, :] * dho_ref[...]).astype(BF16)
            du_acc[...] = jnp.zeros_like(du_acc)

        h1 = h1_ref[...].astype(F32)
        h3 = h3_ref[...].astype(F32)
        sg = jax.nn.sigmoid(h1)
        s = h1 * sg
        da = _dot_nt(df_scr[...], w2_ref[...])
        dh3 = (da * s).astype(BF16)
        dh1 = (da * h3 * (sg * (1.0 + h1 * (1.0 - sg)))).astype(BF16)
        a_ref[...] = (s * h3).astype(BF16)
        dh1_ref[...] = dh1
        dh3_ref[...] = dh3
        du_acc[...] += _dot(dh1, w1_ref[...]) + _dot(dh3, w3_ref[...])

        @pl.when(j == nf - 1)
        def _():
            _, vjp = jax.vjp(normmod, h_ref[...], g_ref[...], mod_ref[1:2, :], mod_ref[0:1, :])
            dh_n, dg, dsc, dsh = vjp(du_acc[...])
            dh_ref[...] = dho_ref[...] + dh_n
            dgt = jnp.sum(0.5 * dho_ref[...] * f_ref[...], axis=0, keepdims=True)
            dmod = jnp.concatenate([dsh, dsc, dgt], axis=0)

            @pl.when(i % tpb == 0)
            def _():
                dmod_ref[...] = dmod

            @pl.when(i % tpb != 0)
            def _():
                dmod_ref[...] += dmod

            @pl.when(i == 0)
            def _():
                dg_ref[...] = dg

            @pl.when(i != 0)
            def _():
                dg_ref[...] += dg

        if extra:
            pl.when((i == nt - 1) & (j == nf - 1))(finish)

    row = lambda i, j: (i, 0)
    col = lambda i, j: (i, j)
    return pl.pallas_call(
        body, grid=(nt, nf),
        in_specs=[pl.BlockSpec((tm, D), row), pl.BlockSpec((tm, D), row), pl.BlockSpec((tm, D), row),
                  pl.BlockSpec((tm, tf), col), pl.BlockSpec((tm, tf), col),
                  pl.BlockSpec((None, 3, D), lambda i, j: (i // tpb, 0, 0)),
                  pl.BlockSpec((1, D), lambda i, j: (0, 0)), _ffn_weight_spec(), _ffn_weight_spec(), _ffn_weight_spec()]
        + [HBM_SPEC] * len(extra),
        out_specs=[pl.BlockSpec((tm, D), row), pl.BlockSpec((tm, tf), col), pl.BlockSpec((tm, tf), col),
                   pl.BlockSpec((tm, tf), col), pl.BlockSpec((tm, D), row),
                   pl.BlockSpec((None, 3, D), lambda i, j: (i // tpb, 0, 0)), pl.BlockSpec((1, D), lambda i, j: (0, 0))]
        + [HBM_SPEC] * len(extra),
        out_shape=[SDS((t_rows, D), F32), SDS((t_rows, FF), BF16), SDS((t_rows, FF), BF16), SDS((t_rows, FF), BF16),
                   SDS((t_rows, D), BF16), SDS(mod3.shape, F32), SDS((1, D), F32)] + [SDS(x.shape, x.dtype) for x in extra],
        scratch_shapes=[pltpu.VMEM((tm, D), F32)] + (_comm_scratch() if extra else []), name=name,
        compiler_params=_cp(2))(dho, h, f_out, h1_in, h3_in, mod3, g, w1, w3, w2, *extra)


def _resident(shape):
    return pl.BlockSpec(shape, lambda i: (0,) * len(shape), pipeline_mode=pl.Buffered(1))


def mix_in_fwd(h, sh, sc, g, ws, seq):
    t_rows = h.shape[0]
    tm = _pick(seq, (256, 128, 64))
    tpb = seq // tm
    nw = len(ws)

    def body(h_ref, sh_ref, sc_ref, g_ref, *rest):
        u = normmod(h_ref[...], g_ref[...], sc_ref[...], sh_ref[...]).astype(BF16)
        rest[nw][...] = u
        for w_ref, p_ref in zip(rest[:nw], rest[nw + 1:]):
            p_ref[...] = _dot_nt(u, w_ref[...])

    row = lambda i: (i, 0)
    batch = pl.BlockSpec((None, 1, D), lambda i: (i // tpb, 0, 0))
    return pl.pallas_call(
        body, grid=(t_rows // tm,),
        in_specs=[pl.BlockSpec((tm, D), row), batch, batch, pl.BlockSpec((1, D), lambda i: (0, 0))]
        + [_resident(w.shape) for w in ws],
        out_specs=[pl.BlockSpec((tm, D), row)] + [pl.BlockSpec((tm, w.shape[0]), row) for w in ws],
        out_shape=[SDS((t_rows, D), BF16)] + [SDS((t_rows, w.shape[0]), F32) for w in ws], name="mix_in_fwd",
        compiler_params=_cp(1))(h, sh, sc, g, *ws)


def mix_in_bwd(dps, ws, h, sh, sc, g, dh_add, seq):
    t_rows = h.shape[0]
    tm = _pick(seq, (256, 128, 64))
    tpb = seq // tm
    nw = len(ws)

    def body(*refs):
        h_ref, sh_ref, sc_ref, g_ref, add_ref, dh_ref, dsh_ref, dsc_ref, dg_ref = refs[2 * nw:]
        i = pl.program_id(0)
        du = _dot(refs[0][...], refs[nw][...])
        for k in range(1, nw):
            du = du + _dot(refs[k][...], refs[nw + k][...])
        _, vjp = jax.vjp(normmod, h_ref[...], g_ref[...], sc_ref[...], sh_ref[...])
        dh_n, dg, dsc, dsh = vjp(du)
        dh_ref[...] = add_ref[...] + dh_n

        @pl.when(i % tpb == 0)
        def _():
            dsh_ref[...] = dsh
            dsc_ref[...] = dsc

        @pl.when(i % tpb != 0)
        def _():
            dsh_ref[...] += dsh
            dsc_ref[...] += dsc

        @pl.when(i == 0)
        def _():
            dg_ref[...] = dg

        @pl.when(i != 0)
        def _():
            dg_ref[...] += dg

    row = lambda i: (i, 0)
    batch = pl.BlockSpec((None, 1, D), lambda i: (i // tpb, 0, 0))
    gain = pl.BlockSpec((1, D), lambda i: (0, 0))
    return pl.pallas_call(
        body, grid=(t_rows // tm,),
        in_specs=[pl.BlockSpec((tm, dp.shape[1]), row) for dp in dps] + [_resident(w.shape) for w in ws]
        + [pl.BlockSpec((tm, D), row), batch, batch, gain, pl.BlockSpec((tm, D), row)],
        out_specs=[pl.BlockSpec((tm, D), row), batch, batch, gain],
        out_shape=[SDS((t_rows, D), F32), SDS(sh.shape, F32), SDS(sc.shape, F32), SDS((1, D), F32)], name="mix_in_bwd",
        compiler_params=_cp(1))(*dps, *ws, h, sh, sc, g, dh_add)


def mix_out_fwd(merged, w_out, h_prev, gt, seq):
    t_rows = merged.shape[0]
    tm = _pick(seq, (256, 128, 64))
    tpb = seq // tm

    def body(m_ref, w_ref, h_ref, gt_ref, mo_ref, ho_ref):
        mo = _dot(m_ref[...], w_ref[...])
        mo_ref[...] = mo
        ho_ref[...] = h_ref[...] + gt_ref[...] * mo

    row = lambda i: (i, 0)
    return pl.pallas_call(
        body, grid=(t_rows // tm,),
        in_specs=[pl.BlockSpec((tm, D), row), _resident(w_out.shape), pl.BlockSpec((tm, D), row),
                  pl.BlockSpec((None, 1, D), lambda i: (i // tpb, 0, 0))],
        out_specs=[pl.BlockSpec((tm, D), row), pl.BlockSpec((tm, D), row)],
        out_shape=[SDS((t_rows, D), F32), SDS((t_rows, D), F32)], name="mix_out_fwd",
        compiler_params=_cp(1))(merged, w_out, h_prev, gt)


def mix_out_bwd(dh, mo, w_out, gt, seq):
    t_rows = dh.shape[0]
    tm = _pick(seq, (256, 128, 64))
    tpb = seq // tm

    def body(dh_ref, mo_ref, w_ref, gt_ref, dmo_ref, dm_ref, dgt_ref):
        i = pl.program_id(0)
        dmo = (gt_ref[...] * dh_ref[...]).astype(BF16)
        dmo_ref[...] = dmo
        dm_ref[...] = _dot_nt(dmo, w_ref[...])
        dgt = jnp.sum(dh_ref[...] * mo_ref[...], axis=0, keepdims=True)

        @pl.when(i % tpb == 0)
        def _():
            dgt_ref[...] = dgt

        @pl.when(i % tpb != 0)
        def _():
            dgt_ref[...] += dgt

    row = lambda i: (i, 0)
    batch = pl.BlockSpec((None, 1, D), lambda i: (i // tpb, 0, 0))
    return pl.pallas_call(
        body, grid=(t_rows // tm,),
        in_specs=[pl.BlockSpec((tm, D), row), pl.BlockSpec((tm, D), row), _resident(w_out.shape), batch],
        out_specs=[pl.BlockSpec((tm, D), row), pl.BlockSpec((tm, D), row), batch],
        out_shape=[SDS((t_rows, D), BF16), SDS((t_rows, D), F32), SDS(gt.shape, F32)], name="mix_out_bwd",
        compiler_params=_cp(1))(dh, mo, w_out, gt)


def _dn_cols(part, hd):
    return slice(part * DNW + hd * DH, part * DNW + (hd + 1) * DH)


def _qkv_stacks(qkv_ref, nb):
    pairs = [(b, hd) for b in range(nb) for hd in range(NH)]
    return [jnp.stack([qkv_ref[b, :, _dn_cols(part, hd)] for b, hd in pairs]) for part in range(3)]


def dn_prep_fwd(p_dn, conv8):
    bl, seq, _ = p_dn.shape
    tp = _pick(seq, (256, 128, 64))

    def body(raw_ref, halo_ref, conv_ref, o_ref):
        hm = (pl.program_id(1) > 0).astype(F32)
        o_ref[...] = dn_prep(jnp.concatenate([halo_ref[...] * hm, raw_ref[...]], axis=0), conv_ref[...])

    return pl.pallas_call(
        body, grid=(bl, seq // tp),
        in_specs=[pl.BlockSpec((None, tp, 3 * DNW), lambda b, i: (b, i, 0)),
                  pl.BlockSpec((None, 8, 3 * DNW), lambda b, i: (b, jnp.maximum(i * (tp // 8) - 1, 0), 0)),
                  pl.BlockSpec((8, 3 * DNW), lambda b, i: (0, 0))],
        out_specs=pl.BlockSpec((None, tp, 3 * DNW), lambda b, i: (b, i, 0)),
        out_shape=SDS((bl, seq, 3 * DNW), F32), name="dn_prep_fwd", compiler_params=_cp(2))(p_dn, p_dn, conv8)


def dn_prep_bwd(p_dn, conv8, d_qkv, d_z):
    bl, seq, _ = p_dn.shape
    tp = _pick(seq, (256, 128, 64))
    nt = seq // tp

    def body(raw_ref, halo_ref, conv_ref, dq_ref, dz_ref, draw_ref, dconv_ref, carry):
        b, r = pl.program_id(0), pl.program_id(1)

        @pl.when((b == 0) & (r == 0))
        def _():
            dconv_ref[...] = jnp.zeros_like(dconv_ref)

        @pl.when(r == 0)
        def _():
            carry[...] = jnp.zeros_like(carry)

        hm = (r < nt - 1).astype(F32)
        _, vjp = jax.vjp(dn_prep, jnp.concatenate([halo_ref[...] * hm, raw_ref[...]], axis=0), conv_ref[...])
        dxc, dw = vjp(dq_ref[...])
        tail = dxc[tp:tp + 8] + carry[...]
        draw_ref[:, 0:3 * DNW] = jnp.concatenate([dxc[8:tp], tail], axis=0).astype(BF16)
        draw_ref[:, 3 * DNW:4 * DNW] = dz_ref[...].astype(BF16)
        carry[...] = dxc[0:8] * hm
        dconv_ref[...] += dw

    blk = lambda b, r: (b, nt - 1 - r, 0)
    return pl.pallas_call(
        body, grid=(bl, nt),
        in_specs=[pl.BlockSpec((None, tp, 3 * DNW), blk),
                  pl.BlockSpec((None, 8, 3 * DNW), lambda b, r: (b, jnp.maximum((nt - 1 - r) * (tp // 8) - 1, 0), 0)),
                  pl.BlockSpec((8, 3 * DNW), lambda b, r: (0, 0)), pl.BlockSpec((None, tp, 3 * DNW), blk),
                  pl.BlockSpec((None, tp, DNW), blk)],
        out_specs=[pl.BlockSpec((None, tp, 4 * DNW), blk), pl.BlockSpec((8, 3 * DNW), lambda b, r: (0, 0))],
        out_shape=[SDS((bl, seq, 4 * DNW), BF16), SDS((8, 3 * DNW), F32)],
        scratch_shapes=[pltpu.VMEM((8, 3 * DNW), F32)], name="dn_prep_bwd", compiler_params=_cp(2))(p_dn, p_dn, conv8, d_qkv, d_z)


def _gate_stacks(gates, nb):
    pairs = [(b, hd) for b in range(nb) for hd in range(NH)]
    bs = jnp.stack([gates[b][0][:, hd:hd + 1] for b, hd in pairs])
    gs = jnp.stack([gates[b][1][:, NH + hd:NH + hd + 1] for b, hd in pairs])
    gts = jnp.stack([gates[b][2][NH + hd:NH + hd + 1, :] for b, hd in pairs])
    return bs, gs, gts


def deltanet_fwd(qkv, p_small, alp, dtp, nb, gather=None):
    bl, seq, _ = qkv.shape
    nc = seq // CH
    ng = nb * NH
    extra = [] if gather is None else [gather]

    def body(*refs):
        qkv_ref, small_ref, alp_ref, dtp_ref = refs[:4]
        o_ref, sprev_ref, tinv_ref = refs[4 + len(extra):7 + len(extra)]
        s_scr = refs[7 + 2 * len(extra)]
        bb, n = pl.program_id(0), pl.program_id(1)
        if extra:
            start, forward, finish = _gather_phases(refs[4], refs[8], *refs[10:13])
            pl.when((bb == 0) & (n == 0))(start)

        @pl.when(n == 0)
        def _():
            s_scr[...] = jnp.zeros_like(s_scr)

        gates = [gate_fn(small_ref[b], alp_ref[...], dtp_ref[...]) for b in range(nb)]
        s_prev = s_scr[...]
        o, s_new, tinv = dn_chunk(*_qkv_stacks(qkv_ref, nb), *_gate_stacks(gates, nb), s_prev)
        sprev_ref[...] = s_prev
        tinv_ref[...] = tinv
        s_scr[...] = s_new
        for b in range(nb):
            for hd in range(NH):
                o_ref[b, :, hd * DH:(hd + 1) * DH] = o[b * NH + hd]
        if extra:
            pl.when((bb == bl // nb - 1) & (n == nc // 2))(forward)
            pl.when((bb == bl // nb - 1) & (n == nc - 1))(finish)

    blk = lambda bb, n: (bb, n, 0)
    const = lambda bb, n: (0, 0)
    saved = pl.BlockSpec((None, ng, DH, DH), lambda bb, n: (bb * nc + n, 0, 0, 0))
    return pl.pallas_call(
        body, grid=(bl // nb, nc),
        in_specs=[pl.BlockSpec((nb, CH, 3 * DNW), blk), pl.BlockSpec((nb, CH, LANES), blk),
                  pl.BlockSpec((1, LANES), const), pl.BlockSpec((1, LANES), const)] + [HBM_SPEC] * len(extra),
        out_specs=[pl.BlockSpec((nb, CH, DNW), blk), saved, saved] + [HBM_SPEC] * len(extra),
        out_shape=[SDS((bl, seq, DNW), F32), SDS((bl // nb * nc, ng, DH, DH), F32), SDS((bl // nb * nc, ng, DH, DH), F32)]
        + [SDS((NDEV,) + x.shape, x.dtype) for x in extra],
        scratch_shapes=[pltpu.VMEM((ng, DH, DH), F32)] + (_comm_scratch() if extra else []), name="deltanet_fwd",
        compiler_params=_cp(2))(qkv, p_small, alp, dtp, *extra)


def deltanet_bwd(qkv, p_small, alp, dtp, sprev, tinv, d_o, nb, exchange=None):
    bl, seq, _ = qkv.shape
    nc = seq // CH
    ng = nb * NH
    extra = [] if exchange is None else [exchange]

    def body(*refs):
        qkv_ref, small_ref, alp_ref, dtp_ref, sprev_ref, tinv_ref, do_ref = refs[:7]
        dqkv_ref, dsmall_ref, dalp_ref, ddtp_ref = refs[7 + len(extra):11 + len(extra)]
        ds_scr = refs[11 + 2 * len(extra)]
        bb, r = pl.program_id(0), pl.program_id(1)
        if extra:
            start, finish = _exchange_phases(refs[7], refs[12], *refs[14:17])
            pl.when((bb == 0) & (r == 0))(start)

        @pl.when((bb == 0) & (r == 0))
        def _():
            dalp_ref[...] = jnp.zeros_like(dalp_ref)
            ddtp_ref[...] = jnp.zeros_like(ddtp_ref)

        @pl.when(r == 0)
        def _():
            ds_scr[...] = jnp.zeros_like(ds_scr)

        gates, gate_vjps = [], []
        for b in range(nb):
            out, gvjp = jax.vjp(gate_fn, small_ref[b], alp_ref[...], dtp_ref[...])
            gates.append(out)
            gate_vjps.append(gvjp)
        t_saved = tinv_ref[...]
        _, vjp = jax.vjp(lambda *args: dn_chunk(*args, t_saved)[:2], *_qkv_stacks(qkv_ref, nb), *_gate_stacks(gates, nb),
                         sprev_ref[...])
        d_out = jnp.stack([do_ref[b, :, hd * DH:(hd + 1) * DH] for b in range(nb) for hd in range(NH)])
        grads = vjp((d_out, ds_scr[...]))
        ds_scr[...] = grads[6]
        lane = _iota2((CH, LANES), 1)
        rowi = _iota2((LANES, CH), 0)
        for b in range(nb):
            d_beta = jnp.zeros((CH, LANES), F32)
            d_gc = jnp.zeros((CH, LANES), F32)
            d_gct = jnp.zeros((LANES, CH), F32)
            for hd in range(NH):
                i = b * NH + hd
                for part in range(3):
                    dqkv_ref[b, :, _dn_cols(part, hd)] = grads[part][i]
                d_beta = d_beta + jnp.where(lane == hd, grads[3][i], 0.0)
                d_gc = d_gc + jnp.where(lane == NH + hd, grads[4][i], 0.0)
                d_gct = d_gct + jnp.where(rowi == NH + hd, grads[5][i], 0.0)
            d_small, d_alp, d_dtp = gate_vjps[b]((d_beta, d_gc, d_gct))
            dsmall_ref[b] = d_small.astype(BF16)
            dalp_ref[...] += d_alp
            ddtp_ref[...] += d_dtp
        if extra:
            pl.when((bb == bl // nb - 1) & (r == nc - 1))(finish)

    blk = lambda bb, r: (bb, nc - 1 - r, 0)
    const = lambda bb, r: (0, 0)
    saved = pl.BlockSpec((None, ng, DH, DH), lambda bb, r: (bb * nc + nc - 1 - r, 0, 0, 0))
    return pl.pallas_call(
        body, grid=(bl // nb, nc),
        in_specs=[pl.BlockSpec((nb, CH, 3 * DNW), blk), pl.BlockSpec((nb, CH, LANES), blk), pl.BlockSpec((1, LANES), const),
                  pl.BlockSpec((1, LANES), const), saved, saved, pl.BlockSpec((nb, CH, DNW), blk)] + [HBM_SPEC] * len(extra),
        out_specs=[pl.BlockSpec((nb, CH, 3 * DNW), blk), pl.BlockSpec((nb, CH, LANES), blk), pl.BlockSpec((1, LANES), const),
                   pl.BlockSpec((1, LANES), const)] + [HBM_SPEC] * len(extra),
        out_shape=[SDS((bl, seq, 3 * DNW), F32), SDS((bl, seq, LANES), BF16), SDS((1, LANES), F32), SDS((1, LANES), F32)]
        + [SDS(x.shape, x.dtype) for x in extra],
        scratch_shapes=[pltpu.VMEM((ng, DH, DH), F32)] + (_comm_scratch() if extra else []), name="deltanet_bwd",
        compiler_params=_cp(2))(qkv, p_small, alp, dtp, sprev, tinv, d_o, *extra)


def _s5_table_specs():
    tab3 = pl.BlockSpec((None, LANES, 512), lambda gb, n: (gb, 0, 0))
    tab2 = pl.BlockSpec((S5_CH, 512), lambda gb, n: (0, gb))
    return [tab3] * 4 + [tab2] * 6 + [pl.BlockSpec((1, LANES), lambda gb, n: (0, gb))]


def s5_fwd(u, tables, dsk):
    bl, seq, _ = u.shape
    nc = seq // S5_CH

    def body(u_ref, *rest):
        tabs, (y_ref, xs_ref, xr_scr, xi_scr) = rest[:11], rest[11:]

        @pl.when(pl.program_id(1) == 0)
        def _():
            xr_scr[...] = jnp.zeros_like(xr_scr)
            xi_scr[...] = jnp.zeros_like(xi_scr)

        xp_re, xp_im = xr_scr[...], xi_scr[...]
        xs_ref[0:bl] = xp_re
        xs_ref[bl:2 * bl] = xp_im
        y, xn_re, xn_im = s5_chunk(u_ref[...], xp_re, xp_im, *[t[...] for t in tabs])
        y_ref[...] = y
        xr_scr[...] = xn_re
        xi_scr[...] = xn_im

    blk = lambda gb, n: (0, n, gb)
    return pl.pallas_call(
        body, grid=(GB, nc), in_specs=[pl.BlockSpec((bl, S5_CH, LANES), blk)] + _s5_table_specs(),
        out_specs=[pl.BlockSpec((bl, S5_CH, LANES), blk),
                   pl.BlockSpec((None, 2 * bl, 1, 512), lambda gb, n: (gb * nc + n, 0, 0, 0))],
        out_shape=[SDS((bl, seq, S5W), F32), SDS((GB * nc, 2 * bl, 1, 512), F32)],
        scratch_shapes=[pltpu.VMEM((bl, 1, 512), F32), pltpu.VMEM((bl, 1, 512), F32)], name="s5_fwd",
        compiler_params=_cp(2))(u, *tables, dsk)


def s5_bwd(u, tables, dsk, xs, dy):
    bl, seq, _ = u.shape
    nc = seq // S5_CH

    def body(u_ref, *rest):
        tabs, xs_ref, dy_ref = rest[:11], rest[11], rest[12]
        du_ref, dtabs, dxr_scr, dxi_scr = rest[13], rest[14:25], rest[25], rest[26]
        r = pl.program_id(1)

        @pl.when(r == 0)
        def _():
            for t in dtabs:
                t[...] = jnp.zeros_like(t)
            dxr_scr[...] = jnp.zeros_like(dxr_scr)
            dxi_scr[...] = jnp.zeros_like(dxi_scr)

        _, vjp = jax.vjp(s5_chunk, u_ref[...], xs_ref[0:bl], xs_ref[bl:2 * bl], *[t[...] for t in tabs])
        grads = vjp((dy_ref[...], dxr_scr[...], dxi_scr[...]))
        du_ref[...] = grads[0].astype(BF16)
        dxr_scr[...] = grads[1]
        dxi_scr[...] = grads[2]
        for t, g in zip(dtabs, grads[3:]):
            t[...] += g

    blk = lambda gb, r: (0, nc - 1 - r, gb)
    tab_shapes = [SDS(t.shape, F32) for t in tables] + [SDS(dsk.shape, F32)]
    return pl.pallas_call(
        body, grid=(GB, nc),
        in_specs=[pl.BlockSpec((bl, S5_CH, LANES), blk)] + _s5_table_specs()
        + [pl.BlockSpec((None, 2 * bl, 1, 512), lambda gb, r: (gb * nc + nc - 1 - r, 0, 0, 0)), pl.BlockSpec((bl, S5_CH, LANES), blk)],
        out_specs=[pl.BlockSpec((bl, S5_CH, LANES), blk)] + _s5_table_specs(),
        out_shape=[SDS((bl, seq, S5W), BF16)] + tab_shapes,
        scratch_shapes=[pltpu.VMEM((bl, 1, 512), F32), pltpu.VMEM((bl, 1, 512), F32)], name="s5_bwd",
        compiler_params=_cp(2))(u, *tables, dsk, xs, dy)


def s5_tables_fwd(params):
    shapes = [SDS((GB, LANES, 512), F32)] * 4 + [SDS((S5_CH, S5N), F32)] * 6

    def body(*refs):
        for r, t in zip(refs[7:], s5_tables(*[p[...] for p in refs[:7]])):
            r[...] = t

    return pl.pallas_call(body, out_shape=shapes, name="s5_tables_fwd", compiler_params=_cp())(*params)


def s5_tables_bwd(params, dtables):
    def body(*refs):
        _, vjp = jax.vjp(s5_tables, *[p[...] for p in refs[:7]])
        for r, g in zip(refs[17:], vjp(tuple(t[...] for t in refs[7:17]))):
            r[...] = g

    return pl.pallas_call(body, out_shape=[SDS(p.shape, F32) for p in params], name="s5_tables_bwd",
                          compiler_params=_cp())(*params, *dtables)


def ada_fwd(c_all, w_loc, b_loc):
    def body(c_ref, w_ref, b_ref, o_ref):
        o_ref[...] = _dot(_silu(c_ref[...]), w_ref[...]) + b_ref[...]

    return pl.pallas_call(body, out_shape=SDS((c_all.shape[0], w_loc.shape[1]), F32), name="ada_fwd",
                          compiler_params=_cp())(c_all, w_loc, b_loc)


def ada_bwd(c_all, dmod_mine, dmod_all):
    def body(c_ref, dm_ref, da_ref, gw_ref, gb_ref):
        gw_ref[...] = _dot_tn(_silu(c_ref[...]), dm_ref[...])
        gb_ref[...] = jnp.sum(da_ref[...], axis=0, keepdims=True)

    return pl.pallas_call(body, out_shape=[SDS((D, dmod_mine.shape[1]), F32), SDS((1, dmod_all.shape[1]), F32)],
                          name="ada_bwd", compiler_params=_cp())(c_all, dmod_mine, dmod_all)


def loss_head(h, tgt, g, seq):
    t_rows = h.shape[0]
    tm = _pick(seq, (256, 128, 64))

    def body(h_ref, t_ref, g_ref, dh_ref, dg_ref, loss_ref):
        i = pl.program_id(0)
        y, vjp = jax.vjp(lambda hh, gg: hh * lax.rsqrt(jnp.mean(hh * hh, axis=-1, keepdims=True) + EPS) * gg,
                         h_ref[...], g_ref[...])
        e = y - t_ref[...]
        dh, dg = vjp(e * (1.0 / D))
        part = jnp.sum(jnp.sum(e * e, axis=1, keepdims=True), axis=0, keepdims=True) * (0.5 / D) + jnp.zeros((1, LANES), F32)
        dh_ref[...] = dh

        @pl.when(i == 0)
        def _():
            dg_ref[...] = dg
            loss_ref[...] = part

        @pl.when(i != 0)
        def _():
            dg_ref[...] += dg
            loss_ref[...] += part

    row = lambda i: (i, 0)
    const = lambda i: (0, 0)
    return pl.pallas_call(
        body, grid=(t_rows // tm,),
        in_specs=[pl.BlockSpec((tm, D), row), pl.BlockSpec((tm, D), row), pl.BlockSpec((1, D), const)],
        out_specs=[pl.BlockSpec((tm, D), row), pl.BlockSpec((1, D), const), pl.BlockSpec((1, LANES), const)],
        out_shape=[SDS((t_rows, D), F32), SDS((1, D), F32), SDS((1, LANES), F32)], name="loss_head",
        compiler_params=_cp(1))(h, tgt, g)


def adamw(name, parts, w, m, v):
    k_parts, rows, cols = parts.shape
    tr = _pick(rows, (256, 128, 64, 32, 16, 8))

    def body(p_ref, w_ref, m_ref, v_ref, g_ref, d_ref, mo_ref, vo_ref):
        g = p_ref[0].astype(F32)
        for k in range(1, k_parts):
            g = g + p_ref[k].astype(F32)
        _adam_store(g, w_ref, m_ref, v_ref, g_ref, d_ref, mo_ref, vo_ref)

    blk = pl.BlockSpec((tr, cols), lambda i: (i, 0))
    return pl.pallas_call(
        body, grid=(rows // tr,), in_specs=[pl.BlockSpec((k_parts, tr, cols), lambda i: (0, i, 0)), blk, blk, blk],
        out_specs=[blk] * 4, out_shape=[SDS((rows, cols), F32)] * 4, name=name, compiler_params=_cp(1))(parts, w, m, v)


def _adam_store(g, w_ref, m_ref, v_ref, g_ref, d_ref, mo_ref, vo_ref):
    m_new = ADAM_B1 * m_ref[...] + (1.0 - ADAM_B1) * g
    v_new = ADAM_B2 * v_ref[...] + (1.0 - ADAM_B2) * (g * g)
    m_hat = m_new / (1.0 - ADAM_B1 ** ADAM_STEP)
    v_hat = v_new / (1.0 - ADAM_B2 ** ADAM_STEP)
    g_ref[...] = g
    d_ref[...] = -ADAM_LR * (m_hat / (jnp.sqrt(v_hat) + ADAM_EPS) + ADAM_WD * w_ref[...])
    mo_ref[...] = m_new
    vo_ref[...] = v_new


def adamw_t(name, parts, w, m, v):
    k_parts, r, c = parts.shape
    tc = _pick(c, (256, 128))

    def body(p_ref, w_ref, m_ref, v_ref, g_ref, d_ref, mo_ref, vo_ref):
        gt = p_ref[0].astype(F32)
        for k in range(1, k_parts):
            gt = gt + p_ref[k].astype(F32)
        _adam_store(gt.T, w_ref, m_ref, v_ref, g_ref, d_ref, mo_ref, vo_ref)

    blk = pl.BlockSpec((tc, r), lambda j: (j, 0))
    return pl.pallas_call(
        body, grid=(c // tc,), in_specs=[pl.BlockSpec((k_parts, r, tc), lambda j: (0, 0, j)), blk, blk, blk],
        out_specs=[blk] * 4, out_shape=[SDS((c, r), F32)] * 4, name=name, compiler_params=_cp(1))(parts, w, m, v)


def _comm_scratch():
    return [pltpu.SemaphoreType.DMA((7,)), pltpu.SemaphoreType.DMA((7,)), pltpu.SemaphoreType.DMA]


HBM_SPEC = pl.BlockSpec(memory_space=pl.ANY)


def _gather_phases(x_ref, out_ref, send_sems, recv_sems, local_sem):
    mx, my, mc = lax.axis_index("x"), lax.axis_index("y"), lax.axis_index("c")
    me, sibling = (mx, my, mc), (mx, my, 1 - mc)
    chips = [(1 - mx, my), (mx, 1 - my), (1 - mx, 1 - my)]

    def slot(px, py, pc):
        return out_ref.at[4 * px + 2 * py + pc]

    def copy(k, block, to, src=None):
        return pltpu.make_async_remote_copy(
            src_ref=slot(*block) if src is None else src, dst_ref=slot(*block), send_sem=send_sems.at[k],
            recv_sem=recv_sems.at[k], device_id=to, device_id_type=pl.DeviceIdType.MESH)

    def first():
        return [copy(0, me, sibling, src=x_ref)] + [copy(1 + j, me, (*chip, mc), src=x_ref) for j, chip in enumerate(chips)]

    def passed():
        return [copy(4 + j, (*chip, mc), sibling) for j, chip in enumerate(chips)]

    def start():
        pltpu.make_async_copy(x_ref, slot(*me), local_sem).start()
        for cp in first():
            cp.start()

    def forward():
        for j, chip in enumerate(chips):
            copy(1 + j, (*chip, mc), me).wait_recv()
            passed()[j].start()

    def finish():
        copy(0, sibling, me).wait_recv()
        for j, chip in enumerate(chips):
            copy(4 + j, (*chip, 1 - mc), me).wait_recv()
        for cp in first() + passed():
            cp.wait_send()
        pltpu.make_async_copy(x_ref, slot(*me), local_sem).wait()

    return start, forward, finish


def _exchange_phases(x_ref, out_ref, send_sems, recv_sems, local_sem):
    mx, my, mc = lax.axis_index("x"), lax.axis_index("y"), lax.axis_index("c")
    me = 4 * mx + 2 * my + mc

    def peer(k):
        return mx ^ (k >> 2), my ^ ((k >> 1) & 1), mc ^ (k & 1)

    def sends():
        out = []
        for k in range(1, NDEV):
            px, py, pc = peer(k)
            out.append(pltpu.make_async_remote_copy(
                src_ref=x_ref.at[4 * px + 2 * py + pc], dst_ref=out_ref.at[me], send_sem=send_sems.at[k - 1],
                recv_sem=recv_sems.at[k - 1], device_id=(px, py, pc), device_id_type=pl.DeviceIdType.MESH))
        return out

    def start():
        pltpu.make_async_copy(x_ref.at[me], out_ref.at[me], local_sem).start()
        for cp in sends():
            cp.start()

    def finish():
        for k in range(1, NDEV):
            px, py, pc = peer(k)
            pltpu.make_async_remote_copy(
                src_ref=x_ref.at[me], dst_ref=out_ref.at[4 * px + 2 * py + pc], send_sem=send_sems.at[k - 1],
                recv_sem=recv_sems.at[k - 1], device_id=(px, py, pc), device_id_type=pl.DeviceIdType.MESH).wait_recv()
        for cp in sends():
            cp.wait_send()
        pltpu.make_async_copy(x_ref.at[me], out_ref.at[me], local_sem).wait()

    return start, finish


def all_gather(name, x):
    def body(x_ref, out_ref, send_sems, recv_sems, local_sem):
        for phase in _gather_phases(x_ref, out_ref, send_sems, recv_sems, local_sem):
            phase()

    return pl.pallas_call(body, out_shape=SDS((NDEV,) + x.shape, x.dtype), in_specs=[HBM_SPEC], out_specs=HBM_SPEC,
                          scratch_shapes=_comm_scratch(), name=name)(x)


def all_gather_pair(name, x1, x2):
    def body(x1_ref, x2_ref, o1_ref, o2_ref, *sems):
        first = _gather_phases(x1_ref, o1_ref, *sems[:3])
        second = _gather_phases(x2_ref, o2_ref, *sems[3:])
        for phase1, phase2 in zip(first, second):
            phase1()
            phase2()

    return pl.pallas_call(
        body, out_shape=[SDS((NDEV,) + x1.shape, x1.dtype), SDS((NDEV,) + x2.shape, x2.dtype)], in_specs=[HBM_SPEC] * 2,
        out_specs=[HBM_SPEC] * 2, scratch_shapes=_comm_scratch() + _comm_scratch(), name=name)(x1, x2)


def all_to_all(name, x):
    def body(x_ref, out_ref, send_sems, recv_sems, local_sem):
        for phase in _exchange_phases(x_ref, out_ref, send_sems, recv_sems, local_sem):
            phase()

    return pl.pallas_call(body, out_shape=SDS(x.shape, x.dtype), in_specs=[HBM_SPEC], out_specs=HBM_SPEC,
                          scratch_shapes=_comm_scratch(), name=name)(x)


def _pack(arrs, dtype, row_mult=8):
    segs = []
    for a in arrs:
        flat = a.reshape(-1).astype(dtype)
        segs.append(jnp.pad(flat, (0, (-flat.shape[0]) % ROW)))
    flat = jnp.concatenate(segs)
    flat = jnp.pad(flat, (0, (-flat.shape[0]) % (ROW * row_mult)))
    return flat.reshape(-1, ROW)


def _unpack(buf, shapes):
    flat = buf.reshape(-1)
    out, off = [], 0
    for s in shapes:
        n = math.prod(s)
        out.append(flat[off:off + n].reshape(s))
        off += n + (-n) % ROW
    return out


def _pack_rows(arrs, axis):
    padded = []
    for t in arrs:
        pad = [(0, 0)] * t.ndim
        pad[axis] = (0, _tile_rows(t.shape[axis]) - t.shape[axis])
        padded.append(jnp.pad(t, pad))
    return jnp.concatenate(padded, axis=axis)


def _tile_rows(r):
    return r + (-r) % BF16_TILE_ROWS


def _unpack8(buf, shapes):
    flat = buf.reshape(NDEV, -1)
    out, off = [], 0
    for s in shapes:
        n = math.prod(s)
        out.append(flat[:, off:off + n].reshape((NDEV,) + tuple(s)))
        off += n + (-n) % ROW
    return out


def kernel(x, c, w_ada, b_ada, g_ffn1, w1_ffn1, w3_ffn1, w2_ffn1, g_mix, w_in, conv_qkv, a_log, dt_bias, g_onorm, lam_re, lam_im, log_step, b_re, b_im, c_re, c_im, d_skip, w_glu, b_glu, w_proj_a, w_proj_b, w_out, g_ffn2, w1_ffn2, w3_ffn2, w2_ffn2, g_final, loss_target, m_w_ada, m_b_ada, m_g_ffn1, m_w1_ffn1, m_w3_ffn1, m_w2_ffn1, m_g_mix, m_w_in, m_conv_qkv, m_a_log, m_dt_bias, m_g_onorm, m_lam_re, m_lam_im, m_log_step, m_b_re, m_b_im, m_c_re, m_c_im, m_d_skip, m_w_glu, m_b_glu, m_w_proj_a, m_w_proj_b, m_w_out, m_g_ffn2, m_w1_ffn2, m_w3_ffn2, m_w2_ffn2, m_g_final, v_w_ada, v_b_ada, v_g_ffn1, v_w1_ffn1, v_w3_ffn1, v_w2_ffn1, v_g_mix, v_w_in, v_conv_qkv, v_a_log, v_dt_bias, v_g_onorm, v_lam_re, v_lam_im, v_log_step, v_b_re, v_b_im, v_c_re, v_c_im, v_d_skip, v_w_glu, v_b_glu, v_w_proj_a, v_w_proj_b, v_w_out, v_g_ffn2, v_w1_ffn2, v_w3_ffn2, v_w2_ffn2, v_g_final):
    a = dict(locals())
    bl, seq, _ = x.shape
    t_rows = bl * seq
    me = 4 * lax.axis_index("x") + 2 * lax.axis_index("y") + lax.axis_index("c")
    tm_ew = _pick(seq, (256, 128, 64))

    loc = {n: (a[n][0].T if n in COL_SHARDED else a[n][0]) for n in RS_WEIGHTS}
    wfull, gw, res = {}, {}, {}

    def pack_local(names):
        return _pack_rows([loc[n].astype(BF16).reshape(-1, ROW) for n in names], 0)

    def unpack_full(buf, names):
        r0 = 0
        for n in names:
            r = loc[n].size // ROW
            wfull[n] = buf[:, r0:r0 + r, :].reshape(-1, loc[n].shape[1])
            r0 += _tile_rows(r)

    def pack_grads(names):
        return _pack_rows([gw[n].astype(BF16).reshape(NDEV, -1, ROW) for n in names], 1)

    def update(buf, names):
        r0 = 0
        for n in names:
            r = loc[n].size // ROW
            parts = buf[:, r0:r0 + r, :].reshape((NDEV,) + loc[n].shape)
            r0 += _tile_rows(r)
            step = adamw_t if n in COL_SHARDED else adamw
            out = step("adamw_" + n, parts, a[n][0], a["m_" + n][0], a["v_" + n][0])
            for kind, t in zip(("grad", "delta", "new_m", "new_v"), out):
                res[kind + "_" + n] = t[None]

    sm, wg_ffn1 = all_gather_pair("gather_inputs", _pack([c, conv_qkv[0]], F32), pack_local(G_FFN1))
    unpack_full(wg_ffn1, G_FFN1)
    c_loc, conv_loc = _unpack8(sm, [c.shape, conv_qkv.shape[1:]])
    c_all = c_loc.reshape(NDEV * bl, D)
    conv_full = conv_loc.transpose(1, 0, 2).reshape(CONVW, 3 * DNW)

    n_ada = w_ada.shape[2]
    mod_part = ada_fwd(c_all, w_ada[0], lax.dynamic_slice(b_ada, (0, me * n_ada), (1, n_ada)))
    mod_all = all_gather("gather_mod", mod_part).transpose(1, 0, 2).reshape(NDEV * bl, 9 * D)
    mod = lax.dynamic_slice(mod_all, (me * bl, 0), (bl, 9 * D)).reshape(bl, 9, D)
    mods = [mod[:, k:k + 1, :] for k in range(9)]

    h0 = x.reshape(t_rows, D)
    h1, f1, u1, pa1, pb1, wg_rest = ffn_fwd("ffn1_fwd", h0, mod[:, 0:3, :], g_ffn1, wfull['w1_ffn1'], wfull['w3_ffn1'],
                                  wfull['w2_ffn1'], seq, gather=pack_local(G_MIX))
    unpack_full(wg_rest, G_MIX)
    win = wfull['w_in']
    o_small, o_s5, o_gate = 4 * DNW, 4 * DNW + 2 * NH, 4 * DNW + 2 * NH + S5W
    w_dn, w_small = win[:o_small], jnp.pad(win[o_small:o_s5], ((0, LANES - 2 * NH), (0, 0)))
    w_s5, w_gate = win[o_s5:o_gate], win[o_gate:]
    w_pieces = [w_dn, w_small, w_s5, w_gate]
    u2, p_dn, p_small, p_s5, p_gate = mix_in_fwd(h1, mods[3], mods[4], g_mix, w_pieces, seq)

    conv8 = jnp.pad(conv_full, ((0, 8 - CONVW), (0, 0)))
    alp = jnp.pad(a_log, ((0, 0), (NH, LANES - 2 * NH)))
    dtp = jnp.pad(dt_bias, ((0, 0), (NH, LANES - 2 * NH)))
    nb_dn = DN_ROWS if bl % DN_ROWS == 0 else 1
    p_dn3, p_small3 = p_dn.reshape(bl, seq, 4 * DNW), p_small.reshape(bl, seq, LANES)
    qkv3 = dn_prep_fwd(p_dn3, conv8)
    o_pre3, sprev, tinv, wg_ffn2 = deltanet_fwd(qkv3, p_small3, alp, dtp, nb_dn, gather=pack_local(G_FFN2))
    unpack_full(wg_ffn2, G_FFN2)
    o_pre = o_pre3.reshape(t_rows, DNW)
    z_raw = p_dn[:, 3 * DNW:]

    s5_params = [lam_re.reshape(1, S5N), lam_im.reshape(1, S5N), log_step,
                 b_re[0].transpose(2, 0, 1).reshape(S5C, S5N), b_im[0].transpose(2, 0, 1).reshape(S5C, S5N),
                 c_re[0].transpose(1, 0, 2).reshape(S5C, S5N), c_im[0].transpose(1, 0, 2).reshape(S5C, S5N)]
    tables = s5_tables_fwd(s5_params)
    p_s53 = p_s5.reshape(bl, seq, S5W)
    y_s53, xs = s5_fwd(p_s53, tables, d_skip)
    y_s5 = y_s53.reshape(t_rows, S5W)
    tail_in = [o_pre, z_raw, y_s5, p_gate]
    tail_w = [g_onorm, wfull['w_glu'], b_glu, wfull['w_proj_a'], wfull['w_proj_b']]
    (merged,) = ew_call("mix_tail", fn_mix_tail, tail_in, [], tail_w, [(D, BF16)], tm_ew, seq)
    mo, h2 = mix_out_fwd(merged, wfull['w_out'], h1, mods[5], seq)
    h3, f3, u3, pa3, pb3 = ffn_fwd("ffn2_fwd", h2, mod[:, 6:9, :], g_ffn2, wfull['w1_ffn2'], wfull['w3_ffn2'], wfull['w2_ffn2'], seq)

    dh3, dg_final, loss_part = loss_head(h3, loss_target.reshape(t_rows, D), g_final.reshape(1, D), seq)

    dh2, a3, d1_3, d3_3, df3, dmod_c, dg_ffn2 = ffn_bwd("ffn2_bwd", dh3, h2, f3, pa3, pb3, mod[:, 6:9, :], g_ffn2, wfull['w1_ffn2'],
                                                   wfull['w3_ffn2'], wfull['w2_ffn2'], seq)
    gw['w1_ffn2'] = mm_tn("gw1_ffn2", d1_3, u3)
    gw['w3_ffn2'] = mm_tn("gw3_ffn2", d3_3, u3)
    gw['w2_ffn2'] = mm_tn("gw2_ffn2", a3, df3)

    dmo, d_merged, dgt2 = mix_out_bwd(dh2, mo, wfull['w_out'], mods[5], seq)
    gw['w_out'] = mm_tn("gw_out", merged, dmo)
    (d_opre, d_z, d_ys5, d_gate), _, tail_gw = ew_vjp_call(
        "mix_tail_bwd", fn_mix_tail, tail_in, [], tail_w, [d_merged], [(0, F32), (1, F32), (2, F32), (3, BF16)],
        _pick(seq, (512, 256, 128, 64)), seq)
    dg_onorm, gw['w_glu'], dg_bglu, gw['w_proj_a'], gw['w_proj_b'] = tail_gw
    d_qkv3, d_psmall3, d_alp, d_dtp, rs_ffn2 = deltanet_bwd(
        qkv3, p_small3, alp, dtp, sprev, tinv, d_opre.reshape(bl, seq, DNW), nb_dn, exchange=pack_grads(G_FFN2))
    d_pdn3, d_conv8 = dn_prep_bwd(p_dn3, conv8, d_qkv3, d_z.reshape(bl, seq, DNW))
    d_pdn, d_psmall = d_pdn3.reshape(t_rows, 4 * DNW), d_psmall3.reshape(t_rows, LANES)

    s5_out = s5_bwd(p_s53, tables, d_skip, xs, d_ys5.reshape(bl, seq, S5W))
    d_ps5, d_tables, dg_dskip = s5_out[0].reshape(t_rows, S5W), s5_out[1:11], s5_out[11]
    d_s5p = s5_tables_bwd(s5_params, d_tables)

    gw['w_in'] = jnp.concatenate([mm_tn("gw_dn", d_pdn, u2), mm_tn("gw_small", d_psmall, u2)[:2 * NH],
                                  mm_tn("gw_s5", d_ps5, u2), mm_tn("gw_gate", d_gate, u2)], axis=0)
    dh1, dsh2, dsc2, dg_mix = mix_in_bwd([d_pdn, d_psmall, d_ps5, d_gate], w_pieces, h1, mods[3], mods[4], g_mix, dh2, seq)

    dh0, a1, d1_1, d3_1, df1, dmod_a, dg_ffn1, rs_mix = ffn_bwd(
        "ffn1_bwd", dh1, h0, f1, pa1, pb1, mod[:, 0:3, :], g_ffn1, wfull['w1_ffn1'], wfull['w3_ffn1'], wfull['w2_ffn1'], seq,
        exchange=pack_grads(G_MIX))
    dmod_mine = jnp.concatenate([dmod_a, dsh2, dsc2, dgt2, dmod_c], axis=1).reshape(bl, 9 * D)
    small_grads = {
        'g_ffn1': dg_ffn1, 'g_mix': dg_mix, 'a_log': d_alp[:, NH:2 * NH], 'dt_bias': d_dtp[:, NH:2 * NH],
        'g_onorm': dg_onorm, 'lam_re': d_s5p[0].reshape(1, S5G, S5P), 'lam_im': d_s5p[1].reshape(1, S5G, S5P),
        'log_step': d_s5p[2],
        'b_re': d_s5p[3].reshape(S5C, S5G, S5P).transpose(1, 2, 0)[None],
        'b_im': d_s5p[4].reshape(S5C, S5G, S5P).transpose(1, 2, 0)[None],
        'c_re': d_s5p[5].reshape(S5C, S5G, S5P).transpose(1, 0, 2)[None],
        'c_im': d_s5p[6].reshape(S5C, S5G, S5P).transpose(1, 0, 2)[None],
        'd_skip': dg_dskip, 'b_glu': dg_bglu, 'g_ffn2': dg_ffn2, 'g_final': dg_final.reshape(D)}
    small_shapes = [a[n].shape for n in SMALL]
    small_pack = _pack([small_grads[n] for n in SMALL] + [loss_part], F32)
    n_small = small_pack.shape[0]
    small_buf = jnp.concatenate([small_pack, _pack([dmod_mine, d_conv8[:CONVW]], F32)], axis=0)

    gw['w1_ffn1'], sg = mm_tn("gw1_ffn1", d1_1, u1, gather=small_buf)
    gw['w3_ffn1'], rs_w1 = mm_tn("gw3_ffn1", d3_1, u1, exchange=pack_grads(['w1_ffn1']))
    gw['w2_ffn1'], rs_w3 = mm_tn("gw2_ffn1", a1, df1, exchange=pack_grads(['w3_ffn1']))
    rs_w2 = all_to_all("scatter_w2_ffn1", pack_grads(['w2_ffn1']))

    update(rs_ffn2, G_FFN2)
    update(rs_mix, G_MIX)
    update(rs_w1, ['w1_ffn1'])
    update(rs_w3, ['w3_ffn1'])
    update(rs_w2, ['w2_ffn1'])
    pieces = _unpack8(sg[:, n_small:, :], [dmod_mine.shape, (CONVW, 3 * DNW)])
    dmod_all = pieces[0].reshape(NDEV * bl, 9 * D)
    g_wada, g_bada = ada_bwd(c_all, lax.dynamic_slice(dmod_all, (0, me * n_ada), (NDEV * bl, n_ada)), dmod_all)

    n_conv = conv_qkv.shape[2]
    conv_parts = lax.dynamic_slice(pieces[1], (0, 0, me * n_conv), (NDEV, CONVW, n_conv))
    conv_parts = jnp.pad(conv_parts.reshape(NDEV, 1, -1), ((0, 0), (0, 7), (0, 0)))
    pad8 = lambda t: jnp.pad(t.reshape(1, -1), ((0, 7), (0, 0)))
    conv_res = adamw("adamw_conv", conv_parts, pad8(conv_qkv), pad8(m_conv_qkv), pad8(v_conv_qkv))
    for kind, buf in zip(("grad", "delta", "new_m", "new_v"), conv_res):
        res[kind + "_conv_qkv"] = buf[0].reshape(conv_qkv.shape)

    no_param = jnp.zeros_like(loss_part)
    small_res = adamw("adamw_small", sg[:, :n_small, :],
                      *[_pack([a[p + n] for n in SMALL] + [no_param], F32) for p in ("", "m_", "v_")])
    for kind, buf in zip(("grad", "delta", "new_m", "new_v"), small_res):
        for n, t in zip(SMALL, _unpack(buf, small_shapes)):
            res[kind + "_" + n] = t
    loss = _unpack(small_res[0], small_shapes + [loss_part.shape])[-1][0, 0]

    for n, g in (("w_ada", g_wada), ("b_ada", g_bada)):
        shp = a[n].shape
        r2 = lambda t: t.reshape(-1, shp[-1]) if n == "w_ada" else pad8(t)
        out = adamw("adamw_" + n, r2(g)[None], r2(a[n]), r2(a["m_" + n]), r2(a["v_" + n]))
        for kind, buf in zip(("grad", "delta", "new_m", "new_v"), out):
            res[kind + "_" + n] = (buf if n == "w_ada" else buf[0:1]).reshape(shp)

    outs = [loss, dh0.reshape(x.shape)]
    for kind in ("grad", "delta", "new_m", "new_v"):
        outs += [res[kind + "_" + n] for n in WEIGHTS]
    return tuple(outs)
```

```python
import math

import jax
import jax.numpy as jnp
from jax import lax
from jax.experimental import pallas as pl
from jax.experimental.pallas import tpu as pltpu

F32 = jnp.float32
BF16 = jnp.bfloat16
HI = lax.Precision.HIGHEST
H3 = lax.Precision.HIGH
SDS = jax.ShapeDtypeStruct

D = 1024
FF = 2816
FFN_TF = FF
FFN_FWD_TM = 256
FFN_BWD_TM = 256
NH = 8
DH = 64
DNW = NH * DH
CONVW = 4
CH = 64
S5_CH = 128
ACC_LIMIT = 6 * 1024 * 1024
BF16_TILE_ROWS = 16
DN_ROWS = 4
S5W = 512
S5G = 32
S5P = 64
S5C = 16
S5N = S5G * S5P
GB = 4
NDEV = 8
EPS = 1e-6
LANES = 128
ROW = 1024
VMEM_LIMIT = 56 * 1024 * 1024

ADAM_LR, ADAM_B1, ADAM_B2, ADAM_EPS, ADAM_WD, ADAM_STEP = 0.001, 0.9, 0.999, 1e-08, 0.01, 10

WEIGHTS = ['w_ada', 'b_ada', 'g_ffn1', 'w1_ffn1', 'w3_ffn1', 'w2_ffn1', 'g_mix', 'w_in', 'conv_qkv', 'a_log',
           'dt_bias', 'g_onorm', 'lam_re', 'lam_im', 'log_step', 'b_re', 'b_im', 'c_re', 'c_im', 'd_skip', 'w_glu',
           'b_glu', 'w_proj_a', 'w_proj_b', 'w_out', 'g_ffn2', 'w1_ffn2', 'w3_ffn2', 'w2_ffn2', 'g_final']
RS_WEIGHTS = ['w1_ffn1', 'w3_ffn1', 'w2_ffn1', 'w_in', 'w_glu', 'w_proj_a', 'w_proj_b', 'w_out', 'w1_ffn2', 'w3_ffn2',
              'w2_ffn2']
COL_SHARDED = {'w1_ffn1', 'w3_ffn1', 'w_in', 'w_proj_a', 'w_proj_b', 'w1_ffn2', 'w3_ffn2'}
G_FFN1 = ['w1_ffn1', 'w3_ffn1', 'w2_ffn1']
G_MIX = ['w_in', 'w_glu', 'w_proj_a', 'w_proj_b', 'w_out']
G_FFN2 = ['w1_ffn2', 'w3_ffn2', 'w2_ffn2']
SMALL = ['g_ffn1', 'g_mix', 'a_log', 'dt_bias', 'g_onorm', 'lam_re', 'lam_im', 'log_step', 'b_re', 'b_im', 'c_re',
         'c_im', 'd_skip', 'b_glu', 'g_ffn2', 'g_final']


def _cp(n_grid=0):
    if n_grid:
        return pltpu.CompilerParams(vmem_limit_bytes=VMEM_LIMIT, dimension_semantics=("arbitrary",) * n_grid)
    return pltpu.CompilerParams(vmem_limit_bytes=VMEM_LIMIT)


def _dot(a, b):
    return jnp.dot(a.astype(BF16), b.astype(BF16), preferred_element_type=F32)


def _dot_nt(a, b):
    return lax.dot_general(a.astype(BF16), b.astype(BF16), (((1,), (1,)), ((), ())), preferred_element_type=F32)


def _dot_tn(a, b):
    return lax.dot_general(a.astype(BF16), b.astype(BF16), (((0,), (0,)), ((), ())), preferred_element_type=F32)


def _dot_hi(a, b):
    return jnp.dot(a, b, precision=HI, preferred_element_type=F32)


@jax.custom_vjp
def bdot(a, b):
    return _dot(a, b)


bdot.defvjp(lambda a, b: (_dot(a, b), (a, b)),
            lambda r, g: (_dot_nt(g, r[1]).astype(r[0].dtype), _dot_tn(r[0], g).astype(r[1].dtype)))


@jax.custom_vjp
def bdot_nt(a, b):
    return _dot_nt(a, b)


bdot_nt.defvjp(lambda a, b: (_dot_nt(a, b), (a, b)),
               lambda r, g: (_dot(g, r[1]).astype(r[0].dtype), _dot_tn(g, r[0]).astype(r[1].dtype)))


def _silu(x):
    return x * jax.nn.sigmoid(x)


def _iota2(shape, axis):
    return lax.broadcasted_iota(jnp.int32, shape, axis)


def normmod(h, g, sc, sh):
    y = h * lax.rsqrt(jnp.mean(h * h, axis=-1, keepdims=True) + EPS) * g
    return y * (1.0 + sc) + sh


def fn_merge(gate, ya, yb):
    return (jax.nn.sigmoid(gate[:, :D]) * ya + jax.nn.sigmoid(gate[:, D:]) * yb,)


def fn_glu(y, w, b):
    ge = jax.nn.gelu(y)
    return (ge * jax.nn.sigmoid(bdot(ge, w) + b),)


def fn_onorm(o, z, g_on):
    r = _iota2((DH, DNW), 0)
    c = _iota2((DH, DNW), 1)
    expand = (c % DH == r).astype(F32)
    r2 = _iota2((DNW, DNW), 0)
    c2 = _iota2((DNW, DNW), 1)
    avg = (r2 // DH == c2 // DH).astype(F32) * (1.0 / DH)
    ms = bdot(o * o, avg)
    return (o * lax.rsqrt(ms + EPS) * _dot_hi(g_on, expand) * _silu(z),)


def fn_mix_tail(o_pre, z, y_s5, gate, g_on, w_glu, b_glu, wa_t, wb_t):
    (oa,) = fn_onorm(o_pre, z, g_on)
    (ob,) = fn_glu(y_s5, w_glu, b_glu)
    return fn_merge(gate, bdot_nt(oa, wa_t), bdot_nt(ob, wb_t))


def gate_fn(small, alp, dtp):
    beta = jax.nn.sigmoid(small)
    la = -jnp.exp(alp) * jax.nn.softplus(small + dtp)
    tri = (_iota2((CH, CH), 0) >= _iota2((CH, CH), 1)).astype(F32)
    gc = _dot_hi(tri, la)
    gct = lax.dot_general(la, tri, (((0,), (1,)), ((), ())), precision=HI, preferred_element_type=F32)
    return beta, gc, gct


def _bdg(a, b, ca, cb, hi):
    if not hi:
        a, b = a.astype(BF16), b.astype(BF16)
    return lax.dot_general(a, b, (((ca,), (cb,)), ((0,), (0,))), precision=H3 if hi else None,
                           preferred_element_type=F32)


def _batched_matmuls(hi):
    nn_ = lambda a, b: _bdg(a, b, 2, 1, hi)
    nt_ = lambda a, b: _bdg(a, b, 2, 2, hi)
    tn_ = lambda a, b: _bdg(a, b, 1, 1, hi)
    nn = jax.custom_vjp(nn_)
    nn.defvjp(lambda a, b: (nn_(a, b), (a, b)), lambda r, g: (nt_(g, r[1]), tn_(r[0], g)))
    nt = jax.custom_vjp(nt_)
    nt.defvjp(lambda a, b: (nt_(a, b), (a, b)), lambda r, g: (nn_(g, r[1]), tn_(g, r[0])))
    tn = jax.custom_vjp(tn_)
    tn.defvjp(lambda a, b: (tn_(a, b), (a, b)), lambda r, g: (nt_(r[1], g), nn_(r[0], g)))
    return nn, nt, tn


bnn, bnt, btn = _batched_matmuls(False)
hnn, hnt, htn = _batched_matmuls(True)


def _unit_lower_inverse(a):
    r = _iota2((1, CH, CH), 1)
    c = _iota2((1, CH, CH), 2)
    eye = (r == c).astype(F32)
    d = jnp.where(r // 8 == c // 8, a, 0.0)
    inv = eye - d
    p = d
    for _ in range(2):
        p = hnn(p, p)
        inv = inv + hnn(inv, p)
    for blk in (16, 32, 64):
        off = jnp.where((r // blk == c // blk) & (r // (blk // 2) != c // (blk // 2)), a, 0.0)
        mm = hnn if blk == 16 else bnn
        inv = inv - mm(mm(inv, off), inv)
    return inv


@jax.custom_vjp
def _inverse_given(a, t):
    return t


_inverse_given.defvjp(lambda a, t: (t, t), lambda t, g: (-hnt(htn(t, g), t), jnp.zeros_like(t)))


def dn_prep(xc, w):
    t = xc.shape[0] - 8
    c = xc[5:5 + t] * w[0:1] + xc[6:6 + t] * w[1:2] + xc[7:7 + t] * w[2:3] + xc[8:8 + t] * w[3:4]
    act = _silu(c)
    q, k, v = act[:, :DNW], act[:, DNW:2 * DNW], act[:, 2 * DNW:]
    ones = (_iota2((DNW, DNW), 0) // DH == _iota2((DNW, DNW), 1) // DH).astype(F32)
    q = q * lax.rsqrt(bdot(q * q, ones) + EPS) * (DH ** -0.5)
    k = k * lax.rsqrt(bdot(k * k, ones) + EPS)
    return jnp.concatenate([q, k, v], axis=1)


def dn_chunk(q, k, v, b, g, gt, s_prev, t_saved=None):
    r = _iota2((1, CH, CH), 1)
    c = _iota2((1, CH, CH), 2)
    causal = r >= c
    dec = jnp.where(causal, jnp.exp(jnp.where(causal, g - gt, 0.0)), 0.0)
    kb = k * b
    qk = bnt(jnp.concatenate([q, kb], axis=1), k)
    attn = qk[:, :CH] * dec
    a = jnp.where(r > c, qk[:, CH:] * dec, 0.0)
    tinv = _unit_lower_inverse(a) if t_saved is None else _inverse_given(a, t_saved)
    eg = jnp.exp(g)
    uw = hnn(tinv, jnp.concatenate([v * b, kb * eg], axis=2))
    g_last = g[:, CH - 1:CH]
    ws = bnn(jnp.concatenate([uw[..., DH:], q * eg], axis=1), s_prev)
    v_new = uw[..., :DH] - ws[:, :CH]
    o = ws[:, CH:] + bnn(attn, v_new)
    s_new = s_prev * jnp.exp(g_last) + btn(k * jnp.exp(g_last - g), v_new)
    return o, s_new, tinv


def s5_chunk(u, xp_re, xp_im, bb_re, bb_im, cc_re, cc_im, p0r, p0i, p1r, p1i, pir, pii, dsk):
    nb, ch, _ = u.shape
    u2 = u.reshape(nb * ch, LANES)
    bu_re = bdot(u2, bb_re).reshape(nb, ch, 512)
    bu_im = bdot(u2, bb_im).reshape(nb, ch, 512)
    xt_re = pir * bu_re - pii * bu_im
    xt_im = pir * bu_im + pii * bu_re
    tri = jnp.broadcast_to((_iota2((1, ch, ch), 1) >= _iota2((1, ch, ch), 2)).astype(F32), (nb, ch, ch))
    cs_re = hnn(tri, xt_re)
    cs_im = hnn(tri, xt_im)
    x_re = p0r * cs_re - p0i * cs_im + p1r * xp_re - p1i * xp_im
    x_im = p0r * cs_im + p0i * cs_re + p1r * xp_im + p1i * xp_re
    y = bdot_nt(x_re.reshape(nb * ch, 512), cc_re) - bdot_nt(x_im.reshape(nb * ch, 512), cc_im) + dsk * u2
    return y.reshape(nb, ch, LANES), x_re[:, ch - 1:ch], x_im[:, ch - 1:ch]


def s5_tables(lam_re, lam_im, log_step, bre, bim, cre, cim):
    expand = (_iota2((S5G, S5N), 1) // S5P == _iota2((S5G, S5N), 0)).astype(F32)
    step = _dot_hi(jnp.exp(log_step), expand)
    lre = jnp.minimum(lam_re, -1e-4)
    lr = lre * step
    ang = lam_im * step
    mag = jnp.exp(lr)
    lb_re = mag * jnp.cos(ang)
    lb_im = mag * jnp.sin(ang)
    den = lre * lre + lam_im * lam_im
    coef_re = ((lb_re - 1.0) * lre + lb_im * lam_im) / den
    coef_im = (lb_im * lre - (lb_re - 1.0) * lam_im) / den
    bb_re = coef_re * bre - coef_im * bim
    bb_im = coef_re * bim + coef_im * bre
    j = _iota2((S5_CH, 1), 0).astype(F32)
    jc = j - S5_CH // 2
    e0 = jnp.exp(jc * lr)
    e1 = jnp.exp((j + 1.0) * lr)
    ei = jnp.exp(-jc * lr)
    mask = (_iota2((LANES, 512), 0) // S5C == _iota2((LANES, 512), 1) // S5P).astype(F32)

    def blocks(t):
        return jnp.concatenate([(jnp.tile(t[:, gb * 512:(gb + 1) * 512], (LANES // S5C, 1)) * mask)[None]
                                for gb in range(GB)], axis=0)

    return (blocks(bb_re), blocks(bb_im), blocks(cre), blocks(cim),
            e0 * jnp.cos(jc * ang), e0 * jnp.sin(jc * ang),
            e1 * jnp.cos((j + 1.0) * ang), e1 * jnp.sin((j + 1.0) * ang),
            ei * jnp.cos(jc * ang), -ei * jnp.sin(jc * ang))


def _row_specs(tiled, batch, bcast, tm, tpb):
    specs = [pl.BlockSpec((tm, a.shape[1]), lambda i: (i, 0)) for a in tiled]
    specs += [pl.BlockSpec((None,) + a.shape[1:], lambda i: (i // tpb, 0, 0)) for a in batch]
    specs += [pl.BlockSpec(a.shape, lambda i, nd=a.ndim: (0,) * nd) for a in bcast]
    return specs


def ew_call(name, fn, tiled, batch, bcast, outs, tm, seq):
    t_rows = tiled[0].shape[0]
    n_in = len(tiled) + len(batch) + len(bcast)

    def body(*refs):
        vals = [r[...].astype(F32) for r in refs[:n_in]]
        for r, o in zip(refs[n_in:], fn(*vals)):
            r[...] = o.astype(r.dtype)

    return pl.pallas_call(
        body, grid=(t_rows // tm,), in_specs=_row_specs(tiled, batch, bcast, tm, seq // tm),
        out_specs=[pl.BlockSpec((tm, w), lambda i: (i, 0)) for w, _ in outs],
        out_shape=[SDS((t_rows, w), dt) for w, dt in outs], name=name, compiler_params=_cp(1))(*tiled, *batch, *bcast)


def ew_vjp_call(name, fn, tiled, batch, bcast, cts, want, tm, seq, addend=None):
    t_rows = tiled[0].shape[0]
    tpb = seq // tm
    n_t, n_b, n_c = len(tiled), len(batch), len(bcast)
    n_in = n_t + n_b + n_c
    extra = [] if addend is None else [addend]

    def body(*refs):
        i = pl.program_id(0)
        vals = [r[...].astype(F32) for r in refs[:n_in]]
        ctv = tuple(r[...].astype(F32) for r in refs[n_in:n_in + len(cts)])
        outs = refs[n_in + len(cts) + len(extra):]
        _, vjp = jax.vjp(fn, *vals)
        grads = vjp(ctv)
        for k, (r, (idx, _)) in enumerate(zip(outs[:len(want)], want)):
            g = grads[idx]
            if k == 0 and extra:
                g = g + refs[n_in + len(cts)][...]
            r[...] = g.astype(r.dtype)
        for k in range(n_b):
            r, g = outs[len(want) + k], grads[n_t + k]

            @pl.when(i % tpb == 0)
            def _(r=r, g=g):
                r[...] = g

            @pl.when(i % tpb != 0)
            def _(r=r, g=g):
                r[...] += g
        for k in range(n_c):
            r, g = outs[len(want) + n_b + k], grads[n_t + n_b + k]

            @pl.when(i == 0)
            def _(r=r, g=g):
                r[...] = g

            @pl.when(i != 0)
            def _(r=r, g=g):
                r[...] += g

    out_specs = [pl.BlockSpec((tm, tiled[idx].shape[1]), lambda i: (i, 0)) for idx, _ in want]
    out_specs += [pl.BlockSpec((None,) + a.shape[1:], lambda i: (i // tpb, 0, 0)) for a in batch]
    out_specs += [pl.BlockSpec(a.shape, lambda i, nd=a.ndim: (0,) * nd) for a in bcast]
    out_shape = [SDS(tiled[idx].shape, dt) for idx, dt in want]
    out_shape += [SDS(a.shape, F32) for a in batch] + [SDS(a.shape, F32) for a in bcast]
    res = pl.pallas_call(
        body, grid=(t_rows // tm,),
        in_specs=_row_specs(tiled, batch, bcast, tm, tpb)
        + [pl.BlockSpec((tm, a.shape[1]), lambda i: (i, 0)) for a in list(cts) + extra],
        out_specs=out_specs, out_shape=out_shape, name=name, compiler_params=_cp(1))(*tiled, *batch, *bcast, *cts, *extra)
    return res[:len(want)], res[len(want):len(want) + n_b], res[len(want) + n_b:]


def _pick(n, cands):
    for c in cands:
        if n % c == 0:
            return c
    return n


def mm_tn(name, a, b, exchange=None, gather=None):
    t_rows, m = a.shape
    n = b.shape[1]
    tn = n if n <= 1024 else _pick(n, (1024, 512, 256, 128))
    tm = max([t for t in range(LANES, m + 1, LANES) if m % t == 0 and t * tn * 4 <= ACC_LIMIT] or [m])
    tk = _pick(t_rows, (512, 256, 128, 64))
    grid = (m // tm, n // tn, t_rows // tk)
    extra = [x for x in (exchange, gather) if x is not None]
    ne = len(extra)

    def body(*refs):
        a_ref, b_ref = refs[:2]
        o_ref, acc = refs[2 + ne], refs[3 + 2 * ne]
        i, j, k = pl.program_id(0), pl.program_id(1), pl.program_id(2)
        first = (i == 0) & (j == 0) & (k == 0)
        middle = (i == grid[0] - 1) & (j == grid[1] - 1) & (k == grid[2] // 2)
        last = (i == grid[0] - 1) & (j == grid[1] - 1) & (k == grid[2] - 1)
        at_end = []
        for e, x in enumerate(extra):
            comm_refs = (refs[2 + e], refs[3 + ne + e]) + tuple(refs[4 + 2 * ne + 3 * e:7 + 2 * ne + 3 * e])
            if x is exchange:
                start, finish = _exchange_phases(*comm_refs)
                pl.when(first)(start)
            else:
                start, forward, finish = _gather_phases(*comm_refs)
                pl.when(first)(start)
                pl.when(middle)(forward)
            at_end.append(finish)

        @pl.when(k == 0)
        def _():
            acc[...] = jnp.zeros_like(acc)

        acc[...] += _dot_tn(a_ref[...], b_ref[...])

        @pl.when(k == grid[2] - 1)
        def _():
            o_ref[...] = acc[...].astype(BF16)

        for phase in at_end:
            pl.when(last)(phase)

    res = pl.pallas_call(
        body, grid=grid,
        in_specs=[pl.BlockSpec((tk, tm), lambda i, j, k: (k, i)), pl.BlockSpec((tk, tn), lambda i, j, k: (k, j))]
        + [HBM_SPEC] * ne,
        out_specs=[pl.BlockSpec((tm, tn), lambda i, j, k: (i, j))] + [HBM_SPEC] * ne,
        out_shape=[SDS((m, n), BF16)] + [SDS(x.shape if x is exchange else (NDEV,) + x.shape, x.dtype) for x in extra],
        scratch_shapes=[pltpu.VMEM((tm, tn), F32)] + _comm_scratch() * ne, name=name,
        compiler_params=_cp(3))(a, b, *extra)
    return res if extra else res[0]


def _ffn_weight_spec():
    if FFN_TF == FF:
        return pl.BlockSpec((FF, D), lambda i, j: (0, 0), pipeline_mode=pl.Buffered(1))
    return pl.BlockSpec((FFN_TF, D), lambda i, j: (j, 0))


def ffn_fwd(name, h, mod3, g, w1, w3, w2, seq, gather=None, loss_head=None):
    t_rows = h.shape[0]
    tm = _pick(seq, (FFN_FWD_TM, 128, 64))
    tf = FFN_TF
    tpb = seq // tm
    nf = FF // tf
    nt = t_rows // tm
    extra = [] if gather is None else [gather]
    head = [] if loss_head is None else list(loss_head)
    nh, ne = len(head), len(extra)

    def body(*refs):
        h_ref, mod_ref, g_ref, w1_ref, w3_ref, w2_ref = refs[:6]
        o0 = 6 + nh + ne
        ho_ref, f_ref, u_ref, h1_ref, h3_ref = refs[o0:o0 + 5]
        s0 = o0 + 5 + nh + ne
        acc = refs[s0]
        i, j = pl.program_id(0), pl.program_id(1)
        if extra:
            start, forward, finish = _gather_phases(refs[6 + nh], refs[o0 + 5 + nh], *refs[s0 + 1:s0 + 4])
            pl.when((i == 0) & (j == 0))(start)
            pl.when((i == nt // 2) & (j == 0))(forward)

        @pl.when(j == 0)
        def _():
            u_ref[...] = normmod(h_ref[...], g_ref[...], mod_ref[1:2, :], mod_ref[0:1, :]).astype(BF16)
            acc[...] = jnp.zeros_like(acc)

        u = u_ref[...]
        h1 = _dot_nt(u, w1_ref[...])
        h3 = _dot_nt(u, w3_ref[...])
        h1_ref[...] = h1.astype(BF16)
        h3_ref[...] = h3.astype(BF16)
        acc[...] += _dot(_silu(h1) * h3, w2_ref[...])

        @pl.when(j == nf - 1)
        def _():
            f_ref[...] = acc[...]
            h_out = h_ref[...] + 0.5 * mod_ref[2:3, :] * acc[...]
            if not head:
                ho_ref[...] = h_out
            else:
                t_ref, gf_ref, dg_ref, loss_ref = refs[6], refs[7], refs[o0 + 5], refs[o0 + 6]
                y, vjp = jax.vjp(lambda hh, gg: hh * lax.rsqrt(jnp.mean(hh * hh, axis=-1, keepdims=True) + EPS) * gg,
                                 h_out, gf_ref[...])
                e = y - t_ref[...]
                dh, dg = vjp(e * (1.0 / D))
                part = jnp.sum(jnp.sum(e * e, axis=1, keepdims=True), axis=0, keepdims=True) * (0.5 / D) \
                    + jnp.zeros((1, LANES), F32)
                ho_ref[...] = dh

                @pl.when(i == 0)
                def _():
                    dg_ref[...] = dg
                    loss_ref[...] = part

                @pl.when(i != 0)
                def _():
                    dg_ref[...] += dg
                    loss_ref[...] += part

        if extra:
            pl.when((i == nt - 1) & (j == nf - 1))(finish)

    row = lambda i, j: (i, 0)
    const = lambda i, j: (0, 0)
    head_in = [pl.BlockSpec((tm, D), row), pl.BlockSpec((1, D), const)] if head else []
    head_out = [pl.BlockSpec((1, D), const), pl.BlockSpec((1, LANES), const)] if head else []
    return pl.pallas_call(
        body, grid=(nt, nf),
        in_specs=[pl.BlockSpec((tm, D), row), pl.BlockSpec((None, 3, D), lambda i, j: (i // tpb, 0, 0)),
                  pl.BlockSpec((1, D), const), _ffn_weight_spec(), _ffn_weight_spec(), _ffn_weight_spec()]
        + head_in + [HBM_SPEC] * ne,
        out_specs=[pl.BlockSpec((tm, D), row), pl.BlockSpec((tm, D), row), pl.BlockSpec((tm, D), row),
                   pl.BlockSpec((tm, tf), lambda i, j: (i, j)), pl.BlockSpec((tm, tf), lambda i, j: (i, j))]
        + head_out + [HBM_SPEC] * ne,
        out_shape=[SDS((t_rows, D), F32), SDS((t_rows, D), F32), SDS((t_rows, D), BF16), SDS((t_rows, FF), BF16),
                   SDS((t_rows, FF), BF16)] + ([SDS((1, D), F32), SDS((1, LANES), F32)] if head else [])
        + [SDS((NDEV,) + x.shape, x.dtype) for x in extra],
        scratch_shapes=[pltpu.VMEM((tm, D), F32)] + (_comm_scratch() if extra else []), name=name,
        compiler_params=_cp(2))(h, mod3, g, w1, w3, w2, *head, *extra)


def ffn_bwd(name, dho, h, f_out, h1_in, h3_in, mod3, g, w1, w3, w2, seq, exchange=None):
    t_rows = h.shape[0]
    tm = _pick(seq, (FFN_BWD_TM, 128, 64))
    tf = FFN_TF
    tpb = seq // tm
    nf = FF // tf
    nt = t_rows // tm
    extra = [] if exchange is None else [exchange]

    def body(*refs):
        dho_ref, h_ref, f_ref, h1_ref, h3_ref, mod_ref, g_ref, w1_ref, w3_ref, w2_ref = refs[:10]
        dh_ref, a_ref, dh1_ref, dh3_ref, df_scr, dmod_ref, dg_ref = refs[10 + len(extra):17 + len(extra)]
        du_acc = refs[17 + 2 * len(extra)]
        i, j = pl.program_id(0), pl.program_id(1)
        if extra:
            start, finish = _exchange_phases(refs[10], refs[18], *refs[20:23])
            pl.when((i == 0) & (j == 0))(start)

        @pl.when(j == 0)
        def _():
            df_scr[...] = (0.5 * mod_ref[2:3, :] * dho_ref[...]).astype(BF16)
            du_acc[...] = jnp.zeros_like(du_acc)

        h1 = h1_ref[...].astype(F32)
        h3 = h3_ref[...].astype(F32)
        sg = jax.nn.sigmoid(h1)
        s = h1 * sg
        da = _dot_nt(df_scr[...], w2_ref[...])
        dh3 = (da * s).astype(BF16)
        dh1 = (da * h3 * (sg * (1.0 + h1 * (1.0 - sg)))).astype(BF16)
        a_ref[...] = (s * h3).astype(BF16)
        dh1_ref[...] = dh1
        dh3_ref[...] = dh3
        du_acc[...] += _dot(dh1, w1_ref[...]) + _dot(dh3, w3_ref[...])

        @pl.when(j == nf - 1)
        def _():
            _, vjp = jax.vjp(normmod, h_ref[...], g_ref[...], mod_ref[1:2, :], mod_ref[0:1, :])
            dh_n, dg, dsc, dsh = vjp(du_acc[...])
            dh_ref[...] = dho_ref[...] + dh_n
            dgt = jnp.sum(0.5 * dho_ref[...] * f_ref[...], axis=0, keepdims=True)
            dmod = jnp.concatenate([dsh, dsc, dgt], axis=0)

            @pl.when(i % tpb == 0)
            def _():
                dmod_ref[...] = dmod

            @pl.when(i % tpb != 0)
            def _():
                dmod_ref[...] += dmod

            @pl.when(i == 0)
            def _():
                dg_ref[...] = dg

            @pl.when(i != 0)
            def _():
                dg_ref[...] += dg

        if extra:
            pl.when((i == nt - 1) & (j == nf - 1))(finish)

    row = lambda i, j: (i, 0)
    col = lambda i, j: (i, j)
    return pl.pallas_call(
        body, grid=(nt, nf),
        in_specs=[pl.BlockSpec((tm, D), row), pl.BlockSpec((tm, D), row), pl.BlockSpec((tm, D), row),
                  pl.BlockSpec((tm, tf), col), pl.BlockSpec((tm, tf), col),
                  pl.BlockSpec((None, 3, D), lambda i, j: (i // tpb, 0, 0)),
                  pl.BlockSpec((1, D), lambda i, j: (0, 0)), _ffn_weight_spec(), _ffn_weight_spec(), _ffn_weight_spec()]
        + [HBM_SPEC] * len(extra),
        out_specs=[pl.BlockSpec((tm, D), row), pl.BlockSpec((tm, tf), col), pl.BlockSpec((tm, tf), col),
                   pl.BlockSpec((tm, tf), col), pl.BlockSpec((tm, D), row),
                   pl.BlockSpec((None, 3, D), lambda i, j: (i // tpb, 0, 0)), pl.BlockSpec((1, D), lambda i, j: (0, 0))]
        + [HBM_SPEC] * len(extra),
        out_shape=[SDS((t_rows, D), F32), SDS((t_rows, FF), BF16), SDS((t_rows, FF), BF16), SDS((t_rows, FF), BF16),
                   SDS((t_rows, D), BF16), SDS(mod3.shape, F32), SDS((1, D), F32)] + [SDS(x.shape, x.dtype) for x in extra],
        scratch_shapes=[pltpu.VMEM((tm, D), F32)] + (_comm_scratch() if extra else []), name=name,
        compiler_params=_cp(2))(dho, h, f_out, h1_in, h3_in, mod3, g, w1, w3, w2, *extra)


def _resident(shape):
    return pl.BlockSpec(shape, lambda i: (0,) * len(shape), pipeline_mode=pl.Buffered(1))


def mix_in_fwd(h, sh, sc, g, ws, seq):
    t_rows = h.shape[0]
    tm = _pick(seq, (256, 128, 64))
    tpb = seq // tm
    nw = len(ws)

    def body(h_ref, sh_ref, sc_ref, g_ref, *rest):
        u = normmod(h_ref[...], g_ref[...], sc_ref[...], sh_ref[...]).astype(BF16)
        rest[nw][...] = u
        for w_ref, p_ref in zip(rest[:nw], rest[nw + 1:]):
            p_ref[...] = _dot_nt(u, w_ref[...])

    row = lambda i: (i, 0)
    batch = pl.BlockSpec((None, 1, D), lambda i: (i // tpb, 0, 0))
    return pl.pallas_call(
        body, grid=(t_rows // tm,),
        in_specs=[pl.BlockSpec((tm, D), row), batch, batch, pl.BlockSpec((1, D), lambda i: (0, 0))]
        + [_resident(w.shape) for w in ws],
        out_specs=[pl.BlockSpec((tm, D), row)] + [pl.BlockSpec((tm, w.shape[0]), row) for w in ws],
        out_shape=[SDS((t_rows, D), BF16)] + [SDS((t_rows, w.shape[0]), F32) for w in ws], name="mix_in_fwd",
        compiler_params=_cp(1))(h, sh, sc, g, *ws)


def mix_in_bwd(dps, ws, h, sh, sc, g, dh_add, seq):
    t_rows = h.shape[0]
    tm = _pick(seq, (256, 128, 64))
    tpb = seq // tm
    nw = len(ws)

    def body(*refs):
        h_ref, sh_ref, sc_ref, g_ref, add_ref, dh_ref, dsh_ref, dsc_ref, dg_ref = refs[2 * nw:]
        i = pl.program_id(0)
        du = _dot(refs[0][...], refs[nw][...])
        for k in range(1, nw):
            du = du + _dot(refs[k][...], refs[nw + k][...])
        _, vjp = jax.vjp(normmod, h_ref[...], g_ref[...], sc_ref[...], sh_ref[...])
        dh_n, dg, dsc, dsh = vjp(du)
        dh_ref[...] = add_ref[...] + dh_n

        @pl.when(i % tpb == 0)
        def _():
            dsh_ref[...] = dsh
            dsc_ref[...] = dsc

        @pl.when(i % tpb != 0)
        def _():
            dsh_ref[...] += dsh
            dsc_ref[...] += dsc

        @pl.when(i == 0)
        def _():
            dg_ref[...] = dg

        @pl.when(i != 0)
        def _():
            dg_ref[...] += dg

    row = lambda i: (i, 0)
    batch = pl.BlockSpec((None, 1, D), lambda i: (i // tpb, 0, 0))
    gain = pl.BlockSpec((1, D), lambda i: (0, 0))
    return pl.pallas_call(
        body, grid=(t_rows // tm,),
        in_specs=[pl.BlockSpec((tm, dp.shape[1]), row) for dp in dps] + [_resident(w.shape) for w in ws]
        + [pl.BlockSpec((tm, D), row), batch, batch, gain, pl.BlockSpec((tm, D), row)],
        out_specs=[pl.BlockSpec((tm, D), row), batch, batch, gain],
        out_shape=[SDS((t_rows, D), F32), SDS(sh.shape, F32), SDS(sc.shape, F32), SDS((1, D), F32)], name="mix_in_bwd",
        compiler_params=_cp(1))(*dps, *ws, h, sh, sc, g, dh_add)


def mix_out_fwd(merged, w_out, h_prev, gt, seq):
    t_rows = merged.shape[0]
    tm = _pick(seq, (256, 128, 64))
    tpb = seq // tm

    def body(m_ref, w_ref, h_ref, gt_ref, mo_ref, ho_ref):
        mo = _dot(m_ref[...], w_ref[...])
        mo_ref[...] = mo
        ho_ref[...] = h_ref[...] + gt_ref[...] * mo

    row = lambda i: (i, 0)
    return pl.pallas_call(
        body, grid=(t_rows // tm,),
        in_specs=[pl.BlockSpec((tm, D), row), _resident(w_out.shape), pl.BlockSpec((tm, D), row),
                  pl.BlockSpec((None, 1, D), lambda i: (i // tpb, 0, 0))],
        out_specs=[pl.BlockSpec((tm, D), row), pl.BlockSpec((tm, D), row)],
        out_shape=[SDS((t_rows, D), F32), SDS((t_rows, D), F32)], name="mix_out_fwd",
        compiler_params=_cp(1))(merged, w_out, h_prev, gt)


def mix_out_bwd(dh, mo, w_out, gt, seq):
    t_rows = dh.shape[0]
    tm = _pick(seq, (256, 128, 64))
    tpb = seq // tm

    def body(dh_ref, mo_ref, w_ref, gt_ref, dmo_ref, dm_ref, dgt_ref):
        i = pl.program_id(0)
        dmo = (gt_ref[...] * dh_ref[...]).astype(BF16)
        dmo_ref[...] = dmo
        dm_ref[...] = _dot_nt(dmo, w_ref[...])
        dgt = jnp.sum(dh_ref[...] * mo_ref[...], axis=0, keepdims=True)

        @pl.when(i % tpb == 0)
        def _():
            dgt_ref[...] = dgt

        @pl.when(i % tpb != 0)
        def _():
            dgt_ref[...] += dgt

    row = lambda i: (i, 0)
    batch = pl.BlockSpec((None, 1, D), lambda i: (i // tpb, 0, 0))
    return pl.pallas_call(
        body, grid=(t_rows // tm,),
        in_specs=[pl.BlockSpec((tm, D), row), pl.BlockSpec((tm, D), row), _resident(w_out.shape), batch],
        out_specs=[pl.BlockSpec((tm, D), row), pl.BlockSpec((tm, D), row), batch],
        out_shape=[SDS((t_rows, D), BF16), SDS((t_rows, D), F32), SDS(gt.shape, F32)], name="mix_out_bwd",
        compiler_params=_cp(1))(dh, mo, w_out, gt)


def _dn_cols(part, hd):
    return slice(part * DNW + hd * DH, part * DNW + (hd + 1) * DH)


def _qkv_stacks(qkv_ref, nb):
    pairs = [(b, hd) for b in range(nb) for hd in range(NH)]
    return [jnp.stack([qkv_ref[b, :, _dn_cols(part, hd)] for b, hd in pairs]) for part in range(3)]


def dn_prep_fwd(p_dn, conv8):
    bl, seq, _ = p_dn.shape
    tp = _pick(seq, (256, 128, 64))

    def body(raw_ref, halo_ref, conv_ref, o_ref):
        hm = (pl.program_id(1) > 0).astype(F32)
        o_ref[...] = dn_prep(jnp.concatenate([halo_ref[...] * hm, raw_ref[...]], axis=0), conv_ref[...])

    return pl.pallas_call(
        body, grid=(bl, seq // tp),
        in_specs=[pl.BlockSpec((None, tp, 3 * DNW), lambda b, i: (b, i, 0)),
                  pl.BlockSpec((None, 8, 3 * DNW), lambda b, i: (b, jnp.maximum(i * (tp // 8) - 1, 0), 0)),
                  pl.BlockSpec((8, 3 * DNW), lambda b, i: (0, 0))],
        out_specs=pl.BlockSpec((None, tp, 3 * DNW), lambda b, i: (b, i, 0)),
        out_shape=SDS((bl, seq, 3 * DNW), F32), name="dn_prep_fwd", compiler_params=_cp(2))(p_dn, p_dn, conv8)


def dn_prep_bwd(p_dn, conv8, d_qkv, d_z):
    bl, seq, _ = p_dn.shape
    tp = _pick(seq, (256, 128, 64))
    nt = seq // tp

    def body(raw_ref, halo_ref, conv_ref, dq_ref, dz_ref, draw_ref, dconv_ref, carry):
        b, r = pl.program_id(0), pl.program_id(1)

        @pl.when((b == 0) & (r == 0))
        def _():
            dconv_ref[...] = jnp.zeros_like(dconv_ref)

        @pl.when(r == 0)
        def _():
            carry[...] = jnp.zeros_like(carry)

        hm = (r < nt - 1).astype(F32)
        _, vjp = jax.vjp(dn_prep, jnp.concatenate([halo_ref[...] * hm, raw_ref[...]], axis=0), conv_ref[...])
        dxc, dw = vjp(dq_ref[...])
        tail = dxc[tp:tp + 8] + carry[...]
        draw_ref[:, 0:3 * DNW] = jnp.concatenate([dxc[8:tp], tail], axis=0).astype(BF16)
        draw_ref[:, 3 * DNW:4 * DNW] = dz_ref[...].astype(BF16)
        carry[...] = dxc[0:8] * hm
        dconv_ref[...] += dw

    blk = lambda b, r: (b, nt - 1 - r, 0)
    return pl.pallas_call(
        body, grid=(bl, nt),
        in_specs=[pl.BlockSpec((None, tp, 3 * DNW), blk),
                  pl.BlockSpec((None, 8, 3 * DNW), lambda b, r: (b, jnp.maximum((nt - 1 - r) * (tp // 8) - 1, 0), 0)),
                  pl.BlockSpec((8, 3 * DNW), lambda b, r: (0, 0)), pl.BlockSpec((None, tp, 3 * DNW), blk),
                  pl.BlockSpec((None, tp, DNW), blk)],
        out_specs=[pl.BlockSpec((None, tp, 4 * DNW), blk), pl.BlockSpec((8, 3 * DNW), lambda b, r: (0, 0))],
        out_shape=[SDS((bl, seq, 4 * DNW), BF16), SDS((8, 3 * DNW), F32)],
        scratch_shapes=[pltpu.VMEM((8, 3 * DNW), F32)], name="dn_prep_bwd", compiler_params=_cp(2))(p_dn, p_dn, conv8, d_qkv, d_z)


def _gate_stacks(gates, nb):
    pairs = [(b, hd) for b in range(nb) for hd in range(NH)]
    bs = jnp.stack([gates[b][0][:, hd:hd + 1] for b, hd in pairs])
    gs = jnp.stack([gates[b][1][:, NH + hd:NH + hd + 1] for b, hd in pairs])
    gts = jnp.stack([gates[b][2][NH + hd:NH + hd + 1, :] for b, hd in pairs])
    return bs, gs, gts


def deltanet_fwd(qkv, p_small, alp, dtp, nb, gather=None):
    bl, seq, _ = qkv.shape
    nc = seq // CH
    ng = nb * NH
    extra = [] if gather is None else [gather]

    def body(*refs):
        qkv_ref, small_ref, alp_ref, dtp_ref = refs[:4]
        o_ref, sprev_ref, tinv_ref = refs[4 + len(extra):7 + len(extra)]
        s_scr = refs[7 + 2 * len(extra)]
        bb, n = pl.program_id(0), pl.program_id(1)
        if extra:
            start, forward, finish = _gather_phases(refs[4], refs[8], *refs[10:13])
            pl.when((bb == 0) & (n == 0))(start)

        @pl.when(n == 0)
        def _():
            s_scr[...] = jnp.zeros_like(s_scr)

        gates = [gate_fn(small_ref[b], alp_ref[...], dtp_ref[...]) for b in range(nb)]
        s_prev = s_scr[...]
        o, s_new, tinv = dn_chunk(*_qkv_stacks(qkv_ref, nb), *_gate_stacks(gates, nb), s_prev)
        sprev_ref[...] = s_prev
        tinv_ref[...] = tinv
        s_scr[...] = s_new
        for b in range(nb):
            for hd in range(NH):
                o_ref[b, :, hd * DH:(hd + 1) * DH] = o[b * NH + hd]
        if extra:
            pl.when((bb == bl // nb - 1) & (n == nc // 2))(forward)
            pl.when((bb == bl // nb - 1) & (n == nc - 1))(finish)

    blk = lambda bb, n: (bb, n, 0)
    const = lambda bb, n: (0, 0)
    saved = pl.BlockSpec((None, ng, DH, DH), lambda bb, n: (bb * nc + n, 0, 0, 0))
    return pl.pallas_call(
        body, grid=(bl // nb, nc),
        in_specs=[pl.BlockSpec((nb, CH, 3 * DNW), blk), pl.BlockSpec((nb, CH, LANES), blk),
                  pl.BlockSpec((1, LANES), const), pl.BlockSpec((1, LANES), const)] + [HBM_SPEC] * len(extra),
        out_specs=[pl.BlockSpec((nb, CH, DNW), blk), saved, saved] + [HBM_SPEC] * len(extra),
        out_shape=[SDS((bl, seq, DNW), F32), SDS((bl // nb * nc, ng, DH, DH), F32), SDS((bl // nb * nc, ng, DH, DH), F32)]
        + [SDS((NDEV,) + x.shape, x.dtype) for x in extra],
        scratch_shapes=[pltpu.VMEM((ng, DH, DH), F32)] + (_comm_scratch() if extra else []), name="deltanet_fwd",
        compiler_params=_cp(2))(qkv, p_small, alp, dtp, *extra)


def deltanet_bwd(qkv, p_small, alp, dtp, sprev, tinv, d_o, nb, exchange=None):
    bl, seq, _ = qkv.shape
    nc = seq // CH
    ng = nb * NH
    extra = [] if exchange is None else [exchange]

    def body(*refs):
        qkv_ref, small_ref, alp_ref, dtp_ref, sprev_ref, tinv_ref, do_ref = refs[:7]
        dqkv_ref, dsmall_ref, dalp_ref, ddtp_ref = refs[7 + len(extra):11 + len(extra)]
        ds_scr = refs[11 + 2 * len(extra)]
        bb, r = pl.program_id(0), pl.program_id(1)
        if extra:
            start, finish = _exchange_phases(refs[7], refs[12], *refs[14:17])
            pl.when((bb == 0) & (r == 0))(start)

        @pl.when((bb == 0) & (r == 0))
        def _():
            dalp_ref[...] = jnp.zeros_like(dalp_ref)
            ddtp_ref[...] = jnp.zeros_like(ddtp_ref)

        @pl.when(r == 0)
        def _():
            ds_scr[...] = jnp.zeros_like(ds_scr)

        gates, gate_vjps = [], []
        for b in range(nb):
            out, gvjp = jax.vjp(gate_fn, small_ref[b], alp_ref[...], dtp_ref[...])
            gates.append(out)
            gate_vjps.append(gvjp)
        t_saved = tinv_ref[...]
        _, vjp = jax.vjp(lambda *args: dn_chunk(*args, t_saved)[:2], *_qkv_stacks(qkv_ref, nb), *_gate_stacks(gates, nb),
                         sprev_ref[...])
        d_out = jnp.stack([do_ref[b, :, hd * DH:(hd + 1) * DH] for b in range(nb) for hd in range(NH)])
        grads = vjp((d_out, ds_scr[...]))
        ds_scr[...] = grads[6]
        lane = _iota2((CH, LANES), 1)
        rowi = _iota2((LANES, CH), 0)
        for b in range(nb):
            d_beta = jnp.zeros((CH, LANES), F32)
            d_gc = jnp.zeros((CH, LANES), F32)
            d_gct = jnp.zeros((LANES, CH), F32)
            for hd in range(NH):
                i = b * NH + hd
                for part in range(3):
                    dqkv_ref[b, :, _dn_cols(part, hd)] = grads[part][i]
                d_beta = d_beta + jnp.where(lane == hd, grads[3][i], 0.0)
                d_gc = d_gc + jnp.where(lane == NH + hd, grads[4][i], 0.0)
                d_gct = d_gct + jnp.where(rowi == NH + hd, grads[5][i], 0.0)
            d_small, d_alp, d_dtp = gate_vjps[b]((d_beta, d_gc, d_gct))
            dsmall_ref[b] = d_small.astype(BF16)
            dalp_ref[...] += d_alp
            ddtp_ref[...] += d_dtp
        if extra:
            pl.when((bb == bl // nb - 1) & (r == nc - 1))(finish)

    blk = lambda bb, r: (bb, nc - 1 - r, 0)
    const = lambda bb, r: (0, 0)
    saved = pl.BlockSpec((None, ng, DH, DH), lambda bb, r: (bb * nc + nc - 1 - r, 0, 0, 0))
    return pl.pallas_call(
        body, grid=(bl // nb, nc),
        in_specs=[pl.BlockSpec((nb, CH, 3 * DNW), blk), pl.BlockSpec((nb, CH, LANES), blk), pl.BlockSpec((1, LANES), const),
                  pl.BlockSpec((1, LANES), const), saved, saved, pl.BlockSpec((nb, CH, DNW), blk)] + [HBM_SPEC] * len(extra),
        out_specs=[pl.BlockSpec((nb, CH, 3 * DNW), blk), pl.BlockSpec((nb, CH, LANES), blk), pl.BlockSpec((1, LANES), const),
                   pl.BlockSpec((1, LANES), const)] + [HBM_SPEC] * len(extra),
        out_shape=[SDS((bl, seq, 3 * DNW), F32), SDS((bl, seq, LANES), BF16), SDS((1, LANES), F32), SDS((1, LANES), F32)]
        + [SDS(x.shape, x.dtype) for x in extra],
        scratch_shapes=[pltpu.VMEM((ng, DH, DH), F32)] + (_comm_scratch() if extra else []), name="deltanet_bwd",
        compiler_params=_cp(2))(qkv, p_small, alp, dtp, sprev, tinv, d_o, *extra)


def _s5_table_specs():
    tab3 = pl.BlockSpec((None, LANES, 512), lambda gb, n: (gb, 0, 0))
    tab2 = pl.BlockSpec((S5_CH, 512), lambda gb, n: (0, gb))
    return [tab3] * 4 + [tab2] * 6 + [pl.BlockSpec((1, LANES), lambda gb, n: (0, gb))]


def s5_fwd(u, tables, dsk):
    bl, seq, _ = u.shape
    nc = seq // S5_CH

    def body(u_ref, *rest):
        tabs, (y_ref, xs_ref, xr_scr, xi_scr) = rest[:11], rest[11:]

        @pl.when(pl.program_id(1) == 0)
        def _():
            xr_scr[...] = jnp.zeros_like(xr_scr)
            xi_scr[...] = jnp.zeros_like(xi_scr)

        xp_re, xp_im = xr_scr[...], xi_scr[...]
        xs_ref[0:bl] = xp_re
        xs_ref[bl:2 * bl] = xp_im
        y, xn_re, xn_im = s5_chunk(u_ref[...], xp_re, xp_im, *[t[...] for t in tabs])
        y_ref[...] = y
        xr_scr[...] = xn_re
        xi_scr[...] = xn_im

    blk = lambda gb, n: (0, n, gb)
    return pl.pallas_call(
        body, grid=(GB, nc), in_specs=[pl.BlockSpec((bl, S5_CH, LANES), blk)] + _s5_table_specs(),
        out_specs=[pl.BlockSpec((bl, S5_CH, LANES), blk),
                   pl.BlockSpec((None, 2 * bl, 1, 512), lambda gb, n: (gb * nc + n, 0, 0, 0))],
        out_shape=[SDS((bl, seq, S5W), F32), SDS((GB * nc, 2 * bl, 1, 512), F32)],
        scratch_shapes=[pltpu.VMEM((bl, 1, 512), F32), pltpu.VMEM((bl, 1, 512), F32)], name="s5_fwd",
        compiler_params=_cp(2))(u, *tables, dsk)


def s5_bwd(u, tables, dsk, xs, dy):
    bl, seq, _ = u.shape
    nc = seq // S5_CH

    def body(u_ref, *rest):
        tabs, xs_ref, dy_ref = rest[:11], rest[11], rest[12]
        du_ref, dtabs, dxr_scr, dxi_scr = rest[13], rest[14:25], rest[25], rest[26]
        r = pl.program_id(1)

        @pl.when(r == 0)
        def _():
            for t in dtabs:
                t[...] = jnp.zeros_like(t)
            dxr_scr[...] = jnp.zeros_like(dxr_scr)
            dxi_scr[...] = jnp.zeros_like(dxi_scr)

        _, vjp = jax.vjp(s5_chunk, u_ref[...], xs_ref[0:bl], xs_ref[bl:2 * bl], *[t[...] for t in tabs])
        grads = vjp((dy_ref[...], dxr_scr[...], dxi_scr[...]))
        du_ref[...] = grads[0].astype(BF16)
        dxr_scr[...] = grads[1]
        dxi_scr[...] = grads[2]
        for t, g in zip(dtabs, grads[3:]):
            t[...] += g

    blk = lambda gb, r: (0, nc - 1 - r, gb)
    tab_shapes = [SDS(t.shape, F32) for t in tables] + [SDS(dsk.shape, F32)]
    return pl.pallas_call(
        body, grid=(GB, nc),
        in_specs=[pl.BlockSpec((bl, S5_CH, LANES), blk)] + _s5_table_specs()
        + [pl.BlockSpec((None, 2 * bl, 1, 512), lambda gb, r: (gb * nc + nc - 1 - r, 0, 0, 0)), pl.BlockSpec((bl, S5_CH, LANES), blk)],
        out_specs=[pl.BlockSpec((bl, S5_CH, LANES), blk)] + _s5_table_specs(),
        out_shape=[SDS((bl, seq, S5W), BF16)] + tab_shapes,
        scratch_shapes=[pltpu.VMEM((bl, 1, 512), F32), pltpu.VMEM((bl, 1, 512), F32)], name="s5_bwd",
        compiler_params=_cp(2))(u, *tables, dsk, xs, dy)


def s5_tables_fwd(params):
    shapes = [SDS((GB, LANES, 512), F32)] * 4 + [SDS((S5_CH, S5N), F32)] * 6

    def body(*refs):
        for r, t in zip(refs[7:], s5_tables(*[p[...] for p in refs[:7]])):
            r[...] = t

    return pl.pallas_call(body, out_shape=shapes, name="s5_tables_fwd", compiler_params=_cp())(*params)


def s5_tables_bwd(params, dtables):
    def body(*refs):
        _, vjp = jax.vjp(s5_tables, *[p[...] for p in refs[:7]])
        for r, g in zip(refs[17:], vjp(tuple(t[...] for t in refs[7:17]))):
            r[...] = g

    return pl.pallas_call(body, out_shape=[SDS(p.shape, F32) for p in params], name="s5_tables_bwd",
                          compiler_params=_cp())(*params, *dtables)


def ada_fwd(c_all, w_loc, b_loc):
    def body(c_ref, w_ref, b_ref, o_ref):
        o_ref[...] = _dot(_silu(c_ref[...]), w_ref[...]) + b_ref[...]

    return pl.pallas_call(body, out_shape=SDS((c_all.shape[0], w_loc.shape[1]), F32), name="ada_fwd",
                          compiler_params=_cp())(c_all, w_loc, b_loc)


def ada_bwd(c_all, dmod_mine, dmod_all):
    def body(c_ref, dm_ref, da_ref, gw_ref, gb_ref):
        gw_ref[...] = _dot_tn(_silu(c_ref[...]), dm_ref[...])
        gb_ref[...] = jnp.sum(da_ref[...], axis=0, keepdims=True)

    return pl.pallas_call(body, out_shape=[SDS((D, dmod_mine.shape[1]), F32), SDS((1, dmod_all.shape[1]), F32)],
                          name="ada_bwd", compiler_params=_cp())(c_all, dmod_mine, dmod_all)


def adamw(name, parts, w, m, v):
    k_parts, rows, cols = parts.shape
    tr = _pick(rows, (256, 128, 64, 32, 16, 8))

    def body(p_ref, w_ref, m_ref, v_ref, g_ref, d_ref, mo_ref, vo_ref):
        g = p_ref[0].astype(F32)
        for k in range(1, k_parts):
            g = g + p_ref[k].astype(F32)
        _adam_store(g, w_ref, m_ref, v_ref, g_ref, d_ref, mo_ref, vo_ref)

    blk = pl.BlockSpec((tr, cols), lambda i: (i, 0))
    return pl.pallas_call(
        body, grid=(rows // tr,), in_specs=[pl.BlockSpec((k_parts, tr, cols), lambda i: (0, i, 0)), blk, blk, blk],
        out_specs=[blk] * 4, out_shape=[SDS((rows, cols), F32)] * 4, name=name, compiler_params=_cp(1))(parts, w, m, v)


def _adam_store(g, w_ref, m_ref, v_ref, g_ref, d_ref, mo_ref, vo_ref):
    m_new = ADAM_B1 * m_ref[...] + (1.0 - ADAM_B1) * g
    v_new = ADAM_B2 * v_ref[...] + (1.0 - ADAM_B2) * (g * g)
    m_hat = m_new / (1.0 - ADAM_B1 ** ADAM_STEP)
    v_hat = v_new / (1.0 - ADAM_B2 ** ADAM_STEP)
    g_ref[...] = g
    d_ref[...] = -ADAM_LR * (m_hat / (jnp.sqrt(v_hat) + ADAM_EPS) + ADAM_WD * w_ref[...])
    mo_ref[...] = m_new
    vo_ref[...] = v_new


def adamw_t(name, parts, w, m, v):
    k_parts, r, c = parts.shape
    tc = _pick(c, (256, 128))

    def body(p_ref, w_ref, m_ref, v_ref, g_ref, d_ref, mo_ref, vo_ref):
        gt = p_ref[0].astype(F32)
        for k in range(1, k_parts):
            gt = gt + p_ref[k].astype(F32)
        _adam_store(gt.T, w_ref, m_ref, v_ref, g_ref, d_ref, mo_ref, vo_ref)

    blk = pl.BlockSpec((tc, r), lambda j: (j, 0))
    return pl.pallas_call(
        body, grid=(c // tc,), in_specs=[pl.BlockSpec((k_parts, r, tc), lambda j: (0, 0, j)), blk, blk, blk],
        out_specs=[blk] * 4, out_shape=[SDS((c, r), F32)] * 4, name=name, compiler_params=_cp(1))(parts, w, m, v)


def _comm_scratch():
    return [pltpu.SemaphoreType.DMA((7,)), pltpu.SemaphoreType.DMA((7,)), pltpu.SemaphoreType.DMA]


HBM_SPEC = pl.BlockSpec(memory_space=pl.ANY)


def _gather_phases(x_ref, out_ref, send_sems, recv_sems, local_sem):
    mx, my, mc = lax.axis_index("x"), lax.axis_index("y"), lax.axis_index("c")
    me, sibling = (mx, my, mc), (mx, my, 1 - mc)
    chips = [(1 - mx, my), (mx, 1 - my), (1 - mx, 1 - my)]

    def slot(px, py, pc):
        return out_ref.at[4 * px + 2 * py + pc]

    def copy(k, block, to, src=None):
        return pltpu.make_async_remote_copy(
            src_ref=slot(*block) if src is None else src, dst_ref=slot(*block), send_sem=send_sems.at[k],
            recv_sem=recv_sems.at[k], device_id=to, device_id_type=pl.DeviceIdType.MESH)

    def first():
        return [copy(0, me, sibling, src=x_ref)] + [copy(1 + j, me, (*chip, mc), src=x_ref) for j, chip in enumerate(chips)]

    def passed():
        return [copy(4 + j, (*chip, mc), sibling) for j, chip in enumerate(chips)]

    def start():
        pltpu.make_async_copy(x_ref, slot(*me), local_sem).start()
        for cp in first():
            cp.start()

    def forward():
        for j, chip in enumerate(chips):
            copy(1 + j, (*chip, mc), me).wait_recv()
            passed()[j].start()

    def finish():
        copy(0, sibling, me).wait_recv()
        for j, chip in enumerate(chips):
            copy(4 + j, (*chip, 1 - mc), me).wait_recv()
        for cp in first() + passed():
            cp.wait_send()
        pltpu.make_async_copy(x_ref, slot(*me), local_sem).wait()

    return start, forward, finish


def _exchange_phases(x_ref, out_ref, send_sems, recv_sems, local_sem):
    mx, my, mc = lax.axis_index("x"), lax.axis_index("y"), lax.axis_index("c")
    me = 4 * mx + 2 * my + mc

    def peer(k):
        return mx ^ (k >> 2), my ^ ((k >> 1) & 1), mc ^ (k & 1)

    def sends():
        out = []
        for k in range(1, NDEV):
            px, py, pc = peer(k)
            out.append(pltpu.make_async_remote_copy(
                src_ref=x_ref.at[4 * px + 2 * py + pc], dst_ref=out_ref.at[me], send_sem=send_sems.at[k - 1],
                recv_sem=recv_sems.at[k - 1], device_id=(px, py, pc), device_id_type=pl.DeviceIdType.MESH))
        return out

    def start():
        pltpu.make_async_copy(x_ref.at[me], out_ref.at[me], local_sem).start()
        for cp in sends():
            cp.start()

    def finish():
        for k in range(1, NDEV):
            px, py, pc = peer(k)
            pltpu.make_async_remote_copy(
                src_ref=x_ref.at[me], dst_ref=out_ref.at[4 * px + 2 * py + pc], send_sem=send_sems.at[k - 1],
                recv_sem=recv_sems.at[k - 1], device_id=(px, py, pc), device_id_type=pl.DeviceIdType.MESH).wait_recv()
        for cp in sends():
            cp.wait_send()
        pltpu.make_async_copy(x_ref.at[me], out_ref.at[me], local_sem).wait()

    return start, finish


def all_gather(name, x):
    def body(x_ref, out_ref, send_sems, recv_sems, local_sem):
        for phase in _gather_phases(x_ref, out_ref, send_sems, recv_sems, local_sem):
            phase()

    return pl.pallas_call(body, out_shape=SDS((NDEV,) + x.shape, x.dtype), in_specs=[HBM_SPEC], out_specs=HBM_SPEC,
                          scratch_shapes=_comm_scratch(), name=name)(x)


def all_gather_pair(name, x1, x2):
    def body(x1_ref, x2_ref, o1_ref, o2_ref, *sems):
        first = _gather_phases(x1_ref, o1_ref, *sems[:3])
        second = _gather_phases(x2_ref, o2_ref, *sems[3:])
        for phase1, phase2 in zip(first, second):
            phase1()
            phase2()

    return pl.pallas_call(
        body, out_shape=[SDS((NDEV,) + x1.shape, x1.dtype), SDS((NDEV,) + x2.shape, x2.dtype)], in_specs=[HBM_SPEC] * 2,
        out_specs=[HBM_SPEC] * 2, scratch_shapes=_comm_scratch() + _comm_scratch(), name=name)(x1, x2)


def all_to_all(name, x):
    def body(x_ref, out_ref, send_sems, recv_sems, local_sem):
        for phase in _exchange_phases(x_ref, out_ref, send_sems, recv_sems, local_sem):
            phase()

    return pl.pallas_call(body, out_shape=SDS(x.shape, x.dtype), in_specs=[HBM_SPEC], out_specs=HBM_SPEC,
                          scratch_shapes=_comm_scratch(), name=name)(x)


def _pack(arrs, dtype, row_mult=8):
    segs = []
    for a in arrs:
        flat = a.reshape(-1).astype(dtype)
        segs.append(jnp.pad(flat, (0, (-flat.shape[0]) % ROW)))
    flat = jnp.concatenate(segs)
    flat = jnp.pad(flat, (0, (-flat.shape[0]) % (ROW * row_mult)))
    return flat.reshape(-1, ROW)


def _unpack(buf, shapes):
    flat = buf.reshape(-1)
    out, off = [], 0
    for s in shapes:
        n = math.prod(s)
        out.append(flat[off:off + n].reshape(s))
        off += n + (-n) % ROW
    return out


def _pack_rows(arrs, axis):
    padded = []
    for t in arrs:
        pad = [(0, 0)] * t.ndim
        pad[axis] = (0, _tile_rows(t.shape[axis]) - t.shape[axis])
        padded.append(jnp.pad(t, pad))
    return jnp.concatenate(padded, axis=axis)


def _tile_rows(r):
    return r + (-r) % BF16_TILE_ROWS


def _unpack8(buf, shapes):
    flat = buf.reshape(NDEV, -1)
    out, off = [], 0
    for s in shapes:
        n = math.prod(s)
        out.append(flat[:, off:off + n].reshape((NDEV,) + tuple(s)))
        off += n + (-n) % ROW
    return out


def kernel(x, c, w_ada, b_ada, g_ffn1, w1_ffn1, w3_ffn1, w2_ffn1, g_mix, w_in, conv_qkv, a_log, dt_bias, g_onorm, lam_re, lam_im, log_step, b_re, b_im, c_re, c_im, d_skip, w_glu, b_glu, w_proj_a, w_proj_b, w_out, g_ffn2, w1_ffn2, w3_ffn2, w2_ffn2, g_final, loss_target, m_w_ada, m_b_ada, m_g_ffn1, m_w1_ffn1, m_w3_ffn1, m_w2_ffn1, m_g_mix, m_w_in, m_conv_qkv, m_a_log, m_dt_bias, m_g_onorm, m_lam_re, m_lam_im, m_log_step, m_b_re, m_b_im, m_c_re, m_c_im, m_d_skip, m_w_glu, m_b_glu, m_w_proj_a, m_w_proj_b, m_w_out, m_g_ffn2, m_w1_ffn2, m_w3_ffn2, m_w2_ffn2, m_g_final, v_w_ada, v_b_ada, v_g_ffn1, v_w1_ffn1, v_w3_ffn1, v_w2_ffn1, v_g_mix, v_w_in, v_conv_qkv, v_a_log, v_dt_bias, v_g_onorm, v_lam_re, v_lam_im, v_log_step, v_b_re, v_b_im, v_c_re, v_c_im, v_d_skip, v_w_glu, v_b_glu, v_w_proj_a, v_w_proj_b, v_w_out, v_g_ffn2, v_w1_ffn2, v_w3_ffn2, v_w2_ffn2, v_g_final):
    a = dict(locals())
    bl, seq, _ = x.shape
    t_rows = bl * seq
    me = 4 * lax.axis_index("x") + 2 * lax.axis_index("y") + lax.axis_index("c")
    tm_ew = _pick(seq, (256, 128, 64))

    loc = {n: (a[n][0].T if n in COL_SHARDED else a[n][0]) for n in RS_WEIGHTS}
    wfull, gw, res = {}, {}, {}

    def pack_local(names):
        return _pack_rows([loc[n].astype(BF16).reshape(-1, ROW) for n in names], 0)

    def unpack_full(buf, names):
        r0 = 0
        for n in names:
            r = loc[n].size // ROW
            wfull[n] = buf[:, r0:r0 + r, :].reshape(-1, loc[n].shape[1])
            r0 += _tile_rows(r)

    def pack_grads(names):
        return _pack_rows([gw[n].astype(BF16).reshape(NDEV, -1, ROW) for n in names], 1)

    def update(buf, names):
        r0 = 0
        for n in names:
            r = loc[n].size // ROW
            parts = buf[:, r0:r0 + r, :].reshape((NDEV,) + loc[n].shape)
            r0 += _tile_rows(r)
            step = adamw_t if n in COL_SHARDED else adamw
            out = step("adamw_" + n, parts, a[n][0], a["m_" + n][0], a["v_" + n][0])
            for kind, t in zip(("grad", "delta", "new_m", "new_v"), out):
                res[kind + "_" + n] = t[None]

    sm, wg_ffn1 = all_gather_pair("gather_inputs", _pack([c, conv_qkv[0]], F32), pack_local(G_FFN1))
    unpack_full(wg_ffn1, G_FFN1)
    c_loc, conv_loc = _unpack8(sm, [c.shape, conv_qkv.shape[1:]])
    c_all = c_loc.reshape(NDEV * bl, D)
    conv_full = conv_loc.transpose(1, 0, 2).reshape(CONVW, 3 * DNW)

    n_ada = w_ada.shape[2]
    mod_part = ada_fwd(c_all, w_ada[0], lax.dynamic_slice(b_ada, (0, me * n_ada), (1, n_ada)))
    mod_all = all_gather("gather_mod", mod_part).transpose(1, 0, 2).reshape(NDEV * bl, 9 * D)
    mod = lax.dynamic_slice(mod_all, (me * bl, 0), (bl, 9 * D)).reshape(bl, 9, D)
    mods = [mod[:, k:k + 1, :] for k in range(9)]

    h0 = x.reshape(t_rows, D)
    h1, f1, u1, pa1, pb1, wg_rest = ffn_fwd("ffn1_fwd", h0, mod[:, 0:3, :], g_ffn1, wfull['w1_ffn1'], wfull['w3_ffn1'],
                                  wfull['w2_ffn1'], seq, gather=pack_local(G_MIX))
    unpack_full(wg_rest, G_MIX)
    win = wfull['w_in']
    o_small, o_s5, o_gate = 4 * DNW, 4 * DNW + 2 * NH, 4 * DNW + 2 * NH + S5W
    w_dn, w_small = win[:o_small], jnp.pad(win[o_small:o_s5], ((0, LANES - 2 * NH), (0, 0)))
    w_s5, w_gate = win[o_s5:o_gate], win[o_gate:]
    w_pieces = [w_dn, w_small, w_s5, w_gate]
    u2, p_dn, p_small, p_s5, p_gate = mix_in_fwd(h1, mods[3], mods[4], g_mix, w_pieces, seq)

    conv8 = jnp.pad(conv_full, ((0, 8 - CONVW), (0, 0)))
    alp = jnp.pad(a_log, ((0, 0), (NH, LANES - 2 * NH)))
    dtp = jnp.pad(dt_bias, ((0, 0), (NH, LANES - 2 * NH)))
    nb_dn = DN_ROWS if bl % DN_ROWS == 0 else 1
    p_dn3, p_small3 = p_dn.reshape(bl, seq, 4 * DNW), p_small.reshape(bl, seq, LANES)
    qkv3 = dn_prep_fwd(p_dn3, conv8)
    o_pre3, sprev, tinv, wg_ffn2 = deltanet_fwd(qkv3, p_small3, alp, dtp, nb_dn, gather=pack_local(G_FFN2))
    unpack_full(wg_ffn2, G_FFN2)
    o_pre = o_pre3.reshape(t_rows, DNW)
    z_raw = p_dn[:, 3 * DNW:]

    s5_params = [lam_re.reshape(1, S5N), lam_im.reshape(1, S5N), log_step,
                 b_re[0].transpose(2, 0, 1).reshape(S5C, S5N), b_im[0].transpose(2, 0, 1).reshape(S5C, S5N),
                 c_re[0].transpose(1, 0, 2).reshape(S5C, S5N), c_im[0].transpose(1, 0, 2).reshape(S5C, S5N)]
    tables = s5_tables_fwd(s5_params)
    p_s53 = p_s5.reshape(bl, seq, S5W)
    y_s53, xs = s5_fwd(p_s53, tables, d_skip)
    y_s5 = y_s53.reshape(t_rows, S5W)
    tail_in = [o_pre, z_raw, y_s5, p_gate]
    tail_w = [g_onorm, wfull['w_glu'], b_glu, wfull['w_proj_a'], wfull['w_proj_b']]
    (merged,) = ew_call("mix_tail", fn_mix_tail, tail_in, [], tail_w, [(D, BF16)], tm_ew, seq)
    mo, h2 = mix_out_fwd(merged, wfull['w_out'], h1, mods[5], seq)
    dh3, f3, u3, pa3, pb3, dg_final, loss_part = ffn_fwd(
        "ffn2_fwd", h2, mod[:, 6:9, :], g_ffn2, wfull['w1_ffn2'], wfull['w3_ffn2'], wfull['w2_ffn2'], seq,
        loss_head=(loss_target.reshape(t_rows, D), g_final.reshape(1, D)))


    dh2, a3, d1_3, d3_3, df3, dmod_c, dg_ffn2 = ffn_bwd("ffn2_bwd", dh3, h2, f3, pa3, pb3, mod[:, 6:9, :], g_ffn2, wfull['w1_ffn2'],
                                                   wfull['w3_ffn2'], wfull['w2_ffn2'], seq)
    gw['w1_ffn2'] = mm_tn("gw1_ffn2", d1_3, u3)
    gw['w3_ffn2'] = mm_tn("gw3_ffn2", d3_3, u3)
    gw['w2_ffn2'] = mm_tn("gw2_ffn2", a3, df3)

    dmo, d_merged, dgt2 = mix_out_bwd(dh2, mo, wfull['w_out'], mods[5], seq)
    gw['w_out'] = mm_tn("gw_out", merged, dmo)
    (d_opre, d_z, d_ys5, d_gate), _, tail_gw = ew_vjp_call(
        "mix_tail_bwd", fn_mix_tail, tail_in, [], tail_w, [d_merged], [(0, F32), (1, F32), (2, F32), (3, BF16)],
        _pick(seq, (512, 256, 128, 64)), seq)
    dg_onorm, gw['w_glu'], dg_bglu, gw['w_proj_a'], gw['w_proj_b'] = tail_gw
    d_qkv3, d_psmall3, d_alp, d_dtp, rs_ffn2 = deltanet_bwd(
        qkv3, p_small3, alp, dtp, sprev, tinv, d_opre.reshape(bl, seq, DNW), nb_dn, exchange=pack_grads(G_FFN2))
    d_pdn3, d_conv8 = dn_prep_bwd(p_dn3, conv8, d_qkv3, d_z.reshape(bl, seq, DNW))
    d_pdn, d_psmall = d_pdn3.reshape(t_rows, 4 * DNW), d_psmall3.reshape(t_rows, LANES)

    s5_out = s5_bwd(p_s53, tables, d_skip, xs, d_ys5.reshape(bl, seq, S5W))
    d_ps5, d_tables, dg_dskip = s5_out[0].reshape(t_rows, S5W), s5_out[1:11], s5_out[11]
    d_s5p = s5_tables_bwd(s5_params, d_tables)

    gw['w_in'] = jnp.concatenate([mm_tn("gw_dn", d_pdn, u2), mm_tn("gw_small", d_psmall, u2)[:2 * NH],
                                  mm_tn("gw_s5", d_ps5, u2), mm_tn("gw_gate", d_gate, u2)], axis=0)
    dh1, dsh2, dsc2, dg_mix = mix_in_bwd([d_pdn, d_psmall, d_ps5, d_gate], w_pieces, h1, mods[3], mods[4], g_mix, dh2, seq)

    dh0, a1, d1_1, d3_1, df1, dmod_a, dg_ffn1, rs_mix = ffn_bwd(
        "ffn1_bwd", dh1, h0, f1, pa1, pb1, mod[:, 0:3, :], g_ffn1, wfull['w1_ffn1'], wfull['w3_ffn1'], wfull['w2_ffn1'], seq,
        exchange=pack_grads(G_MIX))
    dmod_mine = jnp.concatenate([dmod_a, dsh2, dsc2, dgt2, dmod_c], axis=1).reshape(bl, 9 * D)
    small_grads = {
        'g_ffn1': dg_ffn1, 'g_mix': dg_mix, 'a_log': d_alp[:, NH:2 * NH], 'dt_bias': d_dtp[:, NH:2 * NH],
        'g_onorm': dg_onorm, 'lam_re': d_s5p[0].reshape(1, S5G, S5P), 'lam_im': d_s5p[1].reshape(1, S5G, S5P),
        'log_step': d_s5p[2],
        'b_re': d_s5p[3].reshape(S5C, S5G, S5P).transpose(1, 2, 0)[None],
        'b_im': d_s5p[4].reshape(S5C, S5G, S5P).transpose(1, 2, 0)[None],
        'c_re': d_s5p[5].reshape(S5C, S5G, S5P).transpose(1, 0, 2)[None],
        'c_im': d_s5p[6].reshape(S5C, S5G, S5P).transpose(1, 0, 2)[None],
        'd_skip': dg_dskip, 'b_glu': dg_bglu, 'g_ffn2': dg_ffn2, 'g_final': dg_final.reshape(D)}
    small_shapes = [a[n].shape for n in SMALL]
    small_pack = _pack([small_grads[n] for n in SMALL] + [loss_part], F32)
    n_small = small_pack.shape[0]
    small_buf = jnp.concatenate([small_pack, _pack([dmod_mine, d_conv8[:CONVW]], F32)], axis=0)

    gw['w1_ffn1'], sg = mm_tn("gw1_ffn1", d1_1, u1, gather=small_buf)
    gw['w3_ffn1'], rs_w1 = mm_tn("gw3_ffn1", d3_1, u1, exchange=pack_grads(['w1_ffn1']))
    gw['w2_ffn1'], rs_w3 = mm_tn("gw2_ffn1", a1, df1, exchange=pack_grads(['w3_ffn1']))
    rs_w2 = all_to_all("scatter_w2_ffn1", pack_grads(['w2_ffn1']))

    update(rs_ffn2, G_FFN2)
    update(rs_mix, G_MIX)
    update(rs_w1, ['w1_ffn1'])
    update(rs_w3, ['w3_ffn1'])
    update(rs_w2, ['w2_ffn1'])
    pieces = _unpack8(sg[:, n_small:, :], [dmod_mine.shape, (CONVW, 3 * DNW)])
    dmod_all = pieces[0].reshape(NDEV * bl, 9 * D)
    g_wada, g_bada = ada_bwd(c_all, lax.dynamic_slice(dmod_all, (0, me * n_ada), (NDEV * bl, n_ada)), dmod_all)

    n_conv = conv_qkv.shape[2]
    conv_parts = lax.dynamic_slice(pieces[1], (0, 0, me * n_conv), (NDEV, CONVW, n_conv))
    conv_parts = jnp.pad(conv_parts.reshape(NDEV, 1, -1), ((0, 0), (0, 7), (0, 0)))
    pad8 = lambda t: jnp.pad(t.reshape(1, -1), ((0, 7), (0, 0)))
    conv_res = adamw("adamw_conv", conv_parts, pad8(conv_qkv), pad8(m_conv_qkv), pad8(v_conv_qkv))
    for kind, buf in zip(("grad", "delta", "new_m", "new_v"), conv_res):
        res[kind + "_conv_qkv"] = buf[0].reshape(conv_qkv.shape)

    no_param = jnp.zeros_like(loss_part)
    small_res = adamw("adamw_small", sg[:, :n_small, :],
                      *[_pack([a[p + n] for n in SMALL] + [no_param], F32) for p in ("", "m_", "v_")])
    for kind, buf in zip(("grad", "delta", "new_m", "new_v"), small_res):
        for n, t in zip(SMALL, _unpack(buf, small_shapes)):
            res[kind + "_" + n] = t
    loss = _unpack(small_res[0], small_shapes + [loss_part.shape])[-1][0, 0]

    for n, g in (("w_ada", g_wada), ("b_ada", g_bada)):
        shp = a[n].shape
        r2 = lambda t: t.reshape(-1, shp[-1]) if n == "w_ada" else pad8(t)
        out = adamw("adamw_" + n, r2(g)[None], r2(a[n]), r2(a["m_" + n]), r2(a["v_" + n]))
        for kind, buf in zip(("grad", "delta", "new_m", "new_v"), out):
            res[kind + "_" + n] = (buf if n == "w_ada" else buf[0:1]).reshape(shp)

    outs = [loss, dh0.reshape(x.shape)]
    for kind in ("grad", "delta", "new_m", "new_v"):
        outs += [res[kind + "_" + n] for n in WEIGHTS]
    return tuple(outs)
```

```python
import math

import jax
import jax.numpy as jnp
from jax import lax
from jax.experimental import pallas as pl
from jax.experimental.pallas import tpu as pltpu

F32 = jnp.float32
BF16 = jnp.bfloat16
HI = lax.Precision.HIGHEST
H3 = lax.Precision.HIGH
SDS = jax.ShapeDtypeStruct

D = 1024
FF = 2816
FFN_TF = FF
FFN_FWD_TM = 256
FFN_BWD_TM = 256
NH = 8
DH = 64
DNW = NH * DH
CONVW = 4
CH = 64
S5_CH = 128
ACC_LIMIT = 6 * 1024 * 1024
BF16_TILE_ROWS = 16
DN_ROWS = 4
S5W = 512
S5G = 32
S5P = 64
S5C = 16
S5N = S5G * S5P
GB = 4
NDEV = 8
EPS = 1e-6
LANES = 128
ROW = 1024
VMEM_LIMIT = 56 * 1024 * 1024

ADAM_LR, ADAM_B1, ADAM_B2, ADAM_EPS, ADAM_WD, ADAM_STEP = 0.001, 0.9, 0.999, 1e-08, 0.01, 10

WEIGHTS = ['w_ada', 'b_ada', 'g_ffn1', 'w1_ffn1', 'w3_ffn1', 'w2_ffn1', 'g_mix', 'w_in', 'conv_qkv', 'a_log',
           'dt_bias', 'g_onorm', 'lam_re', 'lam_im', 'log_step', 'b_re', 'b_im', 'c_re', 'c_im', 'd_skip', 'w_glu',
           'b_glu', 'w_proj_a', 'w_proj_b', 'w_out', 'g_ffn2', 'w1_ffn2', 'w3_ffn2', 'w2_ffn2', 'g_final']
RS_WEIGHTS = ['w1_ffn1', 'w3_ffn1', 'w2_ffn1', 'w_in', 'w_glu', 'w_proj_a', 'w_proj_b', 'w_out', 'w1_ffn2', 'w3_ffn2',
              'w2_ffn2']
COL_SHARDED = {'w1_ffn1', 'w3_ffn1', 'w_in', 'w_proj_a', 'w_proj_b', 'w1_ffn2', 'w3_ffn2'}
G_FFN1 = ['w1_ffn1', 'w3_ffn1', 'w2_ffn1']
G_MIX = ['w_in', 'w_glu', 'w_proj_a', 'w_proj_b', 'w_out']
G_FFN2 = ['w1_ffn2', 'w3_ffn2', 'w2_ffn2']
SMALL = ['g_ffn1', 'g_mix', 'a_log', 'dt_bias', 'g_onorm', 'lam_re', 'lam_im', 'log_step', 'b_re', 'b_im', 'c_re',
         'c_im', 'd_skip', 'b_glu', 'g_ffn2', 'g_final']


def _cp(n_grid=0):
    if n_grid:
        return pltpu.CompilerParams(vmem_limit_bytes=VMEM_LIMIT, dimension_semantics=("arbitrary",) * n_grid)
    return pltpu.CompilerParams(vmem_limit_bytes=VMEM_LIMIT)


def _dot(a, b):
    return jnp.dot(a.astype(BF16), b.astype(BF16), preferred_element_type=F32)


def _dot_nt(a, b):
    return lax.dot_general(a.astype(BF16), b.astype(BF16), (((1,), (1,)), ((), ())), preferred_element_type=F32)


def _dot_tn(a, b):
    return lax.dot_general(a.astype(BF16), b.astype(BF16), (((0,), (0,)), ((), ())), preferred_element_type=F32)


def _dot_hi(a, b):
    return jnp.dot(a, b, precision=HI, preferred_element_type=F32)


@jax.custom_vjp
def bdot(a, b):
    return _dot(a, b)


bdot.defvjp(lambda a, b: (_dot(a, b), (a, b)),
            lambda r, g: (_dot_nt(g, r[1]).astype(r[0].dtype), _dot_tn(r[0], g).astype(r[1].dtype)))


@jax.custom_vjp
def bdot_nt(a, b):
    return _dot_nt(a, b)


bdot_nt.defvjp(lambda a, b: (_dot_nt(a, b), (a, b)),
               lambda r, g: (_dot(g, r[1]).astype(r[0].dtype), _dot_tn(g, r[0]).astype(r[1].dtype)))


def _silu(x):
    return x * jax.nn.sigmoid(x)


def _iota2(shape, axis):
    return lax.broadcasted_iota(jnp.int32, shape, axis)


def normmod(h, g, sc, sh):
    y = h * lax.rsqrt(jnp.mean(h * h, axis=-1, keepdims=True) + EPS) * g
    return y * (1.0 + sc) + sh


def fn_merge(gate, ya, yb):
    return (jax.nn.sigmoid(gate[:, :D]) * ya + jax.nn.sigmoid(gate[:, D:]) * yb,)


def fn_glu(y, w, b):
    ge = jax.nn.gelu(y)
    return (ge * jax.nn.sigmoid(bdot(ge, w) + b),)


def fn_onorm(o, z, g_on):
    r = _iota2((DH, DNW), 0)
    c = _iota2((DH, DNW), 1)
    expand = (c % DH == r).astype(F32)
    r2 = _iota2((DNW, DNW), 0)
    c2 = _iota2((DNW, DNW), 1)
    avg = (r2 // DH == c2 // DH).astype(F32) * (1.0 / DH)
    ms = bdot(o * o, avg)
    return (o * lax.rsqrt(ms + EPS) * _dot_hi(g_on, expand) * _silu(z),)


def fn_mix_tail(o_pre, z, y_s5, gate, g_on, w_glu, b_glu, wa_t, wb_t):
    (oa,) = fn_onorm(o_pre, z, g_on)
    (ob,) = fn_glu(y_s5, w_glu, b_glu)
    return fn_merge(gate, bdot_nt(oa, wa_t), bdot_nt(ob, wb_t))


def gate_fn(small, alp, dtp):
    beta = jax.nn.sigmoid(small)
    la = -jnp.exp(alp) * jax.nn.softplus(small + dtp)
    tri = (_iota2((CH, CH), 0) >= _iota2((CH, CH), 1)).astype(F32)
    gc = _dot_hi(tri, la)
    gct = lax.dot_general(la, tri, (((0,), (1,)), ((), ())), precision=HI, preferred_element_type=F32)
    return beta, gc, gct


def _bdg(a, b, ca, cb, hi):
    if not hi:
        a, b = a.astype(BF16), b.astype(BF16)
    return lax.dot_general(a, b, (((ca,), (cb,)), ((0,), (0,))), precision=H3 if hi else None,
                           preferred_element_type=F32)


def _batched_matmuls(hi):
    nn_ = lambda a, b: _bdg(a, b, 2, 1, hi)
    nt_ = lambda a, b: _bdg(a, b, 2, 2, hi)
    tn_ = lambda a, b: _bdg(a, b, 1, 1, hi)
    nn = jax.custom_vjp(nn_)
    nn.defvjp(lambda a, b: (nn_(a, b), (a, b)), lambda r, g: (nt_(g, r[1]), tn_(r[0], g)))
    nt = jax.custom_vjp(nt_)
    nt.defvjp(lambda a, b: (nt_(a, b), (a, b)), lambda r, g: (nn_(g, r[1]), tn_(g, r[0])))
    tn = jax.custom_vjp(tn_)
    tn.defvjp(lambda a, b: (tn_(a, b), (a, b)), lambda r, g: (nt_(r[1], g), nn_(r[0], g)))
    return nn, nt, tn


bnn, bnt, btn = _batched_matmuls(False)
hnn, hnt, htn = _batched_matmuls(True)


def _unit_lower_inverse(a):
    r = _iota2((1, CH, CH), 1)
    c = _iota2((1, CH, CH), 2)
    eye = (r == c).astype(F32)
    d = jnp.where(r // 8 == c // 8, a, 0.0)
    inv = eye - d
    p = d
    for _ in range(2):
        p = hnn(p, p)
        inv = inv + hnn(inv, p)
    for blk in (16, 32, 64):
        off = jnp.where((r // blk == c // blk) & (r // (blk // 2) != c // (blk // 2)), a, 0.0)
        mm = hnn if blk == 16 else bnn
        inv = inv - mm(mm(inv, off), inv)
    return inv


@jax.custom_vjp
def _inverse_given(a, t):
    return t


_inverse_given.defvjp(lambda a, t: (t, t), lambda t, g: (-hnt(htn(t, g), t), jnp.zeros_like(t)))


def dn_prep(xc, w):
    t = xc.shape[0] - 8
    c = xc[5:5 + t] * w[0:1] + xc[6:6 + t] * w[1:2] + xc[7:7 + t] * w[2:3] + xc[8:8 + t] * w[3:4]
    act = _silu(c)
    q, k, v = act[:, :DNW], act[:, DNW:2 * DNW], act[:, 2 * DNW:]
    ones = (_iota2((DNW, DNW), 0) // DH == _iota2((DNW, DNW), 1) // DH).astype(F32)
    q = q * lax.rsqrt(bdot(q * q, ones) + EPS) * (DH ** -0.5)
    k = k * lax.rsqrt(bdot(k * k, ones) + EPS)
    return jnp.concatenate([q, k, v], axis=1)


def dn_chunk(q, k, v, b, g, gt, s_prev, t_saved=None):
    r = _iota2((1, CH, CH), 1)
    c = _iota2((1, CH, CH), 2)
    causal = r >= c
    dec = jnp.where(causal, jnp.exp(jnp.where(causal, g - gt, 0.0)), 0.0)
    kb = k * b
    qk = bnt(jnp.concatenate([q, kb], axis=1), k)
    attn = qk[:, :CH] * dec
    a = jnp.where(r > c, qk[:, CH:] * dec, 0.0)
    tinv = _unit_lower_inverse(a) if t_saved is None else _inverse_given(a, t_saved)
    eg = jnp.exp(g)
    uw = hnn(tinv, jnp.concatenate([v * b, kb * eg], axis=2))
    g_last = g[:, CH - 1:CH]
    ws = bnn(jnp.concatenate([uw[..., DH:], q * eg], axis=1), s_prev)
    v_new = uw[..., :DH] - ws[:, :CH]
    o = ws[:, CH:] + bnn(attn, v_new)
    s_new = s_prev * jnp.exp(g_last) + btn(k * jnp.exp(g_last - g), v_new)
    return o, s_new, tinv


def s5_chunk(u, xp_re, xp_im, bb_re, bb_im, cc_re, cc_im, p0r, p0i, p1r, p1i, pir, pii, dsk):
    nb, ch, _ = u.shape
    u2 = u.reshape(nb * ch, LANES)
    bu_re = bdot(u2, bb_re).reshape(nb, ch, 512)
    bu_im = bdot(u2, bb_im).reshape(nb, ch, 512)
    xt_re = pir * bu_re - pii * bu_im
    xt_im = pir * bu_im + pii * bu_re
    tri = jnp.broadcast_to((_iota2((1, ch, ch), 1) >= _iota2((1, ch, ch), 2)).astype(F32), (nb, ch, ch))
    cs_re = hnn(tri, xt_re)
    cs_im = hnn(tri, xt_im)
    x_re = p0r * cs_re - p0i * cs_im + p1r * xp_re - p1i * xp_im
    x_im = p0r * cs_im + p0i * cs_re + p1r * xp_im + p1i * xp_re
    y = bdot_nt(x_re.reshape(nb * ch, 512), cc_re) - bdot_nt(x_im.reshape(nb * ch, 512), cc_im) + dsk * u2
    return y.reshape(nb, ch, LANES), x_re[:, ch - 1:ch], x_im[:, ch - 1:ch]


def s5_tables(lam_re, lam_im, log_step, bre, bim, cre, cim):
    expand = (_iota2((S5G, S5N), 1) // S5P == _iota2((S5G, S5N), 0)).astype(F32)
    step = _dot_hi(jnp.exp(log_step), expand)
    lre = jnp.minimum(lam_re, -1e-4)
    lr = lre * step
    ang = lam_im * step
    mag = jnp.exp(lr)
    lb_re = mag * jnp.cos(ang)
    lb_im = mag * jnp.sin(ang)
    den = lre * lre + lam_im * lam_im
    coef_re = ((lb_re - 1.0) * lre + lb_im * lam_im) / den
    coef_im = (lb_im * lre - (lb_re - 1.0) * lam_im) / den
    bb_re = coef_re * bre - coef_im * bim
    bb_im = coef_re * bim + coef_im * bre
    j = _iota2((S5_CH, 1), 0).astype(F32)
    jc = j - S5_CH // 2
    e0 = jnp.exp(jc * lr)
    e1 = jnp.exp((j + 1.0) * lr)
    ei = jnp.exp(-jc * lr)
    mask = (_iota2((LANES, 512), 0) // S5C == _iota2((LANES, 512), 1) // S5P).astype(F32)

    def blocks(t):
        return jnp.concatenate([(jnp.tile(t[:, gb * 512:(gb + 1) * 512], (LANES // S5C, 1)) * mask)[None]
                                for gb in range(GB)], axis=0)

    return (blocks(bb_re), blocks(bb_im), blocks(cre), blocks(cim),
            e0 * jnp.cos(jc * ang), e0 * jnp.sin(jc * ang),
            e1 * jnp.cos((j + 1.0) * ang), e1 * jnp.sin((j + 1.0) * ang),
            ei * jnp.cos(jc * ang), -ei * jnp.sin(jc * ang))


def _row_specs(tiled, batch, bcast, tm, tpb):
    specs = [pl.BlockSpec((tm, a.shape[1]), lambda i: (i, 0)) for a in tiled]
    specs += [pl.BlockSpec((None,) + a.shape[1:], lambda i: (i // tpb, 0, 0)) for a in batch]
    specs += [pl.BlockSpec(a.shape, lambda i, nd=a.ndim: (0,) * nd) for a in bcast]
    return specs


def ew_call(name, fn, tiled, batch, bcast, outs, tm, seq):
    t_rows = tiled[0].shape[0]
    n_in = len(tiled) + len(batch) + len(bcast)

    def body(*refs):
        vals = [r[...].astype(F32) for r in refs[:n_in]]
        for r, o in zip(refs[n_in:], fn(*vals)):
            r[...] = o.astype(r.dtype)

    return pl.pallas_call(
        body, grid=(t_rows // tm,), in_specs=_row_specs(tiled, batch, bcast, tm, seq // tm),
        out_specs=[pl.BlockSpec((tm, w), lambda i: (i, 0)) for w, _ in outs],
        out_shape=[SDS((t_rows, w), dt) for w, dt in outs], name=name, compiler_params=_cp(1))(*tiled, *batch, *bcast)


def ew_vjp_call(name, fn, tiled, batch, bcast, cts, want, tm, seq, addend=None):
    t_rows = tiled[0].shape[0]
    tpb = seq // tm
    n_t, n_b, n_c = len(tiled), len(batch), len(bcast)
    n_in = n_t + n_b + n_c
    extra = [] if addend is None else [addend]

    def body(*refs):
        i = pl.program_id(0)
        vals = [r[...].astype(F32) for r in refs[:n_in]]
        ctv = tuple(r[...].astype(F32) for r in refs[n_in:n_in + len(cts)])
        outs = refs[n_in + len(cts) + len(extra):]
        _, vjp = jax.vjp(fn, *vals)
        grads = vjp(ctv)
        for k, (r, (idx, _)) in enumerate(zip(outs[:len(want)], want)):
            g = grads[idx]
            if k == 0 and extra:
                g = g + refs[n_in + len(cts)][...]
            r[...] = g.astype(r.dtype)
        for k in range(n_b):
            r, g = outs[len(want) + k], grads[n_t + k]

            @pl.when(i % tpb == 0)
            def _(r=r, g=g):
                r[...] = g

            @pl.when(i % tpb != 0)
            def _(r=r, g=g):
                r[...] += g
        for k in range(n_c):
            r, g = outs[len(want) + n_b + k], grads[n_t + n_b + k]

            @pl.when(i == 0)
            def _(r=r, g=g):
                r[...] = g

            @pl.when(i != 0)
            def _(r=r, g=g):
                r[...] += g

    out_specs = [pl.BlockSpec((tm, tiled[idx].shape[1]), lambda i: (i, 0)) for idx, _ in want]
    out_specs += [pl.BlockSpec((None,) + a.shape[1:], lambda i: (i // tpb, 0, 0)) for a in batch]
    out_specs += [pl.BlockSpec(a.shape, lambda i, nd=a.ndim: (0,) * nd) for a in bcast]
    out_shape = [SDS(tiled[idx].shape, dt) for idx, dt in want]
    out_shape += [SDS(a.shape, F32) for a in batch] + [SDS(a.shape, F32) for a in bcast]
    res = pl.pallas_call(
        body, grid=(t_rows // tm,),
        in_specs=_row_specs(tiled, batch, bcast, tm, tpb)
        + [pl.BlockSpec((tm, a.shape[1]), lambda i: (i, 0)) for a in list(cts) + extra],
        out_specs=out_specs, out_shape=out_shape, name=name, compiler_params=_cp(1))(*tiled, *batch, *bcast, *cts, *extra)
    return res[:len(want)], res[len(want):len(want) + n_b], res[len(want) + n_b:]


def _pick(n, cands):
    for c in cands:
        if n % c == 0:
            return c
    return n


def mm_tn(name, a, b, exchange=None, gather=None):
    t_rows, m = a.shape
    n = b.shape[1]
    tn = n if n <= 1024 else _pick(n, (1024, 512, 256, 128))
    tm = max([t for t in range(LANES, m + 1, LANES) if m % t == 0 and t * tn * 4 <= ACC_LIMIT] or [m])
    tk = _pick(t_rows, (1024, 512, 256, 128, 64))
    grid = (m // tm, n // tn, t_rows // tk)
    extra = [x for x in (exchange, gather) if x is not None]
    ne = len(extra)

    def body(*refs):
        a_ref, b_ref = refs[:2]
        o_ref, acc = refs[2 + ne], refs[3 + 2 * ne]
        i, j, k = pl.program_id(0), pl.program_id(1), pl.program_id(2)
        first = (i == 0) & (j == 0) & (k == 0)
        middle = (i == grid[0] - 1) & (j == grid[1] - 1) & (k == grid[2] // 2)
        last = (i == grid[0] - 1) & (j == grid[1] - 1) & (k == grid[2] - 1)
        at_end = []
        for e, x in enumerate(extra):
            comm_refs = (refs[2 + e], refs[3 + ne + e]) + tuple(refs[4 + 2 * ne + 3 * e:7 + 2 * ne + 3 * e])
            if x is exchange:
                start, finish = _exchange_phases(*comm_refs)
                pl.when(first)(start)
            else:
                start, forward, finish = _gather_phases(*comm_refs)
                pl.when(first)(start)
                pl.when(middle)(forward)
            at_end.append(finish)

        @pl.when(k == 0)
        def _():
            acc[...] = jnp.zeros_like(acc)

        acc[...] += _dot_tn(a_ref[...], b_ref[...])

        @pl.when(k == grid[2] - 1)
        def _():
            o_ref[...] = acc[...].astype(BF16)

        for phase in at_end:
            pl.when(last)(phase)

    res = pl.pallas_call(
        body, grid=grid,
        in_specs=[pl.BlockSpec((tk, tm), lambda i, j, k: (k, i)), pl.BlockSpec((tk, tn), lambda i, j, k: (k, j))]
        + [HBM_SPEC] * ne,
        out_specs=[pl.BlockSpec((tm, tn), lambda i, j, k: (i, j))] + [HBM_SPEC] * ne,
        out_shape=[SDS((m, n), BF16)] + [SDS(x.shape if x is exchange else (NDEV,) + x.shape, x.dtype) for x in extra],
        scratch_shapes=[pltpu.VMEM((tm, tn), F32)] + _comm_scratch() * ne, name=name,
        compiler_params=_cp(3))(a, b, *extra)
    return res if extra else res[0]


def _ffn_weight_spec():
    if FFN_TF == FF:
        return pl.BlockSpec((FF, D), lambda i, j: (0, 0), pipeline_mode=pl.Buffered(1))
    return pl.BlockSpec((FFN_TF, D), lambda i, j: (j, 0))


def ffn_fwd(name, h, mod3, g, w1, w3, w2, seq, gather=None, loss_head=None):
    t_rows = h.shape[0]
    tm = _pick(seq, (FFN_FWD_TM, 128, 64))
    tf = FFN_TF
    tpb = seq // tm
    nf = FF // tf
    nt = t_rows // tm
    extra = [] if gather is None else [gather]
    head = [] if loss_head is None else list(loss_head)
    nh, ne = len(head), len(extra)

    def body(*refs):
        h_ref, mod_ref, g_ref, w1_ref, w3_ref, w2_ref = refs[:6]
        o0 = 6 + nh + ne
        ho_ref, f_ref, u_ref, h1_ref, h3_ref = refs[o0:o0 + 5]
        s0 = o0 + 5 + nh + ne
        acc = refs[s0]
        i, j = pl.program_id(0), pl.program_id(1)
        if extra:
            start, forward, finish = _gather_phases(refs[6 + nh], refs[o0 + 5 + nh], *refs[s0 + 1:s0 + 4])
            pl.when((i == 0) & (j == 0))(start)
            pl.when((i == nt // 2) & (j == 0))(forward)

        @pl.when(j == 0)
        def _():
            u_ref[...] = normmod(h_ref[...], g_ref[...], mod_ref[1:2, :], mod_ref[0:1, :]).astype(BF16)
            acc[...] = jnp.zeros_like(acc)

        u = u_ref[...]
        h1 = _dot_nt(u, w1_ref[...])
        h3 = _dot_nt(u, w3_ref[...])
        h1_ref[...] = h1.astype(BF16)
        h3_ref[...] = h3.astype(BF16)
        acc[...] += _dot(_silu(h1) * h3, w2_ref[...])

        @pl.when(j == nf - 1)
        def _():
            f_ref[...] = acc[...]
            h_out = h_ref[...] + 0.5 * mod_ref[2:3, :] * acc[...]
            if not head:
                ho_ref[...] = h_out
            else:
                t_ref, gf_ref, dg_ref, loss_ref = refs[6], refs[7], refs[o0 + 5], refs[o0 + 6]
                y, vjp = jax.vjp(lambda hh, gg: hh * lax.rsqrt(jnp.mean(hh * hh, axis=-1, keepdims=True) + EPS) * gg,
                                 h_out, gf_ref[...])
                e = y - t_ref[...]
                dh, dg = vjp(e * (1.0 / D))
                part = jnp.sum(jnp.sum(e * e, axis=1, keepdims=True), axis=0, keepdims=True) * (0.5 / D) \
                    + jnp.zeros((1, LANES), F32)
                ho_ref[...] = dh

                @pl.when(i == 0)
                def _():
                    dg_ref[...] = dg
                    loss_ref[...] = part

                @pl.when(i != 0)
                def _():
                    dg_ref[...] += dg
                    loss_ref[...] += part

        if extra:
            pl.when((i == nt - 1) & (j == nf - 1))(finish)

    row = lambda i, j: (i, 0)
    const = lambda i, j: (0, 0)
    head_in = [pl.BlockSpec((tm, D), row), pl.BlockSpec((1, D), const)] if head else []
    head_out = [pl.BlockSpec((1, D), const), pl.BlockSpec((1, LANES), const)] if head else []
    return pl.pallas_call(
        body, grid=(nt, nf),
        in_specs=[pl.BlockSpec((tm, D), row), pl.BlockSpec((None, 3, D), lambda i, j: (i // tpb, 0, 0)),
                  pl.BlockSpec((1, D), const), _ffn_weight_spec(), _ffn_weight_spec(), _ffn_weight_spec()]
        + head_in + [HBM_SPEC] * ne,
        out_specs=[pl.BlockSpec((tm, D), row), pl.BlockSpec((tm, D), row), pl.BlockSpec((tm, D), row),
                   pl.BlockSpec((tm, tf), lambda i, j: (i, j)), pl.BlockSpec((tm, tf), lambda i, j: (i, j))]
        + head_out + [HBM_SPEC] * ne,
        out_shape=[SDS((t_rows, D), F32), SDS((t_rows, D), F32), SDS((t_rows, D), BF16), SDS((t_rows, FF), BF16),
                   SDS((t_rows, FF), BF16)] + ([SDS((1, D), F32), SDS((1, LANES), F32)] if head else [])
        + [SDS((NDEV,) + x.shape, x.dtype) for x in extra],
        scratch_shapes=[pltpu.VMEM((tm, D), F32)] + (_comm_scratch() if extra else []), name=name,
        compiler_params=_cp(2))(h, mod3, g, w1, w3, w2, *head, *extra)


def ffn_bwd(name, dho, h, f_out, h1_in, h3_in, mod3, g, w1, w3, w2, seq, exchange=None):
    t_rows = h.shape[0]
    tm = _pick(seq, (FFN_BWD_TM, 128, 64))
    tf = FFN_TF
    tpb = seq // tm
    nf = FF // tf
    nt = t_rows // tm
    extra = [] if exchange is None else [exchange]

    def body(*refs):
        dho_ref, h_ref, f_ref, h1_ref, h3_ref, mod_ref, g_ref, w1_ref, w3_ref, w2_ref = refs[:10]
        dh_ref, a_ref, dh1_ref, dh3_ref, df_scr, dmod_ref, dg_ref = refs[10 + len(extra):17 + len(extra)]
        du_acc = refs[17 + 2 * len(extra)]
        i, j = pl.program_id(0), pl.program_id(1)
        if extra:
            start, finish = _exchange_phases(refs[10], refs[18], *refs[20:23])
            pl.when((i == 0) & (j == 0))(start)

        @pl.when(j == 0)
        def _():
            df_scr[...] = (0.5 * mod_ref[2:3, :] * dho_ref[...]).astype(BF16)
            du_acc[...] = jnp.zeros_like(du_acc)

        h1 = h1_ref[...].astype(F32)
        h3 = h3_ref[...].astype(F32)
        sg = jax.nn.sigmoid(h1)
        s = h1 * sg
        da = _dot_nt(df_scr[...], w2_ref[...])
        dh3 = (da * s).astype(BF16)
        dh1 = (da * h3 * (sg * (1.0 + h1 * (1.0 - sg)))).astype(BF16)
        a_ref[...] = (s * h3).astype(BF16)
        dh1_ref[...] = dh1
        dh3_ref[...] = dh3
        du_acc[...] += _dot(dh1, w1_ref[...]) + _dot(dh3, w3_ref[...])

        @pl.when(j == nf - 1)
        def _():
            _, vjp = jax.vjp(normmod, h_ref[...], g_ref[...], mod_ref[1:2, :], mod_ref[0:1, :])
            dh_n, dg, dsc, dsh = vjp(du_acc[...])
            dh_ref[...] = dho_ref[...] + dh_n
            dgt = jnp.sum(0.5 * dho_ref[...] * f_ref[...], axis=0, keepdims=True)
            dmod = jnp.concatenate([dsh, dsc, dgt], axis=0)

            @pl.when(i % tpb == 0)
            def _():
                dmod_ref[...] = dmod

            @pl.when(i % tpb != 0)
            def _():
                dmod_ref[...] += dmod

            @pl.when(i == 0)
            def _():
                dg_ref[...] = dg

            @pl.when(i != 0)
            def _():
                dg_ref[...] += dg

        if extra:
            pl.when((i == nt - 1) & (j == nf - 1))(finish)

    row = lambda i, j: (i, 0)
    col = lambda i, j: (i, j)
    return pl.pallas_call(
        body, grid=(nt, nf),
        in_specs=[pl.BlockSpec((tm, D), row), pl.BlockSpec((tm, D), row), pl.BlockSpec((tm, D), row),
                  pl.BlockSpec((tm, tf), col), pl.BlockSpec((tm, tf), col),
                  pl.BlockSpec((None, 3, D), lambda i, j: (i // tpb, 0, 0)),
                  pl.BlockSpec((1, D), lambda i, j: (0, 0)), _ffn_weight_spec(), _ffn_weight_spec(), _ffn_weight_spec()]
        + [HBM_SPEC] * len(extra),
        out_specs=[pl.BlockSpec((tm, D), row), pl.BlockSpec((tm, tf), col), pl.BlockSpec((tm, tf), col),
                   pl.BlockSpec((tm, tf), col), pl.BlockSpec((tm, D), row),
                   pl.BlockSpec((None, 3, D), lambda i, j: (i // tpb, 0, 0)), pl.BlockSpec((1, D), lambda i, j: (0, 0))]
        + [HBM_SPEC] * len(extra),
        out_shape=[SDS((t_rows, D), F32), SDS((t_rows, FF), BF16), SDS((t_rows, FF), BF16), SDS((t_rows, FF), BF16),
                   SDS((t_rows, D), BF16), SDS(mod3.shape, F32), SDS((1, D), F32)] + [SDS(x.shape, x.dtype) for x in extra],
        scratch_shapes=[pltpu.VMEM((tm, D), F32)] + (_comm_scratch() if extra else []), name=name,
        compiler_params=_cp(2))(dho, h, f_out, h1_in, h3_in, mod3, g, w1, w3, w2, *extra)


def _resident(shape):
    return pl.BlockSpec(shape, lambda i: (0,) * len(shape), pipeline_mode=pl.Buffered(1))


def mix_in_fwd(h, sh, sc, g, ws, seq):
    t_rows = h.shape[0]
    tm = _pick(seq, (256, 128, 64))
    tpb = seq // tm
    nw = len(ws)

    def body(h_ref, sh_ref, sc_ref, g_ref, *rest):
        u = normmod(h_ref[...], g_ref[...], sc_ref[...], sh_ref[...]).astype(BF16)
        rest[nw][...] = u
        for w_ref, p_ref in zip(rest[:nw], rest[nw + 1:]):
            p_ref[...] = _dot_nt(u, w_ref[...])

    row = lambda i: (i, 0)
    batch = pl.BlockSpec((None, 1, D), lambda i: (i // tpb, 0, 0))
    return pl.pallas_call(
        body, grid=(t_rows // tm,),
        in_specs=[pl.BlockSpec((tm, D), row), batch, batch, pl.BlockSpec((1, D), lambda i: (0, 0))]
        + [_resident(w.shape) for w in ws],
        out_specs=[pl.BlockSpec((tm, D), row)] + [pl.BlockSpec((tm, w.shape[0]), row) for w in ws],
        out_shape=[SDS((t_rows, D), BF16)] + [SDS((t_rows, w.shape[0]), F32) for w in ws], name="mix_in_fwd",
        compiler_params=_cp(1))(h, sh, sc, g, *ws)


def mix_in_bwd(dps, ws, h, sh, sc, g, dh_add, seq):
    t_rows = h.shape[0]
    tm = _pick(seq, (256, 128, 64))
    tpb = seq // tm
    nw = len(ws)

    def body(*refs):
        h_ref, sh_ref, sc_ref, g_ref, add_ref, dh_ref, dsh_ref, dsc_ref, dg_ref = refs[2 * nw:]
        i = pl.program_id(0)
        du = _dot(refs[0][...], refs[nw][...])
        for k in range(1, nw):
            du = du + _dot(refs[k][...], refs[nw + k][...])
        _, vjp = jax.vjp(normmod, h_ref[...], g_ref[...], sc_ref[...], sh_ref[...])
        dh_n, dg, dsc, dsh = vjp(du)
        dh_ref[...] = add_ref[...] + dh_n

        @pl.when(i % tpb == 0)
        def _():
            dsh_ref[...] = dsh
            dsc_ref[...] = dsc

        @pl.when(i % tpb != 0)
        def _():
            dsh_ref[...] += dsh
            dsc_ref[...] += dsc

        @pl.when(i == 0)
        def _():
            dg_ref[...] = dg

        @pl.when(i != 0)
        def _():
            dg_ref[...] += dg

    row = lambda i: (i, 0)
    batch = pl.BlockSpec((None, 1, D), lambda i: (i // tpb, 0, 0))
    gain = pl.BlockSpec((1, D), lambda i: (0, 0))
    return pl.pallas_call(
        body, grid=(t_rows // tm,),
        in_specs=[pl.BlockSpec((tm, dp.shape[1]), row) for dp in dps] + [_resident(w.shape) for w in ws]
        + [pl.BlockSpec((tm, D), row), batch, batch, gain, pl.BlockSpec((tm, D), row)],
        out_specs=[pl.BlockSpec((tm, D), row), batch, batch, gain],
        out_shape=[SDS((t_rows, D), F32), SDS(sh.shape, F32), SDS(sc.shape, F32), SDS((1, D), F32)], name="mix_in_bwd",
        compiler_params=_cp(1))(*dps, *ws, h, sh, sc, g, dh_add)


def mix_out_fwd(merged, w_out, h_prev, gt, seq):
    t_rows = merged.shape[0]
    tm = _pick(seq, (256, 128, 64))
    tpb = seq // tm

    def body(m_ref, w_ref, h_ref, gt_ref, mo_ref, ho_ref):
        mo = _dot(m_ref[...], w_ref[...])
        mo_ref[...] = mo
        ho_ref[...] = h_ref[...] + gt_ref[...] * mo

    row = lambda i: (i, 0)
    return pl.pallas_call(
        body, grid=(t_rows // tm,),
        in_specs=[pl.BlockSpec((tm, D), row), _resident(w_out.shape), pl.BlockSpec((tm, D), row),
                  pl.BlockSpec((None, 1, D), lambda i: (i // tpb, 0, 0))],
        out_specs=[pl.BlockSpec((tm, D), row), pl.BlockSpec((tm, D), row)],
        out_shape=[SDS((t_rows, D), F32), SDS((t_rows, D), F32)], name="mix_out_fwd",
        compiler_params=_cp(1))(merged, w_out, h_prev, gt)


def mix_out_bwd(dh, mo, w_out, gt, seq):
    t_rows = dh.shape[0]
    tm = _pick(seq, (256, 128, 64))
    tpb = seq // tm

    def body(dh_ref, mo_ref, w_ref, gt_ref, dmo_ref, dm_ref, dgt_ref):
        i = pl.program_id(0)
        dmo = (gt_ref[...] * dh_ref[...]).astype(BF16)
        dmo_ref[...] = dmo
        dm_ref[...] = _dot_nt(dmo, w_ref[...])
        dgt = jnp.sum(dh_ref[...] * mo_ref[...], axis=0, keepdims=True)

        @pl.when(i % tpb == 0)
        def _():
            dgt_ref[...] = dgt

        @pl.when(i % tpb != 0)
        def _():
            dgt_ref[...] += dgt

    row = lambda i: (i, 0)
    batch = pl.BlockSpec((None, 1, D), lambda i: (i // tpb, 0, 0))
    return pl.pallas_call(
        body, grid=(t_rows // tm,),
        in_specs=[pl.BlockSpec((tm, D), row), pl.BlockSpec((tm, D), row), _resident(w_out.shape), batch],
        out_specs=[pl.BlockSpec((tm, D), row), pl.BlockSpec((tm, D), row), batch],
        out_shape=[SDS((t_rows, D), BF16), SDS((t_rows, D), F32), SDS(gt.shape, F32)], name="mix_out_bwd",
        compiler_params=_cp(1))(dh, mo, w_out, gt)


def _dn_cols(part, hd):
    return slice(part * DNW + hd * DH, part * DNW + (hd + 1) * DH)


def _qkv_stacks(qkv_ref, nb):
    pairs = [(b, hd) for b in range(nb) for hd in range(NH)]
    return [jnp.stack([qkv_ref[b, :, _dn_cols(part, hd)] for b, hd in pairs]) for part in range(3)]


def dn_prep_fwd(p_dn, conv8):
    bl, seq, _ = p_dn.shape
    tp = _pick(seq, (256, 128, 64))

    def body(raw_ref, halo_ref, conv_ref, o_ref):
        hm = (pl.program_id(1) > 0).astype(F32)
        o_ref[...] = dn_prep(jnp.concatenate([halo_ref[...] * hm, raw_ref[...]], axis=0), conv_ref[...])

    return pl.pallas_call(
        body, grid=(bl, seq // tp),
        in_specs=[pl.BlockSpec((None, tp, 3 * DNW), lambda b, i: (b, i, 0)),
                  pl.BlockSpec((None, 8, 3 * DNW), lambda b, i: (b, jnp.maximum(i * (tp // 8) - 1, 0), 0)),
                  pl.BlockSpec((8, 3 * DNW), lambda b, i: (0, 0))],
        out_specs=pl.BlockSpec((None, tp, 3 * DNW), lambda b, i: (b, i, 0)),
        out_shape=SDS((bl, seq, 3 * DNW), F32), name="dn_prep_fwd", compiler_params=_cp(2))(p_dn, p_dn, conv8)


def dn_prep_bwd(p_dn, conv8, d_qkv, d_z):
    bl, seq, _ = p_dn.shape
    tp = _pick(seq, (256, 128, 64))
    nt = seq // tp

    def body(raw_ref, halo_ref, conv_ref, dq_ref, dz_ref, draw_ref, dconv_ref, carry):
        b, r = pl.program_id(0), pl.program_id(1)

        @pl.when((b == 0) & (r == 0))
        def _():
            dconv_ref[...] = jnp.zeros_like(dconv_ref)

        @pl.when(r == 0)
        def _():
            carry[...] = jnp.zeros_like(carry)

        hm = (r < nt - 1).astype(F32)
        _, vjp = jax.vjp(dn_prep, jnp.concatenate([halo_ref[...] * hm, raw_ref[...]], axis=0), conv_ref[...])
        dxc, dw = vjp(dq_ref[...])
        tail = dxc[tp:tp + 8] + carry[...]
        draw_ref[:, 0:3 * DNW] = jnp.concatenate([dxc[8:tp], tail], axis=0).astype(BF16)
        draw_ref[:, 3 * DNW:4 * DNW] = dz_ref[...].astype(BF16)
        carry[...] = dxc[0:8] * hm
        dconv_ref[...] += dw

    blk = lambda b, r: (b, nt - 1 - r, 0)
    return pl.pallas_call(
        body, grid=(bl, nt),
        in_specs=[pl.BlockSpec((None, tp, 3 * DNW), blk),
                  pl.BlockSpec((None, 8, 3 * DNW), lambda b, r: (b, jnp.maximum((nt - 1 - r) * (tp // 8) - 1, 0), 0)),
                  pl.BlockSpec((8, 3 * DNW), lambda b, r: (0, 0)), pl.BlockSpec((None, tp, 3 * DNW), blk),
                  pl.BlockSpec((None, tp, DNW), blk)],
        out_specs=[pl.BlockSpec((None, tp, 4 * DNW), blk), pl.BlockSpec((8, 3 * DNW), lambda b, r: (0, 0))],
        out_shape=[SDS((bl, seq, 4 * DNW), BF16), SDS((8, 3 * DNW), F32)],
        scratch_shapes=[pltpu.VMEM((8, 3 * DNW), F32)], name="dn_prep_bwd", compiler_params=_cp(2))(p_dn, p_dn, conv8, d_qkv, d_z)


def _gate_stacks(gates, nb):
    pairs = [(b, hd) for b in range(nb) for hd in range(NH)]
    bs = jnp.stack([gates[b][0][:, hd:hd + 1] for b, hd in pairs])
    gs = jnp.stack([gates[b][1][:, NH + hd:NH + hd + 1] for b, hd in pairs])
    gts = jnp.stack([gates[b][2][NH + hd:NH + hd + 1, :] for b, hd in pairs])
    return bs, gs, gts


def deltanet_fwd(qkv, p_small, alp, dtp, nb, gather=None):
    bl, seq, _ = qkv.shape
    nc = seq // CH
    ng = nb * NH
    extra = [] if gather is None else [gather]

    def body(*refs):
        qkv_ref, small_ref, alp_ref, dtp_ref = refs[:4]
        o_ref, sprev_ref, tinv_ref = refs[4 + len(extra):7 + len(extra)]
        s_scr = refs[7 + 2 * len(extra)]
        bb, n = pl.program_id(0), pl.program_id(1)
        if extra:
            start, forward, finish = _gather_phases(refs[4], refs[8], *refs[10:13])
            pl.when((bb == 0) & (n == 0))(start)

        @pl.when(n == 0)
        def _():
            s_scr[...] = jnp.zeros_like(s_scr)

        gates = [gate_fn(small_ref[b], alp_ref[...], dtp_ref[...]) for b in range(nb)]
        s_prev = s_scr[...]
        o, s_new, tinv = dn_chunk(*_qkv_stacks(qkv_ref, nb), *_gate_stacks(gates, nb), s_prev)
        sprev_ref[...] = s_prev
        tinv_ref[...] = tinv
        s_scr[...] = s_new
        for b in range(nb):
            for hd in range(NH):
                o_ref[b, :, hd * DH:(hd + 1) * DH] = o[b * NH + hd]
        if extra:
            pl.when((bb == bl // nb - 1) & (n == nc // 2))(forward)
            pl.when((bb == bl // nb - 1) & (n == nc - 1))(finish)

    blk = lambda bb, n: (bb, n, 0)
    const = lambda bb, n: (0, 0)
    saved = pl.BlockSpec((None, ng, DH, DH), lambda bb, n: (bb * nc + n, 0, 0, 0))
    return pl.pallas_call(
        body, grid=(bl // nb, nc),
        in_specs=[pl.BlockSpec((nb, CH, 3 * DNW), blk), pl.BlockSpec((nb, CH, LANES), blk),
                  pl.BlockSpec((1, LANES), const), pl.BlockSpec((1, LANES), const)] + [HBM_SPEC] * len(extra),
        out_specs=[pl.BlockSpec((nb, CH, DNW), blk), saved, saved] + [HBM_SPEC] * len(extra),
        out_shape=[SDS((bl, seq, DNW), F32), SDS((bl // nb * nc, ng, DH, DH), F32), SDS((bl // nb * nc, ng, DH, DH), F32)]
        + [SDS((NDEV,) + x.shape, x.dtype) for x in extra],
        scratch_shapes=[pltpu.VMEM((ng, DH, DH), F32)] + (_comm_scratch() if extra else []), name="deltanet_fwd",
        compiler_params=_cp(2))(qkv, p_small, alp, dtp, *extra)


def deltanet_bwd(qkv, p_small, alp, dtp, sprev, tinv, d_o, nb, exchange=None):
    bl, seq, _ = qkv.shape
    nc = seq // CH
    ng = nb * NH
    extra = [] if exchange is None else [exchange]

    def body(*refs):
        qkv_ref, small_ref, alp_ref, dtp_ref, sprev_ref, tinv_ref, do_ref = refs[:7]
        dqkv_ref, dsmall_ref, dalp_ref, ddtp_ref = refs[7 + len(extra):11 + len(extra)]
        ds_scr = refs[11 + 2 * len(extra)]
        bb, r = pl.program_id(0), pl.program_id(1)
        if extra:
            start, finish = _exchange_phases(refs[7], refs[12], *refs[14:17])
            pl.when((bb == 0) & (r == 0))(start)

        @pl.when((bb == 0) & (r == 0))
        def _():
            dalp_ref[...] = jnp.zeros_like(dalp_ref)
            ddtp_ref[...] = jnp.zeros_like(ddtp_ref)

        @pl.when(r == 0)
        def _():
            ds_scr[...] = jnp.zeros_like(ds_scr)

        gates, gate_vjps = [], []
        for b in range(nb):
            out, gvjp = jax.vjp(gate_fn, small_ref[b], alp_ref[...], dtp_ref[...])
            gates.append(out)
            gate_vjps.append(gvjp)
        t_saved = tinv_ref[...]
        _, vjp = jax.vjp(lambda *args: dn_chunk(*args, t_saved)[:2], *_qkv_stacks(qkv_ref, nb), *_gate_stacks(gates, nb),
                         sprev_ref[...])
        d_out = jnp.stack([do_ref[b, :, hd * DH:(hd + 1) * DH] for b in range(nb) for hd in range(NH)])
        grads = vjp((d_out, ds_scr[...]))
        ds_scr[...] = grads[6]
        lane = _iota2((CH, LANES), 1)
        rowi = _iota2((LANES, CH), 0)
        for b in range(nb):
            d_beta = jnp.zeros((CH, LANES), F32)
            d_gc = jnp.zeros((CH, LANES), F32)
            d_gct = jnp.zeros((LANES, CH), F32)
            for hd in range(NH):
                i = b * NH + hd
                for part in range(3):
                    dqkv_ref[b, :, _dn_cols(part, hd)] = grads[part][i]
                d_beta = d_beta + jnp.where(lane == hd, grads[3][i], 0.0)
                d_gc = d_gc + jnp.where(lane == NH + hd, grads[4][i], 0.0)
                d_gct = d_gct + jnp.where(rowi == NH + hd, grads[5][i], 0.0)
            d_small, d_alp, d_dtp = gate_vjps[b]((d_beta, d_gc, d_gct))
            dsmall_ref[b] = d_small.astype(BF16)
            dalp_ref[...] += d_alp
            ddtp_ref[...] += d_dtp
        if extra:
            pl.when((bb == bl // nb - 1) & (r == nc - 1))(finish)

    blk = lambda bb, r: (bb, nc - 1 - r, 0)
    const = lambda bb, r: (0, 0)
    saved = pl.BlockSpec((None, ng, DH, DH), lambda bb, r: (bb * nc + nc - 1 - r, 0, 0, 0))
    return pl.pallas_call(
        body, grid=(bl // nb, nc),
        in_specs=[pl.BlockSpec((nb, CH, 3 * DNW), blk), pl.BlockSpec((nb, CH, LANES), blk), pl.BlockSpec((1, LANES), const),
                  pl.BlockSpec((1, LANES), const), saved, saved, pl.BlockSpec((nb, CH, DNW), blk)] + [HBM_SPEC] * len(extra),
        out_specs=[pl.BlockSpec((nb, CH, 3 * DNW), blk), pl.BlockSpec((nb, CH, LANES), blk), pl.BlockSpec((1, LANES), const),
                   pl.BlockSpec((1, LANES), const)] + [HBM_SPEC] * len(extra),
        out_shape=[SDS((bl, seq, 3 * DNW), F32), SDS((bl, seq, LANES), BF16), SDS((1, LANES), F32), SDS((1, LANES), F32)]
        + [SDS(x.shape, x.dtype) for x in extra],
        scratch_shapes=[pltpu.VMEM((ng, DH, DH), F32)] + (_comm_scratch() if extra else []), name="deltanet_bwd",
        compiler_params=_cp(2))(qkv, p_small, alp, dtp, sprev, tinv, d_o, *extra)


def _s5_table_specs():
    tab3 = pl.BlockSpec((None, LANES, 512), lambda gb, n: (gb, 0, 0))
    tab2 = pl.BlockSpec((S5_CH, 512), lambda gb, n: (0, gb))
    return [tab3] * 4 + [tab2] * 6 + [pl.BlockSpec((1, LANES), lambda gb, n: (0, gb))]


def s5_fwd(u, tables, dsk):
    bl, seq, _ = u.shape
    nc = seq // S5_CH

    def body(u_ref, *rest):
        tabs, (y_ref, xs_ref, xr_scr, xi_scr) = rest[:11], rest[11:]

        @pl.when(pl.program_id(1) == 0)
        def _():
            xr_scr[...] = jnp.zeros_like(xr_scr)
            xi_scr[...] = jnp.zeros_like(xi_scr)

        xp_re, xp_im = xr_scr[...], xi_scr[...]
        xs_ref[0:bl] = xp_re
        xs_ref[bl:2 * bl] = xp_im
        y, xn_re, xn_im = s5_chunk(u_ref[...], xp_re, xp_im, *[t[...] for t in tabs])
        y_ref[...] = y
        xr_scr[...] = xn_re
        xi_scr[...] = xn_im

    blk = lambda gb, n: (0, n, gb)
    return pl.pallas_call(
        body, grid=(GB, nc), in_specs=[pl.BlockSpec((bl, S5_CH, LANES), blk)] + _s5_table_specs(),
        out_specs=[pl.BlockSpec((bl, S5_CH, LANES), blk),
                   pl.BlockSpec((None, 2 * bl, 1, 512), lambda gb, n: (gb * nc + n, 0, 0, 0))],
        out_shape=[SDS((bl, seq, S5W), F32), SDS((GB * nc, 2 * bl, 1, 512), F32)],
        scratch_shapes=[pltpu.VMEM((bl, 1, 512), F32), pltpu.VMEM((bl, 1, 512), F32)], name="s5_fwd",
        compiler_params=_cp(2))(u, *tables, dsk)


def s5_bwd(u, tables, dsk, xs, dy):
    bl, seq, _ = u.shape
    nc = seq // S5_CH

    def body(u_ref, *rest):
        tabs, xs_ref, dy_ref = rest[:11], rest[11], rest[12]
        du_ref, dtabs, dxr_scr, dxi_scr = rest[13], rest[14:25], rest[25], rest[26]
        r = pl.program_id(1)

        @pl.when(r == 0)
        def _():
            for t in dtabs:
                t[...] = jnp.zeros_like(t)
            dxr_scr[...] = jnp.zeros_like(dxr_scr)
            dxi_scr[...] = jnp.zeros_like(dxi_scr)

        _, vjp = jax.vjp(s5_chunk, u_ref[...], xs_ref[0:bl], xs_ref[bl:2 * bl], *[t[...] for t in tabs])
        grads = vjp((dy_ref[...], dxr_scr[...], dxi_scr[...]))
        du_ref[...] = grads[0].astype(BF16)
        dxr_scr[...] = grads[1]
        dxi_scr[...] = grads[2]
        for t, g in zip(dtabs, grads[3:]):
            t[...] += g

    blk = lambda gb, r: (0, nc - 1 - r, gb)
    tab_shapes = [SDS(t.shape, F32) for t in tables] + [SDS(dsk.shape, F32)]
    return pl.pallas_call(
        body, grid=(GB, nc),
        in_specs=[pl.BlockSpec((bl, S5_CH, LANES), blk)] + _s5_table_specs()
        + [pl.BlockSpec((None, 2 * bl, 1, 512), lambda gb, r: (gb * nc + nc - 1 - r, 0, 0, 0)), pl.BlockSpec((bl, S5_CH, LANES), blk)],
        out_specs=[pl.BlockSpec((bl, S5_CH, LANES), blk)] + _s5_table_specs(),
        out_shape=[SDS((bl, seq, S5W), BF16)] + tab_shapes,
        scratch_shapes=[pltpu.VMEM((bl, 1, 512), F32), pltpu.VMEM((bl, 1, 512), F32)], name="s5_bwd",
        compiler_params=_cp(2))(u, *tables, dsk, xs, dy)


def s5_tables_fwd(params):
    shapes = [SDS((GB, LANES, 512), F32)] * 4 + [SDS((S5_CH, S5N), F32)] * 6

    def body(*refs):
        for r, t in zip(refs[7:], s5_tables(*[p[...] for p in refs[:7]])):
            r[...] = t

    return pl.pallas_call(body, out_shape=shapes, name="s5_tables_fwd", compiler_params=_cp())(*params)


def s5_tables_bwd(params, dtables):
    def body(*refs):
        _, vjp = jax.vjp(s5_tables, *[p[...] for p in refs[:7]])
        for r, g in zip(refs[17:], vjp(tuple(t[...] for t in refs[7:17]))):
            r[...] = g

    return pl.pallas_call(body, out_shape=[SDS(p.shape, F32) for p in params], name="s5_tables_bwd",
                          compiler_params=_cp())(*params, *dtables)


def ada_fwd(c_all, w_loc, b_loc):
    def body(c_ref, w_ref, b_ref, o_ref):
        o_ref[...] = _dot(_silu(c_ref[...]), w_ref[...]) + b_ref[...]

    return pl.pallas_call(body, out_shape=SDS((c_all.shape[0], w_loc.shape[1]), F32), name="ada_fwd",
                          compiler_params=_cp())(c_all, w_loc, b_loc)


def ada_bwd(c_all, dmod_mine, dmod_all):
    def body(c_ref, dm_ref, da_ref, gw_ref, gb_ref):
        gw_ref[...] = _dot_tn(_silu(c_ref[...]), dm_ref[...])
        gb_ref[...] = jnp.sum(da_ref[...], axis=0, keepdims=True)

    return pl.pallas_call(body, out_shape=[SDS((D, dmod_mine.shape[1]), F32), SDS((1, dmod_all.shape[1]), F32)],
                          name="ada_bwd", compiler_params=_cp())(c_all, dmod_mine, dmod_all)


def adamw(name, parts, w, m, v):
    k_parts, rows, cols = parts.shape
    tr = _pick(rows, (256, 128, 64, 32, 16, 8))

    def body(p_ref, w_ref, m_ref, v_ref, g_ref, d_ref, mo_ref, vo_ref):
        g = p_ref[0].astype(F32)
        for k in range(1, k_parts):
            g = g + p_ref[k].astype(F32)
        _adam_store(g, w_ref, m_ref, v_ref, g_ref, d_ref, mo_ref, vo_ref)

    blk = pl.BlockSpec((tr, cols), lambda i: (i, 0))
    return pl.pallas_call(
        body, grid=(rows // tr,), in_specs=[pl.BlockSpec((k_parts, tr, cols), lambda i: (0, i, 0)), blk, blk, blk],
        out_specs=[blk] * 4, out_shape=[SDS((rows, cols), F32)] * 4, name=name, compiler_params=_cp(1))(parts, w, m, v)


def _adam_store(g, w_ref, m_ref, v_ref, g_ref, d_ref, mo_ref, vo_ref):
    m_new = ADAM_B1 * m_ref[...] + (1.0 - ADAM_B1) * g
    v_new = ADAM_B2 * v_ref[...] + (1.0 - ADAM_B2) * (g * g)
    m_hat = m_new / (1.0 - ADAM_B1 ** ADAM_STEP)
    v_hat = v_new / (1.0 - ADAM_B2 ** ADAM_STEP)
    g_ref[...] = g
    d_ref[...] = -ADAM_LR * (m_hat / (jnp.sqrt(v_hat) + ADAM_EPS) + ADAM_WD * w_ref[...])
    mo_ref[...] = m_new
    vo_ref[...] = v_new


def adamw_t(name, parts, w, m, v):
    k_parts, r, c = parts.shape
    tc = _pick(c, (256, 128))

    def body(p_ref, w_ref, m_ref, v_ref, g_ref, d_ref, mo_ref, vo_ref):
        gt = p_ref[0].astype(F32)
        for k in range(1, k_parts):
            gt = gt + p_ref[k].astype(F32)
        _adam_store(gt.T, w_ref, m_ref, v_ref, g_ref, d_ref, mo_ref, vo_ref)

    blk = pl.BlockSpec((tc, r), lambda j: (j, 0))
    return pl.pallas_call(
        body, grid=(c // tc,), in_specs=[pl.BlockSpec((k_parts, r, tc), lambda j: (0, 0, j)), blk, blk, blk],
        out_specs=[blk] * 4, out_shape=[SDS((c, r), F32)] * 4, name=name, compiler_params=_cp(1))(parts, w, m, v)


def _comm_scratch():
    return [pltpu.SemaphoreType.DMA((7,)), pltpu.SemaphoreType.DMA((7,)), pltpu.SemaphoreType.DMA]


HBM_SPEC = pl.BlockSpec(memory_space=pl.ANY)


def _gather_phases(x_ref, out_ref, send_sems, recv_sems, local_sem):
    mx, my, mc = lax.axis_index("x"), lax.axis_index("y"), lax.axis_index("c")
    me, sibling = (mx, my, mc), (mx, my, 1 - mc)
    chips = [(1 - mx, my), (mx, 1 - my), (1 - mx, 1 - my)]

    def slot(px, py, pc):
        return out_ref.at[4 * px + 2 * py + pc]

    def copy(k, block, to, src=None):
        return pltpu.make_async_remote_copy(
            src_ref=slot(*block) if src is None else src, dst_ref=slot(*block), send_sem=send_sems.at[k],
            recv_sem=recv_sems.at[k], device_id=to, device_id_type=pl.DeviceIdType.MESH)

    def first():
        return [copy(0, me, sibling, src=x_ref)] + [copy(1 + j, me, (*chip, mc), src=x_ref) for j, chip in enumerate(chips)]

    def passed():
        return [copy(4 + j, (*chip, mc), sibling) for j, chip in enumerate(chips)]

    def start():
        pltpu.make_async_copy(x_ref, slot(*me), local_sem).start()
        for cp in first():
            cp.start()

    def forward():
        for j, chip in enumerate(chips):
            copy(1 + j, (*chip, mc), me).wait_recv()
            passed()[j].start()

    def finish():
        copy(0, sibling, me).wait_recv()
        for j, chip in enumerate(chips):
            copy(4 + j, (*chip, 1 - mc), me).wait_recv()
        for cp in first() + passed():
            cp.wait_send()
        pltpu.make_async_copy(x_ref, slot(*me), local_sem).wait()

    return start, forward, finish


def _exchange_phases(x_ref, out_ref, send_sems, recv_sems, local_sem):
    mx, my, mc = lax.axis_index("x"), lax.axis_index("y"), lax.axis_index("c")
    me = 4 * mx + 2 * my + mc

    def peer(k):
        return mx ^ (k >> 2), my ^ ((k >> 1) & 1), mc ^ (k & 1)

    def sends():
        out = []
        for k in range(1, NDEV):
            px, py, pc = peer(k)
            out.append(pltpu.make_async_remote_copy(
                src_ref=x_ref.at[4 * px + 2 * py + pc], dst_ref=out_ref.at[me], send_sem=send_sems.at[k - 1],
                recv_sem=recv_sems.at[k - 1], device_id=(px, py, pc), device_id_type=pl.DeviceIdType.MESH))
        return out

    def start():
        pltpu.make_async_copy(x_ref.at[me], out_ref.at[me], local_sem).start()
        for cp in sends():
            cp.start()

    def finish():
        for k in range(1, NDEV):
            px, py, pc = peer(k)
            pltpu.make_async_remote_copy(
                src_ref=x_ref.at[me], dst_ref=out_ref.at[4 * px + 2 * py + pc], send_sem=send_sems.at[k - 1],
                recv_sem=recv_sems.at[k - 1], device_id=(px, py, pc), device_id_type=pl.DeviceIdType.MESH).wait_recv()
        for cp in sends():
            cp.wait_send()
        pltpu.make_async_copy(x_ref.at[me], out_ref.at[me], local_sem).wait()

    return start, finish


def all_gather(name, x):
    def body(x_ref, out_ref, send_sems, recv_sems, local_sem):
        for phase in _gather_phases(x_ref, out_ref, send_sems, recv_sems, local_sem):
            phase()

    return pl.pallas_call(body, out_shape=SDS((NDEV,) + x.shape, x.dtype), in_specs=[HBM_SPEC], out_specs=HBM_SPEC,
                          scratch_shapes=_comm_scratch(), name=name)(x)


def all_gather_pair(name, x1, x2):
    def body(x1_ref, x2_ref, o1_ref, o2_ref, *sems):
        first = _gather_phases(x1_ref, o1_ref, *sems[:3])
        second = _gather_phases(x2_ref, o2_ref, *sems[3:])
        for phase1, phase2 in zip(first, second):
            phase1()
            phase2()

    return pl.pallas_call(
        body, out_shape=[SDS((NDEV,) + x1.shape, x1.dtype), SDS((NDEV,) + x2.shape, x2.dtype)], in_specs=[HBM_SPEC] * 2,
        out_specs=[HBM_SPEC] * 2, scratch_shapes=_comm_scratch() + _comm_scratch(), name=name)(x1, x2)


def all_to_all(name, x):
    def body(x_ref, out_ref, send_sems, recv_sems, local_sem):
        for phase in _exchange_phases(x_ref, out_ref, send_sems, recv_sems, local_sem):
            phase()

    return pl.pallas_call(body, out_shape=SDS(x.shape, x.dtype), in_specs=[HBM_SPEC], out_specs=HBM_SPEC,
                          scratch_shapes=_comm_scratch(), name=name)(x)


def _pack(arrs, dtype, row_mult=8):
    segs = []
    for a in arrs:
        flat = a.reshape(-1).astype(dtype)
        segs.append(jnp.pad(flat, (0, (-flat.shape[0]) % ROW)))
    flat = jnp.concatenate(segs)
    flat = jnp.pad(flat, (0, (-flat.shape[0]) % (ROW * row_mult)))
    return flat.reshape(-1, ROW)


def _unpack(buf, shapes):
    flat = buf.reshape(-1)
    out, off = [], 0
    for s in shapes:
        n = math.prod(s)
        out.append(flat[off:off + n].reshape(s))
        off += n + (-n) % ROW
    return out


def _pack_rows(arrs, axis):
    padded = []
    for t in arrs:
        pad = [(0, 0)] * t.ndim
        pad[axis] = (0, _tile_rows(t.shape[axis]) - t.shape[axis])
        padded.append(jnp.pad(t, pad))
    return jnp.concatenate(padded, axis=axis)


def _tile_rows(r):
    return r + (-r) % BF16_TILE_ROWS


def _unpack8(buf, shapes):
    flat = buf.reshape(NDEV, -1)
    out, off = [], 0
    for s in shapes:
        n = math.prod(s)
        out.append(flat[:, off:off + n].reshape((NDEV,) + tuple(s)))
        off += n + (-n) % ROW
    return out


def kernel(x, c, w_ada, b_ada, g_ffn1, w1_ffn1, w3_ffn1, w2_ffn1, g_mix, w_in, conv_qkv, a_log, dt_bias, g_onorm, lam_re, lam_im, log_step, b_re, b_im, c_re, c_im, d_skip, w_glu, b_glu, w_proj_a, w_proj_b, w_out, g_ffn2, w1_ffn2, w3_ffn2, w2_ffn2, g_final, loss_target, m_w_ada, m_b_ada, m_g_ffn1, m_w1_ffn1, m_w3_ffn1, m_w2_ffn1, m_g_mix, m_w_in, m_conv_qkv, m_a_log, m_dt_bias, m_g_onorm, m_lam_re, m_lam_im, m_log_step, m_b_re, m_b_im, m_c_re, m_c_im, m_d_skip, m_w_glu, m_b_glu, m_w_proj_a, m_w_proj_b, m_w_out, m_g_ffn2, m_w1_ffn2, m_w3_ffn2, m_w2_ffn2, m_g_final, v_w_ada, v_b_ada, v_g_ffn1, v_w1_ffn1, v_w3_ffn1, v_w2_ffn1, v_g_mix, v_w_in, v_conv_qkv, v_a_log, v_dt_bias, v_g_onorm, v_lam_re, v_lam_im, v_log_step, v_b_re, v_b_im, v_c_re, v_c_im, v_d_skip, v_w_glu, v_b_glu, v_w_proj_a, v_w_proj_b, v_w_out, v_g_ffn2, v_w1_ffn2, v_w3_ffn2, v_w2_ffn2, v_g_final):
    a = dict(locals())
    bl, seq, _ = x.shape
    t_rows = bl * seq
    me = 4 * lax.axis_index("x") + 2 * lax.axis_index("y") + lax.axis_index("c")
    tm_ew = _pick(seq, (256, 128, 64))

    loc = {n: (a[n][0].T if n in COL_SHARDED else a[n][0]) for n in RS_WEIGHTS}
    wfull, gw, res = {}, {}, {}

    def pack_local(names):
        return _pack_rows([loc[n].astype(BF16).reshape(-1, ROW) for n in names], 0)

    def unpack_full(buf, names):
        r0 = 0
        for n in names:
            r = loc[n].size // ROW
            wfull[n] = buf[:, r0:r0 + r, :].reshape(-1, loc[n].shape[1])
            r0 += _tile_rows(r)

    def pack_grads(names):
        return _pack_rows([gw[n].astype(BF16).reshape(NDEV, -1, ROW) for n in names], 1)

    def update(buf, names):
        r0 = 0
        for n in names:
            r = loc[n].size // ROW
            parts = buf[:, r0:r0 + r, :].reshape((NDEV,) + loc[n].shape)
            r0 += _tile_rows(r)
            step = adamw_t if n in COL_SHARDED else adamw
            out = step("adamw_" + n, parts, a[n][0], a["m_" + n][0], a["v_" + n][0])
            for kind, t in zip(("grad", "delta", "new_m", "new_v"), out):
                res[kind + "_" + n] = t[None]

    sm, wg_ffn1 = all_gather_pair("gather_inputs", _pack([c, conv_qkv[0]], F32), pack_local(G_FFN1))
    unpack_full(wg_ffn1, G_FFN1)
    c_loc, conv_loc = _unpack8(sm, [c.shape, conv_qkv.shape[1:]])
    c_all = c_loc.reshape(NDEV * bl, D)
    conv_full = conv_loc.transpose(1, 0, 2).reshape(CONVW, 3 * DNW)

    n_ada = w_ada.shape[2]
    mod_part = ada_fwd(c_all, w_ada[0], lax.dynamic_slice(b_ada, (0, me * n_ada), (1, n_ada)))
    mod_all = all_gather("gather_mod", mod_part).transpose(1, 0, 2).reshape(NDEV * bl, 9 * D)
    mod = lax.dynamic_slice(mod_all, (me * bl, 0), (bl, 9 * D)).reshape(bl, 9, D)
    mods = [mod[:, k:k + 1, :] for k in range(9)]

    h0 = x.reshape(t_rows, D)
    h1, f1, u1, pa1, pb1, wg_rest = ffn_fwd("ffn1_fwd", h0, mod[:, 0:3, :], g_ffn1, wfull['w1_ffn1'], wfull['w3_ffn1'],
                                  wfull['w2_ffn1'], seq, gather=pack_local(G_MIX))
    unpack_full(wg_rest, G_MIX)
    win = wfull['w_in']
    o_small, o_s5, o_gate = 4 * DNW, 4 * DNW + 2 * NH, 4 * DNW + 2 * NH + S5W
    w_dn, w_small = win[:o_small], jnp.pad(win[o_small:o_s5], ((0, LANES - 2 * NH), (0, 0)))
    w_s5, w_gate = win[o_s5:o_gate], win[o_gate:]
    w_pieces = [w_dn, w_small, w_s5, w_gate]
    u2, p_dn, p_small, p_s5, p_gate = mix_in_fwd(h1, mods[3], mods[4], g_mix, w_pieces, seq)

    conv8 = jnp.pad(conv_full, ((0, 8 - CONVW), (0, 0)))
    alp = jnp.pad(a_log, ((0, 0), (NH, LANES - 2 * NH)))
    dtp = jnp.pad(dt_bias, ((0, 0), (NH, LANES - 2 * NH)))
    nb_dn = DN_ROWS if bl % DN_ROWS == 0 else 1
    p_dn3, p_small3 = p_dn.reshape(bl, seq, 4 * DNW), p_small.reshape(bl, seq, LANES)
    qkv3 = dn_prep_fwd(p_dn3, conv8)
    o_pre3, sprev, tinv, wg_ffn2 = deltanet_fwd(qkv3, p_small3, alp, dtp, nb_dn, gather=pack_local(G_FFN2))
    unpack_full(wg_ffn2, G_FFN2)
    o_pre = o_pre3.reshape(t_rows, DNW)
    z_raw = p_dn[:, 3 * DNW:]

    s5_params = [lam_re.reshape(1, S5N), lam_im.reshape(1, S5N), log_step,
                 b_re[0].transpose(2, 0, 1).reshape(S5C, S5N), b_im[0].transpose(2, 0, 1).reshape(S5C, S5N),
                 c_re[0].transpose(1, 0, 2).reshape(S5C, S5N), c_im[0].transpose(1, 0, 2).reshape(S5C, S5N)]
    tables = s5_tables_fwd(s5_params)
    p_s53 = p_s5.reshape(bl, seq, S5W)
    y_s53, xs = s5_fwd(p_s53, tables, d_skip)
    y_s5 = y_s53.reshape(t_rows, S5W)
    tail_in = [o_pre, z_raw, y_s5, p_gate]
    tail_w = [g_onorm, wfull['w_glu'], b_glu, wfull['w_proj_a'], wfull['w_proj_b']]
    (merged,) = ew_call("mix_tail", fn_mix_tail, tail_in, [], tail_w, [(D, BF16)], tm_ew, seq)
    mo, h2 = mix_out_fwd(merged, wfull['w_out'], h1, mods[5], seq)
    dh3, f3, u3, pa3, pb3, dg_final, loss_part = ffn_fwd(
        "ffn2_fwd", h2, mod[:, 6:9, :], g_ffn2, wfull['w1_ffn2'], wfull['w3_ffn2'], wfull['w2_ffn2'], seq,
        loss_head=(loss_target.reshape(t_rows, D), g_final.reshape(1, D)))


    dh2, a3, d1_3, d3_3, df3, dmod_c, dg_ffn2 = ffn_bwd("ffn2_bwd", dh3, h2, f3, pa3, pb3, mod[:, 6:9, :], g_ffn2, wfull['w1_ffn2'],
                                                   wfull['w3_ffn2'], wfull['w2_ffn2'], seq)
    gw['w1_ffn2'] = mm_tn("gw1_ffn2", d1_3, u3)
    gw['w3_ffn2'] = mm_tn("gw3_ffn2", d3_3, u3)
    gw['w2_ffn2'] = mm_tn("gw2_ffn2", a3, df3)

    dmo, d_merged, dgt2 = mix_out_bwd(dh2, mo, wfull['w_out'], mods[5], seq)
    gw['w_out'] = mm_tn("gw_out", merged, dmo)
    (d_opre, d_z, d_ys5, d_gate), _, tail_gw = ew_vjp_call(
        "mix_tail_bwd", fn_mix_tail, tail_in, [], tail_w, [d_merged], [(0, F32), (1, F32), (2, F32), (3, BF16)],
        _pick(seq, (512, 256, 128, 64)), seq)
    dg_onorm, gw['w_glu'], dg_bglu, gw['w_proj_a'], gw['w_proj_b'] = tail_gw
    d_qkv3, d_psmall3, d_alp, d_dtp, rs_ffn2 = deltanet_bwd(
        qkv3, p_small3, alp, dtp, sprev, tinv, d_opre.reshape(bl, seq, DNW), nb_dn, exchange=pack_grads(G_FFN2))
    d_pdn3, d_conv8 = dn_prep_bwd(p_dn3, conv8, d_qkv3, d_z.reshape(bl, seq, DNW))
    d_pdn, d_psmall = d_pdn3.reshape(t_rows, 4 * DNW), d_psmall3.reshape(t_rows, LANES)

    s5_out = s5_bwd(p_s53, tables, d_skip, xs, d_ys5.reshape(bl, seq, S5W))
    d_ps5, d_tables, dg_dskip = s5_out[0].reshape(t_rows, S5W), s5_out[1:11], s5_out[11]
    d_s5p = s5_tables_bwd(s5_params, d_tables)

    gw['w_in'] = jnp.concatenate([mm_tn("gw_dn", d_pdn, u2), mm_tn("gw_small", d_psmall, u2)[:2 * NH],
                                  mm_tn("gw_s5", d_ps5, u2), mm_tn("gw_gate", d_gate, u2)], axis=0)
    dh1, dsh2, dsc2, dg_mix = mix_in_bwd([d_pdn, d_psmall, d_ps5, d_gate], w_pieces, h1, mods[3], mods[4], g_mix, dh2, seq)

    dh0, a1, d1_1, d3_1, df1, dmod_a, dg_ffn1, rs_mix = ffn_bwd(
        "ffn1_bwd", dh1, h0, f1, pa1, pb1, mod[:, 0:3, :], g_ffn1, wfull['w1_ffn1'], wfull['w3_ffn1'], wfull['w2_ffn1'], seq,
        exchange=pack_grads(G_MIX))
    dmod_mine = jnp.concatenate([dmod_a, dsh2, dsc2, dgt2, dmod_c], axis=1).reshape(bl, 9 * D)
    small_grads = {
        'g_ffn1': dg_ffn1, 'g_mix': dg_mix, 'a_log': d_alp[:, NH:2 * NH], 'dt_bias': d_dtp[:, NH:2 * NH],
        'g_onorm': dg_onorm, 'lam_re': d_s5p[0].reshape(1, S5G, S5P), 'lam_im': d_s5p[1].reshape(1, S5G, S5P),
        'log_step': d_s5p[2],
        'b_re': d_s5p[3].reshape(S5C, S5G, S5P).transpose(1, 2, 0)[None],
        'b_im': d_s5p[4].reshape(S5C, S5G, S5P).transpose(1, 2, 0)[None],
        'c_re': d_s5p[5].reshape(S5C, S5G, S5P).transpose(1, 0, 2)[None],
        'c_im': d_s5p[6].reshape(S5C, S5G, S5P).transpose(1, 0, 2)[None],
        'd_skip': dg_dskip, 'b_glu': dg_bglu, 'g_ffn2': dg_ffn2, 'g_final': dg_final.reshape(D)}
    small_shapes = [a[n].shape for n in SMALL]
    small_pack = _pack([small_grads[n] for n in SMALL] + [loss_part], F32)
    n_small = small_pack.shape[0]
    small_buf = jnp.concatenate([small_pack, _pack([dmod_mine, d_conv8[:CONVW]], F32)], axis=0)

    gw['w1_ffn1'], sg = mm_tn("gw1_ffn1", d1_1, u1, gather=small_buf)
    gw['w3_ffn1'], rs_w1 = mm_tn("gw3_ffn1", d3_1, u1, exchange=pack_grads(['w1_ffn1']))
    gw['w2_ffn1'], rs_w3 = mm_tn("gw2_ffn1", a1, df1, exchange=pack_grads(['w3_ffn1']))
    rs_w2 = all_to_all("scatter_w2_ffn1", pack_grads(['w2_ffn1']))

    update(rs_ffn2, G_FFN2)
    update(rs_mix, G_MIX)
    update(rs_w1, ['w1_ffn1'])
    update(rs_w3, ['w3_ffn1'])
    update(rs_w2, ['w2_ffn1'])
    pieces = _unpack8(sg[:, n_small:, :], [dmod_mine.shape, (CONVW, 3 * DNW)])
    dmod_all = pieces[0].reshape(NDEV * bl, 9 * D)
    g_wada, g_bada = ada_bwd(c_all, lax.dynamic_slice(dmod_all, (0, me * n_ada), (NDEV * bl, n_ada)), dmod_all)

    n_conv = conv_qkv.shape[2]
    conv_parts = lax.dynamic_slice(pieces[1], (0, 0, me * n_conv), (NDEV, CONVW, n_conv))
    conv_parts = jnp.pad(conv_parts.reshape(NDEV, 1, -1), ((0, 0), (0, 7), (0, 0)))
    pad8 = lambda t: jnp.pad(t.reshape(1, -1), ((0, 7), (0, 0)))
    conv_res = adamw("adamw_conv", conv_parts, pad8(conv_qkv), pad8(m_conv_qkv), pad8(v_conv_qkv))
    for kind, buf in zip(("grad", "delta", "new_m", "new_v"), conv_res):
        res[kind + "_conv_qkv"] = buf[0].reshape(conv_qkv.shape)

    no_param = jnp.zeros_like(loss_part)
    small_res = adamw("adamw_small", sg[:, :n_small, :],
                      *[_pack([a[p + n] for n in SMALL] + [no_param], F32) for p in ("", "m_", "v_")])
    for kind, buf in zip(("grad", "delta", "new_m", "new_v"), small_res):
        for n, t in zip(SMALL, _unpack(buf, small_shapes)):
            res[kind + "_" + n] = t
    loss = _unpack(small_res[0], small_shapes + [loss_part.shape])[-1][0, 0]

    for n, g in (("w_ada", g_wada), ("b_ada", g_bada)):
        shp = a[n].shape
        r2 = lambda t: t.reshape(-1, shp[-1]) if n == "w_ada" else pad8(t)
        out = adamw("adamw_" + n, r2(g)[None], r2(a[n]), r2(a["m_" + n]), r2(a["v_" + n]))
        for kind, buf in zip(("grad", "delta", "new_m", "new_v"), out):
            res[kind + "_" + n] = (buf if n == "w_ada" else buf[0:1]).reshape(shp)

    outs = [loss, dh0.reshape(x.shape)]
    for kind in ("grad", "delta", "new_m", "new_v"):
        outs += [res[kind + "_" + n] for n in WEIGHTS]
    return tuple(outs)
```

```python
import math

import jax
import jax.numpy as jnp
from jax import lax
from jax.experimental import pallas as pl
from jax.experimental.pallas import tpu as pltpu

F32 = jnp.float32
BF16 = jnp.bfloat16
HI = lax.Precision.HIGHEST
H3 = lax.Precision.HIGH
SDS = jax.ShapeDtypeStruct

D = 1024
FF = 2816
FFN_TF = FF
FFN_FWD_TM = 256
FFN_BWD_TM = 256
NH = 8
DH = 64
DNW = NH * DH
CONVW = 4
CH = 64
S5_CH = 128
ACC_LIMIT = 6 * 1024 * 1024
BF16_TILE_ROWS = 16
DN_ROWS = 4
S5W = 512
S5G = 32
S5P = 64
S5C = 16
S5N = S5G * S5P
GB = 4
NDEV = 8
EPS = 1e-6
LANES = 128
ROW = 1024
VMEM_LIMIT = 56 * 1024 * 1024

ADAM_LR, ADAM_B1, ADAM_B2, ADAM_EPS, ADAM_WD, ADAM_STEP = 0.001, 0.9, 0.999, 1e-08, 0.01, 10

WEIGHTS = ['w_ada', 'b_ada', 'g_ffn1', 'w1_ffn1', 'w3_ffn1', 'w2_ffn1', 'g_mix', 'w_in', 'conv_qkv', 'a_log',
           'dt_bias', 'g_onorm', 'lam_re', 'lam_im', 'log_step', 'b_re', 'b_im', 'c_re', 'c_im', 'd_skip', 'w_glu',
           'b_glu', 'w_proj_a', 'w_proj_b', 'w_out', 'g_ffn2', 'w1_ffn2', 'w3_ffn2', 'w2_ffn2', 'g_final']
RS_WEIGHTS = ['w1_ffn1', 'w3_ffn1', 'w2_ffn1', 'w_in', 'w_glu', 'w_proj_a', 'w_proj_b', 'w_out', 'w1_ffn2', 'w3_ffn2',
              'w2_ffn2']
COL_SHARDED = {'w1_ffn1', 'w3_ffn1', 'w_in', 'w_proj_a', 'w_proj_b', 'w1_ffn2', 'w3_ffn2'}
G_FFN1 = ['w1_ffn1', 'w3_ffn1', 'w2_ffn1']
G_MIX = ['w_in', 'w_glu', 'w_proj_a', 'w_proj_b', 'w_out']
G_FFN2 = ['w1_ffn2', 'w3_ffn2', 'w2_ffn2']
SMALL = ['g_ffn1', 'g_mix', 'a_log', 'dt_bias', 'g_onorm', 'lam_re', 'lam_im', 'log_step', 'b_re', 'b_im', 'c_re',
         'c_im', 'd_skip', 'b_glu', 'g_ffn2', 'g_final']


def _cp(n_grid=0):
    if n_grid:
        return pltpu.CompilerParams(vmem_limit_bytes=VMEM_LIMIT, dimension_semantics=("arbitrary",) * n_grid)
    return pltpu.CompilerParams(vmem_limit_bytes=VMEM_LIMIT)


def _dot(a, b):
    return jnp.dot(a.astype(BF16), b.astype(BF16), preferred_element_type=F32)


def _dot_nt(a, b):
    return lax.dot_general(a.astype(BF16), b.astype(BF16), (((1,), (1,)), ((), ())), preferred_element_type=F32)


def _dot_tn(a, b):
    return lax.dot_general(a.astype(BF16), b.astype(BF16), (((0,), (0,)), ((), ())), preferred_element_type=F32)


def _dot_hi(a, b):
    return jnp.dot(a, b, precision=HI, preferred_element_type=F32)


@jax.custom_vjp
def bdot(a, b):
    return _dot(a, b)


bdot.defvjp(lambda a, b: (_dot(a, b), (a, b)),
            lambda r, g: (_dot_nt(g, r[1]).astype(r[0].dtype), _dot_tn(r[0], g).astype(r[1].dtype)))


@jax.custom_vjp
def bdot_nt(a, b):
    return _dot_nt(a, b)


bdot_nt.defvjp(lambda a, b: (_dot_nt(a, b), (a, b)),
               lambda r, g: (_dot(g, r[1]).astype(r[0].dtype), _dot_tn(g, r[0]).astype(r[1].dtype)))


def _silu(x):
    return x * jax.nn.sigmoid(x)


def _iota2(shape, axis):
    return lax.broadcasted_iota(jnp.int32, shape, axis)


def normmod(h, g, sc, sh):
    y = h * lax.rsqrt(jnp.mean(h * h, axis=-1, keepdims=True) + EPS) * g
    return y * (1.0 + sc) + sh


def fn_merge(gate, ya, yb):
    return (jax.nn.sigmoid(gate[:, :D]) * ya + jax.nn.sigmoid(gate[:, D:]) * yb,)


def fn_glu(y, w, b):
    ge = jax.nn.gelu(y)
    return (ge * jax.nn.sigmoid(bdot(ge, w) + b),)


def fn_onorm(o, z, g_on):
    r = _iota2((DH, DNW), 0)
    c = _iota2((DH, DNW), 1)
    expand = (c % DH == r).astype(F32)
    r2 = _iota2((DNW, DNW), 0)
    c2 = _iota2((DNW, DNW), 1)
    avg = (r2 // DH == c2 // DH).astype(F32) * (1.0 / DH)
    ms = bdot(o * o, avg)
    return (o * lax.rsqrt(ms + EPS) * _dot_hi(g_on, expand) * _silu(z),)


def fn_mix_tail(o_pre, z, y_s5, gate, g_on, w_glu, b_glu, wa_t, wb_t):
    (oa,) = fn_onorm(o_pre, z, g_on)
    (ob,) = fn_glu(y_s5, w_glu, b_glu)
    return fn_merge(gate, bdot_nt(oa, wa_t), bdot_nt(ob, wb_t))


def gate_fn(small, alp, dtp):
    beta = jax.nn.sigmoid(small)
    la = -jnp.exp(alp) * jax.nn.softplus(small + dtp)
    tri = (_iota2((CH, CH), 0) >= _iota2((CH, CH), 1)).astype(F32)
    gc = _dot_hi(tri, la)
    gct = lax.dot_general(la, tri, (((0,), (1,)), ((), ())), precision=HI, preferred_element_type=F32)
    return beta, gc, gct


def _bdg(a, b, ca, cb, hi):
    if not hi:
        a, b = a.astype(BF16), b.astype(BF16)
    return lax.dot_general(a, b, (((ca,), (cb,)), ((0,), (0,))), precision=H3 if hi else None,
                           preferred_element_type=F32)


def _batched_matmuls(hi):
    nn_ = lambda a, b: _bdg(a, b, 2, 1, hi)
    nt_ = lambda a, b: _bdg(a, b, 2, 2, hi)
    tn_ = lambda a, b: _bdg(a, b, 1, 1, hi)
    nn = jax.custom_vjp(nn_)
    nn.defvjp(lambda a, b: (nn_(a, b), (a, b)), lambda r, g: (nt_(g, r[1]), tn_(r[0], g)))
    nt = jax.custom_vjp(nt_)
    nt.defvjp(lambda a, b: (nt_(a, b), (a, b)), lambda r, g: (nn_(g, r[1]), tn_(g, r[0])))
    tn = jax.custom_vjp(tn_)
    tn.defvjp(lambda a, b: (tn_(a, b), (a, b)), lambda r, g: (nt_(r[1], g), nn_(r[0], g)))
    return nn, nt, tn


bnn, bnt, btn = _batched_matmuls(False)
hnn, hnt, htn = _batched_matmuls(True)


def _unit_lower_inverse(a):
    r = _iota2((1, CH, CH), 1)
    c = _iota2((1, CH, CH), 2)
    eye = (r == c).astype(F32)
    d = jnp.where(r // 8 == c // 8, a, 0.0)
    inv = eye - d
    p = d
    for _ in range(2):
        p = hnn(p, p)
        inv = inv + hnn(inv, p)
    for blk in (16, 32, 64):
        off = jnp.where((r // blk == c // blk) & (r // (blk // 2) != c // (blk // 2)), a, 0.0)
        mm = hnn if blk == 16 else bnn
        inv = inv - mm(mm(inv, off), inv)
    return inv


@jax.custom_vjp
def _inverse_given(a, t):
    return t


_inverse_given.defvjp(lambda a, t: (t, t), lambda t, g: (-hnt(htn(t, g), t), jnp.zeros_like(t)))


def dn_prep(xc, w):
    t = xc.shape[0] - 8
    c = xc[5:5 + t] * w[0:1] + xc[6:6 + t] * w[1:2] + xc[7:7 + t] * w[2:3] + xc[8:8 + t] * w[3:4]
    act = _silu(c)
    q, k, v = act[:, :DNW], act[:, DNW:2 * DNW], act[:, 2 * DNW:]
    ones = (_iota2((DNW, DNW), 0) // DH == _iota2((DNW, DNW), 1) // DH).astype(F32)
    q = q * lax.rsqrt(bdot(q * q, ones) + EPS) * (DH ** -0.5)
    k = k * lax.rsqrt(bdot(k * k, ones) + EPS)
    return jnp.concatenate([q, k, v], axis=1)


def dn_chunk(q, k, v, b, g, gt, s_prev, t_saved=None):
    r = _iota2((1, CH, CH), 1)
    c = _iota2((1, CH, CH), 2)
    causal = r >= c
    dec = jnp.where(causal, jnp.exp(jnp.where(causal, g - gt, 0.0)), 0.0)
    kb = k * b
    qk = bnt(jnp.concatenate([q, kb], axis=1), k)
    attn = qk[:, :CH] * dec
    a = jnp.where(r > c, qk[:, CH:] * dec, 0.0)
    tinv = _unit_lower_inverse(a) if t_saved is None else _inverse_given(a, t_saved)
    eg = jnp.exp(g)
    uw = hnn(tinv, jnp.concatenate([v * b, kb * eg], axis=2))
    g_last = g[:, CH - 1:CH]
    ws = bnn(jnp.concatenate([uw[..., DH:], q * eg], axis=1), s_prev)
    v_new = uw[..., :DH] - ws[:, :CH]
    o = ws[:, CH:] + bnn(attn, v_new)
    s_new = s_prev * jnp.exp(g_last) + btn(k * jnp.exp(g_last - g), v_new)
    return o, s_new, tinv


def s5_chunk(u, xp_re, xp_im, bb_re, bb_im, cc_re, cc_im, p0r, p0i, p1r, p1i, pir, pii, dsk):
    nb, ch, _ = u.shape
    u2 = u.reshape(nb * ch, LANES)
    bu_re = bdot(u2, bb_re).reshape(nb, ch, 512)
    bu_im = bdot(u2, bb_im).reshape(nb, ch, 512)
    xt_re = pir * bu_re - pii * bu_im
    xt_im = pir * bu_im + pii * bu_re
    tri = jnp.broadcast_to((_iota2((1, ch, ch), 1) >= _iota2((1, ch, ch), 2)).astype(F32), (nb, ch, ch))
    cs_re = hnn(tri, xt_re)
    cs_im = hnn(tri, xt_im)
    x_re = p0r * cs_re - p0i * cs_im + p1r * xp_re - p1i * xp_im
    x_im = p0r * cs_im + p0i * cs_re + p1r * xp_im + p1i * xp_re
    y = bdot_nt(x_re.reshape(nb * ch, 512), cc_re) - bdot_nt(x_im.reshape(nb * ch, 512), cc_im) + dsk * u2
    return y.reshape(nb, ch, LANES), x_re[:, ch - 1:ch], x_im[:, ch - 1:ch]


def s5_tables(lam_re, lam_im, log_step, bre, bim, cre, cim):
    expand = (_iota2((S5G, S5N), 1) // S5P == _iota2((S5G, S5N), 0)).astype(F32)
    step = _dot_hi(jnp.exp(log_step), expand)
    lre = jnp.minimum(lam_re, -1e-4)
    lr = lre * step
    ang = lam_im * step
    mag = jnp.exp(lr)
    lb_re = mag * jnp.cos(ang)
    lb_im = mag * jnp.sin(ang)
    den = lre * lre + lam_im * lam_im
    coef_re = ((lb_re - 1.0) * lre + lb_im * lam_im) / den
    coef_im = (lb_im * lre - (lb_re - 1.0) * lam_im) / den
    bb_re = coef_re * bre - coef_im * bim
    bb_im = coef_re * bim + coef_im * bre
    j = _iota2((S5_CH, 1), 0).astype(F32)
    jc = j - S5_CH // 2
    e0 = jnp.exp(jc * lr)
    e1 = jnp.exp((j + 1.0) * lr)
    ei = jnp.exp(-jc * lr)
    mask = (_iota2((LANES, 512), 0) // S5C == _iota2((LANES, 512), 1) // S5P).astype(F32)

    def blocks(t):
        return jnp.concatenate([(jnp.tile(t[:, gb * 512:(gb + 1) * 512], (LANES // S5C, 1)) * mask)[None]
                                for gb in range(GB)], axis=0)

    return (blocks(bb_re), blocks(bb_im), blocks(cre), blocks(cim),
            e0 * jnp.cos(jc * ang), e0 * jnp.sin(jc * ang),
            e1 * jnp.cos((j + 1.0) * ang), e1 * jnp.sin((j + 1.0) * ang),
            ei * jnp.cos(jc * ang), -ei * jnp.sin(jc * ang))


def _row_specs(tiled, batch, bcast, tm, tpb):
    specs = [pl.BlockSpec((tm, a.shape[1]), lambda i: (i, 0)) for a in tiled]
    specs += [pl.BlockSpec((None,) + a.shape[1:], lambda i: (i // tpb, 0, 0)) for a in batch]
    specs += [pl.BlockSpec(a.shape, lambda i, nd=a.ndim: (0,) * nd) for a in bcast]
    return specs


def ew_call(name, fn, tiled, batch, bcast, outs, tm, seq):
    t_rows = tiled[0].shape[0]
    n_in = len(tiled) + len(batch) + len(bcast)

    def body(*refs):
        vals = [r[...].astype(F32) for r in refs[:n_in]]
        for r, o in zip(refs[n_in:], fn(*vals)):
            r[...] = o.astype(r.dtype)

    return pl.pallas_call(
        body, grid=(t_rows // tm,), in_specs=_row_specs(tiled, batch, bcast, tm, seq // tm),
        out_specs=[pl.BlockSpec((tm, w), lambda i: (i, 0)) for w, _ in outs],
        out_shape=[SDS((t_rows, w), dt) for w, dt in outs], name=name, compiler_params=_cp(1))(*tiled, *batch, *bcast)


def ew_vjp_call(name, fn, tiled, batch, bcast, cts, want, tm, seq, addend=None):
    t_rows = tiled[0].shape[0]
    tpb = seq // tm
    n_t, n_b, n_c = len(tiled), len(batch), len(bcast)
    n_in = n_t + n_b + n_c
    extra = [] if addend is None else [addend]

    def body(*refs):
        i = pl.program_id(0)
        vals = [r[...].astype(F32) for r in refs[:n_in]]
        ctv = tuple(r[...].astype(F32) for r in refs[n_in:n_in + len(cts)])
        outs = refs[n_in + len(cts) + len(extra):]
        _, vjp = jax.vjp(fn, *vals)
        grads = vjp(ctv)
        for k, (r, (idx, _)) in enumerate(zip(outs[:len(want)], want)):
            g = grads[idx]
            if k == 0 and extra:
                g = g + refs[n_in + len(cts)][...]
            r[...] = g.astype(r.dtype)
        for k in range(n_b):
            r, g = outs[len(want) + k], grads[n_t + k]

            @pl.when(i % tpb == 0)
            def _(r=r, g=g):
                r[...] = g

            @pl.when(i % tpb != 0)
            def _(r=r, g=g):
                r[...] += g
        for k in range(n_c):
            r, g = outs[len(want) + n_b + k], grads[n_t + n_b + k]

            @pl.when(i == 0)
            def _(r=r, g=g):
                r[...] = g

            @pl.when(i != 0)
            def _(r=r, g=g):
                r[...] += g

    out_specs = [pl.BlockSpec((tm, tiled[idx].shape[1]), lambda i: (i, 0)) for idx, _ in want]
    out_specs += [pl.BlockSpec((None,) + a.shape[1:], lambda i: (i // tpb, 0, 0)) for a in batch]
    out_specs += [pl.BlockSpec(a.shape, lambda i, nd=a.ndim: (0,) * nd) for a in bcast]
    out_shape = [SDS(tiled[idx].shape, dt) for idx, dt in want]
    out_shape += [SDS(a.shape, F32) for a in batch] + [SDS(a.shape, F32) for a in bcast]
    res = pl.pallas_call(
        body, grid=(t_rows // tm,),
        in_specs=_row_specs(tiled, batch, bcast, tm, tpb)
        + [pl.BlockSpec((tm, a.shape[1]), lambda i: (i, 0)) for a in list(cts) + extra],
        out_specs=out_specs, out_shape=out_shape, name=name, compiler_params=_cp(1))(*tiled, *batch, *bcast, *cts, *extra)
    return res[:len(want)], res[len(want):len(want) + n_b], res[len(want) + n_b:]


def _pick(n, cands):
    for c in cands:
        if n % c == 0:
            return c
    return n


def mm_tn(name, a, b, exchange=None, gather=None):
    t_rows, m = a.shape
    n = b.shape[1]
    tn = n if n <= 1024 else _pick(n, (1024, 512, 256, 128))
    tm = max([t for t in range(LANES, m + 1, LANES) if m % t == 0 and t * tn * 4 <= ACC_LIMIT] or [m])
    tk = _pick(t_rows, (1024, 512, 256, 128, 64))
    grid = (m // tm, n // tn, t_rows // tk)
    extra = [x for x in (exchange, gather) if x is not None]
    ne = len(extra)

    def body(*refs):
        a_ref, b_ref = refs[:2]
        o_ref, acc = refs[2 + ne], refs[3 + 2 * ne]
        i, j, k = pl.program_id(0), pl.program_id(1), pl.program_id(2)
        first = (i == 0) & (j == 0) & (k == 0)
        middle = (i == grid[0] - 1) & (j == grid[1] - 1) & (k == grid[2] // 2)
        last = (i == grid[0] - 1) & (j == grid[1] - 1) & (k == grid[2] - 1)
        at_end = []
        for e, x in enumerate(extra):
            comm_refs = (refs[2 + e], refs[3 + ne + e]) + tuple(refs[4 + 2 * ne + 3 * e:7 + 2 * ne + 3 * e])
            if x is exchange:
                start, finish = _exchange_phases(*comm_refs)
                pl.when(first)(start)
            else:
                start, forward, finish = _gather_phases(*comm_refs)
                pl.when(first)(start)
                pl.when(middle)(forward)
            at_end.append(finish)

        @pl.when(k == 0)
        def _():
            acc[...] = jnp.zeros_like(acc)

        acc[...] += _dot_tn(a_ref[...], b_ref[...])

        @pl.when(k == grid[2] - 1)
        def _():
            o_ref[...] = acc[...].astype(BF16)

        for phase in at_end:
            pl.when(last)(phase)

    res = pl.pallas_call(
        body, grid=grid,
        in_specs=[pl.BlockSpec((tk, tm), lambda i, j, k: (k, i)), pl.BlockSpec((tk, tn), lambda i, j, k: (k, j))]
        + [HBM_SPEC] * ne,
        out_specs=[pl.BlockSpec((tm, tn), lambda i, j, k: (i, j))] + [HBM_SPEC] * ne,
        out_shape=[SDS((m, n), BF16)] + [SDS(x.shape if x is exchange else (NDEV,) + x.shape, x.dtype) for x in extra],
        scratch_shapes=[pltpu.VMEM((tm, tn), F32)] + _comm_scratch() * ne, name=name,
        compiler_params=_cp(3))(a, b, *extra)
    return res if extra else res[0]


def _ffn_weight_spec(w, k):
    assert FFN_TF == FF
    if w.ndim == 3:
        return pl.BlockSpec((NDEV, FF // NDEV, D), lambda i, j: (0, k, 0), pipeline_mode=pl.Buffered(1))
    return pl.BlockSpec((FF, D), lambda i, j: (0, 0), pipeline_mode=pl.Buffered(1))


def _ffn_weight(ref):
    return ref[...].reshape(FF, D)


def ffn_fwd(name, h, mod3, g, w1, w3, w2, seq, gather=None, loss_head=None):
    t_rows = h.shape[0]
    tm = _pick(seq, (FFN_FWD_TM, 128, 64))
    tf = FFN_TF
    tpb = seq // tm
    nf = FF // tf
    nt = t_rows // tm
    extra = [] if gather is None else [gather]
    head = [] if loss_head is None else list(loss_head)
    nh, ne = len(head), len(extra)

    def body(*refs):
        h_ref, mod_ref, g_ref, w1_ref, w3_ref, w2_ref = refs[:6]
        o0 = 6 + nh + ne
        ho_ref, f_ref, u_ref, h1_ref, h3_ref = refs[o0:o0 + 5]
        s0 = o0 + 5 + nh + ne
        acc = refs[s0]
        i, j = pl.program_id(0), pl.program_id(1)
        if extra:
            start, forward, finish = _gather_phases(refs[6 + nh], refs[o0 + 5 + nh], *refs[s0 + 1:s0 + 4])
            pl.when((i == 0) & (j == 0))(start)
            pl.when((i == nt // 2) & (j == 0))(forward)

        @pl.when(j == 0)
        def _():
            u_ref[...] = normmod(h_ref[...], g_ref[...], mod_ref[1:2, :], mod_ref[0:1, :]).astype(BF16)
            acc[...] = jnp.zeros_like(acc)

        u = u_ref[...]
        h1 = _dot_nt(u, _ffn_weight(w1_ref))
        h3 = _dot_nt(u, _ffn_weight(w3_ref))
        h1_ref[...] = h1.astype(BF16)
        h3_ref[...] = h3.astype(BF16)
        acc[...] += _dot(_silu(h1) * h3, _ffn_weight(w2_ref))

        @pl.when(j == nf - 1)
        def _():
            f_ref[...] = acc[...]
            h_out = h_ref[...] + 0.5 * mod_ref[2:3, :] * acc[...]
            if not head:
                ho_ref[...] = h_out
            else:
                t_ref, gf_ref, dg_ref, loss_ref = refs[6], refs[7], refs[o0 + 5], refs[o0 + 6]
                y, vjp = jax.vjp(lambda hh, gg: hh * lax.rsqrt(jnp.mean(hh * hh, axis=-1, keepdims=True) + EPS) * gg,
                                 h_out, gf_ref[...])
                e = y - t_ref[...]
                dh, dg = vjp(e * (1.0 / D))
                part = jnp.sum(jnp.sum(e * e, axis=1, keepdims=True), axis=0, keepdims=True) * (0.5 / D) \
                    + jnp.zeros((1, LANES), F32)
                ho_ref[...] = dh

                @pl.when(i == 0)
                def _():
                    dg_ref[...] = dg
                    loss_ref[...] = part

                @pl.when(i != 0)
                def _():
                    dg_ref[...] += dg
                    loss_ref[...] += part

        if extra:
            pl.when((i == nt - 1) & (j == nf - 1))(finish)

    row = lambda i, j: (i, 0)
    const = lambda i, j: (0, 0)
    head_in = [pl.BlockSpec((tm, D), row), pl.BlockSpec((1, D), const)] if head else []
    head_out = [pl.BlockSpec((1, D), const), pl.BlockSpec((1, LANES), const)] if head else []
    return pl.pallas_call(
        body, grid=(nt, nf),
        in_specs=[pl.BlockSpec((tm, D), row), pl.BlockSpec((None, 3, D), lambda i, j: (i // tpb, 0, 0)),
                  pl.BlockSpec((1, D), const)] + [_ffn_weight_spec(w, k) for k, w in enumerate((w1, w3, w2))]
        + head_in + [HBM_SPEC] * ne,
        out_specs=[pl.BlockSpec((tm, D), row), pl.BlockSpec((tm, D), row), pl.BlockSpec((tm, D), row),
                   pl.BlockSpec((tm, tf), lambda i, j: (i, j)), pl.BlockSpec((tm, tf), lambda i, j: (i, j))]
        + head_out + [HBM_SPEC] * ne,
        out_shape=[SDS((t_rows, D), F32), SDS((t_rows, D), F32), SDS((t_rows, D), BF16), SDS((t_rows, FF), BF16),
                   SDS((t_rows, FF), BF16)] + ([SDS((1, D), F32), SDS((1, LANES), F32)] if head else [])
        + [SDS((NDEV,) + x.shape, x.dtype) for x in extra],
        scratch_shapes=[pltpu.VMEM((tm, D), F32)] + (_comm_scratch() if extra else []), name=name,
        compiler_params=_cp(2))(h, mod3, g, w1, w3, w2, *head, *extra)


def ffn_bwd(name, dho, h, f_out, h1_in, h3_in, mod3, g, w1, w3, w2, seq, exchange=None):
    t_rows = h.shape[0]
    tm = _pick(seq, (FFN_BWD_TM, 128, 64))
    tf = FFN_TF
    tpb = seq // tm
    nf = FF // tf
    nt = t_rows // tm
    extra = [] if exchange is None else [exchange]

    def body(*refs):
        dho_ref, h_ref, f_ref, h1_ref, h3_ref, mod_ref, g_ref, w1_ref, w3_ref, w2_ref = refs[:10]
        dh_ref, a_ref, dh1_ref, dh3_ref, df_scr, dmod_ref, dg_ref = refs[10 + len(extra):17 + len(extra)]
        du_acc = refs[17 + 2 * len(extra)]
        i, j = pl.program_id(0), pl.program_id(1)
        if extra:
            start, finish = _exchange_phases(refs[10], refs[18], *refs[20:23])
            pl.when((i == 0) & (j == 0))(start)

        @pl.when(j == 0)
        def _():
            df_scr[...] = (0.5 * mod_ref[2:3, :] * dho_ref[...]).astype(BF16)
            du_acc[...] = jnp.zeros_like(du_acc)

        h1 = h1_ref[...].astype(F32)
        h3 = h3_ref[...].astype(F32)
        sg = jax.nn.sigmoid(h1)
        s = h1 * sg
        da = _dot_nt(df_scr[...], _ffn_weight(w2_ref))
        dh3 = (da * s).astype(BF16)
        dh1 = (da * h3 * (sg * (1.0 + h1 * (1.0 - sg)))).astype(BF16)
        a_ref[...] = (s * h3).astype(BF16)
        dh1_ref[...] = dh1
        dh3_ref[...] = dh3
        du_acc[...] += _dot(dh1, _ffn_weight(w1_ref)) + _dot(dh3, _ffn_weight(w3_ref))

        @pl.when(j == nf - 1)
        def _():
            _, vjp = jax.vjp(normmod, h_ref[...], g_ref[...], mod_ref[1:2, :], mod_ref[0:1, :])
            dh_n, dg, dsc, dsh = vjp(du_acc[...])
            dh_ref[...] = dho_ref[...] + dh_n
            dgt = jnp.sum(0.5 * dho_ref[...] * f_ref[...], axis=0, keepdims=True)
            dmod = jnp.concatenate([dsh, dsc, dgt], axis=0)

            @pl.when(i % tpb == 0)
            def _():
                dmod_ref[...] = dmod

            @pl.when(i % tpb != 0)
            def _():
                dmod_ref[...] += dmod

            @pl.when(i == 0)
            def _():
                dg_ref[...] = dg

            @pl.when(i != 0)
            def _():
                dg_ref[...] += dg

        if extra:
            pl.when((i == nt - 1) & (j == nf - 1))(finish)

    row = lambda i, j: (i, 0)
    col = lambda i, j: (i, j)
    return pl.pallas_call(
        body, grid=(nt, nf),
        in_specs=[pl.BlockSpec((tm, D), row), pl.BlockSpec((tm, D), row), pl.BlockSpec((tm, D), row),
                  pl.BlockSpec((tm, tf), col), pl.BlockSpec((tm, tf), col),
                  pl.BlockSpec((None, 3, D), lambda i, j: (i // tpb, 0, 0)),
                  pl.BlockSpec((1, D), lambda i, j: (0, 0))] + [_ffn_weight_spec(w, k) for k, w in enumerate((w1, w3, w2))]
        + [HBM_SPEC] * len(extra),
        out_specs=[pl.BlockSpec((tm, D), row), pl.BlockSpec((tm, tf), col), pl.BlockSpec((tm, tf), col),
                   pl.BlockSpec((tm, tf), col), pl.BlockSpec((tm, D), row),
                   pl.BlockSpec((None, 3, D), lambda i, j: (i // tpb, 0, 0)), pl.BlockSpec((1, D), lambda i, j: (0, 0))]
        + [HBM_SPEC] * len(extra),
        out_shape=[SDS((t_rows, D), F32), SDS((t_rows, FF), BF16), SDS((t_rows, FF), BF16), SDS((t_rows, FF), BF16),
                   SDS((t_rows, D), BF16), SDS(mod3.shape, F32), SDS((1, D), F32)] + [SDS(x.shape, x.dtype) for x in extra],
        scratch_shapes=[pltpu.VMEM((tm, D), F32)] + (_comm_scratch() if extra else []), name=name,
        compiler_params=_cp(2))(dho, h, f_out, h1_in, h3_in, mod3, g, w1, w3, w2, *extra)


def _resident(shape):
    return pl.BlockSpec(shape, lambda i: (0,) * len(shape), pipeline_mode=pl.Buffered(1))


def mix_in_fwd(h, sh, sc, g, ws, seq):
    t_rows = h.shape[0]
    tm = _pick(seq, (256, 128, 64))
    tpb = seq // tm
    nw = len(ws)

    def body(h_ref, sh_ref, sc_ref, g_ref, *rest):
        u = normmod(h_ref[...], g_ref[...], sc_ref[...], sh_ref[...]).astype(BF16)
        rest[nw][...] = u
        for w_ref, p_ref in zip(rest[:nw], rest[nw + 1:]):
            p_ref[...] = _dot_nt(u, w_ref[...])

    row = lambda i: (i, 0)
    batch = pl.BlockSpec((None, 1, D), lambda i: (i // tpb, 0, 0))
    return pl.pallas_call(
        body, grid=(t_rows // tm,),
        in_specs=[pl.BlockSpec((tm, D), row), batch, batch, pl.BlockSpec((1, D), lambda i: (0, 0))]
        + [_resident(w.shape) for w in ws],
        out_specs=[pl.BlockSpec((tm, D), row)] + [pl.BlockSpec((tm, w.shape[0]), row) for w in ws],
        out_shape=[SDS((t_rows, D), BF16)] + [SDS((t_rows, w.shape[0]), F32) for w in ws], name="mix_in_fwd",
        compiler_params=_cp(1))(h, sh, sc, g, *ws)


def mix_in_bwd(dps, ws, h, sh, sc, g, dh_add, seq):
    t_rows = h.shape[0]
    tm = _pick(seq, (256, 128, 64))
    tpb = seq // tm
    nw = len(ws)

    def body(*refs):
        h_ref, sh_ref, sc_ref, g_ref, add_ref, dh_ref, dsh_ref, dsc_ref, dg_ref = refs[2 * nw:]
        i = pl.program_id(0)
        du = _dot(refs[0][...], refs[nw][...])
        for k in range(1, nw):
            du = du + _dot(refs[k][...], refs[nw + k][...])
        _, vjp = jax.vjp(normmod, h_ref[...], g_ref[...], sc_ref[...], sh_ref[...])
        dh_n, dg, dsc, dsh = vjp(du)
        dh_ref[...] = add_ref[...] + dh_n

        @pl.when(i % tpb == 0)
        def _():
            dsh_ref[...] = dsh
            dsc_ref[...] = dsc

        @pl.when(i % tpb != 0)
        def _():
            dsh_ref[...] += dsh
            dsc_ref[...] += dsc

        @pl.when(i == 0)
        def _():
            dg_ref[...] = dg

        @pl.when(i != 0)
        def _():
            dg_ref[...] += dg

    row = lambda i: (i, 0)
    batch = pl.BlockSpec((None, 1, D), lambda i: (i // tpb, 0, 0))
    gain = pl.BlockSpec((1, D), lambda i: (0, 0))
    return pl.pallas_call(
        body, grid=(t_rows // tm,),
        in_specs=[pl.BlockSpec((tm, dp.shape[1]), row) for dp in dps] + [_resident(w.shape) for w in ws]
        + [pl.BlockSpec((tm, D), row), batch, batch, gain, pl.BlockSpec((tm, D), row)],
        out_specs=[pl.BlockSpec((tm, D), row), batch, batch, gain],
        out_shape=[SDS((t_rows, D), F32), SDS(sh.shape, F32), SDS(sc.shape, F32), SDS((1, D), F32)], name="mix_in_bwd",
        compiler_params=_cp(1))(*dps, *ws, h, sh, sc, g, dh_add)


def mix_out_fwd(merged, w_out, h_prev, gt, seq):
    t_rows = merged.shape[0]
    tm = _pick(seq, (256, 128, 64))
    tpb = seq // tm

    def body(m_ref, w_ref, h_ref, gt_ref, mo_ref, ho_ref):
        mo = _dot(m_ref[...], w_ref[...])
        mo_ref[...] = mo
        ho_ref[...] = h_ref[...] + gt_ref[...] * mo

    row = lambda i: (i, 0)
    return pl.pallas_call(
        body, grid=(t_rows // tm,),
        in_specs=[pl.BlockSpec((tm, D), row), _resident(w_out.shape), pl.BlockSpec((tm, D), row),
                  pl.BlockSpec((None, 1, D), lambda i: (i // tpb, 0, 0))],
        out_specs=[pl.BlockSpec((tm, D), row), pl.BlockSpec((tm, D), row)],
        out_shape=[SDS((t_rows, D), F32), SDS((t_rows, D), F32)], name="mix_out_fwd",
        compiler_params=_cp(1))(merged, w_out, h_prev, gt)


def mix_out_bwd(dh, mo, w_out, gt, seq):
    t_rows = dh.shape[0]
    tm = _pick(seq, (256, 128, 64))
    tpb = seq // tm

    def body(dh_ref, mo_ref, w_ref, gt_ref, dmo_ref, dm_ref, dgt_ref):
        i = pl.program_id(0)
        dmo = (gt_ref[...] * dh_ref[...]).astype(BF16)
        dmo_ref[...] = dmo
        dm_ref[...] = _dot_nt(dmo, w_ref[...])
        dgt = jnp.sum(dh_ref[...] * mo_ref[...], axis=0, keepdims=True)

        @pl.when(i % tpb == 0)
        def _():
            dgt_ref[...] = dgt

        @pl.when(i % tpb != 0)
        def _():
            dgt_ref[...] += dgt

    row = lambda i: (i, 0)
    batch = pl.BlockSpec((None, 1, D), lambda i: (i // tpb, 0, 0))
    return pl.pallas_call(
        body, grid=(t_rows // tm,),
        in_specs=[pl.BlockSpec((tm, D), row), pl.BlockSpec((tm, D), row), _resident(w_out.shape), batch],
        out_specs=[pl.BlockSpec((tm, D), row), pl.BlockSpec((tm, D), row), batch],
        out_shape=[SDS((t_rows, D), BF16), SDS((t_rows, D), F32), SDS(gt.shape, F32)], name="mix_out_bwd",
        compiler_params=_cp(1))(dh, mo, w_out, gt)


def _dn_cols(part, hd):
    return slice(part * DNW + hd * DH, part * DNW + (hd + 1) * DH)


def _qkv_stacks(qkv_ref, nb):
    pairs = [(b, hd) for b in range(nb) for hd in range(NH)]
    return [jnp.stack([qkv_ref[b, :, _dn_cols(part, hd)] for b, hd in pairs]) for part in range(3)]


def dn_prep_fwd(p_dn, conv8):
    bl, seq, _ = p_dn.shape
    tp = _pick(seq, (256, 128, 64))

    def body(raw_ref, halo_ref, conv_ref, o_ref):
        hm = (pl.program_id(1) > 0).astype(F32)
        o_ref[...] = dn_prep(jnp.concatenate([halo_ref[...] * hm, raw_ref[...]], axis=0), conv_ref[...])

    return pl.pallas_call(
        body, grid=(bl, seq // tp),
        in_specs=[pl.BlockSpec((None, tp, 3 * DNW), lambda b, i: (b, i, 0)),
                  pl.BlockSpec((None, 8, 3 * DNW), lambda b, i: (b, jnp.maximum(i * (tp // 8) - 1, 0), 0)),
                  pl.BlockSpec((8, 3 * DNW), lambda b, i: (0, 0))],
        out_specs=pl.BlockSpec((None, tp, 3 * DNW), lambda b, i: (b, i, 0)),
        out_shape=SDS((bl, seq, 3 * DNW), F32), name="dn_prep_fwd", compiler_params=_cp(2))(p_dn, p_dn, conv8)


def dn_prep_bwd(p_dn, conv8, d_qkv, d_z):
    bl, seq, _ = p_dn.shape
    tp = _pick(seq, (256, 128, 64))
    nt = seq // tp

    def body(raw_ref, halo_ref, conv_ref, dq_ref, dz_ref, draw_ref, dconv_ref, carry):
        b, r = pl.program_id(0), pl.program_id(1)

        @pl.when((b == 0) & (r == 0))
        def _():
            dconv_ref[...] = jnp.zeros_like(dconv_ref)

        @pl.when(r == 0)
        def _():
            carry[...] = jnp.zeros_like(carry)

        hm = (r < nt - 1).astype(F32)
        _, vjp = jax.vjp(dn_prep, jnp.concatenate([halo_ref[...] * hm, raw_ref[...]], axis=0), conv_ref[...])
        dxc, dw = vjp(dq_ref[...])
        tail = dxc[tp:tp + 8] + carry[...]
        draw_ref[:, 0:3 * DNW] = jnp.concatenate([dxc[8:tp], tail], axis=0).astype(BF16)
        draw_ref[:, 3 * DNW:4 * DNW] = dz_ref[...].astype(BF16)
        carry[...] = dxc[0:8] * hm
        dconv_ref[...] += dw

    blk = lambda b, r: (b, nt - 1 - r, 0)
    return pl.pallas_call(
        body, grid=(bl, nt),
        in_specs=[pl.BlockSpec((None, tp, 3 * DNW), blk),
                  pl.BlockSpec((None, 8, 3 * DNW), lambda b, r: (b, jnp.maximum((nt - 1 - r) * (tp // 8) - 1, 0), 0)),
                  pl.BlockSpec((8, 3 * DNW), lambda b, r: (0, 0)), pl.BlockSpec((None, tp, 3 * DNW), blk),
                  pl.BlockSpec((None, tp, DNW), blk)],
        out_specs=[pl.BlockSpec((None, tp, 4 * DNW), blk), pl.BlockSpec((8, 3 * DNW), lambda b, r: (0, 0))],
        out_shape=[SDS((bl, seq, 4 * DNW), BF16), SDS((8, 3 * DNW), F32)],
        scratch_shapes=[pltpu.VMEM((8, 3 * DNW), F32)], name="dn_prep_bwd", compiler_params=_cp(2))(p_dn, p_dn, conv8, d_qkv, d_z)


def _gate_stacks(gates, nb):
    pairs = [(b, hd) for b in range(nb) for hd in range(NH)]
    bs = jnp.stack([gates[b][0][:, hd:hd + 1] for b, hd in pairs])
    gs = jnp.stack([gates[b][1][:, NH + hd:NH + hd + 1] for b, hd in pairs])
    gts = jnp.stack([gates[b][2][NH + hd:NH + hd + 1, :] for b, hd in pairs])
    return bs, gs, gts


def deltanet_fwd(qkv, p_small, alp, dtp, nb, gather=None):
    bl, seq, _ = qkv.shape
    nc = seq // CH
    ng = nb * NH
    extra = [] if gather is None else [gather]

    def body(*refs):
        qkv_ref, small_ref, alp_ref, dtp_ref = refs[:4]
        o_ref, sprev_ref, tinv_ref = refs[4 + len(extra):7 + len(extra)]
        s_scr = refs[7 + 2 * len(extra)]
        bb, n = pl.program_id(0), pl.program_id(1)
        if extra:
            start, forward, finish = _gather_phases(refs[4], refs[8], *refs[10:13])
            pl.when((bb == 0) & (n == 0))(start)

        @pl.when(n == 0)
        def _():
            s_scr[...] = jnp.zeros_like(s_scr)

        gates = [gate_fn(small_ref[b], alp_ref[...], dtp_ref[...]) for b in range(nb)]
        s_prev = s_scr[...]
        o, s_new, tinv = dn_chunk(*_qkv_stacks(qkv_ref, nb), *_gate_stacks(gates, nb), s_prev)
        sprev_ref[...] = s_prev
        tinv_ref[...] = tinv
        s_scr[...] = s_new
        for b in range(nb):
            for hd in range(NH):
                o_ref[b, :, hd * DH:(hd + 1) * DH] = o[b * NH + hd]
        if extra:
            pl.when((bb == bl // nb - 1) & (n == nc // 2))(forward)
            pl.when((bb == bl // nb - 1) & (n == nc - 1))(finish)

    blk = lambda bb, n: (bb, n, 0)
    const = lambda bb, n: (0, 0)
    saved = pl.BlockSpec((None, ng, DH, DH), lambda bb, n: (bb * nc + n, 0, 0, 0))
    return pl.pallas_call(
        body, grid=(bl // nb, nc),
        in_specs=[pl.BlockSpec((nb, CH, 3 * DNW), blk), pl.BlockSpec((nb, CH, LANES), blk),
                  pl.BlockSpec((1, LANES), const), pl.BlockSpec((1, LANES), const)] + [HBM_SPEC] * len(extra),
        out_specs=[pl.BlockSpec((nb, CH, DNW), blk), saved, saved] + [HBM_SPEC] * len(extra),
        out_shape=[SDS((bl, seq, DNW), F32), SDS((bl // nb * nc, ng, DH, DH), F32), SDS((bl // nb * nc, ng, DH, DH), F32)]
        + [SDS((NDEV,) + x.shape, x.dtype) for x in extra],
        scratch_shapes=[pltpu.VMEM((ng, DH, DH), F32)] + (_comm_scratch() if extra else []), name="deltanet_fwd",
        compiler_params=_cp(2))(qkv, p_small, alp, dtp, *extra)


def deltanet_bwd(qkv, p_small, alp, dtp, sprev, tinv, d_o, nb, exchange=None):
    bl, seq, _ = qkv.shape
    nc = seq // CH
    ng = nb * NH
    extra = [] if exchange is None else [exchange]

    def body(*refs):
        qkv_ref, small_ref, alp_ref, dtp_ref, sprev_ref, tinv_ref, do_ref = refs[:7]
        dqkv_ref, dsmall_ref, dalp_ref, ddtp_ref = refs[7 + len(extra):11 + len(extra)]
        ds_scr = refs[11 + 2 * len(extra)]
        bb, r = pl.program_id(0), pl.program_id(1)
        if extra:
            start, finish = _exchange_phases(refs[7], refs[12], *refs[14:17])
            pl.when((bb == 0) & (r == 0))(start)

        @pl.when((bb == 0) & (r == 0))
        def _():
            dalp_ref[...] = jnp.zeros_like(dalp_ref)
            ddtp_ref[...] = jnp.zeros_like(ddtp_ref)

        @pl.when(r == 0)
        def _():
            ds_scr[...] = jnp.zeros_like(ds_scr)

        gates, gate_vjps = [], []
        for b in range(nb):
            out, gvjp = jax.vjp(gate_fn, small_ref[b], alp_ref[...], dtp_ref[...])
            gates.append(out)
            gate_vjps.append(gvjp)
        t_saved = tinv_ref[...]
        _, vjp = jax.vjp(lambda *args: dn_chunk(*args, t_saved)[:2], *_qkv_stacks(qkv_ref, nb), *_gate_stacks(gates, nb),
                         sprev_ref[...])
        d_out = jnp.stack([do_ref[b, :, hd * DH:(hd + 1) * DH] for b in range(nb) for hd in range(NH)])
        grads = vjp((d_out, ds_scr[...]))
        ds_scr[...] = grads[6]
        lane = _iota2((CH, LANES), 1)
        rowi = _iota2((LANES, CH), 0)
        for b in range(nb):
            d_beta = jnp.zeros((CH, LANES), F32)
            d_gc = jnp.zeros((CH, LANES), F32)
            d_gct = jnp.zeros((LANES, CH), F32)
            for hd in range(NH):
                i = b * NH + hd
                for part in range(3):
                    dqkv_ref[b, :, _dn_cols(part, hd)] = grads[part][i]
                d_beta = d_beta + jnp.where(lane == hd, grads[3][i], 0.0)
                d_gc = d_gc + jnp.where(lane == NH + hd, grads[4][i], 0.0)
                d_gct = d_gct + jnp.where(rowi == NH + hd, grads[5][i], 0.0)
            d_small, d_alp, d_dtp = gate_vjps[b]((d_beta, d_gc, d_gct))
            dsmall_ref[b] = d_small.astype(BF16)
            dalp_ref[...] += d_alp
            ddtp_ref[...] += d_dtp
        if extra:
            pl.when((bb == bl // nb - 1) & (r == nc - 1))(finish)

    blk = lambda bb, r: (bb, nc - 1 - r, 0)
    const = lambda bb, r: (0, 0)
    saved = pl.BlockSpec((None, ng, DH, DH), lambda bb, r: (bb * nc + nc - 1 - r, 0, 0, 0))
    return pl.pallas_call(
        body, grid=(bl // nb, nc),
        in_specs=[pl.BlockSpec((nb, CH, 3 * DNW), blk), pl.BlockSpec((nb, CH, LANES), blk), pl.BlockSpec((1, LANES), const),
                  pl.BlockSpec((1, LANES), const), saved, saved, pl.BlockSpec((nb, CH, DNW), blk)] + [HBM_SPEC] * len(extra),
        out_specs=[pl.BlockSpec((nb, CH, 3 * DNW), blk), pl.BlockSpec((nb, CH, LANES), blk), pl.BlockSpec((1, LANES), const),
                   pl.BlockSpec((1, LANES), const)] + [HBM_SPEC] * len(extra),
        out_shape=[SDS((bl, seq, 3 * DNW), F32), SDS((bl, seq, LANES), BF16), SDS((1, LANES), F32), SDS((1, LANES), F32)]
        + [SDS(x.shape, x.dtype) for x in extra],
        scratch_shapes=[pltpu.VMEM((ng, DH, DH), F32)] + (_comm_scratch() if extra else []), name="deltanet_bwd",
        compiler_params=_cp(2))(qkv, p_small, alp, dtp, sprev, tinv, d_o, *extra)


def _s5_table_specs():
    tab3 = pl.BlockSpec((None, LANES, 512), lambda gb, n: (gb, 0, 0))
    tab2 = pl.BlockSpec((S5_CH, 512), lambda gb, n: (0, gb))
    return [tab3] * 4 + [tab2] * 6 + [pl.BlockSpec((1, LANES), lambda gb, n: (0, gb))]


def s5_fwd(u, tables, dsk):
    bl, seq, _ = u.shape
    nc = seq // S5_CH

    def body(u_ref, *rest):
        tabs, (y_ref, xs_ref, xr_scr, xi_scr) = rest[:11], rest[11:]

        @pl.when(pl.program_id(1) == 0)
        def _():
            xr_scr[...] = jnp.zeros_like(xr_scr)
            xi_scr[...] = jnp.zeros_like(xi_scr)

        xp_re, xp_im = xr_scr[...], xi_scr[...]
        xs_ref[0:bl] = xp_re
        xs_ref[bl:2 * bl] = xp_im
        y, xn_re, xn_im = s5_chunk(u_ref[...], xp_re, xp_im, *[t[...] for t in tabs])
        y_ref[...] = y
        xr_scr[...] = xn_re
        xi_scr[...] = xn_im

    blk = lambda gb, n: (0, n, gb)
    return pl.pallas_call(
        body, grid=(GB, nc), in_specs=[pl.BlockSpec((bl, S5_CH, LANES), blk)] + _s5_table_specs(),
        out_specs=[pl.BlockSpec((bl, S5_CH, LANES), blk),
                   pl.BlockSpec((None, 2 * bl, 1, 512), lambda gb, n: (gb * nc + n, 0, 0, 0))],
        out_shape=[SDS((bl, seq, S5W), F32), SDS((GB * nc, 2 * bl, 1, 512), F32)],
        scratch_shapes=[pltpu.VMEM((bl, 1, 512), F32), pltpu.VMEM((bl, 1, 512), F32)], name="s5_fwd",
        compiler_params=_cp(2))(u, *tables, dsk)


def s5_bwd(u, tables, dsk, xs, dy):
    bl, seq, _ = u.shape
    nc = seq // S5_CH

    def body(u_ref, *rest):
        tabs, xs_ref, dy_ref = rest[:11], rest[11], rest[12]
        du_ref, dtabs, dxr_scr, dxi_scr = rest[13], rest[14:25], rest[25], rest[26]
        r = pl.program_id(1)

        @pl.when(r == 0)
        def _():
            for t in dtabs:
                t[...] = jnp.zeros_like(t)
            dxr_scr[...] = jnp.zeros_like(dxr_scr)
            dxi_scr[...] = jnp.zeros_like(dxi_scr)

        _, vjp = jax.vjp(s5_chunk, u_ref[...], xs_ref[0:bl], xs_ref[bl:2 * bl], *[t[...] for t in tabs])
        grads = vjp((dy_ref[...], dxr_scr[...], dxi_scr[...]))
        du_ref[...] = grads[0].astype(BF16)
        dxr_scr[...] = grads[1]
        dxi_scr[...] = grads[2]
        for t, g in zip(dtabs, grads[3:]):
            t[...] += g

    blk = lambda gb, r: (0, nc - 1 - r, gb)
    tab_shapes = [SDS(t.shape, F32) for t in tables] + [SDS(dsk.shape, F32)]
    return pl.pallas_call(
        body, grid=(GB, nc),
        in_specs=[pl.BlockSpec((bl, S5_CH, LANES), blk)] + _s5_table_specs()
        + [pl.BlockSpec((None, 2 * bl, 1, 512), lambda gb, r: (gb * nc + nc - 1 - r, 0, 0, 0)), pl.BlockSpec((bl, S5_CH, LANES), blk)],
        out_specs=[pl.BlockSpec((bl, S5_CH, LANES), blk)] + _s5_table_specs(),
        out_shape=[SDS((bl, seq, S5W), BF16)] + tab_shapes,
        scratch_shapes=[pltpu.VMEM((bl, 1, 512), F32), pltpu.VMEM((bl, 1, 512), F32)], name="s5_bwd",
        compiler_params=_cp(2))(u, *tables, dsk, xs, dy)


def s5_tables_fwd(params):
    shapes = [SDS((GB, LANES, 512), F32)] * 4 + [SDS((S5_CH, S5N), F32)] * 6

    def body(*refs):
        for r, t in zip(refs[7:], s5_tables(*[p[...] for p in refs[:7]])):
            r[...] = t

    return pl.pallas_call(body, out_shape=shapes, name="s5_tables_fwd", compiler_params=_cp())(*params)


def s5_tables_bwd(params, dtables):
    def body(*refs):
        _, vjp = jax.vjp(s5_tables, *[p[...] for p in refs[:7]])
        for r, g in zip(refs[17:], vjp(tuple(t[...] for t in refs[7:17]))):
            r[...] = g

    return pl.pallas_call(body, out_shape=[SDS(p.shape, F32) for p in params], name="s5_tables_bwd",
                          compiler_params=_cp())(*params, *dtables)


def ada_fwd(c_all, w_loc, b_loc):
    def body(c_ref, w_ref, b_ref, o_ref):
        o_ref[...] = _dot(_silu(c_ref[...]), w_ref[...]) + b_ref[...]

    return pl.pallas_call(body, out_shape=SDS((c_all.shape[0], w_loc.shape[1]), F32), name="ada_fwd",
                          compiler_params=_cp())(c_all, w_loc, b_loc)


def ada_bwd(c_all, dmod_mine, dmod_all):
    def body(c_ref, dm_ref, da_ref, gw_ref, gb_ref):
        gw_ref[...] = _dot_tn(_silu(c_ref[...]), dm_ref[...])
        gb_ref[...] = jnp.sum(da_ref[...], axis=0, keepdims=True)

    return pl.pallas_call(body, out_shape=[SDS((D, dmod_mine.shape[1]), F32), SDS((1, dmod_all.shape[1]), F32)],
                          name="ada_bwd", compiler_params=_cp())(c_all, dmod_mine, dmod_all)


def adamw(name, parts, w, m, v):
    k_parts, rows, cols = parts.shape
    tr = _pick(rows, (256, 128, 64, 32, 16, 8))

    def body(p_ref, w_ref, m_ref, v_ref, g_ref, d_ref, mo_ref, vo_ref):
        g = p_ref[0].astype(F32)
        for k in range(1, k_parts):
            g = g + p_ref[k].astype(F32)
        _adam_store(g, w_ref, m_ref, v_ref, g_ref, d_ref, mo_ref, vo_ref)

    blk = pl.BlockSpec((tr, cols), lambda i: (i, 0))
    return pl.pallas_call(
        body, grid=(rows // tr,), in_specs=[pl.BlockSpec((k_parts, tr, cols), lambda i: (0, i, 0)), blk, blk, blk],
        out_specs=[blk] * 4, out_shape=[SDS((rows, cols), F32)] * 4, name=name, compiler_params=_cp(1))(parts, w, m, v)


def _adam_store(g, w_ref, m_ref, v_ref, g_ref, d_ref, mo_ref, vo_ref):
    m_new = ADAM_B1 * m_ref[...] + (1.0 - ADAM_B1) * g
    v_new = ADAM_B2 * v_ref[...] + (1.0 - ADAM_B2) * (g * g)
    m_hat = m_new / (1.0 - ADAM_B1 ** ADAM_STEP)
    v_hat = v_new / (1.0 - ADAM_B2 ** ADAM_STEP)
    g_ref[...] = g
    d_ref[...] = -ADAM_LR * (m_hat / (jnp.sqrt(v_hat) + ADAM_EPS) + ADAM_WD * w_ref[...])
    mo_ref[...] = m_new
    vo_ref[...] = v_new


def adamw_t(name, parts, w, m, v):
    k_parts, r, c = parts.shape
    tc = _pick(c, (256, 128))

    def body(p_ref, w_ref, m_ref, v_ref, g_ref, d_ref, mo_ref, vo_ref):
        gt = p_ref[0].astype(F32)
        for k in range(1, k_parts):
            gt = gt + p_ref[k].astype(F32)
        _adam_store(gt.T, w_ref, m_ref, v_ref, g_ref, d_ref, mo_ref, vo_ref)

    blk = pl.BlockSpec((tc, r), lambda j: (j, 0))
    return pl.pallas_call(
        body, grid=(c // tc,), in_specs=[pl.BlockSpec((k_parts, r, tc), lambda j: (0, 0, j)), blk, blk, blk],
        out_specs=[blk] * 4, out_shape=[SDS((c, r), F32)] * 4, name=name, compiler_params=_cp(1))(parts, w, m, v)


def _comm_scratch():
    return [pltpu.SemaphoreType.DMA((7,)), pltpu.SemaphoreType.DMA((7,)), pltpu.SemaphoreType.DMA]


HBM_SPEC = pl.BlockSpec(memory_space=pl.ANY)


def _gather_phases(x_ref, out_ref, send_sems, recv_sems, local_sem):
    mx, my, mc = lax.axis_index("x"), lax.axis_index("y"), lax.axis_index("c")
    me, sibling = (mx, my, mc), (mx, my, 1 - mc)
    chips = [(1 - mx, my), (mx, 1 - my), (1 - mx, 1 - my)]

    def slot(px, py, pc):
        return out_ref.at[4 * px + 2 * py + pc]

    def copy(k, block, to, src=None):
        return pltpu.make_async_remote_copy(
            src_ref=slot(*block) if src is None else src, dst_ref=slot(*block), send_sem=send_sems.at[k],
            recv_sem=recv_sems.at[k], device_id=to, device_id_type=pl.DeviceIdType.MESH)

    def first():
        return [copy(0, me, sibling, src=x_ref)] + [copy(1 + j, me, (*chip, mc), src=x_ref) for j, chip in enumerate(chips)]

    def passed():
        return [copy(4 + j, (*chip, mc), sibling) for j, chip in enumerate(chips)]

    def start():
        pltpu.make_async_copy(x_ref, slot(*me), local_sem).start()
        for cp in first():
            cp.start()

    def forward():
        for j, chip in enumerate(chips):
            copy(1 + j, (*chip, mc), me).wait_recv()
            passed()[j].start()

    def finish():
        copy(0, sibling, me).wait_recv()
        for j, chip in enumerate(chips):
            copy(4 + j, (*chip, 1 - mc), me).wait_recv()
        for cp in first() + passed():
            cp.wait_send()
        pltpu.make_async_copy(x_ref, slot(*me), local_sem).wait()

    return start, forward, finish


def _exchange_phases(x_ref, out_ref, send_sems, recv_sems, local_sem):
    mx, my, mc = lax.axis_index("x"), lax.axis_index("y"), lax.axis_index("c")
    me = 4 * mx + 2 * my + mc

    def peer(k):
        return mx ^ (k >> 2), my ^ ((k >> 1) & 1), mc ^ (k & 1)

    def sends():
        out = []
        for k in range(1, NDEV):
            px, py, pc = peer(k)
            out.append(pltpu.make_async_remote_copy(
                src_ref=x_ref.at[4 * px + 2 * py + pc], dst_ref=out_ref.at[me], send_sem=send_sems.at[k - 1],
                recv_sem=recv_sems.at[k - 1], device_id=(px, py, pc), device_id_type=pl.DeviceIdType.MESH))
        return out

    def start():
        pltpu.make_async_copy(x_ref.at[me], out_ref.at[me], local_sem).start()
        for cp in sends():
            cp.start()

    def finish():
        for k in range(1, NDEV):
            px, py, pc = peer(k)
            pltpu.make_async_remote_copy(
                src_ref=x_ref.at[me], dst_ref=out_ref.at[4 * px + 2 * py + pc], send_sem=send_sems.at[k - 1],
                recv_sem=recv_sems.at[k - 1], device_id=(px, py, pc), device_id_type=pl.DeviceIdType.MESH).wait_recv()
        for cp in sends():
            cp.wait_send()
        pltpu.make_async_copy(x_ref.at[me], out_ref.at[me], local_sem).wait()

    return start, finish


def all_gather(name, x):
    def body(x_ref, out_ref, send_sems, recv_sems, local_sem):
        for phase in _gather_phases(x_ref, out_ref, send_sems, recv_sems, local_sem):
            phase()

    return pl.pallas_call(body, out_shape=SDS((NDEV,) + x.shape, x.dtype), in_specs=[HBM_SPEC], out_specs=HBM_SPEC,
                          scratch_shapes=_comm_scratch(), name=name)(x)


def all_gather_pair(name, x1, x2):
    def body(x1_ref, x2_ref, o1_ref, o2_ref, *sems):
        first = _gather_phases(x1_ref, o1_ref, *sems[:3])
        second = _gather_phases(x2_ref, o2_ref, *sems[3:])
        for phase1, phase2 in zip(first, second):
            phase1()
            phase2()

    return pl.pallas_call(
        body, out_shape=[SDS((NDEV,) + x1.shape, x1.dtype), SDS((NDEV,) + x2.shape, x2.dtype)], in_specs=[HBM_SPEC] * 2,
        out_specs=[HBM_SPEC] * 2, scratch_shapes=_comm_scratch() + _comm_scratch(), name=name)(x1, x2)


def all_to_all(name, x):
    def body(x_ref, out_ref, send_sems, recv_sems, local_sem):
        for phase in _exchange_phases(x_ref, out_ref, send_sems, recv_sems, local_sem):
            phase()

    return pl.pallas_call(body, out_shape=SDS(x.shape, x.dtype), in_specs=[HBM_SPEC], out_specs=HBM_SPEC,
                          scratch_shapes=_comm_scratch(), name=name)(x)


def _pack(arrs, dtype, row_mult=8):
    segs = []
    for a in arrs:
        flat = a.reshape(-1).astype(dtype)
        segs.append(jnp.pad(flat, (0, (-flat.shape[0]) % ROW)))
    flat = jnp.concatenate(segs)
    flat = jnp.pad(flat, (0, (-flat.shape[0]) % (ROW * row_mult)))
    return flat.reshape(-1, ROW)


def _unpack(buf, shapes):
    flat = buf.reshape(-1)
    out, off = [], 0
    for s in shapes:
        n = math.prod(s)
        out.append(flat[off:off + n].reshape(s))
        off += n + (-n) % ROW
    return out


def _pack_rows(arrs, axis):
    padded = []
    for t in arrs:
        pad = [(0, 0)] * t.ndim
        pad[axis] = (0, _tile_rows(t.shape[axis]) - t.shape[axis])
        padded.append(jnp.pad(t, pad))
    return jnp.concatenate(padded, axis=axis)


def _tile_rows(r):
    return r + (-r) % BF16_TILE_ROWS


def _unpack8(buf, shapes):
    flat = buf.reshape(NDEV, -1)
    out, off = [], 0
    for s in shapes:
        n = math.prod(s)
        out.append(flat[:, off:off + n].reshape((NDEV,) + tuple(s)))
        off += n + (-n) % ROW
    return out


def kernel(x, c, w_ada, b_ada, g_ffn1, w1_ffn1, w3_ffn1, w2_ffn1, g_mix, w_in, conv_qkv, a_log, dt_bias, g_onorm, lam_re, lam_im, log_step, b_re, b_im, c_re, c_im, d_skip, w_glu, b_glu, w_proj_a, w_proj_b, w_out, g_ffn2, w1_ffn2, w3_ffn2, w2_ffn2, g_final, loss_target, m_w_ada, m_b_ada, m_g_ffn1, m_w1_ffn1, m_w3_ffn1, m_w2_ffn1, m_g_mix, m_w_in, m_conv_qkv, m_a_log, m_dt_bias, m_g_onorm, m_lam_re, m_lam_im, m_log_step, m_b_re, m_b_im, m_c_re, m_c_im, m_d_skip, m_w_glu, m_b_glu, m_w_proj_a, m_w_proj_b, m_w_out, m_g_ffn2, m_w1_ffn2, m_w3_ffn2, m_w2_ffn2, m_g_final, v_w_ada, v_b_ada, v_g_ffn1, v_w1_ffn1, v_w3_ffn1, v_w2_ffn1, v_g_mix, v_w_in, v_conv_qkv, v_a_log, v_dt_bias, v_g_onorm, v_lam_re, v_lam_im, v_log_step, v_b_re, v_b_im, v_c_re, v_c_im, v_d_skip, v_w_glu, v_b_glu, v_w_proj_a, v_w_proj_b, v_w_out, v_g_ffn2, v_w1_ffn2, v_w3_ffn2, v_w2_ffn2, v_g_final):
    a = dict(locals())
    bl, seq, _ = x.shape
    t_rows = bl * seq
    me = 4 * lax.axis_index("x") + 2 * lax.axis_index("y") + lax.axis_index("c")
    tm_ew = _pick(seq, (256, 128, 64))

    loc = {n: (a[n][0].T if n in COL_SHARDED else a[n][0]) for n in RS_WEIGHTS}
    wfull, gw, res = {}, {}, {}

    def pack_local(names):
        return _pack_rows([loc[n].astype(BF16).reshape(-1, ROW) for n in names], 0)

    def unpack_full(buf, names):
        r0 = 0
        for n in names:
            r = loc[n].size // ROW
            wfull[n] = buf[:, r0:r0 + r, :].reshape(-1, loc[n].shape[1])
            r0 += _tile_rows(r)

    def pack_grads(names):
        return _pack_rows([gw[n].astype(BF16).reshape(NDEV, -1, ROW) for n in names], 1)

    def update(buf, names):
        r0 = 0
        for n in names:
            r = loc[n].size // ROW
            parts = buf[:, r0:r0 + r, :].reshape((NDEV,) + loc[n].shape)
            r0 += _tile_rows(r)
            step = adamw_t if n in COL_SHARDED else adamw
            out = step("adamw_" + n, parts, a[n][0], a["m_" + n][0], a["v_" + n][0])
            for kind, t in zip(("grad", "delta", "new_m", "new_v"), out):
                res[kind + "_" + n] = t[None]

    sm, wg_ffn1 = all_gather_pair("gather_inputs", _pack([c, conv_qkv[0]], F32), pack_local(G_FFN1))
    c_loc, conv_loc = _unpack8(sm, [c.shape, conv_qkv.shape[1:]])
    c_all = c_loc.reshape(NDEV * bl, D)
    conv_full = conv_loc.transpose(1, 0, 2).reshape(CONVW, 3 * DNW)

    n_ada = w_ada.shape[2]
    mod_part = ada_fwd(c_all, w_ada[0], lax.dynamic_slice(b_ada, (0, me * n_ada), (1, n_ada)))
    mod_all = all_gather("gather_mod", mod_part).transpose(1, 0, 2).reshape(NDEV * bl, 9 * D)
    mod = lax.dynamic_slice(mod_all, (me * bl, 0), (bl, 9 * D)).reshape(bl, 9, D)
    mods = [mod[:, k:k + 1, :] for k in range(9)]

    h0 = x.reshape(t_rows, D)
    h1, f1, u1, pa1, pb1, wg_rest = ffn_fwd("ffn1_fwd", h0, mod[:, 0:3, :], g_ffn1, wg_ffn1, wg_ffn1, wg_ffn1, seq,
                                            gather=pack_local(G_MIX))
    unpack_full(wg_rest, G_MIX)
    win = wfull['w_in']
    o_small, o_s5, o_gate = 4 * DNW, 4 * DNW + 2 * NH, 4 * DNW + 2 * NH + S5W
    w_dn, w_small = win[:o_small], jnp.pad(win[o_small:o_s5], ((0, LANES - 2 * NH), (0, 0)))
    w_s5, w_gate = win[o_s5:o_gate], win[o_gate:]
    w_pieces = [w_dn, w_small, w_s5, w_gate]
    u2, p_dn, p_small, p_s5, p_gate = mix_in_fwd(h1, mods[3], mods[4], g_mix, w_pieces, seq)

    conv8 = jnp.pad(conv_full, ((0, 8 - CONVW), (0, 0)))
    alp = jnp.pad(a_log, ((0, 0), (NH, LANES - 2 * NH)))
    dtp = jnp.pad(dt_bias, ((0, 0), (NH, LANES - 2 * NH)))
    nb_dn = DN_ROWS if bl % DN_ROWS == 0 else 1
    p_dn3, p_small3 = p_dn.reshape(bl, seq, 4 * DNW), p_small.reshape(bl, seq, LANES)
    qkv3 = dn_prep_fwd(p_dn3, conv8)
    o_pre3, sprev, tinv, wg_ffn2 = deltanet_fwd(qkv3, p_small3, alp, dtp, nb_dn, gather=pack_local(G_FFN2))
    o_pre = o_pre3.reshape(t_rows, DNW)
    z_raw = p_dn[:, 3 * DNW:]

    s5_params = [lam_re.reshape(1, S5N), lam_im.reshape(1, S5N), log_step,
                 b_re[0].transpose(2, 0, 1).reshape(S5C, S5N), b_im[0].transpose(2, 0, 1).reshape(S5C, S5N),
                 c_re[0].transpose(1, 0, 2).reshape(S5C, S5N), c_im[0].transpose(1, 0, 2).reshape(S5C, S5N)]
    tables = s5_tables_fwd(s5_params)
    p_s53 = p_s5.reshape(bl, seq, S5W)
    y_s53, xs = s5_fwd(p_s53, tables, d_skip)
    y_s5 = y_s53.reshape(t_rows, S5W)
    tail_in = [o_pre, z_raw, y_s5, p_gate]
    tail_w = [g_onorm, wfull['w_glu'], b_glu, wfull['w_proj_a'], wfull['w_proj_b']]
    (merged,) = ew_call("mix_tail", fn_mix_tail, tail_in, [], tail_w, [(D, BF16)], tm_ew, seq)
    mo, h2 = mix_out_fwd(merged, wfull['w_out'], h1, mods[5], seq)
    dh3, f3, u3, pa3, pb3, dg_final, loss_part = ffn_fwd(
        "ffn2_fwd", h2, mod[:, 6:9, :], g_ffn2, wg_ffn2, wg_ffn2, wg_ffn2, seq,
        loss_head=(loss_target.reshape(t_rows, D), g_final.reshape(1, D)))


    dh2, a3, d1_3, d3_3, df3, dmod_c, dg_ffn2 = ffn_bwd("ffn2_bwd", dh3, h2, f3, pa3, pb3, mod[:, 6:9, :], g_ffn2, wg_ffn2,
                                                   wg_ffn2, wg_ffn2, seq)
    gw['w1_ffn2'] = mm_tn("gw1_ffn2", d1_3, u3)
    gw['w3_ffn2'] = mm_tn("gw3_ffn2", d3_3, u3)
    gw['w2_ffn2'] = mm_tn("gw2_ffn2", a3, df3)

    dmo, d_merged, dgt2 = mix_out_bwd(dh2, mo, wfull['w_out'], mods[5], seq)
    gw['w_out'] = mm_tn("gw_out", merged, dmo)
    (d_opre, d_z, d_ys5, d_gate), _, tail_gw = ew_vjp_call(
        "mix_tail_bwd", fn_mix_tail, tail_in, [], tail_w, [d_merged], [(0, F32), (1, F32), (2, F32), (3, BF16)],
        _pick(seq, (512, 256, 128, 64)), seq)
    dg_onorm, gw['w_glu'], dg_bglu, gw['w_proj_a'], gw['w_proj_b'] = tail_gw
    d_qkv3, d_psmall3, d_alp, d_dtp, rs_ffn2 = deltanet_bwd(
        qkv3, p_small3, alp, dtp, sprev, tinv, d_opre.reshape(bl, seq, DNW), nb_dn, exchange=pack_grads(G_FFN2))
    d_pdn3, d_conv8 = dn_prep_bwd(p_dn3, conv8, d_qkv3, d_z.reshape(bl, seq, DNW))
    d_pdn, d_psmall = d_pdn3.reshape(t_rows, 4 * DNW), d_psmall3.reshape(t_rows, LANES)

    s5_out = s5_bwd(p_s53, tables, d_skip, xs, d_ys5.reshape(bl, seq, S5W))
    d_ps5, d_tables, dg_dskip = s5_out[0].reshape(t_rows, S5W), s5_out[1:11], s5_out[11]
    d_s5p = s5_tables_bwd(s5_params, d_tables)

    gw['w_in'] = jnp.concatenate([mm_tn("gw_dn", d_pdn, u2), mm_tn("gw_small", d_psmall, u2)[:2 * NH],
                                  mm_tn("gw_s5", d_ps5, u2), mm_tn("gw_gate", d_gate, u2)], axis=0)
    dh1, dsh2, dsc2, dg_mix = mix_in_bwd([d_pdn, d_psmall, d_ps5, d_gate], w_pieces, h1, mods[3], mods[4], g_mix, dh2, seq)

    dh0, a1, d1_1, d3_1, df1, dmod_a, dg_ffn1, rs_mix = ffn_bwd(
        "ffn1_bwd", dh1, h0, f1, pa1, pb1, mod[:, 0:3, :], g_ffn1, wg_ffn1, wg_ffn1, wg_ffn1, seq,
        exchange=pack_grads(G_MIX))
    dmod_mine = jnp.concatenate([dmod_a, dsh2, dsc2, dgt2, dmod_c], axis=1).reshape(bl, 9 * D)
    small_grads = {
        'g_ffn1': dg_ffn1, 'g_mix': dg_mix, 'a_log': d_alp[:, NH:2 * NH], 'dt_bias': d_dtp[:, NH:2 * NH],
        'g_onorm': dg_onorm, 'lam_re': d_s5p[0].reshape(1, S5G, S5P), 'lam_im': d_s5p[1].reshape(1, S5G, S5P),
        'log_step': d_s5p[2],
        'b_re': d_s5p[3].reshape(S5C, S5G, S5P).transpose(1, 2, 0)[None],
        'b_im': d_s5p[4].reshape(S5C, S5G, S5P).transpose(1, 2, 0)[None],
        'c_re': d_s5p[5].reshape(S5C, S5G, S5P).transpose(1, 0, 2)[None],
        'c_im': d_s5p[6].reshape(S5C, S5G, S5P).transpose(1, 0, 2)[None],
        'd_skip': dg_dskip, 'b_glu': dg_bglu, 'g_ffn2': dg_ffn2, 'g_final': dg_final.reshape(D)}
    small_shapes = [a[n].shape for n in SMALL]
    small_pack = _pack([small_grads[n] for n in SMALL] + [loss_part], F32)
    n_small = small_pack.shape[0]
    small_buf = jnp.concatenate([small_pack, _pack([dmod_mine, d_conv8[:CONVW]], F32)], axis=0)

    gw['w1_ffn1'], sg = mm_tn("gw1_ffn1", d1_1, u1, gather=small_buf)
    gw['w3_ffn1'], rs_w1 = mm_tn("gw3_ffn1", d3_1, u1, exchange=pack_grads(['w1_ffn1']))
    gw['w2_ffn1'], rs_w3 = mm_tn("gw2_ffn1", a1, df1, exchange=pack_grads(['w3_ffn1']))
    rs_w2 = all_to_all("scatter_w2_ffn1", pack_grads(['w2_ffn1']))

    update(rs_ffn2, G_FFN2)
    update(rs_mix, G_MIX)
    update(rs_w1, ['w1_ffn1'])
    update(rs_w3, ['w3_ffn1'])
    update(rs_w2, ['w2_ffn1'])
    pieces = _unpack8(sg[:, n_small:, :], [dmod_mine.shape, (CONVW, 3 * DNW)])
    dmod_all = pieces[0].reshape(NDEV * bl, 9 * D)
    g_wada, g_bada = ada_bwd(c_all, lax.dynamic_slice(dmod_all, (0, me * n_ada), (NDEV * bl, n_ada)), dmod_all)

    n_conv = conv_qkv.shape[2]
    conv_parts = lax.dynamic_slice(pieces[1], (0, 0, me * n_conv), (NDEV, CONVW, n_conv))
    conv_parts = jnp.pad(conv_parts.reshape(NDEV, 1, -1), ((0, 0), (0, 7), (0, 0)))
    pad8 = lambda t: jnp.pad(t.reshape(1, -1), ((0, 7), (0, 0)))
    conv_res = adamw("adamw_conv", conv_parts, pad8(conv_qkv), pad8(m_conv_qkv), pad8(v_conv_qkv))
    for kind, buf in zip(("grad", "delta", "new_m", "new_v"), conv_res):
        res[kind + "_conv_qkv"] = buf[0].reshape(conv_qkv.shape)

    no_param = jnp.zeros_like(loss_part)
    small_res = adamw("adamw_small", sg[:, :n_small, :],
                      *[_pack([a[p + n] for n in SMALL] + [no_param], F32) for p in ("", "m_", "v_")])
    for kind, buf in zip(("grad", "delta", "new_m", "new_v"), small_res):
        for n, t in zip(SMALL, _unpack(buf, small_shapes)):
            res[kind + "_" + n] = t
    loss = _unpack(small_res[0], small_shapes + [loss_part.shape])[-1][0, 0]

    for n, g in (("w_ada", g_wada), ("b_ada", g_bada)):
        shp = a[n].shape
        r2 = lambda t: t.reshape(-1, shp[-1]) if n == "w_ada" else pad8(t)
        out = adamw("adamw_" + n, r2(g)[None], r2(a[n]), r2(a["m_" + n]), r2(a["v_" + n]))
        for kind, buf in zip(("grad", "delta", "new_m", "new_v"), out):
            res[kind + "_" + n] = (buf if n == "w_ada" else buf[0:1]).reshape(shp)

    outs = [loss, dh0.reshape(x.shape)]
    for kind in ("grad", "delta", "new_m", "new_v"):
        outs += [res[kind + "_" + n] for n in WEIGHTS]
    return tuple(outs)
```

```python
import math

import jax
import jax.numpy as jnp
from jax import lax
from jax.experimental import pallas as pl
from jax.experimental.pallas import tpu as pltpu

F32 = jnp.float32
BF16 = jnp.bfloat16
HI = lax.Precision.HIGHEST
H3 = lax.Precision.HIGH
SDS = jax.ShapeDtypeStruct

D = 1024
FF = 2816
FFN_TF = FF
FFN_FWD_TM = 256
FFN_BWD_TM = 256
NH = 8
DH = 64
DNW = NH * DH
CONVW = 4
CH = 64
S5_CH = 128
ACC_LIMIT = 6 * 1024 * 1024
BF16_TILE_ROWS = 16
DN_ROWS = 4
S5W = 512
S5G = 32
S5P = 64
S5C = 16
S5N = S5G * S5P
GB = 4
NDEV = 8
EPS = 1e-6
LANES = 128
ROW = 1024
VMEM_LIMIT = 56 * 1024 * 1024

ADAM_LR, ADAM_B1, ADAM_B2, ADAM_EPS, ADAM_WD, ADAM_STEP = 0.001, 0.9, 0.999, 1e-08, 0.01, 10

WEIGHTS = ['w_ada', 'b_ada', 'g_ffn1', 'w1_ffn1', 'w3_ffn1', 'w2_ffn1', 'g_mix', 'w_in', 'conv_qkv', 'a_log',
           'dt_bias', 'g_onorm', 'lam_re', 'lam_im', 'log_step', 'b_re', 'b_im', 'c_re', 'c_im', 'd_skip', 'w_glu',
           'b_glu', 'w_proj_a', 'w_proj_b', 'w_out', 'g_ffn2', 'w1_ffn2', 'w3_ffn2', 'w2_ffn2', 'g_final']
RS_WEIGHTS = ['w1_ffn1', 'w3_ffn1', 'w2_ffn1', 'w_in', 'w_glu', 'w_proj_a', 'w_proj_b', 'w_out', 'w1_ffn2', 'w3_ffn2',
              'w2_ffn2']
COL_SHARDED = {'w1_ffn1', 'w3_ffn1', 'w_in', 'w_proj_a', 'w_proj_b', 'w1_ffn2', 'w3_ffn2'}
G_FFN1 = ['w1_ffn1', 'w3_ffn1', 'w2_ffn1']
G_MIX = ['w_in', 'w_glu', 'w_proj_a', 'w_proj_b', 'w_out']
G_FFN2 = ['w1_ffn2', 'w3_ffn2', 'w2_ffn2']
SMALL = ['g_ffn1', 'g_mix', 'a_log', 'dt_bias', 'g_onorm', 'lam_re', 'lam_im', 'log_step', 'b_re', 'b_im', 'c_re',
         'c_im', 'd_skip', 'b_glu', 'g_ffn2', 'g_final']


def _cp(n_grid=0):
    if n_grid:
        return pltpu.CompilerParams(vmem_limit_bytes=VMEM_LIMIT, dimension_semantics=("arbitrary",) * n_grid)
    return pltpu.CompilerParams(vmem_limit_bytes=VMEM_LIMIT)


def _dot(a, b):
    return jnp.dot(a.astype(BF16), b.astype(BF16), preferred_element_type=F32)


def _dot_nt(a, b):
    return lax.dot_general(a.astype(BF16), b.astype(BF16), (((1,), (1,)), ((), ())), preferred_element_type=F32)


def _dot_tn(a, b):
    return lax.dot_general(a.astype(BF16), b.astype(BF16), (((0,), (0,)), ((), ())), preferred_element_type=F32)


def _dot_hi(a, b):
    return jnp.dot(a, b, precision=HI, preferred_element_type=F32)


@jax.custom_vjp
def bdot(a, b):
    return _dot(a, b)


bdot.defvjp(lambda a, b: (_dot(a, b), (a, b)),
            lambda r, g: (_dot_nt(g, r[1]).astype(r[0].dtype), _dot_tn(r[0], g).astype(r[1].dtype)))


@jax.custom_vjp
def bdot_nt(a, b):
    return _dot_nt(a, b)


bdot_nt.defvjp(lambda a, b: (_dot_nt(a, b), (a, b)),
               lambda r, g: (_dot(g, r[1]).astype(r[0].dtype), _dot_tn(g, r[0]).astype(r[1].dtype)))


def _silu(x):
    return x * jax.nn.sigmoid(x)


def _iota2(shape, axis):
    return lax.broadcasted_iota(jnp.int32, shape, axis)


def normmod(h, g, sc, sh):
    y = h * lax.rsqrt(jnp.mean(h * h, axis=-1, keepdims=True) + EPS) * g
    return y * (1.0 + sc) + sh


def fn_merge(gate, ya, yb):
    return (jax.nn.sigmoid(gate[:, :D]) * ya + jax.nn.sigmoid(gate[:, D:]) * yb,)


def fn_glu(y, w, b):
    ge = jax.nn.gelu(y)
    return (ge * jax.nn.sigmoid(bdot(ge, w) + b),)


def fn_onorm(o, z, g_on):
    r = _iota2((DH, DNW), 0)
    c = _iota2((DH, DNW), 1)
    expand = (c % DH == r).astype(F32)
    r2 = _iota2((DNW, DNW), 0)
    c2 = _iota2((DNW, DNW), 1)
    avg = (r2 // DH == c2 // DH).astype(F32) * (1.0 / DH)
    ms = bdot(o * o, avg)
    return (o * lax.rsqrt(ms + EPS) * _dot_hi(g_on, expand) * _silu(z),)


def fn_mix_tail(o_pre, z, y_s5, gate, g_on, w_glu, b_glu, wa_t, wb_t):
    (oa,) = fn_onorm(o_pre, z, g_on)
    (ob,) = fn_glu(y_s5, w_glu, b_glu)
    return fn_merge(gate, bdot_nt(oa, wa_t), bdot_nt(ob, wb_t))


def gate_fn(small, alp, dtp):
    beta = jax.nn.sigmoid(small)
    la = -jnp.exp(alp) * jax.nn.softplus(small + dtp)
    tri = (_iota2((CH, CH), 0) >= _iota2((CH, CH), 1)).astype(F32)
    gc = _dot_hi(tri, la)
    gct = lax.dot_general(la, tri, (((0,), (1,)), ((), ())), precision=HI, preferred_element_type=F32)
    return beta, gc, gct


def _bdg(a, b, ca, cb, hi):
    if not hi:
        a, b = a.astype(BF16), b.astype(BF16)
    return lax.dot_general(a, b, (((ca,), (cb,)), ((0,), (0,))), precision=H3 if hi else None,
                           preferred_element_type=F32)


def _batched_matmuls(hi):
    nn_ = lambda a, b: _bdg(a, b, 2, 1, hi)
    nt_ = lambda a, b: _bdg(a, b, 2, 2, hi)
    tn_ = lambda a, b: _bdg(a, b, 1, 1, hi)
    nn = jax.custom_vjp(nn_)
    nn.defvjp(lambda a, b: (nn_(a, b), (a, b)), lambda r, g: (nt_(g, r[1]), tn_(r[0], g)))
    nt = jax.custom_vjp(nt_)
    nt.defvjp(lambda a, b: (nt_(a, b), (a, b)), lambda r, g: (nn_(g, r[1]), tn_(g, r[0])))
    tn = jax.custom_vjp(tn_)
    tn.defvjp(lambda a, b: (tn_(a, b), (a, b)), lambda r, g: (nt_(r[1], g), nn_(r[0], g)))
    return nn, nt, tn


bnn, bnt, btn = _batched_matmuls(False)
hnn, hnt, htn = _batched_matmuls(True)


def _unit_lower_inverse(a):
    r = _iota2((1, CH, CH), 1)
    c = _iota2((1, CH, CH), 2)
    eye = (r == c).astype(F32)
    d = jnp.where(r // 8 == c // 8, a, 0.0)
    inv = eye - d
    p = d
    for _ in range(2):
        p = hnn(p, p)
        inv = inv + hnn(inv, p)
    for blk in (16, 32, 64):
        off = jnp.where((r // blk == c // blk) & (r // (blk // 2) != c // (blk // 2)), a, 0.0)
        mm = hnn if blk == 16 else bnn
        inv = inv - mm(mm(inv, off), inv)
    return inv


@jax.custom_vjp
def _inverse_given(a, t):
    return t


_inverse_given.defvjp(lambda a, t: (t, t), lambda t, g: (-hnt(htn(t, g), t), jnp.zeros_like(t)))


def dn_prep(xc, w):
    t = xc.shape[0] - 8
    c = xc[5:5 + t] * w[0:1] + xc[6:6 + t] * w[1:2] + xc[7:7 + t] * w[2:3] + xc[8:8 + t] * w[3:4]
    act = _silu(c)
    q, k, v = act[:, :DNW], act[:, DNW:2 * DNW], act[:, 2 * DNW:]
    ones = (_iota2((DNW, DNW), 0) // DH == _iota2((DNW, DNW), 1) // DH).astype(F32)
    q = q * lax.rsqrt(bdot(q * q, ones) + EPS) * (DH ** -0.5)
    k = k * lax.rsqrt(bdot(k * k, ones) + EPS)
    return jnp.concatenate([q, k, v], axis=1)


def dn_chunk(q, k, v, b, g, gt, s_prev, t_saved=None):
    r = _iota2((1, CH, CH), 1)
    c = _iota2((1, CH, CH), 2)
    causal = r >= c
    dec = jnp.where(causal, jnp.exp(jnp.where(causal, g - gt, 0.0)), 0.0)
    kb = k * b
    qk = bnt(jnp.concatenate([q, kb], axis=1), k)
    attn = qk[:, :CH] * dec
    a = jnp.where(r > c, qk[:, CH:] * dec, 0.0)
    tinv = _unit_lower_inverse(a) if t_saved is None else _inverse_given(a, t_saved)
    eg = jnp.exp(g)
    uw = hnn(tinv, jnp.concatenate([v * b, kb * eg], axis=2))
    g_last = g[:, CH - 1:CH]
    ws = bnn(jnp.concatenate([uw[..., DH:], q * eg], axis=1), s_prev)
    v_new = uw[..., :DH] - ws[:, :CH]
    o = ws[:, CH:] + bnn(attn, v_new)
    s_new = s_prev * jnp.exp(g_last) + btn(k * jnp.exp(g_last - g), v_new)
    return o, s_new, tinv


def s5_chunk(u, xp_re, xp_im, bb_re, bb_im, cc_re, cc_im, p0r, p0i, p1r, p1i, pir, pii, dsk):
    nb, ch, _ = u.shape
    u2 = u.reshape(nb * ch, LANES)
    bu_re = bdot(u2, bb_re).reshape(nb, ch, 512)
    bu_im = bdot(u2, bb_im).reshape(nb, ch, 512)
    xt_re = pir * bu_re - pii * bu_im
    xt_im = pir * bu_im + pii * bu_re
    tri = jnp.broadcast_to((_iota2((1, ch, ch), 1) >= _iota2((1, ch, ch), 2)).astype(F32), (nb, ch, ch))
    cs_re = hnn(tri, xt_re)
    cs_im = hnn(tri, xt_im)
    x_re = p0r * cs_re - p0i * cs_im + p1r * xp_re - p1i * xp_im
    x_im = p0r * cs_im + p0i * cs_re + p1r * xp_im + p1i * xp_re
    y = bdot_nt(x_re.reshape(nb * ch, 512), cc_re) - bdot_nt(x_im.reshape(nb * ch, 512), cc_im) + dsk * u2
    return y.reshape(nb, ch, LANES), x_re[:, ch - 1:ch], x_im[:, ch - 1:ch]


def s5_tables(lam_re, lam_im, log_step, bre, bim, cre, cim):
    expand = (_iota2((S5G, S5N), 1) // S5P == _iota2((S5G, S5N), 0)).astype(F32)
    step = _dot_hi(jnp.exp(log_step), expand)
    lre = jnp.minimum(lam_re, -1e-4)
    lr = lre * step
    ang = lam_im * step
    mag = jnp.exp(lr)
    lb_re = mag * jnp.cos(ang)
    lb_im = mag * jnp.sin(ang)
    den = lre * lre + lam_im * lam_im
    coef_re = ((lb_re - 1.0) * lre + lb_im * lam_im) / den
    coef_im = (lb_im * lre - (lb_re - 1.0) * lam_im) / den
    bb_re = coef_re * bre - coef_im * bim
    bb_im = coef_re * bim + coef_im * bre
    j = _iota2((S5_CH, 1), 0).astype(F32)
    jc = j - S5_CH // 2
    e0 = jnp.exp(jc * lr)
    e1 = jnp.exp((j + 1.0) * lr)
    ei = jnp.exp(-jc * lr)
    mask = (_iota2((LANES, 512), 0) // S5C == _iota2((LANES, 512), 1) // S5P).astype(F32)

    def blocks(t):
        return jnp.concatenate([(jnp.tile(t[:, gb * 512:(gb + 1) * 512], (LANES // S5C, 1)) * mask)[None]
                                for gb in range(GB)], axis=0)

    return (blocks(bb_re), blocks(bb_im), blocks(cre), blocks(cim),
            e0 * jnp.cos(jc * ang), e0 * jnp.sin(jc * ang),
            e1 * jnp.cos((j + 1.0) * ang), e1 * jnp.sin((j + 1.0) * ang),
            ei * jnp.cos(jc * ang), -ei * jnp.sin(jc * ang))


def _row_specs(tiled, batch, bcast, tm, tpb):
    specs = [pl.BlockSpec((tm, a.shape[1]), lambda i: (i, 0)) for a in tiled]
    specs += [pl.BlockSpec((None,) + a.shape[1:], lambda i: (i // tpb, 0, 0)) for a in batch]
    specs += [pl.BlockSpec(a.shape, lambda i, nd=a.ndim: (0,) * nd) for a in bcast]
    return specs


def ew_call(name, fn, tiled, batch, bcast, outs, tm, seq):
    t_rows = tiled[0].shape[0]
    n_in = len(tiled) + len(batch) + len(bcast)

    def body(*refs):
        vals = [r[...].astype(F32) for r in refs[:n_in]]
        for r, o in zip(refs[n_in:], fn(*vals)):
            r[...] = o.astype(r.dtype)

    return pl.pallas_call(
        body, grid=(t_rows // tm,), in_specs=_row_specs(tiled, batch, bcast, tm, seq // tm),
        out_specs=[pl.BlockSpec((tm, w), lambda i: (i, 0)) for w, _ in outs],
        out_shape=[SDS((t_rows, w), dt) for w, dt in outs], name=name, compiler_params=_cp(1))(*tiled, *batch, *bcast)


def ew_vjp_call(name, fn, tiled, batch, bcast, cts, want, tm, seq, addend=None):
    t_rows = tiled[0].shape[0]
    tpb = seq // tm
    n_t, n_b, n_c = len(tiled), len(batch), len(bcast)
    n_in = n_t + n_b + n_c
    extra = [] if addend is None else [addend]

    def body(*refs):
        i = pl.program_id(0)
        vals = [r[...].astype(F32) for r in refs[:n_in]]
        ctv = tuple(r[...].astype(F32) for r in refs[n_in:n_in + len(cts)])
        outs = refs[n_in + len(cts) + len(extra):]
        _, vjp = jax.vjp(fn, *vals)
        grads = vjp(ctv)
        for k, (r, (idx, _)) in enumerate(zip(outs[:len(want)], want)):
            g = grads[idx]
            if k == 0 and extra:
                g = g + refs[n_in + len(cts)][...]
            r[...] = g.astype(r.dtype)
        for k in range(n_b):
            r, g = outs[len(want) + k], grads[n_t + k]

            @pl.when(i % tpb == 0)
            def _(r=r, g=g):
                r[...] = g

            @pl.when(i % tpb != 0)
            def _(r=r, g=g):
                r[...] += g
        for k in range(n_c):
            r, g = outs[len(want) + n_b + k], grads[n_t + n_b + k]

            @pl.when(i == 0)
            def _(r=r, g=g):
                r[...] = g

            @pl.when(i != 0)
            def _(r=r, g=g):
                r[...] += g

    out_specs = [pl.BlockSpec((tm, tiled[idx].shape[1]), lambda i: (i, 0)) for idx, _ in want]
    out_specs += [pl.BlockSpec((None,) + a.shape[1:], lambda i: (i // tpb, 0, 0)) for a in batch]
    out_specs += [pl.BlockSpec(a.shape, lambda i, nd=a.ndim: (0,) * nd) for a in bcast]
    out_shape = [SDS(tiled[idx].shape, dt) for idx, dt in want]
    out_shape += [SDS(a.shape, F32) for a in batch] + [SDS(a.shape, F32) for a in bcast]
    res = pl.pallas_call(
        body, grid=(t_rows // tm,),
        in_specs=_row_specs(tiled, batch, bcast, tm, tpb)
        + [pl.BlockSpec((tm, a.shape[1]), lambda i: (i, 0)) for a in list(cts) + extra],
        out_specs=out_specs, out_shape=out_shape, name=name, compiler_params=_cp(1))(*tiled, *batch, *bcast, *cts, *extra)
    return res[:len(want)], res[len(want):len(want) + n_b], res[len(want) + n_b:]


def _pick(n, cands):
    for c in cands:
        if n % c == 0:
            return c
    return n


def mm_tn(name, a, b, exchange=None, gather=None):
    t_rows, m = a.shape
    n = b.shape[1]
    tn = n if n <= 1024 else _pick(n, (1024, 512, 256, 128))
    tm = max([t for t in range(LANES, m + 1, LANES) if m % t == 0 and t * tn * 4 <= ACC_LIMIT] or [m])
    tk = _pick(t_rows, (2048, 1024, 512, 256, 128, 64))
    grid = (m // tm, n // tn, t_rows // tk)
    extra = [x for x in (exchange, gather) if x is not None]
    ne = len(extra)

    def body(*refs):
        a_ref, b_ref = refs[:2]
        o_ref, acc = refs[2 + ne], refs[3 + 2 * ne]
        i, j, k = pl.program_id(0), pl.program_id(1), pl.program_id(2)
        first = (i == 0) & (j == 0) & (k == 0)
        middle = (i == grid[0] - 1) & (j == grid[1] - 1) & (k == grid[2] // 2)
        last = (i == grid[0] - 1) & (j == grid[1] - 1) & (k == grid[2] - 1)
        at_end = []
        for e, x in enumerate(extra):
            comm_refs = (refs[2 + e], refs[3 + ne + e]) + tuple(refs[4 + 2 * ne + 3 * e:7 + 2 * ne + 3 * e])
            if x is exchange:
                start, finish = _exchange_phases(*comm_refs)
                pl.when(first)(start)
            else:
                start, forward, finish = _gather_phases(*comm_refs)
                pl.when(first)(start)
                pl.when(middle)(forward)
            at_end.append(finish)

        @pl.when(k == 0)
        def _():
            acc[...] = jnp.zeros_like(acc)

        acc[...] += _dot_tn(a_ref[...], b_ref[...])

        @pl.when(k == grid[2] - 1)
        def _():
            o_ref[...] = acc[...].astype(BF16)

        for phase in at_end:
            pl.when(last)(phase)

    res = pl.pallas_call(
        body, grid=grid,
        in_specs=[pl.BlockSpec((tk, tm), lambda i, j, k: (k, i)), pl.BlockSpec((tk, tn), lambda i, j, k: (k, j))]
        + [HBM_SPEC] * ne,
        out_specs=[pl.BlockSpec((tm, tn), lambda i, j, k: (i, j))] + [HBM_SPEC] * ne,
        out_shape=[SDS((m, n), BF16)] + [SDS(x.shape if x is exchange else (NDEV,) + x.shape, x.dtype) for x in extra],
        scratch_shapes=[pltpu.VMEM((tm, tn), F32)] + _comm_scratch() * ne, name=name,
        compiler_params=_cp(3))(a, b, *extra)
    return res if extra else res[0]


def _ffn_weight_spec(w, k):
    assert FFN_TF == FF
    if w.ndim == 3:
        return pl.BlockSpec((NDEV, FF // NDEV, D), lambda i, j: (0, k, 0), pipeline_mode=pl.Buffered(1))
    return pl.BlockSpec((FF, D), lambda i, j: (0, 0), pipeline_mode=pl.Buffered(1))


def _ffn_weight(ref):
    return ref[...].reshape(FF, D)


def ffn_fwd(name, h, mod3, g, w1, w3, w2, seq, gather=None, loss_head=None):
    t_rows = h.shape[0]
    tm = _pick(seq, (FFN_FWD_TM, 128, 64))
    tf = FFN_TF
    tpb = seq // tm
    nf = FF // tf
    nt = t_rows // tm
    extra = [] if gather is None else [gather]
    head = [] if loss_head is None else list(loss_head)
    nh, ne = len(head), len(extra)

    def body(*refs):
        h_ref, mod_ref, g_ref, w1_ref, w3_ref, w2_ref = refs[:6]
        o0 = 6 + nh + ne
        ho_ref, f_ref, u_ref, h1_ref, h3_ref = refs[o0:o0 + 5]
        s0 = o0 + 5 + nh + ne
        acc = refs[s0]
        i, j = pl.program_id(0), pl.program_id(1)
        if extra:
            start, forward, finish = _gather_phases(refs[6 + nh], refs[o0 + 5 + nh], *refs[s0 + 1:s0 + 4])
            pl.when((i == 0) & (j == 0))(start)
            pl.when((i == nt // 2) & (j == 0))(forward)

        @pl.when(j == 0)
        def _():
            u_ref[...] = normmod(h_ref[...], g_ref[...], mod_ref[1:2, :], mod_ref[0:1, :]).astype(BF16)
            acc[...] = jnp.zeros_like(acc)

        u = u_ref[...]
        h1 = _dot_nt(u, _ffn_weight(w1_ref))
        h3 = _dot_nt(u, _ffn_weight(w3_ref))
        h1_ref[...] = h1.astype(BF16)
        h3_ref[...] = h3.astype(BF16)
        acc[...] += _dot(_silu(h1) * h3, _ffn_weight(w2_ref))

        @pl.when(j == nf - 1)
        def _():
            f_ref[...] = acc[...]
            h_out = h_ref[...] + 0.5 * mod_ref[2:3, :] * acc[...]
            if not head:
                ho_ref[...] = h_out
            else:
                t_ref, gf_ref, dg_ref, loss_ref = refs[6], refs[7], refs[o0 + 5], refs[o0 + 6]
                y, vjp = jax.vjp(lambda hh, gg: hh * lax.rsqrt(jnp.mean(hh * hh, axis=-1, keepdims=True) + EPS) * gg,
                                 h_out, gf_ref[...])
                e = y - t_ref[...]
                dh, dg = vjp(e * (1.0 / D))
                part = jnp.sum(jnp.sum(e * e, axis=1, keepdims=True), axis=0, keepdims=True) * (0.5 / D) \
                    + jnp.zeros((1, LANES), F32)
                ho_ref[...] = dh

                @pl.when(i == 0)
                def _():
                    dg_ref[...] = dg
                    loss_ref[...] = part

                @pl.when(i != 0)
                def _():
                    dg_ref[...] += dg
                    loss_ref[...] += part

        if extra:
            pl.when((i == nt - 1) & (j == nf - 1))(finish)

    row = lambda i, j: (i, 0)
    const = lambda i, j: (0, 0)
    head_in = [pl.BlockSpec((tm, D), row), pl.BlockSpec((1, D), const)] if head else []
    head_out = [pl.BlockSpec((1, D), const), pl.BlockSpec((1, LANES), const)] if head else []
    return pl.pallas_call(
        body, grid=(nt, nf),
        in_specs=[pl.BlockSpec((tm, D), row), pl.BlockSpec((None, 3, D), lambda i, j: (i // tpb, 0, 0)),
                  pl.BlockSpec((1, D), const)] + [_ffn_weight_spec(w, k) for k, w in enumerate((w1, w3, w2))]
        + head_in + [HBM_SPEC] * ne,
        out_specs=[pl.BlockSpec((tm, D), row), pl.BlockSpec((tm, D), row), pl.BlockSpec((tm, D), row),
                   pl.BlockSpec((tm, tf), lambda i, j: (i, j)), pl.BlockSpec((tm, tf), lambda i, j: (i, j))]
        + head_out + [HBM_SPEC] * ne,
        out_shape=[SDS((t_rows, D), F32), SDS((t_rows, D), F32), SDS((t_rows, D), BF16), SDS((t_rows, FF), BF16),
                   SDS((t_rows, FF), BF16)] + ([SDS((1, D), F32), SDS((1, LANES), F32)] if head else [])
        + [SDS((NDEV,) + x.shape, x.dtype) for x in extra],
        scratch_shapes=[pltpu.VMEM((tm, D), F32)] + (_comm_scratch() if extra else []), name=name,
        compiler_params=_cp(2))(h, mod3, g, w1, w3, w2, *head, *extra)


def ffn_bwd(name, dho, h, f_out, h1_in, h3_in, mod3, g, w1, w3, w2, seq, exchange=None):
    t_rows = h.shape[0]
    tm = _pick(seq, (FFN_BWD_TM, 128, 64))
    tf = FFN_TF
    tpb = seq // tm
    nf = FF // tf
    nt = t_rows // tm
    extra = [] if exchange is None else [exchange]

    def body(*refs):
        dho_ref, h_ref, f_ref, h1_ref, h3_ref, mod_ref, g_ref, w1_ref, w3_ref, w2_ref = refs[:10]
        dh_ref, a_ref, dh1_ref, dh3_ref, df_scr, dmod_ref, dg_ref = refs[10 + len(extra):17 + len(extra)]
        du_acc = refs[17 + 2 * len(extra)]
        i, j = pl.program_id(0), pl.program_id(1)
        if extra:
            start, finish = _exchange_phases(refs[10], refs[18], *refs[20:23])
            pl.when((i == 0) & (j == 0))(start)

        @pl.when(j == 0)
        def _():
            df_scr[...] = (0.5 * mod_ref[2:3, :] * dho_ref[...]).astype(BF16)
            du_acc[...] = jnp.zeros_like(du_acc)

        h1 = h1_ref[...].astype(F32)
        h3 = h3_ref[...].astype(F32)
        sg = jax.nn.sigmoid(h1)
        s = h1 * sg
        da = _dot_nt(df_scr[...], _ffn_weight(w2_ref))
        dh3 = (da * s).astype(BF16)
        dh1 = (da * h3 * (sg * (1.0 + h1 * (1.0 - sg)))).astype(BF16)
        a_ref[...] = (s * h3).astype(BF16)
        dh1_ref[...] = dh1
        dh3_ref[...] = dh3
        du_acc[...] += _dot(dh1, _ffn_weight(w1_ref)) + _dot(dh3, _ffn_weight(w3_ref))

        @pl.when(j == nf - 1)
        def _():
            _, vjp = jax.vjp(normmod, h_ref[...], g_ref[...], mod_ref[1:2, :], mod_ref[0:1, :])
            dh_n, dg, dsc, dsh = vjp(du_acc[...])
            dh_ref[...] = dho_ref[...] + dh_n
            dgt = jnp.sum(0.5 * dho_ref[...] * f_ref[...], axis=0, keepdims=True)
            dmod = jnp.concatenate([dsh, dsc, dgt], axis=0)

            @pl.when(i % tpb == 0)
            def _():
                dmod_ref[...] = dmod

            @pl.when(i % tpb != 0)
            def _():
                dmod_ref[...] += dmod

            @pl.when(i == 0)
            def _():
                dg_ref[...] = dg

            @pl.when(i != 0)
            def _():
                dg_ref[...] += dg

        if extra:
            pl.when((i == nt - 1) & (j == nf - 1))(finish)

    row = lambda i, j: (i, 0)
    col = lambda i, j: (i, j)
    return pl.pallas_call(
        body, grid=(nt, nf),
        in_specs=[pl.BlockSpec((tm, D), row), pl.BlockSpec((tm, D), row), pl.BlockSpec((tm, D), row),
                  pl.BlockSpec((tm, tf), col), pl.BlockSpec((tm, tf), col),
                  pl.BlockSpec((None, 3, D), lambda i, j: (i // tpb, 0, 0)),
                  pl.BlockSpec((1, D), lambda i, j: (0, 0))] + [_ffn_weight_spec(w, k) for k, w in enumerate((w1, w3, w2))]
        + [HBM_SPEC] * len(extra),
        out_specs=[pl.BlockSpec((tm, D), row), pl.BlockSpec((tm, tf), col), pl.BlockSpec((tm, tf), col),
                   pl.BlockSpec((tm, tf), col), pl.BlockSpec((tm, D), row),
                   pl.BlockSpec((None, 3, D), lambda i, j: (i // tpb, 0, 0)), pl.BlockSpec((1, D), lambda i, j: (0, 0))]
        + [HBM_SPEC] * len(extra),
        out_shape=[SDS((t_rows, D), F32), SDS((t_rows, FF), BF16), SDS((t_rows, FF), BF16), SDS((t_rows, FF), BF16),
                   SDS((t_rows, D), BF16), SDS(mod3.shape, F32), SDS((1, D), F32)] + [SDS(x.shape, x.dtype) for x in extra],
        scratch_shapes=[pltpu.VMEM((tm, D), F32)] + (_comm_scratch() if extra else []), name=name,
        compiler_params=_cp(2))(dho, h, f_out, h1_in, h3_in, mod3, g, w1, w3, w2, *extra)


def _resident(shape):
    return pl.BlockSpec(shape, lambda i: (0,) * len(shape), pipeline_mode=pl.Buffered(1))


def mix_in_fwd(h, sh, sc, g, ws, seq):
    t_rows = h.shape[0]
    tm = _pick(seq, (256, 128, 64))
    tpb = seq // tm
    nw = len(ws)

    def body(h_ref, sh_ref, sc_ref, g_ref, *rest):
        u = normmod(h_ref[...], g_ref[...], sc_ref[...], sh_ref[...]).astype(BF16)
        rest[nw][...] = u
        for w_ref, p_ref in zip(rest[:nw], rest[nw + 1:]):
            p_ref[...] = _dot_nt(u, w_ref[...])

    row = lambda i: (i, 0)
    batch = pl.BlockSpec((None, 1, D), lambda i: (i // tpb, 0, 0))
    return pl.pallas_call(
        body, grid=(t_rows // tm,),
        in_specs=[pl.BlockSpec((tm, D), row), batch, batch, pl.BlockSpec((1, D), lambda i: (0, 0))]
        + [_resident(w.shape) for w in ws],
        out_specs=[pl.BlockSpec((tm, D), row)] + [pl.BlockSpec((tm, w.shape[0]), row) for w in ws],
        out_shape=[SDS((t_rows, D), BF16)] + [SDS((t_rows, w.shape[0]), F32) for w in ws], name="mix_in_fwd",
        compiler_params=_cp(1))(h, sh, sc, g, *ws)


def mix_in_bwd(dps, ws, h, sh, sc, g, dh_add, seq):
    t_rows = h.shape[0]
    tm = _pick(seq, (256, 128, 64))
    tpb = seq // tm
    nw = len(ws)

    def body(*refs):
        h_ref, sh_ref, sc_ref, g_ref, add_ref, dh_ref, dsh_ref, dsc_ref, dg_ref = refs[2 * nw:]
        i = pl.program_id(0)
        du = _dot(refs[0][...], refs[nw][...])
        for k in range(1, nw):
            du = du + _dot(refs[k][...], refs[nw + k][...])
        _, vjp = jax.vjp(normmod, h_ref[...], g_ref[...], sc_ref[...], sh_ref[...])
        dh_n, dg, dsc, dsh = vjp(du)
        dh_ref[...] = add_ref[...] + dh_n

        @pl.when(i % tpb == 0)
        def _():
            dsh_ref[...] = dsh
            dsc_ref[...] = dsc

        @pl.when(i % tpb != 0)
        def _():
            dsh_ref[...] += dsh
            dsc_ref[...] += dsc

        @pl.when(i == 0)
        def _():
            dg_ref[...] = dg

        @pl.when(i != 0)
        def _():
            dg_ref[...] += dg

    row = lambda i: (i, 0)
    batch = pl.BlockSpec((None, 1, D), lambda i: (i // tpb, 0, 0))
    gain = pl.BlockSpec((1, D), lambda i: (0, 0))
    return pl.pallas_call(
        body, grid=(t_rows // tm,),
        in_specs=[pl.BlockSpec((tm, dp.shape[1]), row) for dp in dps] + [_resident(w.shape) for w in ws]
        + [pl.BlockSpec((tm, D), row), batch, batch, gain, pl.BlockSpec((tm, D), row)],
        out_specs=[pl.BlockSpec((tm, D), row), batch, batch, gain],
        out_shape=[SDS((t_rows, D), F32), SDS(sh.shape, F32), SDS(sc.shape, F32), SDS((1, D), F32)], name="mix_in_bwd",
        compiler_params=_cp(1))(*dps, *ws, h, sh, sc, g, dh_add)


def mix_out_fwd(merged, w_out, h_prev, gt, seq):
    t_rows = merged.shape[0]
    tm = _pick(seq, (256, 128, 64))
    tpb = seq // tm

    def body(m_ref, w_ref, h_ref, gt_ref, mo_ref, ho_ref):
        mo = _dot(m_ref[...], w_ref[...])
        mo_ref[...] = mo
        ho_ref[...] = h_ref[...] + gt_ref[...] * mo

    row = lambda i: (i, 0)
    return pl.pallas_call(
        body, grid=(t_rows // tm,),
        in_specs=[pl.BlockSpec((tm, D), row), _resident(w_out.shape), pl.BlockSpec((tm, D), row),
                  pl.BlockSpec((None, 1, D), lambda i: (i // tpb, 0, 0))],
        out_specs=[pl.BlockSpec((tm, D), row), pl.BlockSpec((tm, D), row)],
        out_shape=[SDS((t_rows, D), F32), SDS((t_rows, D), F32)], name="mix_out_fwd",
        compiler_params=_cp(1))(merged, w_out, h_prev, gt)


def mix_out_bwd(dh, mo, w_out, gt, seq):
    t_rows = dh.shape[0]
    tm = _pick(seq, (256, 128, 64))
    tpb = seq // tm

    def body(dh_ref, mo_ref, w_ref, gt_ref, dmo_ref, dm_ref, dgt_ref):
        i = pl.program_id(0)
        dmo = (gt_ref[...] * dh_ref[...]).astype(BF16)
        dmo_ref[...] = dmo
        dm_ref[...] = _dot_nt(dmo, w_ref[...])
        dgt = jnp.sum(dh_ref[...] * mo_ref[...], axis=0, keepdims=True)

        @pl.when(i % tpb == 0)
        def _():
            dgt_ref[...] = dgt

        @pl.when(i % tpb != 0)
        def _():
            dgt_ref[...] += dgt

    row = lambda i: (i, 0)
    batch = pl.BlockSpec((None, 1, D), lambda i: (i // tpb, 0, 0))
    return pl.pallas_call(
        body, grid=(t_rows // tm,),
        in_specs=[pl.BlockSpec((tm, D), row), pl.BlockSpec((tm, D), row), _resident(w_out.shape), batch],
        out_specs=[pl.BlockSpec((tm, D), row), pl.BlockSpec((tm, D), row), batch],
        out_shape=[SDS((t_rows, D), BF16), SDS((t_rows, D), F32), SDS(gt.shape, F32)], name="mix_out_bwd",
        compiler_params=_cp(1))(dh, mo, w_out, gt)


def _dn_cols(part, hd):
    return slice(part * DNW + hd * DH, part * DNW + (hd + 1) * DH)


def _qkv_stacks(qkv_ref, nb):
    pairs = [(b, hd) for b in range(nb) for hd in range(NH)]
    return [jnp.stack([qkv_ref[b, :, _dn_cols(part, hd)] for b, hd in pairs]) for part in range(3)]


def dn_prep_fwd(p_dn, conv8):
    bl, seq, _ = p_dn.shape
    tp = _pick(seq, (256, 128, 64))

    def body(raw_ref, halo_ref, conv_ref, o_ref):
        hm = (pl.program_id(1) > 0).astype(F32)
        o_ref[...] = dn_prep(jnp.concatenate([halo_ref[...] * hm, raw_ref[...]], axis=0), conv_ref[...])

    return pl.pallas_call(
        body, grid=(bl, seq // tp),
        in_specs=[pl.BlockSpec((None, tp, 3 * DNW), lambda b, i: (b, i, 0)),
                  pl.BlockSpec((None, 8, 3 * DNW), lambda b, i: (b, jnp.maximum(i * (tp // 8) - 1, 0), 0)),
                  pl.BlockSpec((8, 3 * DNW), lambda b, i: (0, 0))],
        out_specs=pl.BlockSpec((None, tp, 3 * DNW), lambda b, i: (b, i, 0)),
        out_shape=SDS((bl, seq, 3 * DNW), F32), name="dn_prep_fwd", compiler_params=_cp(2))(p_dn, p_dn, conv8)


def dn_prep_bwd(p_dn, conv8, d_qkv, d_z):
    bl, seq, _ = p_dn.shape
    tp = _pick(seq, (256, 128, 64))
    nt = seq // tp

    def body(raw_ref, halo_ref, conv_ref, dq_ref, dz_ref, draw_ref, dconv_ref, carry):
        b, r = pl.program_id(0), pl.program_id(1)

        @pl.when((b == 0) & (r == 0))
        def _():
            dconv_ref[...] = jnp.zeros_like(dconv_ref)

        @pl.when(r == 0)
        def _():
            carry[...] = jnp.zeros_like(carry)

        hm = (r < nt - 1).astype(F32)
        _, vjp = jax.vjp(dn_prep, jnp.concatenate([halo_ref[...] * hm, raw_ref[...]], axis=0), conv_ref[...])
        dxc, dw = vjp(dq_ref[...])
        tail = dxc[tp:tp + 8] + carry[...]
        draw_ref[:, 0:3 * DNW] = jnp.concatenate([dxc[8:tp], tail], axis=0).astype(BF16)
        draw_ref[:, 3 * DNW:4 * DNW] = dz_ref[...].astype(BF16)
        carry[...] = dxc[0:8] * hm
        dconv_ref[...] += dw

    blk = lambda b, r: (b, nt - 1 - r, 0)
    return pl.pallas_call(
        body, grid=(bl, nt),
        in_specs=[pl.BlockSpec((None, tp, 3 * DNW), blk),
                  pl.BlockSpec((None, 8, 3 * DNW), lambda b, r: (b, jnp.maximum((nt - 1 - r) * (tp // 8) - 1, 0), 0)),
                  pl.BlockSpec((8, 3 * DNW), lambda b, r: (0, 0)), pl.BlockSpec((None, tp, 3 * DNW), blk),
                  pl.BlockSpec((None, tp, DNW), blk)],
        out_specs=[pl.BlockSpec((None, tp, 4 * DNW), blk), pl.BlockSpec((8, 3 * DNW), lambda b, r: (0, 0))],
        out_shape=[SDS((bl, seq, 4 * DNW), BF16), SDS((8, 3 * DNW), F32)],
        scratch_shapes=[pltpu.VMEM((8, 3 * DNW), F32)], name="dn_prep_bwd", compiler_params=_cp(2))(p_dn, p_dn, conv8, d_qkv, d_z)


def _gate_stacks(gates, nb):
    pairs = [(b, hd) for b in range(nb) for hd in range(NH)]
    bs = jnp.stack([gates[b][0][:, hd:hd + 1] for b, hd in pairs])
    gs = jnp.stack([gates[b][1][:, NH + hd:NH + hd + 1] for b, hd in pairs])
    gts = jnp.stack([gates[b][2][NH + hd:NH + hd + 1, :] for b, hd in pairs])
    return bs, gs, gts


def deltanet_fwd(qkv, p_small, alp, dtp, nb, gather=None):
    bl, seq, _ = qkv.shape
    nc = seq // CH
    ng = nb * NH
    extra = [] if gather is None else [gather]

    def body(*refs):
        qkv_ref, small_ref, alp_ref, dtp_ref = refs[:4]
        o_ref, sprev_ref, tinv_ref = refs[4 + len(extra):7 + len(extra)]
        s_scr = refs[7 + 2 * len(extra)]
        bb, n = pl.program_id(0), pl.program_id(1)
        if extra:
            start, forward, finish = _gather_phases(refs[4], refs[8], *refs[10:13])
            pl.when((bb == 0) & (n == 0))(start)

        @pl.when(n == 0)
        def _():
            s_scr[...] = jnp.zeros_like(s_scr)

        gates = [gate_fn(small_ref[b], alp_ref[...], dtp_ref[...]) for b in range(nb)]
        s_prev = s_scr[...]
        o, s_new, tinv = dn_chunk(*_qkv_stacks(qkv_ref, nb), *_gate_stacks(gates, nb), s_prev)
        sprev_ref[...] = s_prev
        tinv_ref[...] = tinv
        s_scr[...] = s_new
        for b in range(nb):
            for hd in range(NH):
                o_ref[b, :, hd * DH:(hd + 1) * DH] = o[b * NH + hd]
        if extra:
            pl.when((bb == bl // nb - 1) & (n == nc // 2))(forward)
            pl.when((bb == bl // nb - 1) & (n == nc - 1))(finish)

    blk = lambda bb, n: (bb, n, 0)
    const = lambda bb, n: (0, 0)
    saved = pl.BlockSpec((None, ng, DH, DH), lambda bb, n: (bb * nc + n, 0, 0, 0))
    return pl.pallas_call(
        body, grid=(bl // nb, nc),
        in_specs=[pl.BlockSpec((nb, CH, 3 * DNW), blk), pl.BlockSpec((nb, CH, LANES), blk),
                  pl.BlockSpec((1, LANES), const), pl.BlockSpec((1, LANES), const)] + [HBM_SPEC] * len(extra),
        out_specs=[pl.BlockSpec((nb, CH, DNW), blk), saved, saved] + [HBM_SPEC] * len(extra),
        out_shape=[SDS((bl, seq, DNW), F32), SDS((bl // nb * nc, ng, DH, DH), F32), SDS((bl // nb * nc, ng, DH, DH), F32)]
        + [SDS((NDEV,) + x.shape, x.dtype) for x in extra],
        scratch_shapes=[pltpu.VMEM((ng, DH, DH), F32)] + (_comm_scratch() if extra else []), name="deltanet_fwd",
        compiler_params=_cp(2))(qkv, p_small, alp, dtp, *extra)


def deltanet_bwd(qkv, p_small, alp, dtp, sprev, tinv, d_o, nb, exchange=None):
    bl, seq, _ = qkv.shape
    nc = seq // CH
    ng = nb * NH
    extra = [] if exchange is None else [exchange]

    def body(*refs):
        qkv_ref, small_ref, alp_ref, dtp_ref, sprev_ref, tinv_ref, do_ref = refs[:7]
        dqkv_ref, dsmall_ref, dalp_ref, ddtp_ref = refs[7 + len(extra):11 + len(extra)]
        ds_scr = refs[11 + 2 * len(extra)]
        bb, r = pl.program_id(0), pl.program_id(1)
        if extra:
            start, finish = _exchange_phases(refs[7], refs[12], *refs[14:17])
            pl.when((bb == 0) & (r == 0))(start)

        @pl.when((bb == 0) & (r == 0))
        def _():
            dalp_ref[...] = jnp.zeros_like(dalp_ref)
            ddtp_ref[...] = jnp.zeros_like(ddtp_ref)

        @pl.when(r == 0)
        def _():
            ds_scr[...] = jnp.zeros_like(ds_scr)

        gates, gate_vjps = [], []
        for b in range(nb):
            out, gvjp = jax.vjp(gate_fn, small_ref[b], alp_ref[...], dtp_ref[...])
            gates.append(out)
            gate_vjps.append(gvjp)
        t_saved = tinv_ref[...]
        _, vjp = jax.vjp(lambda *args: dn_chunk(*args, t_saved)[:2], *_qkv_stacks(qkv_ref, nb), *_gate_stacks(gates, nb),
                         sprev_ref[...])
        d_out = jnp.stack([do_ref[b, :, hd * DH:(hd + 1) * DH] for b in range(nb) for hd in range(NH)])
        grads = vjp((d_out, ds_scr[...]))
        ds_scr[...] = grads[6]
        lane = _iota2((CH, LANES), 1)
        rowi = _iota2((LANES, CH), 0)
        for b in range(nb):
            d_beta = jnp.zeros((CH, LANES), F32)
            d_gc = jnp.zeros((CH, LANES), F32)
            d_gct = jnp.zeros((LANES, CH), F32)
            for hd in range(NH):
                i = b * NH + hd
                for part in range(3):
                    dqkv_ref[b, :, _dn_cols(part, hd)] = grads[part][i]
                d_beta = d_beta + jnp.where(lane == hd, grads[3][i], 0.0)
                d_gc = d_gc + jnp.where(lane == NH + hd, grads[4][i], 0.0)
                d_gct = d_gct + jnp.where(rowi == NH + hd, grads[5][i], 0.0)
            d_small, d_alp, d_dtp = gate_vjps[b]((d_beta, d_gc, d_gct))
            dsmall_ref[b] = d_small.astype(BF16)
            dalp_ref[...] += d_alp
            ddtp_ref[...] += d_dtp
        if extra:
            pl.when((bb == bl // nb - 1) & (r == nc - 1))(finish)

    blk = lambda bb, r: (bb, nc - 1 - r, 0)
    const = lambda bb, r: (0, 0)
    saved = pl.BlockSpec((None, ng, DH, DH), lambda bb, r: (bb * nc + nc - 1 - r, 0, 0, 0))
    return pl.pallas_call(
        body, grid=(bl // nb, nc),
        in_specs=[pl.BlockSpec((nb, CH, 3 * DNW), blk), pl.BlockSpec((nb, CH, LANES), blk), pl.BlockSpec((1, LANES), const),
                  pl.BlockSpec((1, LANES), const), saved, saved, pl.BlockSpec((nb, CH, DNW), blk)] + [HBM_SPEC] * len(extra),
        out_specs=[pl.BlockSpec((nb, CH, 3 * DNW), blk), pl.BlockSpec((nb, CH, LANES), blk), pl.BlockSpec((1, LANES), const),
                   pl.BlockSpec((1, LANES), const)] + [HBM_SPEC] * len(extra),
        out_shape=[SDS((bl, seq, 3 * DNW), F32), SDS((bl, seq, LANES), BF16), SDS((1, LANES), F32), SDS((1, LANES), F32)]
        + [SDS(x.shape, x.dtype) for x in extra],
        scratch_shapes=[pltpu.VMEM((ng, DH, DH), F32)] + (_comm_scratch() if extra else []), name="deltanet_bwd",
        compiler_params=_cp(2))(qkv, p_small, alp, dtp, sprev, tinv, d_o, *extra)


def _s5_table_specs():
    tab3 = pl.BlockSpec((None, LANES, 512), lambda gb, n: (gb, 0, 0))
    tab2 = pl.BlockSpec((S5_CH, 512), lambda gb, n: (0, gb))
    return [tab3] * 4 + [tab2] * 6 + [pl.BlockSpec((1, LANES), lambda gb, n: (0, gb))]


def s5_fwd(u, tables, dsk):
    bl, seq, _ = u.shape
    nc = seq // S5_CH

    def body(u_ref, *rest):
        tabs, (y_ref, xs_ref, xr_scr, xi_scr) = rest[:11], rest[11:]

        @pl.when(pl.program_id(1) == 0)
        def _():
            xr_scr[...] = jnp.zeros_like(xr_scr)
            xi_scr[...] = jnp.zeros_like(xi_scr)

        xp_re, xp_im = xr_scr[...], xi_scr[...]
        xs_ref[0:bl] = xp_re
        xs_ref[bl:2 * bl] = xp_im
        y, xn_re, xn_im = s5_chunk(u_ref[...], xp_re, xp_im, *[t[...] for t in tabs])
        y_ref[...] = y
        xr_scr[...] = xn_re
        xi_scr[...] = xn_im

    blk = lambda gb, n: (0, n, gb)
    return pl.pallas_call(
        body, grid=(GB, nc), in_specs=[pl.BlockSpec((bl, S5_CH, LANES), blk)] + _s5_table_specs(),
        out_specs=[pl.BlockSpec((bl, S5_CH, LANES), blk),
                   pl.BlockSpec((None, 2 * bl, 1, 512), lambda gb, n: (gb * nc + n, 0, 0, 0))],
        out_shape=[SDS((bl, seq, S5W), F32), SDS((GB * nc, 2 * bl, 1, 512), F32)],
        scratch_shapes=[pltpu.VMEM((bl, 1, 512), F32), pltpu.VMEM((bl, 1, 512), F32)], name="s5_fwd",
        compiler_params=_cp(2))(u, *tables, dsk)


def s5_bwd(u, tables, dsk, xs, dy):
    bl, seq, _ = u.shape
    nc = seq // S5_CH

    def body(u_ref, *rest):
        tabs, xs_ref, dy_ref = rest[:11], rest[11], rest[12]
        du_ref, dtabs, dxr_scr, dxi_scr = rest[13], rest[14:25], rest[25], rest[26]
        r = pl.program_id(1)

        @pl.when(r == 0)
        def _():
            for t in dtabs:
                t[...] = jnp.zeros_like(t)
            dxr_scr[...] = jnp.zeros_like(dxr_scr)
            dxi_scr[...] = jnp.zeros_like(dxi_scr)

        _, vjp = jax.vjp(s5_chunk, u_ref[...], xs_ref[0:bl], xs_ref[bl:2 * bl], *[t[...] for t in tabs])
        grads = vjp((dy_ref[...], dxr_scr[...], dxi_scr[...]))
        du_ref[...] = grads[0].astype(BF16)
        dxr_scr[...] = grads[1]
        dxi_scr[...] = grads[2]
        for t, g in zip(dtabs, grads[3:]):
            t[...] += g

    blk = lambda gb, r: (0, nc - 1 - r, gb)
    tab_shapes = [SDS(t.shape, F32) for t in tables] + [SDS(dsk.shape, F32)]
    return pl.pallas_call(
        body, grid=(GB, nc),
        in_specs=[pl.BlockSpec((bl, S5_CH, LANES), blk)] + _s5_table_specs()
        + [pl.BlockSpec((None, 2 * bl, 1, 512), lambda gb, r: (gb * nc + nc - 1 - r, 0, 0, 0)), pl.BlockSpec((bl, S5_CH, LANES), blk)],
        out_specs=[pl.BlockSpec((bl, S5_CH, LANES), blk)] + _s5_table_specs(),
        out_shape=[SDS((bl, seq, S5W), BF16)] + tab_shapes,
        scratch_shapes=[pltpu.VMEM((bl, 1, 512), F32), pltpu.VMEM((bl, 1, 512), F32)], name="s5_bwd",
        compiler_params=_cp(2))(u, *tables, dsk, xs, dy)


def s5_tables_fwd(params):
    shapes = [SDS((GB, LANES, 512), F32)] * 4 + [SDS((S5_CH, S5N), F32)] * 6

    def body(*refs):
        for r, t in zip(refs[7:], s5_tables(*[p[...] for p in refs[:7]])):
            r[...] = t

    return pl.pallas_call(body, out_shape=shapes, name="s5_tables_fwd", compiler_params=_cp())(*params)


def s5_tables_bwd(params, dtables):
    def body(*refs):
        _, vjp = jax.vjp(s5_tables, *[p[...] for p in refs[:7]])
        for r, g in zip(refs[17:], vjp(tuple(t[...] for t in refs[7:17]))):
            r[...] = g

    return pl.pallas_call(body, out_shape=[SDS(p.shape, F32) for p in params], name="s5_tables_bwd",
                          compiler_params=_cp())(*params, *dtables)


def ada_fwd(c_all, w_loc, b_loc):
    def body(c_ref, w_ref, b_ref, o_ref):
        o_ref[...] = _dot(_silu(c_ref[...]), w_ref[...]) + b_ref[...]

    return pl.pallas_call(body, out_shape=SDS((c_all.shape[0], w_loc.shape[1]), F32), name="ada_fwd",
                          compiler_params=_cp())(c_all, w_loc, b_loc)


def ada_bwd(c_all, dmod_mine, dmod_all):
    def body(c_ref, dm_ref, da_ref, gw_ref, gb_ref):
        gw_ref[...] = _dot_tn(_silu(c_ref[...]), dm_ref[...])
        gb_ref[...] = jnp.sum(da_ref[...], axis=0, keepdims=True)

    return pl.pallas_call(body, out_shape=[SDS((D, dmod_mine.shape[1]), F32), SDS((1, dmod_all.shape[1]), F32)],
                          name="ada_bwd", compiler_params=_cp())(c_all, dmod_mine, dmod_all)


def adamw(name, parts, w, m, v):
    k_parts, rows, cols = parts.shape
    tr = _pick(rows, (256, 128, 64, 32, 16, 8))

    def body(p_ref, w_ref, m_ref, v_ref, g_ref, d_ref, mo_ref, vo_ref):
        g = p_ref[0].astype(F32)
        for k in range(1, k_parts):
            g = g + p_ref[k].astype(F32)
        _adam_store(g, w_ref, m_ref, v_ref, g_ref, d_ref, mo_ref, vo_ref)

    blk = pl.BlockSpec((tr, cols), lambda i: (i, 0))
    return pl.pallas_call(
        body, grid=(rows // tr,), in_specs=[pl.BlockSpec((k_parts, tr, cols), lambda i: (0, i, 0)), blk, blk, blk],
        out_specs=[blk] * 4, out_shape=[SDS((rows, cols), F32)] * 4, name=name, compiler_params=_cp(1))(parts, w, m, v)


def _adam_store(g, w_ref, m_ref, v_ref, g_ref, d_ref, mo_ref, vo_ref):
    m_new = ADAM_B1 * m_ref[...] + (1.0 - ADAM_B1) * g
    v_new = ADAM_B2 * v_ref[...] + (1.0 - ADAM_B2) * (g * g)
    m_hat = m_new / (1.0 - ADAM_B1 ** ADAM_STEP)
    v_hat = v_new / (1.0 - ADAM_B2 ** ADAM_STEP)
    g_ref[...] = g
    d_ref[...] = -ADAM_LR * (m_hat / (jnp.sqrt(v_hat) + ADAM_EPS) + ADAM_WD * w_ref[...])
    mo_ref[...] = m_new
    vo_ref[...] = v_new


def adamw_t(name, parts, w, m, v):
    k_parts, r, c = parts.shape
    tc = _pick(c, (256, 128))

    def body(p_ref, w_ref, m_ref, v_ref, g_ref, d_ref, mo_ref, vo_ref):
        gt = p_ref[0].astype(F32)
        for k in range(1, k_parts):
            gt = gt + p_ref[k].astype(F32)
        _adam_store(gt.T, w_ref, m_ref, v_ref, g_ref, d_ref, mo_ref, vo_ref)

    blk = pl.BlockSpec((tc, r), lambda j: (j, 0))
    return pl.pallas_call(
        body, grid=(c // tc,), in_specs=[pl.BlockSpec((k_parts, r, tc), lambda j: (0, 0, j)), blk, blk, blk],
        out_specs=[blk] * 4, out_shape=[SDS((c, r), F32)] * 4, name=name, compiler_params=_cp(1))(parts, w, m, v)


def _comm_scratch():
    return [pltpu.SemaphoreType.DMA((7,)), pltpu.SemaphoreType.DMA((7,)), pltpu.SemaphoreType.DMA]


HBM_SPEC = pl.BlockSpec(memory_space=pl.ANY)


def _gather_phases(x_ref, out_ref, send_sems, recv_sems, local_sem):
    mx, my, mc = lax.axis_index("x"), lax.axis_index("y"), lax.axis_index("c")
    me, sibling = (mx, my, mc), (mx, my, 1 - mc)
    chips = [(1 - mx, my), (mx, 1 - my), (1 - mx, 1 - my)]

    def slot(px, py, pc):
        return out_ref.at[4 * px + 2 * py + pc]

    def copy(k, block, to, src=None):
        return pltpu.make_async_remote_copy(
            src_ref=slot(*block) if src is None else src, dst_ref=slot(*block), send_sem=send_sems.at[k],
            recv_sem=recv_sems.at[k], device_id=to, device_id_type=pl.DeviceIdType.MESH)

    def first():
        return [copy(0, me, sibling, src=x_ref)] + [copy(1 + j, me, (*chip, mc), src=x_ref) for j, chip in enumerate(chips)]

    def passed():
        return [copy(4 + j, (*chip, mc), sibling) for j, chip in enumerate(chips)]

    def start():
        pltpu.make_async_copy(x_ref, slot(*me), local_sem).start()
        for cp in first():
            cp.start()

    def forward():
        for j, chip in enumerate(chips):
            copy(1 + j, (*chip, mc), me).wait_recv()
            passed()[j].start()

    def finish():
        copy(0, sibling, me).wait_recv()
        for j, chip in enumerate(chips):
            copy(4 + j, (*chip, 1 - mc), me).wait_recv()
        for cp in first() + passed():
            cp.wait_send()
        pltpu.make_async_copy(x_ref, slot(*me), local_sem).wait()

    return start, forward, finish


def _exchange_phases(x_ref, out_ref, send_sems, recv_sems, local_sem):
    mx, my, mc = lax.axis_index("x"), lax.axis_index("y"), lax.axis_index("c")
    me = 4 * mx + 2 * my + mc

    def peer(k):
        return mx ^ (k >> 2), my ^ ((k >> 1) & 1), mc ^ (k & 1)

    def sends():
        out = []
        for k in range(1, NDEV):
            px, py, pc = peer(k)
            out.append(pltpu.make_async_remote_copy(
                src_ref=x_ref.at[4 * px + 2 * py + pc], dst_ref=out_ref.at[me], send_sem=send_sems.at[k - 1],
                recv_sem=recv_sems.at[k - 1], device_id=(px, py, pc), device_id_type=pl.DeviceIdType.MESH))
        return out

    def start():
        pltpu.make_async_copy(x_ref.at[me], out_ref.at[me], local_sem).start()
        for cp in sends():
            cp.start()

    def finish():
        for k in range(1, NDEV):
            px, py, pc = peer(k)
            pltpu.make_async_remote_copy(
                src_ref=x_ref.at[me], dst_ref=out_ref.at[4 * px + 2 * py + pc], send_sem=send_sems.at[k - 1],
                recv_sem=recv_sems.at[k - 1], device_id=(px, py, pc), device_id_type=pl.DeviceIdType.MESH).wait_recv()
        for cp in sends():
            cp.wait_send()
        pltpu.make_async_copy(x_ref.at[me], out_ref.at[me], local_sem).wait()

    return start, finish


def all_gather(name, x):
    def body(x_ref, out_ref, send_sems, recv_sems, local_sem):
        for phase in _gather_phases(x_ref, out_ref, send_sems, recv_sems, local_sem):
            phase()

    return pl.pallas_call(body, out_shape=SDS((NDEV,) + x.shape, x.dtype), in_specs=[HBM_SPEC], out_specs=HBM_SPEC,
                          scratch_shapes=_comm_scratch(), name=name)(x)


def all_gather_pair(name, x1, x2):
    def body(x1_ref, x2_ref, o1_ref, o2_ref, *sems):
        first = _gather_phases(x1_ref, o1_ref, *sems[:3])
        second = _gather_phases(x2_ref, o2_ref, *sems[3:])
        for phase1, phase2 in zip(first, second):
            phase1()
            phase2()

    return pl.pallas_call(
        body, out_shape=[SDS((NDEV,) + x1.shape, x1.dtype), SDS((NDEV,) + x2.shape, x2.dtype)], in_specs=[HBM_SPEC] * 2,
        out_specs=[HBM_SPEC] * 2, scratch_shapes=_comm_scratch() + _comm_scratch(), name=name)(x1, x2)


def all_to_all(name, x):
    def body(x_ref, out_ref, send_sems, recv_sems, local_sem):
        for phase in _exchange_phases(x_ref, out_ref, send_sems, recv_sems, local_sem):
            phase()

    return pl.pallas_call(body, out_shape=SDS(x.shape, x.dtype), in_specs=[HBM_SPEC], out_specs=HBM_SPEC,
                          scratch_shapes=_comm_scratch(), name=name)(x)


def _pack(arrs, dtype, row_mult=8):
    segs = []
    for a in arrs:
        flat = a.reshape(-1).astype(dtype)
        segs.append(jnp.pad(flat, (0, (-flat.shape[0]) % ROW)))
    flat = jnp.concatenate(segs)
    flat = jnp.pad(flat, (0, (-flat.shape[0]) % (ROW * row_mult)))
    return flat.reshape(-1, ROW)


def _unpack(buf, shapes):
    flat = buf.reshape(-1)
    out, off = [], 0
    for s in shapes:
        n = math.prod(s)
        out.append(flat[off:off + n].reshape(s))
        off += n + (-n) % ROW
    return out


def _pack_rows(arrs, axis):
    padded = []
    for t in arrs:
        pad = [(0, 0)] * t.ndim
        pad[axis] = (0, _tile_rows(t.shape[axis]) - t.shape[axis])
        padded.append(jnp.pad(t, pad))
    return jnp.concatenate(padded, axis=axis)


def _tile_rows(r):
    return r + (-r) % BF16_TILE_ROWS


def _unpack8(buf, shapes):
    flat = buf.reshape(NDEV, -1)
    out, off = [], 0
    for s in shapes:
        n = math.prod(s)
        out.append(flat[:, off:off + n].reshape((NDEV,) + tuple(s)))
        off += n + (-n) % ROW
    return out


def kernel(x, c, w_ada, b_ada, g_ffn1, w1_ffn1, w3_ffn1, w2_ffn1, g_mix, w_in, conv_qkv, a_log, dt_bias, g_onorm, lam_re, lam_im, log_step, b_re, b_im, c_re, c_im, d_skip, w_glu, b_glu, w_proj_a, w_proj_b, w_out, g_ffn2, w1_ffn2, w3_ffn2, w2_ffn2, g_final, loss_target, m_w_ada, m_b_ada, m_g_ffn1, m_w1_ffn1, m_w3_ffn1, m_w2_ffn1, m_g_mix, m_w_in, m_conv_qkv, m_a_log, m_dt_bias, m_g_onorm, m_lam_re, m_lam_im, m_log_step, m_b_re, m_b_im, m_c_re, m_c_im, m_d_skip, m_w_glu, m_b_glu, m_w_proj_a, m_w_proj_b, m_w_out, m_g_ffn2, m_w1_ffn2, m_w3_ffn2, m_w2_ffn2, m_g_final, v_w_ada, v_b_ada, v_g_ffn1, v_w1_ffn1, v_w3_ffn1, v_w2_ffn1, v_g_mix, v_w_in, v_conv_qkv, v_a_log, v_dt_bias, v_g_onorm, v_lam_re, v_lam_im, v_log_step, v_b_re, v_b_im, v_c_re, v_c_im, v_d_skip, v_w_glu, v_b_glu, v_w_proj_a, v_w_proj_b, v_w_out, v_g_ffn2, v_w1_ffn2, v_w3_ffn2, v_w2_ffn2, v_g_final):
    a = dict(locals())
    bl, seq, _ = x.shape
    t_rows = bl * seq
    me = 4 * lax.axis_index("x") + 2 * lax.axis_index("y") + lax.axis_index("c")
    tm_ew = _pick(seq, (256, 128, 64))

    loc = {n: (a[n][0].T if n in COL_SHARDED else a[n][0]) for n in RS_WEIGHTS}
    wfull, gw, res = {}, {}, {}

    def pack_local(names):
        return _pack_rows([loc[n].astype(BF16).reshape(-1, ROW) for n in names], 0)

    def unpack_full(buf, names):
        r0 = 0
        for n in names:
            r = loc[n].size // ROW
            wfull[n] = buf[:, r0:r0 + r, :].reshape(-1, loc[n].shape[1])
            r0 += _tile_rows(r)

    def pack_grads(names):
        return _pack_rows([gw[n].astype(BF16).reshape(NDEV, -1, ROW) for n in names], 1)

    def update(buf, names):
        r0 = 0
        for n in names:
            r = loc[n].size // ROW
            parts = buf[:, r0:r0 + r, :].reshape((NDEV,) + loc[n].shape)
            r0 += _tile_rows(r)
            step = adamw_t if n in COL_SHARDED else adamw
            out = step("adamw_" + n, parts, a[n][0], a["m_" + n][0], a["v_" + n][0])
            for kind, t in zip(("grad", "delta", "new_m", "new_v"), out):
                res[kind + "_" + n] = t[None]

    sm, wg_ffn1 = all_gather_pair("gather_inputs", _pack([c, conv_qkv[0]], F32), pack_local(G_FFN1))
    c_loc, conv_loc = _unpack8(sm, [c.shape, conv_qkv.shape[1:]])
    c_all = c_loc.reshape(NDEV * bl, D)
    conv_full = conv_loc.transpose(1, 0, 2).reshape(CONVW, 3 * DNW)

    n_ada = w_ada.shape[2]
    mod_part = ada_fwd(c_all, w_ada[0], lax.dynamic_slice(b_ada, (0, me * n_ada), (1, n_ada)))
    mod_all = all_gather("gather_mod", mod_part).transpose(1, 0, 2).reshape(NDEV * bl, 9 * D)
    mod = lax.dynamic_slice(mod_all, (me * bl, 0), (bl, 9 * D)).reshape(bl, 9, D)
    mods = [mod[:, k:k + 1, :] for k in range(9)]

    h0 = x.reshape(t_rows, D)
    h1, f1, u1, pa1, pb1, wg_rest = ffn_fwd("ffn1_fwd", h0, mod[:, 0:3, :], g_ffn1, wg_ffn1, wg_ffn1, wg_ffn1, seq,
                                            gather=pack_local(G_MIX))
    unpack_full(wg_rest, G_MIX)
    win = wfull['w_in']
    o_small, o_s5, o_gate = 4 * DNW, 4 * DNW + 2 * NH, 4 * DNW + 2 * NH + S5W
    w_dn, w_small = win[:o_small], jnp.pad(win[o_small:o_s5], ((0, LANES - 2 * NH), (0, 0)))
    w_s5, w_gate = win[o_s5:o_gate], win[o_gate:]
    w_pieces = [w_dn, w_small, w_s5, w_gate]
    u2, p_dn, p_small, p_s5, p_gate = mix_in_fwd(h1, mods[3], mods[4], g_mix, w_pieces, seq)

    conv8 = jnp.pad(conv_full, ((0, 8 - CONVW), (0, 0)))
    alp = jnp.pad(a_log, ((0, 0), (NH, LANES - 2 * NH)))
    dtp = jnp.pad(dt_bias, ((0, 0), (NH, LANES - 2 * NH)))
    nb_dn = DN_ROWS if bl % DN_ROWS == 0 else 1
    p_dn3, p_small3 = p_dn.reshape(bl, seq, 4 * DNW), p_small.reshape(bl, seq, LANES)
    qkv3 = dn_prep_fwd(p_dn3, conv8)
    o_pre3, sprev, tinv, wg_ffn2 = deltanet_fwd(qkv3, p_small3, alp, dtp, nb_dn, gather=pack_local(G_FFN2))
    o_pre = o_pre3.reshape(t_rows, DNW)
    z_raw = p_dn[:, 3 * DNW:]

    s5_params = [lam_re.reshape(1, S5N), lam_im.reshape(1, S5N), log_step,
                 b_re[0].transpose(2, 0, 1).reshape(S5C, S5N), b_im[0].transpose(2, 0, 1).reshape(S5C, S5N),
                 c_re[0].transpose(1, 0, 2).reshape(S5C, S5N), c_im[0].transpose(1, 0, 2).reshape(S5C, S5N)]
    tables = s5_tables_fwd(s5_params)
    p_s53 = p_s5.reshape(bl, seq, S5W)
    y_s53, xs = s5_fwd(p_s53, tables, d_skip)
    y_s5 = y_s53.reshape(t_rows, S5W)
    tail_in = [o_pre, z_raw, y_s5, p_gate]
    tail_w = [g_onorm, wfull['w_glu'], b_glu, wfull['w_proj_a'], wfull['w_proj_b']]
    (merged,) = ew_call("mix_tail", fn_mix_tail, tail_in, [], tail_w, [(D, BF16)], tm_ew, seq)
    mo, h2 = mix_out_fwd(merged, wfull['w_out'], h1, mods[5], seq)
    dh3, f3, u3, pa3, pb3, dg_final, loss_part = ffn_fwd(
        "ffn2_fwd", h2, mod[:, 6:9, :], g_ffn2, wg_ffn2, wg_ffn2, wg_ffn2, seq,
        loss_head=(loss_target.reshape(t_rows, D), g_final.reshape(1, D)))


    dh2, a3, d1_3, d3_3, df3, dmod_c, dg_ffn2 = ffn_bwd("ffn2_bwd", dh3, h2, f3, pa3, pb3, mod[:, 6:9, :], g_ffn2, wg_ffn2,
                                                   wg_ffn2, wg_ffn2, seq)
    gw['w1_ffn2'] = mm_tn("gw1_ffn2", d1_3, u3)
    gw['w3_ffn2'] = mm_tn("gw3_ffn2", d3_3, u3)
    gw['w2_ffn2'] = mm_tn("gw2_ffn2", a3, df3)

    dmo, d_merged, dgt2 = mix_out_bwd(dh2, mo, wfull['w_out'], mods[5], seq)
    gw['w_out'] = mm_tn("gw_out", merged, dmo)
    (d_opre, d_z, d_ys5, d_gate), _, tail_gw = ew_vjp_call(
        "mix_tail_bwd", fn_mix_tail, tail_in, [], tail_w, [d_merged], [(0, F32), (1, F32), (2, F32), (3, BF16)],
        _pick(seq, (512, 256, 128, 64)), seq)
    dg_onorm, gw['w_glu'], dg_bglu, gw['w_proj_a'], gw['w_proj_b'] = tail_gw
    d_qkv3, d_psmall3, d_alp, d_dtp, rs_ffn2 = deltanet_bwd(
        qkv3, p_small3, alp, dtp, sprev, tinv, d_opre.reshape(bl, seq, DNW), nb_dn, exchange=pack_grads(G_FFN2))
    d_pdn3, d_conv8 = dn_prep_bwd(p_dn3, conv8, d_qkv3, d_z.reshape(bl, seq, DNW))
    d_pdn, d_psmall = d_pdn3.reshape(t_rows, 4 * DNW), d_psmall3.reshape(t_rows, LANES)

    s5_out = s5_bwd(p_s53, tables, d_skip, xs, d_ys5.reshape(bl, seq, S5W))
    d_ps5, d_tables, dg_dskip = s5_out[0].reshape(t_rows, S5W), s5_out[1:11], s5_out[11]
    d_s5p = s5_tables_bwd(s5_params, d_tables)

    gw['w_in'] = jnp.concatenate([mm_tn("gw_dn", d_pdn, u2), mm_tn("gw_small", d_psmall, u2)[:2 * NH],
                                  mm_tn("gw_s5", d_ps5, u2), mm_tn("gw_gate", d_gate, u2)], axis=0)
    dh1, dsh2, dsc2, dg_mix = mix_in_bwd([d_pdn, d_psmall, d_ps5, d_gate], w_pieces, h1, mods[3], mods[4], g_mix, dh2, seq)

    dh0, a1, d1_1, d3_1, df1, dmod_a, dg_ffn1, rs_mix = ffn_bwd(
        "ffn1_bwd", dh1, h0, f1, pa1, pb1, mod[:, 0:3, :], g_ffn1, wg_ffn1, wg_ffn1, wg_ffn1, seq,
        exchange=pack_grads(G_MIX))
    dmod_mine = jnp.concatenate([dmod_a, dsh2, dsc2, dgt2, dmod_c], axis=1).reshape(bl, 9 * D)
    small_grads = {
        'g_ffn1': dg_ffn1, 'g_mix': dg_mix, 'a_log': d_alp[:, NH:2 * NH], 'dt_bias': d_dtp[:, NH:2 * NH],
        'g_onorm': dg_onorm, 'lam_re': d_s5p[0].reshape(1, S5G, S5P), 'lam_im': d_s5p[1].reshape(1, S5G, S5P),
        'log_step': d_s5p[2],
        'b_re': d_s5p[3].reshape(S5C, S5G, S5P).transpose(1, 2, 0)[None],
        'b_im': d_s5p[4].reshape(S5C, S5G, S5P).transpose(1, 2, 0)[None],
        'c_re': d_s5p[5].reshape(S5C, S5G, S5P).transpose(1, 0, 2)[None],
        'c_im': d_s5p[6].reshape(S5C, S5G, S5P).transpose(1, 0, 2)[None],
        'd_skip': dg_dskip, 'b_glu': dg_bglu, 'g_ffn2': dg_ffn2, 'g_final': dg_final.reshape(D)}
    small_shapes = [a[n].shape for n in SMALL]
    small_pack = _pack([small_grads[n] for n in SMALL] + [loss_part], F32)
    n_small = small_pack.shape[0]
    small_buf = jnp.concatenate([small_pack, _pack([dmod_mine, d_conv8[:CONVW]], F32)], axis=0)

    gw['w1_ffn1'], sg = mm_tn("gw1_ffn1", d1_1, u1, gather=small_buf)
    gw['w3_ffn1'], rs_w1 = mm_tn("gw3_ffn1", d3_1, u1, exchange=pack_grads(['w1_ffn1']))
    gw['w2_ffn1'], rs_w3 = mm_tn("gw2_ffn1", a1, df1, exchange=pack_grads(['w3_ffn1']))
    rs_w2 = all_to_all("scatter_w2_ffn1", pack_grads(['w2_ffn1']))

    update(rs_ffn2, G_FFN2)
    update(rs_mix, G_MIX)
    update(rs_w1, ['w1_ffn1'])
    update(rs_w3, ['w3_ffn1'])
    update(rs_w2, ['w2_ffn1'])
    pieces = _unpack8(sg[:, n_small:, :], [dmod_mine.shape, (CONVW, 3 * DNW)])
    dmod_all = pieces[0].reshape(NDEV * bl, 9 * D)
    g_wada, g_bada = ada_bwd(c_all, lax.dynamic_slice(dmod_all, (0, me * n_ada), (NDEV * bl, n_ada)), dmod_all)

    n_conv = conv_qkv.shape[2]
    conv_parts = lax.dynamic_slice(pieces[1], (0, 0, me * n_conv), (NDEV, CONVW, n_conv))
    conv_parts = jnp.pad(conv_parts.reshape(NDEV, 1, -1), ((0, 0), (0, 7), (0, 0)))
    pad8 = lambda t: jnp.pad(t.reshape(1, -1), ((0, 7), (0, 0)))
    conv_res = adamw("adamw_conv", conv_parts, pad8(conv_qkv), pad8(m_conv_qkv), pad8(v_conv_qkv))
    for kind, buf in zip(("grad", "delta", "new_m", "new_v"), conv_res):
        res[kind + "_conv_qkv"] = buf[0].reshape(conv_qkv.shape)

    no_param = jnp.zeros_like(loss_part)
    small_res = adamw("adamw_small", sg[:, :n_small, :],
                      *[_pack([a[p + n] for n in SMALL] + [no_param], F32) for p in ("", "m_", "v_")])
    for kind, buf in zip(("grad", "delta", "new_m", "new_v"), small_res):
        for n, t in zip(SMALL, _unpack(buf, small_shapes)):
            res[kind + "_" + n] = t
    loss = _unpack(small_res[0], small_shapes + [loss_part.shape])[-1][0, 0]

    for n, g in (("w_ada", g_wada), ("b_ada", g_bada)):
        shp = a[n].shape
        r2 = lambda t: t.reshape(-1, shp[-1]) if n == "w_ada" else pad8(t)
        out = adamw("adamw_" + n, r2(g)[None], r2(a[n]), r2(a["m_" + n]), r2(a["v_" + n]))
        for kind, buf in zip(("grad", "delta", "new_m", "new_v"), out):
            res[kind + "_" + n] = (buf if n == "w_ada" else buf[0:1]).reshape(shp)

    outs = [loss, dh0.reshape(x.shape)]
    for kind in ("grad", "delta", "new_m", "new_v"):
        outs += [res[kind + "_" + n] for n in WEIGHTS]
    return tuple(outs)
```

```python
import math

import jax
import jax.numpy as jnp
from jax import lax
from jax.experimental import pallas as pl
from jax.experimental.pallas import tpu as pltpu

F32 = jnp.float32
BF16 = jnp.bfloat16
HI = lax.Precision.HIGHEST
H3 = lax.Precision.HIGH
SDS = jax.ShapeDtypeStruct

D = 1024
FF = 2816
FFN_TF = FF
FFN_FWD_TM = 256
FFN_BWD_TM = 256
NH = 8
DH = 64
DNW = NH * DH
CONVW = 4
CH = 64
S5_CH = 128
ACC_LIMIT = 6 * 1024 * 1024
BF16_TILE_ROWS = 16
DN_ROWS = 4
S5W = 512
S5G = 32
S5P = 64
S5C = 16
S5N = S5G * S5P
GB = 4
NDEV = 8
EPS = 1e-6
LANES = 128
ROW = 1024
VMEM_LIMIT = 56 * 1024 * 1024

ADAM_LR, ADAM_B1, ADAM_B2, ADAM_EPS, ADAM_WD, ADAM_STEP = 0.001, 0.9, 0.999, 1e-08, 0.01, 10

WEIGHTS = ['w_ada', 'b_ada', 'g_ffn1', 'w1_ffn1', 'w3_ffn1', 'w2_ffn1', 'g_mix', 'w_in', 'conv_qkv', 'a_log',
           'dt_bias', 'g_onorm', 'lam_re', 'lam_im', 'log_step', 'b_re', 'b_im', 'c_re', 'c_im', 'd_skip', 'w_glu',
           'b_glu', 'w_proj_a', 'w_proj_b', 'w_out', 'g_ffn2', 'w1_ffn2', 'w3_ffn2', 'w2_ffn2', 'g_final']
RS_WEIGHTS = ['w1_ffn1', 'w3_ffn1', 'w2_ffn1', 'w_in', 'w_glu', 'w_proj_a', 'w_proj_b', 'w_out', 'w1_ffn2', 'w3_ffn2',
              'w2_ffn2']
COL_SHARDED = {'w1_ffn1', 'w3_ffn1', 'w_in', 'w_proj_a', 'w_proj_b', 'w1_ffn2', 'w3_ffn2'}
G_FFN1 = ['w1_ffn1', 'w3_ffn1', 'w2_ffn1']
G_MIX = ['w_in', 'w_glu', 'w_proj_a', 'w_proj_b', 'w_out']
G_FFN2 = ['w1_ffn2', 'w3_ffn2', 'w2_ffn2']
SMALL = ['g_ffn1', 'g_mix', 'a_log', 'dt_bias', 'g_onorm', 'lam_re', 'lam_im', 'log_step', 'b_re', 'b_im', 'c_re',
         'c_im', 'd_skip', 'b_glu', 'g_ffn2', 'g_final']


def _cp(n_grid=0):
    if n_grid:
        return pltpu.CompilerParams(vmem_limit_bytes=VMEM_LIMIT, dimension_semantics=("arbitrary",) * n_grid)
    return pltpu.CompilerParams(vmem_limit_bytes=VMEM_LIMIT)


def _dot(a, b):
    return jnp.dot(a.astype(BF16), b.astype(BF16), preferred_element_type=F32)


def _dot_nt(a, b):
    return lax.dot_general(a.astype(BF16), b.astype(BF16), (((1,), (1,)), ((), ())), preferred_element_type=F32)


def _dot_tn(a, b):
    return lax.dot_general(a.astype(BF16), b.astype(BF16), (((0,), (0,)), ((), ())), preferred_element_type=F32)


def _dot_hi(a, b):
    return jnp.dot(a, b, precision=HI, preferred_element_type=F32)


@jax.custom_vjp
def bdot(a, b):
    return _dot(a, b)


bdot.defvjp(lambda a, b: (_dot(a, b), (a, b)),
            lambda r, g: (_dot_nt(g, r[1]).astype(r[0].dtype), _dot_tn(r[0], g).astype(r[1].dtype)))


@jax.custom_vjp
def bdot_nt(a, b):
    return _dot_nt(a, b)


bdot_nt.defvjp(lambda a, b: (_dot_nt(a, b), (a, b)),
               lambda r, g: (_dot(g, r[1]).astype(r[0].dtype), _dot_tn(g, r[0]).astype(r[1].dtype)))


def _silu(x):
    return x * jax.nn.sigmoid(x)


def _iota2(shape, axis):
    return lax.broadcasted_iota(jnp.int32, shape, axis)


def normmod(h, g, sc, sh):
    y = h * lax.rsqrt(jnp.mean(h * h, axis=-1, keepdims=True) + EPS) * g
    return y * (1.0 + sc) + sh


def fn_merge(gate, ya, yb):
    return (jax.nn.sigmoid(gate[:, :D]) * ya + jax.nn.sigmoid(gate[:, D:]) * yb,)


def fn_glu(y, w, b):
    ge = jax.nn.gelu(y)
    return (ge * jax.nn.sigmoid(bdot(ge, w) + b),)


def fn_onorm(o, z, g_on):
    r = _iota2((DH, DNW), 0)
    c = _iota2((DH, DNW), 1)
    expand = (c % DH == r).astype(F32)
    r2 = _iota2((DNW, DNW), 0)
    c2 = _iota2((DNW, DNW), 1)
    avg = (r2 // DH == c2 // DH).astype(F32) * (1.0 / DH)
    ms = bdot(o * o, avg)
    return (o * lax.rsqrt(ms + EPS) * _dot_hi(g_on, expand) * _silu(z),)


def fn_mix_tail(o_pre, z, y_s5, gate, g_on, w_glu, b_glu, wa_t, wb_t):
    (oa,) = fn_onorm(o_pre, z, g_on)
    (ob,) = fn_glu(y_s5, w_glu, b_glu)
    return fn_merge(gate, bdot_nt(oa, wa_t), bdot_nt(ob, wb_t))


def gate_fn(small, alp, dtp):
    beta = jax.nn.sigmoid(small)
    la = -jnp.exp(alp) * jax.nn.softplus(small + dtp)
    tri = (_iota2((CH, CH), 0) >= _iota2((CH, CH), 1)).astype(F32)
    gc = _dot_hi(tri, la)
    gct = lax.dot_general(la, tri, (((0,), (1,)), ((), ())), precision=HI, preferred_element_type=F32)
    return beta, gc, gct


def _bdg(a, b, ca, cb, hi):
    if not hi:
        a, b = a.astype(BF16), b.astype(BF16)
    return lax.dot_general(a, b, (((ca,), (cb,)), ((0,), (0,))), precision=H3 if hi else None,
                           preferred_element_type=F32)


def _batched_matmuls(hi):
    nn_ = lambda a, b: _bdg(a, b, 2, 1, hi)
    nt_ = lambda a, b: _bdg(a, b, 2, 2, hi)
    tn_ = lambda a, b: _bdg(a, b, 1, 1, hi)
    nn = jax.custom_vjp(nn_)
    nn.defvjp(lambda a, b: (nn_(a, b), (a, b)), lambda r, g: (nt_(g, r[1]), tn_(r[0], g)))
    nt = jax.custom_vjp(nt_)
    nt.defvjp(lambda a, b: (nt_(a, b), (a, b)), lambda r, g: (nn_(g, r[1]), tn_(g, r[0])))
    tn = jax.custom_vjp(tn_)
    tn.defvjp(lambda a, b: (tn_(a, b), (a, b)), lambda r, g: (nt_(r[1], g), nn_(r[0], g)))
    return nn, nt, tn


bnn, bnt, btn = _batched_matmuls(False)
hnn, hnt, htn = _batched_matmuls(True)


def _unit_lower_inverse(a):
    r = _iota2((1, CH, CH), 1)
    c = _iota2((1, CH, CH), 2)
    eye = (r == c).astype(F32)
    d = jnp.where(r // 8 == c // 8, a, 0.0)
    inv = eye - d
    p = d
    for _ in range(2):
        p = hnn(p, p)
        inv = inv + hnn(inv, p)
    for blk in (16, 32, 64):
        off = jnp.where((r // blk == c // blk) & (r // (blk // 2) != c // (blk // 2)), a, 0.0)
        mm = hnn if blk == 16 else bnn
        inv = inv - mm(mm(inv, off), inv)
    return inv


@jax.custom_vjp
def _inverse_given(a, t):
    return t


_inverse_given.defvjp(lambda a, t: (t, t), lambda t, g: (-hnt(htn(t, g), t), jnp.zeros_like(t)))


def dn_prep(xc, w):
    t = xc.shape[0] - 8
    c = xc[5:5 + t] * w[0:1] + xc[6:6 + t] * w[1:2] + xc[7:7 + t] * w[2:3] + xc[8:8 + t] * w[3:4]
    act = _silu(c)
    q, k, v = act[:, :DNW], act[:, DNW:2 * DNW], act[:, 2 * DNW:]
    ones = (_iota2((DNW, DNW), 0) // DH == _iota2((DNW, DNW), 1) // DH).astype(F32)
    q = q * lax.rsqrt(bdot(q * q, ones) + EPS) * (DH ** -0.5)
    k = k * lax.rsqrt(bdot(k * k, ones) + EPS)
    return jnp.concatenate([q, k, v], axis=1)


def dn_chunk(q, k, v, b, g, gt, s_prev, t_saved=None):
    r = _iota2((1, CH, CH), 1)
    c = _iota2((1, CH, CH), 2)
    causal = r >= c
    dec = jnp.where(causal, jnp.exp(jnp.where(causal, g - gt, 0.0)), 0.0)
    kb = k * b
    qk = bnt(jnp.concatenate([q, kb], axis=1), k)
    attn = qk[:, :CH] * dec
    a = jnp.where(r > c, qk[:, CH:] * dec, 0.0)
    tinv = _unit_lower_inverse(a) if t_saved is None else _inverse_given(a, t_saved)
    eg = jnp.exp(g)
    uw = hnn(tinv, jnp.concatenate([v * b, kb * eg], axis=2))
    g_last = g[:, CH - 1:CH]
    ws = bnn(jnp.concatenate([uw[..., DH:], q * eg], axis=1), s_prev)
    v_new = uw[..., :DH] - ws[:, :CH]
    o = ws[:, CH:] + bnn(attn, v_new)
    s_new = s_prev * jnp.exp(g_last) + btn(k * jnp.exp(g_last - g), v_new)
    return o, s_new, tinv


def s5_chunk(u, xp_re, xp_im, bb_re, bb_im, cc_re, cc_im, p0r, p0i, p1r, p1i, pir, pii, dsk):
    nb, ch, _ = u.shape
    u2 = u.reshape(nb * ch, LANES)
    bu_re = bdot(u2, bb_re).reshape(nb, ch, 512)
    bu_im = bdot(u2, bb_im).reshape(nb, ch, 512)
    xt_re = pir * bu_re - pii * bu_im
    xt_im = pir * bu_im + pii * bu_re
    tri = jnp.broadcast_to((_iota2((1, ch, ch), 1) >= _iota2((1, ch, ch), 2)).astype(F32), (nb, ch, ch))
    cs_re = hnn(tri, xt_re)
    cs_im = hnn(tri, xt_im)
    x_re = p0r * cs_re - p0i * cs_im + p1r * xp_re - p1i * xp_im
    x_im = p0r * cs_im + p0i * cs_re + p1r * xp_im + p1i * xp_re
    y = bdot_nt(x_re.reshape(nb * ch, 512), cc_re) - bdot_nt(x_im.reshape(nb * ch, 512), cc_im) + dsk * u2
    return y.reshape(nb, ch, LANES), x_re[:, ch - 1:ch], x_im[:, ch - 1:ch]


def s5_tables(lam_re, lam_im, log_step, bre, bim, cre, cim):
    expand = (_iota2((S5G, S5N), 1) // S5P == _iota2((S5G, S5N), 0)).astype(F32)
    step = _dot_hi(jnp.exp(log_step), expand)
    lre = jnp.minimum(lam_re, -1e-4)
    lr = lre * step
    ang = lam_im * step
    mag = jnp.exp(lr)
    lb_re = mag * jnp.cos(ang)
    lb_im = mag * jnp.sin(ang)
    den = lre * lre + lam_im * lam_im
    coef_re = ((lb_re - 1.0) * lre + lb_im * lam_im) / den
    coef_im = (lb_im * lre - (lb_re - 1.0) * lam_im) / den
    bb_re = coef_re * bre - coef_im * bim
    bb_im = coef_re * bim + coef_im * bre
    j = _iota2((S5_CH, 1), 0).astype(F32)
    jc = j - S5_CH // 2
    e0 = jnp.exp(jc * lr)
    e1 = jnp.exp((j + 1.0) * lr)
    ei = jnp.exp(-jc * lr)
    mask = (_iota2((LANES, 512), 0) // S5C == _iota2((LANES, 512), 1) // S5P).astype(F32)

    def blocks(t):
        return jnp.concatenate([(jnp.tile(t[:, gb * 512:(gb + 1) * 512], (LANES // S5C, 1)) * mask)[None]
                                for gb in range(GB)], axis=0)

    return (blocks(bb_re), blocks(bb_im), blocks(cre), blocks(cim),
            e0 * jnp.cos(jc * ang), e0 * jnp.sin(jc * ang),
            e1 * jnp.cos((j + 1.0) * ang), e1 * jnp.sin((j + 1.0) * ang),
            ei * jnp.cos(jc * ang), -ei * jnp.sin(jc * ang))


def _row_specs(tiled, batch, bcast, tm, tpb):
    specs = [pl.BlockSpec((tm, a.shape[1]), lambda i: (i, 0)) for a in tiled]
    specs += [pl.BlockSpec((None,) + a.shape[1:], lambda i: (i // tpb, 0, 0)) for a in batch]
    specs += [pl.BlockSpec(a.shape, lambda i, nd=a.ndim: (0,) * nd) for a in bcast]
    return specs


def ew_call(name, fn, tiled, batch, bcast, outs, tm, seq):
    t_rows = tiled[0].shape[0]
    n_in = len(tiled) + len(batch) + len(bcast)

    def body(*refs):
        vals = [r[...].astype(F32) for r in refs[:n_in]]
        for r, o in zip(refs[n_in:], fn(*vals)):
            r[...] = o.astype(r.dtype)

    return pl.pallas_call(
        body, grid=(t_rows // tm,), in_specs=_row_specs(tiled, batch, bcast, tm, seq // tm),
        out_specs=[pl.BlockSpec((tm, w), lambda i: (i, 0)) for w, _ in outs],
        out_shape=[SDS((t_rows, w), dt) for w, dt in outs], name=name, compiler_params=_cp(1))(*tiled, *batch, *bcast)


def ew_vjp_call(name, fn, tiled, batch, bcast, cts, want, tm, seq, addend=None):
    t_rows = tiled[0].shape[0]
    tpb = seq // tm
    n_t, n_b, n_c = len(tiled), len(batch), len(bcast)
    n_in = n_t + n_b + n_c
    extra = [] if addend is None else [addend]

    def body(*refs):
        i = pl.program_id(0)
        vals = [r[...].astype(F32) for r in refs[:n_in]]
        ctv = tuple(r[...].astype(F32) for r in refs[n_in:n_in + len(cts)])
        outs = refs[n_in + len(cts) + len(extra):]
        _, vjp = jax.vjp(fn, *vals)
        grads = vjp(ctv)
        for k, (r, (idx, _)) in enumerate(zip(outs[:len(want)], want)):
            g = grads[idx]
            if k == 0 and extra:
                g = g + refs[n_in + len(cts)][...]
            r[...] = g.astype(r.dtype)
        for k in range(n_b):
            r, g = outs[len(want) + k], grads[n_t + k]

            @pl.when(i % tpb == 0)
            def _(r=r, g=g):
                r[...] = g

            @pl.when(i % tpb != 0)
            def _(r=r, g=g):
                r[...] += g
        for k in range(n_c):
            r, g = outs[len(want) + n_b + k], grads[n_t + n_b + k]

            @pl.when(i == 0)
            def _(r=r, g=g):
                r[...] = g

            @pl.when(i != 0)
            def _(r=r, g=g):
                r[...] += g

    out_specs = [pl.BlockSpec((tm, tiled[idx].shape[1]), lambda i: (i, 0)) for idx, _ in want]
    out_specs += [pl.BlockSpec((None,) + a.shape[1:], lambda i: (i // tpb, 0, 0)) for a in batch]
    out_specs += [pl.BlockSpec(a.shape, lambda i, nd=a.ndim: (0,) * nd) for a in bcast]
    out_shape = [SDS(tiled[idx].shape, dt) for idx, dt in want]
    out_shape += [SDS(a.shape, F32) for a in batch] + [SDS(a.shape, F32) for a in bcast]
    res = pl.pallas_call(
        body, grid=(t_rows // tm,),
        in_specs=_row_specs(tiled, batch, bcast, tm, tpb)
        + [pl.BlockSpec((tm, a.shape[1]), lambda i: (i, 0)) for a in list(cts) + extra],
        out_specs=out_specs, out_shape=out_shape, name=name, compiler_params=_cp(1))(*tiled, *batch, *bcast, *cts, *extra)
    return res[:len(want)], res[len(want):len(want) + n_b], res[len(want) + n_b:]


def _pick(n, cands):
    for c in cands:
        if n % c == 0:
            return c
    return n


def mm_tn(name, a, b, exchange=None, gather=None):
    t_rows, m = a.shape
    n = b.shape[1]
    tn = n if n <= 1024 else _pick(n, (1024, 512, 256, 128))
    tm = max([t for t in range(LANES, m + 1, LANES) if m % t == 0 and t * tn * 4 <= ACC_LIMIT] or [m])
    tk = _pick(t_rows, (2048, 1024, 512, 256, 128, 64))
    grid = (m // tm, n // tn, t_rows // tk)
    extra = [x for x in (exchange, gather) if x is not None]
    ne = len(extra)

    def body(*refs):
        a_ref, b_ref = refs[:2]
        o_ref, acc = refs[2 + ne], refs[3 + 2 * ne]
        i, j, k = pl.program_id(0), pl.program_id(1), pl.program_id(2)
        first = (i == 0) & (j == 0) & (k == 0)
        middle = (i == grid[0] - 1) & (j == grid[1] - 1) & (k == grid[2] // 2)
        last = (i == grid[0] - 1) & (j == grid[1] - 1) & (k == grid[2] - 1)
        at_end = []
        for e, x in enumerate(extra):
            comm_refs = (refs[2 + e], refs[3 + ne + e]) + tuple(refs[4 + 2 * ne + 3 * e:7 + 2 * ne + 3 * e])
            if x is exchange:
                start, finish = _exchange_phases(*comm_refs)
                pl.when(first)(start)
            else:
                start, forward, finish = _gather_phases(*comm_refs)
                pl.when(first)(start)
                pl.when(middle)(forward)
            at_end.append(finish)

        @pl.when(k == 0)
        def _():
            acc[...] = jnp.zeros_like(acc)

        acc[...] += _dot_tn(a_ref[...], b_ref[...])

        @pl.when(k == grid[2] - 1)
        def _():
            o_ref[...] = acc[...].astype(BF16)

        for phase in at_end:
            pl.when(last)(phase)

    res = pl.pallas_call(
        body, grid=grid,
        in_specs=[pl.BlockSpec((tk, tm), lambda i, j, k: (k, i)), pl.BlockSpec((tk, tn), lambda i, j, k: (k, j))]
        + [HBM_SPEC] * ne,
        out_specs=[pl.BlockSpec((tm, tn), lambda i, j, k: (i, j))] + [HBM_SPEC] * ne,
        out_shape=[SDS((m, n), BF16)] + [SDS(x.shape if x is exchange else (NDEV,) + x.shape, x.dtype) for x in extra],
        scratch_shapes=[pltpu.VMEM((tm, tn), F32)] + _comm_scratch() * ne, name=name,
        compiler_params=_cp(3))(a, b, *extra)
    return res if extra else res[0]


def _ffn_weight_spec(w, k):
    assert FFN_TF == FF
    if w.ndim == 3:
        return pl.BlockSpec((NDEV, FF // NDEV, D), lambda i, j: (0, k, 0), pipeline_mode=pl.Buffered(1))
    return pl.BlockSpec((FF, D), lambda i, j: (0, 0), pipeline_mode=pl.Buffered(1))


def _ffn_weight(ref):
    return ref[...].reshape(FF, D)


def ffn_fwd(name, h, mod3, g, w1, w3, w2, seq, gather=None, loss_head=None):
    t_rows = h.shape[0]
    tm = _pick(seq, (FFN_FWD_TM, 128, 64))
    tf = FFN_TF
    tpb = seq // tm
    nf = FF // tf
    nt = t_rows // tm
    extra = [] if gather is None else [gather]
    head = [] if loss_head is None else list(loss_head)
    nh, ne = len(head), len(extra)

    def body(*refs):
        h_ref, mod_ref, g_ref, w1_ref, w3_ref, w2_ref = refs[:6]
        o0 = 6 + nh + ne
        ho_ref, f_ref, u_ref, h1_ref, h3_ref = refs[o0:o0 + 5]
        s0 = o0 + 5 + nh + ne
        acc = refs[s0]
        i, j = pl.program_id(0), pl.program_id(1)
        if extra:
            start, forward, finish = _gather_phases(refs[6 + nh], refs[o0 + 5 + nh], *refs[s0 + 1:s0 + 4])
            pl.when((i == 0) & (j == 0))(start)
            pl.when((i == nt // 2) & (j == 0))(forward)

        @pl.when(j == 0)
        def _():
            u_ref[...] = normmod(h_ref[...], g_ref[...], mod_ref[1:2, :], mod_ref[0:1, :]).astype(BF16)
            acc[...] = jnp.zeros_like(acc)

        u = u_ref[...]
        h1 = _dot_nt(u, _ffn_weight(w1_ref))
        h3 = _dot_nt(u, _ffn_weight(w3_ref))
        h1_ref[...] = h1.astype(BF16)
        h3_ref[...] = h3.astype(BF16)
        acc[...] += _dot(_silu(h1) * h3, _ffn_weight(w2_ref))

        @pl.when(j == nf - 1)
        def _():
            f_ref[...] = acc[...]
            h_out = h_ref[...] + 0.5 * mod_ref[2:3, :] * acc[...]
            if not head:
                ho_ref[...] = h_out
            else:
                t_ref, gf_ref, dg_ref, loss_ref = refs[6], refs[7], refs[o0 + 5], refs[o0 + 6]
                y, vjp = jax.vjp(lambda hh, gg: hh * lax.rsqrt(jnp.mean(hh * hh, axis=-1, keepdims=True) + EPS) * gg,
                                 h_out, gf_ref[...])
                e = y - t_ref[...]
                dh, dg = vjp(e * (1.0 / D))
                part = jnp.sum(jnp.sum(e * e, axis=1, keepdims=True), axis=0, keepdims=True) * (0.5 / D) \
                    + jnp.zeros((1, LANES), F32)
                ho_ref[...] = dh

                @pl.when(i == 0)
                def _():
                    dg_ref[...] = dg
                    loss_ref[...] = part

                @pl.when(i != 0)
                def _():
                    dg_ref[...] += dg
                    loss_ref[...] += part

        if extra:
            pl.when((i == nt - 1) & (j == nf - 1))(finish)

    row = lambda i, j: (i, 0)
    const = lambda i, j: (0, 0)
    head_in = [pl.BlockSpec((tm, D), row), pl.BlockSpec((1, D), const)] if head else []
    head_out = [pl.BlockSpec((1, D), const), pl.BlockSpec((1, LANES), const)] if head else []
    return pl.pallas_call(
        body, grid=(nt, nf),
        in_specs=[pl.BlockSpec((tm, D), row), pl.BlockSpec((None, 3, D), lambda i, j: (i // tpb, 0, 0)),
                  pl.BlockSpec((1, D), const)] + [_ffn_weight_spec(w, k) for k, w in enumerate((w1, w3, w2))]
        + head_in + [HBM_SPEC] * ne,
        out_specs=[pl.BlockSpec((tm, D), row), pl.BlockSpec((tm, D), row), pl.BlockSpec((tm, D), row),
                   pl.BlockSpec((tm, tf), lambda i, j: (i, j)), pl.BlockSpec((tm, tf), lambda i, j: (i, j))]
        + head_out + [HBM_SPEC] * ne,
        out_shape=[SDS((t_rows, D), F32), SDS((t_rows, D), F32), SDS((t_rows, D), BF16), SDS((t_rows, FF), BF16),
                   SDS((t_rows, FF), BF16)] + ([SDS((1, D), F32), SDS((1, LANES), F32)] if head else [])
        + [SDS((NDEV,) + x.shape, x.dtype) for x in extra],
        scratch_shapes=[pltpu.VMEM((tm, D), F32)] + (_comm_scratch() if extra else []), name=name,
        compiler_params=_cp(2))(h, mod3, g, w1, w3, w2, *head, *extra)


def ffn_bwd(name, dho, h, f_out, h1_in, h3_in, mod3, g, w1, w3, w2, seq, exchange=None):
    t_rows = h.shape[0]
    tm = _pick(seq, (FFN_BWD_TM, 128, 64))
    tf = FFN_TF
    tpb = seq // tm
    nf = FF // tf
    nt = t_rows // tm
    extra = [] if exchange is None else [exchange]

    def body(*refs):
        dho_ref, h_ref, f_ref, h1_ref, h3_ref, mod_ref, g_ref, w1_ref, w3_ref, w2_ref = refs[:10]
        dh_ref, a_ref, dh1_ref, dh3_ref, df_scr, dmod_ref, dg_ref = refs[10 + len(extra):17 + len(extra)]
        du_acc = refs[17 + 2 * len(extra)]
        i, j = pl.program_id(0), pl.program_id(1)
        if extra:
            start, finish = _exchange_phases(refs[10], refs[18], *refs[20:23])
            pl.when((i == 0) & (j == 0))(start)

        @pl.when(j == 0)
        def _():
            df_scr[...] = (0.5 * mod_ref[2:3, :] * dho_ref[...]).astype(BF16)
            du_acc[...] = jnp.zeros_like(du_acc)

        h1 = h1_ref[...].astype(F32)
        h3 = h3_ref[...].astype(F32)
        sg = jax.nn.sigmoid(h1)
        s = h1 * sg
        da = _dot_nt(df_scr[...], _ffn_weight(w2_ref))
        dh3 = (da * s).astype(BF16)
        dh1 = (da * h3 * (sg * (1.0 + h1 * (1.0 - sg)))).astype(BF16)
        a_ref[...] = (s * h3).astype(BF16)
        dh1_ref[...] = dh1
        dh3_ref[...] = dh3
        du_acc[...] += _dot(dh1, _ffn_weight(w1_ref)) + _dot(dh3, _ffn_weight(w3_ref))

        @pl.when(j == nf - 1)
        def _():
            _, vjp = jax.vjp(normmod, h_ref[...], g_ref[...], mod_ref[1:2, :], mod_ref[0:1, :])
            dh_n, dg, dsc, dsh = vjp(du_acc[...])
            dh_ref[...] = dho_ref[...] + dh_n
            dgt = jnp.sum(0.5 * dho_ref[...] * f_ref[...], axis=0, keepdims=True)
            dmod = jnp.concatenate([dsh, dsc, dgt], axis=0)

            @pl.when(i % tpb == 0)
            def _():
                dmod_ref[...] = dmod

            @pl.when(i % tpb != 0)
            def _():
                dmod_ref[...] += dmod

            @pl.when(i == 0)
            def _():
                dg_ref[...] = dg

            @pl.when(i != 0)
            def _():
                dg_ref[...] += dg

        if extra:
            pl.when((i == nt - 1) & (j == nf - 1))(finish)

    row = lambda i, j: (i, 0)
    col = lambda i, j: (i, j)
    return pl.pallas_call(
        body, grid=(nt, nf),
        in_specs=[pl.BlockSpec((tm, D), row), pl.BlockSpec((tm, D), row), pl.BlockSpec((tm, D), row),
                  pl.BlockSpec((tm, tf), col), pl.BlockSpec((tm, tf), col),
                  pl.BlockSpec((None, 3, D), lambda i, j: (i // tpb, 0, 0)),
                  pl.BlockSpec((1, D), lambda i, j: (0, 0))] + [_ffn_weight_spec(w, k) for k, w in enumerate((w1, w3, w2))]
        + [HBM_SPEC] * len(extra),
        out_specs=[pl.BlockSpec((tm, D), row), pl.BlockSpec((tm, tf), col), pl.BlockSpec((tm, tf), col),
                   pl.BlockSpec((tm, tf), col), pl.BlockSpec((tm, D), row),
                   pl.BlockSpec((None, 3, D), lambda i, j: (i // tpb, 0, 0)), pl.BlockSpec((1, D), lambda i, j: (0, 0))]
        + [HBM_SPEC] * len(extra),
        out_shape=[SDS((t_rows, D), F32), SDS((t_rows, FF), BF16), SDS((t_rows, FF), BF16), SDS((t_rows, FF), BF16),
                   SDS((t_rows, D), BF16), SDS(mod3.shape, F32), SDS((1, D), F32)] + [SDS(x.shape, x.dtype) for x in extra],
        scratch_shapes=[pltpu.VMEM((tm, D), F32)] + (_comm_scratch() if extra else []), name=name,
        compiler_params=_cp(2))(dho, h, f_out, h1_in, h3_in, mod3, g, w1, w3, w2, *extra)


def _resident(shape):
    return pl.BlockSpec(shape, lambda i: (0,) * len(shape), pipeline_mode=pl.Buffered(1))


def mix_in_fwd(h, sh, sc, g, ws, seq):
    t_rows = h.shape[0]
    tm = _pick(seq, (256, 128, 64))
    tpb = seq // tm
    nw = len(ws)

    def body(h_ref, sh_ref, sc_ref, g_ref, *rest):
        u = normmod(h_ref[...], g_ref[...], sc_ref[...], sh_ref[...]).astype(BF16)
        rest[nw][...] = u
        for w_ref, p_ref in zip(rest[:nw], rest[nw + 1:]):
            p_ref[...] = _dot_nt(u, w_ref[...])

    row = lambda i: (i, 0)
    batch = pl.BlockSpec((None, 1, D), lambda i: (i // tpb, 0, 0))
    return pl.pallas_call(
        body, grid=(t_rows // tm,),
        in_specs=[pl.BlockSpec((tm, D), row), batch, batch, pl.BlockSpec((1, D), lambda i: (0, 0))]
        + [_resident(w.shape) for w in ws],
        out_specs=[pl.BlockSpec((tm, D), row)] + [pl.BlockSpec((tm, w.shape[0]), row) for w in ws],
        out_shape=[SDS((t_rows, D), BF16)] + [SDS((t_rows, w.shape[0]), F32) for w in ws], name="mix_in_fwd",
        compiler_params=_cp(1))(h, sh, sc, g, *ws)


def mix_in_bwd(dps, ws, h, sh, sc, g, dh_add, seq):
    t_rows = h.shape[0]
    tm = _pick(seq, (256, 128, 64))
    tpb = seq // tm
    nw = len(ws)

    def body(*refs):
        h_ref, sh_ref, sc_ref, g_ref, add_ref, dh_ref, dsh_ref, dsc_ref, dg_ref = refs[2 * nw:]
        i = pl.program_id(0)
        du = _dot(refs[0][...], refs[nw][...])
        for k in range(1, nw):
            du = du + _dot(refs[k][...], refs[nw + k][...])
        _, vjp = jax.vjp(normmod, h_ref[...], g_ref[...], sc_ref[...], sh_ref[...])
        dh_n, dg, dsc, dsh = vjp(du)
        dh_ref[...] = add_ref[...] + dh_n

        @pl.when(i % tpb == 0)
        def _():
            dsh_ref[...] = dsh
            dsc_ref[...] = dsc

        @pl.when(i % tpb != 0)
        def _():
            dsh_ref[...] += dsh
            dsc_ref[...] += dsc

        @pl.when(i == 0)
        def _():
            dg_ref[...] = dg

        @pl.when(i != 0)
        def _():
            dg_ref[...] += dg

    row = lambda i: (i, 0)
    batch = pl.BlockSpec((None, 1, D), lambda i: (i // tpb, 0, 0))
    gain = pl.BlockSpec((1, D), lambda i: (0, 0))
    return pl.pallas_call(
        body, grid=(t_rows // tm,),
        in_specs=[pl.BlockSpec((tm, dp.shape[1]), row) for dp in dps] + [_resident(w.shape) for w in ws]
        + [pl.BlockSpec((tm, D), row), batch, batch, gain, pl.BlockSpec((tm, D), row)],
        out_specs=[pl.BlockSpec((tm, D), row), batch, batch, gain],
        out_shape=[SDS((t_rows, D), F32), SDS(sh.shape, F32), SDS(sc.shape, F32), SDS((1, D), F32)], name="mix_in_bwd",
        compiler_params=_cp(1))(*dps, *ws, h, sh, sc, g, dh_add)


def mix_out_fwd(merged, w_out, h_prev, gt, seq):
    t_rows = merged.shape[0]
    tm = _pick(seq, (256, 128, 64))
    tpb = seq // tm

    def body(m_ref, w_ref, h_ref, gt_ref, mo_ref, ho_ref):
        mo = _dot(m_ref[...], w_ref[...])
        mo_ref[...] = mo
        ho_ref[...] = h_ref[...] + gt_ref[...] * mo

    row = lambda i: (i, 0)
    return pl.pallas_call(
        body, grid=(t_rows // tm,),
        in_specs=[pl.BlockSpec((tm, D), row), _resident(w_out.shape), pl.BlockSpec((tm, D), row),
                  pl.BlockSpec((None, 1, D), lambda i: (i // tpb, 0, 0))],
        out_specs=[pl.BlockSpec((tm, D), row), pl.BlockSpec((tm, D), row)],
        out_shape=[SDS((t_rows, D), F32), SDS((t_rows, D), F32)], name="mix_out_fwd",
        compiler_params=_cp(1))(merged, w_out, h_prev, gt)


def mix_out_bwd(dh, mo, w_out, gt, seq):
    t_rows = dh.shape[0]
    tm = _pick(seq, (256, 128, 64))
    tpb = seq // tm

    def body(dh_ref, mo_ref, w_ref, gt_ref, dmo_ref, dm_ref, dgt_ref):
        i = pl.program_id(0)
        dmo = (gt_ref[...] * dh_ref[...]).astype(BF16)
        dmo_ref[...] = dmo
        dm_ref[...] = _dot_nt(dmo, w_ref[...])
        dgt = jnp.sum(dh_ref[...] * mo_ref[...], axis=0, keepdims=True)

        @pl.when(i % tpb == 0)
        def _():
            dgt_ref[...] = dgt

        @pl.when(i % tpb != 0)
        def _():
            dgt_ref[...] += dgt

    row = lambda i: (i, 0)
    batch = pl.BlockSpec((None, 1, D), lambda i: (i // tpb, 0, 0))
    return pl.pallas_call(
        body, grid=(t_rows // tm,),
        in_specs=[pl.BlockSpec((tm, D), row), pl.BlockSpec((tm, D), row), _resident(w_out.shape), batch],
        out_specs=[pl.BlockSpec((tm, D), row), pl.BlockSpec((tm, D), row), batch],
        out_shape=[SDS((t_rows, D), BF16), SDS((t_rows, D), F32), SDS(gt.shape, F32)], name="mix_out_bwd",
        compiler_params=_cp(1))(dh, mo, w_out, gt)


def _dn_cols(part, hd):
    return slice(part * DNW + hd * DH, part * DNW + (hd + 1) * DH)


def _qkv_stacks(qkv_ref, nb):
    pairs = [(b, hd) for b in range(nb) for hd in range(NH)]
    return [jnp.stack([qkv_ref[b, :, _dn_cols(part, hd)] for b, hd in pairs]) for part in range(3)]


def dn_prep_fwd(p_dn, conv8):
    bl, seq, _ = p_dn.shape
    tp = _pick(seq, (256, 128, 64))

    def body(raw_ref, halo_ref, conv_ref, o_ref):
        hm = (pl.program_id(1) > 0).astype(F32)
        o_ref[...] = dn_prep(jnp.concatenate([halo_ref[...] * hm, raw_ref[...]], axis=0), conv_ref[...])

    return pl.pallas_call(
        body, grid=(bl, seq // tp),
        in_specs=[pl.BlockSpec((None, tp, 3 * DNW), lambda b, i: (b, i, 0)),
                  pl.BlockSpec((None, 8, 3 * DNW), lambda b, i: (b, jnp.maximum(i * (tp // 8) - 1, 0), 0)),
                  pl.BlockSpec((8, 3 * DNW), lambda b, i: (0, 0))],
        out_specs=pl.BlockSpec((None, tp, 3 * DNW), lambda b, i: (b, i, 0)),
        out_shape=SDS((bl, seq, 3 * DNW), F32), name="dn_prep_fwd", compiler_params=_cp(2))(p_dn, p_dn, conv8)


def dn_prep_bwd(p_dn, conv8, d_qkv, d_z):
    bl, seq, _ = p_dn.shape
    tp = _pick(seq, (256, 128, 64))
    nt = seq // tp

    def body(raw_ref, halo_ref, conv_ref, dq_ref, dz_ref, draw_ref, dconv_ref, carry):
        b, r = pl.program_id(0), pl.program_id(1)

        @pl.when((b == 0) & (r == 0))
        def _():
            dconv_ref[...] = jnp.zeros_like(dconv_ref)

        @pl.when(r == 0)
        def _():
            carry[...] = jnp.zeros_like(carry)

        hm = (r < nt - 1).astype(F32)
        _, vjp = jax.vjp(dn_prep, jnp.concatenate([halo_ref[...] * hm, raw_ref[...]], axis=0), conv_ref[...])
        dxc, dw = vjp(dq_ref[...])
        tail = dxc[tp:tp + 8] + carry[...]
        draw_ref[:, 0:3 * DNW] = jnp.concatenate([dxc[8:tp], tail], axis=0).astype(BF16)
        draw_ref[:, 3 * DNW:4 * DNW] = dz_ref[...].astype(BF16)
        carry[...] = dxc[0:8] * hm
        dconv_ref[...] += dw

    blk = lambda b, r: (b, nt - 1 - r, 0)
    return pl.pallas_call(
        body, grid=(bl, nt),
        in_specs=[pl.BlockSpec((None, tp, 3 * DNW), blk),
                  pl.BlockSpec((None, 8, 3 * DNW), lambda b, r: (b, jnp.maximum((nt - 1 - r) * (tp // 8) - 1, 0), 0)),
                  pl.BlockSpec((8, 3 * DNW), lambda b, r: (0, 0)), pl.BlockSpec((None, tp, 3 * DNW), blk),
                  pl.BlockSpec((None, tp, DNW), blk)],
        out_specs=[pl.BlockSpec((None, tp, 4 * DNW), blk), pl.BlockSpec((8, 3 * DNW), lambda b, r: (0, 0))],
        out_shape=[SDS((bl, seq, 4 * DNW), BF16), SDS((8, 3 * DNW), F32)],
        scratch_shapes=[pltpu.VMEM((8, 3 * DNW), F32)], name="dn_prep_bwd", compiler_params=_cp(2))(p_dn, p_dn, conv8, d_qkv, d_z)


def _gate_stacks(gates, nb):
    pairs = [(b, hd) for b in range(nb) for hd in range(NH)]
    bs = jnp.stack([gates[b][0][:, hd:hd + 1] for b, hd in pairs])
    gs = jnp.stack([gates[b][1][:, NH + hd:NH + hd + 1] for b, hd in pairs])
    gts = jnp.stack([gates[b][2][NH + hd:NH + hd + 1, :] for b, hd in pairs])
    return bs, gs, gts


def deltanet_fwd(qkv, p_small, alp, dtp, nb, gather=None):
    bl, seq, _ = qkv.shape
    nc = seq // CH
    ng = nb * NH
    extra = [] if gather is None else [gather]

    def body(*refs):
        qkv_ref, small_ref, alp_ref, dtp_ref = refs[:4]
        o_ref, sprev_ref, tinv_ref = refs[4 + len(extra):7 + len(extra)]
        s_scr = refs[7 + 2 * len(extra)]
        bb, n = pl.program_id(0), pl.program_id(1)
        if extra:
            start, forward, finish = _gather_phases(refs[4], refs[8], *refs[10:13])
            pl.when((bb == 0) & (n == 0))(start)

        @pl.when(n == 0)
        def _():
            s_scr[...] = jnp.zeros_like(s_scr)

        gates = [gate_fn(small_ref[b], alp_ref[...], dtp_ref[...]) for b in range(nb)]
        s_prev = s_scr[...]
        o, s_new, tinv = dn_chunk(*_qkv_stacks(qkv_ref, nb), *_gate_stacks(gates, nb), s_prev)
        sprev_ref[...] = s_prev
        tinv_ref[...] = tinv
        s_scr[...] = s_new
        for b in range(nb):
            for hd in range(NH):
                o_ref[b, :, hd * DH:(hd + 1) * DH] = o[b * NH + hd]
        if extra:
            pl.when((bb == bl // nb - 1) & (n == nc // 2))(forward)
            pl.when((bb == bl // nb - 1) & (n == nc - 1))(finish)

    blk = lambda bb, n: (bb, n, 0)
    const = lambda bb, n: (0, 0)
    saved = pl.BlockSpec((None, ng, DH, DH), lambda bb, n: (bb * nc + n, 0, 0, 0))
    return pl.pallas_call(
        body, grid=(bl // nb, nc),
        in_specs=[pl.BlockSpec((nb, CH, 3 * DNW), blk), pl.BlockSpec((nb, CH, LANES), blk),
                  pl.BlockSpec((1, LANES), const), pl.BlockSpec((1, LANES), const)] + [HBM_SPEC] * len(extra),
        out_specs=[pl.BlockSpec((nb, CH, DNW), blk), saved, saved] + [HBM_SPEC] * len(extra),
        out_shape=[SDS((bl, seq, DNW), F32), SDS((bl // nb * nc, ng, DH, DH), F32), SDS((bl // nb * nc, ng, DH, DH), F32)]
        + [SDS((NDEV,) + x.shape, x.dtype) for x in extra],
        scratch_shapes=[pltpu.VMEM((ng, DH, DH), F32)] + (_comm_scratch() if extra else []), name="deltanet_fwd",
        compiler_params=_cp(2))(qkv, p_small, alp, dtp, *extra)


def deltanet_bwd(qkv, p_small, alp, dtp, sprev, tinv, d_o, nb, exchange=None):
    bl, seq, _ = qkv.shape
    nc = seq // CH
    ng = nb * NH
    extra = [] if exchange is None else [exchange]

    def body(*refs):
        qkv_ref, small_ref, alp_ref, dtp_ref, sprev_ref, tinv_ref, do_ref = refs[:7]
        dqkv_ref, dsmall_ref, dalp_ref, ddtp_ref = refs[7 + len(extra):11 + len(extra)]
        ds_scr = refs[11 + 2 * len(extra)]
        bb, r = pl.program_id(0), pl.program_id(1)
        if extra:
            start, finish = _exchange_phases(refs[7], refs[12], *refs[14:17])
            pl.when((bb == 0) & (r == 0))(start)

        @pl.when((bb == 0) & (r == 0))
        def _():
            dalp_ref[...] = jnp.zeros_like(dalp_ref)
            ddtp_ref[...] = jnp.zeros_like(ddtp_ref)

        @pl.when(r == 0)
        def _():
            ds_scr[...] = jnp.zeros_like(ds_scr)

        gates, gate_vjps = [], []
        for b in range(nb):
            out, gvjp = jax.vjp(gate_fn, small_ref[b], alp_ref[...], dtp_ref[...])
            gates.append(out)
            gate_vjps.append(gvjp)
        t_saved = tinv_ref[...]
        _, vjp = jax.vjp(lambda *args: dn_chunk(*args, t_saved)[:2], *_qkv_stacks(qkv_ref, nb), *_gate_stacks(gates, nb),
                         sprev_ref[...])
        d_out = jnp.stack([do_ref[b, :, hd * DH:(hd + 1) * DH] for b in range(nb) for hd in range(NH)])
        grads = vjp((d_out, ds_scr[...]))
        ds_scr[...] = grads[6]
        lane = _iota2((CH, LANES), 1)
        rowi = _iota2((LANES, CH), 0)
        for b in range(nb):
            d_beta = jnp.zeros((CH, LANES), F32)
            d_gc = jnp.zeros((CH, LANES), F32)
            d_gct = jnp.zeros((LANES, CH), F32)
            for hd in range(NH):
                i = b * NH + hd
                for part in range(3):
                    dqkv_ref[b, :, _dn_cols(part, hd)] = grads[part][i]
                d_beta = d_beta + jnp.where(lane == hd, grads[3][i], 0.0)
                d_gc = d_gc + jnp.where(lane == NH + hd, grads[4][i], 0.0)
                d_gct = d_gct + jnp.where(rowi == NH + hd, grads[5][i], 0.0)
            d_small, d_alp, d_dtp = gate_vjps[b]((d_beta, d_gc, d_gct))
            dsmall_ref[b] = d_small.astype(BF16)
            dalp_ref[...] += d_alp
            ddtp_ref[...] += d_dtp
        if extra:
            pl.when((bb == bl // nb - 1) & (r == nc - 1))(finish)

    blk = lambda bb, r: (bb, nc - 1 - r, 0)
    const = lambda bb, r: (0, 0)
    saved = pl.BlockSpec((None, ng, DH, DH), lambda bb, r: (bb * nc + nc - 1 - r, 0, 0, 0))
    return pl.pallas_call(
        body, grid=(bl // nb, nc),
        in_specs=[pl.BlockSpec((nb, CH, 3 * DNW), blk), pl.BlockSpec((nb, CH, LANES), blk), pl.BlockSpec((1, LANES), const),
                  pl.BlockSpec((1, LANES), const), saved, saved, pl.BlockSpec((nb, CH, DNW), blk)] + [HBM_SPEC] * len(extra),
        out_specs=[pl.BlockSpec((nb, CH, 3 * DNW), blk), pl.BlockSpec((nb, CH, LANES), blk), pl.BlockSpec((1, LANES), const),
                   pl.BlockSpec((1, LANES), const)] + [HBM_SPEC] * len(extra),
        out_shape=[SDS((bl, seq, 3 * DNW), F32), SDS((bl, seq, LANES), BF16), SDS((1, LANES), F32), SDS((1, LANES), F32)]
        + [SDS(x.shape, x.dtype) for x in extra],
        scratch_shapes=[pltpu.VMEM((ng, DH, DH), F32)] + (_comm_scratch() if extra else []), name="deltanet_bwd",
        compiler_params=_cp(2))(qkv, p_small, alp, dtp, sprev, tinv, d_o, *extra)


def _s5_table_specs():
    tab3 = pl.BlockSpec((None, LANES, 512), lambda gb, n: (gb, 0, 0))
    tab2 = pl.BlockSpec((S5_CH, 512), lambda gb, n: (0, gb))
    return [tab3] * 4 + [tab2] * 6 + [pl.BlockSpec((1, LANES), lambda gb, n: (0, gb))]


def s5_fwd(u, tables, dsk):
    bl, seq, _ = u.shape
    nc = seq // S5_CH

    def body(u_ref, *rest):
        tabs, (y_ref, xs_ref, xr_scr, xi_scr) = rest[:11], rest[11:]

        @pl.when(pl.program_id(1) == 0)
        def _():
            xr_scr[...] = jnp.zeros_like(xr_scr)
            xi_scr[...] = jnp.zeros_like(xi_scr)

        xp_re, xp_im = xr_scr[...], xi_scr[...]
        xs_ref[0:bl] = xp_re
        xs_ref[bl:2 * bl] = xp_im
        y, xn_re, xn_im = s5_chunk(u_ref[...], xp_re, xp_im, *[t[...] for t in tabs])
        y_ref[...] = y
        xr_scr[...] = xn_re
        xi_scr[...] = xn_im

    blk = lambda gb, n: (0, n, gb)
    return pl.pallas_call(
        body, grid=(GB, nc), in_specs=[pl.BlockSpec((bl, S5_CH, LANES), blk)] + _s5_table_specs(),
        out_specs=[pl.BlockSpec((bl, S5_CH, LANES), blk),
                   pl.BlockSpec((None, 2 * bl, 1, 512), lambda gb, n: (gb * nc + n, 0, 0, 0))],
        out_shape=[SDS((bl, seq, S5W), F32), SDS((GB * nc, 2 * bl, 1, 512), F32)],
        scratch_shapes=[pltpu.VMEM((bl, 1, 512), F32), pltpu.VMEM((bl, 1, 512), F32)], name="s5_fwd",
        compiler_params=_cp(2))(u, *tables, dsk)


def s5_bwd(u, tables, dsk, xs, dy):
    bl, seq, _ = u.shape
    nc = seq // S5_CH

    def body(u_ref, *rest):
        tabs, xs_ref, dy_ref = rest[:11], rest[11], rest[12]
        du_ref, dtabs, dxr_scr, dxi_scr = rest[13], rest[14:25], rest[25], rest[26]
        r = pl.program_id(1)

        @pl.when(r == 0)
        def _():
            for t in dtabs:
                t[...] = jnp.zeros_like(t)
            dxr_scr[...] = jnp.zeros_like(dxr_scr)
            dxi_scr[...] = jnp.zeros_like(dxi_scr)

        _, vjp = jax.vjp(s5_chunk, u_ref[...], xs_ref[0:bl], xs_ref[bl:2 * bl], *[t[...] for t in tabs])
        grads = vjp((dy_ref[...], dxr_scr[...], dxi_scr[...]))
        du_ref[...] = grads[0].astype(BF16)
        dxr_scr[...] = grads[1]
        dxi_scr[...] = grads[2]
        for t, g in zip(dtabs, grads[3:]):
            t[...] += g

    blk = lambda gb, r: (0, nc - 1 - r, gb)
    tab_shapes = [SDS(t.shape, F32) for t in tables] + [SDS(dsk.shape, F32)]
    return pl.pallas_call(
        body, grid=(GB, nc),
        in_specs=[pl.BlockSpec((bl, S5_CH, LANES), blk)] + _s5_table_specs()
        + [pl.BlockSpec((None, 2 * bl, 1, 512), lambda gb, r: (gb * nc + nc - 1 - r, 0, 0, 0)), pl.BlockSpec((bl, S5_CH, LANES), blk)],
        out_specs=[pl.BlockSpec((bl, S5_CH, LANES), blk)] + _s5_table_specs(),
        out_shape=[SDS((bl, seq, S5W), BF16)] + tab_shapes,
        scratch_shapes=[pltpu.VMEM((bl, 1, 512), F32), pltpu.VMEM((bl, 1, 512), F32)], name="s5_bwd",
        compiler_params=_cp(2))(u, *tables, dsk, xs, dy)


def s5_tables_fwd(params):
    shapes = [SDS((GB, LANES, 512), F32)] * 4 + [SDS((S5_CH, S5N), F32)] * 6

    def body(*refs):
        for r, t in zip(refs[7:], s5_tables(*[p[...] for p in refs[:7]])):
            r[...] = t

    return pl.pallas_call(body, out_shape=shapes, name="s5_tables_fwd", compiler_params=_cp())(*params)


def s5_tables_bwd(params, dtables):
    def body(*refs):
        _, vjp = jax.vjp(s5_tables, *[p[...] for p in refs[:7]])
        for r, g in zip(refs[17:], vjp(tuple(t[...] for t in refs[7:17]))):
            r[...] = g

    return pl.pallas_call(body, out_shape=[SDS(p.shape, F32) for p in params], name="s5_tables_bwd",
                          compiler_params=_cp())(*params, *dtables)


def ada_fwd(c_all, w_loc, b_loc):
    def body(c_ref, w_ref, b_ref, o_ref):
        o_ref[...] = _dot(_silu(c_ref[...]), w_ref[...]) + b_ref[...]

    return pl.pallas_call(body, out_shape=SDS((c_all.shape[0], w_loc.shape[1]), F32), name="ada_fwd",
                          compiler_params=_cp())(c_all, w_loc, b_loc)


def ada_bwd(c_all, dmod_mine, dmod_all):
    def body(c_ref, dm_ref, da_ref, gw_ref, gb_ref):
        gw_ref[...] = _dot_tn(_silu(c_ref[...]), dm_ref[...])
        gb_ref[...] = jnp.sum(da_ref[...], axis=0, keepdims=True)

    return pl.pallas_call(body, out_shape=[SDS((D, dmod_mine.shape[1]), F32), SDS((1, dmod_all.shape[1]), F32)],
                          name="ada_bwd", compiler_params=_cp())(c_all, dmod_mine, dmod_all)


def adamw(name, parts, w, m, v):
    k_parts, rows, cols = parts.shape
    tr = _pick(rows, (256, 128, 64, 32, 16, 8))

    def body(p_ref, w_ref, m_ref, v_ref, g_ref, d_ref, mo_ref, vo_ref):
        g = p_ref[0].astype(F32)
        for k in range(1, k_parts):
            g = g + p_ref[k].astype(F32)
        _adam_store(g, w_ref, m_ref, v_ref, g_ref, d_ref, mo_ref, vo_ref)

    blk = pl.BlockSpec((tr, cols), lambda i: (i, 0))
    return pl.pallas_call(
        body, grid=(rows // tr,), in_specs=[pl.BlockSpec((k_parts, tr, cols), lambda i: (0, i, 0)), blk, blk, blk],
        out_specs=[blk] * 4, out_shape=[SDS((rows, cols), F32)] * 4, name=name, compiler_params=_cp(1))(parts, w, m, v)


def _adam_store(g, w_ref, m_ref, v_ref, g_ref, d_ref, mo_ref, vo_ref):
    m_new = ADAM_B1 * m_ref[...] + (1.0 - ADAM_B1) * g
    v_new = ADAM_B2 * v_ref[...] + (1.0 - ADAM_B2) * (g * g)
    m_hat = m_new / (1.0 - ADAM_B1 ** ADAM_STEP)
    v_hat = v_new / (1.0 - ADAM_B2 ** ADAM_STEP)
    g_ref[...] = g
    d_ref[...] = -ADAM_LR * (m_hat / (jnp.sqrt(v_hat) + ADAM_EPS) + ADAM_WD * w_ref[...])
    mo_ref[...] = m_new
    vo_ref[...] = v_new


def adamw_t(name, parts, w, m, v):
    k_parts, r, c = parts.shape
    tc = _pick(c, (256, 128))

    def body(p_ref, w_ref, m_ref, v_ref, g_ref, d_ref, mo_ref, vo_ref):
        gt = p_ref[0].astype(F32)
        for k in range(1, k_parts):
            gt = gt + p_ref[k].astype(F32)
        _adam_store(gt.T, w_ref, m_ref, v_ref, g_ref, d_ref, mo_ref, vo_ref)

    blk = pl.BlockSpec((tc, r), lambda j: (j, 0))
    return pl.pallas_call(
        body, grid=(c // tc,), in_specs=[pl.BlockSpec((k_parts, r, tc), lambda j: (0, 0, j)), blk, blk, blk],
        out_specs=[blk] * 4, out_shape=[SDS((c, r), F32)] * 4, name=name, compiler_params=_cp(1))(parts, w, m, v)


def _comm_scratch():
    return [pltpu.SemaphoreType.DMA((7,)), pltpu.SemaphoreType.DMA((7,)), pltpu.SemaphoreType.DMA]


HBM_SPEC = pl.BlockSpec(memory_space=pl.ANY)


def _gather_phases(x_ref, out_ref, send_sems, recv_sems, local_sem):
    mx, my, mc = lax.axis_index("x"), lax.axis_index("y"), lax.axis_index("c")
    me, sibling = (mx, my, mc), (mx, my, 1 - mc)
    chips = [(1 - mx, my), (mx, 1 - my), (1 - mx, 1 - my)]

    def slot(px, py, pc):
        return out_ref.at[4 * px + 2 * py + pc]

    def copy(k, block, to, src=None):
        return pltpu.make_async_remote_copy(
            src_ref=slot(*block) if src is None else src, dst_ref=slot(*block), send_sem=send_sems.at[k],
            recv_sem=recv_sems.at[k], device_id=to, device_id_type=pl.DeviceIdType.MESH)

    def first():
        return [copy(0, me, sibling, src=x_ref)] + [copy(1 + j, me, (*chip, mc), src=x_ref) for j, chip in enumerate(chips)]

    def passed():
        return [copy(4 + j, (*chip, mc), sibling) for j, chip in enumerate(chips)]

    def start():
        pltpu.make_async_copy(x_ref, slot(*me), local_sem).start()
        for cp in first():
            cp.start()

    def forward():
        for j, chip in enumerate(chips):
            copy(1 + j, (*chip, mc), me).wait_recv()
            passed()[j].start()

    def finish():
        copy(0, sibling, me).wait_recv()
        for j, chip in enumerate(chips):
            copy(4 + j, (*chip, 1 - mc), me).wait_recv()
        for cp in first() + passed():
            cp.wait_send()
        pltpu.make_async_copy(x_ref, slot(*me), local_sem).wait()

    return start, forward, finish


def _exchange_phases(x_ref, out_ref, send_sems, recv_sems, local_sem):
    mx, my, mc = lax.axis_index("x"), lax.axis_index("y"), lax.axis_index("c")
    me = 4 * mx + 2 * my + mc

    def peer(k):
        return mx ^ (k >> 2), my ^ ((k >> 1) & 1), mc ^ (k & 1)

    def sends():
        out = []
        for k in range(1, NDEV):
            px, py, pc = peer(k)
            out.append(pltpu.make_async_remote_copy(
                src_ref=x_ref.at[4 * px + 2 * py + pc], dst_ref=out_ref.at[me], send_sem=send_sems.at[k - 1],
                recv_sem=recv_sems.at[k - 1], device_id=(px, py, pc), device_id_type=pl.DeviceIdType.MESH))
        return out

    def start():
        pltpu.make_async_copy(x_ref.at[me], out_ref.at[me], local_sem).start()
        for cp in sends():
            cp.start()

    def finish():
        for k in range(1, NDEV):
            px, py, pc = peer(k)
            pltpu.make_async_remote_copy(
                src_ref=x_ref.at[me], dst_ref=out_ref.at[4 * px + 2 * py + pc], send_sem=send_sems.at[k - 1],
                recv_sem=recv_sems.at[k - 1], device_id=(px, py, pc), device_id_type=pl.DeviceIdType.MESH).wait_recv()
        for cp in sends():
            cp.wait_send()
        pltpu.make_async_copy(x_ref.at[me], out_ref.at[me], local_sem).wait()

    return start, finish


def all_gather(name, x):
    def body(x_ref, out_ref, send_sems, recv_sems, local_sem):
        for phase in _gather_phases(x_ref, out_ref, send_sems, recv_sems, local_sem):
            phase()

    return pl.pallas_call(body, out_shape=SDS((NDEV,) + x.shape, x.dtype), in_specs=[HBM_SPEC], out_specs=HBM_SPEC,
                          scratch_shapes=_comm_scratch(), name=name)(x)


def all_gather_pair(name, x1, x2):
    def body(x1_ref, x2_ref, o1_ref, o2_ref, *sems):
        first = _gather_phases(x1_ref, o1_ref, *sems[:3])
        second = _gather_phases(x2_ref, o2_ref, *sems[3:])
        for phase1, phase2 in zip(first, second):
            phase1()
            phase2()

    return pl.pallas_call(
        body, out_shape=[SDS((NDEV,) + x1.shape, x1.dtype), SDS((NDEV,) + x2.shape, x2.dtype)], in_specs=[HBM_SPEC] * 2,
        out_specs=[HBM_SPEC] * 2, scratch_shapes=_comm_scratch() + _comm_scratch(), name=name)(x1, x2)


def reduce_scatter_two_level(name, x):
    _, rows, cols = x.shape
    nchip = NDEV // 2

    def body(x_ref, out_ref, own_buf, sib_buf, sum_buf, send_sems, recv_sems, local_sems):
        mx, my, mc = lax.axis_index("x"), lax.axis_index("y"), lax.axis_index("c")
        chip = 2 * mx + my
        to_sibling = pltpu.make_async_remote_copy(
            src_ref=x_ref.at[:, 1 - mc], dst_ref=sib_buf, send_sem=send_sems.at[0], recv_sem=recv_sems.at[0],
            device_id=(mx, my, 1 - mc), device_id_type=pl.DeviceIdType.MESH)
        mine = pltpu.make_async_copy(x_ref.at[:, mc], own_buf, local_sems.at[0])
        to_sibling.start()
        mine.start()
        mine.wait()
        to_sibling.wait_recv()
        for j in range(nchip):
            sum_buf[j] = (own_buf[j].astype(F32) + sib_buf[j].astype(F32)).astype(x.dtype)
        keep = pltpu.make_async_copy(sum_buf.at[chip], out_ref.at[chip], local_sems.at[1])
        keep.start()
        sends = []
        for k in range(1, nchip):
            px, py = mx ^ (k >> 1), my ^ (k & 1)
            sends.append(pltpu.make_async_remote_copy(
                src_ref=sum_buf.at[2 * px + py], dst_ref=out_ref.at[chip], send_sem=send_sems.at[k],
                recv_sem=recv_sems.at[k], device_id=(px, py, mc), device_id_type=pl.DeviceIdType.MESH))
        for cp in sends:
            cp.start()
        for k in range(1, nchip):
            px, py = mx ^ (k >> 1), my ^ (k & 1)
            pltpu.make_async_remote_copy(
                src_ref=sum_buf.at[chip], dst_ref=out_ref.at[2 * px + py], send_sem=send_sems.at[k],
                recv_sem=recv_sems.at[k], device_id=(px, py, mc), device_id_type=pl.DeviceIdType.MESH).wait_recv()
        for cp in sends:
            cp.wait_send()
        to_sibling.wait_send()
        keep.wait()

    buf = pltpu.VMEM((nchip, rows, cols), x.dtype)
    return pl.pallas_call(
        body, out_shape=SDS((nchip, rows, cols), x.dtype), in_specs=[HBM_SPEC], out_specs=HBM_SPEC,
        scratch_shapes=[buf, buf, buf, pltpu.SemaphoreType.DMA((nchip,)), pltpu.SemaphoreType.DMA((nchip,)),
                        pltpu.SemaphoreType.DMA((2,))],
        name=name, compiler_params=_cp())(x.reshape(nchip, 2, rows, cols))


def _pack(arrs, dtype, row_mult=8):
    segs = []
    for a in arrs:
        flat = a.reshape(-1).astype(dtype)
        segs.append(jnp.pad(flat, (0, (-flat.shape[0]) % ROW)))
    flat = jnp.concatenate(segs)
    flat = jnp.pad(flat, (0, (-flat.shape[0]) % (ROW * row_mult)))
    return flat.reshape(-1, ROW)


def _unpack(buf, shapes):
    flat = buf.reshape(-1)
    out, off = [], 0
    for s in shapes:
        n = math.prod(s)
        out.append(flat[off:off + n].reshape(s))
        off += n + (-n) % ROW
    return out


def _pack_rows(arrs, axis):
    padded = []
    for t in arrs:
        pad = [(0, 0)] * t.ndim
        pad[axis] = (0, _tile_rows(t.shape[axis]) - t.shape[axis])
        padded.append(jnp.pad(t, pad))
    return jnp.concatenate(padded, axis=axis)


def _tile_rows(r):
    return r + (-r) % BF16_TILE_ROWS


def _unpack8(buf, shapes):
    flat = buf.reshape(NDEV, -1)
    out, off = [], 0
    for s in shapes:
        n = math.prod(s)
        out.append(flat[:, off:off + n].reshape((NDEV,) + tuple(s)))
        off += n + (-n) % ROW
    return out


def kernel(x, c, w_ada, b_ada, g_ffn1, w1_ffn1, w3_ffn1, w2_ffn1, g_mix, w_in, conv_qkv, a_log, dt_bias, g_onorm, lam_re, lam_im, log_step, b_re, b_im, c_re, c_im, d_skip, w_glu, b_glu, w_proj_a, w_proj_b, w_out, g_ffn2, w1_ffn2, w3_ffn2, w2_ffn2, g_final, loss_target, m_w_ada, m_b_ada, m_g_ffn1, m_w1_ffn1, m_w3_ffn1, m_w2_ffn1, m_g_mix, m_w_in, m_conv_qkv, m_a_log, m_dt_bias, m_g_onorm, m_lam_re, m_lam_im, m_log_step, m_b_re, m_b_im, m_c_re, m_c_im, m_d_skip, m_w_glu, m_b_glu, m_w_proj_a, m_w_proj_b, m_w_out, m_g_ffn2, m_w1_ffn2, m_w3_ffn2, m_w2_ffn2, m_g_final, v_w_ada, v_b_ada, v_g_ffn1, v_w1_ffn1, v_w3_ffn1, v_w2_ffn1, v_g_mix, v_w_in, v_conv_qkv, v_a_log, v_dt_bias, v_g_onorm, v_lam_re, v_lam_im, v_log_step, v_b_re, v_b_im, v_c_re, v_c_im, v_d_skip, v_w_glu, v_b_glu, v_w_proj_a, v_w_proj_b, v_w_out, v_g_ffn2, v_w1_ffn2, v_w3_ffn2, v_w2_ffn2, v_g_final):
    a = dict(locals())
    bl, seq, _ = x.shape
    t_rows = bl * seq
    me = 4 * lax.axis_index("x") + 2 * lax.axis_index("y") + lax.axis_index("c")
    tm_ew = _pick(seq, (256, 128, 64))

    loc = {n: (a[n][0].T if n in COL_SHARDED else a[n][0]) for n in RS_WEIGHTS}
    wfull, gw, res = {}, {}, {}

    def pack_local(names):
        return _pack_rows([loc[n].astype(BF16).reshape(-1, ROW) for n in names], 0)

    def unpack_full(buf, names):
        r0 = 0
        for n in names:
            r = loc[n].size // ROW
            wfull[n] = buf[:, r0:r0 + r, :].reshape(-1, loc[n].shape[1])
            r0 += _tile_rows(r)

    def pack_grads(names):
        return _pack_rows([gw[n].astype(BF16).reshape(NDEV, -1, ROW) for n in names], 1)

    def update(buf, names):
        r0 = 0
        for n in names:
            r = loc[n].size // ROW
            parts = buf[:, r0:r0 + r, :].reshape((buf.shape[0],) + loc[n].shape)
            r0 += _tile_rows(r)
            step = adamw_t if n in COL_SHARDED else adamw
            out = step("adamw_" + n, parts, a[n][0], a["m_" + n][0], a["v_" + n][0])
            for kind, t in zip(("grad", "delta", "new_m", "new_v"), out):
                res[kind + "_" + n] = t[None]

    sm, wg_ffn1 = all_gather_pair("gather_inputs", _pack([c, conv_qkv[0]], F32), pack_local(G_FFN1))
    c_loc, conv_loc = _unpack8(sm, [c.shape, conv_qkv.shape[1:]])
    c_all = c_loc.reshape(NDEV * bl, D)
    conv_full = conv_loc.transpose(1, 0, 2).reshape(CONVW, 3 * DNW)

    n_ada = w_ada.shape[2]
    mod_part = ada_fwd(c_all, w_ada[0], lax.dynamic_slice(b_ada, (0, me * n_ada), (1, n_ada)))
    mod_all = all_gather("gather_mod", mod_part).transpose(1, 0, 2).reshape(NDEV * bl, 9 * D)
    mod = lax.dynamic_slice(mod_all, (me * bl, 0), (bl, 9 * D)).reshape(bl, 9, D)
    mods = [mod[:, k:k + 1, :] for k in range(9)]

    h0 = x.reshape(t_rows, D)
    h1, f1, u1, pa1, pb1, wg_rest = ffn_fwd("ffn1_fwd", h0, mod[:, 0:3, :], g_ffn1, wg_ffn1, wg_ffn1, wg_ffn1, seq,
                                            gather=pack_local(G_MIX))
    unpack_full(wg_rest, G_MIX)
    win = wfull['w_in']
    o_small, o_s5, o_gate = 4 * DNW, 4 * DNW + 2 * NH, 4 * DNW + 2 * NH + S5W
    w_dn, w_small = win[:o_small], jnp.pad(win[o_small:o_s5], ((0, LANES - 2 * NH), (0, 0)))
    w_s5, w_gate = win[o_s5:o_gate], win[o_gate:]
    w_pieces = [w_dn, w_small, w_s5, w_gate]
    u2, p_dn, p_small, p_s5, p_gate = mix_in_fwd(h1, mods[3], mods[4], g_mix, w_pieces, seq)

    conv8 = jnp.pad(conv_full, ((0, 8 - CONVW), (0, 0)))
    alp = jnp.pad(a_log, ((0, 0), (NH, LANES - 2 * NH)))
    dtp = jnp.pad(dt_bias, ((0, 0), (NH, LANES - 2 * NH)))
    nb_dn = DN_ROWS if bl % DN_ROWS == 0 else 1
    p_dn3, p_small3 = p_dn.reshape(bl, seq, 4 * DNW), p_small.reshape(bl, seq, LANES)
    qkv3 = dn_prep_fwd(p_dn3, conv8)
    o_pre3, sprev, tinv, wg_ffn2 = deltanet_fwd(qkv3, p_small3, alp, dtp, nb_dn, gather=pack_local(G_FFN2))
    o_pre = o_pre3.reshape(t_rows, DNW)
    z_raw = p_dn[:, 3 * DNW:]

    s5_params = [lam_re.reshape(1, S5N), lam_im.reshape(1, S5N), log_step,
                 b_re[0].transpose(2, 0, 1).reshape(S5C, S5N), b_im[0].transpose(2, 0, 1).reshape(S5C, S5N),
                 c_re[0].transpose(1, 0, 2).reshape(S5C, S5N), c_im[0].transpose(1, 0, 2).reshape(S5C, S5N)]
    tables = s5_tables_fwd(s5_params)
    p_s53 = p_s5.reshape(bl, seq, S5W)
    y_s53, xs = s5_fwd(p_s53, tables, d_skip)
    y_s5 = y_s53.reshape(t_rows, S5W)
    tail_in = [o_pre, z_raw, y_s5, p_gate]
    tail_w = [g_onorm, wfull['w_glu'], b_glu, wfull['w_proj_a'], wfull['w_proj_b']]
    (merged,) = ew_call("mix_tail", fn_mix_tail, tail_in, [], tail_w, [(D, BF16)], tm_ew, seq)
    mo, h2 = mix_out_fwd(merged, wfull['w_out'], h1, mods[5], seq)
    dh3, f3, u3, pa3, pb3, dg_final, loss_part = ffn_fwd(
        "ffn2_fwd", h2, mod[:, 6:9, :], g_ffn2, wg_ffn2, wg_ffn2, wg_ffn2, seq,
        loss_head=(loss_target.reshape(t_rows, D), g_final.reshape(1, D)))


    dh2, a3, d1_3, d3_3, df3, dmod_c, dg_ffn2 = ffn_bwd("ffn2_bwd", dh3, h2, f3, pa3, pb3, mod[:, 6:9, :], g_ffn2, wg_ffn2,
                                                   wg_ffn2, wg_ffn2, seq)
    gw['w1_ffn2'] = mm_tn("gw1_ffn2", d1_3, u3)
    gw['w3_ffn2'] = mm_tn("gw3_ffn2", d3_3, u3)
    gw['w2_ffn2'] = mm_tn("gw2_ffn2", a3, df3)

    dmo, d_merged, dgt2 = mix_out_bwd(dh2, mo, wfull['w_out'], mods[5], seq)
    gw['w_out'] = mm_tn("gw_out", merged, dmo)
    (d_opre, d_z, d_ys5, d_gate), _, tail_gw = ew_vjp_call(
        "mix_tail_bwd", fn_mix_tail, tail_in, [], tail_w, [d_merged], [(0, F32), (1, F32), (2, F32), (3, BF16)],
        _pick(seq, (512, 256, 128, 64)), seq)
    dg_onorm, gw['w_glu'], dg_bglu, gw['w_proj_a'], gw['w_proj_b'] = tail_gw
    d_qkv3, d_psmall3, d_alp, d_dtp, rs_ffn2 = deltanet_bwd(
        qkv3, p_small3, alp, dtp, sprev, tinv, d_opre.reshape(bl, seq, DNW), nb_dn, exchange=pack_grads(G_FFN2))
    d_pdn3, d_conv8 = dn_prep_bwd(p_dn3, conv8, d_qkv3, d_z.reshape(bl, seq, DNW))
    d_pdn, d_psmall = d_pdn3.reshape(t_rows, 4 * DNW), d_psmall3.reshape(t_rows, LANES)

    s5_out = s5_bwd(p_s53, tables, d_skip, xs, d_ys5.reshape(bl, seq, S5W))
    d_ps5, d_tables, dg_dskip = s5_out[0].reshape(t_rows, S5W), s5_out[1:11], s5_out[11]
    d_s5p = s5_tables_bwd(s5_params, d_tables)

    gw['w_in'] = jnp.concatenate([mm_tn("gw_dn", d_pdn, u2), mm_tn("gw_small", d_psmall, u2)[:2 * NH],
                                  mm_tn("gw_s5", d_ps5, u2), mm_tn("gw_gate", d_gate, u2)], axis=0)
    dh1, dsh2, dsc2, dg_mix = mix_in_bwd([d_pdn, d_psmall, d_ps5, d_gate], w_pieces, h1, mods[3], mods[4], g_mix, dh2, seq)

    dh0, a1, d1_1, d3_1, df1, dmod_a, dg_ffn1, rs_mix = ffn_bwd(
        "ffn1_bwd", dh1, h0, f1, pa1, pb1, mod[:, 0:3, :], g_ffn1, wg_ffn1, wg_ffn1, wg_ffn1, seq,
        exchange=pack_grads(G_MIX))
    dmod_mine = jnp.concatenate([dmod_a, dsh2, dsc2, dgt2, dmod_c], axis=1).reshape(bl, 9 * D)
    small_grads = {
        'g_ffn1': dg_ffn1, 'g_mix': dg_mix, 'a_log': d_alp[:, NH:2 * NH], 'dt_bias': d_dtp[:, NH:2 * NH],
        'g_onorm': dg_onorm, 'lam_re': d_s5p[0].reshape(1, S5G, S5P), 'lam_im': d_s5p[1].reshape(1, S5G, S5P),
        'log_step': d_s5p[2],
        'b_re': d_s5p[3].reshape(S5C, S5G, S5P).transpose(1, 2, 0)[None],
        'b_im': d_s5p[4].reshape(S5C, S5G, S5P).transpose(1, 2, 0)[None],
        'c_re': d_s5p[5].reshape(S5C, S5G, S5P).transpose(1, 0, 2)[None],
        'c_im': d_s5p[6].reshape(S5C, S5G, S5P).transpose(1, 0, 2)[None],
        'd_skip': dg_dskip, 'b_glu': dg_bglu, 'g_ffn2': dg_ffn2, 'g_final': dg_final.reshape(D)}
    small_shapes = [a[n].shape for n in SMALL]
    small_pack = _pack([small_grads[n] for n in SMALL] + [loss_part], F32)
    n_small = small_pack.shape[0]
    small_buf = jnp.concatenate([small_pack, _pack([dmod_mine, d_conv8[:CONVW]], F32)], axis=0)

    gw['w1_ffn1'], sg = mm_tn("gw1_ffn1", d1_1, u1, gather=small_buf)
    gw['w3_ffn1'], rs_w1 = mm_tn("gw3_ffn1", d3_1, u1, exchange=pack_grads(['w1_ffn1']))
    gw['w2_ffn1'], rs_w3 = mm_tn("gw2_ffn1", a1, df1, exchange=pack_grads(['w3_ffn1']))
    rs_w2 = reduce_scatter_two_level("scatter_w2_ffn1", pack_grads(['w2_ffn1']))

    update(rs_ffn2, G_FFN2)
    update(rs_mix, G_MIX)
    update(rs_w1, ['w1_ffn1'])
    update(rs_w3, ['w3_ffn1'])
    update(rs_w2, ['w2_ffn1'])
    pieces = _unpack8(sg[:, n_small:, :], [dmod_mine.shape, (CONVW, 3 * DNW)])
    dmod_all = pieces[0].reshape(NDEV * bl, 9 * D)
    g_wada, g_bada = ada_bwd(c_all, lax.dynamic_slice(dmod_all, (0, me * n_ada), (NDEV * bl, n_ada)), dmod_all)

    n_conv = conv_qkv.shape[2]
    conv_parts = lax.dynamic_slice(pieces[1], (0, 0, me * n_conv), (NDEV, CONVW, n_conv))
    conv_parts = jnp.pad(conv_parts.reshape(NDEV, 1, -1), ((0, 0), (0, 7), (0, 0)))
    pad8 = lambda t: jnp.pad(t.reshape(1, -1), ((0, 7), (0, 0)))
    conv_res = adamw("adamw_conv", conv_parts, pad8(conv_qkv), pad8(m_conv_qkv), pad8(v_conv_qkv))
    for kind, buf in zip(("grad", "delta", "new_m", "new_v"), conv_res):
        res[kind + "_conv_qkv"] = buf[0].reshape(conv_qkv.shape)

    no_param = jnp.zeros_like(loss_part)
    small_res = adamw("adamw_small", sg[:, :n_small, :],
                      *[_pack([a[p + n] for n in SMALL] + [no_param], F32) for p in ("", "m_", "v_")])
    for kind, buf in zip(("grad", "delta", "new_m", "new_v"), small_res):
        for n, t in zip(SMALL, _unpack(buf, small_shapes)):
            res[kind + "_" + n] = t
    loss = _unpack(small_res[0], small_shapes + [loss_part.shape])[-1][0, 0]

    for n, g in (("w_ada", g_wada), ("b_ada", g_bada)):
        shp = a[n].shape
        r2 = lambda t: t.reshape(-1, shp[-1]) if n == "w_ada" else pad8(t)
        out = adamw("adamw_" + n, r2(g)[None], r2(a[n]), r2(a["m_" + n]), r2(a["v_" + n]))
        for kind, buf in zip(("grad", "delta", "new_m", "new_v"), out):
            res[kind + "_" + n] = (buf if n == "w_ada" else buf[0:1]).reshape(shp)

    outs = [loss, dh0.reshape(x.shape)]
    for kind in ("grad", "delta", "new_m", "new_v"):
        outs += [res[kind + "_" + n] for n in WEIGHTS]
    return tuple(outs)
```

```python
import math

import jax
import jax.numpy as jnp
from jax import lax
from jax.experimental import pallas as pl
from jax.experimental.pallas import tpu as pltpu

F32 = jnp.float32
BF16 = jnp.bfloat16
HI = lax.Precision.HIGHEST
H3 = lax.Precision.HIGH
SDS = jax.ShapeDtypeStruct

D = 1024
FF = 2816
FFN_TF = FF
FFN_FWD_TM = 256
FFN_BWD_TM = 256
NH = 8
DH = 64
DNW = NH * DH
CONVW = 4
CH = 64
S5_CH = 128
ACC_LIMIT = 6 * 1024 * 1024
BF16_TILE_ROWS = 16
DN_ROWS = 4
S5W = 512
S5G = 32
S5P = 64
S5C = 16
S5N = S5G * S5P
GB = 4
NDEV = 8
EPS = 1e-6
LANES = 128
ROW = 1024
VMEM_LIMIT = 56 * 1024 * 1024

ADAM_LR, ADAM_B1, ADAM_B2, ADAM_EPS, ADAM_WD, ADAM_STEP = 0.001, 0.9, 0.999, 1e-08, 0.01, 10

WEIGHTS = ['w_ada', 'b_ada', 'g_ffn1', 'w1_ffn1', 'w3_ffn1', 'w2_ffn1', 'g_mix', 'w_in', 'conv_qkv', 'a_log',
           'dt_bias', 'g_onorm', 'lam_re', 'lam_im', 'log_step', 'b_re', 'b_im', 'c_re', 'c_im', 'd_skip', 'w_glu',
           'b_glu', 'w_proj_a', 'w_proj_b', 'w_out', 'g_ffn2', 'w1_ffn2', 'w3_ffn2', 'w2_ffn2', 'g_final']
RS_WEIGHTS = ['w1_ffn1', 'w3_ffn1', 'w2_ffn1', 'w_in', 'w_glu', 'w_proj_a', 'w_proj_b', 'w_out', 'w1_ffn2', 'w3_ffn2',
              'w2_ffn2']
COL_SHARDED = {'w1_ffn1', 'w3_ffn1', 'w_in', 'w_proj_a', 'w_proj_b', 'w1_ffn2', 'w3_ffn2'}
G_FFN1 = ['w1_ffn1', 'w3_ffn1', 'w2_ffn1']
G_MIX = ['w_in', 'w_glu', 'w_proj_a', 'w_proj_b', 'w_out']
G_FFN2 = ['w1_ffn2', 'w3_ffn2', 'w2_ffn2']
SMALL = ['g_ffn1', 'g_mix', 'a_log', 'dt_bias', 'g_onorm', 'lam_re', 'lam_im', 'log_step', 'b_re', 'b_im', 'c_re',
         'c_im', 'd_skip', 'b_glu', 'g_ffn2', 'g_final']


def _cp(n_grid=0):
    if n_grid:
        return pltpu.CompilerParams(vmem_limit_bytes=VMEM_LIMIT, dimension_semantics=("arbitrary",) * n_grid)
    return pltpu.CompilerParams(vmem_limit_bytes=VMEM_LIMIT)


def _dot(a, b):
    return jnp.dot(a.astype(BF16), b.astype(BF16), preferred_element_type=F32)


def _dot_nt(a, b):
    return lax.dot_general(a.astype(BF16), b.astype(BF16), (((1,), (1,)), ((), ())), preferred_element_type=F32)


def _dot_tn(a, b):
    return lax.dot_general(a.astype(BF16), b.astype(BF16), (((0,), (0,)), ((), ())), preferred_element_type=F32)


def _dot_hi(a, b):
    return jnp.dot(a, b, precision=HI, preferred_element_type=F32)


@jax.custom_vjp
def bdot(a, b):
    return _dot(a, b)


bdot.defvjp(lambda a, b: (_dot(a, b), (a, b)),
            lambda r, g: (_dot_nt(g, r[1]).astype(r[0].dtype), _dot_tn(r[0], g).astype(r[1].dtype)))


@jax.custom_vjp
def bdot_nt(a, b):
    return _dot_nt(a, b)


bdot_nt.defvjp(lambda a, b: (_dot_nt(a, b), (a, b)),
               lambda r, g: (_dot(g, r[1]).astype(r[0].dtype), _dot_tn(g, r[0]).astype(r[1].dtype)))


def _silu(x):
    return x * jax.nn.sigmoid(x)


def _iota2(shape, axis):
    return lax.broadcasted_iota(jnp.int32, shape, axis)


def normmod(h, g, sc, sh):
    y = h * lax.rsqrt(jnp.mean(h * h, axis=-1, keepdims=True) + EPS) * g
    return y * (1.0 + sc) + sh


def fn_merge(gate, ya, yb):
    return (jax.nn.sigmoid(gate[:, :D]) * ya + jax.nn.sigmoid(gate[:, D:]) * yb,)


def fn_glu(y, w, b):
    ge = jax.nn.gelu(y)
    return (ge * jax.nn.sigmoid(bdot(ge, w) + b),)


def fn_onorm(o, z, g_on):
    r = _iota2((DH, DNW), 0)
    c = _iota2((DH, DNW), 1)
    expand = (c % DH == r).astype(F32)
    r2 = _iota2((DNW, DNW), 0)
    c2 = _iota2((DNW, DNW), 1)
    avg = (r2 // DH == c2 // DH).astype(F32) * (1.0 / DH)
    ms = bdot(o * o, avg)
    return (o * lax.rsqrt(ms + EPS) * _dot_hi(g_on, expand) * _silu(z),)


def fn_mix_tail(o_pre, z, y_s5, gate, g_on, w_glu, b_glu, wa_t, wb_t):
    (oa,) = fn_onorm(o_pre, z, g_on)
    (ob,) = fn_glu(y_s5, w_glu, b_glu)
    return fn_merge(gate, bdot_nt(oa, wa_t), bdot_nt(ob, wb_t))


def gate_fn(small, alp, dtp):
    beta = jax.nn.sigmoid(small)
    la = -jnp.exp(alp) * jax.nn.softplus(small + dtp)
    tri = (_iota2((CH, CH), 0) >= _iota2((CH, CH), 1)).astype(F32)
    gc = _dot_hi(tri, la)
    gct = lax.dot_general(la, tri, (((0,), (1,)), ((), ())), precision=HI, preferred_element_type=F32)
    return beta, gc, gct


def _bdg(a, b, ca, cb, hi):
    if not hi:
        a, b = a.astype(BF16), b.astype(BF16)
    return lax.dot_general(a, b, (((ca,), (cb,)), ((0,), (0,))), precision=H3 if hi else None,
                           preferred_element_type=F32)


def _batched_matmuls(hi):
    nn_ = lambda a, b: _bdg(a, b, 2, 1, hi)
    nt_ = lambda a, b: _bdg(a, b, 2, 2, hi)
    tn_ = lambda a, b: _bdg(a, b, 1, 1, hi)
    nn = jax.custom_vjp(nn_)
    nn.defvjp(lambda a, b: (nn_(a, b), (a, b)), lambda r, g: (nt_(g, r[1]), tn_(r[0], g)))
    nt = jax.custom_vjp(nt_)
    nt.defvjp(lambda a, b: (nt_(a, b), (a, b)), lambda r, g: (nn_(g, r[1]), tn_(g, r[0])))
    tn = jax.custom_vjp(tn_)
    tn.defvjp(lambda a, b: (tn_(a, b), (a, b)), lambda r, g: (nt_(r[1], g), nn_(r[0], g)))
    return nn, nt, tn


bnn, bnt, btn = _batched_matmuls(False)
hnn, hnt, htn = _batched_matmuls(True)


def _unit_lower_inverse(a):
    r = _iota2((1, CH, CH), 1)
    c = _iota2((1, CH, CH), 2)
    eye = (r == c).astype(F32)
    d = jnp.where(r // 8 == c // 8, a, 0.0)
    inv = eye - d
    p = d
    for _ in range(2):
        p = hnn(p, p)
        inv = inv + hnn(inv, p)
    for blk in (16, 32, 64):
        off = jnp.where((r // blk == c // blk) & (r // (blk // 2) != c // (blk // 2)), a, 0.0)
        mm = hnn if blk == 16 else bnn
        inv = inv - mm(mm(inv, off), inv)
    return inv


@jax.custom_vjp
def _inverse_given(a, t):
    return t


_inverse_given.defvjp(lambda a, t: (t, t), lambda t, g: (-hnt(htn(t, g), t), jnp.zeros_like(t)))


def dn_prep(xc, w):
    t = xc.shape[0] - 8
    c = xc[5:5 + t] * w[0:1] + xc[6:6 + t] * w[1:2] + xc[7:7 + t] * w[2:3] + xc[8:8 + t] * w[3:4]
    act = _silu(c)
    q, k, v = act[:, :DNW], act[:, DNW:2 * DNW], act[:, 2 * DNW:]
    ones = (_iota2((DNW, DNW), 0) // DH == _iota2((DNW, DNW), 1) // DH).astype(F32)
    q = q * lax.rsqrt(bdot(q * q, ones) + EPS) * (DH ** -0.5)
    k = k * lax.rsqrt(bdot(k * k, ones) + EPS)
    return jnp.concatenate([q, k, v], axis=1)


def dn_chunk(q, k, v, b, g, gt, s_prev, t_saved=None):
    r = _iota2((1, CH, CH), 1)
    c = _iota2((1, CH, CH), 2)
    causal = r >= c
    dec = jnp.where(causal, jnp.exp(jnp.where(causal, g - gt, 0.0)), 0.0)
    kb = k * b
    qk = bnt(jnp.concatenate([q, kb], axis=1), k)
    attn = qk[:, :CH] * dec
    a = jnp.where(r > c, qk[:, CH:] * dec, 0.0)
    tinv = _unit_lower_inverse(a) if t_saved is None else _inverse_given(a, t_saved)
    eg = jnp.exp(g)
    uw = hnn(tinv, jnp.concatenate([v * b, kb * eg], axis=2))
    g_last = g[:, CH - 1:CH]
    ws = bnn(jnp.concatenate([uw[..., DH:], q * eg], axis=1), s_prev)
    v_new = uw[..., :DH] - ws[:, :CH]
    o = ws[:, CH:] + bnn(attn, v_new)
    s_new = s_prev * jnp.exp(g_last) + btn(k * jnp.exp(g_last - g), v_new)
    return o, s_new, tinv


def s5_chunk(u, xp_re, xp_im, bb_re, bb_im, cc_re, cc_im, p0r, p0i, p1r, p1i, pir, pii, dsk):
    nb, ch, _ = u.shape
    u2 = u.reshape(nb * ch, LANES)
    bu_re = bdot(u2, bb_re).reshape(nb, ch, 512)
    bu_im = bdot(u2, bb_im).reshape(nb, ch, 512)
    xt_re = pir * bu_re - pii * bu_im
    xt_im = pir * bu_im + pii * bu_re
    tri = jnp.broadcast_to((_iota2((1, ch, ch), 1) >= _iota2((1, ch, ch), 2)).astype(F32), (nb, ch, ch))
    cs_re = hnn(tri, xt_re)
    cs_im = hnn(tri, xt_im)
    x_re = p0r * cs_re - p0i * cs_im + p1r * xp_re - p1i * xp_im
    x_im = p0r * cs_im + p0i * cs_re + p1r * xp_im + p1i * xp_re
    y = bdot_nt(x_re.reshape(nb * ch, 512), cc_re) - bdot_nt(x_im.reshape(nb * ch, 512), cc_im) + dsk * u2
    return y.reshape(nb, ch, LANES), x_re[:, ch - 1:ch], x_im[:, ch - 1:ch]


def s5_tables(lam_re, lam_im, log_step, bre, bim, cre, cim):
    expand = (_iota2((S5G, S5N), 1) // S5P == _iota2((S5G, S5N), 0)).astype(F32)
    step = _dot_hi(jnp.exp(log_step), expand)
    lre = jnp.minimum(lam_re, -1e-4)
    lr = lre * step
    ang = lam_im * step
    mag = jnp.exp(lr)
    lb_re = mag * jnp.cos(ang)
    lb_im = mag * jnp.sin(ang)
    den = lre * lre + lam_im * lam_im
    coef_re = ((lb_re - 1.0) * lre + lb_im * lam_im) / den
    coef_im = (lb_im * lre - (lb_re - 1.0) * lam_im) / den
    bb_re = coef_re * bre - coef_im * bim
    bb_im = coef_re * bim + coef_im * bre
    j = _iota2((S5_CH, 1), 0).astype(F32)
    jc = j - S5_CH // 2
    e0 = jnp.exp(jc * lr)
    e1 = jnp.exp((j + 1.0) * lr)
    ei = jnp.exp(-jc * lr)
    mask = (_iota2((LANES, 512), 0) // S5C == _iota2((LANES, 512), 1) // S5P).astype(F32)

    def blocks(t):
        return jnp.concatenate([(jnp.tile(t[:, gb * 512:(gb + 1) * 512], (LANES // S5C, 1)) * mask)[None]
                                for gb in range(GB)], axis=0)

    return (blocks(bb_re), blocks(bb_im), blocks(cre), blocks(cim),
            e0 * jnp.cos(jc * ang), e0 * jnp.sin(jc * ang),
            e1 * jnp.cos((j + 1.0) * ang), e1 * jnp.sin((j + 1.0) * ang),
            ei * jnp.cos(jc * ang), -ei * jnp.sin(jc * ang))


def _row_specs(tiled, batch, bcast, tm, tpb):
    specs = [pl.BlockSpec((tm, a.shape[1]), lambda i: (i, 0)) for a in tiled]
    specs += [pl.BlockSpec((None,) + a.shape[1:], lambda i: (i // tpb, 0, 0)) for a in batch]
    specs += [pl.BlockSpec(a.shape, lambda i, nd=a.ndim: (0,) * nd) for a in bcast]
    return specs


def ew_call(name, fn, tiled, batch, bcast, outs, tm, seq):
    t_rows = tiled[0].shape[0]
    n_in = len(tiled) + len(batch) + len(bcast)

    def body(*refs):
        vals = [r[...].astype(F32) for r in refs[:n_in]]
        for r, o in zip(refs[n_in:], fn(*vals)):
            r[...] = o.astype(r.dtype)

    return pl.pallas_call(
        body, grid=(t_rows // tm,), in_specs=_row_specs(tiled, batch, bcast, tm, seq // tm),
        out_specs=[pl.BlockSpec((tm, w), lambda i: (i, 0)) for w, _ in outs],
        out_shape=[SDS((t_rows, w), dt) for w, dt in outs], name=name, compiler_params=_cp(1))(*tiled, *batch, *bcast)


def ew_vjp_call(name, fn, tiled, batch, bcast, cts, want, tm, seq, addend=None):
    t_rows = tiled[0].shape[0]
    tpb = seq // tm
    n_t, n_b, n_c = len(tiled), len(batch), len(bcast)
    n_in = n_t + n_b + n_c
    extra = [] if addend is None else [addend]

    def body(*refs):
        i = pl.program_id(0)
        vals = [r[...].astype(F32) for r in refs[:n_in]]
        ctv = tuple(r[...].astype(F32) for r in refs[n_in:n_in + len(cts)])
        outs = refs[n_in + len(cts) + len(extra):]
        _, vjp = jax.vjp(fn, *vals)
        grads = vjp(ctv)
        for k, (r, (idx, _)) in enumerate(zip(outs[:len(want)], want)):
            g = grads[idx]
            if k == 0 and extra:
                g = g + refs[n_in + len(cts)][...]
            r[...] = g.astype(r.dtype)
        for k in range(n_b):
            r, g = outs[len(want) + k], grads[n_t + k]

            @pl.when(i % tpb == 0)
            def _(r=r, g=g):
                r[...] = g

            @pl.when(i % tpb != 0)
            def _(r=r, g=g):
                r[...] += g
        for k in range(n_c):
            r, g = outs[len(want) + n_b + k], grads[n_t + n_b + k]

            @pl.when(i == 0)
            def _(r=r, g=g):
                r[...] = g

            @pl.when(i != 0)
            def _(r=r, g=g):
                r[...] += g

    out_specs = [pl.BlockSpec((tm, tiled[idx].shape[1]), lambda i: (i, 0)) for idx, _ in want]
    out_specs += [pl.BlockSpec((None,) + a.shape[1:], lambda i: (i // tpb, 0, 0)) for a in batch]
    out_specs += [pl.BlockSpec(a.shape, lambda i, nd=a.ndim: (0,) * nd) for a in bcast]
    out_shape = [SDS(tiled[idx].shape, dt) for idx, dt in want]
    out_shape += [SDS(a.shape, F32) for a in batch] + [SDS(a.shape, F32) for a in bcast]
    res = pl.pallas_call(
        body, grid=(t_rows // tm,),
        in_specs=_row_specs(tiled, batch, bcast, tm, tpb)
        + [pl.BlockSpec((tm, a.shape[1]), lambda i: (i, 0)) for a in list(cts) + extra],
        out_specs=out_specs, out_shape=out_shape, name=name, compiler_params=_cp(1))(*tiled, *batch, *bcast, *cts, *extra)
    return res[:len(want)], res[len(want):len(want) + n_b], res[len(want) + n_b:]


def _pick(n, cands):
    for c in cands:
        if n % c == 0:
            return c
    return n


def mm_tn(name, a, b, exchange=None, gather=None):
    t_rows, m = a.shape
    n = b.shape[1]
    tn = n if n <= 1024 else _pick(n, (1024, 512, 256, 128))
    tm = max([t for t in range(LANES, m + 1, LANES) if m % t == 0 and t * tn * 4 <= ACC_LIMIT] or [m])
    tk = _pick(t_rows, (2048, 1024, 512, 256, 128, 64))
    grid = (m // tm, n // tn, t_rows // tk)
    extra = [x for x in (exchange, gather) if x is not None]
    ne = len(extra)
    scratch = [_reduce_scatter_scratch(x) if x is exchange else _comm_scratch() for x in extra]

    def body(*refs):
        a_ref, b_ref = refs[:2]
        o_ref, acc = refs[2 + ne], refs[3 + 2 * ne]
        i, j, k = pl.program_id(0), pl.program_id(1), pl.program_id(2)
        first = (i == 0) & (j == 0) & (k == 0)
        second = (i == 0) & (j == 0) & (k == min(1, grid[2] - 1))
        middle = (i == grid[0] - 1) & (j == grid[1] - 1) & (k == grid[2] // 2)
        last = (i == grid[0] - 1) & (j == grid[1] - 1) & (k == grid[2] - 1)
        at_end = []
        s0 = 4 + 2 * ne
        for e, x in enumerate(extra):
            n_scr = len(scratch[e])
            comm_refs = (refs[2 + e], refs[3 + ne + e]) + tuple(refs[s0:s0 + n_scr])
            s0 += n_scr
            if x is exchange:
                start, combine, finish = _reduce_scatter_phases(*comm_refs)
                pl.when(first)(start)
                pl.when(second)(combine)
            else:
                start, forward, finish = _gather_phases(*comm_refs)
                pl.when(first)(start)
                pl.when(middle)(forward)
            at_end.append(finish)

        @pl.when(k == 0)
        def _():
            acc[...] = jnp.zeros_like(acc)

        acc[...] += _dot_tn(a_ref[...], b_ref[...])

        @pl.when(k == grid[2] - 1)
        def _():
            o_ref[...] = acc[...].astype(BF16)

        for phase in at_end:
            pl.when(last)(phase)

    res = pl.pallas_call(
        body, grid=grid,
        in_specs=[pl.BlockSpec((tk, tm), lambda i, j, k: (k, i)), pl.BlockSpec((tk, tn), lambda i, j, k: (k, j))]
        + [HBM_SPEC] * ne,
        out_specs=[pl.BlockSpec((tm, tn), lambda i, j, k: (i, j))] + [HBM_SPEC] * ne,
        out_shape=[SDS((m, n), BF16)]
        + [SDS(((NCHIP,) + x.shape[1:]) if x is exchange else (NDEV,) + x.shape, x.dtype) for x in extra],
        scratch_shapes=[pltpu.VMEM((tm, tn), F32)] + [s for group in scratch for s in group], name=name,
        compiler_params=_cp(3))(a, b, *[_by_chip(x) if x is exchange else x for x in extra])
    return res if extra else res[0]


def _ffn_weight_spec(w, k):
    assert FFN_TF == FF
    if w.ndim == 3:
        return pl.BlockSpec((NDEV, FF // NDEV, D), lambda i, j: (0, k, 0), pipeline_mode=pl.Buffered(1))
    return pl.BlockSpec((FF, D), lambda i, j: (0, 0), pipeline_mode=pl.Buffered(1))


def _ffn_weight(ref):
    return ref[...].reshape(FF, D)


def ffn_fwd(name, h, mod3, g, w1, w3, w2, seq, gather=None, loss_head=None):
    t_rows = h.shape[0]
    tm = _pick(seq, (FFN_FWD_TM, 128, 64))
    tf = FFN_TF
    tpb = seq // tm
    nf = FF // tf
    nt = t_rows // tm
    extra = [] if gather is None else [gather]
    head = [] if loss_head is None else list(loss_head)
    nh, ne = len(head), len(extra)

    def body(*refs):
        h_ref, mod_ref, g_ref, w1_ref, w3_ref, w2_ref = refs[:6]
        o0 = 6 + nh + ne
        ho_ref, f_ref, u_ref, h1_ref, h3_ref = refs[o0:o0 + 5]
        s0 = o0 + 5 + nh + ne
        acc = refs[s0]
        i, j = pl.program_id(0), pl.program_id(1)
        if extra:
            start, forward, finish = _gather_phases(refs[6 + nh], refs[o0 + 5 + nh], *refs[s0 + 1:s0 + 4])
            pl.when((i == 0) & (j == 0))(start)
            pl.when((i == nt // 2) & (j == 0))(forward)

        @pl.when(j == 0)
        def _():
            u_ref[...] = normmod(h_ref[...], g_ref[...], mod_ref[1:2, :], mod_ref[0:1, :]).astype(BF16)
            acc[...] = jnp.zeros_like(acc)

        u = u_ref[...]
        h1 = _dot_nt(u, _ffn_weight(w1_ref))
        h3 = _dot_nt(u, _ffn_weight(w3_ref))
        h1_ref[...] = h1.astype(BF16)
        h3_ref[...] = h3.astype(BF16)
        acc[...] += _dot(_silu(h1) * h3, _ffn_weight(w2_ref))

        @pl.when(j == nf - 1)
        def _():
            f_ref[...] = acc[...]
            h_out = h_ref[...] + 0.5 * mod_ref[2:3, :] * acc[...]
            if not head:
                ho_ref[...] = h_out
            else:
                t_ref, gf_ref, dg_ref, loss_ref = refs[6], refs[7], refs[o0 + 5], refs[o0 + 6]
                y, vjp = jax.vjp(lambda hh, gg: hh * lax.rsqrt(jnp.mean(hh * hh, axis=-1, keepdims=True) + EPS) * gg,
                                 h_out, gf_ref[...])
                e = y - t_ref[...]
                dh, dg = vjp(e * (1.0 / D))
                part = jnp.sum(jnp.sum(e * e, axis=1, keepdims=True), axis=0, keepdims=True) * (0.5 / D) \
                    + jnp.zeros((1, LANES), F32)
                ho_ref[...] = dh

                @pl.when(i == 0)
                def _():
                    dg_ref[...] = dg
                    loss_ref[...] = part

                @pl.when(i != 0)
                def _():
                    dg_ref[...] += dg
                    loss_ref[...] += part

        if extra:
            pl.when((i == nt - 1) & (j == nf - 1))(finish)

    row = lambda i, j: (i, 0)
    const = lambda i, j: (0, 0)
    head_in = [pl.BlockSpec((tm, D), row), pl.BlockSpec((1, D), const)] if head else []
    head_out = [pl.BlockSpec((1, D), const), pl.BlockSpec((1, LANES), const)] if head else []
    return pl.pallas_call(
        body, grid=(nt, nf),
        in_specs=[pl.BlockSpec((tm, D), row), pl.BlockSpec((None, 3, D), lambda i, j: (i // tpb, 0, 0)),
                  pl.BlockSpec((1, D), const)] + [_ffn_weight_spec(w, k) for k, w in enumerate((w1, w3, w2))]
        + head_in + [HBM_SPEC] * ne,
        out_specs=[pl.BlockSpec((tm, D), row), pl.BlockSpec((tm, D), row), pl.BlockSpec((tm, D), row),
                   pl.BlockSpec((tm, tf), lambda i, j: (i, j)), pl.BlockSpec((tm, tf), lambda i, j: (i, j))]
        + head_out + [HBM_SPEC] * ne,
        out_shape=[SDS((t_rows, D), F32), SDS((t_rows, D), F32), SDS((t_rows, D), BF16), SDS((t_rows, FF), BF16),
                   SDS((t_rows, FF), BF16)] + ([SDS((1, D), F32), SDS((1, LANES), F32)] if head else [])
        + [SDS((NDEV,) + x.shape, x.dtype) for x in extra],
        scratch_shapes=[pltpu.VMEM((tm, D), F32)] + (_comm_scratch() if extra else []), name=name,
        compiler_params=_cp(2))(h, mod3, g, w1, w3, w2, *head, *extra)


def ffn_bwd(name, dho, h, f_out, h1_in, h3_in, mod3, g, w1, w3, w2, seq, exchange=None):
    t_rows = h.shape[0]
    tm = _pick(seq, (FFN_BWD_TM, 128, 64))
    tf = FFN_TF
    tpb = seq // tm
    nf = FF // tf
    nt = t_rows // tm
    extra = [] if exchange is None else [exchange]

    def body(*refs):
        dho_ref, h_ref, f_ref, h1_ref, h3_ref, mod_ref, g_ref, w1_ref, w3_ref, w2_ref = refs[:10]
        dh_ref, a_ref, dh1_ref, dh3_ref, df_scr, dmod_ref, dg_ref = refs[10 + len(extra):17 + len(extra)]
        du_acc = refs[17 + 2 * len(extra)]
        i, j = pl.program_id(0), pl.program_id(1)
        if extra:
            start, finish = _exchange_phases(refs[10], refs[18], *refs[20:23])
            pl.when((i == 0) & (j == 0))(start)

        @pl.when(j == 0)
        def _():
            df_scr[...] = (0.5 * mod_ref[2:3, :] * dho_ref[...]).astype(BF16)
            du_acc[...] = jnp.zeros_like(du_acc)

        h1 = h1_ref[...].astype(F32)
        h3 = h3_ref[...].astype(F32)
        sg = jax.nn.sigmoid(h1)
        s = h1 * sg
        da = _dot_nt(df_scr[...], _ffn_weight(w2_ref))
        dh3 = (da * s).astype(BF16)
        dh1 = (da * h3 * (sg * (1.0 + h1 * (1.0 - sg)))).astype(BF16)
        a_ref[...] = (s * h3).astype(BF16)
        dh1_ref[...] = dh1
        dh3_ref[...] = dh3
        du_acc[...] += _dot(dh1, _ffn_weight(w1_ref)) + _dot(dh3, _ffn_weight(w3_ref))

        @pl.when(j == nf - 1)
        def _():
            _, vjp = jax.vjp(normmod, h_ref[...], g_ref[...], mod_ref[1:2, :], mod_ref[0:1, :])
            dh_n, dg, dsc, dsh = vjp(du_acc[...])
            dh_ref[...] = dho_ref[...] + dh_n
            dgt = jnp.sum(0.5 * dho_ref[...] * f_ref[...], axis=0, keepdims=True)
            dmod = jnp.concatenate([dsh, dsc, dgt], axis=0)

            @pl.when(i % tpb == 0)
            def _():
                dmod_ref[...] = dmod

            @pl.when(i % tpb != 0)
            def _():
                dmod_ref[...] += dmod

            @pl.when(i == 0)
            def _():
                dg_ref[...] = dg

            @pl.when(i != 0)
            def _():
                dg_ref[...] += dg

        if extra:
            pl.when((i == nt - 1) & (j == nf - 1))(finish)

    row = lambda i, j: (i, 0)
    col = lambda i, j: (i, j)
    return pl.pallas_call(
        body, grid=(nt, nf),
        in_specs=[pl.BlockSpec((tm, D), row), pl.BlockSpec((tm, D), row), pl.BlockSpec((tm, D), row),
                  pl.BlockSpec((tm, tf), col), pl.BlockSpec((tm, tf), col),
                  pl.BlockSpec((None, 3, D), lambda i, j: (i // tpb, 0, 0)),
                  pl.BlockSpec((1, D), lambda i, j: (0, 0))] + [_ffn_weight_spec(w, k) for k, w in enumerate((w1, w3, w2))]
        + [HBM_SPEC] * len(extra),
        out_specs=[pl.BlockSpec((tm, D), row), pl.BlockSpec((tm, tf), col), pl.BlockSpec((tm, tf), col),
                   pl.BlockSpec((tm, tf), col), pl.BlockSpec((tm, D), row),
                   pl.BlockSpec((None, 3, D), lambda i, j: (i // tpb, 0, 0)), pl.BlockSpec((1, D), lambda i, j: (0, 0))]
        + [HBM_SPEC] * len(extra),
        out_shape=[SDS((t_rows, D), F32), SDS((t_rows, FF), BF16), SDS((t_rows, FF), BF16), SDS((t_rows, FF), BF16),
                   SDS((t_rows, D), BF16), SDS(mod3.shape, F32), SDS((1, D), F32)] + [SDS(x.shape, x.dtype) for x in extra],
        scratch_shapes=[pltpu.VMEM((tm, D), F32)] + (_comm_scratch() if extra else []), name=name,
        compiler_params=_cp(2))(dho, h, f_out, h1_in, h3_in, mod3, g, w1, w3, w2, *extra)


def _resident(shape):
    return pl.BlockSpec(shape, lambda i: (0,) * len(shape), pipeline_mode=pl.Buffered(1))


def mix_in_fwd(h, sh, sc, g, ws, seq):
    t_rows = h.shape[0]
    tm = _pick(seq, (256, 128, 64))
    tpb = seq // tm
    nw = len(ws)

    def body(h_ref, sh_ref, sc_ref, g_ref, *rest):
        u = normmod(h_ref[...], g_ref[...], sc_ref[...], sh_ref[...]).astype(BF16)
        rest[nw][...] = u
        for w_ref, p_ref in zip(rest[:nw], rest[nw + 1:]):
            p_ref[...] = _dot_nt(u, w_ref[...])

    row = lambda i: (i, 0)
    batch = pl.BlockSpec((None, 1, D), lambda i: (i // tpb, 0, 0))
    return pl.pallas_call(
        body, grid=(t_rows // tm,),
        in_specs=[pl.BlockSpec((tm, D), row), batch, batch, pl.BlockSpec((1, D), lambda i: (0, 0))]
        + [_resident(w.shape) for w in ws],
        out_specs=[pl.BlockSpec((tm, D), row)] + [pl.BlockSpec((tm, w.shape[0]), row) for w in ws],
        out_shape=[SDS((t_rows, D), BF16)] + [SDS((t_rows, w.shape[0]), F32) for w in ws], name="mix_in_fwd",
        compiler_params=_cp(1))(h, sh, sc, g, *ws)


def mix_in_bwd(dps, ws, h, sh, sc, g, dh_add, seq):
    t_rows = h.shape[0]
    tm = _pick(seq, (256, 128, 64))
    tpb = seq // tm
    nw = len(ws)

    def body(*refs):
        h_ref, sh_ref, sc_ref, g_ref, add_ref, dh_ref, dsh_ref, dsc_ref, dg_ref = refs[2 * nw:]
        i = pl.program_id(0)
        du = _dot(refs[0][...], refs[nw][...])
        for k in range(1, nw):
            du = du + _dot(refs[k][...], refs[nw + k][...])
        _, vjp = jax.vjp(normmod, h_ref[...], g_ref[...], sc_ref[...], sh_ref[...])
        dh_n, dg, dsc, dsh = vjp(du)
        dh_ref[...] = add_ref[...] + dh_n

        @pl.when(i % tpb == 0)
        def _():
            dsh_ref[...] = dsh
            dsc_ref[...] = dsc

        @pl.when(i % tpb != 0)
        def _():
            dsh_ref[...] += dsh
            dsc_ref[...] += dsc

        @pl.when(i == 0)
        def _():
            dg_ref[...] = dg

        @pl.when(i != 0)
        def _():
            dg_ref[...] += dg

    row = lambda i: (i, 0)
    batch = pl.BlockSpec((None, 1, D), lambda i: (i // tpb, 0, 0))
    gain = pl.BlockSpec((1, D), lambda i: (0, 0))
    return pl.pallas_call(
        body, grid=(t_rows // tm,),
        in_specs=[pl.BlockSpec((tm, dp.shape[1]), row) for dp in dps] + [_resident(w.shape) for w in ws]
        + [pl.BlockSpec((tm, D), row), batch, batch, gain, pl.BlockSpec((tm, D), row)],
        out_specs=[pl.BlockSpec((tm, D), row), batch, batch, gain],
        out_shape=[SDS((t_rows, D), F32), SDS(sh.shape, F32), SDS(sc.shape, F32), SDS((1, D), F32)], name="mix_in_bwd",
        compiler_params=_cp(1))(*dps, *ws, h, sh, sc, g, dh_add)


def mix_out_fwd(merged, w_out, h_prev, gt, seq):
    t_rows = merged.shape[0]
    tm = _pick(seq, (256, 128, 64))
    tpb = seq // tm

    def body(m_ref, w_ref, h_ref, gt_ref, mo_ref, ho_ref):
        mo = _dot(m_ref[...], w_ref[...])
        mo_ref[...] = mo
        ho_ref[...] = h_ref[...] + gt_ref[...] * mo

    row = lambda i: (i, 0)
    return pl.pallas_call(
        body, grid=(t_rows // tm,),
        in_specs=[pl.BlockSpec((tm, D), row), _resident(w_out.shape), pl.BlockSpec((tm, D), row),
                  pl.BlockSpec((None, 1, D), lambda i: (i // tpb, 0, 0))],
        out_specs=[pl.BlockSpec((tm, D), row), pl.BlockSpec((tm, D), row)],
        out_shape=[SDS((t_rows, D), F32), SDS((t_rows, D), F32)], name="mix_out_fwd",
        compiler_params=_cp(1))(merged, w_out, h_prev, gt)


def mix_out_bwd(dh, mo, w_out, gt, seq):
    t_rows = dh.shape[0]
    tm = _pick(seq, (256, 128, 64))
    tpb = seq // tm

    def body(dh_ref, mo_ref, w_ref, gt_ref, dmo_ref, dm_ref, dgt_ref):
        i = pl.program_id(0)
        dmo = (gt_ref[...] * dh_ref[...]).astype(BF16)
        dmo_ref[...] = dmo
        dm_ref[...] = _dot_nt(dmo, w_ref[...])
        dgt = jnp.sum(dh_ref[...] * mo_ref[...], axis=0, keepdims=True)

        @pl.when(i % tpb == 0)
        def _():
            dgt_ref[...] = dgt

        @pl.when(i % tpb != 0)
        def _():
            dgt_ref[...] += dgt

    row = lambda i: (i, 0)
    batch = pl.BlockSpec((None, 1, D), lambda i: (i // tpb, 0, 0))
    return pl.pallas_call(
        body, grid=(t_rows // tm,),
        in_specs=[pl.BlockSpec((tm, D), row), pl.BlockSpec((tm, D), row), _resident(w_out.shape), batch],
        out_specs=[pl.BlockSpec((tm, D), row), pl.BlockSpec((tm, D), row), batch],
        out_shape=[SDS((t_rows, D), BF16), SDS((t_rows, D), F32), SDS(gt.shape, F32)], name="mix_out_bwd",
        compiler_params=_cp(1))(dh, mo, w_out, gt)


def _dn_cols(part, hd):
    return slice(part * DNW + hd * DH, part * DNW + (hd + 1) * DH)


def _qkv_stacks(qkv_ref, nb):
    pairs = [(b, hd) for b in range(nb) for hd in range(NH)]
    return [jnp.stack([qkv_ref[b, :, _dn_cols(part, hd)] for b, hd in pairs]) for part in range(3)]


def dn_prep_fwd(p_dn, conv8):
    bl, seq, _ = p_dn.shape
    tp = _pick(seq, (256, 128, 64))

    def body(raw_ref, halo_ref, conv_ref, o_ref):
        hm = (pl.program_id(1) > 0).astype(F32)
        o_ref[...] = dn_prep(jnp.concatenate([halo_ref[...] * hm, raw_ref[...]], axis=0), conv_ref[...])

    return pl.pallas_call(
        body, grid=(bl, seq // tp),
        in_specs=[pl.BlockSpec((None, tp, 3 * DNW), lambda b, i: (b, i, 0)),
                  pl.BlockSpec((None, 8, 3 * DNW), lambda b, i: (b, jnp.maximum(i * (tp // 8) - 1, 0), 0)),
                  pl.BlockSpec((8, 3 * DNW), lambda b, i: (0, 0))],
        out_specs=pl.BlockSpec((None, tp, 3 * DNW), lambda b, i: (b, i, 0)),
        out_shape=SDS((bl, seq, 3 * DNW), F32), name="dn_prep_fwd", compiler_params=_cp(2))(p_dn, p_dn, conv8)


def dn_prep_bwd(p_dn, conv8, d_qkv, d_z):
    bl, seq, _ = p_dn.shape
    tp = _pick(seq, (256, 128, 64))
    nt = seq // tp

    def body(raw_ref, halo_ref, conv_ref, dq_ref, dz_ref, draw_ref, dconv_ref, carry):
        b, r = pl.program_id(0), pl.program_id(1)

        @pl.when((b == 0) & (r == 0))
        def _():
            dconv_ref[...] = jnp.zeros_like(dconv_ref)

        @pl.when(r == 0)
        def _():
            carry[...] = jnp.zeros_like(carry)

        hm = (r < nt - 1).astype(F32)
        _, vjp = jax.vjp(dn_prep, jnp.concatenate([halo_ref[...] * hm, raw_ref[...]], axis=0), conv_ref[...])
        dxc, dw = vjp(dq_ref[...])
        tail = dxc[tp:tp + 8] + carry[...]
        draw_ref[:, 0:3 * DNW] = jnp.concatenate([dxc[8:tp], tail], axis=0).astype(BF16)
        draw_ref[:, 3 * DNW:4 * DNW] = dz_ref[...].astype(BF16)
        carry[...] = dxc[0:8] * hm
        dconv_ref[...] += dw

    blk = lambda b, r: (b, nt - 1 - r, 0)
    return pl.pallas_call(
        body, grid=(bl, nt),
        in_specs=[pl.BlockSpec((None, tp, 3 * DNW), blk),
                  pl.BlockSpec((None, 8, 3 * DNW), lambda b, r: (b, jnp.maximum((nt - 1 - r) * (tp // 8) - 1, 0), 0)),
                  pl.BlockSpec((8, 3 * DNW), lambda b, r: (0, 0)), pl.BlockSpec((None, tp, 3 * DNW), blk),
                  pl.BlockSpec((None, tp, DNW), blk)],
        out_specs=[pl.BlockSpec((None, tp, 4 * DNW), blk), pl.BlockSpec((8, 3 * DNW), lambda b, r: (0, 0))],
        out_shape=[SDS((bl, seq, 4 * DNW), BF16), SDS((8, 3 * DNW), F32)],
        scratch_shapes=[pltpu.VMEM((8, 3 * DNW), F32)], name="dn_prep_bwd", compiler_params=_cp(2))(p_dn, p_dn, conv8, d_qkv, d_z)


def _gate_stacks(gates, nb):
    pairs = [(b, hd) for b in range(nb) for hd in range(NH)]
    bs = jnp.stack([gates[b][0][:, hd:hd + 1] for b, hd in pairs])
    gs = jnp.stack([gates[b][1][:, NH + hd:NH + hd + 1] for b, hd in pairs])
    gts = jnp.stack([gates[b][2][NH + hd:NH + hd + 1, :] for b, hd in pairs])
    return bs, gs, gts


def deltanet_fwd(qkv, p_small, alp, dtp, nb, gather=None):
    bl, seq, _ = qkv.shape
    nc = seq // CH
    ng = nb * NH
    extra = [] if gather is None else [gather]

    def body(*refs):
        qkv_ref, small_ref, alp_ref, dtp_ref = refs[:4]
        o_ref, sprev_ref, tinv_ref = refs[4 + len(extra):7 + len(extra)]
        s_scr = refs[7 + 2 * len(extra)]
        bb, n = pl.program_id(0), pl.program_id(1)
        if extra:
            start, forward, finish = _gather_phases(refs[4], refs[8], *refs[10:13])
            pl.when((bb == 0) & (n == 0))(start)

        @pl.when(n == 0)
        def _():
            s_scr[...] = jnp.zeros_like(s_scr)

        gates = [gate_fn(small_ref[b], alp_ref[...], dtp_ref[...]) for b in range(nb)]
        s_prev = s_scr[...]
        o, s_new, tinv = dn_chunk(*_qkv_stacks(qkv_ref, nb), *_gate_stacks(gates, nb), s_prev)
        sprev_ref[...] = s_prev
        tinv_ref[...] = tinv
        s_scr[...] = s_new
        for b in range(nb):
            for hd in range(NH):
                o_ref[b, :, hd * DH:(hd + 1) * DH] = o[b * NH + hd]
        if extra:
            pl.when((bb == bl // nb - 1) & (n == nc // 2))(forward)
            pl.when((bb == bl // nb - 1) & (n == nc - 1))(finish)

    blk = lambda bb, n: (bb, n, 0)
    const = lambda bb, n: (0, 0)
    saved = pl.BlockSpec((None, ng, DH, DH), lambda bb, n: (bb * nc + n, 0, 0, 0))
    return pl.pallas_call(
        body, grid=(bl // nb, nc),
        in_specs=[pl.BlockSpec((nb, CH, 3 * DNW), blk), pl.BlockSpec((nb, CH, LANES), blk),
                  pl.BlockSpec((1, LANES), const), pl.BlockSpec((1, LANES), const)] + [HBM_SPEC] * len(extra),
        out_specs=[pl.BlockSpec((nb, CH, DNW), blk), saved, saved] + [HBM_SPEC] * len(extra),
        out_shape=[SDS((bl, seq, DNW), F32), SDS((bl // nb * nc, ng, DH, DH), F32), SDS((bl // nb * nc, ng, DH, DH), F32)]
        + [SDS((NDEV,) + x.shape, x.dtype) for x in extra],
        scratch_shapes=[pltpu.VMEM((ng, DH, DH), F32)] + (_comm_scratch() if extra else []), name="deltanet_fwd",
        compiler_params=_cp(2))(qkv, p_small, alp, dtp, *extra)


def deltanet_bwd(qkv, p_small, alp, dtp, sprev, tinv, d_o, nb, exchange=None):
    bl, seq, _ = qkv.shape
    nc = seq // CH
    ng = nb * NH
    extra = [] if exchange is None else [exchange]

    def body(*refs):
        qkv_ref, small_ref, alp_ref, dtp_ref, sprev_ref, tinv_ref, do_ref = refs[:7]
        dqkv_ref, dsmall_ref, dalp_ref, ddtp_ref = refs[7 + len(extra):11 + len(extra)]
        ds_scr = refs[11 + 2 * len(extra)]
        bb, r = pl.program_id(0), pl.program_id(1)
        if extra:
            start, finish = _exchange_phases(refs[7], refs[12], *refs[14:17])
            pl.when((bb == 0) & (r == 0))(start)

        @pl.when((bb == 0) & (r == 0))
        def _():
            dalp_ref[...] = jnp.zeros_like(dalp_ref)
            ddtp_ref[...] = jnp.zeros_like(ddtp_ref)

        @pl.when(r == 0)
        def _():
            ds_scr[...] = jnp.zeros_like(ds_scr)

        gates, gate_vjps = [], []
        for b in range(nb):
            out, gvjp = jax.vjp(gate_fn, small_ref[b], alp_ref[...], dtp_ref[...])
            gates.append(out)
            gate_vjps.append(gvjp)
        t_saved = tinv_ref[...]
        _, vjp = jax.vjp(lambda *args: dn_chunk(*args, t_saved)[:2], *_qkv_stacks(qkv_ref, nb), *_gate_stacks(gates, nb),
                         sprev_ref[...])
        d_out = jnp.stack([do_ref[b, :, hd * DH:(hd + 1) * DH] for b in range(nb) for hd in range(NH)])
        grads = vjp((d_out, ds_scr[...]))
        ds_scr[...] = grads[6]
        lane = _iota2((CH, LANES), 1)
        rowi = _iota2((LANES, CH), 0)
        for b in range(nb):
            d_beta = jnp.zeros((CH, LANES), F32)
            d_gc = jnp.zeros((CH, LANES), F32)
            d_gct = jnp.zeros((LANES, CH), F32)
            for hd in range(NH):
                i = b * NH + hd
                for part in range(3):
                    dqkv_ref[b, :, _dn_cols(part, hd)] = grads[part][i]
                d_beta = d_beta + jnp.where(lane == hd, grads[3][i], 0.0)
                d_gc = d_gc + jnp.where(lane == NH + hd, grads[4][i], 0.0)
                d_gct = d_gct + jnp.where(rowi == NH + hd, grads[5][i], 0.0)
            d_small, d_alp, d_dtp = gate_vjps[b]((d_beta, d_gc, d_gct))
            dsmall_ref[b] = d_small.astype(BF16)
            dalp_ref[...] += d_alp
            ddtp_ref[...] += d_dtp
        if extra:
            pl.when((bb == bl // nb - 1) & (r == nc - 1))(finish)

    blk = lambda bb, r: (bb, nc - 1 - r, 0)
    const = lambda bb, r: (0, 0)
    saved = pl.BlockSpec((None, ng, DH, DH), lambda bb, r: (bb * nc + nc - 1 - r, 0, 0, 0))
    return pl.pallas_call(
        body, grid=(bl // nb, nc),
        in_specs=[pl.BlockSpec((nb, CH, 3 * DNW), blk), pl.BlockSpec((nb, CH, LANES), blk), pl.BlockSpec((1, LANES), const),
                  pl.BlockSpec((1, LANES), const), saved, saved, pl.BlockSpec((nb, CH, DNW), blk)] + [HBM_SPEC] * len(extra),
        out_specs=[pl.BlockSpec((nb, CH, 3 * DNW), blk), pl.BlockSpec((nb, CH, LANES), blk), pl.BlockSpec((1, LANES), const),
                   pl.BlockSpec((1, LANES), const)] + [HBM_SPEC] * len(extra),
        out_shape=[SDS((bl, seq, 3 * DNW), F32), SDS((bl, seq, LANES), BF16), SDS((1, LANES), F32), SDS((1, LANES), F32)]
        + [SDS(x.shape, x.dtype) for x in extra],
        scratch_shapes=[pltpu.VMEM((ng, DH, DH), F32)] + (_comm_scratch() if extra else []), name="deltanet_bwd",
        compiler_params=_cp(2))(qkv, p_small, alp, dtp, sprev, tinv, d_o, *extra)


def _s5_table_specs():
    tab3 = pl.BlockSpec((None, LANES, 512), lambda gb, n: (gb, 0, 0))
    tab2 = pl.BlockSpec((S5_CH, 512), lambda gb, n: (0, gb))
    return [tab3] * 4 + [tab2] * 6 + [pl.BlockSpec((1, LANES), lambda gb, n: (0, gb))]


def s5_fwd(u, tables, dsk):
    bl, seq, _ = u.shape
    nc = seq // S5_CH

    def body(u_ref, *rest):
        tabs, (y_ref, xs_ref, xr_scr, xi_scr) = rest[:11], rest[11:]

        @pl.when(pl.program_id(1) == 0)
        def _():
            xr_scr[...] = jnp.zeros_like(xr_scr)
            xi_scr[...] = jnp.zeros_like(xi_scr)

        xp_re, xp_im = xr_scr[...], xi_scr[...]
        xs_ref[0:bl] = xp_re
        xs_ref[bl:2 * bl] = xp_im
        y, xn_re, xn_im = s5_chunk(u_ref[...], xp_re, xp_im, *[t[...] for t in tabs])
        y_ref[...] = y
        xr_scr[...] = xn_re
        xi_scr[...] = xn_im

    blk = lambda gb, n: (0, n, gb)
    return pl.pallas_call(
        body, grid=(GB, nc), in_specs=[pl.BlockSpec((bl, S5_CH, LANES), blk)] + _s5_table_specs(),
        out_specs=[pl.BlockSpec((bl, S5_CH, LANES), blk),
                   pl.BlockSpec((None, 2 * bl, 1, 512), lambda gb, n: (gb * nc + n, 0, 0, 0))],
        out_shape=[SDS((bl, seq, S5W), F32), SDS((GB * nc, 2 * bl, 1, 512), F32)],
        scratch_shapes=[pltpu.VMEM((bl, 1, 512), F32), pltpu.VMEM((bl, 1, 512), F32)], name="s5_fwd",
        compiler_params=_cp(2))(u, *tables, dsk)


def s5_bwd(u, tables, dsk, xs, dy):
    bl, seq, _ = u.shape
    nc = seq // S5_CH

    def body(u_ref, *rest):
        tabs, xs_ref, dy_ref = rest[:11], rest[11], rest[12]
        du_ref, dtabs, dxr_scr, dxi_scr = rest[13], rest[14:25], rest[25], rest[26]
        r = pl.program_id(1)

        @pl.when(r == 0)
        def _():
            for t in dtabs:
                t[...] = jnp.zeros_like(t)
            dxr_scr[...] = jnp.zeros_like(dxr_scr)
            dxi_scr[...] = jnp.zeros_like(dxi_scr)

        _, vjp = jax.vjp(s5_chunk, u_ref[...], xs_ref[0:bl], xs_ref[bl:2 * bl], *[t[...] for t in tabs])
        grads = vjp((dy_ref[...], dxr_scr[...], dxi_scr[...]))
        du_ref[...] = grads[0].astype(BF16)
        dxr_scr[...] = grads[1]
        dxi_scr[...] = grads[2]
        for t, g in zip(dtabs, grads[3:]):
            t[...] += g

    blk = lambda gb, r: (0, nc - 1 - r, gb)
    tab_shapes = [SDS(t.shape, F32) for t in tables] + [SDS(dsk.shape, F32)]
    return pl.pallas_call(
        body, grid=(GB, nc),
        in_specs=[pl.BlockSpec((bl, S5_CH, LANES), blk)] + _s5_table_specs()
        + [pl.BlockSpec((None, 2 * bl, 1, 512), lambda gb, r: (gb * nc + nc - 1 - r, 0, 0, 0)), pl.BlockSpec((bl, S5_CH, LANES), blk)],
        out_specs=[pl.BlockSpec((bl, S5_CH, LANES), blk)] + _s5_table_specs(),
        out_shape=[SDS((bl, seq, S5W), BF16)] + tab_shapes,
        scratch_shapes=[pltpu.VMEM((bl, 1, 512), F32), pltpu.VMEM((bl, 1, 512), F32)], name="s5_bwd",
        compiler_params=_cp(2))(u, *tables, dsk, xs, dy)


def s5_tables_fwd(params):
    shapes = [SDS((GB, LANES, 512), F32)] * 4 + [SDS((S5_CH, S5N), F32)] * 6

    def body(*refs):
        for r, t in zip(refs[7:], s5_tables(*[p[...] for p in refs[:7]])):
            r[...] = t

    return pl.pallas_call(body, out_shape=shapes, name="s5_tables_fwd", compiler_params=_cp())(*params)


def s5_tables_bwd(params, dtables):
    def body(*refs):
        _, vjp = jax.vjp(s5_tables, *[p[...] for p in refs[:7]])
        for r, g in zip(refs[17:], vjp(tuple(t[...] for t in refs[7:17]))):
            r[...] = g

    return pl.pallas_call(body, out_shape=[SDS(p.shape, F32) for p in params], name="s5_tables_bwd",
                          compiler_params=_cp())(*params, *dtables)


def ada_fwd(c_all, w_loc, b_loc):
    def body(c_ref, w_ref, b_ref, o_ref):
        o_ref[...] = _dot(_silu(c_ref[...]), w_ref[...]) + b_ref[...]

    return pl.pallas_call(body, out_shape=SDS((c_all.shape[0], w_loc.shape[1]), F32), name="ada_fwd",
                          compiler_params=_cp())(c_all, w_loc, b_loc)


def ada_bwd(c_all, dmod_mine, dmod_all):
    def body(c_ref, dm_ref, da_ref, gw_ref, gb_ref):
        gw_ref[...] = _dot_tn(_silu(c_ref[...]), dm_ref[...])
        gb_ref[...] = jnp.sum(da_ref[...], axis=0, keepdims=True)

    return pl.pallas_call(body, out_shape=[SDS((D, dmod_mine.shape[1]), F32), SDS((1, dmod_all.shape[1]), F32)],
                          name="ada_bwd", compiler_params=_cp())(c_all, dmod_mine, dmod_all)


def adamw(name, parts, w, m, v):
    k_parts, rows, cols = parts.shape
    tr = _pick(rows, (256, 128, 64, 32, 16, 8))

    def body(p_ref, w_ref, m_ref, v_ref, g_ref, d_ref, mo_ref, vo_ref):
        g = p_ref[0].astype(F32)
        for k in range(1, k_parts):
            g = g + p_ref[k].astype(F32)
        _adam_store(g, w_ref, m_ref, v_ref, g_ref, d_ref, mo_ref, vo_ref)

    blk = pl.BlockSpec((tr, cols), lambda i: (i, 0))
    return pl.pallas_call(
        body, grid=(rows // tr,), in_specs=[pl.BlockSpec((k_parts, tr, cols), lambda i: (0, i, 0)), blk, blk, blk],
        out_specs=[blk] * 4, out_shape=[SDS((rows, cols), F32)] * 4, name=name, compiler_params=_cp(1))(parts, w, m, v)


def _adam_store(g, w_ref, m_ref, v_ref, g_ref, d_ref, mo_ref, vo_ref):
    m_new = ADAM_B1 * m_ref[...] + (1.0 - ADAM_B1) * g
    v_new = ADAM_B2 * v_ref[...] + (1.0 - ADAM_B2) * (g * g)
    m_hat = m_new / (1.0 - ADAM_B1 ** ADAM_STEP)
    v_hat = v_new / (1.0 - ADAM_B2 ** ADAM_STEP)
    g_ref[...] = g
    d_ref[...] = -ADAM_LR * (m_hat / (jnp.sqrt(v_hat) + ADAM_EPS) + ADAM_WD * w_ref[...])
    mo_ref[...] = m_new
    vo_ref[...] = v_new


def adamw_t(name, parts, w, m, v):
    k_parts, r, c = parts.shape
    tc = _pick(c, (256, 128))

    def body(p_ref, w_ref, m_ref, v_ref, g_ref, d_ref, mo_ref, vo_ref):
        gt = p_ref[0].astype(F32)
        for k in range(1, k_parts):
            gt = gt + p_ref[k].astype(F32)
        _adam_store(gt.T, w_ref, m_ref, v_ref, g_ref, d_ref, mo_ref, vo_ref)

    blk = pl.BlockSpec((tc, r), lambda j: (j, 0))
    return pl.pallas_call(
        body, grid=(c // tc,), in_specs=[pl.BlockSpec((k_parts, r, tc), lambda j: (0, 0, j)), blk, blk, blk],
        out_specs=[blk] * 4, out_shape=[SDS((c, r), F32)] * 4, name=name, compiler_params=_cp(1))(parts, w, m, v)


def _comm_scratch():
    return [pltpu.SemaphoreType.DMA((7,)), pltpu.SemaphoreType.DMA((7,)), pltpu.SemaphoreType.DMA]


HBM_SPEC = pl.BlockSpec(memory_space=pl.ANY)


def _gather_phases(x_ref, out_ref, send_sems, recv_sems, local_sem):
    mx, my, mc = lax.axis_index("x"), lax.axis_index("y"), lax.axis_index("c")
    me, sibling = (mx, my, mc), (mx, my, 1 - mc)
    chips = [(1 - mx, my), (mx, 1 - my), (1 - mx, 1 - my)]

    def slot(px, py, pc):
        return out_ref.at[4 * px + 2 * py + pc]

    def copy(k, block, to, src=None):
        return pltpu.make_async_remote_copy(
            src_ref=slot(*block) if src is None else src, dst_ref=slot(*block), send_sem=send_sems.at[k],
            recv_sem=recv_sems.at[k], device_id=to, device_id_type=pl.DeviceIdType.MESH)

    def first():
        return [copy(0, me, sibling, src=x_ref)] + [copy(1 + j, me, (*chip, mc), src=x_ref) for j, chip in enumerate(chips)]

    def passed():
        return [copy(4 + j, (*chip, mc), sibling) for j, chip in enumerate(chips)]

    def start():
        pltpu.make_async_copy(x_ref, slot(*me), local_sem).start()
        for cp in first():
            cp.start()

    def forward():
        for j, chip in enumerate(chips):
            copy(1 + j, (*chip, mc), me).wait_recv()
            passed()[j].start()

    def finish():
        copy(0, sibling, me).wait_recv()
        for j, chip in enumerate(chips):
            copy(4 + j, (*chip, 1 - mc), me).wait_recv()
        for cp in first() + passed():
            cp.wait_send()
        pltpu.make_async_copy(x_ref, slot(*me), local_sem).wait()

    return start, forward, finish


def _exchange_phases(x_ref, out_ref, send_sems, recv_sems, local_sem):
    mx, my, mc = lax.axis_index("x"), lax.axis_index("y"), lax.axis_index("c")
    me = 4 * mx + 2 * my + mc

    def peer(k):
        return mx ^ (k >> 2), my ^ ((k >> 1) & 1), mc ^ (k & 1)

    def sends():
        out = []
        for k in range(1, NDEV):
            px, py, pc = peer(k)
            out.append(pltpu.make_async_remote_copy(
                src_ref=x_ref.at[4 * px + 2 * py + pc], dst_ref=out_ref.at[me], send_sem=send_sems.at[k - 1],
                recv_sem=recv_sems.at[k - 1], device_id=(px, py, pc), device_id_type=pl.DeviceIdType.MESH))
        return out

    def start():
        pltpu.make_async_copy(x_ref.at[me], out_ref.at[me], local_sem).start()
        for cp in sends():
            cp.start()

    def finish():
        for k in range(1, NDEV):
            px, py, pc = peer(k)
            pltpu.make_async_remote_copy(
                src_ref=x_ref.at[me], dst_ref=out_ref.at[4 * px + 2 * py + pc], send_sem=send_sems.at[k - 1],
                recv_sem=recv_sems.at[k - 1], device_id=(px, py, pc), device_id_type=pl.DeviceIdType.MESH).wait_recv()
        for cp in sends():
            cp.wait_send()
        pltpu.make_async_copy(x_ref.at[me], out_ref.at[me], local_sem).wait()

    return start, finish


def all_gather(name, x):
    def body(x_ref, out_ref, send_sems, recv_sems, local_sem):
        for phase in _gather_phases(x_ref, out_ref, send_sems, recv_sems, local_sem):
            phase()

    return pl.pallas_call(body, out_shape=SDS((NDEV,) + x.shape, x.dtype), in_specs=[HBM_SPEC], out_specs=HBM_SPEC,
                          scratch_shapes=_comm_scratch(), name=name)(x)


def all_gather_pair(name, x1, x2):
    def body(x1_ref, x2_ref, o1_ref, o2_ref, *sems):
        first = _gather_phases(x1_ref, o1_ref, *sems[:3])
        second = _gather_phases(x2_ref, o2_ref, *sems[3:])
        for phase1, phase2 in zip(first, second):
            phase1()
            phase2()

    return pl.pallas_call(
        body, out_shape=[SDS((NDEV,) + x1.shape, x1.dtype), SDS((NDEV,) + x2.shape, x2.dtype)], in_specs=[HBM_SPEC] * 2,
        out_specs=[HBM_SPEC] * 2, scratch_shapes=_comm_scratch() + _comm_scratch(), name=name)(x1, x2)


def reduce_scatter_two_level(name, x):
    def body(*refs):
        for phase in _reduce_scatter_phases(*refs):
            phase()

    return pl.pallas_call(
        body, out_shape=SDS((NCHIP,) + x.shape[1:], x.dtype), in_specs=[HBM_SPEC], out_specs=HBM_SPEC,
        scratch_shapes=_reduce_scatter_scratch(x), name=name, compiler_params=_cp())(_by_chip(x))


NCHIP = NDEV // 2


def _by_chip(x):
    return x.reshape((NCHIP, 2) + x.shape[1:])


def _reduce_scatter_scratch(x):
    buf = pltpu.VMEM((NCHIP,) + x.shape[1:], x.dtype)
    return [buf, buf, buf, pltpu.SemaphoreType.DMA((NCHIP,)), pltpu.SemaphoreType.DMA((NCHIP,)),
            pltpu.SemaphoreType.DMA((2,))]


def _reduce_scatter_phases(x_ref, out_ref, own_buf, sib_buf, sum_buf, send_sems, recv_sems, local_sems):
    mx, my, mc = lax.axis_index("x"), lax.axis_index("y"), lax.axis_index("c")
    chip = 2 * mx + my

    def to_sibling():
        return pltpu.make_async_remote_copy(
            src_ref=x_ref.at[:, 1 - mc], dst_ref=sib_buf, send_sem=send_sems.at[0], recv_sem=recv_sems.at[0],
            device_id=(mx, my, 1 - mc), device_id_type=pl.DeviceIdType.MESH)

    def mine():
        return pltpu.make_async_copy(x_ref.at[:, mc], own_buf, local_sems.at[0])

    def keep():
        return pltpu.make_async_copy(sum_buf.at[chip], out_ref.at[chip], local_sems.at[1])

    def peer(k):
        return mx ^ (k >> 1), my ^ (k & 1)

    def sends():
        return [pltpu.make_async_remote_copy(
            src_ref=sum_buf.at[2 * peer(k)[0] + peer(k)[1]], dst_ref=out_ref.at[chip], send_sem=send_sems.at[k],
            recv_sem=recv_sems.at[k], device_id=(*peer(k), mc), device_id_type=pl.DeviceIdType.MESH)
            for k in range(1, NCHIP)]

    def start():
        to_sibling().start()
        mine().start()

    def combine():
        mine().wait()
        to_sibling().wait_recv()
        for j in range(NCHIP):
            sum_buf[j] = (own_buf[j].astype(F32) + sib_buf[j].astype(F32)).astype(sum_buf.dtype)
        keep().start()
        for cp in sends():
            cp.start()

    def finish():
        for k in range(1, NCHIP):
            pltpu.make_async_remote_copy(
                src_ref=sum_buf.at[chip], dst_ref=out_ref.at[2 * peer(k)[0] + peer(k)[1]], send_sem=send_sems.at[k],
                recv_sem=recv_sems.at[k], device_id=(*peer(k), mc), device_id_type=pl.DeviceIdType.MESH).wait_recv()
        for cp in sends():
            cp.wait_send()
        to_sibling().wait_send()
        keep().wait()

    return start, combine, finish


def _pack(arrs, dtype, row_mult=8):
    segs = []
    for a in arrs:
        flat = a.reshape(-1).astype(dtype)
        segs.append(jnp.pad(flat, (0, (-flat.shape[0]) % ROW)))
    flat = jnp.concatenate(segs)
    flat = jnp.pad(flat, (0, (-flat.shape[0]) % (ROW * row_mult)))
    return flat.reshape(-1, ROW)


def _unpack(buf, shapes):
    flat = buf.reshape(-1)
    out, off = [], 0
    for s in shapes:
        n = math.prod(s)
        out.append(flat[off:off + n].reshape(s))
        off += n + (-n) % ROW
    return out


def _pack_rows(arrs, axis):
    padded = []
    for t in arrs:
        pad = [(0, 0)] * t.ndim
        pad[axis] = (0, _tile_rows(t.shape[axis]) - t.shape[axis])
        padded.append(jnp.pad(t, pad))
    return jnp.concatenate(padded, axis=axis)


def _tile_rows(r):
    return r + (-r) % BF16_TILE_ROWS


def _unpack8(buf, shapes):
    flat = buf.reshape(NDEV, -1)
    out, off = [], 0
    for s in shapes:
        n = math.prod(s)
        out.append(flat[:, off:off + n].reshape((NDEV,) + tuple(s)))
        off += n + (-n) % ROW
    return out


def kernel(x, c, w_ada, b_ada, g_ffn1, w1_ffn1, w3_ffn1, w2_ffn1, g_mix, w_in, conv_qkv, a_log, dt_bias, g_onorm, lam_re, lam_im, log_step, b_re, b_im, c_re, c_im, d_skip, w_glu, b_glu, w_proj_a, w_proj_b, w_out, g_ffn2, w1_ffn2, w3_ffn2, w2_ffn2, g_final, loss_target, m_w_ada, m_b_ada, m_g_ffn1, m_w1_ffn1, m_w3_ffn1, m_w2_ffn1, m_g_mix, m_w_in, m_conv_qkv, m_a_log, m_dt_bias, m_g_onorm, m_lam_re, m_lam_im, m_log_step, m_b_re, m_b_im, m_c_re, m_c_im, m_d_skip, m_w_glu, m_b_glu, m_w_proj_a, m_w_proj_b, m_w_out, m_g_ffn2, m_w1_ffn2, m_w3_ffn2, m_w2_ffn2, m_g_final, v_w_ada, v_b_ada, v_g_ffn1, v_w1_ffn1, v_w3_ffn1, v_w2_ffn1, v_g_mix, v_w_in, v_conv_qkv, v_a_log, v_dt_bias, v_g_onorm, v_lam_re, v_lam_im, v_log_step, v_b_re, v_b_im, v_c_re, v_c_im, v_d_skip, v_w_glu, v_b_glu, v_w_proj_a, v_w_proj_b, v_w_out, v_g_ffn2, v_w1_ffn2, v_w3_ffn2, v_w2_ffn2, v_g_final):
    a = dict(locals())
    bl, seq, _ = x.shape
    t_rows = bl * seq
    me = 4 * lax.axis_index("x") + 2 * lax.axis_index("y") + lax.axis_index("c")
    tm_ew = _pick(seq, (256, 128, 64))

    loc = {n: (a[n][0].T if n in COL_SHARDED else a[n][0]) for n in RS_WEIGHTS}
    wfull, gw, res = {}, {}, {}

    def pack_local(names):
        return _pack_rows([loc[n].astype(BF16).reshape(-1, ROW) for n in names], 0)

    def unpack_full(buf, names):
        r0 = 0
        for n in names:
            r = loc[n].size // ROW
            wfull[n] = buf[:, r0:r0 + r, :].reshape(-1, loc[n].shape[1])
            r0 += _tile_rows(r)

    def pack_grads(names):
        return _pack_rows([gw[n].astype(BF16).reshape(NDEV, -1, ROW) for n in names], 1)

    def update(buf, names):
        r0 = 0
        for n in names:
            r = loc[n].size // ROW
            parts = buf[:, r0:r0 + r, :].reshape((buf.shape[0],) + loc[n].shape)
            r0 += _tile_rows(r)
            step = adamw_t if n in COL_SHARDED else adamw
            out = step("adamw_" + n, parts, a[n][0], a["m_" + n][0], a["v_" + n][0])
            for kind, t in zip(("grad", "delta", "new_m", "new_v"), out):
                res[kind + "_" + n] = t[None]

    sm, wg_ffn1 = all_gather_pair("gather_inputs", _pack([c, conv_qkv[0]], F32), pack_local(G_FFN1))
    c_loc, conv_loc = _unpack8(sm, [c.shape, conv_qkv.shape[1:]])
    c_all = c_loc.reshape(NDEV * bl, D)
    conv_full = conv_loc.transpose(1, 0, 2).reshape(CONVW, 3 * DNW)

    n_ada = w_ada.shape[2]
    mod_part = ada_fwd(c_all, w_ada[0], lax.dynamic_slice(b_ada, (0, me * n_ada), (1, n_ada)))
    mod_all = all_gather("gather_mod", mod_part).transpose(1, 0, 2).reshape(NDEV * bl, 9 * D)
    mod = lax.dynamic_slice(mod_all, (me * bl, 0), (bl, 9 * D)).reshape(bl, 9, D)
    mods = [mod[:, k:k + 1, :] for k in range(9)]

    h0 = x.reshape(t_rows, D)
    h1, f1, u1, pa1, pb1, wg_rest = ffn_fwd("ffn1_fwd", h0, mod[:, 0:3, :], g_ffn1, wg_ffn1, wg_ffn1, wg_ffn1, seq,
                                            gather=pack_local(G_MIX))
    unpack_full(wg_rest, G_MIX)
    win = wfull['w_in']
    o_small, o_s5, o_gate = 4 * DNW, 4 * DNW + 2 * NH, 4 * DNW + 2 * NH + S5W
    w_dn, w_small = win[:o_small], jnp.pad(win[o_small:o_s5], ((0, LANES - 2 * NH), (0, 0)))
    w_s5, w_gate = win[o_s5:o_gate], win[o_gate:]
    w_pieces = [w_dn, w_small, w_s5, w_gate]
    u2, p_dn, p_small, p_s5, p_gate = mix_in_fwd(h1, mods[3], mods[4], g_mix, w_pieces, seq)

    conv8 = jnp.pad(conv_full, ((0, 8 - CONVW), (0, 0)))
    alp = jnp.pad(a_log, ((0, 0), (NH, LANES - 2 * NH)))
    dtp = jnp.pad(dt_bias, ((0, 0), (NH, LANES - 2 * NH)))
    nb_dn = DN_ROWS if bl % DN_ROWS == 0 else 1
    p_dn3, p_small3 = p_dn.reshape(bl, seq, 4 * DNW), p_small.reshape(bl, seq, LANES)
    qkv3 = dn_prep_fwd(p_dn3, conv8)
    o_pre3, sprev, tinv, wg_ffn2 = deltanet_fwd(qkv3, p_small3, alp, dtp, nb_dn, gather=pack_local(G_FFN2))
    o_pre = o_pre3.reshape(t_rows, DNW)
    z_raw = p_dn[:, 3 * DNW:]

    s5_params = [lam_re.reshape(1, S5N), lam_im.reshape(1, S5N), log_step,
                 b_re[0].transpose(2, 0, 1).reshape(S5C, S5N), b_im[0].transpose(2, 0, 1).reshape(S5C, S5N),
                 c_re[0].transpose(1, 0, 2).reshape(S5C, S5N), c_im[0].transpose(1, 0, 2).reshape(S5C, S5N)]
    tables = s5_tables_fwd(s5_params)
    p_s53 = p_s5.reshape(bl, seq, S5W)
    y_s53, xs = s5_fwd(p_s53, tables, d_skip)
    y_s5 = y_s53.reshape(t_rows, S5W)
    tail_in = [o_pre, z_raw, y_s5, p_gate]
    tail_w = [g_onorm, wfull['w_glu'], b_glu, wfull['w_proj_a'], wfull['w_proj_b']]
    (merged,) = ew_call("mix_tail", fn_mix_tail, tail_in, [], tail_w, [(D, BF16)], tm_ew, seq)
    mo, h2 = mix_out_fwd(merged, wfull['w_out'], h1, mods[5], seq)
    dh3, f3, u3, pa3, pb3, dg_final, loss_part = ffn_fwd(
        "ffn2_fwd", h2, mod[:, 6:9, :], g_ffn2, wg_ffn2, wg_ffn2, wg_ffn2, seq,
        loss_head=(loss_target.reshape(t_rows, D), g_final.reshape(1, D)))


    dh2, a3, d1_3, d3_3, df3, dmod_c, dg_ffn2 = ffn_bwd("ffn2_bwd", dh3, h2, f3, pa3, pb3, mod[:, 6:9, :], g_ffn2, wg_ffn2,
                                                   wg_ffn2, wg_ffn2, seq)
    gw['w1_ffn2'] = mm_tn("gw1_ffn2", d1_3, u3)
    gw['w3_ffn2'] = mm_tn("gw3_ffn2", d3_3, u3)
    gw['w2_ffn2'] = mm_tn("gw2_ffn2", a3, df3)

    dmo, d_merged, dgt2 = mix_out_bwd(dh2, mo, wfull['w_out'], mods[5], seq)
    gw['w_out'] = mm_tn("gw_out", merged, dmo)
    (d_opre, d_z, d_ys5, d_gate), _, tail_gw = ew_vjp_call(
        "mix_tail_bwd", fn_mix_tail, tail_in, [], tail_w, [d_merged], [(0, F32), (1, F32), (2, F32), (3, BF16)],
        _pick(seq, (512, 256, 128, 64)), seq)
    dg_onorm, gw['w_glu'], dg_bglu, gw['w_proj_a'], gw['w_proj_b'] = tail_gw
    d_qkv3, d_psmall3, d_alp, d_dtp, rs_ffn2 = deltanet_bwd(
        qkv3, p_small3, alp, dtp, sprev, tinv, d_opre.reshape(bl, seq, DNW), nb_dn, exchange=pack_grads(G_FFN2))
    d_pdn3, d_conv8 = dn_prep_bwd(p_dn3, conv8, d_qkv3, d_z.reshape(bl, seq, DNW))
    d_pdn, d_psmall = d_pdn3.reshape(t_rows, 4 * DNW), d_psmall3.reshape(t_rows, LANES)

    s5_out = s5_bwd(p_s53, tables, d_skip, xs, d_ys5.reshape(bl, seq, S5W))
    d_ps5, d_tables, dg_dskip = s5_out[0].reshape(t_rows, S5W), s5_out[1:11], s5_out[11]
    d_s5p = s5_tables_bwd(s5_params, d_tables)

    gw['w_in'] = jnp.concatenate([mm_tn("gw_dn", d_pdn, u2), mm_tn("gw_small", d_psmall, u2)[:2 * NH],
                                  mm_tn("gw_s5", d_ps5, u2), mm_tn("gw_gate", d_gate, u2)], axis=0)
    dh1, dsh2, dsc2, dg_mix = mix_in_bwd([d_pdn, d_psmall, d_ps5, d_gate], w_pieces, h1, mods[3], mods[4], g_mix, dh2, seq)

    dh0, a1, d1_1, d3_1, df1, dmod_a, dg_ffn1, rs_mix = ffn_bwd(
        "ffn1_bwd", dh1, h0, f1, pa1, pb1, mod[:, 0:3, :], g_ffn1, wg_ffn1, wg_ffn1, wg_ffn1, seq,
        exchange=pack_grads(G_MIX))
    dmod_mine = jnp.concatenate([dmod_a, dsh2, dsc2, dgt2, dmod_c], axis=1).reshape(bl, 9 * D)
    small_grads = {
        'g_ffn1': dg_ffn1, 'g_mix': dg_mix, 'a_log': d_alp[:, NH:2 * NH], 'dt_bias': d_dtp[:, NH:2 * NH],
        'g_onorm': dg_onorm, 'lam_re': d_s5p[0].reshape(1, S5G, S5P), 'lam_im': d_s5p[1].reshape(1, S5G, S5P),
        'log_step': d_s5p[2],
        'b_re': d_s5p[3].reshape(S5C, S5G, S5P).transpose(1, 2, 0)[None],
        'b_im': d_s5p[4].reshape(S5C, S5G, S5P).transpose(1, 2, 0)[None],
        'c_re': d_s5p[5].reshape(S5C, S5G, S5P).transpose(1, 0, 2)[None],
        'c_im': d_s5p[6].reshape(S5C, S5G, S5P).transpose(1, 0, 2)[None],
        'd_skip': dg_dskip, 'b_glu': dg_bglu, 'g_ffn2': dg_ffn2, 'g_final': dg_final.reshape(D)}
    small_shapes = [a[n].shape for n in SMALL]
    small_pack = _pack([small_grads[n] for n in SMALL] + [loss_part], F32)
    n_small = small_pack.shape[0]
    small_buf = jnp.concatenate([small_pack, _pack([dmod_mine, d_conv8[:CONVW]], F32)], axis=0)

    gw['w1_ffn1'], sg = mm_tn("gw1_ffn1", d1_1, u1, gather=small_buf)
    gw['w3_ffn1'], rs_w1 = mm_tn("gw3_ffn1", d3_1, u1, exchange=pack_grads(['w1_ffn1']))
    gw['w2_ffn1'], rs_w3 = mm_tn("gw2_ffn1", a1, df1, exchange=pack_grads(['w3_ffn1']))
    rs_w2 = reduce_scatter_two_level("scatter_w2_ffn1", pack_grads(['w2_ffn1']))

    update(rs_ffn2, G_FFN2)
    update(rs_mix, G_MIX)
    update(rs_w1, ['w1_ffn1'])
    update(rs_w3, ['w3_ffn1'])
    update(rs_w2, ['w2_ffn1'])
    pieces = _unpack8(sg[:, n_small:, :], [dmod_mine.shape, (CONVW, 3 * DNW)])
    dmod_all = pieces[0].reshape(NDEV * bl, 9 * D)
    g_wada, g_bada = ada_bwd(c_all, lax.dynamic_slice(dmod_all, (0, me * n_ada), (NDEV * bl, n_ada)), dmod_all)

    n_conv = conv_qkv.shape[2]
    conv_parts = lax.dynamic_slice(pieces[1], (0, 0, me * n_conv), (NDEV, CONVW, n_conv))
    conv_parts = jnp.pad(conv_parts.reshape(NDEV, 1, -1), ((0, 0), (0, 7), (0, 0)))
    pad8 = lambda t: jnp.pad(t.reshape(1, -1), ((0, 7), (0, 0)))
    conv_res = adamw("adamw_conv", conv_parts, pad8(conv_qkv), pad8(m_conv_qkv), pad8(v_conv_qkv))
    for kind, buf in zip(("grad", "delta", "new_m", "new_v"), conv_res):
        res[kind + "_conv_qkv"] = buf[0].reshape(conv_qkv.shape)

    no_param = jnp.zeros_like(loss_part)
    small_res = adamw("adamw_small", sg[:, :n_small, :],
                      *[_pack([a[p + n] for n in SMALL] + [no_param], F32) for p in ("", "m_", "v_")])
    for kind, buf in zip(("grad", "delta", "new_m", "new_v"), small_res):
        for n, t in zip(SMALL, _unpack(buf, small_shapes)):
            res[kind + "_" + n] = t
    loss = _unpack(small_res[0], small_shapes + [loss_part.shape])[-1][0, 0]

    for n, g in (("w_ada", g_wada), ("b_ada", g_bada)):
        shp = a[n].shape
        r2 = lambda t: t.reshape(-1, shp[-1]) if n == "w_ada" else pad8(t)
        out = adamw("adamw_" + n, r2(g)[None], r2(a[n]), r2(a["m_" + n]), r2(a["v_" + n]))
        for kind, buf in zip(("grad", "delta", "new_m", "new_v"), out):
            res[kind + "_" + n] = (buf if n == "w_ada" else buf[0:1]).reshape(shp)

    outs = [loss, dh0.reshape(x.shape)]
    for kind in ("grad", "delta", "new_m", "new_v"):
        outs += [res[kind + "_" + n] for n in WEIGHTS]
    return tuple(outs)
```
